```python
import jax, jax.numpy as jnp
from jax import lax
import numpy as np

D_MODEL = 1024
BATCH = 32
SEQ = 2048
DEPTH = 1

CHUNK = 64
Q_BLOCK = 128
ATT_HEADS = 16
HEAD_DIM = 64
D_ATT = ATT_HEADS * HEAD_DIM
D_RNN = D_MODEL
RNN_BLOCKS = 16
RNN_BLOCK_W = D_RNN // RNN_BLOCKS
CONV_W = 4
RG_C = 8.0
NORM_EPS = 1e-6
MASK_VALUE = -1e30
IN_WIDTHS = (D_ATT, D_ATT, D_ATT, ATT_HEADS, D_ATT, D_RNN, D_RNN, D_MODEL, D_MODEL)
IN_TOTAL = 5 * D_ATT + ATT_HEADS + 2 * D_RNN + 2 * D_MODEL

kernel_name = "hybrid_fox_rglru_gated_block"


def rms_norm(x, w):
    xf = x.astype(jnp.float32)
    y = xf * lax.rsqrt(jnp.mean(xf * xf, axis=-1, keepdims=True) + NORM_EPS)
    return (y * w.astype(jnp.float32)).astype(x.dtype)


def split_columns(z):
    parts = []
    start = 0
    for width in IN_WIDTHS:
        parts.append(z[..., start:start + width])
        start += width
    return parts


def forgetting_attention(q, k, v, log_f):
    seq = q.shape[1]
    scale = HEAD_DIM ** -0.5
    c = jnp.transpose(jnp.cumsum(log_f, axis=1), (0, 2, 1))
    outs = []
    for blk in range(seq // Q_BLOCK):
        q0 = blk * Q_BLOCK
        q1 = q0 + Q_BLOCK
        qb = q[:, q0:q1]
        kb = k[:, :q1]
        vb = v[:, :q1]
        s = jnp.einsum('bqhd,bkhd->bhqk', qb, kb).astype(jnp.float32) * scale
        s = s + c[:, :, q0:q1][:, :, :, None] - c[:, :, :q1][:, :, None, :]
        mask = (q0 + jnp.arange(Q_BLOCK))[:, None] >= jnp.arange(q1)[None, :]
        s = jnp.where(mask[None, None], s, MASK_VALUE)
        p = jax.nn.softmax(s, axis=-1)
        outs.append(jnp.einsum('bhqk,bkhd->bqhd', p.astype(vb.dtype), vb))
    return jnp.concatenate(outs, axis=1)


def causal_depthwise_conv(x, w, b):
    seq = x.shape[1]
    xp = jnp.pad(x, ((0, 0), (CONV_W - 1, 0), (0, 0)))
    y = b + w[0] * xp[:, 0:seq]
    for j in range(1, CONV_W):
        y = y + w[j] * xp[:, j:j + seq]
    return y


def rg_lru(x, r, i, lam):
    log_a = -RG_C * r.astype(jnp.float32) * jax.nn.softplus(-lam.astype(jnp.float32))
    a = jnp.exp(log_a)
    u = jnp.sqrt(jnp.maximum(-jnp.expm1(2.0 * log_a), 0.0)) * (
        i.astype(jnp.float32) * x.astype(jnp.float32))
    a_t = jnp.swapaxes(a, 0, 1)
    u_t = jnp.swapaxes(u, 0, 1)

    def step(h, inp):
        a_s, u_s = inp
        h = a_s * h + u_s
        return h, h

    h0 = jnp.zeros(a_t.shape[1:], jnp.float32)
    _, hs = lax.scan(step, h0, (a_t, u_t))
    return jnp.swapaxes(hs, 0, 1).astype(x.dtype)


def _fwd_setup_inputs(seed: int = 0) -> dict:
    key = jax.random.key(seed)
    ks = jax.random.split(key, 18)
    f32 = jnp.float32
    L = DEPTH

    def nrm(k, shape, fan_in):
        return jax.random.normal(k, shape, f32) * (fan_in ** -0.5)

    x = jax.random.normal(ks[0], (BATCH, SEQ, D_MODEL), f32)
    pre_norm_w = 1.0 + 0.05 * jax.random.normal(ks[1], (L, D_MODEL), f32)
    w_in = nrm(ks[2], (L, D_MODEL, IN_TOTAL), D_MODEL)
    b_in = 0.02 * jax.random.normal(ks[3], (L, IN_TOTAL), f32)
    conv_w = nrm(ks[4], (L, CONV_W, D_RNN), CONV_W)
    conv_b = 0.02 * jax.random.normal(ks[5], (L, D_RNN), f32)
    rg_wa = nrm(ks[6], (L, RNN_BLOCKS, RNN_BLOCK_W, RNN_BLOCK_W), RNN_BLOCK_W)
    rg_ba = 0.02 * jax.random.normal(ks[7], (L, D_RNN), f32)
    rg_wx = nrm(ks[8], (L, RNN_BLOCKS, RNN_BLOCK_W, RNN_BLOCK_W), RNN_BLOCK_W)
    rg_bx = 0.02 * jax.random.normal(ks[9], (L, D_RNN), f32)
    u = jax.random.uniform(ks[10], (L, D_RNN), f32, minval=0.9, maxval=0.999)
    a0 = u ** (1.0 / RG_C)
    rg_lambda = jnp.log(a0) - jnp.log1p(-a0)
    w_branch_a = nrm(ks[11], (L, D_ATT, D_MODEL), D_ATT)
    w_branch_r = nrm(ks[12], (L, D_RNN, D_MODEL), D_RNN)
    w_out = nrm(ks[13], (L, D_MODEL, D_MODEL), D_MODEL)
    post_norm_w = 1.0 + 0.05 * jax.random.normal(ks[14], (L, D_MODEL), f32)
    return {"x": x, "pre_norm_w": pre_norm_w, "w_in": w_in, "b_in": b_in,
            "conv_w": conv_w, "conv_b": conv_b, "rg_wa": rg_wa, "rg_ba": rg_ba,
            "rg_wx": rg_wx, "rg_bx": rg_bx, "rg_lambda": rg_lambda,
            "w_branch_a": w_branch_a, "w_branch_r": w_branch_r, "w_out": w_out,
            "post_norm_w": post_norm_w}


def _fwd_reference(x, pre_norm_w, w_in, b_in, conv_w, conv_b, rg_wa, rg_ba, rg_wx, rg_bx,
              rg_lambda, w_branch_a, w_branch_r, w_out, post_norm_w):
    bsz, seq, _ = x.shape
    for l in range(DEPTH):
        h = rms_norm(x, pre_norm_w[l])
        z = jnp.einsum('bsd,de->bse', h, w_in[l]) + b_in[l]
        q, k, v, f_logit, gate_a, x_r, gate_r, mg_a, mg_r = split_columns(z)

        q = q.reshape(bsz, seq, ATT_HEADS, HEAD_DIM)
        k = k.reshape(bsz, seq, ATT_HEADS, HEAD_DIM)
        v = v.reshape(bsz, seq, ATT_HEADS, HEAD_DIM)
        log_f = jax.nn.log_sigmoid(f_logit.astype(jnp.float32))
        y_a = forgetting_attention(q, k, v, log_f).reshape(bsz, seq, D_ATT)
        y_a = jnp.einsum('bsc,cd->bsd', y_a * jax.nn.silu(gate_a), w_branch_a[l])

        xc = causal_depthwise_conv(x_r, conv_w[l], conv_b[l])
        xb = xc.reshape(bsz, seq, RNN_BLOCKS, RNN_BLOCK_W)
        r = jax.nn.sigmoid(jnp.einsum('bsgi,gij->bsgj', xb, rg_wa[l]).reshape(bsz, seq, D_RNN) + rg_ba[l])
        i = jax.nn.sigmoid(jnp.einsum('bsgi,gij->bsgj', xb, rg_wx[l]).reshape(bsz, seq, D_RNN) + rg_bx[l])
        y_r = rg_lru(xc, r, i, rg_lambda[l])
        y_r = jnp.einsum('bsc,cd->bsd', y_r * jax.nn.silu(gate_r), w_branch_r[l])

        m = jax.nn.sigmoid(mg_a) * y_a + jax.nn.sigmoid(mg_r) * y_r
        o = jnp.einsum('bsd,de->bse', m, w_out[l])
        x = x + rms_norm(o, post_norm_w[l])
    return x


import jax as _jax
import jax.numpy as _jnp

TWIN_FORMAT = 'train_step'
FWD_PARAMS = ['x', 'pre_norm_w', 'w_in', 'b_in', 'conv_w', 'conv_b', 'rg_wa', 'rg_ba', 'rg_wx', 'rg_bx', 'rg_lambda', 'w_branch_a', 'w_branch_r', 'w_out', 'post_norm_w']
TWIN_WEIGHTS = ['pre_norm_w', 'w_in', 'b_in', 'conv_w', 'conv_b', 'rg_wa', 'rg_ba', 'rg_wx', 'rg_bx', 'rg_lambda', 'w_branch_a', 'w_branch_r', 'w_out', 'post_norm_w']
TWIN_DIFF_INPUT = 'x'
TWIN_INPUTS = ['x', 'pre_norm_w', 'w_in', 'b_in', 'conv_w', 'conv_b', 'rg_wa', 'rg_ba', 'rg_wx', 'rg_bx', 'rg_lambda', 'w_branch_a', 'w_branch_r', 'w_out', 'post_norm_w', 'loss_target', 'm_pre_norm_w', 'm_w_in', 'm_b_in', 'm_conv_w', 'm_conv_b', 'm_rg_wa', 'm_rg_ba', 'm_rg_wx', 'm_rg_bx', 'm_rg_lambda', 'm_w_branch_a', 'm_w_branch_r', 'm_w_out', 'm_post_norm_w', 'v_pre_norm_w', 'v_w_in', 'v_b_in', 'v_conv_w', 'v_conv_b', 'v_rg_wa', 'v_rg_ba', 'v_rg_wx', 'v_rg_bx', 'v_rg_lambda', 'v_w_branch_a', 'v_w_branch_r', 'v_w_out', 'v_post_norm_w']
TWIN_OUTPUTS = ['loss', 'grad_x', 'grad_pre_norm_w', 'grad_w_in', 'grad_b_in', 'grad_conv_w', 'grad_conv_b', 'grad_rg_wa', 'grad_rg_ba', 'grad_rg_wx', 'grad_rg_bx', 'grad_rg_lambda', 'grad_w_branch_a', 'grad_w_branch_r', 'grad_w_out', 'grad_post_norm_w', 'delta_pre_norm_w', 'delta_w_in', 'delta_b_in', 'delta_conv_w', 'delta_conv_b', 'delta_rg_wa', 'delta_rg_ba', 'delta_rg_wx', 'delta_rg_bx', 'delta_rg_lambda', 'delta_w_branch_a', 'delta_w_branch_r', 'delta_w_out', 'delta_post_norm_w', 'new_m_pre_norm_w', 'new_m_w_in', 'new_m_b_in', 'new_m_conv_w', 'new_m_conv_b', 'new_m_rg_wa', 'new_m_rg_ba', 'new_m_rg_wx', 'new_m_rg_bx', 'new_m_rg_lambda', 'new_m_w_branch_a', 'new_m_w_branch_r', 'new_m_w_out', 'new_m_post_norm_w', 'new_v_pre_norm_w', 'new_v_w_in', 'new_v_b_in', 'new_v_conv_w', 'new_v_conv_b', 'new_v_rg_wa', 'new_v_rg_ba', 'new_v_rg_wx', 'new_v_rg_bx', 'new_v_rg_lambda', 'new_v_w_branch_a', 'new_v_w_branch_r', 'new_v_w_out', 'new_v_post_norm_w']
TWIN_LEAF_KINDS = {'loss': 'loss', 'grad_x': 'grad_x', 'grad_pre_norm_w': 'grad_w', 'grad_w_in': 'grad_w', 'grad_b_in': 'grad_w', 'grad_conv_w': 'grad_w', 'grad_conv_b': 'grad_w', 'grad_rg_wa': 'grad_w', 'grad_rg_ba': 'grad_w', 'grad_rg_wx': 'grad_w', 'grad_rg_bx': 'grad_w', 'grad_rg_lambda': 'grad_w', 'grad_w_branch_a': 'grad_w', 'grad_w_branch_r': 'grad_w', 'grad_w_out': 'grad_w', 'grad_post_norm_w': 'grad_w', 'delta_pre_norm_w': 'delta_w', 'delta_w_in': 'delta_w', 'delta_b_in': 'delta_w', 'delta_conv_w': 'delta_w', 'delta_conv_b': 'delta_w', 'delta_rg_wa': 'delta_w', 'delta_rg_ba': 'delta_w', 'delta_rg_wx': 'delta_w', 'delta_rg_bx': 'delta_w', 'delta_rg_lambda': 'delta_w', 'delta_w_branch_a': 'delta_w', 'delta_w_branch_r': 'delta_w', 'delta_w_out': 'delta_w', 'delta_post_norm_w': 'delta_w', 'new_m_pre_norm_w': 'new_m', 'new_m_w_in': 'new_m', 'new_m_b_in': 'new_m', 'new_m_conv_w': 'new_m', 'new_m_conv_b': 'new_m', 'new_m_rg_wa': 'new_m', 'new_m_rg_ba': 'new_m', 'new_m_rg_wx': 'new_m', 'new_m_rg_bx': 'new_m', 'new_m_rg_lambda': 'new_m', 'new_m_w_branch_a': 'new_m', 'new_m_w_branch_r': 'new_m', 'new_m_w_out': 'new_m', 'new_m_post_norm_w': 'new_m', 'new_v_pre_norm_w': 'new_v', 'new_v_w_in': 'new_v', 'new_v_b_in': 'new_v', 'new_v_conv_w': 'new_v', 'new_v_conv_b': 'new_v', 'new_v_rg_wa': 'new_v', 'new_v_rg_ba': 'new_v', 'new_v_rg_wx': 'new_v', 'new_v_rg_bx': 'new_v', 'new_v_rg_lambda': 'new_v', 'new_v_w_branch_a': 'new_v', 'new_v_w_branch_r': 'new_v', 'new_v_w_out': 'new_v', 'new_v_post_norm_w': 'new_v'}


def _forward(args):
    return _fwd_reference(*[args[k] for k in FWD_PARAMS])


def _output_shape():
    out = _jax.eval_shape(lambda: _forward(_fwd_setup_inputs(0)))
    return out.shape, out.dtype

N_MICROBATCH = 1
ADAM_LR = 0.001
ADAM_B1 = 0.9
ADAM_B2 = 0.999
ADAM_EPS = 1e-08
ADAM_WD = 0.01
ADAM_STEP = 10
PER_EXAMPLE_BATCH_AXIS = {'x': 0, 'loss_target': 0}
SHARED_INPUTS = []
_WEIGHT_DTYPES = {'pre_norm_w': _jnp.float32, 'w_in': _jnp.float32, 'b_in': _jnp.float32, 'conv_w': _jnp.float32, 'conv_b': _jnp.float32, 'rg_wa': _jnp.float32, 'rg_ba': _jnp.float32, 'rg_wx': _jnp.float32, 'rg_bx': _jnp.float32, 'rg_lambda': _jnp.float32, 'w_branch_a': _jnp.float32, 'w_branch_r': _jnp.float32, 'w_out': _jnp.float32, 'post_norm_w': _jnp.float32}
MOMENT_SCALE = {'pre_norm_w': 6.455216e-01, 'w_in': 2.036283e-01, 'b_in': 2.182810e+00, 'conv_w': 3.801754e-01, 'conv_b': 6.859149e+00, 'rg_wa': 2.123741e-01, 'rg_ba': 1.660408e-01, 'rg_wx': 3.985717e-01, 'rg_bx': 9.487701e-02, 'rg_lambda': 2.534209e-01, 'w_branch_a': 2.726500e-01, 'w_branch_r': 5.020197e-01, 'w_out': 5.125339e-01, 'post_norm_w': 6.423831e+01}


def _to_microbatches(a, axis):
    t = _jnp.moveaxis(a, axis, 0)
    t = t.reshape((N_MICROBATCH, t.shape[0] // N_MICROBATCH) + t.shape[1:])
    return _jnp.moveaxis(t, 1, axis + 1)


def setup_inputs(seed: int = 0) -> dict:
    inp = _fwd_setup_inputs(seed)
    key = _jax.random.fold_in(_jax.random.key(seed), 7919)
    shape, _ = _output_shape()
    out = dict(inp)
    out["loss_target"] = _jax.random.normal(_jax.random.fold_in(key, 0), shape, _jnp.float32)
    for i, name in enumerate(TWIN_WEIGHTS):
        w = inp[name].astype(_jnp.float32)
        if MOMENT_SCALE is None:
            s = _jnp.sqrt(_jnp.mean(_jnp.square(w)) + 1e-30)
        else:
            s = MOMENT_SCALE[name]
        km, kv = _jax.random.split(_jax.random.fold_in(key, i + 1))
        out[name] = w
        out["m_" + name] = s * _jax.random.normal(km, w.shape, _jnp.float32)
        out["v_" + name] = (s * s) * _jax.random.uniform(kv, w.shape, _jnp.float32, 0.5, 1.5)
    if N_MICROBATCH > 1:
        for name, axis in PER_EXAMPLE_BATCH_AXIS.items():
            out[name] = _to_microbatches(out[name], axis)
    return {'x': out['x'], 'pre_norm_w': out['pre_norm_w'], 'w_in': out['w_in'], 'b_in': out['b_in'], 'conv_w': out['conv_w'], 'conv_b': out['conv_b'], 'rg_wa': out['rg_wa'], 'rg_ba': out['rg_ba'], 'rg_wx': out['rg_wx'], 'rg_bx': out['rg_bx'], 'rg_lambda': out['rg_lambda'], 'w_branch_a': out['w_branch_a'], 'w_branch_r': out['w_branch_r'], 'w_out': out['w_out'], 'post_norm_w': out['post_norm_w'], 'loss_target': out['loss_target'], 'm_pre_norm_w': out['m_pre_norm_w'], 'm_w_in': out['m_w_in'], 'm_b_in': out['m_b_in'], 'm_conv_w': out['m_conv_w'], 'm_conv_b': out['m_conv_b'], 'm_rg_wa': out['m_rg_wa'], 'm_rg_ba': out['m_rg_ba'], 'm_rg_wx': out['m_rg_wx'], 'm_rg_bx': out['m_rg_bx'], 'm_rg_lambda': out['m_rg_lambda'], 'm_w_branch_a': out['m_w_branch_a'], 'm_w_branch_r': out['m_w_branch_r'], 'm_w_out': out['m_w_out'], 'm_post_norm_w': out['m_post_norm_w'], 'v_pre_norm_w': out['v_pre_norm_w'], 'v_w_in': out['v_w_in'], 'v_b_in': out['v_b_in'], 'v_conv_w': out['v_conv_w'], 'v_conv_b': out['v_conv_b'], 'v_rg_wa': out['v_rg_wa'], 'v_rg_ba': out['v_rg_ba'], 'v_rg_wx': out['v_rg_wx'], 'v_rg_bx': out['v_rg_bx'], 'v_rg_lambda': out['v_rg_lambda'], 'v_w_branch_a': out['v_w_branch_a'], 'v_w_branch_r': out['v_w_branch_r'], 'v_w_out': out['v_w_out'], 'v_post_norm_w': out['v_post_norm_w']}


def _loss(weights, diff, rest, loss_target):
    with _jax.named_scope("forward"):
        args = {**rest, TWIN_DIFF_INPUT: diff, **{k: w.astype(_WEIGHT_DTYPES[k]) for k, w in weights.items()}}
        y = _forward(args)
    with _jax.named_scope("loss_head"):
        err = _jnp.square(y.astype(_jnp.float32) - loss_target)
        return 0.5 * _jnp.sum(_jnp.mean(err, axis=-1)) if err.ndim else 0.5 * err


def _adamw(w, g, m, v):
    m = ADAM_B1 * m + (1.0 - ADAM_B1) * g
    v = ADAM_B2 * v + (1.0 - ADAM_B2) * _jnp.square(g)
    m_hat = m / (1.0 - ADAM_B1 ** ADAM_STEP)
    v_hat = v / (1.0 - ADAM_B2 ** ADAM_STEP)
    delta = -ADAM_LR * (m_hat / (_jnp.sqrt(v_hat) + ADAM_EPS) + ADAM_WD * w)
    return delta, m, v


def reference(x, pre_norm_w, w_in, b_in, conv_w, conv_b, rg_wa, rg_ba, rg_wx, rg_bx, rg_lambda, w_branch_a, w_branch_r, w_out, post_norm_w, loss_target, m_pre_norm_w, m_w_in, m_b_in, m_conv_w, m_conv_b, m_rg_wa, m_rg_ba, m_rg_wx, m_rg_bx, m_rg_lambda, m_w_branch_a, m_w_branch_r, m_w_out, m_post_norm_w, v_pre_norm_w, v_w_in, v_b_in, v_conv_w, v_conv_b, v_rg_wa, v_rg_ba, v_rg_wx, v_rg_bx, v_rg_lambda, v_w_branch_a, v_w_branch_r, v_w_out, v_post_norm_w):
    given = dict(x=x, pre_norm_w=pre_norm_w, w_in=w_in, b_in=b_in, conv_w=conv_w, conv_b=conv_b, rg_wa=rg_wa, rg_ba=rg_ba, rg_wx=rg_wx, rg_bx=rg_bx, rg_lambda=rg_lambda, w_branch_a=w_branch_a, w_branch_r=w_branch_r, w_out=w_out, post_norm_w=post_norm_w, loss_target=loss_target, m_pre_norm_w=m_pre_norm_w, m_w_in=m_w_in, m_b_in=m_b_in, m_conv_w=m_conv_w, m_conv_b=m_conv_b, m_rg_wa=m_rg_wa, m_rg_ba=m_rg_ba, m_rg_wx=m_rg_wx, m_rg_bx=m_rg_bx, m_rg_lambda=m_rg_lambda, m_w_branch_a=m_w_branch_a, m_w_branch_r=m_w_branch_r, m_w_out=m_w_out, m_post_norm_w=m_post_norm_w, v_pre_norm_w=v_pre_norm_w, v_w_in=v_w_in, v_b_in=v_b_in, v_conv_w=v_conv_w, v_conv_b=v_conv_b, v_rg_wa=v_rg_wa, v_rg_ba=v_rg_ba, v_rg_wx=v_rg_wx, v_rg_bx=v_rg_bx, v_rg_lambda=v_rg_lambda, v_w_branch_a=v_w_branch_a, v_w_branch_r=v_w_branch_r, v_w_out=v_w_out, v_post_norm_w=v_post_norm_w)
    weights = {n: given[n] for n in TWIN_WEIGHTS}
    shared = {n: given[n] for n in SHARED_INPUTS}
    per_example = {n: given[n] for n in ['x']}
    grad_fn = _jax.value_and_grad(_loss, argnums=(0, 1))

    def one_microbatch(ex, loss_target):
        ex = dict(ex)
        diff = ex.pop(TWIN_DIFF_INPUT)
        return grad_fn(weights, diff, {**shared, **ex}, loss_target)

    if N_MICROBATCH == 1:
        loss, (grad_w, grad_x) = one_microbatch(per_example, given["loss_target"])
    else:
        def body(carry, xs):
            loss_sum, grad_sum = carry
            l_k, (gw_k, gx_k) = one_microbatch(xs[0], xs[1])
            with _jax.named_scope("update"):
                return (loss_sum + l_k, _jax.tree.map(_jnp.add, grad_sum, gw_k)), gx_k

        init = (_jnp.zeros((), _jnp.float32), _jax.tree.map(_jnp.zeros_like, weights))
        (loss, grad_w), grad_x = _jax.lax.scan(body, init, (per_example, given["loss_target"]))
    with _jax.named_scope("update"):
        delta_w, new_m, new_v = {}, {}, {}
        for n in TWIN_WEIGHTS:
            delta_w[n], new_m[n], new_v[n] = _adamw(weights[n], grad_w[n], given["m_" + n], given["v_" + n])
    return (loss, grad_x, *[grad_w[n] for n in TWIN_WEIGHTS], *[delta_w[n] for n in TWIN_WEIGHTS],
            *[new_m[n] for n in TWIN_WEIGHTS], *[new_v[n] for n in TWIN_WEIGHTS])
```

```python
import jax
import jax.numpy as jnp
from jax import lax
from jax.experimental import pallas as pl
from jax.experimental.pallas import tpu as pltpu

F32 = jnp.float32
BF16 = jnp.bfloat16

N_DEV = 8
D_MODEL = 1024
HEADS = 16
HEAD_DIM = 64
HEAD_PAIRS = HEADS // 2
LANES = 128
N_CBLK = D_MODEL // LANES
CONV_W = 4
RG_C = 8.0
NORM_EPS = 1e-6
MASK_VALUE = -1e30
IN_USED = 8208
IN_TOTAL = 9232
W_SHARD = IN_TOTAL // N_DEV

ADAM_LR = 0.001
ADAM_B1 = 0.9
ADAM_B2 = 0.999
ADAM_EPS = 1e-08
ADAM_WD = 0.01
ADAM_STEP = 10

ATT_TILE = 256
SCAN_TILE = 256
SMALL_ROWS = 152


def _cparams(sem=None, vmem_mb=None):
    kw = {}
    if sem is not None:
        kw["dimension_semantics"] = sem
    if vmem_mb is not None:
        kw["vmem_limit_bytes"] = vmem_mb * 1024 * 1024
    return pltpu.CompilerParams(**kw)


def _sigmoid(x):
    return 1.0 / (1.0 + jnp.exp(-x))


def _softplus(x):
    return jnp.maximum(x, 0.0) + jnp.log1p(jnp.exp(-jnp.abs(x)))


def _expm1(x):
    p = x * (1.0 + x * (1.0 / 2 + x * (1.0 / 6 + x * (1.0 / 24 + x * (1.0 / 120 + x * (
        1.0 / 720 + x * (1.0 / 5040 + x * (1.0 / 40320))))))))
    return jnp.where(jnp.abs(x) < 0.5, p, jnp.exp(x) - 1.0)


def _split3(x):
    hi = x.astype(BF16)
    r1 = x - hi.astype(F32)
    mid = r1.astype(BF16)
    lo = (r1 - mid.astype(F32)).astype(BF16)
    return hi, mid, lo


def _dot(a, b):
    return jnp.dot(a, b, preferred_element_type=F32)


def _dot_nt(a, b):
    return lax.dot_general(a, b, (((1,), (1,)), ((), ())), preferred_element_type=F32)


def _dot_tn(a, b):
    return lax.dot_general(a, b, (((0,), (0,)), ((), ())), preferred_element_type=F32)


def _iota(shape, dim):
    return lax.broadcasted_iota(jnp.int32, shape, dim)


def _exchange(src, name):
    gather = src.ndim == 2
    blk = src.shape if gather else src.shape[1:]

    def body(src_ref, out_ref, send_sems, recv_sems, local_sem):
        x, y, c = lax.axis_index("x"), lax.axis_index("y"), lax.axis_index("c")
        me = 4 * x + 2 * y + c

        def peer(k):
            return (1 - x if k & 4 else x, 1 - y if k & 2 else y, 1 - c if k & 1 else c)

        def index(p):
            return 4 * p[0] + 2 * p[1] + p[2]

        def block_for(j):
            return src_ref if gather else src_ref.at[j]

        def copy(k, dst_slot):
            p = peer(k)
            return pltpu.make_async_remote_copy(
                src_ref=block_for(index(p)), dst_ref=out_ref.at[dst_slot],
                send_sem=send_sems.at[k - 1], recv_sem=recv_sems.at[k - 1],
                device_id=p, device_id_type=pl.DeviceIdType.MESH)

        local = pltpu.make_async_copy(block_for(me), out_ref.at[me], local_sem)
        local.start()
        sends = [copy(k, me) for k in range(1, N_DEV)]
        for cp in sends:
            cp.start()
        for k in range(1, N_DEV):
            copy(k, index(peer(k))).wait_recv()
        for cp in sends:
            cp.wait_send()
        local.wait()

    return pl.pallas_call(
        body, name=name,
        out_shape=jax.ShapeDtypeStruct((N_DEV,) + tuple(blk), src.dtype),
        in_specs=[pl.BlockSpec(memory_space=pl.ANY)],
        out_specs=pl.BlockSpec(memory_space=pl.ANY),
        scratch_shapes=[pltpu.SemaphoreType.DMA((N_DEV - 1,)), pltpu.SemaphoreType.DMA((N_DEV - 1,)),
                        pltpu.SemaphoreType.DMA],
    )(src)


def _prenorm(x2, w):
    t = x2.shape[0]
    tm = min(512, t)

    def body(x_ref, w_ref, h_ref):
        x = x_ref[...]
        r = lax.rsqrt(jnp.mean(x * x, axis=-1, keepdims=True) + NORM_EPS)
        h_ref[...] = (x * r * w_ref[...]).astype(BF16)

    return pl.pallas_call(
        body, name="prenorm", grid=(t // tm,),
        in_specs=[pl.BlockSpec((tm, D_MODEL), lambda i: (i, 0)), pl.BlockSpec((1, D_MODEL), lambda i: (0, 0))],
        out_specs=pl.BlockSpec((tm, D_MODEL), lambda i: (i, 0)),
        out_shape=jax.ShapeDtypeStruct((t, D_MODEL), BF16),
        compiler_params=_cparams(("parallel",)),
    )(x2, w)


def _mm_bias(a, b, bias, out_dtype, name):
    m, k = a.shape
    n = b.shape[1]
    tm = min(512, m)
    tn = min(1024, n)

    def body(a_ref, b_ref, bias_ref, o_ref):
        o_ref[...] = (_dot(a_ref[...], b_ref[...]) + bias_ref[...]).astype(o_ref.dtype)

    return pl.pallas_call(
        body, name=name, grid=(n // tn, m // tm),
        in_specs=[pl.BlockSpec((tm, k), lambda j, i: (i, 0)), pl.BlockSpec((k, tn), lambda j, i: (0, j)),
                  pl.BlockSpec((1, tn), lambda j, i: (0, j))],
        out_specs=pl.BlockSpec((tm, tn), lambda j, i: (i, j)),
        out_shape=jax.ShapeDtypeStruct((m, n), out_dtype),
        compiler_params=_cparams(("parallel", "parallel")),
    )(a, b, bias)


def _mm_tn(a, b, name):
    t, m = a.shape
    n = b.shape[1]
    tn = min(1024, n)
    tk = min(512, t)

    def body(a_ref, b_ref, o_ref, s_ref):
        kk = pl.program_id(1)

        @pl.when(kk == 0)
        def _():
            o_ref[...] = jnp.zeros_like(o_ref)
            s_ref[...] = jnp.zeros_like(s_ref)

        bb = b_ref[...]
        o_ref[...] += _dot_tn(a_ref[...], bb)
        s_ref[0:1, :] += jnp.sum(bb.astype(F32), axis=0, keepdims=True)

    return pl.pallas_call(
        body, name=name, grid=(n // tn, t // tk),
        in_specs=[pl.BlockSpec((tk, m), lambda j, kk: (kk, 0)), pl.BlockSpec((tk, tn), lambda j, kk: (kk, j))],
        out_specs=[pl.BlockSpec((m, tn), lambda j, kk: (0, j)), pl.BlockSpec((8, tn), lambda j, kk: (0, j))],
        out_shape=[jax.ShapeDtypeStruct((m, n), F32), jax.ShapeDtypeStruct((8, n), F32)],
        compiler_params=_cparams(("parallel", "arbitrary")),
    )(a, b)


def _fgate_fwd(zf3):
    b, s, _ = zf3.shape
    tb = SCAN_TILE
    nb = s // tb

    def body(z_ref, cexp_ref, crow_ref):
        tri = (_iota((tb, tb), 1) <= _iota((tb, tb), 0)).astype(BF16)
        expand = ((_iota((LANES, D_MODEL), 1) >> 6) == _iota((LANES, D_MODEL), 0)).astype(BF16)
        carry = jnp.zeros((1, LANES), F32)
        for i in range(nb):
            rows = slice(i * tb, (i + 1) * tb)
            z = z_ref[rows, :]
            lf = jnp.minimum(z, 0.0) - jnp.log1p(jnp.exp(-jnp.abs(z)))
            cb = sum(_dot(tri, part) for part in _split3(lf)) + carry
            carry = cb[tb - 1:tb, :]
            cexp_ref[rows, :] = sum(_dot(part, expand) for part in _split3(cb))
            crow_ref[:, rows] = cb.T[0:HEADS, :]

    return pl.pallas_call(
        body, name="fgate_fwd", grid=(b,),
        in_specs=[pl.BlockSpec((None, s, LANES), lambda i: (i, 0, 0))],
        out_specs=[pl.BlockSpec((None, s, D_MODEL), lambda i: (i, 0, 0)),
                   pl.BlockSpec((None, HEADS, s), lambda i: (i, 0, 0))],
        out_shape=[jax.ShapeDtypeStruct((b, s, D_MODEL), F32), jax.ShapeDtypeStruct((b, HEADS, s), F32)],
        compiler_params=_cparams(("parallel",)),
    )(zf3)


def _fgate_bwd(dc3, zf3):
    b, s, _ = zf3.shape
    tb = SCAN_TILE
    nb = s // tb

    def body(dc_ref, z_ref, o_ref):
        tri = (_iota((tb, tb), 1) >= _iota((tb, tb), 0)).astype(BF16)
        carry = jnp.zeros((1, LANES), F32)
        for i in reversed(range(nb)):
            rows = slice(i * tb, (i + 1) * tb)
            dlf = sum(_dot(tri, part) for part in _split3(dc_ref[rows, :])) + carry
            carry = dlf[0:1, :]
            o_ref[rows, :] = (dlf * _sigmoid(-z_ref[rows, :])).astype(BF16)

    return pl.pallas_call(
        body, name="fgate_bwd", grid=(b,),
        in_specs=[pl.BlockSpec((None, s, LANES), lambda i: (i, 0, 0)),
                  pl.BlockSpec((None, s, LANES), lambda i: (i, 0, 0))],
        out_specs=pl.BlockSpec((None, s, LANES), lambda i: (i, 0, 0)),
        out_shape=jax.ShapeDtypeStruct((b, s, LANES), BF16),
        compiler_params=_cparams(("parallel",)),
    )(dc3, zf3)


def _attn_fwd(qkv3, cexp3, crow5, zrest3):
    b, s, _ = qkv3.shape
    ta = ATT_TILE
    nq = s // ta

    def body(qkv_ref, cq_ref, ck_ref, g_ref, y_ref, lse_ref, ga_ref):
        head0 = _iota((1, LANES), 1) < HEAD_DIM
        causal = _iota((ta, ta), 0) >= _iota((ta, ta), 1)

        def qblock(qi, carry):
            q0 = pl.multiple_of(qi * ta, ta)
            q = qkv_ref[pl.ds(q0, ta), 0:LANES] * 0.125
            zero = jnp.zeros_like(q)
            qh = (jnp.where(head0, q, zero), jnp.where(head0, zero, q))
            cq = cq_ref[pl.ds(q0, ta), :]
            cqh = (cq[:, 0:1], cq[:, HEAD_DIM:HEAD_DIM + 1])

            def kstep(kj, st, masked):
                k0 = pl.multiple_of(kj * ta, ta)
                k = qkv_ref[pl.ds(k0, ta), LANES:2 * LANES]
                v = qkv_ref[pl.ds(k0, ta), 2 * LANES:3 * LANES]
                new = []
                for hh in range(2):
                    m, l, acc = st[hh]
                    ck = ck_ref[hh, pl.ds(kj, 1), :]
                    sc = _dot_nt(qh[hh], k) + (cqh[hh] - ck)
                    if masked:
                        sc = jnp.where(causal, sc, MASK_VALUE)
                    mn = jnp.maximum(m, jnp.max(sc, axis=-1, keepdims=True))
                    al = jnp.exp(m - mn)
                    p = jnp.exp(sc - mn)
                    l = al * l + jnp.sum(p, axis=-1, keepdims=True)
                    acc = al * acc + _dot(p.astype(BF16), v)
                    new.append((mn, l, acc))
                return tuple(new)

            one = (jnp.full((ta, 1), MASK_VALUE, F32), jnp.zeros((ta, 1), F32), jnp.zeros((ta, LANES), F32))
            st = lax.fori_loop(0, qi, lambda kj, st: kstep(kj, st, False), (one, one))
            (ma, la, acca), (mb, lb, accb) = kstep(qi, st, True)
            y = jnp.where(head0, acca * (1.0 / la), accb * (1.0 / lb))
            lse = jnp.where(head0, ma + jnp.log(la), mb + jnp.log(lb)).T
            lse_ref[0, pl.ds(qi, 1), :] = lse[0:1, :]
            lse_ref[1, pl.ds(qi, 1), :] = lse[HEAD_DIM:HEAD_DIM + 1, :]
            y_ref[pl.ds(q0, ta), :] = y
            g = g_ref[pl.ds(q0, ta), :]
            ga_ref[pl.ds(q0, ta), :] = (y * (g * _sigmoid(g))).astype(BF16)
            return carry

        lax.fori_loop(0, nq, qblock, 0)

    blk = lambda w: pl.BlockSpec((None, s, w), lambda i, p: (i, 0, p))
    rows = pl.BlockSpec((None, None, 2, nq, ta), lambda i, p: (i, p, 0, 0, 0))
    return pl.pallas_call(
        body, name="attn_fwd", grid=(b, HEAD_PAIRS),
        in_specs=[blk(3 * LANES), blk(LANES), rows, blk(LANES)],
        out_specs=[blk(LANES), rows, blk(LANES)],
        out_shape=[jax.ShapeDtypeStruct((b, s, D_MODEL), F32),
                   jax.ShapeDtypeStruct((b, HEAD_PAIRS, 2, nq, ta), F32),
                   jax.ShapeDtypeStruct((b, s, D_MODEL), BF16)],
        compiler_params=_cparams(("parallel", "parallel")),
    )(qkv3, cexp3, crow5, zrest3)


def _attn_bwd(qkv3, do3, y3, lse5, crow5, cexp3):
    b, s, _ = qkv3.shape
    ta = ATT_TILE
    nq = s // ta

    def body(qkv_ref, do_ref, y_ref, lse_ref, cq_ref, ck_ref, dqkv_ref, dc_ref, dqt_scr, dd_scr):
        pair = pl.program_id(1)
        lane = _iota((1, LANES), 1)
        head0 = lane < HEAD_DIM
        top = _iota((LANES, ta), 0) < HEAD_DIM
        causal_t = _iota((ta, ta), 0) <= _iota((ta, ta), 1)

        @pl.when(pair == 0)
        def _():
            dc_ref[...] = jnp.zeros_like(dc_ref)

        dqt_scr[...] = jnp.zeros_like(dqt_scr)
        for qi in range(nq):
            rows = slice(qi * ta, (qi + 1) * ta)
            pr = y_ref[rows, :] * do_ref[rows, :].astype(F32)
            da = jnp.sum(jnp.where(head0, pr, 0.0), axis=-1, keepdims=True)
            db = jnp.sum(jnp.where(head0, 0.0, pr), axis=-1, keepdims=True)
            dt = jnp.where(head0, da, db).T
            dd_scr[0, qi:qi + 1, :] = dt[0:1, :]
            dd_scr[1, qi:qi + 1, :] = dt[HEAD_DIM:HEAD_DIM + 1, :]

        def kblock(kj, carry):
            k0 = pl.multiple_of(kj * ta, ta)
            k = qkv_ref[pl.ds(k0, ta), LANES:2 * LANES]
            v = qkv_ref[pl.ds(k0, ta), 2 * LANES:3 * LANES]
            zero = jnp.zeros_like(k)
            kh = (jnp.where(head0, k, zero), jnp.where(head0, zero, k))
            vh = (jnp.where(head0, v, zero), jnp.where(head0, zero, v))
            kt = k.astype(F32).T
            kth = (jnp.where(top, kt, 0.0).astype(BF16), jnp.where(top, 0.0, kt).astype(BF16))
            ck = ck_ref[pl.ds(k0, ta), :]
            ckh = (ck[:, 0:1], ck[:, HEAD_DIM:HEAD_DIM + 1])

            def qstep(qi, st, masked):
                q0 = pl.multiple_of(qi * ta, ta)
                qs = qkv_ref[pl.ds(q0, ta), 0:LANES] * 0.125
                do = do_ref[pl.ds(q0, ta), :]
                dqt = jnp.zeros((LANES, ta), F32)
                new = []
                for hh in range(2):
                    dk, dv, dc = st[hh]
                    sc = _dot_nt(kh[hh], qs) + (cq_ref[hh, pl.ds(qi, 1), :] - ckh[hh])
                    if masked:
                        sc = jnp.where(causal_t, sc, MASK_VALUE)
                    p = jnp.exp(sc - lse_ref[hh, pl.ds(qi, 1), :])
                    dv = dv + _dot(p.astype(BF16), do)
                    dp = _dot_nt(vh[hh], do)
                    ds = p * (dp - dd_scr[hh, pl.ds(qi, 1), :])
                    dc = dc + jnp.sum(ds, axis=-1, keepdims=True)
                    dsb = ds.astype(BF16)
                    dk = dk + _dot(dsb, qs)
                    dqt = dqt + _dot(kth[hh], dsb)
                    new.append((dk, dv, dc))
                dqt_scr[qi] += dqt
                return tuple(new)

            one = (jnp.zeros((ta, LANES), F32), jnp.zeros((ta, LANES), F32), jnp.zeros((ta, 1), F32))
            st = qstep(kj, (one, one), True)
            (dka, dva, dca), (dkb, dvb, dcb) = lax.fori_loop(
                kj + 1, nq, lambda qi, st: qstep(qi, st, False), st)
            dqkv_ref[pl.ds(k0, ta), LANES:2 * LANES] = jnp.where(head0, dka, dkb).astype(BF16)
            dqkv_ref[pl.ds(k0, ta), 2 * LANES:3 * LANES] = jnp.where(head0, dva, dvb).astype(BF16)
            dc_ref[pl.ds(k0, ta), :] += (jnp.where(lane == 2 * pair, -dca, 0.0)
                                         + jnp.where(lane == 2 * pair + 1, -dcb, 0.0))
            return carry

        lax.fori_loop(0, nq, kblock, 0)
        for qi in range(nq):
            rows = slice(qi * ta, (qi + 1) * ta)
            dqkv_ref[rows, 0:LANES] = (dqt_scr[qi].T * 0.125).astype(BF16)

    blk = lambda w: pl.BlockSpec((None, s, w), lambda i, p: (i, 0, p))
    rows = pl.BlockSpec((None, None, 2, nq, ta), lambda i, p: (i, p, 0, 0, 0))
    return pl.pallas_call(
        body, name="attn_bwd", grid=(b, HEAD_PAIRS),
        in_specs=[blk(3 * LANES), blk(LANES), blk(LANES), rows, rows, blk(LANES)],
        out_specs=[blk(3 * LANES), pl.BlockSpec((None, s, LANES), lambda i, p: (i, 0, 0))],
        out_shape=[jax.ShapeDtypeStruct((b, s, 3 * D_MODEL), BF16), jax.ShapeDtypeStruct((b, s, LANES), F32)],
        scratch_shapes=[pltpu.VMEM((nq, LANES, ta), F32), pltpu.VMEM((2, nq, ta), F32)],
        compiler_params=_cparams(("parallel", "arbitrary")),
    )(qkv3, do3, y3, lse5, crow5, cexp3)


def _rnn_common(xr, cw_ref, cb_ref, bda_ref, bdx_ref, ba_ref, bx_ref, lam_ref, s):
    rows = _iota((s, LANES), 0)

    def down(v, k):
        return jnp.where(rows >= k, pltpu.roll(v, k, 0), 0.0)

    x1, x2, x3 = down(xr, 1), down(xr, 2), down(xr, 3)
    xc = cb_ref[...] + cw_ref[0:1, :] * x3
    xc = xc + cw_ref[1:2, :] * x2
    xc = xc + cw_ref[2:3, :] * x1
    xc = xc + cw_ref[3:4, :] * xr
    xcb = xc.astype(BF16)
    r = _sigmoid(_dot(xcb, bda_ref[...]) + ba_ref[...])
    i = _sigmoid(_dot(xcb, bdx_ref[...]) + bx_ref[...])
    sp = _softplus(-lam_ref[...])
    log_a = (-RG_C * r) * sp
    a = jnp.exp(log_a)
    e2 = -_expm1(2.0 * log_a)
    sq = jnp.sqrt(jnp.maximum(e2, 0.0))
    return rows, (x1, x2, x3), xc, xcb, r, i, sp, a, e2, sq


def _rnn_specs(s):
    blk = lambda off: pl.BlockSpec((None, s, LANES), lambda cb, i: (i, 0, off + cb))
    vec = lambda r: pl.BlockSpec((r, LANES), lambda cb, i: (0, cb))
    mat = pl.BlockSpec((None, LANES, LANES), lambda cb, i: (cb, 0, 0))
    return blk, vec, mat


def _rnn_fwd(zrest3, conv_w, conv_b, bda, bdx, ba, bx, lam):
    b, s, _ = zrest3.shape

    def body(xr_ref, g_ref, cw_ref, cb_ref, bda_ref, bdx_ref, ba_ref, bx_ref, lam_ref, h_ref, gr_ref):
        xr = xr_ref[...]
        rows, _, xc, _, _, i, _, a, _, sq = _rnn_common(
            xr, cw_ref, cb_ref, bda_ref, bdx_ref, ba_ref, bx_ref, lam_ref, s)
        u = sq * (i * xc)
        sh = 1
        while sh < s:
            keep = rows >= sh
            ur = jnp.where(keep, pltpu.roll(u, sh, 0), 0.0)
            u = u + a * ur
            if sh * 2 < s:
                a = a * jnp.where(keep, pltpu.roll(a, sh, 0), 1.0)
            sh *= 2
        h_ref[...] = u
        g = g_ref[...]
        gr_ref[...] = (u * (g * _sigmoid(g))).astype(BF16)

    blk, vec, mat = _rnn_specs(s)
    return pl.pallas_call(
        body, name="rnn_fwd", grid=(N_CBLK, b),
        in_specs=[blk(N_CBLK), blk(2 * N_CBLK), vec(CONV_W), vec(1), mat, mat, vec(1), vec(1), vec(1)],
        out_specs=[blk(0), blk(0)],
        out_shape=[jax.ShapeDtypeStruct((b, s, D_MODEL), F32), jax.ShapeDtypeStruct((b, s, D_MODEL), BF16)],
        compiler_params=_cparams(("parallel", "parallel")),
    )(zrest3, zrest3, conv_w, conv_b, bda, bdx, ba, bx, lam)


def _rnn_bwd(zrest3, h3, dh3, conv_w, conv_b, bda, bdx, ba, bx, lam):
    b, s, _ = zrest3.shape

    def body(xr_ref, h_ref, dh_ref, cw_ref, cb_ref, bda_ref, bdx_ref, ba_ref, bx_ref, lam_ref,
             dxr_ref, pv_ref, dbd_ref):
        @pl.when(pl.program_id(1) == 0)
        def _():
            pv_ref[...] = jnp.zeros_like(pv_ref)
            dbd_ref[...] = jnp.zeros_like(dbd_ref)

        xr = xr_ref[...]
        rows, (x1, x2, x3), xc, xcb, r, i, sp, a, e2, sq = _rnn_common(
            xr, cw_ref, cb_ref, bda_ref, bdx_ref, ba_ref, bx_ref, lam_ref, s)
        h = h_ref[...]
        g = dh_ref[...]
        an = jnp.where(rows < s - 1, pltpu.roll(a, s - 1, 0), 0.0)
        sh = 1
        while sh < s:
            keep = rows < s - sh
            gr = jnp.where(keep, pltpu.roll(g, s - sh, 0), 0.0)
            g = g + an * gr
            if sh * 2 < s:
                an = an * jnp.where(keep, pltpu.roll(an, s - sh, 0), 1.0)
            sh *= 2
        hp = jnp.where(rows >= 1, pltpu.roll(h, 1, 0), 0.0)
        da = g * hp
        dsq = g * (i * xc)
        di = g * (sq * xc)
        dxc = g * (sq * i)
        dlog = da * a - dsq * ((1.0 - e2) / sq)
        dr = dlog * (-RG_C * sp)
        dpr = dr * (r * (1.0 - r))
        dpi = di * (i * (1.0 - i))
        dprb = dpr.astype(BF16)
        dpib = dpi.astype(BF16)
        dxc = dxc + _dot_nt(dprb, bda_ref[...]) + _dot_nt(dpib, bdx_ref[...])

        def up(v, k):
            return jnp.where(rows < s - k, pltpu.roll(v, s - k, 0), 0.0)

        dxr = cw_ref[3:4, :] * dxc + cw_ref[2:3, :] * up(dxc, 1) + cw_ref[1:2, :] * up(dxc, 2) \
            + cw_ref[0:1, :] * up(dxc, 3)
        dxr_ref[...] = dxr.astype(BF16)

        def colsum(v):
            return jnp.sum(v, axis=0, keepdims=True)

        pv_ref[0:1, :] += colsum(dxc * x3)
        pv_ref[1:2, :] += colsum(dxc * x2)
        pv_ref[2:3, :] += colsum(dxc * x1)
        pv_ref[3:4, :] += colsum(dxc * xr)
        pv_ref[4:5, :] += colsum(dxc)
        pv_ref[5:6, :] += colsum(dpr)
        pv_ref[6:7, :] += colsum(dpi)
        pv_ref[7:8, :] += colsum(dlog * r) * (RG_C * _sigmoid(-lam_ref[...]))
        dbd_ref[0] += _dot_tn(xcb, dprb)
        dbd_ref[1] += _dot_tn(xcb, dpib)

    blk, vec, mat = _rnn_specs(s)
    hblk = pl.BlockSpec((None, s, LANES), lambda cb, i: (i, 0, cb))
    return pl.pallas_call(
        body, name="rnn_bwd", grid=(N_CBLK, b),
        in_specs=[blk(N_CBLK), hblk, hblk, vec(CONV_W), vec(1), mat, mat, vec(1), vec(1), vec(1)],
        out_specs=[hblk, pl.BlockSpec((8, LANES), lambda cb, i: (0, cb)),
                   pl.BlockSpec((None, 2, LANES, LANES), lambda cb, i: (cb, 0, 0, 0))],
        out_shape=[jax.ShapeDtypeStruct((b, s, D_MODEL), BF16), jax.ShapeDtypeStruct((8, D_MODEL), F32),
                   jax.ShapeDtypeStruct((N_CBLK, 2, LANES, LANES), F32)],
        compiler_params=_cparams(("parallel", "arbitrary")),
    )(zrest3, h3, dh3, conv_w, conv_b, bda, bdx, ba, bx, lam)


def _branch_merge(ga, gr, wa, wr, zrest):
    t = ga.shape[0]
    tm = min(512, t)
    tn = 512

    def body(ga_ref, gr_ref, wa_ref, wr_ref, mga_ref, mgr_ref, ya_ref, yr_ref, m_ref):
        ya = _dot(ga_ref[...], wa_ref[...])
        yr = _dot(gr_ref[...], wr_ref[...])
        ya_ref[...] = ya
        yr_ref[...] = yr
        m_ref[...] = (_sigmoid(mga_ref[...]) * ya + _sigmoid(mgr_ref[...]) * yr).astype(BF16)

    nj = D_MODEL // tn
    act = pl.BlockSpec((tm, D_MODEL), lambda i, j: (i, 0))
    wgt = pl.BlockSpec((D_MODEL, tn), lambda i, j: (0, j))
    out = pl.BlockSpec((tm, tn), lambda i, j: (i, j))
    return pl.pallas_call(
        body, name="branch_merge", grid=(t // tm, nj),
        in_specs=[act, act, wgt, wgt, pl.BlockSpec((tm, tn), lambda i, j: (i, 3 * nj + j)),
                  pl.BlockSpec((tm, tn), lambda i, j: (i, 4 * nj + j))],
        out_specs=[out, out, out],
        out_shape=[jax.ShapeDtypeStruct((t, D_MODEL), F32), jax.ShapeDtypeStruct((t, D_MODEL), F32),
                   jax.ShapeDtypeStruct((t, D_MODEL), BF16)],
        compiler_params=_cparams(("parallel", "parallel")),
    )(ga, gr, wa, wr, zrest, zrest)


def _out_loss(m, wout, x2, tgt2, wpost):
    t = m.shape[0]
    tm = min(256, t)

    def body(m_ref, w_ref, x_ref, t_ref, wp_ref, dy_ref, do_ref, acc_ref):
        @pl.when(pl.program_id(0) == 0)
        def _():
            acc_ref[...] = jnp.zeros_like(acc_ref)

        o = _dot(m_ref[...], w_ref[...])
        r2 = lax.rsqrt(jnp.mean(o * o, axis=-1, keepdims=True) + NORM_EPS)
        n = o * r2
        wp = wp_ref[...]
        err = (x_ref[...] + n * wp) - t_ref[...]
        dy = err * (1.0 / D_MODEL)
        dn = dy * wp
        do = r2 * (dn - n * jnp.mean(dn * n, axis=-1, keepdims=True))
        dy_ref[...] = dy
        do_ref[...] = do.astype(BF16)
        acc_ref[0:1, :] += jnp.sum(dy * n, axis=0, keepdims=True)
        acc_ref[1:2, :] += jnp.sum(err * err, axis=0, keepdims=True)

    row = pl.BlockSpec((tm, D_MODEL), lambda i: (i, 0))
    return pl.pallas_call(
        body, name="out_loss", grid=(t // tm,),
        in_specs=[row, pl.BlockSpec((D_MODEL, D_MODEL), lambda i: (0, 0)), row, row,
                  pl.BlockSpec((1, D_MODEL), lambda i: (0, 0))],
        out_specs=[row, row, pl.BlockSpec((8, D_MODEL), lambda i: (0, 0))],
        out_shape=[jax.ShapeDtypeStruct((t, D_MODEL), F32), jax.ShapeDtypeStruct((t, D_MODEL), BF16),
                   jax.ShapeDtypeStruct((8, D_MODEL), F32)],
        compiler_params=_cparams(("arbitrary",)),
    )(m, wout, x2, tgt2, wpost)


def _merge_bwd(do, wout, zrest, ya, yr):
    t = do.shape[0]
    tm = min(512, t)
    tn = 512
    nj = D_MODEL // tn

    def body(do_ref, w_ref, mga_ref, mgr_ref, ya_ref, yr_ref, dya_ref, dyr_ref, dmga_ref, dmgr_ref):
        dm = _dot_nt(do_ref[...], w_ref[...])
        sa = _sigmoid(mga_ref[...])
        sr = _sigmoid(mgr_ref[...])
        dya_ref[...] = (dm * sa).astype(BF16)
        dyr_ref[...] = (dm * sr).astype(BF16)
        dmga_ref[...] = (dm * ya_ref[...] * (sa * (1.0 - sa))).astype(BF16)
        dmgr_ref[...] = (dm * yr_ref[...] * (sr * (1.0 - sr))).astype(BF16)

    out = pl.BlockSpec((tm, tn), lambda i, j: (i, j))
    bf = jax.ShapeDtypeStruct((t, D_MODEL), BF16)
    return pl.pallas_call(
        body, name="merge_bwd", grid=(t // tm, nj),
        in_specs=[pl.BlockSpec((tm, D_MODEL), lambda i, j: (i, 0)), pl.BlockSpec((tn, D_MODEL), lambda i, j: (j, 0)),
                  pl.BlockSpec((tm, tn), lambda i, j: (i, 3 * nj + j)),
                  pl.BlockSpec((tm, tn), lambda i, j: (i, 4 * nj + j)), out, out],
        out_specs=[out, out, out, out],
        out_shape=[bf, bf, bf, bf],
        compiler_params=_cparams(("parallel", "parallel")),
    )(do, wout, zrest, zrest, ya, yr)


def _branch_bwd(dya, dyr, wa, wr, zrest, yatt, ylru):
    t = dya.shape[0]
    tm = min(512, t)
    tn = 512
    nj = D_MODEL // tn

    def body(dya_ref, dyr_ref, wa_ref, wr_ref, ga_ref, gr_ref, ya_ref, yl_ref,
             dyatt_ref, dga_ref, dyl_ref, dgr_ref):
        dga = _dot_nt(dya_ref[...], wa_ref[...])
        dgr = _dot_nt(dyr_ref[...], wr_ref[...])
        g = ga_ref[...]
        sg = _sigmoid(g)
        dyatt_ref[...] = (dga * (g * sg)).astype(BF16)
        dga_ref[...] = (dga * ya_ref[...] * (sg * (1.0 + g * (1.0 - sg)))).astype(BF16)
        g = gr_ref[...]
        sg = _sigmoid(g)
        dyl_ref[...] = dgr * (g * sg)
        dgr_ref[...] = (dgr * yl_ref[...] * (sg * (1.0 + g * (1.0 - sg)))).astype(BF16)

    act = pl.BlockSpec((tm, D_MODEL), lambda i, j: (i, 0))
    wgt = pl.BlockSpec((tn, D_MODEL), lambda i, j: (j, 0))
    out = pl.BlockSpec((tm, tn), lambda i, j: (i, j))
    bf = jax.ShapeDtypeStruct((t, D_MODEL), BF16)
    return pl.pallas_call(
        body, name="branch_bwd", grid=(t // tm, nj),
        in_specs=[act, act, wgt, wgt, pl.BlockSpec((tm, tn), lambda i, j: (i, j)),
                  pl.BlockSpec((tm, tn), lambda i, j: (i, 2 * nj + j)), out, out],
        out_specs=[out, out, out, out],
        out_shape=[bf, bf, jax.ShapeDtypeStruct((t, D_MODEL), F32), bf],
        compiler_params=_cparams(("parallel", "parallel")),
    )(dya, dyr, wa, wr, zrest, zrest, yatt, ylru)


def _dh_partial(parts, name):
    t = parts[0][0].shape[0]
    tm = min(256, t)
    np_ = len(parts)

    def body(*refs):
        o_ref = refs[-1]
        acc = _dot_nt(refs[0][...], refs[np_][...])
        for p in range(1, np_):
            acc = acc + _dot_nt(refs[p][...], refs[np_ + p][...])
        o_ref[...] = acc

    in_specs = [pl.BlockSpec((tm, dz.shape[1]), lambda i: (i, 0)) for dz, _ in parts]
    in_specs += [pl.BlockSpec(w.shape, lambda i: (0, 0)) for _, w in parts]
    return pl.pallas_call(
        body, name=name, grid=(t // tm,),
        in_specs=in_specs,
        out_specs=pl.BlockSpec((tm, D_MODEL), lambda i: (i, 0)),
        out_shape=jax.ShapeDtypeStruct((t, D_MODEL), F32),
        compiler_params=_cparams(("parallel",), vmem_mb=48),
    )(*[dz for dz, _ in parts], *[w for _, w in parts])


def _dh_final(parts, acc_in, x2, dy, wpre):
    t = x2.shape[0]
    tm = min(256, t)
    np_ = len(parts)

    def body(*refs):
        acc_ref, x_ref, dy_ref, w_ref = refs[2 * np_:2 * np_ + 4]
        gx_ref, pw_ref = refs[2 * np_ + 4:]

        @pl.when(pl.program_id(0) == 0)
        def _():
            pw_ref[...] = jnp.zeros_like(pw_ref)

        dh = acc_ref[...]
        for p in range(np_):
            dh = dh + _dot_nt(refs[p][...], refs[np_ + p][...])
        x = x_ref[...]
        r = lax.rsqrt(jnp.mean(x * x, axis=-1, keepdims=True) + NORM_EPS)
        xn = x * r
        dxn = dh * w_ref[...]
        gx_ref[...] = r * (dxn - xn * jnp.mean(dxn * xn, axis=-1, keepdims=True)) + dy_ref[...]
        pw_ref[0:1, :] += jnp.sum(dh * xn, axis=0, keepdims=True)

    row = pl.BlockSpec((tm, D_MODEL), lambda i: (i, 0))
    in_specs = [pl.BlockSpec((tm, dz.shape[1]), lambda i: (i, 0)) for dz, _ in parts]
    in_specs += [pl.BlockSpec(w.shape, lambda i: (0, 0)) for _, w in parts]
    in_specs += [row, row, row, pl.BlockSpec((1, D_MODEL), lambda i: (0, 0))]
    return pl.pallas_call(
        body, name="dh_final", grid=(t // tm,),
        in_specs=in_specs,
        out_specs=[row, pl.BlockSpec((8, D_MODEL), lambda i: (0, 0))],
        out_shape=[jax.ShapeDtypeStruct((t, D_MODEL), F32), jax.ShapeDtypeStruct((8, D_MODEL), F32)],
        compiler_params=_cparams(("arbitrary",), vmem_mb=48),
    )(*[dz for dz, _ in parts], *[w for _, w in parts], acc_in, x2, dy, wpre)


def _adamw(w, g, m, v):
    m = ADAM_B1 * m + (1.0 - ADAM_B1) * g
    v = ADAM_B2 * v + (1.0 - ADAM_B2) * (g * g)
    m_hat = m / (1.0 - ADAM_B1 ** ADAM_STEP)
    v_hat = v / (1.0 - ADAM_B2 ** ADAM_STEP)
    delta = -ADAM_LR * (m_hat / (jnp.sqrt(v_hat) + ADAM_EPS) + ADAM_WD * w)
    return delta, m, v


def _reduce_adamw(parts, w, m, v, name):
    r, c = w.shape
    tr = min(64, r)

    def body(p_ref, w_ref, m_ref, v_ref, g_ref, d_ref, nm_ref, nv_ref):
        g = p_ref[0]
        for j in range(1, N_DEV):
            g = g + p_ref[j]
        d, nm, nv = _adamw(w_ref[...], g, m_ref[...], v_ref[...])
        g_ref[...] = g
        d_ref[...] = d
        nm_ref[...] = nm
        nv_ref[...] = nv

    row = pl.BlockSpec((tr, c), lambda i: (i, 0))
    sh = jax.ShapeDtypeStruct((r, c), F32)
    return pl.pallas_call(
        body, name=name, grid=(r // tr,),
        in_specs=[pl.BlockSpec((N_DEV, tr, c), lambda i: (0, i, 0)), row, row, row],
        out_specs=[row, row, row, row],
        out_shape=[sh, sh, sh, sh],
        compiler_params=_cparams(("parallel",)),
    )(parts, w, m, v)


def _interleave_qkv(a):
    lead = a.shape[:-1]
    return a.reshape(lead + (3, HEAD_PAIRS, LANES)).swapaxes(-3, -2).reshape(lead + (3 * D_MODEL,))


def _deinterleave_qkv(a):
    lead = a.shape[:-1]
    return a.reshape(lead + (HEAD_PAIRS, 3, LANES)).swapaxes(-3, -2).reshape(lead + (3 * D_MODEL,))


def _pack_small(pre, conv_b, rg_ba, rg_bx, lam, post, loss_row, b_in, conv_w_full, rg_wa, rg_wx):
    z = jnp.zeros((1, D_MODEL), F32)
    b_used = jnp.concatenate([b_in[:, 0:3 * D_MODEL], b_in[:, 3 * D_MODEL + HEADS:IN_TOTAL]], axis=1)
    b_f = jnp.pad(b_in[:, 3 * D_MODEL:3 * D_MODEL + HEADS], ((0, 0), (0, D_MODEL - HEADS)))
    return jnp.concatenate([
        pre, conv_b, rg_ba, rg_bx, lam, post, loss_row, z,
        b_used.reshape(9, D_MODEL), b_f, conv_w_full, z, z,
        rg_wa.reshape(64, D_MODEL), rg_wx.reshape(64, D_MODEL)], axis=0)


def _unpack_small(p):
    b_used = p[8:17].reshape(1, 9 * D_MODEL)
    b_in = jnp.concatenate([b_used[:, 0:3 * D_MODEL], p[17:18, 0:HEADS], b_used[:, 3 * D_MODEL:]], axis=1)
    return dict(pre_norm_w=p[0:1], conv_b=p[1:2], rg_ba=p[2:3], rg_bx=p[3:4], rg_lambda=p[4:5],
                post_norm_w=p[5:6], loss_row=p[6:7], b_in=b_in, conv_w_full=p[18:22],
                rg_wa=p[24:88].reshape(1, 16, 64, 64), rg_wx=p[88:152].reshape(1, 16, 64, 64))


def _reduce_small(parts, w, m, v):
    def body(p_ref, w_ref, m_ref, v_ref, g_ref, d_ref, nm_ref, nv_ref):
        g = p_ref[0]
        for j in range(1, N_DEV):
            g = g + p_ref[j]
        d, nm, nv = _adamw(w_ref[...], g, m_ref[...], v_ref[...])
        g_ref[...] = g
        d_ref[...] = d
        nm_ref[...] = nm
        nv_ref[...] = nv

    sh = jax.ShapeDtypeStruct((SMALL_ROWS, D_MODEL), F32)
    return pl.pallas_call(body, name="reduce_small", out_shape=[sh, sh, sh, sh])(parts, w, m, v)


def kernel(x, pre_norm_w, w_in, b_in, conv_w, conv_b, rg_wa, rg_ba, rg_wx, rg_bx, rg_lambda, w_branch_a, w_branch_r, w_out, post_norm_w, loss_target, m_pre_norm_w, m_w_in, m_b_in, m_conv_w, m_conv_b, m_rg_wa, m_rg_ba, m_rg_wx, m_rg_bx, m_rg_lambda, m_w_branch_a, m_w_branch_r, m_w_out, m_post_norm_w, v_pre_norm_w, v_w_in, v_b_in, v_conv_w, v_conv_b, v_rg_wa, v_rg_ba, v_rg_wx, v_rg_bx, v_rg_lambda, v_w_branch_a, v_w_branch_r, v_w_out, v_post_norm_w):
    b, s, _ = x.shape
    t = b * s
    me = 4 * lax.axis_index("x") + 2 * lax.axis_index("y") + lax.axis_index("c")
    shard_rows = D_MODEL // N_DEV

    w_in_all = _exchange(w_in[0].astype(BF16), "gather_w_in")
    w_full = w_in_all.transpose(1, 0, 2).reshape(D_MODEL, IN_TOTAL)
    conv_pad = jnp.pad(conv_w[0], ((0, 4), (0, D_MODEL - LANES)))
    sq_stack = jnp.concatenate([w_branch_a[0], w_branch_r[0], w_out[0], conv_pad], axis=0)
    sq_all = _exchange(sq_stack, "gather_w_sq")
    wa = sq_all[:, 0:shard_rows].reshape(D_MODEL, D_MODEL).astype(BF16)
    wr = sq_all[:, shard_rows:2 * shard_rows].reshape(D_MODEL, D_MODEL).astype(BF16)
    wo = sq_all[:, 2 * shard_rows:3 * shard_rows].reshape(D_MODEL, D_MODEL).astype(BF16)
    conv_full = sq_all[:, 3 * shard_rows:3 * shard_rows + CONV_W, 0:LANES].transpose(1, 0, 2).reshape(CONV_W, D_MODEL)

    w_qkv = _interleave_qkv(w_full[:, 0:3 * D_MODEL])
    w_f = jnp.pad(w_full[:, 3 * D_MODEL:3 * D_MODEL + HEADS], ((0, 0), (0, LANES - HEADS)))
    w_rest = w_full[:, 3 * D_MODEL + HEADS:IN_USED]
    b_qkv = _interleave_qkv(b_in[:, 0:3 * D_MODEL])
    b_f = jnp.pad(b_in[:, 3 * D_MODEL:3 * D_MODEL + HEADS], ((0, 0), (0, LANES - HEADS)))
    b_rest = b_in[:, 3 * D_MODEL + HEADS:IN_USED]

    def blockdiag(w):
        w2 = w.reshape(N_CBLK, 2, HEAD_DIM, HEAD_DIM)
        zz = jnp.zeros((N_CBLK, HEAD_DIM, HEAD_DIM), w.dtype)
        top = jnp.concatenate([w2[:, 0], zz], axis=2)
        bot = jnp.concatenate([zz, w2[:, 1]], axis=2)
        return jnp.concatenate([top, bot], axis=1).astype(BF16)

    bda, bdx = blockdiag(rg_wa[0]), blockdiag(rg_wx[0])

    x2 = x.reshape(t, D_MODEL)
    tgt2 = loss_target.reshape(t, D_MODEL)
    h = _prenorm(x2, pre_norm_w)
    qkv = _mm_bias(h, w_qkv, b_qkv, BF16, "inproj_qkv")
    zrest = _mm_bias(h, w_rest, b_rest, F32, "inproj_rest")
    zf = _mm_bias(h, w_f, b_f, F32, "inproj_f")
    qkv3 = qkv.reshape(b, s, 3 * D_MODEL)
    zrest3 = zrest.reshape(b, s, 5 * D_MODEL)
    zf3 = zf.reshape(b, s, LANES)
    nq = s // ATT_TILE
    cexp3, crow = _fgate_fwd(zf3)
    crow5 = crow.reshape(b, HEAD_PAIRS, 2, nq, ATT_TILE)
    yatt3, lse5, ga3 = _attn_fwd(qkv3, cexp3, crow5, zrest3)
    ylru3, gr3 = _rnn_fwd(zrest3, conv_full, conv_b, bda, bdx, rg_ba, rg_bx, rg_lambda)
    ga, gr = ga3.reshape(t, D_MODEL), gr3.reshape(t, D_MODEL)
    ya, yr, mm = _branch_merge(ga, gr, wa, wr, zrest)
    dy, do, acc_out = _out_loss(mm, wo, x2, tgt2, post_norm_w)

    dya, dyr, dz_mga, dz_mgr = _merge_bwd(do, wo, zrest, ya, yr)
    dyatt, dz_ga, dylru, dz_gr = _branch_bwd(dya, dyr, wa, wr, zrest, yatt3.reshape(t, D_MODEL),
                                             ylru3.reshape(t, D_MODEL))
    dz_xr3, pvec, dbd = _rnn_bwd(zrest3, ylru3, dylru.reshape(b, s, D_MODEL), conv_full, conv_b, bda, bdx,
                                 rg_ba, rg_bx, rg_lambda)
    dqkv3, dc3 = _attn_bwd(qkv3, dyatt.reshape(b, s, D_MODEL), yatt3, lse5, crow5, cexp3)
    dz_f = _fgate_bwd(dc3, zf3).reshape(t, LANES)
    dz_qkv = dqkv3.reshape(t, 3 * D_MODEL)
    dz_xr = dz_xr3.reshape(t, D_MODEL)

    wt = lambda lo: w_rest[:, lo * D_MODEL:(lo + 1) * D_MODEL]
    dh_a = _dh_partial([(dz_qkv, w_qkv), (dz_f, w_f)], "dh_qkv")
    grad_x2, acc_pre = _dh_final(
        [(dz_ga, wt(0)), (dz_xr, wt(1)), (dz_gr, wt(2)), (dz_mga, wt(3)), (dz_mgr, wt(4))],
        dh_a, x2, dy, pre_norm_w)

    dw_qkv, db_qkv = _mm_tn(h, dz_qkv, "dw_qkv")
    dw_f, db_f = _mm_tn(h, dz_f, "dw_f")
    dw_parts, db_parts = [], []
    for nm, dzp in (("ga", dz_ga), ("xr", dz_xr), ("gr", dz_gr), ("mga", dz_mga), ("mgr", dz_mgr)):
        dwp, dbp = _mm_tn(h, dzp, "dw_" + nm)
        dw_parts.append(dwp)
        db_parts.append(dbp[0:1])
    dw_a, _ = _mm_tn(ga, dya, "dw_a")
    dw_r, _ = _mm_tn(gr, dyr, "dw_r")
    dw_o, _ = _mm_tn(mm, do, "dw_o")

    zeros_tail = jnp.zeros((D_MODEL, IN_TOTAL - IN_USED), F32)
    dw_in_full = jnp.concatenate([_deinterleave_qkv(dw_qkv), dw_f[:, 0:HEADS]] + dw_parts + [zeros_tail], axis=1)
    dw_in_send = dw_in_full.reshape(D_MODEL, N_DEV, W_SHARD).transpose(1, 0, 2)
    dw_sq_send = jnp.concatenate([dw_a.reshape(N_DEV, shard_rows, D_MODEL), dw_r.reshape(N_DEV, shard_rows, D_MODEL),
                                  dw_o.reshape(N_DEV, shard_rows, D_MODEL)], axis=1)

    db_in_full = jnp.concatenate([_deinterleave_qkv(db_qkv[0:1]), db_f[0:1, 0:HEADS]] + db_parts
                                 + [jnp.zeros((1, IN_TOTAL - IN_USED), F32)], axis=1)
    d_rg_wa = jnp.stack([dbd[:, 0, 0:HEAD_DIM, 0:HEAD_DIM], dbd[:, 0, HEAD_DIM:, HEAD_DIM:]], axis=1)
    d_rg_wx = jnp.stack([dbd[:, 1, 0:HEAD_DIM, 0:HEAD_DIM], dbd[:, 1, HEAD_DIM:, HEAD_DIM:]], axis=1)
    small_g = _pack_small(acc_pre[0:1], pvec[4:5], pvec[5:6], pvec[6:7], pvec[7:8], acc_out[0:1], acc_out[1:2],
                          db_in_full, pvec[0:4], d_rg_wa, d_rg_wx)

    recv_in = _exchange(dw_in_send, "scatter_dw_in")
    recv_sq = _exchange(dw_sq_send, "scatter_dw_sq")
    small_all = _exchange(small_g, "gather_small")

    g_in, d_in, nm_in, nv_in = _reduce_adamw(recv_in, w_in[0], m_w_in[0], v_w_in[0], "adamw_w_in")
    sq_w = jnp.concatenate([w_branch_a[0], w_branch_r[0], w_out[0]], axis=0)
    sq_m = jnp.concatenate([m_w_branch_a[0], m_w_branch_r[0], m_w_out[0]], axis=0)
    sq_v = jnp.concatenate([v_w_branch_a[0], v_w_branch_r[0], v_w_out[0]], axis=0)
    g_sq, d_sq, nm_sq, nv_sq = _reduce_adamw(recv_sq, sq_w, sq_m, sq_v, "adamw_w_sq")

    def place_conv(a):
        return lax.dynamic_update_slice(jnp.zeros((CONV_W, D_MODEL), F32), a[0], (0, me * LANES))

    zrow = jnp.zeros((1, D_MODEL), F32)
    small_w = _pack_small(pre_norm_w, conv_b, rg_ba, rg_bx, rg_lambda, post_norm_w, zrow, b_in,
                          place_conv(conv_w), rg_wa[0], rg_wx[0])
    small_m = _pack_small(m_pre_norm_w, m_conv_b, m_rg_ba, m_rg_bx, m_rg_lambda, m_post_norm_w, zrow, m_b_in,
                          place_conv(m_conv_w), m_rg_wa[0], m_rg_wx[0])
    small_v = _pack_small(v_pre_norm_w, v_conv_b, v_rg_ba, v_rg_bx, v_rg_lambda, v_post_norm_w, zrow, v_b_in,
                          place_conv(v_conv_w), v_rg_wa[0], v_rg_wx[0])
    outs_small = [_unpack_small(p) for p in _reduce_small(small_all, small_w, small_m, small_v)]

    loss = (0.5 / D_MODEL) * jnp.sum(outs_small[0]["loss_row"])

    def leaf(kind, name):
        if name == "w_in":
            return (g_in, d_in, nm_in, nv_in)[kind][None]
        if name in ("w_branch_a", "w_branch_r", "w_out"):
            j = ("w_branch_a", "w_branch_r", "w_out").index(name)
            return (g_sq, d_sq, nm_sq, nv_sq)[kind][None, j * shard_rows:(j + 1) * shard_rows]
        if name == "conv_w":
            return lax.dynamic_slice(outs_small[kind]["conv_w_full"], (0, me * LANES), (CONV_W, LANES))[None]
        return outs_small[kind][name]

    names = ["pre_norm_w", "w_in", "b_in", "conv_w", "conv_b", "rg_wa", "rg_ba", "rg_wx", "rg_bx", "rg_lambda",
             "w_branch_a", "w_branch_r", "w_out", "post_norm_w"]
    out = [loss, grad_x2.reshape(b, s, D_MODEL)]
    for kind in range(4):
        out += [leaf(kind, nm) for nm in names]
    return tuple(out)
```

```python
import jax
import jax.numpy as jnp
from jax import lax
from jax.experimental import pallas as pl
from jax.experimental.pallas import tpu as pltpu

F32 = jnp.float32
BF16 = jnp.bfloat16

N_DEV = 8
D_MODEL = 1024
HEADS = 16
HEAD_DIM = 64
HEAD_PAIRS = HEADS // 2
LANES = 128
N_CBLK = D_MODEL // LANES
CONV_W = 4
RG_C = 8.0
NORM_EPS = 1e-6
MASK_VALUE = -1e30
IN_USED = 8208
IN_TOTAL = 9232
W_SHARD = IN_TOTAL // N_DEV

ADAM_LR = 0.001
ADAM_B1 = 0.9
ADAM_B2 = 0.999
ADAM_EPS = 1e-08
ADAM_WD = 0.01
ADAM_STEP = 10

ATT_TILE = 256
SCAN_TILE = 256
SMALL_ROWS = 152


def _cparams(sem=None, vmem_mb=None):
    kw = {}
    if sem is not None:
        kw["dimension_semantics"] = sem
    if vmem_mb is not None:
        kw["vmem_limit_bytes"] = vmem_mb * 1024 * 1024
    return pltpu.CompilerParams(**kw)


def _sigmoid(x):
    return 1.0 / (1.0 + jnp.exp(-x))


def _softplus(x):
    return jnp.maximum(x, 0.0) + jnp.log1p(jnp.exp(-jnp.abs(x)))


def _expm1(x):
    p = x * (1.0 + x * (1.0 / 2 + x * (1.0 / 6 + x * (1.0 / 24 + x * (1.0 / 120 + x * (
        1.0 / 720 + x * (1.0 / 5040 + x * (1.0 / 40320))))))))
    return jnp.where(jnp.abs(x) < 0.5, p, jnp.exp(x) - 1.0)


def _split3(x):
    hi = x.astype(BF16)
    r1 = x - hi.astype(F32)
    mid = r1.astype(BF16)
    lo = (r1 - mid.astype(F32)).astype(BF16)
    return hi, mid, lo


def _dot(a, b):
    return jnp.dot(a, b, preferred_element_type=F32)


def _dot_nt(a, b):
    return lax.dot_general(a, b, (((1,), (1,)), ((), ())), preferred_element_type=F32)


def _dot_tn(a, b):
    return lax.dot_general(a, b, (((0,), (0,)), ((), ())), preferred_element_type=F32)


def _iota(shape, dim):
    return lax.broadcasted_iota(jnp.int32, shape, dim)


def _exchange(src, name):
    gather = src.ndim == 2
    blk = src.shape if gather else src.shape[1:]

    def body(src_ref, out_ref, send_sems, recv_sems, local_sem):
        x, y, c = lax.axis_index("x"), lax.axis_index("y"), lax.axis_index("c")
        me = 4 * x + 2 * y + c

        def peer(k):
            return (1 - x if k & 4 else x, 1 - y if k & 2 else y, 1 - c if k & 1 else c)

        def index(p):
            return 4 * p[0] + 2 * p[1] + p[2]

        def block_for(j):
            return src_ref if gather else src_ref.at[j]

        def copy(k, dst_slot):
            p = peer(k)
            return pltpu.make_async_remote_copy(
                src_ref=block_for(index(p)), dst_ref=out_ref.at[dst_slot],
                send_sem=send_sems.at[k - 1], recv_sem=recv_sems.at[k - 1],
                device_id=p, device_id_type=pl.DeviceIdType.MESH)

        local = pltpu.make_async_copy(block_for(me), out_ref.at[me], local_sem)
        local.start()
        sends = [copy(k, me) for k in range(1, N_DEV)]
        for cp in sends:
            cp.start()
        for k in range(1, N_DEV):
            copy(k, index(peer(k))).wait_recv()
        for cp in sends:
            cp.wait_send()
        local.wait()

    return pl.pallas_call(
        body, name=name,
        out_shape=jax.ShapeDtypeStruct((N_DEV,) + tuple(blk), src.dtype),
        in_specs=[pl.BlockSpec(memory_space=pl.ANY)],
        out_specs=pl.BlockSpec(memory_space=pl.ANY),
        scratch_shapes=[pltpu.SemaphoreType.DMA((N_DEV - 1,)), pltpu.SemaphoreType.DMA((N_DEV - 1,)),
                        pltpu.SemaphoreType.DMA],
    )(src)


def _prenorm(x2, w):
    t = x2.shape[0]
    tm = min(512, t)

    def body(x_ref, w_ref, h_ref):
        x = x_ref[...]
        r = lax.rsqrt(jnp.mean(x * x, axis=-1, keepdims=True) + NORM_EPS)
        h_ref[...] = (x * r * w_ref[...]).astype(BF16)

    return pl.pallas_call(
        body, name="prenorm", grid=(t // tm,),
        in_specs=[pl.BlockSpec((tm, D_MODEL), lambda i: (i, 0)), pl.BlockSpec((1, D_MODEL), lambda i: (0, 0))],
        out_specs=pl.BlockSpec((tm, D_MODEL), lambda i: (i, 0)),
        out_shape=jax.ShapeDtypeStruct((t, D_MODEL), BF16),
        compiler_params=_cparams(("parallel",)),
    )(x2, w)


def _mm_bias(a, b, bias, out_dtype, name):
    m, k = a.shape
    n = b.shape[1]
    tm = min(512, m)
    tn = min(1024, n)

    def body(a_ref, b_ref, bias_ref, o_ref):
        o_ref[...] = (_dot(a_ref[...], b_ref[...]) + bias_ref[...]).astype(o_ref.dtype)

    return pl.pallas_call(
        body, name=name, grid=(n // tn, m // tm),
        in_specs=[pl.BlockSpec((tm, k), lambda j, i: (i, 0)), pl.BlockSpec((k, tn), lambda j, i: (0, j)),
                  pl.BlockSpec((1, tn), lambda j, i: (0, j))],
        out_specs=pl.BlockSpec((tm, tn), lambda j, i: (i, j)),
        out_shape=jax.ShapeDtypeStruct((m, n), out_dtype),
        compiler_params=_cparams(("parallel", "parallel")),
    )(a, b, bias)


def _mm_tn(a, b, name):
    t, m = a.shape
    n = b.shape[1]
    tn = min(1024, n)
    tk = min(512, t)

    def body(a_ref, b_ref, o_ref, s_ref):
        kk = pl.program_id(1)

        @pl.when(kk == 0)
        def _():
            o_ref[...] = jnp.zeros_like(o_ref)
            s_ref[...] = jnp.zeros_like(s_ref)

        bb = b_ref[...]
        o_ref[...] += _dot_tn(a_ref[...], bb)
        s_ref[0:1, :] += jnp.sum(bb.astype(F32), axis=0, keepdims=True)

    return pl.pallas_call(
        body, name=name, grid=(n // tn, t // tk),
        in_specs=[pl.BlockSpec((tk, m), lambda j, kk: (kk, 0)), pl.BlockSpec((tk, tn), lambda j, kk: (kk, j))],
        out_specs=[pl.BlockSpec((m, tn), lambda j, kk: (0, j)), pl.BlockSpec((8, tn), lambda j, kk: (0, j))],
        out_shape=[jax.ShapeDtypeStruct((m, n), F32), jax.ShapeDtypeStruct((8, n), F32)],
        compiler_params=_cparams(("parallel", "arbitrary")),
    )(a, b)


def _fgate_fwd(zf3):
    b, s, _ = zf3.shape
    tb = SCAN_TILE
    nb = s // tb

    def body(z_ref, cexp_ref, crow_ref):
        tri = (_iota((tb, tb), 1) <= _iota((tb, tb), 0)).astype(BF16)
        expand = ((_iota((LANES, D_MODEL), 1) >> 6) == _iota((LANES, D_MODEL), 0)).astype(BF16)
        carry = jnp.zeros((1, LANES), F32)
        for i in range(nb):
            rows = slice(i * tb, (i + 1) * tb)
            z = z_ref[rows, :]
            lf = jnp.minimum(z, 0.0) - jnp.log1p(jnp.exp(-jnp.abs(z)))
            cb = sum(_dot(tri, part) for part in _split3(lf)) + carry
            carry = cb[tb - 1:tb, :]
            cexp_ref[rows, :] = sum(_dot(part, expand) for part in _split3(cb))
            crow_ref[:, rows] = cb.T[0:HEADS, :]

    return pl.pallas_call(
        body, name="fgate_fwd", grid=(b,),
        in_specs=[pl.BlockSpec((None, s, LANES), lambda i: (i, 0, 0))],
        out_specs=[pl.BlockSpec((None, s, D_MODEL), lambda i: (i, 0, 0)),
                   pl.BlockSpec((None, HEADS, s), lambda i: (i, 0, 0))],
        out_shape=[jax.ShapeDtypeStruct((b, s, D_MODEL), F32), jax.ShapeDtypeStruct((b, HEADS, s), F32)],
        compiler_params=_cparams(("parallel",)),
    )(zf3)


def _fgate_bwd(dc3, zf3):
    b, s, _ = zf3.shape
    tb = SCAN_TILE
    nb = s // tb

    def body(dc_ref, z_ref, o_ref):
        tri = (_iota((tb, tb), 1) >= _iota((tb, tb), 0)).astype(BF16)
        carry = jnp.zeros((1, LANES), F32)
        for i in reversed(range(nb)):
            rows = slice(i * tb, (i + 1) * tb)
            dlf = sum(_dot(tri, part) for part in _split3(dc_ref[rows, :])) + carry
            carry = dlf[0:1, :]
            o_ref[rows, :] = (dlf * _sigmoid(-z_ref[rows, :])).astype(BF16)

    return pl.pallas_call(
        body, name="fgate_bwd", grid=(b,),
        in_specs=[pl.BlockSpec((None, s, LANES), lambda i: (i, 0, 0)),
                  pl.BlockSpec((None, s, LANES), lambda i: (i, 0, 0))],
        out_specs=pl.BlockSpec((None, s, LANES), lambda i: (i, 0, 0)),
        out_shape=jax.ShapeDtypeStruct((b, s, LANES), BF16),
        compiler_params=_cparams(("parallel",)),
    )(dc3, zf3)


def _spare(hh):
    return HEAD_DIM if hh == 0 else 0


def _put_cols(tile, mine, cols, first):
    lane = _iota((1, LANES), 1)
    out = jnp.where(mine, tile, jnp.zeros((), tile.dtype))
    for j, c in enumerate(cols):
        out = jnp.where(lane == first + j, c, out)
    return out


def _put_rows(tile, mine, rows, first):
    sub = _iota((LANES, 1), 0)
    out = jnp.where(mine, tile, jnp.zeros((), tile.dtype))
    for j, r in enumerate(rows):
        out = jnp.where(sub == first + j, r, out)
    return out


def _transpose_bf16(a):
    return a.astype(F32).T.astype(BF16)


def _attn_fwd(qkv3, cexp3, crow5, zrest3):
    b, s, _ = qkv3.shape
    ta = ATT_TILE
    nq = s // ta
    hd = HEAD_DIM

    def body(qkv_ref, cq_ref, ck_ref, g_ref, y_ref, lse_ref, ga_ref, kt_scr, v_scr):
        lane = _iota((1, LANES), 1)
        sub = _iota((LANES, 1), 0)
        lane_mine = (lane < hd, lane >= hd)
        sub_mine = (sub < hd, sub >= hd)
        causal = _iota((ta, ta), 0) >= _iota((ta, ta), 1)
        one = jnp.ones((), BF16)

        for kj in range(nq):
            rows = slice(kj * ta, (kj + 1) * ta)
            kt = _transpose_bf16(qkv_ref[rows, LANES:2 * LANES])
            v = qkv_ref[rows, 2 * LANES:3 * LANES]
            for hh in range(2):
                ck = list(_split3(-ck_ref[hh, kj:kj + 1, :]))
                kt_scr[hh, kj] = _put_rows(kt, sub_mine[hh], [one, one, one] + ck, _spare(hh))
                v_scr[hh, kj] = _put_cols(v, lane_mine[hh], [one], _spare(hh))

        for qi in range(nq):
            rows = slice(qi * ta, (qi + 1) * ta)
            q = qkv_ref[rows, 0:LANES] * 0.125
            cq = cq_ref[rows, :]
            qh = [_put_cols(q, lane_mine[hh], list(_split3(cq[:, hh * hd:hh * hd + 1])) + [one, one, one], _spare(hh))
                  for hh in range(2)]
            st = [(jnp.full((ta, 1), MASK_VALUE, F32), jnp.zeros((ta, LANES), F32))] * 2
            for kj in range(qi + 1):
                for hh in range(2):
                    m, acc = st[hh]
                    sc = _dot(qh[hh], kt_scr[hh, kj])
                    if kj == qi:
                        sc = jnp.where(causal, sc, MASK_VALUE)
                    mn = jnp.maximum(m, jnp.max(sc, axis=-1, keepdims=True))
                    p = jnp.exp(sc - mn).astype(BF16)
                    st[hh] = (mn, jnp.exp(m - mn) * acc + _dot(p, v_scr[hh, kj]))
            (ma, acca), (mb, accb) = st
            la = acca[:, hd:hd + 1]
            lb = accb[:, 0:1]
            y = jnp.where(lane_mine[0], acca * (1.0 / la), accb * (1.0 / lb))
            lse = jnp.where(lane_mine[0], ma + jnp.log(la), mb + jnp.log(lb)).T
            lse_ref[0, qi:qi + 1, :] = lse[0:1, :]
            lse_ref[1, qi:qi + 1, :] = lse[hd:hd + 1, :]
            y_ref[rows, :] = y
            g = g_ref[rows, :]
            ga_ref[rows, :] = (y * (g * _sigmoid(g))).astype(BF16)

    blk = lambda w: pl.BlockSpec((None, s, w), lambda i, p: (i, 0, p))
    rows5 = pl.BlockSpec((None, None, 2, nq, ta), lambda i, p: (i, p, 0, 0, 0))
    return pl.pallas_call(
        body, name="attn_fwd", grid=(b, HEAD_PAIRS),
        in_specs=[blk(3 * LANES), blk(LANES), rows5, blk(LANES)],
        out_specs=[blk(LANES), rows5, blk(LANES)],
        out_shape=[jax.ShapeDtypeStruct((b, s, D_MODEL), F32),
                   jax.ShapeDtypeStruct((b, HEAD_PAIRS, 2, nq, ta), F32),
                   jax.ShapeDtypeStruct((b, s, D_MODEL), BF16)],
        scratch_shapes=[pltpu.VMEM((2, nq, LANES, ta), BF16), pltpu.VMEM((2, nq, ta, LANES), BF16)],
        compiler_params=_cparams(("parallel", "parallel")),
    )(qkv3, cexp3, crow5, zrest3)


def _attn_bwd(qkv3, do3, y3, lse5, crow5, cexp3):
    b, s, _ = qkv3.shape
    ta = ATT_TILE
    nq = s // ta
    hd = HEAD_DIM

    def body(qkv_ref, do_ref, y_ref, lse_ref, crow_ref, cexp_ref, dqkv_ref, dc_ref,
             qa_scr, doa_scr, qst_scr, dot_scr, kt_scr, vt_scr, dq_scr, rs_scr):
        pair = pl.program_id(1)
        lane = _iota((1, LANES), 1)
        sub = _iota((LANES, 1), 0)
        lane_mine = (lane < hd, lane >= hd)
        sub_mine = (sub < hd, sub >= hd)
        causal = _iota((ta, ta), 0) >= _iota((ta, ta), 1)
        one = jnp.ones((), BF16)
        zero = jnp.zeros((), BF16)

        @pl.when(pair == 0)
        def _():
            dc_ref[...] = jnp.zeros_like(dc_ref)

        for i in range(nq):
            rows = slice(i * ta, (i + 1) * ta)
            qs = qkv_ref[rows, 0:LANES] * 0.125
            qst = _transpose_bf16(qs)
            kt = _transpose_bf16(qkv_ref[rows, LANES:2 * LANES])
            vt = _transpose_bf16(qkv_ref[rows, 2 * LANES:3 * LANES])
            do = do_ref[rows, :]
            dof = do.astype(F32)
            dot = dof.T.astype(BF16)
            pr = y_ref[rows, :] * dof
            cq = cexp_ref[rows, :]
            lse_c = jnp.where(sub == 0, lse_ref[0, i:i + 1, :],
                              jnp.where(sub == 1, lse_ref[1, i:i + 1, :], 0.0)).T
            for hh in range(2):
                sp = _spare(hh)
                dsum = jnp.sum(jnp.where(lane_mine[hh], pr, 0.0), axis=-1, keepdims=True)
                bias = cq[:, hh * hd:hh * hd + 1] - lse_c[:, hh:hh + 1]
                qa_scr[hh, i] = _put_cols(qs, lane_mine[hh], list(_split3(bias)) + [one, one, one], sp)
                doa_scr[hh, i] = _put_cols(do, lane_mine[hh], list(_split3(-dsum)), sp)
                qst_scr[hh, i] = jnp.where(sub_mine[hh], qst, zero)
                dot_scr[hh, i] = jnp.where(sub_mine[hh], dot, zero)
                ck = list(_split3(-crow_ref[hh, i:i + 1, :]))
                kt_scr[hh, i] = _put_rows(kt, sub_mine[hh], [one, one, one] + ck, sp)
                vt_scr[hh, i] = _put_rows(vt, sub_mine[hh], [one, one, one], sp)
            dq_scr[i] = jnp.zeros((ta, LANES), F32)
            rs_scr[i] = jnp.zeros((ta, LANES), F32)

        for kj in range(nq):
            krows = slice(kj * ta, (kj + 1) * ta)
            k = qkv_ref[krows, LANES:2 * LANES]
            km = (jnp.where(lane_mine[0], k, zero), jnp.where(lane_mine[1], k, zero))
            dkt = jnp.zeros((LANES, ta), F32)
            dvt = jnp.zeros((LANES, ta), F32)
            dcp = [jnp.zeros((8, ta), F32), jnp.zeros((8, ta), F32)]
            for qi in range(kj, nq):
                dq = jnp.zeros((ta, LANES), F32)
                rs = []
                for hh in range(2):
                    sc = _dot(qa_scr[hh, qi], kt_scr[hh, kj])
                    if qi == kj:
                        sc = jnp.where(causal, sc, MASK_VALUE)
                    p = jnp.exp(sc)
                    dsf = p * _dot(doa_scr[hh, qi], vt_scr[hh, kj])
                    dcp[hh] = dcp[hh] + jnp.sum(dsf.reshape(ta // 8, 8, ta), axis=0)
                    rs.append(jnp.sum(dsf, axis=-1, keepdims=True))
                    ds = dsf.astype(BF16)
                    dq = dq + _dot(ds, km[hh])
                    dkt = dkt + _dot(qst_scr[hh, qi], ds)
                    dvt = dvt + _dot(dot_scr[hh, qi], p.astype(BF16))
                dq_scr[qi] += dq
                rs_scr[qi] += jnp.where(lane == 0, rs[0], jnp.where(lane == 1, rs[1], 0.0))
            dqkv_ref[krows, LANES:2 * LANES] = dkt.T.astype(BF16)
            dqkv_ref[krows, 2 * LANES:3 * LANES] = dvt.T.astype(BF16)
            dca = jnp.sum(dcp[0], axis=0, keepdims=True)
            dcb = jnp.sum(dcp[1], axis=0, keepdims=True)
            dcs = jnp.where(sub == 0, dca, jnp.where(sub == 1, dcb, 0.0)).T
            dc_ref[krows, :] += (jnp.where(lane == 2 * pair, -dcs[:, 0:1], 0.0)
                                 + jnp.where(lane == 2 * pair + 1, -dcs[:, 1:2], 0.0))
        for qi in range(nq):
            rows = slice(qi * ta, (qi + 1) * ta)
            dqkv_ref[rows, 0:LANES] = (dq_scr[qi] * 0.125).astype(BF16)
            rq = rs_scr[qi]
            dc_ref[rows, :] += (jnp.where(lane == 2 * pair, rq[:, 0:1], 0.0)
                                + jnp.where(lane == 2 * pair + 1, rq[:, 1:2], 0.0))

    blk = lambda w: pl.BlockSpec((None, s, w), lambda i, p: (i, 0, p))
    rows5 = pl.BlockSpec((None, None, 2, nq, ta), lambda i, p: (i, p, 0, 0, 0))
    by_rows = lambda: pltpu.VMEM((2, nq, ta, LANES), BF16)
    by_cols = lambda: pltpu.VMEM((2, nq, LANES, ta), BF16)
    return pl.pallas_call(
        body, name="attn_bwd", grid=(b, HEAD_PAIRS),
        in_specs=[blk(3 * LANES), blk(LANES), blk(LANES), rows5, rows5, blk(LANES)],
        out_specs=[blk(3 * LANES), pl.BlockSpec((None, s, LANES), lambda i, p: (i, 0, 0))],
        out_shape=[jax.ShapeDtypeStruct((b, s, 3 * D_MODEL), BF16), jax.ShapeDtypeStruct((b, s, LANES), F32)],
        scratch_shapes=[by_rows(), by_rows(), by_cols(), by_cols(), by_cols(), by_cols(),
                        pltpu.VMEM((nq, ta, LANES), F32), pltpu.VMEM((nq, ta, LANES), F32)],
        compiler_params=_cparams(("parallel", "arbitrary")),
    )(qkv3, do3, y3, lse5, crow5, cexp3)


def _rnn_common(xr, cw_ref, cb_ref, bda_ref, bdx_ref, ba_ref, bx_ref, lam_ref, s):
    rows = _iota((s, LANES), 0)

    def down(v, k):
        return jnp.where(rows >= k, pltpu.roll(v, k, 0), 0.0)

    x1, x2, x3 = down(xr, 1), down(xr, 2), down(xr, 3)
    xc = cb_ref[...] + cw_ref[0:1, :] * x3
    xc = xc + cw_ref[1:2, :] * x2
    xc = xc + cw_ref[2:3, :] * x1
    xc = xc + cw_ref[3:4, :] * xr
    xcb = xc.astype(BF16)
    r = _sigmoid(_dot(xcb, bda_ref[...]) + ba_ref[...])
    i = _sigmoid(_dot(xcb, bdx_ref[...]) + bx_ref[...])
    sp = _softplus(-lam_ref[...])
    log_a = (-RG_C * r) * sp
    a = jnp.exp(log_a)
    e2 = -_expm1(2.0 * log_a)
    sq = jnp.sqrt(jnp.maximum(e2, 0.0))
    return rows, (x1, x2, x3), xc, xcb, r, i, sp, a, e2, sq


def _rnn_specs(s):
    blk = lambda off: pl.BlockSpec((None, s, LANES), lambda cb, i: (i, 0, off + cb))
    vec = lambda r: pl.BlockSpec((r, LANES), lambda cb, i: (0, cb))
    mat = pl.BlockSpec((None, LANES, LANES), lambda cb, i: (cb, 0, 0))
    return blk, vec, mat


def _rnn_fwd(zrest3, conv_w, conv_b, bda, bdx, ba, bx, lam):
    b, s, _ = zrest3.shape

    def body(xr_ref, g_ref, cw_ref, cb_ref, bda_ref, bdx_ref, ba_ref, bx_ref, lam_ref, h_ref, gr_ref):
        xr = xr_ref[...]
        rows, _, xc, _, _, i, _, a, _, sq = _rnn_common(
            xr, cw_ref, cb_ref, bda_ref, bdx_ref, ba_ref, bx_ref, lam_ref, s)
        u = sq * (i * xc)
        sh = 1
        while sh < s:
            keep = rows >= sh
            ur = jnp.where(keep, pltpu.roll(u, sh, 0), 0.0)
            u = u + a * ur
            if sh * 2 < s:
                a = a * jnp.where(keep, pltpu.roll(a, sh, 0), 1.0)
            sh *= 2
        h_ref[...] = u
        g = g_ref[...]
        gr_ref[...] = (u * (g * _sigmoid(g))).astype(BF16)

    blk, vec, mat = _rnn_specs(s)
    return pl.pallas_call(
        body, name="rnn_fwd", grid=(N_CBLK, b),
        in_specs=[blk(N_CBLK), blk(2 * N_CBLK), vec(CONV_W), vec(1), mat, mat, vec(1), vec(1), vec(1)],
        out_specs=[blk(0), blk(0)],
        out_shape=[jax.ShapeDtypeStruct((b, s, D_MODEL), F32), jax.ShapeDtypeStruct((b, s, D_MODEL), BF16)],
        compiler_params=_cparams(("parallel", "parallel")),
    )(zrest3, zrest3, conv_w, conv_b, bda, bdx, ba, bx, lam)


def _rnn_bwd(zrest3, h3, dh3, conv_w, conv_b, bda, bdx, ba, bx, lam):
    b, s, _ = zrest3.shape

    def body(xr_ref, h_ref, dh_ref, cw_ref, cb_ref, bda_ref, bdx_ref, ba_ref, bx_ref, lam_ref,
             dxr_ref, pv_ref, dbd_ref):
        @pl.when(pl.program_id(1) == 0)
        def _():
            pv_ref[...] = jnp.zeros_like(pv_ref)
            dbd_ref[...] = jnp.zeros_like(dbd_ref)

        xr = xr_ref[...]
        rows, (x1, x2, x3), xc, xcb, r, i, sp, a, e2, sq = _rnn_common(
            xr, cw_ref, cb_ref, bda_ref, bdx_ref, ba_ref, bx_ref, lam_ref, s)
        h = h_ref[...]
        g = dh_ref[...]
        an = jnp.where(rows < s - 1, pltpu.roll(a, s - 1, 0), 0.0)
        sh = 1
        while sh < s:
            keep = rows < s - sh
            gr = jnp.where(keep, pltpu.roll(g, s - sh, 0), 0.0)
            g = g + an * gr
            if sh * 2 < s:
                an = an * jnp.where(keep, pltpu.roll(an, s - sh, 0), 1.0)
            sh *= 2
        hp = jnp.where(rows >= 1, pltpu.roll(h, 1, 0), 0.0)
        da = g * hp
        dsq = g * (i * xc)
        di = g * (sq * xc)
        dxc = g * (sq * i)
        dlog = da * a - dsq * ((1.0 - e2) / sq)
        dr = dlog * (-RG_C * sp)
        dpr = dr * (r * (1.0 - r))
        dpi = di * (i * (1.0 - i))
        dprb = dpr.astype(BF16)
        dpib = dpi.astype(BF16)
        dxc = dxc + _dot_nt(dprb, bda_ref[...]) + _dot_nt(dpib, bdx_ref[...])

        def up(v, k):
            return jnp.where(rows < s - k, pltpu.roll(v, s - k, 0), 0.0)

        dxr = cw_ref[3:4, :] * dxc + cw_ref[2:3, :] * up(dxc, 1) + cw_ref[1:2, :] * up(dxc, 2) \
            + cw_ref[0:1, :] * up(dxc, 3)
        dxr_ref[...] = dxr.astype(BF16)

        def colsum(v):
            return jnp.sum(v, axis=0, keepdims=True)

        pv_ref[0:1, :] += colsum(dxc * x3)
        pv_ref[1:2, :] += colsum(dxc * x2)
        pv_ref[2:3, :] += colsum(dxc * x1)
        pv_ref[3:4, :] += colsum(dxc * xr)
        pv_ref[4:5, :] += colsum(dxc)
        pv_ref[5:6, :] += colsum(dpr)
        pv_ref[6:7, :] += colsum(dpi)
        pv_ref[7:8, :] += colsum(dlog * r) * (RG_C * _sigmoid(-lam_ref[...]))
        dbd_ref[0] += _dot_tn(xcb, dprb)
        dbd_ref[1] += _dot_tn(xcb, dpib)

    blk, vec, mat = _rnn_specs(s)
    hblk = pl.BlockSpec((None, s, LANES), lambda cb, i: (i, 0, cb))
    return pl.pallas_call(
        body, name="rnn_bwd", grid=(N_CBLK, b),
        in_specs=[blk(N_CBLK), hblk, hblk, vec(CONV_W), vec(1), mat, mat, vec(1), vec(1), vec(1)],
        out_specs=[hblk, pl.BlockSpec((8, LANES), lambda cb, i: (0, cb)),
                   pl.BlockSpec((None, 2, LANES, LANES), lambda cb, i: (cb, 0, 0, 0))],
        out_shape=[jax.ShapeDtypeStruct((b, s, D_MODEL), BF16), jax.ShapeDtypeStruct((8, D_MODEL), F32),
                   jax.ShapeDtypeStruct((N_CBLK, 2, LANES, LANES), F32)],
        compiler_params=_cparams(("parallel", "arbitrary")),
    )(zrest3, h3, dh3, conv_w, conv_b, bda, bdx, ba, bx, lam)


def _branch_merge(ga, gr, wa, wr, zrest):
    t = ga.shape[0]
    tm = min(512, t)
    tn = 512

    def body(ga_ref, gr_ref, wa_ref, wr_ref, mga_ref, mgr_ref, ya_ref, yr_ref, m_ref):
        ya = _dot(ga_ref[...], wa_ref[...])
        yr = _dot(gr_ref[...], wr_ref[...])
        ya_ref[...] = ya
        yr_ref[...] = yr
        m_ref[...] = (_sigmoid(mga_ref[...]) * ya + _sigmoid(mgr_ref[...]) * yr).astype(BF16)

    nj = D_MODEL // tn
    act = pl.BlockSpec((tm, D_MODEL), lambda i, j: (i, 0))
    wgt = pl.BlockSpec((D_MODEL, tn), lambda i, j: (0, j))
    out = pl.BlockSpec((tm, tn), lambda i, j: (i, j))
    return pl.pallas_call(
        body, name="branch_merge", grid=(t // tm, nj),
        in_specs=[act, act, wgt, wgt, pl.BlockSpec((tm, tn), lambda i, j: (i, 3 * nj + j)),
                  pl.BlockSpec((tm, tn), lambda i, j: (i, 4 * nj + j))],
        out_specs=[out, out, out],
        out_shape=[jax.ShapeDtypeStruct((t, D_MODEL), F32), jax.ShapeDtypeStruct((t, D_MODEL), F32),
                   jax.ShapeDtypeStruct((t, D_MODEL), BF16)],
        compiler_params=_cparams(("parallel", "parallel")),
    )(ga, gr, wa, wr, zrest, zrest)


def _out_loss(m, wout, x2, tgt2, wpost):
    t = m.shape[0]
    tm = min(256, t)

    def body(m_ref, w_ref, x_ref, t_ref, wp_ref, dy_ref, do_ref, acc_ref):
        @pl.when(pl.program_id(0) == 0)
        def _():
            acc_ref[...] = jnp.zeros_like(acc_ref)

        o = _dot(m_ref[...], w_ref[...])
        r2 = lax.rsqrt(jnp.mean(o * o, axis=-1, keepdims=True) + NORM_EPS)
        n = o * r2
        wp = wp_ref[...]
        err = (x_ref[...] + n * wp) - t_ref[...]
        dy = err * (1.0 / D_MODEL)
        dn = dy * wp
        do = r2 * (dn - n * jnp.mean(dn * n, axis=-1, keepdims=True))
        dy_ref[...] = dy
        do_ref[...] = do.astype(BF16)
        acc_ref[0:1, :] += jnp.sum(dy * n, axis=0, keepdims=True)
        acc_ref[1:2, :] += jnp.sum(err * err, axis=0, keepdims=True)

    row = pl.BlockSpec((tm, D_MODEL), lambda i: (i, 0))
    return pl.pallas_call(
        body, name="out_loss", grid=(t // tm,),
        in_specs=[row, pl.BlockSpec((D_MODEL, D_MODEL), lambda i: (0, 0)), row, row,
                  pl.BlockSpec((1, D_MODEL), lambda i: (0, 0))],
        out_specs=[row, row, pl.BlockSpec((8, D_MODEL), lambda i: (0, 0))],
        out_shape=[jax.ShapeDtypeStruct((t, D_MODEL), F32), jax.ShapeDtypeStruct((t, D_MODEL), BF16),
                   jax.ShapeDtypeStruct((8, D_MODEL), F32)],
        compiler_params=_cparams(("arbitrary",)),
    )(m, wout, x2, tgt2, wpost)


def _merge_bwd(do, wout, zrest, ya, yr):
    t = do.shape[0]
    tm = min(512, t)
    tn = 512
    nj = D_MODEL // tn

    def body(do_ref, w_ref, mga_ref, mgr_ref, ya_ref, yr_ref, dya_ref, dyr_ref, dmga_ref, dmgr_ref):
        dm = _dot_nt(do_ref[...], w_ref[...])
        sa = _sigmoid(mga_ref[...])
        sr = _sigmoid(mgr_ref[...])
        dya_ref[...] = (dm * sa).astype(BF16)
        dyr_ref[...] = (dm * sr).astype(BF16)
        dmga_ref[...] = (dm * ya_ref[...] * (sa * (1.0 - sa))).astype(BF16)
        dmgr_ref[...] = (dm * yr_ref[...] * (sr * (1.0 - sr))).astype(BF16)

    out = pl.BlockSpec((tm, tn), lambda i, j: (i, j))
    bf = jax.ShapeDtypeStruct((t, D_MODEL), BF16)
    return pl.pallas_call(
        body, name="merge_bwd", grid=(t // tm, nj),
        in_specs=[pl.BlockSpec((tm, D_MODEL), lambda i, j: (i, 0)), pl.BlockSpec((tn, D_MODEL), lambda i, j: (j, 0)),
                  pl.BlockSpec((tm, tn), lambda i, j: (i, 3 * nj + j)),
                  pl.BlockSpec((tm, tn), lambda i, j: (i, 4 * nj + j)), out, out],
        out_specs=[out, out, out, out],
        out_shape=[bf, bf, bf, bf],
        compiler_params=_cparams(("parallel", "parallel")),
    )(do, wout, zrest, zrest, ya, yr)


def _branch_bwd(dya, dyr, wa, wr, zrest, yatt, ylru):
    t = dya.shape[0]
    tm = min(512, t)
    tn = 512
    nj = D_MODEL // tn

    def body(dya_ref, dyr_ref, wa_ref, wr_ref, ga_ref, gr_ref, ya_ref, yl_ref,
             dyatt_ref, dga_ref, dyl_ref, dgr_ref):
        dga = _dot_nt(dya_ref[...], wa_ref[...])
        dgr = _dot_nt(dyr_ref[...], wr_ref[...])
        g = ga_ref[...]
        sg = _sigmoid(g)
        dyatt_ref[...] = (dga * (g * sg)).astype(BF16)
        dga_ref[...] = (dga * ya_ref[...] * (sg * (1.0 + g * (1.0 - sg)))).astype(BF16)
        g = gr_ref[...]
        sg = _sigmoid(g)
        dyl_ref[...] = dgr * (g * sg)
        dgr_ref[...] = (dgr * yl_ref[...] * (sg * (1.0 + g * (1.0 - sg)))).astype(BF16)

    act = pl.BlockSpec((tm, D_MODEL), lambda i, j: (i, 0))
    wgt = pl.BlockSpec((tn, D_MODEL), lambda i, j: (j, 0))
    out = pl.BlockSpec((tm, tn), lambda i, j: (i, j))
    bf = jax.ShapeDtypeStruct((t, D_MODEL), BF16)
    return pl.pallas_call(
        body, name="branch_bwd", grid=(t // tm, nj),
        in_specs=[act, act, wgt, wgt, pl.BlockSpec((tm, tn), lambda i, j: (i, j)),
                  pl.BlockSpec((tm, tn), lambda i, j: (i, 2 * nj + j)), out, out],
        out_specs=[out, out, out, out],
        out_shape=[bf, bf, jax.ShapeDtypeStruct((t, D_MODEL), F32), bf],
        compiler_params=_cparams(("parallel", "parallel")),
    )(dya, dyr, wa, wr, zrest, zrest, yatt, ylru)


def _dh_partial(parts, name):
    t = parts[0][0].shape[0]
    tm = min(256, t)
    np_ = len(parts)

    def body(*refs):
        o_ref = refs[-1]
        acc = _dot_nt(refs[0][...], refs[np_][...])
        for p in range(1, np_):
            acc = acc + _dot_nt(refs[p][...], refs[np_ + p][...])
        o_ref[...] = acc

    in_specs = [pl.BlockSpec((tm, dz.shape[1]), lambda i: (i, 0)) for dz, _ in parts]
    in_specs += [pl.BlockSpec(w.shape, lambda i: (0, 0)) for _, w in parts]
    return pl.pallas_call(
        body, name=name, grid=(t // tm,),
        in_specs=in_specs,
        out_specs=pl.BlockSpec((tm, D_MODEL), lambda i: (i, 0)),
        out_shape=jax.ShapeDtypeStruct((t, D_MODEL), F32),
        compiler_params=_cparams(("parallel",), vmem_mb=48),
    )(*[dz for dz, _ in parts], *[w for _, w in parts])


def _dh_final(parts, acc_in, x2, dy, wpre):
    t = x2.shape[0]
    tm = min(256, t)
    np_ = len(parts)

    def body(*refs):
        acc_ref, x_ref, dy_ref, w_ref = refs[2 * np_:2 * np_ + 4]
        gx_ref, pw_ref = refs[2 * np_ + 4:]

        @pl.when(pl.program_id(0) == 0)
        def _():
            pw_ref[...] = jnp.zeros_like(pw_ref)

        dh = acc_ref[...]
        for p in range(np_):
            dh = dh + _dot_nt(refs[p][...], refs[np_ + p][...])
        x = x_ref[...]
        r = lax.rsqrt(jnp.mean(x * x, axis=-1, keepdims=True) + NORM_EPS)
        xn = x * r
        dxn = dh * w_ref[...]
        gx_ref[...] = r * (dxn - xn * jnp.mean(dxn * xn, axis=-1, keepdims=True)) + dy_ref[...]
        pw_ref[0:1, :] += jnp.sum(dh * xn, axis=0, keepdims=True)

    row = pl.BlockSpec((tm, D_MODEL), lambda i: (i, 0))
    in_specs = [pl.BlockSpec((tm, dz.shape[1]), lambda i: (i, 0)) for dz, _ in parts]
    in_specs += [pl.BlockSpec(w.shape, lambda i: (0, 0)) for _, w in parts]
    in_specs += [row, row, row, pl.BlockSpec((1, D_MODEL), lambda i: (0, 0))]
    return pl.pallas_call(
        body, name="dh_final", grid=(t // tm,),
        in_specs=in_specs,
        out_specs=[row, pl.BlockSpec((8, D_MODEL), lambda i: (0, 0))],
        out_shape=[jax.ShapeDtypeStruct((t, D_MODEL), F32), jax.ShapeDtypeStruct((8, D_MODEL), F32)],
        compiler_params=_cparams(("arbitrary",), vmem_mb=48),
    )(*[dz for dz, _ in parts], *[w for _, w in parts], acc_in, x2, dy, wpre)


def _adamw(w, g, m, v):
    m = ADAM_B1 * m + (1.0 - ADAM_B1) * g
    v = ADAM_B2 * v + (1.0 - ADAM_B2) * (g * g)
    m_hat = m / (1.0 - ADAM_B1 ** ADAM_STEP)
    v_hat = v / (1.0 - ADAM_B2 ** ADAM_STEP)
    delta = -ADAM_LR * (m_hat / (jnp.sqrt(v_hat) + ADAM_EPS) + ADAM_WD * w)
    return delta, m, v


def _reduce_adamw(parts, w, m, v, name):
    r, c = w.shape
    tr = min(64, r)

    def body(p_ref, w_ref, m_ref, v_ref, g_ref, d_ref, nm_ref, nv_ref):
        g = p_ref[0]
        for j in range(1, N_DEV):
            g = g + p_ref[j]
        d, nm, nv = _adamw(w_ref[...], g, m_ref[...], v_ref[...])
        g_ref[...] = g
        d_ref[...] = d
        nm_ref[...] = nm
        nv_ref[...] = nv

    row = pl.BlockSpec((tr, c), lambda i: (i, 0))
    sh = jax.ShapeDtypeStruct((r, c), F32)
    return pl.pallas_call(
        body, name=name, grid=(r // tr,),
        in_specs=[pl.BlockSpec((N_DEV, tr, c), lambda i: (0, i, 0)), row, row, row],
        out_specs=[row, row, row, row],
        out_shape=[sh, sh, sh, sh],
        compiler_params=_cparams(("parallel",)),
    )(parts, w, m, v)


def _interleave_qkv(a):
    lead = a.shape[:-1]
    return a.reshape(lead + (3, HEAD_PAIRS, LANES)).swapaxes(-3, -2).reshape(lead + (3 * D_MODEL,))


def _deinterleave_qkv(a):
    lead = a.shape[:-1]
    return a.reshape(lead + (HEAD_PAIRS, 3, LANES)).swapaxes(-3, -2).reshape(lead + (3 * D_MODEL,))


def _pack_small(pre, conv_b, rg_ba, rg_bx, lam, post, loss_row, b_in, conv_w_full, rg_wa, rg_wx):
    z = jnp.zeros((1, D_MODEL), F32)
    b_used = jnp.concatenate([b_in[:, 0:3 * D_MODEL], b_in[:, 3 * D_MODEL + HEADS:IN_TOTAL]], axis=1)
    b_f = jnp.pad(b_in[:, 3 * D_MODEL:3 * D_MODEL + HEADS], ((0, 0), (0, D_MODEL - HEADS)))
    return jnp.concatenate([
        pre, conv_b, rg_ba, rg_bx, lam, post, loss_row, z,
        b_used.reshape(9, D_MODEL), b_f, conv_w_full, z, z,
        rg_wa.reshape(64, D_MODEL), rg_wx.reshape(64, D_MODEL)], axis=0)


def _unpack_small(p):
    b_used = p[8:17].reshape(1, 9 * D_MODEL)
    b_in = jnp.concatenate([b_used[:, 0:3 * D_MODEL], p[17:18, 0:HEADS], b_used[:, 3 * D_MODEL:]], axis=1)
    return dict(pre_norm_w=p[0:1], conv_b=p[1:2], rg_ba=p[2:3], rg_bx=p[3:4], rg_lambda=p[4:5],
                post_norm_w=p[5:6], loss_row=p[6:7], b_in=b_in, conv_w_full=p[18:22],
                rg_wa=p[24:88].reshape(1, 16, 64, 64), rg_wx=p[88:152].reshape(1, 16, 64, 64))


def _reduce_small(parts, w, m, v):
    def body(p_ref, w_ref, m_ref, v_ref, g_ref, d_ref, nm_ref, nv_ref):
        g = p_ref[0]
        for j in range(1, N_DEV):
            g = g + p_ref[j]
        d, nm, nv = _adamw(w_ref[...], g, m_ref[...], v_ref[...])
        g_ref[...] = g
        d_ref[...] = d
        nm_ref[...] = nm
        nv_ref[...] = nv

    sh = jax.ShapeDtypeStruct((SMALL_ROWS, D_MODEL), F32)
    return pl.pallas_call(body, name="reduce_small", out_shape=[sh, sh, sh, sh])(parts, w, m, v)


def kernel(x, pre_norm_w, w_in, b_in, conv_w, conv_b, rg_wa, rg_ba, rg_wx, rg_bx, rg_lambda, w_branch_a, w_branch_r, w_out, post_norm_w, loss_target, m_pre_norm_w, m_w_in, m_b_in, m_conv_w, m_conv_b, m_rg_wa, m_rg_ba, m_rg_wx, m_rg_bx, m_rg_lambda, m_w_branch_a, m_w_branch_r, m_w_out, m_post_norm_w, v_pre_norm_w, v_w_in, v_b_in, v_conv_w, v_conv_b, v_rg_wa, v_rg_ba, v_rg_wx, v_rg_bx, v_rg_lambda, v_w_branch_a, v_w_branch_r, v_w_out, v_post_norm_w):
    b, s, _ = x.shape
    t = b * s
    me = 4 * lax.axis_index("x") + 2 * lax.axis_index("y") + lax.axis_index("c")
    shard_rows = D_MODEL // N_DEV

    w_in_all = _exchange(w_in[0].astype(BF16), "gather_w_in")
    w_full = w_in_all.transpose(1, 0, 2).reshape(D_MODEL, IN_TOTAL)
    conv_pad = jnp.pad(conv_w[0], ((0, 4), (0, D_MODEL - LANES)))
    sq_stack = jnp.concatenate([w_branch_a[0], w_branch_r[0], w_out[0], conv_pad], axis=0)
    sq_all = _exchange(sq_stack, "gather_w_sq")
    wa = sq_all[:, 0:shard_rows].reshape(D_MODEL, D_MODEL).astype(BF16)
    wr = sq_all[:, shard_rows:2 * shard_rows].reshape(D_MODEL, D_MODEL).astype(BF16)
    wo = sq_all[:, 2 * shard_rows:3 * shard_rows].reshape(D_MODEL, D_MODEL).astype(BF16)
    conv_full = sq_all[:, 3 * shard_rows:3 * shard_rows + CONV_W, 0:LANES].transpose(1, 0, 2).reshape(CONV_W, D_MODEL)

    w_qkv = _interleave_qkv(w_full[:, 0:3 * D_MODEL])
    w_f = jnp.pad(w_full[:, 3 * D_MODEL:3 * D_MODEL + HEADS], ((0, 0), (0, LANES - HEADS)))
    w_rest = w_full[:, 3 * D_MODEL + HEADS:IN_USED]
    b_qkv = _interleave_qkv(b_in[:, 0:3 * D_MODEL])
    b_f = jnp.pad(b_in[:, 3 * D_MODEL:3 * D_MODEL + HEADS], ((0, 0), (0, LANES - HEADS)))
    b_rest = b_in[:, 3 * D_MODEL + HEADS:IN_USED]

    def blockdiag(w):
        w2 = w.reshape(N_CBLK, 2, HEAD_DIM, HEAD_DIM)
        zz = jnp.zeros((N_CBLK, HEAD_DIM, HEAD_DIM), w.dtype)
        top = jnp.concatenate([w2[:, 0], zz], axis=2)
        bot = jnp.concatenate([zz, w2[:, 1]], axis=2)
        return jnp.concatenate([top, bot], axis=1).astype(BF16)

    bda, bdx = blockdiag(rg_wa[0]), blockdiag(rg_wx[0])

    x2 = x.reshape(t, D_MODEL)
    tgt2 = loss_target.reshape(t, D_MODEL)
    h = _prenorm(x2, pre_norm_w)
    qkv = _mm_bias(h, w_qkv, b_qkv, BF16, "inproj_qkv")
    zrest = _mm_bias(h, w_rest, b_rest, F32, "inproj_rest")
    zf = _mm_bias(h, w_f, b_f, F32, "inproj_f")
    qkv3 = qkv.reshape(b, s, 3 * D_MODEL)
    zrest3 = zrest.reshape(b, s, 5 * D_MODEL)
    zf3 = zf.reshape(b, s, LANES)
    nq = s // ATT_TILE
    cexp3, crow = _fgate_fwd(zf3)
    crow5 = crow.reshape(b, HEAD_PAIRS, 2, nq, ATT_TILE)
    yatt3, lse5, ga3 = _attn_fwd(qkv3, cexp3, crow5, zrest3)
    ylru3, gr3 = _rnn_fwd(zrest3, conv_full, conv_b, bda, bdx, rg_ba, rg_bx, rg_lambda)
    ga, gr = ga3.reshape(t, D_MODEL), gr3.reshape(t, D_MODEL)
    ya, yr, mm = _branch_merge(ga, gr, wa, wr, zrest)
    dy, do, acc_out = _out_loss(mm, wo, x2, tgt2, post_norm_w)

    dya, dyr, dz_mga, dz_mgr = _merge_bwd(do, wo, zrest, ya, yr)
    dyatt, dz_ga, dylru, dz_gr = _branch_bwd(dya, dyr, wa, wr, zrest, yatt3.reshape(t, D_MODEL),
                                             ylru3.reshape(t, D_MODEL))
    dz_xr3, pvec, dbd = _rnn_bwd(zrest3, ylru3, dylru.reshape(b, s, D_MODEL), conv_full, conv_b, bda, bdx,
                                 rg_ba, rg_bx, rg_lambda)
    dqkv3, dc3 = _attn_bwd(qkv3, dyatt.reshape(b, s, D_MODEL), yatt3, lse5, crow5, cexp3)
    dz_f = _fgate_bwd(dc3, zf3).reshape(t, LANES)
    dz_qkv = dqkv3.reshape(t, 3 * D_MODEL)
    dz_xr = dz_xr3.reshape(t, D_MODEL)

    wt = lambda lo: w_rest[:, lo * D_MODEL:(lo + 1) * D_MODEL]
    dh_a = _dh_partial([(dz_qkv, w_qkv), (dz_f, w_f)], "dh_qkv")
    grad_x2, acc_pre = _dh_final(
        [(dz_ga, wt(0)), (dz_xr, wt(1)), (dz_gr, wt(2)), (dz_mga, wt(3)), (dz_mgr, wt(4))],
        dh_a, x2, dy, pre_norm_w)

    dw_qkv, db_qkv = _mm_tn(h, dz_qkv, "dw_qkv")
    dw_f, db_f = _mm_tn(h, dz_f, "dw_f")
    dw_parts, db_parts = [], []
    for nm, dzp in (("ga", dz_ga), ("xr", dz_xr), ("gr", dz_gr), ("mga", dz_mga), ("mgr", dz_mgr)):
        dwp, dbp = _mm_tn(h, dzp, "dw_" + nm)
        dw_parts.append(dwp)
        db_parts.append(dbp[0:1])
    dw_a, _ = _mm_tn(ga, dya, "dw_a")
    dw_r, _ = _mm_tn(gr, dyr, "dw_r")
    dw_o, _ = _mm_tn(mm, do, "dw_o")

    zeros_tail = jnp.zeros((D_MODEL, IN_TOTAL - IN_USED), F32)
    dw_in_full = jnp.concatenate([_deinterleave_qkv(dw_qkv), dw_f[:, 0:HEADS]] + dw_parts + [zeros_tail], axis=1)
    dw_in_send = dw_in_full.reshape(D_MODEL, N_DEV, W_SHARD).transpose(1, 0, 2)
    dw_sq_send = jnp.concatenate([dw_a.reshape(N_DEV, shard_rows, D_MODEL), dw_r.reshape(N_DEV, shard_rows, D_MODEL),
                                  dw_o.reshape(N_DEV, shard_rows, D_MODEL)], axis=1)

    db_in_full = jnp.concatenate([_deinterleave_qkv(db_qkv[0:1]), db_f[0:1, 0:HEADS]] + db_parts
                                 + [jnp.zeros((1, IN_TOTAL - IN_USED), F32)], axis=1)
    d_rg_wa = jnp.stack([dbd[:, 0, 0:HEAD_DIM, 0:HEAD_DIM], dbd[:, 0, HEAD_DIM:, HEAD_DIM:]], axis=1)
    d_rg_wx = jnp.stack([dbd[:, 1, 0:HEAD_DIM, 0:HEAD_DIM], dbd[:, 1, HEAD_DIM:, HEAD_DIM:]], axis=1)
    small_g = _pack_small(acc_pre[0:1], pvec[4:5], pvec[5:6], pvec[6:7], pvec[7:8], acc_out[0:1], acc_out[1:2],
                          db_in_full, pvec[0:4], d_rg_wa, d_rg_wx)

    recv_in = _exchange(dw_in_send, "scatter_dw_in")
    recv_sq = _exchange(dw_sq_send, "scatter_dw_sq")
    small_all = _exchange(small_g, "gather_small")

    g_in, d_in, nm_in, nv_in = _reduce_adamw(recv_in, w_in[0], m_w_in[0], v_w_in[0], "adamw_w_in")
    sq_w = jnp.concatenate([w_branch_a[0], w_branch_r[0], w_out[0]], axis=0)
    sq_m = jnp.concatenate([m_w_branch_a[0], m_w_branch_r[0], m_w_out[0]], axis=0)
    sq_v = jnp.concatenate([v_w_branch_a[0], v_w_branch_r[0], v_w_out[0]], axis=0)
    g_sq, d_sq, nm_sq, nv_sq = _reduce_adamw(recv_sq, sq_w, sq_m, sq_v, "adamw_w_sq")

    def place_conv(a):
        return lax.dynamic_update_slice(jnp.zeros((CONV_W, D_MODEL), F32), a[0], (0, me * LANES))

    zrow = jnp.zeros((1, D_MODEL), F32)
    small_w = _pack_small(pre_norm_w, conv_b, rg_ba, rg_bx, rg_lambda, post_norm_w, zrow, b_in,
                          place_conv(conv_w), rg_wa[0], rg_wx[0])
    small_m = _pack_small(m_pre_norm_w, m_conv_b, m_rg_ba, m_rg_bx, m_rg_lambda, m_post_norm_w, zrow, m_b_in,
                          place_conv(m_conv_w), m_rg_wa[0], m_rg_wx[0])
    small_v = _pack_small(v_pre_norm_w, v_conv_b, v_rg_ba, v_rg_bx, v_rg_lambda, v_post_norm_w, zrow, v_b_in,
                          place_conv(v_conv_w), v_rg_wa[0], v_rg_wx[0])
    outs_small = [_unpack_small(p) for p in _reduce_small(small_all, small_w, small_m, small_v)]

    loss = (0.5 / D_MODEL) * jnp.sum(outs_small[0]["loss_row"])

    def leaf(kind, name):
        if name == "w_in":
            return (g_in, d_in, nm_in, nv_in)[kind][None]
        if name in ("w_branch_a", "w_branch_r", "w_out"):
            j = ("w_branch_a", "w_branch_r", "w_out").index(name)
            return (g_sq, d_sq, nm_sq, nv_sq)[kind][None, j * shard_rows:(j + 1) * shard_rows]
        if name == "conv_w":
            return lax.dynamic_slice(outs_small[kind]["conv_w_full"], (0, me * LANES), (CONV_W, LANES))[None]
        return outs_small[kind][name]

    names = ["pre_norm_w", "w_in", "b_in", "conv_w", "conv_b", "rg_wa", "rg_ba", "rg_wx", "rg_bx", "rg_lambda",
             "w_branch_a", "w_branch_r", "w_out", "post_norm_w"]
    out = [loss, grad_x2.reshape(b, s, D_MODEL)]
    for kind in range(4):
        out += [leaf(kind, nm) for nm in names]
    return tuple(out)
```

```python
import jax
import jax.numpy as jnp
from jax import lax
from jax.experimental import pallas as pl
from jax.experimental.pallas import tpu as pltpu

F32 = jnp.float32
BF16 = jnp.bfloat16

N_DEV = 8
D_MODEL = 1024
HEADS = 16
HEAD_DIM = 64
HEAD_PAIRS = HEADS // 2
LANES = 128
N_CBLK = D_MODEL // LANES
CONV_W = 4
RG_C = 8.0
NORM_EPS = 1e-6
MASK_VALUE = -1e30
IN_USED = 8208
IN_TOTAL = 9232
W_SHARD = IN_TOTAL // N_DEV

ADAM_LR = 0.001
ADAM_B1 = 0.9
ADAM_B2 = 0.999
ADAM_EPS = 1e-08
ADAM_WD = 0.01
ADAM_STEP = 10

ATT_TILE = 256
SCAN_TILE = 256
SMALL_ROWS = 152


def _cparams(sem=None, vmem_mb=None):
    kw = {}
    if sem is not None:
        kw["dimension_semantics"] = sem
    if vmem_mb is not None:
        kw["vmem_limit_bytes"] = vmem_mb * 1024 * 1024
    return pltpu.CompilerParams(**kw)


def _sigmoid(x):
    return 1.0 / (1.0 + jnp.exp(-x))


def _softplus(x):
    return jnp.maximum(x, 0.0) + jnp.log1p(jnp.exp(-jnp.abs(x)))


def _expm1(x):
    p = x * (1.0 + x * (1.0 / 2 + x * (1.0 / 6 + x * (1.0 / 24 + x * (1.0 / 120 + x * (
        1.0 / 720 + x * (1.0 / 5040 + x * (1.0 / 40320))))))))
    return jnp.where(jnp.abs(x) < 0.5, p, jnp.exp(x) - 1.0)


def _split3(x):
    hi = x.astype(BF16)
    r1 = x - hi.astype(F32)
    mid = r1.astype(BF16)
    lo = (r1 - mid.astype(F32)).astype(BF16)
    return hi, mid, lo


def _dot(a, b):
    return jnp.dot(a, b, preferred_element_type=F32)


def _dot_nt(a, b):
    return lax.dot_general(a, b, (((1,), (1,)), ((), ())), preferred_element_type=F32)


def _dot_tn(a, b):
    return lax.dot_general(a, b, (((0,), (0,)), ((), ())), preferred_element_type=F32)


def _iota(shape, dim):
    return lax.broadcasted_iota(jnp.int32, shape, dim)


_ANY = pl.BlockSpec(memory_space=pl.ANY)
_MESH = pl.DeviceIdType.MESH
N_CHIPS = 4


def _place():
    x, y, c = lax.axis_index("x"), lax.axis_index("y"), lax.axis_index("c")
    other_chips = [(1 - x, y), (x, 1 - y), (1 - x, 1 - y)]
    return x, y, c, other_chips


def _gather(x_shard, name):
    def body(x_ref, out_ref, send_sems, recv_sems, local_sem):
        x, y, c, chips = _place()
        me, sibling = (x, y, c), (x, y, 1 - c)

        def slot(p):
            return out_ref.at[4 * p[0] + 2 * p[1] + p[2]]

        def copy(k, block, to, src=None):
            return pltpu.make_async_remote_copy(
                src_ref=slot(block) if src is None else src, dst_ref=slot(block),
                send_sem=send_sems.at[k], recv_sem=recv_sems.at[k], device_id=to, device_id_type=_MESH)

        mine = pltpu.make_async_copy(x_ref, slot(me), local_sem)
        mine.start()
        first = [copy(0, me, sibling, src=x_ref)]
        first += [copy(1 + j, me, (*chip, c), src=x_ref) for j, chip in enumerate(chips)]
        for cp in first:
            cp.start()
        passed = [copy(4 + j, (*chip, c), sibling) for j, chip in enumerate(chips)]
        for j, chip in enumerate(chips):
            copy(1 + j, (*chip, c), me).wait_recv()
            passed[j].start()
        copy(0, sibling, me).wait_recv()
        for j, chip in enumerate(chips):
            copy(4 + j, (*chip, 1 - c), me).wait_recv()
        for cp in first + passed:
            cp.wait_send()
        mine.wait()

    return pl.pallas_call(
        body, name=name,
        out_shape=jax.ShapeDtypeStruct((N_DEV,) + tuple(x_shard.shape), x_shard.dtype),
        in_specs=[_ANY], out_specs=_ANY,
        scratch_shapes=[pltpu.SemaphoreType.DMA((7,)), pltpu.SemaphoreType.DMA((7,)), pltpu.SemaphoreType.DMA],
    )(x_shard)


def _swap_with_sibling(srcs, name):
    n = len(srcs)

    def body(*refs):
        src_refs, out_refs = refs[:n], refs[n:2 * n]
        send_sems, recv_sems = refs[2 * n:]
        x, y, c, _ = _place()
        cps = [pltpu.make_async_remote_copy(
            src_ref=src_refs[i].at[1 - c], dst_ref=out_refs[i], send_sem=send_sems.at[i], recv_sem=recv_sems.at[i],
            device_id=(x, y, 1 - c), device_id_type=_MESH) for i in range(n)]
        for cp in cps:
            cp.start()
        for cp in cps:
            cp.wait()

    return pl.pallas_call(
        body, name=name,
        out_shape=[jax.ShapeDtypeStruct(a.shape[1:], a.dtype) for a in srcs],
        in_specs=[_ANY] * n, out_specs=[_ANY] * n,
        scratch_shapes=[pltpu.SemaphoreType.DMA((n,)), pltpu.SemaphoreType.DMA((n,))],
    )(*srcs)


def _pair_add(src, recv, place, name):
    _, _, r, c = src.shape
    tr = min(128, r)

    def body(place_ref, a_ref, b_ref, q16_ref, own_ref):
        q = a_ref[...] + b_ref[...]
        q16_ref[...] = q.astype(BF16)

        @pl.when(pl.program_id(1) == place_ref[1])
        def _():
            own_ref[...] = q

    grid_spec = pltpu.PrefetchScalarGridSpec(
        num_scalar_prefetch=1, grid=(r // tr, N_CHIPS),
        in_specs=[pl.BlockSpec((None, None, tr, c), lambda i, j, pr: (pr[0], j, i, 0)),
                  pl.BlockSpec((None, tr, c), lambda i, j, pr: (j, i, 0))],
        out_specs=[pl.BlockSpec((None, tr, c), lambda i, j, pr: (j, i, 0)),
                   pl.BlockSpec((tr, c), lambda i, j, pr: (i, 0))])
    return pl.pallas_call(
        body, name=name, grid_spec=grid_spec,
        out_shape=[jax.ShapeDtypeStruct((N_CHIPS, r, c), BF16), jax.ShapeDtypeStruct((r, c), F32)],
        compiler_params=_cparams(("parallel", "arbitrary")),
    )(place, src, recv)


def _exchange_chips(srcs, name):
    n = len(srcs)

    def body(*refs):
        src_refs, out_refs = refs[:n], refs[n:2 * n]
        send_sems, recv_sems, local_sems = refs[2 * n:]
        x, y, c, chips = _place()
        mine = 2 * x + y
        local = [pltpu.make_async_copy(src_refs[i].at[mine], out_refs[i].at[mine], local_sems.at[i])
                 for i in range(n)]
        for cp in local:
            cp.start()

        def copy(i, k, land):
            chip = chips[k]
            return pltpu.make_async_remote_copy(
                src_ref=src_refs[i].at[2 * chip[0] + chip[1]], dst_ref=out_refs[i].at[land],
                send_sem=send_sems.at[i, k], recv_sem=recv_sems.at[i, k],
                device_id=(*chip, c), device_id_type=_MESH)

        sends = [copy(i, k, mine) for i in range(n) for k in range(3)]
        for cp in sends:
            cp.start()
        for i in range(n):
            for k in range(3):
                copy(i, k, 2 * chips[k][0] + chips[k][1]).wait_recv()
        for cp in sends:
            cp.wait_send()
        for cp in local:
            cp.wait()

    return pl.pallas_call(
        body, name=name,
        out_shape=[jax.ShapeDtypeStruct(a.shape, a.dtype) for a in srcs],
        in_specs=[_ANY] * n, out_specs=[_ANY] * n,
        scratch_shapes=[pltpu.SemaphoreType.DMA((n, 3)), pltpu.SemaphoreType.DMA((n, 3)),
                        pltpu.SemaphoreType.DMA((n,))],
    )(*srcs)


def _prenorm(x2, w):
    t = x2.shape[0]
    tm = min(512, t)

    def body(x_ref, w_ref, h_ref):
        x = x_ref[...]
        r = lax.rsqrt(jnp.mean(x * x, axis=-1, keepdims=True) + NORM_EPS)
        h_ref[...] = (x * r * w_ref[...]).astype(BF16)

    return pl.pallas_call(
        body, name="prenorm", grid=(t // tm,),
        in_specs=[pl.BlockSpec((tm, D_MODEL), lambda i: (i, 0)), pl.BlockSpec((1, D_MODEL), lambda i: (0, 0))],
        out_specs=pl.BlockSpec((tm, D_MODEL), lambda i: (i, 0)),
        out_shape=jax.ShapeDtypeStruct((t, D_MODEL), BF16),
        compiler_params=_cparams(("parallel",)),
    )(x2, w)


def _mm_bias(a, b, bias, out_dtype, name):
    m, k = a.shape
    n = b.shape[1]
    tm = min(512, m)
    tn = min(1024, n)

    def body(a_ref, b_ref, bias_ref, o_ref):
        o_ref[...] = (_dot(a_ref[...], b_ref[...]) + bias_ref[...]).astype(o_ref.dtype)

    return pl.pallas_call(
        body, name=name, grid=(n // tn, m // tm),
        in_specs=[pl.BlockSpec((tm, k), lambda j, i: (i, 0)), pl.BlockSpec((k, tn), lambda j, i: (0, j)),
                  pl.BlockSpec((1, tn), lambda j, i: (0, j))],
        out_specs=pl.BlockSpec((tm, tn), lambda j, i: (i, j)),
        out_shape=jax.ShapeDtypeStruct((m, n), out_dtype),
        compiler_params=_cparams(("parallel", "parallel")),
    )(a, b, bias)


def _mm_tn(a, b, name):
    t, m = a.shape
    n = b.shape[1]
    tn = min(1024, n)
    tk = min(512, t)

    def body(a_ref, b_ref, o_ref, s_ref):
        kk = pl.program_id(1)

        @pl.when(kk == 0)
        def _():
            o_ref[...] = jnp.zeros_like(o_ref)
            s_ref[...] = jnp.zeros_like(s_ref)

        bb = b_ref[...]
        o_ref[...] += _dot_tn(a_ref[...], bb)
        s_ref[0:1, :] += jnp.sum(bb.astype(F32), axis=0, keepdims=True)

    return pl.pallas_call(
        body, name=name, grid=(n // tn, t // tk),
        in_specs=[pl.BlockSpec((tk, m), lambda j, kk: (kk, 0)), pl.BlockSpec((tk, tn), lambda j, kk: (kk, j))],
        out_specs=[pl.BlockSpec((m, tn), lambda j, kk: (0, j)), pl.BlockSpec((8, tn), lambda j, kk: (0, j))],
        out_shape=[jax.ShapeDtypeStruct((m, n), F32), jax.ShapeDtypeStruct((8, n), F32)],
        compiler_params=_cparams(("parallel", "arbitrary")),
    )(a, b)


def _fgate_fwd(zf3):
    b, s, _ = zf3.shape
    tb = SCAN_TILE
    nb = s // tb

    def body(z_ref, cexp_ref, crow_ref):
        tri = (_iota((tb, tb), 1) <= _iota((tb, tb), 0)).astype(BF16)
        expand = ((_iota((LANES, D_MODEL), 1) >> 6) == _iota((LANES, D_MODEL), 0)).astype(BF16)
        carry = jnp.zeros((1, LANES), F32)
        for i in range(nb):
            rows = slice(i * tb, (i + 1) * tb)
            z = z_ref[rows, :]
            lf = jnp.minimum(z, 0.0) - jnp.log1p(jnp.exp(-jnp.abs(z)))
            cb = sum(_dot(tri, part) for part in _split3(lf)) + carry
            carry = cb[tb - 1:tb, :]
            cexp_ref[rows, :] = sum(_dot(part, expand) for part in _split3(cb))
            crow_ref[:, rows] = cb.T[0:HEADS, :]

    return pl.pallas_call(
        body, name="fgate_fwd", grid=(b,),
        in_specs=[pl.BlockSpec((None, s, LANES), lambda i: (i, 0, 0))],
        out_specs=[pl.BlockSpec((None, s, D_MODEL), lambda i: (i, 0, 0)),
                   pl.BlockSpec((None, HEADS, s), lambda i: (i, 0, 0))],
        out_shape=[jax.ShapeDtypeStruct((b, s, D_MODEL), F32), jax.ShapeDtypeStruct((b, HEADS, s), F32)],
        compiler_params=_cparams(("parallel",)),
    )(zf3)


def _fgate_bwd(dc3, zf3):
    b, s, _ = zf3.shape
    tb = SCAN_TILE
    nb = s // tb

    def body(dc_ref, z_ref, o_ref):
        tri = (_iota((tb, tb), 1) >= _iota((tb, tb), 0)).astype(BF16)
        carry = jnp.zeros((1, LANES), F32)
        for i in reversed(range(nb)):
            rows = slice(i * tb, (i + 1) * tb)
            dlf = sum(_dot(tri, part) for part in _split3(dc_ref[rows, :])) + carry
            carry = dlf[0:1, :]
            o_ref[rows, :] = (dlf * _sigmoid(-z_ref[rows, :])).astype(BF16)

    return pl.pallas_call(
        body, name="fgate_bwd", grid=(b,),
        in_specs=[pl.BlockSpec((None, s, LANES), lambda i: (i, 0, 0)),
                  pl.BlockSpec((None, s, LANES), lambda i: (i, 0, 0))],
        out_specs=pl.BlockSpec((None, s, LANES), lambda i: (i, 0, 0)),
        out_shape=jax.ShapeDtypeStruct((b, s, LANES), BF16),
        compiler_params=_cparams(("parallel",)),
    )(dc3, zf3)


def _spare(hh):
    return HEAD_DIM if hh == 0 else 0


def _put_cols(tile, mine, cols, first):
    lane = _iota((1, LANES), 1)
    out = jnp.where(mine, tile, jnp.zeros((), tile.dtype))
    for j, c in enumerate(cols):
        out = jnp.where(lane == first + j, c, out)
    return out


def _put_rows(tile, mine, rows, first):
    sub = _iota((LANES, 1), 0)
    out = jnp.where(mine, tile, jnp.zeros((), tile.dtype))
    for j, r in enumerate(rows):
        out = jnp.where(sub == first + j, r, out)
    return out


def _transpose_bf16(a):
    return a.astype(F32).T.astype(BF16)


def _attn_fwd(qkv3, cexp3, crow5, zrest3):
    b, s, _ = qkv3.shape
    ta = ATT_TILE
    nq = s // ta
    hd = HEAD_DIM

    def body(qkv_ref, cq_ref, ck_ref, g_ref, y_ref, lse_ref, ga_ref, kt_scr, v_scr):
        lane = _iota((1, LANES), 1)
        sub = _iota((LANES, 1), 0)
        lane_mine = (lane < hd, lane >= hd)
        sub_mine = (sub < hd, sub >= hd)
        causal = _iota((ta, ta), 0) >= _iota((ta, ta), 1)
        one = jnp.ones((), BF16)

        for kj in range(nq):
            rows = slice(kj * ta, (kj + 1) * ta)
            kt = _transpose_bf16(qkv_ref[rows, LANES:2 * LANES])
            v = qkv_ref[rows, 2 * LANES:3 * LANES]
            for hh in range(2):
                ck = list(_split3(-ck_ref[hh, kj:kj + 1, :]))
                kt_scr[hh, kj] = _put_rows(kt, sub_mine[hh], [one, one, one] + ck, _spare(hh))
                v_scr[hh, kj] = _put_cols(v, lane_mine[hh], [one], _spare(hh))

        for qi in range(nq):
            rows = slice(qi * ta, (qi + 1) * ta)
            q = qkv_ref[rows, 0:LANES] * 0.125
            cq = cq_ref[rows, :]
            qh = [_put_cols(q, lane_mine[hh], list(_split3(cq[:, hh * hd:hh * hd + 1])) + [one, one, one], _spare(hh))
                  for hh in range(2)]
            st = [(jnp.full((ta, 1), MASK_VALUE, F32), jnp.zeros((ta, LANES), F32))] * 2
            for kj in range(qi + 1):
                for hh in range(2):
                    m, acc = st[hh]
                    sc = _dot(qh[hh], kt_scr[hh, kj])
                    if kj == qi:
                        sc = jnp.where(causal, sc, MASK_VALUE)
                    mn = jnp.maximum(m, jnp.max(sc, axis=-1, keepdims=True))
                    p = jnp.exp(sc - mn).astype(BF16)
                    st[hh] = (mn, jnp.exp(m - mn) * acc + _dot(p, v_scr[hh, kj]))
            (ma, acca), (mb, accb) = st
            la = acca[:, hd:hd + 1]
            lb = accb[:, 0:1]
            y = jnp.where(lane_mine[0], acca * (1.0 / la), accb * (1.0 / lb))
            lse = jnp.where(lane_mine[0], ma + jnp.log(la), mb + jnp.log(lb)).T
            lse_ref[0, qi:qi + 1, :] = lse[0:1, :]
            lse_ref[1, qi:qi + 1, :] = lse[hd:hd + 1, :]
            y_ref[rows, :] = y
            g = g_ref[rows, :]
            ga_ref[rows, :] = (y * (g * _sigmoid(g))).astype(BF16)

    blk = lambda w: pl.BlockSpec((None, s, w), lambda i, p: (i, 0, p))
    rows5 = pl.BlockSpec((None, None, 2, nq, ta), lambda i, p: (i, p, 0, 0, 0))
    return pl.pallas_call(
        body, name="attn_fwd", grid=(b, HEAD_PAIRS),
        in_specs=[blk(3 * LANES), blk(LANES), rows5, blk(LANES)],
        out_specs=[blk(LANES), rows5, blk(LANES)],
        out_shape=[jax.ShapeDtypeStruct((b, s, D_MODEL), F32),
                   jax.ShapeDtypeStruct((b, HEAD_PAIRS, 2, nq, ta), F32),
                   jax.ShapeDtypeStruct((b, s, D_MODEL), BF16)],
        scratch_shapes=[pltpu.VMEM((2, nq, LANES, ta), BF16), pltpu.VMEM((2, nq, ta, LANES), BF16)],
        compiler_params=_cparams(("parallel", "parallel")),
    )(qkv3, cexp3, crow5, zrest3)


def _attn_bwd(qkv3, do3, y3, lse5, crow5, cexp3):
    b, s, _ = qkv3.shape
    ta = ATT_TILE
    nq = s // ta
    hd = HEAD_DIM

    def body(qkv_ref, do_ref, y_ref, lse_ref, crow_ref, cexp_ref, dqkv_ref, dc_ref,
             qa_scr, doa_scr, qst_scr, dot_scr, kt_scr, vt_scr, dq_scr, rs_scr):
        pair = pl.program_id(1)
        lane = _iota((1, LANES), 1)
        sub = _iota((LANES, 1), 0)
        lane_mine = (lane < hd, lane >= hd)
        sub_mine = (sub < hd, sub >= hd)
        causal = _iota((ta, ta), 0) >= _iota((ta, ta), 1)
        one = jnp.ones((), BF16)
        zero = jnp.zeros((), BF16)

        @pl.when(pair == 0)
        def _():
            dc_ref[...] = jnp.zeros_like(dc_ref)

        for i in range(nq):
            rows = slice(i * ta, (i + 1) * ta)
            qs = qkv_ref[rows, 0:LANES] * 0.125
            qst = _transpose_bf16(qs)
            kt = _transpose_bf16(qkv_ref[rows, LANES:2 * LANES])
            vt = _transpose_bf16(qkv_ref[rows, 2 * LANES:3 * LANES])
            do = do_ref[rows, :]
            dof = do.astype(F32)
            dot = dof.T.astype(BF16)
            pr = y_ref[rows, :] * dof
            cq = cexp_ref[rows, :]
            lse_c = jnp.where(sub == 0, lse_ref[0, i:i + 1, :],
                              jnp.where(sub == 1, lse_ref[1, i:i + 1, :], 0.0)).T
            for hh in range(2):
                sp = _spare(hh)
                dsum = jnp.sum(jnp.where(lane_mine[hh], pr, 0.0), axis=-1, keepdims=True)
                bias = cq[:, hh * hd:hh * hd + 1] - lse_c[:, hh:hh + 1]
                qa_scr[hh, i] = _put_cols(qs, lane_mine[hh], list(_split3(bias)) + [one, one, one], sp)
                doa_scr[hh, i] = _put_cols(do, lane_mine[hh], list(_split3(-dsum)), sp)
                qst_scr[hh, i] = jnp.where(sub_mine[hh], qst, zero)
                dot_scr[hh, i] = jnp.where(sub_mine[hh], dot, zero)
                ck = list(_split3(-crow_ref[hh, i:i + 1, :]))
                kt_scr[hh, i] = _put_rows(kt, sub_mine[hh], [one, one, one] + ck, sp)
                vt_scr[hh, i] = _put_rows(vt, sub_mine[hh], [one, one, one], sp)
            dq_scr[i] = jnp.zeros((ta, LANES), F32)
            rs_scr[i] = jnp.zeros((ta, LANES), F32)

        for kj in range(nq):
            krows = slice(kj * ta, (kj + 1) * ta)
            k = qkv_ref[krows, LANES:2 * LANES]
            km = (jnp.where(lane_mine[0], k, zero), jnp.where(lane_mine[1], k, zero))
            dkt = jnp.zeros((LANES, ta), F32)
            dvt = jnp.zeros((LANES, ta), F32)
            dcp = [jnp.zeros((8, ta), F32), jnp.zeros((8, ta), F32)]
            for qi in range(kj, nq):
                dq = jnp.zeros((ta, LANES), F32)
                rs = []
                for hh in range(2):
                    sc = _dot(qa_scr[hh, qi], kt_scr[hh, kj])
                    if qi == kj:
                        sc = jnp.where(causal, sc, MASK_VALUE)
                    p = jnp.exp(sc)
                    dsf = p * _dot(doa_scr[hh, qi], vt_scr[hh, kj])
                    dcp[hh] = dcp[hh] + jnp.sum(dsf.reshape(ta // 8, 8, ta), axis=0)
                    rs.append(jnp.sum(dsf, axis=-1, keepdims=True))
                    ds = dsf.astype(BF16)
                    dq = dq + _dot(ds, km[hh])
                    dkt = dkt + _dot(qst_scr[hh, qi], ds)
                    dvt = dvt + _dot(dot_scr[hh, qi], p.astype(BF16))
                dq_scr[qi] += dq
                rs_scr[qi] += jnp.where(lane == 0, rs[0], jnp.where(lane == 1, rs[1], 0.0))
            dqkv_ref[krows, LANES:2 * LANES] = dkt.T.astype(BF16)
            dqkv_ref[krows, 2 * LANES:3 * LANES] = dvt.T.astype(BF16)
            dca = jnp.sum(dcp[0], axis=0, keepdims=True)
            dcb = jnp.sum(dcp[1], axis=0, keepdims=True)
            dcs = jnp.where(sub == 0, dca, jnp.where(sub == 1, dcb, 0.0)).T
            dc_ref[krows, :] += (jnp.where(lane == 2 * pair, -dcs[:, 0:1], 0.0)
                                 + jnp.where(lane == 2 * pair + 1, -dcs[:, 1:2], 0.0))
        for qi in range(nq):
            rows = slice(qi * ta, (qi + 1) * ta)
            dqkv_ref[rows, 0:LANES] = (dq_scr[qi] * 0.125).astype(BF16)
            rq = rs_scr[qi]
            dc_ref[rows, :] += (jnp.where(lane == 2 * pair, rq[:, 0:1], 0.0)
                                + jnp.where(lane == 2 * pair + 1, rq[:, 1:2], 0.0))

    blk = lambda w: pl.BlockSpec((None, s, w), lambda i, p: (i, 0, p))
    rows5 = pl.BlockSpec((None, None, 2, nq, ta), lambda i, p: (i, p, 0, 0, 0))
    by_rows = lambda: pltpu.VMEM((2, nq, ta, LANES), BF16)
    by_cols = lambda: pltpu.VMEM((2, nq, LANES, ta), BF16)
    return pl.pallas_call(
        body, name="attn_bwd", grid=(b, HEAD_PAIRS),
        in_specs=[blk(3 * LANES), blk(LANES), blk(LANES), rows5, rows5, blk(LANES)],
        out_specs=[blk(3 * LANES), pl.BlockSpec((None, s, LANES), lambda i, p: (i, 0, 0))],
        out_shape=[jax.ShapeDtypeStruct((b, s, 3 * D_MODEL), BF16), jax.ShapeDtypeStruct((b, s, LANES), F32)],
        scratch_shapes=[by_rows(), by_rows(), by_cols(), by_cols(), by_cols(), by_cols(),
                        pltpu.VMEM((nq, ta, LANES), F32), pltpu.VMEM((nq, ta, LANES), F32)],
        compiler_params=_cparams(("parallel", "arbitrary")),
    )(qkv3, do3, y3, lse5, crow5, cexp3)


def _rnn_common(xr, cw_ref, cb_ref, bda_ref, bdx_ref, ba_ref, bx_ref, lam_ref, s):
    rows = _iota((s, LANES), 0)

    def down(v, k):
        return jnp.where(rows >= k, pltpu.roll(v, k, 0), 0.0)

    x1, x2, x3 = down(xr, 1), down(xr, 2), down(xr, 3)
    xc = cb_ref[...] + cw_ref[0:1, :] * x3
    xc = xc + cw_ref[1:2, :] * x2
    xc = xc + cw_ref[2:3, :] * x1
    xc = xc + cw_ref[3:4, :] * xr
    xcb = xc.astype(BF16)
    r = _sigmoid(_dot(xcb, bda_ref[...]) + ba_ref[...])
    i = _sigmoid(_dot(xcb, bdx_ref[...]) + bx_ref[...])
    sp = _softplus(-lam_ref[...])
    log_a = (-RG_C * r) * sp
    a = jnp.exp(log_a)
    e2 = -_expm1(2.0 * log_a)
    sq = jnp.sqrt(jnp.maximum(e2, 0.0))
    return rows, (x1, x2, x3), xc, xcb, r, i, sp, a, e2, sq


def _rnn_specs(s):
    blk = lambda off: pl.BlockSpec((None, s, LANES), lambda cb, i: (i, 0, off + cb))
    vec = lambda r: pl.BlockSpec((r, LANES), lambda cb, i: (0, cb))
    mat = pl.BlockSpec((None, LANES, LANES), lambda cb, i: (cb, 0, 0))
    return blk, vec, mat


def _rnn_fwd(zrest3, conv_w, conv_b, bda, bdx, ba, bx, lam):
    b, s, _ = zrest3.shape

    def body(xr_ref, g_ref, cw_ref, cb_ref, bda_ref, bdx_ref, ba_ref, bx_ref, lam_ref, h_ref, gr_ref):
        xr = xr_ref[...]
        rows, _, xc, _, _, i, _, a, _, sq = _rnn_common(
            xr, cw_ref, cb_ref, bda_ref, bdx_ref, ba_ref, bx_ref, lam_ref, s)
        u = sq * (i * xc)
        sh = 1
        while sh < s:
            keep = rows >= sh
            ur = jnp.where(keep, pltpu.roll(u, sh, 0), 0.0)
            u = u + a * ur
            if sh * 2 < s:
                a = a * jnp.where(keep, pltpu.roll(a, sh, 0), 1.0)
            sh *= 2
        h_ref[...] = u
        g = g_ref[...]
        gr_ref[...] = (u * (g * _sigmoid(g))).astype(BF16)

    blk, vec, mat = _rnn_specs(s)
    return pl.pallas_call(
        body, name="rnn_fwd", grid=(N_CBLK, b),
        in_specs=[blk(N_CBLK), blk(2 * N_CBLK), vec(CONV_W), vec(1), mat, mat, vec(1), vec(1), vec(1)],
        out_specs=[blk(0), blk(0)],
        out_shape=[jax.ShapeDtypeStruct((b, s, D_MODEL), F32), jax.ShapeDtypeStruct((b, s, D_MODEL), BF16)],
        compiler_params=_cparams(("parallel", "parallel")),
    )(zrest3, zrest3, conv_w, conv_b, bda, bdx, ba, bx, lam)


def _rnn_bwd(zrest3, h3, dh3, conv_w, conv_b, bda, bdx, ba, bx, lam):
    b, s, _ = zrest3.shape

    def body(xr_ref, h_ref, dh_ref, cw_ref, cb_ref, bda_ref, bdx_ref, ba_ref, bx_ref, lam_ref,
             dxr_ref, pv_ref, dbd_ref):
        @pl.when(pl.program_id(1) == 0)
        def _():
            pv_ref[...] = jnp.zeros_like(pv_ref)
            dbd_ref[...] = jnp.zeros_like(dbd_ref)

        xr = xr_ref[...]
        rows, (x1, x2, x3), xc, xcb, r, i, sp, a, e2, sq = _rnn_common(
            xr, cw_ref, cb_ref, bda_ref, bdx_ref, ba_ref, bx_ref, lam_ref, s)
        h = h_ref[...]
        g = dh_ref[...]
        an = jnp.where(rows < s - 1, pltpu.roll(a, s - 1, 0), 0.0)
        sh = 1
        while sh < s:
            keep = rows < s - sh
            gr = jnp.where(keep, pltpu.roll(g, s - sh, 0), 0.0)
            g = g + an * gr
            if sh * 2 < s:
                an = an * jnp.where(keep, pltpu.roll(an, s - sh, 0), 1.0)
            sh *= 2
        hp = jnp.where(rows >= 1, pltpu.roll(h, 1, 0), 0.0)
        da = g * hp
        dsq = g * (i * xc)
        di = g * (sq * xc)
        dxc = g * (sq * i)
        dlog = da * a - dsq * ((1.0 - e2) / sq)
        dr = dlog * (-RG_C * sp)
        dpr = dr * (r * (1.0 - r))
        dpi = di * (i * (1.0 - i))
        dprb = dpr.astype(BF16)
        dpib = dpi.astype(BF16)
        dxc = dxc + _dot_nt(dprb, bda_ref[...]) + _dot_nt(dpib, bdx_ref[...])

        def up(v, k):
            return jnp.where(rows < s - k, pltpu.roll(v, s - k, 0), 0.0)

        dxr = cw_ref[3:4, :] * dxc + cw_ref[2:3, :] * up(dxc, 1) + cw_ref[1:2, :] * up(dxc, 2) \
            + cw_ref[0:1, :] * up(dxc, 3)
        dxr_ref[...] = dxr.astype(BF16)

        def colsum(v):
            return jnp.sum(v, axis=0, keepdims=True)

        pv_ref[0:1, :] += colsum(dxc * x3)
        pv_ref[1:2, :] += colsum(dxc * x2)
        pv_ref[2:3, :] += colsum(dxc * x1)
        pv_ref[3:4, :] += colsum(dxc * xr)
        pv_ref[4:5, :] += colsum(dxc)
        pv_ref[5:6, :] += colsum(dpr)
        pv_ref[6:7, :] += colsum(dpi)
        pv_ref[7:8, :] += colsum(dlog * r) * (RG_C * _sigmoid(-lam_ref[...]))
        dbd_ref[0] += _dot_tn(xcb, dprb)
        dbd_ref[1] += _dot_tn(xcb, dpib)

    blk, vec, mat = _rnn_specs(s)
    hblk = pl.BlockSpec((None, s, LANES), lambda cb, i: (i, 0, cb))
    return pl.pallas_call(
        body, name="rnn_bwd", grid=(N_CBLK, b),
        in_specs=[blk(N_CBLK), hblk, hblk, vec(CONV_W), vec(1), mat, mat, vec(1), vec(1), vec(1)],
        out_specs=[hblk, pl.BlockSpec((8, LANES), lambda cb, i: (0, cb)),
                   pl.BlockSpec((None, 2, LANES, LANES), lambda cb, i: (cb, 0, 0, 0))],
        out_shape=[jax.ShapeDtypeStruct((b, s, D_MODEL), BF16), jax.ShapeDtypeStruct((8, D_MODEL), F32),
                   jax.ShapeDtypeStruct((N_CBLK, 2, LANES, LANES), F32)],
        compiler_params=_cparams(("parallel", "arbitrary")),
    )(zrest3, h3, dh3, conv_w, conv_b, bda, bdx, ba, bx, lam)


def _branch_merge(ga, gr, wa, wr, zrest):
    t = ga.shape[0]
    tm = min(512, t)
    tn = 512

    def body(ga_ref, gr_ref, wa_ref, wr_ref, mga_ref, mgr_ref, ya_ref, yr_ref, m_ref):
        ya = _dot(ga_ref[...], wa_ref[...])
        yr = _dot(gr_ref[...], wr_ref[...])
        ya_ref[...] = ya
        yr_ref[...] = yr
        m_ref[...] = (_sigmoid(mga_ref[...]) * ya + _sigmoid(mgr_ref[...]) * yr).astype(BF16)

    nj = D_MODEL // tn
    act = pl.BlockSpec((tm, D_MODEL), lambda i, j: (i, 0))
    wgt = pl.BlockSpec((D_MODEL, tn), lambda i, j: (0, j))
    out = pl.BlockSpec((tm, tn), lambda i, j: (i, j))
    return pl.pallas_call(
        body, name="branch_merge", grid=(t // tm, nj),
        in_specs=[act, act, wgt, wgt, pl.BlockSpec((tm, tn), lambda i, j: (i, 3 * nj + j)),
                  pl.BlockSpec((tm, tn), lambda i, j: (i, 4 * nj + j))],
        out_specs=[out, out, out],
        out_shape=[jax.ShapeDtypeStruct((t, D_MODEL), F32), jax.ShapeDtypeStruct((t, D_MODEL), F32),
                   jax.ShapeDtypeStruct((t, D_MODEL), BF16)],
        compiler_params=_cparams(("parallel", "parallel")),
    )(ga, gr, wa, wr, zrest, zrest)


def _out_loss(m, wout, x2, tgt2, wpost):
    t = m.shape[0]
    tm = min(256, t)

    def body(m_ref, w_ref, x_ref, t_ref, wp_ref, dy_ref, do_ref, acc_ref):
        @pl.when(pl.program_id(0) == 0)
        def _():
            acc_ref[...] = jnp.zeros_like(acc_ref)

        o = _dot(m_ref[...], w_ref[...])
        r2 = lax.rsqrt(jnp.mean(o * o, axis=-1, keepdims=True) + NORM_EPS)
        n = o * r2
        wp = wp_ref[...]
        err = (x_ref[...] + n * wp) - t_ref[...]
        dy = err * (1.0 / D_MODEL)
        dn = dy * wp
        do = r2 * (dn - n * jnp.mean(dn * n, axis=-1, keepdims=True))
        dy_ref[...] = dy
        do_ref[...] = do.astype(BF16)
        acc_ref[0:1, :] += jnp.sum(dy * n, axis=0, keepdims=True)
        acc_ref[1:2, :] += jnp.sum(err * err, axis=0, keepdims=True)

    row = pl.BlockSpec((tm, D_MODEL), lambda i: (i, 0))
    return pl.pallas_call(
        body, name="out_loss", grid=(t // tm,),
        in_specs=[row, pl.BlockSpec((D_MODEL, D_MODEL), lambda i: (0, 0)), row, row,
                  pl.BlockSpec((1, D_MODEL), lambda i: (0, 0))],
        out_specs=[row, row, pl.BlockSpec((8, D_MODEL), lambda i: (0, 0))],
        out_shape=[jax.ShapeDtypeStruct((t, D_MODEL), F32), jax.ShapeDtypeStruct((t, D_MODEL), BF16),
                   jax.ShapeDtypeStruct((8, D_MODEL), F32)],
        compiler_params=_cparams(("arbitrary",)),
    )(m, wout, x2, tgt2, wpost)


def _merge_bwd(do, wout, zrest, ya, yr):
    t = do.shape[0]
    tm = min(512, t)
    tn = 512
    nj = D_MODEL // tn

    def body(do_ref, w_ref, mga_ref, mgr_ref, ya_ref, yr_ref, dya_ref, dyr_ref, dmga_ref, dmgr_ref):
        dm = _dot_nt(do_ref[...], w_ref[...])
        sa = _sigmoid(mga_ref[...])
        sr = _sigmoid(mgr_ref[...])
        dya_ref[...] = (dm * sa).astype(BF16)
        dyr_ref[...] = (dm * sr).astype(BF16)
        dmga_ref[...] = (dm * ya_ref[...] * (sa * (1.0 - sa))).astype(BF16)
        dmgr_ref[...] = (dm * yr_ref[...] * (sr * (1.0 - sr))).astype(BF16)

    out = pl.BlockSpec((tm, tn), lambda i, j: (i, j))
    bf = jax.ShapeDtypeStruct((t, D_MODEL), BF16)
    return pl.pallas_call(
        body, name="merge_bwd", grid=(t // tm, nj),
        in_specs=[pl.BlockSpec((tm, D_MODEL), lambda i, j: (i, 0)), pl.BlockSpec((tn, D_MODEL), lambda i, j: (j, 0)),
                  pl.BlockSpec((tm, tn), lambda i, j: (i, 3 * nj + j)),
                  pl.BlockSpec((tm, tn), lambda i, j: (i, 4 * nj + j)), out, out],
        out_specs=[out, out, out, out],
        out_shape=[bf, bf, bf, bf],
        compiler_params=_cparams(("parallel", "parallel")),
    )(do, wout, zrest, zrest, ya, yr)


def _branch_bwd(dya, dyr, wa, wr, zrest, yatt, ylru):
    t = dya.shape[0]
    tm = min(512, t)
    tn = 512
    nj = D_MODEL // tn

    def body(dya_ref, dyr_ref, wa_ref, wr_ref, ga_ref, gr_ref, ya_ref, yl_ref,
             dyatt_ref, dga_ref, dyl_ref, dgr_ref):
        dga = _dot_nt(dya_ref[...], wa_ref[...])
        dgr = _dot_nt(dyr_ref[...], wr_ref[...])
        g = ga_ref[...]
        sg = _sigmoid(g)
        dyatt_ref[...] = (dga * (g * sg)).astype(BF16)
        dga_ref[...] = (dga * ya_ref[...] * (sg * (1.0 + g * (1.0 - sg)))).astype(BF16)
        g = gr_ref[...]
        sg = _sigmoid(g)
        dyl_ref[...] = dgr * (g * sg)
        dgr_ref[...] = (dgr * yl_ref[...] * (sg * (1.0 + g * (1.0 - sg)))).astype(BF16)

    act = pl.BlockSpec((tm, D_MODEL), lambda i, j: (i, 0))
    wgt = pl.BlockSpec((tn, D_MODEL), lambda i, j: (j, 0))
    out = pl.BlockSpec((tm, tn), lambda i, j: (i, j))
    bf = jax.ShapeDtypeStruct((t, D_MODEL), BF16)
    return pl.pallas_call(
        body, name="branch_bwd", grid=(t // tm, nj),
        in_specs=[act, act, wgt, wgt, pl.BlockSpec((tm, tn), lambda i, j: (i, j)),
                  pl.BlockSpec((tm, tn), lambda i, j: (i, 2 * nj + j)), out, out],
        out_specs=[out, out, out, out],
        out_shape=[bf, bf, jax.ShapeDtypeStruct((t, D_MODEL), F32), bf],
        compiler_params=_cparams(("parallel", "parallel")),
    )(dya, dyr, wa, wr, zrest, zrest, yatt, ylru)


def _dh_partial(parts, name):
    t = parts[0][0].shape[0]
    tm = min(256, t)
    np_ = len(parts)

    def body(*refs):
        o_ref = refs[-1]
        acc = _dot_nt(refs[0][...], refs[np_][...])
        for p in range(1, np_):
            acc = acc + _dot_nt(refs[p][...], refs[np_ + p][...])
        o_ref[...] = acc

    in_specs = [pl.BlockSpec((tm, dz.shape[1]), lambda i: (i, 0)) for dz, _ in parts]
    in_specs += [pl.BlockSpec(w.shape, lambda i: (0, 0)) for _, w in parts]
    return pl.pallas_call(
        body, name=name, grid=(t // tm,),
        in_specs=in_specs,
        out_specs=pl.BlockSpec((tm, D_MODEL), lambda i: (i, 0)),
        out_shape=jax.ShapeDtypeStruct((t, D_MODEL), F32),
        compiler_params=_cparams(("parallel",), vmem_mb=48),
    )(*[dz for dz, _ in parts], *[w for _, w in parts])


def _dh_final(parts, acc_in, x2, dy, wpre):
    t = x2.shape[0]
    tm = min(256, t)
    np_ = len(parts)

    def body(*refs):
        acc_ref, x_ref, dy_ref, w_ref = refs[2 * np_:2 * np_ + 4]
        gx_ref, pw_ref = refs[2 * np_ + 4:]

        @pl.when(pl.program_id(0) == 0)
        def _():
            pw_ref[...] = jnp.zeros_like(pw_ref)

        dh = acc_ref[...]
        for p in range(np_):
            dh = dh + _dot_nt(refs[p][...], refs[np_ + p][...])
        x = x_ref[...]
        r = lax.rsqrt(jnp.mean(x * x, axis=-1, keepdims=True) + NORM_EPS)
        xn = x * r
        dxn = dh * w_ref[...]
        gx_ref[...] = r * (dxn - xn * jnp.mean(dxn * xn, axis=-1, keepdims=True)) + dy_ref[...]
        pw_ref[0:1, :] += jnp.sum(dh * xn, axis=0, keepdims=True)

    row = pl.BlockSpec((tm, D_MODEL), lambda i: (i, 0))
    in_specs = [pl.BlockSpec((tm, dz.shape[1]), lambda i: (i, 0)) for dz, _ in parts]
    in_specs += [pl.BlockSpec(w.shape, lambda i: (0, 0)) for _, w in parts]
    in_specs += [row, row, row, pl.BlockSpec((1, D_MODEL), lambda i: (0, 0))]
    return pl.pallas_call(
        body, name="dh_final", grid=(t // tm,),
        in_specs=in_specs,
        out_specs=[row, pl.BlockSpec((8, D_MODEL), lambda i: (0, 0))],
        out_shape=[jax.ShapeDtypeStruct((t, D_MODEL), F32), jax.ShapeDtypeStruct((8, D_MODEL), F32)],
        compiler_params=_cparams(("arbitrary",), vmem_mb=48),
    )(*[dz for dz, _ in parts], *[w for _, w in parts], acc_in, x2, dy, wpre)


def _adamw(w, g, m, v):
    m = ADAM_B1 * m + (1.0 - ADAM_B1) * g
    v = ADAM_B2 * v + (1.0 - ADAM_B2) * (g * g)
    m_hat = m / (1.0 - ADAM_B1 ** ADAM_STEP)
    v_hat = v / (1.0 - ADAM_B2 ** ADAM_STEP)
    delta = -ADAM_LR * (m_hat / (jnp.sqrt(v_hat) + ADAM_EPS) + ADAM_WD * w)
    return delta, m, v


def _reduce_adamw(own, parts, place, w, m, v, name):
    r, c = w.shape
    tr = min(128, r)

    def body(place_ref, own_ref, p_ref, w_ref, m_ref, v_ref, g_ref, d_ref, nm_ref, nv_ref):
        mine = place_ref[1]
        own_blk = own_ref[...]
        g = jnp.where(mine == 0, own_blk, p_ref[0].astype(F32))
        for j in range(1, N_CHIPS):
            g = g + jnp.where(mine == j, own_blk, p_ref[j].astype(F32))
        d, nm, nv = _adamw(w_ref[...], g, m_ref[...], v_ref[...])
        g_ref[...] = g
        d_ref[...] = d
        nm_ref[...] = nm
        nv_ref[...] = nv

    row = pl.BlockSpec((tr, c), lambda i, pr: (i, 0))
    sh = jax.ShapeDtypeStruct((r, c), F32)
    grid_spec = pltpu.PrefetchScalarGridSpec(
        num_scalar_prefetch=1, grid=(r // tr,),
        in_specs=[row, pl.BlockSpec((N_CHIPS, tr, c), lambda i, pr: (0, i, 0)), row, row, row],
        out_specs=[row, row, row, row])
    return pl.pallas_call(
        body, name=name, grid_spec=grid_spec, out_shape=[sh, sh, sh, sh],
        compiler_params=_cparams(("parallel",)),
    )(place, own, parts, w, m, v)


def _interleave_qkv(a):
    lead = a.shape[:-1]
    return a.reshape(lead + (3, HEAD_PAIRS, LANES)).swapaxes(-3, -2).reshape(lead + (3 * D_MODEL,))


def _deinterleave_qkv(a):
    lead = a.shape[:-1]
    return a.reshape(lead + (HEAD_PAIRS, 3, LANES)).swapaxes(-3, -2).reshape(lead + (3 * D_MODEL,))


def _pack_small(pre, conv_b, rg_ba, rg_bx, lam, post, loss_row, b_in, conv_w_full, rg_wa, rg_wx):
    z = jnp.zeros((1, D_MODEL), F32)
    b_used = jnp.concatenate([b_in[:, 0:3 * D_MODEL], b_in[:, 3 * D_MODEL + HEADS:IN_TOTAL]], axis=1)
    b_f = jnp.pad(b_in[:, 3 * D_MODEL:3 * D_MODEL + HEADS], ((0, 0), (0, D_MODEL - HEADS)))
    return jnp.concatenate([
        pre, conv_b, rg_ba, rg_bx, lam, post, loss_row, z,
        b_used.reshape(9, D_MODEL), b_f, conv_w_full, z, z,
        rg_wa.reshape(64, D_MODEL), rg_wx.reshape(64, D_MODEL)], axis=0)


def _unpack_small(p):
    b_used = p[8:17].reshape(1, 9 * D_MODEL)
    b_in = jnp.concatenate([b_used[:, 0:3 * D_MODEL], p[17:18, 0:HEADS], b_used[:, 3 * D_MODEL:]], axis=1)
    return dict(pre_norm_w=p[0:1], conv_b=p[1:2], rg_ba=p[2:3], rg_bx=p[3:4], rg_lambda=p[4:5],
                post_norm_w=p[5:6], loss_row=p[6:7], b_in=b_in, conv_w_full=p[18:22],
                rg_wa=p[24:88].reshape(1, 16, 64, 64), rg_wx=p[88:152].reshape(1, 16, 64, 64))


def _reduce_small(parts, w, m, v):
    def body(p_ref, w_ref, m_ref, v_ref, g_ref, d_ref, nm_ref, nv_ref):
        g = p_ref[0]
        for j in range(1, N_DEV):
            g = g + p_ref[j]
        d, nm, nv = _adamw(w_ref[...], g, m_ref[...], v_ref[...])
        g_ref[...] = g
        d_ref[...] = d
        nm_ref[...] = nm
        nv_ref[...] = nv

    sh = jax.ShapeDtypeStruct((SMALL_ROWS, D_MODEL), F32)
    return pl.pallas_call(body, name="reduce_small", out_shape=[sh, sh, sh, sh])(parts, w, m, v)


def kernel(x, pre_norm_w, w_in, b_in, conv_w, conv_b, rg_wa, rg_ba, rg_wx, rg_bx, rg_lambda, w_branch_a, w_branch_r, w_out, post_norm_w, loss_target, m_pre_norm_w, m_w_in, m_b_in, m_conv_w, m_conv_b, m_rg_wa, m_rg_ba, m_rg_wx, m_rg_bx, m_rg_lambda, m_w_branch_a, m_w_branch_r, m_w_out, m_post_norm_w, v_pre_norm_w, v_w_in, v_b_in, v_conv_w, v_conv_b, v_rg_wa, v_rg_ba, v_rg_wx, v_rg_bx, v_rg_lambda, v_w_branch_a, v_w_branch_r, v_w_out, v_post_norm_w):
    b, s, _ = x.shape
    t = b * s
    me = 4 * lax.axis_index("x") + 2 * lax.axis_index("y") + lax.axis_index("c")
    shard_rows = D_MODEL // N_DEV

    place = jnp.stack([lax.axis_index("c"), 2 * lax.axis_index("x") + lax.axis_index("y")]).astype(jnp.int32)
    w_in_all = _gather(w_in[0].astype(BF16), "gather_w_in")
    w_full = w_in_all.transpose(1, 0, 2).reshape(D_MODEL, IN_TOTAL)
    conv_terms = jnp.concatenate(_split3(conv_w[0]), axis=0)
    conv_pad = jnp.pad(conv_terms, ((0, 16 - 3 * CONV_W), (0, D_MODEL - LANES)))
    sq_stack = jnp.concatenate([w_branch_a[0].astype(BF16), w_branch_r[0].astype(BF16), w_out[0].astype(BF16),
                                conv_pad], axis=0)
    sq_all = _gather(sq_stack, "gather_w_sq")
    wa = sq_all[:, 0:shard_rows].reshape(D_MODEL, D_MODEL)
    wr = sq_all[:, shard_rows:2 * shard_rows].reshape(D_MODEL, D_MODEL)
    wo = sq_all[:, 2 * shard_rows:3 * shard_rows].reshape(D_MODEL, D_MODEL)
    conv_all = sq_all[:, 3 * shard_rows:3 * shard_rows + 3 * CONV_W, 0:LANES].astype(F32)
    conv_all = (conv_all[:, 0:CONV_W] + conv_all[:, CONV_W:2 * CONV_W]) + conv_all[:, 2 * CONV_W:3 * CONV_W]
    conv_full = conv_all.transpose(1, 0, 2).reshape(CONV_W, D_MODEL)

    w_qkv = _interleave_qkv(w_full[:, 0:3 * D_MODEL])
    w_f = jnp.pad(w_full[:, 3 * D_MODEL:3 * D_MODEL + HEADS], ((0, 0), (0, LANES - HEADS)))
    w_rest = w_full[:, 3 * D_MODEL + HEADS:IN_USED]
    b_qkv = _interleave_qkv(b_in[:, 0:3 * D_MODEL])
    b_f = jnp.pad(b_in[:, 3 * D_MODEL:3 * D_MODEL + HEADS], ((0, 0), (0, LANES - HEADS)))
    b_rest = b_in[:, 3 * D_MODEL + HEADS:IN_USED]

    def blockdiag(w):
        w2 = w.reshape(N_CBLK, 2, HEAD_DIM, HEAD_DIM)
        zz = jnp.zeros((N_CBLK, HEAD_DIM, HEAD_DIM), w.dtype)
        top = jnp.concatenate([w2[:, 0], zz], axis=2)
        bot = jnp.concatenate([zz, w2[:, 1]], axis=2)
        return jnp.concatenate([top, bot], axis=1).astype(BF16)

    bda, bdx = blockdiag(rg_wa[0]), blockdiag(rg_wx[0])

    x2 = x.reshape(t, D_MODEL)
    tgt2 = loss_target.reshape(t, D_MODEL)
    h = _prenorm(x2, pre_norm_w)
    qkv = _mm_bias(h, w_qkv, b_qkv, BF16, "inproj_qkv")
    zrest = _mm_bias(h, w_rest, b_rest, F32, "inproj_rest")
    zf = _mm_bias(h, w_f, b_f, F32, "inproj_f")
    qkv3 = qkv.reshape(b, s, 3 * D_MODEL)
    zrest3 = zrest.reshape(b, s, 5 * D_MODEL)
    zf3 = zf.reshape(b, s, LANES)
    nq = s // ATT_TILE
    cexp3, crow = _fgate_fwd(zf3)
    crow5 = crow.reshape(b, HEAD_PAIRS, 2, nq, ATT_TILE)
    yatt3, lse5, ga3 = _attn_fwd(qkv3, cexp3, crow5, zrest3)
    ylru3, gr3 = _rnn_fwd(zrest3, conv_full, conv_b, bda, bdx, rg_ba, rg_bx, rg_lambda)
    ga, gr = ga3.reshape(t, D_MODEL), gr3.reshape(t, D_MODEL)
    ya, yr, mm = _branch_merge(ga, gr, wa, wr, zrest)
    dy, do, acc_out = _out_loss(mm, wo, x2, tgt2, post_norm_w)

    dya, dyr, dz_mga, dz_mgr = _merge_bwd(do, wo, zrest, ya, yr)
    dyatt, dz_ga, dylru, dz_gr = _branch_bwd(dya, dyr, wa, wr, zrest, yatt3.reshape(t, D_MODEL),
                                             ylru3.reshape(t, D_MODEL))
    dz_xr3, pvec, dbd = _rnn_bwd(zrest3, ylru3, dylru.reshape(b, s, D_MODEL), conv_full, conv_b, bda, bdx,
                                 rg_ba, rg_bx, rg_lambda)
    dqkv3, dc3 = _attn_bwd(qkv3, dyatt.reshape(b, s, D_MODEL), yatt3, lse5, crow5, cexp3)
    dz_f = _fgate_bwd(dc3, zf3).reshape(t, LANES)
    dz_qkv = dqkv3.reshape(t, 3 * D_MODEL)
    dz_xr = dz_xr3.reshape(t, D_MODEL)

    wt = lambda lo: w_rest[:, lo * D_MODEL:(lo + 1) * D_MODEL]
    dh_a = _dh_partial([(dz_qkv, w_qkv), (dz_f, w_f)], "dh_qkv")
    grad_x2, acc_pre = _dh_final(
        [(dz_ga, wt(0)), (dz_xr, wt(1)), (dz_gr, wt(2)), (dz_mga, wt(3)), (dz_mgr, wt(4))],
        dh_a, x2, dy, pre_norm_w)

    dw_qkv, db_qkv = _mm_tn(h, dz_qkv, "dw_qkv")
    dw_f, db_f = _mm_tn(h, dz_f, "dw_f")
    dw_parts, db_parts = [], []
    for nm, dzp in (("ga", dz_ga), ("xr", dz_xr), ("gr", dz_gr), ("mga", dz_mga), ("mgr", dz_mgr)):
        dwp, dbp = _mm_tn(h, dzp, "dw_" + nm)
        dw_parts.append(dwp)
        db_parts.append(dbp[0:1])
    dw_a, _ = _mm_tn(ga, dya, "dw_a")
    dw_r, _ = _mm_tn(gr, dyr, "dw_r")
    dw_o, _ = _mm_tn(mm, do, "dw_o")

    zeros_tail = jnp.zeros((D_MODEL, IN_TOTAL - IN_USED), F32)
    dw_in_full = jnp.concatenate([_deinterleave_qkv(dw_qkv), dw_f[:, 0:HEADS]] + dw_parts + [zeros_tail], axis=1)
    dw_in_send = dw_in_full.reshape(D_MODEL, N_CHIPS, 2, W_SHARD).transpose(2, 1, 0, 3)
    by_dest = lambda a: a.reshape(N_CHIPS, 2, shard_rows, D_MODEL).transpose(1, 0, 2, 3)
    dw_sq_send = jnp.concatenate([by_dest(dw_a), by_dest(dw_r), by_dest(dw_o)], axis=2)

    db_in_full = jnp.concatenate([_deinterleave_qkv(db_qkv[0:1]), db_f[0:1, 0:HEADS]] + db_parts
                                 + [jnp.zeros((1, IN_TOTAL - IN_USED), F32)], axis=1)
    d_rg_wa = jnp.stack([dbd[:, 0, 0:HEAD_DIM, 0:HEAD_DIM], dbd[:, 0, HEAD_DIM:, HEAD_DIM:]], axis=1)
    d_rg_wx = jnp.stack([dbd[:, 1, 0:HEAD_DIM, 0:HEAD_DIM], dbd[:, 1, HEAD_DIM:, HEAD_DIM:]], axis=1)
    small_g = _pack_small(acc_pre[0:1], pvec[4:5], pvec[5:6], pvec[6:7], pvec[7:8], acc_out[0:1], acc_out[1:2],
                          db_in_full, pvec[0:4], d_rg_wa, d_rg_wx)

    sib_in, sib_sq = _swap_with_sibling([dw_in_send, dw_sq_send], "swap_dw")
    chip_in, own_in = _pair_add(dw_in_send, sib_in, place, "pair_add_in")
    chip_sq, own_sq = _pair_add(dw_sq_send, sib_sq, place, "pair_add_sq")
    recv_in, recv_sq = _exchange_chips([chip_in, chip_sq], "exchange_dw")
    small_all = _gather(small_g, "gather_small")

    g_in, d_in, nm_in, nv_in = _reduce_adamw(own_in, recv_in, place, w_in[0], m_w_in[0], v_w_in[0], "adamw_w_in")
    sq_w = jnp.concatenate([w_branch_a[0], w_branch_r[0], w_out[0]], axis=0)
    sq_m = jnp.concatenate([m_w_branch_a[0], m_w_branch_r[0], m_w_out[0]], axis=0)
    sq_v = jnp.concatenate([v_w_branch_a[0], v_w_branch_r[0], v_w_out[0]], axis=0)
    g_sq, d_sq, nm_sq, nv_sq = _reduce_adamw(own_sq, recv_sq, place, sq_w, sq_m, sq_v, "adamw_w_sq")

    def place_conv(a):
        return lax.dynamic_update_slice(jnp.zeros((CONV_W, D_MODEL), F32), a[0], (0, me * LANES))

    zrow = jnp.zeros((1, D_MODEL), F32)
    small_w = _pack_small(pre_norm_w, conv_b, rg_ba, rg_bx, rg_lambda, post_norm_w, zrow, b_in,
                          place_conv(conv_w), rg_wa[0], rg_wx[0])
    small_m = _pack_small(m_pre_norm_w, m_conv_b, m_rg_ba, m_rg_bx, m_rg_lambda, m_post_norm_w, zrow, m_b_in,
                          place_conv(m_conv_w), m_rg_wa[0], m_rg_wx[0])
    small_v = _pack_small(v_pre_norm_w, v_conv_b, v_rg_ba, v_rg_bx, v_rg_lambda, v_post_norm_w, zrow, v_b_in,
                          place_conv(v_conv_w), v_rg_wa[0], v_rg_wx[0])
    outs_small = [_unpack_small(p) for p in _reduce_small(small_all, small_w, small_m, small_v)]

    loss = (0.5 / D_MODEL) * jnp.sum(outs_small[0]["loss_row"])

    def leaf(kind, name):
        if name == "w_in":
            return (g_in, d_in, nm_in, nv_in)[kind][None]
        if name in ("w_branch_a", "w_branch_r", "w_out"):
            j = ("w_branch_a", "w_branch_r", "w_out").index(name)
            return (g_sq, d_sq, nm_sq, nv_sq)[kind][None, j * shard_rows:(j + 1) * shard_rows]
        if name == "conv_w":
            return lax.dynamic_slice(outs_small[kind]["conv_w_full"], (0, me * LANES), (CONV_W, LANES))[None]
        return outs_small[kind][name]

    names = ["pre_norm_w", "w_in", "b_in", "conv_w", "conv_b", "rg_wa", "rg_ba", "rg_wx", "rg_bx", "rg_lambda",
             "w_branch_a", "w_branch_r", "w_out", "post_norm_w"]
    out = [loss, grad_x2.reshape(b, s, D_MODEL)]
    for kind in range(4):
        out += [leaf(kind, nm) for nm in names]
    return tuple(out)
```

```python
import jax
import jax.numpy as jnp
from jax import lax
from jax.experimental import pallas as pl
from jax.experimental.pallas import tpu as pltpu

F32 = jnp.float32
BF16 = jnp.bfloat16

N_DEV = 8
D_MODEL = 1024
HEADS = 16
HEAD_DIM = 64
HEAD_PAIRS = HEADS // 2
LANES = 128
N_CBLK = D_MODEL // LANES
CONV_W = 4
RG_C = 8.0
NORM_EPS = 1e-6
MASK_VALUE = -1e30
IN_USED = 8208
IN_TOTAL = 9232
W_SHARD = IN_TOTAL // N_DEV

ADAM_LR = 0.001
ADAM_B1 = 0.9
ADAM_B2 = 0.999
ADAM_EPS = 1e-08
ADAM_WD = 0.01
ADAM_STEP = 10

ATT_TILE = 256
SCAN_TILE = 256
SMALL_ROWS = 152


def _cparams(sem=None, vmem_mb=None):
    kw = {}
    if sem is not None:
        kw["dimension_semantics"] = sem
    if vmem_mb is not None:
        kw["vmem_limit_bytes"] = vmem_mb * 1024 * 1024
    return pltpu.CompilerParams(**kw)


def _sigmoid(x):
    return 1.0 / (1.0 + jnp.exp(-x))


def _softplus(x):
    return jnp.maximum(x, 0.0) + jnp.log1p(jnp.exp(-jnp.abs(x)))


def _expm1(x):
    p = x * (1.0 + x * (1.0 / 2 + x * (1.0 / 6 + x * (1.0 / 24 + x * (1.0 / 120 + x * (
        1.0 / 720 + x * (1.0 / 5040 + x * (1.0 / 40320))))))))
    return jnp.where(jnp.abs(x) < 0.5, p, jnp.exp(x) - 1.0)


def _split3(x):
    hi = x.astype(BF16)
    r1 = x - hi.astype(F32)
    mid = r1.astype(BF16)
    lo = (r1 - mid.astype(F32)).astype(BF16)
    return hi, mid, lo


def _dot(a, b):
    return jnp.dot(a, b, preferred_element_type=F32)


def _dot_nt(a, b):
    return lax.dot_general(a, b, (((1,), (1,)), ((), ())), preferred_element_type=F32)


def _dot_tn(a, b):
    return lax.dot_general(a, b, (((0,), (0,)), ((), ())), preferred_element_type=F32)


def _iota(shape, dim):
    return lax.broadcasted_iota(jnp.int32, shape, dim)


_ANY = pl.BlockSpec(memory_space=pl.ANY)
_MESH = pl.DeviceIdType.MESH
N_CHIPS = 4


def _place():
    x, y, c = lax.axis_index("x"), lax.axis_index("y"), lax.axis_index("c")
    other_chips = [(1 - x, y), (x, 1 - y), (1 - x, 1 - y)]
    return x, y, c, other_chips


def _gather(x_shard, name, after=None):
    deps = [] if after is None else [after]

    def body(x_ref, *refs):
        out_ref, send_sems, recv_sems, local_sem = refs[len(deps):]
        x, y, c, chips = _place()
        me, sibling = (x, y, c), (x, y, 1 - c)

        def slot(p):
            return out_ref.at[4 * p[0] + 2 * p[1] + p[2]]

        def copy(k, block, to, src=None):
            return pltpu.make_async_remote_copy(
                src_ref=slot(block) if src is None else src, dst_ref=slot(block),
                send_sem=send_sems.at[k], recv_sem=recv_sems.at[k], device_id=to, device_id_type=_MESH)

        mine = pltpu.make_async_copy(x_ref, slot(me), local_sem)
        mine.start()
        first = [copy(0, me, sibling, src=x_ref)]
        first += [copy(1 + j, me, (*chip, c), src=x_ref) for j, chip in enumerate(chips)]
        for cp in first:
            cp.start()
        passed = [copy(4 + j, (*chip, c), sibling) for j, chip in enumerate(chips)]
        for j, chip in enumerate(chips):
            copy(1 + j, (*chip, c), me).wait_recv()
            passed[j].start()
        copy(0, sibling, me).wait_recv()
        for j, chip in enumerate(chips):
            copy(4 + j, (*chip, 1 - c), me).wait_recv()
        for cp in first + passed:
            cp.wait_send()
        mine.wait()

    return pl.pallas_call(
        body, name=name,
        out_shape=jax.ShapeDtypeStruct((N_DEV,) + tuple(x_shard.shape), x_shard.dtype),
        in_specs=[_ANY] * (1 + len(deps)), out_specs=_ANY,
        scratch_shapes=[pltpu.SemaphoreType.DMA((7,)), pltpu.SemaphoreType.DMA((7,)), pltpu.SemaphoreType.DMA],
    )(x_shard, *deps)


def _swap_with_sibling(srcs, name):
    n = len(srcs)

    def body(*refs):
        src_refs, out_refs = refs[:n], refs[n:2 * n]
        send_sems, recv_sems = refs[2 * n:]
        x, y, c, _ = _place()
        cps = [pltpu.make_async_remote_copy(
            src_ref=src_refs[i].at[1 - c], dst_ref=out_refs[i], send_sem=send_sems.at[i], recv_sem=recv_sems.at[i],
            device_id=(x, y, 1 - c), device_id_type=_MESH) for i in range(n)]
        for cp in cps:
            cp.start()
        for cp in cps:
            cp.wait()

    return pl.pallas_call(
        body, name=name,
        out_shape=[jax.ShapeDtypeStruct(a.shape[1:], a.dtype) for a in srcs],
        in_specs=[_ANY] * n, out_specs=[_ANY] * n,
        scratch_shapes=[pltpu.SemaphoreType.DMA((n,)), pltpu.SemaphoreType.DMA((n,))],
    )(*srcs)


def _pair_add(src, recv, place, name):
    _, _, r, c = src.shape
    tr = min(128, r)

    def body(place_ref, a_ref, b_ref, q16_ref, own_ref):
        q = a_ref[...] + b_ref[...]
        q16_ref[...] = q.astype(BF16)

        @pl.when(pl.program_id(1) == place_ref[1])
        def _():
            own_ref[...] = q

    grid_spec = pltpu.PrefetchScalarGridSpec(
        num_scalar_prefetch=1, grid=(r // tr, N_CHIPS),
        in_specs=[pl.BlockSpec((None, None, tr, c), lambda i, j, pr: (pr[0], j, i, 0)),
                  pl.BlockSpec((None, tr, c), lambda i, j, pr: (j, i, 0))],
        out_specs=[pl.BlockSpec((None, tr, c), lambda i, j, pr: (j, i, 0)),
                   pl.BlockSpec((tr, c), lambda i, j, pr: (i, 0))])
    return pl.pallas_call(
        body, name=name, grid_spec=grid_spec,
        out_shape=[jax.ShapeDtypeStruct((N_CHIPS, r, c), BF16), jax.ShapeDtypeStruct((r, c), F32)],
        compiler_params=_cparams(("parallel", "arbitrary")),
    )(place, src, recv)


_HBM = pl.BlockSpec(memory_space=pltpu.HBM)
_SEM = pl.BlockSpec(memory_space=pltpu.SEMAPHORE)
_DATAFLOW = pltpu.SideEffectType.DATAFLOW_SIDE_EFFECTING


def _chip_copy(src_ref, land_ref, send_sem, recv_sem, k, chips, c, land):
    chip = chips[k]
    return pltpu.make_async_remote_copy(
        src_ref=src_ref.at[2 * chip[0] + chip[1]], dst_ref=land_ref.at[land],
        send_sem=send_sem, recv_sem=recv_sem, device_id=(*chip, c), device_id_type=_MESH)


def _exchange_chips_start(srcs, name):
    n = len(srcs)
    ncp = 3 * n

    def body(*refs):
        src_refs, land_refs = refs[:n], refs[n:2 * n]
        sems = refs[4 * n:4 * n + 2 * ncp]
        token = refs[-1]
        x, y, c, chips = _place()
        for i in range(n):
            for k in range(3):
                j = 3 * i + k
                _chip_copy(src_refs[i], land_refs[i], sems[j], sems[ncp + j], k, chips, c, 2 * x + y).start()
        token[...] = jnp.zeros_like(token)

    hbm = [pltpu.HBM(a.shape, a.dtype) for a in srcs]
    lands = [pltpu.with_memory_space_constraint(lax.empty(a.shape, a.dtype), pltpu.HBM) for a in srcs]
    res = pl.pallas_call(
        body, name=name,
        out_shape=(*hbm, *hbm, *([pltpu.SemaphoreType.DMA(())] * (2 * ncp)), jax.ShapeDtypeStruct((8, LANES), F32)),
        in_specs=[_HBM] * (2 * n),
        out_specs=(*([_HBM] * (2 * n)), *([_SEM] * (2 * ncp)), pl.BlockSpec(memory_space=pltpu.VMEM)),
        input_output_aliases={i: i for i in range(2 * n)},
        compiler_params=pltpu.CompilerParams(has_side_effects=_DATAFLOW),
    )(*[pltpu.with_memory_space_constraint(a, pltpu.HBM) for a in srcs], *lands)
    return list(res[2 * n:2 * n + 2 * ncp]), list(res[:n]), list(res[n:2 * n]), res[-1]


def _exchange_chips_wait(sems, srcs, lands, after, name):
    n = len(srcs)
    ncp = 3 * n

    def body(*refs):
        src_refs, land_refs = refs[:n], refs[n:2 * n]
        sem_refs = refs[2 * n:2 * n + 2 * ncp]
        x, y, c, chips = _place()
        for i in range(n):
            for k in range(3):
                j = 3 * i + k
                cp = _chip_copy(src_refs[i], land_refs[i], sem_refs[j], sem_refs[ncp + j], k, chips, c,
                                2 * chips[k][0] + chips[k][1])
                cp.wait_send()
                cp.wait_recv()

    hbm = [pltpu.HBM(a.shape, a.dtype) for a in srcs]
    res = pl.pallas_call(
        body, name=name, out_shape=(*hbm, *hbm),
        in_specs=[_HBM] * (2 * n) + [_SEM] * (2 * ncp) + [_ANY], out_specs=tuple([_HBM] * (2 * n)),
        input_output_aliases={i: i for i in range(2 * n)},
        compiler_params=pltpu.CompilerParams(has_side_effects=_DATAFLOW),
    )(*srcs, *lands, *sems, after)
    return list(res[n:2 * n])


def _prenorm(x2, w):
    t = x2.shape[0]
    tm = min(512, t)

    def body(x_ref, w_ref, h_ref):
        x = x_ref[...]
        r = lax.rsqrt(jnp.mean(x * x, axis=-1, keepdims=True) + NORM_EPS)
        h_ref[...] = (x * r * w_ref[...]).astype(BF16)

    return pl.pallas_call(
        body, name="prenorm", grid=(t // tm,),
        in_specs=[pl.BlockSpec((tm, D_MODEL), lambda i: (i, 0)), pl.BlockSpec((1, D_MODEL), lambda i: (0, 0))],
        out_specs=pl.BlockSpec((tm, D_MODEL), lambda i: (i, 0)),
        out_shape=jax.ShapeDtypeStruct((t, D_MODEL), BF16),
        compiler_params=_cparams(("parallel",)),
    )(x2, w)


def _mm_bias(a, b, bias, out_dtype, name):
    m, k = a.shape
    n = b.shape[1]
    tm = min(512, m)
    tn = min(1024, n)

    def body(a_ref, b_ref, bias_ref, o_ref):
        o_ref[...] = (_dot(a_ref[...], b_ref[...]) + bias_ref[...]).astype(o_ref.dtype)

    return pl.pallas_call(
        body, name=name, grid=(n // tn, m // tm),
        in_specs=[pl.BlockSpec((tm, k), lambda j, i: (i, 0)), pl.BlockSpec((k, tn), lambda j, i: (0, j)),
                  pl.BlockSpec((1, tn), lambda j, i: (0, j))],
        out_specs=pl.BlockSpec((tm, tn), lambda j, i: (i, j)),
        out_shape=jax.ShapeDtypeStruct((m, n), out_dtype),
        compiler_params=_cparams(("parallel", "parallel")),
    )(a, b, bias)


def _mm_tn(a, b, name):
    t, m = a.shape
    n = b.shape[1]
    tn = min(1024, n)
    tk = min(512, t)

    def body(a_ref, b_ref, o_ref, s_ref):
        kk = pl.program_id(1)

        @pl.when(kk == 0)
        def _():
            o_ref[...] = jnp.zeros_like(o_ref)
            s_ref[...] = jnp.zeros_like(s_ref)

        bb = b_ref[...]
        o_ref[...] += _dot_tn(a_ref[...], bb)
        s_ref[0:1, :] += jnp.sum(bb.astype(F32), axis=0, keepdims=True)

    return pl.pallas_call(
        body, name=name, grid=(n // tn, t // tk),
        in_specs=[pl.BlockSpec((tk, m), lambda j, kk: (kk, 0)), pl.BlockSpec((tk, tn), lambda j, kk: (kk, j))],
        out_specs=[pl.BlockSpec((m, tn), lambda j, kk: (0, j)), pl.BlockSpec((8, tn), lambda j, kk: (0, j))],
        out_shape=[jax.ShapeDtypeStruct((m, n), F32), jax.ShapeDtypeStruct((8, n), F32)],
        compiler_params=_cparams(("parallel", "arbitrary")),
    )(a, b)


def _fgate_fwd(zf3):
    b, s, _ = zf3.shape
    tb = SCAN_TILE
    nb = s // tb

    def body(z_ref, cexp_ref, crow_ref):
        tri = (_iota((tb, tb), 1) <= _iota((tb, tb), 0)).astype(BF16)
        expand = ((_iota((LANES, D_MODEL), 1) >> 6) == _iota((LANES, D_MODEL), 0)).astype(BF16)
        carry = jnp.zeros((1, LANES), F32)
        for i in range(nb):
            rows = slice(i * tb, (i + 1) * tb)
            z = z_ref[rows, :]
            lf = jnp.minimum(z, 0.0) - jnp.log1p(jnp.exp(-jnp.abs(z)))
            cb = sum(_dot(tri, part) for part in _split3(lf)) + carry
            carry = cb[tb - 1:tb, :]
            cexp_ref[rows, :] = sum(_dot(part, expand) for part in _split3(cb))
            crow_ref[:, rows] = cb.T[0:HEADS, :]

    return pl.pallas_call(
        body, name="fgate_fwd", grid=(b,),
        in_specs=[pl.BlockSpec((None, s, LANES), lambda i: (i, 0, 0))],
        out_specs=[pl.BlockSpec((None, s, D_MODEL), lambda i: (i, 0, 0)),
                   pl.BlockSpec((None, HEADS, s), lambda i: (i, 0, 0))],
        out_shape=[jax.ShapeDtypeStruct((b, s, D_MODEL), F32), jax.ShapeDtypeStruct((b, HEADS, s), F32)],
        compiler_params=_cparams(("parallel",)),
    )(zf3)


def _fgate_bwd(dc3, zf3):
    b, s, _ = zf3.shape
    tb = SCAN_TILE
    nb = s // tb

    def body(dc_ref, z_ref, o_ref):
        tri = (_iota((tb, tb), 1) >= _iota((tb, tb), 0)).astype(BF16)
        carry = jnp.zeros((1, LANES), F32)
        for i in reversed(range(nb)):
            rows = slice(i * tb, (i + 1) * tb)
            dlf = sum(_dot(tri, part) for part in _split3(dc_ref[rows, :])) + carry
            carry = dlf[0:1, :]
            o_ref[rows, :] = (dlf * _sigmoid(-z_ref[rows, :])).astype(BF16)

    return pl.pallas_call(
        body, name="fgate_bwd", grid=(b,),
        in_specs=[pl.BlockSpec((None, s, LANES), lambda i: (i, 0, 0)),
                  pl.BlockSpec((None, s, LANES), lambda i: (i, 0, 0))],
        out_specs=pl.BlockSpec((None, s, LANES), lambda i: (i, 0, 0)),
        out_shape=jax.ShapeDtypeStruct((b, s, LANES), BF16),
        compiler_params=_cparams(("parallel",)),
    )(dc3, zf3)


def _spare(hh):
    return HEAD_DIM if hh == 0 else 0


def _put_cols(tile, mine, cols, first):
    lane = _iota((1, LANES), 1)
    out = jnp.where(mine, tile, jnp.zeros((), tile.dtype))
    for j, c in enumerate(cols):
        out = jnp.where(lane == first + j, c, out)
    return out


def _put_rows(tile, mine, rows, first):
    sub = _iota((LANES, 1), 0)
    out = jnp.where(mine, tile, jnp.zeros((), tile.dtype))
    for j, r in enumerate(rows):
        out = jnp.where(sub == first + j, r, out)
    return out


def _transpose_bf16(a):
    return a.astype(F32).T.astype(BF16)


def _attn_fwd(qkv3, cexp3, crow5, zrest3):
    b, s, _ = qkv3.shape
    ta = ATT_TILE
    nq = s // ta
    hd = HEAD_DIM

    def body(qkv_ref, cq_ref, ck_ref, g_ref, y_ref, lse_ref, ga_ref, kt_scr, v_scr):
        lane = _iota((1, LANES), 1)
        sub = _iota((LANES, 1), 0)
        lane_mine = (lane < hd, lane >= hd)
        sub_mine = (sub < hd, sub >= hd)
        causal = _iota((ta, ta), 0) >= _iota((ta, ta), 1)
        one = jnp.ones((), BF16)

        for kj in range(nq):
            rows = slice(kj * ta, (kj + 1) * ta)
            kt = _transpose_bf16(qkv_ref[rows, LANES:2 * LANES])
            v = qkv_ref[rows, 2 * LANES:3 * LANES]
            for hh in range(2):
                ck = list(_split3(-ck_ref[hh, kj:kj + 1, :]))
                kt_scr[hh, kj] = _put_rows(kt, sub_mine[hh], [one, one, one] + ck, _spare(hh))
                v_scr[hh, kj] = _put_cols(v, lane_mine[hh], [one], _spare(hh))

        for qi in range(nq):
            rows = slice(qi * ta, (qi + 1) * ta)
            q = qkv_ref[rows, 0:LANES] * 0.125
            cq = cq_ref[rows, :]
            qh = [_put_cols(q, lane_mine[hh], list(_split3(cq[:, hh * hd:hh * hd + 1])) + [one, one, one], _spare(hh))
                  for hh in range(2)]
            st = [(jnp.full((ta, 1), MASK_VALUE, F32), jnp.zeros((ta, LANES), F32))] * 2
            for kj in range(qi + 1):
                for hh in range(2):
                    m, acc = st[hh]
                    sc = _dot(qh[hh], kt_scr[hh, kj])
                    if kj == qi:
                        sc = jnp.where(causal, sc, MASK_VALUE)
                    mn = jnp.maximum(m, jnp.max(sc, axis=-1, keepdims=True))
                    p = jnp.exp(sc - mn).astype(BF16)
                    st[hh] = (mn, jnp.exp(m - mn) * acc + _dot(p, v_scr[hh, kj]))
            (ma, acca), (mb, accb) = st
            la = acca[:, hd:hd + 1]
            lb = accb[:, 0:1]
            y = jnp.where(lane_mine[0], acca * (1.0 / la), accb * (1.0 / lb))
            lse = jnp.where(lane_mine[0], ma + jnp.log(la), mb + jnp.log(lb)).T
            lse_ref[0, qi:qi + 1, :] = lse[0:1, :]
            lse_ref[1, qi:qi + 1, :] = lse[hd:hd + 1, :]
            y_ref[rows, :] = y
            g = g_ref[rows, :]
            ga_ref[rows, :] = (y * (g * _sigmoid(g))).astype(BF16)

    blk = lambda w: pl.BlockSpec((None, s, w), lambda i, p: (i, 0, p))
    rows5 = pl.BlockSpec((None, None, 2, nq, ta), lambda i, p: (i, p, 0, 0, 0))
    return pl.pallas_call(
        body, name="attn_fwd", grid=(b, HEAD_PAIRS),
        in_specs=[blk(3 * LANES), blk(LANES), rows5, blk(LANES)],
        out_specs=[blk(LANES), rows5, blk(LANES)],
        out_shape=[jax.ShapeDtypeStruct((b, s, D_MODEL), F32),
                   jax.ShapeDtypeStruct((b, HEAD_PAIRS, 2, nq, ta), F32),
                   jax.ShapeDtypeStruct((b, s, D_MODEL), BF16)],
        scratch_shapes=[pltpu.VMEM((2, nq, LANES, ta), BF16), pltpu.VMEM((2, nq, ta, LANES), BF16)],
        compiler_params=_cparams(("parallel", "parallel")),
    )(qkv3, cexp3, crow5, zrest3)


def _attn_bwd(qkv3, do3, y3, lse5, crow5, cexp3):
    b, s, _ = qkv3.shape
    ta = ATT_TILE
    nq = s // ta
    hd = HEAD_DIM

    def body(qkv_ref, do_ref, y_ref, lse_ref, crow_ref, cexp_ref, dqkv_ref, dc_ref,
             qa_scr, doa_scr, qst_scr, dot_scr, kt_scr, vt_scr, dq_scr, rs_scr):
        pair = pl.program_id(1)
        lane = _iota((1, LANES), 1)
        sub = _iota((LANES, 1), 0)
        lane_mine = (lane < hd, lane >= hd)
        sub_mine = (sub < hd, sub >= hd)
        causal = _iota((ta, ta), 0) >= _iota((ta, ta), 1)
        one = jnp.ones((), BF16)
        zero = jnp.zeros((), BF16)

        @pl.when(pair == 0)
        def _():
            dc_ref[...] = jnp.zeros_like(dc_ref)

        for i in range(nq):
            rows = slice(i * ta, (i + 1) * ta)
            qs = qkv_ref[rows, 0:LANES] * 0.125
            qst = _transpose_bf16(qs)
            kt = _transpose_bf16(qkv_ref[rows, LANES:2 * LANES])
            vt = _transpose_bf16(qkv_ref[rows, 2 * LANES:3 * LANES])
            do = do_ref[rows, :]
            dof = do.astype(F32)
            dot = dof.T.astype(BF16)
            pr = y_ref[rows, :] * dof
            cq = cexp_ref[rows, :]
            lse_c = jnp.where(sub == 0, lse_ref[0, i:i + 1, :],
                              jnp.where(sub == 1, lse_ref[1, i:i + 1, :], 0.0)).T
            for hh in range(2):
                sp = _spare(hh)
                dsum = jnp.sum(jnp.where(lane_mine[hh], pr, 0.0), axis=-1, keepdims=True)
                bias = cq[:, hh * hd:hh * hd + 1] - lse_c[:, hh:hh + 1]
                qa_scr[hh, i] = _put_cols(qs, lane_mine[hh], list(_split3(bias)) + [one, one, one], sp)
                doa_scr[hh, i] = _put_cols(do, lane_mine[hh], list(_split3(-dsum)), sp)
                qst_scr[hh, i] = jnp.where(sub_mine[hh], qst, zero)
                dot_scr[hh, i] = jnp.where(sub_mine[hh], dot, zero)
                ck = list(_split3(-crow_ref[hh, i:i + 1, :]))
                kt_scr[hh, i] = _put_rows(kt, sub_mine[hh], [one, one, one] + ck, sp)
                vt_scr[hh, i] = _put_rows(vt, sub_mine[hh], [one, one, one], sp)
            dq_scr[i] = jnp.zeros((ta, LANES), F32)
            rs_scr[i] = jnp.zeros((ta, LANES), F32)

        for kj in range(nq):
            krows = slice(kj * ta, (kj + 1) * ta)
            k = qkv_ref[krows, LANES:2 * LANES]
            km = (jnp.where(lane_mine[0], k, zero), jnp.where(lane_mine[1], k, zero))
            dkt = jnp.zeros((LANES, ta), F32)
            dvt = jnp.zeros((LANES, ta), F32)
            dcp = [jnp.zeros((8, ta), F32), jnp.zeros((8, ta), F32)]
            for qi in range(kj, nq):
                dq = jnp.zeros((ta, LANES), F32)
                rs = []
                for hh in range(2):
                    sc = _dot(qa_scr[hh, qi], kt_scr[hh, kj])
                    if qi == kj:
                        sc = jnp.where(causal, sc, MASK_VALUE)
                    p = jnp.exp(sc)
                    dsf = p * _dot(doa_scr[hh, qi], vt_scr[hh, kj])
                    dcp[hh] = dcp[hh] + jnp.sum(dsf.reshape(ta // 8, 8, ta), axis=0)
                    rs.append(jnp.sum(dsf, axis=-1, keepdims=True))
                    ds = dsf.astype(BF16)
                    dq = dq + _dot(ds, km[hh])
                    dkt = dkt + _dot(qst_scr[hh, qi], ds)
                    dvt = dvt + _dot(dot_scr[hh, qi], p.astype(BF16))
                dq_scr[qi] += dq
                rs_scr[qi] += jnp.where(lane == 0, rs[0], jnp.where(lane == 1, rs[1], 0.0))
            dqkv_ref[krows, LANES:2 * LANES] = dkt.T.astype(BF16)
            dqkv_ref[krows, 2 * LANES:3 * LANES] = dvt.T.astype(BF16)
            dca = jnp.sum(dcp[0], axis=0, keepdims=True)
            dcb = jnp.sum(dcp[1], axis=0, keepdims=True)
            dcs = jnp.where(sub == 0, dca, jnp.where(sub == 1, dcb, 0.0)).T
            dc_ref[krows, :] += (jnp.where(lane == 2 * pair, -dcs[:, 0:1], 0.0)
                                 + jnp.where(lane == 2 * pair + 1, -dcs[:, 1:2], 0.0))
        for qi in range(nq):
            rows = slice(qi * ta, (qi + 1) * ta)
            dqkv_ref[rows, 0:LANES] = (dq_scr[qi] * 0.125).astype(BF16)
            rq = rs_scr[qi]
            dc_ref[rows, :] += (jnp.where(lane == 2 * pair, rq[:, 0:1], 0.0)
                                + jnp.where(lane == 2 * pair + 1, rq[:, 1:2], 0.0))

    blk = lambda w: pl.BlockSpec((None, s, w), lambda i, p: (i, 0, p))
    rows5 = pl.BlockSpec((None, None, 2, nq, ta), lambda i, p: (i, p, 0, 0, 0))
    by_rows = lambda: pltpu.VMEM((2, nq, ta, LANES), BF16)
    by_cols = lambda: pltpu.VMEM((2, nq, LANES, ta), BF16)
    return pl.pallas_call(
        body, name="attn_bwd", grid=(b, HEAD_PAIRS),
        in_specs=[blk(3 * LANES), blk(LANES), blk(LANES), rows5, rows5, blk(LANES)],
        out_specs=[blk(3 * LANES), pl.BlockSpec((None, s, LANES), lambda i, p: (i, 0, 0))],
        out_shape=[jax.ShapeDtypeStruct((b, s, 3 * D_MODEL), BF16), jax.ShapeDtypeStruct((b, s, LANES), F32)],
        scratch_shapes=[by_rows(), by_rows(), by_cols(), by_cols(), by_cols(), by_cols(),
                        pltpu.VMEM((nq, ta, LANES), F32), pltpu.VMEM((nq, ta, LANES), F32)],
        compiler_params=_cparams(("parallel", "arbitrary")),
    )(qkv3, do3, y3, lse5, crow5, cexp3)


def _rnn_common(xr, cw_ref, cb_ref, bda_ref, bdx_ref, ba_ref, bx_ref, lam_ref, s):
    rows = _iota((s, LANES), 0)

    def down(v, k):
        return jnp.where(rows >= k, pltpu.roll(v, k, 0), 0.0)

    x1, x2, x3 = down(xr, 1), down(xr, 2), down(xr, 3)
    xc = cb_ref[...] + cw_ref[0:1, :] * x3
    xc = xc + cw_ref[1:2, :] * x2
    xc = xc + cw_ref[2:3, :] * x1
    xc = xc + cw_ref[3:4, :] * xr
    xcb = xc.astype(BF16)
    r = _sigmoid(_dot(xcb, bda_ref[...]) + ba_ref[...])
    i = _sigmoid(_dot(xcb, bdx_ref[...]) + bx_ref[...])
    sp = _softplus(-lam_ref[...])
    log_a = (-RG_C * r) * sp
    a = jnp.exp(log_a)
    e2 = -_expm1(2.0 * log_a)
    sq = jnp.sqrt(jnp.maximum(e2, 0.0))
    return rows, (x1, x2, x3), xc, xcb, r, i, sp, a, e2, sq


def _rnn_specs(s):
    blk = lambda off: pl.BlockSpec((None, s, LANES), lambda cb, i: (i, 0, off + cb))
    vec = lambda r: pl.BlockSpec((r, LANES), lambda cb, i: (0, cb))
    mat = pl.BlockSpec((None, LANES, LANES), lambda cb, i: (cb, 0, 0))
    return blk, vec, mat


def _rnn_fwd(zrest3, conv_w, conv_b, bda, bdx, ba, bx, lam):
    b, s, _ = zrest3.shape

    def body(xr_ref, g_ref, cw_ref, cb_ref, bda_ref, bdx_ref, ba_ref, bx_ref, lam_ref, h_ref, gr_ref):
        xr = xr_ref[...]
        rows, _, xc, _, _, i, _, a, _, sq = _rnn_common(
            xr, cw_ref, cb_ref, bda_ref, bdx_ref, ba_ref, bx_ref, lam_ref, s)
        u = sq * (i * xc)
        sh = 1
        while sh < s:
            keep = rows >= sh
            ur = jnp.where(keep, pltpu.roll(u, sh, 0), 0.0)
            u = u + a * ur
            if sh * 2 < s:
                a = a * jnp.where(keep, pltpu.roll(a, sh, 0), 1.0)
            sh *= 2
        h_ref[...] = u
        g = g_ref[...]
        gr_ref[...] = (u * (g * _sigmoid(g))).astype(BF16)

    blk, vec, mat = _rnn_specs(s)
    return pl.pallas_call(
        body, name="rnn_fwd", grid=(N_CBLK, b),
        in_specs=[blk(N_CBLK), blk(2 * N_CBLK), vec(CONV_W), vec(1), mat, mat, vec(1), vec(1), vec(1)],
        out_specs=[blk(0), blk(0)],
        out_shape=[jax.ShapeDtypeStruct((b, s, D_MODEL), F32), jax.ShapeDtypeStruct((b, s, D_MODEL), BF16)],
        compiler_params=_cparams(("parallel", "parallel")),
    )(zrest3, zrest3, conv_w, conv_b, bda, bdx, ba, bx, lam)


def _rnn_bwd(zrest3, h3, dh3, conv_w, conv_b, bda, bdx, ba, bx, lam):
    b, s, _ = zrest3.shape

    def body(xr_ref, h_ref, dh_ref, cw_ref, cb_ref, bda_ref, bdx_ref, ba_ref, bx_ref, lam_ref,
             dxr_ref, pv_ref, dbd_ref):
        @pl.when(pl.program_id(1) == 0)
        def _():
            pv_ref[...] = jnp.zeros_like(pv_ref)
            dbd_ref[...] = jnp.zeros_like(dbd_ref)

        xr = xr_ref[...]
        rows, (x1, x2, x3), xc, xcb, r, i, sp, a, e2, sq = _rnn_common(
            xr, cw_ref, cb_ref, bda_ref, bdx_ref, ba_ref, bx_ref, lam_ref, s)
        h = h_ref[...]
        g = dh_ref[...]
        an = jnp.where(rows < s - 1, pltpu.roll(a, s - 1, 0), 0.0)
        sh = 1
        while sh < s:
            keep = rows < s - sh
            gr = jnp.where(keep, pltpu.roll(g, s - sh, 0), 0.0)
            g = g + an * gr
            if sh * 2 < s:
                an = an * jnp.where(keep, pltpu.roll(an, s - sh, 0), 1.0)
            sh *= 2
        hp = jnp.where(rows >= 1, pltpu.roll(h, 1, 0), 0.0)
        da = g * hp
        dsq = g * (i * xc)
        di = g * (sq * xc)
        dxc = g * (sq * i)
        dlog = da * a - dsq * ((1.0 - e2) / sq)
        dr = dlog * (-RG_C * sp)
        dpr = dr * (r * (1.0 - r))
        dpi = di * (i * (1.0 - i))
        dprb = dpr.astype(BF16)
        dpib = dpi.astype(BF16)
        dxc = dxc + _dot_nt(dprb, bda_ref[...]) + _dot_nt(dpib, bdx_ref[...])

        def up(v, k):
            return jnp.where(rows < s - k, pltpu.roll(v, s - k, 0), 0.0)

        dxr = cw_ref[3:4, :] * dxc + cw_ref[2:3, :] * up(dxc, 1) + cw_ref[1:2, :] * up(dxc, 2) \
            + cw_ref[0:1, :] * up(dxc, 3)
        dxr_ref[...] = dxr.astype(BF16)

        def colsum(v):
            return jnp.sum(v, axis=0, keepdims=True)

        pv_ref[0:1, :] += colsum(dxc * x3)
        pv_ref[1:2, :] += colsum(dxc * x2)
        pv_ref[2:3, :] += colsum(dxc * x1)
        pv_ref[3:4, :] += colsum(dxc * xr)
        pv_ref[4:5, :] += colsum(dxc)
        pv_ref[5:6, :] += colsum(dpr)
        pv_ref[6:7, :] += colsum(dpi)
        pv_ref[7:8, :] += colsum(dlog * r) * (RG_C * _sigmoid(-lam_ref[...]))
        dbd_ref[0] += _dot_tn(xcb, dprb)
        dbd_ref[1] += _dot_tn(xcb, dpib)

    blk, vec, mat = _rnn_specs(s)
    hblk = pl.BlockSpec((None, s, LANES), lambda cb, i: (i, 0, cb))
    return pl.pallas_call(
        body, name="rnn_bwd", grid=(N_CBLK, b),
        in_specs=[blk(N_CBLK), hblk, hblk, vec(CONV_W), vec(1), mat, mat, vec(1), vec(1), vec(1)],
        out_specs=[hblk, pl.BlockSpec((8, LANES), lambda cb, i: (0, cb)),
                   pl.BlockSpec((None, 2, LANES, LANES), lambda cb, i: (cb, 0, 0, 0))],
        out_shape=[jax.ShapeDtypeStruct((b, s, D_MODEL), BF16), jax.ShapeDtypeStruct((8, D_MODEL), F32),
                   jax.ShapeDtypeStruct((N_CBLK, 2, LANES, LANES), F32)],
        compiler_params=_cparams(("parallel", "arbitrary")),
    )(zrest3, h3, dh3, conv_w, conv_b, bda, bdx, ba, bx, lam)


def _branch_merge(ga, gr, wa, wr, zrest):
    t = ga.shape[0]
    tm = min(512, t)
    tn = 512

    def body(ga_ref, gr_ref, wa_ref, wr_ref, mga_ref, mgr_ref, ya_ref, yr_ref, m_ref):
        ya = _dot(ga_ref[...], wa_ref[...])
        yr = _dot(gr_ref[...], wr_ref[...])
        ya_ref[...] = ya
        yr_ref[...] = yr
        m_ref[...] = (_sigmoid(mga_ref[...]) * ya + _sigmoid(mgr_ref[...]) * yr).astype(BF16)

    nj = D_MODEL // tn
    act = pl.BlockSpec((tm, D_MODEL), lambda i, j: (i, 0))
    wgt = pl.BlockSpec((D_MODEL, tn), lambda i, j: (0, j))
    out = pl.BlockSpec((tm, tn), lambda i, j: (i, j))
    return pl.pallas_call(
        body, name="branch_merge", grid=(t // tm, nj),
        in_specs=[act, act, wgt, wgt, pl.BlockSpec((tm, tn), lambda i, j: (i, 3 * nj + j)),
                  pl.BlockSpec((tm, tn), lambda i, j: (i, 4 * nj + j))],
        out_specs=[out, out, out],
        out_shape=[jax.ShapeDtypeStruct((t, D_MODEL), F32), jax.ShapeDtypeStruct((t, D_MODEL), F32),
                   jax.ShapeDtypeStruct((t, D_MODEL), BF16)],
        compiler_params=_cparams(("parallel", "parallel")),
    )(ga, gr, wa, wr, zrest, zrest)


def _out_loss(m, wout, x2, tgt2, wpost):
    t = m.shape[0]
    tm = min(256, t)

    def body(m_ref, w_ref, x_ref, t_ref, wp_ref, dy_ref, do_ref, acc_ref):
        @pl.when(pl.program_id(0) == 0)
        def _():
            acc_ref[...] = jnp.zeros_like(acc_ref)

        o = _dot(m_ref[...], w_ref[...])
        r2 = lax.rsqrt(jnp.mean(o * o, axis=-1, keepdims=True) + NORM_EPS)
        n = o * r2
        wp = wp_ref[...]
        err = (x_ref[...] + n * wp) - t_ref[...]
        dy = err * (1.0 / D_MODEL)
        dn = dy * wp
        do = r2 * (dn - n * jnp.mean(dn * n, axis=-1, keepdims=True))
        dy_ref[...] = dy
        do_ref[...] = do.astype(BF16)
        acc_ref[0:1, :] += jnp.sum(dy * n, axis=0, keepdims=True)
        acc_ref[1:2, :] += jnp.sum(err * err, axis=0, keepdims=True)

    row = pl.BlockSpec((tm, D_MODEL), lambda i: (i, 0))
    return pl.pallas_call(
        body, name="out_loss", grid=(t // tm,),
        in_specs=[row, pl.BlockSpec((D_MODEL, D_MODEL), lambda i: (0, 0)), row, row,
                  pl.BlockSpec((1, D_MODEL), lambda i: (0, 0))],
        out_specs=[row, row, pl.BlockSpec((8, D_MODEL), lambda i: (0, 0))],
        out_shape=[jax.ShapeDtypeStruct((t, D_MODEL), F32), jax.ShapeDtypeStruct((t, D_MODEL), BF16),
                   jax.ShapeDtypeStruct((8, D_MODEL), F32)],
        compiler_params=_cparams(("arbitrary",)),
    )(m, wout, x2, tgt2, wpost)


def _merge_bwd(do, wout, zrest, ya, yr):
    t = do.shape[0]
    tm = min(512, t)
    tn = 512
    nj = D_MODEL // tn

    def body(do_ref, w_ref, mga_ref, mgr_ref, ya_ref, yr_ref, dya_ref, dyr_ref, dmga_ref, dmgr_ref):
        dm = _dot_nt(do_ref[...], w_ref[...])
        sa = _sigmoid(mga_ref[...])
        sr = _sigmoid(mgr_ref[...])
        dya_ref[...] = (dm * sa).astype(BF16)
        dyr_ref[...] = (dm * sr).astype(BF16)
        dmga_ref[...] = (dm * ya_ref[...] * (sa * (1.0 - sa))).astype(BF16)
        dmgr_ref[...] = (dm * yr_ref[...] * (sr * (1.0 - sr))).astype(BF16)

    out = pl.BlockSpec((tm, tn), lambda i, j: (i, j))
    bf = jax.ShapeDtypeStruct((t, D_MODEL), BF16)
    return pl.pallas_call(
        body, name="merge_bwd", grid=(t // tm, nj),
        in_specs=[pl.BlockSpec((tm, D_MODEL), lambda i, j: (i, 0)), pl.BlockSpec((tn, D_MODEL), lambda i, j: (j, 0)),
                  pl.BlockSpec((tm, tn), lambda i, j: (i, 3 * nj + j)),
                  pl.BlockSpec((tm, tn), lambda i, j: (i, 4 * nj + j)), out, out],
        out_specs=[out, out, out, out],
        out_shape=[bf, bf, bf, bf],
        compiler_params=_cparams(("parallel", "parallel")),
    )(do, wout, zrest, zrest, ya, yr)


def _branch_bwd(dya, dyr, wa, wr, zrest, yatt, ylru):
    t = dya.shape[0]
    tm = min(512, t)
    tn = 512
    nj = D_MODEL // tn

    def body(dya_ref, dyr_ref, wa_ref, wr_ref, ga_ref, gr_ref, ya_ref, yl_ref,
             dyatt_ref, dga_ref, dyl_ref, dgr_ref):
        dga = _dot_nt(dya_ref[...], wa_ref[...])
        dgr = _dot_nt(dyr_ref[...], wr_ref[...])
        g = ga_ref[...]
        sg = _sigmoid(g)
        dyatt_ref[...] = (dga * (g * sg)).astype(BF16)
        dga_ref[...] = (dga * ya_ref[...] * (sg * (1.0 + g * (1.0 - sg)))).astype(BF16)
        g = gr_ref[...]
        sg = _sigmoid(g)
        dyl_ref[...] = dgr * (g * sg)
        dgr_ref[...] = (dgr * yl_ref[...] * (sg * (1.0 + g * (1.0 - sg)))).astype(BF16)

    act = pl.BlockSpec((tm, D_MODEL), lambda i, j: (i, 0))
    wgt = pl.BlockSpec((tn, D_MODEL), lambda i, j: (j, 0))
    out = pl.BlockSpec((tm, tn), lambda i, j: (i, j))
    bf = jax.ShapeDtypeStruct((t, D_MODEL), BF16)
    return pl.pallas_call(
        body, name="branch_bwd", grid=(t // tm, nj),
        in_specs=[act, act, wgt, wgt, pl.BlockSpec((tm, tn), lambda i, j: (i, j)),
                  pl.BlockSpec((tm, tn), lambda i, j: (i, 2 * nj + j)), out, out],
        out_specs=[out, out, out, out],
        out_shape=[bf, bf, jax.ShapeDtypeStruct((t, D_MODEL), F32), bf],
        compiler_params=_cparams(("parallel", "parallel")),
    )(dya, dyr, wa, wr, zrest, zrest, yatt, ylru)


def _dh_partial(parts, after, name):
    t = parts[0][0].shape[0]
    tm = min(256, t)
    np_ = len(parts)

    def body(*refs):
        o_ref = refs[-1]
        acc = _dot_nt(refs[0][...], refs[np_][...])
        for p in range(1, np_):
            acc = acc + _dot_nt(refs[p][...], refs[np_ + p][...])
        o_ref[...] = acc

    in_specs = [pl.BlockSpec((tm, dz.shape[1]), lambda i: (i, 0)) for dz, _ in parts]
    in_specs += [pl.BlockSpec(w.shape, lambda i: (0, 0)) for _, w in parts]
    in_specs += [pl.BlockSpec(after.shape, lambda i: (0, 0))]
    return pl.pallas_call(
        body, name=name, grid=(t // tm,),
        in_specs=in_specs,
        out_specs=pl.BlockSpec((tm, D_MODEL), lambda i: (i, 0)),
        out_shape=jax.ShapeDtypeStruct((t, D_MODEL), F32),
        compiler_params=_cparams(("parallel",), vmem_mb=48),
    )(*[dz for dz, _ in parts], *[w for _, w in parts], after)


def _dh_final(parts, acc_in, x2, dy, wpre):
    t = x2.shape[0]
    tm = min(256, t)
    np_ = len(parts)

    def body(*refs):
        acc_ref, x_ref, dy_ref, w_ref = refs[2 * np_:2 * np_ + 4]
        gx_ref, pw_ref = refs[2 * np_ + 4:]

        @pl.when(pl.program_id(0) == 0)
        def _():
            pw_ref[...] = jnp.zeros_like(pw_ref)

        dh = acc_ref[...]
        for p in range(np_):
            dh = dh + _dot_nt(refs[p][...], refs[np_ + p][...])
        x = x_ref[...]
        r = lax.rsqrt(jnp.mean(x * x, axis=-1, keepdims=True) + NORM_EPS)
        xn = x * r
        dxn = dh * w_ref[...]
        gx_ref[...] = r * (dxn - xn * jnp.mean(dxn * xn, axis=-1, keepdims=True)) + dy_ref[...]
        pw_ref[0:1, :] += jnp.sum(dh * xn, axis=0, keepdims=True)

    row = pl.BlockSpec((tm, D_MODEL), lambda i: (i, 0))
    in_specs = [pl.BlockSpec((tm, dz.shape[1]), lambda i: (i, 0)) for dz, _ in parts]
    in_specs += [pl.BlockSpec(w.shape, lambda i: (0, 0)) for _, w in parts]
    in_specs += [row, row, row, pl.BlockSpec((1, D_MODEL), lambda i: (0, 0))]
    return pl.pallas_call(
        body, name="dh_final", grid=(t // tm,),
        in_specs=in_specs,
        out_specs=[row, pl.BlockSpec((8, D_MODEL), lambda i: (0, 0))],
        out_shape=[jax.ShapeDtypeStruct((t, D_MODEL), F32), jax.ShapeDtypeStruct((8, D_MODEL), F32)],
        compiler_params=_cparams(("arbitrary",), vmem_mb=48),
    )(*[dz for dz, _ in parts], *[w for _, w in parts], acc_in, x2, dy, wpre)


def _adamw(w, g, m, v):
    m = ADAM_B1 * m + (1.0 - ADAM_B1) * g
    v = ADAM_B2 * v + (1.0 - ADAM_B2) * (g * g)
    m_hat = m / (1.0 - ADAM_B1 ** ADAM_STEP)
    v_hat = v / (1.0 - ADAM_B2 ** ADAM_STEP)
    delta = -ADAM_LR * (m_hat / (jnp.sqrt(v_hat) + ADAM_EPS) + ADAM_WD * w)
    return delta, m, v


def _reduce_adamw(own, parts, place, w, m, v, name):
    r, c = w.shape
    tr = min(128, r)

    def body(place_ref, own_ref, p_ref, w_ref, m_ref, v_ref, g_ref, d_ref, nm_ref, nv_ref):
        mine = place_ref[1]
        own_blk = own_ref[...]
        g = jnp.where(mine == 0, own_blk, p_ref[0].astype(F32))
        for j in range(1, N_CHIPS):
            g = g + jnp.where(mine == j, own_blk, p_ref[j].astype(F32))
        d, nm, nv = _adamw(w_ref[...], g, m_ref[...], v_ref[...])
        g_ref[...] = g
        d_ref[...] = d
        nm_ref[...] = nm
        nv_ref[...] = nv

    row = pl.BlockSpec((tr, c), lambda i, pr: (i, 0))
    sh = jax.ShapeDtypeStruct((r, c), F32)
    grid_spec = pltpu.PrefetchScalarGridSpec(
        num_scalar_prefetch=1, grid=(r // tr,),
        in_specs=[row, pl.BlockSpec((N_CHIPS, tr, c), lambda i, pr: (0, i, 0)), row, row, row],
        out_specs=[row, row, row, row])
    return pl.pallas_call(
        body, name=name, grid_spec=grid_spec, out_shape=[sh, sh, sh, sh],
        compiler_params=_cparams(("parallel",)),
    )(place, own, parts, w, m, v)


def _interleave_qkv(a):
    lead = a.shape[:-1]
    return a.reshape(lead + (3, HEAD_PAIRS, LANES)).swapaxes(-3, -2).reshape(lead + (3 * D_MODEL,))


def _deinterleave_qkv(a):
    lead = a.shape[:-1]
    return a.reshape(lead + (HEAD_PAIRS, 3, LANES)).swapaxes(-3, -2).reshape(lead + (3 * D_MODEL,))


def _pack_small(pre, conv_b, rg_ba, rg_bx, lam, post, loss_row, b_in, conv_w_full, rg_wa, rg_wx):
    z = jnp.zeros((1, D_MODEL), F32)
    b_used = jnp.concatenate([b_in[:, 0:3 * D_MODEL], b_in[:, 3 * D_MODEL + HEADS:IN_TOTAL]], axis=1)
    b_f = jnp.pad(b_in[:, 3 * D_MODEL:3 * D_MODEL + HEADS], ((0, 0), (0, D_MODEL - HEADS)))
    return jnp.concatenate([
        pre, conv_b, rg_ba, rg_bx, lam, post, loss_row, z,
        b_used.reshape(9, D_MODEL), b_f, conv_w_full, z, z,
        rg_wa.reshape(64, D_MODEL), rg_wx.reshape(64, D_MODEL)], axis=0)


def _unpack_small(p):
    b_used = p[8:17].reshape(1, 9 * D_MODEL)
    b_in = jnp.concatenate([b_used[:, 0:3 * D_MODEL], p[17:18, 0:HEADS], b_used[:, 3 * D_MODEL:]], axis=1)
    return dict(pre_norm_w=p[0:1], conv_b=p[1:2], rg_ba=p[2:3], rg_bx=p[3:4], rg_lambda=p[4:5],
                post_norm_w=p[5:6], loss_row=p[6:7], b_in=b_in, conv_w_full=p[18:22],
                rg_wa=p[24:88].reshape(1, 16, 64, 64), rg_wx=p[88:152].reshape(1, 16, 64, 64))


def _reduce_small(parts, w, m, v):
    def body(p_ref, w_ref, m_ref, v_ref, g_ref, d_ref, nm_ref, nv_ref):
        g = p_ref[0]
        for j in range(1, N_DEV):
            g = g + p_ref[j]
        d, nm, nv = _adamw(w_ref[...], g, m_ref[...], v_ref[...])
        g_ref[...] = g
        d_ref[...] = d
        nm_ref[...] = nm
        nv_ref[...] = nv

    sh = jax.ShapeDtypeStruct((SMALL_ROWS, D_MODEL), F32)
    return pl.pallas_call(body, name="reduce_small", out_shape=[sh, sh, sh, sh])(parts, w, m, v)


def kernel(x, pre_norm_w, w_in, b_in, conv_w, conv_b, rg_wa, rg_ba, rg_wx, rg_bx, rg_lambda, w_branch_a, w_branch_r, w_out, post_norm_w, loss_target, m_pre_norm_w, m_w_in, m_b_in, m_conv_w, m_conv_b, m_rg_wa, m_rg_ba, m_rg_wx, m_rg_bx, m_rg_lambda, m_w_branch_a, m_w_branch_r, m_w_out, m_post_norm_w, v_pre_norm_w, v_w_in, v_b_in, v_conv_w, v_conv_b, v_rg_wa, v_rg_ba, v_rg_wx, v_rg_bx, v_rg_lambda, v_w_branch_a, v_w_branch_r, v_w_out, v_post_norm_w):
    b, s, _ = x.shape
    t = b * s
    me = 4 * lax.axis_index("x") + 2 * lax.axis_index("y") + lax.axis_index("c")
    shard_rows = D_MODEL // N_DEV

    place = jnp.stack([lax.axis_index("c"), 2 * lax.axis_index("x") + lax.axis_index("y")]).astype(jnp.int32)
    w_in_all = _gather(w_in[0].astype(BF16), "gather_w_in")
    w_full = w_in_all.transpose(1, 0, 2).reshape(D_MODEL, IN_TOTAL)
    conv_terms = jnp.concatenate(_split3(conv_w[0]), axis=0)
    conv_pad = jnp.pad(conv_terms, ((0, 16 - 3 * CONV_W), (0, D_MODEL - LANES)))
    sq_stack = jnp.concatenate([w_branch_a[0].astype(BF16), w_branch_r[0].astype(BF16), w_out[0].astype(BF16),
                                conv_pad], axis=0)
    sq_all = _gather(sq_stack, "gather_w_sq")
    wa = sq_all[:, 0:shard_rows].reshape(D_MODEL, D_MODEL)
    wr = sq_all[:, shard_rows:2 * shard_rows].reshape(D_MODEL, D_MODEL)
    wo = sq_all[:, 2 * shard_rows:3 * shard_rows].reshape(D_MODEL, D_MODEL)
    conv_all = sq_all[:, 3 * shard_rows:3 * shard_rows + 3 * CONV_W, 0:LANES].astype(F32)
    conv_all = (conv_all[:, 0:CONV_W] + conv_all[:, CONV_W:2 * CONV_W]) + conv_all[:, 2 * CONV_W:3 * CONV_W]
    conv_full = conv_all.transpose(1, 0, 2).reshape(CONV_W, D_MODEL)

    w_qkv = _interleave_qkv(w_full[:, 0:3 * D_MODEL])
    w_f = jnp.pad(w_full[:, 3 * D_MODEL:3 * D_MODEL + HEADS], ((0, 0), (0, LANES - HEADS)))
    w_rest = w_full[:, 3 * D_MODEL + HEADS:IN_USED]
    b_qkv = _interleave_qkv(b_in[:, 0:3 * D_MODEL])
    b_f = jnp.pad(b_in[:, 3 * D_MODEL:3 * D_MODEL + HEADS], ((0, 0), (0, LANES - HEADS)))
    b_rest = b_in[:, 3 * D_MODEL + HEADS:IN_USED]

    def blockdiag(w):
        w2 = w.reshape(N_CBLK, 2, HEAD_DIM, HEAD_DIM)
        zz = jnp.zeros((N_CBLK, HEAD_DIM, HEAD_DIM), w.dtype)
        top = jnp.concatenate([w2[:, 0], zz], axis=2)
        bot = jnp.concatenate([zz, w2[:, 1]], axis=2)
        return jnp.concatenate([top, bot], axis=1).astype(BF16)

    bda, bdx = blockdiag(rg_wa[0]), blockdiag(rg_wx[0])

    x2 = x.reshape(t, D_MODEL)
    tgt2 = loss_target.reshape(t, D_MODEL)
    h = _prenorm(x2, pre_norm_w)
    qkv = _mm_bias(h, w_qkv, b_qkv, BF16, "inproj_qkv")
    zrest = _mm_bias(h, w_rest, b_rest, F32, "inproj_rest")
    zf = _mm_bias(h, w_f, b_f, F32, "inproj_f")
    qkv3 = qkv.reshape(b, s, 3 * D_MODEL)
    zrest3 = zrest.reshape(b, s, 5 * D_MODEL)
    zf3 = zf.reshape(b, s, LANES)
    nq = s // ATT_TILE
    cexp3, crow = _fgate_fwd(zf3)
    crow5 = crow.reshape(b, HEAD_PAIRS, 2, nq, ATT_TILE)
    yatt3, lse5, ga3 = _attn_fwd(qkv3, cexp3, crow5, zrest3)
    ylru3, gr3 = _rnn_fwd(zrest3, conv_full, conv_b, bda, bdx, rg_ba, rg_bx, rg_lambda)
    ga, gr = ga3.reshape(t, D_MODEL), gr3.reshape(t, D_MODEL)
    ya, yr, mm = _branch_merge(ga, gr, wa, wr, zrest)
    dy, do, acc_out = _out_loss(mm, wo, x2, tgt2, post_norm_w)

    dya, dyr, dz_mga, dz_mgr = _merge_bwd(do, wo, zrest, ya, yr)
    dyatt, dz_ga, dylru, dz_gr = _branch_bwd(dya, dyr, wa, wr, zrest, yatt3.reshape(t, D_MODEL),
                                             ylru3.reshape(t, D_MODEL))
    dz_xr3, pvec, dbd = _rnn_bwd(zrest3, ylru3, dylru.reshape(b, s, D_MODEL), conv_full, conv_b, bda, bdx,
                                 rg_ba, rg_bx, rg_lambda)
    dqkv3, dc3 = _attn_bwd(qkv3, dyatt.reshape(b, s, D_MODEL), yatt3, lse5, crow5, cexp3)
    dz_f = _fgate_bwd(dc3, zf3).reshape(t, LANES)
    dz_qkv = dqkv3.reshape(t, 3 * D_MODEL)
    dz_xr = dz_xr3.reshape(t, D_MODEL)

    dw_qkv, db_qkv = _mm_tn(h, dz_qkv, "dw_qkv")
    dw_f, db_f = _mm_tn(h, dz_f, "dw_f")
    dw_parts, db_parts = [], []
    for nm, dzp in (("ga", dz_ga), ("xr", dz_xr), ("gr", dz_gr), ("mga", dz_mga), ("mgr", dz_mgr)):
        dwp, dbp = _mm_tn(h, dzp, "dw_" + nm)
        dw_parts.append(dwp)
        db_parts.append(dbp[0:1])
    dw_a, _ = _mm_tn(ga, dya, "dw_a")
    dw_r, _ = _mm_tn(gr, dyr, "dw_r")
    dw_o, _ = _mm_tn(mm, do, "dw_o")

    zeros_tail = jnp.zeros((D_MODEL, IN_TOTAL - IN_USED), F32)
    dw_in_full = jnp.concatenate([_deinterleave_qkv(dw_qkv), dw_f[:, 0:HEADS]] + dw_parts + [zeros_tail], axis=1)
    dw_in_send = dw_in_full.reshape(D_MODEL, N_CHIPS, 2, W_SHARD).transpose(2, 1, 0, 3)
    by_dest = lambda a: a.reshape(N_CHIPS, 2, shard_rows, D_MODEL).transpose(1, 0, 2, 3)
    dw_sq_send = jnp.concatenate([by_dest(dw_a), by_dest(dw_r), by_dest(dw_o)], axis=2)

    sib_in, sib_sq = _swap_with_sibling([dw_in_send, dw_sq_send], "swap_dw")
    chip_in, own_in = _pair_add(dw_in_send, sib_in, place, "pair_add_in")
    chip_sq, own_sq = _pair_add(dw_sq_send, sib_sq, place, "pair_add_sq")
    sems, sent, lands, token = _exchange_chips_start([chip_in, chip_sq], "exchange_dw_start")

    wt = lambda lo: w_rest[:, lo * D_MODEL:(lo + 1) * D_MODEL]
    dh_a = _dh_partial([(dz_qkv, w_qkv), (dz_f, w_f)], token, "dh_qkv")
    grad_x2, acc_pre = _dh_final(
        [(dz_ga, wt(0)), (dz_xr, wt(1)), (dz_gr, wt(2)), (dz_mga, wt(3)), (dz_mgr, wt(4))],
        dh_a, x2, dy, pre_norm_w)
    recv_in, recv_sq = _exchange_chips_wait(sems, sent, lands, grad_x2, "exchange_dw_wait")

    db_in_full = jnp.concatenate([_deinterleave_qkv(db_qkv[0:1]), db_f[0:1, 0:HEADS]] + db_parts
                                 + [jnp.zeros((1, IN_TOTAL - IN_USED), F32)], axis=1)
    d_rg_wa = jnp.stack([dbd[:, 0, 0:HEAD_DIM, 0:HEAD_DIM], dbd[:, 0, HEAD_DIM:, HEAD_DIM:]], axis=1)
    d_rg_wx = jnp.stack([dbd[:, 1, 0:HEAD_DIM, 0:HEAD_DIM], dbd[:, 1, HEAD_DIM:, HEAD_DIM:]], axis=1)
    small_g = _pack_small(acc_pre[0:1], pvec[4:5], pvec[5:6], pvec[6:7], pvec[7:8], acc_out[0:1], acc_out[1:2],
                          db_in_full, pvec[0:4], d_rg_wa, d_rg_wx)
    small_all = _gather(small_g, "gather_small", after=recv_sq)

    g_in, d_in, nm_in, nv_in = _reduce_adamw(own_in, recv_in, place, w_in[0], m_w_in[0], v_w_in[0], "adamw_w_in")
    sq_w = jnp.concatenate([w_branch_a[0], w_branch_r[0], w_out[0]], axis=0)
    sq_m = jnp.concatenate([m_w_branch_a[0], m_w_branch_r[0], m_w_out[0]], axis=0)
    sq_v = jnp.concatenate([v_w_branch_a[0], v_w_branch_r[0], v_w_out[0]], axis=0)
    g_sq, d_sq, nm_sq, nv_sq = _reduce_adamw(own_sq, recv_sq, place, sq_w, sq_m, sq_v, "adamw_w_sq")

    def place_conv(a):
        return lax.dynamic_update_slice(jnp.zeros((CONV_W, D_MODEL), F32), a[0], (0, me * LANES))

    zrow = jnp.zeros((1, D_MODEL), F32)
    small_w = _pack_small(pre_norm_w, conv_b, rg_ba, rg_bx, rg_lambda, post_norm_w, zrow, b_in,
                          place_conv(conv_w), rg_wa[0], rg_wx[0])
    small_m = _pack_small(m_pre_norm_w, m_conv_b, m_rg_ba, m_rg_bx, m_rg_lambda, m_post_norm_w, zrow, m_b_in,
                          place_conv(m_conv_w), m_rg_wa[0], m_rg_wx[0])
    small_v = _pack_small(v_pre_norm_w, v_conv_b, v_rg_ba, v_rg_bx, v_rg_lambda, v_post_norm_w, zrow, v_b_in,
                          place_conv(v_conv_w), v_rg_wa[0], v_rg_wx[0])
    outs_small = [_unpack_small(p) for p in _reduce_small(small_all, small_w, small_m, small_v)]

    loss = (0.5 / D_MODEL) * jnp.sum(outs_small[0]["loss_row"])

    def leaf(kind, name):
        if name == "w_in":
            return (g_in, d_in, nm_in, nv_in)[kind][None]
        if name in ("w_branch_a", "w_branch_r", "w_out"):
            j = ("w_branch_a", "w_branch_r", "w_out").index(name)
            return (g_sq, d_sq, nm_sq, nv_sq)[kind][None, j * shard_rows:(j + 1) * shard_rows]
        if name == "conv_w":
            return lax.dynamic_slice(outs_small[kind]["conv_w_full"], (0, me * LANES), (CONV_W, LANES))[None]
        return outs_small[kind][name]

    names = ["pre_norm_w", "w_in", "b_in", "conv_w", "conv_b", "rg_wa", "rg_ba", "rg_wx", "rg_bx", "rg_lambda",
             "w_branch_a", "w_branch_r", "w_out", "post_norm_w"]
    out = [loss, grad_x2.reshape(b, s, D_MODEL)]
    for kind in range(4):
        out += [leaf(kind, nm) for nm in names]
    return tuple(out)
```

```python
import jax
import jax.numpy as jnp
from jax import lax
from jax.experimental import pallas as pl
from jax.experimental.pallas import tpu as pltpu

F32 = jnp.float32
BF16 = jnp.bfloat16

N_DEV = 8
D_MODEL = 1024
HEADS = 16
HEAD_DIM = 64
HEAD_PAIRS = HEADS // 2
LANES = 128
N_CBLK = D_MODEL // LANES
CONV_W = 4
RG_C = 8.0
NORM_EPS = 1e-6
MASK_VALUE = -1e30
IN_USED = 8208
IN_TOTAL = 9232
W_SHARD = IN_TOTAL // N_DEV

ADAM_LR = 0.001
ADAM_B1 = 0.9
ADAM_B2 = 0.999
ADAM_EPS = 1e-08
ADAM_WD = 0.01
ADAM_STEP = 10

ATT_TILE = 256
SCAN_TILE = 256
SMALL_ROWS = 152


def _cparams(sem=None, vmem_mb=None):
    kw = {}
    if sem is not None:
        kw["dimension_semantics"] = sem
    if vmem_mb is not None:
        kw["vmem_limit_bytes"] = vmem_mb * 1024 * 1024
    return pltpu.CompilerParams(**kw)


def _sigmoid(x):
    return 1.0 / (1.0 + jnp.exp(-x))


def _softplus(x):
    return jnp.maximum(x, 0.0) + jnp.log1p(jnp.exp(-jnp.abs(x)))


def _expm1(x):
    p = x * (1.0 + x * (1.0 / 2 + x * (1.0 / 6 + x * (1.0 / 24 + x * (1.0 / 120 + x * (
        1.0 / 720 + x * (1.0 / 5040 + x * (1.0 / 40320))))))))
    return jnp.where(jnp.abs(x) < 0.5, p, jnp.exp(x) - 1.0)


def _split3(x):
    hi = x.astype(BF16)
    r1 = x - hi.astype(F32)
    mid = r1.astype(BF16)
    lo = (r1 - mid.astype(F32)).astype(BF16)
    return hi, mid, lo


def _dot(a, b):
    return jnp.dot(a, b, preferred_element_type=F32)


def _dot_nt(a, b):
    return lax.dot_general(a, b, (((1,), (1,)), ((), ())), preferred_element_type=F32)


def _dot_tn(a, b):
    return lax.dot_general(a, b, (((0,), (0,)), ((), ())), preferred_element_type=F32)


def _iota(shape, dim):
    return lax.broadcasted_iota(jnp.int32, shape, dim)


_ANY = pl.BlockSpec(memory_space=pl.ANY)
_MESH = pl.DeviceIdType.MESH
N_CHIPS = 4


def _place():
    x, y, c = lax.axis_index("x"), lax.axis_index("y"), lax.axis_index("c")
    other_chips = [(1 - x, y), (x, 1 - y), (1 - x, 1 - y)]
    return x, y, c, other_chips


def _gather(x_shard, name):
    def body(x_ref, out_ref, send_sems, recv_sems, local_sem):
        x, y, c, chips = _place()
        me, sibling = (x, y, c), (x, y, 1 - c)

        def slot(p):
            return out_ref.at[4 * p[0] + 2 * p[1] + p[2]]

        def copy(k, block, to, src=None):
            return pltpu.make_async_remote_copy(
                src_ref=slot(block) if src is None else src, dst_ref=slot(block),
                send_sem=send_sems.at[k], recv_sem=recv_sems.at[k], device_id=to, device_id_type=_MESH)

        mine = pltpu.make_async_copy(x_ref, slot(me), local_sem)
        mine.start()
        first = [copy(0, me, sibling, src=x_ref)]
        first += [copy(1 + j, me, (*chip, c), src=x_ref) for j, chip in enumerate(chips)]
        for cp in first:
            cp.start()
        passed = [copy(4 + j, (*chip, c), sibling) for j, chip in enumerate(chips)]
        for j, chip in enumerate(chips):
            copy(1 + j, (*chip, c), me).wait_recv()
            passed[j].start()
        copy(0, sibling, me).wait_recv()
        for j, chip in enumerate(chips):
            copy(4 + j, (*chip, 1 - c), me).wait_recv()
        for cp in first + passed:
            cp.wait_send()
        mine.wait()

    return pl.pallas_call(
        body, name=name,
        out_shape=jax.ShapeDtypeStruct((N_DEV,) + tuple(x_shard.shape), x_shard.dtype),
        in_specs=[_ANY], out_specs=_ANY,
        scratch_shapes=[pltpu.SemaphoreType.DMA((7,)), pltpu.SemaphoreType.DMA((7,)), pltpu.SemaphoreType.DMA],
    )(x_shard)


def _swap_with_sibling(srcs, name):
    n = len(srcs)

    def body(*refs):
        src_refs, out_refs = refs[:n], refs[n:2 * n]
        send_sems, recv_sems = refs[2 * n:]
        x, y, c, _ = _place()
        cps = [pltpu.make_async_remote_copy(
            src_ref=src_refs[i].at[1 - c], dst_ref=out_refs[i], send_sem=send_sems.at[i], recv_sem=recv_sems.at[i],
            device_id=(x, y, 1 - c), device_id_type=_MESH) for i in range(n)]
        for cp in cps:
            cp.start()
        for cp in cps:
            cp.wait()

    return pl.pallas_call(
        body, name=name,
        out_shape=[jax.ShapeDtypeStruct(a.shape[1:], a.dtype) for a in srcs],
        in_specs=[_ANY] * n, out_specs=[_ANY] * n,
        scratch_shapes=[pltpu.SemaphoreType.DMA((n,)), pltpu.SemaphoreType.DMA((n,))],
    )(*srcs)


def _pair_add(src, recv, place, name):
    _, _, r, c = src.shape
    tr = min(128, r)

    def body(place_ref, a_ref, b_ref, q16_ref, own_ref):
        q = a_ref[...] + b_ref[...]
        q16_ref[...] = q.astype(BF16)

        @pl.when(pl.program_id(1) == place_ref[1])
        def _():
            own_ref[...] = q

    grid_spec = pltpu.PrefetchScalarGridSpec(
        num_scalar_prefetch=1, grid=(r // tr, N_CHIPS),
        in_specs=[pl.BlockSpec((None, None, tr, c), lambda i, j, pr: (pr[0], j, i, 0)),
                  pl.BlockSpec((None, tr, c), lambda i, j, pr: (j, i, 0))],
        out_specs=[pl.BlockSpec((None, tr, c), lambda i, j, pr: (j, i, 0)),
                   pl.BlockSpec((tr, c), lambda i, j, pr: (i, 0))])
    return pl.pallas_call(
        body, name=name, grid_spec=grid_spec,
        out_shape=[jax.ShapeDtypeStruct((N_CHIPS, r, c), BF16), jax.ShapeDtypeStruct((r, c), F32)],
        compiler_params=_cparams(("parallel", "arbitrary")),
    )(place, src, recv)


_HBM = pl.BlockSpec(memory_space=pltpu.HBM)
_SEM = pl.BlockSpec(memory_space=pltpu.SEMAPHORE)
_DATAFLOW = pltpu.SideEffectType.DATAFLOW_SIDE_EFFECTING


def _chip_copy(src_ref, land_ref, send_sem, recv_sem, k, chips, c, land):
    chip = chips[k]
    return pltpu.make_async_remote_copy(
        src_ref=src_ref.at[2 * chip[0] + chip[1]], dst_ref=land_ref.at[land],
        send_sem=send_sem, recv_sem=recv_sem, device_id=(*chip, c), device_id_type=_MESH)


def _exchange_chips_start(srcs, name):
    n = len(srcs)
    ncp = 3 * n

    def body(*refs):
        src_refs, land_refs = refs[:n], refs[n:2 * n]
        sems = refs[4 * n:4 * n + 2 * ncp]
        token = refs[-1]
        x, y, c, chips = _place()
        for i in range(n):
            for k in range(3):
                j = 3 * i + k
                _chip_copy(src_refs[i], land_refs[i], sems[j], sems[ncp + j], k, chips, c, 2 * x + y).start()
        token[...] = jnp.zeros_like(token)

    hbm = [pltpu.HBM(a.shape, a.dtype) for a in srcs]
    lands = [pltpu.with_memory_space_constraint(lax.empty(a.shape, a.dtype), pltpu.HBM) for a in srcs]
    res = pl.pallas_call(
        body, name=name,
        out_shape=(*hbm, *hbm, *([pltpu.SemaphoreType.DMA(())] * (2 * ncp)), jax.ShapeDtypeStruct((8, LANES), F32)),
        in_specs=[_HBM] * (2 * n),
        out_specs=(*([_HBM] * (2 * n)), *([_SEM] * (2 * ncp)), pl.BlockSpec(memory_space=pltpu.VMEM)),
        input_output_aliases={i: i for i in range(2 * n)},
        compiler_params=pltpu.CompilerParams(has_side_effects=_DATAFLOW),
    )(*[pltpu.with_memory_space_constraint(a, pltpu.HBM) for a in srcs], *lands)
    return list(res[2 * n:2 * n + 2 * ncp]), list(res[:n]), list(res[n:2 * n]), res[-1]


def _exchange_chips_wait(sems, srcs, lands, after, name):
    n = len(srcs)
    ncp = 3 * n

    def body(*refs):
        src_refs, land_refs = refs[:n], refs[n:2 * n]
        sem_refs = refs[2 * n:2 * n + 2 * ncp]
        x, y, c, chips = _place()
        for i in range(n):
            for k in range(3):
                j = 3 * i + k
                cp = _chip_copy(src_refs[i], land_refs[i], sem_refs[j], sem_refs[ncp + j], k, chips, c,
                                2 * chips[k][0] + chips[k][1])
                cp.wait_send()
                cp.wait_recv()

    hbm = [pltpu.HBM(a.shape, a.dtype) for a in srcs]
    res = pl.pallas_call(
        body, name=name, out_shape=(*hbm, *hbm),
        in_specs=[_HBM] * (2 * n) + [_SEM] * (2 * ncp) + [_ANY], out_specs=tuple([_HBM] * (2 * n)),
        input_output_aliases={i: i for i in range(2 * n)},
        compiler_params=pltpu.CompilerParams(has_side_effects=_DATAFLOW),
    )(*srcs, *lands, *sems, after)
    return list(res[n:2 * n])


def _peer_copy(src_ref, land_ref, send_sem, recv_sem, k, place, land):
    x, y, c = place
    peer = (1 - x if k & 4 else x, 1 - y if k & 2 else y, 1 - c if k & 1 else c)
    return pltpu.make_async_remote_copy(
        src_ref=src_ref, dst_ref=land_ref.at[land], send_sem=send_sem, recv_sem=recv_sem,
        device_id=peer, device_id_type=_MESH)


def _gather_start(x_shard, after, name):
    npeer = N_DEV - 1

    def body(x_ref, land_ref, after_ref, x_thru, land_thru, *rest):
        sems, token = rest[:2 * npeer], rest[-1]
        x, y, c, _ = _place()
        for k in range(1, N_DEV):
            _peer_copy(x_ref, land_ref, sems[k - 1], sems[npeer + k - 1], k, (x, y, c), 4 * x + 2 * y + c).start()
        token[...] = jnp.zeros_like(token)

    land = pltpu.with_memory_space_constraint(lax.empty((N_DEV,) + tuple(x_shard.shape), x_shard.dtype), pltpu.HBM)
    res = pl.pallas_call(
        body, name=name,
        out_shape=(pltpu.HBM(x_shard.shape, x_shard.dtype), pltpu.HBM(land.shape, land.dtype),
                   *([pltpu.SemaphoreType.DMA(())] * (2 * npeer)), jax.ShapeDtypeStruct((8, LANES), F32)),
        in_specs=[_HBM, _HBM, _ANY],
        out_specs=(_HBM, _HBM, *([_SEM] * (2 * npeer)), pl.BlockSpec(memory_space=pltpu.VMEM)),
        input_output_aliases={0: 0, 1: 1},
        compiler_params=pltpu.CompilerParams(has_side_effects=_DATAFLOW),
    )(pltpu.with_memory_space_constraint(x_shard, pltpu.HBM), land, after)
    return list(res[2:2 + 2 * npeer]), res[0], res[1], res[-1]


def _gather_wait(sems, src, land, after, name):
    npeer = N_DEV - 1

    def body(x_ref, land_ref, *rest):
        sem_refs = rest[:2 * npeer]
        x, y, c, _ = _place()
        for k in range(1, N_DEV):
            peer_index = (4 * x + 2 * y + c) ^ k
            cp = _peer_copy(x_ref, land_ref, sem_refs[k - 1], sem_refs[npeer + k - 1], k, (x, y, c), peer_index)
            cp.wait_send()
            cp.wait_recv()

    res = pl.pallas_call(
        body, name=name, out_shape=(pltpu.HBM(src.shape, src.dtype), pltpu.HBM(land.shape, land.dtype)),
        in_specs=[_HBM, _HBM] + [_SEM] * (2 * npeer) + [_ANY], out_specs=(_HBM, _HBM),
        input_output_aliases={0: 0, 1: 1},
        compiler_params=pltpu.CompilerParams(has_side_effects=_DATAFLOW),
    )(src, land, *sems, after)
    return res[1]


def _prenorm(x2, w):
    t = x2.shape[0]
    tm = min(512, t)

    def body(x_ref, w_ref, h_ref):
        x = x_ref[...]
        r = lax.rsqrt(jnp.mean(x * x, axis=-1, keepdims=True) + NORM_EPS)
        h_ref[...] = (x * r * w_ref[...]).astype(BF16)

    return pl.pallas_call(
        body, name="prenorm", grid=(t // tm,),
        in_specs=[pl.BlockSpec((tm, D_MODEL), lambda i: (i, 0)), pl.BlockSpec((1, D_MODEL), lambda i: (0, 0))],
        out_specs=pl.BlockSpec((tm, D_MODEL), lambda i: (i, 0)),
        out_shape=jax.ShapeDtypeStruct((t, D_MODEL), BF16),
        compiler_params=_cparams(("parallel",)),
    )(x2, w)


def _mm_bias(a, b, bias, out_dtype, name):
    m, k = a.shape
    n = b.shape[1]
    tm = min(512, m)
    tn = min(1024, n)

    def body(a_ref, b_ref, bias_ref, o_ref):
        o_ref[...] = (_dot(a_ref[...], b_ref[...]) + bias_ref[...]).astype(o_ref.dtype)

    return pl.pallas_call(
        body, name=name, grid=(n // tn, m // tm),
        in_specs=[pl.BlockSpec((tm, k), lambda j, i: (i, 0)), pl.BlockSpec((k, tn), lambda j, i: (0, j)),
                  pl.BlockSpec((1, tn), lambda j, i: (0, j))],
        out_specs=pl.BlockSpec((tm, tn), lambda j, i: (i, j)),
        out_shape=jax.ShapeDtypeStruct((m, n), out_dtype),
        compiler_params=_cparams(("parallel", "parallel")),
    )(a, b, bias)


def _mm_tn(a, b, name):
    t, m = a.shape
    n = b.shape[1]
    tn = min(1024, n)
    tk = min(512, t)

    def body(a_ref, b_ref, o_ref, s_ref):
        kk = pl.program_id(1)

        @pl.when(kk == 0)
        def _():
            o_ref[...] = jnp.zeros_like(o_ref)
            s_ref[...] = jnp.zeros_like(s_ref)

        bb = b_ref[...]
        o_ref[...] += _dot_tn(a_ref[...], bb)
        s_ref[0:1, :] += jnp.sum(bb.astype(F32), axis=0, keepdims=True)

    return pl.pallas_call(
        body, name=name, grid=(n // tn, t // tk),
        in_specs=[pl.BlockSpec((tk, m), lambda j, kk: (kk, 0)), pl.BlockSpec((tk, tn), lambda j, kk: (kk, j))],
        out_specs=[pl.BlockSpec((m, tn), lambda j, kk: (0, j)), pl.BlockSpec((8, tn), lambda j, kk: (0, j))],
        out_shape=[jax.ShapeDtypeStruct((m, n), F32), jax.ShapeDtypeStruct((8, n), F32)],
        compiler_params=_cparams(("parallel", "arbitrary")),
    )(a, b)


def _fgate_fwd(zf3):
    b, s, _ = zf3.shape
    tb = SCAN_TILE
    nb = s // tb

    def body(z_ref, cexp_ref, crow_ref):
        tri = (_iota((tb, tb), 1) <= _iota((tb, tb), 0)).astype(BF16)
        expand = ((_iota((LANES, D_MODEL), 1) >> 6) == _iota((LANES, D_MODEL), 0)).astype(BF16)
        carry = jnp.zeros((1, LANES), F32)
        for i in range(nb):
            rows = slice(i * tb, (i + 1) * tb)
            z = z_ref[rows, :]
            lf = jnp.minimum(z, 0.0) - jnp.log1p(jnp.exp(-jnp.abs(z)))
            cb = sum(_dot(tri, part) for part in _split3(lf)) + carry
            carry = cb[tb - 1:tb, :]
            cexp_ref[rows, :] = sum(_dot(part, expand) for part in _split3(cb))
            crow_ref[:, rows] = cb.T[0:HEADS, :]

    return pl.pallas_call(
        body, name="fgate_fwd", grid=(b,),
        in_specs=[pl.BlockSpec((None, s, LANES), lambda i: (i, 0, 0))],
        out_specs=[pl.BlockSpec((None, s, D_MODEL), lambda i: (i, 0, 0)),
                   pl.BlockSpec((None, HEADS, s), lambda i: (i, 0, 0))],
        out_shape=[jax.ShapeDtypeStruct((b, s, D_MODEL), F32), jax.ShapeDtypeStruct((b, HEADS, s), F32)],
        compiler_params=_cparams(("parallel",)),
    )(zf3)


def _fgate_bwd(dc3, zf3):
    b, s, _ = zf3.shape
    tb = SCAN_TILE
    nb = s // tb

    def body(dc_ref, z_ref, o_ref):
        tri = (_iota((tb, tb), 1) >= _iota((tb, tb), 0)).astype(BF16)
        carry = jnp.zeros((1, LANES), F32)
        for i in reversed(range(nb)):
            rows = slice(i * tb, (i + 1) * tb)
            dlf = sum(_dot(tri, part) for part in _split3(dc_ref[rows, :])) + carry
            carry = dlf[0:1, :]
            o_ref[rows, :] = (dlf * _sigmoid(-z_ref[rows, :])).astype(BF16)

    return pl.pallas_call(
        body, name="fgate_bwd", grid=(b,),
        in_specs=[pl.BlockSpec((None, s, LANES), lambda i: (i, 0, 0)),
                  pl.BlockSpec((None, s, LANES), lambda i: (i, 0, 0))],
        out_specs=pl.BlockSpec((None, s, LANES), lambda i: (i, 0, 0)),
        out_shape=jax.ShapeDtypeStruct((b, s, LANES), BF16),
        compiler_params=_cparams(("parallel",)),
    )(dc3, zf3)


def _spare(hh):
    return HEAD_DIM if hh == 0 else 0


def _put_cols(tile, mine, cols, first):
    lane = _iota((1, LANES), 1)
    out = jnp.where(mine, tile, jnp.zeros((), tile.dtype))
    for j, c in enumerate(cols):
        out = jnp.where(lane == first + j, c, out)
    return out


def _put_rows(tile, mine, rows, first):
    sub = _iota((LANES, 1), 0)
    out = jnp.where(mine, tile, jnp.zeros((), tile.dtype))
    for j, r in enumerate(rows):
        out = jnp.where(sub == first + j, r, out)
    return out


def _transpose_bf16(a):
    return a.astype(F32).T.astype(BF16)


def _attn_fwd(qkv3, cexp3, crow5, zrest3):
    b, s, _ = qkv3.shape
    ta = ATT_TILE
    nq = s // ta
    hd = HEAD_DIM

    def body(qkv_ref, cq_ref, ck_ref, g_ref, y_ref, lse_ref, ga_ref, kt_scr, v_scr):
        lane = _iota((1, LANES), 1)
        sub = _iota((LANES, 1), 0)
        lane_mine = (lane < hd, lane >= hd)
        sub_mine = (sub < hd, sub >= hd)
        causal = _iota((ta, ta), 0) >= _iota((ta, ta), 1)
        one = jnp.ones((), BF16)

        for kj in range(nq):
            rows = slice(kj * ta, (kj + 1) * ta)
            kt = _transpose_bf16(qkv_ref[rows, LANES:2 * LANES])
            v = qkv_ref[rows, 2 * LANES:3 * LANES]
            for hh in range(2):
                ck = list(_split3(-ck_ref[hh, kj:kj + 1, :]))
                kt_scr[hh, kj] = _put_rows(kt, sub_mine[hh], [one, one, one] + ck, _spare(hh))
                v_scr[hh, kj] = _put_cols(v, lane_mine[hh], [one], _spare(hh))

        for qi in range(nq):
            rows = slice(qi * ta, (qi + 1) * ta)
            q = qkv_ref[rows, 0:LANES] * 0.125
            cq = cq_ref[rows, :]
            qh = [_put_cols(q, lane_mine[hh], list(_split3(cq[:, hh * hd:hh * hd + 1])) + [one, one, one], _spare(hh))
                  for hh in range(2)]
            st = [(jnp.full((ta, 1), MASK_VALUE, F32), jnp.zeros((ta, LANES), F32))] * 2
            for kj in range(qi + 1):
                for hh in range(2):
                    m, acc = st[hh]
                    sc = _dot(qh[hh], kt_scr[hh, kj])
                    if kj == qi:
                        sc = jnp.where(causal, sc, MASK_VALUE)
                    mn = jnp.maximum(m, jnp.max(sc, axis=-1, keepdims=True))
                    p = jnp.exp(sc - mn).astype(BF16)
                    st[hh] = (mn, jnp.exp(m - mn) * acc + _dot(p, v_scr[hh, kj]))
            (ma, acca), (mb, accb) = st
            la = acca[:, hd:hd + 1]
            lb = accb[:, 0:1]
            y = jnp.where(lane_mine[0], acca * (1.0 / la), accb * (1.0 / lb))
            lse = jnp.where(lane_mine[0], ma + jnp.log(la), mb + jnp.log(lb)).T
            lse_ref[0, qi:qi + 1, :] = lse[0:1, :]
            lse_ref[1, qi:qi + 1, :] = lse[hd:hd + 1, :]
            y_ref[rows, :] = y
            g = g_ref[rows, :]
            ga_ref[rows, :] = (y * (g * _sigmoid(g))).astype(BF16)

    blk = lambda w: pl.BlockSpec((None, s, w), lambda i, p: (i, 0, p))
    rows5 = pl.BlockSpec((None, None, 2, nq, ta), lambda i, p: (i, p, 0, 0, 0))
    return pl.pallas_call(
        body, name="attn_fwd", grid=(b, HEAD_PAIRS),
        in_specs=[blk(3 * LANES), blk(LANES), rows5, blk(LANES)],
        out_specs=[blk(LANES), rows5, blk(LANES)],
        out_shape=[jax.ShapeDtypeStruct((b, s, D_MODEL), F32),
                   jax.ShapeDtypeStruct((b, HEAD_PAIRS, 2, nq, ta), F32),
                   jax.ShapeDtypeStruct((b, s, D_MODEL), BF16)],
        scratch_shapes=[pltpu.VMEM((2, nq, LANES, ta), BF16), pltpu.VMEM((2, nq, ta, LANES), BF16)],
        compiler_params=_cparams(("parallel", "parallel")),
    )(qkv3, cexp3, crow5, zrest3)


def _attn_bwd(qkv3, do3, y3, lse5, crow5, cexp3):
    b, s, _ = qkv3.shape
    ta = ATT_TILE
    nq = s // ta
    hd = HEAD_DIM

    def body(qkv_ref, do_ref, y_ref, lse_ref, crow_ref, cexp_ref, dqkv_ref, dc_ref,
             qa_scr, doa_scr, qst_scr, dot_scr, kt_scr, vt_scr, dq_scr, rs_scr):
        pair = pl.program_id(1)
        lane = _iota((1, LANES), 1)
        sub = _iota((LANES, 1), 0)
        lane_mine = (lane < hd, lane >= hd)
        sub_mine = (sub < hd, sub >= hd)
        causal = _iota((ta, ta), 0) >= _iota((ta, ta), 1)
        one = jnp.ones((), BF16)
        zero = jnp.zeros((), BF16)

        @pl.when(pair == 0)
        def _():
            dc_ref[...] = jnp.zeros_like(dc_ref)

        for i in range(nq):
            rows = slice(i * ta, (i + 1) * ta)
            qs = qkv_ref[rows, 0:LANES] * 0.125
            qst = _transpose_bf16(qs)
            kt = _transpose_bf16(qkv_ref[rows, LANES:2 * LANES])
            vt = _transpose_bf16(qkv_ref[rows, 2 * LANES:3 * LANES])
            do = do_ref[rows, :]
            dof = do.astype(F32)
            dot = dof.T.astype(BF16)
            pr = y_ref[rows, :] * dof
            cq = cexp_ref[rows, :]
            lse_c = jnp.where(sub == 0, lse_ref[0, i:i + 1, :],
                              jnp.where(sub == 1, lse_ref[1, i:i + 1, :], 0.0)).T
            for hh in range(2):
                sp = _spare(hh)
                dsum = jnp.sum(jnp.where(lane_mine[hh], pr, 0.0), axis=-1, keepdims=True)
                bias = cq[:, hh * hd:hh * hd + 1] - lse_c[:, hh:hh + 1]
                qa_scr[hh, i] = _put_cols(qs, lane_mine[hh], list(_split3(bias)) + [one, one, one], sp)
                doa_scr[hh, i] = _put_cols(do, lane_mine[hh], list(_split3(-dsum)), sp)
                qst_scr[hh, i] = jnp.where(sub_mine[hh], qst, zero)
                dot_scr[hh, i] = jnp.where(sub_mine[hh], dot, zero)
                ck = list(_split3(-crow_ref[hh, i:i + 1, :]))
                kt_scr[hh, i] = _put_rows(kt, sub_mine[hh], [one, one, one] + ck, sp)
                vt_scr[hh, i] = _put_rows(vt, sub_mine[hh], [one, one, one], sp)
            dq_scr[i] = jnp.zeros((ta, LANES), F32)
            rs_scr[i] = jnp.zeros((ta, LANES), F32)

        for kj in range(nq):
            krows = slice(kj * ta, (kj + 1) * ta)
            k = qkv_ref[krows, LANES:2 * LANES]
            km = (jnp.where(lane_mine[0], k, zero), jnp.where(lane_mine[1], k, zero))
            dkt = jnp.zeros((LANES, ta), F32)
            dvt = jnp.zeros((LANES, ta), F32)
            dcp = [jnp.zeros((8, ta), F32), jnp.zeros((8, ta), F32)]
            for qi in range(kj, nq):
                dq = jnp.zeros((ta, LANES), F32)
                rs = []
                for hh in range(2):
                    sc = _dot(qa_scr[hh, qi], kt_scr[hh, kj])
                    if qi == kj:
                        sc = jnp.where(causal, sc, MASK_VALUE)
                    p = jnp.exp(sc)
                    dsf = p * _dot(doa_scr[hh, qi], vt_scr[hh, kj])
                    dcp[hh] = dcp[hh] + jnp.sum(dsf.reshape(ta // 8, 8, ta), axis=0)
                    rs.append(jnp.sum(dsf, axis=-1, keepdims=True))
                    ds = dsf.astype(BF16)
                    dq = dq + _dot(ds, km[hh])
                    dkt = dkt + _dot(qst_scr[hh, qi], ds)
                    dvt = dvt + _dot(dot_scr[hh, qi], p.astype(BF16))
                dq_scr[qi] += dq
                rs_scr[qi] += jnp.where(lane == 0, rs[0], jnp.where(lane == 1, rs[1], 0.0))
            dqkv_ref[krows, LANES:2 * LANES] = dkt.T.astype(BF16)
            dqkv_ref[krows, 2 * LANES:3 * LANES] = dvt.T.astype(BF16)
            dca = jnp.sum(dcp[0], axis=0, keepdims=True)
            dcb = jnp.sum(dcp[1], axis=0, keepdims=True)
            dcs = jnp.where(sub == 0, dca, jnp.where(sub == 1, dcb, 0.0)).T
            dc_ref[krows, :] += (jnp.where(lane == 2 * pair, -dcs[:, 0:1], 0.0)
                                 + jnp.where(lane == 2 * pair + 1, -dcs[:, 1:2], 0.0))
        for qi in range(nq):
            rows = slice(qi * ta, (qi + 1) * ta)
            dqkv_ref[rows, 0:LANES] = (dq_scr[qi] * 0.125).astype(BF16)
            rq = rs_scr[qi]
            dc_ref[rows, :] += (jnp.where(lane == 2 * pair, rq[:, 0:1], 0.0)
                                + jnp.where(lane == 2 * pair + 1, rq[:, 1:2], 0.0))

    blk = lambda w: pl.BlockSpec((None, s, w), lambda i, p: (i, 0, p))
    rows5 = pl.BlockSpec((None, None, 2, nq, ta), lambda i, p: (i, p, 0, 0, 0))
    by_rows = lambda: pltpu.VMEM((2, nq, ta, LANES), BF16)
    by_cols = lambda: pltpu.VMEM((2, nq, LANES, ta), BF16)
    return pl.pallas_call(
        body, name="attn_bwd", grid=(b, HEAD_PAIRS),
        in_specs=[blk(3 * LANES), blk(LANES), blk(LANES), rows5, rows5, blk(LANES)],
        out_specs=[blk(3 * LANES), pl.BlockSpec((None, s, LANES), lambda i, p: (i, 0, 0))],
        out_shape=[jax.ShapeDtypeStruct((b, s, 3 * D_MODEL), BF16), jax.ShapeDtypeStruct((b, s, LANES), F32)],
        scratch_shapes=[by_rows(), by_rows(), by_cols(), by_cols(), by_cols(), by_cols(),
                        pltpu.VMEM((nq, ta, LANES), F32), pltpu.VMEM((nq, ta, LANES), F32)],
        compiler_params=_cparams(("parallel", "arbitrary")),
    )(qkv3, do3, y3, lse5, crow5, cexp3)


def _rnn_common(xr, cw_ref, cb_ref, bda_ref, bdx_ref, ba_ref, bx_ref, lam_ref, s):
    rows = _iota((s, LANES), 0)

    def down(v, k):
        return jnp.where(rows >= k, pltpu.roll(v, k, 0), 0.0)

    x1, x2, x3 = down(xr, 1), down(xr, 2), down(xr, 3)
    xc = cb_ref[...] + cw_ref[0:1, :] * x3
    xc = xc + cw_ref[1:2, :] * x2
    xc = xc + cw_ref[2:3, :] * x1
    xc = xc + cw_ref[3:4, :] * xr
    xcb = xc.astype(BF16)
    r = _sigmoid(_dot(xcb, bda_ref[...]) + ba_ref[...])
    i = _sigmoid(_dot(xcb, bdx_ref[...]) + bx_ref[...])
    sp = _softplus(-lam_ref[...])
    log_a = (-RG_C * r) * sp
    a = jnp.exp(log_a)
    e2 = -_expm1(2.0 * log_a)
    sq = jnp.sqrt(jnp.maximum(e2, 0.0))
    return rows, (x1, x2, x3), xc, xcb, r, i, sp, a, e2, sq


def _rnn_specs(s):
    blk = lambda off: pl.BlockSpec((None, s, LANES), lambda cb, i: (i, 0, off + cb))
    vec = lambda r: pl.BlockSpec((r, LANES), lambda cb, i: (0, cb))
    mat = pl.BlockSpec((None, LANES, LANES), lambda cb, i: (cb, 0, 0))
    return blk, vec, mat


def _rnn_fwd(zrest3, conv_w, conv_b, bda, bdx, ba, bx, lam):
    b, s, _ = zrest3.shape

    def body(xr_ref, g_ref, cw_ref, cb_ref, bda_ref, bdx_ref, ba_ref, bx_ref, lam_ref, h_ref, gr_ref):
        xr = xr_ref[...]
        rows, _, xc, _, _, i, _, a, _, sq = _rnn_common(
            xr, cw_ref, cb_ref, bda_ref, bdx_ref, ba_ref, bx_ref, lam_ref, s)
        u = sq * (i * xc)
        sh = 1
        while sh < s:
            keep = rows >= sh
            ur = jnp.where(keep, pltpu.roll(u, sh, 0), 0.0)
            u = u + a * ur
            if sh * 2 < s:
                a = a * jnp.where(keep, pltpu.roll(a, sh, 0), 1.0)
            sh *= 2
        h_ref[...] = u
        g = g_ref[...]
        gr_ref[...] = (u * (g * _sigmoid(g))).astype(BF16)

    blk, vec, mat = _rnn_specs(s)
    return pl.pallas_call(
        body, name="rnn_fwd", grid=(N_CBLK, b),
        in_specs=[blk(N_CBLK), blk(2 * N_CBLK), vec(CONV_W), vec(1), mat, mat, vec(1), vec(1), vec(1)],
        out_specs=[blk(0), blk(0)],
        out_shape=[jax.ShapeDtypeStruct((b, s, D_MODEL), F32), jax.ShapeDtypeStruct((b, s, D_MODEL), BF16)],
        compiler_params=_cparams(("parallel", "parallel")),
    )(zrest3, zrest3, conv_w, conv_b, bda, bdx, ba, bx, lam)


def _rnn_bwd(zrest3, h3, dh3, conv_w, conv_b, bda, bdx, ba, bx, lam):
    b, s, _ = zrest3.shape

    def body(xr_ref, h_ref, dh_ref, cw_ref, cb_ref, bda_ref, bdx_ref, ba_ref, bx_ref, lam_ref,
             dxr_ref, pv_ref, dbd_ref):
        @pl.when(pl.program_id(1) == 0)
        def _():
            pv_ref[...] = jnp.zeros_like(pv_ref)
            dbd_ref[...] = jnp.zeros_like(dbd_ref)

        xr = xr_ref[...]
        rows, (x1, x2, x3), xc, xcb, r, i, sp, a, e2, sq = _rnn_common(
            xr, cw_ref, cb_ref, bda_ref, bdx_ref, ba_ref, bx_ref, lam_ref, s)
        h = h_ref[...]
        g = dh_ref[...]
        an = jnp.where(rows < s - 1, pltpu.roll(a, s - 1, 0), 0.0)
        sh = 1
        while sh < s:
            keep = rows < s - sh
            gr = jnp.where(keep, pltpu.roll(g, s - sh, 0), 0.0)
            g = g + an * gr
            if sh * 2 < s:
                an = an * jnp.where(keep, pltpu.roll(an, s - sh, 0), 1.0)
            sh *= 2
        hp = jnp.where(rows >= 1, pltpu.roll(h, 1, 0), 0.0)
        da = g * hp
        dsq = g * (i * xc)
        di = g * (sq * xc)
        dxc = g * (sq * i)
        dlog = da * a - dsq * ((1.0 - e2) / sq)
        dr = dlog * (-RG_C * sp)
        dpr = dr * (r * (1.0 - r))
        dpi = di * (i * (1.0 - i))
        dprb = dpr.astype(BF16)
        dpib = dpi.astype(BF16)
        dxc = dxc + _dot_nt(dprb, bda_ref[...]) + _dot_nt(dpib, bdx_ref[...])

        def up(v, k):
            return jnp.where(rows < s - k, pltpu.roll(v, s - k, 0), 0.0)

        dxr = cw_ref[3:4, :] * dxc + cw_ref[2:3, :] * up(dxc, 1) + cw_ref[1:2, :] * up(dxc, 2) \
            + cw_ref[0:1, :] * up(dxc, 3)
        dxr_ref[...] = dxr.astype(BF16)

        def colsum(v):
            return jnp.sum(v, axis=0, keepdims=True)

        pv_ref[0:1, :] += colsum(dxc * x3)
        pv_ref[1:2, :] += colsum(dxc * x2)
        pv_ref[2:3, :] += colsum(dxc * x1)
        pv_ref[3:4, :] += colsum(dxc * xr)
        pv_ref[4:5, :] += colsum(dxc)
        pv_ref[5:6, :] += colsum(dpr)
        pv_ref[6:7, :] += colsum(dpi)
        pv_ref[7:8, :] += colsum(dlog * r) * (RG_C * _sigmoid(-lam_ref[...]))
        dbd_ref[0] += _dot_tn(xcb, dprb)
        dbd_ref[1] += _dot_tn(xcb, dpib)

    blk, vec, mat = _rnn_specs(s)
    hblk = pl.BlockSpec((None, s, LANES), lambda cb, i: (i, 0, cb))
    return pl.pallas_call(
        body, name="rnn_bwd", grid=(N_CBLK, b),
        in_specs=[blk(N_CBLK), hblk, hblk, vec(CONV_W), vec(1), mat, mat, vec(1), vec(1), vec(1)],
        out_specs=[hblk, pl.BlockSpec((8, LANES), lambda cb, i: (0, cb)),
                   pl.BlockSpec((None, 2, LANES, LANES), lambda cb, i: (cb, 0, 0, 0))],
        out_shape=[jax.ShapeDtypeStruct((b, s, D_MODEL), BF16), jax.ShapeDtypeStruct((8, D_MODEL), F32),
                   jax.ShapeDtypeStruct((N_CBLK, 2, LANES, LANES), F32)],
        compiler_params=_cparams(("parallel", "arbitrary")),
    )(zrest3, h3, dh3, conv_w, conv_b, bda, bdx, ba, bx, lam)


def _branch_merge(ga, gr, wa, wr, zrest):
    t = ga.shape[0]
    tm = min(512, t)
    tn = 512

    def body(ga_ref, gr_ref, wa_ref, wr_ref, mga_ref, mgr_ref, ya_ref, yr_ref, m_ref):
        ya = _dot(ga_ref[...], wa_ref[...])
        yr = _dot(gr_ref[...], wr_ref[...])
        ya_ref[...] = ya
        yr_ref[...] = yr
        m_ref[...] = (_sigmoid(mga_ref[...]) * ya + _sigmoid(mgr_ref[...]) * yr).astype(BF16)

    nj = D_MODEL // tn
    act = pl.BlockSpec((tm, D_MODEL), lambda i, j: (i, 0))
    wgt = pl.BlockSpec((D_MODEL, tn), lambda i, j: (0, j))
    out = pl.BlockSpec((tm, tn), lambda i, j: (i, j))
    return pl.pallas_call(
        body, name="branch_merge", grid=(t // tm, nj),
        in_specs=[act, act, wgt, wgt, pl.BlockSpec((tm, tn), lambda i, j: (i, 3 * nj + j)),
                  pl.BlockSpec((tm, tn), lambda i, j: (i, 4 * nj + j))],
        out_specs=[out, out, out],
        out_shape=[jax.ShapeDtypeStruct((t, D_MODEL), F32), jax.ShapeDtypeStruct((t, D_MODEL), F32),
                   jax.ShapeDtypeStruct((t, D_MODEL), BF16)],
        compiler_params=_cparams(("parallel", "parallel")),
    )(ga, gr, wa, wr, zrest, zrest)


def _out_loss(m, wout, x2, tgt2, wpost):
    t = m.shape[0]
    tm = min(256, t)

    def body(m_ref, w_ref, x_ref, t_ref, wp_ref, dy_ref, do_ref, acc_ref):
        @pl.when(pl.program_id(0) == 0)
        def _():
            acc_ref[...] = jnp.zeros_like(acc_ref)

        o = _dot(m_ref[...], w_ref[...])
        r2 = lax.rsqrt(jnp.mean(o * o, axis=-1, keepdims=True) + NORM_EPS)
        n = o * r2
        wp = wp_ref[...]
        err = (x_ref[...] + n * wp) - t_ref[...]
        dy = err * (1.0 / D_MODEL)
        dn = dy * wp
        do = r2 * (dn - n * jnp.mean(dn * n, axis=-1, keepdims=True))
        dy_ref[...] = dy
        do_ref[...] = do.astype(BF16)
        acc_ref[0:1, :] += jnp.sum(dy * n, axis=0, keepdims=True)
        acc_ref[1:2, :] += jnp.sum(err * err, axis=0, keepdims=True)

    row = pl.BlockSpec((tm, D_MODEL), lambda i: (i, 0))
    return pl.pallas_call(
        body, name="out_loss", grid=(t // tm,),
        in_specs=[row, pl.BlockSpec((D_MODEL, D_MODEL), lambda i: (0, 0)), row, row,
                  pl.BlockSpec((1, D_MODEL), lambda i: (0, 0))],
        out_specs=[row, row, pl.BlockSpec((8, D_MODEL), lambda i: (0, 0))],
        out_shape=[jax.ShapeDtypeStruct((t, D_MODEL), F32), jax.ShapeDtypeStruct((t, D_MODEL), BF16),
                   jax.ShapeDtypeStruct((8, D_MODEL), F32)],
        compiler_params=_cparams(("arbitrary",)),
    )(m, wout, x2, tgt2, wpost)


def _merge_bwd(do, wout, zrest, ya, yr):
    t = do.shape[0]
    tm = min(512, t)
    tn = 512
    nj = D_MODEL // tn

    def body(do_ref, w_ref, mga_ref, mgr_ref, ya_ref, yr_ref, dya_ref, dyr_ref, dmga_ref, dmgr_ref):
        dm = _dot_nt(do_ref[...], w_ref[...])
        sa = _sigmoid(mga_ref[...])
        sr = _sigmoid(mgr_ref[...])
        dya_ref[...] = (dm * sa).astype(BF16)
        dyr_ref[...] = (dm * sr).astype(BF16)
        dmga_ref[...] = (dm * ya_ref[...] * (sa * (1.0 - sa))).astype(BF16)
        dmgr_ref[...] = (dm * yr_ref[...] * (sr * (1.0 - sr))).astype(BF16)

    out = pl.BlockSpec((tm, tn), lambda i, j: (i, j))
    bf = jax.ShapeDtypeStruct((t, D_MODEL), BF16)
    return pl.pallas_call(
        body, name="merge_bwd", grid=(t // tm, nj),
        in_specs=[pl.BlockSpec((tm, D_MODEL), lambda i, j: (i, 0)), pl.BlockSpec((tn, D_MODEL), lambda i, j: (j, 0)),
                  pl.BlockSpec((tm, tn), lambda i, j: (i, 3 * nj + j)),
                  pl.BlockSpec((tm, tn), lambda i, j: (i, 4 * nj + j)), out, out],
        out_specs=[out, out, out, out],
        out_shape=[bf, bf, bf, bf],
        compiler_params=_cparams(("parallel", "parallel")),
    )(do, wout, zrest, zrest, ya, yr)


def _branch_bwd(dya, dyr, wa, wr, zrest, yatt, ylru):
    t = dya.shape[0]
    tm = min(512, t)
    tn = 512
    nj = D_MODEL // tn

    def body(dya_ref, dyr_ref, wa_ref, wr_ref, ga_ref, gr_ref, ya_ref, yl_ref,
             dyatt_ref, dga_ref, dyl_ref, dgr_ref):
        dga = _dot_nt(dya_ref[...], wa_ref[...])
        dgr = _dot_nt(dyr_ref[...], wr_ref[...])
        g = ga_ref[...]
        sg = _sigmoid(g)
        dyatt_ref[...] = (dga * (g * sg)).astype(BF16)
        dga_ref[...] = (dga * ya_ref[...] * (sg * (1.0 + g * (1.0 - sg)))).astype(BF16)
        g = gr_ref[...]
        sg = _sigmoid(g)
        dyl_ref[...] = dgr * (g * sg)
        dgr_ref[...] = (dgr * yl_ref[...] * (sg * (1.0 + g * (1.0 - sg)))).astype(BF16)

    act = pl.BlockSpec((tm, D_MODEL), lambda i, j: (i, 0))
    wgt = pl.BlockSpec((tn, D_MODEL), lambda i, j: (j, 0))
    out = pl.BlockSpec((tm, tn), lambda i, j: (i, j))
    bf = jax.ShapeDtypeStruct((t, D_MODEL), BF16)
    return pl.pallas_call(
        body, name="branch_bwd", grid=(t // tm, nj),
        in_specs=[act, act, wgt, wgt, pl.BlockSpec((tm, tn), lambda i, j: (i, j)),
                  pl.BlockSpec((tm, tn), lambda i, j: (i, 2 * nj + j)), out, out],
        out_specs=[out, out, out, out],
        out_shape=[bf, bf, jax.ShapeDtypeStruct((t, D_MODEL), F32), bf],
        compiler_params=_cparams(("parallel", "parallel")),
    )(dya, dyr, wa, wr, zrest, zrest, yatt, ylru)


def _dh_partial(parts, after, name):
    t = parts[0][0].shape[0]
    tm = min(256, t)
    np_ = len(parts)

    def body(*refs):
        o_ref = refs[-1]
        acc = _dot_nt(refs[0][...], refs[np_][...])
        for p in range(1, np_):
            acc = acc + _dot_nt(refs[p][...], refs[np_ + p][...])
        o_ref[...] = acc

    in_specs = [pl.BlockSpec((tm, dz.shape[1]), lambda i: (i, 0)) for dz, _ in parts]
    in_specs += [pl.BlockSpec(w.shape, lambda i: (0, 0)) for _, w in parts]
    in_specs += [pl.BlockSpec(after.shape, lambda i: (0, 0))]
    return pl.pallas_call(
        body, name=name, grid=(t // tm,),
        in_specs=in_specs,
        out_specs=pl.BlockSpec((tm, D_MODEL), lambda i: (i, 0)),
        out_shape=jax.ShapeDtypeStruct((t, D_MODEL), F32),
        compiler_params=_cparams(("parallel",), vmem_mb=48),
    )(*[dz for dz, _ in parts], *[w for _, w in parts], after)


def _dh_final(parts, acc_in, x2, dy, wpre):
    t = x2.shape[0]
    tm = min(256, t)
    np_ = len(parts)

    def body(*refs):
        acc_ref, x_ref, dy_ref, w_ref = refs[2 * np_:2 * np_ + 4]
        gx_ref, pw_ref = refs[2 * np_ + 4:]

        @pl.when(pl.program_id(0) == 0)
        def _():
            pw_ref[...] = jnp.zeros_like(pw_ref)

        dh = acc_ref[...]
        for p in range(np_):
            dh = dh + _dot_nt(refs[p][...], refs[np_ + p][...])
        x = x_ref[...]
        r = lax.rsqrt(jnp.mean(x * x, axis=-1, keepdims=True) + NORM_EPS)
        xn = x * r
        dxn = dh * w_ref[...]
        gx_ref[...] = r * (dxn - xn * jnp.mean(dxn * xn, axis=-1, keepdims=True)) + dy_ref[...]
        pw_ref[0:1, :] += jnp.sum(dh * xn, axis=0, keepdims=True)

    row = pl.BlockSpec((tm, D_MODEL), lambda i: (i, 0))
    in_specs = [pl.BlockSpec((tm, dz.shape[1]), lambda i: (i, 0)) for dz, _ in parts]
    in_specs += [pl.BlockSpec(w.shape, lambda i: (0, 0)) for _, w in parts]
    in_specs += [row, row, row, pl.BlockSpec((1, D_MODEL), lambda i: (0, 0))]
    return pl.pallas_call(
        body, name="dh_final", grid=(t // tm,),
        in_specs=in_specs,
        out_specs=[row, pl.BlockSpec((8, D_MODEL), lambda i: (0, 0))],
        out_shape=[jax.ShapeDtypeStruct((t, D_MODEL), F32), jax.ShapeDtypeStruct((8, D_MODEL), F32)],
        compiler_params=_cparams(("arbitrary",), vmem_mb=48),
    )(*[dz for dz, _ in parts], *[w for _, w in parts], acc_in, x2, dy, wpre)


def _adamw(w, g, m, v):
    m = ADAM_B1 * m + (1.0 - ADAM_B1) * g
    v = ADAM_B2 * v + (1.0 - ADAM_B2) * (g * g)
    m_hat = m / (1.0 - ADAM_B1 ** ADAM_STEP)
    v_hat = v / (1.0 - ADAM_B2 ** ADAM_STEP)
    delta = -ADAM_LR * (m_hat / (jnp.sqrt(v_hat) + ADAM_EPS) + ADAM_WD * w)
    return delta, m, v


def _reduce_adamw(own, parts, place, w, m, v, name):
    r, c = w.shape
    tr = min(128, r)

    def body(place_ref, own_ref, p_ref, w_ref, m_ref, v_ref, g_ref, d_ref, nm_ref, nv_ref):
        mine = place_ref[1]
        own_blk = own_ref[...]
        g = jnp.where(mine == 0, own_blk, p_ref[0].astype(F32))
        for j in range(1, N_CHIPS):
            g = g + jnp.where(mine == j, own_blk, p_ref[j].astype(F32))
        d, nm, nv = _adamw(w_ref[...], g, m_ref[...], v_ref[...])
        g_ref[...] = g
        d_ref[...] = d
        nm_ref[...] = nm
        nv_ref[...] = nv

    row = pl.BlockSpec((tr, c), lambda i, pr: (i, 0))
    sh = jax.ShapeDtypeStruct((r, c), F32)
    grid_spec = pltpu.PrefetchScalarGridSpec(
        num_scalar_prefetch=1, grid=(r // tr,),
        in_specs=[row, pl.BlockSpec((N_CHIPS, tr, c), lambda i, pr: (0, i, 0)), row, row, row],
        out_specs=[row, row, row, row])
    return pl.pallas_call(
        body, name=name, grid_spec=grid_spec, out_shape=[sh, sh, sh, sh],
        compiler_params=_cparams(("parallel",)),
    )(place, own, parts, w, m, v)


def _interleave_qkv(a):
    lead = a.shape[:-1]
    return a.reshape(lead + (3, HEAD_PAIRS, LANES)).swapaxes(-3, -2).reshape(lead + (3 * D_MODEL,))


def _deinterleave_qkv(a):
    lead = a.shape[:-1]
    return a.reshape(lead + (HEAD_PAIRS, 3, LANES)).swapaxes(-3, -2).reshape(lead + (3 * D_MODEL,))


def _pack_small(pre, conv_b, rg_ba, rg_bx, lam, post, loss_row, b_in, conv_w_full, rg_wa, rg_wx):
    z = jnp.zeros((1, D_MODEL), F32)
    b_used = jnp.concatenate([b_in[:, 0:3 * D_MODEL], b_in[:, 3 * D_MODEL + HEADS:IN_TOTAL]], axis=1)
    b_f = jnp.pad(b_in[:, 3 * D_MODEL:3 * D_MODEL + HEADS], ((0, 0), (0, D_MODEL - HEADS)))
    return jnp.concatenate([
        pre, conv_b, rg_ba, rg_bx, lam, post, loss_row, z,
        b_used.reshape(9, D_MODEL), b_f, conv_w_full, z, z,
        rg_wa.reshape(64, D_MODEL), rg_wx.reshape(64, D_MODEL)], axis=0)


def _unpack_small(p):
    b_used = p[8:17].reshape(1, 9 * D_MODEL)
    b_in = jnp.concatenate([b_used[:, 0:3 * D_MODEL], p[17:18, 0:HEADS], b_used[:, 3 * D_MODEL:]], axis=1)
    return dict(pre_norm_w=p[0:1], conv_b=p[1:2], rg_ba=p[2:3], rg_bx=p[3:4], rg_lambda=p[4:5],
                post_norm_w=p[5:6], loss_row=p[6:7], b_in=b_in, conv_w_full=p[18:22],
                rg_wa=p[24:88].reshape(1, 16, 64, 64), rg_wx=p[88:152].reshape(1, 16, 64, 64))


def _reduce_small(parts, w, m, v):
    def body(p_ref, w_ref, m_ref, v_ref, g_ref, d_ref, nm_ref, nv_ref):
        g = p_ref[0]
        for j in range(1, N_DEV):
            g = g + p_ref[j]
        d, nm, nv = _adamw(w_ref[...], g, m_ref[...], v_ref[...])
        g_ref[...] = g
        d_ref[...] = d
        nm_ref[...] = nm
        nv_ref[...] = nv

    sh = jax.ShapeDtypeStruct((SMALL_ROWS, D_MODEL), F32)
    return pl.pallas_call(body, name="reduce_small", out_shape=[sh, sh, sh, sh])(parts, w, m, v)


def kernel(x, pre_norm_w, w_in, b_in, conv_w, conv_b, rg_wa, rg_ba, rg_wx, rg_bx, rg_lambda, w_branch_a, w_branch_r, w_out, post_norm_w, loss_target, m_pre_norm_w, m_w_in, m_b_in, m_conv_w, m_conv_b, m_rg_wa, m_rg_ba, m_rg_wx, m_rg_bx, m_rg_lambda, m_w_branch_a, m_w_branch_r, m_w_out, m_post_norm_w, v_pre_norm_w, v_w_in, v_b_in, v_conv_w, v_conv_b, v_rg_wa, v_rg_ba, v_rg_wx, v_rg_bx, v_rg_lambda, v_w_branch_a, v_w_branch_r, v_w_out, v_post_norm_w):
    b, s, _ = x.shape
    t = b * s
    me = 4 * lax.axis_index("x") + 2 * lax.axis_index("y") + lax.axis_index("c")
    shard_rows = D_MODEL // N_DEV

    place = jnp.stack([lax.axis_index("c"), 2 * lax.axis_index("x") + lax.axis_index("y")]).astype(jnp.int32)
    w_in_all = _gather(w_in[0].astype(BF16), "gather_w_in")
    w_full = w_in_all.transpose(1, 0, 2).reshape(D_MODEL, IN_TOTAL)
    conv_terms = jnp.concatenate(_split3(conv_w[0]), axis=0)
    conv_pad = jnp.pad(conv_terms, ((0, 16 - 3 * CONV_W), (0, D_MODEL - LANES)))
    sq_stack = jnp.concatenate([w_branch_a[0].astype(BF16), w_branch_r[0].astype(BF16), w_out[0].astype(BF16),
                                conv_pad], axis=0)
    sq_sems, sq_src, sq_land, sq_token = _gather_start(sq_stack, w_in_all, "gather_w_sq_start")

    w_qkv = _interleave_qkv(w_full[:, 0:3 * D_MODEL])
    w_f = jnp.pad(w_full[:, 3 * D_MODEL:3 * D_MODEL + HEADS], ((0, 0), (0, LANES - HEADS)))
    w_rest = w_full[:, 3 * D_MODEL + HEADS:IN_USED]
    b_qkv = _interleave_qkv(b_in[:, 0:3 * D_MODEL]) + sq_token[0, 0]
    b_f = jnp.pad(b_in[:, 3 * D_MODEL:3 * D_MODEL + HEADS], ((0, 0), (0, LANES - HEADS)))
    b_rest = b_in[:, 3 * D_MODEL + HEADS:IN_USED]

    def blockdiag(w):
        w2 = w.reshape(N_CBLK, 2, HEAD_DIM, HEAD_DIM)
        zz = jnp.zeros((N_CBLK, HEAD_DIM, HEAD_DIM), w.dtype)
        top = jnp.concatenate([w2[:, 0], zz], axis=2)
        bot = jnp.concatenate([zz, w2[:, 1]], axis=2)
        return jnp.concatenate([top, bot], axis=1).astype(BF16)

    bda, bdx = blockdiag(rg_wa[0]), blockdiag(rg_wx[0])

    x2 = x.reshape(t, D_MODEL)
    tgt2 = loss_target.reshape(t, D_MODEL)
    h = _prenorm(x2, pre_norm_w)
    qkv = _mm_bias(h, w_qkv, b_qkv, BF16, "inproj_qkv")
    zrest = _mm_bias(h, w_rest, b_rest, F32, "inproj_rest")
    zf = _mm_bias(h, w_f, b_f, F32, "inproj_f")
    qkv3 = qkv.reshape(b, s, 3 * D_MODEL)
    zrest3 = zrest.reshape(b, s, 5 * D_MODEL)
    zf3 = zf.reshape(b, s, LANES)
    nq = s // ATT_TILE
    cexp3, crow = _fgate_fwd(zf3)
    crow5 = crow.reshape(b, HEAD_PAIRS, 2, nq, ATT_TILE)
    yatt3, lse5, ga3 = _attn_fwd(qkv3, cexp3, crow5, zrest3)

    sq_all = _gather_wait(sq_sems, sq_src, sq_land, ga3, "gather_w_sq_wait")
    sq_all = lax.dynamic_update_slice(sq_all, sq_stack[None], (me, 0, 0))
    wa = sq_all[:, 0:shard_rows].reshape(D_MODEL, D_MODEL)
    wr = sq_all[:, shard_rows:2 * shard_rows].reshape(D_MODEL, D_MODEL)
    wo = sq_all[:, 2 * shard_rows:3 * shard_rows].reshape(D_MODEL, D_MODEL)
    conv_all = sq_all[:, 3 * shard_rows:3 * shard_rows + 3 * CONV_W, 0:LANES].astype(F32)
    conv_all = (conv_all[:, 0:CONV_W] + conv_all[:, CONV_W:2 * CONV_W]) + conv_all[:, 2 * CONV_W:3 * CONV_W]
    conv_full = conv_all.transpose(1, 0, 2).reshape(CONV_W, D_MODEL)

    ylru3, gr3 = _rnn_fwd(zrest3, conv_full, conv_b, bda, bdx, rg_ba, rg_bx, rg_lambda)
    ga, gr = ga3.reshape(t, D_MODEL), gr3.reshape(t, D_MODEL)
    ya, yr, mm = _branch_merge(ga, gr, wa, wr, zrest)
    dy, do, acc_out = _out_loss(mm, wo, x2, tgt2, post_norm_w)

    dya, dyr, dz_mga, dz_mgr = _merge_bwd(do, wo, zrest, ya, yr)
    dyatt, dz_ga, dylru, dz_gr = _branch_bwd(dya, dyr, wa, wr, zrest, yatt3.reshape(t, D_MODEL),
                                             ylru3.reshape(t, D_MODEL))
    dz_xr3, pvec, dbd = _rnn_bwd(zrest3, ylru3, dylru.reshape(b, s, D_MODEL), conv_full, conv_b, bda, bdx,
                                 rg_ba, rg_bx, rg_lambda)
    dqkv3, dc3 = _attn_bwd(qkv3, dyatt.reshape(b, s, D_MODEL), yatt3, lse5, crow5, cexp3)
    dz_f = _fgate_bwd(dc3, zf3).reshape(t, LANES)
    dz_qkv = dqkv3.reshape(t, 3 * D_MODEL)
    dz_xr = dz_xr3.reshape(t, D_MODEL)

    dw_qkv, db_qkv = _mm_tn(h, dz_qkv, "dw_qkv")
    dw_f, db_f = _mm_tn(h, dz_f, "dw_f")
    dw_parts, db_parts = [], []
    for nm, dzp in (("ga", dz_ga), ("xr", dz_xr), ("gr", dz_gr), ("mga", dz_mga), ("mgr", dz_mgr)):
        dwp, dbp = _mm_tn(h, dzp, "dw_" + nm)
        dw_parts.append(dwp)
        db_parts.append(dbp[0:1])
    dw_a, _ = _mm_tn(ga, dya, "dw_a")
    dw_r, _ = _mm_tn(gr, dyr, "dw_r")
    dw_o, _ = _mm_tn(mm, do, "dw_o")

    zeros_tail = jnp.zeros((D_MODEL, IN_TOTAL - IN_USED), F32)
    dw_in_full = jnp.concatenate([_deinterleave_qkv(dw_qkv), dw_f[:, 0:HEADS]] + dw_parts + [zeros_tail], axis=1)
    dw_in_send = dw_in_full.reshape(D_MODEL, N_CHIPS, 2, W_SHARD).transpose(2, 1, 0, 3)
    by_dest = lambda a: a.reshape(N_CHIPS, 2, shard_rows, D_MODEL).transpose(1, 0, 2, 3)
    dw_sq_send = jnp.concatenate([by_dest(dw_a), by_dest(dw_r), by_dest(dw_o)], axis=2)

    sib_in, sib_sq = _swap_with_sibling([dw_in_send, dw_sq_send], "swap_dw")
    chip_in, own_in = _pair_add(dw_in_send, sib_in, place, "pair_add_in")
    chip_sq, own_sq = _pair_add(dw_sq_send, sib_sq, place, "pair_add_sq")
    sems, sent, lands, token = _exchange_chips_start([chip_in, chip_sq], "exchange_dw_start")

    wt = lambda lo: w_rest[:, lo * D_MODEL:(lo + 1) * D_MODEL]
    dh_a = _dh_partial([(dz_qkv, w_qkv), (dz_f, w_f)], token, "dh_qkv")
    grad_x2, acc_pre = _dh_final(
        [(dz_ga, wt(0)), (dz_xr, wt(1)), (dz_gr, wt(2)), (dz_mga, wt(3)), (dz_mgr, wt(4))],
        dh_a, x2, dy, pre_norm_w)

    db_in_full = jnp.concatenate([_deinterleave_qkv(db_qkv[0:1]), db_f[0:1, 0:HEADS]] + db_parts
                                 + [jnp.zeros((1, IN_TOTAL - IN_USED), F32)], axis=1)
    d_rg_wa = jnp.stack([dbd[:, 0, 0:HEAD_DIM, 0:HEAD_DIM], dbd[:, 0, HEAD_DIM:, HEAD_DIM:]], axis=1)
    d_rg_wx = jnp.stack([dbd[:, 1, 0:HEAD_DIM, 0:HEAD_DIM], dbd[:, 1, HEAD_DIM:, HEAD_DIM:]], axis=1)
    small_g = _pack_small(acc_pre[0:1], pvec[4:5], pvec[5:6], pvec[6:7], pvec[7:8], acc_out[0:1], acc_out[1:2],
                          db_in_full, pvec[0:4], d_rg_wa, d_rg_wx)
    sm_sems, sm_src, sm_land, sm_token = _gather_start(small_g, grad_x2, "gather_small_start")
    recv_in, recv_sq = _exchange_chips_wait(sems, sent, lands, sm_token, "exchange_dw_wait")

    g_in, d_in, nm_in, nv_in = _reduce_adamw(own_in, recv_in, place, w_in[0], m_w_in[0], v_w_in[0], "adamw_w_in")
    sq_w = jnp.concatenate([w_branch_a[0], w_branch_r[0], w_out[0]], axis=0)
    sq_m = jnp.concatenate([m_w_branch_a[0], m_w_branch_r[0], m_w_out[0]], axis=0)
    sq_v = jnp.concatenate([v_w_branch_a[0], v_w_branch_r[0], v_w_out[0]], axis=0)
    g_sq, d_sq, nm_sq, nv_sq = _reduce_adamw(own_sq, recv_sq, place, sq_w, sq_m, sq_v, "adamw_w_sq")
    small_all = _gather_wait(sm_sems, sm_src, sm_land, d_sq, "gather_small_wait")
    small_all = lax.dynamic_update_slice(small_all, small_g[None], (me, 0, 0))

    def place_conv(a):
        return lax.dynamic_update_slice(jnp.zeros((CONV_W, D_MODEL), F32), a[0], (0, me * LANES))

    zrow = jnp.zeros((1, D_MODEL), F32)
    small_w = _pack_small(pre_norm_w, conv_b, rg_ba, rg_bx, rg_lambda, post_norm_w, zrow, b_in,
                          place_conv(conv_w), rg_wa[0], rg_wx[0])
    small_m = _pack_small(m_pre_norm_w, m_conv_b, m_rg_ba, m_rg_bx, m_rg_lambda, m_post_norm_w, zrow, m_b_in,
                          place_conv(m_conv_w), m_rg_wa[0], m_rg_wx[0])
    small_v = _pack_small(v_pre_norm_w, v_conv_b, v_rg_ba, v_rg_bx, v_rg_lambda, v_post_norm_w, zrow, v_b_in,
                          place_conv(v_conv_w), v_rg_wa[0], v_rg_wx[0])
    outs_small = [_unpack_small(p) for p in _reduce_small(small_all, small_w, small_m, small_v)]

    loss = (0.5 / D_MODEL) * jnp.sum(outs_small[0]["loss_row"])

    def leaf(kind, name):
        if name == "w_in":
            return (g_in, d_in, nm_in, nv_in)[kind][None]
        if name in ("w_branch_a", "w_branch_r", "w_out"):
            j = ("w_branch_a", "w_branch_r", "w_out").index(name)
            return (g_sq, d_sq, nm_sq, nv_sq)[kind][None, j * shard_rows:(j + 1) * shard_rows]
        if name == "conv_w":
            return lax.dynamic_slice(outs_small[kind]["conv_w_full"], (0, me * LANES), (CONV_W, LANES))[None]
        return outs_small[kind][name]

    names = ["pre_norm_w", "w_in", "b_in", "conv_w", "conv_b", "rg_wa", "rg_ba", "rg_wx", "rg_bx", "rg_lambda",
             "w_branch_a", "w_branch_r", "w_out", "post_norm_w"]
    out = [loss, grad_x2.reshape(b, s, D_MODEL)]
    for kind in range(4):
        out += [leaf(kind, nm) for nm in names]
    return tuple(out)
```

```python
import jax
import jax.numpy as jnp
from jax import lax
from jax.experimental import pallas as pl
from jax.experimental.pallas import tpu as pltpu

F32 = jnp.float32
BF16 = jnp.bfloat16

N_DEV = 8
D_MODEL = 1024
HEADS = 16
HEAD_DIM = 64
HEAD_PAIRS = HEADS // 2
LANES = 128
N_CBLK = D_MODEL // LANES
CONV_W = 4
RG_C = 8.0
NORM_EPS = 1e-6
MASK_VALUE = -1e30
IN_USED = 8208
IN_TOTAL = 9232
W_SHARD = IN_TOTAL // N_DEV

ADAM_LR = 0.001
ADAM_B1 = 0.9
ADAM_B2 = 0.999
ADAM_EPS = 1e-08
ADAM_WD = 0.01
ADAM_STEP = 10

ATT_TILE = 256
SCAN_TILE = 256
SMALL_ROWS = 152


def _cparams(sem=None, vmem_mb=None):
    kw = {}
    if sem is not None:
        kw["dimension_semantics"] = sem
    if vmem_mb is not None:
        kw["vmem_limit_bytes"] = vmem_mb * 1024 * 1024
    return pltpu.CompilerParams(**kw)


def _sigmoid(x):
    return 1.0 / (1.0 + jnp.exp(-x))


def _softplus(x):
    return jnp.maximum(x, 0.0) + jnp.log1p(jnp.exp(-jnp.abs(x)))


def _expm1(x):
    p = x * (1.0 + x * (1.0 / 2 + x * (1.0 / 6 + x * (1.0 / 24 + x * (1.0 / 120 + x * (
        1.0 / 720 + x * (1.0 / 5040 + x * (1.0 / 40320))))))))
    return jnp.where(jnp.abs(x) < 0.5, p, jnp.exp(x) - 1.0)


def _split3(x):
    hi = x.astype(BF16)
    r1 = x - hi.astype(F32)
    mid = r1.astype(BF16)
    lo = (r1 - mid.astype(F32)).astype(BF16)
    return hi, mid, lo


def _dot(a, b):
    return jnp.dot(a, b, preferred_element_type=F32)


def _dot_nt(a, b):
    return lax.dot_general(a, b, (((1,), (1,)), ((), ())), preferred_element_type=F32)


def _dot_tn(a, b):
    return lax.dot_general(a, b, (((0,), (0,)), ((), ())), preferred_element_type=F32)


def _iota(shape, dim):
    return lax.broadcasted_iota(jnp.int32, shape, dim)


_ANY = pl.BlockSpec(memory_space=pl.ANY)
_MESH = pl.DeviceIdType.MESH
N_CHIPS = 4


def _place():
    x, y, c = lax.axis_index("x"), lax.axis_index("y"), lax.axis_index("c")
    other_chips = [(1 - x, y), (x, 1 - y), (1 - x, 1 - y)]
    return x, y, c, other_chips


def _gather(x_shard, name):
    def body(x_ref, out_ref, send_sems, recv_sems, local_sem):
        x, y, c, chips = _place()
        me, sibling = (x, y, c), (x, y, 1 - c)

        def slot(p):
            return out_ref.at[4 * p[0] + 2 * p[1] + p[2]]

        def copy(k, block, to, src=None):
            return pltpu.make_async_remote_copy(
                src_ref=slot(block) if src is None else src, dst_ref=slot(block),
                send_sem=send_sems.at[k], recv_sem=recv_sems.at[k], device_id=to, device_id_type=_MESH)

        mine = pltpu.make_async_copy(x_ref, slot(me), local_sem)
        mine.start()
        first = [copy(0, me, sibling, src=x_ref)]
        first += [copy(1 + j, me, (*chip, c), src=x_ref) for j, chip in enumerate(chips)]
        for cp in first:
            cp.start()
        passed = [copy(4 + j, (*chip, c), sibling) for j, chip in enumerate(chips)]
        for j, chip in enumerate(chips):
            copy(1 + j, (*chip, c), me).wait_recv()
            passed[j].start()
        copy(0, sibling, me).wait_recv()
        for j, chip in enumerate(chips):
            copy(4 + j, (*chip, 1 - c), me).wait_recv()
        for cp in first + passed:
            cp.wait_send()
        mine.wait()

    return pl.pallas_call(
        body, name=name,
        out_shape=jax.ShapeDtypeStruct((N_DEV,) + tuple(x_shard.shape), x_shard.dtype),
        in_specs=[_ANY], out_specs=_ANY,
        scratch_shapes=[pltpu.SemaphoreType.DMA((7,)), pltpu.SemaphoreType.DMA((7,)), pltpu.SemaphoreType.DMA],
    )(x_shard)


def _swap_with_sibling(srcs, name):
    n = len(srcs)

    def body(*refs):
        src_refs, out_refs = refs[:n], refs[n:2 * n]
        send_sems, recv_sems = refs[2 * n:]
        x, y, c, _ = _place()
        cps = [pltpu.make_async_remote_copy(
            src_ref=src_refs[i].at[1 - c], dst_ref=out_refs[i], send_sem=send_sems.at[i], recv_sem=recv_sems.at[i],
            device_id=(x, y, 1 - c), device_id_type=_MESH) for i in range(n)]
        for cp in cps:
            cp.start()
        for cp in cps:
            cp.wait()

    return pl.pallas_call(
        body, name=name,
        out_shape=[jax.ShapeDtypeStruct(a.shape[1:], a.dtype) for a in srcs],
        in_specs=[_ANY] * n, out_specs=[_ANY] * n,
        scratch_shapes=[pltpu.SemaphoreType.DMA((n,)), pltpu.SemaphoreType.DMA((n,))],
    )(*srcs)


def _pair_add(src, recv, place, name):
    _, _, r, c = src.shape
    tr = min(128, r)

    def body(place_ref, a_ref, b_ref, q16_ref, own_ref):
        q = a_ref[...] + b_ref[...]
        q16_ref[...] = q.astype(BF16)

        @pl.when(pl.program_id(1) == place_ref[1])
        def _():
            own_ref[...] = q

    grid_spec = pltpu.PrefetchScalarGridSpec(
        num_scalar_prefetch=1, grid=(r // tr, N_CHIPS),
        in_specs=[pl.BlockSpec((None, None, tr, c), lambda i, j, pr: (pr[0], j, i, 0)),
                  pl.BlockSpec((None, tr, c), lambda i, j, pr: (j, i, 0))],
        out_specs=[pl.BlockSpec((None, tr, c), lambda i, j, pr: (j, i, 0)),
                   pl.BlockSpec((tr, c), lambda i, j, pr: (i, 0))])
    return pl.pallas_call(
        body, name=name, grid_spec=grid_spec,
        out_shape=[jax.ShapeDtypeStruct((N_CHIPS, r, c), BF16), jax.ShapeDtypeStruct((r, c), F32)],
        compiler_params=_cparams(("parallel", "arbitrary")),
    )(place, src, recv)


_HBM = pl.BlockSpec(memory_space=pltpu.HBM)
_SEM = pl.BlockSpec(memory_space=pltpu.SEMAPHORE)
_DATAFLOW = pltpu.SideEffectType.DATAFLOW_SIDE_EFFECTING


def _chip_copy(src_ref, land_ref, send_sem, recv_sem, k, chips, c, land):
    chip = chips[k]
    return pltpu.make_async_remote_copy(
        src_ref=src_ref.at[2 * chip[0] + chip[1]], dst_ref=land_ref.at[land],
        send_sem=send_sem, recv_sem=recv_sem, device_id=(*chip, c), device_id_type=_MESH)


def _exchange_chips_start(srcs, name):
    n = len(srcs)
    ncp = 3 * n

    def body(*refs):
        src_refs, land_refs = refs[:n], refs[n:2 * n]
        sems = refs[4 * n:4 * n + 2 * ncp]
        token = refs[-1]
        x, y, c, chips = _place()
        for i in range(n):
            for k in range(3):
                j = 3 * i + k
                _chip_copy(src_refs[i], land_refs[i], sems[j], sems[ncp + j], k, chips, c, 2 * x + y).start()
        token[...] = jnp.zeros_like(token)

    hbm = [pltpu.HBM(a.shape, a.dtype) for a in srcs]
    lands = [pltpu.with_memory_space_constraint(lax.empty(a.shape, a.dtype), pltpu.HBM) for a in srcs]
    res = pl.pallas_call(
        body, name=name,
        out_shape=(*hbm, *hbm, *([pltpu.SemaphoreType.DMA(())] * (2 * ncp)), jax.ShapeDtypeStruct((8, LANES), F32)),
        in_specs=[_HBM] * (2 * n),
        out_specs=(*([_HBM] * (2 * n)), *([_SEM] * (2 * ncp)), pl.BlockSpec(memory_space=pltpu.VMEM)),
        input_output_aliases={i: i for i in range(2 * n)},
        compiler_params=pltpu.CompilerParams(has_side_effects=_DATAFLOW),
    )(*[pltpu.with_memory_space_constraint(a, pltpu.HBM) for a in srcs], *lands)
    return list(res[2 * n:2 * n + 2 * ncp]), list(res[:n]), list(res[n:2 * n]), res[-1]


def _exchange_chips_wait(sems, srcs, lands, after, name):
    n = len(srcs)
    ncp = 3 * n

    def body(*refs):
        src_refs, land_refs = refs[:n], refs[n:2 * n]
        sem_refs = refs[2 * n:2 * n + 2 * ncp]
        x, y, c, chips = _place()
        for i in range(n):
            for k in range(3):
                j = 3 * i + k
                cp = _chip_copy(src_refs[i], land_refs[i], sem_refs[j], sem_refs[ncp + j], k, chips, c,
                                2 * chips[k][0] + chips[k][1])
                cp.wait_send()
                cp.wait_recv()

    hbm = [pltpu.HBM(a.shape, a.dtype) for a in srcs]
    res = pl.pallas_call(
        body, name=name, out_shape=(*hbm, *hbm),
        in_specs=[_HBM] * (2 * n) + [_SEM] * (2 * ncp) + [_ANY], out_specs=tuple([_HBM] * (2 * n)),
        input_output_aliases={i: i for i in range(2 * n)},
        compiler_params=pltpu.CompilerParams(has_side_effects=_DATAFLOW),
    )(*srcs, *lands, *sems, after)
    return list(res[n:2 * n])


def _peer_copy(src_ref, land_ref, send_sem, recv_sem, k, place, land):
    x, y, c = place
    peer = (1 - x if k & 4 else x, 1 - y if k & 2 else y, 1 - c if k & 1 else c)
    return pltpu.make_async_remote_copy(
        src_ref=src_ref, dst_ref=land_ref.at[land], send_sem=send_sem, recv_sem=recv_sem,
        device_id=peer, device_id_type=_MESH)


def _gather_start(x_shard, after, name):
    npeer = N_DEV - 1

    def body(x_ref, land_ref, after_ref, x_thru, land_thru, *rest):
        sems, token = rest[:2 * npeer], rest[-1]
        x, y, c, _ = _place()
        for k in range(1, N_DEV):
            _peer_copy(x_ref, land_ref, sems[k - 1], sems[npeer + k - 1], k, (x, y, c), 4 * x + 2 * y + c).start()
        token[...] = jnp.zeros_like(token)

    land = pltpu.with_memory_space_constraint(lax.empty((N_DEV,) + tuple(x_shard.shape), x_shard.dtype), pltpu.HBM)
    res = pl.pallas_call(
        body, name=name,
        out_shape=(pltpu.HBM(x_shard.shape, x_shard.dtype), pltpu.HBM(land.shape, land.dtype),
                   *([pltpu.SemaphoreType.DMA(())] * (2 * npeer)), jax.ShapeDtypeStruct((8, LANES), F32)),
        in_specs=[_HBM, _HBM, _ANY],
        out_specs=(_HBM, _HBM, *([_SEM] * (2 * npeer)), pl.BlockSpec(memory_space=pltpu.VMEM)),
        input_output_aliases={0: 0, 1: 1},
        compiler_params=pltpu.CompilerParams(has_side_effects=_DATAFLOW),
    )(pltpu.with_memory_space_constraint(x_shard, pltpu.HBM), land, after)
    return list(res[2:2 + 2 * npeer]), res[0], res[1], res[-1]


def _gather_wait(sems, src, land, after, name):
    npeer = N_DEV - 1

    def body(x_ref, land_ref, *rest):
        sem_refs = rest[:2 * npeer]
        x, y, c, _ = _place()
        for k in range(1, N_DEV):
            peer_index = (4 * x + 2 * y + c) ^ k
            cp = _peer_copy(x_ref, land_ref, sem_refs[k - 1], sem_refs[npeer + k - 1], k, (x, y, c), peer_index)
            cp.wait_send()
            cp.wait_recv()

    res = pl.pallas_call(
        body, name=name, out_shape=(pltpu.HBM(src.shape, src.dtype), pltpu.HBM(land.shape, land.dtype)),
        in_specs=[_HBM, _HBM] + [_SEM] * (2 * npeer) + [_ANY], out_specs=(_HBM, _HBM),
        input_output_aliases={0: 0, 1: 1},
        compiler_params=pltpu.CompilerParams(has_side_effects=_DATAFLOW),
    )(src, land, *sems, after)
    return res[1]


def _prenorm(x2, w):
    t = x2.shape[0]
    tm = min(512, t)

    def body(x_ref, w_ref, h_ref):
        x = x_ref[...]
        r = lax.rsqrt(jnp.mean(x * x, axis=-1, keepdims=True) + NORM_EPS)
        h_ref[...] = (x * r * w_ref[...]).astype(BF16)

    return pl.pallas_call(
        body, name="prenorm", grid=(t // tm,),
        in_specs=[pl.BlockSpec((tm, D_MODEL), lambda i: (i, 0)), pl.BlockSpec((1, D_MODEL), lambda i: (0, 0))],
        out_specs=pl.BlockSpec((tm, D_MODEL), lambda i: (i, 0)),
        out_shape=jax.ShapeDtypeStruct((t, D_MODEL), BF16),
        compiler_params=_cparams(("parallel",)),
    )(x2, w)


def _mm_bias(a, b, bias, out_dtype, name):
    m, k = a.shape
    n = b.shape[1]
    tm = min(512, m)
    tn = min(1024, n)

    def body(a_ref, b_ref, bias_ref, o_ref):
        o_ref[...] = (_dot(a_ref[...], b_ref[...]) + bias_ref[...]).astype(o_ref.dtype)

    return pl.pallas_call(
        body, name=name, grid=(n // tn, m // tm),
        in_specs=[pl.BlockSpec((tm, k), lambda j, i: (i, 0)), pl.BlockSpec((k, tn), lambda j, i: (0, j)),
                  pl.BlockSpec((1, tn), lambda j, i: (0, j))],
        out_specs=pl.BlockSpec((tm, tn), lambda j, i: (i, j)),
        out_shape=jax.ShapeDtypeStruct((m, n), out_dtype),
        compiler_params=_cparams(("parallel", "parallel")),
    )(a, b, bias)


def _mm_tn(a, b, name):
    t, m = a.shape
    n = b.shape[1]
    tn = min(1024, n)
    tk = min(512, t)

    def body(a_ref, b_ref, o_ref, s_ref):
        kk = pl.program_id(1)

        @pl.when(kk == 0)
        def _():
            o_ref[...] = jnp.zeros_like(o_ref)
            s_ref[...] = jnp.zeros_like(s_ref)

        bb = b_ref[...]
        o_ref[...] += _dot_tn(a_ref[...], bb)
        s_ref[0:1, :] += jnp.sum(bb.astype(F32), axis=0, keepdims=True)

    return pl.pallas_call(
        body, name=name, grid=(n // tn, t // tk),
        in_specs=[pl.BlockSpec((tk, m), lambda j, kk: (kk, 0)), pl.BlockSpec((tk, tn), lambda j, kk: (kk, j))],
        out_specs=[pl.BlockSpec((m, tn), lambda j, kk: (0, j)), pl.BlockSpec((8, tn), lambda j, kk: (0, j))],
        out_shape=[jax.ShapeDtypeStruct((m, n), F32), jax.ShapeDtypeStruct((8, n), F32)],
        compiler_params=_cparams(("parallel", "arbitrary")),
    )(a, b)


def _fgate_fwd(zf3):
    b, s, _ = zf3.shape
    tb = SCAN_TILE
    nb = s // tb

    def body(z_ref, cexp_ref, crow_ref):
        tri = (_iota((tb, tb), 1) <= _iota((tb, tb), 0)).astype(BF16)
        expand = ((_iota((LANES, D_MODEL), 1) >> 6) == _iota((LANES, D_MODEL), 0)).astype(BF16)
        carry = jnp.zeros((1, LANES), F32)
        for i in range(nb):
            rows = slice(i * tb, (i + 1) * tb)
            z = z_ref[rows, :]
            lf = jnp.minimum(z, 0.0) - jnp.log1p(jnp.exp(-jnp.abs(z)))
            cb = sum(_dot(tri, part) for part in _split3(lf)) + carry
            carry = cb[tb - 1:tb, :]
            cexp_ref[rows, :] = sum(_dot(part, expand) for part in _split3(cb))
            crow_ref[:, rows] = cb.T[0:HEADS, :]

    return pl.pallas_call(
        body, name="fgate_fwd", grid=(b,),
        in_specs=[pl.BlockSpec((None, s, LANES), lambda i: (i, 0, 0))],
        out_specs=[pl.BlockSpec((None, s, D_MODEL), lambda i: (i, 0, 0)),
                   pl.BlockSpec((None, HEADS, s), lambda i: (i, 0, 0))],
        out_shape=[jax.ShapeDtypeStruct((b, s, D_MODEL), F32), jax.ShapeDtypeStruct((b, HEADS, s), F32)],
        compiler_params=_cparams(("parallel",)),
    )(zf3)


def _fgate_bwd(dc3, zf3):
    b, s, _ = zf3.shape
    tb = SCAN_TILE
    nb = s // tb

    def body(dc_ref, z_ref, o_ref):
        tri = (_iota((tb, tb), 1) >= _iota((tb, tb), 0)).astype(BF16)
        carry = jnp.zeros((1, LANES), F32)
        for i in reversed(range(nb)):
            rows = slice(i * tb, (i + 1) * tb)
            dlf = sum(_dot(tri, part) for part in _split3(dc_ref[rows, :])) + carry
            carry = dlf[0:1, :]
            o_ref[rows, :] = (dlf * _sigmoid(-z_ref[rows, :])).astype(BF16)

    return pl.pallas_call(
        body, name="fgate_bwd", grid=(b,),
        in_specs=[pl.BlockSpec((None, s, LANES), lambda i: (i, 0, 0)),
                  pl.BlockSpec((None, s, LANES), lambda i: (i, 0, 0))],
        out_specs=pl.BlockSpec((None, s, LANES), lambda i: (i, 0, 0)),
        out_shape=jax.ShapeDtypeStruct((b, s, LANES), BF16),
        compiler_params=_cparams(("parallel",)),
    )(dc3, zf3)


def _spare(hh):
    return HEAD_DIM if hh == 0 else 0


def _put_cols(tile, mine, cols, first):
    lane = _iota((1, LANES), 1)
    out = jnp.where(mine, tile, jnp.zeros((), tile.dtype))
    for j, c in enumerate(cols):
        out = jnp.where(lane == first + j, c, out)
    return out


def _put_rows(tile, mine, rows, first):
    sub = _iota((LANES, 1), 0)
    out = jnp.where(mine, tile, jnp.zeros((), tile.dtype))
    for j, r in enumerate(rows):
        out = jnp.where(sub == first + j, r, out)
    return out


def _transpose_bf16(a):
    return a.astype(F32).T.astype(BF16)


def _attn_fwd(qkv3, cexp3, crow5, zrest3):
    b, s, _ = qkv3.shape
    ta = ATT_TILE
    nq = s // ta
    hd = HEAD_DIM

    def body(qkv_ref, cq_ref, ck_ref, g_ref, y_ref, lse_ref, ga_ref, kt_scr, v_scr):
        lane = _iota((1, LANES), 1)
        sub = _iota((LANES, 1), 0)
        lane_mine = (lane < hd, lane >= hd)
        sub_mine = (sub < hd, sub >= hd)
        causal = _iota((ta, ta), 0) >= _iota((ta, ta), 1)
        one = jnp.ones((), BF16)

        for kj in range(nq):
            rows = slice(kj * ta, (kj + 1) * ta)
            kt = _transpose_bf16(qkv_ref[rows, LANES:2 * LANES])
            v = qkv_ref[rows, 2 * LANES:3 * LANES]
            for hh in range(2):
                ck = list(_split3(-ck_ref[hh, kj:kj + 1, :]))
                kt_scr[hh, kj] = _put_rows(kt, sub_mine[hh], [one, one, one] + ck, _spare(hh))
                v_scr[hh, kj] = _put_cols(v, lane_mine[hh], [one], _spare(hh))

        for qi in range(nq):
            rows = slice(qi * ta, (qi + 1) * ta)
            q = qkv_ref[rows, 0:LANES] * 0.125
            cq = cq_ref[rows, :]
            qh = [_put_cols(q, lane_mine[hh], list(_split3(cq[:, hh * hd:hh * hd + 1])) + [one, one, one], _spare(hh))
                  for hh in range(2)]
            st = [(jnp.full((ta, 1), MASK_VALUE, F32), jnp.zeros((ta, LANES), F32))] * 2
            for kj in range(qi + 1):
                for hh in range(2):
                    m, acc = st[hh]
                    sc = _dot(qh[hh], kt_scr[hh, kj])
                    if kj == qi:
                        sc = jnp.where(causal, sc, MASK_VALUE)
                    mn = jnp.maximum(m, jnp.max(sc, axis=-1, keepdims=True))
                    p = jnp.exp(sc - mn).astype(BF16)
                    st[hh] = (mn, jnp.exp(m - mn) * acc + _dot(p, v_scr[hh, kj]))
            (ma, acca), (mb, accb) = st
            la = acca[:, hd:hd + 1]
            lb = accb[:, 0:1]
            y = jnp.where(lane_mine[0], acca * (1.0 / la), accb * (1.0 / lb))
            lse = jnp.where(lane_mine[0], ma + jnp.log(la), mb + jnp.log(lb)).T
            lse_ref[0, qi:qi + 1, :] = lse[0:1, :]
            lse_ref[1, qi:qi + 1, :] = lse[hd:hd + 1, :]
            y_ref[rows, :] = y
            g = g_ref[rows, :]
            ga_ref[rows, :] = (y * (g * _sigmoid(g))).astype(BF16)

    blk = lambda w: pl.BlockSpec((None, s, w), lambda i, p: (i, 0, p))
    rows5 = pl.BlockSpec((None, None, 2, nq, ta), lambda i, p: (i, p, 0, 0, 0))
    return pl.pallas_call(
        body, name="attn_fwd", grid=(b, HEAD_PAIRS),
        in_specs=[blk(3 * LANES), blk(LANES), rows5, blk(LANES)],
        out_specs=[blk(LANES), rows5, blk(LANES)],
        out_shape=[jax.ShapeDtypeStruct((b, s, D_MODEL), F32),
                   jax.ShapeDtypeStruct((b, HEAD_PAIRS, 2, nq, ta), F32),
                   jax.ShapeDtypeStruct((b, s, D_MODEL), BF16)],
        scratch_shapes=[pltpu.VMEM((2, nq, LANES, ta), BF16), pltpu.VMEM((2, nq, ta, LANES), BF16)],
        compiler_params=_cparams(("parallel", "parallel")),
    )(qkv3, cexp3, crow5, zrest3)


def _attn_bwd(qkv3, do3, y3, lse5, crow5, cexp3):
    b, s, _ = qkv3.shape
    ta = ATT_TILE
    nq = s // ta
    hd = HEAD_DIM

    def body(qkv_ref, do_ref, y_ref, lse_ref, crow_ref, cexp_ref, dqkv_ref, dc_ref,
             qa_scr, doa_scr, qst_scr, dot_scr, kt_scr, vt_scr, dq_scr, rs_scr):
        pair = pl.program_id(1)
        lane = _iota((1, LANES), 1)
        sub = _iota((LANES, 1), 0)
        lane_mine = (lane < hd, lane >= hd)
        sub_mine = (sub < hd, sub >= hd)
        causal = _iota((ta, ta), 0) >= _iota((ta, ta), 1)
        one = jnp.ones((), BF16)
        zero = jnp.zeros((), BF16)

        @pl.when(pair == 0)
        def _():
            dc_ref[...] = jnp.zeros_like(dc_ref)

        for i in range(nq):
            rows = slice(i * ta, (i + 1) * ta)
            qs = qkv_ref[rows, 0:LANES] * 0.125
            qst = _transpose_bf16(qs)
            kt = _transpose_bf16(qkv_ref[rows, LANES:2 * LANES])
            vt = _transpose_bf16(qkv_ref[rows, 2 * LANES:3 * LANES])
            do = do_ref[rows, :]
            dof = do.astype(F32)
            dot = dof.T.astype(BF16)
            pr = y_ref[rows, :] * dof
            cq = cexp_ref[rows, :]
            lse_c = jnp.where(sub == 0, lse_ref[0, i:i + 1, :],
                              jnp.where(sub == 1, lse_ref[1, i:i + 1, :], 0.0)).T
            for hh in range(2):
                sp = _spare(hh)
                dsum = jnp.sum(jnp.where(lane_mine[hh], pr, 0.0), axis=-1, keepdims=True)
                bias = cq[:, hh * hd:hh * hd + 1] - lse_c[:, hh:hh + 1]
                qa_scr[hh, i] = _put_cols(qs, lane_mine[hh], list(_split3(bias)) + [one, one, one], sp)
                doa_scr[hh, i] = _put_cols(do, lane_mine[hh], list(_split3(-dsum)), sp)
                qst_scr[hh, i] = jnp.where(sub_mine[hh], qst, zero)
                dot_scr[hh, i] = jnp.where(sub_mine[hh], dot, zero)
                ck = list(_split3(-crow_ref[hh, i:i + 1, :]))
                kt_scr[hh, i] = _put_rows(kt, sub_mine[hh], [one, one, one] + ck, sp)
                vt_scr[hh, i] = _put_rows(vt, sub_mine[hh], [one, one, one], sp)
            dq_scr[i] = jnp.zeros((ta, LANES), F32)
            rs_scr[i] = jnp.zeros((ta, LANES), F32)

        for kj in range(nq):
            krows = slice(kj * ta, (kj + 1) * ta)
            k = qkv_ref[krows, LANES:2 * LANES]
            km = (jnp.where(lane_mine[0], k, zero), jnp.where(lane_mine[1], k, zero))
            dkt = jnp.zeros((LANES, ta), F32)
            dvt = jnp.zeros((LANES, ta), F32)
            dcp = [jnp.zeros((8, ta), F32), jnp.zeros((8, ta), F32)]
            for qi in range(kj, nq):
                dq = jnp.zeros((ta, LANES), F32)
                rs = []
                for hh in range(2):
                    sc = _dot(qa_scr[hh, qi], kt_scr[hh, kj])
                    if qi == kj:
                        sc = jnp.where(causal, sc, MASK_VALUE)
                    p = jnp.exp(sc)
                    dsf = p * _dot(doa_scr[hh, qi], vt_scr[hh, kj])
                    dcp[hh] = dcp[hh] + jnp.sum(dsf.reshape(ta // 8, 8, ta), axis=0)
                    rs.append(jnp.sum(dsf, axis=-1, keepdims=True))
                    ds = dsf.astype(BF16)
                    dq = dq + _dot(ds, km[hh])
                    dkt = dkt + _dot(qst_scr[hh, qi], ds)
                    dvt = dvt + _dot(dot_scr[hh, qi], p.astype(BF16))
                dq_scr[qi] += dq
                rs_scr[qi] += jnp.where(lane == 0, rs[0], jnp.where(lane == 1, rs[1], 0.0))
            dqkv_ref[krows, LANES:2 * LANES] = dkt.T.astype(BF16)
            dqkv_ref[krows, 2 * LANES:3 * LANES] = dvt.T.astype(BF16)
            dca = jnp.sum(dcp[0], axis=0, keepdims=True)
            dcb = jnp.sum(dcp[1], axis=0, keepdims=True)
            dcs = jnp.where(sub == 0, dca, jnp.where(sub == 1, dcb, 0.0)).T
            dc_ref[krows, :] += (jnp.where(lane == 2 * pair, -dcs[:, 0:1], 0.0)
                                 + jnp.where(lane == 2 * pair + 1, -dcs[:, 1:2], 0.0))
        for qi in range(nq):
            rows = slice(qi * ta, (qi + 1) * ta)
            dqkv_ref[rows, 0:LANES] = (dq_scr[qi] * 0.125).astype(BF16)
            rq = rs_scr[qi]
            dc_ref[rows, :] += (jnp.where(lane == 2 * pair, rq[:, 0:1], 0.0)
                                + jnp.where(lane == 2 * pair + 1, rq[:, 1:2], 0.0))

    blk = lambda w: pl.BlockSpec((None, s, w), lambda i, p: (i, 0, p))
    rows5 = pl.BlockSpec((None, None, 2, nq, ta), lambda i, p: (i, p, 0, 0, 0))
    by_rows = lambda: pltpu.VMEM((2, nq, ta, LANES), BF16)
    by_cols = lambda: pltpu.VMEM((2, nq, LANES, ta), BF16)
    return pl.pallas_call(
        body, name="attn_bwd", grid=(b, HEAD_PAIRS),
        in_specs=[blk(3 * LANES), blk(LANES), blk(LANES), rows5, rows5, blk(LANES)],
        out_specs=[blk(3 * LANES), pl.BlockSpec((None, s, LANES), lambda i, p: (i, 0, 0))],
        out_shape=[jax.ShapeDtypeStruct((b, s, 3 * D_MODEL), BF16), jax.ShapeDtypeStruct((b, s, LANES), F32)],
        scratch_shapes=[by_rows(), by_rows(), by_cols(), by_cols(), by_cols(), by_cols(),
                        pltpu.VMEM((nq, ta, LANES), F32), pltpu.VMEM((nq, ta, LANES), F32)],
        compiler_params=_cparams(("parallel", "arbitrary")),
    )(qkv3, do3, y3, lse5, crow5, cexp3)


def _rnn_common(xr, cw_ref, cb_ref, bda_ref, bdx_ref, ba_ref, bx_ref, lam_ref, s):
    rows = _iota((s, LANES), 0)

    def down(v, k):
        return jnp.where(rows >= k, pltpu.roll(v, k, 0), 0.0)

    x1, x2, x3 = down(xr, 1), down(xr, 2), down(xr, 3)
    xc = cb_ref[...] + cw_ref[0:1, :] * x3
    xc = xc + cw_ref[1:2, :] * x2
    xc = xc + cw_ref[2:3, :] * x1
    xc = xc + cw_ref[3:4, :] * xr
    xcb = xc.astype(BF16)
    r = _sigmoid(_dot(xcb, bda_ref[...]) + ba_ref[...])
    i = _sigmoid(_dot(xcb, bdx_ref[...]) + bx_ref[...])
    sp = _softplus(-lam_ref[...])
    log_a = (-RG_C * r) * sp
    a = jnp.exp(log_a)
    e2 = -_expm1(2.0 * log_a)
    sq = jnp.sqrt(jnp.maximum(e2, 0.0))
    return rows, (x1, x2, x3), xc, xcb, r, i, sp, a, e2, sq


def _rnn_specs(s):
    blk = lambda off: pl.BlockSpec((None, s, LANES), lambda cb, i: (i, 0, off + cb))
    vec = lambda r: pl.BlockSpec((r, LANES), lambda cb, i: (0, cb))
    mat = pl.BlockSpec((None, LANES, LANES), lambda cb, i: (cb, 0, 0))
    return blk, vec, mat


def _rnn_fwd(zrest3, conv_w, conv_b, bda, bdx, ba, bx, lam):
    b, s, _ = zrest3.shape

    def body(xr_ref, g_ref, cw_ref, cb_ref, bda_ref, bdx_ref, ba_ref, bx_ref, lam_ref, h_ref, gr_ref):
        xr = xr_ref[...]
        rows, _, xc, _, _, i, _, a, _, sq = _rnn_common(
            xr, cw_ref, cb_ref, bda_ref, bdx_ref, ba_ref, bx_ref, lam_ref, s)
        u = sq * (i * xc)
        sh = 1
        while sh < s:
            keep = rows >= sh
            ur = jnp.where(keep, pltpu.roll(u, sh, 0), 0.0)
            u = u + a * ur
            if sh * 2 < s:
                a = a * jnp.where(keep, pltpu.roll(a, sh, 0), 1.0)
            sh *= 2
        h_ref[...] = u
        g = g_ref[...]
        gr_ref[...] = (u * (g * _sigmoid(g))).astype(BF16)

    blk, vec, mat = _rnn_specs(s)
    return pl.pallas_call(
        body, name="rnn_fwd", grid=(N_CBLK, b),
        in_specs=[blk(N_CBLK), blk(2 * N_CBLK), vec(CONV_W), vec(1), mat, mat, vec(1), vec(1), vec(1)],
        out_specs=[blk(0), blk(0)],
        out_shape=[jax.ShapeDtypeStruct((b, s, D_MODEL), F32), jax.ShapeDtypeStruct((b, s, D_MODEL), BF16)],
        compiler_params=_cparams(("parallel", "parallel")),
    )(zrest3, zrest3, conv_w, conv_b, bda, bdx, ba, bx, lam)


def _rnn_bwd(zrest3, h3, dh3, conv_w, conv_b, bda, bdx, ba, bx, lam):
    b, s, _ = zrest3.shape

    def body(xr_ref, h_ref, dh_ref, cw_ref, cb_ref, bda_ref, bdx_ref, ba_ref, bx_ref, lam_ref,
             dxr_ref, pv_ref, dbd_ref):
        @pl.when(pl.program_id(1) == 0)
        def _():
            pv_ref[...] = jnp.zeros_like(pv_ref)
            dbd_ref[...] = jnp.zeros_like(dbd_ref)

        xr = xr_ref[...]
        rows, (x1, x2, x3), xc, xcb, r, i, sp, a, e2, sq = _rnn_common(
            xr, cw_ref, cb_ref, bda_ref, bdx_ref, ba_ref, bx_ref, lam_ref, s)
        h = h_ref[...]
        g = dh_ref[...]
        an = jnp.where(rows < s - 1, pltpu.roll(a, s - 1, 0), 0.0)
        sh = 1
        while sh < s:
            keep = rows < s - sh
            gr = jnp.where(keep, pltpu.roll(g, s - sh, 0), 0.0)
            g = g + an * gr
            if sh * 2 < s:
                an = an * jnp.where(keep, pltpu.roll(an, s - sh, 0), 1.0)
            sh *= 2
        hp = jnp.where(rows >= 1, pltpu.roll(h, 1, 0), 0.0)
        da = g * hp
        dsq = g * (i * xc)
        di = g * (sq * xc)
        dxc = g * (sq * i)
        dlog = da * a - dsq * ((1.0 - e2) / sq)
        dr = dlog * (-RG_C * sp)
        dpr = dr * (r * (1.0 - r))
        dpi = di * (i * (1.0 - i))
        dprb = dpr.astype(BF16)
        dpib = dpi.astype(BF16)
        dxc = dxc + _dot_nt(dprb, bda_ref[...]) + _dot_nt(dpib, bdx_ref[...])

        def up(v, k):
            return jnp.where(rows < s - k, pltpu.roll(v, s - k, 0), 0.0)

        dxr = cw_ref[3:4, :] * dxc + cw_ref[2:3, :] * up(dxc, 1) + cw_ref[1:2, :] * up(dxc, 2) \
            + cw_ref[0:1, :] * up(dxc, 3)
        dxr_ref[...] = dxr.astype(BF16)

        def colsum(v):
            return jnp.sum(v, axis=0, keepdims=True)

        pv_ref[0:1, :] += colsum(dxc * x3)
        pv_ref[1:2, :] += colsum(dxc * x2)
        pv_ref[2:3, :] += colsum(dxc * x1)
        pv_ref[3:4, :] += colsum(dxc * xr)
        pv_ref[4:5, :] += colsum(dxc)
        pv_ref[5:6, :] += colsum(dpr)
        pv_ref[6:7, :] += colsum(dpi)
        pv_ref[7:8, :] += colsum(dlog * r) * (RG_C * _sigmoid(-lam_ref[...]))
        dbd_ref[0] += _dot_tn(xcb, dprb)
        dbd_ref[1] += _dot_tn(xcb, dpib)

    blk, vec, mat = _rnn_specs(s)
    hblk = pl.BlockSpec((None, s, LANES), lambda cb, i: (i, 0, cb))
    return pl.pallas_call(
        body, name="rnn_bwd", grid=(N_CBLK, b),
        in_specs=[blk(N_CBLK), hblk, hblk, vec(CONV_W), vec(1), mat, mat, vec(1), vec(1), vec(1)],
        out_specs=[hblk, pl.BlockSpec((8, LANES), lambda cb, i: (0, cb)),
                   pl.BlockSpec((None, 2, LANES, LANES), lambda cb, i: (cb, 0, 0, 0))],
        out_shape=[jax.ShapeDtypeStruct((b, s, D_MODEL), BF16), jax.ShapeDtypeStruct((8, D_MODEL), F32),
                   jax.ShapeDtypeStruct((N_CBLK, 2, LANES, LANES), F32)],
        compiler_params=_cparams(("parallel", "arbitrary")),
    )(zrest3, h3, dh3, conv_w, conv_b, bda, bdx, ba, bx, lam)


def _branch_merge(ga, gr, wa, wr, zrest):
    t = ga.shape[0]
    tm = min(512, t)
    tn = 512

    def body(ga_ref, gr_ref, wa_ref, wr_ref, mga_ref, mgr_ref, ya_ref, yr_ref, m_ref):
        ya = _dot(ga_ref[...], wa_ref[...])
        yr = _dot(gr_ref[...], wr_ref[...])
        ya_ref[...] = ya
        yr_ref[...] = yr
        m_ref[...] = (_sigmoid(mga_ref[...]) * ya + _sigmoid(mgr_ref[...]) * yr).astype(BF16)

    nj = D_MODEL // tn
    act = pl.BlockSpec((tm, D_MODEL), lambda i, j: (i, 0))
    wgt = pl.BlockSpec((D_MODEL, tn), lambda i, j: (0, j))
    out = pl.BlockSpec((tm, tn), lambda i, j: (i, j))
    return pl.pallas_call(
        body, name="branch_merge", grid=(t // tm, nj),
        in_specs=[act, act, wgt, wgt, pl.BlockSpec((tm, tn), lambda i, j: (i, 3 * nj + j)),
                  pl.BlockSpec((tm, tn), lambda i, j: (i, 4 * nj + j))],
        out_specs=[out, out, out],
        out_shape=[jax.ShapeDtypeStruct((t, D_MODEL), F32), jax.ShapeDtypeStruct((t, D_MODEL), F32),
                   jax.ShapeDtypeStruct((t, D_MODEL), BF16)],
        compiler_params=_cparams(("parallel", "parallel")),
    )(ga, gr, wa, wr, zrest, zrest)


def _out_loss(m, wout, x2, tgt2, wpost):
    t = m.shape[0]
    tm = min(256, t)

    def body(m_ref, w_ref, x_ref, t_ref, wp_ref, dy_ref, do_ref, acc_ref):
        @pl.when(pl.program_id(0) == 0)
        def _():
            acc_ref[...] = jnp.zeros_like(acc_ref)

        o = _dot(m_ref[...], w_ref[...])
        r2 = lax.rsqrt(jnp.mean(o * o, axis=-1, keepdims=True) + NORM_EPS)
        n = o * r2
        wp = wp_ref[...]
        err = (x_ref[...] + n * wp) - t_ref[...]
        dy = err * (1.0 / D_MODEL)
        dn = dy * wp
        do = r2 * (dn - n * jnp.mean(dn * n, axis=-1, keepdims=True))
        dy_ref[...] = dy
        do_ref[...] = do.astype(BF16)
        acc_ref[0:1, :] += jnp.sum(dy * n, axis=0, keepdims=True)
        acc_ref[1:2, :] += jnp.sum(err * err, axis=0, keepdims=True)

    row = pl.BlockSpec((tm, D_MODEL), lambda i: (i, 0))
    return pl.pallas_call(
        body, name="out_loss", grid=(t // tm,),
        in_specs=[row, pl.BlockSpec((D_MODEL, D_MODEL), lambda i: (0, 0)), row, row,
                  pl.BlockSpec((1, D_MODEL), lambda i: (0, 0))],
        out_specs=[row, row, pl.BlockSpec((8, D_MODEL), lambda i: (0, 0))],
        out_shape=[jax.ShapeDtypeStruct((t, D_MODEL), F32), jax.ShapeDtypeStruct((t, D_MODEL), BF16),
                   jax.ShapeDtypeStruct((8, D_MODEL), F32)],
        compiler_params=_cparams(("arbitrary",)),
    )(m, wout, x2, tgt2, wpost)


def _merge_bwd(do, wout, zrest, ya, yr):
    t = do.shape[0]
    tm = min(512, t)
    tn = 512
    nj = D_MODEL // tn

    def body(do_ref, w_ref, mga_ref, mgr_ref, ya_ref, yr_ref, dya_ref, dyr_ref, dmga_ref, dmgr_ref):
        dm = _dot_nt(do_ref[...], w_ref[...])
        sa = _sigmoid(mga_ref[...])
        sr = _sigmoid(mgr_ref[...])
        dya_ref[...] = (dm * sa).astype(BF16)
        dyr_ref[...] = (dm * sr).astype(BF16)
        dmga_ref[...] = (dm * ya_ref[...] * (sa * (1.0 - sa))).astype(BF16)
        dmgr_ref[...] = (dm * yr_ref[...] * (sr * (1.0 - sr))).astype(BF16)

    out = pl.BlockSpec((tm, tn), lambda i, j: (i, j))
    bf = jax.ShapeDtypeStruct((t, D_MODEL), BF16)
    return pl.pallas_call(
        body, name="merge_bwd", grid=(t // tm, nj),
        in_specs=[pl.BlockSpec((tm, D_MODEL), lambda i, j: (i, 0)), pl.BlockSpec((tn, D_MODEL), lambda i, j: (j, 0)),
                  pl.BlockSpec((tm, tn), lambda i, j: (i, 3 * nj + j)),
                  pl.BlockSpec((tm, tn), lambda i, j: (i, 4 * nj + j)), out, out],
        out_specs=[out, out, out, out],
        out_shape=[bf, bf, bf, bf],
        compiler_params=_cparams(("parallel", "parallel")),
    )(do, wout, zrest, zrest, ya, yr)


def _branch_bwd(dya, dyr, wa, wr, zrest, yatt, ylru):
    t = dya.shape[0]
    tm = min(512, t)
    tn = 512
    nj = D_MODEL // tn

    def body(dya_ref, dyr_ref, wa_ref, wr_ref, ga_ref, gr_ref, ya_ref, yl_ref,
             dyatt_ref, dga_ref, dyl_ref, dgr_ref):
        dga = _dot_nt(dya_ref[...], wa_ref[...])
        dgr = _dot_nt(dyr_ref[...], wr_ref[...])
        g = ga_ref[...]
        sg = _sigmoid(g)
        dyatt_ref[...] = (dga * (g * sg)).astype(BF16)
        dga_ref[...] = (dga * ya_ref[...] * (sg * (1.0 + g * (1.0 - sg)))).astype(BF16)
        g = gr_ref[...]
        sg = _sigmoid(g)
        dyl_ref[...] = dgr * (g * sg)
        dgr_ref[...] = (dgr * yl_ref[...] * (sg * (1.0 + g * (1.0 - sg)))).astype(BF16)

    act = pl.BlockSpec((tm, D_MODEL), lambda i, j: (i, 0))
    wgt = pl.BlockSpec((tn, D_MODEL), lambda i, j: (j, 0))
    out = pl.BlockSpec((tm, tn), lambda i, j: (i, j))
    bf = jax.ShapeDtypeStruct((t, D_MODEL), BF16)
    return pl.pallas_call(
        body, name="branch_bwd", grid=(t // tm, nj),
        in_specs=[act, act, wgt, wgt, pl.BlockSpec((tm, tn), lambda i, j: (i, j)),
                  pl.BlockSpec((tm, tn), lambda i, j: (i, 2 * nj + j)), out, out],
        out_specs=[out, out, out, out],
        out_shape=[bf, bf, jax.ShapeDtypeStruct((t, D_MODEL), F32), bf],
        compiler_params=_cparams(("parallel", "parallel")),
    )(dya, dyr, wa, wr, zrest, zrest, yatt, ylru)


def _dh_partial(parts, after, name):
    t = parts[0][0].shape[0]
    tm = min(256, t)
    np_ = len(parts)

    def body(*refs):
        o_ref = refs[-1]
        acc = _dot_nt(refs[0][...], refs[np_][...])
        for p in range(1, np_):
            acc = acc + _dot_nt(refs[p][...], refs[np_ + p][...])
        o_ref[...] = acc

    in_specs = [pl.BlockSpec((tm, dz.shape[1]), lambda i: (i, 0)) for dz, _ in parts]
    in_specs += [pl.BlockSpec(w.shape, lambda i: (0, 0)) for _, w in parts]
    in_specs += [pl.BlockSpec(after.shape, lambda i: (0, 0))]
    return pl.pallas_call(
        body, name=name, grid=(t // tm,),
        in_specs=in_specs,
        out_specs=pl.BlockSpec((tm, D_MODEL), lambda i: (i, 0)),
        out_shape=jax.ShapeDtypeStruct((t, D_MODEL), F32),
        compiler_params=_cparams(("parallel",), vmem_mb=48),
    )(*[dz for dz, _ in parts], *[w for _, w in parts], after)


def _dh_final(parts, acc_in, x2, dy, wpre):
    t = x2.shape[0]
    tm = min(256, t)
    np_ = len(parts)

    def body(*refs):
        acc_ref, x_ref, dy_ref, w_ref = refs[2 * np_:2 * np_ + 4]
        gx_ref, pw_ref = refs[2 * np_ + 4:]

        @pl.when(pl.program_id(0) == 0)
        def _():
            pw_ref[...] = jnp.zeros_like(pw_ref)

        dh = acc_ref[...]
        for p in range(np_):
            dh = dh + _dot_nt(refs[p][...], refs[np_ + p][...])
        x = x_ref[...]
        r = lax.rsqrt(jnp.mean(x * x, axis=-1, keepdims=True) + NORM_EPS)
        xn = x * r
        dxn = dh * w_ref[...]
        gx_ref[...] = r * (dxn - xn * jnp.mean(dxn * xn, axis=-1, keepdims=True)) + dy_ref[...]
        pw_ref[0:1, :] += jnp.sum(dh * xn, axis=0, keepdims=True)

    row = pl.BlockSpec((tm, D_MODEL), lambda i: (i, 0))
    in_specs = [pl.BlockSpec((tm, dz.shape[1]), lambda i: (i, 0)) for dz, _ in parts]
    in_specs += [pl.BlockSpec(w.shape, lambda i: (0, 0)) for _, w in parts]
    in_specs += [row, row, row, pl.BlockSpec((1, D_MODEL), lambda i: (0, 0))]
    return pl.pallas_call(
        body, name="dh_final", grid=(t // tm,),
        in_specs=in_specs,
        out_specs=[row, pl.BlockSpec((8, D_MODEL), lambda i: (0, 0))],
        out_shape=[jax.ShapeDtypeStruct((t, D_MODEL), F32), jax.ShapeDtypeStruct((8, D_MODEL), F32)],
        compiler_params=_cparams(("arbitrary",), vmem_mb=48),
    )(*[dz for dz, _ in parts], *[w for _, w in parts], acc_in, x2, dy, wpre)


def _adamw(w, g, m, v):
    m = ADAM_B1 * m + (1.0 - ADAM_B1) * g
    v = ADAM_B2 * v + (1.0 - ADAM_B2) * (g * g)
    m_hat = m / (1.0 - ADAM_B1 ** ADAM_STEP)
    v_hat = v / (1.0 - ADAM_B2 ** ADAM_STEP)
    delta = -ADAM_LR * (m_hat / (jnp.sqrt(v_hat) + ADAM_EPS) + ADAM_WD * w)
    return delta, m, v


def _sum_chips(own, parts, place, name):
    r, c = own.shape
    tr = min(128, r)

    def body(place_ref, own_ref, p_ref, g_ref):
        mine = place_ref[1]
        own_blk = own_ref[...]
        g = jnp.where(mine == 0, own_blk, p_ref[0].astype(F32))
        for j in range(1, N_CHIPS):
            g = g + jnp.where(mine == j, own_blk, p_ref[j].astype(F32))
        g_ref[...] = g

    row = pl.BlockSpec((tr, c), lambda i, pr: (i, 0))
    grid_spec = pltpu.PrefetchScalarGridSpec(
        num_scalar_prefetch=1, grid=(r // tr,),
        in_specs=[row, pl.BlockSpec((N_CHIPS, tr, c), lambda i, pr: (0, i, 0))], out_specs=row)
    return pl.pallas_call(
        body, name=name, grid_spec=grid_spec, out_shape=jax.ShapeDtypeStruct((r, c), F32),
        compiler_params=_cparams(("parallel",)),
    )(place, own, parts)


def _adamw_by_columns(g, w, m, v, name):
    r, c = w.shape
    tc = 256

    def body(g_ref, w_ref, m_ref, v_ref, d_ref, nm_ref, nv_ref):
        d, nm, nv = _adamw(w_ref[...], g_ref[...], m_ref[...], v_ref[...])
        d_ref[...] = d
        nm_ref[...] = nm
        nv_ref[...] = nv

    col = pl.BlockSpec((r, tc), lambda j: (0, j))
    sh = jax.ShapeDtypeStruct((r, c), F32)
    return pl.pallas_call(
        body, name=name, grid=(c // tc,), in_specs=[col, col, col, col], out_specs=[col, col, col],
        out_shape=[sh, sh, sh], compiler_params=_cparams(("parallel",)),
    )(g, w, m, v)


def _reduce_adamw(own, parts, place, w, m, v, name):
    r, c = w.shape
    tr = min(128, r)

    def body(place_ref, own_ref, p_ref, w_ref, m_ref, v_ref, g_ref, d_ref, nm_ref, nv_ref):
        mine = place_ref[1]
        own_blk = own_ref[...]
        g = jnp.where(mine == 0, own_blk, p_ref[0].astype(F32))
        for j in range(1, N_CHIPS):
            g = g + jnp.where(mine == j, own_blk, p_ref[j].astype(F32))
        d, nm, nv = _adamw(w_ref[...], g, m_ref[...], v_ref[...])
        g_ref[...] = g
        d_ref[...] = d
        nm_ref[...] = nm
        nv_ref[...] = nv

    row = pl.BlockSpec((tr, c), lambda i, pr: (i, 0))
    sh = jax.ShapeDtypeStruct((r, c), F32)
    grid_spec = pltpu.PrefetchScalarGridSpec(
        num_scalar_prefetch=1, grid=(r // tr,),
        in_specs=[row, pl.BlockSpec((N_CHIPS, tr, c), lambda i, pr: (0, i, 0)), row, row, row],
        out_specs=[row, row, row, row])
    return pl.pallas_call(
        body, name=name, grid_spec=grid_spec, out_shape=[sh, sh, sh, sh],
        compiler_params=_cparams(("parallel",)),
    )(place, own, parts, w, m, v)


def _interleave_qkv(a):
    lead = a.shape[:-1]
    return a.reshape(lead + (3, HEAD_PAIRS, LANES)).swapaxes(-3, -2).reshape(lead + (3 * D_MODEL,))


def _deinterleave_qkv(a):
    lead = a.shape[:-1]
    return a.reshape(lead + (HEAD_PAIRS, 3, LANES)).swapaxes(-3, -2).reshape(lead + (3 * D_MODEL,))


def _pack_small(pre, conv_b, rg_ba, rg_bx, lam, post, loss_row, b_in, conv_w_full, rg_wa, rg_wx):
    z = jnp.zeros((1, D_MODEL), F32)
    b_used = jnp.concatenate([b_in[:, 0:3 * D_MODEL], b_in[:, 3 * D_MODEL + HEADS:IN_TOTAL]], axis=1)
    b_f = jnp.pad(b_in[:, 3 * D_MODEL:3 * D_MODEL + HEADS], ((0, 0), (0, D_MODEL - HEADS)))
    return jnp.concatenate([
        pre, conv_b, rg_ba, rg_bx, lam, post, loss_row, z,
        b_used.reshape(9, D_MODEL), b_f, conv_w_full, z, z,
        rg_wa.reshape(64, D_MODEL), rg_wx.reshape(64, D_MODEL)], axis=0)


def _unpack_small(p):
    b_used = p[8:17].reshape(1, 9 * D_MODEL)
    b_in = jnp.concatenate([b_used[:, 0:3 * D_MODEL], p[17:18, 0:HEADS], b_used[:, 3 * D_MODEL:]], axis=1)
    return dict(pre_norm_w=p[0:1], conv_b=p[1:2], rg_ba=p[2:3], rg_bx=p[3:4], rg_lambda=p[4:5],
                post_norm_w=p[5:6], loss_row=p[6:7], b_in=b_in, conv_w_full=p[18:22],
                rg_wa=p[24:88].reshape(1, 16, 64, 64), rg_wx=p[88:152].reshape(1, 16, 64, 64))


def _reduce_small(parts, w, m, v):
    def body(p_ref, w_ref, m_ref, v_ref, g_ref, d_ref, nm_ref, nv_ref):
        g = p_ref[0]
        for j in range(1, N_DEV):
            g = g + p_ref[j]
        d, nm, nv = _adamw(w_ref[...], g, m_ref[...], v_ref[...])
        g_ref[...] = g
        d_ref[...] = d
        nm_ref[...] = nm
        nv_ref[...] = nv

    sh = jax.ShapeDtypeStruct((SMALL_ROWS, D_MODEL), F32)
    return pl.pallas_call(body, name="reduce_small", out_shape=[sh, sh, sh, sh])(parts, w, m, v)


def kernel(x, pre_norm_w, w_in, b_in, conv_w, conv_b, rg_wa, rg_ba, rg_wx, rg_bx, rg_lambda, w_branch_a, w_branch_r, w_out, post_norm_w, loss_target, m_pre_norm_w, m_w_in, m_b_in, m_conv_w, m_conv_b, m_rg_wa, m_rg_ba, m_rg_wx, m_rg_bx, m_rg_lambda, m_w_branch_a, m_w_branch_r, m_w_out, m_post_norm_w, v_pre_norm_w, v_w_in, v_b_in, v_conv_w, v_conv_b, v_rg_wa, v_rg_ba, v_rg_wx, v_rg_bx, v_rg_lambda, v_w_branch_a, v_w_branch_r, v_w_out, v_post_norm_w):
    b, s, _ = x.shape
    t = b * s
    me = 4 * lax.axis_index("x") + 2 * lax.axis_index("y") + lax.axis_index("c")
    shard_rows = D_MODEL // N_DEV

    place = jnp.stack([lax.axis_index("c"), 2 * lax.axis_index("x") + lax.axis_index("y")]).astype(jnp.int32)
    w_in_all = _gather(w_in[0].astype(BF16), "gather_w_in")
    w_full = w_in_all.transpose(1, 0, 2).reshape(D_MODEL, IN_TOTAL)
    conv_terms = jnp.concatenate(_split3(conv_w[0]), axis=0)
    conv_pad = jnp.pad(conv_terms, ((0, 16 - 3 * CONV_W), (0, D_MODEL - LANES)))
    sq_stack = jnp.concatenate([w_branch_a[0].astype(BF16), w_branch_r[0].astype(BF16), w_out[0].astype(BF16),
                                conv_pad], axis=0)
    sq_sems, sq_src, sq_land, sq_token = _gather_start(sq_stack, w_in_all, "gather_w_sq_start")

    w_qkv = _interleave_qkv(w_full[:, 0:3 * D_MODEL])
    w_f = jnp.pad(w_full[:, 3 * D_MODEL:3 * D_MODEL + HEADS], ((0, 0), (0, LANES - HEADS)))
    w_rest = w_full[:, 3 * D_MODEL + HEADS:IN_USED]
    b_qkv = _interleave_qkv(b_in[:, 0:3 * D_MODEL]) + sq_token[0, 0]
    b_f = jnp.pad(b_in[:, 3 * D_MODEL:3 * D_MODEL + HEADS], ((0, 0), (0, LANES - HEADS)))
    b_rest = b_in[:, 3 * D_MODEL + HEADS:IN_USED]

    def blockdiag(w):
        w2 = w.reshape(N_CBLK, 2, HEAD_DIM, HEAD_DIM)
        zz = jnp.zeros((N_CBLK, HEAD_DIM, HEAD_DIM), w.dtype)
        top = jnp.concatenate([w2[:, 0], zz], axis=2)
        bot = jnp.concatenate([zz, w2[:, 1]], axis=2)
        return jnp.concatenate([top, bot], axis=1).astype(BF16)

    bda, bdx = blockdiag(rg_wa[0]), blockdiag(rg_wx[0])

    x2 = x.reshape(t, D_MODEL)
    tgt2 = loss_target.reshape(t, D_MODEL)
    h = _prenorm(x2, pre_norm_w)
    qkv = _mm_bias(h, w_qkv, b_qkv, BF16, "inproj_qkv")
    zrest = _mm_bias(h, w_rest, b_rest, F32, "inproj_rest")
    zf = _mm_bias(h, w_f, b_f, F32, "inproj_f")
    qkv3 = qkv.reshape(b, s, 3 * D_MODEL)
    zrest3 = zrest.reshape(b, s, 5 * D_MODEL)
    zf3 = zf.reshape(b, s, LANES)
    nq = s // ATT_TILE
    cexp3, crow = _fgate_fwd(zf3)
    crow5 = crow.reshape(b, HEAD_PAIRS, 2, nq, ATT_TILE)
    yatt3, lse5, ga3 = _attn_fwd(qkv3, cexp3, crow5, zrest3)

    sq_all = _gather_wait(sq_sems, sq_src, sq_land, ga3, "gather_w_sq_wait")
    sq_all = lax.dynamic_update_slice(sq_all, sq_stack[None], (me, 0, 0))
    wa = sq_all[:, 0:shard_rows].reshape(D_MODEL, D_MODEL)
    wr = sq_all[:, shard_rows:2 * shard_rows].reshape(D_MODEL, D_MODEL)
    wo = sq_all[:, 2 * shard_rows:3 * shard_rows].reshape(D_MODEL, D_MODEL)
    conv_all = sq_all[:, 3 * shard_rows:3 * shard_rows + 3 * CONV_W, 0:LANES].astype(F32)
    conv_all = (conv_all[:, 0:CONV_W] + conv_all[:, CONV_W:2 * CONV_W]) + conv_all[:, 2 * CONV_W:3 * CONV_W]
    conv_full = conv_all.transpose(1, 0, 2).reshape(CONV_W, D_MODEL)

    ylru3, gr3 = _rnn_fwd(zrest3, conv_full, conv_b, bda, bdx, rg_ba, rg_bx, rg_lambda)
    ga, gr = ga3.reshape(t, D_MODEL), gr3.reshape(t, D_MODEL)
    ya, yr, mm = _branch_merge(ga, gr, wa, wr, zrest)
    dy, do, acc_out = _out_loss(mm, wo, x2, tgt2, post_norm_w)

    dya, dyr, dz_mga, dz_mgr = _merge_bwd(do, wo, zrest, ya, yr)
    dyatt, dz_ga, dylru, dz_gr = _branch_bwd(dya, dyr, wa, wr, zrest, yatt3.reshape(t, D_MODEL),
                                             ylru3.reshape(t, D_MODEL))
    dz_xr3, pvec, dbd = _rnn_bwd(zrest3, ylru3, dylru.reshape(b, s, D_MODEL), conv_full, conv_b, bda, bdx,
                                 rg_ba, rg_bx, rg_lambda)
    dqkv3, dc3 = _attn_bwd(qkv3, dyatt.reshape(b, s, D_MODEL), yatt3, lse5, crow5, cexp3)
    dz_f = _fgate_bwd(dc3, zf3).reshape(t, LANES)
    dz_qkv = dqkv3.reshape(t, 3 * D_MODEL)
    dz_xr = dz_xr3.reshape(t, D_MODEL)

    dw_qkv, db_qkv = _mm_tn(h, dz_qkv, "dw_qkv")
    dw_f, db_f = _mm_tn(h, dz_f, "dw_f")
    dw_parts, db_parts = [], []
    for nm, dzp in (("ga", dz_ga), ("xr", dz_xr), ("gr", dz_gr), ("mga", dz_mga), ("mgr", dz_mgr)):
        dwp, dbp = _mm_tn(h, dzp, "dw_" + nm)
        dw_parts.append(dwp)
        db_parts.append(dbp[0:1])
    dw_a, _ = _mm_tn(ga, dya, "dw_a")
    dw_r, _ = _mm_tn(gr, dyr, "dw_r")
    dw_o, _ = _mm_tn(mm, do, "dw_o")

    zeros_tail = jnp.zeros((D_MODEL, IN_TOTAL - IN_USED), F32)
    dw_in_full = jnp.concatenate([_deinterleave_qkv(dw_qkv), dw_f[:, 0:HEADS]] + dw_parts + [zeros_tail], axis=1)
    dw_in_send = dw_in_full.reshape(D_MODEL, N_CHIPS, 2, W_SHARD).transpose(2, 1, 0, 3)
    by_dest = lambda a: a.reshape(N_CHIPS, 2, shard_rows, D_MODEL).transpose(1, 0, 2, 3)
    dw_sq_send = jnp.concatenate([by_dest(dw_a), by_dest(dw_r), by_dest(dw_o)], axis=2)

    sib_in, sib_sq = _swap_with_sibling([dw_in_send, dw_sq_send], "swap_dw")
    chip_in, own_in = _pair_add(dw_in_send, sib_in, place, "pair_add_in")
    chip_sq, own_sq = _pair_add(dw_sq_send, sib_sq, place, "pair_add_sq")
    sems, sent, lands, token = _exchange_chips_start([chip_in, chip_sq], "exchange_dw_start")

    wt = lambda lo: w_rest[:, lo * D_MODEL:(lo + 1) * D_MODEL]
    dh_a = _dh_partial([(dz_qkv, w_qkv), (dz_f, w_f)], token, "dh_qkv")
    grad_x2, acc_pre = _dh_final(
        [(dz_ga, wt(0)), (dz_xr, wt(1)), (dz_gr, wt(2)), (dz_mga, wt(3)), (dz_mgr, wt(4))],
        dh_a, x2, dy, pre_norm_w)

    db_in_full = jnp.concatenate([_deinterleave_qkv(db_qkv[0:1]), db_f[0:1, 0:HEADS]] + db_parts
                                 + [jnp.zeros((1, IN_TOTAL - IN_USED), F32)], axis=1)
    d_rg_wa = jnp.stack([dbd[:, 0, 0:HEAD_DIM, 0:HEAD_DIM], dbd[:, 0, HEAD_DIM:, HEAD_DIM:]], axis=1)
    d_rg_wx = jnp.stack([dbd[:, 1, 0:HEAD_DIM, 0:HEAD_DIM], dbd[:, 1, HEAD_DIM:, HEAD_DIM:]], axis=1)
    small_g = _pack_small(acc_pre[0:1], pvec[4:5], pvec[5:6], pvec[6:7], pvec[7:8], acc_out[0:1], acc_out[1:2],
                          db_in_full, pvec[0:4], d_rg_wa, d_rg_wx)
    sm_sems, sm_src, sm_land, sm_token = _gather_start(small_g, grad_x2, "gather_small_start")
    recv_in, recv_sq = _exchange_chips_wait(sems, sent, lands, sm_token, "exchange_dw_wait")

    g_in = _sum_chips(own_in, recv_in, place, "sum_dw_in")
    d_in_t, nm_in_t, nv_in_t = _adamw_by_columns(g_in.T, w_in[0].T, m_w_in[0].T, v_w_in[0].T, "adamw_w_in")
    d_in, nm_in, nv_in = d_in_t.T, nm_in_t.T, nv_in_t.T
    sq_w = jnp.concatenate([w_branch_a[0], w_branch_r[0], w_out[0]], axis=0)
    sq_m = jnp.concatenate([m_w_branch_a[0], m_w_branch_r[0], m_w_out[0]], axis=0)
    sq_v = jnp.concatenate([v_w_branch_a[0], v_w_branch_r[0], v_w_out[0]], axis=0)
    g_sq, d_sq, nm_sq, nv_sq = _reduce_adamw(own_sq, recv_sq, place, sq_w, sq_m, sq_v, "adamw_w_sq")
    small_all = _gather_wait(sm_sems, sm_src, sm_land, d_sq, "gather_small_wait")
    small_all = lax.dynamic_update_slice(small_all, small_g[None], (me, 0, 0))

    def place_conv(a):
        return lax.dynamic_update_slice(jnp.zeros((CONV_W, D_MODEL), F32), a[0], (0, me * LANES))

    zrow = jnp.zeros((1, D_MODEL), F32)
    small_w = _pack_small(pre_norm_w, conv_b, rg_ba, rg_bx, rg_lambda, post_norm_w, zrow, b_in,
                          place_conv(conv_w), rg_wa[0], rg_wx[0])
    small_m = _pack_small(m_pre_norm_w, m_conv_b, m_rg_ba, m_rg_bx, m_rg_lambda, m_post_norm_w, zrow, m_b_in,
                          place_conv(m_conv_w), m_rg_wa[0], m_rg_wx[0])
    small_v = _pack_small(v_pre_norm_w, v_conv_b, v_rg_ba, v_rg_bx, v_rg_lambda, v_post_norm_w, zrow, v_b_in,
                          place_conv(v_conv_w), v_rg_wa[0], v_rg_wx[0])
    outs_small = [_unpack_small(p) for p in _reduce_small(small_all, small_w, small_m, small_v)]

    loss = (0.5 / D_MODEL) * jnp.sum(outs_small[0]["loss_row"])

    def leaf(kind, name):
        if name == "w_in":
            return (g_in, d_in, nm_in, nv_in)[kind][None]
        if name in ("w_branch_a", "w_branch_r", "w_out"):
            j = ("w_branch_a", "w_branch_r", "w_out").index(name)
            return (g_sq, d_sq, nm_sq, nv_sq)[kind][None, j * shard_rows:(j + 1) * shard_rows]
        if name == "conv_w":
            return lax.dynamic_slice(outs_small[kind]["conv_w_full"], (0, me * LANES), (CONV_W, LANES))[None]
        return outs_small[kind][name]

    names = ["pre_norm_w", "w_in", "b_in", "conv_w", "conv_b", "rg_wa", "rg_ba", "rg_wx", "rg_bx", "rg_lambda",
             "w_branch_a", "w_branch_r", "w_out", "post_norm_w"]
    out = [loss, grad_x2.reshape(b, s, D_MODEL)]
    for kind in range(4):
        out += [leaf(kind, nm) for nm in names]
    return tuple(out)
```

```python
import jax
import jax.numpy as jnp
from jax import lax
from jax.experimental import pallas as pl
from jax.experimental.pallas import tpu as pltpu

F32 = jnp.float32
BF16 = jnp.bfloat16

N_DEV = 8
D_MODEL = 1024
HEADS = 16
HEAD_DIM = 64
HEAD_PAIRS = HEADS // 2
LANES = 128
N_CBLK = D_MODEL // LANES
CONV_W = 4
RG_C = 8.0
NORM_EPS = 1e-6
MASK_VALUE = -1e30
IN_USED = 8208
IN_TOTAL = 9232
W_SHARD = IN_TOTAL // N_DEV

ADAM_LR = 0.001
ADAM_B1 = 0.9
ADAM_B2 = 0.999
ADAM_EPS = 1e-08
ADAM_WD = 0.01
ADAM_STEP = 10

ATT_TILE = 256
SCAN_TILE = 256
SMALL_ROWS = 152


def _cparams(sem=None, vmem_mb=None):
    kw = {}
    if sem is not None:
        kw["dimension_semantics"] = sem
    if vmem_mb is not None:
        kw["vmem_limit_bytes"] = vmem_mb * 1024 * 1024
    return pltpu.CompilerParams(**kw)


def _sigmoid(x):
    return 1.0 / (1.0 + jnp.exp(-x))


def _softplus(x):
    return jnp.maximum(x, 0.0) + jnp.log1p(jnp.exp(-jnp.abs(x)))


def _expm1(x):
    p = x * (1.0 + x * (1.0 / 2 + x * (1.0 / 6 + x * (1.0 / 24 + x * (1.0 / 120 + x * (
        1.0 / 720 + x * (1.0 / 5040 + x * (1.0 / 40320))))))))
    return jnp.where(jnp.abs(x) < 0.5, p, jnp.exp(x) - 1.0)


def _split3(x):
    hi = x.astype(BF16)
    r1 = x - hi.astype(F32)
    mid = r1.astype(BF16)
    lo = (r1 - mid.astype(F32)).astype(BF16)
    return hi, mid, lo


def _dot(a, b):
    return jnp.dot(a, b, preferred_element_type=F32)


def _dot_nt(a, b):
    return lax.dot_general(a, b, (((1,), (1,)), ((), ())), preferred_element_type=F32)


def _dot_tn(a, b):
    return lax.dot_general(a, b, (((0,), (0,)), ((), ())), preferred_element_type=F32)


def _iota(shape, dim):
    return lax.broadcasted_iota(jnp.int32, shape, dim)


_ANY = pl.BlockSpec(memory_space=pl.ANY)
_MESH = pl.DeviceIdType.MESH
N_CHIPS = 4


def _place():
    x, y, c = lax.axis_index("x"), lax.axis_index("y"), lax.axis_index("c")
    other_chips = [(1 - x, y), (x, 1 - y), (1 - x, 1 - y)]
    return x, y, c, other_chips


def _gather(x_shard, name):
    def body(x_ref, out_ref, send_sems, recv_sems, local_sem):
        x, y, c, chips = _place()
        me, sibling = (x, y, c), (x, y, 1 - c)

        def slot(p):
            return out_ref.at[4 * p[0] + 2 * p[1] + p[2]]

        def copy(k, block, to, src=None):
            return pltpu.make_async_remote_copy(
                src_ref=slot(block) if src is None else src, dst_ref=slot(block),
                send_sem=send_sems.at[k], recv_sem=recv_sems.at[k], device_id=to, device_id_type=_MESH)

        mine = pltpu.make_async_copy(x_ref, slot(me), local_sem)
        mine.start()
        first = [copy(0, me, sibling, src=x_ref)]
        first += [copy(1 + j, me, (*chip, c), src=x_ref) for j, chip in enumerate(chips)]
        for cp in first:
            cp.start()
        passed = [copy(4 + j, (*chip, c), sibling) for j, chip in enumerate(chips)]
        for j, chip in enumerate(chips):
            copy(1 + j, (*chip, c), me).wait_recv()
            passed[j].start()
        copy(0, sibling, me).wait_recv()
        for j, chip in enumerate(chips):
            copy(4 + j, (*chip, 1 - c), me).wait_recv()
        for cp in first + passed:
            cp.wait_send()
        mine.wait()

    return pl.pallas_call(
        body, name=name,
        out_shape=jax.ShapeDtypeStruct((N_DEV,) + tuple(x_shard.shape), x_shard.dtype),
        in_specs=[_ANY], out_specs=_ANY,
        scratch_shapes=[pltpu.SemaphoreType.DMA((7,)), pltpu.SemaphoreType.DMA((7,)), pltpu.SemaphoreType.DMA],
    )(x_shard)


def _swap_with_sibling(srcs, name):
    n = len(srcs)

    def body(*refs):
        src_refs, out_refs = refs[:n], refs[n:2 * n]
        send_sems, recv_sems = refs[2 * n:]
        x, y, c, _ = _place()
        cps = [pltpu.make_async_remote_copy(
            src_ref=src_refs[i].at[1 - c], dst_ref=out_refs[i], send_sem=send_sems.at[i], recv_sem=recv_sems.at[i],
            device_id=(x, y, 1 - c), device_id_type=_MESH) for i in range(n)]
        for cp in cps:
            cp.start()
        for cp in cps:
            cp.wait()

    return pl.pallas_call(
        body, name=name,
        out_shape=[jax.ShapeDtypeStruct(a.shape[1:], a.dtype) for a in srcs],
        in_specs=[_ANY] * n, out_specs=[_ANY] * n,
        scratch_shapes=[pltpu.SemaphoreType.DMA((n,)), pltpu.SemaphoreType.DMA((n,))],
    )(*srcs)


def _blocks_2d(r, c):
    if r % 128 == 0:
        return (128, c), r // 128, lambda i: (i, 0)
    return (r, 256), c // 256, lambda i: (0, i)


def _pair_add(src, recv, place, name):
    _, _, r, c = src.shape
    blk, nblk, at = _blocks_2d(r, c)

    def body(place_ref, a_ref, b_ref, q16_ref, own_ref):
        q = a_ref[...] + b_ref[...]
        q16_ref[...] = q.astype(BF16)

        @pl.when(pl.program_id(1) == place_ref[1])
        def _():
            own_ref[...] = q

    grid_spec = pltpu.PrefetchScalarGridSpec(
        num_scalar_prefetch=1, grid=(nblk, N_CHIPS),
        in_specs=[pl.BlockSpec((None, None) + blk, lambda i, j, pr: (pr[0], j) + at(i)),
                  pl.BlockSpec((None,) + blk, lambda i, j, pr: (j,) + at(i))],
        out_specs=[pl.BlockSpec((None,) + blk, lambda i, j, pr: (j,) + at(i)),
                   pl.BlockSpec(blk, lambda i, j, pr: at(i))])
    return pl.pallas_call(
        body, name=name, grid_spec=grid_spec,
        out_shape=[jax.ShapeDtypeStruct((N_CHIPS, r, c), BF16), jax.ShapeDtypeStruct((r, c), F32)],
        compiler_params=_cparams(("parallel", "arbitrary")),
    )(place, src, recv)


_HBM = pl.BlockSpec(memory_space=pltpu.HBM)
_SEM = pl.BlockSpec(memory_space=pltpu.SEMAPHORE)
_DATAFLOW = pltpu.SideEffectType.DATAFLOW_SIDE_EFFECTING


def _chip_copy(src_ref, land_ref, send_sem, recv_sem, k, chips, c, land):
    chip = chips[k]
    return pltpu.make_async_remote_copy(
        src_ref=src_ref.at[2 * chip[0] + chip[1]], dst_ref=land_ref.at[land],
        send_sem=send_sem, recv_sem=recv_sem, device_id=(*chip, c), device_id_type=_MESH)


def _exchange_chips_start(srcs, name):
    n = len(srcs)
    ncp = 3 * n

    def body(*refs):
        src_refs, land_refs = refs[:n], refs[n:2 * n]
        sems = refs[4 * n:4 * n + 2 * ncp]
        token = refs[-1]
        x, y, c, chips = _place()
        for i in range(n):
            for k in range(3):
                j = 3 * i + k
                _chip_copy(src_refs[i], land_refs[i], sems[j], sems[ncp + j], k, chips, c, 2 * x + y).start()
        token[...] = jnp.zeros_like(token)

    hbm = [pltpu.HBM(a.shape, a.dtype) for a in srcs]
    lands = [pltpu.with_memory_space_constraint(lax.empty(a.shape, a.dtype), pltpu.HBM) for a in srcs]
    res = pl.pallas_call(
        body, name=name,
        out_shape=(*hbm, *hbm, *([pltpu.SemaphoreType.DMA(())] * (2 * ncp)), jax.ShapeDtypeStruct((8, LANES), F32)),
        in_specs=[_HBM] * (2 * n),
        out_specs=(*([_HBM] * (2 * n)), *([_SEM] * (2 * ncp)), pl.BlockSpec(memory_space=pltpu.VMEM)),
        input_output_aliases={i: i for i in range(2 * n)},
        compiler_params=pltpu.CompilerParams(has_side_effects=_DATAFLOW),
    )(*[pltpu.with_memory_space_constraint(a, pltpu.HBM) for a in srcs], *lands)
    return list(res[2 * n:2 * n + 2 * ncp]), list(res[:n]), list(res[n:2 * n]), res[-1]


def _exchange_chips_wait(sems, srcs, lands, after, name):
    n = len(srcs)
    ncp = 3 * n

    def body(*refs):
        src_refs, land_refs = refs[:n], refs[n:2 * n]
        sem_refs = refs[2 * n:2 * n + 2 * ncp]
        x, y, c, chips = _place()
        for i in range(n):
            for k in range(3):
                j = 3 * i + k
                cp = _chip_copy(src_refs[i], land_refs[i], sem_refs[j], sem_refs[ncp + j], k, chips, c,
                                2 * chips[k][0] + chips[k][1])
                cp.wait_send()
                cp.wait_recv()

    hbm = [pltpu.HBM(a.shape, a.dtype) for a in srcs]
    res = pl.pallas_call(
        body, name=name, out_shape=(*hbm, *hbm),
        in_specs=[_HBM] * (2 * n) + [_SEM] * (2 * ncp) + [_ANY], out_specs=tuple([_HBM] * (2 * n)),
        input_output_aliases={i: i for i in range(2 * n)},
        compiler_params=pltpu.CompilerParams(has_side_effects=_DATAFLOW),
    )(*srcs, *lands, *sems, after)
    return list(res[n:2 * n])


def _peer_copy(src_ref, land_ref, send_sem, recv_sem, k, place, land):
    x, y, c = place
    peer = (1 - x if k & 4 else x, 1 - y if k & 2 else y, 1 - c if k & 1 else c)
    return pltpu.make_async_remote_copy(
        src_ref=src_ref, dst_ref=land_ref.at[land], send_sem=send_sem, recv_sem=recv_sem,
        device_id=peer, device_id_type=_MESH)


def _gather_start(x_shard, after, name):
    npeer = N_DEV - 1

    def body(x_ref, land_ref, after_ref, x_thru, land_thru, *rest):
        sems, token = rest[:2 * npeer], rest[-1]
        x, y, c, _ = _place()
        for k in range(1, N_DEV):
            _peer_copy(x_ref, land_ref, sems[k - 1], sems[npeer + k - 1], k, (x, y, c), 4 * x + 2 * y + c).start()
        token[...] = jnp.zeros_like(token)

    land = pltpu.with_memory_space_constraint(lax.empty((N_DEV,) + tuple(x_shard.shape), x_shard.dtype), pltpu.HBM)
    res = pl.pallas_call(
        body, name=name,
        out_shape=(pltpu.HBM(x_shard.shape, x_shard.dtype), pltpu.HBM(land.shape, land.dtype),
                   *([pltpu.SemaphoreType.DMA(())] * (2 * npeer)), jax.ShapeDtypeStruct((8, LANES), F32)),
        in_specs=[_HBM, _HBM, _ANY],
        out_specs=(_HBM, _HBM, *([_SEM] * (2 * npeer)), pl.BlockSpec(memory_space=pltpu.VMEM)),
        input_output_aliases={0: 0, 1: 1},
        compiler_params=pltpu.CompilerParams(has_side_effects=_DATAFLOW),
    )(pltpu.with_memory_space_constraint(x_shard, pltpu.HBM), land, after)
    return list(res[2:2 + 2 * npeer]), res[0], res[1], res[-1]


def _gather_wait(sems, src, land, after, name):
    npeer = N_DEV - 1

    def body(x_ref, land_ref, *rest):
        sem_refs = rest[:2 * npeer]
        x, y, c, _ = _place()
        for k in range(1, N_DEV):
            peer_index = (4 * x + 2 * y + c) ^ k
            cp = _peer_copy(x_ref, land_ref, sem_refs[k - 1], sem_refs[npeer + k - 1], k, (x, y, c), peer_index)
            cp.wait_send()
            cp.wait_recv()

    res = pl.pallas_call(
        body, name=name, out_shape=(pltpu.HBM(src.shape, src.dtype), pltpu.HBM(land.shape, land.dtype)),
        in_specs=[_HBM, _HBM] + [_SEM] * (2 * npeer) + [_ANY], out_specs=(_HBM, _HBM),
        input_output_aliases={0: 0, 1: 1},
        compiler_params=pltpu.CompilerParams(has_side_effects=_DATAFLOW),
    )(src, land, *sems, after)
    return res[1]


def _prenorm(x2, w):
    t = x2.shape[0]
    tm = min(512, t)

    def body(x_ref, w_ref, h_ref):
        x = x_ref[...]
        r = lax.rsqrt(jnp.mean(x * x, axis=-1, keepdims=True) + NORM_EPS)
        h_ref[...] = (x * r * w_ref[...]).astype(BF16)

    return pl.pallas_call(
        body, name="prenorm", grid=(t // tm,),
        in_specs=[pl.BlockSpec((tm, D_MODEL), lambda i: (i, 0)), pl.BlockSpec((1, D_MODEL), lambda i: (0, 0))],
        out_specs=pl.BlockSpec((tm, D_MODEL), lambda i: (i, 0)),
        out_shape=jax.ShapeDtypeStruct((t, D_MODEL), BF16),
        compiler_params=_cparams(("parallel",)),
    )(x2, w)


def _mm_bias(a, bt, bias, out_dtype, name):
    m, k = a.shape
    n = bt.shape[0]
    tm = min(512, m)
    tn = min(1024, n)

    def body(a_ref, bt_ref, bias_ref, o_ref, b_scr):
        @pl.when(pl.program_id(1) == 0)
        def _():
            b_scr[...] = _transpose_bf16(bt_ref[...])

        o_ref[...] = (_dot(a_ref[...], b_scr[...]) + bias_ref[...]).astype(o_ref.dtype)

    return pl.pallas_call(
        body, name=name, grid=(n // tn, m // tm),
        in_specs=[pl.BlockSpec((tm, k), lambda j, i: (i, 0)), pl.BlockSpec((tn, k), lambda j, i: (j, 0)),
                  pl.BlockSpec((1, tn), lambda j, i: (0, j))],
        out_specs=pl.BlockSpec((tm, tn), lambda j, i: (i, j)),
        out_shape=jax.ShapeDtypeStruct((m, n), out_dtype),
        scratch_shapes=[pltpu.VMEM((k, tn), BF16)],
        compiler_params=_cparams(("parallel", "arbitrary")),
    )(a, bt, bias)


def _mm_tn(a, b, name):
    t, m = a.shape
    n = b.shape[1]
    tm = min(1024, m)
    tk = min(512, t)

    def body(a_ref, b_ref, o_ref, s_ref):
        kk = pl.program_id(1)

        @pl.when(kk == 0)
        def _():
            o_ref[...] = jnp.zeros_like(o_ref)
            s_ref[...] = jnp.zeros_like(s_ref)

        aa = a_ref[...]
        o_ref[...] += _dot_tn(aa, b_ref[...])
        s_ref[0:1, :] += jnp.sum(aa.astype(F32), axis=0, keepdims=True)

    return pl.pallas_call(
        body, name=name, grid=(m // tm, t // tk),
        in_specs=[pl.BlockSpec((tk, tm), lambda i, kk: (kk, i)), pl.BlockSpec((tk, n), lambda i, kk: (kk, 0))],
        out_specs=[pl.BlockSpec((tm, n), lambda i, kk: (i, 0)), pl.BlockSpec((8, tm), lambda i, kk: (0, i))],
        out_shape=[jax.ShapeDtypeStruct((m, n), F32), jax.ShapeDtypeStruct((8, m), F32)],
        compiler_params=_cparams(("parallel", "arbitrary")),
    )(a, b)


def _fgate_fwd(zf3):
    b, s, _ = zf3.shape
    tb = SCAN_TILE
    nb = s // tb

    def body(z_ref, cexp_ref, crow_ref):
        tri = (_iota((tb, tb), 1) <= _iota((tb, tb), 0)).astype(BF16)
        expand = ((_iota((LANES, D_MODEL), 1) >> 6) == _iota((LANES, D_MODEL), 0)).astype(BF16)
        carry = jnp.zeros((1, LANES), F32)
        for i in range(nb):
            rows = slice(i * tb, (i + 1) * tb)
            z = z_ref[rows, :]
            lf = jnp.minimum(z, 0.0) - jnp.log1p(jnp.exp(-jnp.abs(z)))
            cb = sum(_dot(tri, part) for part in _split3(lf)) + carry
            carry = cb[tb - 1:tb, :]
            cexp_ref[rows, :] = sum(_dot(part, expand) for part in _split3(cb))
            crow_ref[:, rows] = cb.T[0:HEADS, :]

    return pl.pallas_call(
        body, name="fgate_fwd", grid=(b,),
        in_specs=[pl.BlockSpec((None, s, LANES), lambda i: (i, 0, 0))],
        out_specs=[pl.BlockSpec((None, s, D_MODEL), lambda i: (i, 0, 0)),
                   pl.BlockSpec((None, HEADS, s), lambda i: (i, 0, 0))],
        out_shape=[jax.ShapeDtypeStruct((b, s, D_MODEL), F32), jax.ShapeDtypeStruct((b, HEADS, s), F32)],
        compiler_params=_cparams(("parallel",)),
    )(zf3)


def _fgate_bwd(dc3, zf3):
    b, s, _ = zf3.shape
    tb = SCAN_TILE
    nb = s // tb

    def body(dc_ref, z_ref, o_ref):
        tri = (_iota((tb, tb), 1) >= _iota((tb, tb), 0)).astype(BF16)
        carry = jnp.zeros((1, LANES), F32)
        for i in reversed(range(nb)):
            rows = slice(i * tb, (i + 1) * tb)
            dlf = sum(_dot(tri, part) for part in _split3(dc_ref[rows, :])) + carry
            carry = dlf[0:1, :]
            o_ref[rows, :] = (dlf * _sigmoid(-z_ref[rows, :])).astype(BF16)

    return pl.pallas_call(
        body, name="fgate_bwd", grid=(b,),
        in_specs=[pl.BlockSpec((None, s, LANES), lambda i: (i, 0, 0)),
                  pl.BlockSpec((None, s, LANES), lambda i: (i, 0, 0))],
        out_specs=pl.BlockSpec((None, s, LANES), lambda i: (i, 0, 0)),
        out_shape=jax.ShapeDtypeStruct((b, s, LANES), BF16),
        compiler_params=_cparams(("parallel",)),
    )(dc3, zf3)


def _spare(hh):
    return HEAD_DIM if hh == 0 else 0


def _put_cols(tile, mine, cols, first):
    lane = _iota((1, LANES), 1)
    out = jnp.where(mine, tile, jnp.zeros((), tile.dtype))
    for j, c in enumerate(cols):
        out = jnp.where(lane == first + j, c, out)
    return out


def _put_rows(tile, mine, rows, first):
    sub = _iota((LANES, 1), 0)
    out = jnp.where(mine, tile, jnp.zeros((), tile.dtype))
    for j, r in enumerate(rows):
        out = jnp.where(sub == first + j, r, out)
    return out


def _transpose_bf16(a):
    return a.astype(F32).T.astype(BF16)


def _attn_fwd(qkv3, cexp3, crow5, zrest3):
    b, s, _ = qkv3.shape
    ta = ATT_TILE
    nq = s // ta
    hd = HEAD_DIM

    def body(qkv_ref, cq_ref, ck_ref, g_ref, y_ref, lse_ref, ga_ref, kt_scr, v_scr):
        lane = _iota((1, LANES), 1)
        sub = _iota((LANES, 1), 0)
        lane_mine = (lane < hd, lane >= hd)
        sub_mine = (sub < hd, sub >= hd)
        causal = _iota((ta, ta), 0) >= _iota((ta, ta), 1)
        one = jnp.ones((), BF16)

        for kj in range(nq):
            rows = slice(kj * ta, (kj + 1) * ta)
            kt = _transpose_bf16(qkv_ref[rows, LANES:2 * LANES])
            v = qkv_ref[rows, 2 * LANES:3 * LANES]
            for hh in range(2):
                ck = list(_split3(-ck_ref[hh, kj:kj + 1, :]))
                kt_scr[hh, kj] = _put_rows(kt, sub_mine[hh], [one, one, one] + ck, _spare(hh))
                v_scr[hh, kj] = _put_cols(v, lane_mine[hh], [one], _spare(hh))

        for qi in range(nq):
            rows = slice(qi * ta, (qi + 1) * ta)
            q = qkv_ref[rows, 0:LANES] * 0.125
            cq = cq_ref[rows, :]
            qh = [_put_cols(q, lane_mine[hh], list(_split3(cq[:, hh * hd:hh * hd + 1])) + [one, one, one], _spare(hh))
                  for hh in range(2)]
            st = [(jnp.full((ta, 1), MASK_VALUE, F32), jnp.zeros((ta, LANES), F32))] * 2
            for kj in range(qi + 1):
                for hh in range(2):
                    m, acc = st[hh]
                    sc = _dot(qh[hh], kt_scr[hh, kj])
                    if kj == qi:
                        sc = jnp.where(causal, sc, MASK_VALUE)
                    mn = jnp.maximum(m, jnp.max(sc, axis=-1, keepdims=True))
                    p = jnp.exp(sc - mn).astype(BF16)
                    st[hh] = (mn, jnp.exp(m - mn) * acc + _dot(p, v_scr[hh, kj]))
            (ma, acca), (mb, accb) = st
            la = acca[:, hd:hd + 1]
            lb = accb[:, 0:1]
            y = jnp.where(lane_mine[0], acca * (1.0 / la), accb * (1.0 / lb))
            lse = jnp.where(lane_mine[0], ma + jnp.log(la), mb + jnp.log(lb)).T
            lse_ref[0, qi:qi + 1, :] = lse[0:1, :]
            lse_ref[1, qi:qi + 1, :] = lse[hd:hd + 1, :]
            y_ref[rows, :] = y
            g = g_ref[rows, :]
            ga_ref[rows, :] = (y * (g * _sigmoid(g))).astype(BF16)

    blk = lambda w: pl.BlockSpec((None, s, w), lambda i, p: (i, 0, p))
    rows5 = pl.BlockSpec((None, None, 2, nq, ta), lambda i, p: (i, p, 0, 0, 0))
    return pl.pallas_call(
        body, name="attn_fwd", grid=(b, HEAD_PAIRS),
        in_specs=[blk(3 * LANES), blk(LANES), rows5, blk(LANES)],
        out_specs=[blk(LANES), rows5, blk(LANES)],
        out_shape=[jax.ShapeDtypeStruct((b, s, D_MODEL), F32),
                   jax.ShapeDtypeStruct((b, HEAD_PAIRS, 2, nq, ta), F32),
                   jax.ShapeDtypeStruct((b, s, D_MODEL), BF16)],
        scratch_shapes=[pltpu.VMEM((2, nq, LANES, ta), BF16), pltpu.VMEM((2, nq, ta, LANES), BF16)],
        compiler_params=_cparams(("parallel", "parallel")),
    )(qkv3, cexp3, crow5, zrest3)


def _attn_bwd(qkv3, do3, y3, lse5, crow5, cexp3):
    b, s, _ = qkv3.shape
    ta = ATT_TILE
    nq = s // ta
    hd = HEAD_DIM

    def body(qkv_ref, do_ref, y_ref, lse_ref, crow_ref, cexp_ref, dqkv_ref, dc_ref,
             qa_scr, doa_scr, qst_scr, dot_scr, kt_scr, vt_scr, dq_scr, rs_scr):
        pair = pl.program_id(1)
        lane = _iota((1, LANES), 1)
        sub = _iota((LANES, 1), 0)
        lane_mine = (lane < hd, lane >= hd)
        sub_mine = (sub < hd, sub >= hd)
        causal = _iota((ta, ta), 0) >= _iota((ta, ta), 1)
        one = jnp.ones((), BF16)
        zero = jnp.zeros((), BF16)

        @pl.when(pair == 0)
        def _():
            dc_ref[...] = jnp.zeros_like(dc_ref)

        for i in range(nq):
            rows = slice(i * ta, (i + 1) * ta)
            qs = qkv_ref[rows, 0:LANES] * 0.125
            qst = _transpose_bf16(qs)
            kt = _transpose_bf16(qkv_ref[rows, LANES:2 * LANES])
            vt = _transpose_bf16(qkv_ref[rows, 2 * LANES:3 * LANES])
            do = do_ref[rows, :]
            dof = do.astype(F32)
            dot = dof.T.astype(BF16)
            pr = y_ref[rows, :] * dof
            cq = cexp_ref[rows, :]
            lse_c = jnp.where(sub == 0, lse_ref[0, i:i + 1, :],
                              jnp.where(sub == 1, lse_ref[1, i:i + 1, :], 0.0)).T
            for hh in range(2):
                sp = _spare(hh)
                dsum = jnp.sum(jnp.where(lane_mine[hh], pr, 0.0), axis=-1, keepdims=True)
                bias = cq[:, hh * hd:hh * hd + 1] - lse_c[:, hh:hh + 1]
                qa_scr[hh, i] = _put_cols(qs, lane_mine[hh], list(_split3(bias)) + [one, one, one], sp)
                doa_scr[hh, i] = _put_cols(do, lane_mine[hh], list(_split3(-dsum)), sp)
                qst_scr[hh, i] = jnp.where(sub_mine[hh], qst, zero)
                dot_scr[hh, i] = jnp.where(sub_mine[hh], dot, zero)
                ck = list(_split3(-crow_ref[hh, i:i + 1, :]))
                kt_scr[hh, i] = _put_rows(kt, sub_mine[hh], [one, one, one] + ck, sp)
                vt_scr[hh, i] = _put_rows(vt, sub_mine[hh], [one, one, one], sp)
            dq_scr[i] = jnp.zeros((ta, LANES), F32)
            rs_scr[i] = jnp.zeros((ta, LANES), F32)

        for kj in range(nq):
            krows = slice(kj * ta, (kj + 1) * ta)
            k = qkv_ref[krows, LANES:2 * LANES]
            km = (jnp.where(lane_mine[0], k, zero), jnp.where(lane_mine[1], k, zero))
            dkt = jnp.zeros((LANES, ta), F32)
            dvt = jnp.zeros((LANES, ta), F32)
            dcp = [jnp.zeros((8, ta), F32), jnp.zeros((8, ta), F32)]
            for qi in range(kj, nq):
                dq = jnp.zeros((ta, LANES), F32)
                rs = []
                for hh in range(2):
                    sc = _dot(qa_scr[hh, qi], kt_scr[hh, kj])
                    if qi == kj:
                        sc = jnp.where(causal, sc, MASK_VALUE)
                    p = jnp.exp(sc)
                    dsf = p * _dot(doa_scr[hh, qi], vt_scr[hh, kj])
                    dcp[hh] = dcp[hh] + jnp.sum(dsf.reshape(ta // 8, 8, ta), axis=0)
                    rs.append(jnp.sum(dsf, axis=-1, keepdims=True))
                    ds = dsf.astype(BF16)
                    dq = dq + _dot(ds, km[hh])
                    dkt = dkt + _dot(qst_scr[hh, qi], ds)
                    dvt = dvt + _dot(dot_scr[hh, qi], p.astype(BF16))
                dq_scr[qi] += dq
                rs_scr[qi] += jnp.where(lane == 0, rs[0], jnp.where(lane == 1, rs[1], 0.0))
            dqkv_ref[krows, LANES:2 * LANES] = dkt.T.astype(BF16)
            dqkv_ref[krows, 2 * LANES:3 * LANES] = dvt.T.astype(BF16)
            dca = jnp.sum(dcp[0], axis=0, keepdims=True)
            dcb = jnp.sum(dcp[1], axis=0, keepdims=True)
            dcs = jnp.where(sub == 0, dca, jnp.where(sub == 1, dcb, 0.0)).T
            dc_ref[krows, :] += (jnp.where(lane == 2 * pair, -dcs[:, 0:1], 0.0)
                                 + jnp.where(lane == 2 * pair + 1, -dcs[:, 1:2], 0.0))
        for qi in range(nq):
            rows = slice(qi * ta, (qi + 1) * ta)
            dqkv_ref[rows, 0:LANES] = (dq_scr[qi] * 0.125).astype(BF16)
            rq = rs_scr[qi]
            dc_ref[rows, :] += (jnp.where(lane == 2 * pair, rq[:, 0:1], 0.0)
                                + jnp.where(lane == 2 * pair + 1, rq[:, 1:2], 0.0))

    blk = lambda w: pl.BlockSpec((None, s, w), lambda i, p: (i, 0, p))
    rows5 = pl.BlockSpec((None, None, 2, nq, ta), lambda i, p: (i, p, 0, 0, 0))
    by_rows = lambda: pltpu.VMEM((2, nq, ta, LANES), BF16)
    by_cols = lambda: pltpu.VMEM((2, nq, LANES, ta), BF16)
    return pl.pallas_call(
        body, name="attn_bwd", grid=(b, HEAD_PAIRS),
        in_specs=[blk(3 * LANES), blk(LANES), blk(LANES), rows5, rows5, blk(LANES)],
        out_specs=[blk(3 * LANES), pl.BlockSpec((None, s, LANES), lambda i, p: (i, 0, 0))],
        out_shape=[jax.ShapeDtypeStruct((b, s, 3 * D_MODEL), BF16), jax.ShapeDtypeStruct((b, s, LANES), F32)],
        scratch_shapes=[by_rows(), by_rows(), by_cols(), by_cols(), by_cols(), by_cols(),
                        pltpu.VMEM((nq, ta, LANES), F32), pltpu.VMEM((nq, ta, LANES), F32)],
        compiler_params=_cparams(("parallel", "arbitrary")),
    )(qkv3, do3, y3, lse5, crow5, cexp3)


def _rnn_common(xr, cw_ref, cb_ref, bda_ref, bdx_ref, ba_ref, bx_ref, lam_ref, s):
    rows = _iota((s, LANES), 0)

    def down(v, k):
        return jnp.where(rows >= k, pltpu.roll(v, k, 0), 0.0)

    x1, x2, x3 = down(xr, 1), down(xr, 2), down(xr, 3)
    xc = cb_ref[...] + cw_ref[0:1, :] * x3
    xc = xc + cw_ref[1:2, :] * x2
    xc = xc + cw_ref[2:3, :] * x1
    xc = xc + cw_ref[3:4, :] * xr
    xcb = xc.astype(BF16)
    r = _sigmoid(_dot(xcb, bda_ref[...]) + ba_ref[...])
    i = _sigmoid(_dot(xcb, bdx_ref[...]) + bx_ref[...])
    sp = _softplus(-lam_ref[...])
    log_a = (-RG_C * r) * sp
    a = jnp.exp(log_a)
    e2 = -_expm1(2.0 * log_a)
    sq = jnp.sqrt(jnp.maximum(e2, 0.0))
    return rows, (x1, x2, x3), xc, xcb, r, i, sp, a, e2, sq


def _rnn_specs(s):
    blk = lambda off: pl.BlockSpec((None, s, LANES), lambda cb, i: (i, 0, off + cb))
    vec = lambda r: pl.BlockSpec((r, LANES), lambda cb, i: (0, cb))
    mat = pl.BlockSpec((None, LANES, LANES), lambda cb, i: (cb, 0, 0))
    return blk, vec, mat


def _rnn_fwd(zrest3, conv_w, conv_b, bda, bdx, ba, bx, lam):
    b, s, _ = zrest3.shape

    def body(xr_ref, g_ref, cw_ref, cb_ref, bda_ref, bdx_ref, ba_ref, bx_ref, lam_ref, h_ref, gr_ref):
        xr = xr_ref[...]
        rows, _, xc, _, _, i, _, a, _, sq = _rnn_common(
            xr, cw_ref, cb_ref, bda_ref, bdx_ref, ba_ref, bx_ref, lam_ref, s)
        u = sq * (i * xc)
        sh = 1
        while sh < s:
            keep = rows >= sh
            ur = jnp.where(keep, pltpu.roll(u, sh, 0), 0.0)
            u = u + a * ur
            if sh * 2 < s:
                a = a * jnp.where(keep, pltpu.roll(a, sh, 0), 1.0)
            sh *= 2
        h_ref[...] = u
        g = g_ref[...]
        gr_ref[...] = (u * (g * _sigmoid(g))).astype(BF16)

    blk, vec, mat = _rnn_specs(s)
    return pl.pallas_call(
        body, name="rnn_fwd", grid=(N_CBLK, b),
        in_specs=[blk(N_CBLK), blk(2 * N_CBLK), vec(CONV_W), vec(1), mat, mat, vec(1), vec(1), vec(1)],
        out_specs=[blk(0), blk(0)],
        out_shape=[jax.ShapeDtypeStruct((b, s, D_MODEL), F32), jax.ShapeDtypeStruct((b, s, D_MODEL), BF16)],
        compiler_params=_cparams(("parallel", "parallel")),
    )(zrest3, zrest3, conv_w, conv_b, bda, bdx, ba, bx, lam)


def _rnn_bwd(zrest3, h3, dh3, conv_w, conv_b, bda, bdx, ba, bx, lam):
    b, s, _ = zrest3.shape

    def body(xr_ref, h_ref, dh_ref, cw_ref, cb_ref, bda_ref, bdx_ref, ba_ref, bx_ref, lam_ref,
             dxr_ref, pv_ref, dbd_ref):
        @pl.when(pl.program_id(1) == 0)
        def _():
            pv_ref[...] = jnp.zeros_like(pv_ref)
            dbd_ref[...] = jnp.zeros_like(dbd_ref)

        xr = xr_ref[...]
        rows, (x1, x2, x3), xc, xcb, r, i, sp, a, e2, sq = _rnn_common(
            xr, cw_ref, cb_ref, bda_ref, bdx_ref, ba_ref, bx_ref, lam_ref, s)
        h = h_ref[...]
        g = dh_ref[...]
        an = jnp.where(rows < s - 1, pltpu.roll(a, s - 1, 0), 0.0)
        sh = 1
        while sh < s:
            keep = rows < s - sh
            gr = jnp.where(keep, pltpu.roll(g, s - sh, 0), 0.0)
            g = g + an * gr
            if sh * 2 < s:
                an = an * jnp.where(keep, pltpu.roll(an, s - sh, 0), 1.0)
            sh *= 2
        hp = jnp.where(rows >= 1, pltpu.roll(h, 1, 0), 0.0)
        da = g * hp
        dsq = g * (i * xc)
        di = g * (sq * xc)
        dxc = g * (sq * i)
        dlog = da * a - dsq * ((1.0 - e2) / sq)
        dr = dlog * (-RG_C * sp)
        dpr = dr * (r * (1.0 - r))
        dpi = di * (i * (1.0 - i))
        dprb = dpr.astype(BF16)
        dpib = dpi.astype(BF16)
        dxc = dxc + _dot_nt(dprb, bda_ref[...]) + _dot_nt(dpib, bdx_ref[...])

        def up(v, k):
            return jnp.where(rows < s - k, pltpu.roll(v, s - k, 0), 0.0)

        dxr = cw_ref[3:4, :] * dxc + cw_ref[2:3, :] * up(dxc, 1) + cw_ref[1:2, :] * up(dxc, 2) \
            + cw_ref[0:1, :] * up(dxc, 3)
        dxr_ref[...] = dxr.astype(BF16)

        def colsum(v):
            return jnp.sum(v, axis=0, keepdims=True)

        pv_ref[0:1, :] += colsum(dxc * x3)
        pv_ref[1:2, :] += colsum(dxc * x2)
        pv_ref[2:3, :] += colsum(dxc * x1)
        pv_ref[3:4, :] += colsum(dxc * xr)
        pv_ref[4:5, :] += colsum(dxc)
        pv_ref[5:6, :] += colsum(dpr)
        pv_ref[6:7, :] += colsum(dpi)
        pv_ref[7:8, :] += colsum(dlog * r) * (RG_C * _sigmoid(-lam_ref[...]))
        dbd_ref[0] += _dot_tn(xcb, dprb)
        dbd_ref[1] += _dot_tn(xcb, dpib)

    blk, vec, mat = _rnn_specs(s)
    hblk = pl.BlockSpec((None, s, LANES), lambda cb, i: (i, 0, cb))
    return pl.pallas_call(
        body, name="rnn_bwd", grid=(N_CBLK, b),
        in_specs=[blk(N_CBLK), hblk, hblk, vec(CONV_W), vec(1), mat, mat, vec(1), vec(1), vec(1)],
        out_specs=[hblk, pl.BlockSpec((8, LANES), lambda cb, i: (0, cb)),
                   pl.BlockSpec((None, 2, LANES, LANES), lambda cb, i: (cb, 0, 0, 0))],
        out_shape=[jax.ShapeDtypeStruct((b, s, D_MODEL), BF16), jax.ShapeDtypeStruct((8, D_MODEL), F32),
                   jax.ShapeDtypeStruct((N_CBLK, 2, LANES, LANES), F32)],
        compiler_params=_cparams(("parallel", "arbitrary")),
    )(zrest3, h3, dh3, conv_w, conv_b, bda, bdx, ba, bx, lam)


def _branch_merge(ga, gr, wa, wr, zrest):
    t = ga.shape[0]
    tm = min(512, t)
    tn = 512

    def body(ga_ref, gr_ref, wa_ref, wr_ref, mga_ref, mgr_ref, ya_ref, yr_ref, m_ref):
        ya = _dot(ga_ref[...], wa_ref[...])
        yr = _dot(gr_ref[...], wr_ref[...])
        ya_ref[...] = ya
        yr_ref[...] = yr
        m_ref[...] = (_sigmoid(mga_ref[...]) * ya + _sigmoid(mgr_ref[...]) * yr).astype(BF16)

    nj = D_MODEL // tn
    act = pl.BlockSpec((tm, D_MODEL), lambda i, j: (i, 0))
    wgt = pl.BlockSpec((D_MODEL, tn), lambda i, j: (0, j))
    out = pl.BlockSpec((tm, tn), lambda i, j: (i, j))
    return pl.pallas_call(
        body, name="branch_merge", grid=(t // tm, nj),
        in_specs=[act, act, wgt, wgt, pl.BlockSpec((tm, tn), lambda i, j: (i, 3 * nj + j)),
                  pl.BlockSpec((tm, tn), lambda i, j: (i, 4 * nj + j))],
        out_specs=[out, out, out],
        out_shape=[jax.ShapeDtypeStruct((t, D_MODEL), F32), jax.ShapeDtypeStruct((t, D_MODEL), F32),
                   jax.ShapeDtypeStruct((t, D_MODEL), BF16)],
        compiler_params=_cparams(("parallel", "parallel")),
    )(ga, gr, wa, wr, zrest, zrest)


def _out_loss(m, wout, x2, tgt2, wpost):
    t = m.shape[0]
    tm = min(256, t)

    def body(m_ref, w_ref, x_ref, t_ref, wp_ref, dy_ref, do_ref, acc_ref):
        @pl.when(pl.program_id(0) == 0)
        def _():
            acc_ref[...] = jnp.zeros_like(acc_ref)

        o = _dot(m_ref[...], w_ref[...])
        r2 = lax.rsqrt(jnp.mean(o * o, axis=-1, keepdims=True) + NORM_EPS)
        n = o * r2
        wp = wp_ref[...]
        err = (x_ref[...] + n * wp) - t_ref[...]
        dy = err * (1.0 / D_MODEL)
        dn = dy * wp
        do = r2 * (dn - n * jnp.mean(dn * n, axis=-1, keepdims=True))
        dy_ref[...] = dy
        do_ref[...] = do.astype(BF16)
        acc_ref[0:1, :] += jnp.sum(dy * n, axis=0, keepdims=True)
        acc_ref[1:2, :] += jnp.sum(err * err, axis=0, keepdims=True)

    row = pl.BlockSpec((tm, D_MODEL), lambda i: (i, 0))
    return pl.pallas_call(
        body, name="out_loss", grid=(t // tm,),
        in_specs=[row, pl.BlockSpec((D_MODEL, D_MODEL), lambda i: (0, 0)), row, row,
                  pl.BlockSpec((1, D_MODEL), lambda i: (0, 0))],
        out_specs=[row, row, pl.BlockSpec((8, D_MODEL), lambda i: (0, 0))],
        out_shape=[jax.ShapeDtypeStruct((t, D_MODEL), F32), jax.ShapeDtypeStruct((t, D_MODEL), BF16),
                   jax.ShapeDtypeStruct((8, D_MODEL), F32)],
        compiler_params=_cparams(("arbitrary",)),
    )(m, wout, x2, tgt2, wpost)


def _merge_bwd(do, wout, zrest, ya, yr):
    t = do.shape[0]
    tm = min(512, t)
    tn = 512
    nj = D_MODEL // tn

    def body(do_ref, w_ref, mga_ref, mgr_ref, ya_ref, yr_ref, dya_ref, dyr_ref, dmga_ref, dmgr_ref):
        dm = _dot_nt(do_ref[...], w_ref[...])
        sa = _sigmoid(mga_ref[...])
        sr = _sigmoid(mgr_ref[...])
        dya_ref[...] = (dm * sa).astype(BF16)
        dyr_ref[...] = (dm * sr).astype(BF16)
        dmga_ref[...] = (dm * ya_ref[...] * (sa * (1.0 - sa))).astype(BF16)
        dmgr_ref[...] = (dm * yr_ref[...] * (sr * (1.0 - sr))).astype(BF16)

    out = pl.BlockSpec((tm, tn), lambda i, j: (i, j))
    bf = jax.ShapeDtypeStruct((t, D_MODEL), BF16)
    return pl.pallas_call(
        body, name="merge_bwd", grid=(t // tm, nj),
        in_specs=[pl.BlockSpec((tm, D_MODEL), lambda i, j: (i, 0)), pl.BlockSpec((tn, D_MODEL), lambda i, j: (j, 0)),
                  pl.BlockSpec((tm, tn), lambda i, j: (i, 3 * nj + j)),
                  pl.BlockSpec((tm, tn), lambda i, j: (i, 4 * nj + j)), out, out],
        out_specs=[out, out, out, out],
        out_shape=[bf, bf, bf, bf],
        compiler_params=_cparams(("parallel", "parallel")),
    )(do, wout, zrest, zrest, ya, yr)


def _branch_bwd(dya, dyr, wa, wr, zrest, yatt, ylru):
    t = dya.shape[0]
    tm = min(512, t)
    tn = 512
    nj = D_MODEL // tn

    def body(dya_ref, dyr_ref, wa_ref, wr_ref, ga_ref, gr_ref, ya_ref, yl_ref,
             dyatt_ref, dga_ref, dyl_ref, dgr_ref):
        dga = _dot_nt(dya_ref[...], wa_ref[...])
        dgr = _dot_nt(dyr_ref[...], wr_ref[...])
        g = ga_ref[...]
        sg = _sigmoid(g)
        dyatt_ref[...] = (dga * (g * sg)).astype(BF16)
        dga_ref[...] = (dga * ya_ref[...] * (sg * (1.0 + g * (1.0 - sg)))).astype(BF16)
        g = gr_ref[...]
        sg = _sigmoid(g)
        dyl_ref[...] = dgr * (g * sg)
        dgr_ref[...] = (dgr * yl_ref[...] * (sg * (1.0 + g * (1.0 - sg)))).astype(BF16)

    act = pl.BlockSpec((tm, D_MODEL), lambda i, j: (i, 0))
    wgt = pl.BlockSpec((tn, D_MODEL), lambda i, j: (j, 0))
    out = pl.BlockSpec((tm, tn), lambda i, j: (i, j))
    bf = jax.ShapeDtypeStruct((t, D_MODEL), BF16)
    return pl.pallas_call(
        body, name="branch_bwd", grid=(t // tm, nj),
        in_specs=[act, act, wgt, wgt, pl.BlockSpec((tm, tn), lambda i, j: (i, j)),
                  pl.BlockSpec((tm, tn), lambda i, j: (i, 2 * nj + j)), out, out],
        out_specs=[out, out, out, out],
        out_shape=[bf, bf, jax.ShapeDtypeStruct((t, D_MODEL), F32), bf],
        compiler_params=_cparams(("parallel", "parallel")),
    )(dya, dyr, wa, wr, zrest, zrest, yatt, ylru)


def _dh_partial(parts, after, name):
    t = parts[0][0].shape[0]
    tm = min(256, t)
    np_ = len(parts)

    def body(*refs):
        o_ref = refs[-1]
        acc = _dot(refs[0][...], refs[np_][...])
        for p in range(1, np_):
            acc = acc + _dot(refs[p][...], refs[np_ + p][...])
        o_ref[...] = acc

    in_specs = [pl.BlockSpec((tm, dz.shape[1]), lambda i: (i, 0)) for dz, _ in parts]
    in_specs += [pl.BlockSpec(w.shape, lambda i: (0, 0)) for _, w in parts]
    in_specs += [pl.BlockSpec(after.shape, lambda i: (0, 0))]
    return pl.pallas_call(
        body, name=name, grid=(t // tm,),
        in_specs=in_specs,
        out_specs=pl.BlockSpec((tm, D_MODEL), lambda i: (i, 0)),
        out_shape=jax.ShapeDtypeStruct((t, D_MODEL), F32),
        compiler_params=_cparams(("parallel",), vmem_mb=48),
    )(*[dz for dz, _ in parts], *[w for _, w in parts], after)


def _dh_final(parts, acc_in, x2, dy, wpre):
    t = x2.shape[0]
    tm = min(256, t)
    np_ = len(parts)

    def body(*refs):
        acc_ref, x_ref, dy_ref, w_ref = refs[2 * np_:2 * np_ + 4]
        gx_ref, pw_ref = refs[2 * np_ + 4:]

        @pl.when(pl.program_id(0) == 0)
        def _():
            pw_ref[...] = jnp.zeros_like(pw_ref)

        dh = acc_ref[...]
        for p in range(np_):
            dh = dh + _dot(refs[p][...], refs[np_ + p][...])
        x = x_ref[...]
        r = lax.rsqrt(jnp.mean(x * x, axis=-1, keepdims=True) + NORM_EPS)
        xn = x * r
        dxn = dh * w_ref[...]
        gx_ref[...] = r * (dxn - xn * jnp.mean(dxn * xn, axis=-1, keepdims=True)) + dy_ref[...]
        pw_ref[0:1, :] += jnp.sum(dh * xn, axis=0, keepdims=True)

    row = pl.BlockSpec((tm, D_MODEL), lambda i: (i, 0))
    in_specs = [pl.BlockSpec((tm, dz.shape[1]), lambda i: (i, 0)) for dz, _ in parts]
    in_specs += [pl.BlockSpec(w.shape, lambda i: (0, 0)) for _, w in parts]
    in_specs += [row, row, row, pl.BlockSpec((1, D_MODEL), lambda i: (0, 0))]
    return pl.pallas_call(
        body, name="dh_final", grid=(t // tm,),
        in_specs=in_specs,
        out_specs=[row, pl.BlockSpec((8, D_MODEL), lambda i: (0, 0))],
        out_shape=[jax.ShapeDtypeStruct((t, D_MODEL), F32), jax.ShapeDtypeStruct((8, D_MODEL), F32)],
        compiler_params=_cparams(("arbitrary",), vmem_mb=48),
    )(*[dz for dz, _ in parts], *[w for _, w in parts], acc_in, x2, dy, wpre)


def _adamw(w, g, m, v):
    m = ADAM_B1 * m + (1.0 - ADAM_B1) * g
    v = ADAM_B2 * v + (1.0 - ADAM_B2) * (g * g)
    m_hat = m / (1.0 - ADAM_B1 ** ADAM_STEP)
    v_hat = v / (1.0 - ADAM_B2 ** ADAM_STEP)
    delta = -ADAM_LR * (m_hat / (jnp.sqrt(v_hat) + ADAM_EPS) + ADAM_WD * w)
    return delta, m, v


def _reduce_adamw(own, parts, place, w, m, v, name):
    r, c = w.shape
    blk, nblk, at = _blocks_2d(r, c)

    def body(place_ref, own_ref, p_ref, w_ref, m_ref, v_ref, g_ref, d_ref, nm_ref, nv_ref):
        mine = place_ref[1]
        own_blk = own_ref[...]
        g = jnp.where(mine == 0, own_blk, p_ref[0].astype(F32))
        for j in range(1, N_CHIPS):
            g = g + jnp.where(mine == j, own_blk, p_ref[j].astype(F32))
        d, nm, nv = _adamw(w_ref[...], g, m_ref[...], v_ref[...])
        g_ref[...] = g
        d_ref[...] = d
        nm_ref[...] = nm
        nv_ref[...] = nv

    row = pl.BlockSpec(blk, lambda i, pr: at(i))
    sh = jax.ShapeDtypeStruct((r, c), F32)
    grid_spec = pltpu.PrefetchScalarGridSpec(
        num_scalar_prefetch=1, grid=(nblk,),
        in_specs=[row, pl.BlockSpec((N_CHIPS,) + blk, lambda i, pr: (0,) + at(i)), row, row, row],
        out_specs=[row, row, row, row])
    return pl.pallas_call(
        body, name=name, grid_spec=grid_spec, out_shape=[sh, sh, sh, sh],
        compiler_params=_cparams(("parallel",)),
    )(place, own, parts, w, m, v)


def _interleave_qkv(a):
    lead = a.shape[:-1]
    return a.reshape(lead + (3, HEAD_PAIRS, LANES)).swapaxes(-3, -2).reshape(lead + (3 * D_MODEL,))


def _deinterleave_qkv(a):
    lead = a.shape[:-1]
    return a.reshape(lead + (HEAD_PAIRS, 3, LANES)).swapaxes(-3, -2).reshape(lead + (3 * D_MODEL,))


def _interleave_rows(a):
    return a.reshape(3, HEAD_PAIRS, LANES, a.shape[1]).swapaxes(0, 1).reshape(a.shape)


def _deinterleave_rows(a):
    return a.reshape(HEAD_PAIRS, 3, LANES, a.shape[1]).swapaxes(0, 1).reshape(a.shape)


def _pack_small(pre, conv_b, rg_ba, rg_bx, lam, post, loss_row, b_in, conv_w_full, rg_wa, rg_wx):
    z = jnp.zeros((1, D_MODEL), F32)
    b_used = jnp.concatenate([b_in[:, 0:3 * D_MODEL], b_in[:, 3 * D_MODEL + HEADS:IN_TOTAL]], axis=1)
    b_f = jnp.pad(b_in[:, 3 * D_MODEL:3 * D_MODEL + HEADS], ((0, 0), (0, D_MODEL - HEADS)))
    return jnp.concatenate([
        pre, conv_b, rg_ba, rg_bx, lam, post, loss_row, z,
        b_used.reshape(9, D_MODEL), b_f, conv_w_full, z, z,
        rg_wa.reshape(64, D_MODEL), rg_wx.reshape(64, D_MODEL)], axis=0)


def _unpack_small(p):
    b_used = p[8:17].reshape(1, 9 * D_MODEL)
    b_in = jnp.concatenate([b_used[:, 0:3 * D_MODEL], p[17:18, 0:HEADS], b_used[:, 3 * D_MODEL:]], axis=1)
    return dict(pre_norm_w=p[0:1], conv_b=p[1:2], rg_ba=p[2:3], rg_bx=p[3:4], rg_lambda=p[4:5],
                post_norm_w=p[5:6], loss_row=p[6:7], b_in=b_in, conv_w_full=p[18:22],
                rg_wa=p[24:88].reshape(1, 16, 64, 64), rg_wx=p[88:152].reshape(1, 16, 64, 64))


def _reduce_small(parts, w, m, v):
    def body(p_ref, w_ref, m_ref, v_ref, g_ref, d_ref, nm_ref, nv_ref):
        g = p_ref[0]
        for j in range(1, N_DEV):
            g = g + p_ref[j]
        d, nm, nv = _adamw(w_ref[...], g, m_ref[...], v_ref[...])
        g_ref[...] = g
        d_ref[...] = d
        nm_ref[...] = nm
        nv_ref[...] = nv

    sh = jax.ShapeDtypeStruct((SMALL_ROWS, D_MODEL), F32)
    return pl.pallas_call(body, name="reduce_small", out_shape=[sh, sh, sh, sh])(parts, w, m, v)


def kernel(x, pre_norm_w, w_in, b_in, conv_w, conv_b, rg_wa, rg_ba, rg_wx, rg_bx, rg_lambda, w_branch_a, w_branch_r, w_out, post_norm_w, loss_target, m_pre_norm_w, m_w_in, m_b_in, m_conv_w, m_conv_b, m_rg_wa, m_rg_ba, m_rg_wx, m_rg_bx, m_rg_lambda, m_w_branch_a, m_w_branch_r, m_w_out, m_post_norm_w, v_pre_norm_w, v_w_in, v_b_in, v_conv_w, v_conv_b, v_rg_wa, v_rg_ba, v_rg_wx, v_rg_bx, v_rg_lambda, v_w_branch_a, v_w_branch_r, v_w_out, v_post_norm_w):
    b, s, _ = x.shape
    t = b * s
    me = 4 * lax.axis_index("x") + 2 * lax.axis_index("y") + lax.axis_index("c")
    shard_rows = D_MODEL // N_DEV

    place = jnp.stack([lax.axis_index("c"), 2 * lax.axis_index("x") + lax.axis_index("y")]).astype(jnp.int32)
    w_in_all = _gather(w_in[0].T.astype(BF16), "gather_w_in")
    wt_full = w_in_all.reshape(IN_TOTAL, D_MODEL)
    conv_terms = jnp.concatenate(_split3(conv_w[0]), axis=0)
    conv_pad = jnp.pad(conv_terms, ((0, 16 - 3 * CONV_W), (0, D_MODEL - LANES)))
    sq_stack = jnp.concatenate([w_branch_a[0].astype(BF16), w_branch_r[0].astype(BF16), w_out[0].astype(BF16),
                                conv_pad], axis=0)
    sq_sems, sq_src, sq_land, sq_token = _gather_start(sq_stack, w_in_all, "gather_w_sq_start")

    w_qkv = _interleave_rows(wt_full[0:3 * D_MODEL])
    w_f = jnp.pad(wt_full[3 * D_MODEL:3 * D_MODEL + HEADS], ((0, LANES - HEADS), (0, 0)))
    w_rest = wt_full[3 * D_MODEL + HEADS:IN_USED]
    b_qkv = _interleave_qkv(b_in[:, 0:3 * D_MODEL]) + sq_token[0, 0]
    b_f = jnp.pad(b_in[:, 3 * D_MODEL:3 * D_MODEL + HEADS], ((0, 0), (0, LANES - HEADS)))
    b_rest = b_in[:, 3 * D_MODEL + HEADS:IN_USED]

    def blockdiag(w):
        w2 = w.reshape(N_CBLK, 2, HEAD_DIM, HEAD_DIM)
        zz = jnp.zeros((N_CBLK, HEAD_DIM, HEAD_DIM), w.dtype)
        top = jnp.concatenate([w2[:, 0], zz], axis=2)
        bot = jnp.concatenate([zz, w2[:, 1]], axis=2)
        return jnp.concatenate([top, bot], axis=1).astype(BF16)

    bda, bdx = blockdiag(rg_wa[0]), blockdiag(rg_wx[0])

    x2 = x.reshape(t, D_MODEL)
    tgt2 = loss_target.reshape(t, D_MODEL)
    h = _prenorm(x2, pre_norm_w)
    qkv = _mm_bias(h, w_qkv, b_qkv, BF16, "inproj_qkv")
    zrest = _mm_bias(h, w_rest, b_rest, F32, "inproj_rest")
    zf = _mm_bias(h, w_f, b_f, F32, "inproj_f")
    qkv3 = qkv.reshape(b, s, 3 * D_MODEL)
    zrest3 = zrest.reshape(b, s, 5 * D_MODEL)
    zf3 = zf.reshape(b, s, LANES)
    nq = s // ATT_TILE
    cexp3, crow = _fgate_fwd(zf3)
    crow5 = crow.reshape(b, HEAD_PAIRS, 2, nq, ATT_TILE)
    yatt3, lse5, ga3 = _attn_fwd(qkv3, cexp3, crow5, zrest3)

    sq_all = _gather_wait(sq_sems, sq_src, sq_land, ga3, "gather_w_sq_wait")
    sq_all = lax.dynamic_update_slice(sq_all, sq_stack[None], (me, 0, 0))
    wa = sq_all[:, 0:shard_rows].reshape(D_MODEL, D_MODEL)
    wr = sq_all[:, shard_rows:2 * shard_rows].reshape(D_MODEL, D_MODEL)
    wo = sq_all[:, 2 * shard_rows:3 * shard_rows].reshape(D_MODEL, D_MODEL)
    conv_all = sq_all[:, 3 * shard_rows:3 * shard_rows + 3 * CONV_W, 0:LANES].astype(F32)
    conv_all = (conv_all[:, 0:CONV_W] + conv_all[:, CONV_W:2 * CONV_W]) + conv_all[:, 2 * CONV_W:3 * CONV_W]
    conv_full = conv_all.transpose(1, 0, 2).reshape(CONV_W, D_MODEL)

    ylru3, gr3 = _rnn_fwd(zrest3, conv_full, conv_b, bda, bdx, rg_ba, rg_bx, rg_lambda)
    ga, gr = ga3.reshape(t, D_MODEL), gr3.reshape(t, D_MODEL)
    ya, yr, mm = _branch_merge(ga, gr, wa, wr, zrest)
    dy, do, acc_out = _out_loss(mm, wo, x2, tgt2, post_norm_w)

    dya, dyr, dz_mga, dz_mgr = _merge_bwd(do, wo, zrest, ya, yr)
    dyatt, dz_ga, dylru, dz_gr = _branch_bwd(dya, dyr, wa, wr, zrest, yatt3.reshape(t, D_MODEL),
                                             ylru3.reshape(t, D_MODEL))
    dz_xr3, pvec, dbd = _rnn_bwd(zrest3, ylru3, dylru.reshape(b, s, D_MODEL), conv_full, conv_b, bda, bdx,
                                 rg_ba, rg_bx, rg_lambda)
    dqkv3, dc3 = _attn_bwd(qkv3, dyatt.reshape(b, s, D_MODEL), yatt3, lse5, crow5, cexp3)
    dz_f = _fgate_bwd(dc3, zf3).reshape(t, LANES)
    dz_qkv = dqkv3.reshape(t, 3 * D_MODEL)
    dz_xr = dz_xr3.reshape(t, D_MODEL)

    dw_qkv, db_qkv = _mm_tn(dz_qkv, h, "dw_qkv")
    dw_f, db_f = _mm_tn(dz_f, h, "dw_f")
    dw_parts, db_parts = [], []
    for nm, dzp in (("ga", dz_ga), ("xr", dz_xr), ("gr", dz_gr), ("mga", dz_mga), ("mgr", dz_mgr)):
        dwp, dbp = _mm_tn(dzp, h, "dw_" + nm)
        dw_parts.append(dwp)
        db_parts.append(dbp[0:1])
    dw_a, _ = _mm_tn(ga, dya, "dw_a")
    dw_r, _ = _mm_tn(gr, dyr, "dw_r")
    dw_o, _ = _mm_tn(mm, do, "dw_o")

    zeros_tail = jnp.zeros((IN_TOTAL - IN_USED, D_MODEL), F32)
    dwt_full = jnp.concatenate([_deinterleave_rows(dw_qkv), dw_f[0:HEADS]] + dw_parts + [zeros_tail], axis=0)
    dw_in_send = dwt_full.reshape(N_CHIPS, 2, W_SHARD, D_MODEL).transpose(1, 0, 2, 3)
    by_dest = lambda a: a.reshape(N_CHIPS, 2, shard_rows, D_MODEL).transpose(1, 0, 2, 3)
    dw_sq_send = jnp.concatenate([by_dest(dw_a), by_dest(dw_r), by_dest(dw_o)], axis=2)

    sib_in, sib_sq = _swap_with_sibling([dw_in_send, dw_sq_send], "swap_dw")
    chip_in, own_in = _pair_add(dw_in_send, sib_in, place, "pair_add_in")
    chip_sq, own_sq = _pair_add(dw_sq_send, sib_sq, place, "pair_add_sq")
    sems, sent, lands, token = _exchange_chips_start([chip_in, chip_sq], "exchange_dw_start")

    wt = lambda lo: w_rest[lo * D_MODEL:(lo + 1) * D_MODEL]
    dh_a = _dh_partial([(dz_qkv, w_qkv), (dz_f, w_f)], token, "dh_qkv")
    grad_x2, acc_pre = _dh_final(
        [(dz_ga, wt(0)), (dz_xr, wt(1)), (dz_gr, wt(2)), (dz_mga, wt(3)), (dz_mgr, wt(4))],
        dh_a, x2, dy, pre_norm_w)

    db_in_full = jnp.concatenate([_deinterleave_qkv(db_qkv[0:1]), db_f[0:1, 0:HEADS]] + db_parts
                                 + [jnp.zeros((1, IN_TOTAL - IN_USED), F32)], axis=1)
    d_rg_wa = jnp.stack([dbd[:, 0, 0:HEAD_DIM, 0:HEAD_DIM], dbd[:, 0, HEAD_DIM:, HEAD_DIM:]], axis=1)
    d_rg_wx = jnp.stack([dbd[:, 1, 0:HEAD_DIM, 0:HEAD_DIM], dbd[:, 1, HEAD_DIM:, HEAD_DIM:]], axis=1)
    small_g = _pack_small(acc_pre[0:1], pvec[4:5], pvec[5:6], pvec[6:7], pvec[7:8], acc_out[0:1], acc_out[1:2],
                          db_in_full, pvec[0:4], d_rg_wa, d_rg_wx)
    sm_sems, sm_src, sm_land, sm_token = _gather_start(small_g, grad_x2, "gather_small_start")
    recv_in, recv_sq = _exchange_chips_wait(sems, sent, lands, sm_token, "exchange_dw_wait")

    g_in, d_in, nm_in, nv_in = [a.T for a in _reduce_adamw(
        own_in, recv_in, place, w_in[0].T, m_w_in[0].T, v_w_in[0].T, "adamw_w_in")]
    sq_w = jnp.concatenate([w_branch_a[0], w_branch_r[0], w_out[0]], axis=0)
    sq_m = jnp.concatenate([m_w_branch_a[0], m_w_branch_r[0], m_w_out[0]], axis=0)
    sq_v = jnp.concatenate([v_w_branch_a[0], v_w_branch_r[0], v_w_out[0]], axis=0)
    g_sq, d_sq, nm_sq, nv_sq = _reduce_adamw(own_sq, recv_sq, place, sq_w, sq_m, sq_v, "adamw_w_sq")
    small_all = _gather_wait(sm_sems, sm_src, sm_land, d_sq, "gather_small_wait")
    small_all = lax.dynamic_update_slice(small_all, small_g[None], (me, 0, 0))

    def place_conv(a):
        return lax.dynamic_update_slice(jnp.zeros((CONV_W, D_MODEL), F32), a[0], (0, me * LANES))

    zrow = jnp.zeros((1, D_MODEL), F32)
    small_w = _pack_small(pre_norm_w, conv_b, rg_ba, rg_bx, rg_lambda, post_norm_w, zrow, b_in,
                          place_conv(conv_w), rg_wa[0], rg_wx[0])
    small_m = _pack_small(m_pre_norm_w, m_conv_b, m_rg_ba, m_rg_bx, m_rg_lambda, m_post_norm_w, zrow, m_b_in,
                          place_conv(m_conv_w), m_rg_wa[0], m_rg_wx[0])
    small_v = _pack_small(v_pre_norm_w, v_conv_b, v_rg_ba, v_rg_bx, v_rg_lambda, v_post_norm_w, zrow, v_b_in,
                          place_conv(v_conv_w), v_rg_wa[0], v_rg_wx[0])
    outs_small = [_unpack_small(p) for p in _reduce_small(small_all, small_w, small_m, small_v)]

    loss = (0.5 / D_MODEL) * jnp.sum(outs_small[0]["loss_row"])

    def leaf(kind, name):
        if name == "w_in":
            return (g_in, d_in, nm_in, nv_in)[kind][None]
        if name in ("w_branch_a", "w_branch_r", "w_out"):
            j = ("w_branch_a", "w_branch_r", "w_out").index(name)
            return (g_sq, d_sq, nm_sq, nv_sq)[kind][None, j * shard_rows:(j + 1) * shard_rows]
        if name == "conv_w":
            return lax.dynamic_slice(outs_small[kind]["conv_w_full"], (0, me * LANES), (CONV_W, LANES))[None]
        return outs_small[kind][name]

    names = ["pre_norm_w", "w_in", "b_in", "conv_w", "conv_b", "rg_wa", "rg_ba", "rg_wx", "rg_bx", "rg_lambda",
             "w_branch_a", "w_branch_r", "w_out", "post_norm_w"]
    out = [loss, grad_x2.reshape(b, s, D_MODEL)]
    for kind in range(4):
        out += [leaf(kind, nm) for nm in names]
    return tuple(out)
```

```python
import jax
import jax.numpy as jnp
from jax import lax
from jax.experimental import pallas as pl
from jax.experimental.pallas import tpu as pltpu

F32 = jnp.float32
BF16 = jnp.bfloat16

N_DEV = 8
D_MODEL = 1024
HEADS = 16
HEAD_DIM = 64
HEAD_PAIRS = HEADS // 2
LANES = 128
N_CBLK = D_MODEL // LANES
CONV_W = 4
RG_C = 8.0
NORM_EPS = 1e-6
MASK_VALUE = -1e30
IN_USED = 8208
IN_TOTAL = 9232
W_SHARD = IN_TOTAL // N_DEV

ADAM_LR = 0.001
ADAM_B1 = 0.9
ADAM_B2 = 0.999
ADAM_EPS = 1e-08
ADAM_WD = 0.01
ADAM_STEP = 10

ATT_TILE = 256
SCAN_TILE = 256
SMALL_ROWS = 152


def _cparams(sem=None, vmem_mb=None):
    kw = {}
    if sem is not None:
        kw["dimension_semantics"] = sem
    if vmem_mb is not None:
        kw["vmem_limit_bytes"] = vmem_mb * 1024 * 1024
    return pltpu.CompilerParams(**kw)


def _sigmoid(x):
    return 1.0 / (1.0 + jnp.exp(-x))


def _softplus(x):
    return jnp.maximum(x, 0.0) + jnp.log1p(jnp.exp(-jnp.abs(x)))


def _one_minus_exp(y, exp_y):
    series = -y * (1.0 + y * (1.0 / 2 + y * (1.0 / 6 + y * (1.0 / 24 + y * (1.0 / 120)))))
    return jnp.where(y > -0.0625, series, 1.0 - exp_y)


def _split3(x):
    hi = x.astype(BF16)
    r1 = x - hi.astype(F32)
    mid = r1.astype(BF16)
    lo = (r1 - mid.astype(F32)).astype(BF16)
    return hi, mid, lo


def _dot(a, b):
    return jnp.dot(a, b, preferred_element_type=F32)


def _dot_nt(a, b):
    return lax.dot_general(a, b, (((1,), (1,)), ((), ())), preferred_element_type=F32)


def _dot_tn(a, b):
    return lax.dot_general(a, b, (((0,), (0,)), ((), ())), preferred_element_type=F32)


def _iota(shape, dim):
    return lax.broadcasted_iota(jnp.int32, shape, dim)


_ANY = pl.BlockSpec(memory_space=pl.ANY)
_MESH = pl.DeviceIdType.MESH
N_CHIPS = 4


def _place():
    x, y, c = lax.axis_index("x"), lax.axis_index("y"), lax.axis_index("c")
    other_chips = [(1 - x, y), (x, 1 - y), (1 - x, 1 - y)]
    return x, y, c, other_chips


def _gather(x_shard, name):
    def body(x_ref, out_ref, send_sems, recv_sems, local_sem):
        x, y, c, chips = _place()
        me, sibling = (x, y, c), (x, y, 1 - c)

        def slot(p):
            return out_ref.at[4 * p[0] + 2 * p[1] + p[2]]

        def copy(k, block, to, src=None):
            return pltpu.make_async_remote_copy(
                src_ref=slot(block) if src is None else src, dst_ref=slot(block),
                send_sem=send_sems.at[k], recv_sem=recv_sems.at[k], device_id=to, device_id_type=_MESH)

        mine = pltpu.make_async_copy(x_ref, slot(me), local_sem)
        mine.start()
        first = [copy(0, me, sibling, src=x_ref)]
        first += [copy(1 + j, me, (*chip, c), src=x_ref) for j, chip in enumerate(chips)]
        for cp in first:
            cp.start()
        passed = [copy(4 + j, (*chip, c), sibling) for j, chip in enumerate(chips)]
        for j, chip in enumerate(chips):
            copy(1 + j, (*chip, c), me).wait_recv()
            passed[j].start()
        copy(0, sibling, me).wait_recv()
        for j, chip in enumerate(chips):
            copy(4 + j, (*chip, 1 - c), me).wait_recv()
        for cp in first + passed:
            cp.wait_send()
        mine.wait()

    return pl.pallas_call(
        body, name=name,
        out_shape=jax.ShapeDtypeStruct((N_DEV,) + tuple(x_shard.shape), x_shard.dtype),
        in_specs=[_ANY], out_specs=_ANY,
        scratch_shapes=[pltpu.SemaphoreType.DMA((7,)), pltpu.SemaphoreType.DMA((7,)), pltpu.SemaphoreType.DMA],
    )(x_shard)


def _swap_with_sibling(srcs, name):
    n = len(srcs)

    def body(*refs):
        src_refs, out_refs = refs[:n], refs[n:2 * n]
        send_sems, recv_sems = refs[2 * n:]
        x, y, c, _ = _place()
        cps = [pltpu.make_async_remote_copy(
            src_ref=src_refs[i].at[1 - c], dst_ref=out_refs[i], send_sem=send_sems.at[i], recv_sem=recv_sems.at[i],
            device_id=(x, y, 1 - c), device_id_type=_MESH) for i in range(n)]
        for cp in cps:
            cp.start()
        for cp in cps:
            cp.wait()

    return pl.pallas_call(
        body, name=name,
        out_shape=[jax.ShapeDtypeStruct(a.shape[1:], a.dtype) for a in srcs],
        in_specs=[_ANY] * n, out_specs=[_ANY] * n,
        scratch_shapes=[pltpu.SemaphoreType.DMA((n,)), pltpu.SemaphoreType.DMA((n,))],
    )(*srcs)


def _blocks_2d(r, c):
    if r % 128 == 0:
        return (128, c), r // 128, lambda i: (i, 0)
    return (r, 256), c // 256, lambda i: (0, i)


def _pair_add(src, recv, place, name):
    _, _, r, c = src.shape
    blk, nblk, at = _blocks_2d(r, c)

    def body(place_ref, a_ref, b_ref, q16_ref, own_ref):
        q = a_ref[...] + b_ref[...]
        q16_ref[...] = q.astype(BF16)

        @pl.when(pl.program_id(1) == place_ref[1])
        def _():
            own_ref[...] = q

    grid_spec = pltpu.PrefetchScalarGridSpec(
        num_scalar_prefetch=1, grid=(nblk, N_CHIPS),
        in_specs=[pl.BlockSpec((None, None) + blk, lambda i, j, pr: (pr[0], j) + at(i)),
                  pl.BlockSpec((None,) + blk, lambda i, j, pr: (j,) + at(i))],
        out_specs=[pl.BlockSpec((None,) + blk, lambda i, j, pr: (j,) + at(i)),
                   pl.BlockSpec(blk, lambda i, j, pr: at(i))])
    return pl.pallas_call(
        body, name=name, grid_spec=grid_spec,
        out_shape=[jax.ShapeDtypeStruct((N_CHIPS, r, c), BF16), jax.ShapeDtypeStruct((r, c), F32)],
        compiler_params=_cparams(("parallel", "arbitrary")),
    )(place, src, recv)


_HBM = pl.BlockSpec(memory_space=pltpu.HBM)
_SEM = pl.BlockSpec(memory_space=pltpu.SEMAPHORE)
_DATAFLOW = pltpu.SideEffectType.DATAFLOW_SIDE_EFFECTING


def _chip_copy(src_ref, land_ref, send_sem, recv_sem, k, chips, c, land):
    chip = chips[k]
    return pltpu.make_async_remote_copy(
        src_ref=src_ref.at[2 * chip[0] + chip[1]], dst_ref=land_ref.at[land],
        send_sem=send_sem, recv_sem=recv_sem, device_id=(*chip, c), device_id_type=_MESH)


def _exchange_chips_start(srcs, name):
    n = len(srcs)
    ncp = 3 * n

    def body(*refs):
        src_refs, land_refs = refs[:n], refs[n:2 * n]
        sems = refs[4 * n:4 * n + 2 * ncp]
        token = refs[-1]
        x, y, c, chips = _place()
        for i in range(n):
            for k in range(3):
                j = 3 * i + k
                _chip_copy(src_refs[i], land_refs[i], sems[j], sems[ncp + j], k, chips, c, 2 * x + y).start()
        token[...] = jnp.zeros_like(token)

    hbm = [pltpu.HBM(a.shape, a.dtype) for a in srcs]
    lands = [pltpu.with_memory_space_constraint(lax.empty(a.shape, a.dtype), pltpu.HBM) for a in srcs]
    res = pl.pallas_call(
        body, name=name,
        out_shape=(*hbm, *hbm, *([pltpu.SemaphoreType.DMA(())] * (2 * ncp)), jax.ShapeDtypeStruct((8, LANES), F32)),
        in_specs=[_HBM] * (2 * n),
        out_specs=(*([_HBM] * (2 * n)), *([_SEM] * (2 * ncp)), pl.BlockSpec(memory_space=pltpu.VMEM)),
        input_output_aliases={i: i for i in range(2 * n)},
        compiler_params=pltpu.CompilerParams(has_side_effects=_DATAFLOW),
    )(*[pltpu.with_memory_space_constraint(a, pltpu.HBM) for a in srcs], *lands)
    return list(res[2 * n:2 * n + 2 * ncp]), list(res[:n]), list(res[n:2 * n]), res[-1]


def _exchange_chips_wait(sems, srcs, lands, after, name):
    n = len(srcs)
    ncp = 3 * n

    def body(*refs):
        src_refs, land_refs = refs[:n], refs[n:2 * n]
        sem_refs = refs[2 * n:2 * n + 2 * ncp]
        x, y, c, chips = _place()
        for i in range(n):
            for k in range(3):
                j = 3 * i + k
                cp = _chip_copy(src_refs[i], land_refs[i], sem_refs[j], sem_refs[ncp + j], k, chips, c,
                                2 * chips[k][0] + chips[k][1])
                cp.wait_send()
                cp.wait_recv()

    hbm = [pltpu.HBM(a.shape, a.dtype) for a in srcs]
    res = pl.pallas_call(
        body, name=name, out_shape=(*hbm, *hbm),
        in_specs=[_HBM] * (2 * n) + [_SEM] * (2 * ncp) + [_ANY], out_specs=tuple([_HBM] * (2 * n)),
        input_output_aliases={i: i for i in range(2 * n)},
        compiler_params=pltpu.CompilerParams(has_side_effects=_DATAFLOW),
    )(*srcs, *lands, *sems, after)
    return list(res[n:2 * n])


def _peer_copy(src_ref, land_ref, send_sem, recv_sem, k, place, land):
    x, y, c = place
    peer = (1 - x if k & 4 else x, 1 - y if k & 2 else y, 1 - c if k & 1 else c)
    return pltpu.make_async_remote_copy(
        src_ref=src_ref, dst_ref=land_ref.at[land], send_sem=send_sem, recv_sem=recv_sem,
        device_id=peer, device_id_type=_MESH)


def _gather_start(x_shard, after, name):
    npeer = N_DEV - 1

    def body(x_ref, land_ref, after_ref, x_thru, land_thru, *rest):
        sems, token = rest[:2 * npeer], rest[-1]
        x, y, c, _ = _place()
        for k in range(1, N_DEV):
            _peer_copy(x_ref, land_ref, sems[k - 1], sems[npeer + k - 1], k, (x, y, c), 4 * x + 2 * y + c).start()
        token[...] = jnp.zeros_like(token)

    land = pltpu.with_memory_space_constraint(lax.empty((N_DEV,) + tuple(x_shard.shape), x_shard.dtype), pltpu.HBM)
    res = pl.pallas_call(
        body, name=name,
        out_shape=(pltpu.HBM(x_shard.shape, x_shard.dtype), pltpu.HBM(land.shape, land.dtype),
                   *([pltpu.SemaphoreType.DMA(())] * (2 * npeer)), jax.ShapeDtypeStruct((8, LANES), F32)),
        in_specs=[_HBM, _HBM, _ANY],
        out_specs=(_HBM, _HBM, *([_SEM] * (2 * npeer)), pl.BlockSpec(memory_space=pltpu.VMEM)),
        input_output_aliases={0: 0, 1: 1},
        compiler_params=pltpu.CompilerParams(has_side_effects=_DATAFLOW),
    )(pltpu.with_memory_space_constraint(x_shard, pltpu.HBM), land, after)
    return list(res[2:2 + 2 * npeer]), res[0], res[1], res[-1]


def _gather_wait(sems, src, land, after, name):
    npeer = N_DEV - 1

    def body(x_ref, land_ref, *rest):
        sem_refs = rest[:2 * npeer]
        x, y, c, _ = _place()
        for k in range(1, N_DEV):
            peer_index = (4 * x + 2 * y + c) ^ k
            cp = _peer_copy(x_ref, land_ref, sem_refs[k - 1], sem_refs[npeer + k - 1], k, (x, y, c), peer_index)
            cp.wait_send()
            cp.wait_recv()

    res = pl.pallas_call(
        body, name=name, out_shape=(pltpu.HBM(src.shape, src.dtype), pltpu.HBM(land.shape, land.dtype)),
        in_specs=[_HBM, _HBM] + [_SEM] * (2 * npeer) + [_ANY], out_specs=(_HBM, _HBM),
        input_output_aliases={0: 0, 1: 1},
        compiler_params=pltpu.CompilerParams(has_side_effects=_DATAFLOW),
    )(src, land, *sems, after)
    return res[1]


def _prenorm(x2, w):
    t = x2.shape[0]
    tm = min(512, t)

    def body(x_ref, w_ref, h_ref):
        x = x_ref[...]
        r = lax.rsqrt(jnp.mean(x * x, axis=-1, keepdims=True) + NORM_EPS)
        h_ref[...] = (x * r * w_ref[...]).astype(BF16)

    return pl.pallas_call(
        body, name="prenorm", grid=(t // tm,),
        in_specs=[pl.BlockSpec((tm, D_MODEL), lambda i: (i, 0)), pl.BlockSpec((1, D_MODEL), lambda i: (0, 0))],
        out_specs=pl.BlockSpec((tm, D_MODEL), lambda i: (i, 0)),
        out_shape=jax.ShapeDtypeStruct((t, D_MODEL), BF16),
        compiler_params=_cparams(("parallel",)),
    )(x2, w)


def _mm_bias(a, bt, bias, out_dtype, name):
    m, k = a.shape
    n = bt.shape[0]
    tm = min(512, m)
    tn = min(1024, n)

    def body(a_ref, bt_ref, bias_ref, o_ref, b_scr):
        @pl.when(pl.program_id(1) == 0)
        def _():
            b_scr[...] = _transpose_bf16(bt_ref[...])

        o_ref[...] = (_dot(a_ref[...], b_scr[...]) + bias_ref[...]).astype(o_ref.dtype)

    return pl.pallas_call(
        body, name=name, grid=(n // tn, m // tm),
        in_specs=[pl.BlockSpec((tm, k), lambda j, i: (i, 0)), pl.BlockSpec((tn, k), lambda j, i: (j, 0)),
                  pl.BlockSpec((1, tn), lambda j, i: (0, j))],
        out_specs=pl.BlockSpec((tm, tn), lambda j, i: (i, j)),
        out_shape=jax.ShapeDtypeStruct((m, n), out_dtype),
        scratch_shapes=[pltpu.VMEM((k, tn), BF16)],
        compiler_params=_cparams(("parallel", "arbitrary")),
    )(a, bt, bias)


def _mm_tn(a, b, name):
    t, m = a.shape
    n = b.shape[1]
    tm = min(1024, m)
    tk = min(512, t)

    def body(a_ref, b_ref, o_ref, s_ref):
        kk = pl.program_id(1)

        @pl.when(kk == 0)
        def _():
            o_ref[...] = jnp.zeros_like(o_ref)
            s_ref[...] = jnp.zeros_like(s_ref)

        aa = a_ref[...]
        o_ref[...] += _dot_tn(aa, b_ref[...])
        s_ref[0:1, :] += jnp.sum(aa.astype(F32), axis=0, keepdims=True)

    return pl.pallas_call(
        body, name=name, grid=(m // tm, t // tk),
        in_specs=[pl.BlockSpec((tk, tm), lambda i, kk: (kk, i)), pl.BlockSpec((tk, n), lambda i, kk: (kk, 0))],
        out_specs=[pl.BlockSpec((tm, n), lambda i, kk: (i, 0)), pl.BlockSpec((8, tm), lambda i, kk: (0, i))],
        out_shape=[jax.ShapeDtypeStruct((m, n), F32), jax.ShapeDtypeStruct((8, m), F32)],
        compiler_params=_cparams(("parallel", "arbitrary")),
    )(a, b)


def _fgate_fwd(zf3):
    b, s, _ = zf3.shape
    tb = SCAN_TILE
    nb = s // tb

    def body(z_ref, cexp_ref, crow_ref):
        tri = (_iota((tb, tb), 1) <= _iota((tb, tb), 0)).astype(BF16)
        expand = ((_iota((LANES, D_MODEL), 1) >> 6) == _iota((LANES, D_MODEL), 0)).astype(BF16)
        carry = jnp.zeros((1, LANES), F32)
        for i in range(nb):
            rows = slice(i * tb, (i + 1) * tb)
            z = z_ref[rows, :]
            lf = jnp.minimum(z, 0.0) - jnp.log1p(jnp.exp(-jnp.abs(z)))
            cb = sum(_dot(tri, part) for part in _split3(lf)) + carry
            carry = cb[tb - 1:tb, :]
            cexp_ref[rows, :] = sum(_dot(part, expand) for part in _split3(cb))
            crow_ref[:, rows] = cb.T[0:HEADS, :]

    return pl.pallas_call(
        body, name="fgate_fwd", grid=(b,),
        in_specs=[pl.BlockSpec((None, s, LANES), lambda i: (i, 0, 0))],
        out_specs=[pl.BlockSpec((None, s, D_MODEL), lambda i: (i, 0, 0)),
                   pl.BlockSpec((None, HEADS, s), lambda i: (i, 0, 0))],
        out_shape=[jax.ShapeDtypeStruct((b, s, D_MODEL), F32), jax.ShapeDtypeStruct((b, HEADS, s), F32)],
        compiler_params=_cparams(("parallel",)),
    )(zf3)


def _fgate_bwd(dc3, zf3):
    b, s, _ = zf3.shape
    tb = SCAN_TILE
    nb = s // tb

    def body(dc_ref, z_ref, o_ref):
        tri = (_iota((tb, tb), 1) >= _iota((tb, tb), 0)).astype(BF16)
        carry = jnp.zeros((1, LANES), F32)
        for i in reversed(range(nb)):
            rows = slice(i * tb, (i + 1) * tb)
            dlf = sum(_dot(tri, part) for part in _split3(dc_ref[rows, :])) + carry
            carry = dlf[0:1, :]
            o_ref[rows, :] = (dlf * _sigmoid(-z_ref[rows, :])).astype(BF16)

    return pl.pallas_call(
        body, name="fgate_bwd", grid=(b,),
        in_specs=[pl.BlockSpec((None, s, LANES), lambda i: (i, 0, 0)),
                  pl.BlockSpec((None, s, LANES), lambda i: (i, 0, 0))],
        out_specs=pl.BlockSpec((None, s, LANES), lambda i: (i, 0, 0)),
        out_shape=jax.ShapeDtypeStruct((b, s, LANES), BF16),
        compiler_params=_cparams(("parallel",)),
    )(dc3, zf3)


def _spare(hh):
    return HEAD_DIM if hh == 0 else 0


def _put_cols(tile, mine, cols, first):
    lane = _iota((1, LANES), 1)
    out = jnp.where(mine, tile, jnp.zeros((), tile.dtype))
    for j, c in enumerate(cols):
        out = jnp.where(lane == first + j, c, out)
    return out


def _put_rows(tile, mine, rows, first):
    sub = _iota((LANES, 1), 0)
    out = jnp.where(mine, tile, jnp.zeros((), tile.dtype))
    for j, r in enumerate(rows):
        out = jnp.where(sub == first + j, r, out)
    return out


def _transpose_bf16(a):
    return a.astype(F32).T.astype(BF16)


def _attn_fwd(qkv3, cexp3, crow5, zrest3):
    b, s, _ = qkv3.shape
    ta = ATT_TILE
    nq = s // ta
    hd = HEAD_DIM

    def body(qkv_ref, cq_ref, ck_ref, g_ref, y_ref, lse_ref, ga_ref, kt_scr, v_scr):
        lane = _iota((1, LANES), 1)
        sub = _iota((LANES, 1), 0)
        lane_mine = (lane < hd, lane >= hd)
        sub_mine = (sub < hd, sub >= hd)
        causal = _iota((ta, ta), 0) >= _iota((ta, ta), 1)
        one = jnp.ones((), BF16)

        for kj in range(nq):
            rows = slice(kj * ta, (kj + 1) * ta)
            kt = _transpose_bf16(qkv_ref[rows, LANES:2 * LANES])
            v = qkv_ref[rows, 2 * LANES:3 * LANES]
            for hh in range(2):
                ck = list(_split3(-ck_ref[hh, kj:kj + 1, :]))
                kt_scr[hh, kj] = _put_rows(kt, sub_mine[hh], [one, one, one] + ck, _spare(hh))
                v_scr[hh, kj] = _put_cols(v, lane_mine[hh], [one], _spare(hh))

        for qi in range(nq):
            rows = slice(qi * ta, (qi + 1) * ta)
            q = qkv_ref[rows, 0:LANES] * 0.125
            cq = cq_ref[rows, :]
            qh = [_put_cols(q, lane_mine[hh], list(_split3(cq[:, hh * hd:hh * hd + 1])) + [one, one, one], _spare(hh))
                  for hh in range(2)]
            st = [(jnp.full((ta, 1), MASK_VALUE, F32), jnp.zeros((ta, LANES), F32))] * 2
            for kj in range(qi + 1):
                for hh in range(2):
                    m, acc = st[hh]
                    sc = _dot(qh[hh], kt_scr[hh, kj])
                    if kj == qi:
                        sc = jnp.where(causal, sc, MASK_VALUE)
                    mn = jnp.maximum(m, jnp.max(sc, axis=-1, keepdims=True))
                    p = jnp.exp(sc - mn).astype(BF16)
                    st[hh] = (mn, jnp.exp(m - mn) * acc + _dot(p, v_scr[hh, kj]))
            (ma, acca), (mb, accb) = st
            la = acca[:, hd:hd + 1]
            lb = accb[:, 0:1]
            y = jnp.where(lane_mine[0], acca * (1.0 / la), accb * (1.0 / lb))
            lse = jnp.where(lane_mine[0], ma + jnp.log(la), mb + jnp.log(lb)).T
            lse_ref[0, qi:qi + 1, :] = lse[0:1, :]
            lse_ref[1, qi:qi + 1, :] = lse[hd:hd + 1, :]
            y_ref[rows, :] = y
            g = g_ref[rows, :]
            ga_ref[rows, :] = (y * (g * _sigmoid(g))).astype(BF16)

    blk = lambda w: pl.BlockSpec((None, s, w), lambda i, p: (i, 0, p))
    rows5 = pl.BlockSpec((None, None, 2, nq, ta), lambda i, p: (i, p, 0, 0, 0))
    return pl.pallas_call(
        body, name="attn_fwd", grid=(b, HEAD_PAIRS),
        in_specs=[blk(3 * LANES), blk(LANES), rows5, blk(LANES)],
        out_specs=[blk(LANES), rows5, blk(LANES)],
        out_shape=[jax.ShapeDtypeStruct((b, s, D_MODEL), F32),
                   jax.ShapeDtypeStruct((b, HEAD_PAIRS, 2, nq, ta), F32),
                   jax.ShapeDtypeStruct((b, s, D_MODEL), BF16)],
        scratch_shapes=[pltpu.VMEM((2, nq, LANES, ta), BF16), pltpu.VMEM((2, nq, ta, LANES), BF16)],
        compiler_params=_cparams(("parallel", "parallel")),
    )(qkv3, cexp3, crow5, zrest3)


def _attn_bwd(qkv3, do3, y3, lse5, crow5, cexp3):
    b, s, _ = qkv3.shape
    ta = ATT_TILE
    nq = s // ta
    hd = HEAD_DIM

    def body(qkv_ref, do_ref, y_ref, lse_ref, crow_ref, cexp_ref, dqkv_ref, dc_ref,
             qa_scr, doa_scr, qst_scr, dot_scr, kt_scr, vt_scr, dq_scr, rs_scr):
        pair = pl.program_id(1)
        lane = _iota((1, LANES), 1)
        sub = _iota((LANES, 1), 0)
        lane_mine = (lane < hd, lane >= hd)
        sub_mine = (sub < hd, sub >= hd)
        causal = _iota((ta, ta), 0) >= _iota((ta, ta), 1)
        one = jnp.ones((), BF16)
        zero = jnp.zeros((), BF16)

        @pl.when(pair == 0)
        def _():
            dc_ref[...] = jnp.zeros_like(dc_ref)

        for i in range(nq):
            rows = slice(i * ta, (i + 1) * ta)
            qs = qkv_ref[rows, 0:LANES] * 0.125
            qst = _transpose_bf16(qs)
            kt = _transpose_bf16(qkv_ref[rows, LANES:2 * LANES])
            vt = _transpose_bf16(qkv_ref[rows, 2 * LANES:3 * LANES])
            do = do_ref[rows, :]
            dof = do.astype(F32)
            dot = dof.T.astype(BF16)
            pr = y_ref[rows, :] * dof
            cq = cexp_ref[rows, :]
            lse_c = jnp.where(sub == 0, lse_ref[0, i:i + 1, :],
                              jnp.where(sub == 1, lse_ref[1, i:i + 1, :], 0.0)).T
            for hh in range(2):
                sp = _spare(hh)
                dsum = jnp.sum(jnp.where(lane_mine[hh], pr, 0.0), axis=-1, keepdims=True)
                bias = cq[:, hh * hd:hh * hd + 1] - lse_c[:, hh:hh + 1]
                qa_scr[hh, i] = _put_cols(qs, lane_mine[hh], list(_split3(bias)) + [one, one, one], sp)
                doa_scr[hh, i] = _put_cols(do, lane_mine[hh], list(_split3(-dsum)), sp)
                qst_scr[hh, i] = jnp.where(sub_mine[hh], qst, zero)
                dot_scr[hh, i] = jnp.where(sub_mine[hh], dot, zero)
                ck = list(_split3(-crow_ref[hh, i:i + 1, :]))
                kt_scr[hh, i] = _put_rows(kt, sub_mine[hh], [one, one, one] + ck, sp)
                vt_scr[hh, i] = _put_rows(vt, sub_mine[hh], [one, one, one], sp)
            dq_scr[i] = jnp.zeros((ta, LANES), F32)
            rs_scr[i] = jnp.zeros((ta, LANES), F32)

        for kj in range(nq):
            krows = slice(kj * ta, (kj + 1) * ta)
            k = qkv_ref[krows, LANES:2 * LANES]
            km = (jnp.where(lane_mine[0], k, zero), jnp.where(lane_mine[1], k, zero))
            dkt = jnp.zeros((LANES, ta), F32)
            dvt = jnp.zeros((LANES, ta), F32)
            dcp = [jnp.zeros((8, ta), F32), jnp.zeros((8, ta), F32)]
            for qi in range(kj, nq):
                dq = jnp.zeros((ta, LANES), F32)
                rs = []
                for hh in range(2):
                    sc = _dot(qa_scr[hh, qi], kt_scr[hh, kj])
                    if qi == kj:
                        sc = jnp.where(causal, sc, MASK_VALUE)
                    p = jnp.exp(sc)
                    dsf = p * _dot(doa_scr[hh, qi], vt_scr[hh, kj])
                    dcp[hh] = dcp[hh] + jnp.sum(dsf.reshape(ta // 8, 8, ta), axis=0)
                    rs.append(jnp.sum(dsf, axis=-1, keepdims=True))
                    ds = dsf.astype(BF16)
                    dq = dq + _dot(ds, km[hh])
                    dkt = dkt + _dot(qst_scr[hh, qi], ds)
                    dvt = dvt + _dot(dot_scr[hh, qi], p.astype(BF16))
                dq_scr[qi] += dq
                rs_scr[qi] += jnp.where(lane == 0, rs[0], jnp.where(lane == 1, rs[1], 0.0))
            dqkv_ref[krows, LANES:2 * LANES] = dkt.T.astype(BF16)
            dqkv_ref[krows, 2 * LANES:3 * LANES] = dvt.T.astype(BF16)
            dca = jnp.sum(dcp[0], axis=0, keepdims=True)
            dcb = jnp.sum(dcp[1], axis=0, keepdims=True)
            dcs = jnp.where(sub == 0, dca, jnp.where(sub == 1, dcb, 0.0)).T
            dc_ref[krows, :] += (jnp.where(lane == 2 * pair, -dcs[:, 0:1], 0.0)
                                 + jnp.where(lane == 2 * pair + 1, -dcs[:, 1:2], 0.0))
        for qi in range(nq):
            rows = slice(qi * ta, (qi + 1) * ta)
            dqkv_ref[rows, 0:LANES] = (dq_scr[qi] * 0.125).astype(BF16)
            rq = rs_scr[qi]
            dc_ref[rows, :] += (jnp.where(lane == 2 * pair, rq[:, 0:1], 0.0)
                                + jnp.where(lane == 2 * pair + 1, rq[:, 1:2], 0.0))

    blk = lambda w: pl.BlockSpec((None, s, w), lambda i, p: (i, 0, p))
    rows5 = pl.BlockSpec((None, None, 2, nq, ta), lambda i, p: (i, p, 0, 0, 0))
    by_rows = lambda: pltpu.VMEM((2, nq, ta, LANES), BF16)
    by_cols = lambda: pltpu.VMEM((2, nq, LANES, ta), BF16)
    return pl.pallas_call(
        body, name="attn_bwd", grid=(b, HEAD_PAIRS),
        in_specs=[blk(3 * LANES), blk(LANES), blk(LANES), rows5, rows5, blk(LANES)],
        out_specs=[blk(3 * LANES), pl.BlockSpec((None, s, LANES), lambda i, p: (i, 0, 0))],
        out_shape=[jax.ShapeDtypeStruct((b, s, 3 * D_MODEL), BF16), jax.ShapeDtypeStruct((b, s, LANES), F32)],
        scratch_shapes=[by_rows(), by_rows(), by_cols(), by_cols(), by_cols(), by_cols(),
                        pltpu.VMEM((nq, ta, LANES), F32), pltpu.VMEM((nq, ta, LANES), F32)],
        compiler_params=_cparams(("parallel", "arbitrary")),
    )(qkv3, do3, y3, lse5, crow5, cexp3)


def _rnn_common(xr, cw_ref, cb_ref, bda_ref, bdx_ref, ba_ref, bx_ref, lam_ref, s):
    rows = _iota((s, LANES), 0)

    def down(v, k):
        return jnp.where(rows >= k, pltpu.roll(v, k, 0), 0.0)

    x1, x2, x3 = down(xr, 1), down(xr, 2), down(xr, 3)
    xc = cb_ref[...] + cw_ref[0:1, :] * x3
    xc = xc + cw_ref[1:2, :] * x2
    xc = xc + cw_ref[2:3, :] * x1
    xc = xc + cw_ref[3:4, :] * xr
    xcb = xc.astype(BF16)
    r = _sigmoid(_dot(xcb, bda_ref[...]) + ba_ref[...])
    i = _sigmoid(_dot(xcb, bdx_ref[...]) + bx_ref[...])
    sp = _softplus(-lam_ref[...])
    log_a = (-RG_C * r) * sp
    a = jnp.exp(log_a)
    a2 = a * a
    sq = jnp.sqrt(jnp.maximum(_one_minus_exp(2.0 * log_a, a2), 0.0))
    return rows, (x1, x2, x3), xc, xcb, r, i, sp, a, a2, sq


def _scan_down(a, u, rows, s, s1, s2):
    low = rows & 7
    for sh in (1, 2, 4):
        keep = low >= sh
        u = u + a * jnp.where(keep, pltpu.roll(u, sh, 0), 0.0)
        a = a * jnp.where(keep, pltpu.roll(a, sh, 0), 1.0)
    ng = s // 8
    s1[...] = a
    s2[...] = u
    at = s1[pl.ds(7, ng, stride=8), :]
    ut = s2[pl.ds(7, ng, stride=8), :]
    grow = _iota((ng, LANES), 0)
    sh = 1
    while sh < ng:
        keep = grow >= sh
        ut = ut + at * jnp.where(keep, pltpu.roll(ut, sh, 0), 0.0)
        if sh * 2 < ng:
            at = at * jnp.where(keep, pltpu.roll(at, sh, 0), 1.0)
        sh *= 2
    h_in = jnp.where(grow >= 1, pltpu.roll(ut, 1, 0), 0.0)
    for k in range(8):
        s1[pl.ds(k, ng, stride=8), :] = h_in
    return u + a * s1[...]


def _scan_up(a, g, rows, s, s1, s2):
    low = rows & 7
    for sh in (1, 2, 4):
        keep = low < 8 - sh
        g = g + a * jnp.where(keep, pltpu.roll(g, s - sh, 0), 0.0)
        a = a * jnp.where(keep, pltpu.roll(a, s - sh, 0), 1.0)
    ng = s // 8
    s1[...] = a
    s2[...] = g
    at = s1[pl.ds(0, ng, stride=8), :]
    gt = s2[pl.ds(0, ng, stride=8), :]
    grow = _iota((ng, LANES), 0)
    sh = 1
    while sh < ng:
        keep = grow < ng - sh
        gt = gt + at * jnp.where(keep, pltpu.roll(gt, ng - sh, 0), 0.0)
        if sh * 2 < ng:
            at = at * jnp.where(keep, pltpu.roll(at, ng - sh, 0), 1.0)
        sh *= 2
    g_in = jnp.where(grow < ng - 1, pltpu.roll(gt, ng - 1, 0), 0.0)
    for k in range(8):
        s1[pl.ds(k, ng, stride=8), :] = g_in
    return g + a * s1[...]


def _rnn_specs(s):
    blk = lambda off: pl.BlockSpec((None, s, LANES), lambda cb, i: (i, 0, off + cb))
    vec = lambda r: pl.BlockSpec((r, LANES), lambda cb, i: (0, cb))
    mat = pl.BlockSpec((None, LANES, LANES), lambda cb, i: (cb, 0, 0))
    return blk, vec, mat


def _rnn_fwd(zrest3, conv_w, conv_b, bda, bdx, ba, bx, lam):
    b, s, _ = zrest3.shape

    def body(xr_ref, g_ref, cw_ref, cb_ref, bda_ref, bdx_ref, ba_ref, bx_ref, lam_ref, h_ref, gr_ref, s1, s2):
        xr = xr_ref[...]
        rows, _, xc, _, _, i, _, a, _, sq = _rnn_common(
            xr, cw_ref, cb_ref, bda_ref, bdx_ref, ba_ref, bx_ref, lam_ref, s)
        h = _scan_down(a, sq * (i * xc), rows, s, s1, s2)
        h_ref[...] = h
        g = g_ref[...]
        gr_ref[...] = (h * (g * _sigmoid(g))).astype(BF16)

    blk, vec, mat = _rnn_specs(s)
    return pl.pallas_call(
        body, name="rnn_fwd", grid=(N_CBLK, b),
        in_specs=[blk(N_CBLK), blk(2 * N_CBLK), vec(CONV_W), vec(1), mat, mat, vec(1), vec(1), vec(1)],
        out_specs=[blk(0), blk(0)],
        out_shape=[jax.ShapeDtypeStruct((b, s, D_MODEL), F32), jax.ShapeDtypeStruct((b, s, D_MODEL), BF16)],
        scratch_shapes=[pltpu.VMEM((s, LANES), F32), pltpu.VMEM((s, LANES), F32)],
        compiler_params=_cparams(("parallel", "parallel")),
    )(zrest3, zrest3, conv_w, conv_b, bda, bdx, ba, bx, lam)


def _rnn_bwd(zrest3, h3, dh3, conv_w, conv_b, bda, bdx, ba, bx, lam):
    b, s, _ = zrest3.shape

    def body(xr_ref, h_ref, dh_ref, cw_ref, cb_ref, bda_ref, bdx_ref, ba_ref, bx_ref, lam_ref,
             dxr_ref, pv_ref, dbd_ref, s1, s2):
        @pl.when(pl.program_id(1) == 0)
        def _():
            pv_ref[...] = jnp.zeros_like(pv_ref)
            dbd_ref[...] = jnp.zeros_like(dbd_ref)

        xr = xr_ref[...]
        rows, (x1, x2, x3), xc, xcb, r, i, sp, a, a2, sq = _rnn_common(
            xr, cw_ref, cb_ref, bda_ref, bdx_ref, ba_ref, bx_ref, lam_ref, s)
        h = h_ref[...]
        a_next = jnp.where(rows < s - 1, pltpu.roll(a, s - 1, 0), 0.0)
        g = _scan_up(a_next, dh_ref[...], rows, s, s1, s2)
        hp = jnp.where(rows >= 1, pltpu.roll(h, 1, 0), 0.0)
        da = g * hp
        dsq = g * (i * xc)
        di = g * (sq * xc)
        dxc = g * (sq * i)
        dlog = da * a - dsq * (a2 / sq)
        dr = dlog * (-RG_C * sp)
        dpr = dr * (r * (1.0 - r))
        dpi = di * (i * (1.0 - i))
        dprb = dpr.astype(BF16)
        dpib = dpi.astype(BF16)
        dxc = dxc + _dot_nt(dprb, bda_ref[...]) + _dot_nt(dpib, bdx_ref[...])

        def up(v, k):
            return jnp.where(rows < s - k, pltpu.roll(v, s - k, 0), 0.0)

        dxr = cw_ref[3:4, :] * dxc + cw_ref[2:3, :] * up(dxc, 1) + cw_ref[1:2, :] * up(dxc, 2) \
            + cw_ref[0:1, :] * up(dxc, 3)
        dxr_ref[...] = dxr.astype(BF16)

        def colsum(v):
            return jnp.sum(v, axis=0, keepdims=True)

        pv_ref[0:1, :] += colsum(dxc * x3)
        pv_ref[1:2, :] += colsum(dxc * x2)
        pv_ref[2:3, :] += colsum(dxc * x1)
        pv_ref[3:4, :] += colsum(dxc * xr)
        pv_ref[4:5, :] += colsum(dxc)
        pv_ref[5:6, :] += colsum(dpr)
        pv_ref[6:7, :] += colsum(dpi)
        pv_ref[7:8, :] += colsum(dlog * r) * (RG_C * _sigmoid(-lam_ref[...]))
        dbd_ref[0] += _dot_tn(xcb, dprb)
        dbd_ref[1] += _dot_tn(xcb, dpib)

    blk, vec, mat = _rnn_specs(s)
    hblk = pl.BlockSpec((None, s, LANES), lambda cb, i: (i, 0, cb))
    return pl.pallas_call(
        body, name="rnn_bwd", grid=(N_CBLK, b),
        in_specs=[blk(N_CBLK), hblk, hblk, vec(CONV_W), vec(1), mat, mat, vec(1), vec(1), vec(1)],
        out_specs=[hblk, pl.BlockSpec((8, LANES), lambda cb, i: (0, cb)),
                   pl.BlockSpec((None, 2, LANES, LANES), lambda cb, i: (cb, 0, 0, 0))],
        out_shape=[jax.ShapeDtypeStruct((b, s, D_MODEL), BF16), jax.ShapeDtypeStruct((8, D_MODEL), F32),
                   jax.ShapeDtypeStruct((N_CBLK, 2, LANES, LANES), F32)],
        scratch_shapes=[pltpu.VMEM((s, LANES), F32), pltpu.VMEM((s, LANES), F32)],
        compiler_params=_cparams(("parallel", "arbitrary")),
    )(zrest3, h3, dh3, conv_w, conv_b, bda, bdx, ba, bx, lam)


def _branch_merge(ga, gr, wa, wr, zrest):
    t = ga.shape[0]
    tm = min(512, t)
    tn = 512

    def body(ga_ref, gr_ref, wa_ref, wr_ref, mga_ref, mgr_ref, ya_ref, yr_ref, m_ref):
        ya = _dot(ga_ref[...], wa_ref[...])
        yr = _dot(gr_ref[...], wr_ref[...])
        ya_ref[...] = ya
        yr_ref[...] = yr
        m_ref[...] = (_sigmoid(mga_ref[...]) * ya + _sigmoid(mgr_ref[...]) * yr).astype(BF16)

    nj = D_MODEL // tn
    act = pl.BlockSpec((tm, D_MODEL), lambda i, j: (i, 0))
    wgt = pl.BlockSpec((D_MODEL, tn), lambda i, j: (0, j))
    out = pl.BlockSpec((tm, tn), lambda i, j: (i, j))
    return pl.pallas_call(
        body, name="branch_merge", grid=(t // tm, nj),
        in_specs=[act, act, wgt, wgt, pl.BlockSpec((tm, tn), lambda i, j: (i, 3 * nj + j)),
                  pl.BlockSpec((tm, tn), lambda i, j: (i, 4 * nj + j))],
        out_specs=[out, out, out],
        out_shape=[jax.ShapeDtypeStruct((t, D_MODEL), F32), jax.ShapeDtypeStruct((t, D_MODEL), F32),
                   jax.ShapeDtypeStruct((t, D_MODEL), BF16)],
        compiler_params=_cparams(("parallel", "parallel")),
    )(ga, gr, wa, wr, zrest, zrest)


def _out_loss(m, wout, x2, tgt2, wpost):
    t = m.shape[0]
    tm = min(256, t)

    def body(m_ref, w_ref, x_ref, t_ref, wp_ref, dy_ref, do_ref, acc_ref):
        @pl.when(pl.program_id(0) == 0)
        def _():
            acc_ref[...] = jnp.zeros_like(acc_ref)

        o = _dot(m_ref[...], w_ref[...])
        r2 = lax.rsqrt(jnp.mean(o * o, axis=-1, keepdims=True) + NORM_EPS)
        n = o * r2
        wp = wp_ref[...]
        err = (x_ref[...] + n * wp) - t_ref[...]
        dy = err * (1.0 / D_MODEL)
        dn = dy * wp
        do = r2 * (dn - n * jnp.mean(dn * n, axis=-1, keepdims=True))
        dy_ref[...] = dy
        do_ref[...] = do.astype(BF16)
        acc_ref[0:1, :] += jnp.sum(dy * n, axis=0, keepdims=True)
        acc_ref[1:2, :] += jnp.sum(err * err, axis=0, keepdims=True)

    row = pl.BlockSpec((tm, D_MODEL), lambda i: (i, 0))
    return pl.pallas_call(
        body, name="out_loss", grid=(t // tm,),
        in_specs=[row, pl.BlockSpec((D_MODEL, D_MODEL), lambda i: (0, 0)), row, row,
                  pl.BlockSpec((1, D_MODEL), lambda i: (0, 0))],
        out_specs=[row, row, pl.BlockSpec((8, D_MODEL), lambda i: (0, 0))],
        out_shape=[jax.ShapeDtypeStruct((t, D_MODEL), F32), jax.ShapeDtypeStruct((t, D_MODEL), BF16),
                   jax.ShapeDtypeStruct((8, D_MODEL), F32)],
        compiler_params=_cparams(("arbitrary",)),
    )(m, wout, x2, tgt2, wpost)


def _merge_bwd(do, wout, zrest, ya, yr):
    t = do.shape[0]
    tm = min(512, t)
    tn = 512
    nj = D_MODEL // tn

    def body(do_ref, w_ref, mga_ref, mgr_ref, ya_ref, yr_ref, dya_ref, dyr_ref, dmga_ref, dmgr_ref):
        dm = _dot_nt(do_ref[...], w_ref[...])
        sa = _sigmoid(mga_ref[...])
        sr = _sigmoid(mgr_ref[...])
        dya_ref[...] = (dm * sa).astype(BF16)
        dyr_ref[...] = (dm * sr).astype(BF16)
        dmga_ref[...] = (dm * ya_ref[...] * (sa * (1.0 - sa))).astype(BF16)
        dmgr_ref[...] = (dm * yr_ref[...] * (sr * (1.0 - sr))).astype(BF16)

    out = pl.BlockSpec((tm, tn), lambda i, j: (i, j))
    bf = jax.ShapeDtypeStruct((t, D_MODEL), BF16)
    return pl.pallas_call(
        body, name="merge_bwd", grid=(t // tm, nj),
        in_specs=[pl.BlockSpec((tm, D_MODEL), lambda i, j: (i, 0)), pl.BlockSpec((tn, D_MODEL), lambda i, j: (j, 0)),
                  pl.BlockSpec((tm, tn), lambda i, j: (i, 3 * nj + j)),
                  pl.BlockSpec((tm, tn), lambda i, j: (i, 4 * nj + j)), out, out],
        out_specs=[out, out, out, out],
        out_shape=[bf, bf, bf, bf],
        compiler_params=_cparams(("parallel", "parallel")),
    )(do, wout, zrest, zrest, ya, yr)


def _branch_bwd(dya, dyr, wa, wr, zrest, yatt, ylru):
    t = dya.shape[0]
    tm = min(512, t)
    tn = 512
    nj = D_MODEL // tn

    def body(dya_ref, dyr_ref, wa_ref, wr_ref, ga_ref, gr_ref, ya_ref, yl_ref,
             dyatt_ref, dga_ref, dyl_ref, dgr_ref):
        dga = _dot_nt(dya_ref[...], wa_ref[...])
        dgr = _dot_nt(dyr_ref[...], wr_ref[...])
        g = ga_ref[...]
        sg = _sigmoid(g)
        dyatt_ref[...] = (dga * (g * sg)).astype(BF16)
        dga_ref[...] = (dga * ya_ref[...] * (sg * (1.0 + g * (1.0 - sg)))).astype(BF16)
        g = gr_ref[...]
        sg = _sigmoid(g)
        dyl_ref[...] = dgr * (g * sg)
        dgr_ref[...] = (dgr * yl_ref[...] * (sg * (1.0 + g * (1.0 - sg)))).astype(BF16)

    act = pl.BlockSpec((tm, D_MODEL), lambda i, j: (i, 0))
    wgt = pl.BlockSpec((tn, D_MODEL), lambda i, j: (j, 0))
    out = pl.BlockSpec((tm, tn), lambda i, j: (i, j))
    bf = jax.ShapeDtypeStruct((t, D_MODEL), BF16)
    return pl.pallas_call(
        body, name="branch_bwd", grid=(t // tm, nj),
        in_specs=[act, act, wgt, wgt, pl.BlockSpec((tm, tn), lambda i, j: (i, j)),
                  pl.BlockSpec((tm, tn), lambda i, j: (i, 2 * nj + j)), out, out],
        out_specs=[out, out, out, out],
        out_shape=[bf, bf, jax.ShapeDtypeStruct((t, D_MODEL), F32), bf],
        compiler_params=_cparams(("parallel", "parallel")),
    )(dya, dyr, wa, wr, zrest, zrest, yatt, ylru)


def _dh_partial(parts, after, name):
    t = parts[0][0].shape[0]
    tm = min(256, t)
    np_ = len(parts)

    def body(*refs):
        o_ref = refs[-1]
        acc = _dot(refs[0][...], refs[np_][...])
        for p in range(1, np_):
            acc = acc + _dot(refs[p][...], refs[np_ + p][...])
        o_ref[...] = acc

    in_specs = [pl.BlockSpec((tm, dz.shape[1]), lambda i: (i, 0)) for dz, _ in parts]
    in_specs += [pl.BlockSpec(w.shape, lambda i: (0, 0)) for _, w in parts]
    in_specs += [pl.BlockSpec(after.shape, lambda i: (0, 0))]
    return pl.pallas_call(
        body, name=name, grid=(t // tm,),
        in_specs=in_specs,
        out_specs=pl.BlockSpec((tm, D_MODEL), lambda i: (i, 0)),
        out_shape=jax.ShapeDtypeStruct((t, D_MODEL), F32),
        compiler_params=_cparams(("parallel",), vmem_mb=48),
    )(*[dz for dz, _ in parts], *[w for _, w in parts], after)


def _dh_final(parts, acc_in, x2, dy, wpre):
    t = x2.shape[0]
    tm = min(256, t)
    np_ = len(parts)

    def body(*refs):
        acc_ref, x_ref, dy_ref, w_ref = refs[2 * np_:2 * np_ + 4]
        gx_ref, pw_ref = refs[2 * np_ + 4:]

        @pl.when(pl.program_id(0) == 0)
        def _():
            pw_ref[...] = jnp.zeros_like(pw_ref)

        dh = acc_ref[...]
        for p in range(np_):
            dh = dh + _dot(refs[p][...], refs[np_ + p][...])
        x = x_ref[...]
        r = lax.rsqrt(jnp.mean(x * x, axis=-1, keepdims=True) + NORM_EPS)
        xn = x * r
        dxn = dh * w_ref[...]
        gx_ref[...] = r * (dxn - xn * jnp.mean(dxn * xn, axis=-1, keepdims=True)) + dy_ref[...]
        pw_ref[0:1, :] += jnp.sum(dh * xn, axis=0, keepdims=True)

    row = pl.BlockSpec((tm, D_MODEL), lambda i: (i, 0))
    in_specs = [pl.BlockSpec((tm, dz.shape[1]), lambda i: (i, 0)) for dz, _ in parts]
    in_specs += [pl.BlockSpec(w.shape, lambda i: (0, 0)) for _, w in parts]
    in_specs += [row, row, row, pl.BlockSpec((1, D_MODEL), lambda i: (0, 0))]
    return pl.pallas_call(
        body, name="dh_final", grid=(t // tm,),
        in_specs=in_specs,
        out_specs=[row, pl.BlockSpec((8, D_MODEL), lambda i: (0, 0))],
        out_shape=[jax.ShapeDtypeStruct((t, D_MODEL), F32), jax.ShapeDtypeStruct((8, D_MODEL), F32)],
        compiler_params=_cparams(("arbitrary",), vmem_mb=48),
    )(*[dz for dz, _ in parts], *[w for _, w in parts], acc_in, x2, dy, wpre)


def _adamw(w, g, m, v):
    m = ADAM_B1 * m + (1.0 - ADAM_B1) * g
    v = ADAM_B2 * v + (1.0 - ADAM_B2) * (g * g)
    m_hat = m / (1.0 - ADAM_B1 ** ADAM_STEP)
    v_hat = v / (1.0 - ADAM_B2 ** ADAM_STEP)
    delta = -ADAM_LR * (m_hat / (jnp.sqrt(v_hat) + ADAM_EPS) + ADAM_WD * w)
    return delta, m, v


def _reduce_adamw(own, parts, place, w, m, v, name):
    r, c = w.shape
    blk, nblk, at = _blocks_2d(r, c)

    def body(place_ref, own_ref, p_ref, w_ref, m_ref, v_ref, g_ref, d_ref, nm_ref, nv_ref):
        mine = place_ref[1]
        own_blk = own_ref[...]
        g = jnp.where(mine == 0, own_blk, p_ref[0].astype(F32))
        for j in range(1, N_CHIPS):
            g = g + jnp.where(mine == j, own_blk, p_ref[j].astype(F32))
        d, nm, nv = _adamw(w_ref[...], g, m_ref[...], v_ref[...])
        g_ref[...] = g
        d_ref[...] = d
        nm_ref[...] = nm
        nv_ref[...] = nv

    row = pl.BlockSpec(blk, lambda i, pr: at(i))
    sh = jax.ShapeDtypeStruct((r, c), F32)
    grid_spec = pltpu.PrefetchScalarGridSpec(
        num_scalar_prefetch=1, grid=(nblk,),
        in_specs=[row, pl.BlockSpec((N_CHIPS,) + blk, lambda i, pr: (0,) + at(i)), row, row, row],
        out_specs=[row, row, row, row])
    return pl.pallas_call(
        body, name=name, grid_spec=grid_spec, out_shape=[sh, sh, sh, sh],
        compiler_params=_cparams(("parallel",)),
    )(place, own, parts, w, m, v)


def _interleave_qkv(a):
    lead = a.shape[:-1]
    return a.reshape(lead + (3, HEAD_PAIRS, LANES)).swapaxes(-3, -2).reshape(lead + (3 * D_MODEL,))


def _deinterleave_qkv(a):
    lead = a.shape[:-1]
    return a.reshape(lead + (HEAD_PAIRS, 3, LANES)).swapaxes(-3, -2).reshape(lead + (3 * D_MODEL,))


def _interleave_rows(a):
    return a.reshape(3, HEAD_PAIRS, LANES, a.shape[1]).swapaxes(0, 1).reshape(a.shape)


def _deinterleave_rows(a):
    return a.reshape(HEAD_PAIRS, 3, LANES, a.shape[1]).swapaxes(0, 1).reshape(a.shape)


def _pack_small(pre, conv_b, rg_ba, rg_bx, lam, post, loss_row, b_in, conv_w_full, rg_wa, rg_wx):
    z = jnp.zeros((1, D_MODEL), F32)
    b_used = jnp.concatenate([b_in[:, 0:3 * D_MODEL], b_in[:, 3 * D_MODEL + HEADS:IN_TOTAL]], axis=1)
    b_f = jnp.pad(b_in[:, 3 * D_MODEL:3 * D_MODEL + HEADS], ((0, 0), (0, D_MODEL - HEADS)))
    return jnp.concatenate([
        pre, conv_b, rg_ba, rg_bx, lam, post, loss_row, z,
        b_used.reshape(9, D_MODEL), b_f, conv_w_full, z, z,
        rg_wa.reshape(64, D_MODEL), rg_wx.reshape(64, D_MODEL)], axis=0)


def _unpack_small(p):
    b_used = p[8:17].reshape(1, 9 * D_MODEL)
    b_in = jnp.concatenate([b_used[:, 0:3 * D_MODEL], p[17:18, 0:HEADS], b_used[:, 3 * D_MODEL:]], axis=1)
    return dict(pre_norm_w=p[0:1], conv_b=p[1:2], rg_ba=p[2:3], rg_bx=p[3:4], rg_lambda=p[4:5],
                post_norm_w=p[5:6], loss_row=p[6:7], b_in=b_in, conv_w_full=p[18:22],
                rg_wa=p[24:88].reshape(1, 16, 64, 64), rg_wx=p[88:152].reshape(1, 16, 64, 64))


def _reduce_small(parts, w, m, v):
    def body(p_ref, w_ref, m_ref, v_ref, g_ref, d_ref, nm_ref, nv_ref):
        g = p_ref[0]
        for j in range(1, N_DEV):
            g = g + p_ref[j]
        d, nm, nv = _adamw(w_ref[...], g, m_ref[...], v_ref[...])
        g_ref[...] = g
        d_ref[...] = d
        nm_ref[...] = nm
        nv_ref[...] = nv

    sh = jax.ShapeDtypeStruct((SMALL_ROWS, D_MODEL), F32)
    return pl.pallas_call(body, name="reduce_small", out_shape=[sh, sh, sh, sh])(parts, w, m, v)


def kernel(x, pre_norm_w, w_in, b_in, conv_w, conv_b, rg_wa, rg_ba, rg_wx, rg_bx, rg_lambda, w_branch_a, w_branch_r, w_out, post_norm_w, loss_target, m_pre_norm_w, m_w_in, m_b_in, m_conv_w, m_conv_b, m_rg_wa, m_rg_ba, m_rg_wx, m_rg_bx, m_rg_lambda, m_w_branch_a, m_w_branch_r, m_w_out, m_post_norm_w, v_pre_norm_w, v_w_in, v_b_in, v_conv_w, v_conv_b, v_rg_wa, v_rg_ba, v_rg_wx, v_rg_bx, v_rg_lambda, v_w_branch_a, v_w_branch_r, v_w_out, v_post_norm_w):
    b, s, _ = x.shape
    t = b * s
    me = 4 * lax.axis_index("x") + 2 * lax.axis_index("y") + lax.axis_index("c")
    shard_rows = D_MODEL // N_DEV

    place = jnp.stack([lax.axis_index("c"), 2 * lax.axis_index("x") + lax.axis_index("y")]).astype(jnp.int32)
    w_in_all = _gather(w_in[0].T.astype(BF16), "gather_w_in")
    wt_full = w_in_all.reshape(IN_TOTAL, D_MODEL)
    conv_terms = jnp.concatenate(_split3(conv_w[0]), axis=0)
    conv_pad = jnp.pad(conv_terms, ((0, 16 - 3 * CONV_W), (0, D_MODEL - LANES)))
    sq_stack = jnp.concatenate([w_branch_a[0].astype(BF16), w_branch_r[0].astype(BF16), w_out[0].astype(BF16),
                                conv_pad], axis=0)
    sq_sems, sq_src, sq_land, sq_token = _gather_start(sq_stack, w_in_all, "gather_w_sq_start")

    w_qkv = _interleave_rows(wt_full[0:3 * D_MODEL])
    w_f = jnp.pad(wt_full[3 * D_MODEL:3 * D_MODEL + HEADS], ((0, LANES - HEADS), (0, 0)))
    w_rest = wt_full[3 * D_MODEL + HEADS:IN_USED]
    b_qkv = _interleave_qkv(b_in[:, 0:3 * D_MODEL]) + sq_token[0, 0]
    b_f = jnp.pad(b_in[:, 3 * D_MODEL:3 * D_MODEL + HEADS], ((0, 0), (0, LANES - HEADS)))
    b_rest = b_in[:, 3 * D_MODEL + HEADS:IN_USED]

    def blockdiag(w):
        w2 = w.reshape(N_CBLK, 2, HEAD_DIM, HEAD_DIM)
        zz = jnp.zeros((N_CBLK, HEAD_DIM, HEAD_DIM), w.dtype)
        top = jnp.concatenate([w2[:, 0], zz], axis=2)
        bot = jnp.concatenate([zz, w2[:, 1]], axis=2)
        return jnp.concatenate([top, bot], axis=1).astype(BF16)

    bda, bdx = blockdiag(rg_wa[0]), blockdiag(rg_wx[0])

    x2 = x.reshape(t, D_MODEL)
    tgt2 = loss_target.reshape(t, D_MODEL)
    h = _prenorm(x2, pre_norm_w)
    qkv = _mm_bias(h, w_qkv, b_qkv, BF16, "inproj_qkv")
    zrest = _mm_bias(h, w_rest, b_rest, F32, "inproj_rest")
    zf = _mm_bias(h, w_f, b_f, F32, "inproj_f")
    qkv3 = qkv.reshape(b, s, 3 * D_MODEL)
    zrest3 = zrest.reshape(b, s, 5 * D_MODEL)
    zf3 = zf.reshape(b, s, LANES)
    nq = s // ATT_TILE
    cexp3, crow = _fgate_fwd(zf3)
    crow5 = crow.reshape(b, HEAD_PAIRS, 2, nq, ATT_TILE)
    yatt3, lse5, ga3 = _attn_fwd(qkv3, cexp3, crow5, zrest3)

    sq_all = _gather_wait(sq_sems, sq_src, sq_land, ga3, "gather_w_sq_wait")
    sq_all = lax.dynamic_update_slice(sq_all, sq_stack[None], (me, 0, 0))
    wa = sq_all[:, 0:shard_rows].reshape(D_MODEL, D_MODEL)
    wr = sq_all[:, shard_rows:2 * shard_rows].reshape(D_MODEL, D_MODEL)
    wo = sq_all[:, 2 * shard_rows:3 * shard_rows].reshape(D_MODEL, D_MODEL)
    conv_all = sq_all[:, 3 * shard_rows:3 * shard_rows + 3 * CONV_W, 0:LANES].astype(F32)
    conv_all = (conv_all[:, 0:CONV_W] + conv_all[:, CONV_W:2 * CONV_W]) + conv_all[:, 2 * CONV_W:3 * CONV_W]
    conv_full = conv_all.transpose(1, 0, 2).reshape(CONV_W, D_MODEL)

    ylru3, gr3 = _rnn_fwd(zrest3, conv_full, conv_b, bda, bdx, rg_ba, rg_bx, rg_lambda)
    ga, gr = ga3.reshape(t, D_MODEL), gr3.reshape(t, D_MODEL)
    ya, yr, mm = _branch_merge(ga, gr, wa, wr, zrest)
    dy, do, acc_out = _out_loss(mm, wo, x2, tgt2, post_norm_w)

    dya, dyr, dz_mga, dz_mgr = _merge_bwd(do, wo, zrest, ya, yr)
    dyatt, dz_ga, dylru, dz_gr = _branch_bwd(dya, dyr, wa, wr, zrest, yatt3.reshape(t, D_MODEL),
                                             ylru3.reshape(t, D_MODEL))
    dz_xr3, pvec, dbd = _rnn_bwd(zrest3, ylru3, dylru.reshape(b, s, D_MODEL), conv_full, conv_b, bda, bdx,
                                 rg_ba, rg_bx, rg_lambda)
    dqkv3, dc3 = _attn_bwd(qkv3, dyatt.reshape(b, s, D_MODEL), yatt3, lse5, crow5, cexp3)
    dz_f = _fgate_bwd(dc3, zf3).reshape(t, LANES)
    dz_qkv = dqkv3.reshape(t, 3 * D_MODEL)
    dz_xr = dz_xr3.reshape(t, D_MODEL)

    dw_qkv, db_qkv = _mm_tn(dz_qkv, h, "dw_qkv")
    dw_f, db_f = _mm_tn(dz_f, h, "dw_f")
    dw_parts, db_parts = [], []
    for nm, dzp in (("ga", dz_ga), ("xr", dz_xr), ("gr", dz_gr), ("mga", dz_mga), ("mgr", dz_mgr)):
        dwp, dbp = _mm_tn(dzp, h, "dw_" + nm)
        dw_parts.append(dwp)
        db_parts.append(dbp[0:1])
    dw_a, _ = _mm_tn(ga, dya, "dw_a")
    dw_r, _ = _mm_tn(gr, dyr, "dw_r")
    dw_o, _ = _mm_tn(mm, do, "dw_o")

    zeros_tail = jnp.zeros((IN_TOTAL - IN_USED, D_MODEL), F32)
    dwt_full = jnp.concatenate([_deinterleave_rows(dw_qkv), dw_f[0:HEADS]] + dw_parts + [zeros_tail], axis=0)
    dw_in_send = dwt_full.reshape(N_CHIPS, 2, W_SHARD, D_MODEL).transpose(1, 0, 2, 3)
    by_dest = lambda a: a.reshape(N_CHIPS, 2, shard_rows, D_MODEL).transpose(1, 0, 2, 3)
    dw_sq_send = jnp.concatenate([by_dest(dw_a), by_dest(dw_r), by_dest(dw_o)], axis=2)

    sib_in, sib_sq = _swap_with_sibling([dw_in_send, dw_sq_send], "swap_dw")
    chip_in, own_in = _pair_add(dw_in_send, sib_in, place, "pair_add_in")
    chip_sq, own_sq = _pair_add(dw_sq_send, sib_sq, place, "pair_add_sq")
    sems, sent, lands, token = _exchange_chips_start([chip_in, chip_sq], "exchange_dw_start")

    wt = lambda lo: w_rest[lo * D_MODEL:(lo + 1) * D_MODEL]
    dh_a = _dh_partial([(dz_qkv, w_qkv), (dz_f, w_f)], token, "dh_qkv")
    grad_x2, acc_pre = _dh_final(
        [(dz_ga, wt(0)), (dz_xr, wt(1)), (dz_gr, wt(2)), (dz_mga, wt(3)), (dz_mgr, wt(4))],
        dh_a, x2, dy, pre_norm_w)

    db_in_full = jnp.concatenate([_deinterleave_qkv(db_qkv[0:1]), db_f[0:1, 0:HEADS]] + db_parts
                                 + [jnp.zeros((1, IN_TOTAL - IN_USED), F32)], axis=1)
    d_rg_wa = jnp.stack([dbd[:, 0, 0:HEAD_DIM, 0:HEAD_DIM], dbd[:, 0, HEAD_DIM:, HEAD_DIM:]], axis=1)
    d_rg_wx = jnp.stack([dbd[:, 1, 0:HEAD_DIM, 0:HEAD_DIM], dbd[:, 1, HEAD_DIM:, HEAD_DIM:]], axis=1)
    small_g = _pack_small(acc_pre[0:1], pvec[4:5], pvec[5:6], pvec[6:7], pvec[7:8], acc_out[0:1], acc_out[1:2],
                          db_in_full, pvec[0:4], d_rg_wa, d_rg_wx)
    sm_sems, sm_src, sm_land, sm_token = _gather_start(small_g, grad_x2, "gather_small_start")
    recv_in, recv_sq = _exchange_chips_wait(sems, sent, lands, sm_token, "exchange_dw_wait")

    g_in, d_in, nm_in, nv_in = [a.T for a in _reduce_adamw(
        own_in, recv_in, place, w_in[0].T, m_w_in[0].T, v_w_in[0].T, "adamw_w_in")]
    sq_w = jnp.concatenate([w_branch_a[0], w_branch_r[0], w_out[0]], axis=0)
    sq_m = jnp.concatenate([m_w_branch_a[0], m_w_branch_r[0], m_w_out[0]], axis=0)
    sq_v = jnp.concatenate([v_w_branch_a[0], v_w_branch_r[0], v_w_out[0]], axis=0)
    g_sq, d_sq, nm_sq, nv_sq = _reduce_adamw(own_sq, recv_sq, place, sq_w, sq_m, sq_v, "adamw_w_sq")
    small_all = _gather_wait(sm_sems, sm_src, sm_land, d_sq, "gather_small_wait")
    small_all = lax.dynamic_update_slice(small_all, small_g[None], (me, 0, 0))

    def place_conv(a):
        return lax.dynamic_update_slice(jnp.zeros((CONV_W, D_MODEL), F32), a[0], (0, me * LANES))

    zrow = jnp.zeros((1, D_MODEL), F32)
    small_w = _pack_small(pre_norm_w, conv_b, rg_ba, rg_bx, rg_lambda, post_norm_w, zrow, b_in,
                          place_conv(conv_w), rg_wa[0], rg_wx[0])
    small_m = _pack_small(m_pre_norm_w, m_conv_b, m_rg_ba, m_rg_bx, m_rg_lambda, m_post_norm_w, zrow, m_b_in,
                          place_conv(m_conv_w), m_rg_wa[0], m_rg_wx[0])
    small_v = _pack_small(v_pre_norm_w, v_conv_b, v_rg_ba, v_rg_bx, v_rg_lambda, v_post_norm_w, zrow, v_b_in,
                          place_conv(v_conv_w), v_rg_wa[0], v_rg_wx[0])
    outs_small = [_unpack_small(p) for p in _reduce_small(small_all, small_w, small_m, small_v)]

    loss = (0.5 / D_MODEL) * jnp.sum(outs_small[0]["loss_row"])

    def leaf(kind, name):
        if name == "w_in":
            return (g_in, d_in, nm_in, nv_in)[kind][None]
        if name in ("w_branch_a", "w_branch_r", "w_out"):
            j = ("w_branch_a", "w_branch_r", "w_out").index(name)
            return (g_sq, d_sq, nm_sq, nv_sq)[kind][None, j * shard_rows:(j + 1) * shard_rows]
        if name == "conv_w":
            return lax.dynamic_slice(outs_small[kind]["conv_w_full"], (0, me * LANES), (CONV_W, LANES))[None]
        return outs_small[kind][name]

    names = ["pre_norm_w", "w_in", "b_in", "conv_w", "conv_b", "rg_wa", "rg_ba", "rg_wx", "rg_bx", "rg_lambda",
             "w_branch_a", "w_branch_r", "w_out", "post_norm_w"]
    out = [loss, grad_x2.reshape(b, s, D_MODEL)]
    for kind in range(4):
        out += [leaf(kind, nm) for nm in names]
    return tuple(out)
```

```python
import jax
import jax.numpy as jnp
from jax import lax
from jax.experimental import pallas as pl
from jax.experimental.pallas import tpu as pltpu

F32 = jnp.float32
BF16 = jnp.bfloat16

N_DEV = 8
D_MODEL = 1024
HEADS = 16
HEAD_DIM = 64
HEAD_PAIRS = HEADS // 2
LANES = 128
N_CBLK = D_MODEL // LANES
CONV_W = 4
RG_C = 8.0
NORM_EPS = 1e-6
MASK_VALUE = -1e30
IN_USED = 8208
IN_TOTAL = 9232
W_SHARD = IN_TOTAL // N_DEV

ADAM_LR = 0.001
ADAM_B1 = 0.9
ADAM_B2 = 0.999
ADAM_EPS = 1e-08
ADAM_WD = 0.01
ADAM_STEP = 10

ATT_TILE = 256
SCAN_TILE = 256
SMALL_ROWS = 152


def _cparams(sem=None, vmem_mb=None):
    kw = {}
    if sem is not None:
        kw["dimension_semantics"] = sem
    if vmem_mb is not None:
        kw["vmem_limit_bytes"] = vmem_mb * 1024 * 1024
    return pltpu.CompilerParams(**kw)


def _sigmoid(x):
    return 1.0 / (1.0 + jnp.exp(-x))


def _softplus(x):
    return jnp.maximum(x, 0.0) + jnp.log1p(jnp.exp(-jnp.abs(x)))


def _one_minus_exp(y, exp_y):
    series = -y * (1.0 + y * (1.0 / 2 + y * (1.0 / 6 + y * (1.0 / 24 + y * (1.0 / 120)))))
    return jnp.where(y > -0.0625, series, 1.0 - exp_y)


def _split3(x):
    hi = x.astype(BF16)
    r1 = x - hi.astype(F32)
    mid = r1.astype(BF16)
    lo = (r1 - mid.astype(F32)).astype(BF16)
    return hi, mid, lo


def _dot(a, b):
    return jnp.dot(a, b, preferred_element_type=F32)


def _dot_nt(a, b):
    return lax.dot_general(a, b, (((1,), (1,)), ((), ())), preferred_element_type=F32)


def _dot_tn(a, b):
    return lax.dot_general(a, b, (((0,), (0,)), ((), ())), preferred_element_type=F32)


def _iota(shape, dim):
    return lax.broadcasted_iota(jnp.int32, shape, dim)


_ANY = pl.BlockSpec(memory_space=pl.ANY)
_MESH = pl.DeviceIdType.MESH
N_CHIPS = 4


def _place():
    x, y, c = lax.axis_index("x"), lax.axis_index("y"), lax.axis_index("c")
    other_chips = [(1 - x, y), (x, 1 - y), (1 - x, 1 - y)]
    return x, y, c, other_chips


def _gather(x_shard, name):
    def body(x_ref, out_ref, send_sems, recv_sems, local_sem):
        x, y, c, chips = _place()
        me, sibling = (x, y, c), (x, y, 1 - c)

        def slot(p):
            return out_ref.at[4 * p[0] + 2 * p[1] + p[2]]

        def copy(k, block, to, src=None):
            return pltpu.make_async_remote_copy(
                src_ref=slot(block) if src is None else src, dst_ref=slot(block),
                send_sem=send_sems.at[k], recv_sem=recv_sems.at[k], device_id=to, device_id_type=_MESH)

        mine = pltpu.make_async_copy(x_ref, slot(me), local_sem)
        mine.start()
        first = [copy(0, me, sibling, src=x_ref)]
        first += [copy(1 + j, me, (*chip, c), src=x_ref) for j, chip in enumerate(chips)]
        for cp in first:
            cp.start()
        passed = [copy(4 + j, (*chip, c), sibling) for j, chip in enumerate(chips)]
        for j, chip in enumerate(chips):
            copy(1 + j, (*chip, c), me).wait_recv()
            passed[j].start()
        copy(0, sibling, me).wait_recv()
        for j, chip in enumerate(chips):
            copy(4 + j, (*chip, 1 - c), me).wait_recv()
        for cp in first + passed:
            cp.wait_send()
        mine.wait()

    return pl.pallas_call(
        body, name=name,
        out_shape=jax.ShapeDtypeStruct((N_DEV,) + tuple(x_shard.shape), x_shard.dtype),
        in_specs=[_ANY], out_specs=_ANY,
        scratch_shapes=[pltpu.SemaphoreType.DMA((7,)), pltpu.SemaphoreType.DMA((7,)), pltpu.SemaphoreType.DMA],
    )(x_shard)


def _swap_with_sibling(srcs, name):
    n = len(srcs)

    def body(*refs):
        src_refs, out_refs = refs[:n], refs[n:2 * n]
        send_sems, recv_sems = refs[2 * n:]
        x, y, c, _ = _place()
        cps = [pltpu.make_async_remote_copy(
            src_ref=src_refs[i].at[1 - c], dst_ref=out_refs[i], send_sem=send_sems.at[i], recv_sem=recv_sems.at[i],
            device_id=(x, y, 1 - c), device_id_type=_MESH) for i in range(n)]
        for cp in cps:
            cp.start()
        for cp in cps:
            cp.wait()

    return pl.pallas_call(
        body, name=name,
        out_shape=[jax.ShapeDtypeStruct(a.shape[1:], a.dtype) for a in srcs],
        in_specs=[_ANY] * n, out_specs=[_ANY] * n,
        scratch_shapes=[pltpu.SemaphoreType.DMA((n,)), pltpu.SemaphoreType.DMA((n,))],
    )(*srcs)


def _blocks_2d(r, c):
    if r % 128 == 0:
        return (128, c), r // 128, lambda i: (i, 0)
    return (r, 256), c // 256, lambda i: (0, i)


def _pair_add(src, recv, place, name):
    _, _, r, c = src.shape
    blk, nblk, at = _blocks_2d(r, c)

    def body(place_ref, a_ref, b_ref, q16_ref, own_ref):
        q = a_ref[...] + b_ref[...]
        q16_ref[...] = q.astype(BF16)

        @pl.when(pl.program_id(1) == place_ref[1])
        def _():
            own_ref[...] = q

    grid_spec = pltpu.PrefetchScalarGridSpec(
        num_scalar_prefetch=1, grid=(nblk, N_CHIPS),
        in_specs=[pl.BlockSpec((None, None) + blk, lambda i, j, pr: (pr[0], j) + at(i)),
                  pl.BlockSpec((None,) + blk, lambda i, j, pr: (j,) + at(i))],
        out_specs=[pl.BlockSpec((None,) + blk, lambda i, j, pr: (j,) + at(i)),
                   pl.BlockSpec(blk, lambda i, j, pr: at(i))])
    return pl.pallas_call(
        body, name=name, grid_spec=grid_spec,
        out_shape=[jax.ShapeDtypeStruct((N_CHIPS, r, c), BF16), jax.ShapeDtypeStruct((r, c), F32)],
        compiler_params=_cparams(("parallel", "arbitrary")),
    )(place, src, recv)


_HBM = pl.BlockSpec(memory_space=pltpu.HBM)
_SEM = pl.BlockSpec(memory_space=pltpu.SEMAPHORE)
_DATAFLOW = pltpu.SideEffectType.DATAFLOW_SIDE_EFFECTING


def _chip_copy(src_ref, land_ref, send_sem, recv_sem, k, chips, c, land):
    chip = chips[k]
    return pltpu.make_async_remote_copy(
        src_ref=src_ref.at[2 * chip[0] + chip[1]], dst_ref=land_ref.at[land],
        send_sem=send_sem, recv_sem=recv_sem, device_id=(*chip, c), device_id_type=_MESH)


def _exchange_chips_start(srcs, name):
    n = len(srcs)
    ncp = 3 * n

    def body(*refs):
        src_refs, land_refs = refs[:n], refs[n:2 * n]
        sems = refs[4 * n:4 * n + 2 * ncp]
        token = refs[-1]
        x, y, c, chips = _place()
        for i in range(n):
            for k in range(3):
                j = 3 * i + k
                _chip_copy(src_refs[i], land_refs[i], sems[j], sems[ncp + j], k, chips, c, 2 * x + y).start()
        token[...] = jnp.zeros_like(token)

    hbm = [pltpu.HBM(a.shape, a.dtype) for a in srcs]
    lands = [pltpu.with_memory_space_constraint(lax.empty(a.shape, a.dtype), pltpu.HBM) for a in srcs]
    res = pl.pallas_call(
        body, name=name,
        out_shape=(*hbm, *hbm, *([pltpu.SemaphoreType.DMA(())] * (2 * ncp)), jax.ShapeDtypeStruct((8, LANES), F32)),
        in_specs=[_HBM] * (2 * n),
        out_specs=(*([_HBM] * (2 * n)), *([_SEM] * (2 * ncp)), pl.BlockSpec(memory_space=pltpu.VMEM)),
        input_output_aliases={i: i for i in range(2 * n)},
        compiler_params=pltpu.CompilerParams(has_side_effects=_DATAFLOW),
    )(*[pltpu.with_memory_space_constraint(a, pltpu.HBM) for a in srcs], *lands)
    return list(res[2 * n:2 * n + 2 * ncp]), list(res[:n]), list(res[n:2 * n]), res[-1]


def _exchange_chips_wait(sems, srcs, lands, after, name):
    n = len(srcs)
    ncp = 3 * n

    def body(*refs):
        src_refs, land_refs = refs[:n], refs[n:2 * n]
        sem_refs = refs[2 * n:2 * n + 2 * ncp]
        x, y, c, chips = _place()
        for i in range(n):
            for k in range(3):
                j = 3 * i + k
                cp = _chip_copy(src_refs[i], land_refs[i], sem_refs[j], sem_refs[ncp + j], k, chips, c,
                                2 * chips[k][0] + chips[k][1])
                cp.wait_send()
                cp.wait_recv()

    hbm = [pltpu.HBM(a.shape, a.dtype) for a in srcs]
    res = pl.pallas_call(
        body, name=name, out_shape=(*hbm, *hbm),
        in_specs=[_HBM] * (2 * n) + [_SEM] * (2 * ncp) + [_ANY], out_specs=tuple([_HBM] * (2 * n)),
        input_output_aliases={i: i for i in range(2 * n)},
        compiler_params=pltpu.CompilerParams(has_side_effects=_DATAFLOW),
    )(*srcs, *lands, *sems, after)
    return list(res[n:2 * n])


def _peer_copy(src_ref, land_ref, send_sem, recv_sem, k, place, land):
    x, y, c = place
    peer = (1 - x if k & 4 else x, 1 - y if k & 2 else y, 1 - c if k & 1 else c)
    return pltpu.make_async_remote_copy(
        src_ref=src_ref, dst_ref=land_ref.at[land], send_sem=send_sem, recv_sem=recv_sem,
        device_id=peer, device_id_type=_MESH)


def _gather_start(x_shard, after, name):
    npeer = N_DEV - 1

    def body(x_ref, land_ref, after_ref, x_thru, land_thru, *rest):
        sems, token = rest[:2 * npeer], rest[-1]
        x, y, c, _ = _place()
        for k in range(1, N_DEV):
            _peer_copy(x_ref, land_ref, sems[k - 1], sems[npeer + k - 1], k, (x, y, c), 4 * x + 2 * y + c).start()
        token[...] = jnp.zeros_like(token)

    land = pltpu.with_memory_space_constraint(lax.empty((N_DEV,) + tuple(x_shard.shape), x_shard.dtype), pltpu.HBM)
    res = pl.pallas_call(
        body, name=name,
        out_shape=(pltpu.HBM(x_shard.shape, x_shard.dtype), pltpu.HBM(land.shape, land.dtype),
                   *([pltpu.SemaphoreType.DMA(())] * (2 * npeer)), jax.ShapeDtypeStruct((8, LANES), F32)),
        in_specs=[_HBM, _HBM, _ANY],
        out_specs=(_HBM, _HBM, *([_SEM] * (2 * npeer)), pl.BlockSpec(memory_space=pltpu.VMEM)),
        input_output_aliases={0: 0, 1: 1},
        compiler_params=pltpu.CompilerParams(has_side_effects=_DATAFLOW),
    )(pltpu.with_memory_space_constraint(x_shard, pltpu.HBM), land, after)
    return list(res[2:2 + 2 * npeer]), res[0], res[1], res[-1]


def _gather_wait(sems, src, land, after, name):
    npeer = N_DEV - 1

    def body(x_ref, land_ref, *rest):
        sem_refs = rest[:2 * npeer]
        x, y, c, _ = _place()
        for k in range(1, N_DEV):
            peer_index = (4 * x + 2 * y + c) ^ k
            cp = _peer_copy(x_ref, land_ref, sem_refs[k - 1], sem_refs[npeer + k - 1], k, (x, y, c), peer_index)
            cp.wait_send()
            cp.wait_recv()

    res = pl.pallas_call(
        body, name=name, out_shape=(pltpu.HBM(src.shape, src.dtype), pltpu.HBM(land.shape, land.dtype)),
        in_specs=[_HBM, _HBM] + [_SEM] * (2 * npeer) + [_ANY], out_specs=(_HBM, _HBM),
        input_output_aliases={0: 0, 1: 1},
        compiler_params=pltpu.CompilerParams(has_side_effects=_DATAFLOW),
    )(src, land, *sems, after)
    return res[1]


def _prenorm(x2, w):
    t = x2.shape[0]
    tm = min(512, t)

    def body(x_ref, w_ref, h_ref):
        x = x_ref[...]
        r = lax.rsqrt(jnp.mean(x * x, axis=-1, keepdims=True) + NORM_EPS)
        h_ref[...] = (x * r * w_ref[...]).astype(BF16)

    return pl.pallas_call(
        body, name="prenorm", grid=(t // tm,),
        in_specs=[pl.BlockSpec((tm, D_MODEL), lambda i: (i, 0)), pl.BlockSpec((1, D_MODEL), lambda i: (0, 0))],
        out_specs=pl.BlockSpec((tm, D_MODEL), lambda i: (i, 0)),
        out_shape=jax.ShapeDtypeStruct((t, D_MODEL), BF16),
        compiler_params=_cparams(("parallel",)),
    )(x2, w)


def _mm_bias(a, bt, bias, out_dtype, name):
    m, k = a.shape
    n = bt.shape[0]
    tm = min(512, m)
    tn = min(1024, n)

    def body(a_ref, bt_ref, bias_ref, o_ref, b_scr):
        @pl.when(pl.program_id(1) == 0)
        def _():
            b_scr[...] = _transpose_bf16(bt_ref[...])

        o_ref[...] = (_dot(a_ref[...], b_scr[...]) + bias_ref[...]).astype(o_ref.dtype)

    return pl.pallas_call(
        body, name=name, grid=(n // tn, m // tm),
        in_specs=[pl.BlockSpec((tm, k), lambda j, i: (i, 0)), pl.BlockSpec((tn, k), lambda j, i: (j, 0)),
                  pl.BlockSpec((1, tn), lambda j, i: (0, j))],
        out_specs=pl.BlockSpec((tm, tn), lambda j, i: (i, j)),
        out_shape=jax.ShapeDtypeStruct((m, n), out_dtype),
        scratch_shapes=[pltpu.VMEM((k, tn), BF16)],
        compiler_params=_cparams(("parallel", "arbitrary")),
    )(a, bt, bias)


def _mm_tn(a, b, name):
    t, m = a.shape
    n = b.shape[1]
    tm = min(1024, m)
    tk = min(512, t)

    def body(a_ref, b_ref, o_ref, s_ref):
        kk = pl.program_id(1)

        @pl.when(kk == 0)
        def _():
            o_ref[...] = jnp.zeros_like(o_ref)
            s_ref[...] = jnp.zeros_like(s_ref)

        aa = a_ref[...]
        o_ref[...] += _dot_tn(aa, b_ref[...])
        s_ref[0:1, :] += jnp.sum(aa.astype(F32), axis=0, keepdims=True)

    return pl.pallas_call(
        body, name=name, grid=(m // tm, t // tk),
        in_specs=[pl.BlockSpec((tk, tm), lambda i, kk: (kk, i)), pl.BlockSpec((tk, n), lambda i, kk: (kk, 0))],
        out_specs=[pl.BlockSpec((tm, n), lambda i, kk: (i, 0)), pl.BlockSpec((8, tm), lambda i, kk: (0, i))],
        out_shape=[jax.ShapeDtypeStruct((m, n), F32), jax.ShapeDtypeStruct((8, m), F32)],
        compiler_params=_cparams(("parallel", "arbitrary")),
    )(a, b)


def _fgate_fwd(zf3):
    b, s, _ = zf3.shape
    tb = SCAN_TILE
    nb = s // tb

    def body(z_ref, cexp_ref, crow_ref):
        tri = (_iota((tb, tb), 1) <= _iota((tb, tb), 0)).astype(BF16)
        expand = ((_iota((LANES, D_MODEL), 1) >> 6) == _iota((LANES, D_MODEL), 0)).astype(BF16)
        carry = jnp.zeros((1, LANES), F32)
        for i in range(nb):
            rows = slice(i * tb, (i + 1) * tb)
            z = z_ref[rows, :]
            lf = jnp.minimum(z, 0.0) - jnp.log1p(jnp.exp(-jnp.abs(z)))
            cb = sum(_dot(tri, part) for part in _split3(lf)) + carry
            carry = cb[tb - 1:tb, :]
            cexp_ref[rows, :] = sum(_dot(part, expand) for part in _split3(cb))
            crow_ref[:, rows] = cb.T[0:HEADS, :]

    return pl.pallas_call(
        body, name="fgate_fwd", grid=(b,),
        in_specs=[pl.BlockSpec((None, s, LANES), lambda i: (i, 0, 0))],
        out_specs=[pl.BlockSpec((None, s, D_MODEL), lambda i: (i, 0, 0)),
                   pl.BlockSpec((None, HEADS, s), lambda i: (i, 0, 0))],
        out_shape=[jax.ShapeDtypeStruct((b, s, D_MODEL), F32), jax.ShapeDtypeStruct((b, HEADS, s), F32)],
        compiler_params=_cparams(("parallel",)),
    )(zf3)


def _fgate_bwd(dc3, zf3):
    b, s, _ = zf3.shape
    tb = SCAN_TILE
    nb = s // tb

    def body(dc_ref, z_ref, o_ref):
        tri = (_iota((tb, tb), 1) >= _iota((tb, tb), 0)).astype(BF16)
        carry = jnp.zeros((1, LANES), F32)
        for i in reversed(range(nb)):
            rows = slice(i * tb, (i + 1) * tb)
            dlf = sum(_dot(tri, part) for part in _split3(dc_ref[rows, :])) + carry
            carry = dlf[0:1, :]
            o_ref[rows, :] = (dlf * _sigmoid(-z_ref[rows, :])).astype(BF16)

    return pl.pallas_call(
        body, name="fgate_bwd", grid=(b,),
        in_specs=[pl.BlockSpec((None, s, LANES), lambda i: (i, 0, 0)),
                  pl.BlockSpec((None, s, LANES), lambda i: (i, 0, 0))],
        out_specs=pl.BlockSpec((None, s, LANES), lambda i: (i, 0, 0)),
        out_shape=jax.ShapeDtypeStruct((b, s, LANES), BF16),
        compiler_params=_cparams(("parallel",)),
    )(dc3, zf3)


def _spare(hh):
    return HEAD_DIM if hh == 0 else 0


def _put_cols(tile, mine, cols, first):
    lane = _iota((1, LANES), 1)
    out = jnp.where(mine, tile, jnp.zeros((), tile.dtype))
    for j, c in enumerate(cols):
        out = jnp.where(lane == first + j, c, out)
    return out


def _put_rows(tile, mine, rows, first):
    sub = _iota((LANES, 1), 0)
    out = jnp.where(mine, tile, jnp.zeros((), tile.dtype))
    for j, r in enumerate(rows):
        out = jnp.where(sub == first + j, r, out)
    return out


def _transpose_bf16(a):
    return a.astype(F32).T.astype(BF16)


def _attn_fwd(qkv3, cexp3, crow5, zrest3):
    b, s, _ = qkv3.shape
    ta = ATT_TILE
    nq = s // ta
    hd = HEAD_DIM

    def body(qkv_ref, cq_ref, ck_ref, g_ref, y_ref, lse_ref, ga_ref, kt_scr, v_scr):
        lane = _iota((1, LANES), 1)
        sub = _iota((LANES, 1), 0)
        lane_mine = (lane < hd, lane >= hd)
        sub_mine = (sub < hd, sub >= hd)
        causal = _iota((ta, ta), 0) >= _iota((ta, ta), 1)
        one = jnp.ones((), BF16)

        for kj in range(nq):
            rows = slice(kj * ta, (kj + 1) * ta)
            kt = _transpose_bf16(qkv_ref[rows, LANES:2 * LANES])
            v = qkv_ref[rows, 2 * LANES:3 * LANES]
            for hh in range(2):
                ck = list(_split3(-ck_ref[hh, kj:kj + 1, :]))
                kt_scr[hh, kj] = _put_rows(kt, sub_mine[hh], [one, one, one] + ck, _spare(hh))
                v_scr[hh, kj] = _put_cols(v, lane_mine[hh], [one], _spare(hh))

        for qi in range(nq):
            rows = slice(qi * ta, (qi + 1) * ta)
            q = qkv_ref[rows, 0:LANES] * 0.125
            cq = cq_ref[rows, :]
            qh = [_put_cols(q, lane_mine[hh], list(_split3(cq[:, hh * hd:hh * hd + 1])) + [one, one, one], _spare(hh))
                  for hh in range(2)]
            st = [(jnp.full((ta, 1), MASK_VALUE, F32), jnp.zeros((ta, LANES), F32))] * 2
            for kj in range(qi + 1):
                for hh in range(2):
                    m, acc = st[hh]
                    sc = _dot(qh[hh], kt_scr[hh, kj])
                    if kj == qi:
                        sc = jnp.where(causal, sc, MASK_VALUE)
                    mn = jnp.maximum(m, jnp.max(sc, axis=-1, keepdims=True))
                    p = jnp.exp(sc - mn).astype(BF16)
                    st[hh] = (mn, jnp.exp(m - mn) * acc + _dot(p, v_scr[hh, kj]))
            (ma, acca), (mb, accb) = st
            la = acca[:, hd:hd + 1]
            lb = accb[:, 0:1]
            y = jnp.where(lane_mine[0], acca * (1.0 / la), accb * (1.0 / lb))
            lse = jnp.where(lane_mine[0], ma + jnp.log(la), mb + jnp.log(lb)).T
            lse_ref[0, qi:qi + 1, :] = lse[0:1, :]
            lse_ref[1, qi:qi + 1, :] = lse[hd:hd + 1, :]
            y_ref[rows, :] = y
            g = g_ref[rows, :].astype(F32)
            ga_ref[rows, :] = (y * (g * _sigmoid(g))).astype(BF16)

    blk = lambda w: pl.BlockSpec((None, s, w), lambda i, p: (i, 0, p))
    rows5 = pl.BlockSpec((None, None, 2, nq, ta), lambda i, p: (i, p, 0, 0, 0))
    return pl.pallas_call(
        body, name="attn_fwd", grid=(b, HEAD_PAIRS),
        in_specs=[blk(3 * LANES), blk(LANES), rows5, blk(LANES)],
        out_specs=[blk(LANES), rows5, blk(LANES)],
        out_shape=[jax.ShapeDtypeStruct((b, s, D_MODEL), F32),
                   jax.ShapeDtypeStruct((b, HEAD_PAIRS, 2, nq, ta), F32),
                   jax.ShapeDtypeStruct((b, s, D_MODEL), BF16)],
        scratch_shapes=[pltpu.VMEM((2, nq, LANES, ta), BF16), pltpu.VMEM((2, nq, ta, LANES), BF16)],
        compiler_params=_cparams(("parallel", "parallel")),
    )(qkv3, cexp3, crow5, zrest3)


def _attn_bwd(qkv3, do3, y3, lse5, crow5, cexp3):
    b, s, _ = qkv3.shape
    ta = ATT_TILE
    nq = s // ta
    hd = HEAD_DIM

    def body(qkv_ref, do_ref, y_ref, lse_ref, crow_ref, cexp_ref, dqkv_ref, dc_ref,
             qa_scr, doa_scr, qst_scr, dot_scr, kt_scr, vt_scr, dq_scr, rs_scr):
        pair = pl.program_id(1)
        lane = _iota((1, LANES), 1)
        sub = _iota((LANES, 1), 0)
        lane_mine = (lane < hd, lane >= hd)
        sub_mine = (sub < hd, sub >= hd)
        causal = _iota((ta, ta), 0) >= _iota((ta, ta), 1)
        one = jnp.ones((), BF16)
        zero = jnp.zeros((), BF16)

        @pl.when(pair == 0)
        def _():
            dc_ref[...] = jnp.zeros_like(dc_ref)

        for i in range(nq):
            rows = slice(i * ta, (i + 1) * ta)
            qs = qkv_ref[rows, 0:LANES] * 0.125
            qst = _transpose_bf16(qs)
            kt = _transpose_bf16(qkv_ref[rows, LANES:2 * LANES])
            vt = _transpose_bf16(qkv_ref[rows, 2 * LANES:3 * LANES])
            do = do_ref[rows, :]
            dof = do.astype(F32)
            dot = dof.T.astype(BF16)
            pr = y_ref[rows, :] * dof
            cq = cexp_ref[rows, :]
            lse_c = jnp.where(sub == 0, lse_ref[0, i:i + 1, :],
                              jnp.where(sub == 1, lse_ref[1, i:i + 1, :], 0.0)).T
            for hh in range(2):
                sp = _spare(hh)
                dsum = jnp.sum(jnp.where(lane_mine[hh], pr, 0.0), axis=-1, keepdims=True)
                bias = cq[:, hh * hd:hh * hd + 1] - lse_c[:, hh:hh + 1]
                qa_scr[hh, i] = _put_cols(qs, lane_mine[hh], list(_split3(bias)) + [one, one, one], sp)
                doa_scr[hh, i] = _put_cols(do, lane_mine[hh], list(_split3(-dsum)), sp)
                qst_scr[hh, i] = jnp.where(sub_mine[hh], qst, zero)
                dot_scr[hh, i] = jnp.where(sub_mine[hh], dot, zero)
                ck = list(_split3(-crow_ref[hh, i:i + 1, :]))
                kt_scr[hh, i] = _put_rows(kt, sub_mine[hh], [one, one, one] + ck, sp)
                vt_scr[hh, i] = _put_rows(vt, sub_mine[hh], [one, one, one], sp)
            dq_scr[i] = jnp.zeros((ta, LANES), F32)
            rs_scr[i] = jnp.zeros((ta, LANES), F32)

        for kj in range(nq):
            krows = slice(kj * ta, (kj + 1) * ta)
            k = qkv_ref[krows, LANES:2 * LANES]
            km = (jnp.where(lane_mine[0], k, zero), jnp.where(lane_mine[1], k, zero))
            dkt = jnp.zeros((LANES, ta), F32)
            dvt = jnp.zeros((LANES, ta), F32)
            dcp = [jnp.zeros((8, ta), F32), jnp.zeros((8, ta), F32)]
            for qi in range(kj, nq):
                dq = jnp.zeros((ta, LANES), F32)
                rs = []
                for hh in range(2):
                    sc = _dot(qa_scr[hh, qi], kt_scr[hh, kj])
                    if qi == kj:
                        sc = jnp.where(causal, sc, MASK_VALUE)
                    p = jnp.exp(sc)
                    dsf = p * _dot(doa_scr[hh, qi], vt_scr[hh, kj])
                    dcp[hh] = dcp[hh] + jnp.sum(dsf.reshape(ta // 8, 8, ta), axis=0)
                    rs.append(jnp.sum(dsf, axis=-1, keepdims=True))
                    ds = dsf.astype(BF16)
                    dq = dq + _dot(ds, km[hh])
                    dkt = dkt + _dot(qst_scr[hh, qi], ds)
                    dvt = dvt + _dot(dot_scr[hh, qi], p.astype(BF16))
                dq_scr[qi] += dq
                rs_scr[qi] += jnp.where(lane == 0, rs[0], jnp.where(lane == 1, rs[1], 0.0))
            dqkv_ref[krows, LANES:2 * LANES] = dkt.T.astype(BF16)
            dqkv_ref[krows, 2 * LANES:3 * LANES] = dvt.T.astype(BF16)
            dca = jnp.sum(dcp[0], axis=0, keepdims=True)
            dcb = jnp.sum(dcp[1], axis=0, keepdims=True)
            dcs = jnp.where(sub == 0, dca, jnp.where(sub == 1, dcb, 0.0)).T
            dc_ref[krows, :] += (jnp.where(lane == 2 * pair, -dcs[:, 0:1], 0.0)
                                 + jnp.where(lane == 2 * pair + 1, -dcs[:, 1:2], 0.0))
        for qi in range(nq):
            rows = slice(qi * ta, (qi + 1) * ta)
            dqkv_ref[rows, 0:LANES] = (dq_scr[qi] * 0.125).astype(BF16)
            rq = rs_scr[qi]
            dc_ref[rows, :] += (jnp.where(lane == 2 * pair, rq[:, 0:1], 0.0)
                                + jnp.where(lane == 2 * pair + 1, rq[:, 1:2], 0.0))

    blk = lambda w: pl.BlockSpec((None, s, w), lambda i, p: (i, 0, p))
    rows5 = pl.BlockSpec((None, None, 2, nq, ta), lambda i, p: (i, p, 0, 0, 0))
    by_rows = lambda: pltpu.VMEM((2, nq, ta, LANES), BF16)
    by_cols = lambda: pltpu.VMEM((2, nq, LANES, ta), BF16)
    return pl.pallas_call(
        body, name="attn_bwd", grid=(b, HEAD_PAIRS),
        in_specs=[blk(3 * LANES), blk(LANES), blk(LANES), rows5, rows5, blk(LANES)],
        out_specs=[blk(3 * LANES), pl.BlockSpec((None, s, LANES), lambda i, p: (i, 0, 0))],
        out_shape=[jax.ShapeDtypeStruct((b, s, 3 * D_MODEL), BF16), jax.ShapeDtypeStruct((b, s, LANES), F32)],
        scratch_shapes=[by_rows(), by_rows(), by_cols(), by_cols(), by_cols(), by_cols(),
                        pltpu.VMEM((nq, ta, LANES), F32), pltpu.VMEM((nq, ta, LANES), F32)],
        compiler_params=_cparams(("parallel", "arbitrary")),
    )(qkv3, do3, y3, lse5, crow5, cexp3)


def _shifted(v, ks, pad_ref, s):
    pad_ref[0:8, :] = jnp.zeros((8, LANES), F32)
    pad_ref[8 + s:16 + s, :] = jnp.zeros((8, LANES), F32)
    pad_ref[8:8 + s, :] = v
    return [pad_ref[8 - k:8 - k + s, :] for k in ks]


def _rnn_common(xr, cw_ref, cb_ref, bda_ref, bdx_ref, ba_ref, bx_ref, lam_ref, pad, s):
    rows = _iota((s, LANES), 0)
    x1, x2, x3 = _shifted(xr, (1, 2, 3), pad, s)
    xc = cb_ref[...] + cw_ref[0:1, :] * x3
    xc = xc + cw_ref[1:2, :] * x2
    xc = xc + cw_ref[2:3, :] * x1
    xc = xc + cw_ref[3:4, :] * xr
    xcb = xc.astype(BF16)
    r = _sigmoid(_dot(xcb, bda_ref[...]) + ba_ref[...])
    i = _sigmoid(_dot(xcb, bdx_ref[...]) + bx_ref[...])
    sp = _softplus(-lam_ref[...])
    log_a = (-RG_C * r) * sp
    a = jnp.exp(log_a)
    a2 = a * a
    sq = jnp.sqrt(jnp.maximum(_one_minus_exp(2.0 * log_a, a2), 0.0))
    return rows, (x1, x2, x3), xc, xcb, r, i, sp, a, a2, sq


def _scan_down(a, u, rows, s, s1, s2):
    low = rows & 7
    for sh in (1, 2, 4):
        keep = low >= sh
        u = u + a * jnp.where(keep, pltpu.roll(u, sh, 0), 0.0)
        a = a * jnp.where(keep, pltpu.roll(a, sh, 0), 1.0)
    ng = s // 8
    s1[...] = a
    s2[...] = u
    at = s1[pl.ds(7, ng, stride=8), :]
    ut = s2[pl.ds(7, ng, stride=8), :]
    grow = _iota((ng, LANES), 0)
    sh = 1
    while sh < ng:
        keep = grow >= sh
        ut = ut + at * jnp.where(keep, pltpu.roll(ut, sh, 0), 0.0)
        if sh * 2 < ng:
            at = at * jnp.where(keep, pltpu.roll(at, sh, 0), 1.0)
        sh *= 2
    h_in = jnp.where(grow >= 1, pltpu.roll(ut, 1, 0), 0.0)
    for k in range(8):
        s1[pl.ds(k, ng, stride=8), :] = h_in
    return u + a * s1[...]


def _scan_up(a, g, rows, s, s1, s2):
    low = rows & 7
    for sh in (1, 2, 4):
        keep = low < 8 - sh
        g = g + a * jnp.where(keep, pltpu.roll(g, s - sh, 0), 0.0)
        a = a * jnp.where(keep, pltpu.roll(a, s - sh, 0), 1.0)
    ng = s // 8
    s1[...] = a
    s2[...] = g
    at = s1[pl.ds(0, ng, stride=8), :]
    gt = s2[pl.ds(0, ng, stride=8), :]
    grow = _iota((ng, LANES), 0)
    sh = 1
    while sh < ng:
        keep = grow < ng - sh
        gt = gt + at * jnp.where(keep, pltpu.roll(gt, ng - sh, 0), 0.0)
        if sh * 2 < ng:
            at = at * jnp.where(keep, pltpu.roll(at, ng - sh, 0), 1.0)
        sh *= 2
    g_in = jnp.where(grow < ng - 1, pltpu.roll(gt, ng - 1, 0), 0.0)
    for k in range(8):
        s1[pl.ds(k, ng, stride=8), :] = g_in
    return g + a * s1[...]


def _rnn_specs(s):
    blk = lambda off: pl.BlockSpec((None, s, LANES), lambda cb, i: (i, 0, off + cb))
    vec = lambda r: pl.BlockSpec((r, LANES), lambda cb, i: (0, cb))
    mat = pl.BlockSpec((None, LANES, LANES), lambda cb, i: (cb, 0, 0))
    return blk, vec, mat


def _rnn_fwd(zrest3, conv_w, conv_b, bda, bdx, ba, bx, lam):
    b, s, _ = zrest3.shape

    def body(xr_ref, g_ref, cw_ref, cb_ref, bda_ref, bdx_ref, ba_ref, bx_ref, lam_ref, h_ref, gr_ref, s1, s2, pad):
        xr = xr_ref[...].astype(F32)
        rows, _, xc, _, _, i, _, a, _, sq = _rnn_common(
            xr, cw_ref, cb_ref, bda_ref, bdx_ref, ba_ref, bx_ref, lam_ref, pad, s)
        h = _scan_down(a, sq * (i * xc), rows, s, s1, s2)
        h_ref[...] = h
        g = g_ref[...].astype(F32)
        gr_ref[...] = (h * (g * _sigmoid(g))).astype(BF16)

    blk, vec, mat = _rnn_specs(s)
    return pl.pallas_call(
        body, name="rnn_fwd", grid=(N_CBLK, b),
        in_specs=[blk(N_CBLK), blk(2 * N_CBLK), vec(CONV_W), vec(1), mat, mat, vec(1), vec(1), vec(1)],
        out_specs=[blk(0), blk(0)],
        out_shape=[jax.ShapeDtypeStruct((b, s, D_MODEL), F32), jax.ShapeDtypeStruct((b, s, D_MODEL), BF16)],
        scratch_shapes=[pltpu.VMEM((s, LANES), F32), pltpu.VMEM((s, LANES), F32), pltpu.VMEM((s + 16, LANES), F32)],
        compiler_params=_cparams(("parallel", "parallel")),
    )(zrest3, zrest3, conv_w, conv_b, bda, bdx, ba, bx, lam)


def _rnn_bwd(zrest3, h3, dh3, conv_w, conv_b, bda, bdx, ba, bx, lam):
    b, s, _ = zrest3.shape

    def body(xr_ref, h_ref, dh_ref, cw_ref, cb_ref, bda_ref, bdx_ref, ba_ref, bx_ref, lam_ref,
             dxr_ref, pv_ref, dbd_ref, s1, s2, pad):
        @pl.when(pl.program_id(1) == 0)
        def _():
            pv_ref[...] = jnp.zeros_like(pv_ref)
            dbd_ref[...] = jnp.zeros_like(dbd_ref)

        xr = xr_ref[...].astype(F32)
        rows, (x1, x2, x3), xc, xcb, r, i, sp, a, a2, sq = _rnn_common(
            xr, cw_ref, cb_ref, bda_ref, bdx_ref, ba_ref, bx_ref, lam_ref, pad, s)
        (a_next,) = _shifted(a, (-1,), pad, s)
        g = _scan_up(a_next, dh_ref[...], rows, s, s1, s2)
        (hp,) = _shifted(h_ref[...], (1,), pad, s)
        da = g * hp
        dsq = g * (i * xc)
        di = g * (sq * xc)
        dxc = g * (sq * i)
        dlog = da * a - dsq * (a2 / sq)
        dr = dlog * (-RG_C * sp)
        dpr = dr * (r * (1.0 - r))
        dpi = di * (i * (1.0 - i))
        dprb = dpr.astype(BF16)
        dpib = dpi.astype(BF16)
        dxc = dxc + _dot_nt(dprb, bda_ref[...]) + _dot_nt(dpib, bdx_ref[...])

        up1, up2, up3 = _shifted(dxc, (-1, -2, -3), pad, s)
        dxr = cw_ref[3:4, :] * dxc + cw_ref[2:3, :] * up1 + cw_ref[1:2, :] * up2 + cw_ref[0:1, :] * up3
        dxr_ref[...] = dxr.astype(BF16)

        def colsum(v):
            return jnp.sum(v, axis=0, keepdims=True)

        pv_ref[0:1, :] += colsum(dxc * x3)
        pv_ref[1:2, :] += colsum(dxc * x2)
        pv_ref[2:3, :] += colsum(dxc * x1)
        pv_ref[3:4, :] += colsum(dxc * xr)
        pv_ref[4:5, :] += colsum(dxc)
        pv_ref[5:6, :] += colsum(dpr)
        pv_ref[6:7, :] += colsum(dpi)
        pv_ref[7:8, :] += colsum(dlog * r) * (RG_C * _sigmoid(-lam_ref[...]))
        dbd_ref[0] += _dot_tn(xcb, dprb)
        dbd_ref[1] += _dot_tn(xcb, dpib)

    blk, vec, mat = _rnn_specs(s)
    hblk = pl.BlockSpec((None, s, LANES), lambda cb, i: (i, 0, cb))
    return pl.pallas_call(
        body, name="rnn_bwd", grid=(N_CBLK, b),
        in_specs=[blk(N_CBLK), hblk, hblk, vec(CONV_W), vec(1), mat, mat, vec(1), vec(1), vec(1)],
        out_specs=[hblk, pl.BlockSpec((8, LANES), lambda cb, i: (0, cb)),
                   pl.BlockSpec((None, 2, LANES, LANES), lambda cb, i: (cb, 0, 0, 0))],
        out_shape=[jax.ShapeDtypeStruct((b, s, D_MODEL), BF16), jax.ShapeDtypeStruct((8, D_MODEL), F32),
                   jax.ShapeDtypeStruct((N_CBLK, 2, LANES, LANES), F32)],
        scratch_shapes=[pltpu.VMEM((s, LANES), F32), pltpu.VMEM((s, LANES), F32), pltpu.VMEM((s + 16, LANES), F32)],
        compiler_params=_cparams(("parallel", "arbitrary")),
    )(zrest3, h3, dh3, conv_w, conv_b, bda, bdx, ba, bx, lam)


def _branch_merge(ga, gr, wa, wr, zrest):
    t = ga.shape[0]
    tm = min(512, t)
    tn = 512

    def body(ga_ref, gr_ref, wa_ref, wr_ref, mga_ref, mgr_ref, ya_ref, yr_ref, m_ref):
        ya = _dot(ga_ref[...], wa_ref[...])
        yr = _dot(gr_ref[...], wr_ref[...])
        ya_ref[...] = ya.astype(BF16)
        yr_ref[...] = yr.astype(BF16)
        m_ref[...] = (_sigmoid(mga_ref[...].astype(F32)) * ya + _sigmoid(mgr_ref[...].astype(F32)) * yr).astype(BF16)

    nj = D_MODEL // tn
    act = pl.BlockSpec((tm, D_MODEL), lambda i, j: (i, 0))
    wgt = pl.BlockSpec((D_MODEL, tn), lambda i, j: (0, j))
    out = pl.BlockSpec((tm, tn), lambda i, j: (i, j))
    return pl.pallas_call(
        body, name="branch_merge", grid=(t // tm, nj),
        in_specs=[act, act, wgt, wgt, pl.BlockSpec((tm, tn), lambda i, j: (i, 3 * nj + j)),
                  pl.BlockSpec((tm, tn), lambda i, j: (i, 4 * nj + j))],
        out_specs=[out, out, out],
        out_shape=[jax.ShapeDtypeStruct((t, D_MODEL), BF16), jax.ShapeDtypeStruct((t, D_MODEL), BF16),
                   jax.ShapeDtypeStruct((t, D_MODEL), BF16)],
        compiler_params=_cparams(("parallel", "parallel")),
    )(ga, gr, wa, wr, zrest, zrest)


def _out_loss(m, wout, x2, tgt2, wpost):
    t = m.shape[0]
    tm = min(256, t)

    def body(m_ref, w_ref, x_ref, t_ref, wp_ref, dy_ref, do_ref, acc_ref):
        @pl.when(pl.program_id(0) == 0)
        def _():
            acc_ref[...] = jnp.zeros_like(acc_ref)

        o = _dot(m_ref[...], w_ref[...])
        r2 = lax.rsqrt(jnp.mean(o * o, axis=-1, keepdims=True) + NORM_EPS)
        n = o * r2
        wp = wp_ref[...]
        err = (x_ref[...] + n * wp) - t_ref[...]
        dy = err * (1.0 / D_MODEL)
        dn = dy * wp
        do = r2 * (dn - n * jnp.mean(dn * n, axis=-1, keepdims=True))
        dy_ref[...] = dy
        do_ref[...] = do.astype(BF16)
        acc_ref[0:1, :] += jnp.sum(dy * n, axis=0, keepdims=True)
        acc_ref[1:2, :] += jnp.sum(err * err, axis=0, keepdims=True)

    row = pl.BlockSpec((tm, D_MODEL), lambda i: (i, 0))
    return pl.pallas_call(
        body, name="out_loss", grid=(t // tm,),
        in_specs=[row, pl.BlockSpec((D_MODEL, D_MODEL), lambda i: (0, 0)), row, row,
                  pl.BlockSpec((1, D_MODEL), lambda i: (0, 0))],
        out_specs=[row, row, pl.BlockSpec((8, D_MODEL), lambda i: (0, 0))],
        out_shape=[jax.ShapeDtypeStruct((t, D_MODEL), F32), jax.ShapeDtypeStruct((t, D_MODEL), BF16),
                   jax.ShapeDtypeStruct((8, D_MODEL), F32)],
        compiler_params=_cparams(("arbitrary",)),
    )(m, wout, x2, tgt2, wpost)


def _merge_bwd(do, wout, zrest, ya, yr):
    t = do.shape[0]
    tm = min(512, t)
    tn = 512
    nj = D_MODEL // tn

    def body(do_ref, w_ref, mga_ref, mgr_ref, ya_ref, yr_ref, dya_ref, dyr_ref, dmga_ref, dmgr_ref):
        dm = _dot_nt(do_ref[...], w_ref[...])
        sa = _sigmoid(mga_ref[...].astype(F32))
        sr = _sigmoid(mgr_ref[...].astype(F32))
        dya_ref[...] = (dm * sa).astype(BF16)
        dyr_ref[...] = (dm * sr).astype(BF16)
        dmga_ref[...] = (dm * ya_ref[...].astype(F32) * (sa * (1.0 - sa))).astype(BF16)
        dmgr_ref[...] = (dm * yr_ref[...].astype(F32) * (sr * (1.0 - sr))).astype(BF16)

    out = pl.BlockSpec((tm, tn), lambda i, j: (i, j))
    bf = jax.ShapeDtypeStruct((t, D_MODEL), BF16)
    return pl.pallas_call(
        body, name="merge_bwd", grid=(t // tm, nj),
        in_specs=[pl.BlockSpec((tm, D_MODEL), lambda i, j: (i, 0)), pl.BlockSpec((tn, D_MODEL), lambda i, j: (j, 0)),
                  pl.BlockSpec((tm, tn), lambda i, j: (i, 3 * nj + j)),
                  pl.BlockSpec((tm, tn), lambda i, j: (i, 4 * nj + j)), out, out],
        out_specs=[out, out, out, out],
        out_shape=[bf, bf, bf, bf],
        compiler_params=_cparams(("parallel", "parallel")),
    )(do, wout, zrest, zrest, ya, yr)


def _branch_bwd(dya, dyr, wa, wr, zrest, yatt, ylru):
    t = dya.shape[0]
    tm = min(512, t)
    tn = 512
    nj = D_MODEL // tn

    def body(dya_ref, dyr_ref, wa_ref, wr_ref, ga_ref, gr_ref, ya_ref, yl_ref,
             dyatt_ref, dga_ref, dyl_ref, dgr_ref):
        dga = _dot_nt(dya_ref[...], wa_ref[...])
        dgr = _dot_nt(dyr_ref[...], wr_ref[...])
        g = ga_ref[...].astype(F32)
        sg = _sigmoid(g)
        dyatt_ref[...] = (dga * (g * sg)).astype(BF16)
        dga_ref[...] = (dga * ya_ref[...] * (sg * (1.0 + g * (1.0 - sg)))).astype(BF16)
        g = gr_ref[...].astype(F32)
        sg = _sigmoid(g)
        dyl_ref[...] = dgr * (g * sg)
        dgr_ref[...] = (dgr * yl_ref[...] * (sg * (1.0 + g * (1.0 - sg)))).astype(BF16)

    act = pl.BlockSpec((tm, D_MODEL), lambda i, j: (i, 0))
    wgt = pl.BlockSpec((tn, D_MODEL), lambda i, j: (j, 0))
    out = pl.BlockSpec((tm, tn), lambda i, j: (i, j))
    bf = jax.ShapeDtypeStruct((t, D_MODEL), BF16)
    return pl.pallas_call(
        body, name="branch_bwd", grid=(t // tm, nj),
        in_specs=[act, act, wgt, wgt, pl.BlockSpec((tm, tn), lambda i, j: (i, j)),
                  pl.BlockSpec((tm, tn), lambda i, j: (i, 2 * nj + j)), out, out],
        out_specs=[out, out, out, out],
        out_shape=[bf, bf, jax.ShapeDtypeStruct((t, D_MODEL), F32), bf],
        compiler_params=_cparams(("parallel", "parallel")),
    )(dya, dyr, wa, wr, zrest, zrest, yatt, ylru)


def _dh_partial(parts, after, name):
    t = parts[0][0].shape[0]
    tm = min(256, t)
    np_ = len(parts)

    def body(*refs):
        o_ref = refs[-1]
        acc = _dot(refs[0][...], refs[np_][...])
        for p in range(1, np_):
            acc = acc + _dot(refs[p][...], refs[np_ + p][...])
        o_ref[...] = acc

    in_specs = [pl.BlockSpec((tm, dz.shape[1]), lambda i: (i, 0)) for dz, _ in parts]
    in_specs += [pl.BlockSpec(w.shape, lambda i: (0, 0)) for _, w in parts]
    in_specs += [pl.BlockSpec(after.shape, lambda i: (0, 0))]
    return pl.pallas_call(
        body, name=name, grid=(t // tm,),
        in_specs=in_specs,
        out_specs=pl.BlockSpec((tm, D_MODEL), lambda i: (i, 0)),
        out_shape=jax.ShapeDtypeStruct((t, D_MODEL), F32),
        compiler_params=_cparams(("parallel",), vmem_mb=48),
    )(*[dz for dz, _ in parts], *[w for _, w in parts], after)


def _dh_final(parts, acc_in, x2, dy, wpre):
    t = x2.shape[0]
    tm = min(256, t)
    np_ = len(parts)

    def body(*refs):
        acc_ref, x_ref, dy_ref, w_ref = refs[2 * np_:2 * np_ + 4]
        gx_ref, pw_ref = refs[2 * np_ + 4:]

        @pl.when(pl.program_id(0) == 0)
        def _():
            pw_ref[...] = jnp.zeros_like(pw_ref)

        dh = acc_ref[...]
        for p in range(np_):
            dh = dh + _dot(refs[p][...], refs[np_ + p][...])
        x = x_ref[...]
        r = lax.rsqrt(jnp.mean(x * x, axis=-1, keepdims=True) + NORM_EPS)
        xn = x * r
        dxn = dh * w_ref[...]
        gx_ref[...] = r * (dxn - xn * jnp.mean(dxn * xn, axis=-1, keepdims=True)) + dy_ref[...]
        pw_ref[0:1, :] += jnp.sum(dh * xn, axis=0, keepdims=True)

    row = pl.BlockSpec((tm, D_MODEL), lambda i: (i, 0))
    in_specs = [pl.BlockSpec((tm, dz.shape[1]), lambda i: (i, 0)) for dz, _ in parts]
    in_specs += [pl.BlockSpec(w.shape, lambda i: (0, 0)) for _, w in parts]
    in_specs += [row, row, row, pl.BlockSpec((1, D_MODEL), lambda i: (0, 0))]
    return pl.pallas_call(
        body, name="dh_final", grid=(t // tm,),
        in_specs=in_specs,
        out_specs=[row, pl.BlockSpec((8, D_MODEL), lambda i: (0, 0))],
        out_shape=[jax.ShapeDtypeStruct((t, D_MODEL), F32), jax.ShapeDtypeStruct((8, D_MODEL), F32)],
        compiler_params=_cparams(("arbitrary",), vmem_mb=48),
    )(*[dz for dz, _ in parts], *[w for _, w in parts], acc_in, x2, dy, wpre)


def _adamw(w, g, m, v):
    m = ADAM_B1 * m + (1.0 - ADAM_B1) * g
    v = ADAM_B2 * v + (1.0 - ADAM_B2) * (g * g)
    m_hat = m / (1.0 - ADAM_B1 ** ADAM_STEP)
    v_hat = v / (1.0 - ADAM_B2 ** ADAM_STEP)
    delta = -ADAM_LR * (m_hat / (jnp.sqrt(v_hat) + ADAM_EPS) + ADAM_WD * w)
    return delta, m, v


def _reduce_adamw(own, parts, place, w, m, v, name):
    r, c = w.shape
    blk, nblk, at = _blocks_2d(r, c)

    def body(place_ref, own_ref, p_ref, w_ref, m_ref, v_ref, g_ref, d_ref, nm_ref, nv_ref):
        mine = place_ref[1]
        own_blk = own_ref[...]
        g = jnp.where(mine == 0, own_blk, p_ref[0].astype(F32))
        for j in range(1, N_CHIPS):
            g = g + jnp.where(mine == j, own_blk, p_ref[j].astype(F32))
        d, nm, nv = _adamw(w_ref[...], g, m_ref[...], v_ref[...])
        g_ref[...] = g
        d_ref[...] = d
        nm_ref[...] = nm
        nv_ref[...] = nv

    row = pl.BlockSpec(blk, lambda i, pr: at(i))
    sh = jax.ShapeDtypeStruct((r, c), F32)
    grid_spec = pltpu.PrefetchScalarGridSpec(
        num_scalar_prefetch=1, grid=(nblk,),
        in_specs=[row, pl.BlockSpec((N_CHIPS,) + blk, lambda i, pr: (0,) + at(i)), row, row, row],
        out_specs=[row, row, row, row])
    return pl.pallas_call(
        body, name=name, grid_spec=grid_spec, out_shape=[sh, sh, sh, sh],
        compiler_params=_cparams(("parallel",)),
    )(place, own, parts, w, m, v)


def _interleave_qkv(a):
    lead = a.shape[:-1]
    return a.reshape(lead + (3, HEAD_PAIRS, LANES)).swapaxes(-3, -2).reshape(lead + (3 * D_MODEL,))


def _deinterleave_qkv(a):
    lead = a.shape[:-1]
    return a.reshape(lead + (HEAD_PAIRS, 3, LANES)).swapaxes(-3, -2).reshape(lead + (3 * D_MODEL,))


def _interleave_rows(a):
    return a.reshape(3, HEAD_PAIRS, LANES, a.shape[1]).swapaxes(0, 1).reshape(a.shape)


def _deinterleave_rows(a):
    return a.reshape(HEAD_PAIRS, 3, LANES, a.shape[1]).swapaxes(0, 1).reshape(a.shape)


def _pack_small(pre, conv_b, rg_ba, rg_bx, lam, post, loss_row, b_in, conv_w_full, rg_wa, rg_wx):
    z = jnp.zeros((1, D_MODEL), F32)
    b_used = jnp.concatenate([b_in[:, 0:3 * D_MODEL], b_in[:, 3 * D_MODEL + HEADS:IN_TOTAL]], axis=1)
    b_f = jnp.pad(b_in[:, 3 * D_MODEL:3 * D_MODEL + HEADS], ((0, 0), (0, D_MODEL - HEADS)))
    return jnp.concatenate([
        pre, conv_b, rg_ba, rg_bx, lam, post, loss_row, z,
        b_used.reshape(9, D_MODEL), b_f, conv_w_full, z, z,
        rg_wa.reshape(64, D_MODEL), rg_wx.reshape(64, D_MODEL)], axis=0)


def _unpack_small(p):
    b_used = p[8:17].reshape(1, 9 * D_MODEL)
    b_in = jnp.concatenate([b_used[:, 0:3 * D_MODEL], p[17:18, 0:HEADS], b_used[:, 3 * D_MODEL:]], axis=1)
    return dict(pre_norm_w=p[0:1], conv_b=p[1:2], rg_ba=p[2:3], rg_bx=p[3:4], rg_lambda=p[4:5],
                post_norm_w=p[5:6], loss_row=p[6:7], b_in=b_in, conv_w_full=p[18:22],
                rg_wa=p[24:88].reshape(1, 16, 64, 64), rg_wx=p[88:152].reshape(1, 16, 64, 64))


def _reduce_small(parts, w, m, v):
    def body(p_ref, w_ref, m_ref, v_ref, g_ref, d_ref, nm_ref, nv_ref):
        g = p_ref[0]
        for j in range(1, N_DEV):
            g = g + p_ref[j]
        d, nm, nv = _adamw(w_ref[...], g, m_ref[...], v_ref[...])
        g_ref[...] = g
        d_ref[...] = d
        nm_ref[...] = nm
        nv_ref[...] = nv

    sh = jax.ShapeDtypeStruct((SMALL_ROWS, D_MODEL), F32)
    return pl.pallas_call(body, name="reduce_small", out_shape=[sh, sh, sh, sh])(parts, w, m, v)


def kernel(x, pre_norm_w, w_in, b_in, conv_w, conv_b, rg_wa, rg_ba, rg_wx, rg_bx, rg_lambda, w_branch_a, w_branch_r, w_out, post_norm_w, loss_target, m_pre_norm_w, m_w_in, m_b_in, m_conv_w, m_conv_b, m_rg_wa, m_rg_ba, m_rg_wx, m_rg_bx, m_rg_lambda, m_w_branch_a, m_w_branch_r, m_w_out, m_post_norm_w, v_pre_norm_w, v_w_in, v_b_in, v_conv_w, v_conv_b, v_rg_wa, v_rg_ba, v_rg_wx, v_rg_bx, v_rg_lambda, v_w_branch_a, v_w_branch_r, v_w_out, v_post_norm_w):
    b, s, _ = x.shape
    t = b * s
    me = 4 * lax.axis_index("x") + 2 * lax.axis_index("y") + lax.axis_index("c")
    shard_rows = D_MODEL // N_DEV

    place = jnp.stack([lax.axis_index("c"), 2 * lax.axis_index("x") + lax.axis_index("y")]).astype(jnp.int32)
    w_in_all = _gather(w_in[0].T.astype(BF16), "gather_w_in")
    wt_full = w_in_all.reshape(IN_TOTAL, D_MODEL)
    conv_terms = jnp.concatenate(_split3(conv_w[0]), axis=0)
    conv_pad = jnp.pad(conv_terms, ((0, 16 - 3 * CONV_W), (0, D_MODEL - LANES)))
    sq_stack = jnp.concatenate([w_branch_a[0].astype(BF16), w_branch_r[0].astype(BF16), w_out[0].astype(BF16),
                                conv_pad], axis=0)
    sq_sems, sq_src, sq_land, sq_token = _gather_start(sq_stack, w_in_all, "gather_w_sq_start")

    w_qkv = _interleave_rows(wt_full[0:3 * D_MODEL])
    w_f = jnp.pad(wt_full[3 * D_MODEL:3 * D_MODEL + HEADS], ((0, LANES - HEADS), (0, 0)))
    w_rest = wt_full[3 * D_MODEL + HEADS:IN_USED]
    b_qkv = _interleave_qkv(b_in[:, 0:3 * D_MODEL]) + sq_token[0, 0]
    b_f = jnp.pad(b_in[:, 3 * D_MODEL:3 * D_MODEL + HEADS], ((0, 0), (0, LANES - HEADS)))
    b_rest = b_in[:, 3 * D_MODEL + HEADS:IN_USED]

    def blockdiag(w):
        w2 = w.reshape(N_CBLK, 2, HEAD_DIM, HEAD_DIM)
        zz = jnp.zeros((N_CBLK, HEAD_DIM, HEAD_DIM), w.dtype)
        top = jnp.concatenate([w2[:, 0], zz], axis=2)
        bot = jnp.concatenate([zz, w2[:, 1]], axis=2)
        return jnp.concatenate([top, bot], axis=1).astype(BF16)

    bda, bdx = blockdiag(rg_wa[0]), blockdiag(rg_wx[0])

    x2 = x.reshape(t, D_MODEL)
    tgt2 = loss_target.reshape(t, D_MODEL)
    h = _prenorm(x2, pre_norm_w)
    qkv = _mm_bias(h, w_qkv, b_qkv, BF16, "inproj_qkv")
    zrest = _mm_bias(h, w_rest, b_rest, BF16, "inproj_rest")
    zf = _mm_bias(h, w_f, b_f, F32, "inproj_f")
    qkv3 = qkv.reshape(b, s, 3 * D_MODEL)
    zrest3 = zrest.reshape(b, s, 5 * D_MODEL)
    zf3 = zf.reshape(b, s, LANES)
    nq = s // ATT_TILE
    cexp3, crow = _fgate_fwd(zf3)
    crow5 = crow.reshape(b, HEAD_PAIRS, 2, nq, ATT_TILE)
    yatt3, lse5, ga3 = _attn_fwd(qkv3, cexp3, crow5, zrest3)

    sq_all = _gather_wait(sq_sems, sq_src, sq_land, ga3, "gather_w_sq_wait")
    sq_all = lax.dynamic_update_slice(sq_all, sq_stack[None], (me, 0, 0))
    wa = sq_all[:, 0:shard_rows].reshape(D_MODEL, D_MODEL)
    wr = sq_all[:, shard_rows:2 * shard_rows].reshape(D_MODEL, D_MODEL)
    wo = sq_all[:, 2 * shard_rows:3 * shard_rows].reshape(D_MODEL, D_MODEL)
    conv_all = sq_all[:, 3 * shard_rows:3 * shard_rows + 3 * CONV_W, 0:LANES].astype(F32)
    conv_all = (conv_all[:, 0:CONV_W] + conv_all[:, CONV_W:2 * CONV_W]) + conv_all[:, 2 * CONV_W:3 * CONV_W]
    conv_full = conv_all.transpose(1, 0, 2).reshape(CONV_W, D_MODEL)

    ylru3, gr3 = _rnn_fwd(zrest3, conv_full, conv_b, bda, bdx, rg_ba, rg_bx, rg_lambda)
    ga, gr = ga3.reshape(t, D_MODEL), gr3.reshape(t, D_MODEL)
    ya, yr, mm = _branch_merge(ga, gr, wa, wr, zrest)
    dy, do, acc_out = _out_loss(mm, wo, x2, tgt2, post_norm_w)

    dya, dyr, dz_mga, dz_mgr = _merge_bwd(do, wo, zrest, ya, yr)
    dyatt, dz_ga, dylru, dz_gr = _branch_bwd(dya, dyr, wa, wr, zrest, yatt3.reshape(t, D_MODEL),
                                             ylru3.reshape(t, D_MODEL))
    dz_xr3, pvec, dbd = _rnn_bwd(zrest3, ylru3, dylru.reshape(b, s, D_MODEL), conv_full, conv_b, bda, bdx,
                                 rg_ba, rg_bx, rg_lambda)
    dqkv3, dc3 = _attn_bwd(qkv3, dyatt.reshape(b, s, D_MODEL), yatt3, lse5, crow5, cexp3)
    dz_f = _fgate_bwd(dc3, zf3).reshape(t, LANES)
    dz_qkv = dqkv3.reshape(t, 3 * D_MODEL)
    dz_xr = dz_xr3.reshape(t, D_MODEL)

    dw_qkv, db_qkv = _mm_tn(dz_qkv, h, "dw_qkv")
    dw_f, db_f = _mm_tn(dz_f, h, "dw_f")
    dw_parts, db_parts = [], []
    for nm, dzp in (("ga", dz_ga), ("xr", dz_xr), ("gr", dz_gr), ("mga", dz_mga), ("mgr", dz_mgr)):
        dwp, dbp = _mm_tn(dzp, h, "dw_" + nm)
        dw_parts.append(dwp)
        db_parts.append(dbp[0:1])
    dw_a, _ = _mm_tn(ga, dya, "dw_a")
    dw_r, _ = _mm_tn(gr, dyr, "dw_r")
    dw_o, _ = _mm_tn(mm, do, "dw_o")

    zeros_tail = jnp.zeros((IN_TOTAL - IN_USED, D_MODEL), F32)
    dwt_full = jnp.concatenate([_deinterleave_rows(dw_qkv), dw_f[0:HEADS]] + dw_parts + [zeros_tail], axis=0)
    dw_in_send = dwt_full.reshape(N_CHIPS, 2, W_SHARD, D_MODEL).transpose(1, 0, 2, 3)
    by_dest = lambda a: a.reshape(N_CHIPS, 2, shard_rows, D_MODEL).transpose(1, 0, 2, 3)
    dw_sq_send = jnp.concatenate([by_dest(dw_a), by_dest(dw_r), by_dest(dw_o)], axis=2)

    sib_in, sib_sq = _swap_with_sibling([dw_in_send, dw_sq_send], "swap_dw")
    chip_in, own_in = _pair_add(dw_in_send, sib_in, place, "pair_add_in")
    chip_sq, own_sq = _pair_add(dw_sq_send, sib_sq, place, "pair_add_sq")
    sems, sent, lands, token = _exchange_chips_start([chip_in, chip_sq], "exchange_dw_start")

    wt = lambda lo: w_rest[lo * D_MODEL:(lo + 1) * D_MODEL]
    dh_a = _dh_partial([(dz_qkv, w_qkv), (dz_f, w_f)], token, "dh_qkv")
    grad_x2, acc_pre = _dh_final(
        [(dz_ga, wt(0)), (dz_xr, wt(1)), (dz_gr, wt(2)), (dz_mga, wt(3)), (dz_mgr, wt(4))],
        dh_a, x2, dy, pre_norm_w)

    db_in_full = jnp.concatenate([_deinterleave_qkv(db_qkv[0:1]), db_f[0:1, 0:HEADS]] + db_parts
                                 + [jnp.zeros((1, IN_TOTAL - IN_USED), F32)], axis=1)
    d_rg_wa = jnp.stack([dbd[:, 0, 0:HEAD_DIM, 0:HEAD_DIM], dbd[:, 0, HEAD_DIM:, HEAD_DIM:]], axis=1)
    d_rg_wx = jnp.stack([dbd[:, 1, 0:HEAD_DIM, 0:HEAD_DIM], dbd[:, 1, HEAD_DIM:, HEAD_DIM:]], axis=1)
    small_g = _pack_small(acc_pre[0:1], pvec[4:5], pvec[5:6], pvec[6:7], pvec[7:8], acc_out[0:1], acc_out[1:2],
                          db_in_full, pvec[0:4], d_rg_wa, d_rg_wx)
    sm_sems, sm_src, sm_land, sm_token = _gather_start(small_g, grad_x2, "gather_small_start")
    recv_in, recv_sq = _exchange_chips_wait(sems, sent, lands, sm_token, "exchange_dw_wait")

    g_in, d_in, nm_in, nv_in = [a.T for a in _reduce_adamw(
        own_in, recv_in, place, w_in[0].T, m_w_in[0].T, v_w_in[0].T, "adamw_w_in")]
    sq_w = jnp.concatenate([w_branch_a[0], w_branch_r[0], w_out[0]], axis=0)
    sq_m = jnp.concatenate([m_w_branch_a[0], m_w_branch_r[0], m_w_out[0]], axis=0)
    sq_v = jnp.concatenate([v_w_branch_a[0], v_w_branch_r[0], v_w_out[0]], axis=0)
    g_sq, d_sq, nm_sq, nv_sq = _reduce_adamw(own_sq, recv_sq, place, sq_w, sq_m, sq_v, "adamw_w_sq")
    small_all = _gather_wait(sm_sems, sm_src, sm_land, d_sq, "gather_small_wait")
    small_all = lax.dynamic_update_slice(small_all, small_g[None], (me, 0, 0))

    def place_conv(a):
        return lax.dynamic_update_slice(jnp.zeros((CONV_W, D_MODEL), F32), a[0], (0, me * LANES))

    zrow = jnp.zeros((1, D_MODEL), F32)
    small_w = _pack_small(pre_norm_w, conv_b, rg_ba, rg_bx, rg_lambda, post_norm_w, zrow, b_in,
                          place_conv(conv_w), rg_wa[0], rg_wx[0])
    small_m = _pack_small(m_pre_norm_w, m_conv_b, m_rg_ba, m_rg_bx, m_rg_lambda, m_post_norm_w, zrow, m_b_in,
                          place_conv(m_conv_w), m_rg_wa[0], m_rg_wx[0])
    small_v = _pack_small(v_pre_norm_w, v_conv_b, v_rg_ba, v_rg_bx, v_rg_lambda, v_post_norm_w, zrow, v_b_in,
                          place_conv(v_conv_w), v_rg_wa[0], v_rg_wx[0])
    outs_small = [_unpack_small(p) for p in _reduce_small(small_all, small_w, small_m, small_v)]

    loss = (0.5 / D_MODEL) * jnp.sum(outs_small[0]["loss_row"])

    def leaf(kind, name):
        if name == "w_in":
            return (g_in, d_in, nm_in, nv_in)[kind][None]
        if name in ("w_branch_a", "w_branch_r", "w_out"):
            j = ("w_branch_a", "w_branch_r", "w_out").index(name)
            return (g_sq, d_sq, nm_sq, nv_sq)[kind][None, j * shard_rows:(j + 1) * shard_rows]
        if name == "conv_w":
            return lax.dynamic_slice(outs_small[kind]["conv_w_full"], (0, me * LANES), (CONV_W, LANES))[None]
        return outs_small[kind][name]

    names = ["pre_norm_w", "w_in", "b_in", "conv_w", "conv_b", "rg_wa", "rg_ba", "rg_wx", "rg_bx", "rg_lambda",
             "w_branch_a", "w_branch_r", "w_out", "post_norm_w"]
    out = [loss, grad_x2.reshape(b, s, D_MODEL)]
    for kind in range(4):
        out += [leaf(kind, nm) for nm in names]
    return tuple(out)
```

```python
import jax
import jax.numpy as jnp
from jax import lax
from jax.experimental import pallas as pl
from jax.experimental.pallas import tpu as pltpu

F32 = jnp.float32
BF16 = jnp.bfloat16

N_DEV = 8
D_MODEL = 1024
HEADS = 16
HEAD_DIM = 64
HEAD_PAIRS = HEADS // 2
LANES = 128
N_CBLK = D_MODEL // LANES
CONV_W = 4
RG_C = 8.0
NORM_EPS = 1e-6
MASK_VALUE = -1e30
IN_USED = 8208
IN_TOTAL = 9232
W_SHARD = IN_TOTAL // N_DEV

ADAM_LR = 0.001
ADAM_B1 = 0.9
ADAM_B2 = 0.999
ADAM_EPS = 1e-08
ADAM_WD = 0.01
ADAM_STEP = 10

ATT_TILE = 256
SCAN_TILE = 256
SMALL_ROWS = 152


def _cparams(sem=None, vmem_mb=None):
    kw = {}
    if sem is not None:
        kw["dimension_semantics"] = sem
    if vmem_mb is not None:
        kw["vmem_limit_bytes"] = vmem_mb * 1024 * 1024
    return pltpu.CompilerParams(**kw)


def _sigmoid(x):
    return 1.0 / (1.0 + jnp.exp(-x))


def _softplus(x):
    return jnp.maximum(x, 0.0) + jnp.log1p(jnp.exp(-jnp.abs(x)))


def _one_minus_exp(y, exp_y):
    series = -y * (1.0 + y * (1.0 / 2 + y * (1.0 / 6 + y * (1.0 / 24 + y * (1.0 / 120)))))
    return jnp.where(y > -0.0625, series, 1.0 - exp_y)


def _split3(x):
    hi = x.astype(BF16)
    r1 = x - hi.astype(F32)
    mid = r1.astype(BF16)
    lo = (r1 - mid.astype(F32)).astype(BF16)
    return hi, mid, lo


def _dot(a, b):
    return jnp.dot(a, b, preferred_element_type=F32)


def _dot_nt(a, b):
    return lax.dot_general(a, b, (((1,), (1,)), ((), ())), preferred_element_type=F32)


def _dot_tn(a, b):
    return lax.dot_general(a, b, (((0,), (0,)), ((), ())), preferred_element_type=F32)


def _iota(shape, dim):
    return lax.broadcasted_iota(jnp.int32, shape, dim)


_ANY = pl.BlockSpec(memory_space=pl.ANY)
_MESH = pl.DeviceIdType.MESH
N_CHIPS = 4


def _place():
    x, y, c = lax.axis_index("x"), lax.axis_index("y"), lax.axis_index("c")
    other_chips = [(1 - x, y), (x, 1 - y), (1 - x, 1 - y)]
    return x, y, c, other_chips


def _gather(x_shard, name):
    def body(x_ref, out_ref, send_sems, recv_sems, local_sem):
        x, y, c, chips = _place()
        me, sibling = (x, y, c), (x, y, 1 - c)

        def slot(p):
            return out_ref.at[4 * p[0] + 2 * p[1] + p[2]]

        def copy(k, block, to, src=None):
            return pltpu.make_async_remote_copy(
                src_ref=slot(block) if src is None else src, dst_ref=slot(block),
                send_sem=send_sems.at[k], recv_sem=recv_sems.at[k], device_id=to, device_id_type=_MESH)

        mine = pltpu.make_async_copy(x_ref, slot(me), local_sem)
        mine.start()
        first = [copy(0, me, sibling, src=x_ref)]
        first += [copy(1 + j, me, (*chip, c), src=x_ref) for j, chip in enumerate(chips)]
        for cp in first:
            cp.start()
        passed = [copy(4 + j, (*chip, c), sibling) for j, chip in enumerate(chips)]
        for j, chip in enumerate(chips):
            copy(1 + j, (*chip, c), me).wait_recv()
            passed[j].start()
        copy(0, sibling, me).wait_recv()
        for j, chip in enumerate(chips):
            copy(4 + j, (*chip, 1 - c), me).wait_recv()
        for cp in first + passed:
            cp.wait_send()
        mine.wait()

    return pl.pallas_call(
        body, name=name,
        out_shape=jax.ShapeDtypeStruct((N_DEV,) + tuple(x_shard.shape), x_shard.dtype),
        in_specs=[_ANY], out_specs=_ANY,
        scratch_shapes=[pltpu.SemaphoreType.DMA((7,)), pltpu.SemaphoreType.DMA((7,)), pltpu.SemaphoreType.DMA],
    )(x_shard)


def _swap_with_sibling(srcs, name):
    n = len(srcs)

    def body(*refs):
        src_refs, out_refs = refs[:n], refs[n:2 * n]
        send_sems, recv_sems = refs[2 * n:]
        x, y, c, _ = _place()
        cps = [pltpu.make_async_remote_copy(
            src_ref=src_refs[i].at[1 - c], dst_ref=out_refs[i], send_sem=send_sems.at[i], recv_sem=recv_sems.at[i],
            device_id=(x, y, 1 - c), device_id_type=_MESH) for i in range(n)]
        for cp in cps:
            cp.start()
        for cp in cps:
            cp.wait()

    return pl.pallas_call(
        body, name=name,
        out_shape=[jax.ShapeDtypeStruct(a.shape[1:], a.dtype) for a in srcs],
        in_specs=[_ANY] * n, out_specs=[_ANY] * n,
        scratch_shapes=[pltpu.SemaphoreType.DMA((n,)), pltpu.SemaphoreType.DMA((n,))],
    )(*srcs)


def _blocks_2d(r, c):
    if r % 128 == 0:
        return (128, c), r // 128, lambda i: (i, 0)
    return (r, 256), c // 256, lambda i: (0, i)


def _pair_add(src, recv, place, name):
    _, _, r, c = src.shape
    blk, nblk, at = _blocks_2d(r, c)

    def body(place_ref, a_ref, b_ref, q16_ref, own_ref):
        q = a_ref[...] + b_ref[...]
        q16_ref[...] = q.astype(BF16)

        @pl.when(pl.program_id(1) == place_ref[1])
        def _():
            own_ref[...] = q

    grid_spec = pltpu.PrefetchScalarGridSpec(
        num_scalar_prefetch=1, grid=(nblk, N_CHIPS),
        in_specs=[pl.BlockSpec((None, None) + blk, lambda i, j, pr: (pr[0], j) + at(i)),
                  pl.BlockSpec((None,) + blk, lambda i, j, pr: (j,) + at(i))],
        out_specs=[pl.BlockSpec((None,) + blk, lambda i, j, pr: (j,) + at(i)),
                   pl.BlockSpec(blk, lambda i, j, pr: at(i))])
    return pl.pallas_call(
        body, name=name, grid_spec=grid_spec,
        out_shape=[jax.ShapeDtypeStruct((N_CHIPS, r, c), BF16), jax.ShapeDtypeStruct((r, c), F32)],
        compiler_params=_cparams(("parallel", "arbitrary")),
    )(place, src, recv)


_HBM = pl.BlockSpec(memory_space=pltpu.HBM)
_SEM = pl.BlockSpec(memory_space=pltpu.SEMAPHORE)
_DATAFLOW = pltpu.SideEffectType.DATAFLOW_SIDE_EFFECTING


def _chip_copy(src_ref, land_ref, send_sem, recv_sem, k, chips, c, land):
    chip = chips[k]
    return pltpu.make_async_remote_copy(
        src_ref=src_ref.at[2 * chip[0] + chip[1]], dst_ref=land_ref.at[land],
        send_sem=send_sem, recv_sem=recv_sem, device_id=(*chip, c), device_id_type=_MESH)


def _exchange_chips_start(srcs, name):
    n = len(srcs)
    ncp = 3 * n

    def body(*refs):
        src_refs, land_refs = refs[:n], refs[n:2 * n]
        sems = refs[4 * n:4 * n + 2 * ncp]
        token = refs[-1]
        x, y, c, chips = _place()
        for i in range(n):
            for k in range(3):
                j = 3 * i + k
                _chip_copy(src_refs[i], land_refs[i], sems[j], sems[ncp + j], k, chips, c, 2 * x + y).start()
        token[...] = jnp.zeros_like(token)

    hbm = [pltpu.HBM(a.shape, a.dtype) for a in srcs]
    lands = [pltpu.with_memory_space_constraint(lax.empty(a.shape, a.dtype), pltpu.HBM) for a in srcs]
    res = pl.pallas_call(
        body, name=name,
        out_shape=(*hbm, *hbm, *([pltpu.SemaphoreType.DMA(())] * (2 * ncp)), jax.ShapeDtypeStruct((8, LANES), F32)),
        in_specs=[_HBM] * (2 * n),
        out_specs=(*([_HBM] * (2 * n)), *([_SEM] * (2 * ncp)), pl.BlockSpec(memory_space=pltpu.VMEM)),
        input_output_aliases={i: i for i in range(2 * n)},
        compiler_params=pltpu.CompilerParams(has_side_effects=_DATAFLOW),
    )(*[pltpu.with_memory_space_constraint(a, pltpu.HBM) for a in srcs], *lands)
    return list(res[2 * n:2 * n + 2 * ncp]), list(res[:n]), list(res[n:2 * n]), res[-1]


def _exchange_chips_wait(sems, srcs, lands, after, name):
    n = len(srcs)
    ncp = 3 * n

    def body(*refs):
        src_refs, land_refs = refs[:n], refs[n:2 * n]
        sem_refs = refs[2 * n:2 * n + 2 * ncp]
        x, y, c, chips = _place()
        for i in range(n):
            for k in range(3):
                j = 3 * i + k
                cp = _chip_copy(src_refs[i], land_refs[i], sem_refs[j], sem_refs[ncp + j], k, chips, c,
                                2 * chips[k][0] + chips[k][1])
                cp.wait_send()
                cp.wait_recv()

    hbm = [pltpu.HBM(a.shape, a.dtype) for a in srcs]
    res = pl.pallas_call(
        body, name=name, out_shape=(*hbm, *hbm),
        in_specs=[_HBM] * (2 * n) + [_SEM] * (2 * ncp) + [_ANY], out_specs=tuple([_HBM] * (2 * n)),
        input_output_aliases={i: i for i in range(2 * n)},
        compiler_params=pltpu.CompilerParams(has_side_effects=_DATAFLOW),
    )(*srcs, *lands, *sems, after)
    return list(res[n:2 * n])


def _peer_copy(src_ref, land_ref, send_sem, recv_sem, k, place, land):
    x, y, c = place
    peer = (1 - x if k & 4 else x, 1 - y if k & 2 else y, 1 - c if k & 1 else c)
    return pltpu.make_async_remote_copy(
        src_ref=src_ref, dst_ref=land_ref.at[land], send_sem=send_sem, recv_sem=recv_sem,
        device_id=peer, device_id_type=_MESH)


def _gather_start(x_shard, after, name):
    npeer = N_DEV - 1

    def body(x_ref, land_ref, after_ref, x_thru, land_thru, *rest):
        sems, token = rest[:2 * npeer], rest[-1]
        x, y, c, _ = _place()
        for k in range(1, N_DEV):
            _peer_copy(x_ref, land_ref, sems[k - 1], sems[npeer + k - 1], k, (x, y, c), 4 * x + 2 * y + c).start()
        token[...] = jnp.zeros_like(token)

    land = pltpu.with_memory_space_constraint(lax.empty((N_DEV,) + tuple(x_shard.shape), x_shard.dtype), pltpu.HBM)
    res = pl.pallas_call(
        body, name=name,
        out_shape=(pltpu.HBM(x_shard.shape, x_shard.dtype), pltpu.HBM(land.shape, land.dtype),
                   *([pltpu.SemaphoreType.DMA(())] * (2 * npeer)), jax.ShapeDtypeStruct((8, LANES), F32)),
        in_specs=[_HBM, _HBM, _ANY],
        out_specs=(_HBM, _HBM, *([_SEM] * (2 * npeer)), pl.BlockSpec(memory_space=pltpu.VMEM)),
        input_output_aliases={0: 0, 1: 1},
        compiler_params=pltpu.CompilerParams(has_side_effects=_DATAFLOW),
    )(pltpu.with_memory_space_constraint(x_shard, pltpu.HBM), land, after)
    return list(res[2:2 + 2 * npeer]), res[0], res[1], res[-1]


def _gather_wait(sems, src, land, after, name):
    npeer = N_DEV - 1

    def body(x_ref, land_ref, *rest):
        sem_refs = rest[:2 * npeer]
        x, y, c, _ = _place()
        for k in range(1, N_DEV):
            peer_index = (4 * x + 2 * y + c) ^ k
            cp = _peer_copy(x_ref, land_ref, sem_refs[k - 1], sem_refs[npeer + k - 1], k, (x, y, c), peer_index)
            cp.wait_send()
            cp.wait_recv()

    res = pl.pallas_call(
        body, name=name, out_shape=(pltpu.HBM(src.shape, src.dtype), pltpu.HBM(land.shape, land.dtype)),
        in_specs=[_HBM, _HBM] + [_SEM] * (2 * npeer) + [_ANY], out_specs=(_HBM, _HBM),
        input_output_aliases={0: 0, 1: 1},
        compiler_params=pltpu.CompilerParams(has_side_effects=_DATAFLOW),
    )(src, land, *sems, after)
    return res[1]


def _prenorm(x2, w):
    t = x2.shape[0]
    tm = min(512, t)

    def body(x_ref, w_ref, h_ref):
        x = x_ref[...]
        r = lax.rsqrt(jnp.mean(x * x, axis=-1, keepdims=True) + NORM_EPS)
        h_ref[...] = (x * r * w_ref[...]).astype(BF16)

    return pl.pallas_call(
        body, name="prenorm", grid=(t // tm,),
        in_specs=[pl.BlockSpec((tm, D_MODEL), lambda i: (i, 0)), pl.BlockSpec((1, D_MODEL), lambda i: (0, 0))],
        out_specs=pl.BlockSpec((tm, D_MODEL), lambda i: (i, 0)),
        out_shape=jax.ShapeDtypeStruct((t, D_MODEL), BF16),
        compiler_params=_cparams(("parallel",)),
    )(x2, w)


def _mm_bias(a, bt, bias, out_dtype, name):
    m, k = a.shape
    n = bt.shape[0]
    tm = min(512, m)
    tn = min(1024, n)

    def body(a_ref, bt_ref, bias_ref, o_ref):
        o_ref[...] = (_dot_nt(a_ref[...], bt_ref[...]) + bias_ref[...]).astype(o_ref.dtype)

    return pl.pallas_call(
        body, name=name, grid=(n // tn, m // tm),
        in_specs=[pl.BlockSpec((tm, k), lambda j, i: (i, 0)), pl.BlockSpec((tn, k), lambda j, i: (j, 0)),
                  pl.BlockSpec((1, tn), lambda j, i: (0, j))],
        out_specs=pl.BlockSpec((tm, tn), lambda j, i: (i, j)),
        out_shape=jax.ShapeDtypeStruct((m, n), out_dtype),
        compiler_params=_cparams(("parallel", "parallel")),
    )(a, bt, bias)


def _mm_tn(a, b, name):
    t, m = a.shape
    n = b.shape[1]
    tm = min(1024, m)
    tk = min(512, t)

    def body(a_ref, b_ref, o_ref, s_ref):
        kk = pl.program_id(1)

        @pl.when(kk == 0)
        def _():
            o_ref[...] = jnp.zeros_like(o_ref)
            s_ref[...] = jnp.zeros_like(s_ref)

        aa = a_ref[...]
        o_ref[...] += _dot_tn(aa, b_ref[...])
        s_ref[0:1, :] += jnp.sum(aa.astype(F32), axis=0, keepdims=True)

    return pl.pallas_call(
        body, name=name, grid=(m // tm, t // tk),
        in_specs=[pl.BlockSpec((tk, tm), lambda i, kk: (kk, i)), pl.BlockSpec((tk, n), lambda i, kk: (kk, 0))],
        out_specs=[pl.BlockSpec((tm, n), lambda i, kk: (i, 0)), pl.BlockSpec((8, tm), lambda i, kk: (0, i))],
        out_shape=[jax.ShapeDtypeStruct((m, n), F32), jax.ShapeDtypeStruct((8, m), F32)],
        compiler_params=_cparams(("parallel", "arbitrary")),
    )(a, b)


def _fgate_fwd(zf3):
    b, s, _ = zf3.shape
    tb = SCAN_TILE
    nb = s // tb

    def body(z_ref, cexp_ref, crow_ref):
        tri = (_iota((tb, tb), 1) <= _iota((tb, tb), 0)).astype(BF16)
        expand = ((_iota((LANES, D_MODEL), 1) >> 6) == _iota((LANES, D_MODEL), 0)).astype(BF16)
        carry = jnp.zeros((1, LANES), F32)
        for i in range(nb):
            rows = slice(i * tb, (i + 1) * tb)
            z = z_ref[rows, :]
            lf = jnp.minimum(z, 0.0) - jnp.log1p(jnp.exp(-jnp.abs(z)))
            cb = sum(_dot(tri, part) for part in _split3(lf)) + carry
            carry = cb[tb - 1:tb, :]
            cexp_ref[rows, :] = sum(_dot(part, expand) for part in _split3(cb))
            crow_ref[:, rows] = cb.T[0:HEADS, :]

    return pl.pallas_call(
        body, name="fgate_fwd", grid=(b,),
        in_specs=[pl.BlockSpec((None, s, LANES), lambda i: (i, 0, 0))],
        out_specs=[pl.BlockSpec((None, s, D_MODEL), lambda i: (i, 0, 0)),
                   pl.BlockSpec((None, HEADS, s), lambda i: (i, 0, 0))],
        out_shape=[jax.ShapeDtypeStruct((b, s, D_MODEL), F32), jax.ShapeDtypeStruct((b, HEADS, s), F32)],
        compiler_params=_cparams(("parallel",)),
    )(zf3)


def _fgate_bwd(dc3, zf3):
    b, s, _ = zf3.shape
    tb = SCAN_TILE
    nb = s // tb

    def body(dc_ref, z_ref, o_ref):
        tri = (_iota((tb, tb), 1) >= _iota((tb, tb), 0)).astype(BF16)
        carry = jnp.zeros((1, LANES), F32)
        for i in reversed(range(nb)):
            rows = slice(i * tb, (i + 1) * tb)
            dlf = sum(_dot(tri, part) for part in _split3(dc_ref[rows, :])) + carry
            carry = dlf[0:1, :]
            o_ref[rows, :] = (dlf * _sigmoid(-z_ref[rows, :])).astype(BF16)

    return pl.pallas_call(
        body, name="fgate_bwd", grid=(b,),
        in_specs=[pl.BlockSpec((None, s, LANES), lambda i: (i, 0, 0)),
                  pl.BlockSpec((None, s, LANES), lambda i: (i, 0, 0))],
        out_specs=pl.BlockSpec((None, s, LANES), lambda i: (i, 0, 0)),
        out_shape=jax.ShapeDtypeStruct((b, s, LANES), BF16),
        compiler_params=_cparams(("parallel",)),
    )(dc3, zf3)


def _spare(hh):
    return HEAD_DIM if hh == 0 else 0


def _put_cols(tile, mine, cols, first):
    lane = _iota((1, LANES), 1)
    out = jnp.where(mine, tile, jnp.zeros((), tile.dtype))
    for j, c in enumerate(cols):
        out = jnp.where(lane == first + j, c, out)
    return out


def _put_rows(tile, mine, rows, first):
    sub = _iota((LANES, 1), 0)
    out = jnp.where(mine, tile, jnp.zeros((), tile.dtype))
    for j, r in enumerate(rows):
        out = jnp.where(sub == first + j, r, out)
    return out


def _transpose_bf16(a):
    return a.astype(F32).T.astype(BF16)


def _attn_fwd(qkv3, cexp3, crow5, zrest3):
    b, s, _ = qkv3.shape
    ta = ATT_TILE
    nq = s // ta
    hd = HEAD_DIM

    def body(qkv_ref, cq_ref, ck_ref, g_ref, y_ref, lse_ref, ga_ref, kt_scr, v_scr):
        lane = _iota((1, LANES), 1)
        sub = _iota((LANES, 1), 0)
        lane_mine = (lane < hd, lane >= hd)
        sub_mine = (sub < hd, sub >= hd)
        causal = _iota((ta, ta), 0) >= _iota((ta, ta), 1)
        one = jnp.ones((), BF16)

        for kj in range(nq):
            rows = slice(kj * ta, (kj + 1) * ta)
            kt = _transpose_bf16(qkv_ref[rows, LANES:2 * LANES])
            v = qkv_ref[rows, 2 * LANES:3 * LANES]
            for hh in range(2):
                ck = list(_split3(-ck_ref[hh, kj:kj + 1, :]))
                kt_scr[hh, kj] = _put_rows(kt, sub_mine[hh], [one, one, one] + ck, _spare(hh))
                v_scr[hh, kj] = _put_cols(v, lane_mine[hh], [one], _spare(hh))

        for qi in range(nq):
            rows = slice(qi * ta, (qi + 1) * ta)
            q = qkv_ref[rows, 0:LANES] * 0.125
            cq = cq_ref[rows, :]
            qh = [_put_cols(q, lane_mine[hh], list(_split3(cq[:, hh * hd:hh * hd + 1])) + [one, one, one], _spare(hh))
                  for hh in range(2)]
            st = [(jnp.full((ta, 1), MASK_VALUE, F32), jnp.zeros((ta, LANES), F32))] * 2
            for kj in range(qi + 1):
                for hh in range(2):
                    m, acc = st[hh]
                    sc = _dot(qh[hh], kt_scr[hh, kj])
                    if kj == qi:
                        sc = jnp.where(causal, sc, MASK_VALUE)
                    mn = jnp.maximum(m, jnp.max(sc, axis=-1, keepdims=True))
                    p = jnp.exp(sc - mn).astype(BF16)
                    st[hh] = (mn, jnp.exp(m - mn) * acc + _dot(p, v_scr[hh, kj]))
            (ma, acca), (mb, accb) = st
            la = acca[:, hd:hd + 1]
            lb = accb[:, 0:1]
            y = jnp.where(lane_mine[0], acca * (1.0 / la), accb * (1.0 / lb))
            lse = jnp.where(lane_mine[0], ma + jnp.log(la), mb + jnp.log(lb)).T
            lse_ref[0, qi:qi + 1, :] = lse[0:1, :]
            lse_ref[1, qi:qi + 1, :] = lse[hd:hd + 1, :]
            y_ref[rows, :] = y
            g = g_ref[rows, :].astype(F32)
            ga_ref[rows, :] = (y * (g * _sigmoid(g))).astype(BF16)

    blk = lambda w: pl.BlockSpec((None, s, w), lambda i, p: (i, 0, p))
    rows5 = pl.BlockSpec((None, None, 2, nq, ta), lambda i, p: (i, p, 0, 0, 0))
    return pl.pallas_call(
        body, name="attn_fwd", grid=(b, HEAD_PAIRS),
        in_specs=[blk(3 * LANES), blk(LANES), rows5, blk(LANES)],
        out_specs=[blk(LANES), rows5, blk(LANES)],
        out_shape=[jax.ShapeDtypeStruct((b, s, D_MODEL), F32),
                   jax.ShapeDtypeStruct((b, HEAD_PAIRS, 2, nq, ta), F32),
                   jax.ShapeDtypeStruct((b, s, D_MODEL), BF16)],
        scratch_shapes=[pltpu.VMEM((2, nq, LANES, ta), BF16), pltpu.VMEM((2, nq, ta, LANES), BF16)],
        compiler_params=_cparams(("parallel", "parallel")),
    )(qkv3, cexp3, crow5, zrest3)


def _attn_bwd(qkv3, do3, y3, lse5, crow5, cexp3):
    b, s, _ = qkv3.shape
    ta = ATT_TILE
    nq = s // ta
    hd = HEAD_DIM

    def body(qkv_ref, do_ref, y_ref, lse_ref, crow_ref, cexp_ref, dqkv_ref, dc_ref,
             qa_scr, doa_scr, qst_scr, dot_scr, kt_scr, vt_scr, dq_scr, rs_scr):
        pair = pl.program_id(1)
        lane = _iota((1, LANES), 1)
        sub = _iota((LANES, 1), 0)
        lane_mine = (lane < hd, lane >= hd)
        sub_mine = (sub < hd, sub >= hd)
        causal = _iota((ta, ta), 0) >= _iota((ta, ta), 1)
        one = jnp.ones((), BF16)
        zero = jnp.zeros((), BF16)

        @pl.when(pair == 0)
        def _():
            dc_ref[...] = jnp.zeros_like(dc_ref)

        for i in range(nq):
            rows = slice(i * ta, (i + 1) * ta)
            qs = qkv_ref[rows, 0:LANES] * 0.125
            qst = _transpose_bf16(qs)
            kt = _transpose_bf16(qkv_ref[rows, LANES:2 * LANES])
            vt = _transpose_bf16(qkv_ref[rows, 2 * LANES:3 * LANES])
            do = do_ref[rows, :]
            dof = do.astype(F32)
            dot = dof.T.astype(BF16)
            pr = y_ref[rows, :] * dof
            cq = cexp_ref[rows, :]
            lse_c = jnp.where(sub == 0, lse_ref[0, i:i + 1, :],
                              jnp.where(sub == 1, lse_ref[1, i:i + 1, :], 0.0)).T
            for hh in range(2):
                sp = _spare(hh)
                dsum = jnp.sum(jnp.where(lane_mine[hh], pr, 0.0), axis=-1, keepdims=True)
                bias = cq[:, hh * hd:hh * hd + 1] - lse_c[:, hh:hh + 1]
                qa_scr[hh, i] = _put_cols(qs, lane_mine[hh], list(_split3(bias)) + [one, one, one], sp)
                doa_scr[hh, i] = _put_cols(do, lane_mine[hh], list(_split3(-dsum)), sp)
                qst_scr[hh, i] = jnp.where(sub_mine[hh], qst, zero)
                dot_scr[hh, i] = jnp.where(sub_mine[hh], dot, zero)
                ck = list(_split3(-crow_ref[hh, i:i + 1, :]))
                kt_scr[hh, i] = _put_rows(kt, sub_mine[hh], [one, one, one] + ck, sp)
                vt_scr[hh, i] = _put_rows(vt, sub_mine[hh], [one, one, one], sp)
            dq_scr[i] = jnp.zeros((ta, LANES), F32)
            rs_scr[i] = jnp.zeros((ta, LANES), F32)

        for kj in range(nq):
            krows = slice(kj * ta, (kj + 1) * ta)
            k = qkv_ref[krows, LANES:2 * LANES]
            km = (jnp.where(lane_mine[0], k, zero), jnp.where(lane_mine[1], k, zero))
            dkt = jnp.zeros((LANES, ta), F32)
            dvt = jnp.zeros((LANES, ta), F32)
            dcp = [jnp.zeros((8, ta), F32), jnp.zeros((8, ta), F32)]
            for qi in range(kj, nq):
                dq = jnp.zeros((ta, LANES), F32)
                rs = []
                for hh in range(2):
                    sc = _dot(qa_scr[hh, qi], kt_scr[hh, kj])
                    if qi == kj:
                        sc = jnp.where(causal, sc, MASK_VALUE)
                    p = jnp.exp(sc)
                    dsf = p * _dot(doa_scr[hh, qi], vt_scr[hh, kj])
                    dcp[hh] = dcp[hh] + jnp.sum(dsf.reshape(ta // 8, 8, ta), axis=0)
                    rs.append(jnp.sum(dsf, axis=-1, keepdims=True))
                    ds = dsf.astype(BF16)
                    dq = dq + _dot(ds, km[hh])
                    dkt = dkt + _dot(qst_scr[hh, qi], ds)
                    dvt = dvt + _dot(dot_scr[hh, qi], p.astype(BF16))
                dq_scr[qi] += dq
                rs_scr[qi] += jnp.where(lane == 0, rs[0], jnp.where(lane == 1, rs[1], 0.0))
            dqkv_ref[krows, LANES:2 * LANES] = dkt.T.astype(BF16)
            dqkv_ref[krows, 2 * LANES:3 * LANES] = dvt.T.astype(BF16)
            dca = jnp.sum(dcp[0], axis=0, keepdims=True)
            dcb = jnp.sum(dcp[1], axis=0, keepdims=True)
            dcs = jnp.where(sub == 0, dca, jnp.where(sub == 1, dcb, 0.0)).T
            dc_ref[krows, :] += (jnp.where(lane == 2 * pair, -dcs[:, 0:1], 0.0)
                                 + jnp.where(lane == 2 * pair + 1, -dcs[:, 1:2], 0.0))
        for qi in range(nq):
            rows = slice(qi * ta, (qi + 1) * ta)
            dqkv_ref[rows, 0:LANES] = (dq_scr[qi] * 0.125).astype(BF16)
            rq = rs_scr[qi]
            dc_ref[rows, :] += (jnp.where(lane == 2 * pair, rq[:, 0:1], 0.0)
                                + jnp.where(lane == 2 * pair + 1, rq[:, 1:2], 0.0))

    blk = lambda w: pl.BlockSpec((None, s, w), lambda i, p: (i, 0, p))
    rows5 = pl.BlockSpec((None, None, 2, nq, ta), lambda i, p: (i, p, 0, 0, 0))
    by_rows = lambda: pltpu.VMEM((2, nq, ta, LANES), BF16)
    by_cols = lambda: pltpu.VMEM((2, nq, LANES, ta), BF16)
    return pl.pallas_call(
        body, name="attn_bwd", grid=(b, HEAD_PAIRS),
        in_specs=[blk(3 * LANES), blk(LANES), blk(LANES), rows5, rows5, blk(LANES)],
        out_specs=[blk(3 * LANES), pl.BlockSpec((None, s, LANES), lambda i, p: (i, 0, 0))],
        out_shape=[jax.ShapeDtypeStruct((b, s, 3 * D_MODEL), BF16), jax.ShapeDtypeStruct((b, s, LANES), F32)],
        scratch_shapes=[by_rows(), by_rows(), by_cols(), by_cols(), by_cols(), by_cols(),
                        pltpu.VMEM((nq, ta, LANES), F32), pltpu.VMEM((nq, ta, LANES), F32)],
        compiler_params=_cparams(("parallel", "arbitrary")),
    )(qkv3, do3, y3, lse5, crow5, cexp3)


def _shifted(v, ks, rows, s):
    return [jnp.where(rows >= k, pltpu.roll(v, k, 0), 0.0) if k > 0
            else jnp.where(rows < s + k, pltpu.roll(v, s + k, 0), 0.0) for k in ks]


def _rnn_common(xr, cw_ref, cb_ref, bda_ref, bdx_ref, ba_ref, bx_ref, lam_ref, s):
    rows = _iota((s, LANES), 0)
    x1, x2, x3 = _shifted(xr, (1, 2, 3), rows, s)
    xc = cb_ref[...] + cw_ref[0:1, :] * x3
    xc = xc + cw_ref[1:2, :] * x2
    xc = xc + cw_ref[2:3, :] * x1
    xc = xc + cw_ref[3:4, :] * xr
    xcb = xc.astype(BF16)
    r = _sigmoid(_dot(xcb, bda_ref[...]) + ba_ref[...])
    i = _sigmoid(_dot(xcb, bdx_ref[...]) + bx_ref[...])
    sp = _softplus(-lam_ref[...])
    log_a = (-RG_C * r) * sp
    a = jnp.exp(log_a)
    a2 = a * a
    sq = jnp.sqrt(jnp.maximum(_one_minus_exp(2.0 * log_a, a2), 0.0))
    return rows, (x1, x2, x3), xc, xcb, r, i, sp, a, a2, sq


def _scan_down(a, u, rows, s, s1, s2):
    low = rows & 7
    for sh in (1, 2, 4):
        keep = low >= sh
        u = u + a * jnp.where(keep, pltpu.roll(u, sh, 0), 0.0)
        a = a * jnp.where(keep, pltpu.roll(a, sh, 0), 1.0)
    ng = s // 8
    s1[...] = a
    s2[...] = u
    at = s1[pl.ds(7, ng, stride=8), :]
    ut = s2[pl.ds(7, ng, stride=8), :]
    grow = _iota((ng, LANES), 0)
    sh = 1
    while sh < ng:
        keep = grow >= sh
        ut = ut + at * jnp.where(keep, pltpu.roll(ut, sh, 0), 0.0)
        if sh * 2 < ng:
            at = at * jnp.where(keep, pltpu.roll(at, sh, 0), 1.0)
        sh *= 2
    h_in = jnp.where(grow >= 1, pltpu.roll(ut, 1, 0), 0.0)
    for k in range(8):
        s1[pl.ds(k, ng, stride=8), :] = h_in
    return u + a * s1[...]


def _scan_up(a, g, rows, s, s1, s2):
    low = rows & 7
    for sh in (1, 2, 4):
        keep = low < 8 - sh
        g = g + a * jnp.where(keep, pltpu.roll(g, s - sh, 0), 0.0)
        a = a * jnp.where(keep, pltpu.roll(a, s - sh, 0), 1.0)
    ng = s // 8
    s1[...] = a
    s2[...] = g
    at = s1[pl.ds(0, ng, stride=8), :]
    gt = s2[pl.ds(0, ng, stride=8), :]
    grow = _iota((ng, LANES), 0)
    sh = 1
    while sh < ng:
        keep = grow < ng - sh
        gt = gt + at * jnp.where(keep, pltpu.roll(gt, ng - sh, 0), 0.0)
        if sh * 2 < ng:
            at = at * jnp.where(keep, pltpu.roll(at, ng - sh, 0), 1.0)
        sh *= 2
    g_in = jnp.where(grow < ng - 1, pltpu.roll(gt, ng - 1, 0), 0.0)
    for k in range(8):
        s1[pl.ds(k, ng, stride=8), :] = g_in
    return g + a * s1[...]


def _rnn_specs(s):
    blk = lambda off: pl.BlockSpec((None, s, LANES), lambda cb, i: (i, 0, off + cb))
    vec = lambda r: pl.BlockSpec((r, LANES), lambda cb, i: (0, cb))
    mat = pl.BlockSpec((None, LANES, LANES), lambda cb, i: (cb, 0, 0))
    return blk, vec, mat


def _rnn_fwd(zrest3, conv_w, conv_b, bda, bdx, ba, bx, lam):
    b, s, _ = zrest3.shape

    def body(xr_ref, g_ref, cw_ref, cb_ref, bda_ref, bdx_ref, ba_ref, bx_ref, lam_ref, h_ref, gr_ref, s1, s2):
        xr = xr_ref[...].astype(F32)
        rows, _, xc, _, _, i, _, a, _, sq = _rnn_common(
            xr, cw_ref, cb_ref, bda_ref, bdx_ref, ba_ref, bx_ref, lam_ref, s)
        h = _scan_down(a, sq * (i * xc), rows, s, s1, s2)
        h_ref[...] = h
        g = g_ref[...].astype(F32)
        gr_ref[...] = (h * (g * _sigmoid(g))).astype(BF16)

    blk, vec, mat = _rnn_specs(s)
    return pl.pallas_call(
        body, name="rnn_fwd", grid=(N_CBLK, b),
        in_specs=[blk(N_CBLK), blk(2 * N_CBLK), vec(CONV_W), vec(1), mat, mat, vec(1), vec(1), vec(1)],
        out_specs=[blk(0), blk(0)],
        out_shape=[jax.ShapeDtypeStruct((b, s, D_MODEL), F32), jax.ShapeDtypeStruct((b, s, D_MODEL), BF16)],
        scratch_shapes=[pltpu.VMEM((s, LANES), F32), pltpu.VMEM((s, LANES), F32)],
        compiler_params=_cparams(("parallel", "parallel")),
    )(zrest3, zrest3, conv_w, conv_b, bda, bdx, ba, bx, lam)


def _rnn_bwd(zrest3, h3, dh3, conv_w, conv_b, bda, bdx, ba, bx, lam):
    b, s, _ = zrest3.shape

    def body(xr_ref, h_ref, dh_ref, cw_ref, cb_ref, bda_ref, bdx_ref, ba_ref, bx_ref, lam_ref,
             dxr_ref, pv_ref, dbd_ref, s1, s2):
        @pl.when(pl.program_id(1) == 0)
        def _():
            pv_ref[...] = jnp.zeros_like(pv_ref)
            dbd_ref[...] = jnp.zeros_like(dbd_ref)

        xr = xr_ref[...].astype(F32)
        rows, (x1, x2, x3), xc, xcb, r, i, sp, a, a2, sq = _rnn_common(
            xr, cw_ref, cb_ref, bda_ref, bdx_ref, ba_ref, bx_ref, lam_ref, s)
        (a_next,) = _shifted(a, (-1,), rows, s)
        g = _scan_up(a_next, dh_ref[...], rows, s, s1, s2)
        (hp,) = _shifted(h_ref[...], (1,), rows, s)
        da = g * hp
        dsq = g * (i * xc)
        di = g * (sq * xc)
        dxc = g * (sq * i)
        dlog = da * a - dsq * (a2 / sq)
        dr = dlog * (-RG_C * sp)
        dpr = dr * (r * (1.0 - r))
        dpi = di * (i * (1.0 - i))
        dprb = dpr.astype(BF16)
        dpib = dpi.astype(BF16)
        dxc = dxc + _dot_nt(dprb, bda_ref[...]) + _dot_nt(dpib, bdx_ref[...])

        up1, up2, up3 = _shifted(dxc, (-1, -2, -3), rows, s)
        dxr = cw_ref[3:4, :] * dxc + cw_ref[2:3, :] * up1 + cw_ref[1:2, :] * up2 + cw_ref[0:1, :] * up3
        dxr_ref[...] = dxr.astype(BF16)

        def colsum(v):
            return jnp.sum(v, axis=0, keepdims=True)

        pv_ref[0:1, :] += colsum(dxc * x3)
        pv_ref[1:2, :] += colsum(dxc * x2)
        pv_ref[2:3, :] += colsum(dxc * x1)
        pv_ref[3:4, :] += colsum(dxc * xr)
        pv_ref[4:5, :] += colsum(dxc)
        pv_ref[5:6, :] += colsum(dpr)
        pv_ref[6:7, :] += colsum(dpi)
        pv_ref[7:8, :] += colsum(dlog * r) * (RG_C * _sigmoid(-lam_ref[...]))
        dbd_ref[0] += _dot_tn(xcb, dprb)
        dbd_ref[1] += _dot_tn(xcb, dpib)

    blk, vec, mat = _rnn_specs(s)
    hblk = pl.BlockSpec((None, s, LANES), lambda cb, i: (i, 0, cb))
    return pl.pallas_call(
        body, name="rnn_bwd", grid=(N_CBLK, b),
        in_specs=[blk(N_CBLK), hblk, hblk, vec(CONV_W), vec(1), mat, mat, vec(1), vec(1), vec(1)],
        out_specs=[hblk, pl.BlockSpec((8, LANES), lambda cb, i: (0, cb)),
                   pl.BlockSpec((None, 2, LANES, LANES), lambda cb, i: (cb, 0, 0, 0))],
        out_shape=[jax.ShapeDtypeStruct((b, s, D_MODEL), BF16), jax.ShapeDtypeStruct((8, D_MODEL), F32),
                   jax.ShapeDtypeStruct((N_CBLK, 2, LANES, LANES), F32)],
        scratch_shapes=[pltpu.VMEM((s, LANES), F32), pltpu.VMEM((s, LANES), F32)],
        compiler_params=_cparams(("parallel", "arbitrary")),
    )(zrest3, h3, dh3, conv_w, conv_b, bda, bdx, ba, bx, lam)


def _branch_merge(ga, gr, wa, wr, zrest):
    t = ga.shape[0]
    tm = min(512, t)
    tn = 512

    def body(ga_ref, gr_ref, wa_ref, wr_ref, mga_ref, mgr_ref, ya_ref, yr_ref, m_ref):
        ya = _dot(ga_ref[...], wa_ref[...])
        yr = _dot(gr_ref[...], wr_ref[...])
        ya_ref[...] = ya.astype(BF16)
        yr_ref[...] = yr.astype(BF16)
        m_ref[...] = (_sigmoid(mga_ref[...].astype(F32)) * ya + _sigmoid(mgr_ref[...].astype(F32)) * yr).astype(BF16)

    nj = D_MODEL // tn
    act = pl.BlockSpec((tm, D_MODEL), lambda i, j: (i, 0))
    wgt = pl.BlockSpec((D_MODEL, tn), lambda i, j: (0, j))
    out = pl.BlockSpec((tm, tn), lambda i, j: (i, j))
    return pl.pallas_call(
        body, name="branch_merge", grid=(t // tm, nj),
        in_specs=[act, act, wgt, wgt, pl.BlockSpec((tm, tn), lambda i, j: (i, 3 * nj + j)),
                  pl.BlockSpec((tm, tn), lambda i, j: (i, 4 * nj + j))],
        out_specs=[out, out, out],
        out_shape=[jax.ShapeDtypeStruct((t, D_MODEL), BF16), jax.ShapeDtypeStruct((t, D_MODEL), BF16),
                   jax.ShapeDtypeStruct((t, D_MODEL), BF16)],
        compiler_params=_cparams(("parallel", "parallel")),
    )(ga, gr, wa, wr, zrest, zrest)


def _out_loss(m, wout, x2, tgt2, wpost):
    t = m.shape[0]
    tm = min(256, t)

    def body(m_ref, w_ref, x_ref, t_ref, wp_ref, dy_ref, do_ref, acc_ref):
        @pl.when(pl.program_id(0) == 0)
        def _():
            acc_ref[...] = jnp.zeros_like(acc_ref)

        o = _dot(m_ref[...], w_ref[...])
        r2 = lax.rsqrt(jnp.mean(o * o, axis=-1, keepdims=True) + NORM_EPS)
        n = o * r2
        wp = wp_ref[...]
        err = (x_ref[...] + n * wp) - t_ref[...]
        dy = err * (1.0 / D_MODEL)
        dn = dy * wp
        do = r2 * (dn - n * jnp.mean(dn * n, axis=-1, keepdims=True))
        dy_ref[...] = dy
        do_ref[...] = do.astype(BF16)
        acc_ref[0:1, :] += jnp.sum(dy * n, axis=0, keepdims=True)
        acc_ref[1:2, :] += jnp.sum(err * err, axis=0, keepdims=True)

    row = pl.BlockSpec((tm, D_MODEL), lambda i: (i, 0))
    return pl.pallas_call(
        body, name="out_loss", grid=(t // tm,),
        in_specs=[row, pl.BlockSpec((D_MODEL, D_MODEL), lambda i: (0, 0)), row, row,
                  pl.BlockSpec((1, D_MODEL), lambda i: (0, 0))],
        out_specs=[row, row, pl.BlockSpec((8, D_MODEL), lambda i: (0, 0))],
        out_shape=[jax.ShapeDtypeStruct((t, D_MODEL), F32), jax.ShapeDtypeStruct((t, D_MODEL), BF16),
                   jax.ShapeDtypeStruct((8, D_MODEL), F32)],
        compiler_params=_cparams(("arbitrary",)),
    )(m, wout, x2, tgt2, wpost)


def _merge_bwd(do, wout, zrest, ya, yr):
    t = do.shape[0]
    tm = min(512, t)
    tn = 512
    nj = D_MODEL // tn

    def body(do_ref, w_ref, mga_ref, mgr_ref, ya_ref, yr_ref, dya_ref, dyr_ref, dmga_ref, dmgr_ref):
        dm = _dot_nt(do_ref[...], w_ref[...])
        sa = _sigmoid(mga_ref[...].astype(F32))
        sr = _sigmoid(mgr_ref[...].astype(F32))
        dya_ref[...] = (dm * sa).astype(BF16)
        dyr_ref[...] = (dm * sr).astype(BF16)
        dmga_ref[...] = (dm * ya_ref[...].astype(F32) * (sa * (1.0 - sa))).astype(BF16)
        dmgr_ref[...] = (dm * yr_ref[...].astype(F32) * (sr * (1.0 - sr))).astype(BF16)

    out = pl.BlockSpec((tm, tn), lambda i, j: (i, j))
    bf = jax.ShapeDtypeStruct((t, D_MODEL), BF16)
    return pl.pallas_call(
        body, name="merge_bwd", grid=(t // tm, nj),
        in_specs=[pl.BlockSpec((tm, D_MODEL), lambda i, j: (i, 0)), pl.BlockSpec((tn, D_MODEL), lambda i, j: (j, 0)),
                  pl.BlockSpec((tm, tn), lambda i, j: (i, 3 * nj + j)),
                  pl.BlockSpec((tm, tn), lambda i, j: (i, 4 * nj + j)), out, out],
        out_specs=[out, out, out, out],
        out_shape=[bf, bf, bf, bf],
        compiler_params=_cparams(("parallel", "parallel")),
    )(do, wout, zrest, zrest, ya, yr)


def _branch_bwd(dya, dyr, wa, wr, zrest, yatt, ylru):
    t = dya.shape[0]
    tm = min(512, t)
    tn = 512
    nj = D_MODEL // tn

    def body(dya_ref, dyr_ref, wa_ref, wr_ref, ga_ref, gr_ref, ya_ref, yl_ref,
             dyatt_ref, dga_ref, dyl_ref, dgr_ref):
        dga = _dot_nt(dya_ref[...], wa_ref[...])
        dgr = _dot_nt(dyr_ref[...], wr_ref[...])
        g = ga_ref[...].astype(F32)
        sg = _sigmoid(g)
        dyatt_ref[...] = (dga * (g * sg)).astype(BF16)
        dga_ref[...] = (dga * ya_ref[...] * (sg * (1.0 + g * (1.0 - sg)))).astype(BF16)
        g = gr_ref[...].astype(F32)
        sg = _sigmoid(g)
        dyl_ref[...] = dgr * (g * sg)
        dgr_ref[...] = (dgr * yl_ref[...] * (sg * (1.0 + g * (1.0 - sg)))).astype(BF16)

    act = pl.BlockSpec((tm, D_MODEL), lambda i, j: (i, 0))
    wgt = pl.BlockSpec((tn, D_MODEL), lambda i, j: (j, 0))
    out = pl.BlockSpec((tm, tn), lambda i, j: (i, j))
    bf = jax.ShapeDtypeStruct((t, D_MODEL), BF16)
    return pl.pallas_call(
        body, name="branch_bwd", grid=(t // tm, nj),
        in_specs=[act, act, wgt, wgt, pl.BlockSpec((tm, tn), lambda i, j: (i, j)),
                  pl.BlockSpec((tm, tn), lambda i, j: (i, 2 * nj + j)), out, out],
        out_specs=[out, out, out, out],
        out_shape=[bf, bf, jax.ShapeDtypeStruct((t, D_MODEL), F32), bf],
        compiler_params=_cparams(("parallel", "parallel")),
    )(dya, dyr, wa, wr, zrest, zrest, yatt, ylru)


def _dh_partial(parts, after, name):
    t = parts[0][0].shape[0]
    tm = min(256, t)
    np_ = len(parts)

    def body(*refs):
        o_ref = refs[-1]
        acc = _dot(refs[0][...], refs[np_][...])
        for p in range(1, np_):
            acc = acc + _dot(refs[p][...], refs[np_ + p][...])
        o_ref[...] = acc

    in_specs = [pl.BlockSpec((tm, dz.shape[1]), lambda i: (i, 0)) for dz, _ in parts]
    in_specs += [pl.BlockSpec(w.shape, lambda i: (0, 0)) for _, w in parts]
    in_specs += [pl.BlockSpec(after.shape, lambda i: (0, 0))]
    return pl.pallas_call(
        body, name=name, grid=(t // tm,),
        in_specs=in_specs,
        out_specs=pl.BlockSpec((tm, D_MODEL), lambda i: (i, 0)),
        out_shape=jax.ShapeDtypeStruct((t, D_MODEL), F32),
        compiler_params=_cparams(("parallel",), vmem_mb=48),
    )(*[dz for dz, _ in parts], *[w for _, w in parts], after)


def _dh_final(parts, acc_in, x2, dy, wpre):
    t = x2.shape[0]
    tm = min(256, t)
    np_ = len(parts)

    def body(*refs):
        acc_ref, x_ref, dy_ref, w_ref = refs[2 * np_:2 * np_ + 4]
        gx_ref, pw_ref = refs[2 * np_ + 4:]

        @pl.when(pl.program_id(0) == 0)
        def _():
            pw_ref[...] = jnp.zeros_like(pw_ref)

        dh = acc_ref[...]
        for p in range(np_):
            dh = dh + _dot(refs[p][...], refs[np_ + p][...])
        x = x_ref[...]
        r = lax.rsqrt(jnp.mean(x * x, axis=-1, keepdims=True) + NORM_EPS)
        xn = x * r
        dxn = dh * w_ref[...]
        gx_ref[...] = r * (dxn - xn * jnp.mean(dxn * xn, axis=-1, keepdims=True)) + dy_ref[...]
        pw_ref[0:1, :] += jnp.sum(dh * xn, axis=0, keepdims=True)

    row = pl.BlockSpec((tm, D_MODEL), lambda i: (i, 0))
    in_specs = [pl.BlockSpec((tm, dz.shape[1]), lambda i: (i, 0)) for dz, _ in parts]
    in_specs += [pl.BlockSpec(w.shape, lambda i: (0, 0)) for _, w in parts]
    in_specs += [row, row, row, pl.BlockSpec((1, D_MODEL), lambda i: (0, 0))]
    return pl.pallas_call(
        body, name="dh_final", grid=(t // tm,),
        in_specs=in_specs,
        out_specs=[row, pl.BlockSpec((8, D_MODEL), lambda i: (0, 0))],
        out_shape=[jax.ShapeDtypeStruct((t, D_MODEL), F32), jax.ShapeDtypeStruct((8, D_MODEL), F32)],
        compiler_params=_cparams(("arbitrary",), vmem_mb=48),
    )(*[dz for dz, _ in parts], *[w for _, w in parts], acc_in, x2, dy, wpre)


def _adamw(w, g, m, v):
    m = ADAM_B1 * m + (1.0 - ADAM_B1) * g
    v = ADAM_B2 * v + (1.0 - ADAM_B2) * (g * g)
    m_hat = m / (1.0 - ADAM_B1 ** ADAM_STEP)
    v_hat = v / (1.0 - ADAM_B2 ** ADAM_STEP)
    delta = -ADAM_LR * (m_hat / (jnp.sqrt(v_hat) + ADAM_EPS) + ADAM_WD * w)
    return delta, m, v


def _reduce_adamw(own, parts, place, w, m, v, name):
    r, c = w.shape
    blk, nblk, at = _blocks_2d(r, c)

    def body(place_ref, own_ref, p_ref, w_ref, m_ref, v_ref, g_ref, d_ref, nm_ref, nv_ref):
        mine = place_ref[1]
        own_blk = own_ref[...]
        g = jnp.where(mine == 0, own_blk, p_ref[0].astype(F32))
        for j in range(1, N_CHIPS):
            g = g + jnp.where(mine == j, own_blk, p_ref[j].astype(F32))
        d, nm, nv = _adamw(w_ref[...], g, m_ref[...], v_ref[...])
        g_ref[...] = g
        d_ref[...] = d
        nm_ref[...] = nm
        nv_ref[...] = nv

    row = pl.BlockSpec(blk, lambda i, pr: at(i))
    sh = jax.ShapeDtypeStruct((r, c), F32)
    grid_spec = pltpu.PrefetchScalarGridSpec(
        num_scalar_prefetch=1, grid=(nblk,),
        in_specs=[row, pl.BlockSpec((N_CHIPS,) + blk, lambda i, pr: (0,) + at(i)), row, row, row],
        out_specs=[row, row, row, row])
    return pl.pallas_call(
        body, name=name, grid_spec=grid_spec, out_shape=[sh, sh, sh, sh],
        compiler_params=_cparams(("parallel",)),
    )(place, own, parts, w, m, v)


def _interleave_qkv(a):
    lead = a.shape[:-1]
    return a.reshape(lead + (3, HEAD_PAIRS, LANES)).swapaxes(-3, -2).reshape(lead + (3 * D_MODEL,))


def _deinterleave_qkv(a):
    lead = a.shape[:-1]
    return a.reshape(lead + (HEAD_PAIRS, 3, LANES)).swapaxes(-3, -2).reshape(lead + (3 * D_MODEL,))


def _interleave_rows(a):
    return a.reshape(3, HEAD_PAIRS, LANES, a.shape[1]).swapaxes(0, 1).reshape(a.shape)


def _deinterleave_rows(a):
    return a.reshape(HEAD_PAIRS, 3, LANES, a.shape[1]).swapaxes(0, 1).reshape(a.shape)


def _pack_small(pre, conv_b, rg_ba, rg_bx, lam, post, loss_row, b_in, conv_w_full, rg_wa, rg_wx):
    z = jnp.zeros((1, D_MODEL), F32)
    b_used = jnp.concatenate([b_in[:, 0:3 * D_MODEL], b_in[:, 3 * D_MODEL + HEADS:IN_TOTAL]], axis=1)
    b_f = jnp.pad(b_in[:, 3 * D_MODEL:3 * D_MODEL + HEADS], ((0, 0), (0, D_MODEL - HEADS)))
    return jnp.concatenate([
        pre, conv_b, rg_ba, rg_bx, lam, post, loss_row, z,
        b_used.reshape(9, D_MODEL), b_f, conv_w_full, z, z,
        rg_wa.reshape(64, D_MODEL), rg_wx.reshape(64, D_MODEL)], axis=0)


def _unpack_small(p):
    b_used = p[8:17].reshape(1, 9 * D_MODEL)
    b_in = jnp.concatenate([b_used[:, 0:3 * D_MODEL], p[17:18, 0:HEADS], b_used[:, 3 * D_MODEL:]], axis=1)
    return dict(pre_norm_w=p[0:1], conv_b=p[1:2], rg_ba=p[2:3], rg_bx=p[3:4], rg_lambda=p[4:5],
                post_norm_w=p[5:6], loss_row=p[6:7], b_in=b_in, conv_w_full=p[18:22],
                rg_wa=p[24:88].reshape(1, 16, 64, 64), rg_wx=p[88:152].reshape(1, 16, 64, 64))


def _reduce_small(parts, w, m, v):
    def body(p_ref, w_ref, m_ref, v_ref, g_ref, d_ref, nm_ref, nv_ref):
        g = p_ref[0]
        for j in range(1, N_DEV):
            g = g + p_ref[j]
        d, nm, nv = _adamw(w_ref[...], g, m_ref[...], v_ref[...])
        g_ref[...] = g
        d_ref[...] = d
        nm_ref[...] = nm
        nv_ref[...] = nv

    sh = jax.ShapeDtypeStruct((SMALL_ROWS, D_MODEL), F32)
    return pl.pallas_call(body, name="reduce_small", out_shape=[sh, sh, sh, sh])(parts, w, m, v)


def kernel(x, pre_norm_w, w_in, b_in, conv_w, conv_b, rg_wa, rg_ba, rg_wx, rg_bx, rg_lambda, w_branch_a, w_branch_r, w_out, post_norm_w, loss_target, m_pre_norm_w, m_w_in, m_b_in, m_conv_w, m_conv_b, m_rg_wa, m_rg_ba, m_rg_wx, m_rg_bx, m_rg_lambda, m_w_branch_a, m_w_branch_r, m_w_out, m_post_norm_w, v_pre_norm_w, v_w_in, v_b_in, v_conv_w, v_conv_b, v_rg_wa, v_rg_ba, v_rg_wx, v_rg_bx, v_rg_lambda, v_w_branch_a, v_w_branch_r, v_w_out, v_post_norm_w):
    b, s, _ = x.shape
    t = b * s
    me = 4 * lax.axis_index("x") + 2 * lax.axis_index("y") + lax.axis_index("c")
    shard_rows = D_MODEL // N_DEV

    place = jnp.stack([lax.axis_index("c"), 2 * lax.axis_index("x") + lax.axis_index("y")]).astype(jnp.int32)
    w_in_all = _gather(w_in[0].T.astype(BF16), "gather_w_in")
    wt_full = w_in_all.reshape(IN_TOTAL, D_MODEL)
    conv_terms = jnp.concatenate(_split3(conv_w[0]), axis=0)
    conv_pad = jnp.pad(conv_terms, ((0, 16 - 3 * CONV_W), (0, D_MODEL - LANES)))
    sq_stack = jnp.concatenate([w_branch_a[0].astype(BF16), w_branch_r[0].astype(BF16), w_out[0].astype(BF16),
                                conv_pad], axis=0)
    sq_sems, sq_src, sq_land, sq_token = _gather_start(sq_stack, w_in_all, "gather_w_sq_start")

    w_qkv = _interleave_rows(wt_full[0:3 * D_MODEL])
    w_f = jnp.pad(wt_full[3 * D_MODEL:3 * D_MODEL + HEADS], ((0, LANES - HEADS), (0, 0)))
    w_rest = wt_full[3 * D_MODEL + HEADS:IN_USED]
    b_qkv = _interleave_qkv(b_in[:, 0:3 * D_MODEL]) + sq_token[0, 0]
    b_f = jnp.pad(b_in[:, 3 * D_MODEL:3 * D_MODEL + HEADS], ((0, 0), (0, LANES - HEADS)))
    b_rest = b_in[:, 3 * D_MODEL + HEADS:IN_USED]

    def blockdiag(w):
        w2 = w.reshape(N_CBLK, 2, HEAD_DIM, HEAD_DIM)
        zz = jnp.zeros((N_CBLK, HEAD_DIM, HEAD_DIM), w.dtype)
        top = jnp.concatenate([w2[:, 0], zz], axis=2)
        bot = jnp.concatenate([zz, w2[:, 1]], axis=2)
        return jnp.concatenate([top, bot], axis=1).astype(BF16)

    bda, bdx = blockdiag(rg_wa[0]), blockdiag(rg_wx[0])

    x2 = x.reshape(t, D_MODEL)
    tgt2 = loss_target.reshape(t, D_MODEL)
    h = _prenorm(x2, pre_norm_w)
    qkv = _mm_bias(h, w_qkv, b_qkv, BF16, "inproj_qkv")
    zrest = _mm_bias(h, w_rest, b_rest, BF16, "inproj_rest")
    zf = _mm_bias(h, w_f, b_f, F32, "inproj_f")
    qkv3 = qkv.reshape(b, s, 3 * D_MODEL)
    zrest3 = zrest.reshape(b, s, 5 * D_MODEL)
    zf3 = zf.reshape(b, s, LANES)
    nq = s // ATT_TILE
    cexp3, crow = _fgate_fwd(zf3)
    crow5 = crow.reshape(b, HEAD_PAIRS, 2, nq, ATT_TILE)
    yatt3, lse5, ga3 = _attn_fwd(qkv3, cexp3, crow5, zrest3)

    sq_all = _gather_wait(sq_sems, sq_src, sq_land, ga3, "gather_w_sq_wait")
    sq_all = lax.dynamic_update_slice(sq_all, sq_stack[None], (me, 0, 0))
    wa = sq_all[:, 0:shard_rows].reshape(D_MODEL, D_MODEL)
    wr = sq_all[:, shard_rows:2 * shard_rows].reshape(D_MODEL, D_MODEL)
    wo = sq_all[:, 2 * shard_rows:3 * shard_rows].reshape(D_MODEL, D_MODEL)
    conv_all = sq_all[:, 3 * shard_rows:3 * shard_rows + 3 * CONV_W, 0:LANES].astype(F32)
    conv_all = (conv_all[:, 0:CONV_W] + conv_all[:, CONV_W:2 * CONV_W]) + conv_all[:, 2 * CONV_W:3 * CONV_W]
    conv_full = conv_all.transpose(1, 0, 2).reshape(CONV_W, D_MODEL)

    ylru3, gr3 = _rnn_fwd(zrest3, conv_full, conv_b, bda, bdx, rg_ba, rg_bx, rg_lambda)
    ga, gr = ga3.reshape(t, D_MODEL), gr3.reshape(t, D_MODEL)
    ya, yr, mm = _branch_merge(ga, gr, wa, wr, zrest)
    dy, do, acc_out = _out_loss(mm, wo, x2, tgt2, post_norm_w)

    dya, dyr, dz_mga, dz_mgr = _merge_bwd(do, wo, zrest, ya, yr)
    dyatt, dz_ga, dylru, dz_gr = _branch_bwd(dya, dyr, wa, wr, zrest, yatt3.reshape(t, D_MODEL),
                                             ylru3.reshape(t, D_MODEL))
    dz_xr3, pvec, dbd = _rnn_bwd(zrest3, ylru3, dylru.reshape(b, s, D_MODEL), conv_full, conv_b, bda, bdx,
                                 rg_ba, rg_bx, rg_lambda)
    dqkv3, dc3 = _attn_bwd(qkv3, dyatt.reshape(b, s, D_MODEL), yatt3, lse5, crow5, cexp3)
    dz_f = _fgate_bwd(dc3, zf3).reshape(t, LANES)
    dz_qkv = dqkv3.reshape(t, 3 * D_MODEL)
    dz_xr = dz_xr3.reshape(t, D_MODEL)

    dw_qkv, db_qkv = _mm_tn(dz_qkv, h, "dw_qkv")
    dw_f, db_f = _mm_tn(dz_f, h, "dw_f")
    dw_parts, db_parts = [], []
    for nm, dzp in (("ga", dz_ga), ("xr", dz_xr), ("gr", dz_gr), ("mga", dz_mga), ("mgr", dz_mgr)):
        dwp, dbp = _mm_tn(dzp, h, "dw_" + nm)
        dw_parts.append(dwp)
        db_parts.append(dbp[0:1])
    dw_a, _ = _mm_tn(ga, dya, "dw_a")
    dw_r, _ = _mm_tn(gr, dyr, "dw_r")
    dw_o, _ = _mm_tn(mm, do, "dw_o")

    zeros_tail = jnp.zeros((IN_TOTAL - IN_USED, D_MODEL), F32)
    dwt_full = jnp.concatenate([_deinterleave_rows(dw_qkv), dw_f[0:HEADS]] + dw_parts + [zeros_tail], axis=0)
    dw_in_send = dwt_full.reshape(N_CHIPS, 2, W_SHARD, D_MODEL).transpose(1, 0, 2, 3)
    by_dest = lambda a: a.reshape(N_CHIPS, 2, shard_rows, D_MODEL).transpose(1, 0, 2, 3)
    dw_sq_send = jnp.concatenate([by_dest(dw_a), by_dest(dw_r), by_dest(dw_o)], axis=2)

    sib_in, sib_sq = _swap_with_sibling([dw_in_send, dw_sq_send], "swap_dw")
    chip_in, own_in = _pair_add(dw_in_send, sib_in, place, "pair_add_in")
    chip_sq, own_sq = _pair_add(dw_sq_send, sib_sq, place, "pair_add_sq")
    sems, sent, lands, token = _exchange_chips_start([chip_in, chip_sq], "exchange_dw_start")

    wt = lambda lo: w_rest[lo * D_MODEL:(lo + 1) * D_MODEL]
    dh_a = _dh_partial([(dz_qkv, w_qkv), (dz_f, w_f)], token, "dh_qkv")
    grad_x2, acc_pre = _dh_final(
        [(dz_ga, wt(0)), (dz_xr, wt(1)), (dz_gr, wt(2)), (dz_mga, wt(3)), (dz_mgr, wt(4))],
        dh_a, x2, dy, pre_norm_w)

    db_in_full = jnp.concatenate([_deinterleave_qkv(db_qkv[0:1]), db_f[0:1, 0:HEADS]] + db_parts
                                 + [jnp.zeros((1, IN_TOTAL - IN_USED), F32)], axis=1)
    d_rg_wa = jnp.stack([dbd[:, 0, 0:HEAD_DIM, 0:HEAD_DIM], dbd[:, 0, HEAD_DIM:, HEAD_DIM:]], axis=1)
    d_rg_wx = jnp.stack([dbd[:, 1, 0:HEAD_DIM, 0:HEAD_DIM], dbd[:, 1, HEAD_DIM:, HEAD_DIM:]], axis=1)
    small_g = _pack_small(acc_pre[0:1], pvec[4:5], pvec[5:6], pvec[6:7], pvec[7:8], acc_out[0:1], acc_out[1:2],
                          db_in_full, pvec[0:4], d_rg_wa, d_rg_wx)
    sm_sems, sm_src, sm_land, sm_token = _gather_start(small_g, grad_x2, "gather_small_start")
    recv_in, recv_sq = _exchange_chips_wait(sems, sent, lands, sm_token, "exchange_dw_wait")

    g_in, d_in, nm_in, nv_in = [a.T for a in _reduce_adamw(
        own_in, recv_in, place, w_in[0].T, m_w_in[0].T, v_w_in[0].T, "adamw_w_in")]
    sq_w = jnp.concatenate([w_branch_a[0], w_branch_r[0], w_out[0]], axis=0)
    sq_m = jnp.concatenate([m_w_branch_a[0], m_w_branch_r[0], m_w_out[0]], axis=0)
    sq_v = jnp.concatenate([v_w_branch_a[0], v_w_branch_r[0], v_w_out[0]], axis=0)
    g_sq, d_sq, nm_sq, nv_sq = _reduce_adamw(own_sq, recv_sq, place, sq_w, sq_m, sq_v, "adamw_w_sq")
    small_all = _gather_wait(sm_sems, sm_src, sm_land, d_sq, "gather_small_wait")
    small_all = lax.dynamic_update_slice(small_all, small_g[None], (me, 0, 0))

    def place_conv(a):
        return lax.dynamic_update_slice(jnp.zeros((CONV_W, D_MODEL), F32), a[0], (0, me * LANES))

    zrow = jnp.zeros((1, D_MODEL), F32)
    small_w = _pack_small(pre_norm_w, conv_b, rg_ba, rg_bx, rg_lambda, post_norm_w, zrow, b_in,
                          place_conv(conv_w), rg_wa[0], rg_wx[0])
    small_m = _pack_small(m_pre_norm_w, m_conv_b, m_rg_ba, m_rg_bx, m_rg_lambda, m_post_norm_w, zrow, m_b_in,
                          place_conv(m_conv_w), m_rg_wa[0], m_rg_wx[0])
    small_v = _pack_small(v_pre_norm_w, v_conv_b, v_rg_ba, v_rg_bx, v_rg_lambda, v_post_norm_w, zrow, v_b_in,
                          place_conv(v_conv_w), v_rg_wa[0], v_rg_wx[0])
    outs_small = [_unpack_small(p) for p in _reduce_small(small_all, small_w, small_m, small_v)]

    loss = (0.5 / D_MODEL) * jnp.sum(outs_small[0]["loss_row"])

    def leaf(kind, name):
        if name == "w_in":
            return (g_in, d_in, nm_in, nv_in)[kind][None]
        if name in ("w_branch_a", "w_branch_r", "w_out"):
            j = ("w_branch_a", "w_branch_r", "w_out").index(name)
            return (g_sq, d_sq, nm_sq, nv_sq)[kind][None, j * shard_rows:(j + 1) * shard_rows]
        if name == "conv_w":
            return lax.dynamic_slice(outs_small[kind]["conv_w_full"], (0, me * LANES), (CONV_W, LANES))[None]
        return outs_small[kind][name]

    names = ["pre_norm_w", "w_in", "b_in", "conv_w", "conv_b", "rg_wa", "rg_ba", "rg_wx", "rg_bx", "rg_lambda",
             "w_branch_a", "w_branch_r", "w_out", "post_norm_w"]
    out = [loss, grad_x2.reshape(b, s, D_MODEL)]
    for kind in range(4):
        out += [leaf(kind, nm) for nm in names]
    return tuple(out)
```

```python
import jax
import jax.numpy as jnp
from jax import lax
from jax.experimental import pallas as pl
from jax.experimental.pallas import tpu as pltpu

F32 = jnp.float32
BF16 = jnp.bfloat16

N_DEV = 8
D_MODEL = 1024
HEADS = 16
HEAD_DIM = 64
HEAD_PAIRS = HEADS // 2
LANES = 128
N_CBLK = D_MODEL // LANES
CONV_W = 4
RG_C = 8.0
NORM_EPS = 1e-6
MASK_VALUE = -1e30
IN_USED = 8208
IN_TOTAL = 9232
W_SHARD = IN_TOTAL // N_DEV

ADAM_LR = 0.001
ADAM_B1 = 0.9
ADAM_B2 = 0.999
ADAM_EPS = 1e-08
ADAM_WD = 0.01
ADAM_STEP = 10

ATT_TILE = 512
SCAN_TILE = 256
SMALL_ROWS = 152


def _cparams(sem=None, vmem_mb=None):
    kw = {}
    if sem is not None:
        kw["dimension_semantics"] = sem
    if vmem_mb is not None:
        kw["vmem_limit_bytes"] = vmem_mb * 1024 * 1024
    return pltpu.CompilerParams(**kw)


def _sigmoid(x):
    return 1.0 / (1.0 + jnp.exp(-x))


def _softplus(x):
    return jnp.maximum(x, 0.0) + jnp.log1p(jnp.exp(-jnp.abs(x)))


def _one_minus_exp(y, exp_y):
    series = -y * (1.0 + y * (1.0 / 2 + y * (1.0 / 6 + y * (1.0 / 24 + y * (1.0 / 120)))))
    return jnp.where(y > -0.0625, series, 1.0 - exp_y)


def _split3(x):
    hi = x.astype(BF16)
    r1 = x - hi.astype(F32)
    mid = r1.astype(BF16)
    lo = (r1 - mid.astype(F32)).astype(BF16)
    return hi, mid, lo


def _dot(a, b):
    return jnp.dot(a, b, preferred_element_type=F32)


def _dot_nt(a, b):
    return lax.dot_general(a, b, (((1,), (1,)), ((), ())), preferred_element_type=F32)


def _dot_tn(a, b):
    return lax.dot_general(a, b, (((0,), (0,)), ((), ())), preferred_element_type=F32)


def _iota(shape, dim):
    return lax.broadcasted_iota(jnp.int32, shape, dim)


_ANY = pl.BlockSpec(memory_space=pl.ANY)
_MESH = pl.DeviceIdType.MESH
N_CHIPS = 4


def _place():
    x, y, c = lax.axis_index("x"), lax.axis_index("y"), lax.axis_index("c")
    other_chips = [(1 - x, y), (x, 1 - y), (1 - x, 1 - y)]
    return x, y, c, other_chips


def _gather(x_shard, name):
    def body(x_ref, out_ref, send_sems, recv_sems, local_sem):
        x, y, c, chips = _place()
        me, sibling = (x, y, c), (x, y, 1 - c)

        def slot(p):
            return out_ref.at[4 * p[0] + 2 * p[1] + p[2]]

        def copy(k, block, to, src=None):
            return pltpu.make_async_remote_copy(
                src_ref=slot(block) if src is None else src, dst_ref=slot(block),
                send_sem=send_sems.at[k], recv_sem=recv_sems.at[k], device_id=to, device_id_type=_MESH)

        mine = pltpu.make_async_copy(x_ref, slot(me), local_sem)
        mine.start()
        first = [copy(0, me, sibling, src=x_ref)]
        first += [copy(1 + j, me, (*chip, c), src=x_ref) for j, chip in enumerate(chips)]
        for cp in first:
            cp.start()
        passed = [copy(4 + j, (*chip, c), sibling) for j, chip in enumerate(chips)]
        for j, chip in enumerate(chips):
            copy(1 + j, (*chip, c), me).wait_recv()
            passed[j].start()
        copy(0, sibling, me).wait_recv()
        for j, chip in enumerate(chips):
            copy(4 + j, (*chip, 1 - c), me).wait_recv()
        for cp in first + passed:
            cp.wait_send()
        mine.wait()

    return pl.pallas_call(
        body, name=name,
        out_shape=jax.ShapeDtypeStruct((N_DEV,) + tuple(x_shard.shape), x_shard.dtype),
        in_specs=[_ANY], out_specs=_ANY,
        scratch_shapes=[pltpu.SemaphoreType.DMA((7,)), pltpu.SemaphoreType.DMA((7,)), pltpu.SemaphoreType.DMA],
    )(x_shard)


def _swap_with_sibling(srcs, name):
    n = len(srcs)

    def body(*refs):
        src_refs, out_refs = refs[:n], refs[n:2 * n]
        send_sems, recv_sems = refs[2 * n:]
        x, y, c, _ = _place()
        cps = [pltpu.make_async_remote_copy(
            src_ref=src_refs[i].at[1 - c], dst_ref=out_refs[i], send_sem=send_sems.at[i], recv_sem=recv_sems.at[i],
            device_id=(x, y, 1 - c), device_id_type=_MESH) for i in range(n)]
        for cp in cps:
            cp.start()
        for cp in cps:
            cp.wait()

    return pl.pallas_call(
        body, name=name,
        out_shape=[jax.ShapeDtypeStruct(a.shape[1:], a.dtype) for a in srcs],
        in_specs=[_ANY] * n, out_specs=[_ANY] * n,
        scratch_shapes=[pltpu.SemaphoreType.DMA((n,)), pltpu.SemaphoreType.DMA((n,))],
    )(*srcs)


def _blocks_2d(r, c):
    if r % 128 == 0:
        return (128, c), r // 128, lambda i: (i, 0)
    return (r, 256), c // 256, lambda i: (0, i)


def _pair_add(src, recv, place, name):
    _, _, r, c = src.shape
    blk, nblk, at = _blocks_2d(r, c)

    def body(place_ref, a_ref, b_ref, q16_ref, own_ref):
        q = a_ref[...] + b_ref[...]
        q16_ref[...] = q.astype(BF16)

        @pl.when(pl.program_id(1) == place_ref[1])
        def _():
            own_ref[...] = q

    grid_spec = pltpu.PrefetchScalarGridSpec(
        num_scalar_prefetch=1, grid=(nblk, N_CHIPS),
        in_specs=[pl.BlockSpec((None, None) + blk, lambda i, j, pr: (pr[0], j) + at(i)),
                  pl.BlockSpec((None,) + blk, lambda i, j, pr: (j,) + at(i))],
        out_specs=[pl.BlockSpec((None,) + blk, lambda i, j, pr: (j,) + at(i)),
                   pl.BlockSpec(blk, lambda i, j, pr: at(i))])
    return pl.pallas_call(
        body, name=name, grid_spec=grid_spec,
        out_shape=[jax.ShapeDtypeStruct((N_CHIPS, r, c), BF16), jax.ShapeDtypeStruct((r, c), F32)],
        compiler_params=_cparams(("parallel", "arbitrary")),
    )(place, src, recv)


_HBM = pl.BlockSpec(memory_space=pltpu.HBM)
_SEM = pl.BlockSpec(memory_space=pltpu.SEMAPHORE)
_DATAFLOW = pltpu.SideEffectType.DATAFLOW_SIDE_EFFECTING


def _chip_copy(src_ref, land_ref, send_sem, recv_sem, k, chips, c, land):
    chip = chips[k]
    return pltpu.make_async_remote_copy(
        src_ref=src_ref.at[2 * chip[0] + chip[1]], dst_ref=land_ref.at[land],
        send_sem=send_sem, recv_sem=recv_sem, device_id=(*chip, c), device_id_type=_MESH)


def _exchange_chips_start(srcs, name):
    n = len(srcs)
    ncp = 3 * n

    def body(*refs):
        src_refs, land_refs = refs[:n], refs[n:2 * n]
        sems = refs[4 * n:4 * n + 2 * ncp]
        token = refs[-1]
        x, y, c, chips = _place()
        for i in range(n):
            for k in range(3):
                j = 3 * i + k
                _chip_copy(src_refs[i], land_refs[i], sems[j], sems[ncp + j], k, chips, c, 2 * x + y).start()
        token[...] = jnp.zeros_like(token)

    hbm = [pltpu.HBM(a.shape, a.dtype) for a in srcs]
    lands = [pltpu.with_memory_space_constraint(lax.empty(a.shape, a.dtype), pltpu.HBM) for a in srcs]
    res = pl.pallas_call(
        body, name=name,
        out_shape=(*hbm, *hbm, *([pltpu.SemaphoreType.DMA(())] * (2 * ncp)), jax.ShapeDtypeStruct((8, LANES), F32)),
        in_specs=[_HBM] * (2 * n),
        out_specs=(*([_HBM] * (2 * n)), *([_SEM] * (2 * ncp)), pl.BlockSpec(memory_space=pltpu.VMEM)),
        input_output_aliases={i: i for i in range(2 * n)},
        compiler_params=pltpu.CompilerParams(has_side_effects=_DATAFLOW),
    )(*[pltpu.with_memory_space_constraint(a, pltpu.HBM) for a in srcs], *lands)
    return list(res[2 * n:2 * n + 2 * ncp]), list(res[:n]), list(res[n:2 * n]), res[-1]


def _exchange_chips_wait(sems, srcs, lands, after, name):
    n = len(srcs)
    ncp = 3 * n

    def body(*refs):
        src_refs, land_refs = refs[:n], refs[n:2 * n]
        sem_refs = refs[2 * n:2 * n + 2 * ncp]
        x, y, c, chips = _place()
        for i in range(n):
            for k in range(3):
                j = 3 * i + k
                cp = _chip_copy(src_refs[i], land_refs[i], sem_refs[j], sem_refs[ncp + j], k, chips, c,
                                2 * chips[k][0] + chips[k][1])
                cp.wait_send()
                cp.wait_recv()

    hbm = [pltpu.HBM(a.shape, a.dtype) for a in srcs]
    res = pl.pallas_call(
        body, name=name, out_shape=(*hbm, *hbm),
        in_specs=[_HBM] * (2 * n) + [_SEM] * (2 * ncp) + [_ANY], out_specs=tuple([_HBM] * (2 * n)),
        input_output_aliases={i: i for i in range(2 * n)},
        compiler_params=pltpu.CompilerParams(has_side_effects=_DATAFLOW),
    )(*srcs, *lands, *sems, after)
    return list(res[n:2 * n])


def _peer_copy(src_ref, land_ref, send_sem, recv_sem, k, place, land):
    x, y, c = place
    peer = (1 - x if k & 4 else x, 1 - y if k & 2 else y, 1 - c if k & 1 else c)
    return pltpu.make_async_remote_copy(
        src_ref=src_ref, dst_ref=land_ref.at[land], send_sem=send_sem, recv_sem=recv_sem,
        device_id=peer, device_id_type=_MESH)


def _gather_start(x_shard, after, name):
    npeer = N_DEV - 1

    def body(x_ref, land_ref, after_ref, x_thru, land_thru, *rest):
        sems, token = rest[:2 * npeer], rest[-1]
        x, y, c, _ = _place()
        for k in range(1, N_DEV):
            _peer_copy(x_ref, land_ref, sems[k - 1], sems[npeer + k - 1], k, (x, y, c), 4 * x + 2 * y + c).start()
        token[...] = jnp.zeros_like(token)

    land = pltpu.with_memory_space_constraint(lax.empty((N_DEV,) + tuple(x_shard.shape), x_shard.dtype), pltpu.HBM)
    res = pl.pallas_call(
        body, name=name,
        out_shape=(pltpu.HBM(x_shard.shape, x_shard.dtype), pltpu.HBM(land.shape, land.dtype),
                   *([pltpu.SemaphoreType.DMA(())] * (2 * npeer)), jax.ShapeDtypeStruct((8, LANES), F32)),
        in_specs=[_HBM, _HBM, _ANY],
        out_specs=(_HBM, _HBM, *([_SEM] * (2 * npeer)), pl.BlockSpec(memory_space=pltpu.VMEM)),
        input_output_aliases={0: 0, 1: 1},
        compiler_params=pltpu.CompilerParams(has_side_effects=_DATAFLOW),
    )(pltpu.with_memory_space_constraint(x_shard, pltpu.HBM), land, after)
    return list(res[2:2 + 2 * npeer]), res[0], res[1], res[-1]


def _gather_wait(sems, src, land, after, name):
    npeer = N_DEV - 1

    def body(x_ref, land_ref, *rest):
        sem_refs = rest[:2 * npeer]
        x, y, c, _ = _place()
        for k in range(1, N_DEV):
            peer_index = (4 * x + 2 * y + c) ^ k
            cp = _peer_copy(x_ref, land_ref, sem_refs[k - 1], sem_refs[npeer + k - 1], k, (x, y, c), peer_index)
            cp.wait_send()
            cp.wait_recv()

    res = pl.pallas_call(
        body, name=name, out_shape=(pltpu.HBM(src.shape, src.dtype), pltpu.HBM(land.shape, land.dtype)),
        in_specs=[_HBM, _HBM] + [_SEM] * (2 * npeer) + [_ANY], out_specs=(_HBM, _HBM),
        input_output_aliases={0: 0, 1: 1},
        compiler_params=pltpu.CompilerParams(has_side_effects=_DATAFLOW),
    )(src, land, *sems, after)
    return res[1]


def _prenorm(x2, w):
    t = x2.shape[0]
    tm = min(512, t)

    def body(x_ref, w_ref, h_ref):
        x = x_ref[...]
        r = lax.rsqrt(jnp.mean(x * x, axis=-1, keepdims=True) + NORM_EPS)
        h_ref[...] = (x * r * w_ref[...]).astype(BF16)

    return pl.pallas_call(
        body, name="prenorm", grid=(t // tm,),
        in_specs=[pl.BlockSpec((tm, D_MODEL), lambda i: (i, 0)), pl.BlockSpec((1, D_MODEL), lambda i: (0, 0))],
        out_specs=pl.BlockSpec((tm, D_MODEL), lambda i: (i, 0)),
        out_shape=jax.ShapeDtypeStruct((t, D_MODEL), BF16),
        compiler_params=_cparams(("parallel",)),
    )(x2, w)


def _mm_bias(a, bt, bias, out_dtype, name):
    m, k = a.shape
    n = bt.shape[0]
    tm = min(512, m)
    tn = min(1024, n)

    def body(a_ref, bt_ref, bias_ref, o_ref):
        o_ref[...] = (_dot_nt(a_ref[...], bt_ref[...]) + bias_ref[...]).astype(o_ref.dtype)

    return pl.pallas_call(
        body, name=name, grid=(n // tn, m // tm),
        in_specs=[pl.BlockSpec((tm, k), lambda j, i: (i, 0)), pl.BlockSpec((tn, k), lambda j, i: (j, 0)),
                  pl.BlockSpec((1, tn), lambda j, i: (0, j))],
        out_specs=pl.BlockSpec((tm, tn), lambda j, i: (i, j)),
        out_shape=jax.ShapeDtypeStruct((m, n), out_dtype),
        compiler_params=_cparams(("parallel", "parallel")),
    )(a, bt, bias)


def _mm_tn(a, b, name):
    t, m = a.shape
    n = b.shape[1]
    tm = min(1024, m)
    tk = min(512, t)

    def body(a_ref, b_ref, o_ref, s_ref):
        kk = pl.program_id(1)

        @pl.when(kk == 0)
        def _():
            o_ref[...] = jnp.zeros_like(o_ref)
            s_ref[...] = jnp.zeros_like(s_ref)

        aa = a_ref[...]
        o_ref[...] += _dot_tn(aa, b_ref[...])
        s_ref[0:1, :] += jnp.sum(aa.astype(F32), axis=0, keepdims=True)

    return pl.pallas_call(
        body, name=name, grid=(m // tm, t // tk),
        in_specs=[pl.BlockSpec((tk, tm), lambda i, kk: (kk, i)), pl.BlockSpec((tk, n), lambda i, kk: (kk, 0))],
        out_specs=[pl.BlockSpec((tm, n), lambda i, kk: (i, 0)), pl.BlockSpec((8, tm), lambda i, kk: (0, i))],
        out_shape=[jax.ShapeDtypeStruct((m, n), F32), jax.ShapeDtypeStruct((8, m), F32)],
        compiler_params=_cparams(("parallel", "arbitrary")),
    )(a, b)


def _fgate_fwd(zf3):
    b, s, _ = zf3.shape
    tb = SCAN_TILE
    nb = s // tb

    def body(z_ref, cexp_ref, crow_ref):
        tri = (_iota((tb, tb), 1) <= _iota((tb, tb), 0)).astype(BF16)
        expand = ((_iota((LANES, D_MODEL), 1) >> 6) == _iota((LANES, D_MODEL), 0)).astype(BF16)
        carry = jnp.zeros((1, LANES), F32)
        for i in range(nb):
            rows = slice(i * tb, (i + 1) * tb)
            z = z_ref[rows, :]
            lf = jnp.minimum(z, 0.0) - jnp.log1p(jnp.exp(-jnp.abs(z)))
            cb = sum(_dot(tri, part) for part in _split3(lf)) + carry
            carry = cb[tb - 1:tb, :]
            cexp_ref[rows, :] = sum(_dot(part, expand) for part in _split3(cb))
            crow_ref[:, rows] = cb.T[0:HEADS, :]

    return pl.pallas_call(
        body, name="fgate_fwd", grid=(b,),
        in_specs=[pl.BlockSpec((None, s, LANES), lambda i: (i, 0, 0))],
        out_specs=[pl.BlockSpec((None, s, D_MODEL), lambda i: (i, 0, 0)),
                   pl.BlockSpec((None, HEADS, s), lambda i: (i, 0, 0))],
        out_shape=[jax.ShapeDtypeStruct((b, s, D_MODEL), F32), jax.ShapeDtypeStruct((b, HEADS, s), F32)],
        compiler_params=_cparams(("parallel",)),
    )(zf3)


def _fgate_bwd(dc3, zf3):
    b, s, _ = zf3.shape
    tb = SCAN_TILE
    nb = s // tb

    def body(dc_ref, z_ref, o_ref):
        tri = (_iota((tb, tb), 1) >= _iota((tb, tb), 0)).astype(BF16)
        carry = jnp.zeros((1, LANES), F32)
        for i in reversed(range(nb)):
            rows = slice(i * tb, (i + 1) * tb)
            dlf = sum(_dot(tri, part) for part in _split3(dc_ref[rows, :])) + carry
            carry = dlf[0:1, :]
            o_ref[rows, :] = (dlf * _sigmoid(-z_ref[rows, :])).astype(BF16)

    return pl.pallas_call(
        body, name="fgate_bwd", grid=(b,),
        in_specs=[pl.BlockSpec((None, s, LANES), lambda i: (i, 0, 0)),
                  pl.BlockSpec((None, s, LANES), lambda i: (i, 0, 0))],
        out_specs=pl.BlockSpec((None, s, LANES), lambda i: (i, 0, 0)),
        out_shape=jax.ShapeDtypeStruct((b, s, LANES), BF16),
        compiler_params=_cparams(("parallel",)),
    )(dc3, zf3)


def _spare(hh):
    return HEAD_DIM if hh == 0 else 0


def _put_cols(tile, mine, cols, first):
    lane = _iota((1, LANES), 1)
    out = jnp.where(mine, tile, jnp.zeros((), tile.dtype))
    for j, c in enumerate(cols):
        out = jnp.where(lane == first + j, c, out)
    return out


def _put_rows(tile, mine, rows, first):
    sub = _iota((LANES, 1), 0)
    out = jnp.where(mine, tile, jnp.zeros((), tile.dtype))
    for j, r in enumerate(rows):
        out = jnp.where(sub == first + j, r, out)
    return out


def _transpose_bf16(a):
    return a.astype(F32).T.astype(BF16)


def _attn_fwd(qkv3, cexp3, crow5, zrest3):
    b, s, _ = qkv3.shape
    ta = ATT_TILE
    nq = s // ta
    hd = HEAD_DIM

    def body(qkv_ref, cq_ref, ck_ref, g_ref, y_ref, lse_ref, ga_ref, kt_scr, v_scr):
        lane = _iota((1, LANES), 1)
        sub = _iota((LANES, 1), 0)
        lane_mine = (lane < hd, lane >= hd)
        sub_mine = (sub < hd, sub >= hd)
        causal = _iota((ta, ta), 0) >= _iota((ta, ta), 1)
        one = jnp.ones((), BF16)

        for kj in range(nq):
            rows = slice(kj * ta, (kj + 1) * ta)
            kt = _transpose_bf16(qkv_ref[rows, LANES:2 * LANES])
            v = qkv_ref[rows, 2 * LANES:3 * LANES]
            for hh in range(2):
                ck = list(_split3(-ck_ref[hh, kj:kj + 1, :]))
                kt_scr[hh, kj] = _put_rows(kt, sub_mine[hh], [one, one, one] + ck, _spare(hh))
                v_scr[hh, kj] = _put_cols(v, lane_mine[hh], [one], _spare(hh))

        for qi in range(nq):
            rows = slice(qi * ta, (qi + 1) * ta)
            q = qkv_ref[rows, 0:LANES] * 0.125
            cq = cq_ref[rows, :]
            qh = [_put_cols(q, lane_mine[hh], list(_split3(cq[:, hh * hd:hh * hd + 1])) + [one, one, one], _spare(hh))
                  for hh in range(2)]
            st = [(jnp.full((ta, 1), MASK_VALUE, F32), jnp.zeros((ta, LANES), F32))] * 2
            for kj in range(qi + 1):
                for hh in range(2):
                    m, acc = st[hh]
                    sc = _dot(qh[hh], kt_scr[hh, kj])
                    if kj == qi:
                        sc = jnp.where(causal, sc, MASK_VALUE)
                    mn = jnp.maximum(m, jnp.max(sc, axis=-1, keepdims=True))
                    p = jnp.exp(sc - mn).astype(BF16)
                    st[hh] = (mn, jnp.exp(m - mn) * acc + _dot(p, v_scr[hh, kj]))
            (ma, acca), (mb, accb) = st
            la = acca[:, hd:hd + 1]
            lb = accb[:, 0:1]
            y = jnp.where(lane_mine[0], acca * (1.0 / la), accb * (1.0 / lb))
            lse = jnp.where(lane_mine[0], ma + jnp.log(la), mb + jnp.log(lb)).T
            lse_ref[0, qi:qi + 1, :] = lse[0:1, :]
            lse_ref[1, qi:qi + 1, :] = lse[hd:hd + 1, :]
            y_ref[rows, :] = y
            g = g_ref[rows, :].astype(F32)
            ga_ref[rows, :] = (y * (g * _sigmoid(g))).astype(BF16)

    blk = lambda w: pl.BlockSpec((None, s, w), lambda i, p: (i, 0, p))
    rows5 = pl.BlockSpec((None, None, 2, nq, ta), lambda i, p: (i, p, 0, 0, 0))
    return pl.pallas_call(
        body, name="attn_fwd", grid=(b, HEAD_PAIRS),
        in_specs=[blk(3 * LANES), blk(LANES), rows5, blk(LANES)],
        out_specs=[blk(LANES), rows5, blk(LANES)],
        out_shape=[jax.ShapeDtypeStruct((b, s, D_MODEL), F32),
                   jax.ShapeDtypeStruct((b, HEAD_PAIRS, 2, nq, ta), F32),
                   jax.ShapeDtypeStruct((b, s, D_MODEL), BF16)],
        scratch_shapes=[pltpu.VMEM((2, nq, LANES, ta), BF16), pltpu.VMEM((2, nq, ta, LANES), BF16)],
        compiler_params=_cparams(("parallel", "parallel")),
    )(qkv3, cexp3, crow5, zrest3)


def _attn_bwd(qkv3, do3, y3, lse5, crow5, cexp3):
    b, s, _ = qkv3.shape
    ta = ATT_TILE
    nq = s // ta
    hd = HEAD_DIM

    def body(qkv_ref, do_ref, y_ref, lse_ref, crow_ref, cexp_ref, dqkv_ref, dc_ref,
             qa_scr, doa_scr, qst_scr, dot_scr, kt_scr, vt_scr, dq_scr, rs_scr):
        pair = pl.program_id(1)
        lane = _iota((1, LANES), 1)
        sub = _iota((LANES, 1), 0)
        lane_mine = (lane < hd, lane >= hd)
        sub_mine = (sub < hd, sub >= hd)
        causal = _iota((ta, ta), 0) >= _iota((ta, ta), 1)
        one = jnp.ones((), BF16)
        zero = jnp.zeros((), BF16)

        @pl.when(pair == 0)
        def _():
            dc_ref[...] = jnp.zeros_like(dc_ref)

        for i in range(nq):
            rows = slice(i * ta, (i + 1) * ta)
            qs = qkv_ref[rows, 0:LANES] * 0.125
            qst = _transpose_bf16(qs)
            kt = _transpose_bf16(qkv_ref[rows, LANES:2 * LANES])
            vt = _transpose_bf16(qkv_ref[rows, 2 * LANES:3 * LANES])
            do = do_ref[rows, :]
            dof = do.astype(F32)
            dot = dof.T.astype(BF16)
            pr = y_ref[rows, :] * dof
            cq = cexp_ref[rows, :]
            lse_c = jnp.where(sub == 0, lse_ref[0, i:i + 1, :],
                              jnp.where(sub == 1, lse_ref[1, i:i + 1, :], 0.0)).T
            for hh in range(2):
                sp = _spare(hh)
                dsum = jnp.sum(jnp.where(lane_mine[hh], pr, 0.0), axis=-1, keepdims=True)
                bias = cq[:, hh * hd:hh * hd + 1] - lse_c[:, hh:hh + 1]
                qa_scr[hh, i] = _put_cols(qs, lane_mine[hh], list(_split3(bias)) + [one, one, one], sp)
                doa_scr[hh, i] = _put_cols(do, lane_mine[hh], list(_split3(-dsum)), sp)
                qst_scr[hh, i] = jnp.where(sub_mine[hh], qst, zero)
                dot_scr[hh, i] = jnp.where(sub_mine[hh], dot, zero)
                ck = list(_split3(-crow_ref[hh, i:i + 1, :]))
                kt_scr[hh, i] = _put_rows(kt, sub_mine[hh], [one, one, one] + ck, sp)
                vt_scr[hh, i] = _put_rows(vt, sub_mine[hh], [one, one, one], sp)
            dq_scr[i] = jnp.zeros((ta, LANES), F32)
            rs_scr[i] = jnp.zeros((ta, LANES), F32)

        for kj in range(nq):
            krows = slice(kj * ta, (kj + 1) * ta)
            k = qkv_ref[krows, LANES:2 * LANES]
            km = (jnp.where(lane_mine[0], k, zero), jnp.where(lane_mine[1], k, zero))
            dkt = jnp.zeros((LANES, ta), F32)
            dvt = jnp.zeros((LANES, ta), F32)
            dcp = [jnp.zeros((8, ta), F32), jnp.zeros((8, ta), F32)]
            for qi in range(kj, nq):
                dq = jnp.zeros((ta, LANES), F32)
                rs = []
                for hh in range(2):
                    sc = _dot(qa_scr[hh, qi], kt_scr[hh, kj])
                    if qi == kj:
                        sc = jnp.where(causal, sc, MASK_VALUE)
                    p = jnp.exp(sc)
                    dsf = p * _dot(doa_scr[hh, qi], vt_scr[hh, kj])
                    dcp[hh] = dcp[hh] + jnp.sum(dsf.reshape(ta // 8, 8, ta), axis=0)
                    rs.append(jnp.sum(dsf, axis=-1, keepdims=True))
                    ds = dsf.astype(BF16)
                    dq = dq + _dot(ds, km[hh])
                    dkt = dkt + _dot(qst_scr[hh, qi], ds)
                    dvt = dvt + _dot(dot_scr[hh, qi], p.astype(BF16))
                dq_scr[qi] += dq
                rs_scr[qi] += jnp.where(lane == 0, rs[0], jnp.where(lane == 1, rs[1], 0.0))
            dqkv_ref[krows, LANES:2 * LANES] = dkt.T.astype(BF16)
            dqkv_ref[krows, 2 * LANES:3 * LANES] = dvt.T.astype(BF16)
            dca = jnp.sum(dcp[0], axis=0, keepdims=True)
            dcb = jnp.sum(dcp[1], axis=0, keepdims=True)
            dcs = jnp.where(sub == 0, dca, jnp.where(sub == 1, dcb, 0.0)).T
            dc_ref[krows, :] += (jnp.where(lane == 2 * pair, -dcs[:, 0:1], 0.0)
                                 + jnp.where(lane == 2 * pair + 1, -dcs[:, 1:2], 0.0))
        for qi in range(nq):
            rows = slice(qi * ta, (qi + 1) * ta)
            dqkv_ref[rows, 0:LANES] = (dq_scr[qi] * 0.125).astype(BF16)
            rq = rs_scr[qi]
            dc_ref[rows, :] += (jnp.where(lane == 2 * pair, rq[:, 0:1], 0.0)
                                + jnp.where(lane == 2 * pair + 1, rq[:, 1:2], 0.0))

    blk = lambda w: pl.BlockSpec((None, s, w), lambda i, p: (i, 0, p))
    rows5 = pl.BlockSpec((None, None, 2, nq, ta), lambda i, p: (i, p, 0, 0, 0))
    by_rows = lambda: pltpu.VMEM((2, nq, ta, LANES), BF16)
    by_cols = lambda: pltpu.VMEM((2, nq, LANES, ta), BF16)
    return pl.pallas_call(
        body, name="attn_bwd", grid=(b, HEAD_PAIRS),
        in_specs=[blk(3 * LANES), blk(LANES), blk(LANES), rows5, rows5, blk(LANES)],
        out_specs=[blk(3 * LANES), pl.BlockSpec((None, s, LANES), lambda i, p: (i, 0, 0))],
        out_shape=[jax.ShapeDtypeStruct((b, s, 3 * D_MODEL), BF16), jax.ShapeDtypeStruct((b, s, LANES), F32)],
        scratch_shapes=[by_rows(), by_rows(), by_cols(), by_cols(), by_cols(), by_cols(),
                        pltpu.VMEM((nq, ta, LANES), F32), pltpu.VMEM((nq, ta, LANES), F32)],
        compiler_params=_cparams(("parallel", "arbitrary")),
    )(qkv3, do3, y3, lse5, crow5, cexp3)


def _shifted(v, ks, rows, s):
    return [jnp.where(rows >= k, pltpu.roll(v, k, 0), 0.0) if k > 0
            else jnp.where(rows < s + k, pltpu.roll(v, s + k, 0), 0.0) for k in ks]


def _rnn_common(xr, cw_ref, cb_ref, bda_ref, bdx_ref, ba_ref, bx_ref, lam_ref, s):
    rows = _iota((s, LANES), 0)
    x1, x2, x3 = _shifted(xr, (1, 2, 3), rows, s)
    xc = cb_ref[...] + cw_ref[0:1, :] * x3
    xc = xc + cw_ref[1:2, :] * x2
    xc = xc + cw_ref[2:3, :] * x1
    xc = xc + cw_ref[3:4, :] * xr
    xcb = xc.astype(BF16)
    r = _sigmoid(_dot(xcb, bda_ref[...]) + ba_ref[...])
    i = _sigmoid(_dot(xcb, bdx_ref[...]) + bx_ref[...])
    sp = _softplus(-lam_ref[...])
    log_a = (-RG_C * r) * sp
    a = jnp.exp(log_a)
    a2 = a * a
    sq = jnp.sqrt(jnp.maximum(_one_minus_exp(2.0 * log_a, a2), 0.0))
    return rows, (x1, x2, x3), xc, xcb, r, i, sp, a, a2, sq


def _scan_down(a, u, rows, s, s1, s2):
    low = rows & 7
    for sh in (1, 2, 4):
        keep = low >= sh
        u = u + a * jnp.where(keep, pltpu.roll(u, sh, 0), 0.0)
        a = a * jnp.where(keep, pltpu.roll(a, sh, 0), 1.0)
    ng = s // 8
    s1[...] = a
    s2[...] = u
    at = s1[pl.ds(7, ng, stride=8), :]
    ut = s2[pl.ds(7, ng, stride=8), :]
    grow = _iota((ng, LANES), 0)
    sh = 1
    while sh < ng:
        keep = grow >= sh
        ut = ut + at * jnp.where(keep, pltpu.roll(ut, sh, 0), 0.0)
        if sh * 2 < ng:
            at = at * jnp.where(keep, pltpu.roll(at, sh, 0), 1.0)
        sh *= 2
    h_in = jnp.where(grow >= 1, pltpu.roll(ut, 1, 0), 0.0)
    for k in range(8):
        s1[pl.ds(k, ng, stride=8), :] = h_in
    return u + a * s1[...]


def _scan_up(a, g, rows, s, s1, s2):
    low = rows & 7
    for sh in (1, 2, 4):
        keep = low < 8 - sh
        g = g + a * jnp.where(keep, pltpu.roll(g, s - sh, 0), 0.0)
        a = a * jnp.where(keep, pltpu.roll(a, s - sh, 0), 1.0)
    ng = s // 8
    s1[...] = a
    s2[...] = g
    at = s1[pl.ds(0, ng, stride=8), :]
    gt = s2[pl.ds(0, ng, stride=8), :]
    grow = _iota((ng, LANES), 0)
    sh = 1
    while sh < ng:
        keep = grow < ng - sh
        gt = gt + at * jnp.where(keep, pltpu.roll(gt, ng - sh, 0), 0.0)
        if sh * 2 < ng:
            at = at * jnp.where(keep, pltpu.roll(at, ng - sh, 0), 1.0)
        sh *= 2
    g_in = jnp.where(grow < ng - 1, pltpu.roll(gt, ng - 1, 0), 0.0)
    for k in range(8):
        s1[pl.ds(k, ng, stride=8), :] = g_in
    return g + a * s1[...]


def _rnn_specs(s):
    blk = lambda off: pl.BlockSpec((None, s, LANES), lambda cb, i: (i, 0, off + cb))
    vec = lambda r: pl.BlockSpec((r, LANES), lambda cb, i: (0, cb))
    mat = pl.BlockSpec((None, LANES, LANES), lambda cb, i: (cb, 0, 0))
    return blk, vec, mat


def _rnn_fwd(zrest3, conv_w, conv_b, bda, bdx, ba, bx, lam):
    b, s, _ = zrest3.shape

    def body(xr_ref, g_ref, cw_ref, cb_ref, bda_ref, bdx_ref, ba_ref, bx_ref, lam_ref, h_ref, gr_ref, s1, s2):
        xr = xr_ref[...].astype(F32)
        rows, _, xc, _, _, i, _, a, _, sq = _rnn_common(
            xr, cw_ref, cb_ref, bda_ref, bdx_ref, ba_ref, bx_ref, lam_ref, s)
        h = _scan_down(a, sq * (i * xc), rows, s, s1, s2)
        h_ref[...] = h
        g = g_ref[...].astype(F32)
        gr_ref[...] = (h * (g * _sigmoid(g))).astype(BF16)

    blk, vec, mat = _rnn_specs(s)
    return pl.pallas_call(
        body, name="rnn_fwd", grid=(N_CBLK, b),
        in_specs=[blk(N_CBLK), blk(2 * N_CBLK), vec(CONV_W), vec(1), mat, mat, vec(1), vec(1), vec(1)],
        out_specs=[blk(0), blk(0)],
        out_shape=[jax.ShapeDtypeStruct((b, s, D_MODEL), F32), jax.ShapeDtypeStruct((b, s, D_MODEL), BF16)],
        scratch_shapes=[pltpu.VMEM((s, LANES), F32), pltpu.VMEM((s, LANES), F32)],
        compiler_params=_cparams(("parallel", "parallel")),
    )(zrest3, zrest3, conv_w, conv_b, bda, bdx, ba, bx, lam)


def _rnn_bwd(zrest3, h3, dh3, conv_w, conv_b, bda, bdx, ba, bx, lam):
    b, s, _ = zrest3.shape

    def body(xr_ref, h_ref, dh_ref, cw_ref, cb_ref, bda_ref, bdx_ref, ba_ref, bx_ref, lam_ref,
             dxr_ref, pv_ref, dbd_ref, s1, s2):
        @pl.when(pl.program_id(1) == 0)
        def _():
            pv_ref[...] = jnp.zeros_like(pv_ref)
            dbd_ref[...] = jnp.zeros_like(dbd_ref)

        xr = xr_ref[...].astype(F32)
        rows, (x1, x2, x3), xc, xcb, r, i, sp, a, a2, sq = _rnn_common(
            xr, cw_ref, cb_ref, bda_ref, bdx_ref, ba_ref, bx_ref, lam_ref, s)
        (a_next,) = _shifted(a, (-1,), rows, s)
        g = _scan_up(a_next, dh_ref[...], rows, s, s1, s2)
        (hp,) = _shifted(h_ref[...], (1,), rows, s)
        da = g * hp
        dsq = g * (i * xc)
        di = g * (sq * xc)
        dxc = g * (sq * i)
        dlog = da * a - dsq * (a2 / sq)
        dr = dlog * (-RG_C * sp)
        dpr = dr * (r * (1.0 - r))
        dpi = di * (i * (1.0 - i))
        dprb = dpr.astype(BF16)
        dpib = dpi.astype(BF16)
        dxc = dxc + _dot_nt(dprb, bda_ref[...]) + _dot_nt(dpib, bdx_ref[...])

        up1, up2, up3 = _shifted(dxc, (-1, -2, -3), rows, s)
        dxr = cw_ref[3:4, :] * dxc + cw_ref[2:3, :] * up1 + cw_ref[1:2, :] * up2 + cw_ref[0:1, :] * up3
        dxr_ref[...] = dxr.astype(BF16)

        def colsum(v):
            return jnp.sum(v, axis=0, keepdims=True)

        pv_ref[0:1, :] += colsum(dxc * x3)
        pv_ref[1:2, :] += colsum(dxc * x2)
        pv_ref[2:3, :] += colsum(dxc * x1)
        pv_ref[3:4, :] += colsum(dxc * xr)
        pv_ref[4:5, :] += colsum(dxc)
        pv_ref[5:6, :] += colsum(dpr)
        pv_ref[6:7, :] += colsum(dpi)
        pv_ref[7:8, :] += colsum(dlog * r) * (RG_C * _sigmoid(-lam_ref[...]))
        dbd_ref[0] += _dot_tn(xcb, dprb)
        dbd_ref[1] += _dot_tn(xcb, dpib)

    blk, vec, mat = _rnn_specs(s)
    hblk = pl.BlockSpec((None, s, LANES), lambda cb, i: (i, 0, cb))
    return pl.pallas_call(
        body, name="rnn_bwd", grid=(N_CBLK, b),
        in_specs=[blk(N_CBLK), hblk, hblk, vec(CONV_W), vec(1), mat, mat, vec(1), vec(1), vec(1)],
        out_specs=[hblk, pl.BlockSpec((8, LANES), lambda cb, i: (0, cb)),
                   pl.BlockSpec((None, 2, LANES, LANES), lambda cb, i: (cb, 0, 0, 0))],
        out_shape=[jax.ShapeDtypeStruct((b, s, D_MODEL), BF16), jax.ShapeDtypeStruct((8, D_MODEL), F32),
                   jax.ShapeDtypeStruct((N_CBLK, 2, LANES, LANES), F32)],
        scratch_shapes=[pltpu.VMEM((s, LANES), F32), pltpu.VMEM((s, LANES), F32)],
        compiler_params=_cparams(("parallel", "arbitrary")),
    )(zrest3, h3, dh3, conv_w, conv_b, bda, bdx, ba, bx, lam)


def _branch_merge(ga, gr, wa, wr, zrest):
    t = ga.shape[0]
    tm = min(512, t)
    tn = 512

    def body(ga_ref, gr_ref, wa_ref, wr_ref, mga_ref, mgr_ref, ya_ref, yr_ref, m_ref):
        ya = _dot(ga_ref[...], wa_ref[...])
        yr = _dot(gr_ref[...], wr_ref[...])
        ya_ref[...] = ya.astype(BF16)
        yr_ref[...] = yr.astype(BF16)
        m_ref[...] = (_sigmoid(mga_ref[...].astype(F32)) * ya + _sigmoid(mgr_ref[...].astype(F32)) * yr).astype(BF16)

    nj = D_MODEL // tn
    act = pl.BlockSpec((tm, D_MODEL), lambda i, j: (i, 0))
    wgt = pl.BlockSpec((D_MODEL, tn), lambda i, j: (0, j))
    out = pl.BlockSpec((tm, tn), lambda i, j: (i, j))
    return pl.pallas_call(
        body, name="branch_merge", grid=(t // tm, nj),
        in_specs=[act, act, wgt, wgt, pl.BlockSpec((tm, tn), lambda i, j: (i, 3 * nj + j)),
                  pl.BlockSpec((tm, tn), lambda i, j: (i, 4 * nj + j))],
        out_specs=[out, out, out],
        out_shape=[jax.ShapeDtypeStruct((t, D_MODEL), BF16), jax.ShapeDtypeStruct((t, D_MODEL), BF16),
                   jax.ShapeDtypeStruct((t, D_MODEL), BF16)],
        compiler_params=_cparams(("parallel", "parallel")),
    )(ga, gr, wa, wr, zrest, zrest)


def _out_loss(m, wout, x2, tgt2, wpost):
    t = m.shape[0]
    tm = min(256, t)

    def body(m_ref, w_ref, x_ref, t_ref, wp_ref, dy_ref, do_ref, acc_ref):
        @pl.when(pl.program_id(0) == 0)
        def _():
            acc_ref[...] = jnp.zeros_like(acc_ref)

        o = _dot(m_ref[...], w_ref[...])
        r2 = lax.rsqrt(jnp.mean(o * o, axis=-1, keepdims=True) + NORM_EPS)
        n = o * r2
        wp = wp_ref[...]
        err = (x_ref[...] + n * wp) - t_ref[...]
        dy = err * (1.0 / D_MODEL)
        dn = dy * wp
        do = r2 * (dn - n * jnp.mean(dn * n, axis=-1, keepdims=True))
        dy_ref[...] = dy
        do_ref[...] = do.astype(BF16)
        acc_ref[0:1, :] += jnp.sum(dy * n, axis=0, keepdims=True)
        acc_ref[1:2, :] += jnp.sum(err * err, axis=0, keepdims=True)

    row = pl.BlockSpec((tm, D_MODEL), lambda i: (i, 0))
    return pl.pallas_call(
        body, name="out_loss", grid=(t // tm,),
        in_specs=[row, pl.BlockSpec((D_MODEL, D_MODEL), lambda i: (0, 0)), row, row,
                  pl.BlockSpec((1, D_MODEL), lambda i: (0, 0))],
        out_specs=[row, row, pl.BlockSpec((8, D_MODEL), lambda i: (0, 0))],
        out_shape=[jax.ShapeDtypeStruct((t, D_MODEL), F32), jax.ShapeDtypeStruct((t, D_MODEL), BF16),
                   jax.ShapeDtypeStruct((8, D_MODEL), F32)],
        compiler_params=_cparams(("arbitrary",)),
    )(m, wout, x2, tgt2, wpost)


def _merge_bwd(do, wout, zrest, ya, yr):
    t = do.shape[0]
    tm = min(512, t)
    tn = 512
    nj = D_MODEL // tn

    def body(do_ref, w_ref, mga_ref, mgr_ref, ya_ref, yr_ref, dya_ref, dyr_ref, dmga_ref, dmgr_ref):
        dm = _dot_nt(do_ref[...], w_ref[...])
        sa = _sigmoid(mga_ref[...].astype(F32))
        sr = _sigmoid(mgr_ref[...].astype(F32))
        dya_ref[...] = (dm * sa).astype(BF16)
        dyr_ref[...] = (dm * sr).astype(BF16)
        dmga_ref[...] = (dm * ya_ref[...].astype(F32) * (sa * (1.0 - sa))).astype(BF16)
        dmgr_ref[...] = (dm * yr_ref[...].astype(F32) * (sr * (1.0 - sr))).astype(BF16)

    out = pl.BlockSpec((tm, tn), lambda i, j: (i, j))
    bf = jax.ShapeDtypeStruct((t, D_MODEL), BF16)
    return pl.pallas_call(
        body, name="merge_bwd", grid=(t // tm, nj),
        in_specs=[pl.BlockSpec((tm, D_MODEL), lambda i, j: (i, 0)), pl.BlockSpec((tn, D_MODEL), lambda i, j: (j, 0)),
                  pl.BlockSpec((tm, tn), lambda i, j: (i, 3 * nj + j)),
                  pl.BlockSpec((tm, tn), lambda i, j: (i, 4 * nj + j)), out, out],
        out_specs=[out, out, out, out],
        out_shape=[bf, bf, bf, bf],
        compiler_params=_cparams(("parallel", "parallel")),
    )(do, wout, zrest, zrest, ya, yr)


def _branch_bwd(dya, dyr, wa, wr, zrest, yatt, ylru):
    t = dya.shape[0]
    tm = min(512, t)
    tn = 512
    nj = D_MODEL // tn

    def body(dya_ref, dyr_ref, wa_ref, wr_ref, ga_ref, gr_ref, ya_ref, yl_ref,
             dyatt_ref, dga_ref, dyl_ref, dgr_ref):
        dga = _dot_nt(dya_ref[...], wa_ref[...])
        dgr = _dot_nt(dyr_ref[...], wr_ref[...])
        g = ga_ref[...].astype(F32)
        sg = _sigmoid(g)
        dyatt_ref[...] = (dga * (g * sg)).astype(BF16)
        dga_ref[...] = (dga * ya_ref[...] * (sg * (1.0 + g * (1.0 - sg)))).astype(BF16)
        g = gr_ref[...].astype(F32)
        sg = _sigmoid(g)
        dyl_ref[...] = dgr * (g * sg)
        dgr_ref[...] = (dgr * yl_ref[...] * (sg * (1.0 + g * (1.0 - sg)))).astype(BF16)

    act = pl.BlockSpec((tm, D_MODEL), lambda i, j: (i, 0))
    wgt = pl.BlockSpec((tn, D_MODEL), lambda i, j: (j, 0))
    out = pl.BlockSpec((tm, tn), lambda i, j: (i, j))
    bf = jax.ShapeDtypeStruct((t, D_MODEL), BF16)
    return pl.pallas_call(
        body, name="branch_bwd", grid=(t // tm, nj),
        in_specs=[act, act, wgt, wgt, pl.BlockSpec((tm, tn), lambda i, j: (i, j)),
                  pl.BlockSpec((tm, tn), lambda i, j: (i, 2 * nj + j)), out, out],
        out_specs=[out, out, out, out],
        out_shape=[bf, bf, jax.ShapeDtypeStruct((t, D_MODEL), F32), bf],
        compiler_params=_cparams(("parallel", "parallel")),
    )(dya, dyr, wa, wr, zrest, zrest, yatt, ylru)


def _dh_partial(parts, after, name):
    t = parts[0][0].shape[0]
    tm = min(256, t)
    np_ = len(parts)

    def body(*refs):
        o_ref = refs[-1]
        acc = _dot(refs[0][...], refs[np_][...])
        for p in range(1, np_):
            acc = acc + _dot(refs[p][...], refs[np_ + p][...])
        o_ref[...] = acc

    in_specs = [pl.BlockSpec((tm, dz.shape[1]), lambda i: (i, 0)) for dz, _ in parts]
    in_specs += [pl.BlockSpec(w.shape, lambda i: (0, 0)) for _, w in parts]
    in_specs += [pl.BlockSpec(after.shape, lambda i: (0, 0))]
    return pl.pallas_call(
        body, name=name, grid=(t // tm,),
        in_specs=in_specs,
        out_specs=pl.BlockSpec((tm, D_MODEL), lambda i: (i, 0)),
        out_shape=jax.ShapeDtypeStruct((t, D_MODEL), F32),
        compiler_params=_cparams(("parallel",), vmem_mb=48),
    )(*[dz for dz, _ in parts], *[w for _, w in parts], after)


def _dh_final(parts, acc_in, x2, dy, wpre):
    t = x2.shape[0]
    tm = min(256, t)
    np_ = len(parts)

    def body(*refs):
        acc_ref, x_ref, dy_ref, w_ref = refs[2 * np_:2 * np_ + 4]
        gx_ref, pw_ref = refs[2 * np_ + 4:]

        @pl.when(pl.program_id(0) == 0)
        def _():
            pw_ref[...] = jnp.zeros_like(pw_ref)

        dh = acc_ref[...]
        for p in range(np_):
            dh = dh + _dot(refs[p][...], refs[np_ + p][...])
        x = x_ref[...]
        r = lax.rsqrt(jnp.mean(x * x, axis=-1, keepdims=True) + NORM_EPS)
        xn = x * r
        dxn = dh * w_ref[...]
        gx_ref[...] = r * (dxn - xn * jnp.mean(dxn * xn, axis=-1, keepdims=True)) + dy_ref[...]
        pw_ref[0:1, :] += jnp.sum(dh * xn, axis=0, keepdims=True)

    row = pl.BlockSpec((tm, D_MODEL), lambda i: (i, 0))
    in_specs = [pl.BlockSpec((tm, dz.shape[1]), lambda i: (i, 0)) for dz, _ in parts]
    in_specs += [pl.BlockSpec(w.shape, lambda i: (0, 0)) for _, w in parts]
    in_specs += [row, row, row, pl.BlockSpec((1, D_MODEL), lambda i: (0, 0))]
    return pl.pallas_call(
        body, name="dh_final", grid=(t // tm,),
        in_specs=in_specs,
        out_specs=[row, pl.BlockSpec((8, D_MODEL), lambda i: (0, 0))],
        out_shape=[jax.ShapeDtypeStruct((t, D_MODEL), F32), jax.ShapeDtypeStruct((8, D_MODEL), F32)],
        compiler_params=_cparams(("arbitrary",), vmem_mb=48),
    )(*[dz for dz, _ in parts], *[w for _, w in parts], acc_in, x2, dy, wpre)


def _adamw(w, g, m, v):
    m = ADAM_B1 * m + (1.0 - ADAM_B1) * g
    v = ADAM_B2 * v + (1.0 - ADAM_B2) * (g * g)
    m_hat = m / (1.0 - ADAM_B1 ** ADAM_STEP)
    v_hat = v / (1.0 - ADAM_B2 ** ADAM_STEP)
    delta = -ADAM_LR * (m_hat / (jnp.sqrt(v_hat) + ADAM_EPS) + ADAM_WD * w)
    return delta, m, v


def _reduce_adamw(own, parts, place, w, m, v, name):
    r, c = w.shape
    blk, nblk, at = _blocks_2d(r, c)

    def body(place_ref, own_ref, p_ref, w_ref, m_ref, v_ref, g_ref, d_ref, nm_ref, nv_ref):
        mine = place_ref[1]
        own_blk = own_ref[...]
        g = jnp.where(mine == 0, own_blk, p_ref[0].astype(F32))
        for j in range(1, N_CHIPS):
            g = g + jnp.where(mine == j, own_blk, p_ref[j].astype(F32))
        d, nm, nv = _adamw(w_ref[...], g, m_ref[...], v_ref[...])
        g_ref[...] = g
        d_ref[...] = d
        nm_ref[...] = nm
        nv_ref[...] = nv

    row = pl.BlockSpec(blk, lambda i, pr: at(i))
    sh = jax.ShapeDtypeStruct((r, c), F32)
    grid_spec = pltpu.PrefetchScalarGridSpec(
        num_scalar_prefetch=1, grid=(nblk,),
        in_specs=[row, pl.BlockSpec((N_CHIPS,) + blk, lambda i, pr: (0,) + at(i)), row, row, row],
        out_specs=[row, row, row, row])
    return pl.pallas_call(
        body, name=name, grid_spec=grid_spec, out_shape=[sh, sh, sh, sh],
        compiler_params=_cparams(("parallel",)),
    )(place, own, parts, w, m, v)


def _interleave_qkv(a):
    lead = a.shape[:-1]
    return a.reshape(lead + (3, HEAD_PAIRS, LANES)).swapaxes(-3, -2).reshape(lead + (3 * D_MODEL,))


def _deinterleave_qkv(a):
    lead = a.shape[:-1]
    return a.reshape(lead + (HEAD_PAIRS, 3, LANES)).swapaxes(-3, -2).reshape(lead + (3 * D_MODEL,))


def _interleave_rows(a):
    return a.reshape(3, HEAD_PAIRS, LANES, a.shape[1]).swapaxes(0, 1).reshape(a.shape)


def _deinterleave_rows(a):
    return a.reshape(HEAD_PAIRS, 3, LANES, a.shape[1]).swapaxes(0, 1).reshape(a.shape)


def _pack_small(pre, conv_b, rg_ba, rg_bx, lam, post, loss_row, b_in, conv_w_full, rg_wa, rg_wx):
    z = jnp.zeros((1, D_MODEL), F32)
    b_used = jnp.concatenate([b_in[:, 0:3 * D_MODEL], b_in[:, 3 * D_MODEL + HEADS:IN_TOTAL]], axis=1)
    b_f = jnp.pad(b_in[:, 3 * D_MODEL:3 * D_MODEL + HEADS], ((0, 0), (0, D_MODEL - HEADS)))
    return jnp.concatenate([
        pre, conv_b, rg_ba, rg_bx, lam, post, loss_row, z,
        b_used.reshape(9, D_MODEL), b_f, conv_w_full, z, z,
        rg_wa.reshape(64, D_MODEL), rg_wx.reshape(64, D_MODEL)], axis=0)


def _unpack_small(p):
    b_used = p[8:17].reshape(1, 9 * D_MODEL)
    b_in = jnp.concatenate([b_used[:, 0:3 * D_MODEL], p[17:18, 0:HEADS], b_used[:, 3 * D_MODEL:]], axis=1)
    return dict(pre_norm_w=p[0:1], conv_b=p[1:2], rg_ba=p[2:3], rg_bx=p[3:4], rg_lambda=p[4:5],
                post_norm_w=p[5:6], loss_row=p[6:7], b_in=b_in, conv_w_full=p[18:22],
                rg_wa=p[24:88].reshape(1, 16, 64, 64), rg_wx=p[88:152].reshape(1, 16, 64, 64))


def _reduce_small(parts, w, m, v):
    def body(p_ref, w_ref, m_ref, v_ref, g_ref, d_ref, nm_ref, nv_ref):
        g = p_ref[0]
        for j in range(1, N_DEV):
            g = g + p_ref[j]
        d, nm, nv = _adamw(w_ref[...], g, m_ref[...], v_ref[...])
        g_ref[...] = g
        d_ref[...] = d
        nm_ref[...] = nm
        nv_ref[...] = nv

    sh = jax.ShapeDtypeStruct((SMALL_ROWS, D_MODEL), F32)
    return pl.pallas_call(body, name="reduce_small", out_shape=[sh, sh, sh, sh])(parts, w, m, v)


def kernel(x, pre_norm_w, w_in, b_in, conv_w, conv_b, rg_wa, rg_ba, rg_wx, rg_bx, rg_lambda, w_branch_a, w_branch_r, w_out, post_norm_w, loss_target, m_pre_norm_w, m_w_in, m_b_in, m_conv_w, m_conv_b, m_rg_wa, m_rg_ba, m_rg_wx, m_rg_bx, m_rg_lambda, m_w_branch_a, m_w_branch_r, m_w_out, m_post_norm_w, v_pre_norm_w, v_w_in, v_b_in, v_conv_w, v_conv_b, v_rg_wa, v_rg_ba, v_rg_wx, v_rg_bx, v_rg_lambda, v_w_branch_a, v_w_branch_r, v_w_out, v_post_norm_w):
    b, s, _ = x.shape
    t = b * s
    me = 4 * lax.axis_index("x") + 2 * lax.axis_index("y") + lax.axis_index("c")
    shard_rows = D_MODEL // N_DEV

    place = jnp.stack([lax.axis_index("c"), 2 * lax.axis_index("x") + lax.axis_index("y")]).astype(jnp.int32)
    w_in_all = _gather(w_in[0].T.astype(BF16), "gather_w_in")
    wt_full = w_in_all.reshape(IN_TOTAL, D_MODEL)
    conv_terms = jnp.concatenate(_split3(conv_w[0]), axis=0)
    conv_pad = jnp.pad(conv_terms, ((0, 16 - 3 * CONV_W), (0, D_MODEL - LANES)))
    sq_stack = jnp.concatenate([w_branch_a[0].astype(BF16), w_branch_r[0].astype(BF16), w_out[0].astype(BF16),
                                conv_pad], axis=0)
    sq_sems, sq_src, sq_land, sq_token = _gather_start(sq_stack, w_in_all, "gather_w_sq_start")

    w_qkv = _interleave_rows(wt_full[0:3 * D_MODEL])
    w_f = jnp.pad(wt_full[3 * D_MODEL:3 * D_MODEL + HEADS], ((0, LANES - HEADS), (0, 0)))
    w_rest = wt_full[3 * D_MODEL + HEADS:IN_USED]
    b_qkv = _interleave_qkv(b_in[:, 0:3 * D_MODEL]) + sq_token[0, 0]
    b_f = jnp.pad(b_in[:, 3 * D_MODEL:3 * D_MODEL + HEADS], ((0, 0), (0, LANES - HEADS)))
    b_rest = b_in[:, 3 * D_MODEL + HEADS:IN_USED]

    def blockdiag(w):
        w2 = w.reshape(N_CBLK, 2, HEAD_DIM, HEAD_DIM)
        zz = jnp.zeros((N_CBLK, HEAD_DIM, HEAD_DIM), w.dtype)
        top = jnp.concatenate([w2[:, 0], zz], axis=2)
        bot = jnp.concatenate([zz, w2[:, 1]], axis=2)
        return jnp.concatenate([top, bot], axis=1).astype(BF16)

    bda, bdx = blockdiag(rg_wa[0]), blockdiag(rg_wx[0])

    x2 = x.reshape(t, D_MODEL)
    tgt2 = loss_target.reshape(t, D_MODEL)
    h = _prenorm(x2, pre_norm_w)
    qkv = _mm_bias(h, w_qkv, b_qkv, BF16, "inproj_qkv")
    zrest = _mm_bias(h, w_rest, b_rest, BF16, "inproj_rest")
    zf = _mm_bias(h, w_f, b_f, F32, "inproj_f")
    qkv3 = qkv.reshape(b, s, 3 * D_MODEL)
    zrest3 = zrest.reshape(b, s, 5 * D_MODEL)
    zf3 = zf.reshape(b, s, LANES)
    nq = s // ATT_TILE
    cexp3, crow = _fgate_fwd(zf3)
    crow5 = crow.reshape(b, HEAD_PAIRS, 2, nq, ATT_TILE)
    yatt3, lse5, ga3 = _attn_fwd(qkv3, cexp3, crow5, zrest3)

    sq_all = _gather_wait(sq_sems, sq_src, sq_land, ga3, "gather_w_sq_wait")
    sq_all = lax.dynamic_update_slice(sq_all, sq_stack[None], (me, 0, 0))
    wa = sq_all[:, 0:shard_rows].reshape(D_MODEL, D_MODEL)
    wr = sq_all[:, shard_rows:2 * shard_rows].reshape(D_MODEL, D_MODEL)
    wo = sq_all[:, 2 * shard_rows:3 * shard_rows].reshape(D_MODEL, D_MODEL)
    conv_all = sq_all[:, 3 * shard_rows:3 * shard_rows + 3 * CONV_W, 0:LANES].astype(F32)
    conv_all = (conv_all[:, 0:CONV_W] + conv_all[:, CONV_W:2 * CONV_W]) + conv_all[:, 2 * CONV_W:3 * CONV_W]
    conv_full = conv_all.transpose(1, 0, 2).reshape(CONV_W, D_MODEL)

    ylru3, gr3 = _rnn_fwd(zrest3, conv_full, conv_b, bda, bdx, rg_ba, rg_bx, rg_lambda)
    ga, gr = ga3.reshape(t, D_MODEL), gr3.reshape(t, D_MODEL)
    ya, yr, mm = _branch_merge(ga, gr, wa, wr, zrest)
    dy, do, acc_out = _out_loss(mm, wo, x2, tgt2, post_norm_w)

    dya, dyr, dz_mga, dz_mgr = _merge_bwd(do, wo, zrest, ya, yr)
    dyatt, dz_ga, dylru, dz_gr = _branch_bwd(dya, dyr, wa, wr, zrest, yatt3.reshape(t, D_MODEL),
                                             ylru3.reshape(t, D_MODEL))
    dz_xr3, pvec, dbd = _rnn_bwd(zrest3, ylru3, dylru.reshape(b, s, D_MODEL), conv_full, conv_b, bda, bdx,
                                 rg_ba, rg_bx, rg_lambda)
    dqkv3, dc3 = _attn_bwd(qkv3, dyatt.reshape(b, s, D_MODEL), yatt3, lse5, crow5, cexp3)
    dz_f = _fgate_bwd(dc3, zf3).reshape(t, LANES)
    dz_qkv = dqkv3.reshape(t, 3 * D_MODEL)
    dz_xr = dz_xr3.reshape(t, D_MODEL)

    dw_qkv, db_qkv = _mm_tn(dz_qkv, h, "dw_qkv")
    dw_f, db_f = _mm_tn(dz_f, h, "dw_f")
    dw_parts, db_parts = [], []
    for nm, dzp in (("ga", dz_ga), ("xr", dz_xr), ("gr", dz_gr), ("mga", dz_mga), ("mgr", dz_mgr)):
        dwp, dbp = _mm_tn(dzp, h, "dw_" + nm)
        dw_parts.append(dwp)
        db_parts.append(dbp[0:1])
    dw_a, _ = _mm_tn(ga, dya, "dw_a")
    dw_r, _ = _mm_tn(gr, dyr, "dw_r")
    dw_o, _ = _mm_tn(mm, do, "dw_o")

    zeros_tail = jnp.zeros((IN_TOTAL - IN_USED, D_MODEL), F32)
    dwt_full = jnp.concatenate([_deinterleave_rows(dw_qkv), dw_f[0:HEADS]] + dw_parts + [zeros_tail], axis=0)
    dw_in_send = dwt_full.reshape(N_CHIPS, 2, W_SHARD, D_MODEL).transpose(1, 0, 2, 3)
    by_dest = lambda a: a.reshape(N_CHIPS, 2, shard_rows, D_MODEL).transpose(1, 0, 2, 3)
    dw_sq_send = jnp.concatenate([by_dest(dw_a), by_dest(dw_r), by_dest(dw_o)], axis=2)

    sib_in, sib_sq = _swap_with_sibling([dw_in_send, dw_sq_send], "swap_dw")
    chip_in, own_in = _pair_add(dw_in_send, sib_in, place, "pair_add_in")
    chip_sq, own_sq = _pair_add(dw_sq_send, sib_sq, place, "pair_add_sq")
    sems, sent, lands, token = _exchange_chips_start([chip_in, chip_sq], "exchange_dw_start")

    wt = lambda lo: w_rest[lo * D_MODEL:(lo + 1) * D_MODEL]
    dh_a = _dh_partial([(dz_qkv, w_qkv), (dz_f, w_f)], token, "dh_qkv")
    grad_x2, acc_pre = _dh_final(
        [(dz_ga, wt(0)), (dz_xr, wt(1)), (dz_gr, wt(2)), (dz_mga, wt(3)), (dz_mgr, wt(4))],
        dh_a, x2, dy, pre_norm_w)

    db_in_full = jnp.concatenate([_deinterleave_qkv(db_qkv[0:1]), db_f[0:1, 0:HEADS]] + db_parts
                                 + [jnp.zeros((1, IN_TOTAL - IN_USED), F32)], axis=1)
    d_rg_wa = jnp.stack([dbd[:, 0, 0:HEAD_DIM, 0:HEAD_DIM], dbd[:, 0, HEAD_DIM:, HEAD_DIM:]], axis=1)
    d_rg_wx = jnp.stack([dbd[:, 1, 0:HEAD_DIM, 0:HEAD_DIM], dbd[:, 1, HEAD_DIM:, HEAD_DIM:]], axis=1)
    small_g = _pack_small(acc_pre[0:1], pvec[4:5], pvec[5:6], pvec[6:7], pvec[7:8], acc_out[0:1], acc_out[1:2],
                          db_in_full, pvec[0:4], d_rg_wa, d_rg_wx)
    sm_sems, sm_src, sm_land, sm_token = _gather_start(small_g, grad_x2, "gather_small_start")
    recv_in, recv_sq = _exchange_chips_wait(sems, sent, lands, sm_token, "exchange_dw_wait")

    g_in, d_in, nm_in, nv_in = [a.T for a in _reduce_adamw(
        own_in, recv_in, place, w_in[0].T, m_w_in[0].T, v_w_in[0].T, "adamw_w_in")]
    sq_w = jnp.concatenate([w_branch_a[0], w_branch_r[0], w_out[0]], axis=0)
    sq_m = jnp.concatenate([m_w_branch_a[0], m_w_branch_r[0], m_w_out[0]], axis=0)
    sq_v = jnp.concatenate([v_w_branch_a[0], v_w_branch_r[0], v_w_out[0]], axis=0)
    g_sq, d_sq, nm_sq, nv_sq = _reduce_adamw(own_sq, recv_sq, place, sq_w, sq_m, sq_v, "adamw_w_sq")
    small_all = _gather_wait(sm_sems, sm_src, sm_land, d_sq, "gather_small_wait")
    small_all = lax.dynamic_update_slice(small_all, small_g[None], (me, 0, 0))

    def place_conv(a):
        return lax.dynamic_update_slice(jnp.zeros((CONV_W, D_MODEL), F32), a[0], (0, me * LANES))

    zrow = jnp.zeros((1, D_MODEL), F32)
    small_w = _pack_small(pre_norm_w, conv_b, rg_ba, rg_bx, rg_lambda, post_norm_w, zrow, b_in,
                          place_conv(conv_w), rg_wa[0], rg_wx[0])
    small_m = _pack_small(m_pre_norm_w, m_conv_b, m_rg_ba, m_rg_bx, m_rg_lambda, m_post_norm_w, zrow, m_b_in,
                          place_conv(m_conv_w), m_rg_wa[0], m_rg_wx[0])
    small_v = _pack_small(v_pre_norm_w, v_conv_b, v_rg_ba, v_rg_bx, v_rg_lambda, v_post_norm_w, zrow, v_b_in,
                          place_conv(v_conv_w), v_rg_wa[0], v_rg_wx[0])
    outs_small = [_unpack_small(p) for p in _reduce_small(small_all, small_w, small_m, small_v)]

    loss = (0.5 / D_MODEL) * jnp.sum(outs_small[0]["loss_row"])

    def leaf(kind, name):
        if name == "w_in":
            return (g_in, d_in, nm_in, nv_in)[kind][None]
        if name in ("w_branch_a", "w_branch_r", "w_out"):
            j = ("w_branch_a", "w_branch_r", "w_out").index(name)
            return (g_sq, d_sq, nm_sq, nv_sq)[kind][None, j * shard_rows:(j + 1) * shard_rows]
        if name == "conv_w":
            return lax.dynamic_slice(outs_small[kind]["conv_w_full"], (0, me * LANES), (CONV_W, LANES))[None]
        return outs_small[kind][name]

    names = ["pre_norm_w", "w_in", "b_in", "conv_w", "conv_b", "rg_wa", "rg_ba", "rg_wx", "rg_bx", "rg_lambda",
             "w_branch_a", "w_branch_r", "w_out", "post_norm_w"]
    out = [loss, grad_x2.reshape(b, s, D_MODEL)]
    for kind in range(4):
        out += [leaf(kind, nm) for nm in names]
    return tuple(out)
```

```python
import jax
import jax.numpy as jnp
from jax import lax
from jax.experimental import pallas as pl
from jax.experimental.pallas import tpu as pltpu

F32 = jnp.float32
BF16 = jnp.bfloat16

N_DEV = 8
D_MODEL = 1024
HEADS = 16
HEAD_DIM = 64
HEAD_PAIRS = HEADS // 2
LANES = 128
N_CBLK = D_MODEL // LANES
CONV_W = 4
RG_C = 8.0
NORM_EPS = 1e-6
MASK_VALUE = -1e30
IN_USED = 8208
IN_TOTAL = 9232
W_SHARD = IN_TOTAL // N_DEV

ADAM_LR = 0.001
ADAM_B1 = 0.9
ADAM_B2 = 0.999
ADAM_EPS = 1e-08
ADAM_WD = 0.01
ADAM_STEP = 10

ATT_TILE_FWD = 256
ATT_TILE_BWD = 512
SCAN_TILE = 256
SMALL_ROWS = 152


def _cparams(sem=None, vmem_mb=None):
    kw = {}
    if sem is not None:
        kw["dimension_semantics"] = sem
    if vmem_mb is not None:
        kw["vmem_limit_bytes"] = vmem_mb * 1024 * 1024
    return pltpu.CompilerParams(**kw)


def _sigmoid(x):
    return 1.0 / (1.0 + jnp.exp(-x))


def _softplus(x):
    return jnp.maximum(x, 0.0) + jnp.log1p(jnp.exp(-jnp.abs(x)))


def _one_minus_exp(y, exp_y):
    series = -y * (1.0 + y * (1.0 / 2 + y * (1.0 / 6 + y * (1.0 / 24 + y * (1.0 / 120)))))
    return jnp.where(y > -0.0625, series, 1.0 - exp_y)


def _split3(x):
    hi = x.astype(BF16)
    r1 = x - hi.astype(F32)
    mid = r1.astype(BF16)
    lo = (r1 - mid.astype(F32)).astype(BF16)
    return hi, mid, lo


def _dot(a, b):
    return jnp.dot(a, b, preferred_element_type=F32)


def _dot_nt(a, b):
    return lax.dot_general(a, b, (((1,), (1,)), ((), ())), preferred_element_type=F32)


def _dot_tn(a, b):
    return lax.dot_general(a, b, (((0,), (0,)), ((), ())), preferred_element_type=F32)


def _iota(shape, dim):
    return lax.broadcasted_iota(jnp.int32, shape, dim)


_ANY = pl.BlockSpec(memory_space=pl.ANY)
_MESH = pl.DeviceIdType.MESH
N_CHIPS = 4


def _place():
    x, y, c = lax.axis_index("x"), lax.axis_index("y"), lax.axis_index("c")
    other_chips = [(1 - x, y), (x, 1 - y), (1 - x, 1 - y)]
    return x, y, c, other_chips


def _gather(x_shard, name):
    def body(x_ref, out_ref, send_sems, recv_sems, local_sem):
        x, y, c, chips = _place()
        me, sibling = (x, y, c), (x, y, 1 - c)

        def slot(p):
            return out_ref.at[4 * p[0] + 2 * p[1] + p[2]]

        def copy(k, block, to, src=None):
            return pltpu.make_async_remote_copy(
                src_ref=slot(block) if src is None else src, dst_ref=slot(block),
                send_sem=send_sems.at[k], recv_sem=recv_sems.at[k], device_id=to, device_id_type=_MESH)

        mine = pltpu.make_async_copy(x_ref, slot(me), local_sem)
        mine.start()
        first = [copy(0, me, sibling, src=x_ref)]
        first += [copy(1 + j, me, (*chip, c), src=x_ref) for j, chip in enumerate(chips)]
        for cp in first:
            cp.start()
        passed = [copy(4 + j, (*chip, c), sibling) for j, chip in enumerate(chips)]
        for j, chip in enumerate(chips):
            copy(1 + j, (*chip, c), me).wait_recv()
            passed[j].start()
        copy(0, sibling, me).wait_recv()
        for j, chip in enumerate(chips):
            copy(4 + j, (*chip, 1 - c), me).wait_recv()
        for cp in first + passed:
            cp.wait_send()
        mine.wait()

    return pl.pallas_call(
        body, name=name,
        out_shape=jax.ShapeDtypeStruct((N_DEV,) + tuple(x_shard.shape), x_shard.dtype),
        in_specs=[_ANY], out_specs=_ANY,
        scratch_shapes=[pltpu.SemaphoreType.DMA((7,)), pltpu.SemaphoreType.DMA((7,)), pltpu.SemaphoreType.DMA],
    )(x_shard)


def _swap_with_sibling(srcs, name):
    n = len(srcs)

    def body(*refs):
        src_refs, out_refs = refs[:n], refs[n:2 * n]
        send_sems, recv_sems = refs[2 * n:]
        x, y, c, _ = _place()
        cps = [pltpu.make_async_remote_copy(
            src_ref=src_refs[i].at[1 - c], dst_ref=out_refs[i], send_sem=send_sems.at[i], recv_sem=recv_sems.at[i],
            device_id=(x, y, 1 - c), device_id_type=_MESH) for i in range(n)]
        for cp in cps:
            cp.start()
        for cp in cps:
            cp.wait()

    return pl.pallas_call(
        body, name=name,
        out_shape=[jax.ShapeDtypeStruct(a.shape[1:], a.dtype) for a in srcs],
        in_specs=[_ANY] * n, out_specs=[_ANY] * n,
        scratch_shapes=[pltpu.SemaphoreType.DMA((n,)), pltpu.SemaphoreType.DMA((n,))],
    )(*srcs)


def _blocks_2d(r, c):
    if r % 128 == 0:
        return (128, c), r // 128, lambda i: (i, 0)
    return (r, 256), c // 256, lambda i: (0, i)


def _pair_add(src, recv, place, name):
    _, _, r, c = src.shape
    blk, nblk, at = _blocks_2d(r, c)

    def body(place_ref, a_ref, b_ref, q16_ref, own_ref):
        q = a_ref[...] + b_ref[...]
        q16_ref[...] = q.astype(BF16)

        @pl.when(pl.program_id(1) == place_ref[1])
        def _():
            own_ref[...] = q

    grid_spec = pltpu.PrefetchScalarGridSpec(
        num_scalar_prefetch=1, grid=(nblk, N_CHIPS),
        in_specs=[pl.BlockSpec((None, None) + blk, lambda i, j, pr: (pr[0], j) + at(i)),
                  pl.BlockSpec((None,) + blk, lambda i, j, pr: (j,) + at(i))],
        out_specs=[pl.BlockSpec((None,) + blk, lambda i, j, pr: (j,) + at(i)),
                   pl.BlockSpec(blk, lambda i, j, pr: at(i))])
    return pl.pallas_call(
        body, name=name, grid_spec=grid_spec,
        out_shape=[jax.ShapeDtypeStruct((N_CHIPS, r, c), BF16), jax.ShapeDtypeStruct((r, c), F32)],
        compiler_params=_cparams(("parallel", "arbitrary")),
    )(place, src, recv)


_HBM = pl.BlockSpec(memory_space=pltpu.HBM)
_SEM = pl.BlockSpec(memory_space=pltpu.SEMAPHORE)
_DATAFLOW = pltpu.SideEffectType.DATAFLOW_SIDE_EFFECTING


def _chip_copy(src_ref, land_ref, send_sem, recv_sem, k, chips, c, land):
    chip = chips[k]
    return pltpu.make_async_remote_copy(
        src_ref=src_ref.at[2 * chip[0] + chip[1]], dst_ref=land_ref.at[land],
        send_sem=send_sem, recv_sem=recv_sem, device_id=(*chip, c), device_id_type=_MESH)


def _exchange_chips_start(srcs, name):
    n = len(srcs)
    ncp = 3 * n

    def body(*refs):
        src_refs, land_refs = refs[:n], refs[n:2 * n]
        sems = refs[4 * n:4 * n + 2 * ncp]
        token = refs[-1]
        x, y, c, chips = _place()
        for i in range(n):
            for k in range(3):
                j = 3 * i + k
                _chip_copy(src_refs[i], land_refs[i], sems[j], sems[ncp + j], k, chips, c, 2 * x + y).start()
        token[...] = jnp.zeros_like(token)

    hbm = [pltpu.HBM(a.shape, a.dtype) for a in srcs]
    lands = [pltpu.with_memory_space_constraint(lax.empty(a.shape, a.dtype), pltpu.HBM) for a in srcs]
    res = pl.pallas_call(
        body, name=name,
        out_shape=(*hbm, *hbm, *([pltpu.SemaphoreType.DMA(())] * (2 * ncp)), jax.ShapeDtypeStruct((8, LANES), F32)),
        in_specs=[_HBM] * (2 * n),
        out_specs=(*([_HBM] * (2 * n)), *([_SEM] * (2 * ncp)), pl.BlockSpec(memory_space=pltpu.VMEM)),
        input_output_aliases={i: i for i in range(2 * n)},
        compiler_params=pltpu.CompilerParams(has_side_effects=_DATAFLOW),
    )(*[pltpu.with_memory_space_constraint(a, pltpu.HBM) for a in srcs], *lands)
    return list(res[2 * n:2 * n + 2 * ncp]), list(res[:n]), list(res[n:2 * n]), res[-1]


def _exchange_chips_wait(sems, srcs, lands, after, name):
    n = len(srcs)
    ncp = 3 * n

    def body(*refs):
        src_refs, land_refs = refs[:n], refs[n:2 * n]
        sem_refs = refs[2 * n:2 * n + 2 * ncp]
        x, y, c, chips = _place()
        for i in range(n):
            for k in range(3):
                j = 3 * i + k
                cp = _chip_copy(src_refs[i], land_refs[i], sem_refs[j], sem_refs[ncp + j], k, chips, c,
                                2 * chips[k][0] + chips[k][1])
                cp.wait_send()
                cp.wait_recv()

    hbm = [pltpu.HBM(a.shape, a.dtype) for a in srcs]
    res = pl.pallas_call(
        body, name=name, out_shape=(*hbm, *hbm),
        in_specs=[_HBM] * (2 * n) + [_SEM] * (2 * ncp) + [_ANY], out_specs=tuple([_HBM] * (2 * n)),
        input_output_aliases={i: i for i in range(2 * n)},
        compiler_params=pltpu.CompilerParams(has_side_effects=_DATAFLOW),
    )(*srcs, *lands, *sems, after)
    return list(res[n:2 * n])


def _peer_copy(src_ref, land_ref, send_sem, recv_sem, k, place, land):
    x, y, c = place
    peer = (1 - x if k & 4 else x, 1 - y if k & 2 else y, 1 - c if k & 1 else c)
    return pltpu.make_async_remote_copy(
        src_ref=src_ref, dst_ref=land_ref.at[land], send_sem=send_sem, recv_sem=recv_sem,
        device_id=peer, device_id_type=_MESH)


def _gather_start(x_shard, after, name):
    npeer = N_DEV - 1

    def body(x_ref, land_ref, after_ref, x_thru, land_thru, *rest):
        sems, token = rest[:2 * npeer], rest[-1]
        x, y, c, _ = _place()
        for k in range(1, N_DEV):
            _peer_copy(x_ref, land_ref, sems[k - 1], sems[npeer + k - 1], k, (x, y, c), 4 * x + 2 * y + c).start()
        token[...] = jnp.zeros_like(token)

    land = pltpu.with_memory_space_constraint(lax.empty((N_DEV,) + tuple(x_shard.shape), x_shard.dtype), pltpu.HBM)
    res = pl.pallas_call(
        body, name=name,
        out_shape=(pltpu.HBM(x_shard.shape, x_shard.dtype), pltpu.HBM(land.shape, land.dtype),
                   *([pltpu.SemaphoreType.DMA(())] * (2 * npeer)), jax.ShapeDtypeStruct((8, LANES), F32)),
        in_specs=[_HBM, _HBM, _ANY],
        out_specs=(_HBM, _HBM, *([_SEM] * (2 * npeer)), pl.BlockSpec(memory_space=pltpu.VMEM)),
        input_output_aliases={0: 0, 1: 1},
        compiler_params=pltpu.CompilerParams(has_side_effects=_DATAFLOW),
    )(pltpu.with_memory_space_constraint(x_shard, pltpu.HBM), land, after)
    return list(res[2:2 + 2 * npeer]), res[0], res[1], res[-1]


def _gather_wait(sems, src, land, after, name):
    npeer = N_DEV - 1

    def body(x_ref, land_ref, *rest):
        sem_refs = rest[:2 * npeer]
        x, y, c, _ = _place()
        for k in range(1, N_DEV):
            peer_index = (4 * x + 2 * y + c) ^ k
            cp = _peer_copy(x_ref, land_ref, sem_refs[k - 1], sem_refs[npeer + k - 1], k, (x, y, c), peer_index)
            cp.wait_send()
            cp.wait_recv()

    res = pl.pallas_call(
        body, name=name, out_shape=(pltpu.HBM(src.shape, src.dtype), pltpu.HBM(land.shape, land.dtype)),
        in_specs=[_HBM, _HBM] + [_SEM] * (2 * npeer) + [_ANY], out_specs=(_HBM, _HBM),
        input_output_aliases={0: 0, 1: 1},
        compiler_params=pltpu.CompilerParams(has_side_effects=_DATAFLOW),
    )(src, land, *sems, after)
    return res[1]


def _prenorm(x2, w):
    t = x2.shape[0]
    tm = min(512, t)

    def body(x_ref, w_ref, h_ref):
        x = x_ref[...]
        r = lax.rsqrt(jnp.mean(x * x, axis=-1, keepdims=True) + NORM_EPS)
        h_ref[...] = (x * r * w_ref[...]).astype(BF16)

    return pl.pallas_call(
        body, name="prenorm", grid=(t // tm,),
        in_specs=[pl.BlockSpec((tm, D_MODEL), lambda i: (i, 0)), pl.BlockSpec((1, D_MODEL), lambda i: (0, 0))],
        out_specs=pl.BlockSpec((tm, D_MODEL), lambda i: (i, 0)),
        out_shape=jax.ShapeDtypeStruct((t, D_MODEL), BF16),
        compiler_params=_cparams(("parallel",)),
    )(x2, w)


def _mm_bias(a, bt, bias, out_dtype, name):
    m, k = a.shape
    n = bt.shape[0]
    tm = min(512, m)
    tn = min(1024, n)

    def body(a_ref, bt_ref, bias_ref, o_ref):
        o_ref[...] = (_dot_nt(a_ref[...], bt_ref[...]) + bias_ref[...]).astype(o_ref.dtype)

    return pl.pallas_call(
        body, name=name, grid=(n // tn, m // tm),
        in_specs=[pl.BlockSpec((tm, k), lambda j, i: (i, 0)), pl.BlockSpec((tn, k), lambda j, i: (j, 0)),
                  pl.BlockSpec((1, tn), lambda j, i: (0, j))],
        out_specs=pl.BlockSpec((tm, tn), lambda j, i: (i, j)),
        out_shape=jax.ShapeDtypeStruct((m, n), out_dtype),
        compiler_params=_cparams(("parallel", "parallel")),
    )(a, bt, bias)


def _mm_tn(a, b, name):
    t, m = a.shape
    n = b.shape[1]
    tm = min(1024, m)
    tk = min(512, t)

    def body(a_ref, b_ref, o_ref, s_ref):
        kk = pl.program_id(1)

        @pl.when(kk == 0)
        def _():
            o_ref[...] = jnp.zeros_like(o_ref)
            s_ref[...] = jnp.zeros_like(s_ref)

        aa = a_ref[...]
        o_ref[...] += _dot_tn(aa, b_ref[...])
        s_ref[0:1, :] += jnp.sum(aa.astype(F32), axis=0, keepdims=True)

    return pl.pallas_call(
        body, name=name, grid=(m // tm, t // tk),
        in_specs=[pl.BlockSpec((tk, tm), lambda i, kk: (kk, i)), pl.BlockSpec((tk, n), lambda i, kk: (kk, 0))],
        out_specs=[pl.BlockSpec((tm, n), lambda i, kk: (i, 0)), pl.BlockSpec((8, tm), lambda i, kk: (0, i))],
        out_shape=[jax.ShapeDtypeStruct((m, n), F32), jax.ShapeDtypeStruct((8, m), F32)],
        compiler_params=_cparams(("parallel", "arbitrary")),
    )(a, b)


def _fgate_fwd(zf3):
    b, s, _ = zf3.shape
    tb = SCAN_TILE
    nb = s // tb

    def body(z_ref, cexp_ref, crow_ref):
        tri = (_iota((tb, tb), 1) <= _iota((tb, tb), 0)).astype(BF16)
        expand = ((_iota((LANES, D_MODEL), 1) >> 6) == _iota((LANES, D_MODEL), 0)).astype(BF16)
        carry = jnp.zeros((1, LANES), F32)
        for i in range(nb):
            rows = slice(i * tb, (i + 1) * tb)
            z = z_ref[rows, :]
            lf = jnp.minimum(z, 0.0) - jnp.log1p(jnp.exp(-jnp.abs(z)))
            cb = sum(_dot(tri, part) for part in _split3(lf)) + carry
            carry = cb[tb - 1:tb, :]
            cexp_ref[rows, :] = sum(_dot(part, expand) for part in _split3(cb))
            crow_ref[:, rows] = cb.T[0:HEADS, :]

    return pl.pallas_call(
        body, name="fgate_fwd", grid=(b,),
        in_specs=[pl.BlockSpec((None, s, LANES), lambda i: (i, 0, 0))],
        out_specs=[pl.BlockSpec((None, s, D_MODEL), lambda i: (i, 0, 0)),
                   pl.BlockSpec((None, HEADS, s), lambda i: (i, 0, 0))],
        out_shape=[jax.ShapeDtypeStruct((b, s, D_MODEL), F32), jax.ShapeDtypeStruct((b, HEADS, s), F32)],
        compiler_params=_cparams(("parallel",)),
    )(zf3)


def _fgate_bwd(dc3, zf3):
    b, s, _ = zf3.shape
    tb = SCAN_TILE
    nb = s // tb

    def body(dc_ref, z_ref, o_ref):
        tri = (_iota((tb, tb), 1) >= _iota((tb, tb), 0)).astype(BF16)
        carry = jnp.zeros((1, LANES), F32)
        for i in reversed(range(nb)):
            rows = slice(i * tb, (i + 1) * tb)
            dlf = sum(_dot(tri, part) for part in _split3(dc_ref[rows, :])) + carry
            carry = dlf[0:1, :]
            o_ref[rows, :] = (dlf * _sigmoid(-z_ref[rows, :])).astype(BF16)

    return pl.pallas_call(
        body, name="fgate_bwd", grid=(b,),
        in_specs=[pl.BlockSpec((None, s, LANES), lambda i: (i, 0, 0)),
                  pl.BlockSpec((None, s, LANES), lambda i: (i, 0, 0))],
        out_specs=pl.BlockSpec((None, s, LANES), lambda i: (i, 0, 0)),
        out_shape=jax.ShapeDtypeStruct((b, s, LANES), BF16),
        compiler_params=_cparams(("parallel",)),
    )(dc3, zf3)


def _spare(hh):
    return HEAD_DIM if hh == 0 else 0


def _put_cols(tile, mine, cols, first):
    lane = _iota((1, LANES), 1)
    out = jnp.where(mine, tile, jnp.zeros((), tile.dtype))
    for j, c in enumerate(cols):
        out = jnp.where(lane == first + j, c, out)
    return out


def _put_rows(tile, mine, rows, first):
    sub = _iota((LANES, 1), 0)
    out = jnp.where(mine, tile, jnp.zeros((), tile.dtype))
    for j, r in enumerate(rows):
        out = jnp.where(sub == first + j, r, out)
    return out


def _transpose_bf16(a):
    return a.astype(F32).T.astype(BF16)


def _attn_fwd(qkv3, cexp3, crow, zrest3):
    b, s, _ = qkv3.shape
    ta = ATT_TILE_FWD
    nq = s // ta
    hd = HEAD_DIM
    crow5 = crow.reshape(b, HEAD_PAIRS, 2, nq, ta)

    def body(qkv_ref, cq_ref, ck_ref, g_ref, y_ref, lse_ref, ga_ref, kt_scr, v_scr):
        lane = _iota((1, LANES), 1)
        sub = _iota((LANES, 1), 0)
        lane_mine = (lane < hd, lane >= hd)
        sub_mine = (sub < hd, sub >= hd)
        causal = _iota((ta, ta), 0) >= _iota((ta, ta), 1)
        one = jnp.ones((), BF16)

        for kj in range(nq):
            rows = slice(kj * ta, (kj + 1) * ta)
            kt = _transpose_bf16(qkv_ref[rows, LANES:2 * LANES])
            v = qkv_ref[rows, 2 * LANES:3 * LANES]
            for hh in range(2):
                ck = list(_split3(-ck_ref[hh, kj:kj + 1, :]))
                kt_scr[hh, kj] = _put_rows(kt, sub_mine[hh], [one, one, one] + ck, _spare(hh))
                v_scr[hh, kj] = _put_cols(v, lane_mine[hh], [one], _spare(hh))

        for qi in range(nq):
            rows = slice(qi * ta, (qi + 1) * ta)
            q = qkv_ref[rows, 0:LANES] * 0.125
            cq = cq_ref[rows, :]
            qh = [_put_cols(q, lane_mine[hh], list(_split3(cq[:, hh * hd:hh * hd + 1])) + [one, one, one], _spare(hh))
                  for hh in range(2)]
            st = [(jnp.full((ta, 1), MASK_VALUE, F32), jnp.zeros((ta, LANES), F32))] * 2
            for kj in range(qi + 1):
                for hh in range(2):
                    m, acc = st[hh]
                    sc = _dot(qh[hh], kt_scr[hh, kj])
                    if kj == qi:
                        sc = jnp.where(causal, sc, MASK_VALUE)
                    mn = jnp.maximum(m, jnp.max(sc, axis=-1, keepdims=True))
                    p = jnp.exp(sc - mn).astype(BF16)
                    st[hh] = (mn, jnp.exp(m - mn) * acc + _dot(p, v_scr[hh, kj]))
            (ma, acca), (mb, accb) = st
            la = acca[:, hd:hd + 1]
            lb = accb[:, 0:1]
            y = jnp.where(lane_mine[0], acca * (1.0 / la), accb * (1.0 / lb))
            lse = jnp.where(lane_mine[0], ma + jnp.log(la), mb + jnp.log(lb)).T
            lse_ref[0, qi:qi + 1, :] = lse[0:1, :]
            lse_ref[1, qi:qi + 1, :] = lse[hd:hd + 1, :]
            y_ref[rows, :] = y
            g = g_ref[rows, :].astype(F32)
            ga_ref[rows, :] = (y * (g * _sigmoid(g))).astype(BF16)

    blk = lambda w: pl.BlockSpec((None, s, w), lambda i, p: (i, 0, p))
    rows5 = pl.BlockSpec((None, None, 2, nq, ta), lambda i, p: (i, p, 0, 0, 0))
    yatt3, lse5, ga3 = pl.pallas_call(
        body, name="attn_fwd", grid=(b, HEAD_PAIRS),
        in_specs=[blk(3 * LANES), blk(LANES), rows5, blk(LANES)],
        out_specs=[blk(LANES), rows5, blk(LANES)],
        out_shape=[jax.ShapeDtypeStruct((b, s, D_MODEL), F32),
                   jax.ShapeDtypeStruct((b, HEAD_PAIRS, 2, nq, ta), F32),
                   jax.ShapeDtypeStruct((b, s, D_MODEL), BF16)],
        scratch_shapes=[pltpu.VMEM((2, nq, LANES, ta), BF16), pltpu.VMEM((2, nq, ta, LANES), BF16)],
        compiler_params=_cparams(("parallel", "parallel")),
    )(qkv3, cexp3, crow5, zrest3)
    return yatt3, lse5.reshape(b, HEADS, s), ga3


def _attn_bwd(qkv3, do3, y3, lse, crow, cexp3):
    b, s, _ = qkv3.shape
    ta = ATT_TILE_BWD
    nq = s // ta
    hd = HEAD_DIM
    lse5 = lse.reshape(b, HEAD_PAIRS, 2, nq, ta)
    crow5 = crow.reshape(b, HEAD_PAIRS, 2, nq, ta)

    def body(qkv_ref, do_ref, y_ref, lse_ref, crow_ref, cexp_ref, dqkv_ref, dc_ref,
             qa_scr, doa_scr, qst_scr, dot_scr, kt_scr, vt_scr, dq_scr, rs_scr):
        pair = pl.program_id(1)
        lane = _iota((1, LANES), 1)
        sub = _iota((LANES, 1), 0)
        lane_mine = (lane < hd, lane >= hd)
        sub_mine = (sub < hd, sub >= hd)
        causal = _iota((ta, ta), 0) >= _iota((ta, ta), 1)
        one = jnp.ones((), BF16)
        zero = jnp.zeros((), BF16)

        @pl.when(pair == 0)
        def _():
            dc_ref[...] = jnp.zeros_like(dc_ref)

        for i in range(nq):
            rows = slice(i * ta, (i + 1) * ta)
            qs = qkv_ref[rows, 0:LANES] * 0.125
            qst = _transpose_bf16(qs)
            kt = _transpose_bf16(qkv_ref[rows, LANES:2 * LANES])
            vt = _transpose_bf16(qkv_ref[rows, 2 * LANES:3 * LANES])
            do = do_ref[rows, :]
            dof = do.astype(F32)
            dot = dof.T.astype(BF16)
            pr = y_ref[rows, :] * dof
            cq = cexp_ref[rows, :]
            lse_c = jnp.where(sub == 0, lse_ref[0, i:i + 1, :],
                              jnp.where(sub == 1, lse_ref[1, i:i + 1, :], 0.0)).T
            for hh in range(2):
                sp = _spare(hh)
                dsum = jnp.sum(jnp.where(lane_mine[hh], pr, 0.0), axis=-1, keepdims=True)
                bias = cq[:, hh * hd:hh * hd + 1] - lse_c[:, hh:hh + 1]
                qa_scr[hh, i] = _put_cols(qs, lane_mine[hh], list(_split3(bias)) + [one, one, one], sp)
                doa_scr[hh, i] = _put_cols(do, lane_mine[hh], list(_split3(-dsum)), sp)
                qst_scr[hh, i] = jnp.where(sub_mine[hh], qst, zero)
                dot_scr[hh, i] = jnp.where(sub_mine[hh], dot, zero)
                ck = list(_split3(-crow_ref[hh, i:i + 1, :]))
                kt_scr[hh, i] = _put_rows(kt, sub_mine[hh], [one, one, one] + ck, sp)
                vt_scr[hh, i] = _put_rows(vt, sub_mine[hh], [one, one, one], sp)
            dq_scr[i] = jnp.zeros((ta, LANES), F32)
            rs_scr[i] = jnp.zeros((ta, LANES), F32)

        for kj in range(nq):
            krows = slice(kj * ta, (kj + 1) * ta)
            k = qkv_ref[krows, LANES:2 * LANES]
            km = (jnp.where(lane_mine[0], k, zero), jnp.where(lane_mine[1], k, zero))
            dkt = jnp.zeros((LANES, ta), F32)
            dvt = jnp.zeros((LANES, ta), F32)
            dcp = [jnp.zeros((8, ta), F32), jnp.zeros((8, ta), F32)]
            for qi in range(kj, nq):
                dq = jnp.zeros((ta, LANES), F32)
                rs = []
                for hh in range(2):
                    sc = _dot(qa_scr[hh, qi], kt_scr[hh, kj])
                    if qi == kj:
                        sc = jnp.where(causal, sc, MASK_VALUE)
                    p = jnp.exp(sc)
                    dsf = p * _dot(doa_scr[hh, qi], vt_scr[hh, kj])
                    dcp[hh] = dcp[hh] + jnp.sum(dsf.reshape(ta // 8, 8, ta), axis=0)
                    rs.append(jnp.sum(dsf, axis=-1, keepdims=True))
                    ds = dsf.astype(BF16)
                    dq = dq + _dot(ds, km[hh])
                    dkt = dkt + _dot(qst_scr[hh, qi], ds)
                    dvt = dvt + _dot(dot_scr[hh, qi], p.astype(BF16))
                dq_scr[qi] += dq
                rs_scr[qi] += jnp.where(lane == 0, rs[0], jnp.where(lane == 1, rs[1], 0.0))
            dqkv_ref[krows, LANES:2 * LANES] = dkt.T.astype(BF16)
            dqkv_ref[krows, 2 * LANES:3 * LANES] = dvt.T.astype(BF16)
            dca = jnp.sum(dcp[0], axis=0, keepdims=True)
            dcb = jnp.sum(dcp[1], axis=0, keepdims=True)
            dcs = jnp.where(sub == 0, dca, jnp.where(sub == 1, dcb, 0.0)).T
            dc_ref[krows, :] += (jnp.where(lane == 2 * pair, -dcs[:, 0:1], 0.0)
                                 + jnp.where(lane == 2 * pair + 1, -dcs[:, 1:2], 0.0))
        for qi in range(nq):
            rows = slice(qi * ta, (qi + 1) * ta)
            dqkv_ref[rows, 0:LANES] = (dq_scr[qi] * 0.125).astype(BF16)
            rq = rs_scr[qi]
            dc_ref[rows, :] += (jnp.where(lane == 2 * pair, rq[:, 0:1], 0.0)
                                + jnp.where(lane == 2 * pair + 1, rq[:, 1:2], 0.0))

    blk = lambda w: pl.BlockSpec((None, s, w), lambda i, p: (i, 0, p))
    rows5 = pl.BlockSpec((None, None, 2, nq, ta), lambda i, p: (i, p, 0, 0, 0))
    by_rows = lambda: pltpu.VMEM((2, nq, ta, LANES), BF16)
    by_cols = lambda: pltpu.VMEM((2, nq, LANES, ta), BF16)
    return pl.pallas_call(
        body, name="attn_bwd", grid=(b, HEAD_PAIRS),
        in_specs=[blk(3 * LANES), blk(LANES), blk(LANES), rows5, rows5, blk(LANES)],
        out_specs=[blk(3 * LANES), pl.BlockSpec((None, s, LANES), lambda i, p: (i, 0, 0))],
        out_shape=[jax.ShapeDtypeStruct((b, s, 3 * D_MODEL), BF16), jax.ShapeDtypeStruct((b, s, LANES), F32)],
        scratch_shapes=[by_rows(), by_rows(), by_cols(), by_cols(), by_cols(), by_cols(),
                        pltpu.VMEM((nq, ta, LANES), F32), pltpu.VMEM((nq, ta, LANES), F32)],
        compiler_params=_cparams(("parallel", "arbitrary")),
    )(qkv3, do3, y3, lse5, crow5, cexp3)


def _shifted(v, ks, rows, s):
    return [jnp.where(rows >= k, pltpu.roll(v, k, 0), 0.0) if k > 0
            else jnp.where(rows < s + k, pltpu.roll(v, s + k, 0), 0.0) for k in ks]


def _rnn_common(xr, cw_ref, cb_ref, bda_ref, bdx_ref, ba_ref, bx_ref, lam_ref, s):
    rows = _iota((s, LANES), 0)
    x1, x2, x3 = _shifted(xr, (1, 2, 3), rows, s)
    xc = cb_ref[...] + cw_ref[0:1, :] * x3
    xc = xc + cw_ref[1:2, :] * x2
    xc = xc + cw_ref[2:3, :] * x1
    xc = xc + cw_ref[3:4, :] * xr
    xcb = xc.astype(BF16)
    r = _sigmoid(_dot(xcb, bda_ref[...]) + ba_ref[...])
    i = _sigmoid(_dot(xcb, bdx_ref[...]) + bx_ref[...])
    sp = _softplus(-lam_ref[...])
    log_a = (-RG_C * r) * sp
    a = jnp.exp(log_a)
    a2 = a * a
    sq = jnp.sqrt(jnp.maximum(_one_minus_exp(2.0 * log_a, a2), 0.0))
    return rows, (x1, x2, x3), xc, xcb, r, i, sp, a, a2, sq


def _scan_down(a, u, rows, s, s1, s2):
    low = rows & 7
    for sh in (1, 2, 4):
        keep = low >= sh
        u = u + a * jnp.where(keep, pltpu.roll(u, sh, 0), 0.0)
        a = a * jnp.where(keep, pltpu.roll(a, sh, 0), 1.0)
    ng = s // 8
    s1[...] = a
    s2[...] = u
    at = s1[pl.ds(7, ng, stride=8), :]
    ut = s2[pl.ds(7, ng, stride=8), :]
    grow = _iota((ng, LANES), 0)
    sh = 1
    while sh < ng:
        keep = grow >= sh
        ut = ut + at * jnp.where(keep, pltpu.roll(ut, sh, 0), 0.0)
        if sh * 2 < ng:
            at = at * jnp.where(keep, pltpu.roll(at, sh, 0), 1.0)
        sh *= 2
    h_in = jnp.where(grow >= 1, pltpu.roll(ut, 1, 0), 0.0)
    for k in range(8):
        s1[pl.ds(k, ng, stride=8), :] = h_in
    return u + a * s1[...]


def _scan_up(a, g, rows, s, s1, s2):
    low = rows & 7
    for sh in (1, 2, 4):
        keep = low < 8 - sh
        g = g + a * jnp.where(keep, pltpu.roll(g, s - sh, 0), 0.0)
        a = a * jnp.where(keep, pltpu.roll(a, s - sh, 0), 1.0)
    ng = s // 8
    s1[...] = a
    s2[...] = g
    at = s1[pl.ds(0, ng, stride=8), :]
    gt = s2[pl.ds(0, ng, stride=8), :]
    grow = _iota((ng, LANES), 0)
    sh = 1
    while sh < ng:
        keep = grow < ng - sh
        gt = gt + at * jnp.where(keep, pltpu.roll(gt, ng - sh, 0), 0.0)
        if sh * 2 < ng:
            at = at * jnp.where(keep, pltpu.roll(at, ng - sh, 0), 1.0)
        sh *= 2
    g_in = jnp.where(grow < ng - 1, pltpu.roll(gt, ng - 1, 0), 0.0)
    for k in range(8):
        s1[pl.ds(k, ng, stride=8), :] = g_in
    return g + a * s1[...]


def _rnn_specs(s):
    blk = lambda off: pl.BlockSpec((None, s, LANES), lambda cb, i: (i, 0, off + cb))
    vec = lambda r: pl.BlockSpec((r, LANES), lambda cb, i: (0, cb))
    mat = pl.BlockSpec((None, LANES, LANES), lambda cb, i: (cb, 0, 0))
    return blk, vec, mat


def _rnn_fwd(zrest3, conv_w, conv_b, bda, bdx, ba, bx, lam):
    b, s, _ = zrest3.shape

    def body(xr_ref, g_ref, cw_ref, cb_ref, bda_ref, bdx_ref, ba_ref, bx_ref, lam_ref, h_ref, gr_ref, s1, s2):
        xr = xr_ref[...].astype(F32)
        rows, _, xc, _, _, i, _, a, _, sq = _rnn_common(
            xr, cw_ref, cb_ref, bda_ref, bdx_ref, ba_ref, bx_ref, lam_ref, s)
        h = _scan_down(a, sq * (i * xc), rows, s, s1, s2)
        h_ref[...] = h
        g = g_ref[...].astype(F32)
        gr_ref[...] = (h * (g * _sigmoid(g))).astype(BF16)

    blk, vec, mat = _rnn_specs(s)
    return pl.pallas_call(
        body, name="rnn_fwd", grid=(N_CBLK, b),
        in_specs=[blk(N_CBLK), blk(2 * N_CBLK), vec(CONV_W), vec(1), mat, mat, vec(1), vec(1), vec(1)],
        out_specs=[blk(0), blk(0)],
        out_shape=[jax.ShapeDtypeStruct((b, s, D_MODEL), F32), jax.ShapeDtypeStruct((b, s, D_MODEL), BF16)],
        scratch_shapes=[pltpu.VMEM((s, LANES), F32), pltpu.VMEM((s, LANES), F32)],
        compiler_params=_cparams(("parallel", "parallel")),
    )(zrest3, zrest3, conv_w, conv_b, bda, bdx, ba, bx, lam)


def _rnn_bwd(zrest3, h3, dh3, conv_w, conv_b, bda, bdx, ba, bx, lam):
    b, s, _ = zrest3.shape

    def body(xr_ref, h_ref, dh_ref, cw_ref, cb_ref, bda_ref, bdx_ref, ba_ref, bx_ref, lam_ref,
             dxr_ref, pv_ref, dbd_ref, s1, s2):
        @pl.when(pl.program_id(1) == 0)
        def _():
            pv_ref[...] = jnp.zeros_like(pv_ref)
            dbd_ref[...] = jnp.zeros_like(dbd_ref)

        xr = xr_ref[...].astype(F32)
        rows, (x1, x2, x3), xc, xcb, r, i, sp, a, a2, sq = _rnn_common(
            xr, cw_ref, cb_ref, bda_ref, bdx_ref, ba_ref, bx_ref, lam_ref, s)
        (a_next,) = _shifted(a, (-1,), rows, s)
        g = _scan_up(a_next, dh_ref[...], rows, s, s1, s2)
        (hp,) = _shifted(h_ref[...], (1,), rows, s)
        da = g * hp
        dsq = g * (i * xc)
        di = g * (sq * xc)
        dxc = g * (sq * i)
        dlog = da * a - dsq * (a2 / sq)
        dr = dlog * (-RG_C * sp)
        dpr = dr * (r * (1.0 - r))
        dpi = di * (i * (1.0 - i))
        dprb = dpr.astype(BF16)
        dpib = dpi.astype(BF16)
        dxc = dxc + _dot_nt(dprb, bda_ref[...]) + _dot_nt(dpib, bdx_ref[...])

        up1, up2, up3 = _shifted(dxc, (-1, -2, -3), rows, s)
        dxr = cw_ref[3:4, :] * dxc + cw_ref[2:3, :] * up1 + cw_ref[1:2, :] * up2 + cw_ref[0:1, :] * up3
        dxr_ref[...] = dxr.astype(BF16)

        def colsum(v):
            return jnp.sum(v, axis=0, keepdims=True)

        pv_ref[0:1, :] += colsum(dxc * x3)
        pv_ref[1:2, :] += colsum(dxc * x2)
        pv_ref[2:3, :] += colsum(dxc * x1)
        pv_ref[3:4, :] += colsum(dxc * xr)
        pv_ref[4:5, :] += colsum(dxc)
        pv_ref[5:6, :] += colsum(dpr)
        pv_ref[6:7, :] += colsum(dpi)
        pv_ref[7:8, :] += colsum(dlog * r) * (RG_C * _sigmoid(-lam_ref[...]))
        dbd_ref[0] += _dot_tn(xcb, dprb)
        dbd_ref[1] += _dot_tn(xcb, dpib)

    blk, vec, mat = _rnn_specs(s)
    hblk = pl.BlockSpec((None, s, LANES), lambda cb, i: (i, 0, cb))
    return pl.pallas_call(
        body, name="rnn_bwd", grid=(N_CBLK, b),
        in_specs=[blk(N_CBLK), hblk, hblk, vec(CONV_W), vec(1), mat, mat, vec(1), vec(1), vec(1)],
        out_specs=[hblk, pl.BlockSpec((8, LANES), lambda cb, i: (0, cb)),
                   pl.BlockSpec((None, 2, LANES, LANES), lambda cb, i: (cb, 0, 0, 0))],
        out_shape=[jax.ShapeDtypeStruct((b, s, D_MODEL), BF16), jax.ShapeDtypeStruct((8, D_MODEL), F32),
                   jax.ShapeDtypeStruct((N_CBLK, 2, LANES, LANES), F32)],
        scratch_shapes=[pltpu.VMEM((s, LANES), F32), pltpu.VMEM((s, LANES), F32)],
        compiler_params=_cparams(("parallel", "arbitrary")),
    )(zrest3, h3, dh3, conv_w, conv_b, bda, bdx, ba, bx, lam)


def _branch_merge(ga, gr, wa, wr, zrest):
    t = ga.shape[0]
    tm = min(512, t)
    tn = 512

    def body(ga_ref, gr_ref, wa_ref, wr_ref, mga_ref, mgr_ref, ya_ref, yr_ref, m_ref):
        ya = _dot(ga_ref[...], wa_ref[...])
        yr = _dot(gr_ref[...], wr_ref[...])
        ya_ref[...] = ya.astype(BF16)
        yr_ref[...] = yr.astype(BF16)
        m_ref[...] = (_sigmoid(mga_ref[...].astype(F32)) * ya + _sigmoid(mgr_ref[...].astype(F32)) * yr).astype(BF16)

    nj = D_MODEL // tn
    act = pl.BlockSpec((tm, D_MODEL), lambda i, j: (i, 0))
    wgt = pl.BlockSpec((D_MODEL, tn), lambda i, j: (0, j))
    out = pl.BlockSpec((tm, tn), lambda i, j: (i, j))
    return pl.pallas_call(
        body, name="branch_merge", grid=(t // tm, nj),
        in_specs=[act, act, wgt, wgt, pl.BlockSpec((tm, tn), lambda i, j: (i, 3 * nj + j)),
                  pl.BlockSpec((tm, tn), lambda i, j: (i, 4 * nj + j))],
        out_specs=[out, out, out],
        out_shape=[jax.ShapeDtypeStruct((t, D_MODEL), BF16), jax.ShapeDtypeStruct((t, D_MODEL), BF16),
                   jax.ShapeDtypeStruct((t, D_MODEL), BF16)],
        compiler_params=_cparams(("parallel", "parallel")),
    )(ga, gr, wa, wr, zrest, zrest)


def _out_loss(m, wout, x2, tgt2, wpost):
    t = m.shape[0]
    tm = min(256, t)

    def body(m_ref, w_ref, x_ref, t_ref, wp_ref, dy_ref, do_ref, acc_ref):
        @pl.when(pl.program_id(0) == 0)
        def _():
            acc_ref[...] = jnp.zeros_like(acc_ref)

        o = _dot(m_ref[...], w_ref[...])
        r2 = lax.rsqrt(jnp.mean(o * o, axis=-1, keepdims=True) + NORM_EPS)
        n = o * r2
        wp = wp_ref[...]
        err = (x_ref[...] + n * wp) - t_ref[...]
        dy = err * (1.0 / D_MODEL)
        dn = dy * wp
        do = r2 * (dn - n * jnp.mean(dn * n, axis=-1, keepdims=True))
        dy_ref[...] = dy
        do_ref[...] = do.astype(BF16)
        acc_ref[0:1, :] += jnp.sum(dy * n, axis=0, keepdims=True)
        acc_ref[1:2, :] += jnp.sum(err * err, axis=0, keepdims=True)

    row = pl.BlockSpec((tm, D_MODEL), lambda i: (i, 0))
    return pl.pallas_call(
        body, name="out_loss", grid=(t // tm,),
        in_specs=[row, pl.BlockSpec((D_MODEL, D_MODEL), lambda i: (0, 0)), row, row,
                  pl.BlockSpec((1, D_MODEL), lambda i: (0, 0))],
        out_specs=[row, row, pl.BlockSpec((8, D_MODEL), lambda i: (0, 0))],
        out_shape=[jax.ShapeDtypeStruct((t, D_MODEL), F32), jax.ShapeDtypeStruct((t, D_MODEL), BF16),
                   jax.ShapeDtypeStruct((8, D_MODEL), F32)],
        compiler_params=_cparams(("arbitrary",)),
    )(m, wout, x2, tgt2, wpost)


def _merge_bwd(do, wout, zrest, ya, yr):
    t = do.shape[0]
    tm = min(512, t)
    tn = 512
    nj = D_MODEL // tn

    def body(do_ref, w_ref, mga_ref, mgr_ref, ya_ref, yr_ref, dya_ref, dyr_ref, dmga_ref, dmgr_ref):
        dm = _dot_nt(do_ref[...], w_ref[...])
        sa = _sigmoid(mga_ref[...].astype(F32))
        sr = _sigmoid(mgr_ref[...].astype(F32))
        dya_ref[...] = (dm * sa).astype(BF16)
        dyr_ref[...] = (dm * sr).astype(BF16)
        dmga_ref[...] = (dm * ya_ref[...].astype(F32) * (sa * (1.0 - sa))).astype(BF16)
        dmgr_ref[...] = (dm * yr_ref[...].astype(F32) * (sr * (1.0 - sr))).astype(BF16)

    out = pl.BlockSpec((tm, tn), lambda i, j: (i, j))
    bf = jax.ShapeDtypeStruct((t, D_MODEL), BF16)
    return pl.pallas_call(
        body, name="merge_bwd", grid=(t // tm, nj),
        in_specs=[pl.BlockSpec((tm, D_MODEL), lambda i, j: (i, 0)), pl.BlockSpec((tn, D_MODEL), lambda i, j: (j, 0)),
                  pl.BlockSpec((tm, tn), lambda i, j: (i, 3 * nj + j)),
                  pl.BlockSpec((tm, tn), lambda i, j: (i, 4 * nj + j)), out, out],
        out_specs=[out, out, out, out],
        out_shape=[bf, bf, bf, bf],
        compiler_params=_cparams(("parallel", "parallel")),
    )(do, wout, zrest, zrest, ya, yr)


def _branch_bwd(dya, dyr, wa, wr, zrest, yatt, ylru):
    t = dya.shape[0]
    tm = min(512, t)
    tn = 512
    nj = D_MODEL // tn

    def body(dya_ref, dyr_ref, wa_ref, wr_ref, ga_ref, gr_ref, ya_ref, yl_ref,
             dyatt_ref, dga_ref, dyl_ref, dgr_ref):
        dga = _dot_nt(dya_ref[...], wa_ref[...])
        dgr = _dot_nt(dyr_ref[...], wr_ref[...])
        g = ga_ref[...].astype(F32)
        sg = _sigmoid(g)
        dyatt_ref[...] = (dga * (g * sg)).astype(BF16)
        dga_ref[...] = (dga * ya_ref[...] * (sg * (1.0 + g * (1.0 - sg)))).astype(BF16)
        g = gr_ref[...].astype(F32)
        sg = _sigmoid(g)
        dyl_ref[...] = dgr * (g * sg)
        dgr_ref[...] = (dgr * yl_ref[...] * (sg * (1.0 + g * (1.0 - sg)))).astype(BF16)

    act = pl.BlockSpec((tm, D_MODEL), lambda i, j: (i, 0))
    wgt = pl.BlockSpec((tn, D_MODEL), lambda i, j: (j, 0))
    out = pl.BlockSpec((tm, tn), lambda i, j: (i, j))
    bf = jax.ShapeDtypeStruct((t, D_MODEL), BF16)
    return pl.pallas_call(
        body, name="branch_bwd", grid=(t // tm, nj),
        in_specs=[act, act, wgt, wgt, pl.BlockSpec((tm, tn), lambda i, j: (i, j)),
                  pl.BlockSpec((tm, tn), lambda i, j: (i, 2 * nj + j)), out, out],
        out_specs=[out, out, out, out],
        out_shape=[bf, bf, jax.ShapeDtypeStruct((t, D_MODEL), F32), bf],
        compiler_params=_cparams(("parallel", "parallel")),
    )(dya, dyr, wa, wr, zrest, zrest, yatt, ylru)


def _dh_partial(parts, after, name):
    t = parts[0][0].shape[0]
    tm = min(256, t)
    np_ = len(parts)

    def body(*refs):
        o_ref = refs[-1]
        acc = _dot(refs[0][...], refs[np_][...])
        for p in range(1, np_):
            acc = acc + _dot(refs[p][...], refs[np_ + p][...])
        o_ref[...] = acc

    in_specs = [pl.BlockSpec((tm, dz.shape[1]), lambda i: (i, 0)) for dz, _ in parts]
    in_specs += [pl.BlockSpec(w.shape, lambda i: (0, 0)) for _, w in parts]
    in_specs += [pl.BlockSpec(after.shape, lambda i: (0, 0))]
    return pl.pallas_call(
        body, name=name, grid=(t // tm,),
        in_specs=in_specs,
        out_specs=pl.BlockSpec((tm, D_MODEL), lambda i: (i, 0)),
        out_shape=jax.ShapeDtypeStruct((t, D_MODEL), F32),
        compiler_params=_cparams(("parallel",), vmem_mb=48),
    )(*[dz for dz, _ in parts], *[w for _, w in parts], after)


def _dh_final(parts, acc_in, x2, dy, wpre):
    t = x2.shape[0]
    tm = min(256, t)
    np_ = len(parts)

    def body(*refs):
        acc_ref, x_ref, dy_ref, w_ref = refs[2 * np_:2 * np_ + 4]
        gx_ref, pw_ref = refs[2 * np_ + 4:]

        @pl.when(pl.program_id(0) == 0)
        def _():
            pw_ref[...] = jnp.zeros_like(pw_ref)

        dh = acc_ref[...]
        for p in range(np_):
            dh = dh + _dot(refs[p][...], refs[np_ + p][...])
        x = x_ref[...]
        r = lax.rsqrt(jnp.mean(x * x, axis=-1, keepdims=True) + NORM_EPS)
        xn = x * r
        dxn = dh * w_ref[...]
        gx_ref[...] = r * (dxn - xn * jnp.mean(dxn * xn, axis=-1, keepdims=True)) + dy_ref[...]
        pw_ref[0:1, :] += jnp.sum(dh * xn, axis=0, keepdims=True)

    row = pl.BlockSpec((tm, D_MODEL), lambda i: (i, 0))
    in_specs = [pl.BlockSpec((tm, dz.shape[1]), lambda i: (i, 0)) for dz, _ in parts]
    in_specs += [pl.BlockSpec(w.shape, lambda i: (0, 0)) for _, w in parts]
    in_specs += [row, row, row, pl.BlockSpec((1, D_MODEL), lambda i: (0, 0))]
    return pl.pallas_call(
        body, name="dh_final", grid=(t // tm,),
        in_specs=in_specs,
        out_specs=[row, pl.BlockSpec((8, D_MODEL), lambda i: (0, 0))],
        out_shape=[jax.ShapeDtypeStruct((t, D_MODEL), F32), jax.ShapeDtypeStruct((8, D_MODEL), F32)],
        compiler_params=_cparams(("arbitrary",), vmem_mb=48),
    )(*[dz for dz, _ in parts], *[w for _, w in parts], acc_in, x2, dy, wpre)


def _adamw(w, g, m, v):
    m = ADAM_B1 * m + (1.0 - ADAM_B1) * g
    v = ADAM_B2 * v + (1.0 - ADAM_B2) * (g * g)
    m_hat = m / (1.0 - ADAM_B1 ** ADAM_STEP)
    v_hat = v / (1.0 - ADAM_B2 ** ADAM_STEP)
    delta = -ADAM_LR * (m_hat / (jnp.sqrt(v_hat) + ADAM_EPS) + ADAM_WD * w)
    return delta, m, v


def _reduce_adamw(own, parts, place, w, m, v, name):
    r, c = w.shape
    blk, nblk, at = _blocks_2d(r, c)

    def body(place_ref, own_ref, p_ref, w_ref, m_ref, v_ref, g_ref, d_ref, nm_ref, nv_ref):
        mine = place_ref[1]
        own_blk = own_ref[...]
        g = jnp.where(mine == 0, own_blk, p_ref[0].astype(F32))
        for j in range(1, N_CHIPS):
            g = g + jnp.where(mine == j, own_blk, p_ref[j].astype(F32))
        d, nm, nv = _adamw(w_ref[...], g, m_ref[...], v_ref[...])
        g_ref[...] = g
        d_ref[...] = d
        nm_ref[...] = nm
        nv_ref[...] = nv

    row = pl.BlockSpec(blk, lambda i, pr: at(i))
    sh = jax.ShapeDtypeStruct((r, c), F32)
    grid_spec = pltpu.PrefetchScalarGridSpec(
        num_scalar_prefetch=1, grid=(nblk,),
        in_specs=[row, pl.BlockSpec((N_CHIPS,) + blk, lambda i, pr: (0,) + at(i)), row, row, row],
        out_specs=[row, row, row, row])
    return pl.pallas_call(
        body, name=name, grid_spec=grid_spec, out_shape=[sh, sh, sh, sh],
        compiler_params=_cparams(("parallel",)),
    )(place, own, parts, w, m, v)


def _interleave_qkv(a):
    lead = a.shape[:-1]
    return a.reshape(lead + (3, HEAD_PAIRS, LANES)).swapaxes(-3, -2).reshape(lead + (3 * D_MODEL,))


def _deinterleave_qkv(a):
    lead = a.shape[:-1]
    return a.reshape(lead + (HEAD_PAIRS, 3, LANES)).swapaxes(-3, -2).reshape(lead + (3 * D_MODEL,))


def _interleave_rows(a):
    return a.reshape(3, HEAD_PAIRS, LANES, a.shape[1]).swapaxes(0, 1).reshape(a.shape)


def _deinterleave_rows(a):
    return a.reshape(HEAD_PAIRS, 3, LANES, a.shape[1]).swapaxes(0, 1).reshape(a.shape)


def _pack_small(pre, conv_b, rg_ba, rg_bx, lam, post, loss_row, b_in, conv_w_full, rg_wa, rg_wx):
    z = jnp.zeros((1, D_MODEL), F32)
    b_used = jnp.concatenate([b_in[:, 0:3 * D_MODEL], b_in[:, 3 * D_MODEL + HEADS:IN_TOTAL]], axis=1)
    b_f = jnp.pad(b_in[:, 3 * D_MODEL:3 * D_MODEL + HEADS], ((0, 0), (0, D_MODEL - HEADS)))
    return jnp.concatenate([
        pre, conv_b, rg_ba, rg_bx, lam, post, loss_row, z,
        b_used.reshape(9, D_MODEL), b_f, conv_w_full, z, z,
        rg_wa.reshape(64, D_MODEL), rg_wx.reshape(64, D_MODEL)], axis=0)


def _unpack_small(p):
    b_used = p[8:17].reshape(1, 9 * D_MODEL)
    b_in = jnp.concatenate([b_used[:, 0:3 * D_MODEL], p[17:18, 0:HEADS], b_used[:, 3 * D_MODEL:]], axis=1)
    return dict(pre_norm_w=p[0:1], conv_b=p[1:2], rg_ba=p[2:3], rg_bx=p[3:4], rg_lambda=p[4:5],
                post_norm_w=p[5:6], loss_row=p[6:7], b_in=b_in, conv_w_full=p[18:22],
                rg_wa=p[24:88].reshape(1, 16, 64, 64), rg_wx=p[88:152].reshape(1, 16, 64, 64))


def _reduce_small(parts, w, m, v):
    def body(p_ref, w_ref, m_ref, v_ref, g_ref, d_ref, nm_ref, nv_ref):
        g = p_ref[0]
        for j in range(1, N_DEV):
            g = g + p_ref[j]
        d, nm, nv = _adamw(w_ref[...], g, m_ref[...], v_ref[...])
        g_ref[...] = g
        d_ref[...] = d
        nm_ref[...] = nm
        nv_ref[...] = nv

    sh = jax.ShapeDtypeStruct((SMALL_ROWS, D_MODEL), F32)
    return pl.pallas_call(body, name="reduce_small", out_shape=[sh, sh, sh, sh])(parts, w, m, v)


def kernel(x, pre_norm_w, w_in, b_in, conv_w, conv_b, rg_wa, rg_ba, rg_wx, rg_bx, rg_lambda, w_branch_a, w_branch_r, w_out, post_norm_w, loss_target, m_pre_norm_w, m_w_in, m_b_in, m_conv_w, m_conv_b, m_rg_wa, m_rg_ba, m_rg_wx, m_rg_bx, m_rg_lambda, m_w_branch_a, m_w_branch_r, m_w_out, m_post_norm_w, v_pre_norm_w, v_w_in, v_b_in, v_conv_w, v_conv_b, v_rg_wa, v_rg_ba, v_rg_wx, v_rg_bx, v_rg_lambda, v_w_branch_a, v_w_branch_r, v_w_out, v_post_norm_w):
    b, s, _ = x.shape
    t = b * s
    me = 4 * lax.axis_index("x") + 2 * lax.axis_index("y") + lax.axis_index("c")
    shard_rows = D_MODEL // N_DEV

    place = jnp.stack([lax.axis_index("c"), 2 * lax.axis_index("x") + lax.axis_index("y")]).astype(jnp.int32)
    w_in_all = _gather(w_in[0].T.astype(BF16), "gather_w_in")
    wt_full = w_in_all.reshape(IN_TOTAL, D_MODEL)
    conv_terms = jnp.concatenate(_split3(conv_w[0]), axis=0)
    conv_pad = jnp.pad(conv_terms, ((0, 16 - 3 * CONV_W), (0, D_MODEL - LANES)))
    sq_stack = jnp.concatenate([w_branch_a[0].astype(BF16), w_branch_r[0].astype(BF16), w_out[0].astype(BF16),
                                conv_pad], axis=0)
    sq_sems, sq_src, sq_land, sq_token = _gather_start(sq_stack, w_in_all, "gather_w_sq_start")

    w_qkv = _interleave_rows(wt_full[0:3 * D_MODEL])
    w_f = jnp.pad(wt_full[3 * D_MODEL:3 * D_MODEL + HEADS], ((0, LANES - HEADS), (0, 0)))
    w_rest = wt_full[3 * D_MODEL + HEADS:IN_USED]
    b_qkv = _interleave_qkv(b_in[:, 0:3 * D_MODEL]) + sq_token[0, 0]
    b_f = jnp.pad(b_in[:, 3 * D_MODEL:3 * D_MODEL + HEADS], ((0, 0), (0, LANES - HEADS)))
    b_rest = b_in[:, 3 * D_MODEL + HEADS:IN_USED]

    def blockdiag(w):
        w2 = w.reshape(N_CBLK, 2, HEAD_DIM, HEAD_DIM)
        zz = jnp.zeros((N_CBLK, HEAD_DIM, HEAD_DIM), w.dtype)
        top = jnp.concatenate([w2[:, 0], zz], axis=2)
        bot = jnp.concatenate([zz, w2[:, 1]], axis=2)
        return jnp.concatenate([top, bot], axis=1).astype(BF16)

    bda, bdx = blockdiag(rg_wa[0]), blockdiag(rg_wx[0])

    x2 = x.reshape(t, D_MODEL)
    tgt2 = loss_target.reshape(t, D_MODEL)
    h = _prenorm(x2, pre_norm_w)
    qkv = _mm_bias(h, w_qkv, b_qkv, BF16, "inproj_qkv")
    zrest = _mm_bias(h, w_rest, b_rest, BF16, "inproj_rest")
    zf = _mm_bias(h, w_f, b_f, F32, "inproj_f")
    qkv3 = qkv.reshape(b, s, 3 * D_MODEL)
    zrest3 = zrest.reshape(b, s, 5 * D_MODEL)
    zf3 = zf.reshape(b, s, LANES)
    cexp3, crow = _fgate_fwd(zf3)
    yatt3, lse, ga3 = _attn_fwd(qkv3, cexp3, crow, zrest3)

    sq_all = _gather_wait(sq_sems, sq_src, sq_land, ga3, "gather_w_sq_wait")
    sq_all = lax.dynamic_update_slice(sq_all, sq_stack[None], (me, 0, 0))
    wa = sq_all[:, 0:shard_rows].reshape(D_MODEL, D_MODEL)
    wr = sq_all[:, shard_rows:2 * shard_rows].reshape(D_MODEL, D_MODEL)
    wo = sq_all[:, 2 * shard_rows:3 * shard_rows].reshape(D_MODEL, D_MODEL)
    conv_all = sq_all[:, 3 * shard_rows:3 * shard_rows + 3 * CONV_W, 0:LANES].astype(F32)
    conv_all = (conv_all[:, 0:CONV_W] + conv_all[:, CONV_W:2 * CONV_W]) + conv_all[:, 2 * CONV_W:3 * CONV_W]
    conv_full = conv_all.transpose(1, 0, 2).reshape(CONV_W, D_MODEL)

    ylru3, gr3 = _rnn_fwd(zrest3, conv_full, conv_b, bda, bdx, rg_ba, rg_bx, rg_lambda)
    ga, gr = ga3.reshape(t, D_MODEL), gr3.reshape(t, D_MODEL)
    ya, yr, mm = _branch_merge(ga, gr, wa, wr, zrest)
    dy, do, acc_out = _out_loss(mm, wo, x2, tgt2, post_norm_w)

    dya, dyr, dz_mga, dz_mgr = _merge_bwd(do, wo, zrest, ya, yr)
    dyatt, dz_ga, dylru, dz_gr = _branch_bwd(dya, dyr, wa, wr, zrest, yatt3.reshape(t, D_MODEL),
                                             ylru3.reshape(t, D_MODEL))
    dz_xr3, pvec, dbd = _rnn_bwd(zrest3, ylru3, dylru.reshape(b, s, D_MODEL), conv_full, conv_b, bda, bdx,
                                 rg_ba, rg_bx, rg_lambda)
    dqkv3, dc3 = _attn_bwd(qkv3, dyatt.reshape(b, s, D_MODEL), yatt3, lse, crow, cexp3)
    dz_f = _fgate_bwd(dc3, zf3).reshape(t, LANES)
    dz_qkv = dqkv3.reshape(t, 3 * D_MODEL)
    dz_xr = dz_xr3.reshape(t, D_MODEL)

    dw_qkv, db_qkv = _mm_tn(dz_qkv, h, "dw_qkv")
    dw_f, db_f = _mm_tn(dz_f, h, "dw_f")
    dw_parts, db_parts = [], []
    for nm, dzp in (("ga", dz_ga), ("xr", dz_xr), ("gr", dz_gr), ("mga", dz_mga), ("mgr", dz_mgr)):
        dwp, dbp = _mm_tn(dzp, h, "dw_" + nm)
        dw_parts.append(dwp)
        db_parts.append(dbp[0:1])
    dw_a, _ = _mm_tn(ga, dya, "dw_a")
    dw_r, _ = _mm_tn(gr, dyr, "dw_r")
    dw_o, _ = _mm_tn(mm, do, "dw_o")

    zeros_tail = jnp.zeros((IN_TOTAL - IN_USED, D_MODEL), F32)
    dwt_full = jnp.concatenate([_deinterleave_rows(dw_qkv), dw_f[0:HEADS]] + dw_parts + [zeros_tail], axis=0)
    dw_in_send = dwt_full.reshape(N_CHIPS, 2, W_SHARD, D_MODEL).transpose(1, 0, 2, 3)
    by_dest = lambda a: a.reshape(N_CHIPS, 2, shard_rows, D_MODEL).transpose(1, 0, 2, 3)
    dw_sq_send = jnp.concatenate([by_dest(dw_a), by_dest(dw_r), by_dest(dw_o)], axis=2)

    sib_in, sib_sq = _swap_with_sibling([dw_in_send, dw_sq_send], "swap_dw")
    chip_in, own_in = _pair_add(dw_in_send, sib_in, place, "pair_add_in")
    chip_sq, own_sq = _pair_add(dw_sq_send, sib_sq, place, "pair_add_sq")
    sems, sent, lands, token = _exchange_chips_start([chip_in, chip_sq], "exchange_dw_start")

    wt = lambda lo: w_rest[lo * D_MODEL:(lo + 1) * D_MODEL]
    dh_a = _dh_partial([(dz_qkv, w_qkv), (dz_f, w_f)], token, "dh_qkv")
    grad_x2, acc_pre = _dh_final(
        [(dz_ga, wt(0)), (dz_xr, wt(1)), (dz_gr, wt(2)), (dz_mga, wt(3)), (dz_mgr, wt(4))],
        dh_a, x2, dy, pre_norm_w)

    db_in_full = jnp.concatenate([_deinterleave_qkv(db_qkv[0:1]), db_f[0:1, 0:HEADS]] + db_parts
                                 + [jnp.zeros((1, IN_TOTAL - IN_USED), F32)], axis=1)
    d_rg_wa = jnp.stack([dbd[:, 0, 0:HEAD_DIM, 0:HEAD_DIM], dbd[:, 0, HEAD_DIM:, HEAD_DIM:]], axis=1)
    d_rg_wx = jnp.stack([dbd[:, 1, 0:HEAD_DIM, 0:HEAD_DIM], dbd[:, 1, HEAD_DIM:, HEAD_DIM:]], axis=1)
    small_g = _pack_small(acc_pre[0:1], pvec[4:5], pvec[5:6], pvec[6:7], pvec[7:8], acc_out[0:1], acc_out[1:2],
                          db_in_full, pvec[0:4], d_rg_wa, d_rg_wx)
    sm_sems, sm_src, sm_land, sm_token = _gather_start(small_g, grad_x2, "gather_small_start")
    recv_in, recv_sq = _exchange_chips_wait(sems, sent, lands, sm_token, "exchange_dw_wait")

    g_in, d_in, nm_in, nv_in = [a.T for a in _reduce_adamw(
        own_in, recv_in, place, w_in[0].T, m_w_in[0].T, v_w_in[0].T, "adamw_w_in")]
    sq_w = jnp.concatenate([w_branch_a[0], w_branch_r[0], w_out[0]], axis=0)
    sq_m = jnp.concatenate([m_w_branch_a[0], m_w_branch_r[0], m_w_out[0]], axis=0)
    sq_v = jnp.concatenate([v_w_branch_a[0], v_w_branch_r[0], v_w_out[0]], axis=0)
    g_sq, d_sq, nm_sq, nv_sq = _reduce_adamw(own_sq, recv_sq, place, sq_w, sq_m, sq_v, "adamw_w_sq")
    small_all = _gather_wait(sm_sems, sm_src, sm_land, d_sq, "gather_small_wait")
    small_all = lax.dynamic_update_slice(small_all, small_g[None], (me, 0, 0))

    def place_conv(a):
        return lax.dynamic_update_slice(jnp.zeros((CONV_W, D_MODEL), F32), a[0], (0, me * LANES))

    zrow = jnp.zeros((1, D_MODEL), F32)
    small_w = _pack_small(pre_norm_w, conv_b, rg_ba, rg_bx, rg_lambda, post_norm_w, zrow, b_in,
                          place_conv(conv_w), rg_wa[0], rg_wx[0])
    small_m = _pack_small(m_pre_norm_w, m_conv_b, m_rg_ba, m_rg_bx, m_rg_lambda, m_post_norm_w, zrow, m_b_in,
                          place_conv(m_conv_w), m_rg_wa[0], m_rg_wx[0])
    small_v = _pack_small(v_pre_norm_w, v_conv_b, v_rg_ba, v_rg_bx, v_rg_lambda, v_post_norm_w, zrow, v_b_in,
                          place_conv(v_conv_w), v_rg_wa[0], v_rg_wx[0])
    outs_small = [_unpack_small(p) for p in _reduce_small(small_all, small_w, small_m, small_v)]

    loss = (0.5 / D_MODEL) * jnp.sum(outs_small[0]["loss_row"])

    def leaf(kind, name):
        if name == "w_in":
            return (g_in, d_in, nm_in, nv_in)[kind][None]
        if name in ("w_branch_a", "w_branch_r", "w_out"):
            j = ("w_branch_a", "w_branch_r", "w_out").index(name)
            return (g_sq, d_sq, nm_sq, nv_sq)[kind][None, j * shard_rows:(j + 1) * shard_rows]
        if name == "conv_w":
            return lax.dynamic_slice(outs_small[kind]["conv_w_full"], (0, me * LANES), (CONV_W, LANES))[None]
        return outs_small[kind][name]

    names = ["pre_norm_w", "w_in", "b_in", "conv_w", "conv_b", "rg_wa", "rg_ba", "rg_wx", "rg_bx", "rg_lambda",
             "w_branch_a", "w_branch_r", "w_out", "post_norm_w"]
    out = [loss, grad_x2.reshape(b, s, D_MODEL)]
    for kind in range(4):
        out += [leaf(kind, nm) for nm in names]
    return tuple(out)
```

```python
import jax
import jax.numpy as jnp
from jax import lax
from jax.experimental import pallas as pl
from jax.experimental.pallas import tpu as pltpu

F32 = jnp.float32
BF16 = jnp.bfloat16

N_DEV = 8
D_MODEL = 1024
HEADS = 16
HEAD_DIM = 64
HEAD_PAIRS = HEADS // 2
LANES = 128
N_CBLK = D_MODEL // LANES
CONV_W = 4
RG_C = 8.0
NORM_EPS = 1e-6
MASK_VALUE = -1e30
IN_USED = 8208
IN_TOTAL = 9232
W_SHARD = IN_TOTAL // N_DEV

ADAM_LR = 0.001
ADAM_B1 = 0.9
ADAM_B2 = 0.999
ADAM_EPS = 1e-08
ADAM_WD = 0.01
ADAM_STEP = 10

ATT_TILE_FWD = 256
ATT_TILE_BWD = 512
SCAN_TILE = 256
SMALL_ROWS = 152


def _cparams(sem=None, vmem_mb=None):
    kw = {}
    if sem is not None:
        kw["dimension_semantics"] = sem
    if vmem_mb is not None:
        kw["vmem_limit_bytes"] = vmem_mb * 1024 * 1024
    return pltpu.CompilerParams(**kw)


def _sigmoid(x):
    return 1.0 / (1.0 + jnp.exp(-x))


def _softplus(x):
    return jnp.maximum(x, 0.0) + jnp.log1p(jnp.exp(-jnp.abs(x)))


def _one_minus_exp(y, exp_y):
    series = -y * (1.0 + y * (1.0 / 2 + y * (1.0 / 6 + y * (1.0 / 24 + y * (1.0 / 120)))))
    return jnp.where(y > -0.0625, series, 1.0 - exp_y)


def _split3(x):
    hi = x.astype(BF16)
    r1 = x - hi.astype(F32)
    mid = r1.astype(BF16)
    lo = (r1 - mid.astype(F32)).astype(BF16)
    return hi, mid, lo


def _dot(a, b):
    return jnp.dot(a, b, preferred_element_type=F32)


def _dot_nt(a, b):
    return lax.dot_general(a, b, (((1,), (1,)), ((), ())), preferred_element_type=F32)


def _dot_tn(a, b):
    return lax.dot_general(a, b, (((0,), (0,)), ((), ())), preferred_element_type=F32)


def _iota(shape, dim):
    return lax.broadcasted_iota(jnp.int32, shape, dim)


_ANY = pl.BlockSpec(memory_space=pl.ANY)
_MESH = pl.DeviceIdType.MESH
N_CHIPS = 4


def _place():
    x, y, c = lax.axis_index("x"), lax.axis_index("y"), lax.axis_index("c")
    other_chips = [(1 - x, y), (x, 1 - y), (1 - x, 1 - y)]
    return x, y, c, other_chips


def _gather(x_shard, name):
    def body(x_ref, out_ref, send_sems, recv_sems, local_sem):
        x, y, c, chips = _place()
        me, sibling = (x, y, c), (x, y, 1 - c)

        def slot(p):
            return out_ref.at[4 * p[0] + 2 * p[1] + p[2]]

        def copy(k, block, to, src=None):
            return pltpu.make_async_remote_copy(
                src_ref=slot(block) if src is None else src, dst_ref=slot(block),
                send_sem=send_sems.at[k], recv_sem=recv_sems.at[k], device_id=to, device_id_type=_MESH)

        mine = pltpu.make_async_copy(x_ref, slot(me), local_sem)
        mine.start()
        first = [copy(0, me, sibling, src=x_ref)]
        first += [copy(1 + j, me, (*chip, c), src=x_ref) for j, chip in enumerate(chips)]
        for cp in first:
            cp.start()
        passed = [copy(4 + j, (*chip, c), sibling) for j, chip in enumerate(chips)]
        for j, chip in enumerate(chips):
            copy(1 + j, (*chip, c), me).wait_recv()
            passed[j].start()
        copy(0, sibling, me).wait_recv()
        for j, chip in enumerate(chips):
            copy(4 + j, (*chip, 1 - c), me).wait_recv()
        for cp in first + passed:
            cp.wait_send()
        mine.wait()

    return pl.pallas_call(
        body, name=name,
        out_shape=jax.ShapeDtypeStruct((N_DEV,) + tuple(x_shard.shape), x_shard.dtype),
        in_specs=[_ANY], out_specs=_ANY,
        scratch_shapes=[pltpu.SemaphoreType.DMA((7,)), pltpu.SemaphoreType.DMA((7,)), pltpu.SemaphoreType.DMA],
    )(x_shard)


def _swap_with_sibling(srcs, name):
    n = len(srcs)

    def body(*refs):
        src_refs, out_refs = refs[:n], refs[n:2 * n]
        send_sems, recv_sems = refs[2 * n:]
        x, y, c, _ = _place()
        cps = [pltpu.make_async_remote_copy(
            src_ref=src_refs[i].at[1 - c], dst_ref=out_refs[i], send_sem=send_sems.at[i], recv_sem=recv_sems.at[i],
            device_id=(x, y, 1 - c), device_id_type=_MESH) for i in range(n)]
        for cp in cps:
            cp.start()
        for cp in cps:
            cp.wait()

    return pl.pallas_call(
        body, name=name,
        out_shape=[jax.ShapeDtypeStruct(a.shape[1:], a.dtype) for a in srcs],
        in_specs=[_ANY] * n, out_specs=[_ANY] * n,
        scratch_shapes=[pltpu.SemaphoreType.DMA((n,)), pltpu.SemaphoreType.DMA((n,))],
    )(*srcs)


def _blocks_2d(r, c):
    if r % 128 == 0:
        return (128, c), r // 128, lambda i: (i, 0)
    return (r, 256), c // 256, lambda i: (0, i)


def _pair_add(src, recv, place, name):
    _, _, r, c = src.shape
    blk, nblk, at = _blocks_2d(r, c)

    def body(place_ref, a_ref, b_ref, q16_ref, own_ref):
        q = a_ref[...] + b_ref[...]
        q16_ref[...] = q.astype(BF16)

        @pl.when(pl.program_id(1) == place_ref[1])
        def _():
            own_ref[...] = q

    grid_spec = pltpu.PrefetchScalarGridSpec(
        num_scalar_prefetch=1, grid=(nblk, N_CHIPS),
        in_specs=[pl.BlockSpec((None, None) + blk, lambda i, j, pr: (pr[0], j) + at(i)),
                  pl.BlockSpec((None,) + blk, lambda i, j, pr: (j,) + at(i))],
        out_specs=[pl.BlockSpec((None,) + blk, lambda i, j, pr: (j,) + at(i)),
                   pl.BlockSpec(blk, lambda i, j, pr: at(i))])
    return pl.pallas_call(
        body, name=name, grid_spec=grid_spec,
        out_shape=[jax.ShapeDtypeStruct((N_CHIPS, r, c), BF16), jax.ShapeDtypeStruct((r, c), F32)],
        compiler_params=_cparams(("parallel", "arbitrary")),
    )(place, src, recv)


_HBM = pl.BlockSpec(memory_space=pltpu.HBM)
_SEM = pl.BlockSpec(memory_space=pltpu.SEMAPHORE)
_DATAFLOW = pltpu.SideEffectType.DATAFLOW_SIDE_EFFECTING


def _chip_copy(src_ref, land_ref, send_sem, recv_sem, k, chips, c, land):
    chip = chips[k]
    return pltpu.make_async_remote_copy(
        src_ref=src_ref.at[2 * chip[0] + chip[1]], dst_ref=land_ref.at[land],
        send_sem=send_sem, recv_sem=recv_sem, device_id=(*chip, c), device_id_type=_MESH)


def _exchange_chips_start(srcs, name):
    n = len(srcs)
    ncp = 3 * n

    def body(*refs):
        src_refs, land_refs = refs[:n], refs[n:2 * n]
        sems = refs[4 * n:4 * n + 2 * ncp]
        token = refs[-1]
        x, y, c, chips = _place()
        for i in range(n):
            for k in range(3):
                j = 3 * i + k
                _chip_copy(src_refs[i], land_refs[i], sems[j], sems[ncp + j], k, chips, c, 2 * x + y).start()
        token[...] = jnp.zeros_like(token)

    hbm = [pltpu.HBM(a.shape, a.dtype) for a in srcs]
    lands = [pltpu.with_memory_space_constraint(lax.empty(a.shape, a.dtype), pltpu.HBM) for a in srcs]
    res = pl.pallas_call(
        body, name=name,
        out_shape=(*hbm, *hbm, *([pltpu.SemaphoreType.DMA(())] * (2 * ncp)), jax.ShapeDtypeStruct((8, LANES), F32)),
        in_specs=[_HBM] * (2 * n),
        out_specs=(*([_HBM] * (2 * n)), *([_SEM] * (2 * ncp)), pl.BlockSpec(memory_space=pltpu.VMEM)),
        input_output_aliases={i: i for i in range(2 * n)},
        compiler_params=pltpu.CompilerParams(has_side_effects=_DATAFLOW),
    )(*[pltpu.with_memory_space_constraint(a, pltpu.HBM) for a in srcs], *lands)
    return list(res[2 * n:2 * n + 2 * ncp]), list(res[:n]), list(res[n:2 * n]), res[-1]


def _exchange_chips_wait(sems, srcs, lands, after, name):
    n = len(srcs)
    ncp = 3 * n

    def body(*refs):
        src_refs, land_refs = refs[:n], refs[n:2 * n]
        sem_refs = refs[2 * n:2 * n + 2 * ncp]
        x, y, c, chips = _place()
        for i in range(n):
            for k in range(3):
                j = 3 * i + k
                cp = _chip_copy(src_refs[i], land_refs[i], sem_refs[j], sem_refs[ncp + j], k, chips, c,
                                2 * chips[k][0] + chips[k][1])
                cp.wait_send()
                cp.wait_recv()

    hbm = [pltpu.HBM(a.shape, a.dtype) for a in srcs]
    res = pl.pallas_call(
        body, name=name, out_shape=(*hbm, *hbm),
        in_specs=[_HBM] * (2 * n) + [_SEM] * (2 * ncp) + [_ANY], out_specs=tuple([_HBM] * (2 * n)),
        input_output_aliases={i: i for i in range(2 * n)},
        compiler_params=pltpu.CompilerParams(has_side_effects=_DATAFLOW),
    )(*srcs, *lands, *sems, after)
    return list(res[n:2 * n])


def _peer_copy(src_ref, land_ref, send_sem, recv_sem, k, place, land):
    x, y, c = place
    peer = (1 - x if k & 4 else x, 1 - y if k & 2 else y, 1 - c if k & 1 else c)
    return pltpu.make_async_remote_copy(
        src_ref=src_ref, dst_ref=land_ref.at[land], send_sem=send_sem, recv_sem=recv_sem,
        device_id=peer, device_id_type=_MESH)


def _gather_start(x_shard, after, name):
    npeer = N_DEV - 1

    def body(x_ref, land_ref, after_ref, x_thru, land_thru, *rest):
        sems, token = rest[:2 * npeer], rest[-1]
        x, y, c, _ = _place()
        for k in range(1, N_DEV):
            _peer_copy(x_ref, land_ref, sems[k - 1], sems[npeer + k - 1], k, (x, y, c), 4 * x + 2 * y + c).start()
        token[...] = jnp.zeros_like(token)

    land = pltpu.with_memory_space_constraint(lax.empty((N_DEV,) + tuple(x_shard.shape), x_shard.dtype), pltpu.HBM)
    res = pl.pallas_call(
        body, name=name,
        out_shape=(pltpu.HBM(x_shard.shape, x_shard.dtype), pltpu.HBM(land.shape, land.dtype),
                   *([pltpu.SemaphoreType.DMA(())] * (2 * npeer)), jax.ShapeDtypeStruct((8, LANES), F32)),
        in_specs=[_HBM, _HBM, _ANY],
        out_specs=(_HBM, _HBM, *([_SEM] * (2 * npeer)), pl.BlockSpec(memory_space=pltpu.VMEM)),
        input_output_aliases={0: 0, 1: 1},
        compiler_params=pltpu.CompilerParams(has_side_effects=_DATAFLOW),
    )(pltpu.with_memory_space_constraint(x_shard, pltpu.HBM), land, after)
    return list(res[2:2 + 2 * npeer]), res[0], res[1], res[-1]


def _gather_wait(sems, src, land, after, name):
    npeer = N_DEV - 1

    def body(x_ref, land_ref, *rest):
        sem_refs = rest[:2 * npeer]
        x, y, c, _ = _place()
        for k in range(1, N_DEV):
            peer_index = (4 * x + 2 * y + c) ^ k
            cp = _peer_copy(x_ref, land_ref, sem_refs[k - 1], sem_refs[npeer + k - 1], k, (x, y, c), peer_index)
            cp.wait_send()
            cp.wait_recv()

    res = pl.pallas_call(
        body, name=name, out_shape=(pltpu.HBM(src.shape, src.dtype), pltpu.HBM(land.shape, land.dtype)),
        in_specs=[_HBM, _HBM] + [_SEM] * (2 * npeer) + [_ANY], out_specs=(_HBM, _HBM),
        input_output_aliases={0: 0, 1: 1},
        compiler_params=pltpu.CompilerParams(has_side_effects=_DATAFLOW),
    )(src, land, *sems, after)
    return res[1]


def _prenorm(x2, w):
    t = x2.shape[0]
    tm = min(512, t)

    def body(x_ref, w_ref, h_ref):
        x = x_ref[...]
        r = lax.rsqrt(jnp.mean(x * x, axis=-1, keepdims=True) + NORM_EPS)
        h_ref[...] = (x * r * w_ref[...]).astype(BF16)

    return pl.pallas_call(
        body, name="prenorm", grid=(t // tm,),
        in_specs=[pl.BlockSpec((tm, D_MODEL), lambda i: (i, 0)), pl.BlockSpec((1, D_MODEL), lambda i: (0, 0))],
        out_specs=pl.BlockSpec((tm, D_MODEL), lambda i: (i, 0)),
        out_shape=jax.ShapeDtypeStruct((t, D_MODEL), BF16),
        compiler_params=_cparams(("parallel",)),
    )(x2, w)


def _mm_bias(a, bt, bias, out_dtype, name):
    m, k = a.shape
    n = bt.shape[0]
    tm = min(512, m)
    tn = min(1024, n)

    def body(a_ref, bt_ref, bias_ref, o_ref):
        o_ref[...] = (_dot_nt(a_ref[...], bt_ref[...]) + bias_ref[...]).astype(o_ref.dtype)

    return pl.pallas_call(
        body, name=name, grid=(n // tn, m // tm),
        in_specs=[pl.BlockSpec((tm, k), lambda j, i: (i, 0)), pl.BlockSpec((tn, k), lambda j, i: (j, 0)),
                  pl.BlockSpec((1, tn), lambda j, i: (0, j))],
        out_specs=pl.BlockSpec((tm, tn), lambda j, i: (i, j)),
        out_shape=jax.ShapeDtypeStruct((m, n), out_dtype),
        compiler_params=_cparams(("parallel", "parallel")),
    )(a, bt, bias)


def _mm_tn(a, b, name):
    t, m = a.shape
    n = b.shape[1]
    tm = min(1024, m)
    tk = min(2048, t)

    def body(a_ref, b_ref, o_ref, s_ref):
        kk = pl.program_id(1)

        @pl.when(kk == 0)
        def _():
            o_ref[...] = jnp.zeros_like(o_ref)
            s_ref[...] = jnp.zeros_like(s_ref)

        aa = a_ref[...]
        o_ref[...] += _dot_tn(aa, b_ref[...])
        s_ref[0:1, :] += jnp.sum(aa.astype(F32), axis=0, keepdims=True)

    return pl.pallas_call(
        body, name=name, grid=(m // tm, t // tk),
        in_specs=[pl.BlockSpec((tk, tm), lambda i, kk: (kk, i)), pl.BlockSpec((tk, n), lambda i, kk: (kk, 0))],
        out_specs=[pl.BlockSpec((tm, n), lambda i, kk: (i, 0)), pl.BlockSpec((8, tm), lambda i, kk: (0, i))],
        out_shape=[jax.ShapeDtypeStruct((m, n), F32), jax.ShapeDtypeStruct((8, m), F32)],
        compiler_params=_cparams(("parallel", "arbitrary"), vmem_mb=48),
    )(a, b)


def _fgate_fwd(zf3):
    b, s, _ = zf3.shape
    tb = SCAN_TILE
    nb = s // tb

    def body(z_ref, cexp_ref, crow_ref):
        tri = (_iota((tb, tb), 1) <= _iota((tb, tb), 0)).astype(BF16)
        expand = ((_iota((LANES, D_MODEL), 1) >> 6) == _iota((LANES, D_MODEL), 0)).astype(BF16)
        carry = jnp.zeros((1, LANES), F32)
        for i in range(nb):
            rows = slice(i * tb, (i + 1) * tb)
            z = z_ref[rows, :]
            lf = jnp.minimum(z, 0.0) - jnp.log1p(jnp.exp(-jnp.abs(z)))
            cb = sum(_dot(tri, part) for part in _split3(lf)) + carry
            carry = cb[tb - 1:tb, :]
            cexp_ref[rows, :] = sum(_dot(part, expand) for part in _split3(cb))
            crow_ref[:, rows] = cb.T[0:HEADS, :]

    return pl.pallas_call(
        body, name="fgate_fwd", grid=(b,),
        in_specs=[pl.BlockSpec((None, s, LANES), lambda i: (i, 0, 0))],
        out_specs=[pl.BlockSpec((None, s, D_MODEL), lambda i: (i, 0, 0)),
                   pl.BlockSpec((None, HEADS, s), lambda i: (i, 0, 0))],
        out_shape=[jax.ShapeDtypeStruct((b, s, D_MODEL), F32), jax.ShapeDtypeStruct((b, HEADS, s), F32)],
        compiler_params=_cparams(("parallel",)),
    )(zf3)


def _fgate_bwd(dc3, zf3):
    b, s, _ = zf3.shape
    tb = SCAN_TILE
    nb = s // tb

    def body(dc_ref, z_ref, o_ref):
        tri = (_iota((tb, tb), 1) >= _iota((tb, tb), 0)).astype(BF16)
        carry = jnp.zeros((1, LANES), F32)
        for i in reversed(range(nb)):
            rows = slice(i * tb, (i + 1) * tb)
            dlf = sum(_dot(tri, part) for part in _split3(dc_ref[rows, :])) + carry
            carry = dlf[0:1, :]
            o_ref[rows, :] = (dlf * _sigmoid(-z_ref[rows, :])).astype(BF16)

    return pl.pallas_call(
        body, name="fgate_bwd", grid=(b,),
        in_specs=[pl.BlockSpec((None, s, LANES), lambda i: (i, 0, 0)),
                  pl.BlockSpec((None, s, LANES), lambda i: (i, 0, 0))],
        out_specs=pl.BlockSpec((None, s, LANES), lambda i: (i, 0, 0)),
        out_shape=jax.ShapeDtypeStruct((b, s, LANES), BF16),
        compiler_params=_cparams(("parallel",)),
    )(dc3, zf3)


def _spare(hh):
    return HEAD_DIM if hh == 0 else 0


def _put_cols(tile, mine, cols, first):
    lane = _iota((1, LANES), 1)
    out = jnp.where(mine, tile, jnp.zeros((), tile.dtype))
    for j, c in enumerate(cols):
        out = jnp.where(lane == first + j, c, out)
    return out


def _put_rows(tile, mine, rows, first):
    sub = _iota((LANES, 1), 0)
    out = jnp.where(mine, tile, jnp.zeros((), tile.dtype))
    for j, r in enumerate(rows):
        out = jnp.where(sub == first + j, r, out)
    return out


def _transpose_bf16(a):
    return a.astype(F32).T.astype(BF16)


def _attn_fwd(qkv3, cexp3, crow, zrest3):
    b, s, _ = qkv3.shape
    ta = ATT_TILE_FWD
    nq = s // ta
    hd = HEAD_DIM
    crow5 = crow.reshape(b, HEAD_PAIRS, 2, nq, ta)

    def body(qkv_ref, cq_ref, ck_ref, g_ref, y_ref, lse_ref, ga_ref, kt_scr, v_scr):
        lane = _iota((1, LANES), 1)
        sub = _iota((LANES, 1), 0)
        lane_mine = (lane < hd, lane >= hd)
        sub_mine = (sub < hd, sub >= hd)
        causal = _iota((ta, ta), 0) >= _iota((ta, ta), 1)
        one = jnp.ones((), BF16)

        for kj in range(nq):
            rows = slice(kj * ta, (kj + 1) * ta)
            kt = _transpose_bf16(qkv_ref[rows, LANES:2 * LANES])
            v = qkv_ref[rows, 2 * LANES:3 * LANES]
            for hh in range(2):
                ck = list(_split3(-ck_ref[hh, kj:kj + 1, :]))
                kt_scr[hh, kj] = _put_rows(kt, sub_mine[hh], [one, one, one] + ck, _spare(hh))
                v_scr[hh, kj] = _put_cols(v, lane_mine[hh], [one], _spare(hh))

        for qi in range(nq):
            rows = slice(qi * ta, (qi + 1) * ta)
            q = qkv_ref[rows, 0:LANES] * 0.125
            cq = cq_ref[rows, :]
            qh = [_put_cols(q, lane_mine[hh], list(_split3(cq[:, hh * hd:hh * hd + 1])) + [one, one, one], _spare(hh))
                  for hh in range(2)]
            st = [(jnp.full((ta, 1), MASK_VALUE, F32), jnp.zeros((ta, LANES), F32))] * 2
            for kj in range(qi + 1):
                for hh in range(2):
                    m, acc = st[hh]
                    sc = _dot(qh[hh], kt_scr[hh, kj])
                    if kj == qi:
                        sc = jnp.where(causal, sc, MASK_VALUE)
                    mn = jnp.maximum(m, jnp.max(sc, axis=-1, keepdims=True))
                    p = jnp.exp(sc - mn).astype(BF16)
                    st[hh] = (mn, jnp.exp(m - mn) * acc + _dot(p, v_scr[hh, kj]))
            (ma, acca), (mb, accb) = st
            la = acca[:, hd:hd + 1]
            lb = accb[:, 0:1]
            y = jnp.where(lane_mine[0], acca * (1.0 / la), accb * (1.0 / lb))
            lse = jnp.where(lane_mine[0], ma + jnp.log(la), mb + jnp.log(lb)).T
            lse_ref[0, qi:qi + 1, :] = lse[0:1, :]
            lse_ref[1, qi:qi + 1, :] = lse[hd:hd + 1, :]
            y_ref[rows, :] = y
            g = g_ref[rows, :].astype(F32)
            ga_ref[rows, :] = (y * (g * _sigmoid(g))).astype(BF16)

    blk = lambda w: pl.BlockSpec((None, s, w), lambda i, p: (i, 0, p))
    rows5 = pl.BlockSpec((None, None, 2, nq, ta), lambda i, p: (i, p, 0, 0, 0))
    yatt3, lse5, ga3 = pl.pallas_call(
        body, name="attn_fwd", grid=(b, HEAD_PAIRS),
        in_specs=[blk(3 * LANES), blk(LANES), rows5, blk(LANES)],
        out_specs=[blk(LANES), rows5, blk(LANES)],
        out_shape=[jax.ShapeDtypeStruct((b, s, D_MODEL), F32),
                   jax.ShapeDtypeStruct((b, HEAD_PAIRS, 2, nq, ta), F32),
                   jax.ShapeDtypeStruct((b, s, D_MODEL), BF16)],
        scratch_shapes=[pltpu.VMEM((2, nq, LANES, ta), BF16), pltpu.VMEM((2, nq, ta, LANES), BF16)],
        compiler_params=_cparams(("parallel", "parallel")),
    )(qkv3, cexp3, crow5, zrest3)
    return yatt3, lse5.reshape(b, HEADS, s), ga3


def _attn_bwd(qkv3, do3, y3, lse, crow, cexp3):
    b, s, _ = qkv3.shape
    ta = ATT_TILE_BWD
    nq = s // ta
    hd = HEAD_DIM
    lse5 = lse.reshape(b, HEAD_PAIRS, 2, nq, ta)
    crow5 = crow.reshape(b, HEAD_PAIRS, 2, nq, ta)

    def body(qkv_ref, do_ref, y_ref, lse_ref, crow_ref, cexp_ref, dqkv_ref, dc_ref,
             qa_scr, doa_scr, qst_scr, dot_scr, kt_scr, vt_scr, dq_scr, rs_scr):
        pair = pl.program_id(1)
        lane = _iota((1, LANES), 1)
        sub = _iota((LANES, 1), 0)
        lane_mine = (lane < hd, lane >= hd)
        sub_mine = (sub < hd, sub >= hd)
        causal = _iota((ta, ta), 0) >= _iota((ta, ta), 1)
        one = jnp.ones((), BF16)
        zero = jnp.zeros((), BF16)

        @pl.when(pair == 0)
        def _():
            dc_ref[...] = jnp.zeros_like(dc_ref)

        for i in range(nq):
            rows = slice(i * ta, (i + 1) * ta)
            qs = qkv_ref[rows, 0:LANES] * 0.125
            qst = _transpose_bf16(qs)
            kt = _transpose_bf16(qkv_ref[rows, LANES:2 * LANES])
            vt = _transpose_bf16(qkv_ref[rows, 2 * LANES:3 * LANES])
            do = do_ref[rows, :]
            dof = do.astype(F32)
            dot = dof.T.astype(BF16)
            pr = y_ref[rows, :] * dof
            cq = cexp_ref[rows, :]
            lse_c = jnp.where(sub == 0, lse_ref[0, i:i + 1, :],
                              jnp.where(sub == 1, lse_ref[1, i:i + 1, :], 0.0)).T
            for hh in range(2):
                sp = _spare(hh)
                dsum = jnp.sum(jnp.where(lane_mine[hh], pr, 0.0), axis=-1, keepdims=True)
                bias = cq[:, hh * hd:hh * hd + 1] - lse_c[:, hh:hh + 1]
                qa_scr[hh, i] = _put_cols(qs, lane_mine[hh], list(_split3(bias)) + [one, one, one], sp)
                doa_scr[hh, i] = _put_cols(do, lane_mine[hh], list(_split3(-dsum)), sp)
                qst_scr[hh, i] = jnp.where(sub_mine[hh], qst, zero)
                dot_scr[hh, i] = jnp.where(sub_mine[hh], dot, zero)
                ck = list(_split3(-crow_ref[hh, i:i + 1, :]))
                kt_scr[hh, i] = _put_rows(kt, sub_mine[hh], [one, one, one] + ck, sp)
                vt_scr[hh, i] = _put_rows(vt, sub_mine[hh], [one, one, one], sp)
            dq_scr[i] = jnp.zeros((ta, LANES), F32)
            rs_scr[i] = jnp.zeros((ta, LANES), F32)

        for kj in range(nq):
            krows = slice(kj * ta, (kj + 1) * ta)
            k = qkv_ref[krows, LANES:2 * LANES]
            km = (jnp.where(lane_mine[0], k, zero), jnp.where(lane_mine[1], k, zero))
            dkt = jnp.zeros((LANES, ta), F32)
            dvt = jnp.zeros((LANES, ta), F32)
            dcp = [jnp.zeros((8, ta), F32), jnp.zeros((8, ta), F32)]
            for qi in range(kj, nq):
                dq = jnp.zeros((ta, LANES), F32)
                rs = []
                for hh in range(2):
                    sc = _dot(qa_scr[hh, qi], kt_scr[hh, kj])
                    if qi == kj:
                        sc = jnp.where(causal, sc, MASK_VALUE)
                    p = jnp.exp(sc)
                    dsf = p * _dot(doa_scr[hh, qi], vt_scr[hh, kj])
                    dcp[hh] = dcp[hh] + jnp.sum(dsf.reshape(ta // 8, 8, ta), axis=0)
                    rs.append(jnp.sum(dsf, axis=-1, keepdims=True))
                    ds = dsf.astype(BF16)
                    dq = dq + _dot(ds, km[hh])
                    dkt = dkt + _dot(qst_scr[hh, qi], ds)
                    dvt = dvt + _dot(dot_scr[hh, qi], p.astype(BF16))
                dq_scr[qi] += dq
                rs_scr[qi] += jnp.where(lane == 0, rs[0], jnp.where(lane == 1, rs[1], 0.0))
            dqkv_ref[krows, LANES:2 * LANES] = dkt.T.astype(BF16)
            dqkv_ref[krows, 2 * LANES:3 * LANES] = dvt.T.astype(BF16)
            dca = jnp.sum(dcp[0], axis=0, keepdims=True)
            dcb = jnp.sum(dcp[1], axis=0, keepdims=True)
            dcs = jnp.where(sub == 0, dca, jnp.where(sub == 1, dcb, 0.0)).T
            dc_ref[krows, :] += (jnp.where(lane == 2 * pair, -dcs[:, 0:1], 0.0)
                                 + jnp.where(lane == 2 * pair + 1, -dcs[:, 1:2], 0.0))
        for qi in range(nq):
            rows = slice(qi * ta, (qi + 1) * ta)
            dqkv_ref[rows, 0:LANES] = (dq_scr[qi] * 0.125).astype(BF16)
            rq = rs_scr[qi]
            dc_ref[rows, :] += (jnp.where(lane == 2 * pair, rq[:, 0:1], 0.0)
                                + jnp.where(lane == 2 * pair + 1, rq[:, 1:2], 0.0))

    blk = lambda w: pl.BlockSpec((None, s, w), lambda i, p: (i, 0, p))
    rows5 = pl.BlockSpec((None, None, 2, nq, ta), lambda i, p: (i, p, 0, 0, 0))
    by_rows = lambda: pltpu.VMEM((2, nq, ta, LANES), BF16)
    by_cols = lambda: pltpu.VMEM((2, nq, LANES, ta), BF16)
    return pl.pallas_call(
        body, name="attn_bwd", grid=(b, HEAD_PAIRS),
        in_specs=[blk(3 * LANES), blk(LANES), blk(LANES), rows5, rows5, blk(LANES)],
        out_specs=[blk(3 * LANES), pl.BlockSpec((None, s, LANES), lambda i, p: (i, 0, 0))],
        out_shape=[jax.ShapeDtypeStruct((b, s, 3 * D_MODEL), BF16), jax.ShapeDtypeStruct((b, s, LANES), F32)],
        scratch_shapes=[by_rows(), by_rows(), by_cols(), by_cols(), by_cols(), by_cols(),
                        pltpu.VMEM((nq, ta, LANES), F32), pltpu.VMEM((nq, ta, LANES), F32)],
        compiler_params=_cparams(("parallel", "arbitrary")),
    )(qkv3, do3, y3, lse5, crow5, cexp3)


def _shifted(v, ks, rows, s):
    return [jnp.where(rows >= k, pltpu.roll(v, k, 0), 0.0) if k > 0
            else jnp.where(rows < s + k, pltpu.roll(v, s + k, 0), 0.0) for k in ks]


def _rnn_common(xr, cw_ref, cb_ref, bda_ref, bdx_ref, ba_ref, bx_ref, lam_ref, s):
    rows = _iota((s, LANES), 0)
    x1, x2, x3 = _shifted(xr, (1, 2, 3), rows, s)
    xc = cb_ref[...] + cw_ref[0:1, :] * x3
    xc = xc + cw_ref[1:2, :] * x2
    xc = xc + cw_ref[2:3, :] * x1
    xc = xc + cw_ref[3:4, :] * xr
    xcb = xc.astype(BF16)
    r = _sigmoid(_dot(xcb, bda_ref[...]) + ba_ref[...])
    i = _sigmoid(_dot(xcb, bdx_ref[...]) + bx_ref[...])
    sp = _softplus(-lam_ref[...])
    log_a = (-RG_C * r) * sp
    a = jnp.exp(log_a)
    a2 = a * a
    sq = jnp.sqrt(jnp.maximum(_one_minus_exp(log_a + log_a, a2), 0.0))
    return rows, (x1, x2, x3), xc, xcb, r, i, sp, a, a2, sq


def _scan_down(a, u, rows, s, s1, s2):
    low = rows & 7
    for sh in (1, 2, 4):
        keep = low >= sh
        u = u + a * jnp.where(keep, pltpu.roll(u, sh, 0), 0.0)
        a = a * jnp.where(keep, pltpu.roll(a, sh, 0), 1.0)
    ng = s // 8
    s1[...] = a
    s2[...] = u
    at = s1[pl.ds(7, ng, stride=8), :]
    ut = s2[pl.ds(7, ng, stride=8), :]
    grow = _iota((ng, LANES), 0)
    sh = 1
    while sh < ng:
        keep = grow >= sh
        ut = ut + at * jnp.where(keep, pltpu.roll(ut, sh, 0), 0.0)
        if sh * 2 < ng:
            at = at * jnp.where(keep, pltpu.roll(at, sh, 0), 1.0)
        sh *= 2
    h_in = jnp.where(grow >= 1, pltpu.roll(ut, 1, 0), 0.0)
    for k in range(8):
        s1[pl.ds(k, ng, stride=8), :] = h_in
    return u + a * s1[...]


def _scan_up(a, g, rows, s, s1, s2):
    low = rows & 7
    for sh in (1, 2, 4):
        keep = low < 8 - sh
        g = g + a * jnp.where(keep, pltpu.roll(g, s - sh, 0), 0.0)
        a = a * jnp.where(keep, pltpu.roll(a, s - sh, 0), 1.0)
    ng = s // 8
    s1[...] = a
    s2[...] = g
    at = s1[pl.ds(0, ng, stride=8), :]
    gt = s2[pl.ds(0, ng, stride=8), :]
    grow = _iota((ng, LANES), 0)
    sh = 1
    while sh < ng:
        keep = grow < ng - sh
        gt = gt + at * jnp.where(keep, pltpu.roll(gt, ng - sh, 0), 0.0)
        if sh * 2 < ng:
            at = at * jnp.where(keep, pltpu.roll(at, ng - sh, 0), 1.0)
        sh *= 2
    g_in = jnp.where(grow < ng - 1, pltpu.roll(gt, ng - 1, 0), 0.0)
    for k in range(8):
        s1[pl.ds(k, ng, stride=8), :] = g_in
    return g + a * s1[...]


def _rnn_specs(s):
    blk = lambda off: pl.BlockSpec((None, s, LANES), lambda cb, i: (i, 0, off + cb))
    vec = lambda r: pl.BlockSpec((r, LANES), lambda cb, i: (0, cb))
    mat = pl.BlockSpec((None, LANES, LANES), lambda cb, i: (cb, 0, 0))
    return blk, vec, mat


def _rnn_fwd(zrest3, conv_w, conv_b, bda, bdx, ba, bx, lam):
    b, s, _ = zrest3.shape

    def body(xr_ref, g_ref, cw_ref, cb_ref, bda_ref, bdx_ref, ba_ref, bx_ref, lam_ref, h_ref, gr_ref, s1, s2):
        xr = xr_ref[...].astype(F32)
        rows, _, xc, _, _, i, _, a, _, sq = _rnn_common(
            xr, cw_ref, cb_ref, bda_ref, bdx_ref, ba_ref, bx_ref, lam_ref, s)
        h = _scan_down(a, sq * (i * xc), rows, s, s1, s2)
        h_ref[...] = h
        g = g_ref[...].astype(F32)
        gr_ref[...] = (h * (g * _sigmoid(g))).astype(BF16)

    blk, vec, mat = _rnn_specs(s)
    return pl.pallas_call(
        body, name="rnn_fwd", grid=(N_CBLK, b),
        in_specs=[blk(N_CBLK), blk(2 * N_CBLK), vec(CONV_W), vec(1), mat, mat, vec(1), vec(1), vec(1)],
        out_specs=[blk(0), blk(0)],
        out_shape=[jax.ShapeDtypeStruct((b, s, D_MODEL), F32), jax.ShapeDtypeStruct((b, s, D_MODEL), BF16)],
        scratch_shapes=[pltpu.VMEM((s, LANES), F32), pltpu.VMEM((s, LANES), F32)],
        compiler_params=_cparams(("parallel", "parallel")),
    )(zrest3, zrest3, conv_w, conv_b, bda, bdx, ba, bx, lam)


def _rnn_bwd(zrest3, h3, dh3, conv_w, conv_b, bda, bdx, ba, bx, lam):
    b, s, _ = zrest3.shape

    def body(xr_ref, h_ref, dh_ref, cw_ref, cb_ref, bda_ref, bdx_ref, ba_ref, bx_ref, lam_ref,
             dxr_ref, pv_ref, dbd_ref, s1, s2):
        @pl.when(pl.program_id(1) == 0)
        def _():
            pv_ref[...] = jnp.zeros_like(pv_ref)
            dbd_ref[...] = jnp.zeros_like(dbd_ref)

        xr = xr_ref[...].astype(F32)
        rows, (x1, x2, x3), xc, xcb, r, i, sp, a, a2, sq = _rnn_common(
            xr, cw_ref, cb_ref, bda_ref, bdx_ref, ba_ref, bx_ref, lam_ref, s)
        (a_next,) = _shifted(a, (-1,), rows, s)
        g = _scan_up(a_next, dh_ref[...], rows, s, s1, s2)
        (hp,) = _shifted(h_ref[...], (1,), rows, s)
        da = g * hp
        dsq = g * (i * xc)
        di = g * (sq * xc)
        dxc = g * (sq * i)
        dlog = da * a - dsq * (a2 / sq)
        dr = dlog * (-RG_C * sp)
        dpr = dr * (r * (1.0 - r))
        dpi = di * (i * (1.0 - i))
        dprb = dpr.astype(BF16)
        dpib = dpi.astype(BF16)
        dxc = dxc + _dot_nt(dprb, bda_ref[...]) + _dot_nt(dpib, bdx_ref[...])

        up1, up2, up3 = _shifted(dxc, (-1, -2, -3), rows, s)
        dxr = cw_ref[3:4, :] * dxc + cw_ref[2:3, :] * up1 + cw_ref[1:2, :] * up2 + cw_ref[0:1, :] * up3
        dxr_ref[...] = dxr.astype(BF16)

        def colsum(v):
            return jnp.sum(v, axis=0, keepdims=True)

        pv_ref[0:1, :] += colsum(dxc * x3)
        pv_ref[1:2, :] += colsum(dxc * x2)
        pv_ref[2:3, :] += colsum(dxc * x1)
        pv_ref[3:4, :] += colsum(dxc * xr)
        pv_ref[4:5, :] += colsum(dxc)
        pv_ref[5:6, :] += colsum(dpr)
        pv_ref[6:7, :] += colsum(dpi)
        pv_ref[7:8, :] += colsum(dlog * r) * (RG_C * _sigmoid(-lam_ref[...]))
        dbd_ref[0] += _dot_tn(xcb, dprb)
        dbd_ref[1] += _dot_tn(xcb, dpib)

    blk, vec, mat = _rnn_specs(s)
    hblk = pl.BlockSpec((None, s, LANES), lambda cb, i: (i, 0, cb))
    return pl.pallas_call(
        body, name="rnn_bwd", grid=(N_CBLK, b),
        in_specs=[blk(N_CBLK), hblk, hblk, vec(CONV_W), vec(1), mat, mat, vec(1), vec(1), vec(1)],
        out_specs=[hblk, pl.BlockSpec((8, LANES), lambda cb, i: (0, cb)),
                   pl.BlockSpec((None, 2, LANES, LANES), lambda cb, i: (cb, 0, 0, 0))],
        out_shape=[jax.ShapeDtypeStruct((b, s, D_MODEL), BF16), jax.ShapeDtypeStruct((8, D_MODEL), F32),
                   jax.ShapeDtypeStruct((N_CBLK, 2, LANES, LANES), F32)],
        scratch_shapes=[pltpu.VMEM((s, LANES), F32), pltpu.VMEM((s, LANES), F32)],
        compiler_params=_cparams(("parallel", "arbitrary")),
    )(zrest3, h3, dh3, conv_w, conv_b, bda, bdx, ba, bx, lam)


def _branch_merge(ga, gr, wa, wr, zrest):
    t = ga.shape[0]
    tm = min(512, t)
    tn = D_MODEL

    def body(ga_ref, gr_ref, wa_ref, wr_ref, mga_ref, mgr_ref, ya_ref, yr_ref, m_ref):
        ya = _dot(ga_ref[...], wa_ref[...])
        yr = _dot(gr_ref[...], wr_ref[...])
        ya_ref[...] = ya.astype(BF16)
        yr_ref[...] = yr.astype(BF16)
        m_ref[...] = (_sigmoid(mga_ref[...].astype(F32)) * ya + _sigmoid(mgr_ref[...].astype(F32)) * yr).astype(BF16)

    nj = D_MODEL // tn
    act = pl.BlockSpec((tm, D_MODEL), lambda i, j: (i, 0))
    wgt = pl.BlockSpec((D_MODEL, tn), lambda i, j: (0, j))
    out = pl.BlockSpec((tm, tn), lambda i, j: (i, j))
    return pl.pallas_call(
        body, name="branch_merge", grid=(t // tm, nj),
        in_specs=[act, act, wgt, wgt, pl.BlockSpec((tm, tn), lambda i, j: (i, 3 * nj + j)),
                  pl.BlockSpec((tm, tn), lambda i, j: (i, 4 * nj + j))],
        out_specs=[out, out, out],
        out_shape=[jax.ShapeDtypeStruct((t, D_MODEL), BF16), jax.ShapeDtypeStruct((t, D_MODEL), BF16),
                   jax.ShapeDtypeStruct((t, D_MODEL), BF16)],
        compiler_params=_cparams(("parallel", "parallel")),
    )(ga, gr, wa, wr, zrest, zrest)


def _out_loss(m, wout, x2, tgt2, wpost):
    t = m.shape[0]
    tm = min(256, t)

    def body(m_ref, w_ref, x_ref, t_ref, wp_ref, dy_ref, do_ref, acc_ref):
        @pl.when(pl.program_id(0) == 0)
        def _():
            acc_ref[...] = jnp.zeros_like(acc_ref)

        o = _dot(m_ref[...], w_ref[...])
        r2 = lax.rsqrt(jnp.mean(o * o, axis=-1, keepdims=True) + NORM_EPS)
        n = o * r2
        wp = wp_ref[...]
        err = (x_ref[...] + n * wp) - t_ref[...]
        dy = err * (1.0 / D_MODEL)
        dn = dy * wp
        do = r2 * (dn - n * jnp.mean(dn * n, axis=-1, keepdims=True))
        dy_ref[...] = dy
        do_ref[...] = do.astype(BF16)
        acc_ref[0:1, :] += jnp.sum(dy * n, axis=0, keepdims=True)
        acc_ref[1:2, :] += jnp.sum(err * err, axis=0, keepdims=True)

    row = pl.BlockSpec((tm, D_MODEL), lambda i: (i, 0))
    return pl.pallas_call(
        body, name="out_loss", grid=(t // tm,),
        in_specs=[row, pl.BlockSpec((D_MODEL, D_MODEL), lambda i: (0, 0)), row, row,
                  pl.BlockSpec((1, D_MODEL), lambda i: (0, 0))],
        out_specs=[row, row, pl.BlockSpec((8, D_MODEL), lambda i: (0, 0))],
        out_shape=[jax.ShapeDtypeStruct((t, D_MODEL), F32), jax.ShapeDtypeStruct((t, D_MODEL), BF16),
                   jax.ShapeDtypeStruct((8, D_MODEL), F32)],
        compiler_params=_cparams(("arbitrary",)),
    )(m, wout, x2, tgt2, wpost)


def _merge_bwd(do, wout, zrest, ya, yr):
    t = do.shape[0]
    tm = min(512, t)
    tn = D_MODEL
    nj = D_MODEL // tn

    def body(do_ref, w_ref, mga_ref, mgr_ref, ya_ref, yr_ref, dya_ref, dyr_ref, dmga_ref, dmgr_ref):
        dm = _dot_nt(do_ref[...], w_ref[...])
        sa = _sigmoid(mga_ref[...].astype(F32))
        sr = _sigmoid(mgr_ref[...].astype(F32))
        dya_ref[...] = (dm * sa).astype(BF16)
        dyr_ref[...] = (dm * sr).astype(BF16)
        dmga_ref[...] = (dm * ya_ref[...].astype(F32) * (sa * (1.0 - sa))).astype(BF16)
        dmgr_ref[...] = (dm * yr_ref[...].astype(F32) * (sr * (1.0 - sr))).astype(BF16)

    out = pl.BlockSpec((tm, tn), lambda i, j: (i, j))
    bf = jax.ShapeDtypeStruct((t, D_MODEL), BF16)
    return pl.pallas_call(
        body, name="merge_bwd", grid=(t // tm, nj),
        in_specs=[pl.BlockSpec((tm, D_MODEL), lambda i, j: (i, 0)), pl.BlockSpec((tn, D_MODEL), lambda i, j: (j, 0)),
                  pl.BlockSpec((tm, tn), lambda i, j: (i, 3 * nj + j)),
                  pl.BlockSpec((tm, tn), lambda i, j: (i, 4 * nj + j)), out, out],
        out_specs=[out, out, out, out],
        out_shape=[bf, bf, bf, bf],
        compiler_params=_cparams(("parallel", "parallel")),
    )(do, wout, zrest, zrest, ya, yr)


def _branch_bwd(dya, dyr, wa, wr, zrest, yatt, ylru):
    t = dya.shape[0]
    tm = min(512, t)
    tn = D_MODEL
    nj = D_MODEL // tn

    def body(dya_ref, dyr_ref, wa_ref, wr_ref, ga_ref, gr_ref, ya_ref, yl_ref,
             dyatt_ref, dga_ref, dyl_ref, dgr_ref):
        dga = _dot_nt(dya_ref[...], wa_ref[...])
        dgr = _dot_nt(dyr_ref[...], wr_ref[...])
        g = ga_ref[...].astype(F32)
        sg = _sigmoid(g)
        dyatt_ref[...] = (dga * (g * sg)).astype(BF16)
        dga_ref[...] = (dga * ya_ref[...] * (sg * (1.0 + g * (1.0 - sg)))).astype(BF16)
        g = gr_ref[...].astype(F32)
        sg = _sigmoid(g)
        dyl_ref[...] = dgr * (g * sg)
        dgr_ref[...] = (dgr * yl_ref[...] * (sg * (1.0 + g * (1.0 - sg)))).astype(BF16)

    act = pl.BlockSpec((tm, D_MODEL), lambda i, j: (i, 0))
    wgt = pl.BlockSpec((tn, D_MODEL), lambda i, j: (j, 0))
    out = pl.BlockSpec((tm, tn), lambda i, j: (i, j))
    bf = jax.ShapeDtypeStruct((t, D_MODEL), BF16)
    return pl.pallas_call(
        body, name="branch_bwd", grid=(t // tm, nj),
        in_specs=[act, act, wgt, wgt, pl.BlockSpec((tm, tn), lambda i, j: (i, j)),
                  pl.BlockSpec((tm, tn), lambda i, j: (i, 2 * nj + j)), out, out],
        out_specs=[out, out, out, out],
        out_shape=[bf, bf, jax.ShapeDtypeStruct((t, D_MODEL), F32), bf],
        compiler_params=_cparams(("parallel", "parallel")),
    )(dya, dyr, wa, wr, zrest, zrest, yatt, ylru)


def _dh_partial(parts, after, name):
    t = parts[0][0].shape[0]
    tm = min(256, t)
    np_ = len(parts)

    def body(*refs):
        o_ref = refs[-1]
        acc = _dot(refs[0][...], refs[np_][...])
        for p in range(1, np_):
            acc = acc + _dot(refs[p][...], refs[np_ + p][...])
        o_ref[...] = acc

    in_specs = [pl.BlockSpec((tm, dz.shape[1]), lambda i: (i, 0)) for dz, _ in parts]
    in_specs += [pl.BlockSpec(w.shape, lambda i: (0, 0)) for _, w in parts]
    in_specs += [pl.BlockSpec(after.shape, lambda i: (0, 0))]
    return pl.pallas_call(
        body, name=name, grid=(t // tm,),
        in_specs=in_specs,
        out_specs=pl.BlockSpec((tm, D_MODEL), lambda i: (i, 0)),
        out_shape=jax.ShapeDtypeStruct((t, D_MODEL), F32),
        compiler_params=_cparams(("parallel",), vmem_mb=48),
    )(*[dz for dz, _ in parts], *[w for _, w in parts], after)


def _dh_final(parts, acc_in, x2, dy, wpre):
    t = x2.shape[0]
    tm = min(256, t)
    np_ = len(parts)

    def body(*refs):
        acc_ref, x_ref, dy_ref, w_ref = refs[2 * np_:2 * np_ + 4]
        gx_ref, pw_ref = refs[2 * np_ + 4:]

        @pl.when(pl.program_id(0) == 0)
        def _():
            pw_ref[...] = jnp.zeros_like(pw_ref)

        dh = acc_ref[...]
        for p in range(np_):
            dh = dh + _dot(refs[p][...], refs[np_ + p][...])
        x = x_ref[...]
        r = lax.rsqrt(jnp.mean(x * x, axis=-1, keepdims=True) + NORM_EPS)
        xn = x * r
        dxn = dh * w_ref[...]
        gx_ref[...] = r * (dxn - xn * jnp.mean(dxn * xn, axis=-1, keepdims=True)) + dy_ref[...]
        pw_ref[0:1, :] += jnp.sum(dh * xn, axis=0, keepdims=True)

    row = pl.BlockSpec((tm, D_MODEL), lambda i: (i, 0))
    in_specs = [pl.BlockSpec((tm, dz.shape[1]), lambda i: (i, 0)) for dz, _ in parts]
    in_specs += [pl.BlockSpec(w.shape, lambda i: (0, 0)) for _, w in parts]
    in_specs += [row, row, row, pl.BlockSpec((1, D_MODEL), lambda i: (0, 0))]
    return pl.pallas_call(
        body, name="dh_final", grid=(t // tm,),
        in_specs=in_specs,
        out_specs=[row, pl.BlockSpec((8, D_MODEL), lambda i: (0, 0))],
        out_shape=[jax.ShapeDtypeStruct((t, D_MODEL), F32), jax.ShapeDtypeStruct((8, D_MODEL), F32)],
        compiler_params=_cparams(("arbitrary",), vmem_mb=48),
    )(*[dz for dz, _ in parts], *[w for _, w in parts], acc_in, x2, dy, wpre)


def _adamw(w, g, m, v):
    m = ADAM_B1 * m + (1.0 - ADAM_B1) * g
    v = ADAM_B2 * v + (1.0 - ADAM_B2) * (g * g)
    m_hat = m / (1.0 - ADAM_B1 ** ADAM_STEP)
    v_hat = v / (1.0 - ADAM_B2 ** ADAM_STEP)
    delta = -ADAM_LR * (m_hat / (jnp.sqrt(v_hat) + ADAM_EPS) + ADAM_WD * w)
    return delta, m, v


def _reduce_adamw(own, parts, place, w, m, v, name):
    r, c = w.shape
    blk, nblk, at = _blocks_2d(r, c)

    def body(place_ref, own_ref, p_ref, w_ref, m_ref, v_ref, g_ref, d_ref, nm_ref, nv_ref):
        mine = place_ref[1]
        own_blk = own_ref[...]
        g = jnp.where(mine == 0, own_blk, p_ref[0].astype(F32))
        for j in range(1, N_CHIPS):
            g = g + jnp.where(mine == j, own_blk, p_ref[j].astype(F32))
        d, nm, nv = _adamw(w_ref[...], g, m_ref[...], v_ref[...])
        g_ref[...] = g
        d_ref[...] = d
        nm_ref[...] = nm
        nv_ref[...] = nv

    row = pl.BlockSpec(blk, lambda i, pr: at(i))
    sh = jax.ShapeDtypeStruct((r, c), F32)
    grid_spec = pltpu.PrefetchScalarGridSpec(
        num_scalar_prefetch=1, grid=(nblk,),
        in_specs=[row, pl.BlockSpec((N_CHIPS,) + blk, lambda i, pr: (0,) + at(i)), row, row, row],
        out_specs=[row, row, row, row])
    return pl.pallas_call(
        body, name=name, grid_spec=grid_spec, out_shape=[sh, sh, sh, sh],
        compiler_params=_cparams(("parallel",)),
    )(place, own, parts, w, m, v)


def _interleave_qkv(a):
    lead = a.shape[:-1]
    return a.reshape(lead + (3, HEAD_PAIRS, LANES)).swapaxes(-3, -2).reshape(lead + (3 * D_MODEL,))


def _deinterleave_qkv(a):
    lead = a.shape[:-1]
    return a.reshape(lead + (HEAD_PAIRS, 3, LANES)).swapaxes(-3, -2).reshape(lead + (3 * D_MODEL,))


def _interleave_rows(a):
    return a.reshape(3, HEAD_PAIRS, LANES, a.shape[1]).swapaxes(0, 1).reshape(a.shape)


def _deinterleave_rows(a):
    return a.reshape(HEAD_PAIRS, 3, LANES, a.shape[1]).swapaxes(0, 1).reshape(a.shape)


def _pack_small(pre, conv_b, rg_ba, rg_bx, lam, post, loss_row, b_in, conv_w_full, rg_wa, rg_wx):
    z = jnp.zeros((1, D_MODEL), F32)
    b_used = jnp.concatenate([b_in[:, 0:3 * D_MODEL], b_in[:, 3 * D_MODEL + HEADS:IN_TOTAL]], axis=1)
    b_f = jnp.pad(b_in[:, 3 * D_MODEL:3 * D_MODEL + HEADS], ((0, 0), (0, D_MODEL - HEADS)))
    return jnp.concatenate([
        pre, conv_b, rg_ba, rg_bx, lam, post, loss_row, z,
        b_used.reshape(9, D_MODEL), b_f, conv_w_full, z, z,
        rg_wa.reshape(64, D_MODEL), rg_wx.reshape(64, D_MODEL)], axis=0)


def _unpack_small(p):
    b_used = p[8:17].reshape(1, 9 * D_MODEL)
    b_in = jnp.concatenate([b_used[:, 0:3 * D_MODEL], p[17:18, 0:HEADS], b_used[:, 3 * D_MODEL:]], axis=1)
    return dict(pre_norm_w=p[0:1], conv_b=p[1:2], rg_ba=p[2:3], rg_bx=p[3:4], rg_lambda=p[4:5],
                post_norm_w=p[5:6], loss_row=p[6:7], b_in=b_in, conv_w_full=p[18:22],
                rg_wa=p[24:88].reshape(1, 16, 64, 64), rg_wx=p[88:152].reshape(1, 16, 64, 64))


def _reduce_small(parts, w, m, v):
    def body(p_ref, w_ref, m_ref, v_ref, g_ref, d_ref, nm_ref, nv_ref):
        g = p_ref[0]
        for j in range(1, N_DEV):
            g = g + p_ref[j]
        d, nm, nv = _adamw(w_ref[...], g, m_ref[...], v_ref[...])
        g_ref[...] = g
        d_ref[...] = d
        nm_ref[...] = nm
        nv_ref[...] = nv

    sh = jax.ShapeDtypeStruct((SMALL_ROWS, D_MODEL), F32)
    return pl.pallas_call(body, name="reduce_small", out_shape=[sh, sh, sh, sh])(parts, w, m, v)


def kernel(x, pre_norm_w, w_in, b_in, conv_w, conv_b, rg_wa, rg_ba, rg_wx, rg_bx, rg_lambda, w_branch_a, w_branch_r, w_out, post_norm_w, loss_target, m_pre_norm_w, m_w_in, m_b_in, m_conv_w, m_conv_b, m_rg_wa, m_rg_ba, m_rg_wx, m_rg_bx, m_rg_lambda, m_w_branch_a, m_w_branch_r, m_w_out, m_post_norm_w, v_pre_norm_w, v_w_in, v_b_in, v_conv_w, v_conv_b, v_rg_wa, v_rg_ba, v_rg_wx, v_rg_bx, v_rg_lambda, v_w_branch_a, v_w_branch_r, v_w_out, v_post_norm_w):
    b, s, _ = x.shape
    t = b * s
    me = 4 * lax.axis_index("x") + 2 * lax.axis_index("y") + lax.axis_index("c")
    shard_rows = D_MODEL // N_DEV

    place = jnp.stack([lax.axis_index("c"), 2 * lax.axis_index("x") + lax.axis_index("y")]).astype(jnp.int32)
    w_in_all = _gather(w_in[0].T.astype(BF16), "gather_w_in")
    wt_full = w_in_all.reshape(IN_TOTAL, D_MODEL)
    conv_terms = jnp.concatenate(_split3(conv_w[0]), axis=0)
    conv_pad = jnp.pad(conv_terms, ((0, 16 - 3 * CONV_W), (0, D_MODEL - LANES)))
    sq_stack = jnp.concatenate([w_branch_a[0].astype(BF16), w_branch_r[0].astype(BF16), w_out[0].astype(BF16),
                                conv_pad], axis=0)
    sq_sems, sq_src, sq_land, sq_token = _gather_start(sq_stack, w_in_all, "gather_w_sq_start")

    w_qkv = _interleave_rows(wt_full[0:3 * D_MODEL])
    w_f = jnp.pad(wt_full[3 * D_MODEL:3 * D_MODEL + HEADS], ((0, LANES - HEADS), (0, 0)))
    w_rest = wt_full[3 * D_MODEL + HEADS:IN_USED]
    b_qkv = _interleave_qkv(b_in[:, 0:3 * D_MODEL]) + sq_token[0, 0]
    b_f = jnp.pad(b_in[:, 3 * D_MODEL:3 * D_MODEL + HEADS], ((0, 0), (0, LANES - HEADS)))
    b_rest = b_in[:, 3 * D_MODEL + HEADS:IN_USED]

    def blockdiag(w):
        w2 = w.reshape(N_CBLK, 2, HEAD_DIM, HEAD_DIM)
        zz = jnp.zeros((N_CBLK, HEAD_DIM, HEAD_DIM), w.dtype)
        top = jnp.concatenate([w2[:, 0], zz], axis=2)
        bot = jnp.concatenate([zz, w2[:, 1]], axis=2)
        return jnp.concatenate([top, bot], axis=1).astype(BF16)

    bda, bdx = blockdiag(rg_wa[0]), blockdiag(rg_wx[0])

    x2 = x.reshape(t, D_MODEL)
    tgt2 = loss_target.reshape(t, D_MODEL)
    h = _prenorm(x2, pre_norm_w)
    qkv = _mm_bias(h, w_qkv, b_qkv, BF16, "inproj_qkv")
    zrest = _mm_bias(h, w_rest, b_rest, BF16, "inproj_rest")
    zf = _mm_bias(h, w_f, b_f, F32, "inproj_f")
    qkv3 = qkv.reshape(b, s, 3 * D_MODEL)
    zrest3 = zrest.reshape(b, s, 5 * D_MODEL)
    zf3 = zf.reshape(b, s, LANES)
    cexp3, crow = _fgate_fwd(zf3)
    yatt3, lse, ga3 = _attn_fwd(qkv3, cexp3, crow, zrest3)

    sq_all = _gather_wait(sq_sems, sq_src, sq_land, ga3, "gather_w_sq_wait")
    sq_all = lax.dynamic_update_slice(sq_all, sq_stack[None], (me, 0, 0))
    wa = sq_all[:, 0:shard_rows].reshape(D_MODEL, D_MODEL)
    wr = sq_all[:, shard_rows:2 * shard_rows].reshape(D_MODEL, D_MODEL)
    wo = sq_all[:, 2 * shard_rows:3 * shard_rows].reshape(D_MODEL, D_MODEL)
    conv_all = sq_all[:, 3 * shard_rows:3 * shard_rows + 3 * CONV_W, 0:LANES].astype(F32)
    conv_all = (conv_all[:, 0:CONV_W] + conv_all[:, CONV_W:2 * CONV_W]) + conv_all[:, 2 * CONV_W:3 * CONV_W]
    conv_full = conv_all.transpose(1, 0, 2).reshape(CONV_W, D_MODEL)

    ylru3, gr3 = _rnn_fwd(zrest3, conv_full, conv_b, bda, bdx, rg_ba, rg_bx, rg_lambda)
    ga, gr = ga3.reshape(t, D_MODEL), gr3.reshape(t, D_MODEL)
    ya, yr, mm = _branch_merge(ga, gr, wa, wr, zrest)
    dy, do, acc_out = _out_loss(mm, wo, x2, tgt2, post_norm_w)

    dya, dyr, dz_mga, dz_mgr = _merge_bwd(do, wo, zrest, ya, yr)
    dyatt, dz_ga, dylru, dz_gr = _branch_bwd(dya, dyr, wa, wr, zrest, yatt3.reshape(t, D_MODEL),
                                             ylru3.reshape(t, D_MODEL))
    dz_xr3, pvec, dbd = _rnn_bwd(zrest3, ylru3, dylru.reshape(b, s, D_MODEL), conv_full, conv_b, bda, bdx,
                                 rg_ba, rg_bx, rg_lambda)
    dqkv3, dc3 = _attn_bwd(qkv3, dyatt.reshape(b, s, D_MODEL), yatt3, lse, crow, cexp3)
    dz_f = _fgate_bwd(dc3, zf3).reshape(t, LANES)
    dz_qkv = dqkv3.reshape(t, 3 * D_MODEL)
    dz_xr = dz_xr3.reshape(t, D_MODEL)

    dw_qkv, db_qkv = _mm_tn(dz_qkv, h, "dw_qkv")
    dw_f, db_f = _mm_tn(dz_f, h, "dw_f")
    dw_parts, db_parts = [], []
    for nm, dzp in (("ga", dz_ga), ("xr", dz_xr), ("gr", dz_gr), ("mga", dz_mga), ("mgr", dz_mgr)):
        dwp, dbp = _mm_tn(dzp, h, "dw_" + nm)
        dw_parts.append(dwp)
        db_parts.append(dbp[0:1])
    dw_a, _ = _mm_tn(ga, dya, "dw_a")
    dw_r, _ = _mm_tn(gr, dyr, "dw_r")
    dw_o, _ = _mm_tn(mm, do, "dw_o")

    zeros_tail = jnp.zeros((IN_TOTAL - IN_USED, D_MODEL), F32)
    dwt_full = jnp.concatenate([_deinterleave_rows(dw_qkv), dw_f[0:HEADS]] + dw_parts + [zeros_tail], axis=0)
    dw_in_send = dwt_full.reshape(N_CHIPS, 2, W_SHARD, D_MODEL).transpose(1, 0, 2, 3)
    by_dest = lambda a: a.reshape(N_CHIPS, 2, shard_rows, D_MODEL).transpose(1, 0, 2, 3)
    dw_sq_send = jnp.concatenate([by_dest(dw_a), by_dest(dw_r), by_dest(dw_o)], axis=2)

    sib_in, sib_sq = _swap_with_sibling([dw_in_send, dw_sq_send], "swap_dw")
    chip_in, own_in = _pair_add(dw_in_send, sib_in, place, "pair_add_in")
    chip_sq, own_sq = _pair_add(dw_sq_send, sib_sq, place, "pair_add_sq")
    sems, sent, lands, token = _exchange_chips_start([chip_in, chip_sq], "exchange_dw_start")

    wt = lambda lo: w_rest[lo * D_MODEL:(lo + 1) * D_MODEL]
    dh_a = _dh_partial([(dz_qkv, w_qkv), (dz_f, w_f)], token, "dh_qkv")
    grad_x2, acc_pre = _dh_final(
        [(dz_ga, wt(0)), (dz_xr, wt(1)), (dz_gr, wt(2)), (dz_mga, wt(3)), (dz_mgr, wt(4))],
        dh_a, x2, dy, pre_norm_w)

    db_in_full = jnp.concatenate([_deinterleave_qkv(db_qkv[0:1]), db_f[0:1, 0:HEADS]] + db_parts
                                 + [jnp.zeros((1, IN_TOTAL - IN_USED), F32)], axis=1)
    d_rg_wa = jnp.stack([dbd[:, 0, 0:HEAD_DIM, 0:HEAD_DIM], dbd[:, 0, HEAD_DIM:, HEAD_DIM:]], axis=1)
    d_rg_wx = jnp.stack([dbd[:, 1, 0:HEAD_DIM, 0:HEAD_DIM], dbd[:, 1, HEAD_DIM:, HEAD_DIM:]], axis=1)
    small_g = _pack_small(acc_pre[0:1], pvec[4:5], pvec[5:6], pvec[6:7], pvec[7:8], acc_out[0:1], acc_out[1:2],
                          db_in_full, pvec[0:4], d_rg_wa, d_rg_wx)
    sm_sems, sm_src, sm_land, sm_token = _gather_start(small_g, grad_x2, "gather_small_start")
    recv_in, recv_sq = _exchange_chips_wait(sems, sent, lands, sm_token, "exchange_dw_wait")

    g_in, d_in, nm_in, nv_in = [a.T for a in _reduce_adamw(
        own_in, recv_in, place, w_in[0].T, m_w_in[0].T, v_w_in[0].T, "adamw_w_in")]
    sq_w = jnp.concatenate([w_branch_a[0], w_branch_r[0], w_out[0]], axis=0)
    sq_m = jnp.concatenate([m_w_branch_a[0], m_w_branch_r[0], m_w_out[0]], axis=0)
    sq_v = jnp.concatenate([v_w_branch_a[0], v_w_branch_r[0], v_w_out[0]], axis=0)
    g_sq, d_sq, nm_sq, nv_sq = _reduce_adamw(own_sq, recv_sq, place, sq_w, sq_m, sq_v, "adamw_w_sq")
    small_all = _gather_wait(sm_sems, sm_src, sm_land, d_sq, "gather_small_wait")
    small_all = lax.dynamic_update_slice(small_all, small_g[None], (me, 0, 0))

    def place_conv(a):
        return lax.dynamic_update_slice(jnp.zeros((CONV_W, D_MODEL), F32), a[0], (0, me * LANES))

    zrow = jnp.zeros((1, D_MODEL), F32)
    small_w = _pack_small(pre_norm_w, conv_b, rg_ba, rg_bx, rg_lambda, post_norm_w, zrow, b_in,
                          place_conv(conv_w), rg_wa[0], rg_wx[0])
    small_m = _pack_small(m_pre_norm_w, m_conv_b, m_rg_ba, m_rg_bx, m_rg_lambda, m_post_norm_w, zrow, m_b_in,
                          place_conv(m_conv_w), m_rg_wa[0], m_rg_wx[0])
    small_v = _pack_small(v_pre_norm_w, v_conv_b, v_rg_ba, v_rg_bx, v_rg_lambda, v_post_norm_w, zrow, v_b_in,
                          place_conv(v_conv_w), v_rg_wa[0], v_rg_wx[0])
    outs_small = [_unpack_small(p) for p in _reduce_small(small_all, small_w, small_m, small_v)]

    loss = (0.5 / D_MODEL) * jnp.sum(outs_small[0]["loss_row"])

    def leaf(kind, name):
        if name == "w_in":
            return (g_in, d_in, nm_in, nv_in)[kind][None]
        if name in ("w_branch_a", "w_branch_r", "w_out"):
            j = ("w_branch_a", "w_branch_r", "w_out").index(name)
            return (g_sq, d_sq, nm_sq, nv_sq)[kind][None, j * shard_rows:(j + 1) * shard_rows]
        if name == "conv_w":
            return lax.dynamic_slice(outs_small[kind]["conv_w_full"], (0, me * LANES), (CONV_W, LANES))[None]
        return outs_small[kind][name]

    names = ["pre_norm_w", "w_in", "b_in", "conv_w", "conv_b", "rg_wa", "rg_ba", "rg_wx", "rg_bx", "rg_lambda",
             "w_branch_a", "w_branch_r", "w_out", "post_norm_w"]
    out = [loss, grad_x2.reshape(b, s, D_MODEL)]
    for kind in range(4):
        out += [leaf(kind, nm) for nm in names]
    return tuple(out)
```

```python
import jax
import jax.numpy as jnp
from jax import lax
from jax.experimental import pallas as pl
from jax.experimental.pallas import tpu as pltpu

F32 = jnp.float32
BF16 = jnp.bfloat16

N_DEV = 8
D_MODEL = 1024
HEADS = 16
HEAD_DIM = 64
HEAD_PAIRS = HEADS // 2
LANES = 128
N_CBLK = D_MODEL // LANES
CONV_W = 4
RG_C = 8.0
NORM_EPS = 1e-6
MASK_VALUE = -1e30
IN_USED = 8208
IN_TOTAL = 9232
W_SHARD = IN_TOTAL // N_DEV

ADAM_LR = 0.001
ADAM_B1 = 0.9
ADAM_B2 = 0.999
ADAM_EPS = 1e-08
ADAM_WD = 0.01
ADAM_STEP = 10

ATT_TILE_FWD = 256
ATT_TILE_BWD = 512
SCAN_TILE = 256
SMALL_ROWS = 152


def _cparams(sem=None, vmem_mb=None):
    kw = {}
    if sem is not None:
        kw["dimension_semantics"] = sem
    if vmem_mb is not None:
        kw["vmem_limit_bytes"] = vmem_mb * 1024 * 1024
    return pltpu.CompilerParams(**kw)


def _sigmoid(x):
    return 1.0 / (1.0 + jnp.exp(-x))


def _softplus(x):
    return jnp.maximum(x, 0.0) + jnp.log1p(jnp.exp(-jnp.abs(x)))


def _one_minus_exp(y, exp_y):
    series = -y * (1.0 + y * (1.0 / 2 + y * (1.0 / 6 + y * (1.0 / 24 + y * (1.0 / 120)))))
    return jnp.where(y > -0.0625, series, 1.0 - exp_y)


def _split3(x):
    hi = x.astype(BF16)
    r1 = x - hi.astype(F32)
    mid = r1.astype(BF16)
    lo = (r1 - mid.astype(F32)).astype(BF16)
    return hi, mid, lo


def _dot(a, b):
    return jnp.dot(a, b, preferred_element_type=F32)


def _dot_nt(a, b):
    return lax.dot_general(a, b, (((1,), (1,)), ((), ())), preferred_element_type=F32)


def _dot_tn(a, b):
    return lax.dot_general(a, b, (((0,), (0,)), ((), ())), preferred_element_type=F32)


def _iota(shape, dim):
    return lax.broadcasted_iota(jnp.int32, shape, dim)


_ANY = pl.BlockSpec(memory_space=pl.ANY)
_MESH = pl.DeviceIdType.MESH
N_CHIPS = 4


def _place():
    x, y, c = lax.axis_index("x"), lax.axis_index("y"), lax.axis_index("c")
    other_chips = [(1 - x, y), (x, 1 - y), (1 - x, 1 - y)]
    return x, y, c, other_chips


def _gather(x_shard, name):
    def body(x_ref, out_ref, send_sems, recv_sems, local_sem):
        x, y, c, chips = _place()
        me, sibling = (x, y, c), (x, y, 1 - c)

        def slot(p):
            return out_ref.at[4 * p[0] + 2 * p[1] + p[2]]

        def copy(k, block, to, src=None):
            return pltpu.make_async_remote_copy(
                src_ref=slot(block) if src is None else src, dst_ref=slot(block),
                send_sem=send_sems.at[k], recv_sem=recv_sems.at[k], device_id=to, device_id_type=_MESH)

        mine = pltpu.make_async_copy(x_ref, slot(me), local_sem)
        mine.start()
        first = [copy(0, me, sibling, src=x_ref)]
        first += [copy(1 + j, me, (*chip, c), src=x_ref) for j, chip in enumerate(chips)]
        for cp in first:
            cp.start()
        passed = [copy(4 + j, (*chip, c), sibling) for j, chip in enumerate(chips)]
        for j, chip in enumerate(chips):
            copy(1 + j, (*chip, c), me).wait_recv()
            passed[j].start()
        copy(0, sibling, me).wait_recv()
        for j, chip in enumerate(chips):
            copy(4 + j, (*chip, 1 - c), me).wait_recv()
        for cp in first + passed:
            cp.wait_send()
        mine.wait()

    return pl.pallas_call(
        body, name=name,
        out_shape=jax.ShapeDtypeStruct((N_DEV,) + tuple(x_shard.shape), x_shard.dtype),
        in_specs=[_ANY], out_specs=_ANY,
        scratch_shapes=[pltpu.SemaphoreType.DMA((7,)), pltpu.SemaphoreType.DMA((7,)), pltpu.SemaphoreType.DMA],
    )(x_shard)


def _swap_with_sibling(srcs, name):
    n = len(srcs)

    def body(*refs):
        src_refs, out_refs = refs[:n], refs[n:2 * n]
        send_sems, recv_sems = refs[2 * n:]
        x, y, c, _ = _place()
        cps = [pltpu.make_async_remote_copy(
            src_ref=src_refs[i].at[1 - c], dst_ref=out_refs[i], send_sem=send_sems.at[i], recv_sem=recv_sems.at[i],
            device_id=(x, y, 1 - c), device_id_type=_MESH) for i in range(n)]
        for cp in cps:
            cp.start()
        for cp in cps:
            cp.wait()

    return pl.pallas_call(
        body, name=name,
        out_shape=[jax.ShapeDtypeStruct(a.shape[1:], a.dtype) for a in srcs],
        in_specs=[_ANY] * n, out_specs=[_ANY] * n,
        scratch_shapes=[pltpu.SemaphoreType.DMA((n,)), pltpu.SemaphoreType.DMA((n,))],
    )(*srcs)


def _blocks_2d(r, c):
    if r % 128 == 0:
        return (128, c), r // 128, lambda i: (i, 0)
    return (r, 256), c // 256, lambda i: (0, i)


def _pair_add(src, recv, place, name):
    _, _, r, c = src.shape
    blk, nblk, at = _blocks_2d(r, c)

    def body(place_ref, a_ref, b_ref, q16_ref, own_ref):
        q = a_ref[...] + b_ref[...]
        q16_ref[...] = q.astype(BF16)

        @pl.when(pl.program_id(1) == place_ref[1])
        def _():
            own_ref[...] = q

    grid_spec = pltpu.PrefetchScalarGridSpec(
        num_scalar_prefetch=1, grid=(nblk, N_CHIPS),
        in_specs=[pl.BlockSpec((None, None) + blk, lambda i, j, pr: (pr[0], j) + at(i)),
                  pl.BlockSpec((None,) + blk, lambda i, j, pr: (j,) + at(i))],
        out_specs=[pl.BlockSpec((None,) + blk, lambda i, j, pr: (j,) + at(i)),
                   pl.BlockSpec(blk, lambda i, j, pr: at(i))])
    return pl.pallas_call(
        body, name=name, grid_spec=grid_spec,
        out_shape=[jax.ShapeDtypeStruct((N_CHIPS, r, c), BF16), jax.ShapeDtypeStruct((r, c), F32)],
        compiler_params=_cparams(("parallel", "arbitrary")),
    )(place, src, recv)


_HBM = pl.BlockSpec(memory_space=pltpu.HBM)
_SEM = pl.BlockSpec(memory_space=pltpu.SEMAPHORE)
_DATAFLOW = pltpu.SideEffectType.DATAFLOW_SIDE_EFFECTING


def _chip_copy(src_ref, land_ref, send_sem, recv_sem, k, chips, c, land):
    chip = chips[k]
    return pltpu.make_async_remote_copy(
        src_ref=src_ref.at[2 * chip[0] + chip[1]], dst_ref=land_ref.at[land],
        send_sem=send_sem, recv_sem=recv_sem, device_id=(*chip, c), device_id_type=_MESH)


def _exchange_chips_start(srcs, name):
    n = len(srcs)
    ncp = 3 * n

    def body(*refs):
        src_refs, land_refs = refs[:n], refs[n:2 * n]
        sems = refs[4 * n:4 * n + 2 * ncp]
        token = refs[-1]
        x, y, c, chips = _place()
        for i in range(n):
            for k in range(3):
                j = 3 * i + k
                _chip_copy(src_refs[i], land_refs[i], sems[j], sems[ncp + j], k, chips, c, 2 * x + y).start()
        token[...] = jnp.zeros_like(token)

    hbm = [pltpu.HBM(a.shape, a.dtype) for a in srcs]
    lands = [pltpu.with_memory_space_constraint(lax.empty(a.shape, a.dtype), pltpu.HBM) for a in srcs]
    res = pl.pallas_call(
        body, name=name,
        out_shape=(*hbm, *hbm, *([pltpu.SemaphoreType.DMA(())] * (2 * ncp)), jax.ShapeDtypeStruct((8, LANES), F32)),
        in_specs=[_HBM] * (2 * n),
        out_specs=(*([_HBM] * (2 * n)), *([_SEM] * (2 * ncp)), pl.BlockSpec(memory_space=pltpu.VMEM)),
        input_output_aliases={i: i for i in range(2 * n)},
        compiler_params=pltpu.CompilerParams(has_side_effects=_DATAFLOW),
    )(*[pltpu.with_memory_space_constraint(a, pltpu.HBM) for a in srcs], *lands)
    return list(res[2 * n:2 * n + 2 * ncp]), list(res[:n]), list(res[n:2 * n]), res[-1]


def _exchange_chips_wait(sems, srcs, lands, after, name):
    n = len(srcs)
    ncp = 3 * n

    def body(*refs):
        src_refs, land_refs = refs[:n], refs[n:2 * n]
        sem_refs = refs[2 * n:2 * n + 2 * ncp]
        x, y, c, chips = _place()
        for i in range(n):
            for k in range(3):
                j = 3 * i + k
                cp = _chip_copy(src_refs[i], land_refs[i], sem_refs[j], sem_refs[ncp + j], k, chips, c,
                                2 * chips[k][0] + chips[k][1])
                cp.wait_send()
                cp.wait_recv()

    hbm = [pltpu.HBM(a.shape, a.dtype) for a in srcs]
    res = pl.pallas_call(
        body, name=name, out_shape=(*hbm, *hbm),
        in_specs=[_HBM] * (2 * n) + [_SEM] * (2 * ncp) + [_ANY], out_specs=tuple([_HBM] * (2 * n)),
        input_output_aliases={i: i for i in range(2 * n)},
        compiler_params=pltpu.CompilerParams(has_side_effects=_DATAFLOW),
    )(*srcs, *lands, *sems, after)
    return list(res[n:2 * n])


def _peer_copy(src_ref, land_ref, send_sem, recv_sem, k, place, land):
    x, y, c = place
    peer = (1 - x if k & 4 else x, 1 - y if k & 2 else y, 1 - c if k & 1 else c)
    return pltpu.make_async_remote_copy(
        src_ref=src_ref, dst_ref=land_ref.at[land], send_sem=send_sem, recv_sem=recv_sem,
        device_id=peer, device_id_type=_MESH)


def _gather_start(x_shard, after, name):
    npeer = N_DEV - 1

    def body(x_ref, land_ref, after_ref, x_thru, land_thru, *rest):
        sems, token = rest[:2 * npeer], rest[-1]
        x, y, c, _ = _place()
        for k in range(1, N_DEV):
            _peer_copy(x_ref, land_ref, sems[k - 1], sems[npeer + k - 1], k, (x, y, c), 4 * x + 2 * y + c).start()
        token[...] = jnp.zeros_like(token)

    land = pltpu.with_memory_space_constraint(lax.empty((N_DEV,) + tuple(x_shard.shape), x_shard.dtype), pltpu.HBM)
    res = pl.pallas_call(
        body, name=name,
        out_shape=(pltpu.HBM(x_shard.shape, x_shard.dtype), pltpu.HBM(land.shape, land.dtype),
                   *([pltpu.SemaphoreType.DMA(())] * (2 * npeer)), jax.ShapeDtypeStruct((8, LANES), F32)),
        in_specs=[_HBM, _HBM, _ANY],
        out_specs=(_HBM, _HBM, *([_SEM] * (2 * npeer)), pl.BlockSpec(memory_space=pltpu.VMEM)),
        input_output_aliases={0: 0, 1: 1},
        compiler_params=pltpu.CompilerParams(has_side_effects=_DATAFLOW),
    )(pltpu.with_memory_space_constraint(x_shard, pltpu.HBM), land, after)
    return list(res[2:2 + 2 * npeer]), res[0], res[1], res[-1]


def _gather_wait(sems, src, land, after, name):
    npeer = N_DEV - 1

    def body(x_ref, land_ref, *rest):
        sem_refs = rest[:2 * npeer]
        x, y, c, _ = _place()
        for k in range(1, N_DEV):
            peer_index = (4 * x + 2 * y + c) ^ k
            cp = _peer_copy(x_ref, land_ref, sem_refs[k - 1], sem_refs[npeer + k - 1], k, (x, y, c), peer_index)
            cp.wait_send()
            cp.wait_recv()

    res = pl.pallas_call(
        body, name=name, out_shape=(pltpu.HBM(src.shape, src.dtype), pltpu.HBM(land.shape, land.dtype)),
        in_specs=[_HBM, _HBM] + [_SEM] * (2 * npeer) + [_ANY], out_specs=(_HBM, _HBM),
        input_output_aliases={0: 0, 1: 1},
        compiler_params=pltpu.CompilerParams(has_side_effects=_DATAFLOW),
    )(src, land, *sems, after)
    return res[1]


def _prenorm(x2, w):
    t = x2.shape[0]
    tm = min(512, t)

    def body(x_ref, w_ref, h_ref):
        x = x_ref[...]
        r = lax.rsqrt(jnp.mean(x * x, axis=-1, keepdims=True) + NORM_EPS)
        h_ref[...] = (x * r * w_ref[...]).astype(BF16)

    return pl.pallas_call(
        body, name="prenorm", grid=(t // tm,),
        in_specs=[pl.BlockSpec((tm, D_MODEL), lambda i: (i, 0)), pl.BlockSpec((1, D_MODEL), lambda i: (0, 0))],
        out_specs=pl.BlockSpec((tm, D_MODEL), lambda i: (i, 0)),
        out_shape=jax.ShapeDtypeStruct((t, D_MODEL), BF16),
        compiler_params=_cparams(("parallel",)),
    )(x2, w)


def _mm_bias(a, bt, bias, out_dtype, name):
    m, k = a.shape
    n = bt.shape[0]
    tm = min(512, m)
    tn = min(1024, n)

    def body(a_ref, bt_ref, bias_ref, o_ref):
        aa = a_ref[...]
        for j in range(n // tn):
            cols = slice(j * tn, (j + 1) * tn)
            o_ref[:, cols] = (_dot_nt(aa, bt_ref[cols, :]) + bias_ref[:, cols]).astype(o_ref.dtype)

    return pl.pallas_call(
        body, name=name, grid=(m // tm,),
        in_specs=[pl.BlockSpec((tm, k), lambda i: (i, 0)), pl.BlockSpec((n, k), lambda i: (0, 0)),
                  pl.BlockSpec((1, n), lambda i: (0, 0))],
        out_specs=pl.BlockSpec((tm, n), lambda i: (i, 0)),
        out_shape=jax.ShapeDtypeStruct((m, n), out_dtype),
        compiler_params=_cparams(("parallel",), vmem_mb=48),
    )(a, bt, bias)


def _mm_tn(a, b, name):
    t, m = a.shape
    n = b.shape[1]
    tm = min(1024, m)
    tk = min(2048, t)

    def body(a_ref, b_ref, o_ref, s_ref):
        kk = pl.program_id(1)

        @pl.when(kk == 0)
        def _():
            o_ref[...] = jnp.zeros_like(o_ref)
            s_ref[...] = jnp.zeros_like(s_ref)

        aa = a_ref[...]
        o_ref[...] += _dot_tn(aa, b_ref[...])
        s_ref[0:1, :] += jnp.sum(aa.astype(F32), axis=0, keepdims=True)

    return pl.pallas_call(
        body, name=name, grid=(m // tm, t // tk),
        in_specs=[pl.BlockSpec((tk, tm), lambda i, kk: (kk, i)), pl.BlockSpec((tk, n), lambda i, kk: (kk, 0))],
        out_specs=[pl.BlockSpec((tm, n), lambda i, kk: (i, 0)), pl.BlockSpec((8, tm), lambda i, kk: (0, i))],
        out_shape=[jax.ShapeDtypeStruct((m, n), F32), jax.ShapeDtypeStruct((8, m), F32)],
        compiler_params=_cparams(("parallel", "arbitrary"), vmem_mb=48),
    )(a, b)


def _fgate_fwd(zf3):
    b, s, _ = zf3.shape
    tb = SCAN_TILE
    nb = s // tb

    def body(z_ref, cexp_ref, crow_ref):
        tri = (_iota((tb, tb), 1) <= _iota((tb, tb), 0)).astype(BF16)
        expand = ((_iota((LANES, D_MODEL), 1) >> 6) == _iota((LANES, D_MODEL), 0)).astype(BF16)
        carry = jnp.zeros((1, LANES), F32)
        for i in range(nb):
            rows = slice(i * tb, (i + 1) * tb)
            z = z_ref[rows, :]
            lf = jnp.minimum(z, 0.0) - jnp.log1p(jnp.exp(-jnp.abs(z)))
            cb = sum(_dot(tri, part) for part in _split3(lf)) + carry
            carry = cb[tb - 1:tb, :]
            cexp_ref[rows, :] = sum(_dot(part, expand) for part in _split3(cb))
            crow_ref[:, rows] = cb.T[0:HEADS, :]

    return pl.pallas_call(
        body, name="fgate_fwd", grid=(b,),
        in_specs=[pl.BlockSpec((None, s, LANES), lambda i: (i, 0, 0))],
        out_specs=[pl.BlockSpec((None, s, D_MODEL), lambda i: (i, 0, 0)),
                   pl.BlockSpec((None, HEADS, s), lambda i: (i, 0, 0))],
        out_shape=[jax.ShapeDtypeStruct((b, s, D_MODEL), F32), jax.ShapeDtypeStruct((b, HEADS, s), F32)],
        compiler_params=_cparams(("parallel",)),
    )(zf3)


def _fgate_bwd(dc3, zf3):
    b, s, _ = zf3.shape
    tb = SCAN_TILE
    nb = s // tb

    def body(dc_ref, z_ref, o_ref):
        tri = (_iota((tb, tb), 1) >= _iota((tb, tb), 0)).astype(BF16)
        carry = jnp.zeros((1, LANES), F32)
        for i in reversed(range(nb)):
            rows = slice(i * tb, (i + 1) * tb)
            dlf = sum(_dot(tri, part) for part in _split3(dc_ref[rows, :])) + carry
            carry = dlf[0:1, :]
            o_ref[rows, :] = (dlf * _sigmoid(-z_ref[rows, :])).astype(BF16)

    return pl.pallas_call(
        body, name="fgate_bwd", grid=(b,),
        in_specs=[pl.BlockSpec((None, s, LANES), lambda i: (i, 0, 0)),
                  pl.BlockSpec((None, s, LANES), lambda i: (i, 0, 0))],
        out_specs=pl.BlockSpec((None, s, LANES), lambda i: (i, 0, 0)),
        out_shape=jax.ShapeDtypeStruct((b, s, LANES), BF16),
        compiler_params=_cparams(("parallel",)),
    )(dc3, zf3)


def _spare(hh):
    return HEAD_DIM if hh == 0 else 0


def _put_cols(tile, mine, cols, first):
    lane = _iota((1, LANES), 1)
    out = jnp.where(mine, tile, jnp.zeros((), tile.dtype))
    for j, c in enumerate(cols):
        out = jnp.where(lane == first + j, c, out)
    return out


def _put_rows(tile, mine, rows, first):
    sub = _iota((LANES, 1), 0)
    out = jnp.where(mine, tile, jnp.zeros((), tile.dtype))
    for j, r in enumerate(rows):
        out = jnp.where(sub == first + j, r, out)
    return out


def _transpose_bf16(a):
    return a.astype(F32).T.astype(BF16)


def _attn_fwd(qkv3, cexp3, crow, zrest3):
    b, s, _ = qkv3.shape
    ta = ATT_TILE_FWD
    nq = s // ta
    hd = HEAD_DIM
    crow5 = crow.reshape(b, HEAD_PAIRS, 2, nq, ta)

    def body(qkv_ref, cq_ref, ck_ref, g_ref, y_ref, lse_ref, ga_ref, kt_scr, v_scr):
        lane = _iota((1, LANES), 1)
        sub = _iota((LANES, 1), 0)
        lane_mine = (lane < hd, lane >= hd)
        sub_mine = (sub < hd, sub >= hd)
        causal = _iota((ta, ta), 0) >= _iota((ta, ta), 1)
        one = jnp.ones((), BF16)

        for kj in range(nq):
            rows = slice(kj * ta, (kj + 1) * ta)
            kt = _transpose_bf16(qkv_ref[rows, LANES:2 * LANES])
            v = qkv_ref[rows, 2 * LANES:3 * LANES]
            for hh in range(2):
                ck = list(_split3(-ck_ref[hh, kj:kj + 1, :]))
                kt_scr[hh, kj] = _put_rows(kt, sub_mine[hh], [one, one, one] + ck, _spare(hh))
                v_scr[hh, kj] = _put_cols(v, lane_mine[hh], [one], _spare(hh))

        for qi in range(nq):
            rows = slice(qi * ta, (qi + 1) * ta)
            q = qkv_ref[rows, 0:LANES] * 0.125
            cq = cq_ref[rows, :]
            qh = [_put_cols(q, lane_mine[hh], list(_split3(cq[:, hh * hd:hh * hd + 1])) + [one, one, one], _spare(hh))
                  for hh in range(2)]
            st = [(jnp.full((ta, 1), MASK_VALUE, F32), jnp.zeros((ta, LANES), F32))] * 2
            for kj in range(qi + 1):
                for hh in range(2):
                    m, acc = st[hh]
                    sc = _dot(qh[hh], kt_scr[hh, kj])
                    if kj == qi:
                        sc = jnp.where(causal, sc, MASK_VALUE)
                    mn = jnp.maximum(m, jnp.max(sc, axis=-1, keepdims=True))
                    p = jnp.exp(sc - mn).astype(BF16)
                    st[hh] = (mn, jnp.exp(m - mn) * acc + _dot(p, v_scr[hh, kj]))
            (ma, acca), (mb, accb) = st
            la = acca[:, hd:hd + 1]
            lb = accb[:, 0:1]
            y = jnp.where(lane_mine[0], acca * (1.0 / la), accb * (1.0 / lb))
            lse = jnp.where(lane_mine[0], ma + jnp.log(la), mb + jnp.log(lb)).T
            lse_ref[0, qi:qi + 1, :] = lse[0:1, :]
            lse_ref[1, qi:qi + 1, :] = lse[hd:hd + 1, :]
            y_ref[rows, :] = y
            g = g_ref[rows, :].astype(F32)
            ga_ref[rows, :] = (y * (g * _sigmoid(g))).astype(BF16)

    blk = lambda w: pl.BlockSpec((None, s, w), lambda i, p: (i, 0, p))
    rows5 = pl.BlockSpec((None, None, 2, nq, ta), lambda i, p: (i, p, 0, 0, 0))
    yatt3, lse5, ga3 = pl.pallas_call(
        body, name="attn_fwd", grid=(b, HEAD_PAIRS),
        in_specs=[blk(3 * LANES), blk(LANES), rows5, blk(LANES)],
        out_specs=[blk(LANES), rows5, blk(LANES)],
        out_shape=[jax.ShapeDtypeStruct((b, s, D_MODEL), F32),
                   jax.ShapeDtypeStruct((b, HEAD_PAIRS, 2, nq, ta), F32),
                   jax.ShapeDtypeStruct((b, s, D_MODEL), BF16)],
        scratch_shapes=[pltpu.VMEM((2, nq, LANES, ta), BF16), pltpu.VMEM((2, nq, ta, LANES), BF16)],
        compiler_params=_cparams(("parallel", "parallel")),
    )(qkv3, cexp3, crow5, zrest3)
    return yatt3, lse5.reshape(b, HEADS, s), ga3


def _attn_bwd(qkv3, do3, y3, lse, crow, cexp3):
    b, s, _ = qkv3.shape
    ta = ATT_TILE_BWD
    nq = s // ta
    hd = HEAD_DIM
    lse5 = lse.reshape(b, HEAD_PAIRS, 2, nq, ta)
    crow5 = crow.reshape(b, HEAD_PAIRS, 2, nq, ta)

    def body(qkv_ref, do_ref, y_ref, lse_ref, crow_ref, cexp_ref, dqkv_ref, dc_ref,
             qa_scr, doa_scr, qst_scr, dot_scr, kt_scr, vt_scr, dq_scr, rs_scr):
        pair = pl.program_id(1)
        lane = _iota((1, LANES), 1)
        sub = _iota((LANES, 1), 0)
        lane_mine = (lane < hd, lane >= hd)
        sub_mine = (sub < hd, sub >= hd)
        causal = _iota((ta, ta), 0) >= _iota((ta, ta), 1)
        one = jnp.ones((), BF16)
        zero = jnp.zeros((), BF16)

        @pl.when(pair == 0)
        def _():
            dc_ref[...] = jnp.zeros_like(dc_ref)

        for i in range(nq):
            rows = slice(i * ta, (i + 1) * ta)
            qs = qkv_ref[rows, 0:LANES] * 0.125
            qst = _transpose_bf16(qs)
            kt = _transpose_bf16(qkv_ref[rows, LANES:2 * LANES])
            vt = _transpose_bf16(qkv_ref[rows, 2 * LANES:3 * LANES])
            do = do_ref[rows, :]
            dof = do.astype(F32)
            dot = dof.T.astype(BF16)
            pr = y_ref[rows, :] * dof
            cq = cexp_ref[rows, :]
            lse_c = jnp.where(sub == 0, lse_ref[0, i:i + 1, :],
                              jnp.where(sub == 1, lse_ref[1, i:i + 1, :], 0.0)).T
            for hh in range(2):
                sp = _spare(hh)
                dsum = jnp.sum(jnp.where(lane_mine[hh], pr, 0.0), axis=-1, keepdims=True)
                bias = cq[:, hh * hd:hh * hd + 1] - lse_c[:, hh:hh + 1]
                qa_scr[hh, i] = _put_cols(qs, lane_mine[hh], list(_split3(bias)) + [one, one, one], sp)
                doa_scr[hh, i] = _put_cols(do, lane_mine[hh], list(_split3(-dsum)), sp)
                qst_scr[hh, i] = jnp.where(sub_mine[hh], qst, zero)
                dot_scr[hh, i] = jnp.where(sub_mine[hh], dot, zero)
                ck = list(_split3(-crow_ref[hh, i:i + 1, :]))
                kt_scr[hh, i] = _put_rows(kt, sub_mine[hh], [one, one, one] + ck, sp)
                vt_scr[hh, i] = _put_rows(vt, sub_mine[hh], [one, one, one], sp)
            dq_scr[i] = jnp.zeros((ta, LANES), F32)
            rs_scr[i] = jnp.zeros((ta, LANES), F32)

        for kj in range(nq):
            krows = slice(kj * ta, (kj + 1) * ta)
            k = qkv_ref[krows, LANES:2 * LANES]
            km = (jnp.where(lane_mine[0], k, zero), jnp.where(lane_mine[1], k, zero))
            dkt = jnp.zeros((LANES, ta), F32)
            dvt = jnp.zeros((LANES, ta), F32)
            dcp = [jnp.zeros((8, ta), F32), jnp.zeros((8, ta), F32)]
            for qi in range(kj, nq):
                dq = jnp.zeros((ta, LANES), F32)
                rs = []
                for hh in range(2):
                    sc = _dot(qa_scr[hh, qi], kt_scr[hh, kj])
                    if qi == kj:
                        sc = jnp.where(causal, sc, MASK_VALUE)
                    p = jnp.exp(sc)
                    dsf = p * _dot(doa_scr[hh, qi], vt_scr[hh, kj])
                    dcp[hh] = dcp[hh] + jnp.sum(dsf.reshape(ta // 8, 8, ta), axis=0)
                    rs.append(jnp.sum(dsf, axis=-1, keepdims=True))
                    ds = dsf.astype(BF16)
                    dq = dq + _dot(ds, km[hh])
                    dkt = dkt + _dot(qst_scr[hh, qi], ds)
                    dvt = dvt + _dot(dot_scr[hh, qi], p.astype(BF16))
                dq_scr[qi] += dq
                rs_scr[qi] += jnp.where(lane == 0, rs[0], jnp.where(lane == 1, rs[1], 0.0))
            dqkv_ref[krows, LANES:2 * LANES] = dkt.T.astype(BF16)
            dqkv_ref[krows, 2 * LANES:3 * LANES] = dvt.T.astype(BF16)
            dca = jnp.sum(dcp[0], axis=0, keepdims=True)
            dcb = jnp.sum(dcp[1], axis=0, keepdims=True)
            dcs = jnp.where(sub == 0, dca, jnp.where(sub == 1, dcb, 0.0)).T
            dc_ref[krows, :] += (jnp.where(lane == 2 * pair, -dcs[:, 0:1], 0.0)
                                 + jnp.where(lane == 2 * pair + 1, -dcs[:, 1:2], 0.0))
        for qi in range(nq):
            rows = slice(qi * ta, (qi + 1) * ta)
            dqkv_ref[rows, 0:LANES] = (dq_scr[qi] * 0.125).astype(BF16)
            rq = rs_scr[qi]
            dc_ref[rows, :] += (jnp.where(lane == 2 * pair, rq[:, 0:1], 0.0)
                                + jnp.where(lane == 2 * pair + 1, rq[:, 1:2], 0.0))

    blk = lambda w: pl.BlockSpec((None, s, w), lambda i, p: (i, 0, p))
    rows5 = pl.BlockSpec((None, None, 2, nq, ta), lambda i, p: (i, p, 0, 0, 0))
    by_rows = lambda: pltpu.VMEM((2, nq, ta, LANES), BF16)
    by_cols = lambda: pltpu.VMEM((2, nq, LANES, ta), BF16)
    return pl.pallas_call(
        body, name="attn_bwd", grid=(b, HEAD_PAIRS),
        in_specs=[blk(3 * LANES), blk(LANES), blk(LANES), rows5, rows5, blk(LANES)],
        out_specs=[blk(3 * LANES), pl.BlockSpec((None, s, LANES), lambda i, p: (i, 0, 0))],
        out_shape=[jax.ShapeDtypeStruct((b, s, 3 * D_MODEL), BF16), jax.ShapeDtypeStruct((b, s, LANES), F32)],
        scratch_shapes=[by_rows(), by_rows(), by_cols(), by_cols(), by_cols(), by_cols(),
                        pltpu.VMEM((nq, ta, LANES), F32), pltpu.VMEM((nq, ta, LANES), F32)],
        compiler_params=_cparams(("parallel", "arbitrary")),
    )(qkv3, do3, y3, lse5, crow5, cexp3)


def _shifted(v, ks, rows, s):
    return [jnp.where(rows >= k, pltpu.roll(v, k, 0), 0.0) if k > 0
            else jnp.where(rows < s + k, pltpu.roll(v, s + k, 0), 0.0) for k in ks]


def _rnn_common(xr, cw_ref, cb_ref, bda_ref, bdx_ref, ba_ref, bx_ref, lam_ref, s):
    rows = _iota((s, LANES), 0)
    x1, x2, x3 = _shifted(xr, (1, 2, 3), rows, s)
    xc = cb_ref[...] + cw_ref[0:1, :] * x3
    xc = xc + cw_ref[1:2, :] * x2
    xc = xc + cw_ref[2:3, :] * x1
    xc = xc + cw_ref[3:4, :] * xr
    xcb = xc.astype(BF16)
    r = _sigmoid(_dot(xcb, bda_ref[...]) + ba_ref[...])
    i = _sigmoid(_dot(xcb, bdx_ref[...]) + bx_ref[...])
    sp = _softplus(-lam_ref[...])
    log_a = (-RG_C * r) * sp
    a = jnp.exp(log_a)
    a2 = a * a
    sq = jnp.sqrt(jnp.maximum(_one_minus_exp(log_a + log_a, a2), 0.0))
    return rows, (x1, x2, x3), xc, xcb, r, i, sp, a, a2, sq


def _scan_down(a, u, rows, s, s1, s2):
    low = rows & 7
    for sh in (1, 2, 4):
        keep = low >= sh
        u = u + a * jnp.where(keep, pltpu.roll(u, sh, 0), 0.0)
        a = a * jnp.where(keep, pltpu.roll(a, sh, 0), 1.0)
    ng = s // 8
    s1[...] = a
    s2[...] = u
    at = s1[pl.ds(7, ng, stride=8), :]
    ut = s2[pl.ds(7, ng, stride=8), :]
    grow = _iota((ng, LANES), 0)
    sh = 1
    while sh < ng:
        keep = grow >= sh
        ut = ut + at * jnp.where(keep, pltpu.roll(ut, sh, 0), 0.0)
        if sh * 2 < ng:
            at = at * jnp.where(keep, pltpu.roll(at, sh, 0), 1.0)
        sh *= 2
    h_in = jnp.where(grow >= 1, pltpu.roll(ut, 1, 0), 0.0)
    for k in range(8):
        s1[pl.ds(k, ng, stride=8), :] = h_in
    return u + a * s1[...]


def _scan_up(a, g, rows, s, s1, s2):
    low = rows & 7
    for sh in (1, 2, 4):
        keep = low < 8 - sh
        g = g + a * jnp.where(keep, pltpu.roll(g, s - sh, 0), 0.0)
        a = a * jnp.where(keep, pltpu.roll(a, s - sh, 0), 1.0)
    ng = s // 8
    s1[...] = a
    s2[...] = g
    at = s1[pl.ds(0, ng, stride=8), :]
    gt = s2[pl.ds(0, ng, stride=8), :]
    grow = _iota((ng, LANES), 0)
    sh = 1
    while sh < ng:
        keep = grow < ng - sh
        gt = gt + at * jnp.where(keep, pltpu.roll(gt, ng - sh, 0), 0.0)
        if sh * 2 < ng:
            at = at * jnp.where(keep, pltpu.roll(at, ng - sh, 0), 1.0)
        sh *= 2
    g_in = jnp.where(grow < ng - 1, pltpu.roll(gt, ng - 1, 0), 0.0)
    for k in range(8):
        s1[pl.ds(k, ng, stride=8), :] = g_in
    return g + a * s1[...]


def _rnn_specs(s):
    blk = lambda off: pl.BlockSpec((None, s, LANES), lambda cb, i: (i, 0, off + cb))
    vec = lambda r: pl.BlockSpec((r, LANES), lambda cb, i: (0, cb))
    mat = pl.BlockSpec((None, LANES, LANES), lambda cb, i: (cb, 0, 0))
    return blk, vec, mat


def _rnn_fwd(zrest3, conv_w, conv_b, bda, bdx, ba, bx, lam):
    b, s, _ = zrest3.shape

    def body(xr_ref, g_ref, cw_ref, cb_ref, bda_ref, bdx_ref, ba_ref, bx_ref, lam_ref, h_ref, gr_ref, s1, s2):
        xr = xr_ref[...].astype(F32)
        rows, _, xc, _, _, i, _, a, _, sq = _rnn_common(
            xr, cw_ref, cb_ref, bda_ref, bdx_ref, ba_ref, bx_ref, lam_ref, s)
        h = _scan_down(a, sq * (i * xc), rows, s, s1, s2)
        h_ref[...] = h
        g = g_ref[...].astype(F32)
        gr_ref[...] = (h * (g * _sigmoid(g))).astype(BF16)

    blk, vec, mat = _rnn_specs(s)
    return pl.pallas_call(
        body, name="rnn_fwd", grid=(N_CBLK, b),
        in_specs=[blk(N_CBLK), blk(2 * N_CBLK), vec(CONV_W), vec(1), mat, mat, vec(1), vec(1), vec(1)],
        out_specs=[blk(0), blk(0)],
        out_shape=[jax.ShapeDtypeStruct((b, s, D_MODEL), F32), jax.ShapeDtypeStruct((b, s, D_MODEL), BF16)],
        scratch_shapes=[pltpu.VMEM((s, LANES), F32), pltpu.VMEM((s, LANES), F32)],
        compiler_params=_cparams(("parallel", "parallel")),
    )(zrest3, zrest3, conv_w, conv_b, bda, bdx, ba, bx, lam)


def _rnn_bwd(zrest3, h3, dh3, conv_w, conv_b, bda, bdx, ba, bx, lam):
    b, s, _ = zrest3.shape

    def body(xr_ref, h_ref, dh_ref, cw_ref, cb_ref, bda_ref, bdx_ref, ba_ref, bx_ref, lam_ref,
             dxr_ref, pv_ref, dbd_ref, s1, s2):
        @pl.when(pl.program_id(1) == 0)
        def _():
            pv_ref[...] = jnp.zeros_like(pv_ref)
            dbd_ref[...] = jnp.zeros_like(dbd_ref)

        xr = xr_ref[...].astype(F32)
        rows, (x1, x2, x3), xc, xcb, r, i, sp, a, a2, sq = _rnn_common(
            xr, cw_ref, cb_ref, bda_ref, bdx_ref, ba_ref, bx_ref, lam_ref, s)
        (a_next,) = _shifted(a, (-1,), rows, s)
        g = _scan_up(a_next, dh_ref[...], rows, s, s1, s2)
        (hp,) = _shifted(h_ref[...], (1,), rows, s)
        da = g * hp
        dsq = g * (i * xc)
        di = g * (sq * xc)
        dxc = g * (sq * i)
        dlog = da * a - dsq * (a2 / sq)
        dr = dlog * (-RG_C * sp)
        dpr = dr * (r * (1.0 - r))
        dpi = di * (i * (1.0 - i))
        dprb = dpr.astype(BF16)
        dpib = dpi.astype(BF16)
        dxc = dxc + _dot_nt(dprb, bda_ref[...]) + _dot_nt(dpib, bdx_ref[...])

        up1, up2, up3 = _shifted(dxc, (-1, -2, -3), rows, s)
        dxr = cw_ref[3:4, :] * dxc + cw_ref[2:3, :] * up1 + cw_ref[1:2, :] * up2 + cw_ref[0:1, :] * up3
        dxr_ref[...] = dxr.astype(BF16)

        def colsum(v):
            return jnp.sum(v, axis=0, keepdims=True)

        pv_ref[0:1, :] += colsum(dxc * x3)
        pv_ref[1:2, :] += colsum(dxc * x2)
        pv_ref[2:3, :] += colsum(dxc * x1)
        pv_ref[3:4, :] += colsum(dxc * xr)
        pv_ref[4:5, :] += colsum(dxc)
        pv_ref[5:6, :] += colsum(dpr)
        pv_ref[6:7, :] += colsum(dpi)
        pv_ref[7:8, :] += colsum(dlog * r) * (RG_C * _sigmoid(-lam_ref[...]))
        dbd_ref[0] += _dot_tn(xcb, dprb)
        dbd_ref[1] += _dot_tn(xcb, dpib)

    blk, vec, mat = _rnn_specs(s)
    hblk = pl.BlockSpec((None, s, LANES), lambda cb, i: (i, 0, cb))
    return pl.pallas_call(
        body, name="rnn_bwd", grid=(N_CBLK, b),
        in_specs=[blk(N_CBLK), hblk, hblk, vec(CONV_W), vec(1), mat, mat, vec(1), vec(1), vec(1)],
        out_specs=[hblk, pl.BlockSpec((8, LANES), lambda cb, i: (0, cb)),
                   pl.BlockSpec((None, 2, LANES, LANES), lambda cb, i: (cb, 0, 0, 0))],
        out_shape=[jax.ShapeDtypeStruct((b, s, D_MODEL), BF16), jax.ShapeDtypeStruct((8, D_MODEL), F32),
                   jax.ShapeDtypeStruct((N_CBLK, 2, LANES, LANES), F32)],
        scratch_shapes=[pltpu.VMEM((s, LANES), F32), pltpu.VMEM((s, LANES), F32)],
        compiler_params=_cparams(("parallel", "arbitrary")),
    )(zrest3, h3, dh3, conv_w, conv_b, bda, bdx, ba, bx, lam)


def _branch_merge(ga, gr, wa, wr, zrest):
    t = ga.shape[0]
    tm = min(512, t)
    tn = D_MODEL

    def body(ga_ref, gr_ref, wa_ref, wr_ref, mga_ref, mgr_ref, ya_ref, yr_ref, m_ref):
        ya = _dot(ga_ref[...], wa_ref[...])
        yr = _dot(gr_ref[...], wr_ref[...])
        ya_ref[...] = ya.astype(BF16)
        yr_ref[...] = yr.astype(BF16)
        m_ref[...] = (_sigmoid(mga_ref[...].astype(F32)) * ya + _sigmoid(mgr_ref[...].astype(F32)) * yr).astype(BF16)

    nj = D_MODEL // tn
    act = pl.BlockSpec((tm, D_MODEL), lambda i, j: (i, 0))
    wgt = pl.BlockSpec((D_MODEL, tn), lambda i, j: (0, j))
    out = pl.BlockSpec((tm, tn), lambda i, j: (i, j))
    return pl.pallas_call(
        body, name="branch_merge", grid=(t // tm, nj),
        in_specs=[act, act, wgt, wgt, pl.BlockSpec((tm, tn), lambda i, j: (i, 3 * nj + j)),
                  pl.BlockSpec((tm, tn), lambda i, j: (i, 4 * nj + j))],
        out_specs=[out, out, out],
        out_shape=[jax.ShapeDtypeStruct((t, D_MODEL), BF16), jax.ShapeDtypeStruct((t, D_MODEL), BF16),
                   jax.ShapeDtypeStruct((t, D_MODEL), BF16)],
        compiler_params=_cparams(("parallel", "parallel")),
    )(ga, gr, wa, wr, zrest, zrest)


def _out_loss(m, wout, x2, tgt2, wpost):
    t = m.shape[0]
    tm = min(512, t)

    def body(m_ref, w_ref, x_ref, t_ref, wp_ref, dy_ref, do_ref, acc_ref):
        @pl.when(pl.program_id(0) == 0)
        def _():
            acc_ref[...] = jnp.zeros_like(acc_ref)

        o = _dot(m_ref[...], w_ref[...])
        r2 = lax.rsqrt(jnp.mean(o * o, axis=-1, keepdims=True) + NORM_EPS)
        n = o * r2
        wp = wp_ref[...]
        err = (x_ref[...] + n * wp) - t_ref[...]
        dy = err * (1.0 / D_MODEL)
        dn = dy * wp
        do = r2 * (dn - n * jnp.mean(dn * n, axis=-1, keepdims=True))
        dy_ref[...] = dy
        do_ref[...] = do.astype(BF16)
        acc_ref[0:1, :] += jnp.sum(dy * n, axis=0, keepdims=True)
        acc_ref[1:2, :] += jnp.sum(err * err, axis=0, keepdims=True)

    row = pl.BlockSpec((tm, D_MODEL), lambda i: (i, 0))
    return pl.pallas_call(
        body, name="out_loss", grid=(t // tm,),
        in_specs=[row, pl.BlockSpec((D_MODEL, D_MODEL), lambda i: (0, 0)), row, row,
                  pl.BlockSpec((1, D_MODEL), lambda i: (0, 0))],
        out_specs=[row, row, pl.BlockSpec((8, D_MODEL), lambda i: (0, 0))],
        out_shape=[jax.ShapeDtypeStruct((t, D_MODEL), F32), jax.ShapeDtypeStruct((t, D_MODEL), BF16),
                   jax.ShapeDtypeStruct((8, D_MODEL), F32)],
        compiler_params=_cparams(("arbitrary",)),
    )(m, wout, x2, tgt2, wpost)


def _merge_bwd(do, wout, zrest, ya, yr):
    t = do.shape[0]
    tm = min(512, t)
    tn = D_MODEL
    nj = D_MODEL // tn

    def body(do_ref, w_ref, mga_ref, mgr_ref, ya_ref, yr_ref, dya_ref, dyr_ref, dmga_ref, dmgr_ref):
        dm = _dot_nt(do_ref[...], w_ref[...])
        sa = _sigmoid(mga_ref[...].astype(F32))
        sr = _sigmoid(mgr_ref[...].astype(F32))
        dya_ref[...] = (dm * sa).astype(BF16)
        dyr_ref[...] = (dm * sr).astype(BF16)
        dmga_ref[...] = (dm * ya_ref[...].astype(F32) * (sa * (1.0 - sa))).astype(BF16)
        dmgr_ref[...] = (dm * yr_ref[...].astype(F32) * (sr * (1.0 - sr))).astype(BF16)

    out = pl.BlockSpec((tm, tn), lambda i, j: (i, j))
    bf = jax.ShapeDtypeStruct((t, D_MODEL), BF16)
    return pl.pallas_call(
        body, name="merge_bwd", grid=(t // tm, nj),
        in_specs=[pl.BlockSpec((tm, D_MODEL), lambda i, j: (i, 0)), pl.BlockSpec((tn, D_MODEL), lambda i, j: (j, 0)),
                  pl.BlockSpec((tm, tn), lambda i, j: (i, 3 * nj + j)),
                  pl.BlockSpec((tm, tn), lambda i, j: (i, 4 * nj + j)), out, out],
        out_specs=[out, out, out, out],
        out_shape=[bf, bf, bf, bf],
        compiler_params=_cparams(("parallel", "parallel")),
    )(do, wout, zrest, zrest, ya, yr)


def _branch_bwd(dya, dyr, wa, wr, zrest, yatt, ylru):
    t = dya.shape[0]
    tm = min(512, t)
    tn = D_MODEL
    nj = D_MODEL // tn

    def body(dya_ref, dyr_ref, wa_ref, wr_ref, ga_ref, gr_ref, ya_ref, yl_ref,
             dyatt_ref, dga_ref, dyl_ref, dgr_ref):
        dga = _dot_nt(dya_ref[...], wa_ref[...])
        dgr = _dot_nt(dyr_ref[...], wr_ref[...])
        g = ga_ref[...].astype(F32)
        sg = _sigmoid(g)
        dyatt_ref[...] = (dga * (g * sg)).astype(BF16)
        dga_ref[...] = (dga * ya_ref[...] * (sg * (1.0 + g * (1.0 - sg)))).astype(BF16)
        g = gr_ref[...].astype(F32)
        sg = _sigmoid(g)
        dyl_ref[...] = dgr * (g * sg)
        dgr_ref[...] = (dgr * yl_ref[...] * (sg * (1.0 + g * (1.0 - sg)))).astype(BF16)

    act = pl.BlockSpec((tm, D_MODEL), lambda i, j: (i, 0))
    wgt = pl.BlockSpec((tn, D_MODEL), lambda i, j: (j, 0))
    out = pl.BlockSpec((tm, tn), lambda i, j: (i, j))
    bf = jax.ShapeDtypeStruct((t, D_MODEL), BF16)
    return pl.pallas_call(
        body, name="branch_bwd", grid=(t // tm, nj),
        in_specs=[act, act, wgt, wgt, pl.BlockSpec((tm, tn), lambda i, j: (i, j)),
                  pl.BlockSpec((tm, tn), lambda i, j: (i, 2 * nj + j)), out, out],
        out_specs=[out, out, out, out],
        out_shape=[bf, bf, jax.ShapeDtypeStruct((t, D_MODEL), F32), bf],
        compiler_params=_cparams(("parallel", "parallel")),
    )(dya, dyr, wa, wr, zrest, zrest, yatt, ylru)


def _dh_partial(parts, after, name):
    t = parts[0][0].shape[0]
    tm = min(256, t)
    np_ = len(parts)

    def body(*refs):
        o_ref = refs[-1]
        acc = _dot(refs[0][...], refs[np_][...])
        for p in range(1, np_):
            acc = acc + _dot(refs[p][...], refs[np_ + p][...])
        o_ref[...] = acc

    in_specs = [pl.BlockSpec((tm, dz.shape[1]), lambda i: (i, 0)) for dz, _ in parts]
    in_specs += [pl.BlockSpec(w.shape, lambda i: (0, 0)) for _, w in parts]
    in_specs += [pl.BlockSpec(after.shape, lambda i: (0, 0))]
    return pl.pallas_call(
        body, name=name, grid=(t // tm,),
        in_specs=in_specs,
        out_specs=pl.BlockSpec((tm, D_MODEL), lambda i: (i, 0)),
        out_shape=jax.ShapeDtypeStruct((t, D_MODEL), F32),
        compiler_params=_cparams(("parallel",), vmem_mb=48),
    )(*[dz for dz, _ in parts], *[w for _, w in parts], after)


def _dh_final(parts, acc_in, x2, dy, wpre):
    t = x2.shape[0]
    tm = min(256, t)
    np_ = len(parts)

    def body(*refs):
        acc_ref, x_ref, dy_ref, w_ref = refs[2 * np_:2 * np_ + 4]
        gx_ref, pw_ref = refs[2 * np_ + 4:]

        @pl.when(pl.program_id(0) == 0)
        def _():
            pw_ref[...] = jnp.zeros_like(pw_ref)

        dh = acc_ref[...]
        for p in range(np_):
            dh = dh + _dot(refs[p][...], refs[np_ + p][...])
        x = x_ref[...]
        r = lax.rsqrt(jnp.mean(x * x, axis=-1, keepdims=True) + NORM_EPS)
        xn = x * r
        dxn = dh * w_ref[...]
        gx_ref[...] = r * (dxn - xn * jnp.mean(dxn * xn, axis=-1, keepdims=True)) + dy_ref[...]
        pw_ref[0:1, :] += jnp.sum(dh * xn, axis=0, keepdims=True)

    row = pl.BlockSpec((tm, D_MODEL), lambda i: (i, 0))
    in_specs = [pl.BlockSpec((tm, dz.shape[1]), lambda i: (i, 0)) for dz, _ in parts]
    in_specs += [pl.BlockSpec(w.shape, lambda i: (0, 0)) for _, w in parts]
    in_specs += [row, row, row, pl.BlockSpec((1, D_MODEL), lambda i: (0, 0))]
    return pl.pallas_call(
        body, name="dh_final", grid=(t // tm,),
        in_specs=in_specs,
        out_specs=[row, pl.BlockSpec((8, D_MODEL), lambda i: (0, 0))],
        out_shape=[jax.ShapeDtypeStruct((t, D_MODEL), F32), jax.ShapeDtypeStruct((8, D_MODEL), F32)],
        compiler_params=_cparams(("arbitrary",), vmem_mb=48),
    )(*[dz for dz, _ in parts], *[w for _, w in parts], acc_in, x2, dy, wpre)


def _adamw(w, g, m, v):
    m = ADAM_B1 * m + (1.0 - ADAM_B1) * g
    v = ADAM_B2 * v + (1.0 - ADAM_B2) * (g * g)
    m_hat = m / (1.0 - ADAM_B1 ** ADAM_STEP)
    v_hat = v / (1.0 - ADAM_B2 ** ADAM_STEP)
    delta = -ADAM_LR * (m_hat / (jnp.sqrt(v_hat) + ADAM_EPS) + ADAM_WD * w)
    return delta, m, v


def _reduce_adamw(own, parts, place, w, m, v, name):
    r, c = w.shape
    blk, nblk, at = _blocks_2d(r, c)

    def body(place_ref, own_ref, p_ref, w_ref, m_ref, v_ref, g_ref, d_ref, nm_ref, nv_ref):
        mine = place_ref[1]
        own_blk = own_ref[...]
        g = jnp.where(mine == 0, own_blk, p_ref[0].astype(F32))
        for j in range(1, N_CHIPS):
            g = g + jnp.where(mine == j, own_blk, p_ref[j].astype(F32))
        d, nm, nv = _adamw(w_ref[...], g, m_ref[...], v_ref[...])
        g_ref[...] = g
        d_ref[...] = d
        nm_ref[...] = nm
        nv_ref[...] = nv

    row = pl.BlockSpec(blk, lambda i, pr: at(i))
    sh = jax.ShapeDtypeStruct((r, c), F32)
    grid_spec = pltpu.PrefetchScalarGridSpec(
        num_scalar_prefetch=1, grid=(nblk,),
        in_specs=[row, pl.BlockSpec((N_CHIPS,) + blk, lambda i, pr: (0,) + at(i)), row, row, row],
        out_specs=[row, row, row, row])
    return pl.pallas_call(
        body, name=name, grid_spec=grid_spec, out_shape=[sh, sh, sh, sh],
        compiler_params=_cparams(("parallel",)),
    )(place, own, parts, w, m, v)


def _interleave_qkv(a):
    lead = a.shape[:-1]
    return a.reshape(lead + (3, HEAD_PAIRS, LANES)).swapaxes(-3, -2).reshape(lead + (3 * D_MODEL,))


def _deinterleave_qkv(a):
    lead = a.shape[:-1]
    return a.reshape(lead + (HEAD_PAIRS, 3, LANES)).swapaxes(-3, -2).reshape(lead + (3 * D_MODEL,))


def _interleave_rows(a):
    return a.reshape(3, HEAD_PAIRS, LANES, a.shape[1]).swapaxes(0, 1).reshape(a.shape)


def _deinterleave_rows(a):
    return a.reshape(HEAD_PAIRS, 3, LANES, a.shape[1]).swapaxes(0, 1).reshape(a.shape)


def _pack_small(pre, conv_b, rg_ba, rg_bx, lam, post, loss_row, b_in, conv_w_full, rg_wa, rg_wx):
    z = jnp.zeros((1, D_MODEL), F32)
    b_used = jnp.concatenate([b_in[:, 0:3 * D_MODEL], b_in[:, 3 * D_MODEL + HEADS:IN_TOTAL]], axis=1)
    b_f = jnp.pad(b_in[:, 3 * D_MODEL:3 * D_MODEL + HEADS], ((0, 0), (0, D_MODEL - HEADS)))
    return jnp.concatenate([
        pre, conv_b, rg_ba, rg_bx, lam, post, loss_row, z,
        b_used.reshape(9, D_MODEL), b_f, conv_w_full, z, z,
        rg_wa.reshape(64, D_MODEL), rg_wx.reshape(64, D_MODEL)], axis=0)


def _unpack_small(p):
    b_used = p[8:17].reshape(1, 9 * D_MODEL)
    b_in = jnp.concatenate([b_used[:, 0:3 * D_MODEL], p[17:18, 0:HEADS], b_used[:, 3 * D_MODEL:]], axis=1)
    return dict(pre_norm_w=p[0:1], conv_b=p[1:2], rg_ba=p[2:3], rg_bx=p[3:4], rg_lambda=p[4:5],
                post_norm_w=p[5:6], loss_row=p[6:7], b_in=b_in, conv_w_full=p[18:22],
                rg_wa=p[24:88].reshape(1, 16, 64, 64), rg_wx=p[88:152].reshape(1, 16, 64, 64))


def _reduce_small(parts, w, m, v):
    def body(p_ref, w_ref, m_ref, v_ref, g_ref, d_ref, nm_ref, nv_ref):
        g = p_ref[0]
        for j in range(1, N_DEV):
            g = g + p_ref[j]
        d, nm, nv = _adamw(w_ref[...], g, m_ref[...], v_ref[...])
        g_ref[...] = g
        d_ref[...] = d
        nm_ref[...] = nm
        nv_ref[...] = nv

    sh = jax.ShapeDtypeStruct((SMALL_ROWS, D_MODEL), F32)
    return pl.pallas_call(body, name="reduce_small", out_shape=[sh, sh, sh, sh])(parts, w, m, v)


def kernel(x, pre_norm_w, w_in, b_in, conv_w, conv_b, rg_wa, rg_ba, rg_wx, rg_bx, rg_lambda, w_branch_a, w_branch_r, w_out, post_norm_w, loss_target, m_pre_norm_w, m_w_in, m_b_in, m_conv_w, m_conv_b, m_rg_wa, m_rg_ba, m_rg_wx, m_rg_bx, m_rg_lambda, m_w_branch_a, m_w_branch_r, m_w_out, m_post_norm_w, v_pre_norm_w, v_w_in, v_b_in, v_conv_w, v_conv_b, v_rg_wa, v_rg_ba, v_rg_wx, v_rg_bx, v_rg_lambda, v_w_branch_a, v_w_branch_r, v_w_out, v_post_norm_w):
    b, s, _ = x.shape
    t = b * s
    me = 4 * lax.axis_index("x") + 2 * lax.axis_index("y") + lax.axis_index("c")
    shard_rows = D_MODEL // N_DEV

    place = jnp.stack([lax.axis_index("c"), 2 * lax.axis_index("x") + lax.axis_index("y")]).astype(jnp.int32)
    w_in_all = _gather(w_in[0].T.astype(BF16), "gather_w_in")
    wt_full = w_in_all.reshape(IN_TOTAL, D_MODEL)
    conv_terms = jnp.concatenate(_split3(conv_w[0]), axis=0)
    conv_pad = jnp.pad(conv_terms, ((0, 16 - 3 * CONV_W), (0, D_MODEL - LANES)))
    sq_stack = jnp.concatenate([w_branch_a[0].astype(BF16), w_branch_r[0].astype(BF16), w_out[0].astype(BF16),
                                conv_pad], axis=0)
    sq_sems, sq_src, sq_land, sq_token = _gather_start(sq_stack, w_in_all, "gather_w_sq_start")

    w_qkv = _interleave_rows(wt_full[0:3 * D_MODEL])
    w_f = jnp.pad(wt_full[3 * D_MODEL:3 * D_MODEL + HEADS], ((0, LANES - HEADS), (0, 0)))
    w_rest = wt_full[3 * D_MODEL + HEADS:IN_USED]
    b_qkv = _interleave_qkv(b_in[:, 0:3 * D_MODEL]) + sq_token[0, 0]
    b_f = jnp.pad(b_in[:, 3 * D_MODEL:3 * D_MODEL + HEADS], ((0, 0), (0, LANES - HEADS)))
    b_rest = b_in[:, 3 * D_MODEL + HEADS:IN_USED]

    def blockdiag(w):
        w2 = w.reshape(N_CBLK, 2, HEAD_DIM, HEAD_DIM)
        zz = jnp.zeros((N_CBLK, HEAD_DIM, HEAD_DIM), w.dtype)
        top = jnp.concatenate([w2[:, 0], zz], axis=2)
        bot = jnp.concatenate([zz, w2[:, 1]], axis=2)
        return jnp.concatenate([top, bot], axis=1).astype(BF16)

    bda, bdx = blockdiag(rg_wa[0]), blockdiag(rg_wx[0])

    x2 = x.reshape(t, D_MODEL)
    tgt2 = loss_target.reshape(t, D_MODEL)
    h = _prenorm(x2, pre_norm_w)
    qkv = _mm_bias(h, w_qkv, b_qkv, BF16, "inproj_qkv")
    zrest = _mm_bias(h, w_rest, b_rest, BF16, "inproj_rest")
    zf = _mm_bias(h, w_f, b_f, F32, "inproj_f")
    qkv3 = qkv.reshape(b, s, 3 * D_MODEL)
    zrest3 = zrest.reshape(b, s, 5 * D_MODEL)
    zf3 = zf.reshape(b, s, LANES)
    cexp3, crow = _fgate_fwd(zf3)
    yatt3, lse, ga3 = _attn_fwd(qkv3, cexp3, crow, zrest3)

    sq_all = _gather_wait(sq_sems, sq_src, sq_land, ga3, "gather_w_sq_wait")
    sq_all = lax.dynamic_update_slice(sq_all, sq_stack[None], (me, 0, 0))
    wa = sq_all[:, 0:shard_rows].reshape(D_MODEL, D_MODEL)
    wr = sq_all[:, shard_rows:2 * shard_rows].reshape(D_MODEL, D_MODEL)
    wo = sq_all[:, 2 * shard_rows:3 * shard_rows].reshape(D_MODEL, D_MODEL)
    conv_all = sq_all[:, 3 * shard_rows:3 * shard_rows + 3 * CONV_W, 0:LANES].astype(F32)
    conv_all = (conv_all[:, 0:CONV_W] + conv_all[:, CONV_W:2 * CONV_W]) + conv_all[:, 2 * CONV_W:3 * CONV_W]
    conv_full = conv_all.transpose(1, 0, 2).reshape(CONV_W, D_MODEL)

    ylru3, gr3 = _rnn_fwd(zrest3, conv_full, conv_b, bda, bdx, rg_ba, rg_bx, rg_lambda)
    ga, gr = ga3.reshape(t, D_MODEL), gr3.reshape(t, D_MODEL)
    ya, yr, mm = _branch_merge(ga, gr, wa, wr, zrest)
    dy, do, acc_out = _out_loss(mm, wo, x2, tgt2, post_norm_w)

    dya, dyr, dz_mga, dz_mgr = _merge_bwd(do, wo, zrest, ya, yr)
    dyatt, dz_ga, dylru, dz_gr = _branch_bwd(dya, dyr, wa, wr, zrest, yatt3.reshape(t, D_MODEL),
                                             ylru3.reshape(t, D_MODEL))
    dz_xr3, pvec, dbd = _rnn_bwd(zrest3, ylru3, dylru.reshape(b, s, D_MODEL), conv_full, conv_b, bda, bdx,
                                 rg_ba, rg_bx, rg_lambda)
    dqkv3, dc3 = _attn_bwd(qkv3, dyatt.reshape(b, s, D_MODEL), yatt3, lse, crow, cexp3)
    dz_f = _fgate_bwd(dc3, zf3).reshape(t, LANES)
    dz_qkv = dqkv3.reshape(t, 3 * D_MODEL)
    dz_xr = dz_xr3.reshape(t, D_MODEL)

    dw_qkv, db_qkv = _mm_tn(dz_qkv, h, "dw_qkv")
    dw_f, db_f = _mm_tn(dz_f, h, "dw_f")
    dw_parts, db_parts = [], []
    for nm, dzp in (("ga", dz_ga), ("xr", dz_xr), ("gr", dz_gr), ("mga", dz_mga), ("mgr", dz_mgr)):
        dwp, dbp = _mm_tn(dzp, h, "dw_" + nm)
        dw_parts.append(dwp)
        db_parts.append(dbp[0:1])
    dw_a, _ = _mm_tn(ga, dya, "dw_a")
    dw_r, _ = _mm_tn(gr, dyr, "dw_r")
    dw_o, _ = _mm_tn(mm, do, "dw_o")

    zeros_tail = jnp.zeros((IN_TOTAL - IN_USED, D_MODEL), F32)
    dwt_full = jnp.concatenate([_deinterleave_rows(dw_qkv), dw_f[0:HEADS]] + dw_parts + [zeros_tail], axis=0)
    dw_in_send = dwt_full.reshape(N_CHIPS, 2, W_SHARD, D_MODEL).transpose(1, 0, 2, 3)
    by_dest = lambda a: a.reshape(N_CHIPS, 2, shard_rows, D_MODEL).transpose(1, 0, 2, 3)
    dw_sq_send = jnp.concatenate([by_dest(dw_a), by_dest(dw_r), by_dest(dw_o)], axis=2)

    sib_in, sib_sq = _swap_with_sibling([dw_in_send, dw_sq_send], "swap_dw")
    chip_in, own_in = _pair_add(dw_in_send, sib_in, place, "pair_add_in")
    chip_sq, own_sq = _pair_add(dw_sq_send, sib_sq, place, "pair_add_sq")
    sems, sent, lands, token = _exchange_chips_start([chip_in, chip_sq], "exchange_dw_start")

    wt = lambda lo: w_rest[lo * D_MODEL:(lo + 1) * D_MODEL]
    dh_a = _dh_partial([(dz_qkv, w_qkv), (dz_f, w_f)], token, "dh_qkv")
    grad_x2, acc_pre = _dh_final(
        [(dz_ga, wt(0)), (dz_xr, wt(1)), (dz_gr, wt(2)), (dz_mga, wt(3)), (dz_mgr, wt(4))],
        dh_a, x2, dy, pre_norm_w)

    db_in_full = jnp.concatenate([_deinterleave_qkv(db_qkv[0:1]), db_f[0:1, 0:HEADS]] + db_parts
                                 + [jnp.zeros((1, IN_TOTAL - IN_USED), F32)], axis=1)
    d_rg_wa = jnp.stack([dbd[:, 0, 0:HEAD_DIM, 0:HEAD_DIM], dbd[:, 0, HEAD_DIM:, HEAD_DIM:]], axis=1)
    d_rg_wx = jnp.stack([dbd[:, 1, 0:HEAD_DIM, 0:HEAD_DIM], dbd[:, 1, HEAD_DIM:, HEAD_DIM:]], axis=1)
    small_g = _pack_small(acc_pre[0:1], pvec[4:5], pvec[5:6], pvec[6:7], pvec[7:8], acc_out[0:1], acc_out[1:2],
                          db_in_full, pvec[0:4], d_rg_wa, d_rg_wx)
    sm_sems, sm_src, sm_land, sm_token = _gather_start(small_g, grad_x2, "gather_small_start")
    recv_in, recv_sq = _exchange_chips_wait(sems, sent, lands, sm_token, "exchange_dw_wait")

    g_in, d_in, nm_in, nv_in = [a.T for a in _reduce_adamw(
        own_in, recv_in, place, w_in[0].T, m_w_in[0].T, v_w_in[0].T, "adamw_w_in")]
    sq_w = jnp.concatenate([w_branch_a[0], w_branch_r[0], w_out[0]], axis=0)
    sq_m = jnp.concatenate([m_w_branch_a[0], m_w_branch_r[0], m_w_out[0]], axis=0)
    sq_v = jnp.concatenate([v_w_branch_a[0], v_w_branch_r[0], v_w_out[0]], axis=0)
    g_sq, d_sq, nm_sq, nv_sq = _reduce_adamw(own_sq, recv_sq, place, sq_w, sq_m, sq_v, "adamw_w_sq")
    small_all = _gather_wait(sm_sems, sm_src, sm_land, d_sq, "gather_small_wait")
    small_all = lax.dynamic_update_slice(small_all, small_g[None], (me, 0, 0))

    def place_conv(a):
        return lax.dynamic_update_slice(jnp.zeros((CONV_W, D_MODEL), F32), a[0], (0, me * LANES))

    zrow = jnp.zeros((1, D_MODEL), F32)
    small_w = _pack_small(pre_norm_w, conv_b, rg_ba, rg_bx, rg_lambda, post_norm_w, zrow, b_in,
                          place_conv(conv_w), rg_wa[0], rg_wx[0])
    small_m = _pack_small(m_pre_norm_w, m_conv_b, m_rg_ba, m_rg_bx, m_rg_lambda, m_post_norm_w, zrow, m_b_in,
                          place_conv(m_conv_w), m_rg_wa[0], m_rg_wx[0])
    small_v = _pack_small(v_pre_norm_w, v_conv_b, v_rg_ba, v_rg_bx, v_rg_lambda, v_post_norm_w, zrow, v_b_in,
                          place_conv(v_conv_w), v_rg_wa[0], v_rg_wx[0])
    outs_small = [_unpack_small(p) for p in _reduce_small(small_all, small_w, small_m, small_v)]

    loss = (0.5 / D_MODEL) * jnp.sum(outs_small[0]["loss_row"])

    def leaf(kind, name):
        if name == "w_in":
            return (g_in, d_in, nm_in, nv_in)[kind][None]
        if name in ("w_branch_a", "w_branch_r", "w_out"):
            j = ("w_branch_a", "w_branch_r", "w_out").index(name)
            return (g_sq, d_sq, nm_sq, nv_sq)[kind][None, j * shard_rows:(j + 1) * shard_rows]
        if name == "conv_w":
            return lax.dynamic_slice(outs_small[kind]["conv_w_full"], (0, me * LANES), (CONV_W, LANES))[None]
        return outs_small[kind][name]

    names = ["pre_norm_w", "w_in", "b_in", "conv_w", "conv_b", "rg_wa", "rg_ba", "rg_wx", "rg_bx", "rg_lambda",
             "w_branch_a", "w_branch_r", "w_out", "post_norm_w"]
    out = [loss, grad_x2.reshape(b, s, D_MODEL)]
    for kind in range(4):
        out += [leaf(kind, nm) for nm in names]
    return tuple(out)
```

```python
import jax
import jax.numpy as jnp
from jax import lax
from jax.experimental import pallas as pl
from jax.experimental.pallas import tpu as pltpu

F32 = jnp.float32
BF16 = jnp.bfloat16

N_DEV = 8
D_MODEL = 1024
HEADS = 16
HEAD_DIM = 64
HEAD_PAIRS = HEADS // 2
LANES = 128
N_CBLK = D_MODEL // LANES
CONV_W = 4
RG_C = 8.0
NORM_EPS = 1e-6
MASK_VALUE = -1e30
IN_USED = 8208
IN_TOTAL = 9232
W_SHARD = IN_TOTAL // N_DEV

ADAM_LR = 0.001
ADAM_B1 = 0.9
ADAM_B2 = 0.999
ADAM_EPS = 1e-08
ADAM_WD = 0.01
ADAM_STEP = 10

ATT_TILE_FWD = 256
ATT_TILE_BWD = 512
SCAN_TILE = 256
SMALL_ROWS = 152
LOSS_ROW = 6


def _cparams(sem=None, vmem_mb=None):
    kw = {}
    if sem is not None:
        kw["dimension_semantics"] = sem
    if vmem_mb is not None:
        kw["vmem_limit_bytes"] = vmem_mb * 1024 * 1024
    return pltpu.CompilerParams(**kw)


def _sigmoid(x):
    return 1.0 / (1.0 + jnp.exp(-x))


def _softplus(x):
    return jnp.maximum(x, 0.0) + jnp.log1p(jnp.exp(-jnp.abs(x)))


def _one_minus_exp(y, exp_y):
    series = -y * (1.0 + y * (1.0 / 2 + y * (1.0 / 6 + y * (1.0 / 24 + y * (1.0 / 120)))))
    return jnp.where(y > -0.0625, series, 1.0 - exp_y)


def _split3(x):
    hi = x.astype(BF16)
    r1 = x - hi.astype(F32)
    mid = r1.astype(BF16)
    lo = (r1 - mid.astype(F32)).astype(BF16)
    return hi, mid, lo


def _dot(a, b):
    return jnp.dot(a, b, preferred_element_type=F32)


def _dot_nt(a, b):
    return lax.dot_general(a, b, (((1,), (1,)), ((), ())), preferred_element_type=F32)


def _dot_tn(a, b):
    return lax.dot_general(a, b, (((0,), (0,)), ((), ())), preferred_element_type=F32)


def _iota(shape, dim):
    return lax.broadcasted_iota(jnp.int32, shape, dim)


_ANY = pl.BlockSpec(memory_space=pl.ANY)
_MESH = pl.DeviceIdType.MESH
N_CHIPS = 4


def _place():
    x, y, c = lax.axis_index("x"), lax.axis_index("y"), lax.axis_index("c")
    other_chips = [(1 - x, y), (x, 1 - y), (1 - x, 1 - y)]
    return x, y, c, other_chips


def _gather(x_shard, name):
    def body(x_ref, out_ref, send_sems, recv_sems, local_sem):
        x, y, c, chips = _place()
        me, sibling = (x, y, c), (x, y, 1 - c)

        def slot(p):
            return out_ref.at[4 * p[0] + 2 * p[1] + p[2]]

        def copy(k, block, to, src=None):
            return pltpu.make_async_remote_copy(
                src_ref=slot(block) if src is None else src, dst_ref=slot(block),
                send_sem=send_sems.at[k], recv_sem=recv_sems.at[k], device_id=to, device_id_type=_MESH)

        mine = pltpu.make_async_copy(x_ref, slot(me), local_sem)
        mine.start()
        first = [copy(0, me, sibling, src=x_ref)]
        first += [copy(1 + j, me, (*chip, c), src=x_ref) for j, chip in enumerate(chips)]
        for cp in first:
            cp.start()
        passed = [copy(4 + j, (*chip, c), sibling) for j, chip in enumerate(chips)]
        for j, chip in enumerate(chips):
            copy(1 + j, (*chip, c), me).wait_recv()
            passed[j].start()
        copy(0, sibling, me).wait_recv()
        for j, chip in enumerate(chips):
            copy(4 + j, (*chip, 1 - c), me).wait_recv()
        for cp in first + passed:
            cp.wait_send()
        mine.wait()

    return pl.pallas_call(
        body, name=name,
        out_shape=jax.ShapeDtypeStruct((N_DEV,) + tuple(x_shard.shape), x_shard.dtype),
        in_specs=[_ANY], out_specs=_ANY,
        scratch_shapes=[pltpu.SemaphoreType.DMA((7,)), pltpu.SemaphoreType.DMA((7,)), pltpu.SemaphoreType.DMA],
    )(x_shard)


def _swap_with_sibling(srcs, name):
    n = len(srcs)

    def body(*refs):
        src_refs, out_refs = refs[:n], refs[n:2 * n]
        send_sems, recv_sems = refs[2 * n:]
        x, y, c, _ = _place()
        cps = [pltpu.make_async_remote_copy(
            src_ref=src_refs[i].at[1 - c], dst_ref=out_refs[i], send_sem=send_sems.at[i], recv_sem=recv_sems.at[i],
            device_id=(x, y, 1 - c), device_id_type=_MESH) for i in range(n)]
        for cp in cps:
            cp.start()
        for cp in cps:
            cp.wait()

    return pl.pallas_call(
        body, name=name,
        out_shape=[jax.ShapeDtypeStruct(a.shape[1:], a.dtype) for a in srcs],
        in_specs=[_ANY] * n, out_specs=[_ANY] * n,
        scratch_shapes=[pltpu.SemaphoreType.DMA((n,)), pltpu.SemaphoreType.DMA((n,))],
    )(*srcs)


def _blocks_2d(r, c):
    if r % 128 == 0:
        return (128, c), r // 128, lambda i: (i, 0)
    return (r, 256), c // 256, lambda i: (0, i)


def _pair_add(src, recv, place, name):
    _, _, r, c = src.shape
    blk, nblk, at = _blocks_2d(r, c)

    def body(place_ref, a_ref, b_ref, q16_ref, own_ref):
        q = a_ref[...] + b_ref[...]
        q16_ref[...] = q.astype(BF16)

        @pl.when(pl.program_id(1) == place_ref[1])
        def _():
            own_ref[...] = q

    grid_spec = pltpu.PrefetchScalarGridSpec(
        num_scalar_prefetch=1, grid=(nblk, N_CHIPS),
        in_specs=[pl.BlockSpec((None, None) + blk, lambda i, j, pr: (pr[0], j) + at(i)),
                  pl.BlockSpec((None,) + blk, lambda i, j, pr: (j,) + at(i))],
        out_specs=[pl.BlockSpec((None,) + blk, lambda i, j, pr: (j,) + at(i)),
                   pl.BlockSpec(blk, lambda i, j, pr: at(i))])
    return pl.pallas_call(
        body, name=name, grid_spec=grid_spec,
        out_shape=[jax.ShapeDtypeStruct((N_CHIPS, r, c), BF16), jax.ShapeDtypeStruct((r, c), F32)],
        compiler_params=_cparams(("parallel", "arbitrary")),
    )(place, src, recv)


_HBM = pl.BlockSpec(memory_space=pltpu.HBM)
_SEM = pl.BlockSpec(memory_space=pltpu.SEMAPHORE)
_DATAFLOW = pltpu.SideEffectType.DATAFLOW_SIDE_EFFECTING


def _chip_copy(src_ref, land_ref, send_sem, recv_sem, k, chips, c, land):
    chip = chips[k]
    return pltpu.make_async_remote_copy(
        src_ref=src_ref.at[2 * chip[0] + chip[1]], dst_ref=land_ref.at[land],
        send_sem=send_sem, recv_sem=recv_sem, device_id=(*chip, c), device_id_type=_MESH)


def _exchange_chips_start(srcs, name):
    n = len(srcs)
    ncp = 3 * n

    def body(*refs):
        src_refs, land_refs = refs[:n], refs[n:2 * n]
        sems = refs[4 * n:4 * n + 2 * ncp]
        token = refs[-1]
        x, y, c, chips = _place()
        for i in range(n):
            for k in range(3):
                j = 3 * i + k
                _chip_copy(src_refs[i], land_refs[i], sems[j], sems[ncp + j], k, chips, c, 2 * x + y).start()
        token[...] = jnp.zeros_like(token)

    hbm = [pltpu.HBM(a.shape, a.dtype) for a in srcs]
    lands = [pltpu.with_memory_space_constraint(lax.empty(a.shape, a.dtype), pltpu.HBM) for a in srcs]
    res = pl.pallas_call(
        body, name=name,
        out_shape=(*hbm, *hbm, *([pltpu.SemaphoreType.DMA(())] * (2 * ncp)), jax.ShapeDtypeStruct((8, LANES), F32)),
        in_specs=[_HBM] * (2 * n),
        out_specs=(*([_HBM] * (2 * n)), *([_SEM] * (2 * ncp)), pl.BlockSpec(memory_space=pltpu.VMEM)),
        input_output_aliases={i: i for i in range(2 * n)},
        compiler_params=pltpu.CompilerParams(has_side_effects=_DATAFLOW),
    )(*[pltpu.with_memory_space_constraint(a, pltpu.HBM) for a in srcs], *lands)
    return list(res[2 * n:2 * n + 2 * ncp]), list(res[:n]), list(res[n:2 * n]), res[-1]


def _exchange_chips_wait(sems, srcs, lands, after, name):
    n = len(srcs)
    ncp = 3 * n

    def body(*refs):
        src_refs, land_refs = refs[:n], refs[n:2 * n]
        sem_refs = refs[2 * n:2 * n + 2 * ncp]
        x, y, c, chips = _place()
        for i in range(n):
            for k in range(3):
                j = 3 * i + k
                cp = _chip_copy(src_refs[i], land_refs[i], sem_refs[j], sem_refs[ncp + j], k, chips, c,
                                2 * chips[k][0] + chips[k][1])
                cp.wait_send()
                cp.wait_recv()

    hbm = [pltpu.HBM(a.shape, a.dtype) for a in srcs]
    res = pl.pallas_call(
        body, name=name, out_shape=(*hbm, *hbm),
        in_specs=[_HBM] * (2 * n) + [_SEM] * (2 * ncp) + [_ANY], out_specs=tuple([_HBM] * (2 * n)),
        input_output_aliases={i: i for i in range(2 * n)},
        compiler_params=pltpu.CompilerParams(has_side_effects=_DATAFLOW),
    )(*srcs, *lands, *sems, after)
    return list(res[n:2 * n])


def _peer_copy(src_ref, land_ref, send_sem, recv_sem, k, place, land):
    x, y, c = place
    peer = (1 - x if k & 4 else x, 1 - y if k & 2 else y, 1 - c if k & 1 else c)
    return pltpu.make_async_remote_copy(
        src_ref=src_ref, dst_ref=land_ref.at[land], send_sem=send_sem, recv_sem=recv_sem,
        device_id=peer, device_id_type=_MESH)


def _gather_start(x_shard, after, name):
    npeer = N_DEV - 1

    def body(x_ref, land_ref, after_ref, x_thru, land_thru, *rest):
        sems, token = rest[:2 * npeer], rest[-1]
        x, y, c, _ = _place()
        for k in range(1, N_DEV):
            _peer_copy(x_ref, land_ref, sems[k - 1], sems[npeer + k - 1], k, (x, y, c), 4 * x + 2 * y + c).start()
        token[...] = jnp.zeros_like(token)

    land = pltpu.with_memory_space_constraint(lax.empty((N_DEV,) + tuple(x_shard.shape), x_shard.dtype), pltpu.HBM)
    res = pl.pallas_call(
        body, name=name,
        out_shape=(pltpu.HBM(x_shard.shape, x_shard.dtype), pltpu.HBM(land.shape, land.dtype),
                   *([pltpu.SemaphoreType.DMA(())] * (2 * npeer)), jax.ShapeDtypeStruct((8, LANES), F32)),
        in_specs=[_HBM, _HBM, _ANY],
        out_specs=(_HBM, _HBM, *([_SEM] * (2 * npeer)), pl.BlockSpec(memory_space=pltpu.VMEM)),
        input_output_aliases={0: 0, 1: 1},
        compiler_params=pltpu.CompilerParams(has_side_effects=_DATAFLOW),
    )(pltpu.with_memory_space_constraint(x_shard, pltpu.HBM), land, after)
    return list(res[2:2 + 2 * npeer]), res[0], res[1], res[-1]


def _gather_wait(sems, src, land, after, name):
    npeer = N_DEV - 1

    def body(x_ref, land_ref, *rest):
        sem_refs = rest[:2 * npeer]
        x, y, c, _ = _place()
        for k in range(1, N_DEV):
            peer_index = (4 * x + 2 * y + c) ^ k
            cp = _peer_copy(x_ref, land_ref, sem_refs[k - 1], sem_refs[npeer + k - 1], k, (x, y, c), peer_index)
            cp.wait_send()
            cp.wait_recv()

    res = pl.pallas_call(
        body, name=name, out_shape=(pltpu.HBM(src.shape, src.dtype), pltpu.HBM(land.shape, land.dtype)),
        in_specs=[_HBM, _HBM] + [_SEM] * (2 * npeer) + [_ANY], out_specs=(_HBM, _HBM),
        input_output_aliases={0: 0, 1: 1},
        compiler_params=pltpu.CompilerParams(has_side_effects=_DATAFLOW),
    )(src, land, *sems, after)
    return res[1]


def _prenorm(x2, w):
    t = x2.shape[0]
    tm = min(512, t)

    def body(x_ref, w_ref, h_ref):
        x = x_ref[...]
        r = lax.rsqrt(jnp.mean(x * x, axis=-1, keepdims=True) + NORM_EPS)
        h_ref[...] = (x * r * w_ref[...]).astype(BF16)

    return pl.pallas_call(
        body, name="prenorm", grid=(t // tm,),
        in_specs=[pl.BlockSpec((tm, D_MODEL), lambda i: (i, 0)), pl.BlockSpec((1, D_MODEL), lambda i: (0, 0))],
        out_specs=pl.BlockSpec((tm, D_MODEL), lambda i: (i, 0)),
        out_shape=jax.ShapeDtypeStruct((t, D_MODEL), BF16),
        compiler_params=_cparams(("parallel",)),
    )(x2, w)


def _mm_bias(a, bt, bias, out_dtype, name):
    m, k = a.shape
    n = bt.shape[0]
    tm = min(512, m)
    tn = min(1024, n)

    def body(a_ref, bt_ref, bias_ref, o_ref):
        aa = a_ref[...]
        for j in range(n // tn):
            cols = slice(j * tn, (j + 1) * tn)
            o_ref[:, cols] = (_dot_nt(aa, bt_ref[cols, :]) + bias_ref[:, cols]).astype(o_ref.dtype)

    return pl.pallas_call(
        body, name=name, grid=(m // tm,),
        in_specs=[pl.BlockSpec((tm, k), lambda i: (i, 0)), pl.BlockSpec((n, k), lambda i: (0, 0)),
                  pl.BlockSpec((1, n), lambda i: (0, 0))],
        out_specs=pl.BlockSpec((tm, n), lambda i: (i, 0)),
        out_shape=jax.ShapeDtypeStruct((m, n), out_dtype),
        compiler_params=_cparams(("parallel",), vmem_mb=48),
    )(a, bt, bias)


def _mm_tn(a, b, name):
    t, m = a.shape
    n = b.shape[1]
    tm = min(1024, m)
    tk = min(2048, t)

    def body(a_ref, b_ref, o_ref, s_ref):
        kk = pl.program_id(1)

        @pl.when(kk == 0)
        def _():
            o_ref[...] = jnp.zeros_like(o_ref)
            s_ref[...] = jnp.zeros_like(s_ref)

        aa = a_ref[...]
        o_ref[...] += _dot_tn(aa, b_ref[...])
        s_ref[0:1, :] += jnp.sum(aa.astype(F32), axis=0, keepdims=True)

    return pl.pallas_call(
        body, name=name, grid=(m // tm, t // tk),
        in_specs=[pl.BlockSpec((tk, tm), lambda i, kk: (kk, i)), pl.BlockSpec((tk, n), lambda i, kk: (kk, 0))],
        out_specs=[pl.BlockSpec((tm, n), lambda i, kk: (i, 0)), pl.BlockSpec((8, tm), lambda i, kk: (0, i))],
        out_shape=[jax.ShapeDtypeStruct((m, n), F32), jax.ShapeDtypeStruct((8, m), F32)],
        compiler_params=_cparams(("parallel", "arbitrary"), vmem_mb=48),
    )(a, b)


def _fgate_fwd(zf3):
    b, s, _ = zf3.shape
    tb = SCAN_TILE
    nb = s // tb

    def body(z_ref, cexp_ref, crow_ref):
        tri = (_iota((tb, tb), 1) <= _iota((tb, tb), 0)).astype(BF16)
        expand = ((_iota((LANES, D_MODEL), 1) >> 6) == _iota((LANES, D_MODEL), 0)).astype(BF16)
        carry = jnp.zeros((1, LANES), F32)
        for i in range(nb):
            rows = slice(i * tb, (i + 1) * tb)
            z = z_ref[rows, :]
            lf = jnp.minimum(z, 0.0) - jnp.log1p(jnp.exp(-jnp.abs(z)))
            cb = sum(_dot(tri, part) for part in _split3(lf)) + carry
            carry = cb[tb - 1:tb, :]
            cexp_ref[rows, :] = sum(_dot(part, expand) for part in _split3(cb))
            crow_ref[:, rows] = cb.T[0:HEADS, :]

    return pl.pallas_call(
        body, name="fgate_fwd", grid=(b,),
        in_specs=[pl.BlockSpec((None, s, LANES), lambda i: (i, 0, 0))],
        out_specs=[pl.BlockSpec((None, s, D_MODEL), lambda i: (i, 0, 0)),
                   pl.BlockSpec((None, HEADS, s), lambda i: (i, 0, 0))],
        out_shape=[jax.ShapeDtypeStruct((b, s, D_MODEL), F32), jax.ShapeDtypeStruct((b, HEADS, s), F32)],
        compiler_params=_cparams(("parallel",)),
    )(zf3)


def _fgate_bwd(dc3, zf3):
    b, s, _ = zf3.shape
    tb = SCAN_TILE
    nb = s // tb

    def body(dc_ref, z_ref, o_ref):
        tri = (_iota((tb, tb), 1) >= _iota((tb, tb), 0)).astype(BF16)
        carry = jnp.zeros((1, LANES), F32)
        for i in reversed(range(nb)):
            rows = slice(i * tb, (i + 1) * tb)
            dlf = sum(_dot(tri, part) for part in _split3(dc_ref[rows, :])) + carry
            carry = dlf[0:1, :]
            o_ref[rows, :] = (dlf * _sigmoid(-z_ref[rows, :])).astype(BF16)

    return pl.pallas_call(
        body, name="fgate_bwd", grid=(b,),
        in_specs=[pl.BlockSpec((None, s, LANES), lambda i: (i, 0, 0)),
                  pl.BlockSpec((None, s, LANES), lambda i: (i, 0, 0))],
        out_specs=pl.BlockSpec((None, s, LANES), lambda i: (i, 0, 0)),
        out_shape=jax.ShapeDtypeStruct((b, s, LANES), BF16),
        compiler_params=_cparams(("parallel",)),
    )(dc3, zf3)


def _spare(hh):
    return HEAD_DIM if hh == 0 else 0


def _put_cols(tile, mine, cols, first):
    lane = _iota((1, LANES), 1)
    out = jnp.where(mine, tile, jnp.zeros((), tile.dtype))
    for j, c in enumerate(cols):
        out = jnp.where(lane == first + j, c, out)
    return out


def _put_rows(tile, mine, rows, first):
    sub = _iota((LANES, 1), 0)
    out = jnp.where(mine, tile, jnp.zeros((), tile.dtype))
    for j, r in enumerate(rows):
        out = jnp.where(sub == first + j, r, out)
    return out


def _transpose_bf16(a):
    return a.astype(F32).T.astype(BF16)


def _attn_fwd(qkv3, cexp3, crow, zrest3):
    b, s, _ = qkv3.shape
    ta = ATT_TILE_FWD
    nq = s // ta
    hd = HEAD_DIM
    crow5 = crow.reshape(b, HEAD_PAIRS, 2, nq, ta)

    def body(qkv_ref, cq_ref, ck_ref, g_ref, y_ref, lse_ref, ga_ref, kt_scr, v_scr):
        lane = _iota((1, LANES), 1)
        sub = _iota((LANES, 1), 0)
        lane_mine = (lane < hd, lane >= hd)
        sub_mine = (sub < hd, sub >= hd)
        causal = _iota((ta, ta), 0) >= _iota((ta, ta), 1)
        one = jnp.ones((), BF16)

        for kj in range(nq):
            rows = slice(kj * ta, (kj + 1) * ta)
            kt = _transpose_bf16(qkv_ref[rows, LANES:2 * LANES])
            v = qkv_ref[rows, 2 * LANES:3 * LANES]
            for hh in range(2):
                ck = list(_split3(-ck_ref[hh, kj:kj + 1, :]))
                kt_scr[hh, kj] = _put_rows(kt, sub_mine[hh], [one, one, one] + ck, _spare(hh))
                v_scr[hh, kj] = _put_cols(v, lane_mine[hh], [one], _spare(hh))

        for qi in range(nq):
            rows = slice(qi * ta, (qi + 1) * ta)
            q = qkv_ref[rows, 0:LANES] * 0.125
            cq = cq_ref[rows, :]
            qh = [_put_cols(q, lane_mine[hh], list(_split3(cq[:, hh * hd:hh * hd + 1])) + [one, one, one], _spare(hh))
                  for hh in range(2)]
            st = [(jnp.full((ta, 1), MASK_VALUE, F32), jnp.zeros((ta, LANES), F32))] * 2
            for kj in range(qi + 1):
                for hh in range(2):
                    m, acc = st[hh]
                    sc = _dot(qh[hh], kt_scr[hh, kj])
                    if kj == qi:
                        sc = jnp.where(causal, sc, MASK_VALUE)
                    mn = jnp.maximum(m, jnp.max(sc, axis=-1, keepdims=True))
                    p = jnp.exp(sc - mn).astype(BF16)
                    st[hh] = (mn, jnp.exp(m - mn) * acc + _dot(p, v_scr[hh, kj]))
            (ma, acca), (mb, accb) = st
            la = acca[:, hd:hd + 1]
            lb = accb[:, 0:1]
            y = jnp.where(lane_mine[0], acca * (1.0 / la), accb * (1.0 / lb))
            lse = jnp.where(lane_mine[0], ma + jnp.log(la), mb + jnp.log(lb)).T
            lse_ref[0, qi:qi + 1, :] = lse[0:1, :]
            lse_ref[1, qi:qi + 1, :] = lse[hd:hd + 1, :]
            y_ref[rows, :] = y
            g = g_ref[rows, :].astype(F32)
            ga_ref[rows, :] = (y * (g * _sigmoid(g))).astype(BF16)

    blk = lambda w: pl.BlockSpec((None, s, w), lambda i, p: (i, 0, p))
    rows5 = pl.BlockSpec((None, None, 2, nq, ta), lambda i, p: (i, p, 0, 0, 0))
    yatt3, lse5, ga3 = pl.pallas_call(
        body, name="attn_fwd", grid=(b, HEAD_PAIRS),
        in_specs=[blk(3 * LANES), blk(LANES), rows5, blk(LANES)],
        out_specs=[blk(LANES), rows5, blk(LANES)],
        out_shape=[jax.ShapeDtypeStruct((b, s, D_MODEL), F32),
                   jax.ShapeDtypeStruct((b, HEAD_PAIRS, 2, nq, ta), F32),
                   jax.ShapeDtypeStruct((b, s, D_MODEL), BF16)],
        scratch_shapes=[pltpu.VMEM((2, nq, LANES, ta), BF16), pltpu.VMEM((2, nq, ta, LANES), BF16)],
        compiler_params=_cparams(("parallel", "parallel")),
    )(qkv3, cexp3, crow5, zrest3)
    return yatt3, lse5.reshape(b, HEADS, s), ga3


def _attn_bwd(qkv3, do3, y3, lse, crow, cexp3):
    b, s, _ = qkv3.shape
    ta = ATT_TILE_BWD
    nq = s // ta
    hd = HEAD_DIM
    lse5 = lse.reshape(b, HEAD_PAIRS, 2, nq, ta)
    crow5 = crow.reshape(b, HEAD_PAIRS, 2, nq, ta)

    def body(qkv_ref, do_ref, y_ref, lse_ref, crow_ref, cexp_ref, dqkv_ref, dc_ref,
             qa_scr, doa_scr, qst_scr, dot_scr, kt_scr, vt_scr, dq_scr, rs_scr):
        pair = pl.program_id(1)
        lane = _iota((1, LANES), 1)
        sub = _iota((LANES, 1), 0)
        lane_mine = (lane < hd, lane >= hd)
        sub_mine = (sub < hd, sub >= hd)
        causal = _iota((ta, ta), 0) >= _iota((ta, ta), 1)
        one = jnp.ones((), BF16)
        zero = jnp.zeros((), BF16)

        @pl.when(pair == 0)
        def _():
            dc_ref[...] = jnp.zeros_like(dc_ref)

        for i in range(nq):
            rows = slice(i * ta, (i + 1) * ta)
            qs = qkv_ref[rows, 0:LANES] * 0.125
            qst = _transpose_bf16(qs)
            kt = _transpose_bf16(qkv_ref[rows, LANES:2 * LANES])
            vt = _transpose_bf16(qkv_ref[rows, 2 * LANES:3 * LANES])
            do = do_ref[rows, :]
            dof = do.astype(F32)
            dot = dof.T.astype(BF16)
            pr = y_ref[rows, :] * dof
            cq = cexp_ref[rows, :]
            lse_c = jnp.where(sub == 0, lse_ref[0, i:i + 1, :],
                              jnp.where(sub == 1, lse_ref[1, i:i + 1, :], 0.0)).T
            for hh in range(2):
                sp = _spare(hh)
                dsum = jnp.sum(jnp.where(lane_mine[hh], pr, 0.0), axis=-1, keepdims=True)
                bias = cq[:, hh * hd:hh * hd + 1] - lse_c[:, hh:hh + 1]
                qa_scr[hh, i] = _put_cols(qs, lane_mine[hh], list(_split3(bias)) + [one, one, one], sp)
                doa_scr[hh, i] = _put_cols(do, lane_mine[hh], list(_split3(-dsum)), sp)
                qst_scr[hh, i] = jnp.where(sub_mine[hh], qst, zero)
                dot_scr[hh, i] = jnp.where(sub_mine[hh], dot, zero)
                ck = list(_split3(-crow_ref[hh, i:i + 1, :]))
                kt_scr[hh, i] = _put_rows(kt, sub_mine[hh], [one, one, one] + ck, sp)
                vt_scr[hh, i] = _put_rows(vt, sub_mine[hh], [one, one, one], sp)
            dq_scr[i] = jnp.zeros((ta, LANES), F32)
            rs_scr[i] = jnp.zeros((ta, LANES), F32)

        for kj in range(nq):
            krows = slice(kj * ta, (kj + 1) * ta)
            k = qkv_ref[krows, LANES:2 * LANES]
            km = (jnp.where(lane_mine[0], k, zero), jnp.where(lane_mine[1], k, zero))
            dkt = jnp.zeros((LANES, ta), F32)
            dvt = jnp.zeros((LANES, ta), F32)
            dcp = [jnp.zeros((8, ta), F32), jnp.zeros((8, ta), F32)]
            for qi in range(kj, nq):
                dq = jnp.zeros((ta, LANES), F32)
                rs = []
                for hh in range(2):
                    sc = _dot(qa_scr[hh, qi], kt_scr[hh, kj])
                    if qi == kj:
                        sc = jnp.where(causal, sc, MASK_VALUE)
                    p = jnp.exp(sc)
                    dsf = p * _dot(doa_scr[hh, qi], vt_scr[hh, kj])
                    dcp[hh] = dcp[hh] + jnp.sum(dsf.reshape(ta // 8, 8, ta), axis=0)
                    rs.append(jnp.sum(dsf, axis=-1, keepdims=True))
                    ds = dsf.astype(BF16)
                    dq = dq + _dot(ds, km[hh])
                    dkt = dkt + _dot(qst_scr[hh, qi], ds)
                    dvt = dvt + _dot(dot_scr[hh, qi], p.astype(BF16))
                dq_scr[qi] += dq
                rs_scr[qi] += jnp.where(lane == 0, rs[0], jnp.where(lane == 1, rs[1], 0.0))
            dqkv_ref[krows, LANES:2 * LANES] = dkt.T.astype(BF16)
            dqkv_ref[krows, 2 * LANES:3 * LANES] = dvt.T.astype(BF16)
            dca = jnp.sum(dcp[0], axis=0, keepdims=True)
            dcb = jnp.sum(dcp[1], axis=0, keepdims=True)
            dcs = jnp.where(sub == 0, dca, jnp.where(sub == 1, dcb, 0.0)).T
            dc_ref[krows, :] += (jnp.where(lane == 2 * pair, -dcs[:, 0:1], 0.0)
                                 + jnp.where(lane == 2 * pair + 1, -dcs[:, 1:2], 0.0))
        for qi in range(nq):
            rows = slice(qi * ta, (qi + 1) * ta)
            dqkv_ref[rows, 0:LANES] = (dq_scr[qi] * 0.125).astype(BF16)
            rq = rs_scr[qi]
            dc_ref[rows, :] += (jnp.where(lane == 2 * pair, rq[:, 0:1], 0.0)
                                + jnp.where(lane == 2 * pair + 1, rq[:, 1:2], 0.0))

    blk = lambda w: pl.BlockSpec((None, s, w), lambda i, p: (i, 0, p))
    rows5 = pl.BlockSpec((None, None, 2, nq, ta), lambda i, p: (i, p, 0, 0, 0))
    by_rows = lambda: pltpu.VMEM((2, nq, ta, LANES), BF16)
    by_cols = lambda: pltpu.VMEM((2, nq, LANES, ta), BF16)
    return pl.pallas_call(
        body, name="attn_bwd", grid=(b, HEAD_PAIRS),
        in_specs=[blk(3 * LANES), blk(LANES), blk(LANES), rows5, rows5, blk(LANES)],
        out_specs=[blk(3 * LANES), pl.BlockSpec((None, s, LANES), lambda i, p: (i, 0, 0))],
        out_shape=[jax.ShapeDtypeStruct((b, s, 3 * D_MODEL), BF16), jax.ShapeDtypeStruct((b, s, LANES), F32)],
        scratch_shapes=[by_rows(), by_rows(), by_cols(), by_cols(), by_cols(), by_cols(),
                        pltpu.VMEM((nq, ta, LANES), F32), pltpu.VMEM((nq, ta, LANES), F32)],
        compiler_params=_cparams(("parallel", "arbitrary")),
    )(qkv3, do3, y3, lse5, crow5, cexp3)


def _shifted(v, ks, rows, s):
    return [jnp.where(rows >= k, pltpu.roll(v, k, 0), 0.0) if k > 0
            else jnp.where(rows < s + k, pltpu.roll(v, s + k, 0), 0.0) for k in ks]


def _rnn_common(xr, cw_ref, cb_ref, bda_ref, bdx_ref, ba_ref, bx_ref, lam_ref, s):
    rows = _iota((s, LANES), 0)
    x1, x2, x3 = _shifted(xr, (1, 2, 3), rows, s)
    xc = cb_ref[...] + cw_ref[0:1, :] * x3
    xc = xc + cw_ref[1:2, :] * x2
    xc = xc + cw_ref[2:3, :] * x1
    xc = xc + cw_ref[3:4, :] * xr
    xcb = xc.astype(BF16)
    r = _sigmoid(_dot(xcb, bda_ref[...]) + ba_ref[...])
    i = _sigmoid(_dot(xcb, bdx_ref[...]) + bx_ref[...])
    sp = _softplus(-lam_ref[...])
    log_a = (-RG_C * r) * sp
    a = jnp.exp(log_a)
    a2 = a * a
    sq = jnp.sqrt(jnp.maximum(_one_minus_exp(log_a + log_a, a2), 0.0))
    return rows, (x1, x2, x3), xc, xcb, r, i, sp, a, a2, sq


def _scan_down(a, u, rows, s, s1, s2):
    low = rows & 7
    for sh in (1, 2, 4):
        keep = low >= sh
        u = u + a * jnp.where(keep, pltpu.roll(u, sh, 0), 0.0)
        a = a * jnp.where(keep, pltpu.roll(a, sh, 0), 1.0)
    ng = s // 8
    s1[...] = a
    s2[...] = u
    at = s1[pl.ds(7, ng, stride=8), :]
    ut = s2[pl.ds(7, ng, stride=8), :]
    grow = _iota((ng, LANES), 0)
    sh = 1
    while sh < ng:
        keep = grow >= sh
        ut = ut + at * jnp.where(keep, pltpu.roll(ut, sh, 0), 0.0)
        if sh * 2 < ng:
            at = at * jnp.where(keep, pltpu.roll(at, sh, 0), 1.0)
        sh *= 2
    h_in = jnp.where(grow >= 1, pltpu.roll(ut, 1, 0), 0.0)
    for k in range(8):
        s1[pl.ds(k, ng, stride=8), :] = h_in
    return u + a * s1[...]


def _scan_up(a, g, rows, s, s1, s2):
    low = rows & 7
    for sh in (1, 2, 4):
        keep = low < 8 - sh
        g = g + a * jnp.where(keep, pltpu.roll(g, s - sh, 0), 0.0)
        a = a * jnp.where(keep, pltpu.roll(a, s - sh, 0), 1.0)
    ng = s // 8
    s1[...] = a
    s2[...] = g
    at = s1[pl.ds(0, ng, stride=8), :]
    gt = s2[pl.ds(0, ng, stride=8), :]
    grow = _iota((ng, LANES), 0)
    sh = 1
    while sh < ng:
        keep = grow < ng - sh
        gt = gt + at * jnp.where(keep, pltpu.roll(gt, ng - sh, 0), 0.0)
        if sh * 2 < ng:
            at = at * jnp.where(keep, pltpu.roll(at, ng - sh, 0), 1.0)
        sh *= 2
    g_in = jnp.where(grow < ng - 1, pltpu.roll(gt, ng - 1, 0), 0.0)
    for k in range(8):
        s1[pl.ds(k, ng, stride=8), :] = g_in
    return g + a * s1[...]


def _rnn_specs(s):
    blk = lambda off: pl.BlockSpec((None, s, LANES), lambda cb, i: (i, 0, off + cb))
    vec = lambda r: pl.BlockSpec((r, LANES), lambda cb, i: (0, cb))
    mat = pl.BlockSpec((None, LANES, LANES), lambda cb, i: (cb, 0, 0))
    return blk, vec, mat


def _rnn_fwd(zrest3, conv_w, conv_b, bda, bdx, ba, bx, lam):
    b, s, _ = zrest3.shape

    def body(xr_ref, g_ref, cw_ref, cb_ref, bda_ref, bdx_ref, ba_ref, bx_ref, lam_ref, h_ref, gr_ref, s1, s2):
        xr = xr_ref[...].astype(F32)
        rows, _, xc, _, _, i, _, a, _, sq = _rnn_common(
            xr, cw_ref, cb_ref, bda_ref, bdx_ref, ba_ref, bx_ref, lam_ref, s)
        h = _scan_down(a, sq * (i * xc), rows, s, s1, s2)
        h_ref[...] = h
        g = g_ref[...].astype(F32)
        gr_ref[...] = (h * (g * _sigmoid(g))).astype(BF16)

    blk, vec, mat = _rnn_specs(s)
    return pl.pallas_call(
        body, name="rnn_fwd", grid=(N_CBLK, b),
        in_specs=[blk(N_CBLK), blk(2 * N_CBLK), vec(CONV_W), vec(1), mat, mat, vec(1), vec(1), vec(1)],
        out_specs=[blk(0), blk(0)],
        out_shape=[jax.ShapeDtypeStruct((b, s, D_MODEL), F32), jax.ShapeDtypeStruct((b, s, D_MODEL), BF16)],
        scratch_shapes=[pltpu.VMEM((s, LANES), F32), pltpu.VMEM((s, LANES), F32)],
        compiler_params=_cparams(("parallel", "parallel")),
    )(zrest3, zrest3, conv_w, conv_b, bda, bdx, ba, bx, lam)


def _rnn_bwd(zrest3, h3, dh3, conv_w, conv_b, bda, bdx, ba, bx, lam):
    b, s, _ = zrest3.shape

    def body(xr_ref, h_ref, dh_ref, cw_ref, cb_ref, bda_ref, bdx_ref, ba_ref, bx_ref, lam_ref,
             dxr_ref, pv_ref, dbd_ref, s1, s2):
        @pl.when(pl.program_id(1) == 0)
        def _():
            pv_ref[...] = jnp.zeros_like(pv_ref)
            dbd_ref[...] = jnp.zeros_like(dbd_ref)

        xr = xr_ref[...].astype(F32)
        rows, (x1, x2, x3), xc, xcb, r, i, sp, a, a2, sq = _rnn_common(
            xr, cw_ref, cb_ref, bda_ref, bdx_ref, ba_ref, bx_ref, lam_ref, s)
        (a_next,) = _shifted(a, (-1,), rows, s)
        g = _scan_up(a_next, dh_ref[...], rows, s, s1, s2)
        (hp,) = _shifted(h_ref[...], (1,), rows, s)
        da = g * hp
        dsq = g * (i * xc)
        di = g * (sq * xc)
        dxc = g * (sq * i)
        dlog = da * a - dsq * (a2 / sq)
        dr = dlog * (-RG_C * sp)
        dpr = dr * (r * (1.0 - r))
        dpi = di * (i * (1.0 - i))
        dprb = dpr.astype(BF16)
        dpib = dpi.astype(BF16)
        dxc = dxc + _dot_nt(dprb, bda_ref[...]) + _dot_nt(dpib, bdx_ref[...])

        up1, up2, up3 = _shifted(dxc, (-1, -2, -3), rows, s)
        dxr = cw_ref[3:4, :] * dxc + cw_ref[2:3, :] * up1 + cw_ref[1:2, :] * up2 + cw_ref[0:1, :] * up3
        dxr_ref[...] = dxr.astype(BF16)

        def colsum(v):
            return jnp.sum(v, axis=0, keepdims=True)

        pv_ref[0:1, :] += colsum(dxc * x3)
        pv_ref[1:2, :] += colsum(dxc * x2)
        pv_ref[2:3, :] += colsum(dxc * x1)
        pv_ref[3:4, :] += colsum(dxc * xr)
        pv_ref[4:5, :] += colsum(dxc)
        pv_ref[5:6, :] += colsum(dpr)
        pv_ref[6:7, :] += colsum(dpi)
        pv_ref[7:8, :] += colsum(dlog * r) * (RG_C * _sigmoid(-lam_ref[...]))
        dbd_ref[0] += _dot_tn(xcb, dprb)
        dbd_ref[1] += _dot_tn(xcb, dpib)

    blk, vec, mat = _rnn_specs(s)
    hblk = pl.BlockSpec((None, s, LANES), lambda cb, i: (i, 0, cb))
    return pl.pallas_call(
        body, name="rnn_bwd", grid=(N_CBLK, b),
        in_specs=[blk(N_CBLK), hblk, hblk, vec(CONV_W), vec(1), mat, mat, vec(1), vec(1), vec(1)],
        out_specs=[hblk, pl.BlockSpec((8, LANES), lambda cb, i: (0, cb)),
                   pl.BlockSpec((None, 2, LANES, LANES), lambda cb, i: (cb, 0, 0, 0))],
        out_shape=[jax.ShapeDtypeStruct((b, s, D_MODEL), BF16), jax.ShapeDtypeStruct((8, D_MODEL), F32),
                   jax.ShapeDtypeStruct((N_CBLK, 2, LANES, LANES), F32)],
        scratch_shapes=[pltpu.VMEM((s, LANES), F32), pltpu.VMEM((s, LANES), F32)],
        compiler_params=_cparams(("parallel", "arbitrary")),
    )(zrest3, h3, dh3, conv_w, conv_b, bda, bdx, ba, bx, lam)


def _branch_merge(ga, gr, wa, wr, zrest):
    t = ga.shape[0]
    tm = min(512, t)
    tn = D_MODEL

    def body(ga_ref, gr_ref, wa_ref, wr_ref, mga_ref, mgr_ref, ya_ref, yr_ref, m_ref):
        ya = _dot(ga_ref[...], wa_ref[...])
        yr = _dot(gr_ref[...], wr_ref[...])
        ya_ref[...] = ya.astype(BF16)
        yr_ref[...] = yr.astype(BF16)
        m_ref[...] = (_sigmoid(mga_ref[...].astype(F32)) * ya + _sigmoid(mgr_ref[...].astype(F32)) * yr).astype(BF16)

    nj = D_MODEL // tn
    act = pl.BlockSpec((tm, D_MODEL), lambda i, j: (i, 0))
    wgt = pl.BlockSpec((D_MODEL, tn), lambda i, j: (0, j))
    out = pl.BlockSpec((tm, tn), lambda i, j: (i, j))
    return pl.pallas_call(
        body, name="branch_merge", grid=(t // tm, nj),
        in_specs=[act, act, wgt, wgt, pl.BlockSpec((tm, tn), lambda i, j: (i, 3 * nj + j)),
                  pl.BlockSpec((tm, tn), lambda i, j: (i, 4 * nj + j))],
        out_specs=[out, out, out],
        out_shape=[jax.ShapeDtypeStruct((t, D_MODEL), BF16), jax.ShapeDtypeStruct((t, D_MODEL), BF16),
                   jax.ShapeDtypeStruct((t, D_MODEL), BF16)],
        compiler_params=_cparams(("parallel", "parallel")),
    )(ga, gr, wa, wr, zrest, zrest)


def _out_loss(m, wout, x2, tgt2, wpost):
    t = m.shape[0]
    tm = min(512, t)

    def body(m_ref, w_ref, x_ref, t_ref, wp_ref, dy_ref, do_ref, acc_ref):
        @pl.when(pl.program_id(0) == 0)
        def _():
            acc_ref[...] = jnp.zeros_like(acc_ref)

        o = _dot(m_ref[...], w_ref[...])
        r2 = lax.rsqrt(jnp.mean(o * o, axis=-1, keepdims=True) + NORM_EPS)
        n = o * r2
        wp = wp_ref[...]
        err = (x_ref[...] + n * wp) - t_ref[...]
        dy = err * (1.0 / D_MODEL)
        dn = dy * wp
        do = r2 * (dn - n * jnp.mean(dn * n, axis=-1, keepdims=True))
        dy_ref[...] = dy
        do_ref[...] = do.astype(BF16)
        acc_ref[0:1, :] += jnp.sum(dy * n, axis=0, keepdims=True)
        acc_ref[1:2, :] += jnp.sum(err * err, axis=0, keepdims=True)

    row = pl.BlockSpec((tm, D_MODEL), lambda i: (i, 0))
    return pl.pallas_call(
        body, name="out_loss", grid=(t // tm,),
        in_specs=[row, pl.BlockSpec((D_MODEL, D_MODEL), lambda i: (0, 0)), row, row,
                  pl.BlockSpec((1, D_MODEL), lambda i: (0, 0))],
        out_specs=[row, row, pl.BlockSpec((8, D_MODEL), lambda i: (0, 0))],
        out_shape=[jax.ShapeDtypeStruct((t, D_MODEL), F32), jax.ShapeDtypeStruct((t, D_MODEL), BF16),
                   jax.ShapeDtypeStruct((8, D_MODEL), F32)],
        compiler_params=_cparams(("arbitrary",)),
    )(m, wout, x2, tgt2, wpost)


def _merge_bwd(do, wout, zrest, ya, yr):
    t = do.shape[0]
    tm = min(512, t)
    tn = D_MODEL
    nj = D_MODEL // tn

    def body(do_ref, w_ref, mga_ref, mgr_ref, ya_ref, yr_ref, dya_ref, dyr_ref, dmga_ref, dmgr_ref):
        dm = _dot_nt(do_ref[...], w_ref[...])
        sa = _sigmoid(mga_ref[...].astype(F32))
        sr = _sigmoid(mgr_ref[...].astype(F32))
        dya_ref[...] = (dm * sa).astype(BF16)
        dyr_ref[...] = (dm * sr).astype(BF16)
        dmga_ref[...] = (dm * ya_ref[...].astype(F32) * (sa * (1.0 - sa))).astype(BF16)
        dmgr_ref[...] = (dm * yr_ref[...].astype(F32) * (sr * (1.0 - sr))).astype(BF16)

    out = pl.BlockSpec((tm, tn), lambda i, j: (i, j))
    bf = jax.ShapeDtypeStruct((t, D_MODEL), BF16)
    return pl.pallas_call(
        body, name="merge_bwd", grid=(t // tm, nj),
        in_specs=[pl.BlockSpec((tm, D_MODEL), lambda i, j: (i, 0)), pl.BlockSpec((tn, D_MODEL), lambda i, j: (j, 0)),
                  pl.BlockSpec((tm, tn), lambda i, j: (i, 3 * nj + j)),
                  pl.BlockSpec((tm, tn), lambda i, j: (i, 4 * nj + j)), out, out],
        out_specs=[out, out, out, out],
        out_shape=[bf, bf, bf, bf],
        compiler_params=_cparams(("parallel", "parallel")),
    )(do, wout, zrest, zrest, ya, yr)


def _branch_bwd(dya, dyr, wa, wr, zrest, yatt, ylru):
    t = dya.shape[0]
    tm = min(512, t)
    tn = D_MODEL
    nj = D_MODEL // tn

    def body(dya_ref, dyr_ref, wa_ref, wr_ref, ga_ref, gr_ref, ya_ref, yl_ref,
             dyatt_ref, dga_ref, dyl_ref, dgr_ref):
        dga = _dot_nt(dya_ref[...], wa_ref[...])
        dgr = _dot_nt(dyr_ref[...], wr_ref[...])
        g = ga_ref[...].astype(F32)
        sg = _sigmoid(g)
        dyatt_ref[...] = (dga * (g * sg)).astype(BF16)
        dga_ref[...] = (dga * ya_ref[...] * (sg * (1.0 + g * (1.0 - sg)))).astype(BF16)
        g = gr_ref[...].astype(F32)
        sg = _sigmoid(g)
        dyl_ref[...] = dgr * (g * sg)
        dgr_ref[...] = (dgr * yl_ref[...] * (sg * (1.0 + g * (1.0 - sg)))).astype(BF16)

    act = pl.BlockSpec((tm, D_MODEL), lambda i, j: (i, 0))
    wgt = pl.BlockSpec((tn, D_MODEL), lambda i, j: (j, 0))
    out = pl.BlockSpec((tm, tn), lambda i, j: (i, j))
    bf = jax.ShapeDtypeStruct((t, D_MODEL), BF16)
    return pl.pallas_call(
        body, name="branch_bwd", grid=(t // tm, nj),
        in_specs=[act, act, wgt, wgt, pl.BlockSpec((tm, tn), lambda i, j: (i, j)),
                  pl.BlockSpec((tm, tn), lambda i, j: (i, 2 * nj + j)), out, out],
        out_specs=[out, out, out, out],
        out_shape=[bf, bf, jax.ShapeDtypeStruct((t, D_MODEL), F32), bf],
        compiler_params=_cparams(("parallel", "parallel")),
    )(dya, dyr, wa, wr, zrest, zrest, yatt, ylru)


def _dh_partial(parts, after, name):
    t = parts[0][0].shape[0]
    tm = min(256, t)
    np_ = len(parts)

    def body(*refs):
        o_ref = refs[-1]
        acc = _dot(refs[0][...], refs[np_][...])
        for p in range(1, np_):
            acc = acc + _dot(refs[p][...], refs[np_ + p][...])
        o_ref[...] = acc

    in_specs = [pl.BlockSpec((tm, dz.shape[1]), lambda i: (i, 0)) for dz, _ in parts]
    in_specs += [pl.BlockSpec(w.shape, lambda i: (0, 0)) for _, w in parts]
    in_specs += [pl.BlockSpec(after.shape, lambda i: (0, 0))]
    return pl.pallas_call(
        body, name=name, grid=(t // tm,),
        in_specs=in_specs,
        out_specs=pl.BlockSpec((tm, D_MODEL), lambda i: (i, 0)),
        out_shape=jax.ShapeDtypeStruct((t, D_MODEL), F32),
        compiler_params=_cparams(("parallel",), vmem_mb=48),
    )(*[dz for dz, _ in parts], *[w for _, w in parts], after)


def _dh_final(parts, acc_in, x2, dy, wpre):
    t = x2.shape[0]
    tm = min(256, t)
    np_ = len(parts)

    def body(*refs):
        acc_ref, x_ref, dy_ref, w_ref = refs[2 * np_:2 * np_ + 4]
        gx_ref, pw_ref = refs[2 * np_ + 4:]

        @pl.when(pl.program_id(0) == 0)
        def _():
            pw_ref[...] = jnp.zeros_like(pw_ref)

        dh = acc_ref[...]
        for p in range(np_):
            dh = dh + _dot(refs[p][...], refs[np_ + p][...])
        x = x_ref[...]
        r = lax.rsqrt(jnp.mean(x * x, axis=-1, keepdims=True) + NORM_EPS)
        xn = x * r
        dxn = dh * w_ref[...]
        gx_ref[...] = r * (dxn - xn * jnp.mean(dxn * xn, axis=-1, keepdims=True)) + dy_ref[...]
        pw_ref[0:1, :] += jnp.sum(dh * xn, axis=0, keepdims=True)

    row = pl.BlockSpec((tm, D_MODEL), lambda i: (i, 0))
    in_specs = [pl.BlockSpec((tm, dz.shape[1]), lambda i: (i, 0)) for dz, _ in parts]
    in_specs += [pl.BlockSpec(w.shape, lambda i: (0, 0)) for _, w in parts]
    in_specs += [row, row, row, pl.BlockSpec((1, D_MODEL), lambda i: (0, 0))]
    return pl.pallas_call(
        body, name="dh_final", grid=(t // tm,),
        in_specs=in_specs,
        out_specs=[row, pl.BlockSpec((8, D_MODEL), lambda i: (0, 0))],
        out_shape=[jax.ShapeDtypeStruct((t, D_MODEL), F32), jax.ShapeDtypeStruct((8, D_MODEL), F32)],
        compiler_params=_cparams(("arbitrary",), vmem_mb=48),
    )(*[dz for dz, _ in parts], *[w for _, w in parts], acc_in, x2, dy, wpre)


def _adamw(w, g, m, v):
    m = ADAM_B1 * m + (1.0 - ADAM_B1) * g
    v = ADAM_B2 * v + (1.0 - ADAM_B2) * (g * g)
    m_hat = m / (1.0 - ADAM_B1 ** ADAM_STEP)
    v_hat = v / (1.0 - ADAM_B2 ** ADAM_STEP)
    delta = -ADAM_LR * (m_hat / (jnp.sqrt(v_hat) + ADAM_EPS) + ADAM_WD * w)
    return delta, m, v


def _reduce_adamw(own, parts, place, w, m, v, name):
    r, c = w.shape
    blk, nblk, at = _blocks_2d(r, c)

    def body(place_ref, own_ref, p_ref, w_ref, m_ref, v_ref, g_ref, d_ref, nm_ref, nv_ref):
        mine = place_ref[1]
        own_blk = own_ref[...]
        g = jnp.where(mine == 0, own_blk, p_ref[0].astype(F32))
        for j in range(1, N_CHIPS):
            g = g + jnp.where(mine == j, own_blk, p_ref[j].astype(F32))
        d, nm, nv = _adamw(w_ref[...], g, m_ref[...], v_ref[...])
        g_ref[...] = g
        d_ref[...] = d
        nm_ref[...] = nm
        nv_ref[...] = nv

    row = pl.BlockSpec(blk, lambda i, pr: at(i))
    sh = jax.ShapeDtypeStruct((r, c), F32)
    grid_spec = pltpu.PrefetchScalarGridSpec(
        num_scalar_prefetch=1, grid=(nblk,),
        in_specs=[row, pl.BlockSpec((N_CHIPS,) + blk, lambda i, pr: (0,) + at(i)), row, row, row],
        out_specs=[row, row, row, row])
    return pl.pallas_call(
        body, name=name, grid_spec=grid_spec, out_shape=[sh, sh, sh, sh],
        compiler_params=_cparams(("parallel",)),
    )(place, own, parts, w, m, v)


def _reduce_adamw_stacked(own, parts, place, triples, name):
    n = len(triples)
    _, r, c = triples[0][0].shape

    def body(place_ref, own_ref, p_ref, *refs):
        ins, outs = refs[:3 * n], refs[3 * n:]
        mine = place_ref[1]
        for i in range(n):
            rows = slice(i * r, (i + 1) * r)
            own_blk = own_ref[rows, :]
            g = jnp.where(mine == 0, own_blk, p_ref[0, rows, :].astype(F32))
            for j in range(1, N_CHIPS):
                g = g + jnp.where(mine == j, own_blk, p_ref[j, rows, :].astype(F32))
            d, nm, nv = _adamw(ins[3 * i][0], g, ins[3 * i + 1][0], ins[3 * i + 2][0])
            for k, val in enumerate((g, d, nm, nv)):
                outs[4 * i + k][0] = val

    whole = lambda shape: pl.BlockSpec(shape, lambda i, pr: (0,) * len(shape))
    grid_spec = pltpu.PrefetchScalarGridSpec(
        num_scalar_prefetch=1, grid=(1,),
        in_specs=[whole(own.shape), whole(parts.shape)] + [whole((1, r, c))] * (3 * n),
        out_specs=[whole((1, r, c))] * (4 * n))
    res = pl.pallas_call(
        body, name=name, grid_spec=grid_spec,
        out_shape=[jax.ShapeDtypeStruct((1, r, c), F32)] * (4 * n),
        compiler_params=_cparams(("arbitrary",)),
    )(place, own, parts, *[a for t3 in triples for a in t3])
    return [res[4 * i:4 * i + 4] for i in range(n)]


def _interleave_qkv(a):
    lead = a.shape[:-1]
    return a.reshape(lead + (3, HEAD_PAIRS, LANES)).swapaxes(-3, -2).reshape(lead + (3 * D_MODEL,))


def _deinterleave_qkv(a):
    lead = a.shape[:-1]
    return a.reshape(lead + (HEAD_PAIRS, 3, LANES)).swapaxes(-3, -2).reshape(lead + (3 * D_MODEL,))


def _interleave_rows(a):
    return a.reshape(3, HEAD_PAIRS, LANES, a.shape[1]).swapaxes(0, 1).reshape(a.shape)


def _deinterleave_rows(a):
    return a.reshape(HEAD_PAIRS, 3, LANES, a.shape[1]).swapaxes(0, 1).reshape(a.shape)


def _pack_small(pre, conv_b, rg_ba, rg_bx, lam, post, loss_row, b_in, conv_w_full, rg_wa, rg_wx):
    z = jnp.zeros((1, D_MODEL), F32)
    b_used = jnp.concatenate([b_in[:, 0:3 * D_MODEL], b_in[:, 3 * D_MODEL + HEADS:IN_TOTAL]], axis=1)
    b_f = jnp.pad(b_in[:, 3 * D_MODEL:3 * D_MODEL + HEADS], ((0, 0), (0, D_MODEL - HEADS)))
    return jnp.concatenate([
        pre, conv_b, rg_ba, rg_bx, lam, post, loss_row, z,
        b_used.reshape(9, D_MODEL), b_f, conv_w_full, z, z,
        rg_wa.reshape(64, D_MODEL), rg_wx.reshape(64, D_MODEL)], axis=0)


def _unpack_small(p):
    b_used = p[8:17].reshape(1, 9 * D_MODEL)
    b_in = jnp.concatenate([b_used[:, 0:3 * D_MODEL], p[17:18, 0:HEADS], b_used[:, 3 * D_MODEL:]], axis=1)
    return dict(pre_norm_w=p[0:1], conv_b=p[1:2], rg_ba=p[2:3], rg_bx=p[3:4], rg_lambda=p[4:5],
                post_norm_w=p[5:6], loss_row=p[6:7], b_in=b_in, conv_w_full=p[18:22],
                rg_wa=p[24:88].reshape(1, 16, 64, 64), rg_wx=p[88:152].reshape(1, 16, 64, 64))


def _reduce_small(parts, w, m, v, vectors):
    nvec = len(vectors)

    def body(p_ref, w_ref, m_ref, v_ref, *refs):
        ins, outs = refs[:3 * nvec], refs[3 * nvec:]
        g = p_ref[0]
        for j in range(1, N_DEV):
            g = g + p_ref[j]
        d, nm, nv = _adamw(w_ref[...], g, m_ref[...], v_ref[...])
        for k, val in enumerate((g, d, nm, nv)):
            outs[k][...] = val
        for i in range(nvec):
            gi = g[i:i + 1, :]
            di, nmi, nvi = _adamw(ins[3 * i][...], gi, ins[3 * i + 1][...], ins[3 * i + 2][...])
            for k, val in enumerate((gi, di, nmi, nvi)):
                outs[4 + 4 * i + k][...] = val
        outs[-1][...] = jnp.zeros((8, LANES), F32) + (0.5 / D_MODEL) * jnp.sum(g[LOSS_ROW:LOSS_ROW + 1, :])

    sh = jax.ShapeDtypeStruct((SMALL_ROWS, D_MODEL), F32)
    vec = jax.ShapeDtypeStruct((1, D_MODEL), F32)
    res = pl.pallas_call(
        body, name="reduce_small",
        out_shape=[sh, sh, sh, sh] + [vec] * (4 * nvec) + [jax.ShapeDtypeStruct((8, LANES), F32)],
    )(parts, w, m, v, *[a for t3 in vectors for a in t3])
    return res[:4], [res[4 + 4 * i:8 + 4 * i] for i in range(nvec)], res[-1]


def kernel(x, pre_norm_w, w_in, b_in, conv_w, conv_b, rg_wa, rg_ba, rg_wx, rg_bx, rg_lambda, w_branch_a, w_branch_r, w_out, post_norm_w, loss_target, m_pre_norm_w, m_w_in, m_b_in, m_conv_w, m_conv_b, m_rg_wa, m_rg_ba, m_rg_wx, m_rg_bx, m_rg_lambda, m_w_branch_a, m_w_branch_r, m_w_out, m_post_norm_w, v_pre_norm_w, v_w_in, v_b_in, v_conv_w, v_conv_b, v_rg_wa, v_rg_ba, v_rg_wx, v_rg_bx, v_rg_lambda, v_w_branch_a, v_w_branch_r, v_w_out, v_post_norm_w):
    b, s, _ = x.shape
    t = b * s
    me = 4 * lax.axis_index("x") + 2 * lax.axis_index("y") + lax.axis_index("c")
    shard_rows = D_MODEL // N_DEV

    place = jnp.stack([lax.axis_index("c"), 2 * lax.axis_index("x") + lax.axis_index("y")]).astype(jnp.int32)
    w_in_all = _gather(w_in[0].T.astype(BF16), "gather_w_in")
    wt_full = w_in_all.reshape(IN_TOTAL, D_MODEL)
    conv_terms = jnp.concatenate(_split3(conv_w[0]), axis=0)
    conv_pad = jnp.pad(conv_terms, ((0, 16 - 3 * CONV_W), (0, D_MODEL - LANES)))
    sq_stack = jnp.concatenate([w_branch_a[0].astype(BF16), w_branch_r[0].astype(BF16), w_out[0].astype(BF16),
                                conv_pad], axis=0)
    sq_sems, sq_src, sq_land, sq_token = _gather_start(sq_stack, w_in_all, "gather_w_sq_start")

    w_qkv = _interleave_rows(wt_full[0:3 * D_MODEL])
    w_f = jnp.pad(wt_full[3 * D_MODEL:3 * D_MODEL + HEADS], ((0, LANES - HEADS), (0, 0)))
    w_rest = wt_full[3 * D_MODEL + HEADS:IN_USED]
    b_qkv = _interleave_qkv(b_in[:, 0:3 * D_MODEL]) + sq_token[0, 0]
    b_f = jnp.pad(b_in[:, 3 * D_MODEL:3 * D_MODEL + HEADS], ((0, 0), (0, LANES - HEADS)))
    b_rest = b_in[:, 3 * D_MODEL + HEADS:IN_USED]

    def blockdiag(w):
        w2 = w.reshape(N_CBLK, 2, HEAD_DIM, HEAD_DIM)
        zz = jnp.zeros((N_CBLK, HEAD_DIM, HEAD_DIM), w.dtype)
        top = jnp.concatenate([w2[:, 0], zz], axis=2)
        bot = jnp.concatenate([zz, w2[:, 1]], axis=2)
        return jnp.concatenate([top, bot], axis=1).astype(BF16)

    bda, bdx = blockdiag(rg_wa[0]), blockdiag(rg_wx[0])

    x2 = x.reshape(t, D_MODEL)
    tgt2 = loss_target.reshape(t, D_MODEL)
    h = _prenorm(x2, pre_norm_w)
    qkv = _mm_bias(h, w_qkv, b_qkv, BF16, "inproj_qkv")
    zrest = _mm_bias(h, w_rest, b_rest, BF16, "inproj_rest")
    zf = _mm_bias(h, w_f, b_f, F32, "inproj_f")
    qkv3 = qkv.reshape(b, s, 3 * D_MODEL)
    zrest3 = zrest.reshape(b, s, 5 * D_MODEL)
    zf3 = zf.reshape(b, s, LANES)
    cexp3, crow = _fgate_fwd(zf3)
    yatt3, lse, ga3 = _attn_fwd(qkv3, cexp3, crow, zrest3)

    sq_all = _gather_wait(sq_sems, sq_src, sq_land, ga3, "gather_w_sq_wait")
    sq_all = lax.dynamic_update_slice(sq_all, sq_stack[None], (me, 0, 0))
    wa = sq_all[:, 0:shard_rows].reshape(D_MODEL, D_MODEL)
    wr = sq_all[:, shard_rows:2 * shard_rows].reshape(D_MODEL, D_MODEL)
    wo = sq_all[:, 2 * shard_rows:3 * shard_rows].reshape(D_MODEL, D_MODEL)
    conv_all = sq_all[:, 3 * shard_rows:3 * shard_rows + 3 * CONV_W, 0:LANES].astype(F32)
    conv_all = (conv_all[:, 0:CONV_W] + conv_all[:, CONV_W:2 * CONV_W]) + conv_all[:, 2 * CONV_W:3 * CONV_W]
    conv_full = conv_all.transpose(1, 0, 2).reshape(CONV_W, D_MODEL)

    ylru3, gr3 = _rnn_fwd(zrest3, conv_full, conv_b, bda, bdx, rg_ba, rg_bx, rg_lambda)
    ga, gr = ga3.reshape(t, D_MODEL), gr3.reshape(t, D_MODEL)
    ya, yr, mm = _branch_merge(ga, gr, wa, wr, zrest)
    dy, do, acc_out = _out_loss(mm, wo, x2, tgt2, post_norm_w)

    dya, dyr, dz_mga, dz_mgr = _merge_bwd(do, wo, zrest, ya, yr)
    dyatt, dz_ga, dylru, dz_gr = _branch_bwd(dya, dyr, wa, wr, zrest, yatt3.reshape(t, D_MODEL),
                                             ylru3.reshape(t, D_MODEL))
    dz_xr3, pvec, dbd = _rnn_bwd(zrest3, ylru3, dylru.reshape(b, s, D_MODEL), conv_full, conv_b, bda, bdx,
                                 rg_ba, rg_bx, rg_lambda)
    dqkv3, dc3 = _attn_bwd(qkv3, dyatt.reshape(b, s, D_MODEL), yatt3, lse, crow, cexp3)
    dz_f = _fgate_bwd(dc3, zf3).reshape(t, LANES)
    dz_qkv = dqkv3.reshape(t, 3 * D_MODEL)
    dz_xr = dz_xr3.reshape(t, D_MODEL)

    dw_qkv, db_qkv = _mm_tn(dz_qkv, h, "dw_qkv")
    dw_f, db_f = _mm_tn(dz_f, h, "dw_f")
    dw_parts, db_parts = [], []
    for nm, dzp in (("ga", dz_ga), ("xr", dz_xr), ("gr", dz_gr), ("mga", dz_mga), ("mgr", dz_mgr)):
        dwp, dbp = _mm_tn(dzp, h, "dw_" + nm)
        dw_parts.append(dwp)
        db_parts.append(dbp[0:1])
    dw_a, _ = _mm_tn(ga, dya, "dw_a")
    dw_r, _ = _mm_tn(gr, dyr, "dw_r")
    dw_o, _ = _mm_tn(mm, do, "dw_o")

    zeros_tail = jnp.zeros((IN_TOTAL - IN_USED, D_MODEL), F32)
    dwt_full = jnp.concatenate([_deinterleave_rows(dw_qkv), dw_f[0:HEADS]] + dw_parts + [zeros_tail], axis=0)
    dw_in_send = dwt_full.reshape(N_CHIPS, 2, W_SHARD, D_MODEL).transpose(1, 0, 2, 3)
    by_dest = lambda a: a.reshape(N_CHIPS, 2, shard_rows, D_MODEL).transpose(1, 0, 2, 3)
    dw_sq_send = jnp.concatenate([by_dest(dw_a), by_dest(dw_r), by_dest(dw_o)], axis=2)

    sib_in, sib_sq = _swap_with_sibling([dw_in_send, dw_sq_send], "swap_dw")
    chip_in, own_in = _pair_add(dw_in_send, sib_in, place, "pair_add_in")
    chip_sq, own_sq = _pair_add(dw_sq_send, sib_sq, place, "pair_add_sq")
    sems, sent, lands, token = _exchange_chips_start([chip_in, chip_sq], "exchange_dw_start")

    wt = lambda lo: w_rest[lo * D_MODEL:(lo + 1) * D_MODEL]
    dh_a = _dh_partial([(dz_qkv, w_qkv), (dz_f, w_f)], token, "dh_qkv")
    grad_x2, acc_pre = _dh_final(
        [(dz_ga, wt(0)), (dz_xr, wt(1)), (dz_gr, wt(2)), (dz_mga, wt(3)), (dz_mgr, wt(4))],
        dh_a, x2, dy, pre_norm_w)

    db_in_full = jnp.concatenate([_deinterleave_qkv(db_qkv[0:1]), db_f[0:1, 0:HEADS]] + db_parts
                                 + [jnp.zeros((1, IN_TOTAL - IN_USED), F32)], axis=1)
    d_rg_wa = jnp.stack([dbd[:, 0, 0:HEAD_DIM, 0:HEAD_DIM], dbd[:, 0, HEAD_DIM:, HEAD_DIM:]], axis=1)
    d_rg_wx = jnp.stack([dbd[:, 1, 0:HEAD_DIM, 0:HEAD_DIM], dbd[:, 1, HEAD_DIM:, HEAD_DIM:]], axis=1)
    small_g = _pack_small(acc_pre[0:1], pvec[4:5], pvec[5:6], pvec[6:7], pvec[7:8], acc_out[0:1], acc_out[1:2],
                          db_in_full, pvec[0:4], d_rg_wa, d_rg_wx)
    sm_sems, sm_src, sm_land, sm_token = _gather_start(small_g, grad_x2, "gather_small_start")
    recv_in, recv_sq = _exchange_chips_wait(sems, sent, lands, sm_token, "exchange_dw_wait")

    g_in, d_in, nm_in, nv_in = [a.T for a in _reduce_adamw(
        own_in, recv_in, place, w_in[0].T, m_w_in[0].T, v_w_in[0].T, "adamw_w_in")]
    sq_out = _reduce_adamw_stacked(
        own_sq, recv_sq, place,
        [(w_branch_a, m_w_branch_a, v_w_branch_a), (w_branch_r, m_w_branch_r, v_w_branch_r),
         (w_out, m_w_out, v_w_out)], "adamw_w_sq")
    small_all = _gather_wait(sm_sems, sm_src, sm_land, sq_out[2][1], "gather_small_wait")
    small_all = lax.dynamic_update_slice(small_all, small_g[None], (me, 0, 0))

    def place_conv(a):
        return lax.dynamic_update_slice(jnp.zeros((CONV_W, D_MODEL), F32), a[0], (0, me * LANES))

    zrow = jnp.zeros((1, D_MODEL), F32)
    vector_names = ["pre_norm_w", "conv_b", "rg_ba", "rg_bx", "rg_lambda", "post_norm_w"]
    vectors = [(pre_norm_w, m_pre_norm_w, v_pre_norm_w), (conv_b, m_conv_b, v_conv_b), (rg_ba, m_rg_ba, v_rg_ba),
               (rg_bx, m_rg_bx, v_rg_bx), (rg_lambda, m_rg_lambda, v_rg_lambda),
               (post_norm_w, m_post_norm_w, v_post_norm_w)]
    small_w = _pack_small(zrow, zrow, zrow, zrow, zrow, zrow, zrow, b_in, place_conv(conv_w), rg_wa[0], rg_wx[0])
    small_m = _pack_small(zrow, zrow, zrow, zrow, zrow, zrow, zrow, m_b_in, place_conv(m_conv_w), m_rg_wa[0],
                          m_rg_wx[0])
    small_v = _pack_small(zrow, zrow, zrow, zrow, zrow, zrow, zrow, v_b_in, place_conv(v_conv_w), v_rg_wa[0],
                          v_rg_wx[0])
    packed, vector_out, loss_tile = _reduce_small(small_all, small_w, small_m, small_v, vectors)
    outs_small = [_unpack_small(p) for p in packed]
    loss = loss_tile[0, 0]

    def leaf(kind, name):
        if name == "w_in":
            return (g_in, d_in, nm_in, nv_in)[kind][None]
        if name in ("w_branch_a", "w_branch_r", "w_out"):
            return sq_out[("w_branch_a", "w_branch_r", "w_out").index(name)][kind]
        if name == "conv_w":
            return lax.dynamic_slice(outs_small[kind]["conv_w_full"], (0, me * LANES), (CONV_W, LANES))[None]
        if name in vector_names:
            return vector_out[vector_names.index(name)][kind]
        return outs_small[kind][name]

    names = ["pre_norm_w", "w_in", "b_in", "conv_w", "conv_b", "rg_wa", "rg_ba", "rg_wx", "rg_bx", "rg_lambda",
             "w_branch_a", "w_branch_r", "w_out", "post_norm_w"]
    out = [loss, grad_x2.reshape(b, s, D_MODEL)]
    for kind in range(4):
        out += [leaf(kind, nm) for nm in names]
    return tuple(out)
```

```python
import jax
import jax.numpy as jnp
from jax import lax
from jax.experimental import pallas as pl
from jax.experimental.pallas import tpu as pltpu

F32 = jnp.float32
BF16 = jnp.bfloat16

N_DEV = 8
D_MODEL = 1024
HEADS = 16
HEAD_DIM = 64
HEAD_PAIRS = HEADS // 2
LANES = 128
N_CBLK = D_MODEL // LANES
CONV_W = 4
RG_C = 8.0
NORM_EPS = 1e-6
MASK_VALUE = -1e30
IN_USED = 8208
IN_TOTAL = 9232
W_SHARD = IN_TOTAL // N_DEV

ADAM_LR = 0.001
ADAM_B1 = 0.9
ADAM_B2 = 0.999
ADAM_EPS = 1e-08
ADAM_WD = 0.01
ADAM_STEP = 10

ATT_TILE_FWD = 256
ATT_TILE_BWD = 512
SCAN_TILE = 256
SMALL_ROWS = 152
LOSS_ROW = 6


def _cparams(sem=None, vmem_mb=None):
    kw = {}
    if sem is not None:
        kw["dimension_semantics"] = sem
    if vmem_mb is not None:
        kw["vmem_limit_bytes"] = vmem_mb * 1024 * 1024
    return pltpu.CompilerParams(**kw)


def _sigmoid(x):
    return 1.0 / (1.0 + jnp.exp(-x))


def _softplus(x):
    return jnp.maximum(x, 0.0) + jnp.log1p(jnp.exp(-jnp.abs(x)))


def _one_minus_exp(y, exp_y):
    series = -y * (1.0 + y * (1.0 / 2 + y * (1.0 / 6 + y * (1.0 / 24 + y * (1.0 / 120)))))
    return jnp.where(y > -0.0625, series, 1.0 - exp_y)


def _split3(x):
    hi = x.astype(BF16)
    r1 = x - hi.astype(F32)
    mid = r1.astype(BF16)
    lo = (r1 - mid.astype(F32)).astype(BF16)
    return hi, mid, lo


def _dot(a, b):
    return jnp.dot(a, b, preferred_element_type=F32)


def _dot_nt(a, b):
    return lax.dot_general(a, b, (((1,), (1,)), ((), ())), preferred_element_type=F32)


def _dot_tn(a, b):
    return lax.dot_general(a, b, (((0,), (0,)), ((), ())), preferred_element_type=F32)


def _iota(shape, dim):
    return lax.broadcasted_iota(jnp.int32, shape, dim)


_ANY = pl.BlockSpec(memory_space=pl.ANY)
_MESH = pl.DeviceIdType.MESH
N_CHIPS = 4


def _place():
    x, y, c = lax.axis_index("x"), lax.axis_index("y"), lax.axis_index("c")
    other_chips = [(1 - x, y), (x, 1 - y), (1 - x, 1 - y)]
    return x, y, c, other_chips


def _gather(x_shard, name):
    def body(x_ref, out_ref, send_sems, recv_sems, local_sem):
        x, y, c, chips = _place()
        me, sibling = (x, y, c), (x, y, 1 - c)

        def slot(p):
            return out_ref.at[4 * p[0] + 2 * p[1] + p[2]]

        def copy(k, block, to, src=None):
            return pltpu.make_async_remote_copy(
                src_ref=slot(block) if src is None else src, dst_ref=slot(block),
                send_sem=send_sems.at[k], recv_sem=recv_sems.at[k], device_id=to, device_id_type=_MESH)

        mine = pltpu.make_async_copy(x_ref, slot(me), local_sem)
        mine.start()
        first = [copy(0, me, sibling, src=x_ref)]
        first += [copy(1 + j, me, (*chip, c), src=x_ref) for j, chip in enumerate(chips)]
        for cp in first:
            cp.start()
        passed = [copy(4 + j, (*chip, c), sibling) for j, chip in enumerate(chips)]
        for j, chip in enumerate(chips):
            copy(1 + j, (*chip, c), me).wait_recv()
            passed[j].start()
        copy(0, sibling, me).wait_recv()
        for j, chip in enumerate(chips):
            copy(4 + j, (*chip, 1 - c), me).wait_recv()
        for cp in first + passed:
            cp.wait_send()
        mine.wait()

    return pl.pallas_call(
        body, name=name,
        out_shape=jax.ShapeDtypeStruct((N_DEV,) + tuple(x_shard.shape), x_shard.dtype),
        in_specs=[_ANY], out_specs=_ANY,
        scratch_shapes=[pltpu.SemaphoreType.DMA((7,)), pltpu.SemaphoreType.DMA((7,)), pltpu.SemaphoreType.DMA],
    )(x_shard)


def _swap_with_sibling(srcs, name):
    n = len(srcs)

    def body(*refs):
        src_refs, out_refs = refs[:n], refs[n:2 * n]
        send_sems, recv_sems = refs[2 * n:]
        x, y, c, _ = _place()
        cps = [pltpu.make_async_remote_copy(
            src_ref=src_refs[i].at[1 - c], dst_ref=out_refs[i], send_sem=send_sems.at[i], recv_sem=recv_sems.at[i],
            device_id=(x, y, 1 - c), device_id_type=_MESH) for i in range(n)]
        for cp in cps:
            cp.start()
        for cp in cps:
            cp.wait()

    return pl.pallas_call(
        body, name=name,
        out_shape=[jax.ShapeDtypeStruct(a.shape[1:], a.dtype) for a in srcs],
        in_specs=[_ANY] * n, out_specs=[_ANY] * n,
        scratch_shapes=[pltpu.SemaphoreType.DMA((n,)), pltpu.SemaphoreType.DMA((n,))],
    )(*srcs)


def _blocks_2d(r, c):
    if r % 128 == 0:
        return (128, c), r // 128, lambda i: (i, 0)
    return (r, 256), c // 256, lambda i: (0, i)


def _pair_add(src, recv, place, name):
    _, _, r, c = src.shape
    blk, nblk, at = _blocks_2d(r, c)

    def body(place_ref, a_ref, b_ref, q16_ref, own_ref):
        q = a_ref[...] + b_ref[...]
        q16_ref[...] = q.astype(BF16)

        @pl.when(pl.program_id(1) == place_ref[1])
        def _():
            own_ref[...] = q

    grid_spec = pltpu.PrefetchScalarGridSpec(
        num_scalar_prefetch=1, grid=(nblk, N_CHIPS),
        in_specs=[pl.BlockSpec((None, None) + blk, lambda i, j, pr: (pr[0], j) + at(i)),
                  pl.BlockSpec((None,) + blk, lambda i, j, pr: (j,) + at(i))],
        out_specs=[pl.BlockSpec((None,) + blk, lambda i, j, pr: (j,) + at(i)),
                   pl.BlockSpec(blk, lambda i, j, pr: at(i))])
    return pl.pallas_call(
        body, name=name, grid_spec=grid_spec,
        out_shape=[jax.ShapeDtypeStruct((N_CHIPS, r, c), BF16), jax.ShapeDtypeStruct((r, c), F32)],
        compiler_params=_cparams(("parallel", "arbitrary")),
    )(place, src, recv)


_HBM = pl.BlockSpec(memory_space=pltpu.HBM)
_SEM = pl.BlockSpec(memory_space=pltpu.SEMAPHORE)
_DATAFLOW = pltpu.SideEffectType.DATAFLOW_SIDE_EFFECTING


def _chip_copy(src_ref, land_ref, send_sem, recv_sem, k, chips, c, land):
    chip = chips[k]
    return pltpu.make_async_remote_copy(
        src_ref=src_ref.at[2 * chip[0] + chip[1]], dst_ref=land_ref.at[land],
        send_sem=send_sem, recv_sem=recv_sem, device_id=(*chip, c), device_id_type=_MESH)


def _exchange_chips_start(srcs, name):
    n = len(srcs)
    ncp = 3 * n

    def body(*refs):
        src_refs, land_refs = refs[:n], refs[n:2 * n]
        sems = refs[4 * n:4 * n + 2 * ncp]
        token = refs[-1]
        x, y, c, chips = _place()
        for i in range(n):
            for k in range(3):
                j = 3 * i + k
                _chip_copy(src_refs[i], land_refs[i], sems[j], sems[ncp + j], k, chips, c, 2 * x + y).start()
        token[...] = jnp.zeros_like(token)

    hbm = [pltpu.HBM(a.shape, a.dtype) for a in srcs]
    lands = [pltpu.with_memory_space_constraint(lax.empty(a.shape, a.dtype), pltpu.HBM) for a in srcs]
    res = pl.pallas_call(
        body, name=name,
        out_shape=(*hbm, *hbm, *([pltpu.SemaphoreType.DMA(())] * (2 * ncp)), jax.ShapeDtypeStruct((8, LANES), F32)),
        in_specs=[_HBM] * (2 * n),
        out_specs=(*([_HBM] * (2 * n)), *([_SEM] * (2 * ncp)), pl.BlockSpec(memory_space=pltpu.VMEM)),
        input_output_aliases={i: i for i in range(2 * n)},
        compiler_params=pltpu.CompilerParams(has_side_effects=_DATAFLOW),
    )(*[pltpu.with_memory_space_constraint(a, pltpu.HBM) for a in srcs], *lands)
    return list(res[2 * n:2 * n + 2 * ncp]), list(res[:n]), list(res[n:2 * n]), res[-1]


def _exchange_chips_wait(sems, srcs, lands, after, name):
    n = len(srcs)
    ncp = 3 * n

    def body(*refs):
        src_refs, land_refs = refs[:n], refs[n:2 * n]
        sem_refs = refs[2 * n:2 * n + 2 * ncp]
        x, y, c, chips = _place()
        for i in range(n):
            for k in range(3):
                j = 3 * i + k
                cp = _chip_copy(src_refs[i], land_refs[i], sem_refs[j], sem_refs[ncp + j], k, chips, c,
                                2 * chips[k][0] + chips[k][1])
                cp.wait_send()
                cp.wait_recv()

    hbm = [pltpu.HBM(a.shape, a.dtype) for a in srcs]
    res = pl.pallas_call(
        body, name=name, out_shape=(*hbm, *hbm),
        in_specs=[_HBM] * (2 * n) + [_SEM] * (2 * ncp) + [_ANY], out_specs=tuple([_HBM] * (2 * n)),
        input_output_aliases={i: i for i in range(2 * n)},
        compiler_params=pltpu.CompilerParams(has_side_effects=_DATAFLOW),
    )(*srcs, *lands, *sems, after)
    return list(res[n:2 * n])


def _peer_copy(src_ref, land_ref, send_sem, recv_sem, k, place, land):
    x, y, c = place
    peer = (1 - x if k & 4 else x, 1 - y if k & 2 else y, 1 - c if k & 1 else c)
    return pltpu.make_async_remote_copy(
        src_ref=src_ref, dst_ref=land_ref.at[land], send_sem=send_sem, recv_sem=recv_sem,
        device_id=peer, device_id_type=_MESH)


def _gather_start(x_shard, after, name):
    npeer = N_DEV - 1

    def body(x_ref, land_ref, after_ref, x_thru, land_thru, *rest):
        sems, token = rest[:2 * npeer], rest[-1]
        x, y, c, _ = _place()
        for k in range(1, N_DEV):
            _peer_copy(x_ref, land_ref, sems[k - 1], sems[npeer + k - 1], k, (x, y, c), 4 * x + 2 * y + c).start()
        token[...] = jnp.zeros_like(token)

    land = pltpu.with_memory_space_constraint(lax.empty((N_DEV,) + tuple(x_shard.shape), x_shard.dtype), pltpu.HBM)
    res = pl.pallas_call(
        body, name=name,
        out_shape=(pltpu.HBM(x_shard.shape, x_shard.dtype), pltpu.HBM(land.shape, land.dtype),
                   *([pltpu.SemaphoreType.DMA(())] * (2 * npeer)), jax.ShapeDtypeStruct((8, LANES), F32)),
        in_specs=[_HBM, _HBM, _ANY],
        out_specs=(_HBM, _HBM, *([_SEM] * (2 * npeer)), pl.BlockSpec(memory_space=pltpu.VMEM)),
        input_output_aliases={0: 0, 1: 1},
        compiler_params=pltpu.CompilerParams(has_side_effects=_DATAFLOW),
    )(pltpu.with_memory_space_constraint(x_shard, pltpu.HBM), land, after)
    return list(res[2:2 + 2 * npeer]), res[0], res[1], res[-1]


def _gather_wait(sems, src, land, after, name):
    npeer = N_DEV - 1

    def body(x_ref, land_ref, *rest):
        sem_refs = rest[:2 * npeer]
        x, y, c, _ = _place()
        for k in range(1, N_DEV):
            peer_index = (4 * x + 2 * y + c) ^ k
            cp = _peer_copy(x_ref, land_ref, sem_refs[k - 1], sem_refs[npeer + k - 1], k, (x, y, c), peer_index)
            cp.wait_send()
            cp.wait_recv()

    res = pl.pallas_call(
        body, name=name, out_shape=(pltpu.HBM(src.shape, src.dtype), pltpu.HBM(land.shape, land.dtype)),
        in_specs=[_HBM, _HBM] + [_SEM] * (2 * npeer) + [_ANY], out_specs=(_HBM, _HBM),
        input_output_aliases={0: 0, 1: 1},
        compiler_params=pltpu.CompilerParams(has_side_effects=_DATAFLOW),
    )(src, land, *sems, after)
    return res[1]


def _prenorm(x2, w):
    t = x2.shape[0]
    tm = min(512, t)

    def body(x_ref, w_ref, h_ref):
        x = x_ref[...]
        r = lax.rsqrt(jnp.mean(x * x, axis=-1, keepdims=True) + NORM_EPS)
        h_ref[...] = (x * r * w_ref[...]).astype(BF16)

    return pl.pallas_call(
        body, name="prenorm", grid=(t // tm,),
        in_specs=[pl.BlockSpec((tm, D_MODEL), lambda i: (i, 0)), pl.BlockSpec((1, D_MODEL), lambda i: (0, 0))],
        out_specs=pl.BlockSpec((tm, D_MODEL), lambda i: (i, 0)),
        out_shape=jax.ShapeDtypeStruct((t, D_MODEL), BF16),
        compiler_params=_cparams(("parallel",)),
    )(x2, w)


def _mm_bias(a, bt, bias, out_dtype, name):
    m, k = a.shape
    n = bt.shape[0]
    tm = min(512, m)
    tn = min(1024, n)

    def body(a_ref, bt_ref, bias_ref, o_ref):
        aa = a_ref[...]
        for j in range(n // tn):
            cols = slice(j * tn, (j + 1) * tn)
            o_ref[:, cols] = (_dot_nt(aa, bt_ref[cols, :]) + bias_ref[:, cols]).astype(o_ref.dtype)

    return pl.pallas_call(
        body, name=name, grid=(m // tm,),
        in_specs=[pl.BlockSpec((tm, k), lambda i: (i, 0)), pl.BlockSpec((n, k), lambda i: (0, 0)),
                  pl.BlockSpec((1, n), lambda i: (0, 0))],
        out_specs=pl.BlockSpec((tm, n), lambda i: (i, 0)),
        out_shape=jax.ShapeDtypeStruct((m, n), out_dtype),
        compiler_params=_cparams(("parallel",), vmem_mb=48),
    )(a, bt, bias)


def _mm_tn(a, b, name):
    t, m = a.shape
    n = b.shape[1]
    tm = min(1024, m)
    tk = min(2048, t)

    def body(a_ref, b_ref, o_ref, s_ref):
        kk = pl.program_id(1)

        @pl.when(kk == 0)
        def _():
            o_ref[...] = jnp.zeros_like(o_ref)
            s_ref[...] = jnp.zeros_like(s_ref)

        aa = a_ref[...]
        o_ref[...] += _dot_tn(aa, b_ref[...])
        s_ref[0:1, :] += jnp.sum(aa.astype(F32), axis=0, keepdims=True)

    return pl.pallas_call(
        body, name=name, grid=(m // tm, t // tk),
        in_specs=[pl.BlockSpec((tk, tm), lambda i, kk: (kk, i)), pl.BlockSpec((tk, n), lambda i, kk: (kk, 0))],
        out_specs=[pl.BlockSpec((tm, n), lambda i, kk: (i, 0)), pl.BlockSpec((8, tm), lambda i, kk: (0, i))],
        out_shape=[jax.ShapeDtypeStruct((m, n), F32), jax.ShapeDtypeStruct((8, m), F32)],
        compiler_params=_cparams(("parallel", "arbitrary"), vmem_mb=48),
    )(a, b)


def _fgate_fwd(zf3):
    b, s, _ = zf3.shape
    tb = SCAN_TILE
    nb = s // tb

    def body(z_ref, cexp_ref, crow_ref):
        tri = (_iota((tb, tb), 1) <= _iota((tb, tb), 0)).astype(BF16)
        expand = ((_iota((LANES, D_MODEL), 1) >> 6) == _iota((LANES, D_MODEL), 0)).astype(BF16)
        carry = jnp.zeros((1, LANES), F32)
        for i in range(nb):
            rows = slice(i * tb, (i + 1) * tb)
            z = z_ref[rows, :]
            lf = jnp.minimum(z, 0.0) - jnp.log1p(jnp.exp(-jnp.abs(z)))
            cb = sum(_dot(tri, part) for part in _split3(lf)) + carry
            carry = cb[tb - 1:tb, :]
            cexp_ref[rows, :] = sum(_dot(part, expand) for part in _split3(cb))
            crow_ref[:, rows] = cb.T[0:HEADS, :]

    return pl.pallas_call(
        body, name="fgate_fwd", grid=(b,),
        in_specs=[pl.BlockSpec((None, s, LANES), lambda i: (i, 0, 0))],
        out_specs=[pl.BlockSpec((None, s, D_MODEL), lambda i: (i, 0, 0)),
                   pl.BlockSpec((None, HEADS, s), lambda i: (i, 0, 0))],
        out_shape=[jax.ShapeDtypeStruct((b, s, D_MODEL), F32), jax.ShapeDtypeStruct((b, HEADS, s), F32)],
        compiler_params=_cparams(("parallel",)),
    )(zf3)


def _fgate_bwd(dc3, zf3):
    b, s, _ = zf3.shape
    tb = SCAN_TILE
    nb = s // tb

    def body(dc_ref, z_ref, o_ref):
        tri = (_iota((tb, tb), 1) >= _iota((tb, tb), 0)).astype(BF16)
        carry = jnp.zeros((1, LANES), F32)
        for i in reversed(range(nb)):
            rows = slice(i * tb, (i + 1) * tb)
            dlf = sum(_dot(tri, part) for part in _split3(dc_ref[rows, :])) + carry
            carry = dlf[0:1, :]
            o_ref[rows, :] = (dlf * _sigmoid(-z_ref[rows, :])).astype(BF16)

    return pl.pallas_call(
        body, name="fgate_bwd", grid=(b,),
        in_specs=[pl.BlockSpec((None, s, LANES), lambda i: (i, 0, 0)),
                  pl.BlockSpec((None, s, LANES), lambda i: (i, 0, 0))],
        out_specs=pl.BlockSpec((s, LANES), lambda i: (i, 0)),
        out_shape=jax.ShapeDtypeStruct((b * s, LANES), BF16),
        compiler_params=_cparams(("parallel",)),
    )(dc3, zf3)


def _spare(hh):
    return HEAD_DIM if hh == 0 else 0


def _put_cols(tile, mine, cols, first):
    lane = _iota((1, LANES), 1)
    out = jnp.where(mine, tile, jnp.zeros((), tile.dtype))
    for j, c in enumerate(cols):
        out = jnp.where(lane == first + j, c, out)
    return out


def _put_rows(tile, mine, rows, first):
    sub = _iota((LANES, 1), 0)
    out = jnp.where(mine, tile, jnp.zeros((), tile.dtype))
    for j, r in enumerate(rows):
        out = jnp.where(sub == first + j, r, out)
    return out


def _transpose_bf16(a):
    return a.astype(F32).T.astype(BF16)


def _attn_fwd(qkv3, cexp3, crow, zrest3):
    b, s, _ = qkv3.shape
    ta = ATT_TILE_FWD
    nq = s // ta
    hd = HEAD_DIM
    crow5 = crow.reshape(b, HEAD_PAIRS, 2, nq, ta)

    def body(qkv_ref, cq_ref, ck_ref, g_ref, y_ref, lse_ref, ga_ref, kt_scr, v_scr):
        lane = _iota((1, LANES), 1)
        sub = _iota((LANES, 1), 0)
        lane_mine = (lane < hd, lane >= hd)
        sub_mine = (sub < hd, sub >= hd)
        causal = _iota((ta, ta), 0) >= _iota((ta, ta), 1)
        one = jnp.ones((), BF16)

        for kj in range(nq):
            rows = slice(kj * ta, (kj + 1) * ta)
            kt = _transpose_bf16(qkv_ref[rows, LANES:2 * LANES])
            v = qkv_ref[rows, 2 * LANES:3 * LANES]
            for hh in range(2):
                ck = list(_split3(-ck_ref[hh, kj:kj + 1, :]))
                kt_scr[hh, kj] = _put_rows(kt, sub_mine[hh], [one, one, one] + ck, _spare(hh))
                v_scr[hh, kj] = _put_cols(v, lane_mine[hh], [one], _spare(hh))

        for qi in range(nq):
            rows = slice(qi * ta, (qi + 1) * ta)
            q = qkv_ref[rows, 0:LANES] * 0.125
            cq = cq_ref[rows, :]
            qh = [_put_cols(q, lane_mine[hh], list(_split3(cq[:, hh * hd:hh * hd + 1])) + [one, one, one], _spare(hh))
                  for hh in range(2)]
            st = [(jnp.full((ta, 1), MASK_VALUE, F32), jnp.zeros((ta, LANES), F32))] * 2
            for kj in range(qi + 1):
                for hh in range(2):
                    m, acc = st[hh]
                    sc = _dot(qh[hh], kt_scr[hh, kj])
                    if kj == qi:
                        sc = jnp.where(causal, sc, MASK_VALUE)
                    mn = jnp.maximum(m, jnp.max(sc, axis=-1, keepdims=True))
                    p = jnp.exp(sc - mn).astype(BF16)
                    st[hh] = (mn, jnp.exp(m - mn) * acc + _dot(p, v_scr[hh, kj]))
            (ma, acca), (mb, accb) = st
            la = acca[:, hd:hd + 1]
            lb = accb[:, 0:1]
            y = jnp.where(lane_mine[0], acca * (1.0 / la), accb * (1.0 / lb))
            lse = jnp.where(lane_mine[0], ma + jnp.log(la), mb + jnp.log(lb)).T
            lse_ref[0, qi:qi + 1, :] = lse[0:1, :]
            lse_ref[1, qi:qi + 1, :] = lse[hd:hd + 1, :]
            y_ref[rows, :] = y
            g = g_ref[rows, :].astype(F32)
            ga_ref[rows, :] = (y * (g * _sigmoid(g))).astype(BF16)

    blk = lambda w: pl.BlockSpec((None, s, w), lambda i, p: (i, 0, p))
    rows5 = pl.BlockSpec((None, None, 2, nq, ta), lambda i, p: (i, p, 0, 0, 0))
    yatt3, lse5, ga3 = pl.pallas_call(
        body, name="attn_fwd", grid=(b, HEAD_PAIRS),
        in_specs=[blk(3 * LANES), blk(LANES), rows5, blk(LANES)],
        out_specs=[blk(LANES), rows5, blk(LANES)],
        out_shape=[jax.ShapeDtypeStruct((b, s, D_MODEL), F32),
                   jax.ShapeDtypeStruct((b, HEAD_PAIRS, 2, nq, ta), F32),
                   jax.ShapeDtypeStruct((b, s, D_MODEL), BF16)],
        scratch_shapes=[pltpu.VMEM((2, nq, LANES, ta), BF16), pltpu.VMEM((2, nq, ta, LANES), BF16)],
        compiler_params=_cparams(("parallel", "parallel")),
    )(qkv3, cexp3, crow5, zrest3)
    return yatt3, lse5.reshape(b, HEADS, s), ga3


def _attn_bwd(qkv3, do3, y3, lse, crow, cexp3):
    b, s, _ = qkv3.shape
    ta = ATT_TILE_BWD
    nq = s // ta
    hd = HEAD_DIM
    lse5 = lse.reshape(b, HEAD_PAIRS, 2, nq, ta)
    crow5 = crow.reshape(b, HEAD_PAIRS, 2, nq, ta)

    def body(qkv_ref, do_ref, y_ref, lse_ref, crow_ref, cexp_ref, dqkv_ref, dc_ref,
             qa_scr, doa_scr, qst_scr, dot_scr, kt_scr, vt_scr, dq_scr, rs_scr):
        pair = pl.program_id(1)
        lane = _iota((1, LANES), 1)
        sub = _iota((LANES, 1), 0)
        lane_mine = (lane < hd, lane >= hd)
        sub_mine = (sub < hd, sub >= hd)
        causal = _iota((ta, ta), 0) >= _iota((ta, ta), 1)
        one = jnp.ones((), BF16)
        zero = jnp.zeros((), BF16)

        @pl.when(pair == 0)
        def _():
            dc_ref[...] = jnp.zeros_like(dc_ref)

        for i in range(nq):
            rows = slice(i * ta, (i + 1) * ta)
            qs = qkv_ref[rows, 0:LANES] * 0.125
            qst = _transpose_bf16(qs)
            kt = _transpose_bf16(qkv_ref[rows, LANES:2 * LANES])
            vt = _transpose_bf16(qkv_ref[rows, 2 * LANES:3 * LANES])
            do = do_ref[rows, :]
            dof = do.astype(F32)
            dot = dof.T.astype(BF16)
            pr = y_ref[rows, :] * dof
            cq = cexp_ref[rows, :]
            lse_c = jnp.where(sub == 0, lse_ref[0, i:i + 1, :],
                              jnp.where(sub == 1, lse_ref[1, i:i + 1, :], 0.0)).T
            for hh in range(2):
                sp = _spare(hh)
                dsum = jnp.sum(jnp.where(lane_mine[hh], pr, 0.0), axis=-1, keepdims=True)
                bias = cq[:, hh * hd:hh * hd + 1] - lse_c[:, hh:hh + 1]
                qa_scr[hh, i] = _put_cols(qs, lane_mine[hh], list(_split3(bias)) + [one, one, one], sp)
                doa_scr[hh, i] = _put_cols(do, lane_mine[hh], list(_split3(-dsum)), sp)
                qst_scr[hh, i] = jnp.where(sub_mine[hh], qst, zero)
                dot_scr[hh, i] = jnp.where(sub_mine[hh], dot, zero)
                ck = list(_split3(-crow_ref[hh, i:i + 1, :]))
                kt_scr[hh, i] = _put_rows(kt, sub_mine[hh], [one, one, one] + ck, sp)
                vt_scr[hh, i] = _put_rows(vt, sub_mine[hh], [one, one, one], sp)
            dq_scr[i] = jnp.zeros((ta, LANES), F32)
            rs_scr[i] = jnp.zeros((ta, LANES), F32)

        for kj in range(nq):
            krows = slice(kj * ta, (kj + 1) * ta)
            k = qkv_ref[krows, LANES:2 * LANES]
            km = (jnp.where(lane_mine[0], k, zero), jnp.where(lane_mine[1], k, zero))
            dkt = jnp.zeros((LANES, ta), F32)
            dvt = jnp.zeros((LANES, ta), F32)
            dcp = [jnp.zeros((8, ta), F32), jnp.zeros((8, ta), F32)]
            for qi in range(kj, nq):
                dq = jnp.zeros((ta, LANES), F32)
                rs = []
                for hh in range(2):
                    sc = _dot(qa_scr[hh, qi], kt_scr[hh, kj])
                    if qi == kj:
                        sc = jnp.where(causal, sc, MASK_VALUE)
                    p = jnp.exp(sc)
                    dsf = p * _dot(doa_scr[hh, qi], vt_scr[hh, kj])
                    dcp[hh] = dcp[hh] + jnp.sum(dsf.reshape(ta // 8, 8, ta), axis=0)
                    rs.append(jnp.sum(dsf, axis=-1, keepdims=True))
                    ds = dsf.astype(BF16)
                    dq = dq + _dot(ds, km[hh])
                    dkt = dkt + _dot(qst_scr[hh, qi], ds)
                    dvt = dvt + _dot(dot_scr[hh, qi], p.astype(BF16))
                dq_scr[qi] += dq
                rs_scr[qi] += jnp.where(lane == 0, rs[0], jnp.where(lane == 1, rs[1], 0.0))
            dqkv_ref[krows, LANES:2 * LANES] = dkt.T.astype(BF16)
            dqkv_ref[krows, 2 * LANES:3 * LANES] = dvt.T.astype(BF16)
            dca = jnp.sum(dcp[0], axis=0, keepdims=True)
            dcb = jnp.sum(dcp[1], axis=0, keepdims=True)
            dcs = jnp.where(sub == 0, dca, jnp.where(sub == 1, dcb, 0.0)).T
            dc_ref[krows, :] += (jnp.where(lane == 2 * pair, -dcs[:, 0:1], 0.0)
                                 + jnp.where(lane == 2 * pair + 1, -dcs[:, 1:2], 0.0))
        for qi in range(nq):
            rows = slice(qi * ta, (qi + 1) * ta)
            dqkv_ref[rows, 0:LANES] = (dq_scr[qi] * 0.125).astype(BF16)
            rq = rs_scr[qi]
            dc_ref[rows, :] += (jnp.where(lane == 2 * pair, rq[:, 0:1], 0.0)
                                + jnp.where(lane == 2 * pair + 1, rq[:, 1:2], 0.0))

    blk = lambda w: pl.BlockSpec((None, s, w), lambda i, p: (i, 0, p))
    rows5 = pl.BlockSpec((None, None, 2, nq, ta), lambda i, p: (i, p, 0, 0, 0))
    by_rows = lambda: pltpu.VMEM((2, nq, ta, LANES), BF16)
    by_cols = lambda: pltpu.VMEM((2, nq, LANES, ta), BF16)
    return pl.pallas_call(
        body, name="attn_bwd", grid=(b, HEAD_PAIRS),
        in_specs=[blk(3 * LANES), blk(LANES), blk(LANES), rows5, rows5, blk(LANES)],
        out_specs=[blk(3 * LANES), pl.BlockSpec((None, s, LANES), lambda i, p: (i, 0, 0))],
        out_shape=[jax.ShapeDtypeStruct((b, s, 3 * D_MODEL), BF16), jax.ShapeDtypeStruct((b, s, LANES), F32)],
        scratch_shapes=[by_rows(), by_rows(), by_cols(), by_cols(), by_cols(), by_cols(),
                        pltpu.VMEM((nq, ta, LANES), F32), pltpu.VMEM((nq, ta, LANES), F32)],
        compiler_params=_cparams(("parallel", "arbitrary")),
    )(qkv3, do3, y3, lse5, crow5, cexp3)


def _shifted(v, ks, rows, s):
    low = rows[0:8, :]
    out = []
    for k in ks:
        r = pltpu.roll(v, k % s, 0)
        if k > 0:
            out.append(jnp.concatenate([jnp.where(low >= k, r[0:8, :], 0.0), r[8:, :]], axis=0))
        else:
            out.append(jnp.concatenate([r[:s - 8, :], jnp.where(low < 8 + k, r[s - 8:, :], 0.0)], axis=0))
    return out


def _rnn_common(xr, cw_ref, cb_ref, bda_ref, bdx_ref, ba_ref, bx_ref, lam_ref, s):
    rows = _iota((s, LANES), 0)
    x1, x2, x3 = _shifted(xr, (1, 2, 3), rows, s)
    xc = cb_ref[...] + cw_ref[0:1, :] * x3
    xc = xc + cw_ref[1:2, :] * x2
    xc = xc + cw_ref[2:3, :] * x1
    xc = xc + cw_ref[3:4, :] * xr
    xcb = xc.astype(BF16)
    r = _sigmoid(_dot(xcb, bda_ref[...]) + ba_ref[...])
    i = _sigmoid(_dot(xcb, bdx_ref[...]) + bx_ref[...])
    sp = _softplus(-lam_ref[...])
    log_a = (-RG_C * r) * sp
    a = jnp.exp(log_a)
    a2 = a * a
    sq = jnp.sqrt(jnp.maximum(_one_minus_exp(log_a + log_a, a2), 0.0))
    return rows, (x1, x2, x3), xc, xcb, r, i, sp, a, a2, sq


def _scan_down(a, u, rows, s, s1, s2):
    low = rows & 7
    for sh in (1, 2, 4):
        keep = low >= sh
        u = u + a * jnp.where(keep, pltpu.roll(u, sh, 0), 0.0)
        a = a * jnp.where(keep, pltpu.roll(a, sh, 0), 1.0)
    ng = s // 8
    s1[...] = a
    s2[...] = u
    at = s1[pl.ds(7, ng, stride=8), :]
    ut = s2[pl.ds(7, ng, stride=8), :]
    grow = _iota((ng, LANES), 0)
    sh = 1
    while sh < ng:
        keep = grow >= sh
        ut = ut + at * jnp.where(keep, pltpu.roll(ut, sh, 0), 0.0)
        if sh * 2 < ng:
            at = at * jnp.where(keep, pltpu.roll(at, sh, 0), 1.0)
        sh *= 2
    h_in = jnp.where(grow >= 1, pltpu.roll(ut, 1, 0), 0.0)
    for k in range(8):
        s1[pl.ds(k, ng, stride=8), :] = h_in
    return u + a * s1[...]


def _scan_up(a, g, rows, s, s1, s2):
    low = rows & 7
    for sh in (1, 2, 4):
        keep = low < 8 - sh
        g = g + a * jnp.where(keep, pltpu.roll(g, s - sh, 0), 0.0)
        a = a * jnp.where(keep, pltpu.roll(a, s - sh, 0), 1.0)
    ng = s // 8
    s1[...] = a
    s2[...] = g
    at = s1[pl.ds(0, ng, stride=8), :]
    gt = s2[pl.ds(0, ng, stride=8), :]
    grow = _iota((ng, LANES), 0)
    sh = 1
    while sh < ng:
        keep = grow < ng - sh
        gt = gt + at * jnp.where(keep, pltpu.roll(gt, ng - sh, 0), 0.0)
        if sh * 2 < ng:
            at = at * jnp.where(keep, pltpu.roll(at, ng - sh, 0), 1.0)
        sh *= 2
    g_in = jnp.where(grow < ng - 1, pltpu.roll(gt, ng - 1, 0), 0.0)
    for k in range(8):
        s1[pl.ds(k, ng, stride=8), :] = g_in
    return g + a * s1[...]


def _rnn_specs(s):
    blk = lambda off: pl.BlockSpec((None, s, LANES), lambda cb, i: (i, 0, off + cb))
    vec = lambda r: pl.BlockSpec((r, LANES), lambda cb, i: (0, cb))
    mat = pl.BlockSpec((None, LANES, LANES), lambda cb, i: (cb, 0, 0))
    return blk, vec, mat


def _rnn_fwd(zrest3, conv_w, conv_b, bda, bdx, ba, bx, lam):
    b, s, _ = zrest3.shape

    def body(xr_ref, g_ref, cw_ref, cb_ref, bda_ref, bdx_ref, ba_ref, bx_ref, lam_ref, h_ref, gr_ref, s1, s2):
        xr = xr_ref[...].astype(F32)
        rows, _, xc, _, _, i, _, a, _, sq = _rnn_common(
            xr, cw_ref, cb_ref, bda_ref, bdx_ref, ba_ref, bx_ref, lam_ref, s)
        h = _scan_down(a, sq * (i * xc), rows, s, s1, s2)
        h_ref[...] = h
        g = g_ref[...].astype(F32)
        gr_ref[...] = (h * (g * _sigmoid(g))).astype(BF16)

    blk, vec, mat = _rnn_specs(s)
    return pl.pallas_call(
        body, name="rnn_fwd", grid=(N_CBLK, b),
        in_specs=[blk(N_CBLK), blk(2 * N_CBLK), vec(CONV_W), vec(1), mat, mat, vec(1), vec(1), vec(1)],
        out_specs=[blk(0), blk(0)],
        out_shape=[jax.ShapeDtypeStruct((b, s, D_MODEL), F32), jax.ShapeDtypeStruct((b, s, D_MODEL), BF16)],
        scratch_shapes=[pltpu.VMEM((s, LANES), F32), pltpu.VMEM((s, LANES), F32)],
        compiler_params=_cparams(("parallel", "parallel")),
    )(zrest3, zrest3, conv_w, conv_b, bda, bdx, ba, bx, lam)


def _rnn_bwd(zrest3, h3, dh3, conv_w, conv_b, bda, bdx, ba, bx, lam):
    b, s, _ = zrest3.shape

    def body(xr_ref, h_ref, dh_ref, cw_ref, cb_ref, bda_ref, bdx_ref, ba_ref, bx_ref, lam_ref,
             dxr_ref, pv_ref, dbd_ref, s1, s2):
        @pl.when(pl.program_id(1) == 0)
        def _():
            pv_ref[...] = jnp.zeros_like(pv_ref)
            dbd_ref[...] = jnp.zeros_like(dbd_ref)

        xr = xr_ref[...].astype(F32)
        rows, (x1, x2, x3), xc, xcb, r, i, sp, a, a2, sq = _rnn_common(
            xr, cw_ref, cb_ref, bda_ref, bdx_ref, ba_ref, bx_ref, lam_ref, s)
        (a_next,) = _shifted(a, (-1,), rows, s)
        g = _scan_up(a_next, dh_ref[...], rows, s, s1, s2)
        (hp,) = _shifted(h_ref[...], (1,), rows, s)
        da = g * hp
        dsq = g * (i * xc)
        di = g * (sq * xc)
        dxc = g * (sq * i)
        dlog = da * a - dsq * (a2 / sq)
        dr = dlog * (-RG_C * sp)
        dpr = dr * (r * (1.0 - r))
        dpi = di * (i * (1.0 - i))
        dprb = dpr.astype(BF16)
        dpib = dpi.astype(BF16)
        dxc = dxc + _dot_nt(dprb, bda_ref[...]) + _dot_nt(dpib, bdx_ref[...])

        up1, up2, up3 = _shifted(dxc, (-1, -2, -3), rows, s)
        dxr = cw_ref[3:4, :] * dxc + cw_ref[2:3, :] * up1 + cw_ref[1:2, :] * up2 + cw_ref[0:1, :] * up3
        dxr_ref[...] = dxr.astype(BF16)

        def colsum(v):
            return jnp.sum(v, axis=0, keepdims=True)

        pv_ref[0:1, :] += colsum(dxc * x3)
        pv_ref[1:2, :] += colsum(dxc * x2)
        pv_ref[2:3, :] += colsum(dxc * x1)
        pv_ref[3:4, :] += colsum(dxc * xr)
        pv_ref[4:5, :] += colsum(dxc)
        pv_ref[5:6, :] += colsum(dpr)
        pv_ref[6:7, :] += colsum(dpi)
        pv_ref[7:8, :] += colsum(dlog * r) * (RG_C * _sigmoid(-lam_ref[...]))
        dbd_ref[0] += _dot_tn(xcb, dprb)
        dbd_ref[1] += _dot_tn(xcb, dpib)

    blk, vec, mat = _rnn_specs(s)
    hblk = pl.BlockSpec((None, s, LANES), lambda cb, i: (i, 0, cb))
    return pl.pallas_call(
        body, name="rnn_bwd", grid=(N_CBLK, b),
        in_specs=[blk(N_CBLK), hblk, hblk, vec(CONV_W), vec(1), mat, mat, vec(1), vec(1), vec(1)],
        out_specs=[hblk, pl.BlockSpec((8, LANES), lambda cb, i: (0, cb)),
                   pl.BlockSpec((None, 2, LANES, LANES), lambda cb, i: (cb, 0, 0, 0))],
        out_shape=[jax.ShapeDtypeStruct((b, s, D_MODEL), BF16), jax.ShapeDtypeStruct((8, D_MODEL), F32),
                   jax.ShapeDtypeStruct((N_CBLK, 2, LANES, LANES), F32)],
        scratch_shapes=[pltpu.VMEM((s, LANES), F32), pltpu.VMEM((s, LANES), F32)],
        compiler_params=_cparams(("parallel", "arbitrary")),
    )(zrest3, h3, dh3, conv_w, conv_b, bda, bdx, ba, bx, lam)


def _branch_merge(ga, gr, wa, wr, zrest):
    t = ga.shape[0]
    tm = min(512, t)
    tn = D_MODEL

    def body(ga_ref, gr_ref, wa_ref, wr_ref, mga_ref, mgr_ref, ya_ref, yr_ref, m_ref):
        ya = _dot(ga_ref[...], wa_ref[...])
        yr = _dot(gr_ref[...], wr_ref[...])
        ya_ref[...] = ya.astype(BF16)
        yr_ref[...] = yr.astype(BF16)
        m_ref[...] = (_sigmoid(mga_ref[...].astype(F32)) * ya + _sigmoid(mgr_ref[...].astype(F32)) * yr).astype(BF16)

    nj = D_MODEL // tn
    act = pl.BlockSpec((tm, D_MODEL), lambda i, j: (i, 0))
    wgt = pl.BlockSpec((D_MODEL, tn), lambda i, j: (0, j))
    out = pl.BlockSpec((tm, tn), lambda i, j: (i, j))
    return pl.pallas_call(
        body, name="branch_merge", grid=(t // tm, nj),
        in_specs=[act, act, wgt, wgt, pl.BlockSpec((tm, tn), lambda i, j: (i, 3 * nj + j)),
                  pl.BlockSpec((tm, tn), lambda i, j: (i, 4 * nj + j))],
        out_specs=[out, out, out],
        out_shape=[jax.ShapeDtypeStruct((t, D_MODEL), BF16), jax.ShapeDtypeStruct((t, D_MODEL), BF16),
                   jax.ShapeDtypeStruct((t, D_MODEL), BF16)],
        compiler_params=_cparams(("parallel", "parallel")),
    )(ga, gr, wa, wr, zrest, zrest)


def _out_loss(m, wout, x2, tgt2, wpost):
    t = m.shape[0]
    tm = min(512, t)

    def body(m_ref, w_ref, x_ref, t_ref, wp_ref, dy_ref, do_ref, acc_ref):
        @pl.when(pl.program_id(0) == 0)
        def _():
            acc_ref[...] = jnp.zeros_like(acc_ref)

        o = _dot(m_ref[...], w_ref[...])
        r2 = lax.rsqrt(jnp.mean(o * o, axis=-1, keepdims=True) + NORM_EPS)
        n = o * r2
        wp = wp_ref[...]
        err = (x_ref[...] + n * wp) - t_ref[...]
        dy = err * (1.0 / D_MODEL)
        dn = dy * wp
        do = r2 * (dn - n * jnp.mean(dn * n, axis=-1, keepdims=True))
        dy_ref[...] = dy
        do_ref[...] = do.astype(BF16)
        acc_ref[0:1, :] += jnp.sum(dy * n, axis=0, keepdims=True)
        acc_ref[1:2, :] += jnp.sum(err * err, axis=0, keepdims=True)

    row = pl.BlockSpec((tm, D_MODEL), lambda i: (i, 0))
    return pl.pallas_call(
        body, name="out_loss", grid=(t // tm,),
        in_specs=[row, pl.BlockSpec((D_MODEL, D_MODEL), lambda i: (0, 0)), row, row,
                  pl.BlockSpec((1, D_MODEL), lambda i: (0, 0))],
        out_specs=[row, row, pl.BlockSpec((8, D_MODEL), lambda i: (0, 0))],
        out_shape=[jax.ShapeDtypeStruct((t, D_MODEL), F32), jax.ShapeDtypeStruct((t, D_MODEL), BF16),
                   jax.ShapeDtypeStruct((8, D_MODEL), F32)],
        compiler_params=_cparams(("arbitrary",)),
    )(m, wout, x2, tgt2, wpost)


def _merge_bwd(do, wout, zrest, ya, yr):
    t = do.shape[0]
    tm = min(512, t)
    tn = D_MODEL
    nj = D_MODEL // tn

    def body(do_ref, w_ref, mga_ref, mgr_ref, ya_ref, yr_ref, dya_ref, dyr_ref, dmga_ref, dmgr_ref):
        dm = _dot_nt(do_ref[...], w_ref[...])
        sa = _sigmoid(mga_ref[...].astype(F32))
        sr = _sigmoid(mgr_ref[...].astype(F32))
        dya_ref[...] = (dm * sa).astype(BF16)
        dyr_ref[...] = (dm * sr).astype(BF16)
        dmga_ref[...] = (dm * ya_ref[...].astype(F32) * (sa * (1.0 - sa))).astype(BF16)
        dmgr_ref[...] = (dm * yr_ref[...].astype(F32) * (sr * (1.0 - sr))).astype(BF16)

    out = pl.BlockSpec((tm, tn), lambda i, j: (i, j))
    bf = jax.ShapeDtypeStruct((t, D_MODEL), BF16)
    return pl.pallas_call(
        body, name="merge_bwd", grid=(t // tm, nj),
        in_specs=[pl.BlockSpec((tm, D_MODEL), lambda i, j: (i, 0)), pl.BlockSpec((tn, D_MODEL), lambda i, j: (j, 0)),
                  pl.BlockSpec((tm, tn), lambda i, j: (i, 3 * nj + j)),
                  pl.BlockSpec((tm, tn), lambda i, j: (i, 4 * nj + j)), out, out],
        out_specs=[out, out, out, out],
        out_shape=[bf, bf, bf, bf],
        compiler_params=_cparams(("parallel", "parallel")),
    )(do, wout, zrest, zrest, ya, yr)


def _branch_bwd(dya, dyr, wa, wr, zrest, yatt, ylru):
    t = dya.shape[0]
    tm = min(512, t)
    tn = D_MODEL
    nj = D_MODEL // tn

    def body(dya_ref, dyr_ref, wa_ref, wr_ref, ga_ref, gr_ref, ya_ref, yl_ref,
             dyatt_ref, dga_ref, dyl_ref, dgr_ref):
        dga = _dot_nt(dya_ref[...], wa_ref[...])
        dgr = _dot_nt(dyr_ref[...], wr_ref[...])
        g = ga_ref[...].astype(F32)
        sg = _sigmoid(g)
        dyatt_ref[...] = (dga * (g * sg)).astype(BF16)
        dga_ref[...] = (dga * ya_ref[...] * (sg * (1.0 + g * (1.0 - sg)))).astype(BF16)
        g = gr_ref[...].astype(F32)
        sg = _sigmoid(g)
        dyl_ref[...] = dgr * (g * sg)
        dgr_ref[...] = (dgr * yl_ref[...] * (sg * (1.0 + g * (1.0 - sg)))).astype(BF16)

    act = pl.BlockSpec((tm, D_MODEL), lambda i, j: (i, 0))
    wgt = pl.BlockSpec((tn, D_MODEL), lambda i, j: (j, 0))
    out = pl.BlockSpec((tm, tn), lambda i, j: (i, j))
    bf = jax.ShapeDtypeStruct((t, D_MODEL), BF16)
    return pl.pallas_call(
        body, name="branch_bwd", grid=(t // tm, nj),
        in_specs=[act, act, wgt, wgt, pl.BlockSpec((tm, tn), lambda i, j: (i, j)),
                  pl.BlockSpec((tm, tn), lambda i, j: (i, 2 * nj + j)), out, out],
        out_specs=[out, out, out, out],
        out_shape=[bf, bf, jax.ShapeDtypeStruct((t, D_MODEL), F32), bf],
        compiler_params=_cparams(("parallel", "parallel")),
    )(dya, dyr, wa, wr, zrest, zrest, yatt, ylru)


def _dh_partial(parts, after, name):
    t = parts[0][0].shape[0]
    tm = min(256, t)
    np_ = len(parts)

    def body(*refs):
        o_ref = refs[-1]
        acc = _dot(refs[0][...], refs[np_][...])
        for p in range(1, np_):
            acc = acc + _dot(refs[p][...], refs[np_ + p][...])
        o_ref[...] = acc

    in_specs = [pl.BlockSpec((tm, dz.shape[1]), lambda i: (i, 0)) for dz, _ in parts]
    in_specs += [pl.BlockSpec(w.shape, lambda i: (0, 0)) for _, w in parts]
    in_specs += [pl.BlockSpec(after.shape, lambda i: (0, 0))]
    return pl.pallas_call(
        body, name=name, grid=(t // tm,),
        in_specs=in_specs,
        out_specs=pl.BlockSpec((tm, D_MODEL), lambda i: (i, 0)),
        out_shape=jax.ShapeDtypeStruct((t, D_MODEL), F32),
        compiler_params=_cparams(("parallel",), vmem_mb=48),
    )(*[dz for dz, _ in parts], *[w for _, w in parts], after)


def _dh_final(parts, acc_in, x2, dy, wpre):
    t = x2.shape[0]
    tm = min(256, t)
    np_ = len(parts)

    def body(*refs):
        acc_ref, x_ref, dy_ref, w_ref = refs[2 * np_:2 * np_ + 4]
        gx_ref, pw_ref = refs[2 * np_ + 4:]

        @pl.when(pl.program_id(0) == 0)
        def _():
            pw_ref[...] = jnp.zeros_like(pw_ref)

        dh = acc_ref[...]
        for p in range(np_):
            dh = dh + _dot(refs[p][...], refs[np_ + p][...])
        x = x_ref[...]
        r = lax.rsqrt(jnp.mean(x * x, axis=-1, keepdims=True) + NORM_EPS)
        xn = x * r
        dxn = dh * w_ref[...]
        gx_ref[...] = r * (dxn - xn * jnp.mean(dxn * xn, axis=-1, keepdims=True)) + dy_ref[...]
        pw_ref[0:1, :] += jnp.sum(dh * xn, axis=0, keepdims=True)

    row = pl.BlockSpec((tm, D_MODEL), lambda i: (i, 0))
    in_specs = [pl.BlockSpec((tm, dz.shape[1]), lambda i: (i, 0)) for dz, _ in parts]
    in_specs += [pl.BlockSpec(w.shape, lambda i: (0, 0)) for _, w in parts]
    in_specs += [row, row, row, pl.BlockSpec((1, D_MODEL), lambda i: (0, 0))]
    return pl.pallas_call(
        body, name="dh_final", grid=(t // tm,),
        in_specs=in_specs,
        out_specs=[row, pl.BlockSpec((8, D_MODEL), lambda i: (0, 0))],
        out_shape=[jax.ShapeDtypeStruct((t, D_MODEL), F32), jax.ShapeDtypeStruct((8, D_MODEL), F32)],
        compiler_params=_cparams(("arbitrary",), vmem_mb=48),
    )(*[dz for dz, _ in parts], *[w for _, w in parts], acc_in, x2, dy, wpre)


def _adamw(w, g, m, v):
    m = ADAM_B1 * m + (1.0 - ADAM_B1) * g
    v = ADAM_B2 * v + (1.0 - ADAM_B2) * (g * g)
    m_hat = m / (1.0 - ADAM_B1 ** ADAM_STEP)
    v_hat = v / (1.0 - ADAM_B2 ** ADAM_STEP)
    delta = -ADAM_LR * (m_hat / (jnp.sqrt(v_hat) + ADAM_EPS) + ADAM_WD * w)
    return delta, m, v


def _reduce_adamw(own, parts, place, w, m, v, name):
    r, c = w.shape
    blk, nblk, at = _blocks_2d(r, c)

    def body(place_ref, own_ref, p_ref, w_ref, m_ref, v_ref, g_ref, d_ref, nm_ref, nv_ref):
        mine = place_ref[1]
        own_blk = own_ref[...]
        g = jnp.where(mine == 0, own_blk, p_ref[0].astype(F32))
        for j in range(1, N_CHIPS):
            g = g + jnp.where(mine == j, own_blk, p_ref[j].astype(F32))
        d, nm, nv = _adamw(w_ref[...], g, m_ref[...], v_ref[...])
        g_ref[...] = g
        d_ref[...] = d
        nm_ref[...] = nm
        nv_ref[...] = nv

    row = pl.BlockSpec(blk, lambda i, pr: at(i))
    sh = jax.ShapeDtypeStruct((r, c), F32)
    grid_spec = pltpu.PrefetchScalarGridSpec(
        num_scalar_prefetch=1, grid=(nblk,),
        in_specs=[row, pl.BlockSpec((N_CHIPS,) + blk, lambda i, pr: (0,) + at(i)), row, row, row],
        out_specs=[row, row, row, row])
    return pl.pallas_call(
        body, name=name, grid_spec=grid_spec, out_shape=[sh, sh, sh, sh],
        compiler_params=_cparams(("parallel",)),
    )(place, own, parts, w, m, v)


def _reduce_adamw_stacked(own, parts, place, triples, name):
    n = len(triples)
    _, r, c = triples[0][0].shape

    def body(place_ref, own_ref, p_ref, *refs):
        ins, outs = refs[:3 * n], refs[3 * n:]
        mine = place_ref[1]
        for i in range(n):
            rows = slice(i * r, (i + 1) * r)
            own_blk = own_ref[rows, :]
            g = jnp.where(mine == 0, own_blk, p_ref[0, rows, :].astype(F32))
            for j in range(1, N_CHIPS):
                g = g + jnp.where(mine == j, own_blk, p_ref[j, rows, :].astype(F32))
            d, nm, nv = _adamw(ins[3 * i][0], g, ins[3 * i + 1][0], ins[3 * i + 2][0])
            for k, val in enumerate((g, d, nm, nv)):
                outs[4 * i + k][0] = val

    whole = lambda shape: pl.BlockSpec(shape, lambda i, pr: (0,) * len(shape))
    grid_spec = pltpu.PrefetchScalarGridSpec(
        num_scalar_prefetch=1, grid=(1,),
        in_specs=[whole(own.shape), whole(parts.shape)] + [whole((1, r, c))] * (3 * n),
        out_specs=[whole((1, r, c))] * (4 * n))
    res = pl.pallas_call(
        body, name=name, grid_spec=grid_spec,
        out_shape=[jax.ShapeDtypeStruct((1, r, c), F32)] * (4 * n),
        compiler_params=_cparams(("arbitrary",)),
    )(place, own, parts, *[a for t3 in triples for a in t3])
    return [res[4 * i:4 * i + 4] for i in range(n)]


def _interleave_qkv(a):
    lead = a.shape[:-1]
    return a.reshape(lead + (3, HEAD_PAIRS, LANES)).swapaxes(-3, -2).reshape(lead + (3 * D_MODEL,))


def _deinterleave_qkv(a):
    lead = a.shape[:-1]
    return a.reshape(lead + (HEAD_PAIRS, 3, LANES)).swapaxes(-3, -2).reshape(lead + (3 * D_MODEL,))


def _interleave_rows(a):
    return a.reshape(3, HEAD_PAIRS, LANES, a.shape[1]).swapaxes(0, 1).reshape(a.shape)


def _deinterleave_rows(a):
    return a.reshape(HEAD_PAIRS, 3, LANES, a.shape[1]).swapaxes(0, 1).reshape(a.shape)


def _pack_small(pre, conv_b, rg_ba, rg_bx, lam, post, loss_row, b_in, conv_w_full, rg_wa, rg_wx):
    z = jnp.zeros((1, D_MODEL), F32)
    b_used = jnp.concatenate([b_in[:, 0:3 * D_MODEL], b_in[:, 3 * D_MODEL + HEADS:IN_TOTAL]], axis=1)
    b_f = jnp.pad(b_in[:, 3 * D_MODEL:3 * D_MODEL + HEADS], ((0, 0), (0, D_MODEL - HEADS)))
    return jnp.concatenate([
        pre, conv_b, rg_ba, rg_bx, lam, post, loss_row, z,
        b_used.reshape(9, D_MODEL), b_f, conv_w_full, z, z,
        rg_wa.reshape(64, D_MODEL), rg_wx.reshape(64, D_MODEL)], axis=0)


def _unpack_small(p):
    b_used = p[8:17].reshape(1, 9 * D_MODEL)
    b_in = jnp.concatenate([b_used[:, 0:3 * D_MODEL], p[17:18, 0:HEADS], b_used[:, 3 * D_MODEL:]], axis=1)
    return dict(pre_norm_w=p[0:1], conv_b=p[1:2], rg_ba=p[2:3], rg_bx=p[3:4], rg_lambda=p[4:5],
                post_norm_w=p[5:6], loss_row=p[6:7], b_in=b_in, conv_w_full=p[18:22],
                rg_wa=p[24:88].reshape(1, 16, 64, 64), rg_wx=p[88:152].reshape(1, 16, 64, 64))


def _reduce_small(parts, w, m, v, vectors):
    nvec = len(vectors)

    def body(p_ref, w_ref, m_ref, v_ref, *refs):
        ins, outs = refs[:3 * nvec], refs[3 * nvec:]
        g = p_ref[0]
        for j in range(1, N_DEV):
            g = g + p_ref[j]
        d, nm, nv = _adamw(w_ref[...], g, m_ref[...], v_ref[...])
        for k, val in enumerate((g, d, nm, nv)):
            outs[k][...] = val
        for i in range(nvec):
            gi = g[i:i + 1, :]
            di, nmi, nvi = _adamw(ins[3 * i][...], gi, ins[3 * i + 1][...], ins[3 * i + 2][...])
            for k, val in enumerate((gi, di, nmi, nvi)):
                outs[4 + 4 * i + k][...] = val
        outs[-1][...] = jnp.zeros((8, LANES), F32) + (0.5 / D_MODEL) * jnp.sum(g[LOSS_ROW:LOSS_ROW + 1, :])

    sh = jax.ShapeDtypeStruct((SMALL_ROWS, D_MODEL), F32)
    vec = jax.ShapeDtypeStruct((1, D_MODEL), F32)
    res = pl.pallas_call(
        body, name="reduce_small",
        out_shape=[sh, sh, sh, sh] + [vec] * (4 * nvec) + [jax.ShapeDtypeStruct((8, LANES), F32)],
    )(parts, w, m, v, *[a for t3 in vectors for a in t3])
    return res[:4], [res[4 + 4 * i:8 + 4 * i] for i in range(nvec)], res[-1]


def kernel(x, pre_norm_w, w_in, b_in, conv_w, conv_b, rg_wa, rg_ba, rg_wx, rg_bx, rg_lambda, w_branch_a, w_branch_r, w_out, post_norm_w, loss_target, m_pre_norm_w, m_w_in, m_b_in, m_conv_w, m_conv_b, m_rg_wa, m_rg_ba, m_rg_wx, m_rg_bx, m_rg_lambda, m_w_branch_a, m_w_branch_r, m_w_out, m_post_norm_w, v_pre_norm_w, v_w_in, v_b_in, v_conv_w, v_conv_b, v_rg_wa, v_rg_ba, v_rg_wx, v_rg_bx, v_rg_lambda, v_w_branch_a, v_w_branch_r, v_w_out, v_post_norm_w):
    b, s, _ = x.shape
    t = b * s
    me = 4 * lax.axis_index("x") + 2 * lax.axis_index("y") + lax.axis_index("c")
    shard_rows = D_MODEL // N_DEV

    place = jnp.stack([lax.axis_index("c"), 2 * lax.axis_index("x") + lax.axis_index("y")]).astype(jnp.int32)
    w_in_all = _gather(w_in[0].T.astype(BF16), "gather_w_in")
    wt_full = w_in_all.reshape(IN_TOTAL, D_MODEL)
    conv_terms = jnp.concatenate(_split3(conv_w[0]), axis=0)
    conv_pad = jnp.pad(conv_terms, ((0, 16 - 3 * CONV_W), (0, D_MODEL - LANES)))
    sq_stack = jnp.concatenate([w_branch_a[0].astype(BF16), w_branch_r[0].astype(BF16), w_out[0].astype(BF16),
                                conv_pad], axis=0)
    sq_sems, sq_src, sq_land, sq_token = _gather_start(sq_stack, w_in_all, "gather_w_sq_start")

    w_qkv = _interleave_rows(wt_full[0:3 * D_MODEL])
    w_f = jnp.pad(wt_full[3 * D_MODEL:3 * D_MODEL + HEADS], ((0, LANES - HEADS), (0, 0)))
    w_rest = wt_full[3 * D_MODEL + HEADS:IN_USED]
    b_qkv = _interleave_qkv(b_in[:, 0:3 * D_MODEL]) + sq_token[0, 0]
    b_f = jnp.pad(b_in[:, 3 * D_MODEL:3 * D_MODEL + HEADS], ((0, 0), (0, LANES - HEADS)))
    b_rest = b_in[:, 3 * D_MODEL + HEADS:IN_USED]

    def blockdiag(w):
        w2 = w.reshape(N_CBLK, 2, HEAD_DIM, HEAD_DIM)
        zz = jnp.zeros((N_CBLK, HEAD_DIM, HEAD_DIM), w.dtype)
        top = jnp.concatenate([w2[:, 0], zz], axis=2)
        bot = jnp.concatenate([zz, w2[:, 1]], axis=2)
        return jnp.concatenate([top, bot], axis=1).astype(BF16)

    bda, bdx = blockdiag(rg_wa[0]), blockdiag(rg_wx[0])

    x2 = x.reshape(t, D_MODEL)
    tgt2 = loss_target.reshape(t, D_MODEL)
    h = _prenorm(x2, pre_norm_w)
    qkv = _mm_bias(h, w_qkv, b_qkv, BF16, "inproj_qkv")
    zrest = _mm_bias(h, w_rest, b_rest, BF16, "inproj_rest")
    zf = _mm_bias(h, w_f, b_f, F32, "inproj_f")
    qkv3 = qkv.reshape(b, s, 3 * D_MODEL)
    zrest3 = zrest.reshape(b, s, 5 * D_MODEL)
    zf3 = zf.reshape(b, s, LANES)
    cexp3, crow = _fgate_fwd(zf3)
    yatt3, lse, ga3 = _attn_fwd(qkv3, cexp3, crow, zrest3)

    sq_all = _gather_wait(sq_sems, sq_src, sq_land, ga3, "gather_w_sq_wait")
    sq_all = lax.dynamic_update_slice(sq_all, sq_stack[None], (me, 0, 0))
    wa = sq_all[:, 0:shard_rows].reshape(D_MODEL, D_MODEL)
    wr = sq_all[:, shard_rows:2 * shard_rows].reshape(D_MODEL, D_MODEL)
    wo = sq_all[:, 2 * shard_rows:3 * shard_rows].reshape(D_MODEL, D_MODEL)
    conv_all = sq_all[:, 3 * shard_rows:3 * shard_rows + 3 * CONV_W, 0:LANES].astype(F32)
    conv_all = (conv_all[:, 0:CONV_W] + conv_all[:, CONV_W:2 * CONV_W]) + conv_all[:, 2 * CONV_W:3 * CONV_W]
    conv_full = conv_all.transpose(1, 0, 2).reshape(CONV_W, D_MODEL)

    ylru3, gr3 = _rnn_fwd(zrest3, conv_full, conv_b, bda, bdx, rg_ba, rg_bx, rg_lambda)
    ga, gr = ga3.reshape(t, D_MODEL), gr3.reshape(t, D_MODEL)
    ya, yr, mm = _branch_merge(ga, gr, wa, wr, zrest)
    dy, do, acc_out = _out_loss(mm, wo, x2, tgt2, post_norm_w)

    dya, dyr, dz_mga, dz_mgr = _merge_bwd(do, wo, zrest, ya, yr)
    dyatt, dz_ga, dylru, dz_gr = _branch_bwd(dya, dyr, wa, wr, zrest, yatt3.reshape(t, D_MODEL),
                                             ylru3.reshape(t, D_MODEL))
    dz_xr3, pvec, dbd = _rnn_bwd(zrest3, ylru3, dylru.reshape(b, s, D_MODEL), conv_full, conv_b, bda, bdx,
                                 rg_ba, rg_bx, rg_lambda)
    dqkv3, dc3 = _attn_bwd(qkv3, dyatt.reshape(b, s, D_MODEL), yatt3, lse, crow, cexp3)
    dz_f = _fgate_bwd(dc3, zf3)
    dz_qkv = dqkv3.reshape(t, 3 * D_MODEL)
    dz_xr = dz_xr3.reshape(t, D_MODEL)

    dw_qkv, db_qkv = _mm_tn(dz_qkv, h, "dw_qkv")
    dw_f, db_f = _mm_tn(dz_f, h, "dw_f")
    dw_parts, db_parts = [], []
    for nm, dzp in (("ga", dz_ga), ("xr", dz_xr), ("gr", dz_gr), ("mga", dz_mga), ("mgr", dz_mgr)):
        dwp, dbp = _mm_tn(dzp, h, "dw_" + nm)
        dw_parts.append(dwp)
        db_parts.append(dbp[0:1])
    dw_a, _ = _mm_tn(ga, dya, "dw_a")
    dw_r, _ = _mm_tn(gr, dyr, "dw_r")
    dw_o, _ = _mm_tn(mm, do, "dw_o")

    zeros_tail = jnp.zeros((IN_TOTAL - IN_USED, D_MODEL), F32)
    dwt_full = jnp.concatenate([_deinterleave_rows(dw_qkv), dw_f[0:HEADS]] + dw_parts + [zeros_tail], axis=0)
    dw_in_send = dwt_full.reshape(N_CHIPS, 2, W_SHARD, D_MODEL).transpose(1, 0, 2, 3)
    by_dest = lambda a: a.reshape(N_CHIPS, 2, shard_rows, D_MODEL).transpose(1, 0, 2, 3)
    dw_sq_send = jnp.concatenate([by_dest(dw_a), by_dest(dw_r), by_dest(dw_o)], axis=2)

    sib_in, sib_sq = _swap_with_sibling([dw_in_send, dw_sq_send], "swap_dw")
    chip_in, own_in = _pair_add(dw_in_send, sib_in, place, "pair_add_in")
    chip_sq, own_sq = _pair_add(dw_sq_send, sib_sq, place, "pair_add_sq")
    sems, sent, lands, token = _exchange_chips_start([chip_in, chip_sq], "exchange_dw_start")

    wt = lambda lo: w_rest[lo * D_MODEL:(lo + 1) * D_MODEL]
    dh_a = _dh_partial([(dz_qkv, w_qkv), (dz_f, w_f)], token, "dh_qkv")
    grad_x2, acc_pre = _dh_final(
        [(dz_ga, wt(0)), (dz_xr, wt(1)), (dz_gr, wt(2)), (dz_mga, wt(3)), (dz_mgr, wt(4))],
        dh_a, x2, dy, pre_norm_w)

    db_in_full = jnp.concatenate([_deinterleave_qkv(db_qkv[0:1]), db_f[0:1, 0:HEADS]] + db_parts
                                 + [jnp.zeros((1, IN_TOTAL - IN_USED), F32)], axis=1)
    d_rg_wa = jnp.stack([dbd[:, 0, 0:HEAD_DIM, 0:HEAD_DIM], dbd[:, 0, HEAD_DIM:, HEAD_DIM:]], axis=1)
    d_rg_wx = jnp.stack([dbd[:, 1, 0:HEAD_DIM, 0:HEAD_DIM], dbd[:, 1, HEAD_DIM:, HEAD_DIM:]], axis=1)
    small_g = _pack_small(acc_pre[0:1], pvec[4:5], pvec[5:6], pvec[6:7], pvec[7:8], acc_out[0:1], acc_out[1:2],
                          db_in_full, pvec[0:4], d_rg_wa, d_rg_wx)
    sm_sems, sm_src, sm_land, sm_token = _gather_start(small_g, grad_x2, "gather_small_start")
    recv_in, recv_sq = _exchange_chips_wait(sems, sent, lands, sm_token, "exchange_dw_wait")

    g_in, d_in, nm_in, nv_in = [a.T for a in _reduce_adamw(
        own_in, recv_in, place, w_in[0].T, m_w_in[0].T, v_w_in[0].T, "adamw_w_in")]
    sq_out = _reduce_adamw_stacked(
        own_sq, recv_sq, place,
        [(w_branch_a, m_w_branch_a, v_w_branch_a), (w_branch_r, m_w_branch_r, v_w_branch_r),
         (w_out, m_w_out, v_w_out)], "adamw_w_sq")
    small_all = _gather_wait(sm_sems, sm_src, sm_land, sq_out[2][1], "gather_small_wait")
    small_all = lax.dynamic_update_slice(small_all, small_g[None], (me, 0, 0))

    def place_conv(a):
        return lax.dynamic_update_slice(jnp.zeros((CONV_W, D_MODEL), F32), a[0], (0, me * LANES))

    zrow = jnp.zeros((1, D_MODEL), F32)
    vector_names = ["pre_norm_w", "conv_b", "rg_ba", "rg_bx", "rg_lambda", "post_norm_w"]
    vectors = [(pre_norm_w, m_pre_norm_w, v_pre_norm_w), (conv_b, m_conv_b, v_conv_b), (rg_ba, m_rg_ba, v_rg_ba),
               (rg_bx, m_rg_bx, v_rg_bx), (rg_lambda, m_rg_lambda, v_rg_lambda),
               (post_norm_w, m_post_norm_w, v_post_norm_w)]
    small_w = _pack_small(zrow, zrow, zrow, zrow, zrow, zrow, zrow, b_in, place_conv(conv_w), rg_wa[0], rg_wx[0])
    small_m = _pack_small(zrow, zrow, zrow, zrow, zrow, zrow, zrow, m_b_in, place_conv(m_conv_w), m_rg_wa[0],
                          m_rg_wx[0])
    small_v = _pack_small(zrow, zrow, zrow, zrow, zrow, zrow, zrow, v_b_in, place_conv(v_conv_w), v_rg_wa[0],
                          v_rg_wx[0])
    packed, vector_out, loss_tile = _reduce_small(small_all, small_w, small_m, small_v, vectors)
    outs_small = [_unpack_small(p) for p in packed]
    loss = loss_tile[0, 0]

    def leaf(kind, name):
        if name == "w_in":
            return (g_in, d_in, nm_in, nv_in)[kind][None]
        if name in ("w_branch_a", "w_branch_r", "w_out"):
            return sq_out[("w_branch_a", "w_branch_r", "w_out").index(name)][kind]
        if name == "conv_w":
            return lax.dynamic_slice(outs_small[kind]["conv_w_full"], (0, me * LANES), (CONV_W, LANES))[None]
        if name in vector_names:
            return vector_out[vector_names.index(name)][kind]
        return outs_small[kind][name]

    names = ["pre_norm_w", "w_in", "b_in", "conv_w", "conv_b", "rg_wa", "rg_ba", "rg_wx", "rg_bx", "rg_lambda",
             "w_branch_a", "w_branch_r", "w_out", "post_norm_w"]
    out = [loss, grad_x2.reshape(b, s, D_MODEL)]
    for kind in range(4):
        out += [leaf(kind, nm) for nm in names]
    return tuple(out)
```

```python
import jax
import jax.numpy as jnp
from jax import lax
from jax.experimental import pallas as pl
from jax.experimental.pallas import tpu as pltpu

F32 = jnp.float32
BF16 = jnp.bfloat16

N_DEV = 8
D_MODEL = 1024
HEADS = 16
HEAD_DIM = 64
HEAD_PAIRS = HEADS // 2
LANES = 128
N_CBLK = D_MODEL // LANES
CONV_W = 4
RG_C = 8.0
NORM_EPS = 1e-6
MASK_VALUE = -1e30
IN_USED = 8208
IN_TOTAL = 9232
W_SHARD = IN_TOTAL // N_DEV

ADAM_LR = 0.001
ADAM_B1 = 0.9
ADAM_B2 = 0.999
ADAM_EPS = 1e-08
ADAM_WD = 0.01
ADAM_STEP = 10

ATT_TILE_FWD = 256
ATT_TILE_BWD = 512
SCAN_TILE = 256
SMALL_ROWS = 152
LOSS_ROW = 6


def _cparams(sem=None, vmem_mb=None):
    kw = {}
    if sem is not None:
        kw["dimension_semantics"] = sem
    if vmem_mb is not None:
        kw["vmem_limit_bytes"] = vmem_mb * 1024 * 1024
    return pltpu.CompilerParams(**kw)


def _sigmoid(x):
    return 1.0 / (1.0 + jnp.exp(-x))


def _softplus(x):
    return jnp.maximum(x, 0.0) + jnp.log1p(jnp.exp(-jnp.abs(x)))


def _one_minus_exp(y, exp_y):
    series = -y * (1.0 + y * (1.0 / 2 + y * (1.0 / 6 + y * (1.0 / 24 + y * (1.0 / 120)))))
    return jnp.where(y > -0.0625, series, 1.0 - exp_y)


def _split3(x):
    hi = x.astype(BF16)
    r1 = x - hi.astype(F32)
    mid = r1.astype(BF16)
    lo = (r1 - mid.astype(F32)).astype(BF16)
    return hi, mid, lo


def _dot(a, b):
    return jnp.dot(a, b, preferred_element_type=F32)


def _dot_nt(a, b):
    return lax.dot_general(a, b, (((1,), (1,)), ((), ())), preferred_element_type=F32)


def _dot_tn(a, b):
    return lax.dot_general(a, b, (((0,), (0,)), ((), ())), preferred_element_type=F32)


def _iota(shape, dim):
    return lax.broadcasted_iota(jnp.int32, shape, dim)


_ANY = pl.BlockSpec(memory_space=pl.ANY)
_MESH = pl.DeviceIdType.MESH
N_CHIPS = 4


def _place():
    x, y, c = lax.axis_index("x"), lax.axis_index("y"), lax.axis_index("c")
    other_chips = [(1 - x, y), (x, 1 - y), (1 - x, 1 - y)]
    return x, y, c, other_chips


def _gather(x_shard, name):
    def body(x_ref, out_ref, send_sems, recv_sems, local_sem):
        x, y, c, chips = _place()
        me, sibling = (x, y, c), (x, y, 1 - c)

        def slot(p):
            return out_ref.at[4 * p[0] + 2 * p[1] + p[2]]

        def copy(k, block, to, src=None):
            return pltpu.make_async_remote_copy(
                src_ref=slot(block) if src is None else src, dst_ref=slot(block),
                send_sem=send_sems.at[k], recv_sem=recv_sems.at[k], device_id=to, device_id_type=_MESH)

        mine = pltpu.make_async_copy(x_ref, slot(me), local_sem)
        mine.start()
        first = [copy(0, me, sibling, src=x_ref)]
        first += [copy(1 + j, me, (*chip, c), src=x_ref) for j, chip in enumerate(chips)]
        for cp in first:
            cp.start()
        passed = [copy(4 + j, (*chip, c), sibling) for j, chip in enumerate(chips)]
        for j, chip in enumerate(chips):
            copy(1 + j, (*chip, c), me).wait_recv()
            passed[j].start()
        copy(0, sibling, me).wait_recv()
        for j, chip in enumerate(chips):
            copy(4 + j, (*chip, 1 - c), me).wait_recv()
        for cp in first + passed:
            cp.wait_send()
        mine.wait()

    return pl.pallas_call(
        body, name=name,
        out_shape=jax.ShapeDtypeStruct((N_DEV,) + tuple(x_shard.shape), x_shard.dtype),
        in_specs=[_ANY], out_specs=_ANY,
        scratch_shapes=[pltpu.SemaphoreType.DMA((7,)), pltpu.SemaphoreType.DMA((7,)), pltpu.SemaphoreType.DMA],
    )(x_shard)


def _swap_with_sibling(srcs, name):
    n = len(srcs)

    def body(*refs):
        src_refs, out_refs = refs[:n], refs[n:2 * n]
        send_sems, recv_sems = refs[2 * n:]
        x, y, c, _ = _place()
        cps = [pltpu.make_async_remote_copy(
            src_ref=src_refs[i].at[1 - c], dst_ref=out_refs[i], send_sem=send_sems.at[i], recv_sem=recv_sems.at[i],
            device_id=(x, y, 1 - c), device_id_type=_MESH) for i in range(n)]
        for cp in cps:
            cp.start()
        for cp in cps:
            cp.wait()

    return pl.pallas_call(
        body, name=name,
        out_shape=[jax.ShapeDtypeStruct(a.shape[1:], a.dtype) for a in srcs],
        in_specs=[_ANY] * n, out_specs=[_ANY] * n,
        scratch_shapes=[pltpu.SemaphoreType.DMA((n,)), pltpu.SemaphoreType.DMA((n,))],
    )(*srcs)


def _blocks_2d(r, c):
    if r % 128 == 0:
        return (128, c), r // 128, lambda i: (i, 0)
    return (r, 256), c // 256, lambda i: (0, i)


def _pair_add(src, recv, place, name):
    _, _, r, c = src.shape
    blk, nblk, at = _blocks_2d(r, c)

    def body(place_ref, a_ref, b_ref, q16_ref, own_ref):
        q = a_ref[...] + b_ref[...]
        q16_ref[...] = q.astype(BF16)

        @pl.when(pl.program_id(1) == place_ref[1])
        def _():
            own_ref[...] = q

    grid_spec = pltpu.PrefetchScalarGridSpec(
        num_scalar_prefetch=1, grid=(nblk, N_CHIPS),
        in_specs=[pl.BlockSpec((None, None) + blk, lambda i, j, pr: (pr[0], j) + at(i)),
                  pl.BlockSpec((None,) + blk, lambda i, j, pr: (j,) + at(i))],
        out_specs=[pl.BlockSpec((None,) + blk, lambda i, j, pr: (j,) + at(i)),
                   pl.BlockSpec(blk, lambda i, j, pr: at(i))])
    return pl.pallas_call(
        body, name=name, grid_spec=grid_spec,
        out_shape=[jax.ShapeDtypeStruct((N_CHIPS, r, c), BF16), jax.ShapeDtypeStruct((r, c), F32)],
        compiler_params=_cparams(("parallel", "arbitrary")),
    )(place, src, recv)


_HBM = pl.BlockSpec(memory_space=pltpu.HBM)
_SEM = pl.BlockSpec(memory_space=pltpu.SEMAPHORE)
_DATAFLOW = pltpu.SideEffectType.DATAFLOW_SIDE_EFFECTING


def _chip_copy(src_ref, land_ref, send_sem, recv_sem, k, chips, c, land):
    chip = chips[k]
    return pltpu.make_async_remote_copy(
        src_ref=src_ref.at[2 * chip[0] + chip[1]], dst_ref=land_ref.at[land],
        send_sem=send_sem, recv_sem=recv_sem, device_id=(*chip, c), device_id_type=_MESH)


def _exchange_chips_start(srcs, name):
    n = len(srcs)
    ncp = 3 * n

    def body(*refs):
        src_refs, land_refs = refs[:n], refs[n:2 * n]
        sems = refs[4 * n:4 * n + 2 * ncp]
        token = refs[-1]
        x, y, c, chips = _place()
        for i in range(n):
            for k in range(3):
                j = 3 * i + k
                _chip_copy(src_refs[i], land_refs[i], sems[j], sems[ncp + j], k, chips, c, 2 * x + y).start()
        token[...] = jnp.zeros_like(token)

    hbm = [pltpu.HBM(a.shape, a.dtype) for a in srcs]
    lands = [pltpu.with_memory_space_constraint(lax.empty(a.shape, a.dtype), pltpu.HBM) for a in srcs]
    res = pl.pallas_call(
        body, name=name,
        out_shape=(*hbm, *hbm, *([pltpu.SemaphoreType.DMA(())] * (2 * ncp)), jax.ShapeDtypeStruct((8, LANES), F32)),
        in_specs=[_HBM] * (2 * n),
        out_specs=(*([_HBM] * (2 * n)), *([_SEM] * (2 * ncp)), pl.BlockSpec(memory_space=pltpu.VMEM)),
        input_output_aliases={i: i for i in range(2 * n)},
        compiler_params=pltpu.CompilerParams(has_side_effects=_DATAFLOW),
    )(*[pltpu.with_memory_space_constraint(a, pltpu.HBM) for a in srcs], *lands)
    return list(res[2 * n:2 * n + 2 * ncp]), list(res[:n]), list(res[n:2 * n]), res[-1]


def _exchange_chips_wait(sems, srcs, lands, after, name):
    n = len(srcs)
    ncp = 3 * n

    def body(*refs):
        src_refs, land_refs = refs[:n], refs[n:2 * n]
        sem_refs = refs[2 * n:2 * n + 2 * ncp]
        x, y, c, chips = _place()
        for i in range(n):
            for k in range(3):
                j = 3 * i + k
                cp = _chip_copy(src_refs[i], land_refs[i], sem_refs[j], sem_refs[ncp + j], k, chips, c,
                                2 * chips[k][0] + chips[k][1])
                cp.wait_send()
                cp.wait_recv()

    hbm = [pltpu.HBM(a.shape, a.dtype) for a in srcs]
    res = pl.pallas_call(
        body, name=name, out_shape=(*hbm, *hbm),
        in_specs=[_HBM] * (2 * n) + [_SEM] * (2 * ncp) + [_ANY], out_specs=tuple([_HBM] * (2 * n)),
        input_output_aliases={i: i for i in range(2 * n)},
        compiler_params=pltpu.CompilerParams(has_side_effects=_DATAFLOW),
    )(*srcs, *lands, *sems, after)
    return list(res[n:2 * n])


def _peer_copy(src_ref, land_ref, send_sem, recv_sem, k, place, land):
    x, y, c = place
    peer = (1 - x if k & 4 else x, 1 - y if k & 2 else y, 1 - c if k & 1 else c)
    return pltpu.make_async_remote_copy(
        src_ref=src_ref, dst_ref=land_ref.at[land], send_sem=send_sem, recv_sem=recv_sem,
        device_id=peer, device_id_type=_MESH)


def _gather_start(x_shard, after, name):
    npeer = N_DEV - 1

    def body(x_ref, land_ref, after_ref, x_thru, land_thru, *rest):
        sems, token = rest[:2 * npeer], rest[-1]
        x, y, c, _ = _place()
        for k in range(1, N_DEV):
            _peer_copy(x_ref, land_ref, sems[k - 1], sems[npeer + k - 1], k, (x, y, c), 4 * x + 2 * y + c).start()
        token[...] = jnp.zeros_like(token)

    land = pltpu.with_memory_space_constraint(lax.empty((N_DEV,) + tuple(x_shard.shape), x_shard.dtype), pltpu.HBM)
    res = pl.pallas_call(
        body, name=name,
        out_shape=(pltpu.HBM(x_shard.shape, x_shard.dtype), pltpu.HBM(land.shape, land.dtype),
                   *([pltpu.SemaphoreType.DMA(())] * (2 * npeer)), jax.ShapeDtypeStruct((8, LANES), F32)),
        in_specs=[_HBM, _HBM, _ANY],
        out_specs=(_HBM, _HBM, *([_SEM] * (2 * npeer)), pl.BlockSpec(memory_space=pltpu.VMEM)),
        input_output_aliases={0: 0, 1: 1},
        compiler_params=pltpu.CompilerParams(has_side_effects=_DATAFLOW),
    )(pltpu.with_memory_space_constraint(x_shard, pltpu.HBM), land, after)
    return list(res[2:2 + 2 * npeer]), res[0], res[1], res[-1]


def _gather_wait(sems, src, land, after, name):
    npeer = N_DEV - 1

    def body(x_ref, land_ref, *rest):
        sem_refs = rest[:2 * npeer]
        x, y, c, _ = _place()
        for k in range(1, N_DEV):
            peer_index = (4 * x + 2 * y + c) ^ k
            cp = _peer_copy(x_ref, land_ref, sem_refs[k - 1], sem_refs[npeer + k - 1], k, (x, y, c), peer_index)
            cp.wait_send()
            cp.wait_recv()

    res = pl.pallas_call(
        body, name=name, out_shape=(pltpu.HBM(src.shape, src.dtype), pltpu.HBM(land.shape, land.dtype)),
        in_specs=[_HBM, _HBM] + [_SEM] * (2 * npeer) + [_ANY], out_specs=(_HBM, _HBM),
        input_output_aliases={0: 0, 1: 1},
        compiler_params=pltpu.CompilerParams(has_side_effects=_DATAFLOW),
    )(src, land, *sems, after)
    return res[1]


def _prenorm(x2, w):
    t = x2.shape[0]
    tm = min(512, t)

    def body(x_ref, w_ref, h_ref):
        x = x_ref[...]
        r = lax.rsqrt(jnp.mean(x * x, axis=-1, keepdims=True) + NORM_EPS)
        h_ref[...] = (x * r * w_ref[...]).astype(BF16)

    return pl.pallas_call(
        body, name="prenorm", grid=(t // tm,),
        in_specs=[pl.BlockSpec((tm, D_MODEL), lambda i: (i, 0)), pl.BlockSpec((1, D_MODEL), lambda i: (0, 0))],
        out_specs=pl.BlockSpec((tm, D_MODEL), lambda i: (i, 0)),
        out_shape=jax.ShapeDtypeStruct((t, D_MODEL), BF16),
        compiler_params=_cparams(("parallel",)),
    )(x2, w)


def _mm_bias(a, bt, bias, out_dtype, name):
    m, k = a.shape
    n = bt.shape[0]
    tm = min(512, m)
    tn = min(1024, n)

    def body(a_ref, bt_ref, bias_ref, o_ref):
        aa = a_ref[...]
        for j in range(n // tn):
            cols = slice(j * tn, (j + 1) * tn)
            o_ref[:, cols] = (_dot_nt(aa, bt_ref[cols, :]) + bias_ref[:, cols]).astype(o_ref.dtype)

    return pl.pallas_call(
        body, name=name, grid=(m // tm,),
        in_specs=[pl.BlockSpec((tm, k), lambda i: (i, 0)), pl.BlockSpec((n, k), lambda i: (0, 0)),
                  pl.BlockSpec((1, n), lambda i: (0, 0))],
        out_specs=pl.BlockSpec((tm, n), lambda i: (i, 0)),
        out_shape=jax.ShapeDtypeStruct((m, n), out_dtype),
        compiler_params=_cparams(("parallel",), vmem_mb=48),
    )(a, bt, bias)


def _mm_tn(a, b, name):
    t, m = a.shape
    n = b.shape[1]
    tm = min(1024, m)
    tk = min(2048, t)

    def body(a_ref, b_ref, o_ref, s_ref):
        kk = pl.program_id(1)

        @pl.when(kk == 0)
        def _():
            o_ref[...] = jnp.zeros_like(o_ref)
            s_ref[...] = jnp.zeros_like(s_ref)

        aa = a_ref[...]
        o_ref[...] += _dot_tn(aa, b_ref[...])
        s_ref[0:1, :] += jnp.sum(aa.astype(F32), axis=0, keepdims=True)

    return pl.pallas_call(
        body, name=name, grid=(m // tm, t // tk),
        in_specs=[pl.BlockSpec((tk, tm), lambda i, kk: (kk, i)), pl.BlockSpec((tk, n), lambda i, kk: (kk, 0))],
        out_specs=[pl.BlockSpec((tm, n), lambda i, kk: (i, 0)), pl.BlockSpec((8, tm), lambda i, kk: (0, i))],
        out_shape=[jax.ShapeDtypeStruct((m, n), F32), jax.ShapeDtypeStruct((8, m), F32)],
        compiler_params=_cparams(("parallel", "arbitrary"), vmem_mb=48),
    )(a, b)


def _fgate_fwd(zf3):
    b, s, _ = zf3.shape
    tb = SCAN_TILE
    nb = s // tb

    def body(z_ref, cexp_ref, crow_ref):
        tri = (_iota((tb, tb), 1) <= _iota((tb, tb), 0)).astype(BF16)
        expand = ((_iota((LANES, D_MODEL), 1) >> 6) == _iota((LANES, D_MODEL), 0)).astype(BF16)
        carry = jnp.zeros((1, LANES), F32)
        for i in range(nb):
            rows = slice(i * tb, (i + 1) * tb)
            z = z_ref[rows, :]
            lf = jnp.minimum(z, 0.0) - jnp.log1p(jnp.exp(-jnp.abs(z)))
            cb = sum(_dot(tri, part) for part in _split3(lf)) + carry
            carry = cb[tb - 1:tb, :]
            cexp_ref[rows, :] = sum(_dot(part, expand) for part in _split3(cb))
            crow_ref[:, rows] = cb.T[0:HEADS, :]

    return pl.pallas_call(
        body, name="fgate_fwd", grid=(b,),
        in_specs=[pl.BlockSpec((None, s, LANES), lambda i: (i, 0, 0))],
        out_specs=[pl.BlockSpec((None, s, D_MODEL), lambda i: (i, 0, 0)),
                   pl.BlockSpec((None, HEADS, s), lambda i: (i, 0, 0))],
        out_shape=[jax.ShapeDtypeStruct((b, s, D_MODEL), F32), jax.ShapeDtypeStruct((b, HEADS, s), F32)],
        compiler_params=_cparams(("parallel",)),
    )(zf3)


def _fgate_bwd(dc3, zf3):
    b, s, _ = zf3.shape
    tb = SCAN_TILE
    nb = s // tb

    def body(dc_ref, z_ref, o_ref):
        tri = (_iota((tb, tb), 1) >= _iota((tb, tb), 0)).astype(BF16)
        carry = jnp.zeros((1, LANES), F32)
        for i in reversed(range(nb)):
            rows = slice(i * tb, (i + 1) * tb)
            dlf = sum(_dot(tri, part) for part in _split3(dc_ref[rows, :])) + carry
            carry = dlf[0:1, :]
            o_ref[rows, :] = (dlf * _sigmoid(-z_ref[rows, :])).astype(BF16)

    return pl.pallas_call(
        body, name="fgate_bwd", grid=(b,),
        in_specs=[pl.BlockSpec((None, s, LANES), lambda i: (i, 0, 0)),
                  pl.BlockSpec((None, s, LANES), lambda i: (i, 0, 0))],
        out_specs=pl.BlockSpec((s, LANES), lambda i: (i, 0)),
        out_shape=jax.ShapeDtypeStruct((b * s, LANES), BF16),
        compiler_params=_cparams(("parallel",)),
    )(dc3, zf3)


def _spare(hh):
    return HEAD_DIM if hh == 0 else 0


def _put_cols(tile, mine, cols, first):
    lane = _iota((1, LANES), 1)
    out = jnp.where(mine, tile, jnp.zeros((), tile.dtype))
    for j, c in enumerate(cols):
        out = jnp.where(lane == first + j, c, out)
    return out


def _put_rows(tile, mine, rows, first):
    sub = _iota((LANES, 1), 0)
    out = jnp.where(mine, tile, jnp.zeros((), tile.dtype))
    for j, r in enumerate(rows):
        out = jnp.where(sub == first + j, r, out)
    return out


def _transpose_bf16(a):
    return a.astype(F32).T.astype(BF16)


def _attn_fwd(qkv3, cexp3, crow, zrest3):
    b, s, _ = qkv3.shape
    ta = ATT_TILE_FWD
    nq = s // ta
    hd = HEAD_DIM
    crow5 = crow.reshape(b, HEAD_PAIRS, 2, nq, ta)

    def body(qkv_ref, cq_ref, ck_ref, g_ref, y_ref, lse_ref, ga_ref, kt_scr, v_scr):
        lane = _iota((1, LANES), 1)
        sub = _iota((LANES, 1), 0)
        lane_mine = (lane < hd, lane >= hd)
        sub_mine = (sub < hd, sub >= hd)
        causal = _iota((ta, ta), 0) >= _iota((ta, ta), 1)
        one = jnp.ones((), BF16)

        for kj in range(nq):
            rows = slice(kj * ta, (kj + 1) * ta)
            kt = _transpose_bf16(qkv_ref[rows, LANES:2 * LANES])
            v = qkv_ref[rows, 2 * LANES:3 * LANES]
            for hh in range(2):
                ck = list(_split3(-ck_ref[hh, kj:kj + 1, :]))
                kt_scr[hh, kj] = _put_rows(kt, sub_mine[hh], [one, one, one] + ck, _spare(hh))
                v_scr[hh, kj] = _put_cols(v, lane_mine[hh], [one], _spare(hh))

        for qi in range(nq):
            rows = slice(qi * ta, (qi + 1) * ta)
            q = qkv_ref[rows, 0:LANES] * 0.125
            cq = cq_ref[rows, :]
            qh = [_put_cols(q, lane_mine[hh], list(_split3(cq[:, hh * hd:hh * hd + 1])) + [one, one, one], _spare(hh))
                  for hh in range(2)]
            st = [(jnp.full((ta, 1), MASK_VALUE, F32), jnp.zeros((ta, LANES), F32))] * 2
            for kj in range(qi + 1):
                for hh in range(2):
                    m, acc = st[hh]
                    sc = _dot(qh[hh], kt_scr[hh, kj])
                    if kj == qi:
                        sc = jnp.where(causal, sc, MASK_VALUE)
                    mn = jnp.maximum(m, jnp.max(sc, axis=-1, keepdims=True))
                    p = jnp.exp(sc - mn).astype(BF16)
                    st[hh] = (mn, jnp.exp(m - mn) * acc + _dot(p, v_scr[hh, kj]))
            (ma, acca), (mb, accb) = st
            la = acca[:, hd:hd + 1]
            lb = accb[:, 0:1]
            y = jnp.where(lane_mine[0], acca * (1.0 / la), accb * (1.0 / lb))
            lse = jnp.where(lane_mine[0], ma + jnp.log(la), mb + jnp.log(lb)).T
            lse_ref[0, qi:qi + 1, :] = lse[0:1, :]
            lse_ref[1, qi:qi + 1, :] = lse[hd:hd + 1, :]
            y_ref[rows, :] = y
            g = g_ref[rows, :].astype(F32)
            ga_ref[rows, :] = (y * (g * _sigmoid(g))).astype(BF16)

    blk = lambda w: pl.BlockSpec((None, s, w), lambda i, p: (i, 0, p))
    rows5 = pl.BlockSpec((None, None, 2, nq, ta), lambda i, p: (i, p, 0, 0, 0))
    yatt3, lse5, ga3 = pl.pallas_call(
        body, name="attn_fwd", grid=(b, HEAD_PAIRS),
        in_specs=[blk(3 * LANES), blk(LANES), rows5, blk(LANES)],
        out_specs=[blk(LANES), rows5, blk(LANES)],
        out_shape=[jax.ShapeDtypeStruct((b, s, D_MODEL), F32),
                   jax.ShapeDtypeStruct((b, HEAD_PAIRS, 2, nq, ta), F32),
                   jax.ShapeDtypeStruct((b, s, D_MODEL), BF16)],
        scratch_shapes=[pltpu.VMEM((2, nq, LANES, ta), BF16), pltpu.VMEM((2, nq, ta, LANES), BF16)],
        compiler_params=_cparams(("parallel", "parallel")),
    )(qkv3, cexp3, crow5, zrest3)
    return yatt3, lse5.reshape(b, HEADS, s), ga3


def _attn_bwd(qkv3, do3, y3, lse, crow, cexp3):
    b, s, _ = qkv3.shape
    ta = ATT_TILE_BWD
    nq = s // ta
    hd = HEAD_DIM
    lse5 = lse.reshape(b, HEAD_PAIRS, 2, nq, ta)
    crow5 = crow.reshape(b, HEAD_PAIRS, 2, nq, ta)

    def body(qkv_ref, do_ref, y_ref, lse_ref, crow_ref, cexp_ref, dqkv_ref, dc_ref,
             qa_scr, doa_scr, qst_scr, dot_scr, kt_scr, vt_scr, dq_scr, rs_scr):
        pair = pl.program_id(1)
        lane = _iota((1, LANES), 1)
        sub = _iota((LANES, 1), 0)
        lane_mine = (lane < hd, lane >= hd)
        sub_mine = (sub < hd, sub >= hd)
        causal = _iota((ta, ta), 0) >= _iota((ta, ta), 1)
        one = jnp.ones((), BF16)
        zero = jnp.zeros((), BF16)

        @pl.when(pair == 0)
        def _():
            dc_ref[...] = jnp.zeros_like(dc_ref)

        for i in range(nq):
            rows = slice(i * ta, (i + 1) * ta)
            qs = qkv_ref[rows, 0:LANES] * 0.125
            qst = _transpose_bf16(qs)
            kt = _transpose_bf16(qkv_ref[rows, LANES:2 * LANES])
            vt = _transpose_bf16(qkv_ref[rows, 2 * LANES:3 * LANES])
            do = do_ref[rows, :]
            dof = do.astype(F32)
            dot = dof.T.astype(BF16)
            pr = y_ref[rows, :] * dof
            cq = cexp_ref[rows, :]
            lse_c = jnp.where(sub == 0, lse_ref[0, i:i + 1, :],
                              jnp.where(sub == 1, lse_ref[1, i:i + 1, :], 0.0)).T
            for hh in range(2):
                sp = _spare(hh)
                dsum = jnp.sum(jnp.where(lane_mine[hh], pr, 0.0), axis=-1, keepdims=True)
                bias = cq[:, hh * hd:hh * hd + 1] - lse_c[:, hh:hh + 1]
                qa_scr[hh, i] = _put_cols(qs, lane_mine[hh], list(_split3(bias)) + [one, one, one], sp)
                doa_scr[hh, i] = _put_cols(do, lane_mine[hh], list(_split3(-dsum)), sp)
                qst_scr[hh, i] = jnp.where(sub_mine[hh], qst, zero)
                dot_scr[hh, i] = jnp.where(sub_mine[hh], dot, zero)
                ck = list(_split3(-crow_ref[hh, i:i + 1, :]))
                kt_scr[hh, i] = _put_rows(kt, sub_mine[hh], [one, one, one] + ck, sp)
                vt_scr[hh, i] = _put_rows(vt, sub_mine[hh], [one, one, one], sp)
            dq_scr[i] = jnp.zeros((ta, LANES), F32)
            rs_scr[i] = jnp.zeros((ta, LANES), F32)

        for kj in range(nq):
            krows = slice(kj * ta, (kj + 1) * ta)
            k = qkv_ref[krows, LANES:2 * LANES]
            km = (jnp.where(lane_mine[0], k, zero), jnp.where(lane_mine[1], k, zero))
            dkt = jnp.zeros((LANES, ta), F32)
            dvt = jnp.zeros((LANES, ta), F32)
            dcp = [jnp.zeros((8, ta), F32), jnp.zeros((8, ta), F32)]
            for qi in range(kj, nq):
                dq = jnp.zeros((ta, LANES), F32)
                rs = []
                for hh in range(2):
                    sc = _dot(qa_scr[hh, qi], kt_scr[hh, kj])
                    if qi == kj:
                        sc = jnp.where(causal, sc, MASK_VALUE)
                    p = jnp.exp(sc)
                    dsf = p * _dot(doa_scr[hh, qi], vt_scr[hh, kj])
                    dcp[hh] = dcp[hh] + jnp.sum(dsf.reshape(ta // 8, 8, ta), axis=0)
                    rs.append(jnp.sum(dsf, axis=-1, keepdims=True))
                    ds = dsf.astype(BF16)
                    dq = dq + _dot(ds, km[hh])
                    dkt = dkt + _dot(qst_scr[hh, qi], ds)
                    dvt = dvt + _dot(dot_scr[hh, qi], p.astype(BF16))
                dq_scr[qi] += dq
                rs_scr[qi] += jnp.where(lane == 0, rs[0], jnp.where(lane == 1, rs[1], 0.0))
            dqkv_ref[krows, LANES:2 * LANES] = dkt.T.astype(BF16)
            dqkv_ref[krows, 2 * LANES:3 * LANES] = dvt.T.astype(BF16)
            dca = jnp.sum(dcp[0], axis=0, keepdims=True)
            dcb = jnp.sum(dcp[1], axis=0, keepdims=True)
            dcs = jnp.where(sub == 0, dca, jnp.where(sub == 1, dcb, 0.0)).T
            dc_ref[krows, :] += (jnp.where(lane == 2 * pair, -dcs[:, 0:1], 0.0)
                                 + jnp.where(lane == 2 * pair + 1, -dcs[:, 1:2], 0.0))
        for qi in range(nq):
            rows = slice(qi * ta, (qi + 1) * ta)
            dqkv_ref[rows, 0:LANES] = (dq_scr[qi] * 0.125).astype(BF16)
            rq = rs_scr[qi]
            dc_ref[rows, :] += (jnp.where(lane == 2 * pair, rq[:, 0:1], 0.0)
                                + jnp.where(lane == 2 * pair + 1, rq[:, 1:2], 0.0))

    blk = lambda w: pl.BlockSpec((None, s, w), lambda i, p: (i, 0, p))
    rows5 = pl.BlockSpec((None, None, 2, nq, ta), lambda i, p: (i, p, 0, 0, 0))
    by_rows = lambda: pltpu.VMEM((2, nq, ta, LANES), BF16)
    by_cols = lambda: pltpu.VMEM((2, nq, LANES, ta), BF16)
    return pl.pallas_call(
        body, name="attn_bwd", grid=(b, HEAD_PAIRS),
        in_specs=[blk(3 * LANES), blk(LANES), blk(LANES), rows5, rows5, blk(LANES)],
        out_specs=[blk(3 * LANES), pl.BlockSpec((None, s, LANES), lambda i, p: (i, 0, 0))],
        out_shape=[jax.ShapeDtypeStruct((b, s, 3 * D_MODEL), BF16), jax.ShapeDtypeStruct((b, s, LANES), F32)],
        scratch_shapes=[by_rows(), by_rows(), by_cols(), by_cols(), by_cols(), by_cols(),
                        pltpu.VMEM((nq, ta, LANES), F32), pltpu.VMEM((nq, ta, LANES), F32)],
        compiler_params=_cparams(("parallel", "arbitrary")),
    )(qkv3, do3, y3, lse5, crow5, cexp3)


def _shifted(v, ks, rows, s):
    low = rows[0:8, :]
    out = []
    for k in ks:
        r = pltpu.roll(v, k % s, 0)
        if k > 0:
            out.append(jnp.concatenate([jnp.where(low >= k, r[0:8, :], 0.0), r[8:, :]], axis=0))
        else:
            out.append(jnp.concatenate([r[:s - 8, :], jnp.where(low < 8 + k, r[s - 8:, :], 0.0)], axis=0))
    return out


def _rnn_common(xr, cw_ref, cb_ref, bda_ref, bdx_ref, ba_ref, bx_ref, lam_ref, s):
    rows = _iota((s, LANES), 0)
    x1, x2, x3 = _shifted(xr, (1, 2, 3), rows, s)
    xc = cb_ref[...] + cw_ref[0:1, :] * x3
    xc = xc + cw_ref[1:2, :] * x2
    xc = xc + cw_ref[2:3, :] * x1
    xc = xc + cw_ref[3:4, :] * xr
    xcb = xc.astype(BF16)
    r = _sigmoid(_dot(xcb, bda_ref[...]) + ba_ref[...])
    i = _sigmoid(_dot(xcb, bdx_ref[...]) + bx_ref[...])
    sp = _softplus(-lam_ref[...])
    log_a = (-RG_C * r) * sp
    a = jnp.exp(log_a)
    a2 = a * a
    sq = jnp.sqrt(jnp.maximum(_one_minus_exp(log_a + log_a, a2), 0.0))
    return rows, (x1, x2, x3), xc, xcb, r, i, sp, a, a2, sq


def _scan_down(a, u, rows, s, s1, s2):
    low = rows & 7
    for sh in (1, 2, 4):
        keep = low >= sh
        u = u + a * jnp.where(keep, pltpu.roll(u, sh, 0), 0.0)
        a = a * jnp.where(keep, pltpu.roll(a, sh, 0), 1.0)
    ng = s // 8
    s1[...] = a
    s2[...] = u
    at = s1[pl.ds(7, ng, stride=8), :]
    ut = s2[pl.ds(7, ng, stride=8), :]
    grow = _iota((ng, LANES), 0)
    sh = 1
    while sh < ng:
        keep = grow >= sh
        ut = ut + at * jnp.where(keep, pltpu.roll(ut, sh, 0), 0.0)
        if sh * 2 < ng:
            at = at * jnp.where(keep, pltpu.roll(at, sh, 0), 1.0)
        sh *= 2
    h_in = jnp.where(grow >= 1, pltpu.roll(ut, 1, 0), 0.0)
    for k in range(8):
        s1[pl.ds(k, ng, stride=8), :] = h_in
    return u + a * s1[...]


def _scan_up(a, g, rows, s, s1, s2):
    low = rows & 7
    for sh in (1, 2, 4):
        keep = low < 8 - sh
        g = g + a * jnp.where(keep, pltpu.roll(g, s - sh, 0), 0.0)
        a = a * jnp.where(keep, pltpu.roll(a, s - sh, 0), 1.0)
    ng = s // 8
    s1[...] = a
    s2[...] = g
    at = s1[pl.ds(0, ng, stride=8), :]
    gt = s2[pl.ds(0, ng, stride=8), :]
    grow = _iota((ng, LANES), 0)
    sh = 1
    while sh < ng:
        keep = grow < ng - sh
        gt = gt + at * jnp.where(keep, pltpu.roll(gt, ng - sh, 0), 0.0)
        if sh * 2 < ng:
            at = at * jnp.where(keep, pltpu.roll(at, ng - sh, 0), 1.0)
        sh *= 2
    g_in = jnp.where(grow < ng - 1, pltpu.roll(gt, ng - 1, 0), 0.0)
    for k in range(8):
        s1[pl.ds(k, ng, stride=8), :] = g_in
    return g + a * s1[...]


def _rnn_specs(s):
    blk = lambda off: pl.BlockSpec((None, s, LANES), lambda cb, i: (i, 0, off + cb))
    vec = lambda r: pl.BlockSpec((r, LANES), lambda cb, i: (0, cb))
    mat = pl.BlockSpec((None, LANES, LANES), lambda cb, i: (cb, 0, 0))
    return blk, vec, mat


def _rnn_fwd(zrest3, conv_w, conv_b, bda, bdx, ba, bx, lam):
    b, s, _ = zrest3.shape

    def body(xr_ref, g_ref, cw_ref, cb_ref, bda_ref, bdx_ref, ba_ref, bx_ref, lam_ref, h_ref, gr_ref, s1, s2):
        xr = xr_ref[...].astype(F32)
        rows, _, xc, _, _, i, _, a, _, sq = _rnn_common(
            xr, cw_ref, cb_ref, bda_ref, bdx_ref, ba_ref, bx_ref, lam_ref, s)
        h = _scan_down(a, sq * (i * xc), rows, s, s1, s2)
        h_ref[...] = h
        g = g_ref[...].astype(F32)
        gr_ref[...] = (h * (g * _sigmoid(g))).astype(BF16)

    blk, vec, mat = _rnn_specs(s)
    return pl.pallas_call(
        body, name="rnn_fwd", grid=(N_CBLK, b),
        in_specs=[blk(N_CBLK), blk(2 * N_CBLK), vec(CONV_W), vec(1), mat, mat, vec(1), vec(1), vec(1)],
        out_specs=[blk(0), blk(0)],
        out_shape=[jax.ShapeDtypeStruct((b, s, D_MODEL), F32), jax.ShapeDtypeStruct((b, s, D_MODEL), BF16)],
        scratch_shapes=[pltpu.VMEM((s, LANES), F32), pltpu.VMEM((s, LANES), F32)],
        compiler_params=_cparams(("parallel", "parallel")),
    )(zrest3, zrest3, conv_w, conv_b, bda, bdx, ba, bx, lam)


def _rnn_bwd(zrest3, h3, dh3, conv_w, conv_b, bda, bdx, ba, bx, lam):
    b, s, _ = zrest3.shape

    def body(xr_ref, h_ref, dh_ref, cw_ref, cb_ref, bda_ref, bdx_ref, ba_ref, bx_ref, lam_ref,
             dxr_ref, pv_ref, dbd_ref, s1, s2):
        @pl.when(pl.program_id(1) == 0)
        def _():
            pv_ref[...] = jnp.zeros_like(pv_ref)
            dbd_ref[...] = jnp.zeros_like(dbd_ref)

        xr = xr_ref[...].astype(F32)
        rows, (x1, x2, x3), xc, xcb, r, i, sp, a, a2, sq = _rnn_common(
            xr, cw_ref, cb_ref, bda_ref, bdx_ref, ba_ref, bx_ref, lam_ref, s)
        (a_next,) = _shifted(a, (-1,), rows, s)
        g = _scan_up(a_next, dh_ref[...], rows, s, s1, s2)
        (hp,) = _shifted(h_ref[...], (1,), rows, s)
        da = g * hp
        dsq = g * (i * xc)
        di = g * (sq * xc)
        dxc = g * (sq * i)
        dlog = da * a - dsq * (a2 / sq)
        dr = dlog * (-RG_C * sp)
        dpr = dr * (r * (1.0 - r))
        dpi = di * (i * (1.0 - i))
        dprb = dpr.astype(BF16)
        dpib = dpi.astype(BF16)
        dxc = dxc + _dot_nt(dprb, bda_ref[...]) + _dot_nt(dpib, bdx_ref[...])

        up1, up2, up3 = _shifted(dxc, (-1, -2, -3), rows, s)
        dxr = cw_ref[3:4, :] * dxc + cw_ref[2:3, :] * up1 + cw_ref[1:2, :] * up2 + cw_ref[0:1, :] * up3
        dxr_ref[...] = dxr.astype(BF16)

        def colsum(v):
            return jnp.sum(v, axis=0, keepdims=True)

        pv_ref[0:1, :] += colsum(dxc * x3)
        pv_ref[1:2, :] += colsum(dxc * x2)
        pv_ref[2:3, :] += colsum(dxc * x1)
        pv_ref[3:4, :] += colsum(dxc * xr)
        pv_ref[4:5, :] += colsum(dxc)
        pv_ref[5:6, :] += colsum(dpr)
        pv_ref[6:7, :] += colsum(dpi)
        pv_ref[7:8, :] += colsum(dlog * r) * (RG_C * _sigmoid(-lam_ref[...]))
        dbd_ref[0] += _dot_tn(xcb, dprb)
        dbd_ref[1] += _dot_tn(xcb, dpib)

    blk, vec, mat = _rnn_specs(s)
    hblk = pl.BlockSpec((None, s, LANES), lambda cb, i: (i, 0, cb))
    return pl.pallas_call(
        body, name="rnn_bwd", grid=(N_CBLK, b),
        in_specs=[blk(N_CBLK), hblk, hblk, vec(CONV_W), vec(1), mat, mat, vec(1), vec(1), vec(1)],
        out_specs=[pl.BlockSpec((s, LANES), lambda cb, i: (i, cb)), pl.BlockSpec((8, LANES), lambda cb, i: (0, cb)),
                   pl.BlockSpec((None, 2, LANES, LANES), lambda cb, i: (cb, 0, 0, 0))],
        out_shape=[jax.ShapeDtypeStruct((b * s, D_MODEL), BF16), jax.ShapeDtypeStruct((8, D_MODEL), F32),
                   jax.ShapeDtypeStruct((N_CBLK, 2, LANES, LANES), F32)],
        scratch_shapes=[pltpu.VMEM((s, LANES), F32), pltpu.VMEM((s, LANES), F32)],
        compiler_params=_cparams(("parallel", "arbitrary")),
    )(zrest3, h3, dh3, conv_w, conv_b, bda, bdx, ba, bx, lam)


def _branch_merge(ga, gr, wa, wr, zrest):
    t = ga.shape[0]
    tm = min(512, t)
    tn = D_MODEL

    def body(ga_ref, gr_ref, wa_ref, wr_ref, mga_ref, mgr_ref, ya_ref, yr_ref, m_ref):
        ya = _dot(ga_ref[...], wa_ref[...])
        yr = _dot(gr_ref[...], wr_ref[...])
        ya_ref[...] = ya.astype(BF16)
        yr_ref[...] = yr.astype(BF16)
        m_ref[...] = (_sigmoid(mga_ref[...].astype(F32)) * ya + _sigmoid(mgr_ref[...].astype(F32)) * yr).astype(BF16)

    nj = D_MODEL // tn
    act = pl.BlockSpec((tm, D_MODEL), lambda i, j: (i, 0))
    wgt = pl.BlockSpec((D_MODEL, tn), lambda i, j: (0, j))
    out = pl.BlockSpec((tm, tn), lambda i, j: (i, j))
    return pl.pallas_call(
        body, name="branch_merge", grid=(t // tm, nj),
        in_specs=[act, act, wgt, wgt, pl.BlockSpec((tm, tn), lambda i, j: (i, 3 * nj + j)),
                  pl.BlockSpec((tm, tn), lambda i, j: (i, 4 * nj + j))],
        out_specs=[out, out, out],
        out_shape=[jax.ShapeDtypeStruct((t, D_MODEL), BF16), jax.ShapeDtypeStruct((t, D_MODEL), BF16),
                   jax.ShapeDtypeStruct((t, D_MODEL), BF16)],
        compiler_params=_cparams(("parallel", "parallel")),
    )(ga, gr, wa, wr, zrest, zrest)


def _out_loss(m, wout, x2, tgt2, wpost):
    t = m.shape[0]
    tm = min(512, t)

    def body(m_ref, w_ref, x_ref, t_ref, wp_ref, dy_ref, do_ref, acc_ref):
        @pl.when(pl.program_id(0) == 0)
        def _():
            acc_ref[...] = jnp.zeros_like(acc_ref)

        o = _dot(m_ref[...], w_ref[...])
        r2 = lax.rsqrt(jnp.mean(o * o, axis=-1, keepdims=True) + NORM_EPS)
        n = o * r2
        wp = wp_ref[...]
        err = (x_ref[...] + n * wp) - t_ref[...]
        dy = err * (1.0 / D_MODEL)
        dn = dy * wp
        do = r2 * (dn - n * jnp.mean(dn * n, axis=-1, keepdims=True))
        dy_ref[...] = dy
        do_ref[...] = do.astype(BF16)
        acc_ref[0:1, :] += jnp.sum(dy * n, axis=0, keepdims=True)
        acc_ref[1:2, :] += jnp.sum(err * err, axis=0, keepdims=True)

    row = pl.BlockSpec((tm, D_MODEL), lambda i: (i, 0))
    return pl.pallas_call(
        body, name="out_loss", grid=(t // tm,),
        in_specs=[row, pl.BlockSpec((D_MODEL, D_MODEL), lambda i: (0, 0)), row, row,
                  pl.BlockSpec((1, D_MODEL), lambda i: (0, 0))],
        out_specs=[row, row, pl.BlockSpec((8, D_MODEL), lambda i: (0, 0))],
        out_shape=[jax.ShapeDtypeStruct((t, D_MODEL), F32), jax.ShapeDtypeStruct((t, D_MODEL), BF16),
                   jax.ShapeDtypeStruct((8, D_MODEL), F32)],
        compiler_params=_cparams(("arbitrary",)),
    )(m, wout, x2, tgt2, wpost)


def _merge_bwd(do, wout, zrest, ya, yr):
    t = do.shape[0]
    tm = min(512, t)
    tn = D_MODEL
    nj = D_MODEL // tn

    def body(do_ref, w_ref, mga_ref, mgr_ref, ya_ref, yr_ref, dya_ref, dyr_ref, dmga_ref, dmgr_ref):
        dm = _dot_nt(do_ref[...], w_ref[...])
        sa = _sigmoid(mga_ref[...].astype(F32))
        sr = _sigmoid(mgr_ref[...].astype(F32))
        dya_ref[...] = (dm * sa).astype(BF16)
        dyr_ref[...] = (dm * sr).astype(BF16)
        dmga_ref[...] = (dm * ya_ref[...].astype(F32) * (sa * (1.0 - sa))).astype(BF16)
        dmgr_ref[...] = (dm * yr_ref[...].astype(F32) * (sr * (1.0 - sr))).astype(BF16)

    out = pl.BlockSpec((tm, tn), lambda i, j: (i, j))
    bf = jax.ShapeDtypeStruct((t, D_MODEL), BF16)
    return pl.pallas_call(
        body, name="merge_bwd", grid=(t // tm, nj),
        in_specs=[pl.BlockSpec((tm, D_MODEL), lambda i, j: (i, 0)), pl.BlockSpec((tn, D_MODEL), lambda i, j: (j, 0)),
                  pl.BlockSpec((tm, tn), lambda i, j: (i, 3 * nj + j)),
                  pl.BlockSpec((tm, tn), lambda i, j: (i, 4 * nj + j)), out, out],
        out_specs=[out, out, out, out],
        out_shape=[bf, bf, bf, bf],
        compiler_params=_cparams(("parallel", "parallel")),
    )(do, wout, zrest, zrest, ya, yr)


def _branch_bwd(dya, dyr, wa, wr, zrest, yatt, ylru):
    t = dya.shape[0]
    tm = min(512, t)
    tn = D_MODEL
    nj = D_MODEL // tn

    def body(dya_ref, dyr_ref, wa_ref, wr_ref, ga_ref, gr_ref, ya_ref, yl_ref,
             dyatt_ref, dga_ref, dyl_ref, dgr_ref):
        dga = _dot_nt(dya_ref[...], wa_ref[...])
        dgr = _dot_nt(dyr_ref[...], wr_ref[...])
        g = ga_ref[...].astype(F32)
        sg = _sigmoid(g)
        dyatt_ref[...] = (dga * (g * sg)).astype(BF16)
        dga_ref[...] = (dga * ya_ref[...] * (sg * (1.0 + g * (1.0 - sg)))).astype(BF16)
        g = gr_ref[...].astype(F32)
        sg = _sigmoid(g)
        dyl_ref[...] = dgr * (g * sg)
        dgr_ref[...] = (dgr * yl_ref[...] * (sg * (1.0 + g * (1.0 - sg)))).astype(BF16)

    act = pl.BlockSpec((tm, D_MODEL), lambda i, j: (i, 0))
    wgt = pl.BlockSpec((tn, D_MODEL), lambda i, j: (j, 0))
    out = pl.BlockSpec((tm, tn), lambda i, j: (i, j))
    bf = jax.ShapeDtypeStruct((t, D_MODEL), BF16)
    return pl.pallas_call(
        body, name="branch_bwd", grid=(t // tm, nj),
        in_specs=[act, act, wgt, wgt, pl.BlockSpec((tm, tn), lambda i, j: (i, j)),
                  pl.BlockSpec((tm, tn), lambda i, j: (i, 2 * nj + j)), out, out],
        out_specs=[out, out, out, out],
        out_shape=[bf, bf, jax.ShapeDtypeStruct((t, D_MODEL), F32), bf],
        compiler_params=_cparams(("parallel", "parallel")),
    )(dya, dyr, wa, wr, zrest, zrest, yatt, ylru)


def _dh_partial(parts, after, name):
    t = parts[0][0].shape[0]
    tm = min(256, t)
    np_ = len(parts)

    def body(*refs):
        o_ref = refs[-1]
        acc = _dot(refs[0][...], refs[np_][...])
        for p in range(1, np_):
            acc = acc + _dot(refs[p][...], refs[np_ + p][...])
        o_ref[...] = acc

    in_specs = [pl.BlockSpec((tm, dz.shape[1]), lambda i: (i, 0)) for dz, _ in parts]
    in_specs += [pl.BlockSpec(w.shape, lambda i: (0, 0)) for _, w in parts]
    in_specs += [pl.BlockSpec(after.shape, lambda i: (0, 0))]
    return pl.pallas_call(
        body, name=name, grid=(t // tm,),
        in_specs=in_specs,
        out_specs=pl.BlockSpec((tm, D_MODEL), lambda i: (i, 0)),
        out_shape=jax.ShapeDtypeStruct((t, D_MODEL), F32),
        compiler_params=_cparams(("parallel",), vmem_mb=48),
    )(*[dz for dz, _ in parts], *[w for _, w in parts], after)


def _dh_final(parts, acc_in, x2, dy, wpre):
    t = x2.shape[0]
    tm = min(256, t)
    np_ = len(parts)

    def body(*refs):
        acc_ref, x_ref, dy_ref, w_ref = refs[2 * np_:2 * np_ + 4]
        gx_ref, pw_ref = refs[2 * np_ + 4:]

        @pl.when(pl.program_id(0) == 0)
        def _():
            pw_ref[...] = jnp.zeros_like(pw_ref)

        dh = acc_ref[...]
        for p in range(np_):
            dh = dh + _dot(refs[p][...], refs[np_ + p][...])
        x = x_ref[...]
        r = lax.rsqrt(jnp.mean(x * x, axis=-1, keepdims=True) + NORM_EPS)
        xn = x * r
        dxn = dh * w_ref[...]
        gx_ref[...] = r * (dxn - xn * jnp.mean(dxn * xn, axis=-1, keepdims=True)) + dy_ref[...]
        pw_ref[0:1, :] += jnp.sum(dh * xn, axis=0, keepdims=True)

    row = pl.BlockSpec((tm, D_MODEL), lambda i: (i, 0))
    in_specs = [pl.BlockSpec((tm, dz.shape[1]), lambda i: (i, 0)) for dz, _ in parts]
    in_specs += [pl.BlockSpec(w.shape, lambda i: (0, 0)) for _, w in parts]
    in_specs += [row, row, row, pl.BlockSpec((1, D_MODEL), lambda i: (0, 0))]
    return pl.pallas_call(
        body, name="dh_final", grid=(t // tm,),
        in_specs=in_specs,
        out_specs=[row, pl.BlockSpec((8, D_MODEL), lambda i: (0, 0))],
        out_shape=[jax.ShapeDtypeStruct((t, D_MODEL), F32), jax.ShapeDtypeStruct((8, D_MODEL), F32)],
        compiler_params=_cparams(("arbitrary",), vmem_mb=48),
    )(*[dz for dz, _ in parts], *[w for _, w in parts], acc_in, x2, dy, wpre)


def _adamw(w, g, m, v):
    m = ADAM_B1 * m + (1.0 - ADAM_B1) * g
    v = ADAM_B2 * v + (1.0 - ADAM_B2) * (g * g)
    m_hat = m / (1.0 - ADAM_B1 ** ADAM_STEP)
    v_hat = v / (1.0 - ADAM_B2 ** ADAM_STEP)
    delta = -ADAM_LR * (m_hat / (jnp.sqrt(v_hat) + ADAM_EPS) + ADAM_WD * w)
    return delta, m, v


def _reduce_adamw(own, parts, place, w, m, v, name):
    r, c = w.shape
    blk, nblk, at = _blocks_2d(r, c)

    def body(place_ref, own_ref, p_ref, w_ref, m_ref, v_ref, g_ref, d_ref, nm_ref, nv_ref):
        mine = place_ref[1]
        own_blk = own_ref[...]
        g = jnp.where(mine == 0, own_blk, p_ref[0].astype(F32))
        for j in range(1, N_CHIPS):
            g = g + jnp.where(mine == j, own_blk, p_ref[j].astype(F32))
        d, nm, nv = _adamw(w_ref[...], g, m_ref[...], v_ref[...])
        g_ref[...] = g
        d_ref[...] = d
        nm_ref[...] = nm
        nv_ref[...] = nv

    row = pl.BlockSpec(blk, lambda i, pr: at(i))
    sh = jax.ShapeDtypeStruct((r, c), F32)
    grid_spec = pltpu.PrefetchScalarGridSpec(
        num_scalar_prefetch=1, grid=(nblk,),
        in_specs=[row, pl.BlockSpec((N_CHIPS,) + blk, lambda i, pr: (0,) + at(i)), row, row, row],
        out_specs=[row, row, row, row])
    return pl.pallas_call(
        body, name=name, grid_spec=grid_spec, out_shape=[sh, sh, sh, sh],
        compiler_params=_cparams(("parallel",)),
    )(place, own, parts, w, m, v)


def _reduce_adamw_stacked(own, parts, place, triples, name):
    n = len(triples)
    _, r, c = triples[0][0].shape

    def body(place_ref, own_ref, p_ref, *refs):
        ins, outs = refs[:3 * n], refs[3 * n:]
        mine = place_ref[1]
        for i in range(n):
            rows = slice(i * r, (i + 1) * r)
            own_blk = own_ref[rows, :]
            g = jnp.where(mine == 0, own_blk, p_ref[0, rows, :].astype(F32))
            for j in range(1, N_CHIPS):
                g = g + jnp.where(mine == j, own_blk, p_ref[j, rows, :].astype(F32))
            d, nm, nv = _adamw(ins[3 * i][0], g, ins[3 * i + 1][0], ins[3 * i + 2][0])
            for k, val in enumerate((g, d, nm, nv)):
                outs[4 * i + k][0] = val

    whole = lambda shape: pl.BlockSpec(shape, lambda i, pr: (0,) * len(shape))
    grid_spec = pltpu.PrefetchScalarGridSpec(
        num_scalar_prefetch=1, grid=(1,),
        in_specs=[whole(own.shape), whole(parts.shape)] + [whole((1, r, c))] * (3 * n),
        out_specs=[whole((1, r, c))] * (4 * n))
    res = pl.pallas_call(
        body, name=name, grid_spec=grid_spec,
        out_shape=[jax.ShapeDtypeStruct((1, r, c), F32)] * (4 * n),
        compiler_params=_cparams(("arbitrary",)),
    )(place, own, parts, *[a for t3 in triples for a in t3])
    return [res[4 * i:4 * i + 4] for i in range(n)]


def _interleave_qkv(a):
    lead = a.shape[:-1]
    return a.reshape(lead + (3, HEAD_PAIRS, LANES)).swapaxes(-3, -2).reshape(lead + (3 * D_MODEL,))


def _deinterleave_qkv(a):
    lead = a.shape[:-1]
    return a.reshape(lead + (HEAD_PAIRS, 3, LANES)).swapaxes(-3, -2).reshape(lead + (3 * D_MODEL,))


def _interleave_rows(a):
    return a.reshape(3, HEAD_PAIRS, LANES, a.shape[1]).swapaxes(0, 1).reshape(a.shape)


def _deinterleave_rows(a):
    return a.reshape(HEAD_PAIRS, 3, LANES, a.shape[1]).swapaxes(0, 1).reshape(a.shape)


def _pack_small(pre, conv_b, rg_ba, rg_bx, lam, post, loss_row, b_in, conv_w_full, rg_wa, rg_wx):
    z = jnp.zeros((1, D_MODEL), F32)
    b_used = jnp.concatenate([b_in[:, 0:3 * D_MODEL], b_in[:, 3 * D_MODEL + HEADS:IN_TOTAL]], axis=1)
    b_f = jnp.pad(b_in[:, 3 * D_MODEL:3 * D_MODEL + HEADS], ((0, 0), (0, D_MODEL - HEADS)))
    return jnp.concatenate([
        pre, conv_b, rg_ba, rg_bx, lam, post, loss_row, z,
        b_used.reshape(9, D_MODEL), b_f, conv_w_full, z, z,
        rg_wa.reshape(64, D_MODEL), rg_wx.reshape(64, D_MODEL)], axis=0)


def _unpack_small(p):
    b_used = p[8:17].reshape(1, 9 * D_MODEL)
    b_in = jnp.concatenate([b_used[:, 0:3 * D_MODEL], p[17:18, 0:HEADS], b_used[:, 3 * D_MODEL:]], axis=1)
    return dict(pre_norm_w=p[0:1], conv_b=p[1:2], rg_ba=p[2:3], rg_bx=p[3:4], rg_lambda=p[4:5],
                post_norm_w=p[5:6], loss_row=p[6:7], b_in=b_in, conv_w_full=p[18:22],
                rg_wa=p[24:88].reshape(1, 16, 64, 64), rg_wx=p[88:152].reshape(1, 16, 64, 64))


def _reduce_small(parts, first, w, m, v, vectors):
    nvec = len(vectors)

    def body(p_ref, f_ref, w_ref, m_ref, v_ref, *refs):
        ins, outs = refs[:3 * nvec], refs[3 * nvec:]
        g = p_ref[0]
        g0 = f_ref[0, 0:1, :]
        for j in range(1, N_DEV):
            g = g + p_ref[j]
            g0 = g0 + f_ref[j, 0:1, :]
        d, nm, nv = _adamw(w_ref[...], g, m_ref[...], v_ref[...])
        for k, val in enumerate((g, d, nm, nv)):
            outs[k][...] = val
        for i in range(nvec):
            gi = g0 if i == 0 else g[i:i + 1, :]
            di, nmi, nvi = _adamw(ins[3 * i][...], gi, ins[3 * i + 1][...], ins[3 * i + 2][...])
            for k, val in enumerate((gi, di, nmi, nvi)):
                outs[4 + 4 * i + k][...] = val
        outs[-1][...] = jnp.zeros((8, LANES), F32) + (0.5 / D_MODEL) * jnp.sum(g[LOSS_ROW:LOSS_ROW + 1, :])

    sh = jax.ShapeDtypeStruct((SMALL_ROWS, D_MODEL), F32)
    vec = jax.ShapeDtypeStruct((1, D_MODEL), F32)
    res = pl.pallas_call(
        body, name="reduce_small",
        out_shape=[sh, sh, sh, sh] + [vec] * (4 * nvec) + [jax.ShapeDtypeStruct((8, LANES), F32)],
    )(parts, first, w, m, v, *[a for t3 in vectors for a in t3])
    return res[:4], [res[4 + 4 * i:8 + 4 * i] for i in range(nvec)], res[-1]


def kernel(x, pre_norm_w, w_in, b_in, conv_w, conv_b, rg_wa, rg_ba, rg_wx, rg_bx, rg_lambda, w_branch_a, w_branch_r, w_out, post_norm_w, loss_target, m_pre_norm_w, m_w_in, m_b_in, m_conv_w, m_conv_b, m_rg_wa, m_rg_ba, m_rg_wx, m_rg_bx, m_rg_lambda, m_w_branch_a, m_w_branch_r, m_w_out, m_post_norm_w, v_pre_norm_w, v_w_in, v_b_in, v_conv_w, v_conv_b, v_rg_wa, v_rg_ba, v_rg_wx, v_rg_bx, v_rg_lambda, v_w_branch_a, v_w_branch_r, v_w_out, v_post_norm_w):
    b, s, _ = x.shape
    t = b * s
    me = 4 * lax.axis_index("x") + 2 * lax.axis_index("y") + lax.axis_index("c")
    shard_rows = D_MODEL // N_DEV

    place = jnp.stack([lax.axis_index("c"), 2 * lax.axis_index("x") + lax.axis_index("y")]).astype(jnp.int32)
    w_in_all = _gather(w_in[0].T.astype(BF16), "gather_w_in")
    wt_full = w_in_all.reshape(IN_TOTAL, D_MODEL)
    conv_terms = jnp.concatenate(_split3(conv_w[0]), axis=0)
    conv_pad = jnp.pad(conv_terms, ((0, 16 - 3 * CONV_W), (0, D_MODEL - LANES)))
    sq_stack = jnp.concatenate([w_branch_a[0].astype(BF16), w_branch_r[0].astype(BF16), w_out[0].astype(BF16),
                                conv_pad], axis=0)
    sq_sems, sq_src, sq_land, sq_token = _gather_start(sq_stack, w_in_all, "gather_w_sq_start")

    w_qkv = _interleave_rows(wt_full[0:3 * D_MODEL])
    w_f = jnp.pad(wt_full[3 * D_MODEL:3 * D_MODEL + HEADS], ((0, LANES - HEADS), (0, 0)))
    w_rest = wt_full[3 * D_MODEL + HEADS:IN_USED]
    b_qkv = _interleave_qkv(b_in[:, 0:3 * D_MODEL]) + sq_token[0, 0]
    b_f = jnp.pad(b_in[:, 3 * D_MODEL:3 * D_MODEL + HEADS], ((0, 0), (0, LANES - HEADS)))
    b_rest = b_in[:, 3 * D_MODEL + HEADS:IN_USED]

    def blockdiag(w):
        w2 = w.reshape(N_CBLK, 2, HEAD_DIM, HEAD_DIM)
        zz = jnp.zeros((N_CBLK, HEAD_DIM, HEAD_DIM), w.dtype)
        top = jnp.concatenate([w2[:, 0], zz], axis=2)
        bot = jnp.concatenate([zz, w2[:, 1]], axis=2)
        return jnp.concatenate([top, bot], axis=1).astype(BF16)

    bda, bdx = blockdiag(rg_wa[0]), blockdiag(rg_wx[0])

    x2 = x.reshape(t, D_MODEL)
    tgt2 = loss_target.reshape(t, D_MODEL)
    h = _prenorm(x2, pre_norm_w)
    qkv = _mm_bias(h, w_qkv, b_qkv, BF16, "inproj_qkv")
    zrest = _mm_bias(h, w_rest, b_rest, BF16, "inproj_rest")
    zf = _mm_bias(h, w_f, b_f, F32, "inproj_f")
    qkv3 = qkv.reshape(b, s, 3 * D_MODEL)
    zrest3 = zrest.reshape(b, s, 5 * D_MODEL)
    zf3 = zf.reshape(b, s, LANES)
    cexp3, crow = _fgate_fwd(zf3)
    yatt3, lse, ga3 = _attn_fwd(qkv3, cexp3, crow, zrest3)

    sq_all = _gather_wait(sq_sems, sq_src, sq_land, ga3, "gather_w_sq_wait")
    sq_all = lax.dynamic_update_slice(sq_all, sq_stack[None], (me, 0, 0))
    wa = sq_all[:, 0:shard_rows].reshape(D_MODEL, D_MODEL)
    wr = sq_all[:, shard_rows:2 * shard_rows].reshape(D_MODEL, D_MODEL)
    wo = sq_all[:, 2 * shard_rows:3 * shard_rows].reshape(D_MODEL, D_MODEL)
    conv_all = sq_all[:, 3 * shard_rows:3 * shard_rows + 3 * CONV_W, 0:LANES].astype(F32)
    conv_all = (conv_all[:, 0:CONV_W] + conv_all[:, CONV_W:2 * CONV_W]) + conv_all[:, 2 * CONV_W:3 * CONV_W]
    conv_full = conv_all.transpose(1, 0, 2).reshape(CONV_W, D_MODEL)

    ylru3, gr3 = _rnn_fwd(zrest3, conv_full, conv_b, bda, bdx, rg_ba, rg_bx, rg_lambda)
    ga, gr = ga3.reshape(t, D_MODEL), gr3.reshape(t, D_MODEL)
    ya, yr, mm = _branch_merge(ga, gr, wa, wr, zrest)
    dy, do, acc_out = _out_loss(mm, wo, x2, tgt2, post_norm_w)

    dya, dyr, dz_mga, dz_mgr = _merge_bwd(do, wo, zrest, ya, yr)
    dyatt, dz_ga, dylru, dz_gr = _branch_bwd(dya, dyr, wa, wr, zrest, yatt3.reshape(t, D_MODEL),
                                             ylru3.reshape(t, D_MODEL))
    dz_xr, pvec, dbd = _rnn_bwd(zrest3, ylru3, dylru.reshape(b, s, D_MODEL), conv_full, conv_b, bda, bdx,
                                rg_ba, rg_bx, rg_lambda)
    dqkv3, dc3 = _attn_bwd(qkv3, dyatt.reshape(b, s, D_MODEL), yatt3, lse, crow, cexp3)
    dz_f = _fgate_bwd(dc3, zf3)
    dz_qkv = dqkv3.reshape(t, 3 * D_MODEL)

    dw_qkv, db_qkv = _mm_tn(dz_qkv, h, "dw_qkv")
    dw_f, db_f = _mm_tn(dz_f, h, "dw_f")
    dw_parts, db_parts = [], []
    for nm, dzp in (("ga", dz_ga), ("xr", dz_xr), ("gr", dz_gr), ("mga", dz_mga), ("mgr", dz_mgr)):
        dwp, dbp = _mm_tn(dzp, h, "dw_" + nm)
        dw_parts.append(dwp)
        db_parts.append(dbp[0:1])
    dw_a, _ = _mm_tn(ga, dya, "dw_a")
    dw_r, _ = _mm_tn(gr, dyr, "dw_r")
    dw_o, _ = _mm_tn(mm, do, "dw_o")

    zeros_tail = jnp.zeros((IN_TOTAL - IN_USED, D_MODEL), F32)
    dwt_full = jnp.concatenate([_deinterleave_rows(dw_qkv), dw_f[0:HEADS]] + dw_parts + [zeros_tail], axis=0)
    dw_in_send = dwt_full.reshape(N_CHIPS, 2, W_SHARD, D_MODEL).transpose(1, 0, 2, 3)
    by_dest = lambda a: a.reshape(N_CHIPS, 2, shard_rows, D_MODEL).transpose(1, 0, 2, 3)
    dw_sq_send = jnp.concatenate([by_dest(dw_a), by_dest(dw_r), by_dest(dw_o)], axis=2)

    db_in_full = jnp.concatenate([_deinterleave_qkv(db_qkv[0:1]), db_f[0:1, 0:HEADS]] + db_parts
                                 + [jnp.zeros((1, IN_TOTAL - IN_USED), F32)], axis=1)
    d_rg_wa = jnp.stack([dbd[:, 0, 0:HEAD_DIM, 0:HEAD_DIM], dbd[:, 0, HEAD_DIM:, HEAD_DIM:]], axis=1)
    d_rg_wx = jnp.stack([dbd[:, 1, 0:HEAD_DIM, 0:HEAD_DIM], dbd[:, 1, HEAD_DIM:, HEAD_DIM:]], axis=1)
    small_g = _pack_small(jnp.zeros((1, D_MODEL), F32), pvec[4:5], pvec[5:6], pvec[6:7], pvec[7:8], acc_out[0:1],
                          acc_out[1:2], db_in_full, pvec[0:4], d_rg_wa, d_rg_wx)
    sm_sems, sm_src, sm_land, sm_token = _gather_start(small_g, dw_o, "gather_small_start")

    dw_sq_send = dw_sq_send + sm_token[0, 0]
    sib_in, sib_sq = _swap_with_sibling([dw_in_send, dw_sq_send], "swap_dw")
    chip_in, own_in = _pair_add(dw_in_send, sib_in, place, "pair_add_in")
    chip_sq, own_sq = _pair_add(dw_sq_send, sib_sq, place, "pair_add_sq")
    sems, sent, lands, token = _exchange_chips_start([chip_in, chip_sq], "exchange_dw_start")

    wt = lambda lo: w_rest[lo * D_MODEL:(lo + 1) * D_MODEL]
    dh_a = _dh_partial([(dz_qkv, w_qkv), (dz_f, w_f)], token, "dh_qkv")
    grad_x2, acc_pre = _dh_final(
        [(dz_ga, wt(0)), (dz_xr, wt(1)), (dz_gr, wt(2)), (dz_mga, wt(3)), (dz_mgr, wt(4))],
        dh_a, x2, dy, pre_norm_w)
    pre_sems, pre_src, pre_land, pre_token = _gather_start(acc_pre, grad_x2, "gather_pre_start")
    recv_in, recv_sq = _exchange_chips_wait(sems, sent, lands, pre_token, "exchange_dw_wait")

    g_in, d_in, nm_in, nv_in = [a.T for a in _reduce_adamw(
        own_in, recv_in, place, w_in[0].T, m_w_in[0].T, v_w_in[0].T, "adamw_w_in")]
    sq_out = _reduce_adamw_stacked(
        own_sq, recv_sq, place,
        [(w_branch_a, m_w_branch_a, v_w_branch_a), (w_branch_r, m_w_branch_r, v_w_branch_r),
         (w_out, m_w_out, v_w_out)], "adamw_w_sq")
    pre_all = _gather_wait(pre_sems, pre_src, pre_land, sq_out[2][1], "gather_pre_wait")
    pre_all = lax.dynamic_update_slice(pre_all, acc_pre[None], (me, 0, 0))
    small_all = _gather_wait(sm_sems, sm_src, sm_land, pre_all, "gather_small_wait")
    small_all = lax.dynamic_update_slice(small_all, small_g[None], (me, 0, 0))

    def place_conv(a):
        return lax.dynamic_update_slice(jnp.zeros((CONV_W, D_MODEL), F32), a[0], (0, me * LANES))

    zrow = jnp.zeros((1, D_MODEL), F32)
    vector_names = ["pre_norm_w", "conv_b", "rg_ba", "rg_bx", "rg_lambda", "post_norm_w"]
    vectors = [(pre_norm_w, m_pre_norm_w, v_pre_norm_w), (conv_b, m_conv_b, v_conv_b), (rg_ba, m_rg_ba, v_rg_ba),
               (rg_bx, m_rg_bx, v_rg_bx), (rg_lambda, m_rg_lambda, v_rg_lambda),
               (post_norm_w, m_post_norm_w, v_post_norm_w)]
    small_w = _pack_small(zrow, zrow, zrow, zrow, zrow, zrow, zrow, b_in, place_conv(conv_w), rg_wa[0], rg_wx[0])
    small_m = _pack_small(zrow, zrow, zrow, zrow, zrow, zrow, zrow, m_b_in, place_conv(m_conv_w), m_rg_wa[0],
                          m_rg_wx[0])
    small_v = _pack_small(zrow, zrow, zrow, zrow, zrow, zrow, zrow, v_b_in, place_conv(v_conv_w), v_rg_wa[0],
                          v_rg_wx[0])
    packed, vector_out, loss_tile = _reduce_small(small_all, pre_all, small_w, small_m, small_v, vectors)
    outs_small = [_unpack_small(p) for p in packed]
    loss = loss_tile[0, 0]

    def leaf(kind, name):
        if name == "w_in":
            return (g_in, d_in, nm_in, nv_in)[kind][None]
        if name in ("w_branch_a", "w_branch_r", "w_out"):
            return sq_out[("w_branch_a", "w_branch_r", "w_out").index(name)][kind]
        if name == "conv_w":
            return lax.dynamic_slice(outs_small[kind]["conv_w_full"], (0, me * LANES), (CONV_W, LANES))[None]
        if name in vector_names:
            return vector_out[vector_names.index(name)][kind]
        return outs_small[kind][name]

    names = ["pre_norm_w", "w_in", "b_in", "conv_w", "conv_b", "rg_wa", "rg_ba", "rg_wx", "rg_bx", "rg_lambda",
             "w_branch_a", "w_branch_r", "w_out", "post_norm_w"]
    out = [loss, grad_x2.reshape(b, s, D_MODEL)]
    for kind in range(4):
        out += [leaf(kind, nm) for nm in names]
    return tuple(out)
```

```python
import jax
import jax.numpy as jnp
from jax import lax
from jax.experimental import pallas as pl
from jax.experimental.pallas import tpu as pltpu

F32 = jnp.float32
BF16 = jnp.bfloat16

N_DEV = 8
D_MODEL = 1024
HEADS = 16
HEAD_DIM = 64
HEAD_PAIRS = HEADS // 2
LANES = 128
N_CBLK = D_MODEL // LANES
CONV_W = 4
RG_C = 8.0
NORM_EPS = 1e-6
MASK_VALUE = -1e30
IN_USED = 8208
IN_TOTAL = 9232
W_SHARD = IN_TOTAL // N_DEV

ADAM_LR = 0.001
ADAM_B1 = 0.9
ADAM_B2 = 0.999
ADAM_EPS = 1e-08
ADAM_WD = 0.01
ADAM_STEP = 10

ATT_TILE_FWD = 256
ATT_TILE_BWD = 512
SCAN_TILE = 256
SMALL_ROWS = 152
LOSS_ROW = 6


def _cparams(sem=None, vmem_mb=None):
    kw = {}
    if sem is not None:
        kw["dimension_semantics"] = sem
    if vmem_mb is not None:
        kw["vmem_limit_bytes"] = vmem_mb * 1024 * 1024
    return pltpu.CompilerParams(**kw)


def _sigmoid(x):
    return 1.0 / (1.0 + jnp.exp(-x))


def _softplus(x):
    return jnp.maximum(x, 0.0) + jnp.log1p(jnp.exp(-jnp.abs(x)))


def _one_minus_exp(y, exp_y):
    series = -y * (1.0 + y * (1.0 / 2 + y * (1.0 / 6 + y * (1.0 / 24 + y * (1.0 / 120)))))
    return jnp.where(y > -0.0625, series, 1.0 - exp_y)


def _split3(x):
    hi = x.astype(BF16)
    r1 = x - hi.astype(F32)
    mid = r1.astype(BF16)
    lo = (r1 - mid.astype(F32)).astype(BF16)
    return hi, mid, lo


def _dot(a, b):
    return jnp.dot(a, b, preferred_element_type=F32)


def _dot_nt(a, b):
    return lax.dot_general(a, b, (((1,), (1,)), ((), ())), preferred_element_type=F32)


def _dot_tn(a, b):
    return lax.dot_general(a, b, (((0,), (0,)), ((), ())), preferred_element_type=F32)


def _iota(shape, dim):
    return lax.broadcasted_iota(jnp.int32, shape, dim)


_ANY = pl.BlockSpec(memory_space=pl.ANY)
_MESH = pl.DeviceIdType.MESH
N_CHIPS = 4


def _place():
    x, y, c = lax.axis_index("x"), lax.axis_index("y"), lax.axis_index("c")
    other_chips = [(1 - x, y), (x, 1 - y), (1 - x, 1 - y)]
    return x, y, c, other_chips


def _gather(x_shard, name):
    def body(x_ref, out_ref, send_sems, recv_sems, local_sem):
        x, y, c, chips = _place()
        me, sibling = (x, y, c), (x, y, 1 - c)

        def slot(p):
            return out_ref.at[4 * p[0] + 2 * p[1] + p[2]]

        def copy(k, block, to, src=None):
            return pltpu.make_async_remote_copy(
                src_ref=slot(block) if src is None else src, dst_ref=slot(block),
                send_sem=send_sems.at[k], recv_sem=recv_sems.at[k], device_id=to, device_id_type=_MESH)

        mine = pltpu.make_async_copy(x_ref, slot(me), local_sem)
        mine.start()
        first = [copy(0, me, sibling, src=x_ref)]
        first += [copy(1 + j, me, (*chip, c), src=x_ref) for j, chip in enumerate(chips)]
        for cp in first:
            cp.start()
        passed = [copy(4 + j, (*chip, c), sibling) for j, chip in enumerate(chips)]
        for j, chip in enumerate(chips):
            copy(1 + j, (*chip, c), me).wait_recv()
            passed[j].start()
        copy(0, sibling, me).wait_recv()
        for j, chip in enumerate(chips):
            copy(4 + j, (*chip, 1 - c), me).wait_recv()
        for cp in first + passed:
            cp.wait_send()
        mine.wait()

    return pl.pallas_call(
        body, name=name,
        out_shape=jax.ShapeDtypeStruct((N_DEV,) + tuple(x_shard.shape), x_shard.dtype),
        in_specs=[_ANY], out_specs=_ANY,
        scratch_shapes=[pltpu.SemaphoreType.DMA((7,)), pltpu.SemaphoreType.DMA((7,)), pltpu.SemaphoreType.DMA],
    )(x_shard)


def _swap_with_sibling(srcs, name):
    n = len(srcs)

    def body(*refs):
        src_refs, out_refs = refs[:n], refs[n:2 * n]
        send_sems, recv_sems = refs[2 * n:]
        x, y, c, _ = _place()
        cps = [pltpu.make_async_remote_copy(
            src_ref=src_refs[i].at[1 - c], dst_ref=out_refs[i], send_sem=send_sems.at[i], recv_sem=recv_sems.at[i],
            device_id=(x, y, 1 - c), device_id_type=_MESH) for i in range(n)]
        for cp in cps:
            cp.start()
        for cp in cps:
            cp.wait()

    return pl.pallas_call(
        body, name=name,
        out_shape=[jax.ShapeDtypeStruct(a.shape[1:], a.dtype) for a in srcs],
        in_specs=[_ANY] * n, out_specs=[_ANY] * n,
        scratch_shapes=[pltpu.SemaphoreType.DMA((n,)), pltpu.SemaphoreType.DMA((n,))],
    )(*srcs)


def _blocks_2d(r, c):
    if r % 128 == 0:
        return (128, c), r // 128, lambda i: (i, 0)
    return (r, 256), c // 256, lambda i: (0, i)


def _pair_add(src, recv, place, name):
    _, _, r, c = src.shape
    blk, nblk, at = _blocks_2d(r, c)

    def body(place_ref, a_ref, b_ref, q16_ref, own_ref):
        q = a_ref[...] + b_ref[...]
        q16_ref[...] = q.astype(BF16)

        @pl.when(pl.program_id(1) == place_ref[1])
        def _():
            own_ref[...] = q

    grid_spec = pltpu.PrefetchScalarGridSpec(
        num_scalar_prefetch=1, grid=(nblk, N_CHIPS),
        in_specs=[pl.BlockSpec((None, None) + blk, lambda i, j, pr: (pr[0], j) + at(i)),
                  pl.BlockSpec((None,) + blk, lambda i, j, pr: (j,) + at(i))],
        out_specs=[pl.BlockSpec((None,) + blk, lambda i, j, pr: (j,) + at(i)),
                   pl.BlockSpec(blk, lambda i, j, pr: at(i))])
    return pl.pallas_call(
        body, name=name, grid_spec=grid_spec,
        out_shape=[jax.ShapeDtypeStruct((N_CHIPS, r, c), BF16), jax.ShapeDtypeStruct((r, c), F32)],
        compiler_params=_cparams(("parallel", "arbitrary")),
    )(place, src, recv)


_HBM = pl.BlockSpec(memory_space=pltpu.HBM)
_SEM = pl.BlockSpec(memory_space=pltpu.SEMAPHORE)
_DATAFLOW = pltpu.SideEffectType.DATAFLOW_SIDE_EFFECTING


def _chip_copy(src_ref, land_ref, send_sem, recv_sem, k, chips, c, land):
    chip = chips[k]
    return pltpu.make_async_remote_copy(
        src_ref=src_ref.at[2 * chip[0] + chip[1]], dst_ref=land_ref.at[land],
        send_sem=send_sem, recv_sem=recv_sem, device_id=(*chip, c), device_id_type=_MESH)


def _exchange_chips_start(srcs, name):
    n = len(srcs)
    ncp = 3 * n

    def body(*refs):
        src_refs, land_refs = refs[:n], refs[n:2 * n]
        sems = refs[4 * n:4 * n + 2 * ncp]
        token = refs[-1]
        x, y, c, chips = _place()
        for i in range(n):
            for k in range(3):
                j = 3 * i + k
                _chip_copy(src_refs[i], land_refs[i], sems[j], sems[ncp + j], k, chips, c, 2 * x + y).start()
        token[...] = jnp.zeros_like(token)

    hbm = [pltpu.HBM(a.shape, a.dtype) for a in srcs]
    lands = [pltpu.with_memory_space_constraint(lax.empty(a.shape, a.dtype), pltpu.HBM) for a in srcs]
    res = pl.pallas_call(
        body, name=name,
        out_shape=(*hbm, *hbm, *([pltpu.SemaphoreType.DMA(())] * (2 * ncp)), jax.ShapeDtypeStruct((8, LANES), F32)),
        in_specs=[_HBM] * (2 * n),
        out_specs=(*([_HBM] * (2 * n)), *([_SEM] * (2 * ncp)), pl.BlockSpec(memory_space=pltpu.VMEM)),
        input_output_aliases={i: i for i in range(2 * n)},
        compiler_params=pltpu.CompilerParams(has_side_effects=_DATAFLOW),
    )(*[pltpu.with_memory_space_constraint(a, pltpu.HBM) for a in srcs], *lands)
    return list(res[2 * n:2 * n + 2 * ncp]), list(res[:n]), list(res[n:2 * n]), res[-1]


def _exchange_chips_wait(sems, srcs, lands, after, name):
    n = len(srcs)
    ncp = 3 * n

    def body(*refs):
        src_refs, land_refs = refs[:n], refs[n:2 * n]
        sem_refs = refs[2 * n:2 * n + 2 * ncp]
        x, y, c, chips = _place()
        for i in range(n):
            for k in range(3):
                j = 3 * i + k
                cp = _chip_copy(src_refs[i], land_refs[i], sem_refs[j], sem_refs[ncp + j], k, chips, c,
                                2 * chips[k][0] + chips[k][1])
                cp.wait_send()
                cp.wait_recv()

    hbm = [pltpu.HBM(a.shape, a.dtype) for a in srcs]
    res = pl.pallas_call(
        body, name=name, out_shape=(*hbm, *hbm),
        in_specs=[_HBM] * (2 * n) + [_SEM] * (2 * ncp) + [_ANY], out_specs=tuple([_HBM] * (2 * n)),
        input_output_aliases={i: i for i in range(2 * n)},
        compiler_params=pltpu.CompilerParams(has_side_effects=_DATAFLOW),
    )(*srcs, *lands, *sems, after)
    return list(res[n:2 * n])


def _swap_start(src, after, name):
    def body(src_ref, land_ref, after_ref, src_thru, land_thru, send_sem, recv_sem, token):
        x, y, c, _ = _place()
        pltpu.make_async_remote_copy(src_ref=src_ref.at[1 - c], dst_ref=land_ref, send_sem=send_sem,
                                     recv_sem=recv_sem, device_id=(x, y, 1 - c), device_id_type=_MESH).start()
        token[...] = jnp.zeros_like(token)

    land = pltpu.with_memory_space_constraint(lax.empty(src.shape[1:], src.dtype), pltpu.HBM)
    res = pl.pallas_call(
        body, name=name,
        out_shape=(pltpu.HBM(src.shape, src.dtype), pltpu.HBM(land.shape, land.dtype),
                   pltpu.SemaphoreType.DMA(()), pltpu.SemaphoreType.DMA(()), jax.ShapeDtypeStruct((8, LANES), F32)),
        in_specs=[_HBM, _HBM, _ANY],
        out_specs=(_HBM, _HBM, _SEM, _SEM, pl.BlockSpec(memory_space=pltpu.VMEM)),
        input_output_aliases={0: 0, 1: 1},
        compiler_params=pltpu.CompilerParams(has_side_effects=_DATAFLOW),
    )(pltpu.with_memory_space_constraint(src, pltpu.HBM), land, after)
    return [res[2], res[3]], res[0], res[1], res[-1]


def _swap_wait(sems, src, land, after, name):
    def body(src_ref, land_ref, send_sem, recv_sem, after_ref, src_out, land_out):
        x, y, c, _ = _place()
        cp = pltpu.make_async_remote_copy(src_ref=src_ref.at[1 - c], dst_ref=land_ref, send_sem=send_sem,
                                          recv_sem=recv_sem, device_id=(x, y, 1 - c), device_id_type=_MESH)
        cp.wait_send()
        cp.wait_recv()

    res = pl.pallas_call(
        body, name=name, out_shape=(pltpu.HBM(src.shape, src.dtype), pltpu.HBM(land.shape, land.dtype)),
        in_specs=[_HBM, _HBM, _SEM, _SEM, _ANY], out_specs=(_HBM, _HBM),
        input_output_aliases={0: 0, 1: 1},
        compiler_params=pltpu.CompilerParams(has_side_effects=_DATAFLOW),
    )(src, land, *sems, after)
    return res[0], res[1]


def _peer_copy(src_ref, land_ref, send_sem, recv_sem, k, place, land):
    x, y, c = place
    peer = (1 - x if k & 4 else x, 1 - y if k & 2 else y, 1 - c if k & 1 else c)
    return pltpu.make_async_remote_copy(
        src_ref=src_ref, dst_ref=land_ref.at[land], send_sem=send_sem, recv_sem=recv_sem,
        device_id=peer, device_id_type=_MESH)


def _gather_start(x_shard, after, name):
    npeer = N_DEV - 1

    def body(x_ref, land_ref, after_ref, x_thru, land_thru, *rest):
        sems, token = rest[:2 * npeer], rest[-1]
        x, y, c, _ = _place()
        for k in range(1, N_DEV):
            _peer_copy(x_ref, land_ref, sems[k - 1], sems[npeer + k - 1], k, (x, y, c), 4 * x + 2 * y + c).start()
        token[...] = jnp.zeros_like(token)

    land = pltpu.with_memory_space_constraint(lax.empty((N_DEV,) + tuple(x_shard.shape), x_shard.dtype), pltpu.HBM)
    res = pl.pallas_call(
        body, name=name,
        out_shape=(pltpu.HBM(x_shard.shape, x_shard.dtype), pltpu.HBM(land.shape, land.dtype),
                   *([pltpu.SemaphoreType.DMA(())] * (2 * npeer)), jax.ShapeDtypeStruct((8, LANES), F32)),
        in_specs=[_HBM, _HBM, _ANY],
        out_specs=(_HBM, _HBM, *([_SEM] * (2 * npeer)), pl.BlockSpec(memory_space=pltpu.VMEM)),
        input_output_aliases={0: 0, 1: 1},
        compiler_params=pltpu.CompilerParams(has_side_effects=_DATAFLOW),
    )(pltpu.with_memory_space_constraint(x_shard, pltpu.HBM), land, after)
    return list(res[2:2 + 2 * npeer]), res[0], res[1], res[-1]


def _gather_wait(sems, src, land, after, name):
    npeer = N_DEV - 1

    def body(x_ref, land_ref, *rest):
        sem_refs = rest[:2 * npeer]
        x, y, c, _ = _place()
        for k in range(1, N_DEV):
            peer_index = (4 * x + 2 * y + c) ^ k
            cp = _peer_copy(x_ref, land_ref, sem_refs[k - 1], sem_refs[npeer + k - 1], k, (x, y, c), peer_index)
            cp.wait_send()
            cp.wait_recv()

    res = pl.pallas_call(
        body, name=name, out_shape=(pltpu.HBM(src.shape, src.dtype), pltpu.HBM(land.shape, land.dtype)),
        in_specs=[_HBM, _HBM] + [_SEM] * (2 * npeer) + [_ANY], out_specs=(_HBM, _HBM),
        input_output_aliases={0: 0, 1: 1},
        compiler_params=pltpu.CompilerParams(has_side_effects=_DATAFLOW),
    )(src, land, *sems, after)
    return res[1]


def _prenorm_inproj(x2, w, wt_qkv, b_qkv, wt_f, b_f):
    t = x2.shape[0]
    tm = min(512, t)
    n = wt_qkv.shape[0]
    tn = D_MODEL

    def body(x_ref, w_ref, wq_ref, bq_ref, wf_ref, bf_ref, h_ref, qkv_ref, zf_ref):
        x = x_ref[...]
        r = lax.rsqrt(jnp.mean(x * x, axis=-1, keepdims=True) + NORM_EPS)
        h = (x * r * w_ref[...]).astype(BF16)
        h_ref[...] = h
        for j in range(n // tn):
            cols = slice(j * tn, (j + 1) * tn)
            qkv_ref[:, cols] = (_dot_nt(h, wq_ref[cols, :]) + bq_ref[:, cols]).astype(BF16)
        zf_ref[...] = _dot_nt(h, wf_ref[...]) + bf_ref[...]

    row = lambda c: pl.BlockSpec((tm, c), lambda i: (i, 0))
    whole = lambda a: pl.BlockSpec(a.shape, lambda i: (0, 0))
    return pl.pallas_call(
        body, name="prenorm_inproj_qkv", grid=(t // tm,),
        in_specs=[row(D_MODEL), whole(w), whole(wt_qkv), whole(b_qkv), whole(wt_f), whole(b_f)],
        out_specs=[row(D_MODEL), row(n), row(LANES)],
        out_shape=[jax.ShapeDtypeStruct((t, D_MODEL), BF16), jax.ShapeDtypeStruct((t, n), BF16),
                   jax.ShapeDtypeStruct((t, LANES), F32)],
        compiler_params=_cparams(("parallel",), vmem_mb=48),
    )(x2, w, wt_qkv, b_qkv, wt_f, b_f)


def _mm_bias(a, bt, bias, out_dtype, name):
    m, k = a.shape
    n = bt.shape[0]
    tm = min(512, m)
    tn = min(1024, n)

    def body(a_ref, bt_ref, bias_ref, o_ref):
        aa = a_ref[...]
        for j in range(n // tn):
            cols = slice(j * tn, (j + 1) * tn)
            o_ref[:, cols] = (_dot_nt(aa, bt_ref[cols, :]) + bias_ref[:, cols]).astype(o_ref.dtype)

    return pl.pallas_call(
        body, name=name, grid=(m // tm,),
        in_specs=[pl.BlockSpec((tm, k), lambda i: (i, 0)), pl.BlockSpec((n, k), lambda i: (0, 0)),
                  pl.BlockSpec((1, n), lambda i: (0, 0))],
        out_specs=pl.BlockSpec((tm, n), lambda i: (i, 0)),
        out_shape=jax.ShapeDtypeStruct((m, n), out_dtype),
        compiler_params=_cparams(("parallel",), vmem_mb=48),
    )(a, bt, bias)


def _mm_tn(a, b, name, after=None):
    t, m = a.shape
    n = b.shape[1]
    tm = min(1024, m)
    tk = min(2048, t)
    deps = [] if after is None else [after]

    def body(a_ref, b_ref, *refs):
        o_ref, s_ref = refs[len(deps):]
        kk = pl.program_id(1)

        @pl.when(kk == 0)
        def _():
            o_ref[...] = jnp.zeros_like(o_ref)
            s_ref[...] = jnp.zeros_like(s_ref)

        aa = a_ref[...]
        o_ref[...] += _dot_tn(aa, b_ref[...])
        s_ref[0:1, :] += jnp.sum(aa.astype(F32), axis=0, keepdims=True)

    return pl.pallas_call(
        body, name=name, grid=(m // tm, t // tk),
        in_specs=[pl.BlockSpec((tk, tm), lambda i, kk: (kk, i)), pl.BlockSpec((tk, n), lambda i, kk: (kk, 0))]
        + [pl.BlockSpec(d.shape, lambda i, kk: (0, 0)) for d in deps],
        out_specs=[pl.BlockSpec((tm, n), lambda i, kk: (i, 0)), pl.BlockSpec((8, tm), lambda i, kk: (0, i))],
        out_shape=[jax.ShapeDtypeStruct((m, n), F32), jax.ShapeDtypeStruct((8, m), F32)],
        compiler_params=_cparams(("parallel", "arbitrary"), vmem_mb=48),
    )(a, b, *deps)


def _fgate_fwd(zf3):
    b, s, _ = zf3.shape
    tb = SCAN_TILE
    nb = s // tb

    def body(z_ref, cexp_ref, crow_ref):
        tri = (_iota((tb, tb), 1) <= _iota((tb, tb), 0)).astype(BF16)
        expand = ((_iota((LANES, D_MODEL), 1) >> 6) == _iota((LANES, D_MODEL), 0)).astype(BF16)
        carry = jnp.zeros((1, LANES), F32)
        for i in range(nb):
            rows = slice(i * tb, (i + 1) * tb)
            z = z_ref[rows, :]
            lf = jnp.minimum(z, 0.0) - jnp.log1p(jnp.exp(-jnp.abs(z)))
            cb = sum(_dot(tri, part) for part in _split3(lf)) + carry
            carry = cb[tb - 1:tb, :]
            cexp_ref[rows, :] = sum(_dot(part, expand) for part in _split3(cb))
            crow_ref[:, rows] = cb.T[0:HEADS, :]

    return pl.pallas_call(
        body, name="fgate_fwd", grid=(b,),
        in_specs=[pl.BlockSpec((None, s, LANES), lambda i: (i, 0, 0))],
        out_specs=[pl.BlockSpec((None, s, D_MODEL), lambda i: (i, 0, 0)),
                   pl.BlockSpec((None, HEADS, s), lambda i: (i, 0, 0))],
        out_shape=[jax.ShapeDtypeStruct((b, s, D_MODEL), F32), jax.ShapeDtypeStruct((b, HEADS, s), F32)],
        compiler_params=_cparams(("parallel",)),
    )(zf3)


def _fgate_bwd(dc3, zf3):
    b, s, _ = zf3.shape
    tb = SCAN_TILE
    nb = s // tb

    def body(dc_ref, z_ref, o_ref):
        tri = (_iota((tb, tb), 1) >= _iota((tb, tb), 0)).astype(BF16)
        carry = jnp.zeros((1, LANES), F32)
        for i in reversed(range(nb)):
            rows = slice(i * tb, (i + 1) * tb)
            dlf = sum(_dot(tri, part) for part in _split3(dc_ref[rows, :])) + carry
            carry = dlf[0:1, :]
            o_ref[rows, :] = (dlf * _sigmoid(-z_ref[rows, :])).astype(BF16)

    return pl.pallas_call(
        body, name="fgate_bwd", grid=(b,),
        in_specs=[pl.BlockSpec((None, s, LANES), lambda i: (i, 0, 0)),
                  pl.BlockSpec((None, s, LANES), lambda i: (i, 0, 0))],
        out_specs=pl.BlockSpec((s, LANES), lambda i: (i, 0)),
        out_shape=jax.ShapeDtypeStruct((b * s, LANES), BF16),
        compiler_params=_cparams(("parallel",)),
    )(dc3, zf3)


def _spare(hh):
    return HEAD_DIM if hh == 0 else 0


def _put_cols(tile, mine, cols, first):
    lane = _iota((1, LANES), 1)
    out = jnp.where(mine, tile, jnp.zeros((), tile.dtype))
    for j, c in enumerate(cols):
        out = jnp.where(lane == first + j, c, out)
    return out


def _put_rows(tile, mine, rows, first):
    sub = _iota((LANES, 1), 0)
    out = jnp.where(mine, tile, jnp.zeros((), tile.dtype))
    for j, r in enumerate(rows):
        out = jnp.where(sub == first + j, r, out)
    return out


def _transpose_bf16(a):
    return a.astype(F32).T.astype(BF16)


def _attn_fwd(qkv3, cexp3, crow, zrest3):
    b, s, _ = qkv3.shape
    ta = ATT_TILE_FWD
    nq = s // ta
    hd = HEAD_DIM
    crow5 = crow.reshape(b, HEAD_PAIRS, 2, nq, ta)

    def body(qkv_ref, cq_ref, ck_ref, g_ref, y_ref, lse_ref, ga_ref, kt_scr, v_scr):
        lane = _iota((1, LANES), 1)
        sub = _iota((LANES, 1), 0)
        lane_mine = (lane < hd, lane >= hd)
        sub_mine = (sub < hd, sub >= hd)
        causal = _iota((ta, ta), 0) >= _iota((ta, ta), 1)
        one = jnp.ones((), BF16)

        for kj in range(nq):
            rows = slice(kj * ta, (kj + 1) * ta)
            kt = _transpose_bf16(qkv_ref[rows, LANES:2 * LANES])
            v = qkv_ref[rows, 2 * LANES:3 * LANES]
            for hh in range(2):
                ck = list(_split3(-ck_ref[hh, kj:kj + 1, :]))
                kt_scr[hh, kj] = _put_rows(kt, sub_mine[hh], [one, one, one] + ck, _spare(hh))
                v_scr[hh, kj] = _put_cols(v, lane_mine[hh], [one], _spare(hh))

        for qi in range(nq):
            rows = slice(qi * ta, (qi + 1) * ta)
            q = qkv_ref[rows, 0:LANES] * 0.125
            cq = cq_ref[rows, :]
            qh = [_put_cols(q, lane_mine[hh], list(_split3(cq[:, hh * hd:hh * hd + 1])) + [one, one, one], _spare(hh))
                  for hh in range(2)]
            st = [(jnp.full((ta, 1), MASK_VALUE, F32), jnp.zeros((ta, LANES), F32))] * 2
            for kj in range(qi + 1):
                for hh in range(2):
                    m, acc = st[hh]
                    sc = _dot(qh[hh], kt_scr[hh, kj])
                    if kj == qi:
                        sc = jnp.where(causal, sc, MASK_VALUE)
                    mn = jnp.maximum(m, jnp.max(sc, axis=-1, keepdims=True))
                    p = jnp.exp(sc - mn).astype(BF16)
                    st[hh] = (mn, jnp.exp(m - mn) * acc + _dot(p, v_scr[hh, kj]))
            (ma, acca), (mb, accb) = st
            la = acca[:, hd:hd + 1]
            lb = accb[:, 0:1]
            y = jnp.where(lane_mine[0], acca * (1.0 / la), accb * (1.0 / lb))
            lse = jnp.where(lane_mine[0], ma + jnp.log(la), mb + jnp.log(lb)).T
            lse_ref[0, qi:qi + 1, :] = lse[0:1, :]
            lse_ref[1, qi:qi + 1, :] = lse[hd:hd + 1, :]
            y_ref[rows, :] = y
            g = g_ref[rows, :].astype(F32)
            ga_ref[rows, :] = (y * (g * _sigmoid(g))).astype(BF16)

    blk = lambda w: pl.BlockSpec((None, s, w), lambda i, p: (i, 0, p))
    rows5 = pl.BlockSpec((None, None, 2, nq, ta), lambda i, p: (i, p, 0, 0, 0))
    yatt3, lse5, ga3 = pl.pallas_call(
        body, name="attn_fwd", grid=(b, HEAD_PAIRS),
        in_specs=[blk(3 * LANES), blk(LANES), rows5, blk(LANES)],
        out_specs=[blk(LANES), rows5, blk(LANES)],
        out_shape=[jax.ShapeDtypeStruct((b, s, D_MODEL), F32),
                   jax.ShapeDtypeStruct((b, HEAD_PAIRS, 2, nq, ta), F32),
                   jax.ShapeDtypeStruct((b, s, D_MODEL), BF16)],
        scratch_shapes=[pltpu.VMEM((2, nq, LANES, ta), BF16), pltpu.VMEM((2, nq, ta, LANES), BF16)],
        compiler_params=_cparams(("parallel", "parallel")),
    )(qkv3, cexp3, crow5, zrest3)
    return yatt3, lse5.reshape(b, HEADS, s), ga3


def _attn_bwd(qkv3, do3, y3, lse, crow, cexp3):
    b, s, _ = qkv3.shape
    ta = ATT_TILE_BWD
    nq = s // ta
    hd = HEAD_DIM
    lse5 = lse.reshape(b, HEAD_PAIRS, 2, nq, ta)
    crow5 = crow.reshape(b, HEAD_PAIRS, 2, nq, ta)

    def body(qkv_ref, do_ref, y_ref, lse_ref, crow_ref, cexp_ref, dqkv_ref, dc_ref,
             qa_scr, doa_scr, qst_scr, dot_scr, kt_scr, vt_scr, dq_scr, rs_scr):
        pair = pl.program_id(1)
        lane = _iota((1, LANES), 1)
        sub = _iota((LANES, 1), 0)
        lane_mine = (lane < hd, lane >= hd)
        sub_mine = (sub < hd, sub >= hd)
        causal = _iota((ta, ta), 0) >= _iota((ta, ta), 1)
        one = jnp.ones((), BF16)
        zero = jnp.zeros((), BF16)

        @pl.when(pair == 0)
        def _():
            dc_ref[...] = jnp.zeros_like(dc_ref)

        for i in range(nq):
            rows = slice(i * ta, (i + 1) * ta)
            qs = qkv_ref[rows, 0:LANES] * 0.125
            qst = _transpose_bf16(qs)
            kt = _transpose_bf16(qkv_ref[rows, LANES:2 * LANES])
            vt = _transpose_bf16(qkv_ref[rows, 2 * LANES:3 * LANES])
            do = do_ref[rows, :]
            dof = do.astype(F32)
            dot = dof.T.astype(BF16)
            pr = y_ref[rows, :] * dof
            cq = cexp_ref[rows, :]
            lse_c = jnp.where(sub == 0, lse_ref[0, i:i + 1, :],
                              jnp.where(sub == 1, lse_ref[1, i:i + 1, :], 0.0)).T
            for hh in range(2):
                sp = _spare(hh)
                dsum = jnp.sum(jnp.where(lane_mine[hh], pr, 0.0), axis=-1, keepdims=True)
                bias = cq[:, hh * hd:hh * hd + 1] - lse_c[:, hh:hh + 1]
                qa_scr[hh, i] = _put_cols(qs, lane_mine[hh], list(_split3(bias)) + [one, one, one], sp)
                doa_scr[hh, i] = _put_cols(do, lane_mine[hh], list(_split3(-dsum)), sp)
                qst_scr[hh, i] = jnp.where(sub_mine[hh], qst, zero)
                dot_scr[hh, i] = jnp.where(sub_mine[hh], dot, zero)
                ck = list(_split3(-crow_ref[hh, i:i + 1, :]))
                kt_scr[hh, i] = _put_rows(kt, sub_mine[hh], [one, one, one] + ck, sp)
                vt_scr[hh, i] = _put_rows(vt, sub_mine[hh], [one, one, one], sp)
            dq_scr[i] = jnp.zeros((ta, LANES), F32)
            rs_scr[i] = jnp.zeros((ta, LANES), F32)

        for kj in range(nq):
            krows = slice(kj * ta, (kj + 1) * ta)
            k = qkv_ref[krows, LANES:2 * LANES]
            km = (jnp.where(lane_mine[0], k, zero), jnp.where(lane_mine[1], k, zero))
            dkt = jnp.zeros((LANES, ta), F32)
            dvt = jnp.zeros((LANES, ta), F32)
            dcp = [jnp.zeros((8, ta), F32), jnp.zeros((8, ta), F32)]
            for qi in range(kj, nq):
                dq = jnp.zeros((ta, LANES), F32)
                rs = []
                for hh in range(2):
                    sc = _dot(qa_scr[hh, qi], kt_scr[hh, kj])
                    if qi == kj:
                        sc = jnp.where(causal, sc, MASK_VALUE)
                    p = jnp.exp(sc)
                    dsf = p * _dot(doa_scr[hh, qi], vt_scr[hh, kj])
                    dcp[hh] = dcp[hh] + jnp.sum(dsf.reshape(ta // 8, 8, ta), axis=0)
                    rs.append(jnp.sum(dsf, axis=-1, keepdims=True))
                    ds = dsf.astype(BF16)
                    dq = dq + _dot(ds, km[hh])
                    dkt = dkt + _dot(qst_scr[hh, qi], ds)
                    dvt = dvt + _dot(dot_scr[hh, qi], p.astype(BF16))
                dq_scr[qi] += dq
                rs_scr[qi] += jnp.where(lane == 0, rs[0], jnp.where(lane == 1, rs[1], 0.0))
            dqkv_ref[krows, LANES:2 * LANES] = dkt.T.astype(BF16)
            dqkv_ref[krows, 2 * LANES:3 * LANES] = dvt.T.astype(BF16)
            dca = jnp.sum(dcp[0], axis=0, keepdims=True)
            dcb = jnp.sum(dcp[1], axis=0, keepdims=True)
            dcs = jnp.where(sub == 0, dca, jnp.where(sub == 1, dcb, 0.0)).T
            dc_ref[krows, :] += (jnp.where(lane == 2 * pair, -dcs[:, 0:1], 0.0)
                                 + jnp.where(lane == 2 * pair + 1, -dcs[:, 1:2], 0.0))
        for qi in range(nq):
            rows = slice(qi * ta, (qi + 1) * ta)
            dqkv_ref[rows, 0:LANES] = (dq_scr[qi] * 0.125).astype(BF16)
            rq = rs_scr[qi]
            dc_ref[rows, :] += (jnp.where(lane == 2 * pair, rq[:, 0:1], 0.0)
                                + jnp.where(lane == 2 * pair + 1, rq[:, 1:2], 0.0))

    blk = lambda w: pl.BlockSpec((None, s, w), lambda i, p: (i, 0, p))
    rows5 = pl.BlockSpec((None, None, 2, nq, ta), lambda i, p: (i, p, 0, 0, 0))
    by_rows = lambda: pltpu.VMEM((2, nq, ta, LANES), BF16)
    by_cols = lambda: pltpu.VMEM((2, nq, LANES, ta), BF16)
    return pl.pallas_call(
        body, name="attn_bwd", grid=(b, HEAD_PAIRS),
        in_specs=[blk(3 * LANES), blk(LANES), blk(LANES), rows5, rows5, blk(LANES)],
        out_specs=[blk(3 * LANES), pl.BlockSpec((None, s, LANES), lambda i, p: (i, 0, 0))],
        out_shape=[jax.ShapeDtypeStruct((b, s, 3 * D_MODEL), BF16), jax.ShapeDtypeStruct((b, s, LANES), F32)],
        scratch_shapes=[by_rows(), by_rows(), by_cols(), by_cols(), by_cols(), by_cols(),
                        pltpu.VMEM((nq, ta, LANES), F32), pltpu.VMEM((nq, ta, LANES), F32)],
        compiler_params=_cparams(("parallel", "arbitrary")),
    )(qkv3, do3, y3, lse5, crow5, cexp3)


def _shifted(v, ks, rows, s):
    low = rows[0:8, :]
    out = []
    for k in ks:
        r = pltpu.roll(v, k % s, 0)
        if k > 0:
            out.append(jnp.concatenate([jnp.where(low >= k, r[0:8, :], 0.0), r[8:, :]], axis=0))
        else:
            out.append(jnp.concatenate([r[:s - 8, :], jnp.where(low < 8 + k, r[s - 8:, :], 0.0)], axis=0))
    return out


def _rnn_common(xr, cw_ref, cb_ref, bda_ref, bdx_ref, ba_ref, bx_ref, lam_ref, s):
    rows = _iota((s, LANES), 0)
    x1, x2, x3 = _shifted(xr, (1, 2, 3), rows, s)
    xc = cb_ref[...] + cw_ref[0:1, :] * x3
    xc = xc + cw_ref[1:2, :] * x2
    xc = xc + cw_ref[2:3, :] * x1
    xc = xc + cw_ref[3:4, :] * xr
    xcb = xc.astype(BF16)
    r = _sigmoid(_dot(xcb, bda_ref[...]) + ba_ref[...])
    i = _sigmoid(_dot(xcb, bdx_ref[...]) + bx_ref[...])
    sp = _softplus(-lam_ref[...])
    log_a = (-RG_C * r) * sp
    a = jnp.exp(log_a)
    a2 = a * a
    sq = jnp.sqrt(jnp.maximum(_one_minus_exp(log_a + log_a, a2), 0.0))
    return rows, (x1, x2, x3), xc, xcb, r, i, sp, a, a2, sq


def _scan_down(a, u, rows, s, s1, s2):
    low = rows & 7
    for sh in (1, 2, 4):
        keep = low >= sh
        u = u + a * jnp.where(keep, pltpu.roll(u, sh, 0), 0.0)
        a = a * jnp.where(keep, pltpu.roll(a, sh, 0), 1.0)
    ng = s // 8
    s1[...] = a
    s2[...] = u
    at = s1[pl.ds(7, ng, stride=8), :]
    ut = s2[pl.ds(7, ng, stride=8), :]
    grow = _iota((ng, LANES), 0)
    sh = 1
    while sh < ng:
        keep = grow >= sh
        ut = ut + at * jnp.where(keep, pltpu.roll(ut, sh, 0), 0.0)
        if sh * 2 < ng:
            at = at * jnp.where(keep, pltpu.roll(at, sh, 0), 1.0)
        sh *= 2
    h_in = jnp.where(grow >= 1, pltpu.roll(ut, 1, 0), 0.0)
    for k in range(8):
        s1[pl.ds(k, ng, stride=8), :] = h_in
    return u + a * s1[...]


def _scan_up(a, g, rows, s, s1, s2):
    low = rows & 7
    for sh in (1, 2, 4):
        keep = low < 8 - sh
        g = g + a * jnp.where(keep, pltpu.roll(g, s - sh, 0), 0.0)
        a = a * jnp.where(keep, pltpu.roll(a, s - sh, 0), 1.0)
    ng = s // 8
    s1[...] = a
    s2[...] = g
    at = s1[pl.ds(0, ng, stride=8), :]
    gt = s2[pl.ds(0, ng, stride=8), :]
    grow = _iota((ng, LANES), 0)
    sh = 1
    while sh < ng:
        keep = grow < ng - sh
        gt = gt + at * jnp.where(keep, pltpu.roll(gt, ng - sh, 0), 0.0)
        if sh * 2 < ng:
            at = at * jnp.where(keep, pltpu.roll(at, ng - sh, 0), 1.0)
        sh *= 2
    g_in = jnp.where(grow < ng - 1, pltpu.roll(gt, ng - 1, 0), 0.0)
    for k in range(8):
        s1[pl.ds(k, ng, stride=8), :] = g_in
    return g + a * s1[...]


def _rnn_specs(s):
    blk = lambda off: pl.BlockSpec((None, s, LANES), lambda cb, i: (i, 0, off + cb))
    vec = lambda r: pl.BlockSpec((r, LANES), lambda cb, i: (0, cb))
    mat = pl.BlockSpec((None, LANES, LANES), lambda cb, i: (cb, 0, 0))
    return blk, vec, mat


def _rnn_fwd(zrest3, conv_w, conv_b, bda, bdx, ba, bx, lam):
    b, s, _ = zrest3.shape

    def body(xr_ref, g_ref, cw_ref, cb_ref, bda_ref, bdx_ref, ba_ref, bx_ref, lam_ref, h_ref, gr_ref, s1, s2):
        xr = xr_ref[...].astype(F32)
        rows, _, xc, _, _, i, _, a, _, sq = _rnn_common(
            xr, cw_ref, cb_ref, bda_ref, bdx_ref, ba_ref, bx_ref, lam_ref, s)
        h = _scan_down(a, sq * (i * xc), rows, s, s1, s2)
        h_ref[...] = h
        g = g_ref[...].astype(F32)
        gr_ref[...] = (h * (g * _sigmoid(g))).astype(BF16)

    blk, vec, mat = _rnn_specs(s)
    return pl.pallas_call(
        body, name="rnn_fwd", grid=(N_CBLK, b),
        in_specs=[blk(N_CBLK), blk(2 * N_CBLK), vec(CONV_W), vec(1), mat, mat, vec(1), vec(1), vec(1)],
        out_specs=[blk(0), blk(0)],
        out_shape=[jax.ShapeDtypeStruct((b, s, D_MODEL), F32), jax.ShapeDtypeStruct((b, s, D_MODEL), BF16)],
        scratch_shapes=[pltpu.VMEM((s, LANES), F32), pltpu.VMEM((s, LANES), F32)],
        compiler_params=_cparams(("parallel", "parallel")),
    )(zrest3, zrest3, conv_w, conv_b, bda, bdx, ba, bx, lam)


def _rnn_bwd(zrest3, h3, dh3, conv_w, conv_b, bda, bdx, ba, bx, lam):
    b, s, _ = zrest3.shape

    def body(xr_ref, h_ref, dh_ref, cw_ref, cb_ref, bda_ref, bdx_ref, ba_ref, bx_ref, lam_ref,
             dxr_ref, pv_ref, dbd_ref, s1, s2):
        @pl.when(pl.program_id(1) == 0)
        def _():
            pv_ref[...] = jnp.zeros_like(pv_ref)
            dbd_ref[...] = jnp.zeros_like(dbd_ref)

        xr = xr_ref[...].astype(F32)
        rows, (x1, x2, x3), xc, xcb, r, i, sp, a, a2, sq = _rnn_common(
            xr, cw_ref, cb_ref, bda_ref, bdx_ref, ba_ref, bx_ref, lam_ref, s)
        (a_next,) = _shifted(a, (-1,), rows, s)
        g = _scan_up(a_next, dh_ref[...], rows, s, s1, s2)
        (hp,) = _shifted(h_ref[...], (1,), rows, s)
        da = g * hp
        dsq = g * (i * xc)
        di = g * (sq * xc)
        dxc = g * (sq * i)
        dlog = da * a - dsq * (a2 / sq)
        dr = dlog * (-RG_C * sp)
        dpr = dr * (r * (1.0 - r))
        dpi = di * (i * (1.0 - i))
        dprb = dpr.astype(BF16)
        dpib = dpi.astype(BF16)
        dxc = dxc + _dot_nt(dprb, bda_ref[...]) + _dot_nt(dpib, bdx_ref[...])

        up1, up2, up3 = _shifted(dxc, (-1, -2, -3), rows, s)
        dxr = cw_ref[3:4, :] * dxc + cw_ref[2:3, :] * up1 + cw_ref[1:2, :] * up2 + cw_ref[0:1, :] * up3
        dxr_ref[...] = dxr.astype(BF16)

        def colsum(v):
            return jnp.sum(v, axis=0, keepdims=True)

        pv_ref[0:1, :] += colsum(dxc * x3)
        pv_ref[1:2, :] += colsum(dxc * x2)
        pv_ref[2:3, :] += colsum(dxc * x1)
        pv_ref[3:4, :] += colsum(dxc * xr)
        pv_ref[4:5, :] += colsum(dxc)
        pv_ref[5:6, :] += colsum(dpr)
        pv_ref[6:7, :] += colsum(dpi)
        pv_ref[7:8, :] += colsum(dlog * r) * (RG_C * _sigmoid(-lam_ref[...]))
        dbd_ref[0] += _dot_tn(xcb, dprb)
        dbd_ref[1] += _dot_tn(xcb, dpib)

    blk, vec, mat = _rnn_specs(s)
    hblk = pl.BlockSpec((None, s, LANES), lambda cb, i: (i, 0, cb))
    return pl.pallas_call(
        body, name="rnn_bwd", grid=(N_CBLK, b),
        in_specs=[blk(N_CBLK), hblk, hblk, vec(CONV_W), vec(1), mat, mat, vec(1), vec(1), vec(1)],
        out_specs=[pl.BlockSpec((s, LANES), lambda cb, i: (i, cb)), pl.BlockSpec((8, LANES), lambda cb, i: (0, cb)),
                   pl.BlockSpec((None, 2, LANES, LANES), lambda cb, i: (cb, 0, 0, 0))],
        out_shape=[jax.ShapeDtypeStruct((b * s, D_MODEL), BF16), jax.ShapeDtypeStruct((8, D_MODEL), F32),
                   jax.ShapeDtypeStruct((N_CBLK, 2, LANES, LANES), F32)],
        scratch_shapes=[pltpu.VMEM((s, LANES), F32), pltpu.VMEM((s, LANES), F32)],
        compiler_params=_cparams(("parallel", "arbitrary")),
    )(zrest3, h3, dh3, conv_w, conv_b, bda, bdx, ba, bx, lam)


def _branch_merge(ga, gr, wa, wr, zrest):
    t = ga.shape[0]
    tm = min(512, t)
    tn = D_MODEL

    def body(ga_ref, gr_ref, wa_ref, wr_ref, mga_ref, mgr_ref, ya_ref, yr_ref, m_ref):
        ya = _dot(ga_ref[...], wa_ref[...])
        yr = _dot(gr_ref[...], wr_ref[...])
        ya_ref[...] = ya.astype(BF16)
        yr_ref[...] = yr.astype(BF16)
        m_ref[...] = (_sigmoid(mga_ref[...].astype(F32)) * ya + _sigmoid(mgr_ref[...].astype(F32)) * yr).astype(BF16)

    nj = D_MODEL // tn
    act = pl.BlockSpec((tm, D_MODEL), lambda i, j: (i, 0))
    wgt = pl.BlockSpec((D_MODEL, tn), lambda i, j: (0, j))
    out = pl.BlockSpec((tm, tn), lambda i, j: (i, j))
    return pl.pallas_call(
        body, name="branch_merge", grid=(t // tm, nj),
        in_specs=[act, act, wgt, wgt, pl.BlockSpec((tm, tn), lambda i, j: (i, 3 * nj + j)),
                  pl.BlockSpec((tm, tn), lambda i, j: (i, 4 * nj + j))],
        out_specs=[out, out, out],
        out_shape=[jax.ShapeDtypeStruct((t, D_MODEL), BF16), jax.ShapeDtypeStruct((t, D_MODEL), BF16),
                   jax.ShapeDtypeStruct((t, D_MODEL), BF16)],
        compiler_params=_cparams(("parallel", "parallel")),
    )(ga, gr, wa, wr, zrest, zrest)


def _out_loss(m, wout, x2, tgt2, wpost):
    t = m.shape[0]
    tm = min(512, t)

    def body(m_ref, w_ref, x_ref, t_ref, wp_ref, dy_ref, do_ref, acc_ref):
        @pl.when(pl.program_id(0) == 0)
        def _():
            acc_ref[...] = jnp.zeros_like(acc_ref)

        o = _dot(m_ref[...], w_ref[...])
        r2 = lax.rsqrt(jnp.mean(o * o, axis=-1, keepdims=True) + NORM_EPS)
        n = o * r2
        wp = wp_ref[...]
        err = (x_ref[...] + n * wp) - t_ref[...]
        dy = err * (1.0 / D_MODEL)
        dn = dy * wp
        do = r2 * (dn - n * jnp.mean(dn * n, axis=-1, keepdims=True))
        dy_ref[...] = dy
        do_ref[...] = do.astype(BF16)
        acc_ref[0:1, :] += jnp.sum(dy * n, axis=0, keepdims=True)
        acc_ref[1:2, :] += jnp.sum(err * err, axis=0, keepdims=True)

    row = pl.BlockSpec((tm, D_MODEL), lambda i: (i, 0))
    return pl.pallas_call(
        body, name="out_loss", grid=(t // tm,),
        in_specs=[row, pl.BlockSpec((D_MODEL, D_MODEL), lambda i: (0, 0)), row, row,
                  pl.BlockSpec((1, D_MODEL), lambda i: (0, 0))],
        out_specs=[row, row, pl.BlockSpec((8, D_MODEL), lambda i: (0, 0))],
        out_shape=[jax.ShapeDtypeStruct((t, D_MODEL), F32), jax.ShapeDtypeStruct((t, D_MODEL), BF16),
                   jax.ShapeDtypeStruct((8, D_MODEL), F32)],
        compiler_params=_cparams(("arbitrary",)),
    )(m, wout, x2, tgt2, wpost)


def _merge_bwd(do, wout, zrest, ya, yr):
    t = do.shape[0]
    tm = min(512, t)
    tn = D_MODEL
    nj = D_MODEL // tn

    def body(do_ref, w_ref, mga_ref, mgr_ref, ya_ref, yr_ref, dya_ref, dyr_ref, dmga_ref, dmgr_ref):
        dm = _dot_nt(do_ref[...], w_ref[...])
        sa = _sigmoid(mga_ref[...].astype(F32))
        sr = _sigmoid(mgr_ref[...].astype(F32))
        dya_ref[...] = (dm * sa).astype(BF16)
        dyr_ref[...] = (dm * sr).astype(BF16)
        dmga_ref[...] = (dm * ya_ref[...].astype(F32) * (sa * (1.0 - sa))).astype(BF16)
        dmgr_ref[...] = (dm * yr_ref[...].astype(F32) * (sr * (1.0 - sr))).astype(BF16)

    out = pl.BlockSpec((tm, tn), lambda i, j: (i, j))
    bf = jax.ShapeDtypeStruct((t, D_MODEL), BF16)
    return pl.pallas_call(
        body, name="merge_bwd", grid=(t // tm, nj),
        in_specs=[pl.BlockSpec((tm, D_MODEL), lambda i, j: (i, 0)), pl.BlockSpec((tn, D_MODEL), lambda i, j: (j, 0)),
                  pl.BlockSpec((tm, tn), lambda i, j: (i, 3 * nj + j)),
                  pl.BlockSpec((tm, tn), lambda i, j: (i, 4 * nj + j)), out, out],
        out_specs=[out, out, out, out],
        out_shape=[bf, bf, bf, bf],
        compiler_params=_cparams(("parallel", "parallel")),
    )(do, wout, zrest, zrest, ya, yr)


def _branch_bwd(dya, dyr, wa, wr, zrest, yatt, ylru):
    t = dya.shape[0]
    tm = min(512, t)
    tn = D_MODEL
    nj = D_MODEL // tn

    def body(dya_ref, dyr_ref, wa_ref, wr_ref, ga_ref, gr_ref, ya_ref, yl_ref,
             dyatt_ref, dga_ref, dyl_ref, dgr_ref):
        dga = _dot_nt(dya_ref[...], wa_ref[...])
        dgr = _dot_nt(dyr_ref[...], wr_ref[...])
        g = ga_ref[...].astype(F32)
        sg = _sigmoid(g)
        dyatt_ref[...] = (dga * (g * sg)).astype(BF16)
        dga_ref[...] = (dga * ya_ref[...] * (sg * (1.0 + g * (1.0 - sg)))).astype(BF16)
        g = gr_ref[...].astype(F32)
        sg = _sigmoid(g)
        dyl_ref[...] = dgr * (g * sg)
        dgr_ref[...] = (dgr * yl_ref[...] * (sg * (1.0 + g * (1.0 - sg)))).astype(BF16)

    act = pl.BlockSpec((tm, D_MODEL), lambda i, j: (i, 0))
    wgt = pl.BlockSpec((tn, D_MODEL), lambda i, j: (j, 0))
    out = pl.BlockSpec((tm, tn), lambda i, j: (i, j))
    bf = jax.ShapeDtypeStruct((t, D_MODEL), BF16)
    return pl.pallas_call(
        body, name="branch_bwd", grid=(t // tm, nj),
        in_specs=[act, act, wgt, wgt, pl.BlockSpec((tm, tn), lambda i, j: (i, j)),
                  pl.BlockSpec((tm, tn), lambda i, j: (i, 2 * nj + j)), out, out],
        out_specs=[out, out, out, out],
        out_shape=[bf, bf, jax.ShapeDtypeStruct((t, D_MODEL), F32), bf],
        compiler_params=_cparams(("parallel", "parallel")),
    )(dya, dyr, wa, wr, zrest, zrest, yatt, ylru)


def _dh_partial(parts, after, name):
    t = parts[0][0].shape[0]
    tm = min(256, t)
    np_ = len(parts)

    def body(*refs):
        o_ref = refs[-1]
        acc = _dot(refs[0][...], refs[np_][...])
        for p in range(1, np_):
            acc = acc + _dot(refs[p][...], refs[np_ + p][...])
        o_ref[...] = acc

    in_specs = [pl.BlockSpec((tm, dz.shape[1]), lambda i: (i, 0)) for dz, _ in parts]
    in_specs += [pl.BlockSpec(w.shape, lambda i: (0, 0)) for _, w in parts]
    in_specs += [pl.BlockSpec(after.shape, lambda i: (0, 0))]
    return pl.pallas_call(
        body, name=name, grid=(t // tm,),
        in_specs=in_specs,
        out_specs=pl.BlockSpec((tm, D_MODEL), lambda i: (i, 0)),
        out_shape=jax.ShapeDtypeStruct((t, D_MODEL), F32),
        compiler_params=_cparams(("parallel",), vmem_mb=48),
    )(*[dz for dz, _ in parts], *[w for _, w in parts], after)


def _dh_final(parts, acc_in, x2, dy, wpre):
    t = x2.shape[0]
    tm = min(256, t)
    np_ = len(parts)

    def body(*refs):
        acc_ref, x_ref, dy_ref, w_ref = refs[2 * np_:2 * np_ + 4]
        gx_ref, pw_ref = refs[2 * np_ + 4:]

        @pl.when(pl.program_id(0) == 0)
        def _():
            pw_ref[...] = jnp.zeros_like(pw_ref)

        dh = acc_ref[...]
        for p in range(np_):
            dh = dh + _dot(refs[p][...], refs[np_ + p][...])
        x = x_ref[...]
        r = lax.rsqrt(jnp.mean(x * x, axis=-1, keepdims=True) + NORM_EPS)
        xn = x * r
        dxn = dh * w_ref[...]
        gx_ref[...] = r * (dxn - xn * jnp.mean(dxn * xn, axis=-1, keepdims=True)) + dy_ref[...]
        pw_ref[0:1, :] += jnp.sum(dh * xn, axis=0, keepdims=True)

    row = pl.BlockSpec((tm, D_MODEL), lambda i: (i, 0))
    in_specs = [pl.BlockSpec((tm, dz.shape[1]), lambda i: (i, 0)) for dz, _ in parts]
    in_specs += [pl.BlockSpec(w.shape, lambda i: (0, 0)) for _, w in parts]
    in_specs += [row, row, row, pl.BlockSpec((1, D_MODEL), lambda i: (0, 0))]
    return pl.pallas_call(
        body, name="dh_final", grid=(t // tm,),
        in_specs=in_specs,
        out_specs=[row, pl.BlockSpec((8, D_MODEL), lambda i: (0, 0))],
        out_shape=[jax.ShapeDtypeStruct((t, D_MODEL), F32), jax.ShapeDtypeStruct((8, D_MODEL), F32)],
        compiler_params=_cparams(("arbitrary",), vmem_mb=48),
    )(*[dz for dz, _ in parts], *[w for _, w in parts], acc_in, x2, dy, wpre)


def _adamw(w, g, m, v):
    m = ADAM_B1 * m + (1.0 - ADAM_B1) * g
    v = ADAM_B2 * v + (1.0 - ADAM_B2) * (g * g)
    m_hat = m / (1.0 - ADAM_B1 ** ADAM_STEP)
    v_hat = v / (1.0 - ADAM_B2 ** ADAM_STEP)
    delta = -ADAM_LR * (m_hat / (jnp.sqrt(v_hat) + ADAM_EPS) + ADAM_WD * w)
    return delta, m, v


def _reduce_adamw(own, parts, place, w, m, v, name):
    r, c = w.shape
    blk, nblk, at = _blocks_2d(r, c)

    def body(place_ref, own_ref, p_ref, w_ref, m_ref, v_ref, g_ref, d_ref, nm_ref, nv_ref):
        mine = place_ref[1]
        own_blk = own_ref[...]
        g = jnp.where(mine == 0, own_blk, p_ref[0].astype(F32))
        for j in range(1, N_CHIPS):
            g = g + jnp.where(mine == j, own_blk, p_ref[j].astype(F32))
        d, nm, nv = _adamw(w_ref[...], g, m_ref[...], v_ref[...])
        g_ref[...] = g
        d_ref[...] = d
        nm_ref[...] = nm
        nv_ref[...] = nv

    row = pl.BlockSpec(blk, lambda i, pr: at(i))
    sh = jax.ShapeDtypeStruct((r, c), F32)
    grid_spec = pltpu.PrefetchScalarGridSpec(
        num_scalar_prefetch=1, grid=(nblk,),
        in_specs=[row, pl.BlockSpec((N_CHIPS,) + blk, lambda i, pr: (0,) + at(i)), row, row, row],
        out_specs=[row, row, row, row])
    return pl.pallas_call(
        body, name=name, grid_spec=grid_spec, out_shape=[sh, sh, sh, sh],
        compiler_params=_cparams(("parallel",)),
    )(place, own, parts, w, m, v)


def _reduce_adamw_stacked(own, parts, place, triples, name):
    n = len(triples)
    _, r, c = triples[0][0].shape

    def body(place_ref, own_ref, p_ref, *refs):
        ins, outs = refs[:3 * n], refs[3 * n:]
        mine = place_ref[1]
        for i in range(n):
            rows = slice(i * r, (i + 1) * r)
            own_blk = own_ref[rows, :]
            g = jnp.where(mine == 0, own_blk, p_ref[0, rows, :].astype(F32))
            for j in range(1, N_CHIPS):
                g = g + jnp.where(mine == j, own_blk, p_ref[j, rows, :].astype(F32))
            d, nm, nv = _adamw(ins[3 * i][0], g, ins[3 * i + 1][0], ins[3 * i + 2][0])
            for k, val in enumerate((g, d, nm, nv)):
                outs[4 * i + k][0] = val

    whole = lambda shape: pl.BlockSpec(shape, lambda i, pr: (0,) * len(shape))
    grid_spec = pltpu.PrefetchScalarGridSpec(
        num_scalar_prefetch=1, grid=(1,),
        in_specs=[whole(own.shape), whole(parts.shape)] + [whole((1, r, c))] * (3 * n),
        out_specs=[whole((1, r, c))] * (4 * n))
    res = pl.pallas_call(
        body, name=name, grid_spec=grid_spec,
        out_shape=[jax.ShapeDtypeStruct((1, r, c), F32)] * (4 * n),
        compiler_params=_cparams(("arbitrary",)),
    )(place, own, parts, *[a for t3 in triples for a in t3])
    return [res[4 * i:4 * i + 4] for i in range(n)]


def _interleave_qkv(a):
    lead = a.shape[:-1]
    return a.reshape(lead + (3, HEAD_PAIRS, LANES)).swapaxes(-3, -2).reshape(lead + (3 * D_MODEL,))


def _deinterleave_qkv(a):
    lead = a.shape[:-1]
    return a.reshape(lead + (HEAD_PAIRS, 3, LANES)).swapaxes(-3, -2).reshape(lead + (3 * D_MODEL,))


def _interleave_rows(a):
    return a.reshape(3, HEAD_PAIRS, LANES, a.shape[1]).swapaxes(0, 1).reshape(a.shape)


def _deinterleave_rows(a):
    return a.reshape(HEAD_PAIRS, 3, LANES, a.shape[1]).swapaxes(0, 1).reshape(a.shape)


def _pack_small(pre, conv_b, rg_ba, rg_bx, lam, post, loss_row, b_in, conv_w_full, rg_wa, rg_wx):
    z = jnp.zeros((1, D_MODEL), F32)
    b_used = jnp.concatenate([b_in[:, 0:3 * D_MODEL], b_in[:, 3 * D_MODEL + HEADS:IN_TOTAL]], axis=1)
    b_f = jnp.pad(b_in[:, 3 * D_MODEL:3 * D_MODEL + HEADS], ((0, 0), (0, D_MODEL - HEADS)))
    return jnp.concatenate([
        pre, conv_b, rg_ba, rg_bx, lam, post, loss_row, z,
        b_used.reshape(9, D_MODEL), b_f, conv_w_full, z, z,
        rg_wa.reshape(64, D_MODEL), rg_wx.reshape(64, D_MODEL)], axis=0)


def _unpack_small(p):
    b_used = p[8:17].reshape(1, 9 * D_MODEL)
    b_in = jnp.concatenate([b_used[:, 0:3 * D_MODEL], p[17:18, 0:HEADS], b_used[:, 3 * D_MODEL:]], axis=1)
    return dict(pre_norm_w=p[0:1], conv_b=p[1:2], rg_ba=p[2:3], rg_bx=p[3:4], rg_lambda=p[4:5],
                post_norm_w=p[5:6], loss_row=p[6:7], b_in=b_in, conv_w_full=p[18:22],
                rg_wa=p[24:88].reshape(1, 16, 64, 64), rg_wx=p[88:152].reshape(1, 16, 64, 64))


def _reduce_small(parts, first, w, m, v, vectors):
    nvec = len(vectors)

    def body(p_ref, f_ref, w_ref, m_ref, v_ref, *refs):
        ins, outs = refs[:3 * nvec], refs[3 * nvec:]
        g = p_ref[0]
        g0 = f_ref[0, 0:1, :]
        for j in range(1, N_DEV):
            g = g + p_ref[j]
            g0 = g0 + f_ref[j, 0:1, :]
        d, nm, nv = _adamw(w_ref[...], g, m_ref[...], v_ref[...])
        for k, val in enumerate((g, d, nm, nv)):
            outs[k][...] = val
        for i in range(nvec):
            gi = g0 if i == 0 else g[i:i + 1, :]
            di, nmi, nvi = _adamw(ins[3 * i][...], gi, ins[3 * i + 1][...], ins[3 * i + 2][...])
            for k, val in enumerate((gi, di, nmi, nvi)):
                outs[4 + 4 * i + k][...] = val
        outs[-1][...] = jnp.zeros((8, LANES), F32) + (0.5 / D_MODEL) * jnp.sum(g[LOSS_ROW:LOSS_ROW + 1, :])

    sh = jax.ShapeDtypeStruct((SMALL_ROWS, D_MODEL), F32)
    vec = jax.ShapeDtypeStruct((1, D_MODEL), F32)
    res = pl.pallas_call(
        body, name="reduce_small",
        out_shape=[sh, sh, sh, sh] + [vec] * (4 * nvec) + [jax.ShapeDtypeStruct((8, LANES), F32)],
    )(parts, first, w, m, v, *[a for t3 in vectors for a in t3])
    return res[:4], [res[4 + 4 * i:8 + 4 * i] for i in range(nvec)], res[-1]


def kernel(x, pre_norm_w, w_in, b_in, conv_w, conv_b, rg_wa, rg_ba, rg_wx, rg_bx, rg_lambda, w_branch_a, w_branch_r, w_out, post_norm_w, loss_target, m_pre_norm_w, m_w_in, m_b_in, m_conv_w, m_conv_b, m_rg_wa, m_rg_ba, m_rg_wx, m_rg_bx, m_rg_lambda, m_w_branch_a, m_w_branch_r, m_w_out, m_post_norm_w, v_pre_norm_w, v_w_in, v_b_in, v_conv_w, v_conv_b, v_rg_wa, v_rg_ba, v_rg_wx, v_rg_bx, v_rg_lambda, v_w_branch_a, v_w_branch_r, v_w_out, v_post_norm_w):
    b, s, _ = x.shape
    t = b * s
    me = 4 * lax.axis_index("x") + 2 * lax.axis_index("y") + lax.axis_index("c")
    shard_rows = D_MODEL // N_DEV

    place = jnp.stack([lax.axis_index("c"), 2 * lax.axis_index("x") + lax.axis_index("y")]).astype(jnp.int32)
    w_in_all = _gather(w_in[0].T.astype(BF16), "gather_w_in")
    wt_full = w_in_all.reshape(IN_TOTAL, D_MODEL)
    conv_terms = jnp.concatenate(_split3(conv_w[0]), axis=0)
    conv_pad = jnp.pad(conv_terms, ((0, 16 - 3 * CONV_W), (0, D_MODEL - LANES)))
    sq_stack = jnp.concatenate([w_branch_a[0].astype(BF16), w_branch_r[0].astype(BF16), w_out[0].astype(BF16),
                                conv_pad], axis=0)
    sq_sems, sq_src, sq_land, sq_token = _gather_start(sq_stack, w_in_all, "gather_w_sq_start")

    w_qkv = _interleave_rows(wt_full[0:3 * D_MODEL])
    w_f = jnp.pad(wt_full[3 * D_MODEL:3 * D_MODEL + HEADS], ((0, LANES - HEADS), (0, 0)))
    w_rest = wt_full[3 * D_MODEL + HEADS:IN_USED]
    b_qkv = _interleave_qkv(b_in[:, 0:3 * D_MODEL]) + sq_token[0, 0]
    b_f = jnp.pad(b_in[:, 3 * D_MODEL:3 * D_MODEL + HEADS], ((0, 0), (0, LANES - HEADS)))
    b_rest = b_in[:, 3 * D_MODEL + HEADS:IN_USED]

    def blockdiag(w):
        w2 = w.reshape(N_CBLK, 2, HEAD_DIM, HEAD_DIM)
        zz = jnp.zeros((N_CBLK, HEAD_DIM, HEAD_DIM), w.dtype)
        top = jnp.concatenate([w2[:, 0], zz], axis=2)
        bot = jnp.concatenate([zz, w2[:, 1]], axis=2)
        return jnp.concatenate([top, bot], axis=1).astype(BF16)

    bda, bdx = blockdiag(rg_wa[0]), blockdiag(rg_wx[0])

    x2 = x.reshape(t, D_MODEL)
    tgt2 = loss_target.reshape(t, D_MODEL)
    h, qkv, zf = _prenorm_inproj(x2, pre_norm_w, w_qkv, b_qkv, w_f, b_f)
    zrest = _mm_bias(h, w_rest, b_rest, BF16, "inproj_rest")
    qkv3 = qkv.reshape(b, s, 3 * D_MODEL)
    zrest3 = zrest.reshape(b, s, 5 * D_MODEL)
    zf3 = zf.reshape(b, s, LANES)
    cexp3, crow = _fgate_fwd(zf3)
    yatt3, lse, ga3 = _attn_fwd(qkv3, cexp3, crow, zrest3)

    sq_all = _gather_wait(sq_sems, sq_src, sq_land, ga3, "gather_w_sq_wait")
    sq_all = lax.dynamic_update_slice(sq_all, sq_stack[None], (me, 0, 0))
    wa = sq_all[:, 0:shard_rows].reshape(D_MODEL, D_MODEL)
    wr = sq_all[:, shard_rows:2 * shard_rows].reshape(D_MODEL, D_MODEL)
    wo = sq_all[:, 2 * shard_rows:3 * shard_rows].reshape(D_MODEL, D_MODEL)
    conv_all = sq_all[:, 3 * shard_rows:3 * shard_rows + 3 * CONV_W, 0:LANES].astype(F32)
    conv_all = (conv_all[:, 0:CONV_W] + conv_all[:, CONV_W:2 * CONV_W]) + conv_all[:, 2 * CONV_W:3 * CONV_W]
    conv_full = conv_all.transpose(1, 0, 2).reshape(CONV_W, D_MODEL)

    ylru3, gr3 = _rnn_fwd(zrest3, conv_full, conv_b, bda, bdx, rg_ba, rg_bx, rg_lambda)
    ga, gr = ga3.reshape(t, D_MODEL), gr3.reshape(t, D_MODEL)
    ya, yr, mm = _branch_merge(ga, gr, wa, wr, zrest)
    dy, do, acc_out = _out_loss(mm, wo, x2, tgt2, post_norm_w)

    dya, dyr, dz_mga, dz_mgr = _merge_bwd(do, wo, zrest, ya, yr)
    dyatt, dz_ga, dylru, dz_gr = _branch_bwd(dya, dyr, wa, wr, zrest, yatt3.reshape(t, D_MODEL),
                                             ylru3.reshape(t, D_MODEL))
    dz_xr, pvec, dbd = _rnn_bwd(zrest3, ylru3, dylru.reshape(b, s, D_MODEL), conv_full, conv_b, bda, bdx,
                                rg_ba, rg_bx, rg_lambda)
    dqkv3, dc3 = _attn_bwd(qkv3, dyatt.reshape(b, s, D_MODEL), yatt3, lse, crow, cexp3)
    dz_f = _fgate_bwd(dc3, zf3)
    dz_qkv = dqkv3.reshape(t, 3 * D_MODEL)

    dw_qkv, db_qkv = _mm_tn(dz_qkv, h, "dw_qkv")
    dw_f, db_f = _mm_tn(dz_f, h, "dw_f")
    dw_parts, db_parts = [], []
    for nm, dzp in (("ga", dz_ga), ("xr", dz_xr), ("gr", dz_gr), ("mga", dz_mga), ("mgr", dz_mgr)):
        dwp, dbp = _mm_tn(dzp, h, "dw_" + nm)
        dw_parts.append(dwp)
        db_parts.append(dbp[0:1])

    zeros_tail = jnp.zeros((IN_TOTAL - IN_USED, D_MODEL), F32)
    dwt_full = jnp.concatenate([_deinterleave_rows(dw_qkv), dw_f[0:HEADS]] + dw_parts + [zeros_tail], axis=0)
    dw_in_send = dwt_full.reshape(N_CHIPS, 2, W_SHARD, D_MODEL).transpose(1, 0, 2, 3)
    swp_sems, dw_in_src, swp_land, swp_token = _swap_start(dw_in_send, db_f, "swap_dw_in_start")
    dw_a, _ = _mm_tn(ga, dya, "dw_a", after=swp_token)
    dw_r, _ = _mm_tn(gr, dyr, "dw_r", after=swp_token)
    dw_o, _ = _mm_tn(mm, do, "dw_o", after=swp_token)
    dw_in_send, sib_in = _swap_wait(swp_sems, dw_in_src, swp_land, dw_o, "swap_dw_in_wait")
    by_dest = lambda a: a.reshape(N_CHIPS, 2, shard_rows, D_MODEL).transpose(1, 0, 2, 3)
    dw_sq_send = jnp.concatenate([by_dest(dw_a), by_dest(dw_r), by_dest(dw_o)], axis=2)

    db_in_full = jnp.concatenate([_deinterleave_qkv(db_qkv[0:1]), db_f[0:1, 0:HEADS]] + db_parts
                                 + [jnp.zeros((1, IN_TOTAL - IN_USED), F32)], axis=1)
    d_rg_wa = jnp.stack([dbd[:, 0, 0:HEAD_DIM, 0:HEAD_DIM], dbd[:, 0, HEAD_DIM:, HEAD_DIM:]], axis=1)
    d_rg_wx = jnp.stack([dbd[:, 1, 0:HEAD_DIM, 0:HEAD_DIM], dbd[:, 1, HEAD_DIM:, HEAD_DIM:]], axis=1)
    small_g = _pack_small(jnp.zeros((1, D_MODEL), F32), pvec[4:5], pvec[5:6], pvec[6:7], pvec[7:8], acc_out[0:1],
                          acc_out[1:2], db_in_full, pvec[0:4], d_rg_wa, d_rg_wx)
    sm_sems, sm_src, sm_land, sm_token = _gather_start(small_g, dw_o, "gather_small_start")

    dw_sq_send = dw_sq_send + sm_token[0, 0]
    (sib_sq,) = _swap_with_sibling([dw_sq_send], "swap_dw_sq")
    chip_in, own_in = _pair_add(dw_in_send, sib_in, place, "pair_add_in")
    chip_sq, own_sq = _pair_add(dw_sq_send, sib_sq, place, "pair_add_sq")
    sems, sent, lands, token = _exchange_chips_start([chip_in, chip_sq], "exchange_dw_start")

    wt = lambda lo: w_rest[lo * D_MODEL:(lo + 1) * D_MODEL]
    dh_a = _dh_partial([(dz_qkv, w_qkv), (dz_f, w_f)], token, "dh_qkv")
    grad_x2, acc_pre = _dh_final(
        [(dz_ga, wt(0)), (dz_xr, wt(1)), (dz_gr, wt(2)), (dz_mga, wt(3)), (dz_mgr, wt(4))],
        dh_a, x2, dy, pre_norm_w)
    pre_sems, pre_src, pre_land, pre_token = _gather_start(acc_pre, grad_x2, "gather_pre_start")
    recv_in, recv_sq = _exchange_chips_wait(sems, sent, lands, pre_token, "exchange_dw_wait")

    g_in, d_in, nm_in, nv_in = [a.T for a in _reduce_adamw(
        own_in, recv_in, place, w_in[0].T, m_w_in[0].T, v_w_in[0].T, "adamw_w_in")]
    sq_out = _reduce_adamw_stacked(
        own_sq, recv_sq, place,
        [(w_branch_a, m_w_branch_a, v_w_branch_a), (w_branch_r, m_w_branch_r, v_w_branch_r),
         (w_out, m_w_out, v_w_out)], "adamw_w_sq")
    pre_all = _gather_wait(pre_sems, pre_src, pre_land, sq_out[2][1], "gather_pre_wait")
    pre_all = lax.dynamic_update_slice(pre_all, acc_pre[None], (me, 0, 0))
    small_all = _gather_wait(sm_sems, sm_src, sm_land, pre_all, "gather_small_wait")
    small_all = lax.dynamic_update_slice(small_all, small_g[None], (me, 0, 0))

    def place_conv(a):
        return lax.dynamic_update_slice(jnp.zeros((CONV_W, D_MODEL), F32), a[0], (0, me * LANES))

    zrow = jnp.zeros((1, D_MODEL), F32)
    vector_names = ["pre_norm_w", "conv_b", "rg_ba", "rg_bx", "rg_lambda", "post_norm_w"]
    vectors = [(pre_norm_w, m_pre_norm_w, v_pre_norm_w), (conv_b, m_conv_b, v_conv_b), (rg_ba, m_rg_ba, v_rg_ba),
               (rg_bx, m_rg_bx, v_rg_bx), (rg_lambda, m_rg_lambda, v_rg_lambda),
               (post_norm_w, m_post_norm_w, v_post_norm_w)]
    small_w = _pack_small(zrow, zrow, zrow, zrow, zrow, zrow, zrow, b_in, place_conv(conv_w), rg_wa[0], rg_wx[0])
    small_m = _pack_small(zrow, zrow, zrow, zrow, zrow, zrow, zrow, m_b_in, place_conv(m_conv_w), m_rg_wa[0],
                          m_rg_wx[0])
    small_v = _pack_small(zrow, zrow, zrow, zrow, zrow, zrow, zrow, v_b_in, place_conv(v_conv_w), v_rg_wa[0],
                          v_rg_wx[0])
    packed, vector_out, loss_tile = _reduce_small(small_all, pre_all, small_w, small_m, small_v, vectors)
    outs_small = [_unpack_small(p) for p in packed]
    loss = loss_tile[0, 0]

    def leaf(kind, name):
        if name == "w_in":
            return (g_in, d_in, nm_in, nv_in)[kind][None]
        if name in ("w_branch_a", "w_branch_r", "w_out"):
            return sq_out[("w_branch_a", "w_branch_r", "w_out").index(name)][kind]
        if name == "conv_w":
            return lax.dynamic_slice(outs_small[kind]["conv_w_full"], (0, me * LANES), (CONV_W, LANES))[None]
        if name in vector_names:
            return vector_out[vector_names.index(name)][kind]
        return outs_small[kind][name]

    names = ["pre_norm_w", "w_in", "b_in", "conv_w", "conv_b", "rg_wa", "rg_ba", "rg_wx", "rg_bx", "rg_lambda",
             "w_branch_a", "w_branch_r", "w_out", "post_norm_w"]
    out = [loss, grad_x2.reshape(b, s, D_MODEL)]
    for kind in range(4):
        out += [leaf(kind, nm) for nm in names]
    return tuple(out)
```

```python
import jax
import jax.numpy as jnp
from jax import lax
from jax.experimental import pallas as pl
from jax.experimental.pallas import tpu as pltpu

F32 = jnp.float32
BF16 = jnp.bfloat16

N_DEV = 8
D_MODEL = 1024
HEADS = 16
HEAD_DIM = 64
HEAD_PAIRS = HEADS // 2
LANES = 128
N_CBLK = D_MODEL // LANES
CONV_W = 4
RG_C = 8.0
NORM_EPS = 1e-6
MASK_VALUE = -1e30
IN_USED = 8208
IN_TOTAL = 9232
W_SHARD = IN_TOTAL // N_DEV

ADAM_LR = 0.001
ADAM_B1 = 0.9
ADAM_B2 = 0.999
ADAM_EPS = 1e-08
ADAM_WD = 0.01
ADAM_STEP = 10

ATT_TILE_FWD = 256
ATT_TILE_BWD = 512
SCAN_TILE = 256
SMALL_ROWS = 152
LOSS_ROW = 6


def _cparams(sem=None, vmem_mb=None):
    kw = {}
    if sem is not None:
        kw["dimension_semantics"] = sem
    if vmem_mb is not None:
        kw["vmem_limit_bytes"] = vmem_mb * 1024 * 1024
    return pltpu.CompilerParams(**kw)


def _sigmoid(x):
    return 1.0 / (1.0 + jnp.exp(-x))


def _softplus(x):
    return jnp.maximum(x, 0.0) + jnp.log1p(jnp.exp(-jnp.abs(x)))


def _one_minus_exp(y, exp_y):
    series = -y * (1.0 + y * (1.0 / 2 + y * (1.0 / 6 + y * (1.0 / 24 + y * (1.0 / 120)))))
    return jnp.where(y > -0.0625, series, 1.0 - exp_y)


def _split3(x):
    hi = x.astype(BF16)
    r1 = x - hi.astype(F32)
    mid = r1.astype(BF16)
    lo = (r1 - mid.astype(F32)).astype(BF16)
    return hi, mid, lo


def _dot(a, b):
    return jnp.dot(a, b, preferred_element_type=F32)


def _dot_nt(a, b):
    return lax.dot_general(a, b, (((1,), (1,)), ((), ())), preferred_element_type=F32)


def _dot_tn(a, b):
    return lax.dot_general(a, b, (((0,), (0,)), ((), ())), preferred_element_type=F32)


def _iota(shape, dim):
    return lax.broadcasted_iota(jnp.int32, shape, dim)


_ANY = pl.BlockSpec(memory_space=pl.ANY)
_MESH = pl.DeviceIdType.MESH
N_CHIPS = 4


def _place():
    x, y, c = lax.axis_index("x"), lax.axis_index("y"), lax.axis_index("c")
    other_chips = [(1 - x, y), (x, 1 - y), (1 - x, 1 - y)]
    return x, y, c, other_chips


def _gather(x_shard, name):
    def body(x_ref, out_ref, send_sems, recv_sems, local_sem):
        x, y, c, chips = _place()
        me, sibling = (x, y, c), (x, y, 1 - c)

        def slot(p):
            return out_ref.at[4 * p[0] + 2 * p[1] + p[2]]

        def copy(k, block, to, src=None):
            return pltpu.make_async_remote_copy(
                src_ref=slot(block) if src is None else src, dst_ref=slot(block),
                send_sem=send_sems.at[k], recv_sem=recv_sems.at[k], device_id=to, device_id_type=_MESH)

        mine = pltpu.make_async_copy(x_ref, slot(me), local_sem)
        mine.start()
        first = [copy(0, me, sibling, src=x_ref)]
        first += [copy(1 + j, me, (*chip, c), src=x_ref) for j, chip in enumerate(chips)]
        for cp in first:
            cp.start()
        passed = [copy(4 + j, (*chip, c), sibling) for j, chip in enumerate(chips)]
        for j, chip in enumerate(chips):
            copy(1 + j, (*chip, c), me).wait_recv()
            passed[j].start()
        copy(0, sibling, me).wait_recv()
        for j, chip in enumerate(chips):
            copy(4 + j, (*chip, 1 - c), me).wait_recv()
        for cp in first + passed:
            cp.wait_send()
        mine.wait()

    return pl.pallas_call(
        body, name=name,
        out_shape=jax.ShapeDtypeStruct((N_DEV,) + tuple(x_shard.shape), x_shard.dtype),
        in_specs=[_ANY], out_specs=_ANY,
        scratch_shapes=[pltpu.SemaphoreType.DMA((7,)), pltpu.SemaphoreType.DMA((7,)), pltpu.SemaphoreType.DMA],
    )(x_shard)


def _swap_with_sibling(srcs, name):
    n = len(srcs)

    def body(*refs):
        src_refs, out_refs = refs[:n], refs[n:2 * n]
        send_sems, recv_sems = refs[2 * n:]
        x, y, c, _ = _place()
        cps = [pltpu.make_async_remote_copy(
            src_ref=src_refs[i].at[1 - c], dst_ref=out_refs[i], send_sem=send_sems.at[i], recv_sem=recv_sems.at[i],
            device_id=(x, y, 1 - c), device_id_type=_MESH) for i in range(n)]
        for cp in cps:
            cp.start()
        for cp in cps:
            cp.wait()

    return pl.pallas_call(
        body, name=name,
        out_shape=[jax.ShapeDtypeStruct(a.shape[1:], a.dtype) for a in srcs],
        in_specs=[_ANY] * n, out_specs=[_ANY] * n,
        scratch_shapes=[pltpu.SemaphoreType.DMA((n,)), pltpu.SemaphoreType.DMA((n,))],
    )(*srcs)


def _blocks_2d(r, c):
    if r % 128 == 0:
        return (128, c), r // 128, lambda i: (i, 0)
    return (r, 256), c // 256, lambda i: (0, i)


def _pair_add(src, recv, place, name):
    _, _, r, c = src.shape
    blk, nblk, at = _blocks_2d(r, c)

    def body(place_ref, a_ref, b_ref, q16_ref, own_ref):
        q = a_ref[...] + b_ref[...]
        q16_ref[...] = q.astype(BF16)

        @pl.when(pl.program_id(1) == place_ref[1])
        def _():
            own_ref[...] = q

    grid_spec = pltpu.PrefetchScalarGridSpec(
        num_scalar_prefetch=1, grid=(nblk, N_CHIPS),
        in_specs=[pl.BlockSpec((None, None) + blk, lambda i, j, pr: (pr[0], j) + at(i)),
                  pl.BlockSpec((None,) + blk, lambda i, j, pr: (j,) + at(i))],
        out_specs=[pl.BlockSpec((None,) + blk, lambda i, j, pr: (j,) + at(i)),
                   pl.BlockSpec(blk, lambda i, j, pr: at(i))])
    return pl.pallas_call(
        body, name=name, grid_spec=grid_spec,
        out_shape=[jax.ShapeDtypeStruct((N_CHIPS, r, c), BF16), jax.ShapeDtypeStruct((r, c), F32)],
        compiler_params=_cparams(("parallel", "arbitrary")),
    )(place, src, recv)


_HBM = pl.BlockSpec(memory_space=pltpu.HBM)
_SEM = pl.BlockSpec(memory_space=pltpu.SEMAPHORE)
_DATAFLOW = pltpu.SideEffectType.DATAFLOW_SIDE_EFFECTING


def _chip_copy(src_ref, land_ref, send_sem, recv_sem, k, chips, c, land):
    chip = chips[k]
    return pltpu.make_async_remote_copy(
        src_ref=src_ref.at[2 * chip[0] + chip[1]], dst_ref=land_ref.at[land],
        send_sem=send_sem, recv_sem=recv_sem, device_id=(*chip, c), device_id_type=_MESH)


def _exchange_chips_start(srcs, name):
    n = len(srcs)
    ncp = 3 * n

    def body(*refs):
        src_refs, land_refs = refs[:n], refs[n:2 * n]
        sems = refs[4 * n:4 * n + 2 * ncp]
        token = refs[-1]
        x, y, c, chips = _place()
        for i in range(n):
            for k in range(3):
                j = 3 * i + k
                _chip_copy(src_refs[i], land_refs[i], sems[j], sems[ncp + j], k, chips, c, 2 * x + y).start()
        token[...] = jnp.zeros_like(token)

    hbm = [pltpu.HBM(a.shape, a.dtype) for a in srcs]
    lands = [pltpu.with_memory_space_constraint(lax.empty(a.shape, a.dtype), pltpu.HBM) for a in srcs]
    res = pl.pallas_call(
        body, name=name,
        out_shape=(*hbm, *hbm, *([pltpu.SemaphoreType.DMA(())] * (2 * ncp)), jax.ShapeDtypeStruct((8, LANES), F32)),
        in_specs=[_HBM] * (2 * n),
        out_specs=(*([_HBM] * (2 * n)), *([_SEM] * (2 * ncp)), pl.BlockSpec(memory_space=pltpu.VMEM)),
        input_output_aliases={i: i for i in range(2 * n)},
        compiler_params=pltpu.CompilerParams(has_side_effects=_DATAFLOW),
    )(*[pltpu.with_memory_space_constraint(a, pltpu.HBM) for a in srcs], *lands)
    return list(res[2 * n:2 * n + 2 * ncp]), list(res[:n]), list(res[n:2 * n]), res[-1]


def _exchange_chips_wait(sems, srcs, lands, after, name):
    n = len(srcs)
    ncp = 3 * n

    def body(*refs):
        src_refs, land_refs = refs[:n], refs[n:2 * n]
        sem_refs = refs[2 * n:2 * n + 2 * ncp]
        x, y, c, chips = _place()
        for i in range(n):
            for k in range(3):
                j = 3 * i + k
                cp = _chip_copy(src_refs[i], land_refs[i], sem_refs[j], sem_refs[ncp + j], k, chips, c,
                                2 * chips[k][0] + chips[k][1])
                cp.wait_send()
                cp.wait_recv()

    hbm = [pltpu.HBM(a.shape, a.dtype) for a in srcs]
    res = pl.pallas_call(
        body, name=name, out_shape=(*hbm, *hbm),
        in_specs=[_HBM] * (2 * n) + [_SEM] * (2 * ncp) + [_ANY], out_specs=tuple([_HBM] * (2 * n)),
        input_output_aliases={i: i for i in range(2 * n)},
        compiler_params=pltpu.CompilerParams(has_side_effects=_DATAFLOW),
    )(*srcs, *lands, *sems, after)
    return list(res[n:2 * n])


def _swap_start(src, after, name):
    def body(src_ref, land_ref, after_ref, src_thru, land_thru, send_sem, recv_sem, token):
        x, y, c, _ = _place()
        pltpu.make_async_remote_copy(src_ref=src_ref.at[1 - c], dst_ref=land_ref, send_sem=send_sem,
                                     recv_sem=recv_sem, device_id=(x, y, 1 - c), device_id_type=_MESH).start()
        token[...] = jnp.zeros_like(token)

    land = pltpu.with_memory_space_constraint(lax.empty(src.shape[1:], src.dtype), pltpu.HBM)
    res = pl.pallas_call(
        body, name=name,
        out_shape=(pltpu.HBM(src.shape, src.dtype), pltpu.HBM(land.shape, land.dtype),
                   pltpu.SemaphoreType.DMA(()), pltpu.SemaphoreType.DMA(()), jax.ShapeDtypeStruct((8, LANES), F32)),
        in_specs=[_HBM, _HBM, _ANY],
        out_specs=(_HBM, _HBM, _SEM, _SEM, pl.BlockSpec(memory_space=pltpu.VMEM)),
        input_output_aliases={0: 0, 1: 1},
        compiler_params=pltpu.CompilerParams(has_side_effects=_DATAFLOW),
    )(pltpu.with_memory_space_constraint(src, pltpu.HBM), land, after)
    return [res[2], res[3]], res[0], res[1], res[-1]


def _swap_wait(sems, src, land, after, name):
    def body(src_ref, land_ref, send_sem, recv_sem, after_ref, src_out, land_out):
        x, y, c, _ = _place()
        cp = pltpu.make_async_remote_copy(src_ref=src_ref.at[1 - c], dst_ref=land_ref, send_sem=send_sem,
                                          recv_sem=recv_sem, device_id=(x, y, 1 - c), device_id_type=_MESH)
        cp.wait_send()
        cp.wait_recv()

    res = pl.pallas_call(
        body, name=name, out_shape=(pltpu.HBM(src.shape, src.dtype), pltpu.HBM(land.shape, land.dtype)),
        in_specs=[_HBM, _HBM, _SEM, _SEM, _ANY], out_specs=(_HBM, _HBM),
        input_output_aliases={0: 0, 1: 1},
        compiler_params=pltpu.CompilerParams(has_side_effects=_DATAFLOW),
    )(src, land, *sems, after)
    return res[0], res[1]


def _peer_copy(src_ref, land_ref, send_sem, recv_sem, k, place, land):
    x, y, c = place
    peer = (1 - x if k & 4 else x, 1 - y if k & 2 else y, 1 - c if k & 1 else c)
    return pltpu.make_async_remote_copy(
        src_ref=src_ref, dst_ref=land_ref.at[land], send_sem=send_sem, recv_sem=recv_sem,
        device_id=peer, device_id_type=_MESH)


def _gather_start(x_shard, after, name):
    npeer = N_DEV - 1

    def body(x_ref, land_ref, after_ref, x_thru, land_thru, *rest):
        sems, token = rest[:2 * npeer], rest[-1]
        x, y, c, _ = _place()
        for k in range(1, N_DEV):
            _peer_copy(x_ref, land_ref, sems[k - 1], sems[npeer + k - 1], k, (x, y, c), 4 * x + 2 * y + c).start()
        token[...] = jnp.zeros_like(token)

    land = pltpu.with_memory_space_constraint(lax.empty((N_DEV,) + tuple(x_shard.shape), x_shard.dtype), pltpu.HBM)
    res = pl.pallas_call(
        body, name=name,
        out_shape=(pltpu.HBM(x_shard.shape, x_shard.dtype), pltpu.HBM(land.shape, land.dtype),
                   *([pltpu.SemaphoreType.DMA(())] * (2 * npeer)), jax.ShapeDtypeStruct((8, LANES), F32)),
        in_specs=[_HBM, _HBM, _ANY],
        out_specs=(_HBM, _HBM, *([_SEM] * (2 * npeer)), pl.BlockSpec(memory_space=pltpu.VMEM)),
        input_output_aliases={0: 0, 1: 1},
        compiler_params=pltpu.CompilerParams(has_side_effects=_DATAFLOW),
    )(pltpu.with_memory_space_constraint(x_shard, pltpu.HBM), land, after)
    return list(res[2:2 + 2 * npeer]), res[0], res[1], res[-1]


def _gather_wait(sems, src, land, after, name):
    npeer = N_DEV - 1

    def body(x_ref, land_ref, *rest):
        sem_refs = rest[:2 * npeer]
        x, y, c, _ = _place()
        for k in range(1, N_DEV):
            peer_index = (4 * x + 2 * y + c) ^ k
            cp = _peer_copy(x_ref, land_ref, sem_refs[k - 1], sem_refs[npeer + k - 1], k, (x, y, c), peer_index)
            cp.wait_send()
            cp.wait_recv()

    res = pl.pallas_call(
        body, name=name, out_shape=(pltpu.HBM(src.shape, src.dtype), pltpu.HBM(land.shape, land.dtype)),
        in_specs=[_HBM, _HBM] + [_SEM] * (2 * npeer) + [_ANY], out_specs=(_HBM, _HBM),
        input_output_aliases={0: 0, 1: 1},
        compiler_params=pltpu.CompilerParams(has_side_effects=_DATAFLOW),
    )(src, land, *sems, after)
    return res[1]


def _prenorm_inproj(x2, w, wt_qkv, b_qkv, wt_f, b_f):
    t = x2.shape[0]
    tm = min(512, t)
    n = wt_qkv.shape[0]
    tn = D_MODEL

    def body(x_ref, w_ref, wq_ref, bq_ref, wf_ref, bf_ref, h_ref, qkv_ref, zf_ref):
        x = x_ref[...]
        r = lax.rsqrt(jnp.mean(x * x, axis=-1, keepdims=True) + NORM_EPS)
        h = (x * r * w_ref[...]).astype(BF16)
        h_ref[...] = h
        for j in range(n // tn):
            cols = slice(j * tn, (j + 1) * tn)
            qkv_ref[:, cols] = (_dot_nt(h, wq_ref[cols, :]) + bq_ref[:, cols]).astype(BF16)
        zf_ref[...] = _dot_nt(h, wf_ref[...]) + bf_ref[...]

    row = lambda c: pl.BlockSpec((tm, c), lambda i: (i, 0))
    whole = lambda a: pl.BlockSpec(a.shape, lambda i: (0, 0))
    return pl.pallas_call(
        body, name="prenorm_inproj_qkv", grid=(t // tm,),
        in_specs=[row(D_MODEL), whole(w), whole(wt_qkv), whole(b_qkv), whole(wt_f), whole(b_f)],
        out_specs=[row(D_MODEL), row(n), row(LANES)],
        out_shape=[jax.ShapeDtypeStruct((t, D_MODEL), BF16), jax.ShapeDtypeStruct((t, n), BF16),
                   jax.ShapeDtypeStruct((t, LANES), F32)],
        compiler_params=_cparams(("parallel",), vmem_mb=48),
    )(x2, w, wt_qkv, b_qkv, wt_f, b_f)


def _mm_bias(a, bt, bias, out_dtype, name):
    m, k = a.shape
    n = bt.shape[0]
    tm = min(512, m)
    tn = min(1024, n)

    def body(a_ref, bt_ref, bias_ref, o_ref):
        aa = a_ref[...]
        for j in range(n // tn):
            cols = slice(j * tn, (j + 1) * tn)
            o_ref[:, cols] = (_dot_nt(aa, bt_ref[cols, :]) + bias_ref[:, cols]).astype(o_ref.dtype)

    return pl.pallas_call(
        body, name=name, grid=(m // tm,),
        in_specs=[pl.BlockSpec((tm, k), lambda i: (i, 0)), pl.BlockSpec((n, k), lambda i: (0, 0)),
                  pl.BlockSpec((1, n), lambda i: (0, 0))],
        out_specs=pl.BlockSpec((tm, n), lambda i: (i, 0)),
        out_shape=jax.ShapeDtypeStruct((m, n), out_dtype),
        compiler_params=_cparams(("parallel",), vmem_mb=48),
    )(a, bt, bias)


def _mm_tn(a, b, name, after=None):
    t, m = a.shape
    n = b.shape[1]
    tm = min(1024, m)
    tk = min(2048, t)
    deps = [] if after is None else [after]

    def body(a_ref, b_ref, *refs):
        o_ref, s_ref = refs[len(deps):]
        kk = pl.program_id(1)

        @pl.when(kk == 0)
        def _():
            o_ref[...] = jnp.zeros_like(o_ref)
            s_ref[...] = jnp.zeros_like(s_ref)

        aa = a_ref[...]
        o_ref[...] += _dot_tn(aa, b_ref[...])
        s_ref[0:1, :] += jnp.sum(aa.astype(F32), axis=0, keepdims=True)

    return pl.pallas_call(
        body, name=name, grid=(m // tm, t // tk),
        in_specs=[pl.BlockSpec((tk, tm), lambda i, kk: (kk, i)), pl.BlockSpec((tk, n), lambda i, kk: (kk, 0))]
        + [pl.BlockSpec(d.shape, lambda i, kk: (0, 0)) for d in deps],
        out_specs=[pl.BlockSpec((tm, n), lambda i, kk: (i, 0)), pl.BlockSpec((8, tm), lambda i, kk: (0, i))],
        out_shape=[jax.ShapeDtypeStruct((m, n), F32), jax.ShapeDtypeStruct((8, m), F32)],
        compiler_params=_cparams(("parallel", "arbitrary"), vmem_mb=48),
    )(a, b, *deps)


def _fgate_fwd(zf3):
    b, s, _ = zf3.shape
    tb = SCAN_TILE
    nb = s // tb

    def body(z_ref, cexp_ref, crow_ref):
        tri = (_iota((tb, tb), 1) <= _iota((tb, tb), 0)).astype(BF16)
        expand = ((_iota((LANES, D_MODEL), 1) >> 6) == _iota((LANES, D_MODEL), 0)).astype(BF16)
        carry = jnp.zeros((1, LANES), F32)
        for i in range(nb):
            rows = slice(i * tb, (i + 1) * tb)
            z = z_ref[rows, :]
            lf = jnp.minimum(z, 0.0) - jnp.log1p(jnp.exp(-jnp.abs(z)))
            cb = sum(_dot(tri, part) for part in _split3(lf)) + carry
            carry = cb[tb - 1:tb, :]
            cexp_ref[rows, :] = sum(_dot(part, expand) for part in _split3(cb))
            crow_ref[:, rows] = cb.T[0:HEADS, :]

    return pl.pallas_call(
        body, name="fgate_fwd", grid=(b,),
        in_specs=[pl.BlockSpec((None, s, LANES), lambda i: (i, 0, 0))],
        out_specs=[pl.BlockSpec((None, s, D_MODEL), lambda i: (i, 0, 0)),
                   pl.BlockSpec((None, HEADS, s), lambda i: (i, 0, 0))],
        out_shape=[jax.ShapeDtypeStruct((b, s, D_MODEL), F32), jax.ShapeDtypeStruct((b, HEADS, s), F32)],
        compiler_params=_cparams(("parallel",)),
    )(zf3)


def _fgate_bwd(dc3, zf3):
    b, s, _ = zf3.shape
    tb = SCAN_TILE
    nb = s // tb

    def body(dc_ref, z_ref, o_ref):
        tri = (_iota((tb, tb), 1) >= _iota((tb, tb), 0)).astype(BF16)
        carry = jnp.zeros((1, LANES), F32)
        for i in reversed(range(nb)):
            rows = slice(i * tb, (i + 1) * tb)
            dlf = sum(_dot(tri, part) for part in _split3(dc_ref[rows, :])) + carry
            carry = dlf[0:1, :]
            o_ref[rows, :] = (dlf * _sigmoid(-z_ref[rows, :])).astype(BF16)

    return pl.pallas_call(
        body, name="fgate_bwd", grid=(b,),
        in_specs=[pl.BlockSpec((None, s, LANES), lambda i: (i, 0, 0)),
                  pl.BlockSpec((None, s, LANES), lambda i: (i, 0, 0))],
        out_specs=pl.BlockSpec((s, LANES), lambda i: (i, 0)),
        out_shape=jax.ShapeDtypeStruct((b * s, LANES), BF16),
        compiler_params=_cparams(("parallel",)),
    )(dc3, zf3)


def _spare(hh):
    return HEAD_DIM if hh == 0 else 0


def _put_cols(tile, mine, cols, first):
    lane = _iota((1, LANES), 1)
    out = jnp.where(mine, tile, jnp.zeros((), tile.dtype))
    for j, c in enumerate(cols):
        out = jnp.where(lane == first + j, c, out)
    return out


def _put_rows(tile, mine, rows, first):
    sub = _iota((LANES, 1), 0)
    out = jnp.where(mine, tile, jnp.zeros((), tile.dtype))
    for j, r in enumerate(rows):
        out = jnp.where(sub == first + j, r, out)
    return out


def _transpose_bf16(a):
    return a.astype(F32).T.astype(BF16)


def _attn_fwd(qkv3, cexp3, crow, zrest3):
    b, s, _ = qkv3.shape
    ta = ATT_TILE_FWD
    nq = s // ta
    hd = HEAD_DIM
    crow5 = crow.reshape(b, HEAD_PAIRS, 2, nq, ta)

    def body(qkv_ref, cq_ref, ck_ref, g_ref, y_ref, lse_ref, ga_ref, kt_scr, v_scr):
        lane = _iota((1, LANES), 1)
        sub = _iota((LANES, 1), 0)
        lane_mine = (lane < hd, lane >= hd)
        sub_mine = (sub < hd, sub >= hd)
        causal = _iota((ta, ta), 0) >= _iota((ta, ta), 1)
        one = jnp.ones((), BF16)

        for kj in range(nq):
            rows = slice(kj * ta, (kj + 1) * ta)
            kt = _transpose_bf16(qkv_ref[rows, LANES:2 * LANES])
            v = qkv_ref[rows, 2 * LANES:3 * LANES]
            for hh in range(2):
                ck = list(_split3(-ck_ref[hh, kj:kj + 1, :]))
                kt_scr[hh, kj] = _put_rows(kt, sub_mine[hh], [one, one, one] + ck, _spare(hh))
                v_scr[hh, kj] = _put_cols(v, lane_mine[hh], [one], _spare(hh))

        for qi in range(nq):
            rows = slice(qi * ta, (qi + 1) * ta)
            q = qkv_ref[rows, 0:LANES] * 0.125
            cq = cq_ref[rows, :]
            qh = [_put_cols(q, lane_mine[hh], list(_split3(cq[:, hh * hd:hh * hd + 1])) + [one, one, one], _spare(hh))
                  for hh in range(2)]
            st = [(jnp.full((ta, 1), MASK_VALUE, F32), jnp.zeros((ta, LANES), F32))] * 2
            for kj in range(qi + 1):
                for hh in range(2):
                    m, acc = st[hh]
                    sc = _dot(qh[hh], kt_scr[hh, kj])
                    if kj == qi:
                        sc = jnp.where(causal, sc, MASK_VALUE)
                    mn = jnp.maximum(m, jnp.max(sc, axis=-1, keepdims=True))
                    p = jnp.exp(sc - mn).astype(BF16)
                    st[hh] = (mn, jnp.exp(m - mn) * acc + _dot(p, v_scr[hh, kj]))
            (ma, acca), (mb, accb) = st
            la = acca[:, hd:hd + 1]
            lb = accb[:, 0:1]
            y = jnp.where(lane_mine[0], acca * (1.0 / la), accb * (1.0 / lb))
            lse = jnp.where(lane_mine[0], ma + jnp.log(la), mb + jnp.log(lb)).T
            lse_ref[0, qi:qi + 1, :] = lse[0:1, :]
            lse_ref[1, qi:qi + 1, :] = lse[hd:hd + 1, :]
            y_ref[rows, :] = y
            g = g_ref[rows, :].astype(F32)
            ga_ref[rows, :] = (y * (g * _sigmoid(g))).astype(BF16)

    blk = lambda w: pl.BlockSpec((None, s, w), lambda i, p: (i, 0, p))
    rows5 = pl.BlockSpec((None, None, 2, nq, ta), lambda i, p: (i, p, 0, 0, 0))
    yatt3, lse5, ga = pl.pallas_call(
        body, name="attn_fwd", grid=(b, HEAD_PAIRS),
        in_specs=[blk(3 * LANES), blk(LANES), rows5, blk(LANES)],
        out_specs=[blk(LANES), rows5, pl.BlockSpec((s, LANES), lambda i, p: (i, p))],
        out_shape=[jax.ShapeDtypeStruct((b, s, D_MODEL), F32),
                   jax.ShapeDtypeStruct((b, HEAD_PAIRS, 2, nq, ta), F32),
                   jax.ShapeDtypeStruct((b * s, D_MODEL), BF16)],
        scratch_shapes=[pltpu.VMEM((2, nq, LANES, ta), BF16), pltpu.VMEM((2, nq, ta, LANES), BF16)],
        compiler_params=_cparams(("parallel", "parallel")),
    )(qkv3, cexp3, crow5, zrest3)
    return yatt3, lse5.reshape(b, HEADS, s), ga


def _attn_bwd(qkv3, do3, y3, lse, crow, cexp3):
    b, s, _ = qkv3.shape
    ta = ATT_TILE_BWD
    nq = s // ta
    hd = HEAD_DIM
    lse5 = lse.reshape(b, HEAD_PAIRS, 2, nq, ta)
    crow5 = crow.reshape(b, HEAD_PAIRS, 2, nq, ta)

    def body(qkv_ref, do_ref, y_ref, lse_ref, crow_ref, cexp_ref, dqkv_ref, dc_ref,
             qa_scr, doa_scr, qst_scr, dot_scr, kt_scr, vt_scr, dq_scr, rs_scr):
        pair = pl.program_id(1)
        lane = _iota((1, LANES), 1)
        sub = _iota((LANES, 1), 0)
        lane_mine = (lane < hd, lane >= hd)
        sub_mine = (sub < hd, sub >= hd)
        causal = _iota((ta, ta), 0) >= _iota((ta, ta), 1)
        one = jnp.ones((), BF16)
        zero = jnp.zeros((), BF16)

        @pl.when(pair == 0)
        def _():
            dc_ref[...] = jnp.zeros_like(dc_ref)

        for i in range(nq):
            rows = slice(i * ta, (i + 1) * ta)
            qs = qkv_ref[rows, 0:LANES] * 0.125
            qst = _transpose_bf16(qs)
            kt = _transpose_bf16(qkv_ref[rows, LANES:2 * LANES])
            vt = _transpose_bf16(qkv_ref[rows, 2 * LANES:3 * LANES])
            do = do_ref[rows, :]
            dof = do.astype(F32)
            dot = dof.T.astype(BF16)
            pr = y_ref[rows, :] * dof
            cq = cexp_ref[rows, :]
            lse_c = jnp.where(sub == 0, lse_ref[0, i:i + 1, :],
                              jnp.where(sub == 1, lse_ref[1, i:i + 1, :], 0.0)).T
            for hh in range(2):
                sp = _spare(hh)
                dsum = jnp.sum(jnp.where(lane_mine[hh], pr, 0.0), axis=-1, keepdims=True)
                bias = cq[:, hh * hd:hh * hd + 1] - lse_c[:, hh:hh + 1]
                qa_scr[hh, i] = _put_cols(qs, lane_mine[hh], list(_split3(bias)) + [one, one, one], sp)
                doa_scr[hh, i] = _put_cols(do, lane_mine[hh], list(_split3(-dsum)), sp)
                qst_scr[hh, i] = jnp.where(sub_mine[hh], qst, zero)
                dot_scr[hh, i] = jnp.where(sub_mine[hh], dot, zero)
                ck = list(_split3(-crow_ref[hh, i:i + 1, :]))
                kt_scr[hh, i] = _put_rows(kt, sub_mine[hh], [one, one, one] + ck, sp)
                vt_scr[hh, i] = _put_rows(vt, sub_mine[hh], [one, one, one], sp)
            dq_scr[i] = jnp.zeros((ta, LANES), F32)
            rs_scr[i] = jnp.zeros((ta, LANES), F32)

        for kj in range(nq):
            krows = slice(kj * ta, (kj + 1) * ta)
            k = qkv_ref[krows, LANES:2 * LANES]
            km = (jnp.where(lane_mine[0], k, zero), jnp.where(lane_mine[1], k, zero))
            dkt = jnp.zeros((LANES, ta), F32)
            dvt = jnp.zeros((LANES, ta), F32)
            dcp = [jnp.zeros((8, ta), F32), jnp.zeros((8, ta), F32)]
            for qi in range(kj, nq):
                dq = jnp.zeros((ta, LANES), F32)
                rs = []
                for hh in range(2):
                    sc = _dot(qa_scr[hh, qi], kt_scr[hh, kj])
                    if qi == kj:
                        sc = jnp.where(causal, sc, MASK_VALUE)
                    p = jnp.exp(sc)
                    dsf = p * _dot(doa_scr[hh, qi], vt_scr[hh, kj])
                    dcp[hh] = dcp[hh] + jnp.sum(dsf.reshape(ta // 8, 8, ta), axis=0)
                    rs.append(jnp.sum(dsf, axis=-1, keepdims=True))
                    ds = dsf.astype(BF16)
                    dq = dq + _dot(ds, km[hh])
                    dkt = dkt + _dot(qst_scr[hh, qi], ds)
                    dvt = dvt + _dot(dot_scr[hh, qi], p.astype(BF16))
                dq_scr[qi] += dq
                rs_scr[qi] += jnp.where(lane == 0, rs[0], jnp.where(lane == 1, rs[1], 0.0))
            dqkv_ref[krows, LANES:2 * LANES] = dkt.T.astype(BF16)
            dqkv_ref[krows, 2 * LANES:3 * LANES] = dvt.T.astype(BF16)
            dca = jnp.sum(dcp[0], axis=0, keepdims=True)
            dcb = jnp.sum(dcp[1], axis=0, keepdims=True)
            dcs = jnp.where(sub == 0, dca, jnp.where(sub == 1, dcb, 0.0)).T
            dc_ref[krows, :] += (jnp.where(lane == 2 * pair, -dcs[:, 0:1], 0.0)
                                 + jnp.where(lane == 2 * pair + 1, -dcs[:, 1:2], 0.0))
        for qi in range(nq):
            rows = slice(qi * ta, (qi + 1) * ta)
            dqkv_ref[rows, 0:LANES] = (dq_scr[qi] * 0.125).astype(BF16)
            rq = rs_scr[qi]
            dc_ref[rows, :] += (jnp.where(lane == 2 * pair, rq[:, 0:1], 0.0)
                                + jnp.where(lane == 2 * pair + 1, rq[:, 1:2], 0.0))

    blk = lambda w: pl.BlockSpec((None, s, w), lambda i, p: (i, 0, p))
    rows5 = pl.BlockSpec((None, None, 2, nq, ta), lambda i, p: (i, p, 0, 0, 0))
    by_rows = lambda: pltpu.VMEM((2, nq, ta, LANES), BF16)
    by_cols = lambda: pltpu.VMEM((2, nq, LANES, ta), BF16)
    return pl.pallas_call(
        body, name="attn_bwd", grid=(b, HEAD_PAIRS),
        in_specs=[blk(3 * LANES), blk(LANES), blk(LANES), rows5, rows5, blk(LANES)],
        out_specs=[pl.BlockSpec((s, 3 * LANES), lambda i, p: (i, p)),
                   pl.BlockSpec((None, s, LANES), lambda i, p: (i, 0, 0))],
        out_shape=[jax.ShapeDtypeStruct((b * s, 3 * D_MODEL), BF16), jax.ShapeDtypeStruct((b, s, LANES), F32)],
        scratch_shapes=[by_rows(), by_rows(), by_cols(), by_cols(), by_cols(), by_cols(),
                        pltpu.VMEM((nq, ta, LANES), F32), pltpu.VMEM((nq, ta, LANES), F32)],
        compiler_params=_cparams(("parallel", "arbitrary")),
    )(qkv3, do3, y3, lse5, crow5, cexp3)


def _shifted(v, ks, rows, s):
    low = rows[0:8, :]
    out = []
    for k in ks:
        r = pltpu.roll(v, k % s, 0)
        if k > 0:
            out.append(jnp.concatenate([jnp.where(low >= k, r[0:8, :], 0.0), r[8:, :]], axis=0))
        else:
            out.append(jnp.concatenate([r[:s - 8, :], jnp.where(low < 8 + k, r[s - 8:, :], 0.0)], axis=0))
    return out


def _rnn_common(xr, cw_ref, cb_ref, bda_ref, bdx_ref, ba_ref, bx_ref, lam_ref, s):
    rows = _iota((s, LANES), 0)
    x1, x2, x3 = _shifted(xr, (1, 2, 3), rows, s)
    xc = cb_ref[...] + cw_ref[0:1, :] * x3
    xc = xc + cw_ref[1:2, :] * x2
    xc = xc + cw_ref[2:3, :] * x1
    xc = xc + cw_ref[3:4, :] * xr
    xcb = xc.astype(BF16)
    r = _sigmoid(_dot(xcb, bda_ref[...]) + ba_ref[...])
    i = _sigmoid(_dot(xcb, bdx_ref[...]) + bx_ref[...])
    sp = _softplus(-lam_ref[...])
    log_a = (-RG_C * r) * sp
    a = jnp.exp(log_a)
    a2 = a * a
    sq = jnp.sqrt(jnp.maximum(_one_minus_exp(log_a + log_a, a2), 0.0))
    return rows, (x1, x2, x3), xc, xcb, r, i, sp, a, a2, sq


def _scan_down(a, u, rows, s, s1, s2):
    low = rows & 7
    for sh in (1, 2, 4):
        keep = low >= sh
        u = u + a * jnp.where(keep, pltpu.roll(u, sh, 0), 0.0)
        a = a * jnp.where(keep, pltpu.roll(a, sh, 0), 1.0)
    ng = s // 8
    s1[...] = a
    s2[...] = u
    at = s1[pl.ds(7, ng, stride=8), :]
    ut = s2[pl.ds(7, ng, stride=8), :]
    grow = _iota((ng, LANES), 0)
    sh = 1
    while sh < ng:
        keep = grow >= sh
        ut = ut + at * jnp.where(keep, pltpu.roll(ut, sh, 0), 0.0)
        if sh * 2 < ng:
            at = at * jnp.where(keep, pltpu.roll(at, sh, 0), 1.0)
        sh *= 2
    h_in = jnp.where(grow >= 1, pltpu.roll(ut, 1, 0), 0.0)
    for k in range(8):
        s1[pl.ds(k, ng, stride=8), :] = h_in
    return u + a * s1[...]


def _scan_up(a, g, rows, s, s1, s2):
    low = rows & 7
    for sh in (1, 2, 4):
        keep = low < 8 - sh
        g = g + a * jnp.where(keep, pltpu.roll(g, s - sh, 0), 0.0)
        a = a * jnp.where(keep, pltpu.roll(a, s - sh, 0), 1.0)
    ng = s // 8
    s1[...] = a
    s2[...] = g
    at = s1[pl.ds(0, ng, stride=8), :]
    gt = s2[pl.ds(0, ng, stride=8), :]
    grow = _iota((ng, LANES), 0)
    sh = 1
    while sh < ng:
        keep = grow < ng - sh
        gt = gt + at * jnp.where(keep, pltpu.roll(gt, ng - sh, 0), 0.0)
        if sh * 2 < ng:
            at = at * jnp.where(keep, pltpu.roll(at, ng - sh, 0), 1.0)
        sh *= 2
    g_in = jnp.where(grow < ng - 1, pltpu.roll(gt, ng - 1, 0), 0.0)
    for k in range(8):
        s1[pl.ds(k, ng, stride=8), :] = g_in
    return g + a * s1[...]


def _rnn_specs(s):
    blk = lambda off: pl.BlockSpec((None, s, LANES), lambda cb, i: (i, 0, off + cb))
    vec = lambda r: pl.BlockSpec((r, LANES), lambda cb, i: (0, cb))
    mat = pl.BlockSpec((None, LANES, LANES), lambda cb, i: (cb, 0, 0))
    return blk, vec, mat


def _rnn_fwd(zrest3, conv_w, conv_b, bda, bdx, ba, bx, lam):
    b, s, _ = zrest3.shape

    def body(xr_ref, g_ref, cw_ref, cb_ref, bda_ref, bdx_ref, ba_ref, bx_ref, lam_ref, h_ref, gr_ref, s1, s2):
        xr = xr_ref[...].astype(F32)
        rows, _, xc, _, _, i, _, a, _, sq = _rnn_common(
            xr, cw_ref, cb_ref, bda_ref, bdx_ref, ba_ref, bx_ref, lam_ref, s)
        h = _scan_down(a, sq * (i * xc), rows, s, s1, s2)
        h_ref[...] = h
        g = g_ref[...].astype(F32)
        gr_ref[...] = (h * (g * _sigmoid(g))).astype(BF16)

    blk, vec, mat = _rnn_specs(s)
    return pl.pallas_call(
        body, name="rnn_fwd", grid=(N_CBLK, b),
        in_specs=[blk(N_CBLK), blk(2 * N_CBLK), vec(CONV_W), vec(1), mat, mat, vec(1), vec(1), vec(1)],
        out_specs=[blk(0), pl.BlockSpec((s, LANES), lambda cb, i: (i, cb))],
        out_shape=[jax.ShapeDtypeStruct((b, s, D_MODEL), F32), jax.ShapeDtypeStruct((b * s, D_MODEL), BF16)],
        scratch_shapes=[pltpu.VMEM((s, LANES), F32), pltpu.VMEM((s, LANES), F32)],
        compiler_params=_cparams(("parallel", "parallel")),
    )(zrest3, zrest3, conv_w, conv_b, bda, bdx, ba, bx, lam)


def _rnn_bwd(zrest3, h3, dh3, conv_w, conv_b, bda, bdx, ba, bx, lam):
    b, s, _ = zrest3.shape

    def body(xr_ref, h_ref, dh_ref, cw_ref, cb_ref, bda_ref, bdx_ref, ba_ref, bx_ref, lam_ref,
             dxr_ref, pv_ref, dbd_ref, s1, s2):
        @pl.when(pl.program_id(1) == 0)
        def _():
            pv_ref[...] = jnp.zeros_like(pv_ref)
            dbd_ref[...] = jnp.zeros_like(dbd_ref)

        xr = xr_ref[...].astype(F32)
        rows, (x1, x2, x3), xc, xcb, r, i, sp, a, a2, sq = _rnn_common(
            xr, cw_ref, cb_ref, bda_ref, bdx_ref, ba_ref, bx_ref, lam_ref, s)
        (a_next,) = _shifted(a, (-1,), rows, s)
        g = _scan_up(a_next, dh_ref[...], rows, s, s1, s2)
        (hp,) = _shifted(h_ref[...], (1,), rows, s)
        da = g * hp
        dsq = g * (i * xc)
        di = g * (sq * xc)
        dxc = g * (sq * i)
        dlog = da * a - dsq * (a2 / sq)
        dr = dlog * (-RG_C * sp)
        dpr = dr * (r * (1.0 - r))
        dpi = di * (i * (1.0 - i))
        dprb = dpr.astype(BF16)
        dpib = dpi.astype(BF16)
        dxc = dxc + _dot_nt(dprb, bda_ref[...]) + _dot_nt(dpib, bdx_ref[...])

        up1, up2, up3 = _shifted(dxc, (-1, -2, -3), rows, s)
        dxr = cw_ref[3:4, :] * dxc + cw_ref[2:3, :] * up1 + cw_ref[1:2, :] * up2 + cw_ref[0:1, :] * up3
        dxr_ref[...] = dxr.astype(BF16)

        def colsum(v):
            return jnp.sum(v, axis=0, keepdims=True)

        pv_ref[0:1, :] += colsum(dxc * x3)
        pv_ref[1:2, :] += colsum(dxc * x2)
        pv_ref[2:3, :] += colsum(dxc * x1)
        pv_ref[3:4, :] += colsum(dxc * xr)
        pv_ref[4:5, :] += colsum(dxc)
        pv_ref[5:6, :] += colsum(dpr)
        pv_ref[6:7, :] += colsum(dpi)
        pv_ref[7:8, :] += colsum(dlog * r) * (RG_C * _sigmoid(-lam_ref[...]))
        dbd_ref[0] += _dot_tn(xcb, dprb)
        dbd_ref[1] += _dot_tn(xcb, dpib)

    blk, vec, mat = _rnn_specs(s)
    hblk = pl.BlockSpec((None, s, LANES), lambda cb, i: (i, 0, cb))
    return pl.pallas_call(
        body, name="rnn_bwd", grid=(N_CBLK, b),
        in_specs=[blk(N_CBLK), hblk, hblk, vec(CONV_W), vec(1), mat, mat, vec(1), vec(1), vec(1)],
        out_specs=[pl.BlockSpec((s, LANES), lambda cb, i: (i, cb)), pl.BlockSpec((8, LANES), lambda cb, i: (0, cb)),
                   pl.BlockSpec((None, 2, LANES, LANES), lambda cb, i: (cb, 0, 0, 0))],
        out_shape=[jax.ShapeDtypeStruct((b * s, D_MODEL), BF16), jax.ShapeDtypeStruct((8, D_MODEL), F32),
                   jax.ShapeDtypeStruct((N_CBLK, 2, LANES, LANES), F32)],
        scratch_shapes=[pltpu.VMEM((s, LANES), F32), pltpu.VMEM((s, LANES), F32)],
        compiler_params=_cparams(("parallel", "arbitrary")),
    )(zrest3, h3, dh3, conv_w, conv_b, bda, bdx, ba, bx, lam)


def _branch_merge(ga, gr, wa, wr, zrest):
    t = ga.shape[0]
    tm = min(512, t)
    tn = D_MODEL

    def body(ga_ref, gr_ref, wa_ref, wr_ref, mga_ref, mgr_ref, ya_ref, yr_ref, m_ref):
        ya = _dot(ga_ref[...], wa_ref[...])
        yr = _dot(gr_ref[...], wr_ref[...])
        ya_ref[...] = ya.astype(BF16)
        yr_ref[...] = yr.astype(BF16)
        m_ref[...] = (_sigmoid(mga_ref[...].astype(F32)) * ya + _sigmoid(mgr_ref[...].astype(F32)) * yr).astype(BF16)

    nj = D_MODEL // tn
    act = pl.BlockSpec((tm, D_MODEL), lambda i, j: (i, 0))
    wgt = pl.BlockSpec((D_MODEL, tn), lambda i, j: (0, j))
    out = pl.BlockSpec((tm, tn), lambda i, j: (i, j))
    return pl.pallas_call(
        body, name="branch_merge", grid=(t // tm, nj),
        in_specs=[act, act, wgt, wgt, pl.BlockSpec((tm, tn), lambda i, j: (i, 3 * nj + j)),
                  pl.BlockSpec((tm, tn), lambda i, j: (i, 4 * nj + j))],
        out_specs=[out, out, out],
        out_shape=[jax.ShapeDtypeStruct((t, D_MODEL), BF16), jax.ShapeDtypeStruct((t, D_MODEL), BF16),
                   jax.ShapeDtypeStruct((t, D_MODEL), BF16)],
        compiler_params=_cparams(("parallel", "parallel")),
    )(ga, gr, wa, wr, zrest, zrest)


def _out_loss(m, wout, x2, tgt2, wpost):
    t = m.shape[0]
    tm = min(512, t)

    def body(m_ref, w_ref, x_ref, t_ref, wp_ref, dy_ref, do_ref, acc_ref):
        @pl.when(pl.program_id(0) == 0)
        def _():
            acc_ref[...] = jnp.zeros_like(acc_ref)

        o = _dot(m_ref[...], w_ref[...])
        r2 = lax.rsqrt(jnp.mean(o * o, axis=-1, keepdims=True) + NORM_EPS)
        n = o * r2
        wp = wp_ref[...]
        err = (x_ref[...] + n * wp) - t_ref[...]
        dy = err * (1.0 / D_MODEL)
        dn = dy * wp
        do = r2 * (dn - n * jnp.mean(dn * n, axis=-1, keepdims=True))
        dy_ref[...] = dy
        do_ref[...] = do.astype(BF16)
        acc_ref[0:1, :] += jnp.sum(dy * n, axis=0, keepdims=True)
        acc_ref[1:2, :] += jnp.sum(err * err, axis=0, keepdims=True)

    row = pl.BlockSpec((tm, D_MODEL), lambda i: (i, 0))
    return pl.pallas_call(
        body, name="out_loss", grid=(t // tm,),
        in_specs=[row, pl.BlockSpec((D_MODEL, D_MODEL), lambda i: (0, 0)), row, row,
                  pl.BlockSpec((1, D_MODEL), lambda i: (0, 0))],
        out_specs=[row, row, pl.BlockSpec((8, D_MODEL), lambda i: (0, 0))],
        out_shape=[jax.ShapeDtypeStruct((t, D_MODEL), F32), jax.ShapeDtypeStruct((t, D_MODEL), BF16),
                   jax.ShapeDtypeStruct((8, D_MODEL), F32)],
        compiler_params=_cparams(("arbitrary",)),
    )(m, wout, x2, tgt2, wpost)


def _merge_bwd(do, wout, zrest, ya, yr):
    t = do.shape[0]
    tm = min(512, t)
    tn = D_MODEL
    nj = D_MODEL // tn

    def body(do_ref, w_ref, mga_ref, mgr_ref, ya_ref, yr_ref, dya_ref, dyr_ref, dmga_ref, dmgr_ref):
        dm = _dot_nt(do_ref[...], w_ref[...])
        sa = _sigmoid(mga_ref[...].astype(F32))
        sr = _sigmoid(mgr_ref[...].astype(F32))
        dya_ref[...] = (dm * sa).astype(BF16)
        dyr_ref[...] = (dm * sr).astype(BF16)
        dmga_ref[...] = (dm * ya_ref[...].astype(F32) * (sa * (1.0 - sa))).astype(BF16)
        dmgr_ref[...] = (dm * yr_ref[...].astype(F32) * (sr * (1.0 - sr))).astype(BF16)

    out = pl.BlockSpec((tm, tn), lambda i, j: (i, j))
    bf = jax.ShapeDtypeStruct((t, D_MODEL), BF16)
    return pl.pallas_call(
        body, name="merge_bwd", grid=(t // tm, nj),
        in_specs=[pl.BlockSpec((tm, D_MODEL), lambda i, j: (i, 0)), pl.BlockSpec((tn, D_MODEL), lambda i, j: (j, 0)),
                  pl.BlockSpec((tm, tn), lambda i, j: (i, 3 * nj + j)),
                  pl.BlockSpec((tm, tn), lambda i, j: (i, 4 * nj + j)), out, out],
        out_specs=[out, out, out, out],
        out_shape=[bf, bf, bf, bf],
        compiler_params=_cparams(("parallel", "parallel")),
    )(do, wout, zrest, zrest, ya, yr)


def _branch_bwd(dya, dyr, wa, wr, zrest, yatt, ylru):
    t = dya.shape[0]
    tm = min(512, t)
    tn = D_MODEL
    nj = D_MODEL // tn

    def body(dya_ref, dyr_ref, wa_ref, wr_ref, ga_ref, gr_ref, ya_ref, yl_ref,
             dyatt_ref, dga_ref, dyl_ref, dgr_ref):
        dga = _dot_nt(dya_ref[...], wa_ref[...])
        dgr = _dot_nt(dyr_ref[...], wr_ref[...])
        g = ga_ref[...].astype(F32)
        sg = _sigmoid(g)
        dyatt_ref[...] = (dga * (g * sg)).astype(BF16)
        dga_ref[...] = (dga * ya_ref[...] * (sg * (1.0 + g * (1.0 - sg)))).astype(BF16)
        g = gr_ref[...].astype(F32)
        sg = _sigmoid(g)
        dyl_ref[...] = dgr * (g * sg)
        dgr_ref[...] = (dgr * yl_ref[...] * (sg * (1.0 + g * (1.0 - sg)))).astype(BF16)

    act = pl.BlockSpec((tm, D_MODEL), lambda i, j: (i, 0))
    wgt = pl.BlockSpec((tn, D_MODEL), lambda i, j: (j, 0))
    out = pl.BlockSpec((tm, tn), lambda i, j: (i, j))
    bf = jax.ShapeDtypeStruct((t, D_MODEL), BF16)
    return pl.pallas_call(
        body, name="branch_bwd", grid=(t // tm, nj),
        in_specs=[act, act, wgt, wgt, pl.BlockSpec((tm, tn), lambda i, j: (i, j)),
                  pl.BlockSpec((tm, tn), lambda i, j: (i, 2 * nj + j)), out, out],
        out_specs=[out, out, out, out],
        out_shape=[bf, bf, jax.ShapeDtypeStruct((t, D_MODEL), F32), bf],
        compiler_params=_cparams(("parallel", "parallel")),
    )(dya, dyr, wa, wr, zrest, zrest, yatt, ylru)


def _dh_partial(parts, after, name):
    t = parts[0][0].shape[0]
    tm = min(256, t)
    np_ = len(parts)

    def body(*refs):
        o_ref = refs[-1]
        acc = _dot(refs[0][...], refs[np_][...])
        for p in range(1, np_):
            acc = acc + _dot(refs[p][...], refs[np_ + p][...])
        o_ref[...] = acc

    in_specs = [pl.BlockSpec((tm, dz.shape[1]), lambda i: (i, 0)) for dz, _ in parts]
    in_specs += [pl.BlockSpec(w.shape, lambda i: (0, 0)) for _, w in parts]
    in_specs += [pl.BlockSpec(after.shape, lambda i: (0, 0))]
    return pl.pallas_call(
        body, name=name, grid=(t // tm,),
        in_specs=in_specs,
        out_specs=pl.BlockSpec((tm, D_MODEL), lambda i: (i, 0)),
        out_shape=jax.ShapeDtypeStruct((t, D_MODEL), F32),
        compiler_params=_cparams(("parallel",), vmem_mb=48),
    )(*[dz for dz, _ in parts], *[w for _, w in parts], after)


def _dh_final(parts, acc_in, x2, dy, wpre):
    t = x2.shape[0]
    tm = min(256, t)
    np_ = len(parts)

    def body(*refs):
        acc_ref, x_ref, dy_ref, w_ref = refs[2 * np_:2 * np_ + 4]
        gx_ref, pw_ref = refs[2 * np_ + 4:]

        @pl.when(pl.program_id(0) == 0)
        def _():
            pw_ref[...] = jnp.zeros_like(pw_ref)

        dh = acc_ref[...]
        for p in range(np_):
            dh = dh + _dot(refs[p][...], refs[np_ + p][...])
        x = x_ref[...]
        r = lax.rsqrt(jnp.mean(x * x, axis=-1, keepdims=True) + NORM_EPS)
        xn = x * r
        dxn = dh * w_ref[...]
        gx_ref[...] = r * (dxn - xn * jnp.mean(dxn * xn, axis=-1, keepdims=True)) + dy_ref[...]
        pw_ref[0:1, :] += jnp.sum(dh * xn, axis=0, keepdims=True)

    row = pl.BlockSpec((tm, D_MODEL), lambda i: (i, 0))
    in_specs = [pl.BlockSpec((tm, dz.shape[1]), lambda i: (i, 0)) for dz, _ in parts]
    in_specs += [pl.BlockSpec(w.shape, lambda i: (0, 0)) for _, w in parts]
    in_specs += [row, row, row, pl.BlockSpec((1, D_MODEL), lambda i: (0, 0))]
    return pl.pallas_call(
        body, name="dh_final", grid=(t // tm,),
        in_specs=in_specs,
        out_specs=[row, pl.BlockSpec((8, D_MODEL), lambda i: (0, 0))],
        out_shape=[jax.ShapeDtypeStruct((t, D_MODEL), F32), jax.ShapeDtypeStruct((8, D_MODEL), F32)],
        compiler_params=_cparams(("arbitrary",), vmem_mb=48),
    )(*[dz for dz, _ in parts], *[w for _, w in parts], acc_in, x2, dy, wpre)


def _adamw(w, g, m, v):
    m = ADAM_B1 * m + (1.0 - ADAM_B1) * g
    v = ADAM_B2 * v + (1.0 - ADAM_B2) * (g * g)
    m_hat = m / (1.0 - ADAM_B1 ** ADAM_STEP)
    v_hat = v / (1.0 - ADAM_B2 ** ADAM_STEP)
    delta = -ADAM_LR * (m_hat / (jnp.sqrt(v_hat) + ADAM_EPS) + ADAM_WD * w)
    return delta, m, v


def _reduce_adamw(own, parts, place, w, m, v, name):
    r, c = w.shape
    blk, nblk, at = _blocks_2d(r, c)

    def body(place_ref, own_ref, p_ref, w_ref, m_ref, v_ref, g_ref, d_ref, nm_ref, nv_ref):
        mine = place_ref[1]
        own_blk = own_ref[...]
        g = jnp.where(mine == 0, own_blk, p_ref[0].astype(F32))
        for j in range(1, N_CHIPS):
            g = g + jnp.where(mine == j, own_blk, p_ref[j].astype(F32))
        d, nm, nv = _adamw(w_ref[...], g, m_ref[...], v_ref[...])
        g_ref[...] = g
        d_ref[...] = d
        nm_ref[...] = nm
        nv_ref[...] = nv

    row = pl.BlockSpec(blk, lambda i, pr: at(i))
    sh = jax.ShapeDtypeStruct((r, c), F32)
    grid_spec = pltpu.PrefetchScalarGridSpec(
        num_scalar_prefetch=1, grid=(nblk,),
        in_specs=[row, pl.BlockSpec((N_CHIPS,) + blk, lambda i, pr: (0,) + at(i)), row, row, row],
        out_specs=[row, row, row, row])
    return pl.pallas_call(
        body, name=name, grid_spec=grid_spec, out_shape=[sh, sh, sh, sh],
        compiler_params=_cparams(("parallel",)),
    )(place, own, parts, w, m, v)


def _reduce_adamw_stacked(own, parts, place, triples, name):
    n = len(triples)
    _, r, c = triples[0][0].shape

    def body(place_ref, own_ref, p_ref, *refs):
        ins, outs = refs[:3 * n], refs[3 * n:]
        mine = place_ref[1]
        for i in range(n):
            rows = slice(i * r, (i + 1) * r)
            own_blk = own_ref[rows, :]
            g = jnp.where(mine == 0, own_blk, p_ref[0, rows, :].astype(F32))
            for j in range(1, N_CHIPS):
                g = g + jnp.where(mine == j, own_blk, p_ref[j, rows, :].astype(F32))
            d, nm, nv = _adamw(ins[3 * i][0], g, ins[3 * i + 1][0], ins[3 * i + 2][0])
            for k, val in enumerate((g, d, nm, nv)):
                outs[4 * i + k][0] = val

    whole = lambda shape: pl.BlockSpec(shape, lambda i, pr: (0,) * len(shape))
    grid_spec = pltpu.PrefetchScalarGridSpec(
        num_scalar_prefetch=1, grid=(1,),
        in_specs=[whole(own.shape), whole(parts.shape)] + [whole((1, r, c))] * (3 * n),
        out_specs=[whole((1, r, c))] * (4 * n))
    res = pl.pallas_call(
        body, name=name, grid_spec=grid_spec,
        out_shape=[jax.ShapeDtypeStruct((1, r, c), F32)] * (4 * n),
        compiler_params=_cparams(("arbitrary",)),
    )(place, own, parts, *[a for t3 in triples for a in t3])
    return [res[4 * i:4 * i + 4] for i in range(n)]


def _interleave_qkv(a):
    lead = a.shape[:-1]
    return a.reshape(lead + (3, HEAD_PAIRS, LANES)).swapaxes(-3, -2).reshape(lead + (3 * D_MODEL,))


def _deinterleave_qkv(a):
    lead = a.shape[:-1]
    return a.reshape(lead + (HEAD_PAIRS, 3, LANES)).swapaxes(-3, -2).reshape(lead + (3 * D_MODEL,))


def _interleave_rows(a):
    return a.reshape(3, HEAD_PAIRS, LANES, a.shape[1]).swapaxes(0, 1).reshape(a.shape)


def _deinterleave_rows(a):
    return a.reshape(HEAD_PAIRS, 3, LANES, a.shape[1]).swapaxes(0, 1).reshape(a.shape)


def _pack_small(pre, conv_b, rg_ba, rg_bx, lam, post, loss_row, b_in, conv_w_full, rg_wa, rg_wx):
    z = jnp.zeros((1, D_MODEL), F32)
    b_used = jnp.concatenate([b_in[:, 0:3 * D_MODEL], b_in[:, 3 * D_MODEL + HEADS:IN_TOTAL]], axis=1)
    b_f = jnp.pad(b_in[:, 3 * D_MODEL:3 * D_MODEL + HEADS], ((0, 0), (0, D_MODEL - HEADS)))
    return jnp.concatenate([
        pre, conv_b, rg_ba, rg_bx, lam, post, loss_row, z,
        b_used.reshape(9, D_MODEL), b_f, conv_w_full, z, z,
        rg_wa.reshape(64, D_MODEL), rg_wx.reshape(64, D_MODEL)], axis=0)


def _unpack_small(p):
    b_used = p[8:17].reshape(1, 9 * D_MODEL)
    b_in = jnp.concatenate([b_used[:, 0:3 * D_MODEL], p[17:18, 0:HEADS], b_used[:, 3 * D_MODEL:]], axis=1)
    return dict(pre_norm_w=p[0:1], conv_b=p[1:2], rg_ba=p[2:3], rg_bx=p[3:4], rg_lambda=p[4:5],
                post_norm_w=p[5:6], loss_row=p[6:7], b_in=b_in, conv_w_full=p[18:22],
                rg_wa=p[24:88].reshape(1, 16, 64, 64), rg_wx=p[88:152].reshape(1, 16, 64, 64))


def _reduce_small(parts, first, w, m, v, vectors):
    nvec = len(vectors)

    def body(p_ref, f_ref, w_ref, m_ref, v_ref, *refs):
        ins, outs = refs[:3 * nvec], refs[3 * nvec:]
        g = p_ref[0]
        g0 = f_ref[0, 0:1, :]
        for j in range(1, N_DEV):
            g = g + p_ref[j]
            g0 = g0 + f_ref[j, 0:1, :]
        d, nm, nv = _adamw(w_ref[...], g, m_ref[...], v_ref[...])
        for k, val in enumerate((g, d, nm, nv)):
            outs[k][...] = val
        for i in range(nvec):
            gi = g0 if i == 0 else g[i:i + 1, :]
            di, nmi, nvi = _adamw(ins[3 * i][...], gi, ins[3 * i + 1][...], ins[3 * i + 2][...])
            for k, val in enumerate((gi, di, nmi, nvi)):
                outs[4 + 4 * i + k][...] = val
        outs[-1][...] = jnp.zeros((8, LANES), F32) + (0.5 / D_MODEL) * jnp.sum(g[LOSS_ROW:LOSS_ROW + 1, :])

    sh = jax.ShapeDtypeStruct((SMALL_ROWS, D_MODEL), F32)
    vec = jax.ShapeDtypeStruct((1, D_MODEL), F32)
    res = pl.pallas_call(
        body, name="reduce_small",
        out_shape=[sh, sh, sh, sh] + [vec] * (4 * nvec) + [jax.ShapeDtypeStruct((8, LANES), F32)],
    )(parts, first, w, m, v, *[a for t3 in vectors for a in t3])
    return res[:4], [res[4 + 4 * i:8 + 4 * i] for i in range(nvec)], res[-1]


def kernel(x, pre_norm_w, w_in, b_in, conv_w, conv_b, rg_wa, rg_ba, rg_wx, rg_bx, rg_lambda, w_branch_a, w_branch_r, w_out, post_norm_w, loss_target, m_pre_norm_w, m_w_in, m_b_in, m_conv_w, m_conv_b, m_rg_wa, m_rg_ba, m_rg_wx, m_rg_bx, m_rg_lambda, m_w_branch_a, m_w_branch_r, m_w_out, m_post_norm_w, v_pre_norm_w, v_w_in, v_b_in, v_conv_w, v_conv_b, v_rg_wa, v_rg_ba, v_rg_wx, v_rg_bx, v_rg_lambda, v_w_branch_a, v_w_branch_r, v_w_out, v_post_norm_w):
    b, s, _ = x.shape
    t = b * s
    me = 4 * lax.axis_index("x") + 2 * lax.axis_index("y") + lax.axis_index("c")
    shard_rows = D_MODEL // N_DEV

    place = jnp.stack([lax.axis_index("c"), 2 * lax.axis_index("x") + lax.axis_index("y")]).astype(jnp.int32)
    w_in_all = _gather(w_in[0].T.astype(BF16), "gather_w_in")
    wt_full = w_in_all.reshape(IN_TOTAL, D_MODEL)
    conv_terms = jnp.concatenate(_split3(conv_w[0]), axis=0)
    conv_pad = jnp.pad(conv_terms, ((0, 16 - 3 * CONV_W), (0, D_MODEL - LANES)))
    sq_stack = jnp.concatenate([w_branch_a[0].astype(BF16), w_branch_r[0].astype(BF16), w_out[0].astype(BF16),
                                conv_pad], axis=0)
    sq_sems, sq_src, sq_land, sq_token = _gather_start(sq_stack, w_in_all, "gather_w_sq_start")

    w_qkv = _interleave_rows(wt_full[0:3 * D_MODEL])
    w_f = jnp.pad(wt_full[3 * D_MODEL:3 * D_MODEL + HEADS], ((0, LANES - HEADS), (0, 0)))
    w_rest = wt_full[3 * D_MODEL + HEADS:IN_USED]
    b_qkv = _interleave_qkv(b_in[:, 0:3 * D_MODEL]) + sq_token[0, 0]
    b_f = jnp.pad(b_in[:, 3 * D_MODEL:3 * D_MODEL + HEADS], ((0, 0), (0, LANES - HEADS)))
    b_rest = b_in[:, 3 * D_MODEL + HEADS:IN_USED]

    def blockdiag(w):
        w2 = w.reshape(N_CBLK, 2, HEAD_DIM, HEAD_DIM)
        zz = jnp.zeros((N_CBLK, HEAD_DIM, HEAD_DIM), w.dtype)
        top = jnp.concatenate([w2[:, 0], zz], axis=2)
        bot = jnp.concatenate([zz, w2[:, 1]], axis=2)
        return jnp.concatenate([top, bot], axis=1).astype(BF16)

    bda, bdx = blockdiag(rg_wa[0]), blockdiag(rg_wx[0])

    x2 = x.reshape(t, D_MODEL)
    tgt2 = loss_target.reshape(t, D_MODEL)
    h, qkv, zf = _prenorm_inproj(x2, pre_norm_w, w_qkv, b_qkv, w_f, b_f)
    zrest = _mm_bias(h, w_rest, b_rest, BF16, "inproj_rest")
    qkv3 = qkv.reshape(b, s, 3 * D_MODEL)
    zrest3 = zrest.reshape(b, s, 5 * D_MODEL)
    zf3 = zf.reshape(b, s, LANES)
    cexp3, crow = _fgate_fwd(zf3)
    yatt3, lse, ga = _attn_fwd(qkv3, cexp3, crow, zrest3)

    sq_all = _gather_wait(sq_sems, sq_src, sq_land, ga, "gather_w_sq_wait")
    sq_all = lax.dynamic_update_slice(sq_all, sq_stack[None], (me, 0, 0))
    wa = sq_all[:, 0:shard_rows].reshape(D_MODEL, D_MODEL)
    wr = sq_all[:, shard_rows:2 * shard_rows].reshape(D_MODEL, D_MODEL)
    wo = sq_all[:, 2 * shard_rows:3 * shard_rows].reshape(D_MODEL, D_MODEL)
    conv_all = sq_all[:, 3 * shard_rows:3 * shard_rows + 3 * CONV_W, 0:LANES].astype(F32)
    conv_all = (conv_all[:, 0:CONV_W] + conv_all[:, CONV_W:2 * CONV_W]) + conv_all[:, 2 * CONV_W:3 * CONV_W]
    conv_full = conv_all.transpose(1, 0, 2).reshape(CONV_W, D_MODEL)

    ylru3, gr = _rnn_fwd(zrest3, conv_full, conv_b, bda, bdx, rg_ba, rg_bx, rg_lambda)
    ya, yr, mm = _branch_merge(ga, gr, wa, wr, zrest)
    dy, do, acc_out = _out_loss(mm, wo, x2, tgt2, post_norm_w)

    dya, dyr, dz_mga, dz_mgr = _merge_bwd(do, wo, zrest, ya, yr)
    dyatt, dz_ga, dylru, dz_gr = _branch_bwd(dya, dyr, wa, wr, zrest, yatt3.reshape(t, D_MODEL),
                                             ylru3.reshape(t, D_MODEL))
    dz_xr, pvec, dbd = _rnn_bwd(zrest3, ylru3, dylru.reshape(b, s, D_MODEL), conv_full, conv_b, bda, bdx,
                                rg_ba, rg_bx, rg_lambda)
    dz_qkv, dc3 = _attn_bwd(qkv3, dyatt.reshape(b, s, D_MODEL), yatt3, lse, crow, cexp3)
    dz_f = _fgate_bwd(dc3, zf3)

    dw_qkv, db_qkv = _mm_tn(dz_qkv, h, "dw_qkv")
    dw_f, db_f = _mm_tn(dz_f, h, "dw_f")
    dw_parts, db_parts = [], []
    for nm, dzp in (("ga", dz_ga), ("xr", dz_xr), ("gr", dz_gr), ("mga", dz_mga), ("mgr", dz_mgr)):
        dwp, dbp = _mm_tn(dzp, h, "dw_" + nm)
        dw_parts.append(dwp)
        db_parts.append(dbp[0:1])

    zeros_tail = jnp.zeros((IN_TOTAL - IN_USED, D_MODEL), F32)
    dwt_full = jnp.concatenate([_deinterleave_rows(dw_qkv), dw_f[0:HEADS]] + dw_parts + [zeros_tail], axis=0)
    dw_in_send = dwt_full.reshape(N_CHIPS, 2, W_SHARD, D_MODEL).transpose(1, 0, 2, 3)
    swp_sems, dw_in_src, swp_land, swp_token = _swap_start(dw_in_send, db_f, "swap_dw_in_start")
    dw_a, _ = _mm_tn(ga, dya, "dw_a", after=swp_token)
    dw_r, _ = _mm_tn(gr, dyr, "dw_r", after=swp_token)
    dw_o, _ = _mm_tn(mm, do, "dw_o", after=swp_token)
    dw_in_send, sib_in = _swap_wait(swp_sems, dw_in_src, swp_land, dw_o, "swap_dw_in_wait")
    by_dest = lambda a: a.reshape(N_CHIPS, 2, shard_rows, D_MODEL).transpose(1, 0, 2, 3)
    dw_sq_send = jnp.concatenate([by_dest(dw_a), by_dest(dw_r), by_dest(dw_o)], axis=2)

    db_in_full = jnp.concatenate([_deinterleave_qkv(db_qkv[0:1]), db_f[0:1, 0:HEADS]] + db_parts
                                 + [jnp.zeros((1, IN_TOTAL - IN_USED), F32)], axis=1)
    d_rg_wa = jnp.stack([dbd[:, 0, 0:HEAD_DIM, 0:HEAD_DIM], dbd[:, 0, HEAD_DIM:, HEAD_DIM:]], axis=1)
    d_rg_wx = jnp.stack([dbd[:, 1, 0:HEAD_DIM, 0:HEAD_DIM], dbd[:, 1, HEAD_DIM:, HEAD_DIM:]], axis=1)
    small_g = _pack_small(jnp.zeros((1, D_MODEL), F32), pvec[4:5], pvec[5:6], pvec[6:7], pvec[7:8], acc_out[0:1],
                          acc_out[1:2], db_in_full, pvec[0:4], d_rg_wa, d_rg_wx)
    sm_sems, sm_src, sm_land, sm_token = _gather_start(small_g, dw_o, "gather_small_start")

    dw_sq_send = dw_sq_send + sm_token[0, 0]
    (sib_sq,) = _swap_with_sibling([dw_sq_send], "swap_dw_sq")
    chip_in, own_in = _pair_add(dw_in_send, sib_in, place, "pair_add_in")
    chip_sq, own_sq = _pair_add(dw_sq_send, sib_sq, place, "pair_add_sq")
    sems, sent, lands, token = _exchange_chips_start([chip_in, chip_sq], "exchange_dw_start")

    wt = lambda lo: w_rest[lo * D_MODEL:(lo + 1) * D_MODEL]
    dh_a = _dh_partial([(dz_qkv, w_qkv), (dz_f, w_f)], token, "dh_qkv")
    grad_x2, acc_pre = _dh_final(
        [(dz_ga, wt(0)), (dz_xr, wt(1)), (dz_gr, wt(2)), (dz_mga, wt(3)), (dz_mgr, wt(4))],
        dh_a, x2, dy, pre_norm_w)
    pre_sems, pre_src, pre_land, pre_token = _gather_start(acc_pre, grad_x2, "gather_pre_start")
    recv_in, recv_sq = _exchange_chips_wait(sems, sent, lands, pre_token, "exchange_dw_wait")

    g_in, d_in, nm_in, nv_in = [a.T for a in _reduce_adamw(
        own_in, recv_in, place, w_in[0].T, m_w_in[0].T, v_w_in[0].T, "adamw_w_in")]
    sq_out = _reduce_adamw_stacked(
        own_sq, recv_sq, place,
        [(w_branch_a, m_w_branch_a, v_w_branch_a), (w_branch_r, m_w_branch_r, v_w_branch_r),
         (w_out, m_w_out, v_w_out)], "adamw_w_sq")
    pre_all = _gather_wait(pre_sems, pre_src, pre_land, sq_out[2][1], "gather_pre_wait")
    pre_all = lax.dynamic_update_slice(pre_all, acc_pre[None], (me, 0, 0))
    small_all = _gather_wait(sm_sems, sm_src, sm_land, pre_all, "gather_small_wait")
    small_all = lax.dynamic_update_slice(small_all, small_g[None], (me, 0, 0))

    def place_conv(a):
        return lax.dynamic_update_slice(jnp.zeros((CONV_W, D_MODEL), F32), a[0], (0, me * LANES))

    zrow = jnp.zeros((1, D_MODEL), F32)
    vector_names = ["pre_norm_w", "conv_b", "rg_ba", "rg_bx", "rg_lambda", "post_norm_w"]
    vectors = [(pre_norm_w, m_pre_norm_w, v_pre_norm_w), (conv_b, m_conv_b, v_conv_b), (rg_ba, m_rg_ba, v_rg_ba),
               (rg_bx, m_rg_bx, v_rg_bx), (rg_lambda, m_rg_lambda, v_rg_lambda),
               (post_norm_w, m_post_norm_w, v_post_norm_w)]
    small_w = _pack_small(zrow, zrow, zrow, zrow, zrow, zrow, zrow, b_in, place_conv(conv_w), rg_wa[0], rg_wx[0])
    small_m = _pack_small(zrow, zrow, zrow, zrow, zrow, zrow, zrow, m_b_in, place_conv(m_conv_w), m_rg_wa[0],
                          m_rg_wx[0])
    small_v = _pack_small(zrow, zrow, zrow, zrow, zrow, zrow, zrow, v_b_in, place_conv(v_conv_w), v_rg_wa[0],
                          v_rg_wx[0])
    packed, vector_out, loss_tile = _reduce_small(small_all, pre_all, small_w, small_m, small_v, vectors)
    outs_small = [_unpack_small(p) for p in packed]
    loss = loss_tile[0, 0]

    def leaf(kind, name):
        if name == "w_in":
            return (g_in, d_in, nm_in, nv_in)[kind][None]
        if name in ("w_branch_a", "w_branch_r", "w_out"):
            return sq_out[("w_branch_a", "w_branch_r", "w_out").index(name)][kind]
        if name == "conv_w":
            return lax.dynamic_slice(outs_small[kind]["conv_w_full"], (0, me * LANES), (CONV_W, LANES))[None]
        if name in vector_names:
            return vector_out[vector_names.index(name)][kind]
        return outs_small[kind][name]

    names = ["pre_norm_w", "w_in", "b_in", "conv_w", "conv_b", "rg_wa", "rg_ba", "rg_wx", "rg_bx", "rg_lambda",
             "w_branch_a", "w_branch_r", "w_out", "post_norm_w"]
    out = [loss, grad_x2.reshape(b, s, D_MODEL)]
    for kind in range(4):
        out += [leaf(kind, nm) for nm in names]
    return tuple(out)
```

```python
import jax
import jax.numpy as jnp
from jax import lax
from jax.experimental import pallas as pl
from jax.experimental.pallas import tpu as pltpu

F32 = jnp.float32
BF16 = jnp.bfloat16

N_DEV = 8
D_MODEL = 1024
HEADS = 16
HEAD_DIM = 64
HEAD_PAIRS = HEADS // 2
LANES = 128
N_CBLK = D_MODEL // LANES
CONV_W = 4
RG_C = 8.0
NORM_EPS = 1e-6
MASK_VALUE = -1e30
IN_USED = 8208
IN_TOTAL = 9232
W_SHARD = IN_TOTAL // N_DEV

ADAM_LR = 0.001
ADAM_B1 = 0.9
ADAM_B2 = 0.999
ADAM_EPS = 1e-08
ADAM_WD = 0.01
ADAM_STEP = 10

ATT_TILE_FWD = 256
ATT_TILE_BWD = 512
SCAN_TILE = 256
SMALL_ROWS = 152
LOSS_ROW = 6


def _cparams(sem=None, vmem_mb=None):
    kw = {}
    if sem is not None:
        kw["dimension_semantics"] = sem
    if vmem_mb is not None:
        kw["vmem_limit_bytes"] = vmem_mb * 1024 * 1024
    return pltpu.CompilerParams(**kw)


def _sigmoid(x):
    return 1.0 / (1.0 + jnp.exp(-x))


def _softplus(x):
    return jnp.maximum(x, 0.0) + jnp.log1p(jnp.exp(-jnp.abs(x)))


def _one_minus_exp(y, exp_y):
    series = -y * (1.0 + y * (1.0 / 2 + y * (1.0 / 6 + y * (1.0 / 24 + y * (1.0 / 120)))))
    return jnp.where(y > -0.0625, series, 1.0 - exp_y)


def _split3(x):
    hi = x.astype(BF16)
    r1 = x - hi.astype(F32)
    mid = r1.astype(BF16)
    lo = (r1 - mid.astype(F32)).astype(BF16)
    return hi, mid, lo


def _dot(a, b):
    return jnp.dot(a, b, preferred_element_type=F32)


def _dot_nt(a, b):
    return lax.dot_general(a, b, (((1,), (1,)), ((), ())), preferred_element_type=F32)


def _dot_tn(a, b):
    return lax.dot_general(a, b, (((0,), (0,)), ((), ())), preferred_element_type=F32)


def _iota(shape, dim):
    return lax.broadcasted_iota(jnp.int32, shape, dim)


_ANY = pl.BlockSpec(memory_space=pl.ANY)
_MESH = pl.DeviceIdType.MESH
N_CHIPS = 4


def _place():
    x, y, c = lax.axis_index("x"), lax.axis_index("y"), lax.axis_index("c")
    other_chips = [(1 - x, y), (x, 1 - y), (1 - x, 1 - y)]
    return x, y, c, other_chips


def _gather(x_shard, name):
    def body(x_ref, out_ref, send_sems, recv_sems, local_sem):
        x, y, c, chips = _place()
        me, sibling = (x, y, c), (x, y, 1 - c)

        def slot(p):
            return out_ref.at[4 * p[0] + 2 * p[1] + p[2]]

        def copy(k, block, to, src=None):
            return pltpu.make_async_remote_copy(
                src_ref=slot(block) if src is None else src, dst_ref=slot(block),
                send_sem=send_sems.at[k], recv_sem=recv_sems.at[k], device_id=to, device_id_type=_MESH)

        mine = pltpu.make_async_copy(x_ref, slot(me), local_sem)
        mine.start()
        first = [copy(0, me, sibling, src=x_ref)]
        first += [copy(1 + j, me, (*chip, c), src=x_ref) for j, chip in enumerate(chips)]
        for cp in first:
            cp.start()
        passed = [copy(4 + j, (*chip, c), sibling) for j, chip in enumerate(chips)]
        for j, chip in enumerate(chips):
            copy(1 + j, (*chip, c), me).wait_recv()
            passed[j].start()
        copy(0, sibling, me).wait_recv()
        for j, chip in enumerate(chips):
            copy(4 + j, (*chip, 1 - c), me).wait_recv()
        for cp in first + passed:
            cp.wait_send()
        mine.wait()

    return pl.pallas_call(
        body, name=name,
        out_shape=jax.ShapeDtypeStruct((N_DEV,) + tuple(x_shard.shape), x_shard.dtype),
        in_specs=[_ANY], out_specs=_ANY,
        scratch_shapes=[pltpu.SemaphoreType.DMA((7,)), pltpu.SemaphoreType.DMA((7,)), pltpu.SemaphoreType.DMA],
    )(x_shard)


def _swap_with_sibling(srcs, name):
    n = len(srcs)

    def body(*refs):
        src_refs, out_refs = refs[:n], refs[n:2 * n]
        send_sems, recv_sems = refs[2 * n:]
        x, y, c, _ = _place()
        cps = [pltpu.make_async_remote_copy(
            src_ref=src_refs[i].at[1 - c], dst_ref=out_refs[i], send_sem=send_sems.at[i], recv_sem=recv_sems.at[i],
            device_id=(x, y, 1 - c), device_id_type=_MESH) for i in range(n)]
        for cp in cps:
            cp.start()
        for cp in cps:
            cp.wait()

    return pl.pallas_call(
        body, name=name,
        out_shape=[jax.ShapeDtypeStruct(a.shape[1:], a.dtype) for a in srcs],
        in_specs=[_ANY] * n, out_specs=[_ANY] * n,
        scratch_shapes=[pltpu.SemaphoreType.DMA((n,)), pltpu.SemaphoreType.DMA((n,))],
    )(*srcs)


def _blocks_2d(r, c):
    if r % 128 == 0:
        return (128, c), r // 128, lambda i: (i, 0)
    return (r, 256), c // 256, lambda i: (0, i)


def _pair_add(src, recv, place, name):
    _, _, r, c = src.shape
    blk, nblk, at = _blocks_2d(r, c)

    def body(place_ref, a_ref, b_ref, q16_ref, own_ref):
        q = a_ref[...] + b_ref[...]
        q16_ref[...] = q.astype(BF16)

        @pl.when(pl.program_id(1) == place_ref[1])
        def _():
            own_ref[...] = q

    grid_spec = pltpu.PrefetchScalarGridSpec(
        num_scalar_prefetch=1, grid=(nblk, N_CHIPS),
        in_specs=[pl.BlockSpec((None, None) + blk, lambda i, j, pr: (pr[0], j) + at(i)),
                  pl.BlockSpec((None,) + blk, lambda i, j, pr: (j,) + at(i))],
        out_specs=[pl.BlockSpec((None,) + blk, lambda i, j, pr: (j,) + at(i)),
                   pl.BlockSpec(blk, lambda i, j, pr: at(i))])
    return pl.pallas_call(
        body, name=name, grid_spec=grid_spec,
        out_shape=[jax.ShapeDtypeStruct((N_CHIPS, r, c), BF16), jax.ShapeDtypeStruct((r, c), F32)],
        compiler_params=_cparams(("parallel", "arbitrary")),
    )(place, src, recv)


_HBM = pl.BlockSpec(memory_space=pltpu.HBM)
_SEM = pl.BlockSpec(memory_space=pltpu.SEMAPHORE)
_DATAFLOW = pltpu.SideEffectType.DATAFLOW_SIDE_EFFECTING


def _chip_copy(src_ref, land_ref, send_sem, recv_sem, k, chips, c, land):
    chip = chips[k]
    return pltpu.make_async_remote_copy(
        src_ref=src_ref.at[2 * chip[0] + chip[1]], dst_ref=land_ref.at[land],
        send_sem=send_sem, recv_sem=recv_sem, device_id=(*chip, c), device_id_type=_MESH)


def _exchange_chips_start(srcs, name):
    n = len(srcs)
    ncp = 3 * n

    def body(*refs):
        src_refs, land_refs = refs[:n], refs[n:2 * n]
        sems = refs[4 * n:4 * n + 2 * ncp]
        token = refs[-1]
        x, y, c, chips = _place()
        for i in range(n):
            for k in range(3):
                j = 3 * i + k
                _chip_copy(src_refs[i], land_refs[i], sems[j], sems[ncp + j], k, chips, c, 2 * x + y).start()
        token[...] = jnp.zeros_like(token)

    hbm = [pltpu.HBM(a.shape, a.dtype) for a in srcs]
    lands = [pltpu.with_memory_space_constraint(lax.empty(a.shape, a.dtype), pltpu.HBM) for a in srcs]
    res = pl.pallas_call(
        body, name=name,
        out_shape=(*hbm, *hbm, *([pltpu.SemaphoreType.DMA(())] * (2 * ncp)), jax.ShapeDtypeStruct((8, LANES), F32)),
        in_specs=[_HBM] * (2 * n),
        out_specs=(*([_HBM] * (2 * n)), *([_SEM] * (2 * ncp)), pl.BlockSpec(memory_space=pltpu.VMEM)),
        input_output_aliases={i: i for i in range(2 * n)},
        compiler_params=pltpu.CompilerParams(has_side_effects=_DATAFLOW),
    )(*[pltpu.with_memory_space_constraint(a, pltpu.HBM) for a in srcs], *lands)
    return list(res[2 * n:2 * n + 2 * ncp]), list(res[:n]), list(res[n:2 * n]), res[-1]


def _exchange_chips_wait(sems, srcs, lands, after, name):
    n = len(srcs)
    ncp = 3 * n

    def body(*refs):
        src_refs, land_refs = refs[:n], refs[n:2 * n]
        sem_refs = refs[2 * n:2 * n + 2 * ncp]
        x, y, c, chips = _place()
        for i in range(n):
            for k in range(3):
                j = 3 * i + k
                cp = _chip_copy(src_refs[i], land_refs[i], sem_refs[j], sem_refs[ncp + j], k, chips, c,
                                2 * chips[k][0] + chips[k][1])
                cp.wait_send()
                cp.wait_recv()

    hbm = [pltpu.HBM(a.shape, a.dtype) for a in srcs]
    res = pl.pallas_call(
        body, name=name, out_shape=(*hbm, *hbm),
        in_specs=[_HBM] * (2 * n) + [_SEM] * (2 * ncp) + [_ANY], out_specs=tuple([_HBM] * (2 * n)),
        input_output_aliases={i: i for i in range(2 * n)},
        compiler_params=pltpu.CompilerParams(has_side_effects=_DATAFLOW),
    )(*srcs, *lands, *sems, after)
    return list(res[n:2 * n])


def _swap_start(src, after, name):
    def body(src_ref, land_ref, after_ref, src_thru, land_thru, send_sem, recv_sem, token):
        x, y, c, _ = _place()
        pltpu.make_async_remote_copy(src_ref=src_ref.at[1 - c], dst_ref=land_ref, send_sem=send_sem,
                                     recv_sem=recv_sem, device_id=(x, y, 1 - c), device_id_type=_MESH).start()
        token[...] = jnp.zeros_like(token)

    land = pltpu.with_memory_space_constraint(lax.empty(src.shape[1:], src.dtype), pltpu.HBM)
    res = pl.pallas_call(
        body, name=name,
        out_shape=(pltpu.HBM(src.shape, src.dtype), pltpu.HBM(land.shape, land.dtype),
                   pltpu.SemaphoreType.DMA(()), pltpu.SemaphoreType.DMA(()), jax.ShapeDtypeStruct((8, LANES), F32)),
        in_specs=[_HBM, _HBM, _ANY],
        out_specs=(_HBM, _HBM, _SEM, _SEM, pl.BlockSpec(memory_space=pltpu.VMEM)),
        input_output_aliases={0: 0, 1: 1},
        compiler_params=pltpu.CompilerParams(has_side_effects=_DATAFLOW),
    )(pltpu.with_memory_space_constraint(src, pltpu.HBM), land, after)
    return [res[2], res[3]], res[0], res[1], res[-1]


def _swap_wait(sems, src, land, after, name):
    def body(src_ref, land_ref, send_sem, recv_sem, after_ref, src_out, land_out):
        x, y, c, _ = _place()
        cp = pltpu.make_async_remote_copy(src_ref=src_ref.at[1 - c], dst_ref=land_ref, send_sem=send_sem,
                                          recv_sem=recv_sem, device_id=(x, y, 1 - c), device_id_type=_MESH)
        cp.wait_send()
        cp.wait_recv()

    res = pl.pallas_call(
        body, name=name, out_shape=(pltpu.HBM(src.shape, src.dtype), pltpu.HBM(land.shape, land.dtype)),
        in_specs=[_HBM, _HBM, _SEM, _SEM, _ANY], out_specs=(_HBM, _HBM),
        input_output_aliases={0: 0, 1: 1},
        compiler_params=pltpu.CompilerParams(has_side_effects=_DATAFLOW),
    )(src, land, *sems, after)
    return res[0], res[1]


def _peer_copy(src_ref, land_ref, send_sem, recv_sem, k, place, land):
    x, y, c = place
    peer = (1 - x if k & 4 else x, 1 - y if k & 2 else y, 1 - c if k & 1 else c)
    return pltpu.make_async_remote_copy(
        src_ref=src_ref, dst_ref=land_ref.at[land], send_sem=send_sem, recv_sem=recv_sem,
        device_id=peer, device_id_type=_MESH)


def _gather_start(x_shard, after, name):
    npeer = N_DEV - 1

    def body(x_ref, land_ref, after_ref, x_thru, land_thru, *rest):
        sems, token = rest[:2 * npeer], rest[-1]
        x, y, c, _ = _place()
        for k in range(1, N_DEV):
            _peer_copy(x_ref, land_ref, sems[k - 1], sems[npeer + k - 1], k, (x, y, c), 4 * x + 2 * y + c).start()
        token[...] = jnp.zeros_like(token)

    land = pltpu.with_memory_space_constraint(lax.empty((N_DEV,) + tuple(x_shard.shape), x_shard.dtype), pltpu.HBM)
    res = pl.pallas_call(
        body, name=name,
        out_shape=(pltpu.HBM(x_shard.shape, x_shard.dtype), pltpu.HBM(land.shape, land.dtype),
                   *([pltpu.SemaphoreType.DMA(())] * (2 * npeer)), jax.ShapeDtypeStruct((8, LANES), F32)),
        in_specs=[_HBM, _HBM, _ANY],
        out_specs=(_HBM, _HBM, *([_SEM] * (2 * npeer)), pl.BlockSpec(memory_space=pltpu.VMEM)),
        input_output_aliases={0: 0, 1: 1},
        compiler_params=pltpu.CompilerParams(has_side_effects=_DATAFLOW),
    )(pltpu.with_memory_space_constraint(x_shard, pltpu.HBM), land, after)
    return list(res[2:2 + 2 * npeer]), res[0], res[1], res[-1]


def _gather_wait(sems, src, land, after, name):
    npeer = N_DEV - 1

    def body(x_ref, land_ref, *rest):
        sem_refs = rest[:2 * npeer]
        x, y, c, _ = _place()
        for k in range(1, N_DEV):
            peer_index = (4 * x + 2 * y + c) ^ k
            cp = _peer_copy(x_ref, land_ref, sem_refs[k - 1], sem_refs[npeer + k - 1], k, (x, y, c), peer_index)
            cp.wait_send()
            cp.wait_recv()

    res = pl.pallas_call(
        body, name=name, out_shape=(pltpu.HBM(src.shape, src.dtype), pltpu.HBM(land.shape, land.dtype)),
        in_specs=[_HBM, _HBM] + [_SEM] * (2 * npeer) + [_ANY], out_specs=(_HBM, _HBM),
        input_output_aliases={0: 0, 1: 1},
        compiler_params=pltpu.CompilerParams(has_side_effects=_DATAFLOW),
    )(src, land, *sems, after)
    return res[1]


def _prenorm_inproj(x2, w, wt_qkv, b_qkv, wt_f, b_f):
    t = x2.shape[0]
    tm = min(512, t)
    n = wt_qkv.shape[0]
    tn = D_MODEL

    def body(x_ref, w_ref, wq_ref, bq_ref, wf_ref, bf_ref, h_ref, qkv_ref, zf_ref):
        x = x_ref[...]
        r = lax.rsqrt(jnp.mean(x * x, axis=-1, keepdims=True) + NORM_EPS)
        h = (x * r * w_ref[...]).astype(BF16)
        h_ref[...] = h
        for j in range(n // tn):
            cols = slice(j * tn, (j + 1) * tn)
            qkv_ref[:, cols] = (_dot_nt(h, wq_ref[cols, :]) + bq_ref[:, cols]).astype(BF16)
        zf_ref[...] = _dot_nt(h, wf_ref[...]) + bf_ref[...]

    row = lambda c: pl.BlockSpec((tm, c), lambda i: (i, 0))
    whole = lambda a: pl.BlockSpec(a.shape, lambda i: (0, 0))
    return pl.pallas_call(
        body, name="prenorm_inproj_qkv", grid=(t // tm,),
        in_specs=[row(D_MODEL), whole(w), whole(wt_qkv), whole(b_qkv), whole(wt_f), whole(b_f)],
        out_specs=[row(D_MODEL), row(n), row(LANES)],
        out_shape=[jax.ShapeDtypeStruct((t, D_MODEL), BF16), jax.ShapeDtypeStruct((t, n), BF16),
                   jax.ShapeDtypeStruct((t, LANES), F32)],
        compiler_params=_cparams(("parallel",), vmem_mb=48),
    )(x2, w, wt_qkv, b_qkv, wt_f, b_f)


def _mm_bias(a, bt, bias, out_dtype, name):
    m, k = a.shape
    n = bt.shape[0]
    tm = min(512, m)
    tn = min(1024, n)

    def body(a_ref, bt_ref, bias_ref, o_ref):
        aa = a_ref[...]
        for j in range(n // tn):
            cols = slice(j * tn, (j + 1) * tn)
            o_ref[:, cols] = (_dot_nt(aa, bt_ref[cols, :]) + bias_ref[:, cols]).astype(o_ref.dtype)

    return pl.pallas_call(
        body, name=name, grid=(m // tm,),
        in_specs=[pl.BlockSpec((tm, k), lambda i: (i, 0)), pl.BlockSpec((n, k), lambda i: (0, 0)),
                  pl.BlockSpec((1, n), lambda i: (0, 0))],
        out_specs=pl.BlockSpec((tm, n), lambda i: (i, 0)),
        out_shape=jax.ShapeDtypeStruct((m, n), out_dtype),
        compiler_params=_cparams(("parallel",), vmem_mb=48),
    )(a, bt, bias)


def _mm_tn(a, b, name, after=None):
    t, m = a.shape
    n = b.shape[1]
    tm = min(1024, m)
    tk = min(2048, t)
    deps = [] if after is None else [after]

    def body(a_ref, b_ref, *refs):
        o_ref, s_ref = refs[len(deps):]
        kk = pl.program_id(1)

        @pl.when(kk == 0)
        def _():
            o_ref[...] = jnp.zeros_like(o_ref)
            s_ref[...] = jnp.zeros_like(s_ref)

        aa = a_ref[...]
        o_ref[...] += _dot_tn(aa, b_ref[...])
        s_ref[0:1, :] += jnp.sum(aa.astype(F32), axis=0, keepdims=True)

    return pl.pallas_call(
        body, name=name, grid=(m // tm, t // tk),
        in_specs=[pl.BlockSpec((tk, tm), lambda i, kk: (kk, i)), pl.BlockSpec((tk, n), lambda i, kk: (kk, 0))]
        + [pl.BlockSpec(d.shape, lambda i, kk: (0, 0)) for d in deps],
        out_specs=[pl.BlockSpec((tm, n), lambda i, kk: (i, 0)), pl.BlockSpec((8, tm), lambda i, kk: (0, i))],
        out_shape=[jax.ShapeDtypeStruct((m, n), F32), jax.ShapeDtypeStruct((8, m), F32)],
        compiler_params=_cparams(("parallel", "arbitrary"), vmem_mb=48),
    )(a, b, *deps)


def _fgate_fwd(zf3):
    b, s, _ = zf3.shape
    tb = SCAN_TILE
    nb = s // tb

    def body(z_ref, cexp_ref, crow_ref):
        tri = (_iota((tb, tb), 1) <= _iota((tb, tb), 0)).astype(BF16)
        expand = ((_iota((LANES, D_MODEL), 1) >> 6) == _iota((LANES, D_MODEL), 0)).astype(BF16)
        carry = jnp.zeros((1, LANES), F32)
        for i in range(nb):
            rows = slice(i * tb, (i + 1) * tb)
            z = z_ref[rows, :]
            lf = jnp.minimum(z, 0.0) - jnp.log1p(jnp.exp(-jnp.abs(z)))
            cb = sum(_dot(tri, part) for part in _split3(lf)) + carry
            carry = cb[tb - 1:tb, :]
            cexp_ref[rows, :] = sum(_dot(part, expand) for part in _split3(cb))
            crow_ref[:, rows] = cb.T[0:HEADS, :]

    return pl.pallas_call(
        body, name="fgate_fwd", grid=(b,),
        in_specs=[pl.BlockSpec((None, s, LANES), lambda i: (i, 0, 0))],
        out_specs=[pl.BlockSpec((None, s, D_MODEL), lambda i: (i, 0, 0)),
                   pl.BlockSpec((None, HEADS, s), lambda i: (i, 0, 0))],
        out_shape=[jax.ShapeDtypeStruct((b, s, D_MODEL), F32), jax.ShapeDtypeStruct((b, HEADS, s), F32)],
        compiler_params=_cparams(("parallel",)),
    )(zf3)


def _fgate_bwd(dc3, zf3):
    b, s, _ = zf3.shape
    tb = SCAN_TILE
    nb = s // tb

    def body(dc_ref, z_ref, o_ref):
        tri = (_iota((tb, tb), 1) >= _iota((tb, tb), 0)).astype(BF16)
        carry = jnp.zeros((1, LANES), F32)
        for i in reversed(range(nb)):
            rows = slice(i * tb, (i + 1) * tb)
            dlf = sum(_dot(tri, part) for part in _split3(dc_ref[rows, :])) + carry
            carry = dlf[0:1, :]
            o_ref[rows, :] = (dlf * _sigmoid(-z_ref[rows, :])).astype(BF16)

    return pl.pallas_call(
        body, name="fgate_bwd", grid=(b,),
        in_specs=[pl.BlockSpec((None, s, LANES), lambda i: (i, 0, 0)),
                  pl.BlockSpec((None, s, LANES), lambda i: (i, 0, 0))],
        out_specs=pl.BlockSpec((s, LANES), lambda i: (i, 0)),
        out_shape=jax.ShapeDtypeStruct((b * s, LANES), BF16),
        compiler_params=_cparams(("parallel",)),
    )(dc3, zf3)


def _spare(hh):
    return HEAD_DIM if hh == 0 else 0


def _put_cols(tile, mine, cols, first):
    lane = _iota((1, LANES), 1)
    out = jnp.where(mine, tile, jnp.zeros((), tile.dtype))
    for j, c in enumerate(cols):
        out = jnp.where(lane == first + j, c, out)
    return out


def _put_rows(tile, mine, rows, first):
    sub = _iota((LANES, 1), 0)
    out = jnp.where(mine, tile, jnp.zeros((), tile.dtype))
    for j, r in enumerate(rows):
        out = jnp.where(sub == first + j, r, out)
    return out


def _transpose_bf16(a):
    return a.astype(F32).T.astype(BF16)


def _attn_fwd(qkv3, cexp3, crow, zrest3):
    b, s, _ = qkv3.shape
    ta = ATT_TILE_FWD
    nq = s // ta
    hd = HEAD_DIM
    crow5 = crow.reshape(b, HEAD_PAIRS, 2, nq, ta)

    def body(qkv_ref, cq_ref, ck_ref, g_ref, y_ref, lse_ref, ga_ref, kt_scr, v_scr):
        lane = _iota((1, LANES), 1)
        sub = _iota((LANES, 1), 0)
        lane_mine = (lane < hd, lane >= hd)
        sub_mine = (sub < hd, sub >= hd)
        causal = _iota((ta, ta), 0) >= _iota((ta, ta), 1)
        one = jnp.ones((), BF16)

        for kj in range(nq):
            rows = slice(kj * ta, (kj + 1) * ta)
            kt = _transpose_bf16(qkv_ref[rows, LANES:2 * LANES])
            v = qkv_ref[rows, 2 * LANES:3 * LANES]
            for hh in range(2):
                ck = list(_split3(-ck_ref[hh, kj:kj + 1, :]))
                kt_scr[hh, kj] = _put_rows(kt, sub_mine[hh], [one, one, one] + ck, _spare(hh))
                v_scr[hh, kj] = _put_cols(v, lane_mine[hh], [one], _spare(hh))

        for qi in range(nq):
            rows = slice(qi * ta, (qi + 1) * ta)
            q = qkv_ref[rows, 0:LANES] * 0.125
            cq = cq_ref[rows, :]
            qh = [_put_cols(q, lane_mine[hh], list(_split3(cq[:, hh * hd:hh * hd + 1])) + [one, one, one], _spare(hh))
                  for hh in range(2)]
            st = [(jnp.full((ta, 1), MASK_VALUE, F32), jnp.zeros((ta, LANES), F32))] * 2
            for kj in range(qi + 1):
                for hh in range(2):
                    m, acc = st[hh]
                    sc = _dot(qh[hh], kt_scr[hh, kj])
                    if kj == qi:
                        sc = jnp.where(causal, sc, MASK_VALUE)
                    mn = jnp.maximum(m, jnp.max(sc, axis=-1, keepdims=True))
                    p = jnp.exp(sc - mn).astype(BF16)
                    st[hh] = (mn, jnp.exp(m - mn) * acc + _dot(p, v_scr[hh, kj]))
            (ma, acca), (mb, accb) = st
            la = acca[:, hd:hd + 1]
            lb = accb[:, 0:1]
            y = jnp.where(lane_mine[0], acca * (1.0 / la), accb * (1.0 / lb))
            lse = jnp.where(lane_mine[0], ma + jnp.log(la), mb + jnp.log(lb)).T
            lse_ref[0, qi:qi + 1, :] = lse[0:1, :]
            lse_ref[1, qi:qi + 1, :] = lse[hd:hd + 1, :]
            y_ref[rows, :] = y
            g = g_ref[rows, :].astype(F32)
            ga_ref[rows, :] = (y * (g * _sigmoid(g))).astype(BF16)

    blk = lambda w: pl.BlockSpec((None, s, w), lambda i, p: (i, 0, p))
    rows5 = pl.BlockSpec((None, None, 2, nq, ta), lambda i, p: (i, p, 0, 0, 0))
    yatt3, lse5, ga = pl.pallas_call(
        body, name="attn_fwd", grid=(b, HEAD_PAIRS),
        in_specs=[blk(3 * LANES), blk(LANES), rows5, blk(LANES)],
        out_specs=[blk(LANES), rows5, pl.BlockSpec((s, LANES), lambda i, p: (i, p))],
        out_shape=[jax.ShapeDtypeStruct((b, s, D_MODEL), F32),
                   jax.ShapeDtypeStruct((b, HEAD_PAIRS, 2, nq, ta), F32),
                   jax.ShapeDtypeStruct((b * s, D_MODEL), BF16)],
        scratch_shapes=[pltpu.VMEM((2, nq, LANES, ta), BF16), pltpu.VMEM((2, nq, ta, LANES), BF16)],
        compiler_params=_cparams(("parallel", "parallel")),
    )(qkv3, cexp3, crow5, zrest3)
    return yatt3, lse5.reshape(b, HEADS, s), ga


def _attn_bwd(qkv3, do3, y3, lse, crow, cexp3):
    b, s, _ = qkv3.shape
    ta = ATT_TILE_BWD
    nq = s // ta
    hd = HEAD_DIM
    lse5 = lse.reshape(b, HEAD_PAIRS, 2, nq, ta)
    crow5 = crow.reshape(b, HEAD_PAIRS, 2, nq, ta)

    def body(qkv_ref, do_ref, y_ref, lse_ref, crow_ref, cexp_ref, dqkv_ref, dc_ref,
             qa_scr, doa_scr, qst_scr, dot_scr, kt_scr, vt_scr, dq_scr, rs_scr):
        pair = pl.program_id(1)
        lane = _iota((1, LANES), 1)
        sub = _iota((LANES, 1), 0)
        lane_mine = (lane < hd, lane >= hd)
        sub_mine = (sub < hd, sub >= hd)
        causal = _iota((ta, ta), 0) >= _iota((ta, ta), 1)
        one = jnp.ones((), BF16)
        zero = jnp.zeros((), BF16)

        @pl.when(pair == 0)
        def _():
            dc_ref[...] = jnp.zeros_like(dc_ref)

        for i in range(nq):
            rows = slice(i * ta, (i + 1) * ta)
            qs = qkv_ref[rows, 0:LANES] * 0.125
            qst = _transpose_bf16(qs)
            kt = _transpose_bf16(qkv_ref[rows, LANES:2 * LANES])
            vt = _transpose_bf16(qkv_ref[rows, 2 * LANES:3 * LANES])
            do = do_ref[rows, :]
            dof = do.astype(F32)
            dot = dof.T.astype(BF16)
            pr = y_ref[rows, :] * dof
            cq = cexp_ref[rows, :]
            lse_c = jnp.where(sub == 0, lse_ref[0, i:i + 1, :],
                              jnp.where(sub == 1, lse_ref[1, i:i + 1, :], 0.0)).T
            for hh in range(2):
                sp = _spare(hh)
                dsum = jnp.sum(jnp.where(lane_mine[hh], pr, 0.0), axis=-1, keepdims=True)
                bias = cq[:, hh * hd:hh * hd + 1] - lse_c[:, hh:hh + 1]
                qa_scr[hh, i] = _put_cols(qs, lane_mine[hh], list(_split3(bias)) + [one, one, one], sp)
                doa_scr[hh, i] = _put_cols(do, lane_mine[hh], list(_split3(-dsum)), sp)
                qst_scr[hh, i] = jnp.where(sub_mine[hh], qst, zero)
                dot_scr[hh, i] = jnp.where(sub_mine[hh], dot, zero)
                ck = list(_split3(-crow_ref[hh, i:i + 1, :]))
                kt_scr[hh, i] = _put_rows(kt, sub_mine[hh], [one, one, one] + ck, sp)
                vt_scr[hh, i] = _put_rows(vt, sub_mine[hh], [one, one, one], sp)
            dq_scr[i] = jnp.zeros((ta, LANES), F32)
            rs_scr[i] = jnp.zeros((ta, LANES), F32)

        for kj in range(nq):
            krows = slice(kj * ta, (kj + 1) * ta)
            k = qkv_ref[krows, LANES:2 * LANES]
            km = (jnp.where(lane_mine[0], k, zero), jnp.where(lane_mine[1], k, zero))
            dkt = jnp.zeros((LANES, ta), F32)
            dvt = jnp.zeros((LANES, ta), F32)
            dcp = [jnp.zeros((8, ta), F32), jnp.zeros((8, ta), F32)]
            for qi in range(kj, nq):
                dq = jnp.zeros((ta, LANES), F32)
                rs = []
                for hh in range(2):
                    sc = _dot(qa_scr[hh, qi], kt_scr[hh, kj])
                    if qi == kj:
                        sc = jnp.where(causal, sc, MASK_VALUE)
                    p = jnp.exp(sc)
                    dsf = p * _dot(doa_scr[hh, qi], vt_scr[hh, kj])
                    dcp[hh] = dcp[hh] + jnp.sum(dsf.reshape(ta // 8, 8, ta), axis=0)
                    rs.append(jnp.sum(dsf, axis=-1, keepdims=True))
                    ds = dsf.astype(BF16)
                    dq = dq + _dot(ds, km[hh])
                    dkt = dkt + _dot(qst_scr[hh, qi], ds)
                    dvt = dvt + _dot(dot_scr[hh, qi], p.astype(BF16))
                dq_scr[qi] += dq
                rs_scr[qi] += jnp.where(lane == 0, rs[0], jnp.where(lane == 1, rs[1], 0.0))
            dqkv_ref[krows, LANES:2 * LANES] = dkt.T.astype(BF16)
            dqkv_ref[krows, 2 * LANES:3 * LANES] = dvt.T.astype(BF16)
            dca = jnp.sum(dcp[0], axis=0, keepdims=True)
            dcb = jnp.sum(dcp[1], axis=0, keepdims=True)
            dcs = jnp.where(sub == 0, dca, jnp.where(sub == 1, dcb, 0.0)).T
            dc_ref[krows, :] += (jnp.where(lane == 2 * pair, -dcs[:, 0:1], 0.0)
                                 + jnp.where(lane == 2 * pair + 1, -dcs[:, 1:2], 0.0))
        for qi in range(nq):
            rows = slice(qi * ta, (qi + 1) * ta)
            dqkv_ref[rows, 0:LANES] = (dq_scr[qi] * 0.125).astype(BF16)
            rq = rs_scr[qi]
            dc_ref[rows, :] += (jnp.where(lane == 2 * pair, rq[:, 0:1], 0.0)
                                + jnp.where(lane == 2 * pair + 1, rq[:, 1:2], 0.0))

    blk = lambda w: pl.BlockSpec((None, s, w), lambda i, p: (i, 0, p))
    rows5 = pl.BlockSpec((None, None, 2, nq, ta), lambda i, p: (i, p, 0, 0, 0))
    by_rows = lambda: pltpu.VMEM((2, nq, ta, LANES), BF16)
    by_cols = lambda: pltpu.VMEM((2, nq, LANES, ta), BF16)
    return pl.pallas_call(
        body, name="attn_bwd", grid=(b, HEAD_PAIRS),
        in_specs=[blk(3 * LANES), blk(LANES), blk(LANES), rows5, rows5, blk(LANES)],
        out_specs=[pl.BlockSpec((s, 3 * LANES), lambda i, p: (i, p)),
                   pl.BlockSpec((None, s, LANES), lambda i, p: (i, 0, 0))],
        out_shape=[jax.ShapeDtypeStruct((b * s, 3 * D_MODEL), BF16), jax.ShapeDtypeStruct((b, s, LANES), F32)],
        scratch_shapes=[by_rows(), by_rows(), by_cols(), by_cols(), by_cols(), by_cols(),
                        pltpu.VMEM((nq, ta, LANES), F32), pltpu.VMEM((nq, ta, LANES), F32)],
        compiler_params=_cparams(("parallel", "arbitrary")),
    )(qkv3, do3, y3, lse5, crow5, cexp3)


def _shifted(v, ks, rows, s):
    low = rows[0:8, :]
    out = []
    for k in ks:
        r = pltpu.roll(v, k % s, 0)
        if k > 0:
            out.append(jnp.concatenate([jnp.where(low >= k, r[0:8, :], 0.0), r[8:, :]], axis=0))
        else:
            out.append(jnp.concatenate([r[:s - 8, :], jnp.where(low < 8 + k, r[s - 8:, :], 0.0)], axis=0))
    return out


def _rnn_common(xr, cw_ref, cb_ref, bda_ref, bdx_ref, ba_ref, bx_ref, lam_ref, s):
    rows = _iota((s, LANES), 0)
    x1, x2, x3 = _shifted(xr, (1, 2, 3), rows, s)
    xc = cb_ref[...] + cw_ref[0:1, :] * x3
    xc = xc + cw_ref[1:2, :] * x2
    xc = xc + cw_ref[2:3, :] * x1
    xc = xc + cw_ref[3:4, :] * xr
    xcb = xc.astype(BF16)
    r = _sigmoid(_dot(xcb, bda_ref[...]) + ba_ref[...])
    i = _sigmoid(_dot(xcb, bdx_ref[...]) + bx_ref[...])
    sp = _softplus(-lam_ref[...])
    log_a = (-RG_C * r) * sp
    a = jnp.exp(log_a)
    a2 = a * a
    sq = jnp.sqrt(jnp.maximum(_one_minus_exp(log_a + log_a, a2), 0.0))
    return rows, (x1, x2, x3), xc, xcb, r, i, sp, a, a2, sq


def _scan_down(a, u, rows, s, s1, s2):
    low = rows & 7
    for sh in (1, 2, 4):
        keep = low >= sh
        u = u + a * jnp.where(keep, pltpu.roll(u, sh, 0), 0.0)
        a = a * jnp.where(keep, pltpu.roll(a, sh, 0), 1.0)
    ng = s // 8
    s1[...] = a
    s2[...] = u
    at = s1[pl.ds(7, ng, stride=8), :]
    ut = s2[pl.ds(7, ng, stride=8), :]
    grow = _iota((ng, LANES), 0)
    sh = 1
    while sh < ng:
        keep = grow >= sh
        ut = ut + at * jnp.where(keep, pltpu.roll(ut, sh, 0), 0.0)
        if sh * 2 < ng:
            at = at * jnp.where(keep, pltpu.roll(at, sh, 0), 1.0)
        sh *= 2
    h_in = jnp.where(grow >= 1, pltpu.roll(ut, 1, 0), 0.0)
    for k in range(8):
        s1[pl.ds(k, ng, stride=8), :] = h_in
    return u + a * s1[...]


def _scan_up(a, g, rows, s, s1, s2):
    low = rows & 7
    for sh in (1, 2, 4):
        keep = low < 8 - sh
        g = g + a * jnp.where(keep, pltpu.roll(g, s - sh, 0), 0.0)
        a = a * jnp.where(keep, pltpu.roll(a, s - sh, 0), 1.0)
    ng = s // 8
    s1[...] = a
    s2[...] = g
    at = s1[pl.ds(0, ng, stride=8), :]
    gt = s2[pl.ds(0, ng, stride=8), :]
    grow = _iota((ng, LANES), 0)
    sh = 1
    while sh < ng:
        keep = grow < ng - sh
        gt = gt + at * jnp.where(keep, pltpu.roll(gt, ng - sh, 0), 0.0)
        if sh * 2 < ng:
            at = at * jnp.where(keep, pltpu.roll(at, ng - sh, 0), 1.0)
        sh *= 2
    g_in = jnp.where(grow < ng - 1, pltpu.roll(gt, ng - 1, 0), 0.0)
    for k in range(8):
        s1[pl.ds(k, ng, stride=8), :] = g_in
    return g + a * s1[...]


def _rnn_specs(s):
    blk = lambda off: pl.BlockSpec((None, s, LANES), lambda cb, i: (i, 0, off + cb))
    vec = lambda r: pl.BlockSpec((r, LANES), lambda cb, i: (0, cb))
    mat = pl.BlockSpec((None, LANES, LANES), lambda cb, i: (cb, 0, 0))
    return blk, vec, mat


def _rnn_fwd(zrest3, conv_w, conv_b, bda, bdx, ba, bx, lam):
    b, s, _ = zrest3.shape

    def body(xr_ref, g_ref, cw_ref, cb_ref, bda_ref, bdx_ref, ba_ref, bx_ref, lam_ref, h_ref, gr_ref, s1, s2):
        xr = xr_ref[...].astype(F32)
        rows, _, xc, _, _, i, _, a, _, sq = _rnn_common(
            xr, cw_ref, cb_ref, bda_ref, bdx_ref, ba_ref, bx_ref, lam_ref, s)
        h = _scan_down(a, sq * (i * xc), rows, s, s1, s2)
        h_ref[...] = h
        g = g_ref[...].astype(F32)
        gr_ref[...] = (h * (g * _sigmoid(g))).astype(BF16)

    blk, vec, mat = _rnn_specs(s)
    return pl.pallas_call(
        body, name="rnn_fwd", grid=(N_CBLK, b),
        in_specs=[blk(N_CBLK), blk(2 * N_CBLK), vec(CONV_W), vec(1), mat, mat, vec(1), vec(1), vec(1)],
        out_specs=[blk(0), pl.BlockSpec((s, LANES), lambda cb, i: (i, cb))],
        out_shape=[jax.ShapeDtypeStruct((b, s, D_MODEL), F32), jax.ShapeDtypeStruct((b * s, D_MODEL), BF16)],
        scratch_shapes=[pltpu.VMEM((s, LANES), F32), pltpu.VMEM((s, LANES), F32)],
        compiler_params=_cparams(("parallel", "parallel")),
    )(zrest3, zrest3, conv_w, conv_b, bda, bdx, ba, bx, lam)


def _rnn_bwd(zrest3, h3, dh3, conv_w, conv_b, bda, bdx, ba, bx, lam):
    b, s, _ = zrest3.shape

    def body(xr_ref, h_ref, dh_ref, cw_ref, cb_ref, bda_ref, bdx_ref, ba_ref, bx_ref, lam_ref,
             dxr_ref, pv_ref, dbd_ref, s1, s2):
        @pl.when(pl.program_id(1) == 0)
        def _():
            pv_ref[...] = jnp.zeros_like(pv_ref)
            dbd_ref[...] = jnp.zeros_like(dbd_ref)

        xr = xr_ref[...].astype(F32)
        rows, (x1, x2, x3), xc, xcb, r, i, sp, a, a2, sq = _rnn_common(
            xr, cw_ref, cb_ref, bda_ref, bdx_ref, ba_ref, bx_ref, lam_ref, s)
        (a_next,) = _shifted(a, (-1,), rows, s)
        g = _scan_up(a_next, dh_ref[...], rows, s, s1, s2)
        (hp,) = _shifted(h_ref[...], (1,), rows, s)
        da = g * hp
        dsq = g * (i * xc)
        di = g * (sq * xc)
        dxc = g * (sq * i)
        dlog = da * a - dsq * (a2 / sq)
        dr = dlog * (-RG_C * sp)
        dpr = dr * (r * (1.0 - r))
        dpi = di * (i * (1.0 - i))
        dprb = dpr.astype(BF16)
        dpib = dpi.astype(BF16)
        dxc = dxc + _dot_nt(dprb, bda_ref[...]) + _dot_nt(dpib, bdx_ref[...])

        up1, up2, up3 = _shifted(dxc, (-1, -2, -3), rows, s)
        dxr = cw_ref[3:4, :] * dxc + cw_ref[2:3, :] * up1 + cw_ref[1:2, :] * up2 + cw_ref[0:1, :] * up3
        dxr_ref[...] = dxr.astype(BF16)

        def colsum(v):
            return jnp.sum(v, axis=0, keepdims=True)

        pv_ref[0:1, :] += colsum(dxc * x3)
        pv_ref[1:2, :] += colsum(dxc * x2)
        pv_ref[2:3, :] += colsum(dxc * x1)
        pv_ref[3:4, :] += colsum(dxc * xr)
        pv_ref[4:5, :] += colsum(dxc)
        pv_ref[5:6, :] += colsum(dpr)
        pv_ref[6:7, :] += colsum(dpi)
        pv_ref[7:8, :] += colsum(dlog * r) * (RG_C * _sigmoid(-lam_ref[...]))
        dbd_ref[0] += _dot_tn(xcb, dprb)
        dbd_ref[1] += _dot_tn(xcb, dpib)

    blk, vec, mat = _rnn_specs(s)
    hblk = pl.BlockSpec((None, s, LANES), lambda cb, i: (i, 0, cb))
    return pl.pallas_call(
        body, name="rnn_bwd", grid=(N_CBLK, b),
        in_specs=[blk(N_CBLK), hblk, hblk, vec(CONV_W), vec(1), mat, mat, vec(1), vec(1), vec(1)],
        out_specs=[pl.BlockSpec((s, LANES), lambda cb, i: (i, cb)), pl.BlockSpec((8, LANES), lambda cb, i: (0, cb)),
                   pl.BlockSpec((None, 2, LANES, LANES), lambda cb, i: (cb, 0, 0, 0))],
        out_shape=[jax.ShapeDtypeStruct((b * s, D_MODEL), BF16), jax.ShapeDtypeStruct((8, D_MODEL), F32),
                   jax.ShapeDtypeStruct((N_CBLK, 2, LANES, LANES), F32)],
        scratch_shapes=[pltpu.VMEM((s, LANES), F32), pltpu.VMEM((s, LANES), F32)],
        compiler_params=_cparams(("parallel", "arbitrary")),
    )(zrest3, h3, dh3, conv_w, conv_b, bda, bdx, ba, bx, lam)


def _branch_merge(ga, gr, wa, wr, zrest):
    t = ga.shape[0]
    tm = min(512, t)
    tn = D_MODEL

    def body(ga_ref, gr_ref, wa_ref, wr_ref, mga_ref, mgr_ref, ya_ref, yr_ref, m_ref):
        ya = _dot(ga_ref[...], wa_ref[...])
        yr = _dot(gr_ref[...], wr_ref[...])
        ya_ref[...] = ya.astype(BF16)
        yr_ref[...] = yr.astype(BF16)
        m_ref[...] = (_sigmoid(mga_ref[...].astype(F32)) * ya + _sigmoid(mgr_ref[...].astype(F32)) * yr).astype(BF16)

    nj = D_MODEL // tn
    act = pl.BlockSpec((tm, D_MODEL), lambda i, j: (i, 0))
    wgt = pl.BlockSpec((D_MODEL, tn), lambda i, j: (0, j))
    out = pl.BlockSpec((tm, tn), lambda i, j: (i, j))
    return pl.pallas_call(
        body, name="branch_merge", grid=(t // tm, nj),
        in_specs=[act, act, wgt, wgt, pl.BlockSpec((tm, tn), lambda i, j: (i, 3 * nj + j)),
                  pl.BlockSpec((tm, tn), lambda i, j: (i, 4 * nj + j))],
        out_specs=[out, out, out],
        out_shape=[jax.ShapeDtypeStruct((t, D_MODEL), BF16), jax.ShapeDtypeStruct((t, D_MODEL), BF16),
                   jax.ShapeDtypeStruct((t, D_MODEL), BF16)],
        compiler_params=_cparams(("parallel", "parallel")),
    )(ga, gr, wa, wr, zrest, zrest)


def _out_loss(m, wout, x2, tgt2, wpost):
    t = m.shape[0]
    tm = min(512, t)

    def body(m_ref, w_ref, x_ref, t_ref, wp_ref, dy_ref, do_ref, acc_ref):
        @pl.when(pl.program_id(0) == 0)
        def _():
            acc_ref[...] = jnp.zeros_like(acc_ref)

        o = _dot(m_ref[...], w_ref[...])
        r2 = lax.rsqrt(jnp.mean(o * o, axis=-1, keepdims=True) + NORM_EPS)
        n = o * r2
        wp = wp_ref[...]
        err = (x_ref[...] + n * wp) - t_ref[...]
        dy = err * (1.0 / D_MODEL)
        dn = dy * wp
        do = r2 * (dn - n * jnp.mean(dn * n, axis=-1, keepdims=True))
        dy_ref[...] = dy
        do_ref[...] = do.astype(BF16)
        acc_ref[0:1, :] += jnp.sum(dy * n, axis=0, keepdims=True)
        acc_ref[1:2, :] += jnp.sum(err * err, axis=0, keepdims=True)

    row = pl.BlockSpec((tm, D_MODEL), lambda i: (i, 0))
    return pl.pallas_call(
        body, name="out_loss", grid=(t // tm,),
        in_specs=[row, pl.BlockSpec((D_MODEL, D_MODEL), lambda i: (0, 0)), row, row,
                  pl.BlockSpec((1, D_MODEL), lambda i: (0, 0))],
        out_specs=[row, row, pl.BlockSpec((8, D_MODEL), lambda i: (0, 0))],
        out_shape=[jax.ShapeDtypeStruct((t, D_MODEL), F32), jax.ShapeDtypeStruct((t, D_MODEL), BF16),
                   jax.ShapeDtypeStruct((8, D_MODEL), F32)],
        compiler_params=_cparams(("arbitrary",)),
    )(m, wout, x2, tgt2, wpost)


def _merge_bwd(do, wout, zrest, ya, yr):
    t = do.shape[0]
    tm = min(512, t)
    tn = D_MODEL
    nj = D_MODEL // tn

    def body(do_ref, w_ref, mga_ref, mgr_ref, ya_ref, yr_ref, dya_ref, dyr_ref, dmga_ref, dmgr_ref):
        dm = _dot_nt(do_ref[...], w_ref[...])
        sa = _sigmoid(mga_ref[...].astype(F32))
        sr = _sigmoid(mgr_ref[...].astype(F32))
        dya_ref[...] = (dm * sa).astype(BF16)
        dyr_ref[...] = (dm * sr).astype(BF16)
        dmga_ref[...] = (dm * ya_ref[...].astype(F32) * (sa * (1.0 - sa))).astype(BF16)
        dmgr_ref[...] = (dm * yr_ref[...].astype(F32) * (sr * (1.0 - sr))).astype(BF16)

    out = pl.BlockSpec((tm, tn), lambda i, j: (i, j))
    bf = jax.ShapeDtypeStruct((t, D_MODEL), BF16)
    return pl.pallas_call(
        body, name="merge_bwd", grid=(t // tm, nj),
        in_specs=[pl.BlockSpec((tm, D_MODEL), lambda i, j: (i, 0)), pl.BlockSpec((tn, D_MODEL), lambda i, j: (j, 0)),
                  pl.BlockSpec((tm, tn), lambda i, j: (i, 3 * nj + j)),
                  pl.BlockSpec((tm, tn), lambda i, j: (i, 4 * nj + j)), out, out],
        out_specs=[out, out, out, out],
        out_shape=[bf, bf, bf, bf],
        compiler_params=_cparams(("parallel", "parallel")),
    )(do, wout, zrest, zrest, ya, yr)


def _branch_bwd(dya, dyr, wa, wr, zrest, yatt, ylru):
    t = dya.shape[0]
    tm = min(512, t)
    tn = D_MODEL
    nj = D_MODEL // tn

    def body(dya_ref, dyr_ref, wa_ref, wr_ref, ga_ref, gr_ref, ya_ref, yl_ref,
             dyatt_ref, dga_ref, dyl_ref, dgr_ref):
        dga = _dot_nt(dya_ref[...], wa_ref[...])
        dgr = _dot_nt(dyr_ref[...], wr_ref[...])
        g = ga_ref[...].astype(F32)
        sg = _sigmoid(g)
        dyatt_ref[...] = (dga * (g * sg)).astype(BF16)
        dga_ref[...] = (dga * ya_ref[...] * (sg * (1.0 + g * (1.0 - sg)))).astype(BF16)
        g = gr_ref[...].astype(F32)
        sg = _sigmoid(g)
        dyl_ref[...] = dgr * (g * sg)
        dgr_ref[...] = (dgr * yl_ref[...] * (sg * (1.0 + g * (1.0 - sg)))).astype(BF16)

    act = pl.BlockSpec((tm, D_MODEL), lambda i, j: (i, 0))
    wgt = pl.BlockSpec((tn, D_MODEL), lambda i, j: (j, 0))
    out = pl.BlockSpec((tm, tn), lambda i, j: (i, j))
    bf = jax.ShapeDtypeStruct((t, D_MODEL), BF16)
    return pl.pallas_call(
        body, name="branch_bwd", grid=(t // tm, nj),
        in_specs=[act, act, wgt, wgt, pl.BlockSpec((tm, tn), lambda i, j: (i, j)),
                  pl.BlockSpec((tm, tn), lambda i, j: (i, 2 * nj + j)), out, out],
        out_specs=[out, out, out, out],
        out_shape=[bf, bf, jax.ShapeDtypeStruct((t, D_MODEL), F32), bf],
        compiler_params=_cparams(("parallel", "parallel")),
    )(dya, dyr, wa, wr, zrest, zrest, yatt, ylru)


def _dh_final(parts, after, x2, dy, wpre):
    t = x2.shape[0]
    tm = min(256, t)
    np_ = len(parts)

    def body(*refs):
        x_ref, dy_ref, w_ref = refs[2 * np_ + 1:2 * np_ + 4]
        gx_ref, pw_ref = refs[2 * np_ + 4:]

        @pl.when(pl.program_id(0) == 0)
        def _():
            pw_ref[...] = jnp.zeros_like(pw_ref)

        dh = _dot(refs[0][...], refs[np_][...])
        for p in range(1, np_):
            dh = dh + _dot(refs[p][...], refs[np_ + p][...])
        x = x_ref[...]
        r = lax.rsqrt(jnp.mean(x * x, axis=-1, keepdims=True) + NORM_EPS)
        xn = x * r
        dxn = dh * w_ref[...]
        gx_ref[...] = r * (dxn - xn * jnp.mean(dxn * xn, axis=-1, keepdims=True)) + dy_ref[...]
        pw_ref[0:1, :] += jnp.sum(dh * xn, axis=0, keepdims=True)

    row = pl.BlockSpec((tm, D_MODEL), lambda i: (i, 0))
    in_specs = [pl.BlockSpec((tm, dz.shape[1]), lambda i: (i, 0)) for dz, _ in parts]
    in_specs += [pl.BlockSpec(w.shape, lambda i: (0, 0), pipeline_mode=pl.Buffered(1)) for _, w in parts]
    in_specs += [pl.BlockSpec(after.shape, lambda i: (0, 0)), row, row, pl.BlockSpec((1, D_MODEL), lambda i: (0, 0))]
    return pl.pallas_call(
        body, name="dh_final", grid=(t // tm,),
        in_specs=in_specs,
        out_specs=[row, pl.BlockSpec((8, D_MODEL), lambda i: (0, 0))],
        out_shape=[jax.ShapeDtypeStruct((t, D_MODEL), F32), jax.ShapeDtypeStruct((8, D_MODEL), F32)],
        compiler_params=_cparams(("arbitrary",), vmem_mb=48),
    )(*[dz for dz, _ in parts], *[w for _, w in parts], after, x2, dy, wpre)


def _adamw(w, g, m, v):
    m = ADAM_B1 * m + (1.0 - ADAM_B1) * g
    v = ADAM_B2 * v + (1.0 - ADAM_B2) * (g * g)
    m_hat = m / (1.0 - ADAM_B1 ** ADAM_STEP)
    v_hat = v / (1.0 - ADAM_B2 ** ADAM_STEP)
    delta = -ADAM_LR * (m_hat / (jnp.sqrt(v_hat) + ADAM_EPS) + ADAM_WD * w)
    return delta, m, v


def _reduce_adamw(own, parts, place, w, m, v, name):
    r, c = w.shape
    blk, nblk, at = _blocks_2d(r, c)

    def body(place_ref, own_ref, p_ref, w_ref, m_ref, v_ref, g_ref, d_ref, nm_ref, nv_ref):
        mine = place_ref[1]
        own_blk = own_ref[...]
        g = jnp.where(mine == 0, own_blk, p_ref[0].astype(F32))
        for j in range(1, N_CHIPS):
            g = g + jnp.where(mine == j, own_blk, p_ref[j].astype(F32))
        d, nm, nv = _adamw(w_ref[...], g, m_ref[...], v_ref[...])
        g_ref[...] = g
        d_ref[...] = d
        nm_ref[...] = nm
        nv_ref[...] = nv

    row = pl.BlockSpec(blk, lambda i, pr: at(i))
    sh = jax.ShapeDtypeStruct((r, c), F32)
    grid_spec = pltpu.PrefetchScalarGridSpec(
        num_scalar_prefetch=1, grid=(nblk,),
        in_specs=[row, pl.BlockSpec((N_CHIPS,) + blk, lambda i, pr: (0,) + at(i)), row, row, row],
        out_specs=[row, row, row, row])
    return pl.pallas_call(
        body, name=name, grid_spec=grid_spec, out_shape=[sh, sh, sh, sh],
        compiler_params=_cparams(("parallel",)),
    )(place, own, parts, w, m, v)


def _reduce_adamw_stacked(own, parts, place, triples, name):
    n = len(triples)
    _, r, c = triples[0][0].shape

    def body(place_ref, own_ref, p_ref, *refs):
        ins, outs = refs[:3 * n], refs[3 * n:]
        mine = place_ref[1]
        for i in range(n):
            rows = slice(i * r, (i + 1) * r)
            own_blk = own_ref[rows, :]
            g = jnp.where(mine == 0, own_blk, p_ref[0, rows, :].astype(F32))
            for j in range(1, N_CHIPS):
                g = g + jnp.where(mine == j, own_blk, p_ref[j, rows, :].astype(F32))
            d, nm, nv = _adamw(ins[3 * i][0], g, ins[3 * i + 1][0], ins[3 * i + 2][0])
            for k, val in enumerate((g, d, nm, nv)):
                outs[4 * i + k][0] = val

    whole = lambda shape: pl.BlockSpec(shape, lambda i, pr: (0,) * len(shape))
    grid_spec = pltpu.PrefetchScalarGridSpec(
        num_scalar_prefetch=1, grid=(1,),
        in_specs=[whole(own.shape), whole(parts.shape)] + [whole((1, r, c))] * (3 * n),
        out_specs=[whole((1, r, c))] * (4 * n))
    res = pl.pallas_call(
        body, name=name, grid_spec=grid_spec,
        out_shape=[jax.ShapeDtypeStruct((1, r, c), F32)] * (4 * n),
        compiler_params=_cparams(("arbitrary",)),
    )(place, own, parts, *[a for t3 in triples for a in t3])
    return [res[4 * i:4 * i + 4] for i in range(n)]


def _interleave_qkv(a):
    lead = a.shape[:-1]
    return a.reshape(lead + (3, HEAD_PAIRS, LANES)).swapaxes(-3, -2).reshape(lead + (3 * D_MODEL,))


def _deinterleave_qkv(a):
    lead = a.shape[:-1]
    return a.reshape(lead + (HEAD_PAIRS, 3, LANES)).swapaxes(-3, -2).reshape(lead + (3 * D_MODEL,))


def _interleave_rows(a):
    return a.reshape(3, HEAD_PAIRS, LANES, a.shape[1]).swapaxes(0, 1).reshape(a.shape)


def _deinterleave_rows(a):
    return a.reshape(HEAD_PAIRS, 3, LANES, a.shape[1]).swapaxes(0, 1).reshape(a.shape)


def _pack_small(pre, conv_b, rg_ba, rg_bx, lam, post, loss_row, b_in, conv_w_full, rg_wa, rg_wx):
    z = jnp.zeros((1, D_MODEL), F32)
    b_used = jnp.concatenate([b_in[:, 0:3 * D_MODEL], b_in[:, 3 * D_MODEL + HEADS:IN_TOTAL]], axis=1)
    b_f = jnp.pad(b_in[:, 3 * D_MODEL:3 * D_MODEL + HEADS], ((0, 0), (0, D_MODEL - HEADS)))
    return jnp.concatenate([
        pre, conv_b, rg_ba, rg_bx, lam, post, loss_row, z,
        b_used.reshape(9, D_MODEL), b_f, conv_w_full, z, z,
        rg_wa.reshape(64, D_MODEL), rg_wx.reshape(64, D_MODEL)], axis=0)


def _unpack_small(p):
    b_used = p[8:17].reshape(1, 9 * D_MODEL)
    b_in = jnp.concatenate([b_used[:, 0:3 * D_MODEL], p[17:18, 0:HEADS], b_used[:, 3 * D_MODEL:]], axis=1)
    return dict(pre_norm_w=p[0:1], conv_b=p[1:2], rg_ba=p[2:3], rg_bx=p[3:4], rg_lambda=p[4:5],
                post_norm_w=p[5:6], loss_row=p[6:7], b_in=b_in, conv_w_full=p[18:22],
                rg_wa=p[24:88].reshape(1, 16, 64, 64), rg_wx=p[88:152].reshape(1, 16, 64, 64))


def _reduce_small(parts, first, w, m, v, vectors):
    nvec = len(vectors)

    def body(p_ref, f_ref, w_ref, m_ref, v_ref, *refs):
        ins, outs = refs[:3 * nvec], refs[3 * nvec:]
        g = p_ref[0]
        g0 = f_ref[0, 0:1, :]
        for j in range(1, N_DEV):
            g = g + p_ref[j]
            g0 = g0 + f_ref[j, 0:1, :]
        d, nm, nv = _adamw(w_ref[...], g, m_ref[...], v_ref[...])
        for k, val in enumerate((g, d, nm, nv)):
            outs[k][...] = val
        for i in range(nvec):
            gi = g0 if i == 0 else g[i:i + 1, :]
            di, nmi, nvi = _adamw(ins[3 * i][...], gi, ins[3 * i + 1][...], ins[3 * i + 2][...])
            for k, val in enumerate((gi, di, nmi, nvi)):
                outs[4 + 4 * i + k][...] = val
        outs[-1][...] = jnp.zeros((8, LANES), F32) + (0.5 / D_MODEL) * jnp.sum(g[LOSS_ROW:LOSS_ROW + 1, :])

    sh = jax.ShapeDtypeStruct((SMALL_ROWS, D_MODEL), F32)
    vec = jax.ShapeDtypeStruct((1, D_MODEL), F32)
    res = pl.pallas_call(
        body, name="reduce_small",
        out_shape=[sh, sh, sh, sh] + [vec] * (4 * nvec) + [jax.ShapeDtypeStruct((8, LANES), F32)],
    )(parts, first, w, m, v, *[a for t3 in vectors for a in t3])
    return res[:4], [res[4 + 4 * i:8 + 4 * i] for i in range(nvec)], res[-1]


def kernel(x, pre_norm_w, w_in, b_in, conv_w, conv_b, rg_wa, rg_ba, rg_wx, rg_bx, rg_lambda, w_branch_a, w_branch_r, w_out, post_norm_w, loss_target, m_pre_norm_w, m_w_in, m_b_in, m_conv_w, m_conv_b, m_rg_wa, m_rg_ba, m_rg_wx, m_rg_bx, m_rg_lambda, m_w_branch_a, m_w_branch_r, m_w_out, m_post_norm_w, v_pre_norm_w, v_w_in, v_b_in, v_conv_w, v_conv_b, v_rg_wa, v_rg_ba, v_rg_wx, v_rg_bx, v_rg_lambda, v_w_branch_a, v_w_branch_r, v_w_out, v_post_norm_w):
    b, s, _ = x.shape
    t = b * s
    me = 4 * lax.axis_index("x") + 2 * lax.axis_index("y") + lax.axis_index("c")
    shard_rows = D_MODEL // N_DEV

    place = jnp.stack([lax.axis_index("c"), 2 * lax.axis_index("x") + lax.axis_index("y")]).astype(jnp.int32)
    w_in_all = _gather(w_in[0].T.astype(BF16), "gather_w_in")
    wt_full = w_in_all.reshape(IN_TOTAL, D_MODEL)
    conv_terms = jnp.concatenate(_split3(conv_w[0]), axis=0)
    conv_pad = jnp.pad(conv_terms, ((0, 16 - 3 * CONV_W), (0, D_MODEL - LANES)))
    sq_stack = jnp.concatenate([w_branch_a[0].astype(BF16), w_branch_r[0].astype(BF16), w_out[0].astype(BF16),
                                conv_pad], axis=0)
    sq_sems, sq_src, sq_land, sq_token = _gather_start(sq_stack, w_in_all, "gather_w_sq_start")

    w_qkv = _interleave_rows(wt_full[0:3 * D_MODEL])
    w_f = jnp.pad(wt_full[3 * D_MODEL:3 * D_MODEL + HEADS], ((0, LANES - HEADS), (0, 0)))
    w_rest = wt_full[3 * D_MODEL + HEADS:IN_USED]
    b_qkv = _interleave_qkv(b_in[:, 0:3 * D_MODEL]) + sq_token[0, 0]
    b_f = jnp.pad(b_in[:, 3 * D_MODEL:3 * D_MODEL + HEADS], ((0, 0), (0, LANES - HEADS)))
    b_rest = b_in[:, 3 * D_MODEL + HEADS:IN_USED]

    def blockdiag(w):
        w2 = w.reshape(N_CBLK, 2, HEAD_DIM, HEAD_DIM)
        zz = jnp.zeros((N_CBLK, HEAD_DIM, HEAD_DIM), w.dtype)
        top = jnp.concatenate([w2[:, 0], zz], axis=2)
        bot = jnp.concatenate([zz, w2[:, 1]], axis=2)
        return jnp.concatenate([top, bot], axis=1).astype(BF16)

    bda, bdx = blockdiag(rg_wa[0]), blockdiag(rg_wx[0])

    x2 = x.reshape(t, D_MODEL)
    tgt2 = loss_target.reshape(t, D_MODEL)
    h, qkv, zf = _prenorm_inproj(x2, pre_norm_w, w_qkv, b_qkv, w_f, b_f)
    zrest = _mm_bias(h, w_rest, b_rest, BF16, "inproj_rest")
    qkv3 = qkv.reshape(b, s, 3 * D_MODEL)
    zrest3 = zrest.reshape(b, s, 5 * D_MODEL)
    zf3 = zf.reshape(b, s, LANES)
    cexp3, crow = _fgate_fwd(zf3)
    yatt3, lse, ga = _attn_fwd(qkv3, cexp3, crow, zrest3)

    sq_all = _gather_wait(sq_sems, sq_src, sq_land, ga, "gather_w_sq_wait")
    sq_all = lax.dynamic_update_slice(sq_all, sq_stack[None], (me, 0, 0))
    wa = sq_all[:, 0:shard_rows].reshape(D_MODEL, D_MODEL)
    wr = sq_all[:, shard_rows:2 * shard_rows].reshape(D_MODEL, D_MODEL)
    wo = sq_all[:, 2 * shard_rows:3 * shard_rows].reshape(D_MODEL, D_MODEL)
    conv_all = sq_all[:, 3 * shard_rows:3 * shard_rows + 3 * CONV_W, 0:LANES].astype(F32)
    conv_all = (conv_all[:, 0:CONV_W] + conv_all[:, CONV_W:2 * CONV_W]) + conv_all[:, 2 * CONV_W:3 * CONV_W]
    conv_full = conv_all.transpose(1, 0, 2).reshape(CONV_W, D_MODEL)

    ylru3, gr = _rnn_fwd(zrest3, conv_full, conv_b, bda, bdx, rg_ba, rg_bx, rg_lambda)
    ya, yr, mm = _branch_merge(ga, gr, wa, wr, zrest)
    dy, do, acc_out = _out_loss(mm, wo, x2, tgt2, post_norm_w)

    dya, dyr, dz_mga, dz_mgr = _merge_bwd(do, wo, zrest, ya, yr)
    dyatt, dz_ga, dylru, dz_gr = _branch_bwd(dya, dyr, wa, wr, zrest, yatt3.reshape(t, D_MODEL),
                                             ylru3.reshape(t, D_MODEL))
    dz_xr, pvec, dbd = _rnn_bwd(zrest3, ylru3, dylru.reshape(b, s, D_MODEL), conv_full, conv_b, bda, bdx,
                                rg_ba, rg_bx, rg_lambda)
    dz_qkv, dc3 = _attn_bwd(qkv3, dyatt.reshape(b, s, D_MODEL), yatt3, lse, crow, cexp3)
    dz_f = _fgate_bwd(dc3, zf3)

    dw_qkv, db_qkv = _mm_tn(dz_qkv, h, "dw_qkv")
    dw_f, db_f = _mm_tn(dz_f, h, "dw_f")
    dw_parts, db_parts = [], []
    for nm, dzp in (("ga", dz_ga), ("xr", dz_xr), ("gr", dz_gr), ("mga", dz_mga), ("mgr", dz_mgr)):
        dwp, dbp = _mm_tn(dzp, h, "dw_" + nm)
        dw_parts.append(dwp)
        db_parts.append(dbp[0:1])

    zeros_tail = jnp.zeros((IN_TOTAL - IN_USED, D_MODEL), F32)
    dwt_full = jnp.concatenate([_deinterleave_rows(dw_qkv), dw_f[0:HEADS]] + dw_parts + [zeros_tail], axis=0)
    dw_in_send = dwt_full.reshape(N_CHIPS, 2, W_SHARD, D_MODEL).transpose(1, 0, 2, 3)
    swp_sems, dw_in_src, swp_land, swp_token = _swap_start(dw_in_send, db_f, "swap_dw_in_start")
    dw_a, _ = _mm_tn(ga, dya, "dw_a", after=swp_token)
    dw_r, _ = _mm_tn(gr, dyr, "dw_r", after=swp_token)
    dw_o, _ = _mm_tn(mm, do, "dw_o", after=swp_token)
    dw_in_send, sib_in = _swap_wait(swp_sems, dw_in_src, swp_land, dw_o, "swap_dw_in_wait")
    by_dest = lambda a: a.reshape(N_CHIPS, 2, shard_rows, D_MODEL).transpose(1, 0, 2, 3)
    dw_sq_send = jnp.concatenate([by_dest(dw_a), by_dest(dw_r), by_dest(dw_o)], axis=2)

    db_in_full = jnp.concatenate([_deinterleave_qkv(db_qkv[0:1]), db_f[0:1, 0:HEADS]] + db_parts
                                 + [jnp.zeros((1, IN_TOTAL - IN_USED), F32)], axis=1)
    d_rg_wa = jnp.stack([dbd[:, 0, 0:HEAD_DIM, 0:HEAD_DIM], dbd[:, 0, HEAD_DIM:, HEAD_DIM:]], axis=1)
    d_rg_wx = jnp.stack([dbd[:, 1, 0:HEAD_DIM, 0:HEAD_DIM], dbd[:, 1, HEAD_DIM:, HEAD_DIM:]], axis=1)
    small_g = _pack_small(jnp.zeros((1, D_MODEL), F32), pvec[4:5], pvec[5:6], pvec[6:7], pvec[7:8], acc_out[0:1],
                          acc_out[1:2], db_in_full, pvec[0:4], d_rg_wa, d_rg_wx)
    sm_sems, sm_src, sm_land, sm_token = _gather_start(small_g, dw_o, "gather_small_start")

    dw_sq_send = dw_sq_send + sm_token[0, 0]
    (sib_sq,) = _swap_with_sibling([dw_sq_send], "swap_dw_sq")
    chip_in, own_in = _pair_add(dw_in_send, sib_in, place, "pair_add_in")
    chip_sq, own_sq = _pair_add(dw_sq_send, sib_sq, place, "pair_add_sq")
    sems, sent, lands, token = _exchange_chips_start([chip_in, chip_sq], "exchange_dw_start")

    wt = lambda lo: w_rest[lo * D_MODEL:(lo + 1) * D_MODEL]
    grad_x2, acc_pre = _dh_final(
        [(dz_qkv, w_qkv), (dz_f, w_f), (dz_ga, wt(0)), (dz_xr, wt(1)), (dz_gr, wt(2)), (dz_mga, wt(3)),
         (dz_mgr, wt(4))], token, x2, dy, pre_norm_w)
    pre_sems, pre_src, pre_land, pre_token = _gather_start(acc_pre, grad_x2, "gather_pre_start")
    recv_in, recv_sq = _exchange_chips_wait(sems, sent, lands, pre_token, "exchange_dw_wait")

    g_in, d_in, nm_in, nv_in = [a.T for a in _reduce_adamw(
        own_in, recv_in, place, w_in[0].T, m_w_in[0].T, v_w_in[0].T, "adamw_w_in")]
    sq_out = _reduce_adamw_stacked(
        own_sq, recv_sq, place,
        [(w_branch_a, m_w_branch_a, v_w_branch_a), (w_branch_r, m_w_branch_r, v_w_branch_r),
         (w_out, m_w_out, v_w_out)], "adamw_w_sq")
    pre_all = _gather_wait(pre_sems, pre_src, pre_land, sq_out[2][1], "gather_pre_wait")
    pre_all = lax.dynamic_update_slice(pre_all, acc_pre[None], (me, 0, 0))
    small_all = _gather_wait(sm_sems, sm_src, sm_land, pre_all, "gather_small_wait")
    small_all = lax.dynamic_update_slice(small_all, small_g[None], (me, 0, 0))

    def place_conv(a):
        return lax.dynamic_update_slice(jnp.zeros((CONV_W, D_MODEL), F32), a[0], (0, me * LANES))

    zrow = jnp.zeros((1, D_MODEL), F32)
    vector_names = ["pre_norm_w", "conv_b", "rg_ba", "rg_bx", "rg_lambda", "post_norm_w"]
    vectors = [(pre_norm_w, m_pre_norm_w, v_pre_norm_w), (conv_b, m_conv_b, v_conv_b), (rg_ba, m_rg_ba, v_rg_ba),
               (rg_bx, m_rg_bx, v_rg_bx), (rg_lambda, m_rg_lambda, v_rg_lambda),
               (post_norm_w, m_post_norm_w, v_post_norm_w)]
    small_w = _pack_small(zrow, zrow, zrow, zrow, zrow, zrow, zrow, b_in, place_conv(conv_w), rg_wa[0], rg_wx[0])
    small_m = _pack_small(zrow, zrow, zrow, zrow, zrow, zrow, zrow, m_b_in, place_conv(m_conv_w), m_rg_wa[0],
                          m_rg_wx[0])
    small_v = _pack_small(zrow, zrow, zrow, zrow, zrow, zrow, zrow, v_b_in, place_conv(v_conv_w), v_rg_wa[0],
                          v_rg_wx[0])
    packed, vector_out, loss_tile = _reduce_small(small_all, pre_all, small_w, small_m, small_v, vectors)
    outs_small = [_unpack_small(p) for p in packed]
    loss = loss_tile[0, 0]

    def leaf(kind, name):
        if name == "w_in":
            return (g_in, d_in, nm_in, nv_in)[kind][None]
        if name in ("w_branch_a", "w_branch_r", "w_out"):
            return sq_out[("w_branch_a", "w_branch_r", "w_out").index(name)][kind]
        if name == "conv_w":
            return lax.dynamic_slice(outs_small[kind]["conv_w_full"], (0, me * LANES), (CONV_W, LANES))[None]
        if name in vector_names:
            return vector_out[vector_names.index(name)][kind]
        return outs_small[kind][name]

    names = ["pre_norm_w", "w_in", "b_in", "conv_w", "conv_b", "rg_wa", "rg_ba", "rg_wx", "rg_bx", "rg_lambda",
             "w_branch_a", "w_branch_r", "w_out", "post_norm_w"]
    out = [loss, grad_x2.reshape(b, s, D_MODEL)]
    for kind in range(4):
        out += [leaf(kind, nm) for nm in names]
    return tuple(out)
```

```python
import jax
import jax.numpy as jnp
from jax import lax
from jax.experimental import pallas as pl
from jax.experimental.pallas import tpu as pltpu

F32 = jnp.float32
BF16 = jnp.bfloat16

N_DEV = 8
D_MODEL = 1024
HEADS = 16
HEAD_DIM = 64
HEAD_PAIRS = HEADS // 2
LANES = 128
N_CBLK = D_MODEL // LANES
CONV_W = 4
RG_C = 8.0
NORM_EPS = 1e-6
MASK_VALUE = -1e30
IN_USED = 8208
IN_TOTAL = 9232
W_SHARD = IN_TOTAL // N_DEV

ADAM_LR = 0.001
ADAM_B1 = 0.9
ADAM_B2 = 0.999
ADAM_EPS = 1e-08
ADAM_WD = 0.01
ADAM_STEP = 10

ATT_TILE_FWD = 256
ATT_TILE_BWD = 512
SCAN_TILE = 256
SMALL_ROWS = 152
LOSS_ROW = 6


def _cparams(sem=None, vmem_mb=None):
    kw = {}
    if sem is not None:
        kw["dimension_semantics"] = sem
    if vmem_mb is not None:
        kw["vmem_limit_bytes"] = vmem_mb * 1024 * 1024
    return pltpu.CompilerParams(**kw)


def _sigmoid(x):
    return 1.0 / (1.0 + jnp.exp(-x))


def _softplus(x):
    return jnp.maximum(x, 0.0) + jnp.log1p(jnp.exp(-jnp.abs(x)))


def _one_minus_exp(y, exp_y):
    series = -y * (1.0 + y * (1.0 / 2 + y * (1.0 / 6 + y * (1.0 / 24 + y * (1.0 / 120)))))
    return jnp.where(y > -0.0625, series, 1.0 - exp_y)


def _split3(x):
    hi = x.astype(BF16)
    r1 = x - hi.astype(F32)
    mid = r1.astype(BF16)
    lo = (r1 - mid.astype(F32)).astype(BF16)
    return hi, mid, lo


def _dot(a, b):
    return jnp.dot(a, b, preferred_element_type=F32)


def _dot_nt(a, b):
    return lax.dot_general(a, b, (((1,), (1,)), ((), ())), preferred_element_type=F32)


def _dot_tn(a, b):
    return lax.dot_general(a, b, (((0,), (0,)), ((), ())), preferred_element_type=F32)


def _iota(shape, dim):
    return lax.broadcasted_iota(jnp.int32, shape, dim)


_ANY = pl.BlockSpec(memory_space=pl.ANY)
_MESH = pl.DeviceIdType.MESH
N_CHIPS = 4


def _place():
    x, y, c = lax.axis_index("x"), lax.axis_index("y"), lax.axis_index("c")
    other_chips = [(1 - x, y), (x, 1 - y), (1 - x, 1 - y)]
    return x, y, c, other_chips


def _gather(x_shard, name):
    def body(x_ref, out_ref, send_sems, recv_sems, local_sem):
        x, y, c, chips = _place()
        me, sibling = (x, y, c), (x, y, 1 - c)

        def slot(p):
            return out_ref.at[4 * p[0] + 2 * p[1] + p[2]]

        def copy(k, block, to, src=None):
            return pltpu.make_async_remote_copy(
                src_ref=slot(block) if src is None else src, dst_ref=slot(block),
                send_sem=send_sems.at[k], recv_sem=recv_sems.at[k], device_id=to, device_id_type=_MESH)

        mine = pltpu.make_async_copy(x_ref, slot(me), local_sem)
        mine.start()
        first = [copy(0, me, sibling, src=x_ref)]
        first += [copy(1 + j, me, (*chip, c), src=x_ref) for j, chip in enumerate(chips)]
        for cp in first:
            cp.start()
        passed = [copy(4 + j, (*chip, c), sibling) for j, chip in enumerate(chips)]
        for j, chip in enumerate(chips):
            copy(1 + j, (*chip, c), me).wait_recv()
            passed[j].start()
        copy(0, sibling, me).wait_recv()
        for j, chip in enumerate(chips):
            copy(4 + j, (*chip, 1 - c), me).wait_recv()
        for cp in first + passed:
            cp.wait_send()
        mine.wait()

    return pl.pallas_call(
        body, name=name,
        out_shape=jax.ShapeDtypeStruct((N_DEV,) + tuple(x_shard.shape), x_shard.dtype),
        in_specs=[_ANY], out_specs=_ANY,
        scratch_shapes=[pltpu.SemaphoreType.DMA((7,)), pltpu.SemaphoreType.DMA((7,)), pltpu.SemaphoreType.DMA],
    )(x_shard)


def _blocks_2d(r, c):
    if r % 128 == 0:
        return (128, c), r // 128, lambda i: (i, 0)
    return (r, 256), c // 256, lambda i: (0, i)


def _pair_add(src, recv, place, name):
    _, _, r, c = src.shape
    blk, nblk, at = _blocks_2d(r, c)

    def body(place_ref, a_ref, b_ref, q16_ref, own_ref):
        q = a_ref[...] + b_ref[...]
        q16_ref[...] = q.astype(BF16)

        @pl.when(pl.program_id(1) == place_ref[1])
        def _():
            own_ref[...] = q

    grid_spec = pltpu.PrefetchScalarGridSpec(
        num_scalar_prefetch=1, grid=(nblk, N_CHIPS),
        in_specs=[pl.BlockSpec((None, None) + blk, lambda i, j, pr: (pr[0], j) + at(i)),
                  pl.BlockSpec((None,) + blk, lambda i, j, pr: (j,) + at(i))],
        out_specs=[pl.BlockSpec((None,) + blk, lambda i, j, pr: (j,) + at(i)),
                   pl.BlockSpec(blk, lambda i, j, pr: at(i))])
    return pl.pallas_call(
        body, name=name, grid_spec=grid_spec,
        out_shape=[jax.ShapeDtypeStruct((N_CHIPS, r, c), BF16), jax.ShapeDtypeStruct((r, c), F32)],
        compiler_params=_cparams(("parallel", "arbitrary")),
    )(place, src, recv)


_HBM = pl.BlockSpec(memory_space=pltpu.HBM)
_SEM = pl.BlockSpec(memory_space=pltpu.SEMAPHORE)
_DATAFLOW = pltpu.SideEffectType.DATAFLOW_SIDE_EFFECTING


def _chip_copy(src_ref, land_ref, send_sem, recv_sem, k, chips, c, land):
    chip = chips[k]
    return pltpu.make_async_remote_copy(
        src_ref=src_ref.at[2 * chip[0] + chip[1]], dst_ref=land_ref.at[land],
        send_sem=send_sem, recv_sem=recv_sem, device_id=(*chip, c), device_id_type=_MESH)


def _exchange_chips_start(srcs, name):
    n = len(srcs)
    ncp = 3 * n

    def body(*refs):
        src_refs, land_refs = refs[:n], refs[n:2 * n]
        sems = refs[4 * n:4 * n + 2 * ncp]
        token = refs[-1]
        x, y, c, chips = _place()
        for i in range(n):
            for k in range(3):
                j = 3 * i + k
                _chip_copy(src_refs[i], land_refs[i], sems[j], sems[ncp + j], k, chips, c, 2 * x + y).start()
        token[...] = jnp.zeros_like(token)

    hbm = [pltpu.HBM(a.shape, a.dtype) for a in srcs]
    lands = [pltpu.with_memory_space_constraint(lax.empty(a.shape, a.dtype), pltpu.HBM) for a in srcs]
    res = pl.pallas_call(
        body, name=name,
        out_shape=(*hbm, *hbm, *([pltpu.SemaphoreType.DMA(())] * (2 * ncp)), jax.ShapeDtypeStruct((8, LANES), F32)),
        in_specs=[_HBM] * (2 * n),
        out_specs=(*([_HBM] * (2 * n)), *([_SEM] * (2 * ncp)), pl.BlockSpec(memory_space=pltpu.VMEM)),
        input_output_aliases={i: i for i in range(2 * n)},
        compiler_params=pltpu.CompilerParams(has_side_effects=_DATAFLOW),
    )(*[pltpu.with_memory_space_constraint(a, pltpu.HBM) for a in srcs], *lands)
    return list(res[2 * n:2 * n + 2 * ncp]), list(res[:n]), list(res[n:2 * n]), res[-1]


def _exchange_chips_wait(sems, srcs, lands, after, name):
    n = len(srcs)
    ncp = 3 * n

    def body(*refs):
        src_refs, land_refs = refs[:n], refs[n:2 * n]
        sem_refs = refs[2 * n:2 * n + 2 * ncp]
        x, y, c, chips = _place()
        for i in range(n):
            for k in range(3):
                j = 3 * i + k
                cp = _chip_copy(src_refs[i], land_refs[i], sem_refs[j], sem_refs[ncp + j], k, chips, c,
                                2 * chips[k][0] + chips[k][1])
                cp.wait_send()
                cp.wait_recv()

    hbm = [pltpu.HBM(a.shape, a.dtype) for a in srcs]
    res = pl.pallas_call(
        body, name=name, out_shape=(*hbm, *hbm),
        in_specs=[_HBM] * (2 * n) + [_SEM] * (2 * ncp) + [_ANY], out_specs=tuple([_HBM] * (2 * n)),
        input_output_aliases={i: i for i in range(2 * n)},
        compiler_params=pltpu.CompilerParams(has_side_effects=_DATAFLOW),
    )(*srcs, *lands, *sems, after)
    return list(res[n:2 * n])


def _swap_start(src, after, name):
    def body(src_ref, land_ref, after_ref, src_thru, land_thru, send_sem, recv_sem, token):
        x, y, c, _ = _place()
        pltpu.make_async_remote_copy(src_ref=src_ref.at[1 - c], dst_ref=land_ref, send_sem=send_sem,
                                     recv_sem=recv_sem, device_id=(x, y, 1 - c), device_id_type=_MESH).start()
        token[...] = jnp.zeros_like(token)

    land = pltpu.with_memory_space_constraint(lax.empty(src.shape[1:], src.dtype), pltpu.HBM)
    res = pl.pallas_call(
        body, name=name,
        out_shape=(pltpu.HBM(src.shape, src.dtype), pltpu.HBM(land.shape, land.dtype),
                   pltpu.SemaphoreType.DMA(()), pltpu.SemaphoreType.DMA(()), jax.ShapeDtypeStruct((8, LANES), F32)),
        in_specs=[_HBM, _HBM, _ANY],
        out_specs=(_HBM, _HBM, _SEM, _SEM, pl.BlockSpec(memory_space=pltpu.VMEM)),
        input_output_aliases={0: 0, 1: 1},
        compiler_params=pltpu.CompilerParams(has_side_effects=_DATAFLOW),
    )(pltpu.with_memory_space_constraint(src, pltpu.HBM), land, after)
    return [res[2], res[3]], res[0], res[1], res[-1]


def _swap_wait(sems, src, land, after, name):
    def body(src_ref, land_ref, send_sem, recv_sem, after_ref, src_out, land_out):
        x, y, c, _ = _place()
        cp = pltpu.make_async_remote_copy(src_ref=src_ref.at[1 - c], dst_ref=land_ref, send_sem=send_sem,
                                          recv_sem=recv_sem, device_id=(x, y, 1 - c), device_id_type=_MESH)
        cp.wait_send()
        cp.wait_recv()

    res = pl.pallas_call(
        body, name=name, out_shape=(pltpu.HBM(src.shape, src.dtype), pltpu.HBM(land.shape, land.dtype)),
        in_specs=[_HBM, _HBM, _SEM, _SEM, _ANY], out_specs=(_HBM, _HBM),
        input_output_aliases={0: 0, 1: 1},
        compiler_params=pltpu.CompilerParams(has_side_effects=_DATAFLOW),
    )(src, land, *sems, after)
    return res[0], res[1]


def _peer_copy(src_ref, land_ref, send_sem, recv_sem, k, place, land):
    x, y, c = place
    peer = (1 - x if k & 4 else x, 1 - y if k & 2 else y, 1 - c if k & 1 else c)
    return pltpu.make_async_remote_copy(
        src_ref=src_ref, dst_ref=land_ref.at[land], send_sem=send_sem, recv_sem=recv_sem,
        device_id=peer, device_id_type=_MESH)


def _gather_start(x_shard, after, name):
    npeer = N_DEV - 1

    def body(x_ref, land_ref, after_ref, x_thru, land_thru, *rest):
        sems, token = rest[:2 * npeer], rest[-1]
        x, y, c, _ = _place()
        for k in range(1, N_DEV):
            _peer_copy(x_ref, land_ref, sems[k - 1], sems[npeer + k - 1], k, (x, y, c), 4 * x + 2 * y + c).start()
        token[...] = jnp.zeros_like(token)

    land = pltpu.with_memory_space_constraint(lax.empty((N_DEV,) + tuple(x_shard.shape), x_shard.dtype), pltpu.HBM)
    res = pl.pallas_call(
        body, name=name,
        out_shape=(pltpu.HBM(x_shard.shape, x_shard.dtype), pltpu.HBM(land.shape, land.dtype),
                   *([pltpu.SemaphoreType.DMA(())] * (2 * npeer)), jax.ShapeDtypeStruct((8, LANES), F32)),
        in_specs=[_HBM, _HBM, _ANY],
        out_specs=(_HBM, _HBM, *([_SEM] * (2 * npeer)), pl.BlockSpec(memory_space=pltpu.VMEM)),
        input_output_aliases={0: 0, 1: 1},
        compiler_params=pltpu.CompilerParams(has_side_effects=_DATAFLOW),
    )(pltpu.with_memory_space_constraint(x_shard, pltpu.HBM), land, after)
    return list(res[2:2 + 2 * npeer]), res[0], res[1], res[-1]


def _gather_wait(sems, src, land, after, name):
    npeer = N_DEV - 1

    def body(x_ref, land_ref, *rest):
        sem_refs = rest[:2 * npeer]
        x, y, c, _ = _place()
        for k in range(1, N_DEV):
            peer_index = (4 * x + 2 * y + c) ^ k
            cp = _peer_copy(x_ref, land_ref, sem_refs[k - 1], sem_refs[npeer + k - 1], k, (x, y, c), peer_index)
            cp.wait_send()
            cp.wait_recv()

    res = pl.pallas_call(
        body, name=name, out_shape=(pltpu.HBM(src.shape, src.dtype), pltpu.HBM(land.shape, land.dtype)),
        in_specs=[_HBM, _HBM] + [_SEM] * (2 * npeer) + [_ANY], out_specs=(_HBM, _HBM),
        input_output_aliases={0: 0, 1: 1},
        compiler_params=pltpu.CompilerParams(has_side_effects=_DATAFLOW),
    )(src, land, *sems, after)
    return res[1]


def _prenorm_inproj(x2, w, wt_qkv, b_qkv, wt_f, b_f):
    t = x2.shape[0]
    tm = min(512, t)
    n = wt_qkv.shape[0]
    tn = D_MODEL

    def body(x_ref, w_ref, wq_ref, bq_ref, wf_ref, bf_ref, h_ref, qkv_ref, zf_ref):
        x = x_ref[...]
        r = lax.rsqrt(jnp.mean(x * x, axis=-1, keepdims=True) + NORM_EPS)
        h = (x * r * w_ref[...]).astype(BF16)
        h_ref[...] = h
        for j in range(n // tn):
            cols = slice(j * tn, (j + 1) * tn)
            qkv_ref[:, cols] = (_dot_nt(h, wq_ref[cols, :]) + bq_ref[:, cols]).astype(BF16)
        zf_ref[...] = _dot_nt(h, wf_ref[...]) + bf_ref[...]

    row = lambda c: pl.BlockSpec((tm, c), lambda i: (i, 0))
    whole = lambda a: pl.BlockSpec(a.shape, lambda i: (0, 0))
    return pl.pallas_call(
        body, name="prenorm_inproj_qkv", grid=(t // tm,),
        in_specs=[row(D_MODEL), whole(w), whole(wt_qkv), whole(b_qkv), whole(wt_f), whole(b_f)],
        out_specs=[row(D_MODEL), row(n), row(LANES)],
        out_shape=[jax.ShapeDtypeStruct((t, D_MODEL), BF16), jax.ShapeDtypeStruct((t, n), BF16),
                   jax.ShapeDtypeStruct((t, LANES), F32)],
        compiler_params=_cparams(("parallel",), vmem_mb=48),
    )(x2, w, wt_qkv, b_qkv, wt_f, b_f)


def _mm_bias(a, bt, bias, out_dtype, name):
    m, k = a.shape
    n = bt.shape[0]
    tm = min(512, m)
    tn = min(1024, n)

    def body(a_ref, bt_ref, bias_ref, o_ref):
        aa = a_ref[...]
        for j in range(n // tn):
            cols = slice(j * tn, (j + 1) * tn)
            o_ref[:, cols] = (_dot_nt(aa, bt_ref[cols, :]) + bias_ref[:, cols]).astype(o_ref.dtype)

    return pl.pallas_call(
        body, name=name, grid=(m // tm,),
        in_specs=[pl.BlockSpec((tm, k), lambda i: (i, 0)), pl.BlockSpec((n, k), lambda i: (0, 0)),
                  pl.BlockSpec((1, n), lambda i: (0, 0))],
        out_specs=pl.BlockSpec((tm, n), lambda i: (i, 0)),
        out_shape=jax.ShapeDtypeStruct((m, n), out_dtype),
        compiler_params=_cparams(("parallel",), vmem_mb=48),
    )(a, bt, bias)


def _mm_tn(a, b, name, after=None):
    t, m = a.shape
    n = b.shape[1]
    tm = min(1024, m)
    tk = min(2048, t)
    deps = [] if after is None else [after]

    def body(a_ref, b_ref, *refs):
        o_ref, s_ref = refs[len(deps):]
        kk = pl.program_id(1)

        @pl.when(kk == 0)
        def _():
            o_ref[...] = jnp.zeros_like(o_ref)
            s_ref[...] = jnp.zeros_like(s_ref)

        aa = a_ref[...]
        o_ref[...] += _dot_tn(aa, b_ref[...])
        s_ref[0:1, :] += jnp.sum(aa.astype(F32), axis=0, keepdims=True)

    return pl.pallas_call(
        body, name=name, grid=(m // tm, t // tk),
        in_specs=[pl.BlockSpec((tk, tm), lambda i, kk: (kk, i)), pl.BlockSpec((tk, n), lambda i, kk: (kk, 0))]
        + [pl.BlockSpec(d.shape, lambda i, kk: (0, 0)) for d in deps],
        out_specs=[pl.BlockSpec((tm, n), lambda i, kk: (i, 0)), pl.BlockSpec((8, tm), lambda i, kk: (0, i))],
        out_shape=[jax.ShapeDtypeStruct((m, n), F32), jax.ShapeDtypeStruct((8, m), F32)],
        compiler_params=_cparams(("parallel", "arbitrary"), vmem_mb=48),
    )(a, b, *deps)


def _fgate_fwd(zf3):
    b, s, _ = zf3.shape
    tb = SCAN_TILE
    nb = s // tb

    def body(z_ref, cexp_ref, crow_ref):
        tri = (_iota((tb, tb), 1) <= _iota((tb, tb), 0)).astype(BF16)
        expand = ((_iota((LANES, D_MODEL), 1) >> 6) == _iota((LANES, D_MODEL), 0)).astype(BF16)
        carry = jnp.zeros((1, LANES), F32)
        for i in range(nb):
            rows = slice(i * tb, (i + 1) * tb)
            z = z_ref[rows, :]
            lf = jnp.minimum(z, 0.0) - jnp.log1p(jnp.exp(-jnp.abs(z)))
            cb = sum(_dot(tri, part) for part in _split3(lf)) + carry
            carry = cb[tb - 1:tb, :]
            cexp_ref[rows, :] = sum(_dot(part, expand) for part in _split3(cb))
            crow_ref[:, rows] = cb.T[0:HEADS, :]

    return pl.pallas_call(
        body, name="fgate_fwd", grid=(b,),
        in_specs=[pl.BlockSpec((None, s, LANES), lambda i: (i, 0, 0))],
        out_specs=[pl.BlockSpec((None, s, D_MODEL), lambda i: (i, 0, 0)),
                   pl.BlockSpec((None, HEADS, s), lambda i: (i, 0, 0))],
        out_shape=[jax.ShapeDtypeStruct((b, s, D_MODEL), F32), jax.ShapeDtypeStruct((b, HEADS, s), F32)],
        compiler_params=_cparams(("parallel",)),
    )(zf3)


def _fgate_bwd(dc3, zf3):
    b, s, _ = zf3.shape
    tb = SCAN_TILE
    nb = s // tb

    def body(dc_ref, z_ref, o_ref):
        tri = (_iota((tb, tb), 1) >= _iota((tb, tb), 0)).astype(BF16)
        carry = jnp.zeros((1, LANES), F32)
        for i in reversed(range(nb)):
            rows = slice(i * tb, (i + 1) * tb)
            dlf = sum(_dot(tri, part) for part in _split3(dc_ref[rows, :])) + carry
            carry = dlf[0:1, :]
            o_ref[rows, :] = (dlf * _sigmoid(-z_ref[rows, :])).astype(BF16)

    return pl.pallas_call(
        body, name="fgate_bwd", grid=(b,),
        in_specs=[pl.BlockSpec((None, s, LANES), lambda i: (i, 0, 0)),
                  pl.BlockSpec((None, s, LANES), lambda i: (i, 0, 0))],
        out_specs=pl.BlockSpec((s, LANES), lambda i: (i, 0)),
        out_shape=jax.ShapeDtypeStruct((b * s, LANES), BF16),
        compiler_params=_cparams(("parallel",)),
    )(dc3, zf3)


def _spare(hh):
    return HEAD_DIM if hh == 0 else 0


def _put_cols(tile, mine, cols, first):
    lane = _iota((1, LANES), 1)
    out = jnp.where(mine, tile, jnp.zeros((), tile.dtype))
    for j, c in enumerate(cols):
        out = jnp.where(lane == first + j, c, out)
    return out


def _put_rows(tile, mine, rows, first):
    sub = _iota((LANES, 1), 0)
    out = jnp.where(mine, tile, jnp.zeros((), tile.dtype))
    for j, r in enumerate(rows):
        out = jnp.where(sub == first + j, r, out)
    return out


def _transpose_bf16(a):
    return a.astype(F32).T.astype(BF16)


def _attn_fwd(qkv3, cexp3, crow, zrest3):
    b, s, _ = qkv3.shape
    ta = ATT_TILE_FWD
    nq = s // ta
    hd = HEAD_DIM
    crow5 = crow.reshape(b, HEAD_PAIRS, 2, nq, ta)

    def body(qkv_ref, cq_ref, ck_ref, g_ref, y_ref, lse_ref, ga_ref, kt_scr, v_scr):
        lane = _iota((1, LANES), 1)
        sub = _iota((LANES, 1), 0)
        lane_mine = (lane < hd, lane >= hd)
        sub_mine = (sub < hd, sub >= hd)
        causal = _iota((ta, ta), 0) >= _iota((ta, ta), 1)
        one = jnp.ones((), BF16)

        for kj in range(nq):
            rows = slice(kj * ta, (kj + 1) * ta)
            kt = _transpose_bf16(qkv_ref[rows, LANES:2 * LANES])
            v = qkv_ref[rows, 2 * LANES:3 * LANES]
            for hh in range(2):
                ck = list(_split3(-ck_ref[hh, kj:kj + 1, :]))
                kt_scr[hh, kj] = _put_rows(kt, sub_mine[hh], [one, one, one] + ck, _spare(hh))
                v_scr[hh, kj] = _put_cols(v, lane_mine[hh], [one], _spare(hh))

        for qi in range(nq):
            rows = slice(qi * ta, (qi + 1) * ta)
            q = qkv_ref[rows, 0:LANES] * 0.125
            cq = cq_ref[rows, :]
            qh = [_put_cols(q, lane_mine[hh], list(_split3(cq[:, hh * hd:hh * hd + 1])) + [one, one, one], _spare(hh))
                  for hh in range(2)]
            st = [(jnp.full((ta, 1), MASK_VALUE, F32), jnp.zeros((ta, LANES), F32))] * 2
            for kj in range(qi + 1):
                for hh in range(2):
                    m, acc = st[hh]
                    sc = _dot(qh[hh], kt_scr[hh, kj])
                    if kj == qi:
                        sc = jnp.where(causal, sc, MASK_VALUE)
                    mn = jnp.maximum(m, jnp.max(sc, axis=-1, keepdims=True))
                    p = jnp.exp(sc - mn).astype(BF16)
                    st[hh] = (mn, jnp.exp(m - mn) * acc + _dot(p, v_scr[hh, kj]))
            (ma, acca), (mb, accb) = st
            la = acca[:, hd:hd + 1]
            lb = accb[:, 0:1]
            y = jnp.where(lane_mine[0], acca * (1.0 / la), accb * (1.0 / lb))
            lse = jnp.where(lane_mine[0], ma + jnp.log(la), mb + jnp.log(lb)).T
            lse_ref[0, qi:qi + 1, :] = lse[0:1, :]
            lse_ref[1, qi:qi + 1, :] = lse[hd:hd + 1, :]
            y_ref[rows, :] = y
            g = g_ref[rows, :].astype(F32)
            ga_ref[rows, :] = (y * (g * _sigmoid(g))).astype(BF16)

    blk = lambda w: pl.BlockSpec((None, s, w), lambda i, p: (i, 0, p))
    rows5 = pl.BlockSpec((None, None, 2, nq, ta), lambda i, p: (i, p, 0, 0, 0))
    yatt3, lse5, ga = pl.pallas_call(
        body, name="attn_fwd", grid=(b, HEAD_PAIRS),
        in_specs=[blk(3 * LANES), blk(LANES), rows5, blk(LANES)],
        out_specs=[blk(LANES), rows5, pl.BlockSpec((s, LANES), lambda i, p: (i, p))],
        out_shape=[jax.ShapeDtypeStruct((b, s, D_MODEL), F32),
                   jax.ShapeDtypeStruct((b, HEAD_PAIRS, 2, nq, ta), F32),
                   jax.ShapeDtypeStruct((b * s, D_MODEL), BF16)],
        scratch_shapes=[pltpu.VMEM((2, nq, LANES, ta), BF16), pltpu.VMEM((2, nq, ta, LANES), BF16)],
        compiler_params=_cparams(("parallel", "parallel")),
    )(qkv3, cexp3, crow5, zrest3)
    return yatt3, lse5.reshape(b, HEADS, s), ga


def _attn_bwd(qkv3, do3, y3, lse, crow, cexp3):
    b, s, _ = qkv3.shape
    ta = ATT_TILE_BWD
    nq = s // ta
    hd = HEAD_DIM
    lse5 = lse.reshape(b, HEAD_PAIRS, 2, nq, ta)
    crow5 = crow.reshape(b, HEAD_PAIRS, 2, nq, ta)

    def body(qkv_ref, do_ref, y_ref, lse_ref, crow_ref, cexp_ref, dqkv_ref, dc_ref,
             qa_scr, doa_scr, qst_scr, dot_scr, kt_scr, vt_scr, dq_scr, rs_scr):
        pair = pl.program_id(1)
        lane = _iota((1, LANES), 1)
        sub = _iota((LANES, 1), 0)
        lane_mine = (lane < hd, lane >= hd)
        sub_mine = (sub < hd, sub >= hd)
        causal = _iota((ta, ta), 0) >= _iota((ta, ta), 1)
        one = jnp.ones((), BF16)
        zero = jnp.zeros((), BF16)

        @pl.when(pair == 0)
        def _():
            dc_ref[...] = jnp.zeros_like(dc_ref)

        for i in range(nq):
            rows = slice(i * ta, (i + 1) * ta)
            qs = qkv_ref[rows, 0:LANES] * 0.125
            qst = _transpose_bf16(qs)
            kt = _transpose_bf16(qkv_ref[rows, LANES:2 * LANES])
            vt = _transpose_bf16(qkv_ref[rows, 2 * LANES:3 * LANES])
            do = do_ref[rows, :]
            dof = do.astype(F32)
            dot = dof.T.astype(BF16)
            pr = y_ref[rows, :] * dof
            cq = cexp_ref[rows, :]
            lse_c = jnp.where(sub == 0, lse_ref[0, i:i + 1, :],
                              jnp.where(sub == 1, lse_ref[1, i:i + 1, :], 0.0)).T
            for hh in range(2):
                sp = _spare(hh)
                dsum = jnp.sum(jnp.where(lane_mine[hh], pr, 0.0), axis=-1, keepdims=True)
                bias = cq[:, hh * hd:hh * hd + 1] - lse_c[:, hh:hh + 1]
                qa_scr[hh, i] = _put_cols(qs, lane_mine[hh], list(_split3(bias)) + [one, one, one], sp)
                doa_scr[hh, i] = _put_cols(do, lane_mine[hh], list(_split3(-dsum)), sp)
                qst_scr[hh, i] = jnp.where(sub_mine[hh], qst, zero)
                dot_scr[hh, i] = jnp.where(sub_mine[hh], dot, zero)
                ck = list(_split3(-crow_ref[hh, i:i + 1, :]))
                kt_scr[hh, i] = _put_rows(kt, sub_mine[hh], [one, one, one] + ck, sp)
                vt_scr[hh, i] = _put_rows(vt, sub_mine[hh], [one, one, one], sp)
            dq_scr[i] = jnp.zeros((ta, LANES), F32)
            rs_scr[i] = jnp.zeros((ta, LANES), F32)

        for kj in range(nq):
            krows = slice(kj * ta, (kj + 1) * ta)
            k = qkv_ref[krows, LANES:2 * LANES]
            km = (jnp.where(lane_mine[0], k, zero), jnp.where(lane_mine[1], k, zero))
            dkt = jnp.zeros((LANES, ta), F32)
            dvt = jnp.zeros((LANES, ta), F32)
            dcp = [jnp.zeros((8, ta), F32), jnp.zeros((8, ta), F32)]
            for qi in range(kj, nq):
                dq = jnp.zeros((ta, LANES), F32)
                rs = []
                for hh in range(2):
                    sc = _dot(qa_scr[hh, qi], kt_scr[hh, kj])
                    if qi == kj:
                        sc = jnp.where(causal, sc, MASK_VALUE)
                    p = jnp.exp(sc)
                    dsf = p * _dot(doa_scr[hh, qi], vt_scr[hh, kj])
                    dcp[hh] = dcp[hh] + jnp.sum(dsf.reshape(ta // 8, 8, ta), axis=0)
                    rs.append(jnp.sum(dsf, axis=-1, keepdims=True))
                    ds = dsf.astype(BF16)
                    dq = dq + _dot(ds, km[hh])
                    dkt = dkt + _dot(qst_scr[hh, qi], ds)
                    dvt = dvt + _dot(dot_scr[hh, qi], p.astype(BF16))
                dq_scr[qi] += dq
                rs_scr[qi] += jnp.where(lane == 0, rs[0], jnp.where(lane == 1, rs[1], 0.0))
            dqkv_ref[krows, LANES:2 * LANES] = dkt.T.astype(BF16)
            dqkv_ref[krows, 2 * LANES:3 * LANES] = dvt.T.astype(BF16)
            dca = jnp.sum(dcp[0], axis=0, keepdims=True)
            dcb = jnp.sum(dcp[1], axis=0, keepdims=True)
            dcs = jnp.where(sub == 0, dca, jnp.where(sub == 1, dcb, 0.0)).T
            dc_ref[krows, :] += (jnp.where(lane == 2 * pair, -dcs[:, 0:1], 0.0)
                                 + jnp.where(lane == 2 * pair + 1, -dcs[:, 1:2], 0.0))
        for qi in range(nq):
            rows = slice(qi * ta, (qi + 1) * ta)
            dqkv_ref[rows, 0:LANES] = (dq_scr[qi] * 0.125).astype(BF16)
            rq = rs_scr[qi]
            dc_ref[rows, :] += (jnp.where(lane == 2 * pair, rq[:, 0:1], 0.0)
                                + jnp.where(lane == 2 * pair + 1, rq[:, 1:2], 0.0))

    blk = lambda w: pl.BlockSpec((None, s, w), lambda i, p: (i, 0, p))
    rows5 = pl.BlockSpec((None, None, 2, nq, ta), lambda i, p: (i, p, 0, 0, 0))
    by_rows = lambda: pltpu.VMEM((2, nq, ta, LANES), BF16)
    by_cols = lambda: pltpu.VMEM((2, nq, LANES, ta), BF16)
    return pl.pallas_call(
        body, name="attn_bwd", grid=(b, HEAD_PAIRS),
        in_specs=[blk(3 * LANES), blk(LANES), blk(LANES), rows5, rows5, blk(LANES)],
        out_specs=[pl.BlockSpec((s, 3 * LANES), lambda i, p: (i, p)),
                   pl.BlockSpec((None, s, LANES), lambda i, p: (i, 0, 0))],
        out_shape=[jax.ShapeDtypeStruct((b * s, 3 * D_MODEL), BF16), jax.ShapeDtypeStruct((b, s, LANES), F32)],
        scratch_shapes=[by_rows(), by_rows(), by_cols(), by_cols(), by_cols(), by_cols(),
                        pltpu.VMEM((nq, ta, LANES), F32), pltpu.VMEM((nq, ta, LANES), F32)],
        compiler_params=_cparams(("parallel", "arbitrary")),
    )(qkv3, do3, y3, lse5, crow5, cexp3)


def _shifted(v, ks, rows, s):
    low = rows[0:8, :]
    out = []
    for k in ks:
        r = pltpu.roll(v, k % s, 0)
        if k > 0:
            out.append(jnp.concatenate([jnp.where(low >= k, r[0:8, :], 0.0), r[8:, :]], axis=0))
        else:
            out.append(jnp.concatenate([r[:s - 8, :], jnp.where(low < 8 + k, r[s - 8:, :], 0.0)], axis=0))
    return out


def _rnn_common(xr, cw_ref, cb_ref, bda_ref, bdx_ref, ba_ref, bx_ref, lam_ref, s):
    rows = _iota((s, LANES), 0)
    x1, x2, x3 = _shifted(xr, (1, 2, 3), rows, s)
    xc = cb_ref[...] + cw_ref[0:1, :] * x3
    xc = xc + cw_ref[1:2, :] * x2
    xc = xc + cw_ref[2:3, :] * x1
    xc = xc + cw_ref[3:4, :] * xr
    xcb = xc.astype(BF16)
    r = _sigmoid(_dot(xcb, bda_ref[...]) + ba_ref[...])
    i = _sigmoid(_dot(xcb, bdx_ref[...]) + bx_ref[...])
    sp = _softplus(-lam_ref[...])
    log_a = (-RG_C * r) * sp
    a = jnp.exp(log_a)
    a2 = a * a
    sq = jnp.sqrt(jnp.maximum(_one_minus_exp(log_a + log_a, a2), 0.0))
    return rows, (x1, x2, x3), xc, xcb, r, i, sp, a, a2, sq


def _scan_down(a, u, rows, s, s1, s2):
    low = rows & 7
    for sh in (1, 2, 4):
        keep = low >= sh
        u = u + a * jnp.where(keep, pltpu.roll(u, sh, 0), 0.0)
        a = a * jnp.where(keep, pltpu.roll(a, sh, 0), 1.0)
    ng = s // 8
    s1[...] = a
    s2[...] = u
    at = s1[pl.ds(7, ng, stride=8), :]
    ut = s2[pl.ds(7, ng, stride=8), :]
    grow = _iota((ng, LANES), 0)
    sh = 1
    while sh < ng:
        keep = grow >= sh
        ut = ut + at * jnp.where(keep, pltpu.roll(ut, sh, 0), 0.0)
        if sh * 2 < ng:
            at = at * jnp.where(keep, pltpu.roll(at, sh, 0), 1.0)
        sh *= 2
    h_in = jnp.where(grow >= 1, pltpu.roll(ut, 1, 0), 0.0)
    for k in range(8):
        s1[pl.ds(k, ng, stride=8), :] = h_in
    return u + a * s1[...]


def _scan_up(a, g, rows, s, s1, s2):
    low = rows & 7
    for sh in (1, 2, 4):
        keep = low < 8 - sh
        g = g + a * jnp.where(keep, pltpu.roll(g, s - sh, 0), 0.0)
        a = a * jnp.where(keep, pltpu.roll(a, s - sh, 0), 1.0)
    ng = s // 8
    s1[...] = a
    s2[...] = g
    at = s1[pl.ds(0, ng, stride=8), :]
    gt = s2[pl.ds(0, ng, stride=8), :]
    grow = _iota((ng, LANES), 0)
    sh = 1
    while sh < ng:
        keep = grow < ng - sh
        gt = gt + at * jnp.where(keep, pltpu.roll(gt, ng - sh, 0), 0.0)
        if sh * 2 < ng:
            at = at * jnp.where(keep, pltpu.roll(at, ng - sh, 0), 1.0)
        sh *= 2
    g_in = jnp.where(grow < ng - 1, pltpu.roll(gt, ng - 1, 0), 0.0)
    for k in range(8):
        s1[pl.ds(k, ng, stride=8), :] = g_in
    return g + a * s1[...]


def _rnn_specs(s):
    blk = lambda off: pl.BlockSpec((None, s, LANES), lambda cb, i: (i, 0, off + cb))
    vec = lambda r: pl.BlockSpec((r, LANES), lambda cb, i: (0, cb))
    mat = pl.BlockSpec((None, LANES, LANES), lambda cb, i: (cb, 0, 0))
    return blk, vec, mat


def _rnn_fwd(zrest3, conv_w, conv_b, bda, bdx, ba, bx, lam):
    b, s, _ = zrest3.shape

    def body(xr_ref, g_ref, cw_ref, cb_ref, bda_ref, bdx_ref, ba_ref, bx_ref, lam_ref, h_ref, gr_ref, s1, s2):
        xr = xr_ref[...].astype(F32)
        rows, _, xc, _, _, i, _, a, _, sq = _rnn_common(
            xr, cw_ref, cb_ref, bda_ref, bdx_ref, ba_ref, bx_ref, lam_ref, s)
        h = _scan_down(a, sq * (i * xc), rows, s, s1, s2)
        h_ref[...] = h
        g = g_ref[...].astype(F32)
        gr_ref[...] = (h * (g * _sigmoid(g))).astype(BF16)

    blk, vec, mat = _rnn_specs(s)
    return pl.pallas_call(
        body, name="rnn_fwd", grid=(N_CBLK, b),
        in_specs=[blk(N_CBLK), blk(2 * N_CBLK), vec(CONV_W), vec(1), mat, mat, vec(1), vec(1), vec(1)],
        out_specs=[blk(0), pl.BlockSpec((s, LANES), lambda cb, i: (i, cb))],
        out_shape=[jax.ShapeDtypeStruct((b, s, D_MODEL), F32), jax.ShapeDtypeStruct((b * s, D_MODEL), BF16)],
        scratch_shapes=[pltpu.VMEM((s, LANES), F32), pltpu.VMEM((s, LANES), F32)],
        compiler_params=_cparams(("parallel", "parallel")),
    )(zrest3, zrest3, conv_w, conv_b, bda, bdx, ba, bx, lam)


def _rnn_bwd(zrest3, h3, dh3, conv_w, conv_b, bda, bdx, ba, bx, lam):
    b, s, _ = zrest3.shape

    def body(xr_ref, h_ref, dh_ref, cw_ref, cb_ref, bda_ref, bdx_ref, ba_ref, bx_ref, lam_ref,
             dxr_ref, pv_ref, dbd_ref, s1, s2):
        @pl.when(pl.program_id(1) == 0)
        def _():
            pv_ref[...] = jnp.zeros_like(pv_ref)
            dbd_ref[...] = jnp.zeros_like(dbd_ref)

        xr = xr_ref[...].astype(F32)
        rows, (x1, x2, x3), xc, xcb, r, i, sp, a, a2, sq = _rnn_common(
            xr, cw_ref, cb_ref, bda_ref, bdx_ref, ba_ref, bx_ref, lam_ref, s)
        (a_next,) = _shifted(a, (-1,), rows, s)
        g = _scan_up(a_next, dh_ref[...], rows, s, s1, s2)
        (hp,) = _shifted(h_ref[...], (1,), rows, s)
        da = g * hp
        dsq = g * (i * xc)
        di = g * (sq * xc)
        dxc = g * (sq * i)
        dlog = da * a - dsq * (a2 / sq)
        dr = dlog * (-RG_C * sp)
        dpr = dr * (r * (1.0 - r))
        dpi = di * (i * (1.0 - i))
        dprb = dpr.astype(BF16)
        dpib = dpi.astype(BF16)
        dxc = dxc + _dot_nt(dprb, bda_ref[...]) + _dot_nt(dpib, bdx_ref[...])

        up1, up2, up3 = _shifted(dxc, (-1, -2, -3), rows, s)
        dxr = cw_ref[3:4, :] * dxc + cw_ref[2:3, :] * up1 + cw_ref[1:2, :] * up2 + cw_ref[0:1, :] * up3
        dxr_ref[...] = dxr.astype(BF16)

        def colsum(v):
            return jnp.sum(v, axis=0, keepdims=True)

        pv_ref[0:1, :] += colsum(dxc * x3)
        pv_ref[1:2, :] += colsum(dxc * x2)
        pv_ref[2:3, :] += colsum(dxc * x1)
        pv_ref[3:4, :] += colsum(dxc * xr)
        pv_ref[4:5, :] += colsum(dxc)
        pv_ref[5:6, :] += colsum(dpr)
        pv_ref[6:7, :] += colsum(dpi)
        pv_ref[7:8, :] += colsum(dlog * r) * (RG_C * _sigmoid(-lam_ref[...]))
        dbd_ref[0] += _dot_tn(xcb, dprb)
        dbd_ref[1] += _dot_tn(xcb, dpib)

    blk, vec, mat = _rnn_specs(s)
    hblk = pl.BlockSpec((None, s, LANES), lambda cb, i: (i, 0, cb))
    return pl.pallas_call(
        body, name="rnn_bwd", grid=(N_CBLK, b),
        in_specs=[blk(N_CBLK), hblk, hblk, vec(CONV_W), vec(1), mat, mat, vec(1), vec(1), vec(1)],
        out_specs=[pl.BlockSpec((s, LANES), lambda cb, i: (i, cb)), pl.BlockSpec((8, LANES), lambda cb, i: (0, cb)),
                   pl.BlockSpec((None, 2, LANES, LANES), lambda cb, i: (cb, 0, 0, 0))],
        out_shape=[jax.ShapeDtypeStruct((b * s, D_MODEL), BF16), jax.ShapeDtypeStruct((8, D_MODEL), F32),
                   jax.ShapeDtypeStruct((N_CBLK, 2, LANES, LANES), F32)],
        scratch_shapes=[pltpu.VMEM((s, LANES), F32), pltpu.VMEM((s, LANES), F32)],
        compiler_params=_cparams(("parallel", "arbitrary")),
    )(zrest3, h3, dh3, conv_w, conv_b, bda, bdx, ba, bx, lam)


def _branch_merge(ga, gr, wa, wr, zrest):
    t = ga.shape[0]
    tm = min(512, t)
    tn = D_MODEL

    def body(ga_ref, gr_ref, wa_ref, wr_ref, mga_ref, mgr_ref, ya_ref, yr_ref, m_ref):
        ya = _dot(ga_ref[...], wa_ref[...])
        yr = _dot(gr_ref[...], wr_ref[...])
        ya_ref[...] = ya.astype(BF16)
        yr_ref[...] = yr.astype(BF16)
        m_ref[...] = (_sigmoid(mga_ref[...].astype(F32)) * ya + _sigmoid(mgr_ref[...].astype(F32)) * yr).astype(BF16)

    nj = D_MODEL // tn
    act = pl.BlockSpec((tm, D_MODEL), lambda i, j: (i, 0))
    wgt = pl.BlockSpec((D_MODEL, tn), lambda i, j: (0, j))
    out = pl.BlockSpec((tm, tn), lambda i, j: (i, j))
    return pl.pallas_call(
        body, name="branch_merge", grid=(t // tm, nj),
        in_specs=[act, act, wgt, wgt, pl.BlockSpec((tm, tn), lambda i, j: (i, 3 * nj + j)),
                  pl.BlockSpec((tm, tn), lambda i, j: (i, 4 * nj + j))],
        out_specs=[out, out, out],
        out_shape=[jax.ShapeDtypeStruct((t, D_MODEL), BF16), jax.ShapeDtypeStruct((t, D_MODEL), BF16),
                   jax.ShapeDtypeStruct((t, D_MODEL), BF16)],
        compiler_params=_cparams(("parallel", "parallel")),
    )(ga, gr, wa, wr, zrest, zrest)


def _out_loss(m, wout, x2, tgt2, wpost):
    t = m.shape[0]
    tm = min(512, t)

    def body(m_ref, w_ref, x_ref, t_ref, wp_ref, dy_ref, do_ref, acc_ref):
        @pl.when(pl.program_id(0) == 0)
        def _():
            acc_ref[...] = jnp.zeros_like(acc_ref)

        o = _dot(m_ref[...], w_ref[...])
        r2 = lax.rsqrt(jnp.mean(o * o, axis=-1, keepdims=True) + NORM_EPS)
        n = o * r2
        wp = wp_ref[...]
        err = (x_ref[...] + n * wp) - t_ref[...]
        dy = err * (1.0 / D_MODEL)
        dn = dy * wp
        do = r2 * (dn - n * jnp.mean(dn * n, axis=-1, keepdims=True))
        dy_ref[...] = dy
        do_ref[...] = do.astype(BF16)
        acc_ref[0:1, :] += jnp.sum(dy * n, axis=0, keepdims=True)
        acc_ref[1:2, :] += jnp.sum(err * err, axis=0, keepdims=True)

    row = pl.BlockSpec((tm, D_MODEL), lambda i: (i, 0))
    return pl.pallas_call(
        body, name="out_loss", grid=(t // tm,),
        in_specs=[row, pl.BlockSpec((D_MODEL, D_MODEL), lambda i: (0, 0)), row, row,
                  pl.BlockSpec((1, D_MODEL), lambda i: (0, 0))],
        out_specs=[row, row, pl.BlockSpec((8, D_MODEL), lambda i: (0, 0))],
        out_shape=[jax.ShapeDtypeStruct((t, D_MODEL), F32), jax.ShapeDtypeStruct((t, D_MODEL), BF16),
                   jax.ShapeDtypeStruct((8, D_MODEL), F32)],
        compiler_params=_cparams(("arbitrary",)),
    )(m, wout, x2, tgt2, wpost)


def _merge_bwd(do, wout, zrest, ya, yr):
    t = do.shape[0]
    tm = min(512, t)
    tn = D_MODEL
    nj = D_MODEL // tn

    def body(do_ref, w_ref, mga_ref, mgr_ref, ya_ref, yr_ref, dya_ref, dyr_ref, dmga_ref, dmgr_ref):
        dm = _dot_nt(do_ref[...], w_ref[...])
        sa = _sigmoid(mga_ref[...].astype(F32))
        sr = _sigmoid(mgr_ref[...].astype(F32))
        dya_ref[...] = (dm * sa).astype(BF16)
        dyr_ref[...] = (dm * sr).astype(BF16)
        dmga_ref[...] = (dm * ya_ref[...].astype(F32) * (sa * (1.0 - sa))).astype(BF16)
        dmgr_ref[...] = (dm * yr_ref[...].astype(F32) * (sr * (1.0 - sr))).astype(BF16)

    out = pl.BlockSpec((tm, tn), lambda i, j: (i, j))
    bf = jax.ShapeDtypeStruct((t, D_MODEL), BF16)
    return pl.pallas_call(
        body, name="merge_bwd", grid=(t // tm, nj),
        in_specs=[pl.BlockSpec((tm, D_MODEL), lambda i, j: (i, 0)), pl.BlockSpec((tn, D_MODEL), lambda i, j: (j, 0)),
                  pl.BlockSpec((tm, tn), lambda i, j: (i, 3 * nj + j)),
                  pl.BlockSpec((tm, tn), lambda i, j: (i, 4 * nj + j)), out, out],
        out_specs=[out, out, out, out],
        out_shape=[bf, bf, bf, bf],
        compiler_params=_cparams(("parallel", "parallel")),
    )(do, wout, zrest, zrest, ya, yr)


def _branch_bwd(dya, dyr, wa, wr, zrest, yatt, ylru):
    t = dya.shape[0]
    tm = min(512, t)
    tn = D_MODEL
    nj = D_MODEL // tn

    def body(dya_ref, dyr_ref, wa_ref, wr_ref, ga_ref, gr_ref, ya_ref, yl_ref,
             dyatt_ref, dga_ref, dyl_ref, dgr_ref):
        dga = _dot_nt(dya_ref[...], wa_ref[...])
        dgr = _dot_nt(dyr_ref[...], wr_ref[...])
        g = ga_ref[...].astype(F32)
        sg = _sigmoid(g)
        dyatt_ref[...] = (dga * (g * sg)).astype(BF16)
        dga_ref[...] = (dga * ya_ref[...] * (sg * (1.0 + g * (1.0 - sg)))).astype(BF16)
        g = gr_ref[...].astype(F32)
        sg = _sigmoid(g)
        dyl_ref[...] = dgr * (g * sg)
        dgr_ref[...] = (dgr * yl_ref[...] * (sg * (1.0 + g * (1.0 - sg)))).astype(BF16)

    act = pl.BlockSpec((tm, D_MODEL), lambda i, j: (i, 0))
    wgt = pl.BlockSpec((tn, D_MODEL), lambda i, j: (j, 0))
    out = pl.BlockSpec((tm, tn), lambda i, j: (i, j))
    bf = jax.ShapeDtypeStruct((t, D_MODEL), BF16)
    return pl.pallas_call(
        body, name="branch_bwd", grid=(t // tm, nj),
        in_specs=[act, act, wgt, wgt, pl.BlockSpec((tm, tn), lambda i, j: (i, j)),
                  pl.BlockSpec((tm, tn), lambda i, j: (i, 2 * nj + j)), out, out],
        out_specs=[out, out, out, out],
        out_shape=[bf, bf, jax.ShapeDtypeStruct((t, D_MODEL), F32), bf],
        compiler_params=_cparams(("parallel", "parallel")),
    )(dya, dyr, wa, wr, zrest, zrest, yatt, ylru)


def _dh_final(parts, after, x2, dy, wpre):
    t = x2.shape[0]
    tm = min(256, t)
    np_ = len(parts)

    def body(*refs):
        x_ref, dy_ref, w_ref = refs[2 * np_ + 1:2 * np_ + 4]
        gx_ref, pw_ref = refs[2 * np_ + 4:]

        @pl.when(pl.program_id(0) == 0)
        def _():
            pw_ref[...] = jnp.zeros_like(pw_ref)

        dh = _dot(refs[0][...], refs[np_][...])
        for p in range(1, np_):
            dh = dh + _dot(refs[p][...], refs[np_ + p][...])
        x = x_ref[...]
        r = lax.rsqrt(jnp.mean(x * x, axis=-1, keepdims=True) + NORM_EPS)
        xn = x * r
        dxn = dh * w_ref[...]
        gx_ref[...] = r * (dxn - xn * jnp.mean(dxn * xn, axis=-1, keepdims=True)) + dy_ref[...]
        pw_ref[0:1, :] += jnp.sum(dh * xn, axis=0, keepdims=True)

    row = pl.BlockSpec((tm, D_MODEL), lambda i: (i, 0))
    in_specs = [pl.BlockSpec((tm, dz.shape[1]), lambda i: (i, 0)) for dz, _ in parts]
    in_specs += [pl.BlockSpec(w.shape, lambda i: (0, 0), pipeline_mode=pl.Buffered(1)) for _, w in parts]
    in_specs += [pl.BlockSpec(after.shape, lambda i: (0, 0)), row, row, pl.BlockSpec((1, D_MODEL), lambda i: (0, 0))]
    return pl.pallas_call(
        body, name="dh_final", grid=(t // tm,),
        in_specs=in_specs,
        out_specs=[row, pl.BlockSpec((8, D_MODEL), lambda i: (0, 0))],
        out_shape=[jax.ShapeDtypeStruct((t, D_MODEL), F32), jax.ShapeDtypeStruct((8, D_MODEL), F32)],
        compiler_params=_cparams(("arbitrary",), vmem_mb=48),
    )(*[dz for dz, _ in parts], *[w for _, w in parts], after, x2, dy, wpre)


def _adamw(w, g, m, v):
    m = ADAM_B1 * m + (1.0 - ADAM_B1) * g
    v = ADAM_B2 * v + (1.0 - ADAM_B2) * (g * g)
    m_hat = m / (1.0 - ADAM_B1 ** ADAM_STEP)
    v_hat = v / (1.0 - ADAM_B2 ** ADAM_STEP)
    delta = -ADAM_LR * (m_hat / (jnp.sqrt(v_hat) + ADAM_EPS) + ADAM_WD * w)
    return delta, m, v


def _reduce_adamw(own, parts, place, w, m, v, name):
    r, c = w.shape
    blk, nblk, at = _blocks_2d(r, c)

    def body(place_ref, own_ref, p_ref, w_ref, m_ref, v_ref, g_ref, d_ref, nm_ref, nv_ref):
        mine = place_ref[1]
        own_blk = own_ref[...]
        g = jnp.where(mine == 0, own_blk, p_ref[0].astype(F32))
        for j in range(1, N_CHIPS):
            g = g + jnp.where(mine == j, own_blk, p_ref[j].astype(F32))
        d, nm, nv = _adamw(w_ref[...], g, m_ref[...], v_ref[...])
        g_ref[...] = g
        d_ref[...] = d
        nm_ref[...] = nm
        nv_ref[...] = nv

    row = pl.BlockSpec(blk, lambda i, pr: at(i))
    sh = jax.ShapeDtypeStruct((r, c), F32)
    grid_spec = pltpu.PrefetchScalarGridSpec(
        num_scalar_prefetch=1, grid=(nblk,),
        in_specs=[row, pl.BlockSpec((N_CHIPS,) + blk, lambda i, pr: (0,) + at(i)), row, row, row],
        out_specs=[row, row, row, row])
    return pl.pallas_call(
        body, name=name, grid_spec=grid_spec, out_shape=[sh, sh, sh, sh],
        compiler_params=_cparams(("parallel",)),
    )(place, own, parts, w, m, v)


def _reduce_adamw_stacked(own, parts, place, triples, name):
    n = len(triples)
    _, r, c = triples[0][0].shape

    def body(place_ref, own_ref, p_ref, *refs):
        ins, outs = refs[:3 * n], refs[3 * n:]
        mine = place_ref[1]
        for i in range(n):
            rows = slice(i * r, (i + 1) * r)
            own_blk = own_ref[rows, :]
            g = jnp.where(mine == 0, own_blk, p_ref[0, rows, :].astype(F32))
            for j in range(1, N_CHIPS):
                g = g + jnp.where(mine == j, own_blk, p_ref[j, rows, :].astype(F32))
            d, nm, nv = _adamw(ins[3 * i][0], g, ins[3 * i + 1][0], ins[3 * i + 2][0])
            for k, val in enumerate((g, d, nm, nv)):
                outs[4 * i + k][0] = val

    whole = lambda shape: pl.BlockSpec(shape, lambda i, pr: (0,) * len(shape))
    grid_spec = pltpu.PrefetchScalarGridSpec(
        num_scalar_prefetch=1, grid=(1,),
        in_specs=[whole(own.shape), whole(parts.shape)] + [whole((1, r, c))] * (3 * n),
        out_specs=[whole((1, r, c))] * (4 * n))
    res = pl.pallas_call(
        body, name=name, grid_spec=grid_spec,
        out_shape=[jax.ShapeDtypeStruct((1, r, c), F32)] * (4 * n),
        compiler_params=_cparams(("arbitrary",)),
    )(place, own, parts, *[a for t3 in triples for a in t3])
    return [res[4 * i:4 * i + 4] for i in range(n)]


def _interleave_qkv(a):
    lead = a.shape[:-1]
    return a.reshape(lead + (3, HEAD_PAIRS, LANES)).swapaxes(-3, -2).reshape(lead + (3 * D_MODEL,))


def _deinterleave_qkv(a):
    lead = a.shape[:-1]
    return a.reshape(lead + (HEAD_PAIRS, 3, LANES)).swapaxes(-3, -2).reshape(lead + (3 * D_MODEL,))


def _interleave_rows(a):
    return a.reshape(3, HEAD_PAIRS, LANES, a.shape[1]).swapaxes(0, 1).reshape(a.shape)


def _deinterleave_rows(a):
    return a.reshape(HEAD_PAIRS, 3, LANES, a.shape[1]).swapaxes(0, 1).reshape(a.shape)


def _pack_small(pre, conv_b, rg_ba, rg_bx, lam, post, loss_row, b_in, conv_w_full, rg_wa, rg_wx):
    z = jnp.zeros((1, D_MODEL), F32)
    b_used = jnp.concatenate([b_in[:, 0:3 * D_MODEL], b_in[:, 3 * D_MODEL + HEADS:IN_TOTAL]], axis=1)
    b_f = jnp.pad(b_in[:, 3 * D_MODEL:3 * D_MODEL + HEADS], ((0, 0), (0, D_MODEL - HEADS)))
    return jnp.concatenate([
        pre, conv_b, rg_ba, rg_bx, lam, post, loss_row, z,
        b_used.reshape(9, D_MODEL), b_f, conv_w_full, z, z,
        rg_wa.reshape(64, D_MODEL), rg_wx.reshape(64, D_MODEL)], axis=0)


def _unpack_small(p):
    b_used = p[8:17].reshape(1, 9 * D_MODEL)
    b_in = jnp.concatenate([b_used[:, 0:3 * D_MODEL], p[17:18, 0:HEADS], b_used[:, 3 * D_MODEL:]], axis=1)
    return dict(pre_norm_w=p[0:1], conv_b=p[1:2], rg_ba=p[2:3], rg_bx=p[3:4], rg_lambda=p[4:5],
                post_norm_w=p[5:6], loss_row=p[6:7], b_in=b_in, conv_w_full=p[18:22],
                rg_wa=p[24:88].reshape(1, 16, 64, 64), rg_wx=p[88:152].reshape(1, 16, 64, 64))


def _reduce_small(parts, first, w, m, v, vectors):
    nvec = len(vectors)

    def body(p_ref, f_ref, w_ref, m_ref, v_ref, *refs):
        ins, outs = refs[:3 * nvec], refs[3 * nvec:]
        g = p_ref[0]
        g0 = f_ref[0, 0:1, :]
        for j in range(1, N_DEV):
            g = g + p_ref[j]
            g0 = g0 + f_ref[j, 0:1, :]
        d, nm, nv = _adamw(w_ref[...], g, m_ref[...], v_ref[...])
        for k, val in enumerate((g, d, nm, nv)):
            outs[k][...] = val
        for i in range(nvec):
            gi = g0 if i == 0 else g[i:i + 1, :]
            di, nmi, nvi = _adamw(ins[3 * i][...], gi, ins[3 * i + 1][...], ins[3 * i + 2][...])
            for k, val in enumerate((gi, di, nmi, nvi)):
                outs[4 + 4 * i + k][...] = val
        outs[-1][...] = jnp.zeros((8, LANES), F32) + (0.5 / D_MODEL) * jnp.sum(g[LOSS_ROW:LOSS_ROW + 1, :])

    sh = jax.ShapeDtypeStruct((SMALL_ROWS, D_MODEL), F32)
    vec = jax.ShapeDtypeStruct((1, D_MODEL), F32)
    res = pl.pallas_call(
        body, name="reduce_small",
        out_shape=[sh, sh, sh, sh] + [vec] * (4 * nvec) + [jax.ShapeDtypeStruct((8, LANES), F32)],
    )(parts, first, w, m, v, *[a for t3 in vectors for a in t3])
    return res[:4], [res[4 + 4 * i:8 + 4 * i] for i in range(nvec)], res[-1]


def kernel(x, pre_norm_w, w_in, b_in, conv_w, conv_b, rg_wa, rg_ba, rg_wx, rg_bx, rg_lambda, w_branch_a, w_branch_r, w_out, post_norm_w, loss_target, m_pre_norm_w, m_w_in, m_b_in, m_conv_w, m_conv_b, m_rg_wa, m_rg_ba, m_rg_wx, m_rg_bx, m_rg_lambda, m_w_branch_a, m_w_branch_r, m_w_out, m_post_norm_w, v_pre_norm_w, v_w_in, v_b_in, v_conv_w, v_conv_b, v_rg_wa, v_rg_ba, v_rg_wx, v_rg_bx, v_rg_lambda, v_w_branch_a, v_w_branch_r, v_w_out, v_post_norm_w):
    b, s, _ = x.shape
    t = b * s
    me = 4 * lax.axis_index("x") + 2 * lax.axis_index("y") + lax.axis_index("c")
    shard_rows = D_MODEL // N_DEV

    place = jnp.stack([lax.axis_index("c"), 2 * lax.axis_index("x") + lax.axis_index("y")]).astype(jnp.int32)
    w_in_all = _gather(w_in[0].T.astype(BF16), "gather_w_in")
    wt_full = w_in_all.reshape(IN_TOTAL, D_MODEL)
    conv_terms = jnp.concatenate(_split3(conv_w[0]), axis=0)
    conv_pad = jnp.pad(conv_terms, ((0, 16 - 3 * CONV_W), (0, D_MODEL - LANES)))
    sq_stack = jnp.concatenate([w_branch_a[0].astype(BF16), w_branch_r[0].astype(BF16), w_out[0].astype(BF16),
                                conv_pad], axis=0)
    sq_sems, sq_src, sq_land, sq_token = _gather_start(sq_stack, w_in_all, "gather_w_sq_start")

    w_qkv = _interleave_rows(wt_full[0:3 * D_MODEL])
    w_f = jnp.pad(wt_full[3 * D_MODEL:3 * D_MODEL + HEADS], ((0, LANES - HEADS), (0, 0)))
    w_rest = wt_full[3 * D_MODEL + HEADS:IN_USED]
    b_qkv = _interleave_qkv(b_in[:, 0:3 * D_MODEL]) + sq_token[0, 0]
    b_f = jnp.pad(b_in[:, 3 * D_MODEL:3 * D_MODEL + HEADS], ((0, 0), (0, LANES - HEADS)))
    b_rest = b_in[:, 3 * D_MODEL + HEADS:IN_USED]

    def blockdiag(w):
        w2 = w.reshape(N_CBLK, 2, HEAD_DIM, HEAD_DIM)
        zz = jnp.zeros((N_CBLK, HEAD_DIM, HEAD_DIM), w.dtype)
        top = jnp.concatenate([w2[:, 0], zz], axis=2)
        bot = jnp.concatenate([zz, w2[:, 1]], axis=2)
        return jnp.concatenate([top, bot], axis=1).astype(BF16)

    bda, bdx = blockdiag(rg_wa[0]), blockdiag(rg_wx[0])

    x2 = x.reshape(t, D_MODEL)
    tgt2 = loss_target.reshape(t, D_MODEL)
    h, qkv, zf = _prenorm_inproj(x2, pre_norm_w, w_qkv, b_qkv, w_f, b_f)
    zrest = _mm_bias(h, w_rest, b_rest, BF16, "inproj_rest")
    qkv3 = qkv.reshape(b, s, 3 * D_MODEL)
    zrest3 = zrest.reshape(b, s, 5 * D_MODEL)
    zf3 = zf.reshape(b, s, LANES)
    cexp3, crow = _fgate_fwd(zf3)
    yatt3, lse, ga = _attn_fwd(qkv3, cexp3, crow, zrest3)

    sq_all = _gather_wait(sq_sems, sq_src, sq_land, ga, "gather_w_sq_wait")
    sq_all = lax.dynamic_update_slice(sq_all, sq_stack[None], (me, 0, 0))
    wa = sq_all[:, 0:shard_rows].reshape(D_MODEL, D_MODEL)
    wr = sq_all[:, shard_rows:2 * shard_rows].reshape(D_MODEL, D_MODEL)
    wo = sq_all[:, 2 * shard_rows:3 * shard_rows].reshape(D_MODEL, D_MODEL)
    conv_all = sq_all[:, 3 * shard_rows:3 * shard_rows + 3 * CONV_W, 0:LANES].astype(F32)
    conv_all = (conv_all[:, 0:CONV_W] + conv_all[:, CONV_W:2 * CONV_W]) + conv_all[:, 2 * CONV_W:3 * CONV_W]
    conv_full = conv_all.transpose(1, 0, 2).reshape(CONV_W, D_MODEL)

    ylru3, gr = _rnn_fwd(zrest3, conv_full, conv_b, bda, bdx, rg_ba, rg_bx, rg_lambda)
    ya, yr, mm = _branch_merge(ga, gr, wa, wr, zrest)
    dy, do, acc_out = _out_loss(mm, wo, x2, tgt2, post_norm_w)

    dya, dyr, dz_mga, dz_mgr = _merge_bwd(do, wo, zrest, ya, yr)
    dyatt, dz_ga, dylru, dz_gr = _branch_bwd(dya, dyr, wa, wr, zrest, yatt3.reshape(t, D_MODEL),
                                             ylru3.reshape(t, D_MODEL))
    dz_xr, pvec, dbd = _rnn_bwd(zrest3, ylru3, dylru.reshape(b, s, D_MODEL), conv_full, conv_b, bda, bdx,
                                rg_ba, rg_bx, rg_lambda)
    dz_qkv, dc3 = _attn_bwd(qkv3, dyatt.reshape(b, s, D_MODEL), yatt3, lse, crow, cexp3)
    dz_f = _fgate_bwd(dc3, zf3)

    dw_qkv, db_qkv = _mm_tn(dz_qkv, h, "dw_qkv")
    dw_f, db_f = _mm_tn(dz_f, h, "dw_f")
    dw_parts, db_parts = [], []
    for nm, dzp in (("ga", dz_ga), ("xr", dz_xr), ("gr", dz_gr), ("mga", dz_mga), ("mgr", dz_mgr)):
        dwp, dbp = _mm_tn(dzp, h, "dw_" + nm)
        dw_parts.append(dwp)
        db_parts.append(dbp[0:1])

    zeros_tail = jnp.zeros((IN_TOTAL - IN_USED, D_MODEL), F32)
    dwt_full = jnp.concatenate([_deinterleave_rows(dw_qkv), dw_f[0:HEADS]] + dw_parts + [zeros_tail], axis=0)
    dw_in_send = dwt_full.reshape(N_CHIPS, 2, W_SHARD, D_MODEL).transpose(1, 0, 2, 3)
    swp_sems, dw_in_src, swp_land, swp_token = _swap_start(dw_in_send, db_f, "swap_dw_in_start")
    dw_a, _ = _mm_tn(ga, dya, "dw_a", after=swp_token)
    dw_r, _ = _mm_tn(gr, dyr, "dw_r", after=swp_token)
    dw_o, _ = _mm_tn(mm, do, "dw_o", after=swp_token)
    dw_in_send, sib_in = _swap_wait(swp_sems, dw_in_src, swp_land, dw_o, "swap_dw_in_wait")
    by_dest = lambda a: a.reshape(N_CHIPS, 2, shard_rows, D_MODEL).transpose(1, 0, 2, 3)
    dw_sq_send = jnp.concatenate([by_dest(dw_a), by_dest(dw_r), by_dest(dw_o)], axis=2)

    db_in_full = jnp.concatenate([_deinterleave_qkv(db_qkv[0:1]), db_f[0:1, 0:HEADS]] + db_parts
                                 + [jnp.zeros((1, IN_TOTAL - IN_USED), F32)], axis=1)
    d_rg_wa = jnp.stack([dbd[:, 0, 0:HEAD_DIM, 0:HEAD_DIM], dbd[:, 0, HEAD_DIM:, HEAD_DIM:]], axis=1)
    d_rg_wx = jnp.stack([dbd[:, 1, 0:HEAD_DIM, 0:HEAD_DIM], dbd[:, 1, HEAD_DIM:, HEAD_DIM:]], axis=1)
    small_g = _pack_small(jnp.zeros((1, D_MODEL), F32), pvec[4:5], pvec[5:6], pvec[6:7], pvec[7:8], acc_out[0:1],
                          acc_out[1:2], db_in_full, pvec[0:4], d_rg_wa, d_rg_wx)
    sm_sems, sm_src, sm_land, sm_token = _gather_start(small_g, dw_o, "gather_small_start")

    sqs_sems, dw_sq_src, sqs_land, _ = _swap_start(dw_sq_send, sm_token, "swap_dw_sq_start")
    chip_in, own_in = _pair_add(dw_in_send, sib_in, place, "pair_add_in")
    dw_sq_send, sib_sq = _swap_wait(sqs_sems, dw_sq_src, sqs_land, chip_in, "swap_dw_sq_wait")
    chip_sq, own_sq = _pair_add(dw_sq_send, sib_sq, place, "pair_add_sq")
    sems, sent, lands, token = _exchange_chips_start([chip_in, chip_sq], "exchange_dw_start")

    wt = lambda lo: w_rest[lo * D_MODEL:(lo + 1) * D_MODEL]
    grad_x2, acc_pre = _dh_final(
        [(dz_qkv, w_qkv), (dz_f, w_f), (dz_ga, wt(0)), (dz_xr, wt(1)), (dz_gr, wt(2)), (dz_mga, wt(3)),
         (dz_mgr, wt(4))], token, x2, dy, pre_norm_w)
    pre_sems, pre_src, pre_land, pre_token = _gather_start(acc_pre, grad_x2, "gather_pre_start")
    recv_in, recv_sq = _exchange_chips_wait(sems, sent, lands, pre_token, "exchange_dw_wait")

    g_in, d_in, nm_in, nv_in = [a.T for a in _reduce_adamw(
        own_in, recv_in, place, w_in[0].T, m_w_in[0].T, v_w_in[0].T, "adamw_w_in")]
    sq_out = _reduce_adamw_stacked(
        own_sq, recv_sq, place,
        [(w_branch_a, m_w_branch_a, v_w_branch_a), (w_branch_r, m_w_branch_r, v_w_branch_r),
         (w_out, m_w_out, v_w_out)], "adamw_w_sq")
    pre_all = _gather_wait(pre_sems, pre_src, pre_land, sq_out[2][1], "gather_pre_wait")
    pre_all = lax.dynamic_update_slice(pre_all, acc_pre[None], (me, 0, 0))
    small_all = _gather_wait(sm_sems, sm_src, sm_land, pre_all, "gather_small_wait")
    small_all = lax.dynamic_update_slice(small_all, small_g[None], (me, 0, 0))

    def place_conv(a):
        return lax.dynamic_update_slice(jnp.zeros((CONV_W, D_MODEL), F32), a[0], (0, me * LANES))

    zrow = jnp.zeros((1, D_MODEL), F32)
    vector_names = ["pre_norm_w", "conv_b", "rg_ba", "rg_bx", "rg_lambda", "post_norm_w"]
    vectors = [(pre_norm_w, m_pre_norm_w, v_pre_norm_w), (conv_b, m_conv_b, v_conv_b), (rg_ba, m_rg_ba, v_rg_ba),
               (rg_bx, m_rg_bx, v_rg_bx), (rg_lambda, m_rg_lambda, v_rg_lambda),
               (post_norm_w, m_post_norm_w, v_post_norm_w)]
    small_w = _pack_small(zrow, zrow, zrow, zrow, zrow, zrow, zrow, b_in, place_conv(conv_w), rg_wa[0], rg_wx[0])
    small_m = _pack_small(zrow, zrow, zrow, zrow, zrow, zrow, zrow, m_b_in, place_conv(m_conv_w), m_rg_wa[0],
                          m_rg_wx[0])
    small_v = _pack_small(zrow, zrow, zrow, zrow, zrow, zrow, zrow, v_b_in, place_conv(v_conv_w), v_rg_wa[0],
                          v_rg_wx[0])
    packed, vector_out, loss_tile = _reduce_small(small_all, pre_all, small_w, small_m, small_v, vectors)
    outs_small = [_unpack_small(p) for p in packed]
    loss = loss_tile[0, 0]

    def leaf(kind, name):
        if name == "w_in":
            return (g_in, d_in, nm_in, nv_in)[kind][None]
        if name in ("w_branch_a", "w_branch_r", "w_out"):
            return sq_out[("w_branch_a", "w_branch_r", "w_out").index(name)][kind]
        if name == "conv_w":
            return lax.dynamic_slice(outs_small[kind]["conv_w_full"], (0, me * LANES), (CONV_W, LANES))[None]
        if name in vector_names:
            return vector_out[vector_names.index(name)][kind]
        return outs_small[kind][name]

    names = ["pre_norm_w", "w_in", "b_in", "conv_w", "conv_b", "rg_wa", "rg_ba", "rg_wx", "rg_bx", "rg_lambda",
             "w_branch_a", "w_branch_r", "w_out", "post_norm_w"]
    out = [loss, grad_x2.reshape(b, s, D_MODEL)]
    for kind in range(4):
        out += [leaf(kind, nm) for nm in names]
    return tuple(out)
```

```python
import jax
import jax.numpy as jnp
from jax import lax
from jax.experimental import pallas as pl
from jax.experimental.pallas import tpu as pltpu

F32 = jnp.float32
BF16 = jnp.bfloat16

N_DEV = 8
D_MODEL = 1024
HEADS = 16
HEAD_DIM = 64
HEAD_PAIRS = HEADS // 2
LANES = 128
N_CBLK = D_MODEL // LANES
CONV_W = 4
RG_C = 8.0
NORM_EPS = 1e-6
MASK_VALUE = -1e30
IN_USED = 8208
IN_TOTAL = 9232
W_SHARD = IN_TOTAL // N_DEV

ADAM_LR = 0.001
ADAM_B1 = 0.9
ADAM_B2 = 0.999
ADAM_EPS = 1e-08
ADAM_WD = 0.01
ADAM_STEP = 10

ATT_TILE_FWD = 256
ATT_TILE_BWD = 512
SCAN_TILE = 256
SMALL_ROWS = 152
LOSS_ROW = 6


def _cparams(sem=None, vmem_mb=None):
    kw = {}
    if sem is not None:
        kw["dimension_semantics"] = sem
    if vmem_mb is not None:
        kw["vmem_limit_bytes"] = vmem_mb * 1024 * 1024
    return pltpu.CompilerParams(**kw)


def _sigmoid(x):
    return 1.0 / (1.0 + jnp.exp(-x))


def _softplus(x):
    return jnp.maximum(x, 0.0) + jnp.log1p(jnp.exp(-jnp.abs(x)))


def _one_minus_exp(y, exp_y):
    series = -y * (1.0 + y * (1.0 / 2 + y * (1.0 / 6 + y * (1.0 / 24 + y * (1.0 / 120)))))
    return jnp.where(y > -0.0625, series, 1.0 - exp_y)


def _split3(x):
    hi = x.astype(BF16)
    r1 = x - hi.astype(F32)
    mid = r1.astype(BF16)
    lo = (r1 - mid.astype(F32)).astype(BF16)
    return hi, mid, lo


def _dot(a, b):
    return jnp.dot(a, b, preferred_element_type=F32)


def _dot_nt(a, b):
    return lax.dot_general(a, b, (((1,), (1,)), ((), ())), preferred_element_type=F32)


def _dot_tn(a, b):
    return lax.dot_general(a, b, (((0,), (0,)), ((), ())), preferred_element_type=F32)


def _iota(shape, dim):
    return lax.broadcasted_iota(jnp.int32, shape, dim)


_ANY = pl.BlockSpec(memory_space=pl.ANY)
_MESH = pl.DeviceIdType.MESH
N_CHIPS = 4


def _place():
    x, y, c = lax.axis_index("x"), lax.axis_index("y"), lax.axis_index("c")
    other_chips = [(1 - x, y), (x, 1 - y), (1 - x, 1 - y)]
    return x, y, c, other_chips


def _gather(x_shard, name):
    def body(x_ref, out_ref, send_sems, recv_sems, local_sem):
        x, y, c, chips = _place()
        me, sibling = (x, y, c), (x, y, 1 - c)

        def slot(p):
            return out_ref.at[4 * p[0] + 2 * p[1] + p[2]]

        def copy(k, block, to, src=None):
            return pltpu.make_async_remote_copy(
                src_ref=slot(block) if src is None else src, dst_ref=slot(block),
                send_sem=send_sems.at[k], recv_sem=recv_sems.at[k], device_id=to, device_id_type=_MESH)

        mine = pltpu.make_async_copy(x_ref, slot(me), local_sem)
        mine.start()
        first = [copy(0, me, sibling, src=x_ref)]
        first += [copy(1 + j, me, (*chip, c), src=x_ref) for j, chip in enumerate(chips)]
        for cp in first:
            cp.start()
        passed = [copy(4 + j, (*chip, c), sibling) for j, chip in enumerate(chips)]
        for j, chip in enumerate(chips):
            copy(1 + j, (*chip, c), me).wait_recv()
            passed[j].start()
        copy(0, sibling, me).wait_recv()
        for j, chip in enumerate(chips):
            copy(4 + j, (*chip, 1 - c), me).wait_recv()
        for cp in first + passed:
            cp.wait_send()
        mine.wait()

    return pl.pallas_call(
        body, name=name,
        out_shape=jax.ShapeDtypeStruct((N_DEV,) + tuple(x_shard.shape), x_shard.dtype),
        in_specs=[_ANY], out_specs=_ANY,
        scratch_shapes=[pltpu.SemaphoreType.DMA((7,)), pltpu.SemaphoreType.DMA((7,)), pltpu.SemaphoreType.DMA],
    )(x_shard)


def _blocks_2d(r, c):
    if r % 128 == 0:
        return (128, c), r // 128, lambda i: (i, 0)
    return (r, 256), c // 256, lambda i: (0, i)


def _pair_add(src, recv, place, name):
    _, _, r, c = src.shape
    blk, nblk, at = _blocks_2d(r, c)

    def body(place_ref, a_ref, b_ref, q16_ref, own_ref):
        q = a_ref[...] + b_ref[...]
        q16_ref[...] = q.astype(BF16)

        @pl.when(pl.program_id(1) == place_ref[1])
        def _():
            own_ref[...] = q

    grid_spec = pltpu.PrefetchScalarGridSpec(
        num_scalar_prefetch=1, grid=(nblk, N_CHIPS),
        in_specs=[pl.BlockSpec((None, None) + blk, lambda i, j, pr: (pr[0], j) + at(i)),
                  pl.BlockSpec((None,) + blk, lambda i, j, pr: (j,) + at(i))],
        out_specs=[pl.BlockSpec((None,) + blk, lambda i, j, pr: (j,) + at(i)),
                   pl.BlockSpec(blk, lambda i, j, pr: at(i))])
    return pl.pallas_call(
        body, name=name, grid_spec=grid_spec,
        out_shape=[jax.ShapeDtypeStruct((N_CHIPS, r, c), BF16), jax.ShapeDtypeStruct((r, c), F32)],
        compiler_params=_cparams(("parallel", "arbitrary")),
    )(place, src, recv)


_HBM = pl.BlockSpec(memory_space=pltpu.HBM)
_SEM = pl.BlockSpec(memory_space=pltpu.SEMAPHORE)
_DATAFLOW = pltpu.SideEffectType.DATAFLOW_SIDE_EFFECTING


def _chip_copy(src_ref, land_ref, send_sem, recv_sem, k, chips, c, land):
    chip = chips[k]
    return pltpu.make_async_remote_copy(
        src_ref=src_ref.at[2 * chip[0] + chip[1]], dst_ref=land_ref.at[land],
        send_sem=send_sem, recv_sem=recv_sem, device_id=(*chip, c), device_id_type=_MESH)


def _exchange_chips_start(srcs, name):
    n = len(srcs)
    ncp = 3 * n

    def body(*refs):
        src_refs, land_refs = refs[:n], refs[n:2 * n]
        sems = refs[4 * n:4 * n + 2 * ncp]
        token = refs[-1]
        x, y, c, chips = _place()
        for i in range(n):
            for k in range(3):
                j = 3 * i + k
                _chip_copy(src_refs[i], land_refs[i], sems[j], sems[ncp + j], k, chips, c, 2 * x + y).start()
        token[...] = jnp.zeros_like(token)

    hbm = [pltpu.HBM(a.shape, a.dtype) for a in srcs]
    lands = [pltpu.with_memory_space_constraint(lax.empty(a.shape, a.dtype), pltpu.HBM) for a in srcs]
    res = pl.pallas_call(
        body, name=name,
        out_shape=(*hbm, *hbm, *([pltpu.SemaphoreType.DMA(())] * (2 * ncp)), jax.ShapeDtypeStruct((8, LANES), F32)),
        in_specs=[_HBM] * (2 * n),
        out_specs=(*([_HBM] * (2 * n)), *([_SEM] * (2 * ncp)), pl.BlockSpec(memory_space=pltpu.VMEM)),
        input_output_aliases={i: i for i in range(2 * n)},
        compiler_params=pltpu.CompilerParams(has_side_effects=_DATAFLOW),
    )(*[pltpu.with_memory_space_constraint(a, pltpu.HBM) for a in srcs], *lands)
    return list(res[2 * n:2 * n + 2 * ncp]), list(res[:n]), list(res[n:2 * n]), res[-1]


def _exchange_chips_wait(sems, srcs, lands, after, name):
    n = len(srcs)
    ncp = 3 * n

    def body(*refs):
        src_refs, land_refs = refs[:n], refs[n:2 * n]
        sem_refs = refs[2 * n:2 * n + 2 * ncp]
        x, y, c, chips = _place()
        for i in range(n):
            for k in range(3):
                j = 3 * i + k
                cp = _chip_copy(src_refs[i], land_refs[i], sem_refs[j], sem_refs[ncp + j], k, chips, c,
                                2 * chips[k][0] + chips[k][1])
                cp.wait_send()
                cp.wait_recv()

    hbm = [pltpu.HBM(a.shape, a.dtype) for a in srcs]
    res = pl.pallas_call(
        body, name=name, out_shape=(*hbm, *hbm),
        in_specs=[_HBM] * (2 * n) + [_SEM] * (2 * ncp) + [_ANY], out_specs=tuple([_HBM] * (2 * n)),
        input_output_aliases={i: i for i in range(2 * n)},
        compiler_params=pltpu.CompilerParams(has_side_effects=_DATAFLOW),
    )(*srcs, *lands, *sems, after)
    return list(res[n:2 * n])


def _swap_start(src, after, name):
    def body(src_ref, land_ref, after_ref, src_thru, land_thru, send_sem, recv_sem, token):
        x, y, c, _ = _place()
        pltpu.make_async_remote_copy(src_ref=src_ref.at[1 - c], dst_ref=land_ref, send_sem=send_sem,
                                     recv_sem=recv_sem, device_id=(x, y, 1 - c), device_id_type=_MESH).start()
        token[...] = jnp.zeros_like(token)

    land = pltpu.with_memory_space_constraint(lax.empty(src.shape[1:], src.dtype), pltpu.HBM)
    res = pl.pallas_call(
        body, name=name,
        out_shape=(pltpu.HBM(src.shape, src.dtype), pltpu.HBM(land.shape, land.dtype),
                   pltpu.SemaphoreType.DMA(()), pltpu.SemaphoreType.DMA(()), jax.ShapeDtypeStruct((8, LANES), F32)),
        in_specs=[_HBM, _HBM, _ANY],
        out_specs=(_HBM, _HBM, _SEM, _SEM, pl.BlockSpec(memory_space=pltpu.VMEM)),
        input_output_aliases={0: 0, 1: 1},
        compiler_params=pltpu.CompilerParams(has_side_effects=_DATAFLOW),
    )(pltpu.with_memory_space_constraint(src, pltpu.HBM), land, after)
    return [res[2], res[3]], res[0], res[1], res[-1]


def _swap_wait(sems, src, land, after, name):
    def body(src_ref, land_ref, send_sem, recv_sem, after_ref, src_out, land_out):
        x, y, c, _ = _place()
        cp = pltpu.make_async_remote_copy(src_ref=src_ref.at[1 - c], dst_ref=land_ref, send_sem=send_sem,
                                          recv_sem=recv_sem, device_id=(x, y, 1 - c), device_id_type=_MESH)
        cp.wait_send()
        cp.wait_recv()

    res = pl.pallas_call(
        body, name=name, out_shape=(pltpu.HBM(src.shape, src.dtype), pltpu.HBM(land.shape, land.dtype)),
        in_specs=[_HBM, _HBM, _SEM, _SEM, _ANY], out_specs=(_HBM, _HBM),
        input_output_aliases={0: 0, 1: 1},
        compiler_params=pltpu.CompilerParams(has_side_effects=_DATAFLOW),
    )(src, land, *sems, after)
    return res[0], res[1]


def _peer_copy(src_ref, land_ref, send_sem, recv_sem, k, place, land):
    x, y, c = place
    peer = (1 - x if k & 4 else x, 1 - y if k & 2 else y, 1 - c if k & 1 else c)
    return pltpu.make_async_remote_copy(
        src_ref=src_ref, dst_ref=land_ref.at[land], send_sem=send_sem, recv_sem=recv_sem,
        device_id=peer, device_id_type=_MESH)


def _gather_start(x_shard, after, name):
    npeer = N_DEV - 1

    def body(x_ref, land_ref, after_ref, x_thru, land_thru, *rest):
        sems, token = rest[:2 * npeer], rest[-1]
        x, y, c, _ = _place()
        for k in range(1, N_DEV):
            _peer_copy(x_ref, land_ref, sems[k - 1], sems[npeer + k - 1], k, (x, y, c), 4 * x + 2 * y + c).start()
        token[...] = jnp.zeros_like(token)

    land = pltpu.with_memory_space_constraint(lax.empty((N_DEV,) + tuple(x_shard.shape), x_shard.dtype), pltpu.HBM)
    res = pl.pallas_call(
        body, name=name,
        out_shape=(pltpu.HBM(x_shard.shape, x_shard.dtype), pltpu.HBM(land.shape, land.dtype),
                   *([pltpu.SemaphoreType.DMA(())] * (2 * npeer)), jax.ShapeDtypeStruct((8, LANES), F32)),
        in_specs=[_HBM, _HBM, _ANY],
        out_specs=(_HBM, _HBM, *([_SEM] * (2 * npeer)), pl.BlockSpec(memory_space=pltpu.VMEM)),
        input_output_aliases={0: 0, 1: 1},
        compiler_params=pltpu.CompilerParams(has_side_effects=_DATAFLOW),
    )(pltpu.with_memory_space_constraint(x_shard, pltpu.HBM), land, after)
    return list(res[2:2 + 2 * npeer]), res[0], res[1], res[-1]


def _gather_wait(sems, src, land, after, name):
    npeer = N_DEV - 1

    def body(x_ref, land_ref, *rest):
        sem_refs = rest[:2 * npeer]
        x, y, c, _ = _place()
        for k in range(1, N_DEV):
            peer_index = (4 * x + 2 * y + c) ^ k
            cp = _peer_copy(x_ref, land_ref, sem_refs[k - 1], sem_refs[npeer + k - 1], k, (x, y, c), peer_index)
            cp.wait_send()
            cp.wait_recv()

    res = pl.pallas_call(
        body, name=name, out_shape=(pltpu.HBM(src.shape, src.dtype), pltpu.HBM(land.shape, land.dtype)),
        in_specs=[_HBM, _HBM] + [_SEM] * (2 * npeer) + [_ANY], out_specs=(_HBM, _HBM),
        input_output_aliases={0: 0, 1: 1},
        compiler_params=pltpu.CompilerParams(has_side_effects=_DATAFLOW),
    )(src, land, *sems, after)
    return res[1]


def _prenorm_inproj(x2, w, wt_qkv, b_qkv, wt_f, b_f):
    t = x2.shape[0]
    tm = min(512, t)
    n = wt_qkv.shape[0]
    tn = D_MODEL

    def body(x_ref, w_ref, wq_ref, bq_ref, wf_ref, bf_ref, h_ref, qkv_ref, zf_ref):
        x = x_ref[...]
        r = lax.rsqrt(jnp.mean(x * x, axis=-1, keepdims=True) + NORM_EPS)
        h = (x * r * w_ref[...]).astype(BF16)
        h_ref[...] = h
        for j in range(n // tn):
            cols = slice(j * tn, (j + 1) * tn)
            qkv_ref[:, cols] = (_dot_nt(h, wq_ref[cols, :]) + bq_ref[:, cols]).astype(BF16)
        zf_ref[...] = _dot_nt(h, wf_ref[...]) + bf_ref[...]

    row = lambda c: pl.BlockSpec((tm, c), lambda i: (i, 0))
    whole = lambda a: pl.BlockSpec(a.shape, lambda i: (0, 0))
    return pl.pallas_call(
        body, name="prenorm_inproj_qkv", grid=(t // tm,),
        in_specs=[row(D_MODEL), whole(w), whole(wt_qkv), whole(b_qkv), whole(wt_f), whole(b_f)],
        out_specs=[row(D_MODEL), row(n), row(LANES)],
        out_shape=[jax.ShapeDtypeStruct((t, D_MODEL), BF16), jax.ShapeDtypeStruct((t, n), BF16),
                   jax.ShapeDtypeStruct((t, LANES), F32)],
        compiler_params=_cparams(("parallel",), vmem_mb=48),
    )(x2, w, wt_qkv, b_qkv, wt_f, b_f)


def _mm_bias(a, bt, bias, out_dtype, name):
    m, k = a.shape
    n = bt.shape[0]
    tm = min(1024, m)
    tn = min(1024, n)

    def body(a_ref, bt_ref, bias_ref, o_ref):
        aa = a_ref[...]
        for j in range(n // tn):
            cols = slice(j * tn, (j + 1) * tn)
            o_ref[:, cols] = (_dot_nt(aa, bt_ref[cols, :]) + bias_ref[:, cols]).astype(o_ref.dtype)

    return pl.pallas_call(
        body, name=name, grid=(m // tm,),
        in_specs=[pl.BlockSpec((tm, k), lambda i: (i, 0)),
                  pl.BlockSpec((n, k), lambda i: (0, 0), pipeline_mode=pl.Buffered(1)),
                  pl.BlockSpec((1, n), lambda i: (0, 0))],
        out_specs=pl.BlockSpec((tm, n), lambda i: (i, 0)),
        out_shape=jax.ShapeDtypeStruct((m, n), out_dtype),
        compiler_params=_cparams(("parallel",), vmem_mb=48),
    )(a, bt, bias)


def _mm_tn(a, b, name, after=None):
    t, m = a.shape
    n = b.shape[1]
    tm = min(1024, m)
    tk = min(2048, t)
    deps = [] if after is None else [after]

    def body(a_ref, b_ref, *refs):
        o_ref, s_ref = refs[len(deps):]
        kk = pl.program_id(1)

        @pl.when(kk == 0)
        def _():
            o_ref[...] = jnp.zeros_like(o_ref)
            s_ref[...] = jnp.zeros_like(s_ref)

        aa = a_ref[...]
        o_ref[...] += _dot_tn(aa, b_ref[...])
        s_ref[0:1, :] += jnp.sum(aa.astype(F32), axis=0, keepdims=True)

    return pl.pallas_call(
        body, name=name, grid=(m // tm, t // tk),
        in_specs=[pl.BlockSpec((tk, tm), lambda i, kk: (kk, i)), pl.BlockSpec((tk, n), lambda i, kk: (kk, 0))]
        + [pl.BlockSpec(d.shape, lambda i, kk: (0, 0)) for d in deps],
        out_specs=[pl.BlockSpec((tm, n), lambda i, kk: (i, 0)), pl.BlockSpec((8, tm), lambda i, kk: (0, i))],
        out_shape=[jax.ShapeDtypeStruct((m, n), F32), jax.ShapeDtypeStruct((8, m), F32)],
        compiler_params=_cparams(("parallel", "arbitrary"), vmem_mb=48),
    )(a, b, *deps)


def _fgate_fwd(zf3):
    b, s, _ = zf3.shape
    tb = SCAN_TILE
    nb = s // tb

    def body(z_ref, cexp_ref, crow_ref):
        tri = (_iota((tb, tb), 1) <= _iota((tb, tb), 0)).astype(BF16)
        expand = ((_iota((LANES, D_MODEL), 1) >> 6) == _iota((LANES, D_MODEL), 0)).astype(BF16)
        carry = jnp.zeros((1, LANES), F32)
        for i in range(nb):
            rows = slice(i * tb, (i + 1) * tb)
            z = z_ref[rows, :]
            lf = jnp.minimum(z, 0.0) - jnp.log1p(jnp.exp(-jnp.abs(z)))
            cb = sum(_dot(tri, part) for part in _split3(lf)) + carry
            carry = cb[tb - 1:tb, :]
            cexp_ref[rows, :] = sum(_dot(part, expand) for part in _split3(cb))
            crow_ref[:, rows] = cb.T[0:HEADS, :]

    return pl.pallas_call(
        body, name="fgate_fwd", grid=(b,),
        in_specs=[pl.BlockSpec((None, s, LANES), lambda i: (i, 0, 0))],
        out_specs=[pl.BlockSpec((None, s, D_MODEL), lambda i: (i, 0, 0)),
                   pl.BlockSpec((None, HEADS, s), lambda i: (i, 0, 0))],
        out_shape=[jax.ShapeDtypeStruct((b, s, D_MODEL), F32), jax.ShapeDtypeStruct((b, HEADS, s), F32)],
        compiler_params=_cparams(("parallel",)),
    )(zf3)


def _fgate_bwd(dc3, zf3):
    b, s, _ = zf3.shape
    tb = SCAN_TILE
    nb = s // tb

    def body(dc_ref, z_ref, o_ref):
        tri = (_iota((tb, tb), 1) >= _iota((tb, tb), 0)).astype(BF16)
        carry = jnp.zeros((1, LANES), F32)
        for i in reversed(range(nb)):
            rows = slice(i * tb, (i + 1) * tb)
            dlf = sum(_dot(tri, part) for part in _split3(dc_ref[rows, :])) + carry
            carry = dlf[0:1, :]
            o_ref[rows, :] = (dlf * _sigmoid(-z_ref[rows, :])).astype(BF16)

    return pl.pallas_call(
        body, name="fgate_bwd", grid=(b,),
        in_specs=[pl.BlockSpec((None, s, LANES), lambda i: (i, 0, 0)),
                  pl.BlockSpec((None, s, LANES), lambda i: (i, 0, 0))],
        out_specs=pl.BlockSpec((s, LANES), lambda i: (i, 0)),
        out_shape=jax.ShapeDtypeStruct((b * s, LANES), BF16),
        compiler_params=_cparams(("parallel",)),
    )(dc3, zf3)


def _spare(hh):
    return HEAD_DIM if hh == 0 else 0


def _put_cols(tile, mine, cols, first):
    lane = _iota((1, LANES), 1)
    out = jnp.where(mine, tile, jnp.zeros((), tile.dtype))
    for j, c in enumerate(cols):
        out = jnp.where(lane == first + j, c, out)
    return out


def _put_rows(tile, mine, rows, first):
    sub = _iota((LANES, 1), 0)
    out = jnp.where(mine, tile, jnp.zeros((), tile.dtype))
    for j, r in enumerate(rows):
        out = jnp.where(sub == first + j, r, out)
    return out


def _transpose_bf16(a):
    return a.astype(F32).T.astype(BF16)


def _attn_fwd(qkv3, cexp3, crow, zrest3):
    b, s, _ = qkv3.shape
    ta = ATT_TILE_FWD
    nq = s // ta
    hd = HEAD_DIM
    crow5 = crow.reshape(b, HEAD_PAIRS, 2, nq, ta)

    def body(qkv_ref, cq_ref, ck_ref, g_ref, y_ref, lse_ref, ga_ref, kt_scr, v_scr):
        lane = _iota((1, LANES), 1)
        sub = _iota((LANES, 1), 0)
        lane_mine = (lane < hd, lane >= hd)
        sub_mine = (sub < hd, sub >= hd)
        causal = _iota((ta, ta), 0) >= _iota((ta, ta), 1)
        one = jnp.ones((), BF16)

        for kj in range(nq):
            rows = slice(kj * ta, (kj + 1) * ta)
            kt = _transpose_bf16(qkv_ref[rows, LANES:2 * LANES])
            v = qkv_ref[rows, 2 * LANES:3 * LANES]
            for hh in range(2):
                ck = list(_split3(-ck_ref[hh, kj:kj + 1, :]))
                kt_scr[hh, kj] = _put_rows(kt, sub_mine[hh], [one, one, one] + ck, _spare(hh))
                v_scr[hh, kj] = _put_cols(v, lane_mine[hh], [one], _spare(hh))

        for qi in range(nq):
            rows = slice(qi * ta, (qi + 1) * ta)
            q = qkv_ref[rows, 0:LANES] * 0.125
            cq = cq_ref[rows, :]
            qh = [_put_cols(q, lane_mine[hh], list(_split3(cq[:, hh * hd:hh * hd + 1])) + [one, one, one], _spare(hh))
                  for hh in range(2)]
            st = [(jnp.full((ta, 1), MASK_VALUE, F32), jnp.zeros((ta, LANES), F32))] * 2
            for kj in range(qi + 1):
                for hh in range(2):
                    m, acc = st[hh]
                    sc = _dot(qh[hh], kt_scr[hh, kj])
                    if kj == qi:
                        sc = jnp.where(causal, sc, MASK_VALUE)
                    mn = jnp.maximum(m, jnp.max(sc, axis=-1, keepdims=True))
                    p = jnp.exp(sc - mn).astype(BF16)
                    st[hh] = (mn, jnp.exp(m - mn) * acc + _dot(p, v_scr[hh, kj]))
            (ma, acca), (mb, accb) = st
            la = acca[:, hd:hd + 1]
            lb = accb[:, 0:1]
            y = jnp.where(lane_mine[0], acca * (1.0 / la), accb * (1.0 / lb))
            lse = jnp.where(lane_mine[0], ma + jnp.log(la), mb + jnp.log(lb)).T
            lse_ref[0, qi:qi + 1, :] = lse[0:1, :]
            lse_ref[1, qi:qi + 1, :] = lse[hd:hd + 1, :]
            y_ref[rows, :] = y
            g = g_ref[rows, :].astype(F32)
            ga_ref[rows, :] = (y * (g * _sigmoid(g))).astype(BF16)

    blk = lambda w: pl.BlockSpec((None, s, w), lambda i, p: (i, 0, p))
    rows5 = pl.BlockSpec((None, None, 2, nq, ta), lambda i, p: (i, p, 0, 0, 0))
    yatt3, lse5, ga = pl.pallas_call(
        body, name="attn_fwd", grid=(b, HEAD_PAIRS),
        in_specs=[blk(3 * LANES), blk(LANES), rows5, blk(LANES)],
        out_specs=[blk(LANES), rows5, pl.BlockSpec((s, LANES), lambda i, p: (i, p))],
        out_shape=[jax.ShapeDtypeStruct((b, s, D_MODEL), F32),
                   jax.ShapeDtypeStruct((b, HEAD_PAIRS, 2, nq, ta), F32),
                   jax.ShapeDtypeStruct((b * s, D_MODEL), BF16)],
        scratch_shapes=[pltpu.VMEM((2, nq, LANES, ta), BF16), pltpu.VMEM((2, nq, ta, LANES), BF16)],
        compiler_params=_cparams(("parallel", "parallel")),
    )(qkv3, cexp3, crow5, zrest3)
    return yatt3, lse5.reshape(b, HEADS, s), ga


def _attn_bwd(qkv3, do3, y3, lse, crow, cexp3):
    b, s, _ = qkv3.shape
    ta = ATT_TILE_BWD
    nq = s // ta
    hd = HEAD_DIM
    lse5 = lse.reshape(b, HEAD_PAIRS, 2, nq, ta)
    crow5 = crow.reshape(b, HEAD_PAIRS, 2, nq, ta)

    def body(qkv_ref, do_ref, y_ref, lse_ref, crow_ref, cexp_ref, dqkv_ref, dc_ref,
             qa_scr, doa_scr, qst_scr, dot_scr, kt_scr, vt_scr, dq_scr, rs_scr):
        pair = pl.program_id(1)
        lane = _iota((1, LANES), 1)
        sub = _iota((LANES, 1), 0)
        lane_mine = (lane < hd, lane >= hd)
        sub_mine = (sub < hd, sub >= hd)
        causal = _iota((ta, ta), 0) >= _iota((ta, ta), 1)
        one = jnp.ones((), BF16)
        zero = jnp.zeros((), BF16)

        @pl.when(pair == 0)
        def _():
            dc_ref[...] = jnp.zeros_like(dc_ref)

        for i in range(nq):
            rows = slice(i * ta, (i + 1) * ta)
            qs = qkv_ref[rows, 0:LANES] * 0.125
            qst = _transpose_bf16(qs)
            kt = _transpose_bf16(qkv_ref[rows, LANES:2 * LANES])
            vt = _transpose_bf16(qkv_ref[rows, 2 * LANES:3 * LANES])
            do = do_ref[rows, :]
            dof = do.astype(F32)
            dot = dof.T.astype(BF16)
            pr = y_ref[rows, :] * dof
            cq = cexp_ref[rows, :]
            lse_c = jnp.where(sub == 0, lse_ref[0, i:i + 1, :],
                              jnp.where(sub == 1, lse_ref[1, i:i + 1, :], 0.0)).T
            for hh in range(2):
                sp = _spare(hh)
                dsum = jnp.sum(jnp.where(lane_mine[hh], pr, 0.0), axis=-1, keepdims=True)
                bias = cq[:, hh * hd:hh * hd + 1] - lse_c[:, hh:hh + 1]
                qa_scr[hh, i] = _put_cols(qs, lane_mine[hh], list(_split3(bias)) + [one, one, one], sp)
                doa_scr[hh, i] = _put_cols(do, lane_mine[hh], list(_split3(-dsum)), sp)
                qst_scr[hh, i] = jnp.where(sub_mine[hh], qst, zero)
                dot_scr[hh, i] = jnp.where(sub_mine[hh], dot, zero)
                ck = list(_split3(-crow_ref[hh, i:i + 1, :]))
                kt_scr[hh, i] = _put_rows(kt, sub_mine[hh], [one, one, one] + ck, sp)
                vt_scr[hh, i] = _put_rows(vt, sub_mine[hh], [one, one, one], sp)
            dq_scr[i] = jnp.zeros((ta, LANES), F32)
            rs_scr[i] = jnp.zeros((ta, LANES), F32)

        for kj in range(nq):
            krows = slice(kj * ta, (kj + 1) * ta)
            k = qkv_ref[krows, LANES:2 * LANES]
            km = (jnp.where(lane_mine[0], k, zero), jnp.where(lane_mine[1], k, zero))
            dkt = jnp.zeros((LANES, ta), F32)
            dvt = jnp.zeros((LANES, ta), F32)
            dcp = [jnp.zeros((8, ta), F32), jnp.zeros((8, ta), F32)]
            for qi in range(kj, nq):
                dq = jnp.zeros((ta, LANES), F32)
                rs = []
                for hh in range(2):
                    sc = _dot(qa_scr[hh, qi], kt_scr[hh, kj])
                    if qi == kj:
                        sc = jnp.where(causal, sc, MASK_VALUE)
                    p = jnp.exp(sc)
                    dsf = p * _dot(doa_scr[hh, qi], vt_scr[hh, kj])
                    dcp[hh] = dcp[hh] + jnp.sum(dsf.reshape(ta // 8, 8, ta), axis=0)
                    rs.append(jnp.sum(dsf, axis=-1, keepdims=True))
                    ds = dsf.astype(BF16)
                    dq = dq + _dot(ds, km[hh])
                    dkt = dkt + _dot(qst_scr[hh, qi], ds)
                    dvt = dvt + _dot(dot_scr[hh, qi], p.astype(BF16))
                dq_scr[qi] += dq
                rs_scr[qi] += jnp.where(lane == 0, rs[0], jnp.where(lane == 1, rs[1], 0.0))
            dqkv_ref[krows, LANES:2 * LANES] = dkt.T.astype(BF16)
            dqkv_ref[krows, 2 * LANES:3 * LANES] = dvt.T.astype(BF16)
            dca = jnp.sum(dcp[0], axis=0, keepdims=True)
            dcb = jnp.sum(dcp[1], axis=0, keepdims=True)
            dcs = jnp.where(sub == 0, dca, jnp.where(sub == 1, dcb, 0.0)).T
            dc_ref[krows, :] += (jnp.where(lane == 2 * pair, -dcs[:, 0:1], 0.0)
                                 + jnp.where(lane == 2 * pair + 1, -dcs[:, 1:2], 0.0))
        for qi in range(nq):
            rows = slice(qi * ta, (qi + 1) * ta)
            dqkv_ref[rows, 0:LANES] = (dq_scr[qi] * 0.125).astype(BF16)
            rq = rs_scr[qi]
            dc_ref[rows, :] += (jnp.where(lane == 2 * pair, rq[:, 0:1], 0.0)
                                + jnp.where(lane == 2 * pair + 1, rq[:, 1:2], 0.0))

    blk = lambda w: pl.BlockSpec((None, s, w), lambda i, p: (i, 0, p))
    rows5 = pl.BlockSpec((None, None, 2, nq, ta), lambda i, p: (i, p, 0, 0, 0))
    by_rows = lambda: pltpu.VMEM((2, nq, ta, LANES), BF16)
    by_cols = lambda: pltpu.VMEM((2, nq, LANES, ta), BF16)
    return pl.pallas_call(
        body, name="attn_bwd", grid=(b, HEAD_PAIRS),
        in_specs=[blk(3 * LANES), blk(LANES), blk(LANES), rows5, rows5, blk(LANES)],
        out_specs=[pl.BlockSpec((s, 3 * LANES), lambda i, p: (i, p)),
                   pl.BlockSpec((None, s, LANES), lambda i, p: (i, 0, 0))],
        out_shape=[jax.ShapeDtypeStruct((b * s, 3 * D_MODEL), BF16), jax.ShapeDtypeStruct((b, s, LANES), F32)],
        scratch_shapes=[by_rows(), by_rows(), by_cols(), by_cols(), by_cols(), by_cols(),
                        pltpu.VMEM((nq, ta, LANES), F32), pltpu.VMEM((nq, ta, LANES), F32)],
        compiler_params=_cparams(("parallel", "arbitrary")),
    )(qkv3, do3, y3, lse5, crow5, cexp3)


def _shifted(v, ks, rows, s):
    low = rows[0:8, :]
    out = []
    for k in ks:
        r = pltpu.roll(v, k % s, 0)
        if k > 0:
            out.append(jnp.concatenate([jnp.where(low >= k, r[0:8, :], 0.0), r[8:, :]], axis=0))
        else:
            out.append(jnp.concatenate([r[:s - 8, :], jnp.where(low < 8 + k, r[s - 8:, :], 0.0)], axis=0))
    return out


def _rnn_common(xr, cw_ref, cb_ref, bda_ref, bdx_ref, ba_ref, bx_ref, lam_ref, s):
    rows = _iota((s, LANES), 0)
    x1, x2, x3 = _shifted(xr, (1, 2, 3), rows, s)
    xc = cb_ref[...] + cw_ref[0:1, :] * x3
    xc = xc + cw_ref[1:2, :] * x2
    xc = xc + cw_ref[2:3, :] * x1
    xc = xc + cw_ref[3:4, :] * xr
    xcb = xc.astype(BF16)
    r = _sigmoid(_dot(xcb, bda_ref[...]) + ba_ref[...])
    i = _sigmoid(_dot(xcb, bdx_ref[...]) + bx_ref[...])
    sp = _softplus(-lam_ref[...])
    log_a = (-RG_C * r) * sp
    a = jnp.exp(log_a)
    a2 = a * a
    sq = jnp.sqrt(jnp.maximum(_one_minus_exp(log_a + log_a, a2), 0.0))
    return rows, (x1, x2, x3), xc, xcb, r, i, sp, a, a2, sq


def _scan_down(a, u, rows, s, s1, s2):
    low = rows & 7
    for sh in (1, 2, 4):
        keep = low >= sh
        u = u + a * jnp.where(keep, pltpu.roll(u, sh, 0), 0.0)
        a = a * jnp.where(keep, pltpu.roll(a, sh, 0), 1.0)
    ng = s // 8
    s1[...] = a
    s2[...] = u
    at = s1[pl.ds(7, ng, stride=8), :]
    ut = s2[pl.ds(7, ng, stride=8), :]
    grow = _iota((ng, LANES), 0)
    sh = 1
    while sh < ng:
        keep = grow >= sh
        ut = ut + at * jnp.where(keep, pltpu.roll(ut, sh, 0), 0.0)
        if sh * 2 < ng:
            at = at * jnp.where(keep, pltpu.roll(at, sh, 0), 1.0)
        sh *= 2
    h_in = jnp.where(grow >= 1, pltpu.roll(ut, 1, 0), 0.0)
    for k in range(8):
        s1[pl.ds(k, ng, stride=8), :] = h_in
    return u + a * s1[...]


def _scan_up(a, g, rows, s, s1, s2):
    low = rows & 7
    for sh in (1, 2, 4):
        keep = low < 8 - sh
        g = g + a * jnp.where(keep, pltpu.roll(g, s - sh, 0), 0.0)
        a = a * jnp.where(keep, pltpu.roll(a, s - sh, 0), 1.0)
    ng = s // 8
    s1[...] = a
    s2[...] = g
    at = s1[pl.ds(0, ng, stride=8), :]
    gt = s2[pl.ds(0, ng, stride=8), :]
    grow = _iota((ng, LANES), 0)
    sh = 1
    while sh < ng:
        keep = grow < ng - sh
        gt = gt + at * jnp.where(keep, pltpu.roll(gt, ng - sh, 0), 0.0)
        if sh * 2 < ng:
            at = at * jnp.where(keep, pltpu.roll(at, ng - sh, 0), 1.0)
        sh *= 2
    g_in = jnp.where(grow < ng - 1, pltpu.roll(gt, ng - 1, 0), 0.0)
    for k in range(8):
        s1[pl.ds(k, ng, stride=8), :] = g_in
    return g + a * s1[...]


def _rnn_specs(s):
    blk = lambda off: pl.BlockSpec((None, s, LANES), lambda cb, i: (i, 0, off + cb))
    vec = lambda r: pl.BlockSpec((r, LANES), lambda cb, i: (0, cb))
    mat = pl.BlockSpec((None, LANES, LANES), lambda cb, i: (cb, 0, 0))
    return blk, vec, mat


def _rnn_fwd(zrest3, conv_w, conv_b, bda, bdx, ba, bx, lam):
    b, s, _ = zrest3.shape

    def body(xr_ref, g_ref, cw_ref, cb_ref, bda_ref, bdx_ref, ba_ref, bx_ref, lam_ref, h_ref, gr_ref, s1, s2):
        xr = xr_ref[...].astype(F32)
        rows, _, xc, _, _, i, _, a, _, sq = _rnn_common(
            xr, cw_ref, cb_ref, bda_ref, bdx_ref, ba_ref, bx_ref, lam_ref, s)
        h = _scan_down(a, sq * (i * xc), rows, s, s1, s2)
        h_ref[...] = h
        g = g_ref[...].astype(F32)
        gr_ref[...] = (h * (g * _sigmoid(g))).astype(BF16)

    blk, vec, mat = _rnn_specs(s)
    return pl.pallas_call(
        body, name="rnn_fwd", grid=(N_CBLK, b),
        in_specs=[blk(N_CBLK), blk(2 * N_CBLK), vec(CONV_W), vec(1), mat, mat, vec(1), vec(1), vec(1)],
        out_specs=[blk(0), pl.BlockSpec((s, LANES), lambda cb, i: (i, cb))],
        out_shape=[jax.ShapeDtypeStruct((b, s, D_MODEL), F32), jax.ShapeDtypeStruct((b * s, D_MODEL), BF16)],
        scratch_shapes=[pltpu.VMEM((s, LANES), F32), pltpu.VMEM((s, LANES), F32)],
        compiler_params=_cparams(("parallel", "parallel")),
    )(zrest3, zrest3, conv_w, conv_b, bda, bdx, ba, bx, lam)


def _rnn_bwd(zrest3, h3, dh3, conv_w, conv_b, bda, bdx, ba, bx, lam):
    b, s, _ = zrest3.shape

    def body(xr_ref, h_ref, dh_ref, cw_ref, cb_ref, bda_ref, bdx_ref, ba_ref, bx_ref, lam_ref,
             dxr_ref, pv_ref, dbd_ref, s1, s2):
        @pl.when(pl.program_id(1) == 0)
        def _():
            pv_ref[...] = jnp.zeros_like(pv_ref)
            dbd_ref[...] = jnp.zeros_like(dbd_ref)

        xr = xr_ref[...].astype(F32)
        rows, (x1, x2, x3), xc, xcb, r, i, sp, a, a2, sq = _rnn_common(
            xr, cw_ref, cb_ref, bda_ref, bdx_ref, ba_ref, bx_ref, lam_ref, s)
        (a_next,) = _shifted(a, (-1,), rows, s)
        g = _scan_up(a_next, dh_ref[...], rows, s, s1, s2)
        (hp,) = _shifted(h_ref[...], (1,), rows, s)
        da = g * hp
        dsq = g * (i * xc)
        di = g * (sq * xc)
        dxc = g * (sq * i)
        dlog = da * a - dsq * (a2 / sq)
        dr = dlog * (-RG_C * sp)
        dpr = dr * (r * (1.0 - r))
        dpi = di * (i * (1.0 - i))
        dprb = dpr.astype(BF16)
        dpib = dpi.astype(BF16)
        dxc = dxc + _dot_nt(dprb, bda_ref[...]) + _dot_nt(dpib, bdx_ref[...])

        up1, up2, up3 = _shifted(dxc, (-1, -2, -3), rows, s)
        dxr = cw_ref[3:4, :] * dxc + cw_ref[2:3, :] * up1 + cw_ref[1:2, :] * up2 + cw_ref[0:1, :] * up3
        dxr_ref[...] = dxr.astype(BF16)

        def colsum(v):
            return jnp.sum(v, axis=0, keepdims=True)

        pv_ref[0:1, :] += colsum(dxc * x3)
        pv_ref[1:2, :] += colsum(dxc * x2)
        pv_ref[2:3, :] += colsum(dxc * x1)
        pv_ref[3:4, :] += colsum(dxc * xr)
        pv_ref[4:5, :] += colsum(dxc)
        pv_ref[5:6, :] += colsum(dpr)
        pv_ref[6:7, :] += colsum(dpi)
        pv_ref[7:8, :] += colsum(dlog * r) * (RG_C * _sigmoid(-lam_ref[...]))
        dbd_ref[0] += _dot_tn(xcb, dprb)
        dbd_ref[1] += _dot_tn(xcb, dpib)

    blk, vec, mat = _rnn_specs(s)
    hblk = pl.BlockSpec((None, s, LANES), lambda cb, i: (i, 0, cb))
    return pl.pallas_call(
        body, name="rnn_bwd", grid=(N_CBLK, b),
        in_specs=[blk(N_CBLK), hblk, hblk, vec(CONV_W), vec(1), mat, mat, vec(1), vec(1), vec(1)],
        out_specs=[pl.BlockSpec((s, LANES), lambda cb, i: (i, cb)), pl.BlockSpec((8, LANES), lambda cb, i: (0, cb)),
                   pl.BlockSpec((None, 2, LANES, LANES), lambda cb, i: (cb, 0, 0, 0))],
        out_shape=[jax.ShapeDtypeStruct((b * s, D_MODEL), BF16), jax.ShapeDtypeStruct((8, D_MODEL), F32),
                   jax.ShapeDtypeStruct((N_CBLK, 2, LANES, LANES), F32)],
        scratch_shapes=[pltpu.VMEM((s, LANES), F32), pltpu.VMEM((s, LANES), F32)],
        compiler_params=_cparams(("parallel", "arbitrary")),
    )(zrest3, h3, dh3, conv_w, conv_b, bda, bdx, ba, bx, lam)


def _branch_merge(ga, gr, wa, wr, zrest):
    t = ga.shape[0]
    tm = min(512, t)
    tn = D_MODEL

    def body(ga_ref, gr_ref, wa_ref, wr_ref, mga_ref, mgr_ref, ya_ref, yr_ref, m_ref):
        ya = _dot(ga_ref[...], wa_ref[...])
        yr = _dot(gr_ref[...], wr_ref[...])
        ya_ref[...] = ya.astype(BF16)
        yr_ref[...] = yr.astype(BF16)
        m_ref[...] = (_sigmoid(mga_ref[...].astype(F32)) * ya + _sigmoid(mgr_ref[...].astype(F32)) * yr).astype(BF16)

    nj = D_MODEL // tn
    act = pl.BlockSpec((tm, D_MODEL), lambda i, j: (i, 0))
    wgt = pl.BlockSpec((D_MODEL, tn), lambda i, j: (0, j))
    out = pl.BlockSpec((tm, tn), lambda i, j: (i, j))
    return pl.pallas_call(
        body, name="branch_merge", grid=(t // tm, nj),
        in_specs=[act, act, wgt, wgt, pl.BlockSpec((tm, tn), lambda i, j: (i, 3 * nj + j)),
                  pl.BlockSpec((tm, tn), lambda i, j: (i, 4 * nj + j))],
        out_specs=[out, out, out],
        out_shape=[jax.ShapeDtypeStruct((t, D_MODEL), BF16), jax.ShapeDtypeStruct((t, D_MODEL), BF16),
                   jax.ShapeDtypeStruct((t, D_MODEL), BF16)],
        compiler_params=_cparams(("parallel", "parallel")),
    )(ga, gr, wa, wr, zrest, zrest)


def _out_loss(m, wout, x2, tgt2, wpost):
    t = m.shape[0]
    tm = min(512, t)

    def body(m_ref, w_ref, x_ref, t_ref, wp_ref, dy_ref, do_ref, acc_ref):
        @pl.when(pl.program_id(0) == 0)
        def _():
            acc_ref[...] = jnp.zeros_like(acc_ref)

        o = _dot(m_ref[...], w_ref[...])
        r2 = lax.rsqrt(jnp.mean(o * o, axis=-1, keepdims=True) + NORM_EPS)
        n = o * r2
        wp = wp_ref[...]
        err = (x_ref[...] + n * wp) - t_ref[...]
        dy = err * (1.0 / D_MODEL)
        dn = dy * wp
        do = r2 * (dn - n * jnp.mean(dn * n, axis=-1, keepdims=True))
        dy_ref[...] = dy
        do_ref[...] = do.astype(BF16)
        acc_ref[0:1, :] += jnp.sum(dy * n, axis=0, keepdims=True)
        acc_ref[1:2, :] += jnp.sum(err * err, axis=0, keepdims=True)

    row = pl.BlockSpec((tm, D_MODEL), lambda i: (i, 0))
    return pl.pallas_call(
        body, name="out_loss", grid=(t // tm,),
        in_specs=[row, pl.BlockSpec((D_MODEL, D_MODEL), lambda i: (0, 0)), row, row,
                  pl.BlockSpec((1, D_MODEL), lambda i: (0, 0))],
        out_specs=[row, row, pl.BlockSpec((8, D_MODEL), lambda i: (0, 0))],
        out_shape=[jax.ShapeDtypeStruct((t, D_MODEL), F32), jax.ShapeDtypeStruct((t, D_MODEL), BF16),
                   jax.ShapeDtypeStruct((8, D_MODEL), F32)],
        compiler_params=_cparams(("arbitrary",)),
    )(m, wout, x2, tgt2, wpost)


def _merge_bwd(do, wout, zrest, ya, yr):
    t = do.shape[0]
    tm = min(512, t)
    tn = D_MODEL
    nj = D_MODEL // tn

    def body(do_ref, w_ref, mga_ref, mgr_ref, ya_ref, yr_ref, dya_ref, dyr_ref, dmga_ref, dmgr_ref):
        dm = _dot_nt(do_ref[...], w_ref[...])
        sa = _sigmoid(mga_ref[...].astype(F32))
        sr = _sigmoid(mgr_ref[...].astype(F32))
        dya_ref[...] = (dm * sa).astype(BF16)
        dyr_ref[...] = (dm * sr).astype(BF16)
        dmga_ref[...] = (dm * ya_ref[...].astype(F32) * (sa * (1.0 - sa))).astype(BF16)
        dmgr_ref[...] = (dm * yr_ref[...].astype(F32) * (sr * (1.0 - sr))).astype(BF16)

    out = pl.BlockSpec((tm, tn), lambda i, j: (i, j))
    bf = jax.ShapeDtypeStruct((t, D_MODEL), BF16)
    return pl.pallas_call(
        body, name="merge_bwd", grid=(t // tm, nj),
        in_specs=[pl.BlockSpec((tm, D_MODEL), lambda i, j: (i, 0)), pl.BlockSpec((tn, D_MODEL), lambda i, j: (j, 0)),
                  pl.BlockSpec((tm, tn), lambda i, j: (i, 3 * nj + j)),
                  pl.BlockSpec((tm, tn), lambda i, j: (i, 4 * nj + j)), out, out],
        out_specs=[out, out, out, out],
        out_shape=[bf, bf, bf, bf],
        compiler_params=_cparams(("parallel", "parallel")),
    )(do, wout, zrest, zrest, ya, yr)


def _branch_bwd(dya, dyr, wa, wr, zrest, yatt, ylru):
    t = dya.shape[0]
    tm = min(512, t)
    tn = D_MODEL
    nj = D_MODEL // tn

    def body(dya_ref, dyr_ref, wa_ref, wr_ref, ga_ref, gr_ref, ya_ref, yl_ref,
             dyatt_ref, dga_ref, dyl_ref, dgr_ref):
        dga = _dot_nt(dya_ref[...], wa_ref[...])
        dgr = _dot_nt(dyr_ref[...], wr_ref[...])
        g = ga_ref[...].astype(F32)
        sg = _sigmoid(g)
        dyatt_ref[...] = (dga * (g * sg)).astype(BF16)
        dga_ref[...] = (dga * ya_ref[...] * (sg * (1.0 + g * (1.0 - sg)))).astype(BF16)
        g = gr_ref[...].astype(F32)
        sg = _sigmoid(g)
        dyl_ref[...] = dgr * (g * sg)
        dgr_ref[...] = (dgr * yl_ref[...] * (sg * (1.0 + g * (1.0 - sg)))).astype(BF16)

    act = pl.BlockSpec((tm, D_MODEL), lambda i, j: (i, 0))
    wgt = pl.BlockSpec((tn, D_MODEL), lambda i, j: (j, 0))
    out = pl.BlockSpec((tm, tn), lambda i, j: (i, j))
    bf = jax.ShapeDtypeStruct((t, D_MODEL), BF16)
    return pl.pallas_call(
        body, name="branch_bwd", grid=(t // tm, nj),
        in_specs=[act, act, wgt, wgt, pl.BlockSpec((tm, tn), lambda i, j: (i, j)),
                  pl.BlockSpec((tm, tn), lambda i, j: (i, 2 * nj + j)), out, out],
        out_specs=[out, out, out, out],
        out_shape=[bf, bf, jax.ShapeDtypeStruct((t, D_MODEL), F32), bf],
        compiler_params=_cparams(("parallel", "parallel")),
    )(dya, dyr, wa, wr, zrest, zrest, yatt, ylru)


def _dh_final(parts, after, x2, dy, wpre):
    t = x2.shape[0]
    tm = min(256, t)
    np_ = len(parts)

    def body(*refs):
        x_ref, dy_ref, w_ref = refs[2 * np_ + 1:2 * np_ + 4]
        gx_ref, pw_ref = refs[2 * np_ + 4:]

        @pl.when(pl.program_id(0) == 0)
        def _():
            pw_ref[...] = jnp.zeros_like(pw_ref)

        dh = _dot(refs[0][...], refs[np_][...])
        for p in range(1, np_):
            dh = dh + _dot(refs[p][...], refs[np_ + p][...])
        x = x_ref[...]
        r = lax.rsqrt(jnp.mean(x * x, axis=-1, keepdims=True) + NORM_EPS)
        xn = x * r
        dxn = dh * w_ref[...]
        gx_ref[...] = r * (dxn - xn * jnp.mean(dxn * xn, axis=-1, keepdims=True)) + dy_ref[...]
        pw_ref[0:1, :] += jnp.sum(dh * xn, axis=0, keepdims=True)

    row = pl.BlockSpec((tm, D_MODEL), lambda i: (i, 0))
    in_specs = [pl.BlockSpec((tm, dz.shape[1]), lambda i: (i, 0)) for dz, _ in parts]
    in_specs += [pl.BlockSpec(w.shape, lambda i: (0, 0), pipeline_mode=pl.Buffered(1)) for _, w in parts]
    in_specs += [pl.BlockSpec(after.shape, lambda i: (0, 0)), row, row, pl.BlockSpec((1, D_MODEL), lambda i: (0, 0))]
    return pl.pallas_call(
        body, name="dh_final", grid=(t // tm,),
        in_specs=in_specs,
        out_specs=[row, pl.BlockSpec((8, D_MODEL), lambda i: (0, 0))],
        out_shape=[jax.ShapeDtypeStruct((t, D_MODEL), F32), jax.ShapeDtypeStruct((8, D_MODEL), F32)],
        compiler_params=_cparams(("arbitrary",), vmem_mb=48),
    )(*[dz for dz, _ in parts], *[w for _, w in parts], after, x2, dy, wpre)


def _adamw(w, g, m, v):
    m = ADAM_B1 * m + (1.0 - ADAM_B1) * g
    v = ADAM_B2 * v + (1.0 - ADAM_B2) * (g * g)
    m_hat = m / (1.0 - ADAM_B1 ** ADAM_STEP)
    v_hat = v / (1.0 - ADAM_B2 ** ADAM_STEP)
    delta = -ADAM_LR * (m_hat / (jnp.sqrt(v_hat) + ADAM_EPS) + ADAM_WD * w)
    return delta, m, v


def _reduce_adamw(own, parts, place, w, m, v, name):
    r, c = w.shape
    blk, nblk, at = _blocks_2d(r, c)

    def body(place_ref, own_ref, p_ref, w_ref, m_ref, v_ref, g_ref, d_ref, nm_ref, nv_ref):
        mine = place_ref[1]
        own_blk = own_ref[...]
        g = jnp.where(mine == 0, own_blk, p_ref[0].astype(F32))
        for j in range(1, N_CHIPS):
            g = g + jnp.where(mine == j, own_blk, p_ref[j].astype(F32))
        d, nm, nv = _adamw(w_ref[...], g, m_ref[...], v_ref[...])
        g_ref[...] = g
        d_ref[...] = d
        nm_ref[...] = nm
        nv_ref[...] = nv

    row = pl.BlockSpec(blk, lambda i, pr: at(i))
    sh = jax.ShapeDtypeStruct((r, c), F32)
    grid_spec = pltpu.PrefetchScalarGridSpec(
        num_scalar_prefetch=1, grid=(nblk,),
        in_specs=[row, pl.BlockSpec((N_CHIPS,) + blk, lambda i, pr: (0,) + at(i)), row, row, row],
        out_specs=[row, row, row, row])
    return pl.pallas_call(
        body, name=name, grid_spec=grid_spec, out_shape=[sh, sh, sh, sh],
        compiler_params=_cparams(("parallel",)),
    )(place, own, parts, w, m, v)


def _reduce_adamw_stacked(own, parts, place, triples, name):
    n = len(triples)
    _, r, c = triples[0][0].shape

    def body(place_ref, own_ref, p_ref, *refs):
        ins, outs = refs[:3 * n], refs[3 * n:]
        mine = place_ref[1]
        for i in range(n):
            rows = slice(i * r, (i + 1) * r)
            own_blk = own_ref[rows, :]
            g = jnp.where(mine == 0, own_blk, p_ref[0, rows, :].astype(F32))
            for j in range(1, N_CHIPS):
                g = g + jnp.where(mine == j, own_blk, p_ref[j, rows, :].astype(F32))
            d, nm, nv = _adamw(ins[3 * i][0], g, ins[3 * i + 1][0], ins[3 * i + 2][0])
            for k, val in enumerate((g, d, nm, nv)):
                outs[4 * i + k][0] = val

    whole = lambda shape: pl.BlockSpec(shape, lambda i, pr: (0,) * len(shape))
    grid_spec = pltpu.PrefetchScalarGridSpec(
        num_scalar_prefetch=1, grid=(1,),
        in_specs=[whole(own.shape), whole(parts.shape)] + [whole((1, r, c))] * (3 * n),
        out_specs=[whole((1, r, c))] * (4 * n))
    res = pl.pallas_call(
        body, name=name, grid_spec=grid_spec,
        out_shape=[jax.ShapeDtypeStruct((1, r, c), F32)] * (4 * n),
        compiler_params=_cparams(("arbitrary",)),
    )(place, own, parts, *[a for t3 in triples for a in t3])
    return [res[4 * i:4 * i + 4] for i in range(n)]


def _interleave_qkv(a):
    lead = a.shape[:-1]
    return a.reshape(lead + (3, HEAD_PAIRS, LANES)).swapaxes(-3, -2).reshape(lead + (3 * D_MODEL,))


def _deinterleave_qkv(a):
    lead = a.shape[:-1]
    return a.reshape(lead + (HEAD_PAIRS, 3, LANES)).swapaxes(-3, -2).reshape(lead + (3 * D_MODEL,))


def _interleave_rows(a):
    return a.reshape(3, HEAD_PAIRS, LANES, a.shape[1]).swapaxes(0, 1).reshape(a.shape)


def _deinterleave_rows(a):
    return a.reshape(HEAD_PAIRS, 3, LANES, a.shape[1]).swapaxes(0, 1).reshape(a.shape)


def _pack_small(pre, conv_b, rg_ba, rg_bx, lam, post, loss_row, b_in, conv_w_full, rg_wa, rg_wx):
    z = jnp.zeros((1, D_MODEL), F32)
    b_used = jnp.concatenate([b_in[:, 0:3 * D_MODEL], b_in[:, 3 * D_MODEL + HEADS:IN_TOTAL]], axis=1)
    b_f = jnp.pad(b_in[:, 3 * D_MODEL:3 * D_MODEL + HEADS], ((0, 0), (0, D_MODEL - HEADS)))
    return jnp.concatenate([
        pre, conv_b, rg_ba, rg_bx, lam, post, loss_row, z,
        b_used.reshape(9, D_MODEL), b_f, conv_w_full, z, z,
        rg_wa.reshape(64, D_MODEL), rg_wx.reshape(64, D_MODEL)], axis=0)


def _unpack_small(p):
    b_used = p[8:17].reshape(1, 9 * D_MODEL)
    b_in = jnp.concatenate([b_used[:, 0:3 * D_MODEL], p[17:18, 0:HEADS], b_used[:, 3 * D_MODEL:]], axis=1)
    return dict(pre_norm_w=p[0:1], conv_b=p[1:2], rg_ba=p[2:3], rg_bx=p[3:4], rg_lambda=p[4:5],
                post_norm_w=p[5:6], loss_row=p[6:7], b_in=b_in, conv_w_full=p[18:22],
                rg_wa=p[24:88].reshape(1, 16, 64, 64), rg_wx=p[88:152].reshape(1, 16, 64, 64))


def _reduce_small(parts, first, w, m, v, vectors):
    nvec = len(vectors)

    def body(p_ref, f_ref, w_ref, m_ref, v_ref, *refs):
        ins, outs = refs[:3 * nvec], refs[3 * nvec:]
        g = p_ref[0]
        g0 = f_ref[0, 0:1, :]
        for j in range(1, N_DEV):
            g = g + p_ref[j]
            g0 = g0 + f_ref[j, 0:1, :]
        d, nm, nv = _adamw(w_ref[...], g, m_ref[...], v_ref[...])
        for k, val in enumerate((g, d, nm, nv)):
            outs[k][...] = val
        for i in range(nvec):
            gi = g0 if i == 0 else g[i:i + 1, :]
            di, nmi, nvi = _adamw(ins[3 * i][...], gi, ins[3 * i + 1][...], ins[3 * i + 2][...])
            for k, val in enumerate((gi, di, nmi, nvi)):
                outs[4 + 4 * i + k][...] = val
        outs[-1][...] = jnp.zeros((8, LANES), F32) + (0.5 / D_MODEL) * jnp.sum(g[LOSS_ROW:LOSS_ROW + 1, :])

    sh = jax.ShapeDtypeStruct((SMALL_ROWS, D_MODEL), F32)
    vec = jax.ShapeDtypeStruct((1, D_MODEL), F32)
    res = pl.pallas_call(
        body, name="reduce_small",
        out_shape=[sh, sh, sh, sh] + [vec] * (4 * nvec) + [jax.ShapeDtypeStruct((8, LANES), F32)],
    )(parts, first, w, m, v, *[a for t3 in vectors for a in t3])
    return res[:4], [res[4 + 4 * i:8 + 4 * i] for i in range(nvec)], res[-1]


def kernel(x, pre_norm_w, w_in, b_in, conv_w, conv_b, rg_wa, rg_ba, rg_wx, rg_bx, rg_lambda, w_branch_a, w_branch_r, w_out, post_norm_w, loss_target, m_pre_norm_w, m_w_in, m_b_in, m_conv_w, m_conv_b, m_rg_wa, m_rg_ba, m_rg_wx, m_rg_bx, m_rg_lambda, m_w_branch_a, m_w_branch_r, m_w_out, m_post_norm_w, v_pre_norm_w, v_w_in, v_b_in, v_conv_w, v_conv_b, v_rg_wa, v_rg_ba, v_rg_wx, v_rg_bx, v_rg_lambda, v_w_branch_a, v_w_branch_r, v_w_out, v_post_norm_w):
    b, s, _ = x.shape
    t = b * s
    me = 4 * lax.axis_index("x") + 2 * lax.axis_index("y") + lax.axis_index("c")
    shard_rows = D_MODEL // N_DEV

    place = jnp.stack([lax.axis_index("c"), 2 * lax.axis_index("x") + lax.axis_index("y")]).astype(jnp.int32)
    w_in_all = _gather(w_in[0].T.astype(BF16), "gather_w_in")
    wt_full = w_in_all.reshape(IN_TOTAL, D_MODEL)
    conv_terms = jnp.concatenate(_split3(conv_w[0]), axis=0)
    conv_pad = jnp.pad(conv_terms, ((0, 16 - 3 * CONV_W), (0, D_MODEL - LANES)))
    sq_stack = jnp.concatenate([w_branch_a[0].astype(BF16), w_branch_r[0].astype(BF16), w_out[0].astype(BF16),
                                conv_pad], axis=0)
    sq_sems, sq_src, sq_land, sq_token = _gather_start(sq_stack, w_in_all, "gather_w_sq_start")

    w_qkv = _interleave_rows(wt_full[0:3 * D_MODEL])
    w_f = jnp.pad(wt_full[3 * D_MODEL:3 * D_MODEL + HEADS], ((0, LANES - HEADS), (0, 0)))
    w_rest = wt_full[3 * D_MODEL + HEADS:IN_USED]
    b_qkv = _interleave_qkv(b_in[:, 0:3 * D_MODEL]) + sq_token[0, 0]
    b_f = jnp.pad(b_in[:, 3 * D_MODEL:3 * D_MODEL + HEADS], ((0, 0), (0, LANES - HEADS)))
    b_rest = b_in[:, 3 * D_MODEL + HEADS:IN_USED]

    def blockdiag(w):
        w2 = w.reshape(N_CBLK, 2, HEAD_DIM, HEAD_DIM)
        zz = jnp.zeros((N_CBLK, HEAD_DIM, HEAD_DIM), w.dtype)
        top = jnp.concatenate([w2[:, 0], zz], axis=2)
        bot = jnp.concatenate([zz, w2[:, 1]], axis=2)
        return jnp.concatenate([top, bot], axis=1).astype(BF16)

    bda, bdx = blockdiag(rg_wa[0]), blockdiag(rg_wx[0])

    x2 = x.reshape(t, D_MODEL)
    tgt2 = loss_target.reshape(t, D_MODEL)
    h, qkv, zf = _prenorm_inproj(x2, pre_norm_w, w_qkv, b_qkv, w_f, b_f)
    zrest = _mm_bias(h, w_rest, b_rest, BF16, "inproj_rest")
    qkv3 = qkv.reshape(b, s, 3 * D_MODEL)
    zrest3 = zrest.reshape(b, s, 5 * D_MODEL)
    zf3 = zf.reshape(b, s, LANES)
    cexp3, crow = _fgate_fwd(zf3)
    yatt3, lse, ga = _attn_fwd(qkv3, cexp3, crow, zrest3)

    sq_all = _gather_wait(sq_sems, sq_src, sq_land, ga, "gather_w_sq_wait")
    sq_all = lax.dynamic_update_slice(sq_all, sq_stack[None], (me, 0, 0))
    wa = sq_all[:, 0:shard_rows].reshape(D_MODEL, D_MODEL)
    wr = sq_all[:, shard_rows:2 * shard_rows].reshape(D_MODEL, D_MODEL)
    wo = sq_all[:, 2 * shard_rows:3 * shard_rows].reshape(D_MODEL, D_MODEL)
    conv_all = sq_all[:, 3 * shard_rows:3 * shard_rows + 3 * CONV_W, 0:LANES].astype(F32)
    conv_all = (conv_all[:, 0:CONV_W] + conv_all[:, CONV_W:2 * CONV_W]) + conv_all[:, 2 * CONV_W:3 * CONV_W]
    conv_full = conv_all.transpose(1, 0, 2).reshape(CONV_W, D_MODEL)

    ylru3, gr = _rnn_fwd(zrest3, conv_full, conv_b, bda, bdx, rg_ba, rg_bx, rg_lambda)
    ya, yr, mm = _branch_merge(ga, gr, wa, wr, zrest)
    dy, do, acc_out = _out_loss(mm, wo, x2, tgt2, post_norm_w)

    dya, dyr, dz_mga, dz_mgr = _merge_bwd(do, wo, zrest, ya, yr)
    dyatt, dz_ga, dylru, dz_gr = _branch_bwd(dya, dyr, wa, wr, zrest, yatt3.reshape(t, D_MODEL),
                                             ylru3.reshape(t, D_MODEL))
    dz_xr, pvec, dbd = _rnn_bwd(zrest3, ylru3, dylru.reshape(b, s, D_MODEL), conv_full, conv_b, bda, bdx,
                                rg_ba, rg_bx, rg_lambda)
    dz_qkv, dc3 = _attn_bwd(qkv3, dyatt.reshape(b, s, D_MODEL), yatt3, lse, crow, cexp3)
    dz_f = _fgate_bwd(dc3, zf3)

    dw_qkv, db_qkv = _mm_tn(dz_qkv, h, "dw_qkv")
    dw_f, db_f = _mm_tn(dz_f, h, "dw_f")
    dw_parts, db_parts = [], []
    for nm, dzp in (("ga", dz_ga), ("xr", dz_xr), ("gr", dz_gr), ("mga", dz_mga), ("mgr", dz_mgr)):
        dwp, dbp = _mm_tn(dzp, h, "dw_" + nm)
        dw_parts.append(dwp)
        db_parts.append(dbp[0:1])

    zeros_tail = jnp.zeros((IN_TOTAL - IN_USED, D_MODEL), F32)
    dwt_full = jnp.concatenate([_deinterleave_rows(dw_qkv), dw_f[0:HEADS]] + dw_parts + [zeros_tail], axis=0)
    dw_in_send = dwt_full.reshape(N_CHIPS, 2, W_SHARD, D_MODEL).transpose(1, 0, 2, 3)
    swp_sems, dw_in_src, swp_land, swp_token = _swap_start(dw_in_send, db_f, "swap_dw_in_start")
    dw_a, _ = _mm_tn(ga, dya, "dw_a", after=swp_token)
    dw_r, _ = _mm_tn(gr, dyr, "dw_r", after=swp_token)
    dw_o, _ = _mm_tn(mm, do, "dw_o", after=swp_token)
    dw_in_send, sib_in = _swap_wait(swp_sems, dw_in_src, swp_land, dw_o, "swap_dw_in_wait")
    by_dest = lambda a: a.reshape(N_CHIPS, 2, shard_rows, D_MODEL).transpose(1, 0, 2, 3)
    dw_sq_send = jnp.concatenate([by_dest(dw_a), by_dest(dw_r), by_dest(dw_o)], axis=2)

    db_in_full = jnp.concatenate([_deinterleave_qkv(db_qkv[0:1]), db_f[0:1, 0:HEADS]] + db_parts
                                 + [jnp.zeros((1, IN_TOTAL - IN_USED), F32)], axis=1)
    d_rg_wa = jnp.stack([dbd[:, 0, 0:HEAD_DIM, 0:HEAD_DIM], dbd[:, 0, HEAD_DIM:, HEAD_DIM:]], axis=1)
    d_rg_wx = jnp.stack([dbd[:, 1, 0:HEAD_DIM, 0:HEAD_DIM], dbd[:, 1, HEAD_DIM:, HEAD_DIM:]], axis=1)
    small_g = _pack_small(jnp.zeros((1, D_MODEL), F32), pvec[4:5], pvec[5:6], pvec[6:7], pvec[7:8], acc_out[0:1],
                          acc_out[1:2], db_in_full, pvec[0:4], d_rg_wa, d_rg_wx)
    sm_sems, sm_src, sm_land, sm_token = _gather_start(small_g, dw_o, "gather_small_start")

    sqs_sems, dw_sq_src, sqs_land, _ = _swap_start(dw_sq_send, sm_token, "swap_dw_sq_start")
    chip_in, own_in = _pair_add(dw_in_send, sib_in, place, "pair_add_in")
    dw_sq_send, sib_sq = _swap_wait(sqs_sems, dw_sq_src, sqs_land, chip_in, "swap_dw_sq_wait")
    chip_sq, own_sq = _pair_add(dw_sq_send, sib_sq, place, "pair_add_sq")
    sems, sent, lands, token = _exchange_chips_start([chip_in, chip_sq], "exchange_dw_start")

    wt = lambda lo: w_rest[lo * D_MODEL:(lo + 1) * D_MODEL]
    grad_x2, acc_pre = _dh_final(
        [(dz_qkv, w_qkv), (dz_f, w_f), (dz_ga, wt(0)), (dz_xr, wt(1)), (dz_gr, wt(2)), (dz_mga, wt(3)),
         (dz_mgr, wt(4))], token, x2, dy, pre_norm_w)
    pre_sems, pre_src, pre_land, pre_token = _gather_start(acc_pre, grad_x2, "gather_pre_start")
    recv_in, recv_sq = _exchange_chips_wait(sems, sent, lands, pre_token, "exchange_dw_wait")

    g_in, d_in, nm_in, nv_in = [a.T for a in _reduce_adamw(
        own_in, recv_in, place, w_in[0].T, m_w_in[0].T, v_w_in[0].T, "adamw_w_in")]
    sq_out = _reduce_adamw_stacked(
        own_sq, recv_sq, place,
        [(w_branch_a, m_w_branch_a, v_w_branch_a), (w_branch_r, m_w_branch_r, v_w_branch_r),
         (w_out, m_w_out, v_w_out)], "adamw_w_sq")
    pre_all = _gather_wait(pre_sems, pre_src, pre_land, sq_out[2][1], "gather_pre_wait")
    pre_all = lax.dynamic_update_slice(pre_all, acc_pre[None], (me, 0, 0))
    small_all = _gather_wait(sm_sems, sm_src, sm_land, pre_all, "gather_small_wait")
    small_all = lax.dynamic_update_slice(small_all, small_g[None], (me, 0, 0))

    def place_conv(a):
        return lax.dynamic_update_slice(jnp.zeros((CONV_W, D_MODEL), F32), a[0], (0, me * LANES))

    zrow = jnp.zeros((1, D_MODEL), F32)
    vector_names = ["pre_norm_w", "conv_b", "rg_ba", "rg_bx", "rg_lambda", "post_norm_w"]
    vectors = [(pre_norm_w, m_pre_norm_w, v_pre_norm_w), (conv_b, m_conv_b, v_conv_b), (rg_ba, m_rg_ba, v_rg_ba),
               (rg_bx, m_rg_bx, v_rg_bx), (rg_lambda, m_rg_lambda, v_rg_lambda),
               (post_norm_w, m_post_norm_w, v_post_norm_w)]
    small_w = _pack_small(zrow, zrow, zrow, zrow, zrow, zrow, zrow, b_in, place_conv(conv_w), rg_wa[0], rg_wx[0])
    small_m = _pack_small(zrow, zrow, zrow, zrow, zrow, zrow, zrow, m_b_in, place_conv(m_conv_w), m_rg_wa[0],
                          m_rg_wx[0])
    small_v = _pack_small(zrow, zrow, zrow, zrow, zrow, zrow, zrow, v_b_in, place_conv(v_conv_w), v_rg_wa[0],
                          v_rg_wx[0])
    packed, vector_out, loss_tile = _reduce_small(small_all, pre_all, small_w, small_m, small_v, vectors)
    outs_small = [_unpack_small(p) for p in packed]
    loss = loss_tile[0, 0]

    def leaf(kind, name):
        if name == "w_in":
            return (g_in, d_in, nm_in, nv_in)[kind][None]
        if name in ("w_branch_a", "w_branch_r", "w_out"):
            return sq_out[("w_branch_a", "w_branch_r", "w_out").index(name)][kind]
        if name == "conv_w":
            return lax.dynamic_slice(outs_small[kind]["conv_w_full"], (0, me * LANES), (CONV_W, LANES))[None]
        if name in vector_names:
            return vector_out[vector_names.index(name)][kind]
        return outs_small[kind][name]

    names = ["pre_norm_w", "w_in", "b_in", "conv_w", "conv_b", "rg_wa", "rg_ba", "rg_wx", "rg_bx", "rg_lambda",
             "w_branch_a", "w_branch_r", "w_out", "post_norm_w"]
    out = [loss, grad_x2.reshape(b, s, D_MODEL)]
    for kind in range(4):
        out += [leaf(kind, nm) for nm in names]
    return tuple(out)
```

```python
import jax
import jax.numpy as jnp
from jax import lax
from jax.experimental import pallas as pl
from jax.experimental.pallas import tpu as pltpu

F32 = jnp.float32
BF16 = jnp.bfloat16

N_DEV = 8
D_MODEL = 1024
HEADS = 16
HEAD_DIM = 64
HEAD_PAIRS = HEADS // 2
LANES = 128
N_CBLK = D_MODEL // LANES
CONV_W = 4
RG_C = 8.0
NORM_EPS = 1e-6
MASK_VALUE = -1e30
IN_USED = 8208
IN_TOTAL = 9232
W_SHARD = IN_TOTAL // N_DEV

ADAM_LR = 0.001
ADAM_B1 = 0.9
ADAM_B2 = 0.999
ADAM_EPS = 1e-08
ADAM_WD = 0.01
ADAM_STEP = 10

ATT_TILE_FWD = 256
ATT_TILE_BWD = 512
SCAN_TILE = 256
SMALL_ROWS = 152
LOSS_ROW = 6


def _cparams(sem=None, vmem_mb=None):
    kw = {}
    if sem is not None:
        kw["dimension_semantics"] = sem
    if vmem_mb is not None:
        kw["vmem_limit_bytes"] = vmem_mb * 1024 * 1024
    return pltpu.CompilerParams(**kw)


def _sigmoid(x):
    return 1.0 / (1.0 + jnp.exp(-x))


def _softplus(x):
    return jnp.maximum(x, 0.0) + jnp.log1p(jnp.exp(-jnp.abs(x)))


def _one_minus_exp(y, exp_y):
    series = -y * (1.0 + y * (1.0 / 2 + y * (1.0 / 6 + y * (1.0 / 24 + y * (1.0 / 120)))))
    return jnp.where(y > -0.0625, series, 1.0 - exp_y)


def _split3(x):
    hi = x.astype(BF16)
    r1 = x - hi.astype(F32)
    mid = r1.astype(BF16)
    lo = (r1 - mid.astype(F32)).astype(BF16)
    return hi, mid, lo


def _dot(a, b):
    return jnp.dot(a, b, preferred_element_type=F32)


def _dot_nt(a, b):
    return lax.dot_general(a, b, (((1,), (1,)), ((), ())), preferred_element_type=F32)


def _dot_tn(a, b):
    return lax.dot_general(a, b, (((0,), (0,)), ((), ())), preferred_element_type=F32)


def _iota(shape, dim):
    return lax.broadcasted_iota(jnp.int32, shape, dim)


_ANY = pl.BlockSpec(memory_space=pl.ANY)
_MESH = pl.DeviceIdType.MESH
N_CHIPS = 4


def _place():
    x, y, c = lax.axis_index("x"), lax.axis_index("y"), lax.axis_index("c")
    other_chips = [(1 - x, y), (x, 1 - y), (1 - x, 1 - y)]
    return x, y, c, other_chips


def _gather(x_shard, name):
    def body(x_ref, out_ref, send_sems, recv_sems, local_sem):
        x, y, c, chips = _place()
        me, sibling = (x, y, c), (x, y, 1 - c)

        def slot(p):
            return out_ref.at[4 * p[0] + 2 * p[1] + p[2]]

        def copy(k, block, to, src=None):
            return pltpu.make_async_remote_copy(
                src_ref=slot(block) if src is None else src, dst_ref=slot(block),
                send_sem=send_sems.at[k], recv_sem=recv_sems.at[k], device_id=to, device_id_type=_MESH)

        mine = pltpu.make_async_copy(x_ref, slot(me), local_sem)
        mine.start()
        first = [copy(0, me, sibling, src=x_ref)]
        first += [copy(1 + j, me, (*chip, c), src=x_ref) for j, chip in enumerate(chips)]
        for cp in first:
            cp.start()
        passed = [copy(4 + j, (*chip, c), sibling) for j, chip in enumerate(chips)]
        for j, chip in enumerate(chips):
            copy(1 + j, (*chip, c), me).wait_recv()
            passed[j].start()
        copy(0, sibling, me).wait_recv()
        for j, chip in enumerate(chips):
            copy(4 + j, (*chip, 1 - c), me).wait_recv()
        for cp in first + passed:
            cp.wait_send()
        mine.wait()

    return pl.pallas_call(
        body, name=name,
        out_shape=jax.ShapeDtypeStruct((N_DEV,) + tuple(x_shard.shape), x_shard.dtype),
        in_specs=[_ANY], out_specs=_ANY,
        scratch_shapes=[pltpu.SemaphoreType.DMA((7,)), pltpu.SemaphoreType.DMA((7,)), pltpu.SemaphoreType.DMA],
    )(x_shard)


def _blocks_2d(r, c):
    if r % 128 == 0:
        return (128, c), r // 128, lambda i: (i, 0)
    return (r, 256), c // 256, lambda i: (0, i)


def _pair_add(src, recv, place, name, after=None):
    _, _, r, c = src.shape
    blk, nblk, at = _blocks_2d(r, c)
    deps = [] if after is None else [after]

    def body(place_ref, a_ref, b_ref, *refs):
        q16_ref, own_ref = refs[len(deps):]
        q = a_ref[...] + b_ref[...]
        q16_ref[...] = q.astype(BF16)

        @pl.when(pl.program_id(1) == place_ref[1])
        def _():
            own_ref[...] = q

    grid_spec = pltpu.PrefetchScalarGridSpec(
        num_scalar_prefetch=1, grid=(nblk, N_CHIPS),
        in_specs=[pl.BlockSpec((None, None) + blk, lambda i, j, pr: (pr[0], j) + at(i)),
                  pl.BlockSpec((None,) + blk, lambda i, j, pr: (j,) + at(i))]
        + [pl.BlockSpec(d.shape, lambda i, j, pr: (0, 0)) for d in deps],
        out_specs=[pl.BlockSpec((None,) + blk, lambda i, j, pr: (j,) + at(i)),
                   pl.BlockSpec(blk, lambda i, j, pr: at(i))])
    return pl.pallas_call(
        body, name=name, grid_spec=grid_spec,
        out_shape=[jax.ShapeDtypeStruct((N_CHIPS, r, c), BF16), jax.ShapeDtypeStruct((r, c), F32)],
        compiler_params=_cparams(("parallel", "arbitrary")),
    )(place, src, recv, *deps)


_HBM = pl.BlockSpec(memory_space=pltpu.HBM)
_SEM = pl.BlockSpec(memory_space=pltpu.SEMAPHORE)
_DATAFLOW = pltpu.SideEffectType.DATAFLOW_SIDE_EFFECTING


def _chip_copy(src_ref, land_ref, send_sem, recv_sem, k, chips, c, land):
    chip = chips[k]
    return pltpu.make_async_remote_copy(
        src_ref=src_ref.at[2 * chip[0] + chip[1]], dst_ref=land_ref.at[land],
        send_sem=send_sem, recv_sem=recv_sem, device_id=(*chip, c), device_id_type=_MESH)


def _exchange_chips_start(srcs, name):
    n = len(srcs)
    ncp = 3 * n

    def body(*refs):
        src_refs, land_refs = refs[:n], refs[n:2 * n]
        sems = refs[4 * n:4 * n + 2 * ncp]
        token = refs[-1]
        x, y, c, chips = _place()
        for i in range(n):
            for k in range(3):
                j = 3 * i + k
                _chip_copy(src_refs[i], land_refs[i], sems[j], sems[ncp + j], k, chips, c, 2 * x + y).start()
        token[...] = jnp.zeros_like(token)

    hbm = [pltpu.HBM(a.shape, a.dtype) for a in srcs]
    lands = [pltpu.with_memory_space_constraint(lax.empty(a.shape, a.dtype), pltpu.HBM) for a in srcs]
    res = pl.pallas_call(
        body, name=name,
        out_shape=(*hbm, *hbm, *([pltpu.SemaphoreType.DMA(())] * (2 * ncp)), jax.ShapeDtypeStruct((8, LANES), F32)),
        in_specs=[_HBM] * (2 * n),
        out_specs=(*([_HBM] * (2 * n)), *([_SEM] * (2 * ncp)), pl.BlockSpec(memory_space=pltpu.VMEM)),
        input_output_aliases={i: i for i in range(2 * n)},
        compiler_params=pltpu.CompilerParams(has_side_effects=_DATAFLOW),
    )(*[pltpu.with_memory_space_constraint(a, pltpu.HBM) for a in srcs], *lands)
    return list(res[2 * n:2 * n + 2 * ncp]), list(res[:n]), list(res[n:2 * n]), res[-1]


def _exchange_chips_wait(sems, srcs, lands, after, name):
    n = len(srcs)
    ncp = 3 * n

    def body(*refs):
        src_refs, land_refs = refs[:n], refs[n:2 * n]
        sem_refs = refs[2 * n:2 * n + 2 * ncp]
        x, y, c, chips = _place()
        for i in range(n):
            for k in range(3):
                j = 3 * i + k
                cp = _chip_copy(src_refs[i], land_refs[i], sem_refs[j], sem_refs[ncp + j], k, chips, c,
                                2 * chips[k][0] + chips[k][1])
                cp.wait_send()
                cp.wait_recv()

    hbm = [pltpu.HBM(a.shape, a.dtype) for a in srcs]
    res = pl.pallas_call(
        body, name=name, out_shape=(*hbm, *hbm),
        in_specs=[_HBM] * (2 * n) + [_SEM] * (2 * ncp) + [_ANY], out_specs=tuple([_HBM] * (2 * n)),
        input_output_aliases={i: i for i in range(2 * n)},
        compiler_params=pltpu.CompilerParams(has_side_effects=_DATAFLOW),
    )(*srcs, *lands, *sems, after)
    return list(res[n:2 * n])


def _swap_start(src, after, name):
    def body(src_ref, land_ref, after_ref, src_thru, land_thru, send_sem, recv_sem, token):
        x, y, c, _ = _place()
        pltpu.make_async_remote_copy(src_ref=src_ref.at[1 - c], dst_ref=land_ref, send_sem=send_sem,
                                     recv_sem=recv_sem, device_id=(x, y, 1 - c), device_id_type=_MESH).start()
        token[...] = jnp.zeros_like(token)

    land = pltpu.with_memory_space_constraint(lax.empty(src.shape[1:], src.dtype), pltpu.HBM)
    res = pl.pallas_call(
        body, name=name,
        out_shape=(pltpu.HBM(src.shape, src.dtype), pltpu.HBM(land.shape, land.dtype),
                   pltpu.SemaphoreType.DMA(()), pltpu.SemaphoreType.DMA(()), jax.ShapeDtypeStruct((8, LANES), F32)),
        in_specs=[_HBM, _HBM, _ANY],
        out_specs=(_HBM, _HBM, _SEM, _SEM, pl.BlockSpec(memory_space=pltpu.VMEM)),
        input_output_aliases={0: 0, 1: 1},
        compiler_params=pltpu.CompilerParams(has_side_effects=_DATAFLOW),
    )(pltpu.with_memory_space_constraint(src, pltpu.HBM), land, after)
    return [res[2], res[3]], res[0], res[1], res[-1]


def _swap_wait(sems, src, land, after, name):
    def body(src_ref, land_ref, send_sem, recv_sem, after_ref, src_out, land_out):
        x, y, c, _ = _place()
        cp = pltpu.make_async_remote_copy(src_ref=src_ref.at[1 - c], dst_ref=land_ref, send_sem=send_sem,
                                          recv_sem=recv_sem, device_id=(x, y, 1 - c), device_id_type=_MESH)
        cp.wait_send()
        cp.wait_recv()

    res = pl.pallas_call(
        body, name=name, out_shape=(pltpu.HBM(src.shape, src.dtype), pltpu.HBM(land.shape, land.dtype)),
        in_specs=[_HBM, _HBM, _SEM, _SEM, _ANY], out_specs=(_HBM, _HBM),
        input_output_aliases={0: 0, 1: 1},
        compiler_params=pltpu.CompilerParams(has_side_effects=_DATAFLOW),
    )(src, land, *sems, after)
    return res[0], res[1]


def _peer_copy(src_ref, land_ref, send_sem, recv_sem, k, place, land):
    x, y, c = place
    peer = (1 - x if k & 4 else x, 1 - y if k & 2 else y, 1 - c if k & 1 else c)
    return pltpu.make_async_remote_copy(
        src_ref=src_ref, dst_ref=land_ref.at[land], send_sem=send_sem, recv_sem=recv_sem,
        device_id=peer, device_id_type=_MESH)


def _gather_start(x_shard, after, name):
    npeer = N_DEV - 1

    def body(x_ref, land_ref, after_ref, x_thru, land_thru, *rest):
        sems, token = rest[:2 * npeer], rest[-1]
        x, y, c, _ = _place()
        for k in range(1, N_DEV):
            _peer_copy(x_ref, land_ref, sems[k - 1], sems[npeer + k - 1], k, (x, y, c), 4 * x + 2 * y + c).start()
        token[...] = jnp.zeros_like(token)

    land = pltpu.with_memory_space_constraint(lax.empty((N_DEV,) + tuple(x_shard.shape), x_shard.dtype), pltpu.HBM)
    res = pl.pallas_call(
        body, name=name,
        out_shape=(pltpu.HBM(x_shard.shape, x_shard.dtype), pltpu.HBM(land.shape, land.dtype),
                   *([pltpu.SemaphoreType.DMA(())] * (2 * npeer)), jax.ShapeDtypeStruct((8, LANES), F32)),
        in_specs=[_HBM, _HBM, _ANY],
        out_specs=(_HBM, _HBM, *([_SEM] * (2 * npeer)), pl.BlockSpec(memory_space=pltpu.VMEM)),
        input_output_aliases={0: 0, 1: 1},
        compiler_params=pltpu.CompilerParams(has_side_effects=_DATAFLOW),
    )(pltpu.with_memory_space_constraint(x_shard, pltpu.HBM), land, after)
    return list(res[2:2 + 2 * npeer]), res[0], res[1], res[-1]


def _gather_wait(sems, src, land, after, name):
    npeer = N_DEV - 1

    def body(x_ref, land_ref, *rest):
        sem_refs = rest[:2 * npeer]
        x, y, c, _ = _place()
        for k in range(1, N_DEV):
            peer_index = (4 * x + 2 * y + c) ^ k
            cp = _peer_copy(x_ref, land_ref, sem_refs[k - 1], sem_refs[npeer + k - 1], k, (x, y, c), peer_index)
            cp.wait_send()
            cp.wait_recv()

    res = pl.pallas_call(
        body, name=name, out_shape=(pltpu.HBM(src.shape, src.dtype), pltpu.HBM(land.shape, land.dtype)),
        in_specs=[_HBM, _HBM] + [_SEM] * (2 * npeer) + [_ANY], out_specs=(_HBM, _HBM),
        input_output_aliases={0: 0, 1: 1},
        compiler_params=pltpu.CompilerParams(has_side_effects=_DATAFLOW),
    )(src, land, *sems, after)
    return res[1]


def _prenorm_inproj(x2, w, wt_qkv, b_qkv, wt_f, b_f):
    t = x2.shape[0]
    tm = min(512, t)
    n = wt_qkv.shape[0]
    tn = D_MODEL

    def body(x_ref, w_ref, wq_ref, bq_ref, wf_ref, bf_ref, h_ref, qkv_ref, zf_ref):
        x = x_ref[...]
        r = lax.rsqrt(jnp.mean(x * x, axis=-1, keepdims=True) + NORM_EPS)
        h = (x * r * w_ref[...]).astype(BF16)
        h_ref[...] = h
        for j in range(n // tn):
            cols = slice(j * tn, (j + 1) * tn)
            qkv_ref[:, cols] = (_dot_nt(h, wq_ref[cols, :]) + bq_ref[:, cols]).astype(BF16)
        zf_ref[...] = _dot_nt(h, wf_ref[...]) + bf_ref[...]

    row = lambda c: pl.BlockSpec((tm, c), lambda i: (i, 0))
    whole = lambda a: pl.BlockSpec(a.shape, lambda i: (0, 0))
    return pl.pallas_call(
        body, name="prenorm_inproj_qkv", grid=(t // tm,),
        in_specs=[row(D_MODEL), whole(w), whole(wt_qkv), whole(b_qkv), whole(wt_f), whole(b_f)],
        out_specs=[row(D_MODEL), row(n), row(LANES)],
        out_shape=[jax.ShapeDtypeStruct((t, D_MODEL), BF16), jax.ShapeDtypeStruct((t, n), BF16),
                   jax.ShapeDtypeStruct((t, LANES), F32)],
        compiler_params=_cparams(("parallel",), vmem_mb=48),
    )(x2, w, wt_qkv, b_qkv, wt_f, b_f)


def _mm_bias(a, bt, bias, out_dtype, name):
    m, k = a.shape
    n = bt.shape[0]
    tm = min(1024, m)
    tn = min(1024, n)

    def body(a_ref, bt_ref, bias_ref, o_ref):
        aa = a_ref[...]
        for j in range(n // tn):
            cols = slice(j * tn, (j + 1) * tn)
            o_ref[:, cols] = (_dot_nt(aa, bt_ref[cols, :]) + bias_ref[:, cols]).astype(o_ref.dtype)

    return pl.pallas_call(
        body, name=name, grid=(m // tm,),
        in_specs=[pl.BlockSpec((tm, k), lambda i: (i, 0)),
                  pl.BlockSpec((n, k), lambda i: (0, 0), pipeline_mode=pl.Buffered(1)),
                  pl.BlockSpec((1, n), lambda i: (0, 0))],
        out_specs=pl.BlockSpec((tm, n), lambda i: (i, 0)),
        out_shape=jax.ShapeDtypeStruct((m, n), out_dtype),
        compiler_params=_cparams(("parallel",), vmem_mb=48),
    )(a, bt, bias)


def _mm_tn(a, b, name, after=None):
    t, m = a.shape
    n = b.shape[1]
    tm = min(1024, m)
    tk = min(2048, t)
    deps = [] if after is None else [after]

    def body(a_ref, b_ref, *refs):
        o_ref, s_ref = refs[len(deps):]
        kk = pl.program_id(1)

        @pl.when(kk == 0)
        def _():
            o_ref[...] = jnp.zeros_like(o_ref)
            s_ref[...] = jnp.zeros_like(s_ref)

        aa = a_ref[...]
        o_ref[...] += _dot_tn(aa, b_ref[...])
        s_ref[0:1, :] += jnp.sum(aa.astype(F32), axis=0, keepdims=True)

    return pl.pallas_call(
        body, name=name, grid=(m // tm, t // tk),
        in_specs=[pl.BlockSpec((tk, tm), lambda i, kk: (kk, i)), pl.BlockSpec((tk, n), lambda i, kk: (kk, 0))]
        + [pl.BlockSpec(d.shape, lambda i, kk: (0, 0)) for d in deps],
        out_specs=[pl.BlockSpec((tm, n), lambda i, kk: (i, 0)), pl.BlockSpec((8, tm), lambda i, kk: (0, i))],
        out_shape=[jax.ShapeDtypeStruct((m, n), F32), jax.ShapeDtypeStruct((8, m), F32)],
        compiler_params=_cparams(("parallel", "arbitrary"), vmem_mb=48),
    )(a, b, *deps)


def _fgate_fwd(zf3):
    b, s, _ = zf3.shape
    tb = SCAN_TILE
    nb = s // tb

    def body(z_ref, cexp_ref, crow_ref):
        tri = (_iota((tb, tb), 1) <= _iota((tb, tb), 0)).astype(BF16)
        expand = ((_iota((LANES, D_MODEL), 1) >> 6) == _iota((LANES, D_MODEL), 0)).astype(BF16)
        carry = jnp.zeros((1, LANES), F32)
        for i in range(nb):
            rows = slice(i * tb, (i + 1) * tb)
            z = z_ref[rows, :]
            lf = jnp.minimum(z, 0.0) - jnp.log1p(jnp.exp(-jnp.abs(z)))
            cb = sum(_dot(tri, part) for part in _split3(lf)) + carry
            carry = cb[tb - 1:tb, :]
            cexp_ref[rows, :] = sum(_dot(part, expand) for part in _split3(cb))
            crow_ref[:, rows] = cb.T[0:HEADS, :]

    return pl.pallas_call(
        body, name="fgate_fwd", grid=(b,),
        in_specs=[pl.BlockSpec((None, s, LANES), lambda i: (i, 0, 0))],
        out_specs=[pl.BlockSpec((None, s, D_MODEL), lambda i: (i, 0, 0)),
                   pl.BlockSpec((None, HEADS, s), lambda i: (i, 0, 0))],
        out_shape=[jax.ShapeDtypeStruct((b, s, D_MODEL), F32), jax.ShapeDtypeStruct((b, HEADS, s), F32)],
        compiler_params=_cparams(("parallel",)),
    )(zf3)


def _fgate_bwd(dc3, zf3):
    b, s, _ = zf3.shape
    tb = SCAN_TILE
    nb = s // tb

    def body(dc_ref, z_ref, o_ref):
        tri = (_iota((tb, tb), 1) >= _iota((tb, tb), 0)).astype(BF16)
        carry = jnp.zeros((1, LANES), F32)
        for i in reversed(range(nb)):
            rows = slice(i * tb, (i + 1) * tb)
            dlf = sum(_dot(tri, part) for part in _split3(dc_ref[rows, :])) + carry
            carry = dlf[0:1, :]
            o_ref[rows, :] = (dlf * _sigmoid(-z_ref[rows, :])).astype(BF16)

    return pl.pallas_call(
        body, name="fgate_bwd", grid=(b,),
        in_specs=[pl.BlockSpec((None, s, LANES), lambda i: (i, 0, 0)),
                  pl.BlockSpec((None, s, LANES), lambda i: (i, 0, 0))],
        out_specs=pl.BlockSpec((s, LANES), lambda i: (i, 0)),
        out_shape=jax.ShapeDtypeStruct((b * s, LANES), BF16),
        compiler_params=_cparams(("parallel",)),
    )(dc3, zf3)


def _spare(hh):
    return HEAD_DIM if hh == 0 else 0


def _put_cols(tile, mine, cols, first):
    lane = _iota((1, LANES), 1)
    out = jnp.where(mine, tile, jnp.zeros((), tile.dtype))
    for j, c in enumerate(cols):
        out = jnp.where(lane == first + j, c, out)
    return out


def _put_rows(tile, mine, rows, first):
    sub = _iota((LANES, 1), 0)
    out = jnp.where(mine, tile, jnp.zeros((), tile.dtype))
    for j, r in enumerate(rows):
        out = jnp.where(sub == first + j, r, out)
    return out


def _transpose_bf16(a):
    return a.astype(F32).T.astype(BF16)


def _attn_fwd(qkv3, cexp3, crow, zrest3):
    b, s, _ = qkv3.shape
    ta = ATT_TILE_FWD
    nq = s // ta
    hd = HEAD_DIM
    crow5 = crow.reshape(b, HEAD_PAIRS, 2, nq, ta)

    def body(qkv_ref, cq_ref, ck_ref, g_ref, y_ref, lse_ref, ga_ref, kt_scr, v_scr):
        lane = _iota((1, LANES), 1)
        sub = _iota((LANES, 1), 0)
        lane_mine = (lane < hd, lane >= hd)
        sub_mine = (sub < hd, sub >= hd)
        causal = _iota((ta, ta), 0) >= _iota((ta, ta), 1)
        one = jnp.ones((), BF16)

        for kj in range(nq):
            rows = slice(kj * ta, (kj + 1) * ta)
            kt = _transpose_bf16(qkv_ref[rows, LANES:2 * LANES])
            v = qkv_ref[rows, 2 * LANES:3 * LANES]
            for hh in range(2):
                ck = list(_split3(-ck_ref[hh, kj:kj + 1, :]))
                kt_scr[hh, kj] = _put_rows(kt, sub_mine[hh], [one, one, one] + ck, _spare(hh))
                v_scr[hh, kj] = _put_cols(v, lane_mine[hh], [one], _spare(hh))

        for qi in range(nq):
            rows = slice(qi * ta, (qi + 1) * ta)
            q = qkv_ref[rows, 0:LANES] * 0.125
            cq = cq_ref[rows, :]
            qh = [_put_cols(q, lane_mine[hh], list(_split3(cq[:, hh * hd:hh * hd + 1])) + [one, one, one], _spare(hh))
                  for hh in range(2)]
            st = [(jnp.full((ta, 1), MASK_VALUE, F32), jnp.zeros((ta, LANES), F32))] * 2
            for kj in range(qi + 1):
                for hh in range(2):
                    m, acc = st[hh]
                    sc = _dot(qh[hh], kt_scr[hh, kj])
                    if kj == qi:
                        sc = jnp.where(causal, sc, MASK_VALUE)
                    mn = jnp.maximum(m, jnp.max(sc, axis=-1, keepdims=True))
                    p = jnp.exp(sc - mn).astype(BF16)
                    st[hh] = (mn, jnp.exp(m - mn) * acc + _dot(p, v_scr[hh, kj]))
            (ma, acca), (mb, accb) = st
            la = acca[:, hd:hd + 1]
            lb = accb[:, 0:1]
            y = jnp.where(lane_mine[0], acca * (1.0 / la), accb * (1.0 / lb))
            lse = jnp.where(lane_mine[0], ma + jnp.log(la), mb + jnp.log(lb)).T
            lse_ref[0, qi:qi + 1, :] = lse[0:1, :]
            lse_ref[1, qi:qi + 1, :] = lse[hd:hd + 1, :]
            y_ref[rows, :] = y
            g = g_ref[rows, :].astype(F32)
            ga_ref[rows, :] = (y * (g * _sigmoid(g))).astype(BF16)

    blk = lambda w: pl.BlockSpec((None, s, w), lambda i, p: (i, 0, p))
    rows5 = pl.BlockSpec((None, None, 2, nq, ta), lambda i, p: (i, p, 0, 0, 0))
    yatt3, lse5, ga = pl.pallas_call(
        body, name="attn_fwd", grid=(b, HEAD_PAIRS),
        in_specs=[blk(3 * LANES), blk(LANES), rows5, blk(LANES)],
        out_specs=[blk(LANES), rows5, pl.BlockSpec((s, LANES), lambda i, p: (i, p))],
        out_shape=[jax.ShapeDtypeStruct((b, s, D_MODEL), F32),
                   jax.ShapeDtypeStruct((b, HEAD_PAIRS, 2, nq, ta), F32),
                   jax.ShapeDtypeStruct((b * s, D_MODEL), BF16)],
        scratch_shapes=[pltpu.VMEM((2, nq, LANES, ta), BF16), pltpu.VMEM((2, nq, ta, LANES), BF16)],
        compiler_params=_cparams(("parallel", "parallel")),
    )(qkv3, cexp3, crow5, zrest3)
    return yatt3, lse5.reshape(b, HEADS, s), ga


def _attn_bwd(qkv3, do3, y3, lse, crow, cexp3):
    b, s, _ = qkv3.shape
    ta = ATT_TILE_BWD
    nq = s // ta
    hd = HEAD_DIM
    lse5 = lse.reshape(b, HEAD_PAIRS, 2, nq, ta)
    crow5 = crow.reshape(b, HEAD_PAIRS, 2, nq, ta)

    def body(qkv_ref, do_ref, y_ref, lse_ref, crow_ref, cexp_ref, dqkv_ref, dc_ref,
             qa_scr, doa_scr, qst_scr, dot_scr, kt_scr, vt_scr, dq_scr, rs_scr):
        pair = pl.program_id(1)
        lane = _iota((1, LANES), 1)
        sub = _iota((LANES, 1), 0)
        lane_mine = (lane < hd, lane >= hd)
        sub_mine = (sub < hd, sub >= hd)
        causal = _iota((ta, ta), 0) >= _iota((ta, ta), 1)
        one = jnp.ones((), BF16)
        zero = jnp.zeros((), BF16)

        @pl.when(pair == 0)
        def _():
            dc_ref[...] = jnp.zeros_like(dc_ref)

        for i in range(nq):
            rows = slice(i * ta, (i + 1) * ta)
            qs = qkv_ref[rows, 0:LANES] * 0.125
            qst = _transpose_bf16(qs)
            kt = _transpose_bf16(qkv_ref[rows, LANES:2 * LANES])
            vt = _transpose_bf16(qkv_ref[rows, 2 * LANES:3 * LANES])
            do = do_ref[rows, :]
            dof = do.astype(F32)
            dot = dof.T.astype(BF16)
            pr = y_ref[rows, :] * dof
            cq = cexp_ref[rows, :]
            lse_c = jnp.where(sub == 0, lse_ref[0, i:i + 1, :],
                              jnp.where(sub == 1, lse_ref[1, i:i + 1, :], 0.0)).T
            for hh in range(2):
                sp = _spare(hh)
                dsum = jnp.sum(jnp.where(lane_mine[hh], pr, 0.0), axis=-1, keepdims=True)
                bias = cq[:, hh * hd:hh * hd + 1] - lse_c[:, hh:hh + 1]
                qa_scr[hh, i] = _put_cols(qs, lane_mine[hh], list(_split3(bias)) + [one, one, one], sp)
                doa_scr[hh, i] = _put_cols(do, lane_mine[hh], list(_split3(-dsum)), sp)
                qst_scr[hh, i] = jnp.where(sub_mine[hh], qst, zero)
                dot_scr[hh, i] = jnp.where(sub_mine[hh], dot, zero)
                ck = list(_split3(-crow_ref[hh, i:i + 1, :]))
                kt_scr[hh, i] = _put_rows(kt, sub_mine[hh], [one, one, one] + ck, sp)
                vt_scr[hh, i] = _put_rows(vt, sub_mine[hh], [one, one, one], sp)
            dq_scr[i] = jnp.zeros((ta, LANES), F32)
            rs_scr[i] = jnp.zeros((ta, LANES), F32)

        for kj in range(nq):
            krows = slice(kj * ta, (kj + 1) * ta)
            k = qkv_ref[krows, LANES:2 * LANES]
            km = (jnp.where(lane_mine[0], k, zero), jnp.where(lane_mine[1], k, zero))
            dkt = jnp.zeros((LANES, ta), F32)
            dvt = jnp.zeros((LANES, ta), F32)
            dcp = [jnp.zeros((8, ta), F32), jnp.zeros((8, ta), F32)]
            for qi in range(kj, nq):
                dq = jnp.zeros((ta, LANES), F32)
                rs = []
                for hh in range(2):
                    sc = _dot(qa_scr[hh, qi], kt_scr[hh, kj])
                    if qi == kj:
                        sc = jnp.where(causal, sc, MASK_VALUE)
                    p = jnp.exp(sc)
                    dsf = p * _dot(doa_scr[hh, qi], vt_scr[hh, kj])
                    dcp[hh] = dcp[hh] + jnp.sum(dsf.reshape(ta // 8, 8, ta), axis=0)
                    rs.append(jnp.sum(dsf, axis=-1, keepdims=True))
                    ds = dsf.astype(BF16)
                    dq = dq + _dot(ds, km[hh])
                    dkt = dkt + _dot(qst_scr[hh, qi], ds)
                    dvt = dvt + _dot(dot_scr[hh, qi], p.astype(BF16))
                dq_scr[qi] += dq
                rs_scr[qi] += jnp.where(lane == 0, rs[0], jnp.where(lane == 1, rs[1], 0.0))
            dqkv_ref[krows, LANES:2 * LANES] = dkt.T.astype(BF16)
            dqkv_ref[krows, 2 * LANES:3 * LANES] = dvt.T.astype(BF16)
            dca = jnp.sum(dcp[0], axis=0, keepdims=True)
            dcb = jnp.sum(dcp[1], axis=0, keepdims=True)
            dcs = jnp.where(sub == 0, dca, jnp.where(sub == 1, dcb, 0.0)).T
            dc_ref[krows, :] += (jnp.where(lane == 2 * pair, -dcs[:, 0:1], 0.0)
                                 + jnp.where(lane == 2 * pair + 1, -dcs[:, 1:2], 0.0))
        for qi in range(nq):
            rows = slice(qi * ta, (qi + 1) * ta)
            dqkv_ref[rows, 0:LANES] = (dq_scr[qi] * 0.125).astype(BF16)
            rq = rs_scr[qi]
            dc_ref[rows, :] += (jnp.where(lane == 2 * pair, rq[:, 0:1], 0.0)
                                + jnp.where(lane == 2 * pair + 1, rq[:, 1:2], 0.0))

    blk = lambda w: pl.BlockSpec((None, s, w), lambda i, p: (i, 0, p))
    rows5 = pl.BlockSpec((None, None, 2, nq, ta), lambda i, p: (i, p, 0, 0, 0))
    by_rows = lambda: pltpu.VMEM((2, nq, ta, LANES), BF16)
    by_cols = lambda: pltpu.VMEM((2, nq, LANES, ta), BF16)
    return pl.pallas_call(
        body, name="attn_bwd", grid=(b, HEAD_PAIRS),
        in_specs=[blk(3 * LANES), blk(LANES), blk(LANES), rows5, rows5, blk(LANES)],
        out_specs=[pl.BlockSpec((s, 3 * LANES), lambda i, p: (i, p)),
                   pl.BlockSpec((None, s, LANES), lambda i, p: (i, 0, 0))],
        out_shape=[jax.ShapeDtypeStruct((b * s, 3 * D_MODEL), BF16), jax.ShapeDtypeStruct((b, s, LANES), F32)],
        scratch_shapes=[by_rows(), by_rows(), by_cols(), by_cols(), by_cols(), by_cols(),
                        pltpu.VMEM((nq, ta, LANES), F32), pltpu.VMEM((nq, ta, LANES), F32)],
        compiler_params=_cparams(("parallel", "arbitrary")),
    )(qkv3, do3, y3, lse5, crow5, cexp3)


def _shifted(v, ks, rows, s):
    low = rows[0:8, :]
    out = []
    for k in ks:
        r = pltpu.roll(v, k % s, 0)
        if k > 0:
            out.append(jnp.concatenate([jnp.where(low >= k, r[0:8, :], 0.0), r[8:, :]], axis=0))
        else:
            out.append(jnp.concatenate([r[:s - 8, :], jnp.where(low < 8 + k, r[s - 8:, :], 0.0)], axis=0))
    return out


def _rnn_common(xr, cw_ref, cb_ref, bda_ref, bdx_ref, ba_ref, bx_ref, lam_ref, s):
    rows = _iota((s, LANES), 0)
    x1, x2, x3 = _shifted(xr, (1, 2, 3), rows, s)
    xc = cb_ref[...] + cw_ref[0:1, :] * x3
    xc = xc + cw_ref[1:2, :] * x2
    xc = xc + cw_ref[2:3, :] * x1
    xc = xc + cw_ref[3:4, :] * xr
    xcb = xc.astype(BF16)
    r = _sigmoid(_dot(xcb, bda_ref[...]) + ba_ref[...])
    i = _sigmoid(_dot(xcb, bdx_ref[...]) + bx_ref[...])
    sp = _softplus(-lam_ref[...])
    log_a = (-RG_C * r) * sp
    a = jnp.exp(log_a)
    a2 = a * a
    sq = jnp.sqrt(jnp.maximum(_one_minus_exp(log_a + log_a, a2), 0.0))
    return rows, (x1, x2, x3), xc, xcb, r, i, sp, a, a2, sq


def _scan_down(a, u, rows, s, s1, s2):
    low = rows & 7
    for sh in (1, 2, 4):
        keep = low >= sh
        u = u + a * jnp.where(keep, pltpu.roll(u, sh, 0), 0.0)
        a = a * jnp.where(keep, pltpu.roll(a, sh, 0), 1.0)
    ng = s // 8
    s1[...] = a
    s2[...] = u
    at = s1[pl.ds(7, ng, stride=8), :]
    ut = s2[pl.ds(7, ng, stride=8), :]
    grow = _iota((ng, LANES), 0)
    sh = 1
    while sh < ng:
        keep = grow >= sh
        ut = ut + at * jnp.where(keep, pltpu.roll(ut, sh, 0), 0.0)
        if sh * 2 < ng:
            at = at * jnp.where(keep, pltpu.roll(at, sh, 0), 1.0)
        sh *= 2
    h_in = jnp.where(grow >= 1, pltpu.roll(ut, 1, 0), 0.0)
    for k in range(8):
        s1[pl.ds(k, ng, stride=8), :] = h_in
    return u + a * s1[...]


def _scan_up(a, g, rows, s, s1, s2):
    low = rows & 7
    for sh in (1, 2, 4):
        keep = low < 8 - sh
        g = g + a * jnp.where(keep, pltpu.roll(g, s - sh, 0), 0.0)
        a = a * jnp.where(keep, pltpu.roll(a, s - sh, 0), 1.0)
    ng = s // 8
    s1[...] = a
    s2[...] = g
    at = s1[pl.ds(0, ng, stride=8), :]
    gt = s2[pl.ds(0, ng, stride=8), :]
    grow = _iota((ng, LANES), 0)
    sh = 1
    while sh < ng:
        keep = grow < ng - sh
        gt = gt + at * jnp.where(keep, pltpu.roll(gt, ng - sh, 0), 0.0)
        if sh * 2 < ng:
            at = at * jnp.where(keep, pltpu.roll(at, ng - sh, 0), 1.0)
        sh *= 2
    g_in = jnp.where(grow < ng - 1, pltpu.roll(gt, ng - 1, 0), 0.0)
    for k in range(8):
        s1[pl.ds(k, ng, stride=8), :] = g_in
    return g + a * s1[...]


def _rnn_specs(s):
    blk = lambda off: pl.BlockSpec((None, s, LANES), lambda cb, i: (i, 0, off + cb))
    vec = lambda r: pl.BlockSpec((r, LANES), lambda cb, i: (0, cb))
    mat = pl.BlockSpec((None, LANES, LANES), lambda cb, i: (cb, 0, 0))
    return blk, vec, mat


def _rnn_fwd(zrest3, conv_w, conv_b, bda, bdx, ba, bx, lam):
    b, s, _ = zrest3.shape

    def body(xr_ref, g_ref, cw_ref, cb_ref, bda_ref, bdx_ref, ba_ref, bx_ref, lam_ref, h_ref, gr_ref, s1, s2):
        xr = xr_ref[...].astype(F32)
        rows, _, xc, _, _, i, _, a, _, sq = _rnn_common(
            xr, cw_ref, cb_ref, bda_ref, bdx_ref, ba_ref, bx_ref, lam_ref, s)
        h = _scan_down(a, sq * (i * xc), rows, s, s1, s2)
        h_ref[...] = h
        g = g_ref[...].astype(F32)
        gr_ref[...] = (h * (g * _sigmoid(g))).astype(BF16)

    blk, vec, mat = _rnn_specs(s)
    return pl.pallas_call(
        body, name="rnn_fwd", grid=(N_CBLK, b),
        in_specs=[blk(N_CBLK), blk(2 * N_CBLK), vec(CONV_W), vec(1), mat, mat, vec(1), vec(1), vec(1)],
        out_specs=[blk(0), pl.BlockSpec((s, LANES), lambda cb, i: (i, cb))],
        out_shape=[jax.ShapeDtypeStruct((b, s, D_MODEL), F32), jax.ShapeDtypeStruct((b * s, D_MODEL), BF16)],
        scratch_shapes=[pltpu.VMEM((s, LANES), F32), pltpu.VMEM((s, LANES), F32)],
        compiler_params=_cparams(("parallel", "parallel")),
    )(zrest3, zrest3, conv_w, conv_b, bda, bdx, ba, bx, lam)


def _rnn_bwd(zrest3, h3, dh3, conv_w, conv_b, bda, bdx, ba, bx, lam):
    b, s, _ = zrest3.shape

    def body(xr_ref, h_ref, dh_ref, cw_ref, cb_ref, bda_ref, bdx_ref, ba_ref, bx_ref, lam_ref,
             dxr_ref, pv_ref, dbd_ref, s1, s2):
        @pl.when(pl.program_id(1) == 0)
        def _():
            pv_ref[...] = jnp.zeros_like(pv_ref)
            dbd_ref[...] = jnp.zeros_like(dbd_ref)

        xr = xr_ref[...].astype(F32)
        rows, (x1, x2, x3), xc, xcb, r, i, sp, a, a2, sq = _rnn_common(
            xr, cw_ref, cb_ref, bda_ref, bdx_ref, ba_ref, bx_ref, lam_ref, s)
        (a_next,) = _shifted(a, (-1,), rows, s)
        g = _scan_up(a_next, dh_ref[...], rows, s, s1, s2)
        (hp,) = _shifted(h_ref[...], (1,), rows, s)
        da = g * hp
        dsq = g * (i * xc)
        di = g * (sq * xc)
        dxc = g * (sq * i)
        dlog = da * a - dsq * (a2 / sq)
        dr = dlog * (-RG_C * sp)
        dpr = dr * (r * (1.0 - r))
        dpi = di * (i * (1.0 - i))
        dprb = dpr.astype(BF16)
        dpib = dpi.astype(BF16)
        dxc = dxc + _dot_nt(dprb, bda_ref[...]) + _dot_nt(dpib, bdx_ref[...])

        up1, up2, up3 = _shifted(dxc, (-1, -2, -3), rows, s)
        dxr = cw_ref[3:4, :] * dxc + cw_ref[2:3, :] * up1 + cw_ref[1:2, :] * up2 + cw_ref[0:1, :] * up3
        dxr_ref[...] = dxr.astype(BF16)

        def colsum(v):
            return jnp.sum(v, axis=0, keepdims=True)

        pv_ref[0:1, :] += colsum(dxc * x3)
        pv_ref[1:2, :] += colsum(dxc * x2)
        pv_ref[2:3, :] += colsum(dxc * x1)
        pv_ref[3:4, :] += colsum(dxc * xr)
        pv_ref[4:5, :] += colsum(dxc)
        pv_ref[5:6, :] += colsum(dpr)
        pv_ref[6:7, :] += colsum(dpi)
        pv_ref[7:8, :] += colsum(dlog * r) * (RG_C * _sigmoid(-lam_ref[...]))
        dbd_ref[0] += _dot_tn(xcb, dprb)
        dbd_ref[1] += _dot_tn(xcb, dpib)

    blk, vec, mat = _rnn_specs(s)
    hblk = pl.BlockSpec((None, s, LANES), lambda cb, i: (i, 0, cb))
    return pl.pallas_call(
        body, name="rnn_bwd", grid=(N_CBLK, b),
        in_specs=[blk(N_CBLK), hblk, hblk, vec(CONV_W), vec(1), mat, mat, vec(1), vec(1), vec(1)],
        out_specs=[pl.BlockSpec((s, LANES), lambda cb, i: (i, cb)), pl.BlockSpec((8, LANES), lambda cb, i: (0, cb)),
                   pl.BlockSpec((None, 2, LANES, LANES), lambda cb, i: (cb, 0, 0, 0))],
        out_shape=[jax.ShapeDtypeStruct((b * s, D_MODEL), BF16), jax.ShapeDtypeStruct((8, D_MODEL), F32),
                   jax.ShapeDtypeStruct((N_CBLK, 2, LANES, LANES), F32)],
        scratch_shapes=[pltpu.VMEM((s, LANES), F32), pltpu.VMEM((s, LANES), F32)],
        compiler_params=_cparams(("parallel", "arbitrary")),
    )(zrest3, h3, dh3, conv_w, conv_b, bda, bdx, ba, bx, lam)


def _branch_merge(ga, gr, wa, wr, zrest):
    t = ga.shape[0]
    tm = min(512, t)
    tn = D_MODEL

    def body(ga_ref, gr_ref, wa_ref, wr_ref, mga_ref, mgr_ref, ya_ref, yr_ref, m_ref):
        ya = _dot(ga_ref[...], wa_ref[...])
        yr = _dot(gr_ref[...], wr_ref[...])
        ya_ref[...] = ya.astype(BF16)
        yr_ref[...] = yr.astype(BF16)
        m_ref[...] = (_sigmoid(mga_ref[...].astype(F32)) * ya + _sigmoid(mgr_ref[...].astype(F32)) * yr).astype(BF16)

    nj = D_MODEL // tn
    act = pl.BlockSpec((tm, D_MODEL), lambda i, j: (i, 0))
    wgt = pl.BlockSpec((D_MODEL, tn), lambda i, j: (0, j))
    out = pl.BlockSpec((tm, tn), lambda i, j: (i, j))
    return pl.pallas_call(
        body, name="branch_merge", grid=(t // tm, nj),
        in_specs=[act, act, wgt, wgt, pl.BlockSpec((tm, tn), lambda i, j: (i, 3 * nj + j)),
                  pl.BlockSpec((tm, tn), lambda i, j: (i, 4 * nj + j))],
        out_specs=[out, out, out],
        out_shape=[jax.ShapeDtypeStruct((t, D_MODEL), BF16), jax.ShapeDtypeStruct((t, D_MODEL), BF16),
                   jax.ShapeDtypeStruct((t, D_MODEL), BF16)],
        compiler_params=_cparams(("parallel", "parallel")),
    )(ga, gr, wa, wr, zrest, zrest)


def _out_loss(m, wout, x2, tgt2, wpost):
    t = m.shape[0]
    tm = min(512, t)

    def body(m_ref, w_ref, x_ref, t_ref, wp_ref, dy_ref, do_ref, acc_ref):
        @pl.when(pl.program_id(0) == 0)
        def _():
            acc_ref[...] = jnp.zeros_like(acc_ref)

        o = _dot(m_ref[...], w_ref[...])
        r2 = lax.rsqrt(jnp.mean(o * o, axis=-1, keepdims=True) + NORM_EPS)
        n = o * r2
        wp = wp_ref[...]
        err = (x_ref[...] + n * wp) - t_ref[...]
        dy = err * (1.0 / D_MODEL)
        dn = dy * wp
        do = r2 * (dn - n * jnp.mean(dn * n, axis=-1, keepdims=True))
        dy_ref[...] = dy
        do_ref[...] = do.astype(BF16)
        acc_ref[0:1, :] += jnp.sum(dy * n, axis=0, keepdims=True)
        acc_ref[1:2, :] += jnp.sum(err * err, axis=0, keepdims=True)

    row = pl.BlockSpec((tm, D_MODEL), lambda i: (i, 0))
    return pl.pallas_call(
        body, name="out_loss", grid=(t // tm,),
        in_specs=[row, pl.BlockSpec((D_MODEL, D_MODEL), lambda i: (0, 0)), row, row,
                  pl.BlockSpec((1, D_MODEL), lambda i: (0, 0))],
        out_specs=[row, row, pl.BlockSpec((8, D_MODEL), lambda i: (0, 0))],
        out_shape=[jax.ShapeDtypeStruct((t, D_MODEL), F32), jax.ShapeDtypeStruct((t, D_MODEL), BF16),
                   jax.ShapeDtypeStruct((8, D_MODEL), F32)],
        compiler_params=_cparams(("arbitrary",)),
    )(m, wout, x2, tgt2, wpost)


def _merge_bwd(do, wout, zrest, ya, yr):
    t = do.shape[0]
    tm = min(512, t)
    tn = D_MODEL
    nj = D_MODEL // tn

    def body(do_ref, w_ref, mga_ref, mgr_ref, ya_ref, yr_ref, dya_ref, dyr_ref, dmga_ref, dmgr_ref):
        dm = _dot_nt(do_ref[...], w_ref[...])
        sa = _sigmoid(mga_ref[...].astype(F32))
        sr = _sigmoid(mgr_ref[...].astype(F32))
        dya_ref[...] = (dm * sa).astype(BF16)
        dyr_ref[...] = (dm * sr).astype(BF16)
        dmga_ref[...] = (dm * ya_ref[...].astype(F32) * (sa * (1.0 - sa))).astype(BF16)
        dmgr_ref[...] = (dm * yr_ref[...].astype(F32) * (sr * (1.0 - sr))).astype(BF16)

    out = pl.BlockSpec((tm, tn), lambda i, j: (i, j))
    bf = jax.ShapeDtypeStruct((t, D_MODEL), BF16)
    return pl.pallas_call(
        body, name="merge_bwd", grid=(t // tm, nj),
        in_specs=[pl.BlockSpec((tm, D_MODEL), lambda i, j: (i, 0)), pl.BlockSpec((tn, D_MODEL), lambda i, j: (j, 0)),
                  pl.BlockSpec((tm, tn), lambda i, j: (i, 3 * nj + j)),
                  pl.BlockSpec((tm, tn), lambda i, j: (i, 4 * nj + j)), out, out],
        out_specs=[out, out, out, out],
        out_shape=[bf, bf, bf, bf],
        compiler_params=_cparams(("parallel", "parallel")),
    )(do, wout, zrest, zrest, ya, yr)


def _branch_bwd(dya, dyr, wa, wr, zrest, yatt, ylru):
    t = dya.shape[0]
    tm = min(512, t)
    tn = D_MODEL
    nj = D_MODEL // tn

    def body(dya_ref, dyr_ref, wa_ref, wr_ref, ga_ref, gr_ref, ya_ref, yl_ref,
             dyatt_ref, dga_ref, dyl_ref, dgr_ref):
        dga = _dot_nt(dya_ref[...], wa_ref[...])
        dgr = _dot_nt(dyr_ref[...], wr_ref[...])
        g = ga_ref[...].astype(F32)
        sg = _sigmoid(g)
        dyatt_ref[...] = (dga * (g * sg)).astype(BF16)
        dga_ref[...] = (dga * ya_ref[...] * (sg * (1.0 + g * (1.0 - sg)))).astype(BF16)
        g = gr_ref[...].astype(F32)
        sg = _sigmoid(g)
        dyl_ref[...] = dgr * (g * sg)
        dgr_ref[...] = (dgr * yl_ref[...] * (sg * (1.0 + g * (1.0 - sg)))).astype(BF16)

    act = pl.BlockSpec((tm, D_MODEL), lambda i, j: (i, 0))
    wgt = pl.BlockSpec((tn, D_MODEL), lambda i, j: (j, 0))
    out = pl.BlockSpec((tm, tn), lambda i, j: (i, j))
    bf = jax.ShapeDtypeStruct((t, D_MODEL), BF16)
    return pl.pallas_call(
        body, name="branch_bwd", grid=(t // tm, nj),
        in_specs=[act, act, wgt, wgt, pl.BlockSpec((tm, tn), lambda i, j: (i, j)),
                  pl.BlockSpec((tm, tn), lambda i, j: (i, 2 * nj + j)), out, out],
        out_specs=[out, out, out, out],
        out_shape=[bf, bf, jax.ShapeDtypeStruct((t, D_MODEL), F32), bf],
        compiler_params=_cparams(("parallel", "parallel")),
    )(dya, dyr, wa, wr, zrest, zrest, yatt, ylru)


def _dh_final(parts, after, x2, dy, wpre):
    t = x2.shape[0]
    tm = min(256, t)
    np_ = len(parts)

    def body(*refs):
        x_ref, dy_ref, w_ref = refs[2 * np_ + 1:2 * np_ + 4]
        gx_ref, pw_ref = refs[2 * np_ + 4:]

        @pl.when(pl.program_id(0) == 0)
        def _():
            pw_ref[...] = jnp.zeros_like(pw_ref)

        dh = _dot(refs[0][...], refs[np_][...])
        for p in range(1, np_):
            dh = dh + _dot(refs[p][...], refs[np_ + p][...])
        x = x_ref[...]
        r = lax.rsqrt(jnp.mean(x * x, axis=-1, keepdims=True) + NORM_EPS)
        xn = x * r
        dxn = dh * w_ref[...]
        gx_ref[...] = r * (dxn - xn * jnp.mean(dxn * xn, axis=-1, keepdims=True)) + dy_ref[...]
        pw_ref[0:1, :] += jnp.sum(dh * xn, axis=0, keepdims=True)

    row = pl.BlockSpec((tm, D_MODEL), lambda i: (i, 0))
    in_specs = [pl.BlockSpec((tm, dz.shape[1]), lambda i: (i, 0)) for dz, _ in parts]
    in_specs += [pl.BlockSpec(w.shape, lambda i: (0, 0), pipeline_mode=pl.Buffered(1)) for _, w in parts]
    in_specs += [pl.BlockSpec(after.shape, lambda i: (0, 0)), row, row, pl.BlockSpec((1, D_MODEL), lambda i: (0, 0))]
    return pl.pallas_call(
        body, name="dh_final", grid=(t // tm,),
        in_specs=in_specs,
        out_specs=[row, pl.BlockSpec((8, D_MODEL), lambda i: (0, 0))],
        out_shape=[jax.ShapeDtypeStruct((t, D_MODEL), F32), jax.ShapeDtypeStruct((8, D_MODEL), F32)],
        compiler_params=_cparams(("arbitrary",), vmem_mb=48),
    )(*[dz for dz, _ in parts], *[w for _, w in parts], after, x2, dy, wpre)


def _adamw(w, g, m, v):
    m = ADAM_B1 * m + (1.0 - ADAM_B1) * g
    v = ADAM_B2 * v + (1.0 - ADAM_B2) * (g * g)
    m_hat = m / (1.0 - ADAM_B1 ** ADAM_STEP)
    v_hat = v / (1.0 - ADAM_B2 ** ADAM_STEP)
    delta = -ADAM_LR * (m_hat / (jnp.sqrt(v_hat) + ADAM_EPS) + ADAM_WD * w)
    return delta, m, v


def _reduce_adamw(own, parts, place, w, m, v, name):
    r, c = w.shape
    blk, nblk, at = _blocks_2d(r, c)

    def body(place_ref, own_ref, p_ref, w_ref, m_ref, v_ref, g_ref, d_ref, nm_ref, nv_ref):
        mine = place_ref[1]
        own_blk = own_ref[...]
        g = jnp.where(mine == 0, own_blk, p_ref[0].astype(F32))
        for j in range(1, N_CHIPS):
            g = g + jnp.where(mine == j, own_blk, p_ref[j].astype(F32))
        d, nm, nv = _adamw(w_ref[...], g, m_ref[...], v_ref[...])
        g_ref[...] = g
        d_ref[...] = d
        nm_ref[...] = nm
        nv_ref[...] = nv

    row = pl.BlockSpec(blk, lambda i, pr: at(i))
    sh = jax.ShapeDtypeStruct((r, c), F32)
    grid_spec = pltpu.PrefetchScalarGridSpec(
        num_scalar_prefetch=1, grid=(nblk,),
        in_specs=[row, pl.BlockSpec((N_CHIPS,) + blk, lambda i, pr: (0,) + at(i)), row, row, row],
        out_specs=[row, row, row, row])
    return pl.pallas_call(
        body, name=name, grid_spec=grid_spec, out_shape=[sh, sh, sh, sh],
        compiler_params=_cparams(("parallel",)),
    )(place, own, parts, w, m, v)


def _reduce_adamw_stacked(own, parts, place, triples, name):
    n = len(triples)
    _, r, c = triples[0][0].shape

    def body(place_ref, own_ref, p_ref, *refs):
        ins, outs = refs[:3 * n], refs[3 * n:]
        mine = place_ref[1]
        for i in range(n):
            rows = slice(i * r, (i + 1) * r)
            own_blk = own_ref[rows, :]
            g = jnp.where(mine == 0, own_blk, p_ref[0, rows, :].astype(F32))
            for j in range(1, N_CHIPS):
                g = g + jnp.where(mine == j, own_blk, p_ref[j, rows, :].astype(F32))
            d, nm, nv = _adamw(ins[3 * i][0], g, ins[3 * i + 1][0], ins[3 * i + 2][0])
            for k, val in enumerate((g, d, nm, nv)):
                outs[4 * i + k][0] = val

    whole = lambda shape: pl.BlockSpec(shape, lambda i, pr: (0,) * len(shape))
    grid_spec = pltpu.PrefetchScalarGridSpec(
        num_scalar_prefetch=1, grid=(1,),
        in_specs=[whole(own.shape), whole(parts.shape)] + [whole((1, r, c))] * (3 * n),
        out_specs=[whole((1, r, c))] * (4 * n))
    res = pl.pallas_call(
        body, name=name, grid_spec=grid_spec,
        out_shape=[jax.ShapeDtypeStruct((1, r, c), F32)] * (4 * n),
        compiler_params=_cparams(("arbitrary",)),
    )(place, own, parts, *[a for t3 in triples for a in t3])
    return [res[4 * i:4 * i + 4] for i in range(n)]


def _interleave_qkv(a):
    lead = a.shape[:-1]
    return a.reshape(lead + (3, HEAD_PAIRS, LANES)).swapaxes(-3, -2).reshape(lead + (3 * D_MODEL,))


def _deinterleave_qkv(a):
    lead = a.shape[:-1]
    return a.reshape(lead + (HEAD_PAIRS, 3, LANES)).swapaxes(-3, -2).reshape(lead + (3 * D_MODEL,))


def _interleave_rows(a):
    return a.reshape(3, HEAD_PAIRS, LANES, a.shape[1]).swapaxes(0, 1).reshape(a.shape)


def _deinterleave_rows(a):
    return a.reshape(HEAD_PAIRS, 3, LANES, a.shape[1]).swapaxes(0, 1).reshape(a.shape)


def _pack_small(pre, conv_b, rg_ba, rg_bx, lam, post, loss_row, b_in, conv_w_full, rg_wa, rg_wx):
    z = jnp.zeros((1, D_MODEL), F32)
    b_used = jnp.concatenate([b_in[:, 0:3 * D_MODEL], b_in[:, 3 * D_MODEL + HEADS:IN_TOTAL]], axis=1)
    b_f = jnp.pad(b_in[:, 3 * D_MODEL:3 * D_MODEL + HEADS], ((0, 0), (0, D_MODEL - HEADS)))
    return jnp.concatenate([
        pre, conv_b, rg_ba, rg_bx, lam, post, loss_row, z,
        b_used.reshape(9, D_MODEL), b_f, conv_w_full, z, z,
        rg_wa.reshape(64, D_MODEL), rg_wx.reshape(64, D_MODEL)], axis=0)


def _unpack_small(p):
    b_used = p[8:17].reshape(1, 9 * D_MODEL)
    b_in = jnp.concatenate([b_used[:, 0:3 * D_MODEL], p[17:18, 0:HEADS], b_used[:, 3 * D_MODEL:]], axis=1)
    return dict(pre_norm_w=p[0:1], conv_b=p[1:2], rg_ba=p[2:3], rg_bx=p[3:4], rg_lambda=p[4:5],
                post_norm_w=p[5:6], loss_row=p[6:7], b_in=b_in, conv_w_full=p[18:22],
                rg_wa=p[24:88].reshape(1, 16, 64, 64), rg_wx=p[88:152].reshape(1, 16, 64, 64))


def _reduce_small(parts, first, w, m, v, vectors):
    nvec = len(vectors)

    def body(p_ref, f_ref, w_ref, m_ref, v_ref, *refs):
        ins, outs = refs[:3 * nvec], refs[3 * nvec:]
        g = p_ref[0]
        g0 = f_ref[0, 0:1, :]
        for j in range(1, N_DEV):
            g = g + p_ref[j]
            g0 = g0 + f_ref[j, 0:1, :]
        d, nm, nv = _adamw(w_ref[...], g, m_ref[...], v_ref[...])
        for k, val in enumerate((g, d, nm, nv)):
            outs[k][...] = val
        for i in range(nvec):
            gi = g0 if i == 0 else g[i:i + 1, :]
            di, nmi, nvi = _adamw(ins[3 * i][...], gi, ins[3 * i + 1][...], ins[3 * i + 2][...])
            for k, val in enumerate((gi, di, nmi, nvi)):
                outs[4 + 4 * i + k][...] = val
        outs[-1][...] = jnp.zeros((8, LANES), F32) + (0.5 / D_MODEL) * jnp.sum(g[LOSS_ROW:LOSS_ROW + 1, :])

    sh = jax.ShapeDtypeStruct((SMALL_ROWS, D_MODEL), F32)
    vec = jax.ShapeDtypeStruct((1, D_MODEL), F32)
    res = pl.pallas_call(
        body, name="reduce_small",
        out_shape=[sh, sh, sh, sh] + [vec] * (4 * nvec) + [jax.ShapeDtypeStruct((8, LANES), F32)],
    )(parts, first, w, m, v, *[a for t3 in vectors for a in t3])
    return res[:4], [res[4 + 4 * i:8 + 4 * i] for i in range(nvec)], res[-1]


def kernel(x, pre_norm_w, w_in, b_in, conv_w, conv_b, rg_wa, rg_ba, rg_wx, rg_bx, rg_lambda, w_branch_a, w_branch_r, w_out, post_norm_w, loss_target, m_pre_norm_w, m_w_in, m_b_in, m_conv_w, m_conv_b, m_rg_wa, m_rg_ba, m_rg_wx, m_rg_bx, m_rg_lambda, m_w_branch_a, m_w_branch_r, m_w_out, m_post_norm_w, v_pre_norm_w, v_w_in, v_b_in, v_conv_w, v_conv_b, v_rg_wa, v_rg_ba, v_rg_wx, v_rg_bx, v_rg_lambda, v_w_branch_a, v_w_branch_r, v_w_out, v_post_norm_w):
    b, s, _ = x.shape
    t = b * s
    me = 4 * lax.axis_index("x") + 2 * lax.axis_index("y") + lax.axis_index("c")
    shard_rows = D_MODEL // N_DEV

    place = jnp.stack([lax.axis_index("c"), 2 * lax.axis_index("x") + lax.axis_index("y")]).astype(jnp.int32)
    w_in_all = _gather(w_in[0].T.astype(BF16), "gather_w_in")
    wt_full = w_in_all.reshape(IN_TOTAL, D_MODEL)
    conv_terms = jnp.concatenate(_split3(conv_w[0]), axis=0)
    conv_pad = jnp.pad(conv_terms, ((0, 16 - 3 * CONV_W), (0, D_MODEL - LANES)))
    sq_stack = jnp.concatenate([w_branch_a[0].astype(BF16), w_branch_r[0].astype(BF16), w_out[0].astype(BF16),
                                conv_pad], axis=0)
    sq_sems, sq_src, sq_land, sq_token = _gather_start(sq_stack, w_in_all, "gather_w_sq_start")

    w_qkv = _interleave_rows(wt_full[0:3 * D_MODEL])
    w_f = jnp.pad(wt_full[3 * D_MODEL:3 * D_MODEL + HEADS], ((0, LANES - HEADS), (0, 0)))
    w_rest = wt_full[3 * D_MODEL + HEADS:IN_USED]
    b_qkv = _interleave_qkv(b_in[:, 0:3 * D_MODEL]) + sq_token[0, 0]
    b_f = jnp.pad(b_in[:, 3 * D_MODEL:3 * D_MODEL + HEADS], ((0, 0), (0, LANES - HEADS)))
    b_rest = b_in[:, 3 * D_MODEL + HEADS:IN_USED]

    def blockdiag(w):
        w2 = w.reshape(N_CBLK, 2, HEAD_DIM, HEAD_DIM)
        zz = jnp.zeros((N_CBLK, HEAD_DIM, HEAD_DIM), w.dtype)
        top = jnp.concatenate([w2[:, 0], zz], axis=2)
        bot = jnp.concatenate([zz, w2[:, 1]], axis=2)
        return jnp.concatenate([top, bot], axis=1).astype(BF16)

    bda, bdx = blockdiag(rg_wa[0]), blockdiag(rg_wx[0])

    x2 = x.reshape(t, D_MODEL)
    tgt2 = loss_target.reshape(t, D_MODEL)
    h, qkv, zf = _prenorm_inproj(x2, pre_norm_w, w_qkv, b_qkv, w_f, b_f)
    zrest = _mm_bias(h, w_rest, b_rest, BF16, "inproj_rest")
    qkv3 = qkv.reshape(b, s, 3 * D_MODEL)
    zrest3 = zrest.reshape(b, s, 5 * D_MODEL)
    zf3 = zf.reshape(b, s, LANES)
    cexp3, crow = _fgate_fwd(zf3)
    yatt3, lse, ga = _attn_fwd(qkv3, cexp3, crow, zrest3)

    sq_all = _gather_wait(sq_sems, sq_src, sq_land, ga, "gather_w_sq_wait")
    sq_all = lax.dynamic_update_slice(sq_all, sq_stack[None], (me, 0, 0))
    wa = sq_all[:, 0:shard_rows].reshape(D_MODEL, D_MODEL)
    wr = sq_all[:, shard_rows:2 * shard_rows].reshape(D_MODEL, D_MODEL)
    wo = sq_all[:, 2 * shard_rows:3 * shard_rows].reshape(D_MODEL, D_MODEL)
    conv_all = sq_all[:, 3 * shard_rows:3 * shard_rows + 3 * CONV_W, 0:LANES].astype(F32)
    conv_all = (conv_all[:, 0:CONV_W] + conv_all[:, CONV_W:2 * CONV_W]) + conv_all[:, 2 * CONV_W:3 * CONV_W]
    conv_full = conv_all.transpose(1, 0, 2).reshape(CONV_W, D_MODEL)

    ylru3, gr = _rnn_fwd(zrest3, conv_full, conv_b, bda, bdx, rg_ba, rg_bx, rg_lambda)
    ya, yr, mm = _branch_merge(ga, gr, wa, wr, zrest)
    dy, do, acc_out = _out_loss(mm, wo, x2, tgt2, post_norm_w)

    dya, dyr, dz_mga, dz_mgr = _merge_bwd(do, wo, zrest, ya, yr)
    dyatt, dz_ga, dylru, dz_gr = _branch_bwd(dya, dyr, wa, wr, zrest, yatt3.reshape(t, D_MODEL),
                                             ylru3.reshape(t, D_MODEL))
    dz_xr, pvec, dbd = _rnn_bwd(zrest3, ylru3, dylru.reshape(b, s, D_MODEL), conv_full, conv_b, bda, bdx,
                                rg_ba, rg_bx, rg_lambda)
    dz_qkv, dc3 = _attn_bwd(qkv3, dyatt.reshape(b, s, D_MODEL), yatt3, lse, crow, cexp3)
    dz_f = _fgate_bwd(dc3, zf3)

    dw_qkv, db_qkv = _mm_tn(dz_qkv, h, "dw_qkv")
    dw_f, db_f = _mm_tn(dz_f, h, "dw_f")
    dw_parts, db_parts = [], []
    for nm, dzp in (("ga", dz_ga), ("xr", dz_xr), ("gr", dz_gr), ("mga", dz_mga), ("mgr", dz_mgr)):
        dwp, dbp = _mm_tn(dzp, h, "dw_" + nm)
        dw_parts.append(dwp)
        db_parts.append(dbp[0:1])

    zeros_tail = jnp.zeros((IN_TOTAL - IN_USED, D_MODEL), F32)
    dwt_full = jnp.concatenate([_deinterleave_rows(dw_qkv), dw_f[0:HEADS]] + dw_parts + [zeros_tail], axis=0)
    dw_in_send = dwt_full.reshape(N_CHIPS, 2, W_SHARD, D_MODEL).transpose(1, 0, 2, 3)
    swp_sems, dw_in_src, swp_land, swp_token = _swap_start(dw_in_send, db_f, "swap_dw_in_start")
    dw_a, _ = _mm_tn(ga, dya, "dw_a", after=swp_token)
    dw_r, _ = _mm_tn(gr, dyr, "dw_r", after=swp_token)
    dw_o, _ = _mm_tn(mm, do, "dw_o", after=swp_token)
    dw_in_send, sib_in = _swap_wait(swp_sems, dw_in_src, swp_land, dw_o, "swap_dw_in_wait")
    by_dest = lambda a: a.reshape(N_CHIPS, 2, shard_rows, D_MODEL).transpose(1, 0, 2, 3)
    dw_sq_send = jnp.concatenate([by_dest(dw_a), by_dest(dw_r), by_dest(dw_o)], axis=2)

    db_in_full = jnp.concatenate([_deinterleave_qkv(db_qkv[0:1]), db_f[0:1, 0:HEADS]] + db_parts
                                 + [jnp.zeros((1, IN_TOTAL - IN_USED), F32)], axis=1)
    d_rg_wa = jnp.stack([dbd[:, 0, 0:HEAD_DIM, 0:HEAD_DIM], dbd[:, 0, HEAD_DIM:, HEAD_DIM:]], axis=1)
    d_rg_wx = jnp.stack([dbd[:, 1, 0:HEAD_DIM, 0:HEAD_DIM], dbd[:, 1, HEAD_DIM:, HEAD_DIM:]], axis=1)
    small_g = _pack_small(jnp.zeros((1, D_MODEL), F32), pvec[4:5], pvec[5:6], pvec[6:7], pvec[7:8], acc_out[0:1],
                          acc_out[1:2], db_in_full, pvec[0:4], d_rg_wa, d_rg_wx)
    sm_sems, sm_src, sm_land, sm_token = _gather_start(small_g, dw_o, "gather_small_start")

    sqs_sems, dw_sq_src, sqs_land, sqs_token = _swap_start(dw_sq_send, sm_token, "swap_dw_sq_start")
    chip_in, own_in = _pair_add(dw_in_send, sib_in, place, "pair_add_in", after=sqs_token)
    dw_sq_send, sib_sq = _swap_wait(sqs_sems, dw_sq_src, sqs_land, chip_in, "swap_dw_sq_wait")
    chip_sq, own_sq = _pair_add(dw_sq_send, sib_sq, place, "pair_add_sq")
    sems, sent, lands, token = _exchange_chips_start([chip_in, chip_sq], "exchange_dw_start")

    wt = lambda lo: w_rest[lo * D_MODEL:(lo + 1) * D_MODEL]
    grad_x2, acc_pre = _dh_final(
        [(dz_qkv, w_qkv), (dz_f, w_f), (dz_ga, wt(0)), (dz_xr, wt(1)), (dz_gr, wt(2)), (dz_mga, wt(3)),
         (dz_mgr, wt(4))], token, x2, dy, pre_norm_w)
    pre_sems, pre_src, pre_land, pre_token = _gather_start(acc_pre, grad_x2, "gather_pre_start")
    recv_in, recv_sq = _exchange_chips_wait(sems, sent, lands, pre_token, "exchange_dw_wait")

    g_in, d_in, nm_in, nv_in = [a.T for a in _reduce_adamw(
        own_in, recv_in, place, w_in[0].T, m_w_in[0].T, v_w_in[0].T, "adamw_w_in")]
    sq_out = _reduce_adamw_stacked(
        own_sq, recv_sq, place,
        [(w_branch_a, m_w_branch_a, v_w_branch_a), (w_branch_r, m_w_branch_r, v_w_branch_r),
         (w_out, m_w_out, v_w_out)], "adamw_w_sq")
    pre_all = _gather_wait(pre_sems, pre_src, pre_land, sq_out[2][1], "gather_pre_wait")
    pre_all = lax.dynamic_update_slice(pre_all, acc_pre[None], (me, 0, 0))
    small_all = _gather_wait(sm_sems, sm_src, sm_land, pre_all, "gather_small_wait")
    small_all = lax.dynamic_update_slice(small_all, small_g[None], (me, 0, 0))

    def place_conv(a):
        return lax.dynamic_update_slice(jnp.zeros((CONV_W, D_MODEL), F32), a[0], (0, me * LANES))

    zrow = jnp.zeros((1, D_MODEL), F32)
    vector_names = ["pre_norm_w", "conv_b", "rg_ba", "rg_bx", "rg_lambda", "post_norm_w"]
    vectors = [(pre_norm_w, m_pre_norm_w, v_pre_norm_w), (conv_b, m_conv_b, v_conv_b), (rg_ba, m_rg_ba, v_rg_ba),
               (rg_bx, m_rg_bx, v_rg_bx), (rg_lambda, m_rg_lambda, v_rg_lambda),
               (post_norm_w, m_post_norm_w, v_post_norm_w)]
    small_w = _pack_small(zrow, zrow, zrow, zrow, zrow, zrow, zrow, b_in, place_conv(conv_w), rg_wa[0], rg_wx[0])
    small_m = _pack_small(zrow, zrow, zrow, zrow, zrow, zrow, zrow, m_b_in, place_conv(m_conv_w), m_rg_wa[0],
                          m_rg_wx[0])
    small_v = _pack_small(zrow, zrow, zrow, zrow, zrow, zrow, zrow, v_b_in, place_conv(v_conv_w), v_rg_wa[0],
                          v_rg_wx[0])
    packed, vector_out, loss_tile = _reduce_small(small_all, pre_all, small_w, small_m, small_v, vectors)
    outs_small = [_unpack_small(p) for p in packed]
    loss = loss_tile[0, 0]

    def leaf(kind, name):
        if name == "w_in":
            return (g_in, d_in, nm_in, nv_in)[kind][None]
        if name in ("w_branch_a", "w_branch_r", "w_out"):
            return sq_out[("w_branch_a", "w_branch_r", "w_out").index(name)][kind]
        if name == "conv_w":
            return lax.dynamic_slice(outs_small[kind]["conv_w_full"], (0, me * LANES), (CONV_W, LANES))[None]
        if name in vector_names:
            return vector_out[vector_names.index(name)][kind]
        return outs_small[kind][name]

    names = ["pre_norm_w", "w_in", "b_in", "conv_w", "conv_b", "rg_wa", "rg_ba", "rg_wx", "rg_bx", "rg_lambda",
             "w_branch_a", "w_branch_r", "w_out", "post_norm_w"]
    out = [loss, grad_x2.reshape(b, s, D_MODEL)]
    for kind in range(4):
        out += [leaf(kind, nm) for nm in names]
    return tuple(out)
```

```python
import jax
import jax.numpy as jnp
from jax import lax
from jax.experimental import pallas as pl
from jax.experimental.pallas import tpu as pltpu

F32 = jnp.float32
BF16 = jnp.bfloat16

N_DEV = 8
D_MODEL = 1024
HEADS = 16
HEAD_DIM = 64
HEAD_PAIRS = HEADS // 2
LANES = 128
N_CBLK = D_MODEL // LANES
CONV_W = 4
RG_C = 8.0
NORM_EPS = 1e-6
MASK_VALUE = -1e30
IN_USED = 8208
IN_TOTAL = 9232
W_SHARD = IN_TOTAL // N_DEV

ADAM_LR = 0.001
ADAM_B1 = 0.9
ADAM_B2 = 0.999
ADAM_EPS = 1e-08
ADAM_WD = 0.01
ADAM_STEP = 10

ATT_TILE_FWD = 256
ATT_TILE_BWD = 512
SCAN_TILE = 512
SMALL_ROWS = 152
LOSS_ROW = 6


def _cparams(sem=None, vmem_mb=None):
    kw = {}
    if sem is not None:
        kw["dimension_semantics"] = sem
    if vmem_mb is not None:
        kw["vmem_limit_bytes"] = vmem_mb * 1024 * 1024
    return pltpu.CompilerParams(**kw)


def _sigmoid(x):
    return 1.0 / (1.0 + jnp.exp(-x))


def _softplus(x):
    return jnp.maximum(x, 0.0) + jnp.log1p(jnp.exp(-jnp.abs(x)))


def _one_minus_exp(y, exp_y):
    series = -y * (1.0 + y * (1.0 / 2 + y * (1.0 / 6 + y * (1.0 / 24 + y * (1.0 / 120)))))
    return jnp.where(y > -0.0625, series, 1.0 - exp_y)


def _split3(x):
    hi = x.astype(BF16)
    r1 = x - hi.astype(F32)
    mid = r1.astype(BF16)
    lo = (r1 - mid.astype(F32)).astype(BF16)
    return hi, mid, lo


def _dot(a, b):
    return jnp.dot(a, b, preferred_element_type=F32)


def _dot_nt(a, b):
    return lax.dot_general(a, b, (((1,), (1,)), ((), ())), preferred_element_type=F32)


def _dot_tn(a, b):
    return lax.dot_general(a, b, (((0,), (0,)), ((), ())), preferred_element_type=F32)


def _iota(shape, dim):
    return lax.broadcasted_iota(jnp.int32, shape, dim)


_ANY = pl.BlockSpec(memory_space=pl.ANY)
_MESH = pl.DeviceIdType.MESH
N_CHIPS = 4


def _place():
    x, y, c = lax.axis_index("x"), lax.axis_index("y"), lax.axis_index("c")
    other_chips = [(1 - x, y), (x, 1 - y), (1 - x, 1 - y)]
    return x, y, c, other_chips


def _gather(x_shard, name):
    def body(x_ref, out_ref, send_sems, recv_sems, local_sem):
        x, y, c, chips = _place()
        me, sibling = (x, y, c), (x, y, 1 - c)

        def slot(p):
            return out_ref.at[4 * p[0] + 2 * p[1] + p[2]]

        def copy(k, block, to, src=None):
            return pltpu.make_async_remote_copy(
                src_ref=slot(block) if src is None else src, dst_ref=slot(block),
                send_sem=send_sems.at[k], recv_sem=recv_sems.at[k], device_id=to, device_id_type=_MESH)

        mine = pltpu.make_async_copy(x_ref, slot(me), local_sem)
        mine.start()
        first = [copy(0, me, sibling, src=x_ref)]
        first += [copy(1 + j, me, (*chip, c), src=x_ref) for j, chip in enumerate(chips)]
        for cp in first:
            cp.start()
        passed = [copy(4 + j, (*chip, c), sibling) for j, chip in enumerate(chips)]
        for j, chip in enumerate(chips):
            copy(1 + j, (*chip, c), me).wait_recv()
            passed[j].start()
        copy(0, sibling, me).wait_recv()
        for j, chip in enumerate(chips):
            copy(4 + j, (*chip, 1 - c), me).wait_recv()
        for cp in first + passed:
            cp.wait_send()
        mine.wait()

    return pl.pallas_call(
        body, name=name,
        out_shape=jax.ShapeDtypeStruct((N_DEV,) + tuple(x_shard.shape), x_shard.dtype),
        in_specs=[_ANY], out_specs=_ANY,
        scratch_shapes=[pltpu.SemaphoreType.DMA((7,)), pltpu.SemaphoreType.DMA((7,)), pltpu.SemaphoreType.DMA],
    )(x_shard)


def _blocks_2d(r, c):
    if r % 128 == 0:
        return (128, c), r // 128, lambda i: (i, 0)
    return (r, 256), c // 256, lambda i: (0, i)


def _pair_add(src, recv, place, name, after=None):
    _, _, r, c = src.shape
    blk, nblk, at = _blocks_2d(r, c)
    deps = [] if after is None else [after]

    def body(place_ref, a_ref, b_ref, *refs):
        q16_ref, own_ref = refs[len(deps):]
        q = a_ref[...] + b_ref[...]
        q16_ref[...] = q.astype(BF16)

        @pl.when(pl.program_id(1) == place_ref[1])
        def _():
            own_ref[...] = q

    grid_spec = pltpu.PrefetchScalarGridSpec(
        num_scalar_prefetch=1, grid=(nblk, N_CHIPS),
        in_specs=[pl.BlockSpec((None, None) + blk, lambda i, j, pr: (pr[0], j) + at(i)),
                  pl.BlockSpec((None,) + blk, lambda i, j, pr: (j,) + at(i))]
        + [pl.BlockSpec(d.shape, lambda i, j, pr: (0, 0)) for d in deps],
        out_specs=[pl.BlockSpec((None,) + blk, lambda i, j, pr: (j,) + at(i)),
                   pl.BlockSpec(blk, lambda i, j, pr: at(i))])
    return pl.pallas_call(
        body, name=name, grid_spec=grid_spec,
        out_shape=[jax.ShapeDtypeStruct((N_CHIPS, r, c), BF16), jax.ShapeDtypeStruct((r, c), F32)],
        compiler_params=_cparams(("parallel", "arbitrary")),
    )(place, src, recv, *deps)


_HBM = pl.BlockSpec(memory_space=pltpu.HBM)
_SEM = pl.BlockSpec(memory_space=pltpu.SEMAPHORE)
_DATAFLOW = pltpu.SideEffectType.DATAFLOW_SIDE_EFFECTING


def _chip_copy(src_ref, land_ref, send_sem, recv_sem, k, chips, c, land):
    chip = chips[k]
    return pltpu.make_async_remote_copy(
        src_ref=src_ref.at[2 * chip[0] + chip[1]], dst_ref=land_ref.at[land],
        send_sem=send_sem, recv_sem=recv_sem, device_id=(*chip, c), device_id_type=_MESH)


def _exchange_chips_start(srcs, name):
    n = len(srcs)
    ncp = 3 * n

    def body(*refs):
        src_refs, land_refs = refs[:n], refs[n:2 * n]
        sems = refs[4 * n:4 * n + 2 * ncp]
        token = refs[-1]
        x, y, c, chips = _place()
        for i in range(n):
            for k in range(3):
                j = 3 * i + k
                _chip_copy(src_refs[i], land_refs[i], sems[j], sems[ncp + j], k, chips, c, 2 * x + y).start()
        token[...] = jnp.zeros_like(token)

    hbm = [pltpu.HBM(a.shape, a.dtype) for a in srcs]
    lands = [pltpu.with_memory_space_constraint(lax.empty(a.shape, a.dtype), pltpu.HBM) for a in srcs]
    res = pl.pallas_call(
        body, name=name,
        out_shape=(*hbm, *hbm, *([pltpu.SemaphoreType.DMA(())] * (2 * ncp)), jax.ShapeDtypeStruct((8, LANES), F32)),
        in_specs=[_HBM] * (2 * n),
        out_specs=(*([_HBM] * (2 * n)), *([_SEM] * (2 * ncp)), pl.BlockSpec(memory_space=pltpu.VMEM)),
        input_output_aliases={i: i for i in range(2 * n)},
        compiler_params=pltpu.CompilerParams(has_side_effects=_DATAFLOW),
    )(*[pltpu.with_memory_space_constraint(a, pltpu.HBM) for a in srcs], *lands)
    return list(res[2 * n:2 * n + 2 * ncp]), list(res[:n]), list(res[n:2 * n]), res[-1]


def _exchange_chips_wait(sems, srcs, lands, after, name):
    n = len(srcs)
    ncp = 3 * n

    def body(*refs):
        src_refs, land_refs = refs[:n], refs[n:2 * n]
        sem_refs = refs[2 * n:2 * n + 2 * ncp]
        x, y, c, chips = _place()
        for i in range(n):
            for k in range(3):
                j = 3 * i + k
                cp = _chip_copy(src_refs[i], land_refs[i], sem_refs[j], sem_refs[ncp + j], k, chips, c,
                                2 * chips[k][0] + chips[k][1])
                cp.wait_send()
                cp.wait_recv()

    hbm = [pltpu.HBM(a.shape, a.dtype) for a in srcs]
    res = pl.pallas_call(
        body, name=name, out_shape=(*hbm, *hbm),
        in_specs=[_HBM] * (2 * n) + [_SEM] * (2 * ncp) + [_ANY], out_specs=tuple([_HBM] * (2 * n)),
        input_output_aliases={i: i for i in range(2 * n)},
        compiler_params=pltpu.CompilerParams(has_side_effects=_DATAFLOW),
    )(*srcs, *lands, *sems, after)
    return list(res[n:2 * n])


def _swap_start(src, after, name):
    def body(src_ref, land_ref, after_ref, src_thru, land_thru, send_sem, recv_sem, token):
        x, y, c, _ = _place()
        pltpu.make_async_remote_copy(src_ref=src_ref.at[1 - c], dst_ref=land_ref, send_sem=send_sem,
                                     recv_sem=recv_sem, device_id=(x, y, 1 - c), device_id_type=_MESH).start()
        token[...] = jnp.zeros_like(token)

    land = pltpu.with_memory_space_constraint(lax.empty(src.shape[1:], src.dtype), pltpu.HBM)
    res = pl.pallas_call(
        body, name=name,
        out_shape=(pltpu.HBM(src.shape, src.dtype), pltpu.HBM(land.shape, land.dtype),
                   pltpu.SemaphoreType.DMA(()), pltpu.SemaphoreType.DMA(()), jax.ShapeDtypeStruct((8, LANES), F32)),
        in_specs=[_HBM, _HBM, _ANY],
        out_specs=(_HBM, _HBM, _SEM, _SEM, pl.BlockSpec(memory_space=pltpu.VMEM)),
        input_output_aliases={0: 0, 1: 1},
        compiler_params=pltpu.CompilerParams(has_side_effects=_DATAFLOW),
    )(pltpu.with_memory_space_constraint(src, pltpu.HBM), land, after)
    return [res[2], res[3]], res[0], res[1], res[-1]


def _swap_wait(sems, src, land, after, name):
    def body(src_ref, land_ref, send_sem, recv_sem, after_ref, src_out, land_out):
        x, y, c, _ = _place()
        cp = pltpu.make_async_remote_copy(src_ref=src_ref.at[1 - c], dst_ref=land_ref, send_sem=send_sem,
                                          recv_sem=recv_sem, device_id=(x, y, 1 - c), device_id_type=_MESH)
        cp.wait_send()
        cp.wait_recv()

    res = pl.pallas_call(
        body, name=name, out_shape=(pltpu.HBM(src.shape, src.dtype), pltpu.HBM(land.shape, land.dtype)),
        in_specs=[_HBM, _HBM, _SEM, _SEM, _ANY], out_specs=(_HBM, _HBM),
        input_output_aliases={0: 0, 1: 1},
        compiler_params=pltpu.CompilerParams(has_side_effects=_DATAFLOW),
    )(src, land, *sems, after)
    return res[0], res[1]


def _peer_copy(src_ref, land_ref, send_sem, recv_sem, k, place, land):
    x, y, c = place
    peer = (1 - x if k & 4 else x, 1 - y if k & 2 else y, 1 - c if k & 1 else c)
    return pltpu.make_async_remote_copy(
        src_ref=src_ref, dst_ref=land_ref.at[land], send_sem=send_sem, recv_sem=recv_sem,
        device_id=peer, device_id_type=_MESH)


def _gather_start(x_shard, after, name):
    npeer = N_DEV - 1

    def body(x_ref, land_ref, after_ref, x_thru, land_thru, *rest):
        sems, token = rest[:2 * npeer], rest[-1]
        x, y, c, _ = _place()
        for k in range(1, N_DEV):
            _peer_copy(x_ref, land_ref, sems[k - 1], sems[npeer + k - 1], k, (x, y, c), 4 * x + 2 * y + c).start()
        token[...] = jnp.zeros_like(token)

    land = pltpu.with_memory_space_constraint(lax.empty((N_DEV,) + tuple(x_shard.shape), x_shard.dtype), pltpu.HBM)
    res = pl.pallas_call(
        body, name=name,
        out_shape=(pltpu.HBM(x_shard.shape, x_shard.dtype), pltpu.HBM(land.shape, land.dtype),
                   *([pltpu.SemaphoreType.DMA(())] * (2 * npeer)), jax.ShapeDtypeStruct((8, LANES), F32)),
        in_specs=[_HBM, _HBM, _ANY],
        out_specs=(_HBM, _HBM, *([_SEM] * (2 * npeer)), pl.BlockSpec(memory_space=pltpu.VMEM)),
        input_output_aliases={0: 0, 1: 1},
        compiler_params=pltpu.CompilerParams(has_side_effects=_DATAFLOW),
    )(pltpu.with_memory_space_constraint(x_shard, pltpu.HBM), land, after)
    return list(res[2:2 + 2 * npeer]), res[0], res[1], res[-1]


def _gather_wait(sems, src, land, after, name):
    npeer = N_DEV - 1

    def body(x_ref, land_ref, *rest):
        sem_refs = rest[:2 * npeer]
        x, y, c, _ = _place()
        for k in range(1, N_DEV):
            peer_index = (4 * x + 2 * y + c) ^ k
            cp = _peer_copy(x_ref, land_ref, sem_refs[k - 1], sem_refs[npeer + k - 1], k, (x, y, c), peer_index)
            cp.wait_send()
            cp.wait_recv()

    res = pl.pallas_call(
        body, name=name, out_shape=(pltpu.HBM(src.shape, src.dtype), pltpu.HBM(land.shape, land.dtype)),
        in_specs=[_HBM, _HBM] + [_SEM] * (2 * npeer) + [_ANY], out_specs=(_HBM, _HBM),
        input_output_aliases={0: 0, 1: 1},
        compiler_params=pltpu.CompilerParams(has_side_effects=_DATAFLOW),
    )(src, land, *sems, after)
    return res[1]


def _prenorm_inproj(x2, w, wt_qkv, b_qkv, wt_f, b_f):
    t = x2.shape[0]
    tm = min(512, t)
    n = wt_qkv.shape[0]
    tn = D_MODEL

    def body(x_ref, w_ref, wq_ref, bq_ref, wf_ref, bf_ref, h_ref, qkv_ref, zf_ref):
        x = x_ref[...]
        r = lax.rsqrt(jnp.mean(x * x, axis=-1, keepdims=True) + NORM_EPS)
        h = (x * r * w_ref[...]).astype(BF16)
        h_ref[...] = h
        for j in range(n // tn):
            cols = slice(j * tn, (j + 1) * tn)
            qkv_ref[:, cols] = (_dot_nt(h, wq_ref[cols, :]) + bq_ref[:, cols]).astype(BF16)
        zf_ref[...] = _dot_nt(h, wf_ref[...]) + bf_ref[...]

    row = lambda c: pl.BlockSpec((tm, c), lambda i: (i, 0))
    whole = lambda a: pl.BlockSpec(a.shape, lambda i: (0, 0))
    return pl.pallas_call(
        body, name="prenorm_inproj_qkv", grid=(t // tm,),
        in_specs=[row(D_MODEL), whole(w), whole(wt_qkv), whole(b_qkv), whole(wt_f), whole(b_f)],
        out_specs=[row(D_MODEL), row(n), row(LANES)],
        out_shape=[jax.ShapeDtypeStruct((t, D_MODEL), BF16), jax.ShapeDtypeStruct((t, n), BF16),
                   jax.ShapeDtypeStruct((t, LANES), F32)],
        compiler_params=_cparams(("parallel",), vmem_mb=48),
    )(x2, w, wt_qkv, b_qkv, wt_f, b_f)


def _mm_bias(a, bt, bias, out_dtype, name):
    m, k = a.shape
    n = bt.shape[0]
    tm = min(1024, m)
    tn = min(1024, n)

    def body(a_ref, bt_ref, bias_ref, o_ref):
        aa = a_ref[...]
        for j in range(n // tn):
            cols = slice(j * tn, (j + 1) * tn)
            o_ref[:, cols] = (_dot_nt(aa, bt_ref[cols, :]) + bias_ref[:, cols]).astype(o_ref.dtype)

    return pl.pallas_call(
        body, name=name, grid=(m // tm,),
        in_specs=[pl.BlockSpec((tm, k), lambda i: (i, 0)),
                  pl.BlockSpec((n, k), lambda i: (0, 0), pipeline_mode=pl.Buffered(1)),
                  pl.BlockSpec((1, n), lambda i: (0, 0))],
        out_specs=pl.BlockSpec((tm, n), lambda i: (i, 0)),
        out_shape=jax.ShapeDtypeStruct((m, n), out_dtype),
        compiler_params=_cparams(("parallel",), vmem_mb=48),
    )(a, bt, bias)


def _mm_tn(a, b, name, after=None):
    t, m = a.shape
    n = b.shape[1]
    tm = min(1024, m)
    tk = min(2048, t)
    deps = [] if after is None else [after]

    def body(a_ref, b_ref, *refs):
        o_ref, s_ref = refs[len(deps):]
        kk = pl.program_id(1)

        @pl.when(kk == 0)
        def _():
            o_ref[...] = jnp.zeros_like(o_ref)
            s_ref[...] = jnp.zeros_like(s_ref)

        aa = a_ref[...]
        o_ref[...] += _dot_tn(aa, b_ref[...])
        s_ref[0:1, :] += jnp.sum(aa.astype(F32), axis=0, keepdims=True)

    return pl.pallas_call(
        body, name=name, grid=(m // tm, t // tk),
        in_specs=[pl.BlockSpec((tk, tm), lambda i, kk: (kk, i)), pl.BlockSpec((tk, n), lambda i, kk: (kk, 0))]
        + [pl.BlockSpec(d.shape, lambda i, kk: (0, 0)) for d in deps],
        out_specs=[pl.BlockSpec((tm, n), lambda i, kk: (i, 0)), pl.BlockSpec((8, tm), lambda i, kk: (0, i))],
        out_shape=[jax.ShapeDtypeStruct((m, n), F32), jax.ShapeDtypeStruct((8, m), F32)],
        compiler_params=_cparams(("parallel", "arbitrary"), vmem_mb=48),
    )(a, b, *deps)


def _fgate_fwd(zf3):
    b, s, _ = zf3.shape
    tb = SCAN_TILE
    nb = s // tb

    def body(z_ref, cexp_ref, crow_ref):
        tri = (_iota((tb, tb), 1) <= _iota((tb, tb), 0)).astype(BF16)
        expand = ((_iota((LANES, D_MODEL), 1) >> 6) == _iota((LANES, D_MODEL), 0)).astype(BF16)
        carry = jnp.zeros((1, LANES), F32)
        for i in range(nb):
            rows = slice(i * tb, (i + 1) * tb)
            z = z_ref[rows, :]
            lf = jnp.minimum(z, 0.0) - jnp.log1p(jnp.exp(-jnp.abs(z)))
            cb = sum(_dot(tri, part) for part in _split3(lf)) + carry
            carry = cb[tb - 1:tb, :]
            cexp_ref[rows, :] = sum(_dot(part, expand) for part in _split3(cb))
            crow_ref[:, rows] = cb.T[0:HEADS, :]

    return pl.pallas_call(
        body, name="fgate_fwd", grid=(b,),
        in_specs=[pl.BlockSpec((None, s, LANES), lambda i: (i, 0, 0))],
        out_specs=[pl.BlockSpec((None, s, D_MODEL), lambda i: (i, 0, 0)),
                   pl.BlockSpec((None, HEADS, s), lambda i: (i, 0, 0))],
        out_shape=[jax.ShapeDtypeStruct((b, s, D_MODEL), F32), jax.ShapeDtypeStruct((b, HEADS, s), F32)],
        compiler_params=_cparams(("parallel",)),
    )(zf3)


def _fgate_bwd(dc3, zf3):
    b, s, _ = zf3.shape
    tb = SCAN_TILE
    nb = s // tb

    def body(dc_ref, z_ref, o_ref):
        tri = (_iota((tb, tb), 1) >= _iota((tb, tb), 0)).astype(BF16)
        carry = jnp.zeros((1, LANES), F32)
        for i in reversed(range(nb)):
            rows = slice(i * tb, (i + 1) * tb)
            dlf = sum(_dot(tri, part) for part in _split3(dc_ref[rows, :])) + carry
            carry = dlf[0:1, :]
            o_ref[rows, :] = (dlf * _sigmoid(-z_ref[rows, :])).astype(BF16)

    return pl.pallas_call(
        body, name="fgate_bwd", grid=(b,),
        in_specs=[pl.BlockSpec((None, s, LANES), lambda i: (i, 0, 0)),
                  pl.BlockSpec((None, s, LANES), lambda i: (i, 0, 0))],
        out_specs=pl.BlockSpec((s, LANES), lambda i: (i, 0)),
        out_shape=jax.ShapeDtypeStruct((b * s, LANES), BF16),
        compiler_params=_cparams(("parallel",)),
    )(dc3, zf3)


def _spare(hh):
    return HEAD_DIM if hh == 0 else 0


def _put_cols(tile, mine, cols, first):
    lane = _iota((1, LANES), 1)
    out = jnp.where(mine, tile, jnp.zeros((), tile.dtype))
    for j, c in enumerate(cols):
        out = jnp.where(lane == first + j, c, out)
    return out


def _put_rows(tile, mine, rows, first):
    sub = _iota((LANES, 1), 0)
    out = jnp.where(mine, tile, jnp.zeros((), tile.dtype))
    for j, r in enumerate(rows):
        out = jnp.where(sub == first + j, r, out)
    return out


def _transpose_bf16(a):
    return a.astype(F32).T.astype(BF16)


def _attn_fwd(qkv3, cexp3, crow, zrest3):
    b, s, _ = qkv3.shape
    ta = ATT_TILE_FWD
    nq = s // ta
    hd = HEAD_DIM
    crow5 = crow.reshape(b, HEAD_PAIRS, 2, nq, ta)

    def body(qkv_ref, cq_ref, ck_ref, g_ref, y_ref, lse_ref, ga_ref, kt_scr, v_scr):
        lane = _iota((1, LANES), 1)
        sub = _iota((LANES, 1), 0)
        lane_mine = (lane < hd, lane >= hd)
        sub_mine = (sub < hd, sub >= hd)
        causal = _iota((ta, ta), 0) >= _iota((ta, ta), 1)
        one = jnp.ones((), BF16)

        for kj in range(nq):
            rows = slice(kj * ta, (kj + 1) * ta)
            kt = _transpose_bf16(qkv_ref[rows, LANES:2 * LANES])
            v = qkv_ref[rows, 2 * LANES:3 * LANES]
            for hh in range(2):
                ck = list(_split3(-ck_ref[hh, kj:kj + 1, :]))
                kt_scr[hh, kj] = _put_rows(kt, sub_mine[hh], [one, one, one] + ck, _spare(hh))
                v_scr[hh, kj] = _put_cols(v, lane_mine[hh], [one], _spare(hh))

        for qi in range(nq):
            rows = slice(qi * ta, (qi + 1) * ta)
            q = qkv_ref[rows, 0:LANES] * 0.125
            cq = cq_ref[rows, :]
            qh = [_put_cols(q, lane_mine[hh], list(_split3(cq[:, hh * hd:hh * hd + 1])) + [one, one, one], _spare(hh))
                  for hh in range(2)]
            st = [(jnp.full((ta, 1), MASK_VALUE, F32), jnp.zeros((ta, LANES), F32))] * 2
            for kj in range(qi + 1):
                for hh in range(2):
                    m, acc = st[hh]
                    sc = _dot(qh[hh], kt_scr[hh, kj])
                    if kj == qi:
                        sc = jnp.where(causal, sc, MASK_VALUE)
                    mn = jnp.maximum(m, jnp.max(sc, axis=-1, keepdims=True))
                    p = jnp.exp(sc - mn).astype(BF16)
                    st[hh] = (mn, jnp.exp(m - mn) * acc + _dot(p, v_scr[hh, kj]))
            (ma, acca), (mb, accb) = st
            la = acca[:, hd:hd + 1]
            lb = accb[:, 0:1]
            y = jnp.where(lane_mine[0], acca * (1.0 / la), accb * (1.0 / lb))
            lse = jnp.where(lane_mine[0], ma + jnp.log(la), mb + jnp.log(lb)).T
            lse_ref[0, qi:qi + 1, :] = lse[0:1, :]
            lse_ref[1, qi:qi + 1, :] = lse[hd:hd + 1, :]
            y_ref[rows, :] = y
            g = g_ref[rows, :].astype(F32)
            ga_ref[rows, :] = (y * (g * _sigmoid(g))).astype(BF16)

    blk = lambda w: pl.BlockSpec((None, s, w), lambda i, p: (i, 0, p))
    rows5 = pl.BlockSpec((None, None, 2, nq, ta), lambda i, p: (i, p, 0, 0, 0))
    yatt3, lse5, ga = pl.pallas_call(
        body, name="attn_fwd", grid=(b, HEAD_PAIRS),
        in_specs=[blk(3 * LANES), blk(LANES), rows5, blk(LANES)],
        out_specs=[blk(LANES), rows5, pl.BlockSpec((s, LANES), lambda i, p: (i, p))],
        out_shape=[jax.ShapeDtypeStruct((b, s, D_MODEL), F32),
                   jax.ShapeDtypeStruct((b, HEAD_PAIRS, 2, nq, ta), F32),
                   jax.ShapeDtypeStruct((b * s, D_MODEL), BF16)],
        scratch_shapes=[pltpu.VMEM((2, nq, LANES, ta), BF16), pltpu.VMEM((2, nq, ta, LANES), BF16)],
        compiler_params=_cparams(("parallel", "parallel")),
    )(qkv3, cexp3, crow5, zrest3)
    return yatt3, lse5.reshape(b, HEADS, s), ga


def _attn_bwd(qkv3, do3, y3, lse, crow, cexp3):
    b, s, _ = qkv3.shape
    ta = ATT_TILE_BWD
    nq = s // ta
    hd = HEAD_DIM
    lse5 = lse.reshape(b, HEAD_PAIRS, 2, nq, ta)
    crow5 = crow.reshape(b, HEAD_PAIRS, 2, nq, ta)

    def body(qkv_ref, do_ref, y_ref, lse_ref, crow_ref, cexp_ref, dqkv_ref, dc_ref,
             qa_scr, doa_scr, qst_scr, dot_scr, kt_scr, vt_scr, dq_scr, rs_scr):
        pair = pl.program_id(1)
        lane = _iota((1, LANES), 1)
        sub = _iota((LANES, 1), 0)
        lane_mine = (lane < hd, lane >= hd)
        sub_mine = (sub < hd, sub >= hd)
        causal = _iota((ta, ta), 0) >= _iota((ta, ta), 1)
        one = jnp.ones((), BF16)
        zero = jnp.zeros((), BF16)

        @pl.when(pair == 0)
        def _():
            dc_ref[...] = jnp.zeros_like(dc_ref)

        for i in range(nq):
            rows = slice(i * ta, (i + 1) * ta)
            qs = qkv_ref[rows, 0:LANES] * 0.125
            qst = _transpose_bf16(qs)
            kt = _transpose_bf16(qkv_ref[rows, LANES:2 * LANES])
            vt = _transpose_bf16(qkv_ref[rows, 2 * LANES:3 * LANES])
            do = do_ref[rows, :]
            dof = do.astype(F32)
            dot = dof.T.astype(BF16)
            pr = y_ref[rows, :] * dof
            cq = cexp_ref[rows, :]
            lse_c = jnp.where(sub == 0, lse_ref[0, i:i + 1, :],
                              jnp.where(sub == 1, lse_ref[1, i:i + 1, :], 0.0)).T
            for hh in range(2):
                sp = _spare(hh)
                dsum = jnp.sum(jnp.where(lane_mine[hh], pr, 0.0), axis=-1, keepdims=True)
                bias = cq[:, hh * hd:hh * hd + 1] - lse_c[:, hh:hh + 1]
                qa_scr[hh, i] = _put_cols(qs, lane_mine[hh], list(_split3(bias)) + [one, one, one], sp)
                doa_scr[hh, i] = _put_cols(do, lane_mine[hh], list(_split3(-dsum)), sp)
                qst_scr[hh, i] = jnp.where(sub_mine[hh], qst, zero)
                dot_scr[hh, i] = jnp.where(sub_mine[hh], dot, zero)
                ck = list(_split3(-crow_ref[hh, i:i + 1, :]))
                kt_scr[hh, i] = _put_rows(kt, sub_mine[hh], [one, one, one] + ck, sp)
                vt_scr[hh, i] = _put_rows(vt, sub_mine[hh], [one, one, one], sp)
            dq_scr[i] = jnp.zeros((ta, LANES), F32)
            rs_scr[i] = jnp.zeros((ta, LANES), F32)

        for kj in range(nq):
            krows = slice(kj * ta, (kj + 1) * ta)
            k = qkv_ref[krows, LANES:2 * LANES]
            km = (jnp.where(lane_mine[0], k, zero), jnp.where(lane_mine[1], k, zero))
            dkt = jnp.zeros((LANES, ta), F32)
            dvt = jnp.zeros((LANES, ta), F32)
            dcp = [jnp.zeros((8, ta), F32), jnp.zeros((8, ta), F32)]
            for qi in range(kj, nq):
                dq = jnp.zeros((ta, LANES), F32)
                rs = []
                for hh in range(2):
                    sc = _dot(qa_scr[hh, qi], kt_scr[hh, kj])
                    if qi == kj:
                        sc = jnp.where(causal, sc, MASK_VALUE)
                    p = jnp.exp(sc)
                    dsf = p * _dot(doa_scr[hh, qi], vt_scr[hh, kj])
                    dcp[hh] = dcp[hh] + jnp.sum(dsf.reshape(ta // 8, 8, ta), axis=0)
                    rs.append(jnp.sum(dsf, axis=-1, keepdims=True))
                    ds = dsf.astype(BF16)
                    dq = dq + _dot(ds, km[hh])
                    dkt = dkt + _dot(qst_scr[hh, qi], ds)
                    dvt = dvt + _dot(dot_scr[hh, qi], p.astype(BF16))
                dq_scr[qi] += dq
                rs_scr[qi] += jnp.where(lane == 0, rs[0], jnp.where(lane == 1, rs[1], 0.0))
            dqkv_ref[krows, LANES:2 * LANES] = dkt.T.astype(BF16)
            dqkv_ref[krows, 2 * LANES:3 * LANES] = dvt.T.astype(BF16)
            dca = jnp.sum(dcp[0], axis=0, keepdims=True)
            dcb = jnp.sum(dcp[1], axis=0, keepdims=True)
            dcs = jnp.where(sub == 0, dca, jnp.where(sub == 1, dcb, 0.0)).T
            dc_ref[krows, :] += (jnp.where(lane == 2 * pair, -dcs[:, 0:1], 0.0)
                                 + jnp.where(lane == 2 * pair + 1, -dcs[:, 1:2], 0.0))
        for qi in range(nq):
            rows = slice(qi * ta, (qi + 1) * ta)
            dqkv_ref[rows, 0:LANES] = (dq_scr[qi] * 0.125).astype(BF16)
            rq = rs_scr[qi]
            dc_ref[rows, :] += (jnp.where(lane == 2 * pair, rq[:, 0:1], 0.0)
                                + jnp.where(lane == 2 * pair + 1, rq[:, 1:2], 0.0))

    blk = lambda w: pl.BlockSpec((None, s, w), lambda i, p: (i, 0, p))
    rows5 = pl.BlockSpec((None, None, 2, nq, ta), lambda i, p: (i, p, 0, 0, 0))
    by_rows = lambda: pltpu.VMEM((2, nq, ta, LANES), BF16)
    by_cols = lambda: pltpu.VMEM((2, nq, LANES, ta), BF16)
    return pl.pallas_call(
        body, name="attn_bwd", grid=(b, HEAD_PAIRS),
        in_specs=[blk(3 * LANES), blk(LANES), blk(LANES), rows5, rows5, blk(LANES)],
        out_specs=[pl.BlockSpec((s, 3 * LANES), lambda i, p: (i, p)),
                   pl.BlockSpec((None, s, LANES), lambda i, p: (i, 0, 0))],
        out_shape=[jax.ShapeDtypeStruct((b * s, 3 * D_MODEL), BF16), jax.ShapeDtypeStruct((b, s, LANES), F32)],
        scratch_shapes=[by_rows(), by_rows(), by_cols(), by_cols(), by_cols(), by_cols(),
                        pltpu.VMEM((nq, ta, LANES), F32), pltpu.VMEM((nq, ta, LANES), F32)],
        compiler_params=_cparams(("parallel", "arbitrary")),
    )(qkv3, do3, y3, lse5, crow5, cexp3)


def _shifted(v, ks, rows, s):
    low = rows[0:8, :]
    out = []
    for k in ks:
        r = pltpu.roll(v, k % s, 0)
        if k > 0:
            out.append(jnp.concatenate([jnp.where(low >= k, r[0:8, :], 0.0), r[8:, :]], axis=0))
        else:
            out.append(jnp.concatenate([r[:s - 8, :], jnp.where(low < 8 + k, r[s - 8:, :], 0.0)], axis=0))
    return out


def _rnn_common(xr, cw_ref, cb_ref, bda_ref, bdx_ref, ba_ref, bx_ref, lam_ref, s):
    rows = _iota((s, LANES), 0)
    x1, x2, x3 = _shifted(xr, (1, 2, 3), rows, s)
    xc = cb_ref[...] + cw_ref[0:1, :] * x3
    xc = xc + cw_ref[1:2, :] * x2
    xc = xc + cw_ref[2:3, :] * x1
    xc = xc + cw_ref[3:4, :] * xr
    xcb = xc.astype(BF16)
    r = _sigmoid(_dot(xcb, bda_ref[...]) + ba_ref[...])
    i = _sigmoid(_dot(xcb, bdx_ref[...]) + bx_ref[...])
    sp = _softplus(-lam_ref[...])
    log_a = (-RG_C * r) * sp
    a = jnp.exp(log_a)
    a2 = a * a
    sq = jnp.sqrt(jnp.maximum(_one_minus_exp(log_a + log_a, a2), 0.0))
    return rows, (x1, x2, x3), xc, xcb, r, i, sp, a, a2, sq


def _scan_down(a, u, rows, s, s1, s2):
    low = rows & 7
    for sh in (1, 2, 4):
        keep = low >= sh
        u = u + a * jnp.where(keep, pltpu.roll(u, sh, 0), 0.0)
        a = a * jnp.where(keep, pltpu.roll(a, sh, 0), 1.0)
    ng = s // 8
    s1[...] = a
    s2[...] = u
    at = s1[pl.ds(7, ng, stride=8), :]
    ut = s2[pl.ds(7, ng, stride=8), :]
    grow = _iota((ng, LANES), 0)
    sh = 1
    while sh < ng:
        keep = grow >= sh
        ut = ut + at * jnp.where(keep, pltpu.roll(ut, sh, 0), 0.0)
        if sh * 2 < ng:
            at = at * jnp.where(keep, pltpu.roll(at, sh, 0), 1.0)
        sh *= 2
    h_in = jnp.where(grow >= 1, pltpu.roll(ut, 1, 0), 0.0)
    for k in range(8):
        s1[pl.ds(k, ng, stride=8), :] = h_in
    return u + a * s1[...]


def _scan_up(a, g, rows, s, s1, s2):
    low = rows & 7
    for sh in (1, 2, 4):
        keep = low < 8 - sh
        g = g + a * jnp.where(keep, pltpu.roll(g, s - sh, 0), 0.0)
        a = a * jnp.where(keep, pltpu.roll(a, s - sh, 0), 1.0)
    ng = s // 8
    s1[...] = a
    s2[...] = g
    at = s1[pl.ds(0, ng, stride=8), :]
    gt = s2[pl.ds(0, ng, stride=8), :]
    grow = _iota((ng, LANES), 0)
    sh = 1
    while sh < ng:
        keep = grow < ng - sh
        gt = gt + at * jnp.where(keep, pltpu.roll(gt, ng - sh, 0), 0.0)
        if sh * 2 < ng:
            at = at * jnp.where(keep, pltpu.roll(at, ng - sh, 0), 1.0)
        sh *= 2
    g_in = jnp.where(grow < ng - 1, pltpu.roll(gt, ng - 1, 0), 0.0)
    for k in range(8):
        s1[pl.ds(k, ng, stride=8), :] = g_in
    return g + a * s1[...]


def _rnn_specs(s):
    blk = lambda off: pl.BlockSpec((None, s, LANES), lambda cb, i: (i, 0, off + cb))
    vec = lambda r: pl.BlockSpec((r, LANES), lambda cb, i: (0, cb))
    mat = pl.BlockSpec((None, LANES, LANES), lambda cb, i: (cb, 0, 0))
    return blk, vec, mat


def _rnn_fwd(zrest3, conv_w, conv_b, bda, bdx, ba, bx, lam):
    b, s, _ = zrest3.shape

    def body(xr_ref, g_ref, cw_ref, cb_ref, bda_ref, bdx_ref, ba_ref, bx_ref, lam_ref, h_ref, gr_ref, s1, s2):
        xr = xr_ref[...].astype(F32)
        rows, _, xc, _, _, i, _, a, _, sq = _rnn_common(
            xr, cw_ref, cb_ref, bda_ref, bdx_ref, ba_ref, bx_ref, lam_ref, s)
        h = _scan_down(a, sq * (i * xc), rows, s, s1, s2)
        h_ref[...] = h
        g = g_ref[...].astype(F32)
        gr_ref[...] = (h * (g * _sigmoid(g))).astype(BF16)

    blk, vec, mat = _rnn_specs(s)
    return pl.pallas_call(
        body, name="rnn_fwd", grid=(N_CBLK, b),
        in_specs=[blk(N_CBLK), blk(2 * N_CBLK), vec(CONV_W), vec(1), mat, mat, vec(1), vec(1), vec(1)],
        out_specs=[blk(0), pl.BlockSpec((s, LANES), lambda cb, i: (i, cb))],
        out_shape=[jax.ShapeDtypeStruct((b, s, D_MODEL), F32), jax.ShapeDtypeStruct((b * s, D_MODEL), BF16)],
        scratch_shapes=[pltpu.VMEM((s, LANES), F32), pltpu.VMEM((s, LANES), F32)],
        compiler_params=_cparams(("parallel", "parallel")),
    )(zrest3, zrest3, conv_w, conv_b, bda, bdx, ba, bx, lam)


def _rnn_bwd(zrest3, h3, dh3, conv_w, conv_b, bda, bdx, ba, bx, lam):
    b, s, _ = zrest3.shape

    def body(xr_ref, h_ref, dh_ref, cw_ref, cb_ref, bda_ref, bdx_ref, ba_ref, bx_ref, lam_ref,
             dxr_ref, pv_ref, dbd_ref, s1, s2):
        @pl.when(pl.program_id(1) == 0)
        def _():
            pv_ref[...] = jnp.zeros_like(pv_ref)
            dbd_ref[...] = jnp.zeros_like(dbd_ref)

        xr = xr_ref[...].astype(F32)
        rows, (x1, x2, x3), xc, xcb, r, i, sp, a, a2, sq = _rnn_common(
            xr, cw_ref, cb_ref, bda_ref, bdx_ref, ba_ref, bx_ref, lam_ref, s)
        (a_next,) = _shifted(a, (-1,), rows, s)
        g = _scan_up(a_next, dh_ref[...], rows, s, s1, s2)
        (hp,) = _shifted(h_ref[...], (1,), rows, s)
        da = g * hp
        dsq = g * (i * xc)
        di = g * (sq * xc)
        dxc = g * (sq * i)
        dlog = da * a - dsq * (a2 / sq)
        dr = dlog * (-RG_C * sp)
        dpr = dr * (r * (1.0 - r))
        dpi = di * (i * (1.0 - i))
        dprb = dpr.astype(BF16)
        dpib = dpi.astype(BF16)
        dxc = dxc + _dot_nt(dprb, bda_ref[...]) + _dot_nt(dpib, bdx_ref[...])

        up1, up2, up3 = _shifted(dxc, (-1, -2, -3), rows, s)
        dxr = cw_ref[3:4, :] * dxc + cw_ref[2:3, :] * up1 + cw_ref[1:2, :] * up2 + cw_ref[0:1, :] * up3
        dxr_ref[...] = dxr.astype(BF16)

        def colsum(v):
            return jnp.sum(v, axis=0, keepdims=True)

        pv_ref[0:1, :] += colsum(dxc * x3)
        pv_ref[1:2, :] += colsum(dxc * x2)
        pv_ref[2:3, :] += colsum(dxc * x1)
        pv_ref[3:4, :] += colsum(dxc * xr)
        pv_ref[4:5, :] += colsum(dxc)
        pv_ref[5:6, :] += colsum(dpr)
        pv_ref[6:7, :] += colsum(dpi)
        pv_ref[7:8, :] += colsum(dlog * r) * (RG_C * _sigmoid(-lam_ref[...]))
        dbd_ref[0] += _dot_tn(xcb, dprb)
        dbd_ref[1] += _dot_tn(xcb, dpib)

    blk, vec, mat = _rnn_specs(s)
    hblk = pl.BlockSpec((None, s, LANES), lambda cb, i: (i, 0, cb))
    return pl.pallas_call(
        body, name="rnn_bwd", grid=(N_CBLK, b),
        in_specs=[blk(N_CBLK), hblk, hblk, vec(CONV_W), vec(1), mat, mat, vec(1), vec(1), vec(1)],
        out_specs=[pl.BlockSpec((s, LANES), lambda cb, i: (i, cb)), pl.BlockSpec((8, LANES), lambda cb, i: (0, cb)),
                   pl.BlockSpec((None, 2, LANES, LANES), lambda cb, i: (cb, 0, 0, 0))],
        out_shape=[jax.ShapeDtypeStruct((b * s, D_MODEL), BF16), jax.ShapeDtypeStruct((8, D_MODEL), F32),
                   jax.ShapeDtypeStruct((N_CBLK, 2, LANES, LANES), F32)],
        scratch_shapes=[pltpu.VMEM((s, LANES), F32), pltpu.VMEM((s, LANES), F32)],
        compiler_params=_cparams(("parallel", "arbitrary")),
    )(zrest3, h3, dh3, conv_w, conv_b, bda, bdx, ba, bx, lam)


def _branch_merge(ga, gr, wa, wr, zrest):
    t = ga.shape[0]
    tm = min(512, t)
    tn = D_MODEL

    def body(ga_ref, gr_ref, wa_ref, wr_ref, mga_ref, mgr_ref, ya_ref, yr_ref, m_ref):
        ya = _dot(ga_ref[...], wa_ref[...])
        yr = _dot(gr_ref[...], wr_ref[...])
        ya_ref[...] = ya.astype(BF16)
        yr_ref[...] = yr.astype(BF16)
        m_ref[...] = (_sigmoid(mga_ref[...].astype(F32)) * ya + _sigmoid(mgr_ref[...].astype(F32)) * yr).astype(BF16)

    nj = D_MODEL // tn
    act = pl.BlockSpec((tm, D_MODEL), lambda i, j: (i, 0))
    wgt = pl.BlockSpec((D_MODEL, tn), lambda i, j: (0, j))
    out = pl.BlockSpec((tm, tn), lambda i, j: (i, j))
    return pl.pallas_call(
        body, name="branch_merge", grid=(t // tm, nj),
        in_specs=[act, act, wgt, wgt, pl.BlockSpec((tm, tn), lambda i, j: (i, 3 * nj + j)),
                  pl.BlockSpec((tm, tn), lambda i, j: (i, 4 * nj + j))],
        out_specs=[out, out, out],
        out_shape=[jax.ShapeDtypeStruct((t, D_MODEL), BF16), jax.ShapeDtypeStruct((t, D_MODEL), BF16),
                   jax.ShapeDtypeStruct((t, D_MODEL), BF16)],
        compiler_params=_cparams(("parallel", "parallel")),
    )(ga, gr, wa, wr, zrest, zrest)


def _out_loss(m, wout, x2, tgt2, wpost):
    t = m.shape[0]
    tm = min(512, t)

    def body(m_ref, w_ref, x_ref, t_ref, wp_ref, dy_ref, do_ref, acc_ref):
        @pl.when(pl.program_id(0) == 0)
        def _():
            acc_ref[...] = jnp.zeros_like(acc_ref)

        o = _dot(m_ref[...], w_ref[...])
        r2 = lax.rsqrt(jnp.mean(o * o, axis=-1, keepdims=True) + NORM_EPS)
        n = o * r2
        wp = wp_ref[...]
        err = (x_ref[...] + n * wp) - t_ref[...]
        dy = err * (1.0 / D_MODEL)
        dn = dy * wp
        do = r2 * (dn - n * jnp.mean(dn * n, axis=-1, keepdims=True))
        dy_ref[...] = dy
        do_ref[...] = do.astype(BF16)
        acc_ref[0:1, :] += jnp.sum(dy * n, axis=0, keepdims=True)
        acc_ref[1:2, :] += jnp.sum(err * err, axis=0, keepdims=True)

    row = pl.BlockSpec((tm, D_MODEL), lambda i: (i, 0))
    return pl.pallas_call(
        body, name="out_loss", grid=(t // tm,),
        in_specs=[row, pl.BlockSpec((D_MODEL, D_MODEL), lambda i: (0, 0)), row, row,
                  pl.BlockSpec((1, D_MODEL), lambda i: (0, 0))],
        out_specs=[row, row, pl.BlockSpec((8, D_MODEL), lambda i: (0, 0))],
        out_shape=[jax.ShapeDtypeStruct((t, D_MODEL), F32), jax.ShapeDtypeStruct((t, D_MODEL), BF16),
                   jax.ShapeDtypeStruct((8, D_MODEL), F32)],
        compiler_params=_cparams(("arbitrary",)),
    )(m, wout, x2, tgt2, wpost)


def _merge_bwd(do, wout, zrest, ya, yr):
    t = do.shape[0]
    tm = min(512, t)
    tn = D_MODEL
    nj = D_MODEL // tn

    def body(do_ref, w_ref, mga_ref, mgr_ref, ya_ref, yr_ref, dya_ref, dyr_ref, dmga_ref, dmgr_ref):
        dm = _dot_nt(do_ref[...], w_ref[...])
        sa = _sigmoid(mga_ref[...].astype(F32))
        sr = _sigmoid(mgr_ref[...].astype(F32))
        dya_ref[...] = (dm * sa).astype(BF16)
        dyr_ref[...] = (dm * sr).astype(BF16)
        dmga_ref[...] = (dm * ya_ref[...].astype(F32) * (sa * (1.0 - sa))).astype(BF16)
        dmgr_ref[...] = (dm * yr_ref[...].astype(F32) * (sr * (1.0 - sr))).astype(BF16)

    out = pl.BlockSpec((tm, tn), lambda i, j: (i, j))
    bf = jax.ShapeDtypeStruct((t, D_MODEL), BF16)
    return pl.pallas_call(
        body, name="merge_bwd", grid=(t // tm, nj),
        in_specs=[pl.BlockSpec((tm, D_MODEL), lambda i, j: (i, 0)), pl.BlockSpec((tn, D_MODEL), lambda i, j: (j, 0)),
                  pl.BlockSpec((tm, tn), lambda i, j: (i, 3 * nj + j)),
                  pl.BlockSpec((tm, tn), lambda i, j: (i, 4 * nj + j)), out, out],
        out_specs=[out, out, out, out],
        out_shape=[bf, bf, bf, bf],
        compiler_params=_cparams(("parallel", "parallel")),
    )(do, wout, zrest, zrest, ya, yr)


def _branch_bwd(dya, dyr, wa, wr, zrest, yatt, ylru):
    t = dya.shape[0]
    tm = min(512, t)
    tn = D_MODEL
    nj = D_MODEL // tn

    def body(dya_ref, dyr_ref, wa_ref, wr_ref, ga_ref, gr_ref, ya_ref, yl_ref,
             dyatt_ref, dga_ref, dyl_ref, dgr_ref):
        dga = _dot_nt(dya_ref[...], wa_ref[...])
        dgr = _dot_nt(dyr_ref[...], wr_ref[...])
        g = ga_ref[...].astype(F32)
        sg = _sigmoid(g)
        dyatt_ref[...] = (dga * (g * sg)).astype(BF16)
        dga_ref[...] = (dga * ya_ref[...] * (sg * (1.0 + g * (1.0 - sg)))).astype(BF16)
        g = gr_ref[...].astype(F32)
        sg = _sigmoid(g)
        dyl_ref[...] = dgr * (g * sg)
        dgr_ref[...] = (dgr * yl_ref[...] * (sg * (1.0 + g * (1.0 - sg)))).astype(BF16)

    act = pl.BlockSpec((tm, D_MODEL), lambda i, j: (i, 0))
    wgt = pl.BlockSpec((tn, D_MODEL), lambda i, j: (j, 0))
    out = pl.BlockSpec((tm, tn), lambda i, j: (i, j))
    bf = jax.ShapeDtypeStruct((t, D_MODEL), BF16)
    return pl.pallas_call(
        body, name="branch_bwd", grid=(t // tm, nj),
        in_specs=[act, act, wgt, wgt, pl.BlockSpec((tm, tn), lambda i, j: (i, j)),
                  pl.BlockSpec((tm, tn), lambda i, j: (i, 2 * nj + j)), out, out],
        out_specs=[out, out, out, out],
        out_shape=[bf, bf, jax.ShapeDtypeStruct((t, D_MODEL), F32), bf],
        compiler_params=_cparams(("parallel", "parallel")),
    )(dya, dyr, wa, wr, zrest, zrest, yatt, ylru)


def _dh_final(parts, after, x2, dy, wpre):
    t = x2.shape[0]
    tm = min(256, t)
    np_ = len(parts)

    def body(*refs):
        x_ref, dy_ref, w_ref = refs[2 * np_ + 1:2 * np_ + 4]
        gx_ref, pw_ref = refs[2 * np_ + 4:]

        @pl.when(pl.program_id(0) == 0)
        def _():
            pw_ref[...] = jnp.zeros_like(pw_ref)

        dh = _dot(refs[0][...], refs[np_][...])
        for p in range(1, np_):
            dh = dh + _dot(refs[p][...], refs[np_ + p][...])
        x = x_ref[...]
        r = lax.rsqrt(jnp.mean(x * x, axis=-1, keepdims=True) + NORM_EPS)
        xn = x * r
        dxn = dh * w_ref[...]
        gx_ref[...] = r * (dxn - xn * jnp.mean(dxn * xn, axis=-1, keepdims=True)) + dy_ref[...]
        pw_ref[0:1, :] += jnp.sum(dh * xn, axis=0, keepdims=True)

    row = pl.BlockSpec((tm, D_MODEL), lambda i: (i, 0))
    in_specs = [pl.BlockSpec((tm, dz.shape[1]), lambda i: (i, 0)) for dz, _ in parts]
    in_specs += [pl.BlockSpec(w.shape, lambda i: (0, 0), pipeline_mode=pl.Buffered(1)) for _, w in parts]
    in_specs += [pl.BlockSpec(after.shape, lambda i: (0, 0)), row, row, pl.BlockSpec((1, D_MODEL), lambda i: (0, 0))]
    return pl.pallas_call(
        body, name="dh_final", grid=(t // tm,),
        in_specs=in_specs,
        out_specs=[row, pl.BlockSpec((8, D_MODEL), lambda i: (0, 0))],
        out_shape=[jax.ShapeDtypeStruct((t, D_MODEL), F32), jax.ShapeDtypeStruct((8, D_MODEL), F32)],
        compiler_params=_cparams(("arbitrary",), vmem_mb=48),
    )(*[dz for dz, _ in parts], *[w for _, w in parts], after, x2, dy, wpre)


def _adamw(w, g, m, v):
    m = ADAM_B1 * m + (1.0 - ADAM_B1) * g
    v = ADAM_B2 * v + (1.0 - ADAM_B2) * (g * g)
    m_hat = m / (1.0 - ADAM_B1 ** ADAM_STEP)
    v_hat = v / (1.0 - ADAM_B2 ** ADAM_STEP)
    delta = -ADAM_LR * (m_hat / (jnp.sqrt(v_hat) + ADAM_EPS) + ADAM_WD * w)
    return delta, m, v


def _reduce_adamw(own, parts, place, w, m, v, name):
    r, c = w.shape
    blk, nblk, at = _blocks_2d(r, c)

    def body(place_ref, own_ref, p_ref, w_ref, m_ref, v_ref, g_ref, d_ref, nm_ref, nv_ref):
        mine = place_ref[1]
        own_blk = own_ref[...]
        g = jnp.where(mine == 0, own_blk, p_ref[0].astype(F32))
        for j in range(1, N_CHIPS):
            g = g + jnp.where(mine == j, own_blk, p_ref[j].astype(F32))
        d, nm, nv = _adamw(w_ref[...], g, m_ref[...], v_ref[...])
        g_ref[...] = g
        d_ref[...] = d
        nm_ref[...] = nm
        nv_ref[...] = nv

    row = pl.BlockSpec(blk, lambda i, pr: at(i))
    sh = jax.ShapeDtypeStruct((r, c), F32)
    grid_spec = pltpu.PrefetchScalarGridSpec(
        num_scalar_prefetch=1, grid=(nblk,),
        in_specs=[row, pl.BlockSpec((N_CHIPS,) + blk, lambda i, pr: (0,) + at(i)), row, row, row],
        out_specs=[row, row, row, row])
    return pl.pallas_call(
        body, name=name, grid_spec=grid_spec, out_shape=[sh, sh, sh, sh],
        compiler_params=_cparams(("parallel",)),
    )(place, own, parts, w, m, v)


def _reduce_adamw_stacked(own, parts, place, triples, name):
    n = len(triples)
    _, r, c = triples[0][0].shape

    def body(place_ref, own_ref, p_ref, *refs):
        ins, outs = refs[:3 * n], refs[3 * n:]
        mine = place_ref[1]
        for i in range(n):
            rows = slice(i * r, (i + 1) * r)
            own_blk = own_ref[rows, :]
            g = jnp.where(mine == 0, own_blk, p_ref[0, rows, :].astype(F32))
            for j in range(1, N_CHIPS):
                g = g + jnp.where(mine == j, own_blk, p_ref[j, rows, :].astype(F32))
            d, nm, nv = _adamw(ins[3 * i][0], g, ins[3 * i + 1][0], ins[3 * i + 2][0])
            for k, val in enumerate((g, d, nm, nv)):
                outs[4 * i + k][0] = val

    whole = lambda shape: pl.BlockSpec(shape, lambda i, pr: (0,) * len(shape))
    grid_spec = pltpu.PrefetchScalarGridSpec(
        num_scalar_prefetch=1, grid=(1,),
        in_specs=[whole(own.shape), whole(parts.shape)] + [whole((1, r, c))] * (3 * n),
        out_specs=[whole((1, r, c))] * (4 * n))
    res = pl.pallas_call(
        body, name=name, grid_spec=grid_spec,
        out_shape=[jax.ShapeDtypeStruct((1, r, c), F32)] * (4 * n),
        compiler_params=_cparams(("arbitrary",)),
    )(place, own, parts, *[a for t3 in triples for a in t3])
    return [res[4 * i:4 * i + 4] for i in range(n)]


def _interleave_qkv(a):
    lead = a.shape[:-1]
    return a.reshape(lead + (3, HEAD_PAIRS, LANES)).swapaxes(-3, -2).reshape(lead + (3 * D_MODEL,))


def _deinterleave_qkv(a):
    lead = a.shape[:-1]
    return a.reshape(lead + (HEAD_PAIRS, 3, LANES)).swapaxes(-3, -2).reshape(lead + (3 * D_MODEL,))


def _interleave_rows(a):
    return a.reshape(3, HEAD_PAIRS, LANES, a.shape[1]).swapaxes(0, 1).reshape(a.shape)


def _deinterleave_rows(a):
    return a.reshape(HEAD_PAIRS, 3, LANES, a.shape[1]).swapaxes(0, 1).reshape(a.shape)


def _pack_small(pre, conv_b, rg_ba, rg_bx, lam, post, loss_row, b_in, conv_w_full, rg_wa, rg_wx):
    z = jnp.zeros((1, D_MODEL), F32)
    b_used = jnp.concatenate([b_in[:, 0:3 * D_MODEL], b_in[:, 3 * D_MODEL + HEADS:IN_TOTAL]], axis=1)
    b_f = jnp.pad(b_in[:, 3 * D_MODEL:3 * D_MODEL + HEADS], ((0, 0), (0, D_MODEL - HEADS)))
    return jnp.concatenate([
        pre, conv_b, rg_ba, rg_bx, lam, post, loss_row, z,
        b_used.reshape(9, D_MODEL), b_f, conv_w_full, z, z,
        rg_wa.reshape(64, D_MODEL), rg_wx.reshape(64, D_MODEL)], axis=0)


def _unpack_small(p):
    b_used = p[8:17].reshape(1, 9 * D_MODEL)
    b_in = jnp.concatenate([b_used[:, 0:3 * D_MODEL], p[17:18, 0:HEADS], b_used[:, 3 * D_MODEL:]], axis=1)
    return dict(pre_norm_w=p[0:1], conv_b=p[1:2], rg_ba=p[2:3], rg_bx=p[3:4], rg_lambda=p[4:5],
                post_norm_w=p[5:6], loss_row=p[6:7], b_in=b_in, conv_w_full=p[18:22],
                rg_wa=p[24:88].reshape(1, 16, 64, 64), rg_wx=p[88:152].reshape(1, 16, 64, 64))


def _reduce_small(parts, first, w, m, v, vectors):
    nvec = len(vectors)

    def body(p_ref, f_ref, w_ref, m_ref, v_ref, *refs):
        ins, outs = refs[:3 * nvec], refs[3 * nvec:]
        g = p_ref[0]
        g0 = f_ref[0, 0:1, :]
        for j in range(1, N_DEV):
            g = g + p_ref[j]
            g0 = g0 + f_ref[j, 0:1, :]
        d, nm, nv = _adamw(w_ref[...], g, m_ref[...], v_ref[...])
        for k, val in enumerate((g, d, nm, nv)):
            outs[k][...] = val
        for i in range(nvec):
            gi = g0 if i == 0 else g[i:i + 1, :]
            di, nmi, nvi = _adamw(ins[3 * i][...], gi, ins[3 * i + 1][...], ins[3 * i + 2][...])
            for k, val in enumerate((gi, di, nmi, nvi)):
                outs[4 + 4 * i + k][...] = val
        outs[-1][...] = jnp.zeros((8, LANES), F32) + (0.5 / D_MODEL) * jnp.sum(g[LOSS_ROW:LOSS_ROW + 1, :])

    sh = jax.ShapeDtypeStruct((SMALL_ROWS, D_MODEL), F32)
    vec = jax.ShapeDtypeStruct((1, D_MODEL), F32)
    res = pl.pallas_call(
        body, name="reduce_small",
        out_shape=[sh, sh, sh, sh] + [vec] * (4 * nvec) + [jax.ShapeDtypeStruct((8, LANES), F32)],
    )(parts, first, w, m, v, *[a for t3 in vectors for a in t3])
    return res[:4], [res[4 + 4 * i:8 + 4 * i] for i in range(nvec)], res[-1]


def kernel(x, pre_norm_w, w_in, b_in, conv_w, conv_b, rg_wa, rg_ba, rg_wx, rg_bx, rg_lambda, w_branch_a, w_branch_r, w_out, post_norm_w, loss_target, m_pre_norm_w, m_w_in, m_b_in, m_conv_w, m_conv_b, m_rg_wa, m_rg_ba, m_rg_wx, m_rg_bx, m_rg_lambda, m_w_branch_a, m_w_branch_r, m_w_out, m_post_norm_w, v_pre_norm_w, v_w_in, v_b_in, v_conv_w, v_conv_b, v_rg_wa, v_rg_ba, v_rg_wx, v_rg_bx, v_rg_lambda, v_w_branch_a, v_w_branch_r, v_w_out, v_post_norm_w):
    b, s, _ = x.shape
    t = b * s
    me = 4 * lax.axis_index("x") + 2 * lax.axis_index("y") + lax.axis_index("c")
    shard_rows = D_MODEL // N_DEV

    place = jnp.stack([lax.axis_index("c"), 2 * lax.axis_index("x") + lax.axis_index("y")]).astype(jnp.int32)
    w_in_all = _gather(w_in[0].T.astype(BF16), "gather_w_in")
    wt_full = w_in_all.reshape(IN_TOTAL, D_MODEL)
    conv_terms = jnp.concatenate(_split3(conv_w[0]), axis=0)
    conv_pad = jnp.pad(conv_terms, ((0, 16 - 3 * CONV_W), (0, D_MODEL - LANES)))
    sq_stack = jnp.concatenate([w_branch_a[0].astype(BF16), w_branch_r[0].astype(BF16), w_out[0].astype(BF16),
                                conv_pad], axis=0)
    sq_sems, sq_src, sq_land, sq_token = _gather_start(sq_stack, w_in_all, "gather_w_sq_start")

    w_qkv = _interleave_rows(wt_full[0:3 * D_MODEL])
    w_f = jnp.pad(wt_full[3 * D_MODEL:3 * D_MODEL + HEADS], ((0, LANES - HEADS), (0, 0)))
    w_rest = wt_full[3 * D_MODEL + HEADS:IN_USED]
    b_qkv = _interleave_qkv(b_in[:, 0:3 * D_MODEL]) + sq_token[0, 0]
    b_f = jnp.pad(b_in[:, 3 * D_MODEL:3 * D_MODEL + HEADS], ((0, 0), (0, LANES - HEADS)))
    b_rest = b_in[:, 3 * D_MODEL + HEADS:IN_USED]

    def blockdiag(w):
        w2 = w.reshape(N_CBLK, 2, HEAD_DIM, HEAD_DIM)
        zz = jnp.zeros((N_CBLK, HEAD_DIM, HEAD_DIM), w.dtype)
        top = jnp.concatenate([w2[:, 0], zz], axis=2)
        bot = jnp.concatenate([zz, w2[:, 1]], axis=2)
        return jnp.concatenate([top, bot], axis=1).astype(BF16)

    bda, bdx = blockdiag(rg_wa[0]), blockdiag(rg_wx[0])

    x2 = x.reshape(t, D_MODEL)
    tgt2 = loss_target.reshape(t, D_MODEL)
    h, qkv, zf = _prenorm_inproj(x2, pre_norm_w, w_qkv, b_qkv, w_f, b_f)
    zrest = _mm_bias(h, w_rest, b_rest, BF16, "inproj_rest")
    qkv3 = qkv.reshape(b, s, 3 * D_MODEL)
    zrest3 = zrest.reshape(b, s, 5 * D_MODEL)
    zf3 = zf.reshape(b, s, LANES)
    cexp3, crow = _fgate_fwd(zf3)
    yatt3, lse, ga = _attn_fwd(qkv3, cexp3, crow, zrest3)

    sq_all = _gather_wait(sq_sems, sq_src, sq_land, ga, "gather_w_sq_wait")
    sq_all = lax.dynamic_update_slice(sq_all, sq_stack[None], (me, 0, 0))
    wa = sq_all[:, 0:shard_rows].reshape(D_MODEL, D_MODEL)
    wr = sq_all[:, shard_rows:2 * shard_rows].reshape(D_MODEL, D_MODEL)
    wo = sq_all[:, 2 * shard_rows:3 * shard_rows].reshape(D_MODEL, D_MODEL)
    conv_all = sq_all[:, 3 * shard_rows:3 * shard_rows + 3 * CONV_W, 0:LANES].astype(F32)
    conv_all = (conv_all[:, 0:CONV_W] + conv_all[:, CONV_W:2 * CONV_W]) + conv_all[:, 2 * CONV_W:3 * CONV_W]
    conv_full = conv_all.transpose(1, 0, 2).reshape(CONV_W, D_MODEL)

    ylru3, gr = _rnn_fwd(zrest3, conv_full, conv_b, bda, bdx, rg_ba, rg_bx, rg_lambda)
    ya, yr, mm = _branch_merge(ga, gr, wa, wr, zrest)
    dy, do, acc_out = _out_loss(mm, wo, x2, tgt2, post_norm_w)

    dya, dyr, dz_mga, dz_mgr = _merge_bwd(do, wo, zrest, ya, yr)
    dyatt, dz_ga, dylru, dz_gr = _branch_bwd(dya, dyr, wa, wr, zrest, yatt3.reshape(t, D_MODEL),
                                             ylru3.reshape(t, D_MODEL))
    dz_xr, pvec, dbd = _rnn_bwd(zrest3, ylru3, dylru.reshape(b, s, D_MODEL), conv_full, conv_b, bda, bdx,
                                rg_ba, rg_bx, rg_lambda)
    dz_qkv, dc3 = _attn_bwd(qkv3, dyatt.reshape(b, s, D_MODEL), yatt3, lse, crow, cexp3)
    dz_f = _fgate_bwd(dc3, zf3)

    dw_qkv, db_qkv = _mm_tn(dz_qkv, h, "dw_qkv")
    dw_f, db_f = _mm_tn(dz_f, h, "dw_f")
    dw_parts, db_parts = [], []
    for nm, dzp in (("ga", dz_ga), ("xr", dz_xr), ("gr", dz_gr), ("mga", dz_mga), ("mgr", dz_mgr)):
        dwp, dbp = _mm_tn(dzp, h, "dw_" + nm)
        dw_parts.append(dwp)
        db_parts.append(dbp[0:1])

    zeros_tail = jnp.zeros((IN_TOTAL - IN_USED, D_MODEL), F32)
    dwt_full = jnp.concatenate([_deinterleave_rows(dw_qkv), dw_f[0:HEADS]] + dw_parts + [zeros_tail], axis=0)
    dw_in_send = dwt_full.reshape(N_CHIPS, 2, W_SHARD, D_MODEL).transpose(1, 0, 2, 3)
    swp_sems, dw_in_src, swp_land, swp_token = _swap_start(dw_in_send, db_f, "swap_dw_in_start")
    dw_a, _ = _mm_tn(ga, dya, "dw_a", after=swp_token)
    dw_r, _ = _mm_tn(gr, dyr, "dw_r", after=swp_token)
    dw_o, _ = _mm_tn(mm, do, "dw_o", after=swp_token)
    dw_in_send, sib_in = _swap_wait(swp_sems, dw_in_src, swp_land, dw_o, "swap_dw_in_wait")
    by_dest = lambda a: a.reshape(N_CHIPS, 2, shard_rows, D_MODEL).transpose(1, 0, 2, 3)
    dw_sq_send = jnp.concatenate([by_dest(dw_a), by_dest(dw_r), by_dest(dw_o)], axis=2)

    db_in_full = jnp.concatenate([_deinterleave_qkv(db_qkv[0:1]), db_f[0:1, 0:HEADS]] + db_parts
                                 + [jnp.zeros((1, IN_TOTAL - IN_USED), F32)], axis=1)
    d_rg_wa = jnp.stack([dbd[:, 0, 0:HEAD_DIM, 0:HEAD_DIM], dbd[:, 0, HEAD_DIM:, HEAD_DIM:]], axis=1)
    d_rg_wx = jnp.stack([dbd[:, 1, 0:HEAD_DIM, 0:HEAD_DIM], dbd[:, 1, HEAD_DIM:, HEAD_DIM:]], axis=1)
    small_g = _pack_small(jnp.zeros((1, D_MODEL), F32), pvec[4:5], pvec[5:6], pvec[6:7], pvec[7:8], acc_out[0:1],
                          acc_out[1:2], db_in_full, pvec[0:4], d_rg_wa, d_rg_wx)
    sm_sems, sm_src, sm_land, sm_token = _gather_start(small_g, dw_o, "gather_small_start")

    sqs_sems, dw_sq_src, sqs_land, sqs_token = _swap_start(dw_sq_send, sm_token, "swap_dw_sq_start")
    chip_in, own_in = _pair_add(dw_in_send, sib_in, place, "pair_add_in", after=sqs_token)
    dw_sq_send, sib_sq = _swap_wait(sqs_sems, dw_sq_src, sqs_land, chip_in, "swap_dw_sq_wait")
    chip_sq, own_sq = _pair_add(dw_sq_send, sib_sq, place, "pair_add_sq")
    sems, sent, lands, token = _exchange_chips_start([chip_in, chip_sq], "exchange_dw_start")

    wt = lambda lo: w_rest[lo * D_MODEL:(lo + 1) * D_MODEL]
    grad_x2, acc_pre = _dh_final(
        [(dz_qkv, w_qkv), (dz_f, w_f), (dz_ga, wt(0)), (dz_xr, wt(1)), (dz_gr, wt(2)), (dz_mga, wt(3)),
         (dz_mgr, wt(4))], token, x2, dy, pre_norm_w)
    pre_sems, pre_src, pre_land, pre_token = _gather_start(acc_pre, grad_x2, "gather_pre_start")
    recv_in, recv_sq = _exchange_chips_wait(sems, sent, lands, pre_token, "exchange_dw_wait")

    g_in, d_in, nm_in, nv_in = [a.T for a in _reduce_adamw(
        own_in, recv_in, place, w_in[0].T, m_w_in[0].T, v_w_in[0].T, "adamw_w_in")]
    sq_out = _reduce_adamw_stacked(
        own_sq, recv_sq, place,
        [(w_branch_a, m_w_branch_a, v_w_branch_a), (w_branch_r, m_w_branch_r, v_w_branch_r),
         (w_out, m_w_out, v_w_out)], "adamw_w_sq")
    pre_all = _gather_wait(pre_sems, pre_src, pre_land, sq_out[2][1], "gather_pre_wait")
    pre_all = lax.dynamic_update_slice(pre_all, acc_pre[None], (me, 0, 0))
    small_all = _gather_wait(sm_sems, sm_src, sm_land, pre_all, "gather_small_wait")
    small_all = lax.dynamic_update_slice(small_all, small_g[None], (me, 0, 0))

    def place_conv(a):
        return lax.dynamic_update_slice(jnp.zeros((CONV_W, D_MODEL), F32), a[0], (0, me * LANES))

    zrow = jnp.zeros((1, D_MODEL), F32)
    vector_names = ["pre_norm_w", "conv_b", "rg_ba", "rg_bx", "rg_lambda", "post_norm_w"]
    vectors = [(pre_norm_w, m_pre_norm_w, v_pre_norm_w), (conv_b, m_conv_b, v_conv_b), (rg_ba, m_rg_ba, v_rg_ba),
               (rg_bx, m_rg_bx, v_rg_bx), (rg_lambda, m_rg_lambda, v_rg_lambda),
               (post_norm_w, m_post_norm_w, v_post_norm_w)]
    small_w = _pack_small(zrow, zrow, zrow, zrow, zrow, zrow, zrow, b_in, place_conv(conv_w), rg_wa[0], rg_wx[0])
    small_m = _pack_small(zrow, zrow, zrow, zrow, zrow, zrow, zrow, m_b_in, place_conv(m_conv_w), m_rg_wa[0],
                          m_rg_wx[0])
    small_v = _pack_small(zrow, zrow, zrow, zrow, zrow, zrow, zrow, v_b_in, place_conv(v_conv_w), v_rg_wa[0],
                          v_rg_wx[0])
    packed, vector_out, loss_tile = _reduce_small(small_all, pre_all, small_w, small_m, small_v, vectors)
    outs_small = [_unpack_small(p) for p in packed]
    loss = loss_tile[0, 0]

    def leaf(kind, name):
        if name == "w_in":
            return (g_in, d_in, nm_in, nv_in)[kind][None]
        if name in ("w_branch_a", "w_branch_r", "w_out"):
            return sq_out[("w_branch_a", "w_branch_r", "w_out").index(name)][kind]
        if name == "conv_w":
            return lax.dynamic_slice(outs_small[kind]["conv_w_full"], (0, me * LANES), (CONV_W, LANES))[None]
        if name in vector_names:
            return vector_out[vector_names.index(name)][kind]
        return outs_small[kind][name]

    names = ["pre_norm_w", "w_in", "b_in", "conv_w", "conv_b", "rg_wa", "rg_ba", "rg_wx", "rg_bx", "rg_lambda",
             "w_branch_a", "w_branch_r", "w_out", "post_norm_w"]
    out = [loss, grad_x2.reshape(b, s, D_MODEL)]
    for kind in range(4):
        out += [leaf(kind, nm) for nm in names]
    return tuple(out)
```

```python
import jax
import jax.numpy as jnp
from jax import lax
from jax.experimental import pallas as pl
from jax.experimental.pallas import tpu as pltpu

F32 = jnp.float32
BF16 = jnp.bfloat16

N_DEV = 8
D_MODEL = 1024
HEADS = 16
HEAD_DIM = 64
HEAD_PAIRS = HEADS // 2
LANES = 128
N_CBLK = D_MODEL // LANES
CONV_W = 4
RG_C = 8.0
NORM_EPS = 1e-6
MASK_VALUE = -1e30
IN_USED = 8208
IN_TOTAL = 9232
W_SHARD = IN_TOTAL // N_DEV

ADAM_LR = 0.001
ADAM_B1 = 0.9
ADAM_B2 = 0.999
ADAM_EPS = 1e-08
ADAM_WD = 0.01
ADAM_STEP = 10

ATT_TILE_FWD = 256
ATT_TILE_BWD = 512
SCAN_TILE = 256
SMALL_ROWS = 152
LOSS_ROW = 6


def _cparams(sem=None, vmem_mb=None):
    kw = {}
    if sem is not None:
        kw["dimension_semantics"] = sem
    if vmem_mb is not None:
        kw["vmem_limit_bytes"] = vmem_mb * 1024 * 1024
    return pltpu.CompilerParams(**kw)


def _sigmoid(x):
    return 1.0 / (1.0 + jnp.exp(-x))


def _softplus(x):
    return jnp.maximum(x, 0.0) + jnp.log1p(jnp.exp(-jnp.abs(x)))


def _one_minus_exp(y, exp_y):
    series = -y * (1.0 + y * (1.0 / 2 + y * (1.0 / 6 + y * (1.0 / 24 + y * (1.0 / 120)))))
    return jnp.where(y > -0.0625, series, 1.0 - exp_y)


def _split3(x):
    hi = x.astype(BF16)
    r1 = x - hi.astype(F32)
    mid = r1.astype(BF16)
    lo = (r1 - mid.astype(F32)).astype(BF16)
    return hi, mid, lo


def _dot(a, b):
    return jnp.dot(a, b, preferred_element_type=F32)


def _dot_nt(a, b):
    return lax.dot_general(a, b, (((1,), (1,)), ((), ())), preferred_element_type=F32)


def _dot_tn(a, b):
    return lax.dot_general(a, b, (((0,), (0,)), ((), ())), preferred_element_type=F32)


def _iota(shape, dim):
    return lax.broadcasted_iota(jnp.int32, shape, dim)


_ANY = pl.BlockSpec(memory_space=pl.ANY)
_MESH = pl.DeviceIdType.MESH
N_CHIPS = 4


def _place():
    x, y, c = lax.axis_index("x"), lax.axis_index("y"), lax.axis_index("c")
    other_chips = [(1 - x, y), (x, 1 - y), (1 - x, 1 - y)]
    return x, y, c, other_chips


def _gather(x_shard, name):
    def body(x_ref, out_ref, send_sems, recv_sems, local_sem):
        x, y, c, chips = _place()
        me, sibling = (x, y, c), (x, y, 1 - c)

        def slot(p):
            return out_ref.at[4 * p[0] + 2 * p[1] + p[2]]

        def copy(k, block, to, src=None):
            return pltpu.make_async_remote_copy(
                src_ref=slot(block) if src is None else src, dst_ref=slot(block),
                send_sem=send_sems.at[k], recv_sem=recv_sems.at[k], device_id=to, device_id_type=_MESH)

        mine = pltpu.make_async_copy(x_ref, slot(me), local_sem)
        mine.start()
        first = [copy(0, me, sibling, src=x_ref)]
        first += [copy(1 + j, me, (*chip, c), src=x_ref) for j, chip in enumerate(chips)]
        for cp in first:
            cp.start()
        passed = [copy(4 + j, (*chip, c), sibling) for j, chip in enumerate(chips)]
        for j, chip in enumerate(chips):
            copy(1 + j, (*chip, c), me).wait_recv()
            passed[j].start()
        copy(0, sibling, me).wait_recv()
        for j, chip in enumerate(chips):
            copy(4 + j, (*chip, 1 - c), me).wait_recv()
        for cp in first + passed:
            cp.wait_send()
        mine.wait()

    return pl.pallas_call(
        body, name=name,
        out_shape=jax.ShapeDtypeStruct((N_DEV,) + tuple(x_shard.shape), x_shard.dtype),
        in_specs=[_ANY], out_specs=_ANY,
        scratch_shapes=[pltpu.SemaphoreType.DMA((7,)), pltpu.SemaphoreType.DMA((7,)), pltpu.SemaphoreType.DMA],
    )(x_shard)


def _blocks_2d(r, c):
    if r % 128 == 0:
        return (128, c), r // 128, lambda i: (i, 0)
    return (r, 256), c // 256, lambda i: (0, i)


def _pair_add(src, recv, place, name, after=None):
    _, _, r, c = src.shape
    blk, nblk, at = _blocks_2d(r, c)
    deps = [] if after is None else [after]

    def body(place_ref, a_ref, b_ref, *refs):
        q16_ref, own_ref = refs[len(deps):]
        q = a_ref[...] + b_ref[...]
        q16_ref[...] = q.astype(BF16)

        @pl.when(pl.program_id(1) == place_ref[1])
        def _():
            own_ref[...] = q

    grid_spec = pltpu.PrefetchScalarGridSpec(
        num_scalar_prefetch=1, grid=(nblk, N_CHIPS),
        in_specs=[pl.BlockSpec((None, None) + blk, lambda i, j, pr: (pr[0], j) + at(i)),
                  pl.BlockSpec((None,) + blk, lambda i, j, pr: (j,) + at(i))]
        + [pl.BlockSpec(d.shape, lambda i, j, pr: (0, 0)) for d in deps],
        out_specs=[pl.BlockSpec((None,) + blk, lambda i, j, pr: (j,) + at(i)),
                   pl.BlockSpec(blk, lambda i, j, pr: at(i))])
    return pl.pallas_call(
        body, name=name, grid_spec=grid_spec,
        out_shape=[jax.ShapeDtypeStruct((N_CHIPS, r, c), BF16), jax.ShapeDtypeStruct((r, c), F32)],
        compiler_params=_cparams(("parallel", "arbitrary")),
    )(place, src, recv, *deps)


_HBM = pl.BlockSpec(memory_space=pltpu.HBM)
_SEM = pl.BlockSpec(memory_space=pltpu.SEMAPHORE)
_DATAFLOW = pltpu.SideEffectType.DATAFLOW_SIDE_EFFECTING


def _chip_copy(src_ref, land_ref, send_sem, recv_sem, k, chips, c, land):
    chip = chips[k]
    return pltpu.make_async_remote_copy(
        src_ref=src_ref.at[2 * chip[0] + chip[1]], dst_ref=land_ref.at[land],
        send_sem=send_sem, recv_sem=recv_sem, device_id=(*chip, c), device_id_type=_MESH)


def _exchange_chips_start(srcs, name):
    n = len(srcs)
    ncp = 3 * n

    def body(*refs):
        src_refs, land_refs = refs[:n], refs[n:2 * n]
        sems = refs[4 * n:4 * n + 2 * ncp]
        token = refs[-1]
        x, y, c, chips = _place()
        for i in range(n):
            for k in range(3):
                j = 3 * i + k
                _chip_copy(src_refs[i], land_refs[i], sems[j], sems[ncp + j], k, chips, c, 2 * x + y).start()
        token[...] = jnp.zeros_like(token)

    hbm = [pltpu.HBM(a.shape, a.dtype) for a in srcs]
    lands = [pltpu.with_memory_space_constraint(lax.empty(a.shape, a.dtype), pltpu.HBM) for a in srcs]
    res = pl.pallas_call(
        body, name=name,
        out_shape=(*hbm, *hbm, *([pltpu.SemaphoreType.DMA(())] * (2 * ncp)), jax.ShapeDtypeStruct((8, LANES), F32)),
        in_specs=[_HBM] * (2 * n),
        out_specs=(*([_HBM] * (2 * n)), *([_SEM] * (2 * ncp)), pl.BlockSpec(memory_space=pltpu.VMEM)),
        input_output_aliases={i: i for i in range(2 * n)},
        compiler_params=pltpu.CompilerParams(has_side_effects=_DATAFLOW),
    )(*[pltpu.with_memory_space_constraint(a, pltpu.HBM) for a in srcs], *lands)
    return list(res[2 * n:2 * n + 2 * ncp]), list(res[:n]), list(res[n:2 * n]), res[-1]


def _exchange_chips_wait(sems, srcs, lands, after, name):
    n = len(srcs)
    ncp = 3 * n

    def body(*refs):
        src_refs, land_refs = refs[:n], refs[n:2 * n]
        sem_refs = refs[2 * n:2 * n + 2 * ncp]
        x, y, c, chips = _place()
        for i in range(n):
            for k in range(3):
                j = 3 * i + k
                cp = _chip_copy(src_refs[i], land_refs[i], sem_refs[j], sem_refs[ncp + j], k, chips, c,
                                2 * chips[k][0] + chips[k][1])
                cp.wait_send()
                cp.wait_recv()

    hbm = [pltpu.HBM(a.shape, a.dtype) for a in srcs]
    res = pl.pallas_call(
        body, name=name, out_shape=(*hbm, *hbm),
        in_specs=[_HBM] * (2 * n) + [_SEM] * (2 * ncp) + [_ANY], out_specs=tuple([_HBM] * (2 * n)),
        input_output_aliases={i: i for i in range(2 * n)},
        compiler_params=pltpu.CompilerParams(has_side_effects=_DATAFLOW),
    )(*srcs, *lands, *sems, after)
    return list(res[n:2 * n])


def _swap_start(src, after, name):
    def body(src_ref, land_ref, after_ref, src_thru, land_thru, send_sem, recv_sem, token):
        x, y, c, _ = _place()
        pltpu.make_async_remote_copy(src_ref=src_ref.at[1 - c], dst_ref=land_ref, send_sem=send_sem,
                                     recv_sem=recv_sem, device_id=(x, y, 1 - c), device_id_type=_MESH).start()
        token[...] = jnp.zeros_like(token)

    land = pltpu.with_memory_space_constraint(lax.empty(src.shape[1:], src.dtype), pltpu.HBM)
    res = pl.pallas_call(
        body, name=name,
        out_shape=(pltpu.HBM(src.shape, src.dtype), pltpu.HBM(land.shape, land.dtype),
                   pltpu.SemaphoreType.DMA(()), pltpu.SemaphoreType.DMA(()), jax.ShapeDtypeStruct((8, LANES), F32)),
        in_specs=[_HBM, _HBM, _ANY],
        out_specs=(_HBM, _HBM, _SEM, _SEM, pl.BlockSpec(memory_space=pltpu.VMEM)),
        input_output_aliases={0: 0, 1: 1},
        compiler_params=pltpu.CompilerParams(has_side_effects=_DATAFLOW),
    )(pltpu.with_memory_space_constraint(src, pltpu.HBM), land, after)
    return [res[2], res[3]], res[0], res[1], res[-1]


def _swap_wait(sems, src, land, after, name):
    def body(src_ref, land_ref, send_sem, recv_sem, after_ref, src_out, land_out):
        x, y, c, _ = _place()
        cp = pltpu.make_async_remote_copy(src_ref=src_ref.at[1 - c], dst_ref=land_ref, send_sem=send_sem,
                                          recv_sem=recv_sem, device_id=(x, y, 1 - c), device_id_type=_MESH)
        cp.wait_send()
        cp.wait_recv()

    res = pl.pallas_call(
        body, name=name, out_shape=(pltpu.HBM(src.shape, src.dtype), pltpu.HBM(land.shape, land.dtype)),
        in_specs=[_HBM, _HBM, _SEM, _SEM, _ANY], out_specs=(_HBM, _HBM),
        input_output_aliases={0: 0, 1: 1},
        compiler_params=pltpu.CompilerParams(has_side_effects=_DATAFLOW),
    )(src, land, *sems, after)
    return res[0], res[1]


def _peer_copy(src_ref, land_ref, send_sem, recv_sem, k, place, land):
    x, y, c = place
    peer = (1 - x if k & 4 else x, 1 - y if k & 2 else y, 1 - c if k & 1 else c)
    return pltpu.make_async_remote_copy(
        src_ref=src_ref, dst_ref=land_ref.at[land], send_sem=send_sem, recv_sem=recv_sem,
        device_id=peer, device_id_type=_MESH)


def _gather_start(x_shard, after, name):
    npeer = N_DEV - 1

    def body(x_ref, land_ref, after_ref, x_thru, land_thru, *rest):
        sems, token = rest[:2 * npeer], rest[-1]
        x, y, c, _ = _place()
        for k in range(1, N_DEV):
            _peer_copy(x_ref, land_ref, sems[k - 1], sems[npeer + k - 1], k, (x, y, c), 4 * x + 2 * y + c).start()
        token[...] = jnp.zeros_like(token)

    land = pltpu.with_memory_space_constraint(lax.empty((N_DEV,) + tuple(x_shard.shape), x_shard.dtype), pltpu.HBM)
    res = pl.pallas_call(
        body, name=name,
        out_shape=(pltpu.HBM(x_shard.shape, x_shard.dtype), pltpu.HBM(land.shape, land.dtype),
                   *([pltpu.SemaphoreType.DMA(())] * (2 * npeer)), jax.ShapeDtypeStruct((8, LANES), F32)),
        in_specs=[_HBM, _HBM, _ANY],
        out_specs=(_HBM, _HBM, *([_SEM] * (2 * npeer)), pl.BlockSpec(memory_space=pltpu.VMEM)),
        input_output_aliases={0: 0, 1: 1},
        compiler_params=pltpu.CompilerParams(has_side_effects=_DATAFLOW),
    )(pltpu.with_memory_space_constraint(x_shard, pltpu.HBM), land, after)
    return list(res[2:2 + 2 * npeer]), res[0], res[1], res[-1]


def _gather_wait(sems, src, land, after, name):
    npeer = N_DEV - 1

    def body(x_ref, land_ref, *rest):
        sem_refs = rest[:2 * npeer]
        x, y, c, _ = _place()
        for k in range(1, N_DEV):
            peer_index = (4 * x + 2 * y + c) ^ k
            cp = _peer_copy(x_ref, land_ref, sem_refs[k - 1], sem_refs[npeer + k - 1], k, (x, y, c), peer_index)
            cp.wait_send()
            cp.wait_recv()

    res = pl.pallas_call(
        body, name=name, out_shape=(pltpu.HBM(src.shape, src.dtype), pltpu.HBM(land.shape, land.dtype)),
        in_specs=[_HBM, _HBM] + [_SEM] * (2 * npeer) + [_ANY], out_specs=(_HBM, _HBM),
        input_output_aliases={0: 0, 1: 1},
        compiler_params=pltpu.CompilerParams(has_side_effects=_DATAFLOW),
    )(src, land, *sems, after)
    return res[1]


def _prenorm_inproj(x2, w, wt_qkv, b_qkv, wt_f, b_f):
    t = x2.shape[0]
    tm = min(512, t)
    n = wt_qkv.shape[0]
    tn = D_MODEL

    def body(x_ref, w_ref, wq_ref, bq_ref, wf_ref, bf_ref, h_ref, qkv_ref, zf_ref):
        x = x_ref[...]
        r = lax.rsqrt(jnp.mean(x * x, axis=-1, keepdims=True) + NORM_EPS)
        h = (x * r * w_ref[...]).astype(BF16)
        h_ref[...] = h
        for j in range(n // tn):
            cols = slice(j * tn, (j + 1) * tn)
            qkv_ref[:, cols] = (_dot_nt(h, wq_ref[cols, :]) + bq_ref[:, cols]).astype(BF16)
        zf_ref[...] = _dot_nt(h, wf_ref[...]) + bf_ref[...]

    row = lambda c: pl.BlockSpec((tm, c), lambda i: (i, 0))
    whole = lambda a: pl.BlockSpec(a.shape, lambda i: (0, 0))
    return pl.pallas_call(
        body, name="prenorm_inproj_qkv", grid=(t // tm,),
        in_specs=[row(D_MODEL), whole(w), whole(wt_qkv), whole(b_qkv), whole(wt_f), whole(b_f)],
        out_specs=[row(D_MODEL), row(n), row(LANES)],
        out_shape=[jax.ShapeDtypeStruct((t, D_MODEL), BF16), jax.ShapeDtypeStruct((t, n), BF16),
                   jax.ShapeDtypeStruct((t, LANES), F32)],
        compiler_params=_cparams(("parallel",), vmem_mb=48),
    )(x2, w, wt_qkv, b_qkv, wt_f, b_f)


def _mm_bias(a, bt, bias, out_dtype, name):
    m, k = a.shape
    n = bt.shape[0]
    tm = min(1024, m)
    tn = min(1024, n)

    def body(a_ref, bt_ref, bias_ref, o_ref):
        aa = a_ref[...]
        for j in range(n // tn):
            cols = slice(j * tn, (j + 1) * tn)
            o_ref[:, cols] = (_dot_nt(aa, bt_ref[cols, :]) + bias_ref[:, cols]).astype(o_ref.dtype)

    return pl.pallas_call(
        body, name=name, grid=(m // tm,),
        in_specs=[pl.BlockSpec((tm, k), lambda i: (i, 0)),
                  pl.BlockSpec((n, k), lambda i: (0, 0), pipeline_mode=pl.Buffered(1)),
                  pl.BlockSpec((1, n), lambda i: (0, 0))],
        out_specs=pl.BlockSpec((tm, n), lambda i: (i, 0)),
        out_shape=jax.ShapeDtypeStruct((m, n), out_dtype),
        compiler_params=_cparams(("parallel",), vmem_mb=48),
    )(a, bt, bias)


def _mm_tn(a, b, name, after=None):
    t, m = a.shape
    n = b.shape[1]
    tm = min(1024, m)
    tk = min(4096, t)
    deps = [] if after is None else [after]

    def body(a_ref, b_ref, *refs):
        o_ref, s_ref = refs[len(deps):]
        kk = pl.program_id(1)

        @pl.when(kk == 0)
        def _():
            o_ref[...] = jnp.zeros_like(o_ref)
            s_ref[...] = jnp.zeros_like(s_ref)

        aa = a_ref[...]
        o_ref[...] += _dot_tn(aa, b_ref[...])
        s_ref[0:1, :] += jnp.sum(aa.astype(F32), axis=0, keepdims=True)

    return pl.pallas_call(
        body, name=name, grid=(m // tm, t // tk),
        in_specs=[pl.BlockSpec((tk, tm), lambda i, kk: (kk, i)), pl.BlockSpec((tk, n), lambda i, kk: (kk, 0))]
        + [pl.BlockSpec(d.shape, lambda i, kk: (0, 0)) for d in deps],
        out_specs=[pl.BlockSpec((tm, n), lambda i, kk: (i, 0)), pl.BlockSpec((8, tm), lambda i, kk: (0, i))],
        out_shape=[jax.ShapeDtypeStruct((m, n), F32), jax.ShapeDtypeStruct((8, m), F32)],
        compiler_params=_cparams(("parallel", "arbitrary"), vmem_mb=48),
    )(a, b, *deps)


def _fgate_fwd(zf3):
    b, s, _ = zf3.shape
    tb = SCAN_TILE
    nb = s // tb

    def body(z_ref, cexp_ref, crow_ref):
        tri = (_iota((tb, tb), 1) <= _iota((tb, tb), 0)).astype(BF16)
        expand = ((_iota((LANES, D_MODEL), 1) >> 6) == _iota((LANES, D_MODEL), 0)).astype(BF16)
        carry = jnp.zeros((1, LANES), F32)
        for i in range(nb):
            rows = slice(i * tb, (i + 1) * tb)
            z = z_ref[rows, :]
            lf = jnp.minimum(z, 0.0) - jnp.log1p(jnp.exp(-jnp.abs(z)))
            cb = sum(_dot(tri, part) for part in _split3(lf)) + carry
            carry = cb[tb - 1:tb, :]
            cexp_ref[rows, :] = sum(_dot(part, expand) for part in _split3(cb))
            crow_ref[:, rows] = cb.T[0:HEADS, :]

    return pl.pallas_call(
        body, name="fgate_fwd", grid=(b,),
        in_specs=[pl.BlockSpec((None, s, LANES), lambda i: (i, 0, 0))],
        out_specs=[pl.BlockSpec((None, s, D_MODEL), lambda i: (i, 0, 0)),
                   pl.BlockSpec((None, HEADS, s), lambda i: (i, 0, 0))],
        out_shape=[jax.ShapeDtypeStruct((b, s, D_MODEL), F32), jax.ShapeDtypeStruct((b, HEADS, s), F32)],
        compiler_params=_cparams(("parallel",)),
    )(zf3)


def _fgate_bwd(dc3, zf3):
    b, s, _ = zf3.shape
    tb = SCAN_TILE
    nb = s // tb

    def body(dc_ref, z_ref, o_ref):
        tri = (_iota((tb, tb), 1) >= _iota((tb, tb), 0)).astype(BF16)
        carry = jnp.zeros((1, LANES), F32)
        for i in reversed(range(nb)):
            rows = slice(i * tb, (i + 1) * tb)
            dlf = sum(_dot(tri, part) for part in _split3(dc_ref[rows, :])) + carry
            carry = dlf[0:1, :]
            o_ref[rows, :] = (dlf * _sigmoid(-z_ref[rows, :])).astype(BF16)

    return pl.pallas_call(
        body, name="fgate_bwd", grid=(b,),
        in_specs=[pl.BlockSpec((None, s, LANES), lambda i: (i, 0, 0)),
                  pl.BlockSpec((None, s, LANES), lambda i: (i, 0, 0))],
        out_specs=pl.BlockSpec((s, LANES), lambda i: (i, 0)),
        out_shape=jax.ShapeDtypeStruct((b * s, LANES), BF16),
        compiler_params=_cparams(("parallel",)),
    )(dc3, zf3)


def _spare(hh):
    return HEAD_DIM if hh == 0 else 0


def _put_cols(tile, mine, cols, first):
    lane = _iota((1, LANES), 1)
    out = jnp.where(mine, tile, jnp.zeros((), tile.dtype))
    for j, c in enumerate(cols):
        out = jnp.where(lane == first + j, c, out)
    return out


def _put_rows(tile, mine, rows, first):
    sub = _iota((LANES, 1), 0)
    out = jnp.where(mine, tile, jnp.zeros((), tile.dtype))
    for j, r in enumerate(rows):
        out = jnp.where(sub == first + j, r, out)
    return out


def _transpose_bf16(a):
    return a.astype(F32).T.astype(BF16)


def _attn_fwd(qkv3, cexp3, crow, zrest3):
    b, s, _ = qkv3.shape
    ta = ATT_TILE_FWD
    nq = s // ta
    hd = HEAD_DIM
    crow5 = crow.reshape(b, HEAD_PAIRS, 2, nq, ta)

    def body(qkv_ref, cq_ref, ck_ref, g_ref, y_ref, lse_ref, ga_ref, kt_scr, v_scr):
        lane = _iota((1, LANES), 1)
        sub = _iota((LANES, 1), 0)
        lane_mine = (lane < hd, lane >= hd)
        sub_mine = (sub < hd, sub >= hd)
        causal = _iota((ta, ta), 0) >= _iota((ta, ta), 1)
        one = jnp.ones((), BF16)

        for kj in range(nq):
            rows = slice(kj * ta, (kj + 1) * ta)
            kt = _transpose_bf16(qkv_ref[rows, LANES:2 * LANES])
            v = qkv_ref[rows, 2 * LANES:3 * LANES]
            for hh in range(2):
                ck = list(_split3(-ck_ref[hh, kj:kj + 1, :]))
                kt_scr[hh, kj] = _put_rows(kt, sub_mine[hh], [one, one, one] + ck, _spare(hh))
                v_scr[hh, kj] = _put_cols(v, lane_mine[hh], [one], _spare(hh))

        for qi in range(nq):
            rows = slice(qi * ta, (qi + 1) * ta)
            q = qkv_ref[rows, 0:LANES] * 0.125
            cq = cq_ref[rows, :]
            qh = [_put_cols(q, lane_mine[hh], list(_split3(cq[:, hh * hd:hh * hd + 1])) + [one, one, one], _spare(hh))
                  for hh in range(2)]
            st = [(jnp.full((ta, 1), MASK_VALUE, F32), jnp.zeros((ta, LANES), F32))] * 2
            for kj in range(qi + 1):
                for hh in range(2):
                    m, acc = st[hh]
                    sc = _dot(qh[hh], kt_scr[hh, kj])
                    if kj == qi:
                        sc = jnp.where(causal, sc, MASK_VALUE)
                    mn = jnp.maximum(m, jnp.max(sc, axis=-1, keepdims=True))
                    p = jnp.exp(sc - mn).astype(BF16)
                    st[hh] = (mn, jnp.exp(m - mn) * acc + _dot(p, v_scr[hh, kj]))
            (ma, acca), (mb, accb) = st
            la = acca[:, hd:hd + 1]
            lb = accb[:, 0:1]
            y = jnp.where(lane_mine[0], acca * (1.0 / la), accb * (1.0 / lb))
            lse = jnp.where(lane_mine[0], ma + jnp.log(la), mb + jnp.log(lb)).T
            lse_ref[0, qi:qi + 1, :] = lse[0:1, :]
            lse_ref[1, qi:qi + 1, :] = lse[hd:hd + 1, :]
            y_ref[rows, :] = y
            g = g_ref[rows, :].astype(F32)
            ga_ref[rows, :] = (y * (g * _sigmoid(g))).astype(BF16)

    blk = lambda w: pl.BlockSpec((None, s, w), lambda i, p: (i, 0, p))
    rows5 = pl.BlockSpec((None, None, 2, nq, ta), lambda i, p: (i, p, 0, 0, 0))
    yatt3, lse5, ga = pl.pallas_call(
        body, name="attn_fwd", grid=(b, HEAD_PAIRS),
        in_specs=[blk(3 * LANES), blk(LANES), rows5, blk(LANES)],
        out_specs=[blk(LANES), rows5, pl.BlockSpec((s, LANES), lambda i, p: (i, p))],
        out_shape=[jax.ShapeDtypeStruct((b, s, D_MODEL), F32),
                   jax.ShapeDtypeStruct((b, HEAD_PAIRS, 2, nq, ta), F32),
                   jax.ShapeDtypeStruct((b * s, D_MODEL), BF16)],
        scratch_shapes=[pltpu.VMEM((2, nq, LANES, ta), BF16), pltpu.VMEM((2, nq, ta, LANES), BF16)],
        compiler_params=_cparams(("parallel", "parallel")),
    )(qkv3, cexp3, crow5, zrest3)
    return yatt3, lse5.reshape(b, HEADS, s), ga


def _attn_bwd(qkv3, do3, y3, lse, crow, cexp3):
    b, s, _ = qkv3.shape
    ta = ATT_TILE_BWD
    nq = s // ta
    hd = HEAD_DIM
    lse5 = lse.reshape(b, HEAD_PAIRS, 2, nq, ta)
    crow5 = crow.reshape(b, HEAD_PAIRS, 2, nq, ta)

    def body(qkv_ref, do_ref, y_ref, lse_ref, crow_ref, cexp_ref, dqkv_ref, dc_ref,
             qa_scr, doa_scr, qst_scr, dot_scr, kt_scr, vt_scr, dq_scr, rs_scr):
        pair = pl.program_id(1)
        lane = _iota((1, LANES), 1)
        sub = _iota((LANES, 1), 0)
        lane_mine = (lane < hd, lane >= hd)
        sub_mine = (sub < hd, sub >= hd)
        causal = _iota((ta, ta), 0) >= _iota((ta, ta), 1)
        one = jnp.ones((), BF16)
        zero = jnp.zeros((), BF16)

        @pl.when(pair == 0)
        def _():
            dc_ref[...] = jnp.zeros_like(dc_ref)

        for i in range(nq):
            rows = slice(i * ta, (i + 1) * ta)
            qs = qkv_ref[rows, 0:LANES] * 0.125
            qst = _transpose_bf16(qs)
            kt = _transpose_bf16(qkv_ref[rows, LANES:2 * LANES])
            vt = _transpose_bf16(qkv_ref[rows, 2 * LANES:3 * LANES])
            do = do_ref[rows, :]
            dof = do.astype(F32)
            dot = dof.T.astype(BF16)
            pr = y_ref[rows, :] * dof
            cq = cexp_ref[rows, :]
            lse_c = jnp.where(sub == 0, lse_ref[0, i:i + 1, :],
                              jnp.where(sub == 1, lse_ref[1, i:i + 1, :], 0.0)).T
            for hh in range(2):
                sp = _spare(hh)
                dsum = jnp.sum(jnp.where(lane_mine[hh], pr, 0.0), axis=-1, keepdims=True)
                bias = cq[:, hh * hd:hh * hd + 1] - lse_c[:, hh:hh + 1]
                qa_scr[hh, i] = _put_cols(qs, lane_mine[hh], list(_split3(bias)) + [one, one, one], sp)
                doa_scr[hh, i] = _put_cols(do, lane_mine[hh], list(_split3(-dsum)), sp)
                qst_scr[hh, i] = jnp.where(sub_mine[hh], qst, zero)
                dot_scr[hh, i] = jnp.where(sub_mine[hh], dot, zero)
                ck = list(_split3(-crow_ref[hh, i:i + 1, :]))
                kt_scr[hh, i] = _put_rows(kt, sub_mine[hh], [one, one, one] + ck, sp)
                vt_scr[hh, i] = _put_rows(vt, sub_mine[hh], [one, one, one], sp)
            dq_scr[i] = jnp.zeros((ta, LANES), F32)
            rs_scr[i] = jnp.zeros((ta, LANES), F32)

        for kj in range(nq):
            krows = slice(kj * ta, (kj + 1) * ta)
            k = qkv_ref[krows, LANES:2 * LANES]
            km = (jnp.where(lane_mine[0], k, zero), jnp.where(lane_mine[1], k, zero))
            dkt = jnp.zeros((LANES, ta), F32)
            dvt = jnp.zeros((LANES, ta), F32)
            dcp = [jnp.zeros((8, ta), F32), jnp.zeros((8, ta), F32)]
            for qi in range(kj, nq):
                dq = jnp.zeros((ta, LANES), F32)
                rs = []
                for hh in range(2):
                    sc = _dot(qa_scr[hh, qi], kt_scr[hh, kj])
                    if qi == kj:
                        sc = jnp.where(causal, sc, MASK_VALUE)
                    p = jnp.exp(sc)
                    dsf = p * _dot(doa_scr[hh, qi], vt_scr[hh, kj])
                    dcp[hh] = dcp[hh] + jnp.sum(dsf.reshape(ta // 8, 8, ta), axis=0)
                    rs.append(jnp.sum(dsf, axis=-1, keepdims=True))
                    ds = dsf.astype(BF16)
                    dq = dq + _dot(ds, km[hh])
                    dkt = dkt + _dot(qst_scr[hh, qi], ds)
                    dvt = dvt + _dot(dot_scr[hh, qi], p.astype(BF16))
                dq_scr[qi] += dq
                rs_scr[qi] += jnp.where(lane == 0, rs[0], jnp.where(lane == 1, rs[1], 0.0))
            dqkv_ref[krows, LANES:2 * LANES] = dkt.T.astype(BF16)
            dqkv_ref[krows, 2 * LANES:3 * LANES] = dvt.T.astype(BF16)
            dca = jnp.sum(dcp[0], axis=0, keepdims=True)
            dcb = jnp.sum(dcp[1], axis=0, keepdims=True)
            dcs = jnp.where(sub == 0, dca, jnp.where(sub == 1, dcb, 0.0)).T
            dc_ref[krows, :] += (jnp.where(lane == 2 * pair, -dcs[:, 0:1], 0.0)
                                 + jnp.where(lane == 2 * pair + 1, -dcs[:, 1:2], 0.0))
        for qi in range(nq):
            rows = slice(qi * ta, (qi + 1) * ta)
            dqkv_ref[rows, 0:LANES] = (dq_scr[qi] * 0.125).astype(BF16)
            rq = rs_scr[qi]
            dc_ref[rows, :] += (jnp.where(lane == 2 * pair, rq[:, 0:1], 0.0)
                                + jnp.where(lane == 2 * pair + 1, rq[:, 1:2], 0.0))

    blk = lambda w: pl.BlockSpec((None, s, w), lambda i, p: (i, 0, p))
    rows5 = pl.BlockSpec((None, None, 2, nq, ta), lambda i, p: (i, p, 0, 0, 0))
    by_rows = lambda: pltpu.VMEM((2, nq, ta, LANES), BF16)
    by_cols = lambda: pltpu.VMEM((2, nq, LANES, ta), BF16)
    return pl.pallas_call(
        body, name="attn_bwd", grid=(b, HEAD_PAIRS),
        in_specs=[blk(3 * LANES), blk(LANES), blk(LANES), rows5, rows5, blk(LANES)],
        out_specs=[pl.BlockSpec((s, 3 * LANES), lambda i, p: (i, p)),
                   pl.BlockSpec((None, s, LANES), lambda i, p: (i, 0, 0))],
        out_shape=[jax.ShapeDtypeStruct((b * s, 3 * D_MODEL), BF16), jax.ShapeDtypeStruct((b, s, LANES), F32)],
        scratch_shapes=[by_rows(), by_rows(), by_cols(), by_cols(), by_cols(), by_cols(),
                        pltpu.VMEM((nq, ta, LANES), F32), pltpu.VMEM((nq, ta, LANES), F32)],
        compiler_params=_cparams(("parallel", "arbitrary")),
    )(qkv3, do3, y3, lse5, crow5, cexp3)


def _shifted(v, ks, rows, s):
    low = rows[0:8, :]
    out = []
    for k in ks:
        r = pltpu.roll(v, k % s, 0)
        if k > 0:
            out.append(jnp.concatenate([jnp.where(low >= k, r[0:8, :], 0.0), r[8:, :]], axis=0))
        else:
            out.append(jnp.concatenate([r[:s - 8, :], jnp.where(low < 8 + k, r[s - 8:, :], 0.0)], axis=0))
    return out


def _rnn_common(xr, cw_ref, cb_ref, bda_ref, bdx_ref, ba_ref, bx_ref, lam_ref, s):
    rows = _iota((s, LANES), 0)
    x1, x2, x3 = _shifted(xr, (1, 2, 3), rows, s)
    xc = cb_ref[...] + cw_ref[0:1, :] * x3
    xc = xc + cw_ref[1:2, :] * x2
    xc = xc + cw_ref[2:3, :] * x1
    xc = xc + cw_ref[3:4, :] * xr
    xcb = xc.astype(BF16)
    r = _sigmoid(_dot(xcb, bda_ref[...]) + ba_ref[...])
    i = _sigmoid(_dot(xcb, bdx_ref[...]) + bx_ref[...])
    sp = _softplus(-lam_ref[...])
    log_a = (-RG_C * r) * sp
    a = jnp.exp(log_a)
    a2 = a * a
    sq = jnp.sqrt(jnp.maximum(_one_minus_exp(log_a + log_a, a2), 0.0))
    return rows, (x1, x2, x3), xc, xcb, r, i, sp, a, a2, sq


def _scan_down(a, u, rows, s, s1, s2):
    low = rows & 7
    for sh in (1, 2, 4):
        keep = low >= sh
        u = u + a * jnp.where(keep, pltpu.roll(u, sh, 0), 0.0)
        a = a * jnp.where(keep, pltpu.roll(a, sh, 0), 1.0)
    ng = s // 8
    s1[...] = a
    s2[...] = u
    at = s1[pl.ds(7, ng, stride=8), :]
    ut = s2[pl.ds(7, ng, stride=8), :]
    grow = _iota((ng, LANES), 0)
    sh = 1
    while sh < ng:
        keep = grow >= sh
        ut = ut + at * jnp.where(keep, pltpu.roll(ut, sh, 0), 0.0)
        if sh * 2 < ng:
            at = at * jnp.where(keep, pltpu.roll(at, sh, 0), 1.0)
        sh *= 2
    h_in = jnp.where(grow >= 1, pltpu.roll(ut, 1, 0), 0.0)
    for k in range(8):
        s1[pl.ds(k, ng, stride=8), :] = h_in
    return u + a * s1[...]


def _scan_up(a, g, rows, s, s1, s2):
    low = rows & 7
    for sh in (1, 2, 4):
        keep = low < 8 - sh
        g = g + a * jnp.where(keep, pltpu.roll(g, s - sh, 0), 0.0)
        a = a * jnp.where(keep, pltpu.roll(a, s - sh, 0), 1.0)
    ng = s // 8
    s1[...] = a
    s2[...] = g
    at = s1[pl.ds(0, ng, stride=8), :]
    gt = s2[pl.ds(0, ng, stride=8), :]
    grow = _iota((ng, LANES), 0)
    sh = 1
    while sh < ng:
        keep = grow < ng - sh
        gt = gt + at * jnp.where(keep, pltpu.roll(gt, ng - sh, 0), 0.0)
        if sh * 2 < ng:
            at = at * jnp.where(keep, pltpu.roll(at, ng - sh, 0), 1.0)
        sh *= 2
    g_in = jnp.where(grow < ng - 1, pltpu.roll(gt, ng - 1, 0), 0.0)
    for k in range(8):
        s1[pl.ds(k, ng, stride=8), :] = g_in
    return g + a * s1[...]


def _rnn_specs(s):
    blk = lambda off: pl.BlockSpec((None, s, LANES), lambda cb, i: (i, 0, off + cb))
    vec = lambda r: pl.BlockSpec((r, LANES), lambda cb, i: (0, cb))
    mat = pl.BlockSpec((None, LANES, LANES), lambda cb, i: (cb, 0, 0))
    return blk, vec, mat


def _rnn_fwd(zrest3, conv_w, conv_b, bda, bdx, ba, bx, lam):
    b, s, _ = zrest3.shape

    def body(xr_ref, g_ref, cw_ref, cb_ref, bda_ref, bdx_ref, ba_ref, bx_ref, lam_ref, h_ref, gr_ref, s1, s2):
        xr = xr_ref[...].astype(F32)
        rows, _, xc, _, _, i, _, a, _, sq = _rnn_common(
            xr, cw_ref, cb_ref, bda_ref, bdx_ref, ba_ref, bx_ref, lam_ref, s)
        h = _scan_down(a, sq * (i * xc), rows, s, s1, s2)
        h_ref[...] = h
        g = g_ref[...].astype(F32)
        gr_ref[...] = (h * (g * _sigmoid(g))).astype(BF16)

    blk, vec, mat = _rnn_specs(s)
    return pl.pallas_call(
        body, name="rnn_fwd", grid=(N_CBLK, b),
        in_specs=[blk(N_CBLK), blk(2 * N_CBLK), vec(CONV_W), vec(1), mat, mat, vec(1), vec(1), vec(1)],
        out_specs=[blk(0), pl.BlockSpec((s, LANES), lambda cb, i: (i, cb))],
        out_shape=[jax.ShapeDtypeStruct((b, s, D_MODEL), F32), jax.ShapeDtypeStruct((b * s, D_MODEL), BF16)],
        scratch_shapes=[pltpu.VMEM((s, LANES), F32), pltpu.VMEM((s, LANES), F32)],
        compiler_params=_cparams(("parallel", "parallel")),
    )(zrest3, zrest3, conv_w, conv_b, bda, bdx, ba, bx, lam)


def _rnn_bwd(zrest3, h3, dh3, conv_w, conv_b, bda, bdx, ba, bx, lam):
    b, s, _ = zrest3.shape

    def body(xr_ref, h_ref, dh_ref, cw_ref, cb_ref, bda_ref, bdx_ref, ba_ref, bx_ref, lam_ref,
             dxr_ref, pv_ref, dbd_ref, s1, s2):
        @pl.when(pl.program_id(1) == 0)
        def _():
            pv_ref[...] = jnp.zeros_like(pv_ref)
            dbd_ref[...] = jnp.zeros_like(dbd_ref)

        xr = xr_ref[...].astype(F32)
        rows, (x1, x2, x3), xc, xcb, r, i, sp, a, a2, sq = _rnn_common(
            xr, cw_ref, cb_ref, bda_ref, bdx_ref, ba_ref, bx_ref, lam_ref, s)
        (a_next,) = _shifted(a, (-1,), rows, s)
        g = _scan_up(a_next, dh_ref[...], rows, s, s1, s2)
        (hp,) = _shifted(h_ref[...], (1,), rows, s)
        da = g * hp
        dsq = g * (i * xc)
        di = g * (sq * xc)
        dxc = g * (sq * i)
        dlog = da * a - dsq * (a2 / sq)
        dr = dlog * (-RG_C * sp)
        dpr = dr * (r * (1.0 - r))
        dpi = di * (i * (1.0 - i))
        dprb = dpr.astype(BF16)
        dpib = dpi.astype(BF16)
        dxc = dxc + _dot_nt(dprb, bda_ref[...]) + _dot_nt(dpib, bdx_ref[...])

        up1, up2, up3 = _shifted(dxc, (-1, -2, -3), rows, s)
        dxr = cw_ref[3:4, :] * dxc + cw_ref[2:3, :] * up1 + cw_ref[1:2, :] * up2 + cw_ref[0:1, :] * up3
        dxr_ref[...] = dxr.astype(BF16)

        def colsum(v):
            return jnp.sum(v, axis=0, keepdims=True)

        pv_ref[0:1, :] += colsum(dxc * x3)
        pv_ref[1:2, :] += colsum(dxc * x2)
        pv_ref[2:3, :] += colsum(dxc * x1)
        pv_ref[3:4, :] += colsum(dxc * xr)
        pv_ref[4:5, :] += colsum(dxc)
        pv_ref[5:6, :] += colsum(dpr)
        pv_ref[6:7, :] += colsum(dpi)
        pv_ref[7:8, :] += colsum(dlog * r) * (RG_C * _sigmoid(-lam_ref[...]))
        dbd_ref[0] += _dot_tn(xcb, dprb)
        dbd_ref[1] += _dot_tn(xcb, dpib)

    blk, vec, mat = _rnn_specs(s)
    hblk = pl.BlockSpec((None, s, LANES), lambda cb, i: (i, 0, cb))
    return pl.pallas_call(
        body, name="rnn_bwd", grid=(N_CBLK, b),
        in_specs=[blk(N_CBLK), hblk, hblk, vec(CONV_W), vec(1), mat, mat, vec(1), vec(1), vec(1)],
        out_specs=[pl.BlockSpec((s, LANES), lambda cb, i: (i, cb)), pl.BlockSpec((8, LANES), lambda cb, i: (0, cb)),
                   pl.BlockSpec((None, 2, LANES, LANES), lambda cb, i: (cb, 0, 0, 0))],
        out_shape=[jax.ShapeDtypeStruct((b * s, D_MODEL), BF16), jax.ShapeDtypeStruct((8, D_MODEL), F32),
                   jax.ShapeDtypeStruct((N_CBLK, 2, LANES, LANES), F32)],
        scratch_shapes=[pltpu.VMEM((s, LANES), F32), pltpu.VMEM((s, LANES), F32)],
        compiler_params=_cparams(("parallel", "arbitrary")),
    )(zrest3, h3, dh3, conv_w, conv_b, bda, bdx, ba, bx, lam)


def _branch_merge(ga, gr, wa, wr, zrest):
    t = ga.shape[0]
    tm = min(512, t)
    tn = D_MODEL

    def body(ga_ref, gr_ref, wa_ref, wr_ref, mga_ref, mgr_ref, ya_ref, yr_ref, m_ref):
        ya = _dot(ga_ref[...], wa_ref[...])
        yr = _dot(gr_ref[...], wr_ref[...])
        ya_ref[...] = ya.astype(BF16)
        yr_ref[...] = yr.astype(BF16)
        m_ref[...] = (_sigmoid(mga_ref[...].astype(F32)) * ya + _sigmoid(mgr_ref[...].astype(F32)) * yr).astype(BF16)

    nj = D_MODEL // tn
    act = pl.BlockSpec((tm, D_MODEL), lambda i, j: (i, 0))
    wgt = pl.BlockSpec((D_MODEL, tn), lambda i, j: (0, j))
    out = pl.BlockSpec((tm, tn), lambda i, j: (i, j))
    return pl.pallas_call(
        body, name="branch_merge", grid=(t // tm, nj),
        in_specs=[act, act, wgt, wgt, pl.BlockSpec((tm, tn), lambda i, j: (i, 3 * nj + j)),
                  pl.BlockSpec((tm, tn), lambda i, j: (i, 4 * nj + j))],
        out_specs=[out, out, out],
        out_shape=[jax.ShapeDtypeStruct((t, D_MODEL), BF16), jax.ShapeDtypeStruct((t, D_MODEL), BF16),
                   jax.ShapeDtypeStruct((t, D_MODEL), BF16)],
        compiler_params=_cparams(("parallel", "parallel")),
    )(ga, gr, wa, wr, zrest, zrest)


def _out_loss(m, wout, x2, tgt2, wpost):
    t = m.shape[0]
    tm = min(512, t)

    def body(m_ref, w_ref, x_ref, t_ref, wp_ref, dy_ref, do_ref, acc_ref):
        @pl.when(pl.program_id(0) == 0)
        def _():
            acc_ref[...] = jnp.zeros_like(acc_ref)

        o = _dot(m_ref[...], w_ref[...])
        r2 = lax.rsqrt(jnp.mean(o * o, axis=-1, keepdims=True) + NORM_EPS)
        n = o * r2
        wp = wp_ref[...]
        err = (x_ref[...] + n * wp) - t_ref[...]
        dy = err * (1.0 / D_MODEL)
        dn = dy * wp
        do = r2 * (dn - n * jnp.mean(dn * n, axis=-1, keepdims=True))
        dy_ref[...] = dy
        do_ref[...] = do.astype(BF16)
        acc_ref[0:1, :] += jnp.sum(dy * n, axis=0, keepdims=True)
        acc_ref[1:2, :] += jnp.sum(err * err, axis=0, keepdims=True)

    row = pl.BlockSpec((tm, D_MODEL), lambda i: (i, 0))
    return pl.pallas_call(
        body, name="out_loss", grid=(t // tm,),
        in_specs=[row, pl.BlockSpec((D_MODEL, D_MODEL), lambda i: (0, 0)), row, row,
                  pl.BlockSpec((1, D_MODEL), lambda i: (0, 0))],
        out_specs=[row, row, pl.BlockSpec((8, D_MODEL), lambda i: (0, 0))],
        out_shape=[jax.ShapeDtypeStruct((t, D_MODEL), F32), jax.ShapeDtypeStruct((t, D_MODEL), BF16),
                   jax.ShapeDtypeStruct((8, D_MODEL), F32)],
        compiler_params=_cparams(("arbitrary",)),
    )(m, wout, x2, tgt2, wpost)


def _merge_bwd(do, wout, zrest, ya, yr):
    t = do.shape[0]
    tm = min(512, t)
    tn = D_MODEL
    nj = D_MODEL // tn

    def body(do_ref, w_ref, mga_ref, mgr_ref, ya_ref, yr_ref, dya_ref, dyr_ref, dmga_ref, dmgr_ref):
        dm = _dot_nt(do_ref[...], w_ref[...])
        sa = _sigmoid(mga_ref[...].astype(F32))
        sr = _sigmoid(mgr_ref[...].astype(F32))
        dya_ref[...] = (dm * sa).astype(BF16)
        dyr_ref[...] = (dm * sr).astype(BF16)
        dmga_ref[...] = (dm * ya_ref[...].astype(F32) * (sa * (1.0 - sa))).astype(BF16)
        dmgr_ref[...] = (dm * yr_ref[...].astype(F32) * (sr * (1.0 - sr))).astype(BF16)

    out = pl.BlockSpec((tm, tn), lambda i, j: (i, j))
    bf = jax.ShapeDtypeStruct((t, D_MODEL), BF16)
    return pl.pallas_call(
        body, name="merge_bwd", grid=(t // tm, nj),
        in_specs=[pl.BlockSpec((tm, D_MODEL), lambda i, j: (i, 0)), pl.BlockSpec((tn, D_MODEL), lambda i, j: (j, 0)),
                  pl.BlockSpec((tm, tn), lambda i, j: (i, 3 * nj + j)),
                  pl.BlockSpec((tm, tn), lambda i, j: (i, 4 * nj + j)), out, out],
        out_specs=[out, out, out, out],
        out_shape=[bf, bf, bf, bf],
        compiler_params=_cparams(("parallel", "parallel")),
    )(do, wout, zrest, zrest, ya, yr)


def _branch_bwd(dya, dyr, wa, wr, zrest, yatt, ylru):
    t = dya.shape[0]
    tm = min(512, t)
    tn = D_MODEL
    nj = D_MODEL // tn

    def body(dya_ref, dyr_ref, wa_ref, wr_ref, ga_ref, gr_ref, ya_ref, yl_ref,
             dyatt_ref, dga_ref, dyl_ref, dgr_ref):
        dga = _dot_nt(dya_ref[...], wa_ref[...])
        dgr = _dot_nt(dyr_ref[...], wr_ref[...])
        g = ga_ref[...].astype(F32)
        sg = _sigmoid(g)
        dyatt_ref[...] = (dga * (g * sg)).astype(BF16)
        dga_ref[...] = (dga * ya_ref[...] * (sg * (1.0 + g * (1.0 - sg)))).astype(BF16)
        g = gr_ref[...].astype(F32)
        sg = _sigmoid(g)
        dyl_ref[...] = dgr * (g * sg)
        dgr_ref[...] = (dgr * yl_ref[...] * (sg * (1.0 + g * (1.0 - sg)))).astype(BF16)

    act = pl.BlockSpec((tm, D_MODEL), lambda i, j: (i, 0))
    wgt = pl.BlockSpec((tn, D_MODEL), lambda i, j: (j, 0))
    out = pl.BlockSpec((tm, tn), lambda i, j: (i, j))
    bf = jax.ShapeDtypeStruct((t, D_MODEL), BF16)
    return pl.pallas_call(
        body, name="branch_bwd", grid=(t // tm, nj),
        in_specs=[act, act, wgt, wgt, pl.BlockSpec((tm, tn), lambda i, j: (i, j)),
                  pl.BlockSpec((tm, tn), lambda i, j: (i, 2 * nj + j)), out, out],
        out_specs=[out, out, out, out],
        out_shape=[bf, bf, jax.ShapeDtypeStruct((t, D_MODEL), F32), bf],
        compiler_params=_cparams(("parallel", "parallel")),
    )(dya, dyr, wa, wr, zrest, zrest, yatt, ylru)


def _dh_final(parts, after, x2, dy, wpre):
    t = x2.shape[0]
    tm = min(256, t)
    np_ = len(parts)

    def body(*refs):
        x_ref, dy_ref, w_ref = refs[2 * np_ + 1:2 * np_ + 4]
        gx_ref, pw_ref = refs[2 * np_ + 4:]

        @pl.when(pl.program_id(0) == 0)
        def _():
            pw_ref[...] = jnp.zeros_like(pw_ref)

        dh = _dot(refs[0][...], refs[np_][...])
        for p in range(1, np_):
            dh = dh + _dot(refs[p][...], refs[np_ + p][...])
        x = x_ref[...]
        r = lax.rsqrt(jnp.mean(x * x, axis=-1, keepdims=True) + NORM_EPS)
        xn = x * r
        dxn = dh * w_ref[...]
        gx_ref[...] = r * (dxn - xn * jnp.mean(dxn * xn, axis=-1, keepdims=True)) + dy_ref[...]
        pw_ref[0:1, :] += jnp.sum(dh * xn, axis=0, keepdims=True)

    row = pl.BlockSpec((tm, D_MODEL), lambda i: (i, 0))
    in_specs = [pl.BlockSpec((tm, dz.shape[1]), lambda i: (i, 0)) for dz, _ in parts]
    in_specs += [pl.BlockSpec(w.shape, lambda i: (0, 0), pipeline_mode=pl.Buffered(1)) for _, w in parts]
    in_specs += [pl.BlockSpec(after.shape, lambda i: (0, 0)), row, row, pl.BlockSpec((1, D_MODEL), lambda i: (0, 0))]
    return pl.pallas_call(
        body, name="dh_final", grid=(t // tm,),
        in_specs=in_specs,
        out_specs=[row, pl.BlockSpec((8, D_MODEL), lambda i: (0, 0))],
        out_shape=[jax.ShapeDtypeStruct((t, D_MODEL), F32), jax.ShapeDtypeStruct((8, D_MODEL), F32)],
        compiler_params=_cparams(("arbitrary",), vmem_mb=48),
    )(*[dz for dz, _ in parts], *[w for _, w in parts], after, x2, dy, wpre)


def _adamw(w, g, m, v):
    m = ADAM_B1 * m + (1.0 - ADAM_B1) * g
    v = ADAM_B2 * v + (1.0 - ADAM_B2) * (g * g)
    m_hat = m / (1.0 - ADAM_B1 ** ADAM_STEP)
    v_hat = v / (1.0 - ADAM_B2 ** ADAM_STEP)
    delta = -ADAM_LR * (m_hat / (jnp.sqrt(v_hat) + ADAM_EPS) + ADAM_WD * w)
    return delta, m, v


def _reduce_adamw(own, parts, place, w, m, v, name):
    r, c = w.shape
    blk, nblk, at = _blocks_2d(r, c)

    def body(place_ref, own_ref, p_ref, w_ref, m_ref, v_ref, g_ref, d_ref, nm_ref, nv_ref):
        mine = place_ref[1]
        own_blk = own_ref[...]
        g = jnp.where(mine == 0, own_blk, p_ref[0].astype(F32))
        for j in range(1, N_CHIPS):
            g = g + jnp.where(mine == j, own_blk, p_ref[j].astype(F32))
        d, nm, nv = _adamw(w_ref[...], g, m_ref[...], v_ref[...])
        g_ref[...] = g
        d_ref[...] = d
        nm_ref[...] = nm
        nv_ref[...] = nv

    row = pl.BlockSpec(blk, lambda i, pr: at(i))
    sh = jax.ShapeDtypeStruct((r, c), F32)
    grid_spec = pltpu.PrefetchScalarGridSpec(
        num_scalar_prefetch=1, grid=(nblk,),
        in_specs=[row, pl.BlockSpec((N_CHIPS,) + blk, lambda i, pr: (0,) + at(i)), row, row, row],
        out_specs=[row, row, row, row])
    return pl.pallas_call(
        body, name=name, grid_spec=grid_spec, out_shape=[sh, sh, sh, sh],
        compiler_params=_cparams(("parallel",)),
    )(place, own, parts, w, m, v)


def _reduce_adamw_stacked(own, parts, place, triples, name):
    n = len(triples)
    _, r, c = triples[0][0].shape

    def body(place_ref, own_ref, p_ref, *refs):
        ins, outs = refs[:3 * n], refs[3 * n:]
        mine = place_ref[1]
        for i in range(n):
            rows = slice(i * r, (i + 1) * r)
            own_blk = own_ref[rows, :]
            g = jnp.where(mine == 0, own_blk, p_ref[0, rows, :].astype(F32))
            for j in range(1, N_CHIPS):
                g = g + jnp.where(mine == j, own_blk, p_ref[j, rows, :].astype(F32))
            d, nm, nv = _adamw(ins[3 * i][0], g, ins[3 * i + 1][0], ins[3 * i + 2][0])
            for k, val in enumerate((g, d, nm, nv)):
                outs[4 * i + k][0] = val

    whole = lambda shape: pl.BlockSpec(shape, lambda i, pr: (0,) * len(shape))
    grid_spec = pltpu.PrefetchScalarGridSpec(
        num_scalar_prefetch=1, grid=(1,),
        in_specs=[whole(own.shape), whole(parts.shape)] + [whole((1, r, c))] * (3 * n),
        out_specs=[whole((1, r, c))] * (4 * n))
    res = pl.pallas_call(
        body, name=name, grid_spec=grid_spec,
        out_shape=[jax.ShapeDtypeStruct((1, r, c), F32)] * (4 * n),
        compiler_params=_cparams(("arbitrary",)),
    )(place, own, parts, *[a for t3 in triples for a in t3])
    return [res[4 * i:4 * i + 4] for i in range(n)]


def _interleave_qkv(a):
    lead = a.shape[:-1]
    return a.reshape(lead + (3, HEAD_PAIRS, LANES)).swapaxes(-3, -2).reshape(lead + (3 * D_MODEL,))


def _deinterleave_qkv(a):
    lead = a.shape[:-1]
    return a.reshape(lead + (HEAD_PAIRS, 3, LANES)).swapaxes(-3, -2).reshape(lead + (3 * D_MODEL,))


def _interleave_rows(a):
    return a.reshape(3, HEAD_PAIRS, LANES, a.shape[1]).swapaxes(0, 1).reshape(a.shape)


def _deinterleave_rows(a):
    return a.reshape(HEAD_PAIRS, 3, LANES, a.shape[1]).swapaxes(0, 1).reshape(a.shape)


def _pack_small(pre, conv_b, rg_ba, rg_bx, lam, post, loss_row, b_in, conv_w_full, rg_wa, rg_wx):
    z = jnp.zeros((1, D_MODEL), F32)
    b_used = jnp.concatenate([b_in[:, 0:3 * D_MODEL], b_in[:, 3 * D_MODEL + HEADS:IN_TOTAL]], axis=1)
    b_f = jnp.pad(b_in[:, 3 * D_MODEL:3 * D_MODEL + HEADS], ((0, 0), (0, D_MODEL - HEADS)))
    return jnp.concatenate([
        pre, conv_b, rg_ba, rg_bx, lam, post, loss_row, z,
        b_used.reshape(9, D_MODEL), b_f, conv_w_full, z, z,
        rg_wa.reshape(64, D_MODEL), rg_wx.reshape(64, D_MODEL)], axis=0)


def _unpack_small(p):
    b_used = p[8:17].reshape(1, 9 * D_MODEL)
    b_in = jnp.concatenate([b_used[:, 0:3 * D_MODEL], p[17:18, 0:HEADS], b_used[:, 3 * D_MODEL:]], axis=1)
    return dict(pre_norm_w=p[0:1], conv_b=p[1:2], rg_ba=p[2:3], rg_bx=p[3:4], rg_lambda=p[4:5],
                post_norm_w=p[5:6], loss_row=p[6:7], b_in=b_in, conv_w_full=p[18:22],
                rg_wa=p[24:88].reshape(1, 16, 64, 64), rg_wx=p[88:152].reshape(1, 16, 64, 64))


def _reduce_small(parts, first, w, m, v, vectors):
    nvec = len(vectors)

    def body(p_ref, f_ref, w_ref, m_ref, v_ref, *refs):
        ins, outs = refs[:3 * nvec], refs[3 * nvec:]
        g = p_ref[0]
        g0 = f_ref[0, 0:1, :]
        for j in range(1, N_DEV):
            g = g + p_ref[j]
            g0 = g0 + f_ref[j, 0:1, :]
        d, nm, nv = _adamw(w_ref[...], g, m_ref[...], v_ref[...])
        for k, val in enumerate((g, d, nm, nv)):
            outs[k][...] = val
        for i in range(nvec):
            gi = g0 if i == 0 else g[i:i + 1, :]
            di, nmi, nvi = _adamw(ins[3 * i][...], gi, ins[3 * i + 1][...], ins[3 * i + 2][...])
            for k, val in enumerate((gi, di, nmi, nvi)):
                outs[4 + 4 * i + k][...] = val
        outs[-1][...] = jnp.zeros((8, LANES), F32) + (0.5 / D_MODEL) * jnp.sum(g[LOSS_ROW:LOSS_ROW + 1, :])

    sh = jax.ShapeDtypeStruct((SMALL_ROWS, D_MODEL), F32)
    vec = jax.ShapeDtypeStruct((1, D_MODEL), F32)
    res = pl.pallas_call(
        body, name="reduce_small",
        out_shape=[sh, sh, sh, sh] + [vec] * (4 * nvec) + [jax.ShapeDtypeStruct((8, LANES), F32)],
    )(parts, first, w, m, v, *[a for t3 in vectors for a in t3])
    return res[:4], [res[4 + 4 * i:8 + 4 * i] for i in range(nvec)], res[-1]


def kernel(x, pre_norm_w, w_in, b_in, conv_w, conv_b, rg_wa, rg_ba, rg_wx, rg_bx, rg_lambda, w_branch_a, w_branch_r, w_out, post_norm_w, loss_target, m_pre_norm_w, m_w_in, m_b_in, m_conv_w, m_conv_b, m_rg_wa, m_rg_ba, m_rg_wx, m_rg_bx, m_rg_lambda, m_w_branch_a, m_w_branch_r, m_w_out, m_post_norm_w, v_pre_norm_w, v_w_in, v_b_in, v_conv_w, v_conv_b, v_rg_wa, v_rg_ba, v_rg_wx, v_rg_bx, v_rg_lambda, v_w_branch_a, v_w_branch_r, v_w_out, v_post_norm_w):
    b, s, _ = x.shape
    t = b * s
    me = 4 * lax.axis_index("x") + 2 * lax.axis_index("y") + lax.axis_index("c")
    shard_rows = D_MODEL // N_DEV

    place = jnp.stack([lax.axis_index("c"), 2 * lax.axis_index("x") + lax.axis_index("y")]).astype(jnp.int32)
    w_in_all = _gather(w_in[0].T.astype(BF16), "gather_w_in")
    wt_full = w_in_all.reshape(IN_TOTAL, D_MODEL)
    conv_terms = jnp.concatenate(_split3(conv_w[0]), axis=0)
    conv_pad = jnp.pad(conv_terms, ((0, 16 - 3 * CONV_W), (0, D_MODEL - LANES)))
    sq_stack = jnp.concatenate([w_branch_a[0].astype(BF16), w_branch_r[0].astype(BF16), w_out[0].astype(BF16),
                                conv_pad], axis=0)
    sq_sems, sq_src, sq_land, sq_token = _gather_start(sq_stack, w_in_all, "gather_w_sq_start")

    w_qkv = _interleave_rows(wt_full[0:3 * D_MODEL])
    w_f = jnp.pad(wt_full[3 * D_MODEL:3 * D_MODEL + HEADS], ((0, LANES - HEADS), (0, 0)))
    w_rest = wt_full[3 * D_MODEL + HEADS:IN_USED]
    b_qkv = _interleave_qkv(b_in[:, 0:3 * D_MODEL]) + sq_token[0, 0]
    b_f = jnp.pad(b_in[:, 3 * D_MODEL:3 * D_MODEL + HEADS], ((0, 0), (0, LANES - HEADS)))
    b_rest = b_in[:, 3 * D_MODEL + HEADS:IN_USED]

    def blockdiag(w):
        w2 = w.reshape(N_CBLK, 2, HEAD_DIM, HEAD_DIM)
        zz = jnp.zeros((N_CBLK, HEAD_DIM, HEAD_DIM), w.dtype)
        top = jnp.concatenate([w2[:, 0], zz], axis=2)
        bot = jnp.concatenate([zz, w2[:, 1]], axis=2)
        return jnp.concatenate([top, bot], axis=1).astype(BF16)

    bda, bdx = blockdiag(rg_wa[0]), blockdiag(rg_wx[0])

    x2 = x.reshape(t, D_MODEL)
    tgt2 = loss_target.reshape(t, D_MODEL)
    h, qkv, zf = _prenorm_inproj(x2, pre_norm_w, w_qkv, b_qkv, w_f, b_f)
    zrest = _mm_bias(h, w_rest, b_rest, BF16, "inproj_rest")
    qkv3 = qkv.reshape(b, s, 3 * D_MODEL)
    zrest3 = zrest.reshape(b, s, 5 * D_MODEL)
    zf3 = zf.reshape(b, s, LANES)
    cexp3, crow = _fgate_fwd(zf3)
    yatt3, lse, ga = _attn_fwd(qkv3, cexp3, crow, zrest3)

    sq_all = _gather_wait(sq_sems, sq_src, sq_land, ga, "gather_w_sq_wait")
    sq_all = lax.dynamic_update_slice(sq_all, sq_stack[None], (me, 0, 0))
    wa = sq_all[:, 0:shard_rows].reshape(D_MODEL, D_MODEL)
    wr = sq_all[:, shard_rows:2 * shard_rows].reshape(D_MODEL, D_MODEL)
    wo = sq_all[:, 2 * shard_rows:3 * shard_rows].reshape(D_MODEL, D_MODEL)
    conv_all = sq_all[:, 3 * shard_rows:3 * shard_rows + 3 * CONV_W, 0:LANES].astype(F32)
    conv_all = (conv_all[:, 0:CONV_W] + conv_all[:, CONV_W:2 * CONV_W]) + conv_all[:, 2 * CONV_W:3 * CONV_W]
    conv_full = conv_all.transpose(1, 0, 2).reshape(CONV_W, D_MODEL)

    ylru3, gr = _rnn_fwd(zrest3, conv_full, conv_b, bda, bdx, rg_ba, rg_bx, rg_lambda)
    ya, yr, mm = _branch_merge(ga, gr, wa, wr, zrest)
    dy, do, acc_out = _out_loss(mm, wo, x2, tgt2, post_norm_w)

    dya, dyr, dz_mga, dz_mgr = _merge_bwd(do, wo, zrest, ya, yr)
    dyatt, dz_ga, dylru, dz_gr = _branch_bwd(dya, dyr, wa, wr, zrest, yatt3.reshape(t, D_MODEL),
                                             ylru3.reshape(t, D_MODEL))
    dz_xr, pvec, dbd = _rnn_bwd(zrest3, ylru3, dylru.reshape(b, s, D_MODEL), conv_full, conv_b, bda, bdx,
                                rg_ba, rg_bx, rg_lambda)
    dz_qkv, dc3 = _attn_bwd(qkv3, dyatt.reshape(b, s, D_MODEL), yatt3, lse, crow, cexp3)
    dz_f = _fgate_bwd(dc3, zf3)

    dw_qkv, db_qkv = _mm_tn(dz_qkv, h, "dw_qkv")
    dw_f, db_f = _mm_tn(dz_f, h, "dw_f")
    dw_parts, db_parts = [], []
    for nm, dzp in (("ga", dz_ga), ("xr", dz_xr), ("gr", dz_gr), ("mga", dz_mga), ("mgr", dz_mgr)):
        dwp, dbp = _mm_tn(dzp, h, "dw_" + nm)
        dw_parts.append(dwp)
        db_parts.append(dbp[0:1])

    zeros_tail = jnp.zeros((IN_TOTAL - IN_USED, D_MODEL), F32)
    dwt_full = jnp.concatenate([_deinterleave_rows(dw_qkv), dw_f[0:HEADS]] + dw_parts + [zeros_tail], axis=0)
    dw_in_send = dwt_full.reshape(N_CHIPS, 2, W_SHARD, D_MODEL).transpose(1, 0, 2, 3)
    swp_sems, dw_in_src, swp_land, swp_token = _swap_start(dw_in_send, db_f, "swap_dw_in_start")
    dw_a, _ = _mm_tn(ga, dya, "dw_a", after=swp_token)
    dw_r, _ = _mm_tn(gr, dyr, "dw_r", after=swp_token)
    dw_o, _ = _mm_tn(mm, do, "dw_o", after=swp_token)
    dw_in_send, sib_in = _swap_wait(swp_sems, dw_in_src, swp_land, dw_o, "swap_dw_in_wait")
    by_dest = lambda a: a.reshape(N_CHIPS, 2, shard_rows, D_MODEL).transpose(1, 0, 2, 3)
    dw_sq_send = jnp.concatenate([by_dest(dw_a), by_dest(dw_r), by_dest(dw_o)], axis=2)

    db_in_full = jnp.concatenate([_deinterleave_qkv(db_qkv[0:1]), db_f[0:1, 0:HEADS]] + db_parts
                                 + [jnp.zeros((1, IN_TOTAL - IN_USED), F32)], axis=1)
    d_rg_wa = jnp.stack([dbd[:, 0, 0:HEAD_DIM, 0:HEAD_DIM], dbd[:, 0, HEAD_DIM:, HEAD_DIM:]], axis=1)
    d_rg_wx = jnp.stack([dbd[:, 1, 0:HEAD_DIM, 0:HEAD_DIM], dbd[:, 1, HEAD_DIM:, HEAD_DIM:]], axis=1)
    small_g = _pack_small(jnp.zeros((1, D_MODEL), F32), pvec[4:5], pvec[5:6], pvec[6:7], pvec[7:8], acc_out[0:1],
                          acc_out[1:2], db_in_full, pvec[0:4], d_rg_wa, d_rg_wx)
    sm_sems, sm_src, sm_land, sm_token = _gather_start(small_g, dw_o, "gather_small_start")

    sqs_sems, dw_sq_src, sqs_land, sqs_token = _swap_start(dw_sq_send, sm_token, "swap_dw_sq_start")
    chip_in, own_in = _pair_add(dw_in_send, sib_in, place, "pair_add_in", after=sqs_token)
    dw_sq_send, sib_sq = _swap_wait(sqs_sems, dw_sq_src, sqs_land, chip_in, "swap_dw_sq_wait")
    chip_sq, own_sq = _pair_add(dw_sq_send, sib_sq, place, "pair_add_sq")
    sems, sent, lands, token = _exchange_chips_start([chip_in, chip_sq], "exchange_dw_start")

    wt = lambda lo: w_rest[lo * D_MODEL:(lo + 1) * D_MODEL]
    grad_x2, acc_pre = _dh_final(
        [(dz_qkv, w_qkv), (dz_f, w_f), (dz_ga, wt(0)), (dz_xr, wt(1)), (dz_gr, wt(2)), (dz_mga, wt(3)),
         (dz_mgr, wt(4))], token, x2, dy, pre_norm_w)
    pre_sems, pre_src, pre_land, pre_token = _gather_start(acc_pre, grad_x2, "gather_pre_start")
    recv_in, recv_sq = _exchange_chips_wait(sems, sent, lands, pre_token, "exchange_dw_wait")

    g_in, d_in, nm_in, nv_in = [a.T for a in _reduce_adamw(
        own_in, recv_in, place, w_in[0].T, m_w_in[0].T, v_w_in[0].T, "adamw_w_in")]
    sq_out = _reduce_adamw_stacked(
        own_sq, recv_sq, place,
        [(w_branch_a, m_w_branch_a, v_w_branch_a), (w_branch_r, m_w_branch_r, v_w_branch_r),
         (w_out, m_w_out, v_w_out)], "adamw_w_sq")
    pre_all = _gather_wait(pre_sems, pre_src, pre_land, sq_out[2][1], "gather_pre_wait")
    pre_all = lax.dynamic_update_slice(pre_all, acc_pre[None], (me, 0, 0))
    small_all = _gather_wait(sm_sems, sm_src, sm_land, pre_all, "gather_small_wait")
    small_all = lax.dynamic_update_slice(small_all, small_g[None], (me, 0, 0))

    def place_conv(a):
        return lax.dynamic_update_slice(jnp.zeros((CONV_W, D_MODEL), F32), a[0], (0, me * LANES))

    zrow = jnp.zeros((1, D_MODEL), F32)
    vector_names = ["pre_norm_w", "conv_b", "rg_ba", "rg_bx", "rg_lambda", "post_norm_w"]
    vectors = [(pre_norm_w, m_pre_norm_w, v_pre_norm_w), (conv_b, m_conv_b, v_conv_b), (rg_ba, m_rg_ba, v_rg_ba),
               (rg_bx, m_rg_bx, v_rg_bx), (rg_lambda, m_rg_lambda, v_rg_lambda),
               (post_norm_w, m_post_norm_w, v_post_norm_w)]
    small_w = _pack_small(zrow, zrow, zrow, zrow, zrow, zrow, zrow, b_in, place_conv(conv_w), rg_wa[0], rg_wx[0])
    small_m = _pack_small(zrow, zrow, zrow, zrow, zrow, zrow, zrow, m_b_in, place_conv(m_conv_w), m_rg_wa[0],
                          m_rg_wx[0])
    small_v = _pack_small(zrow, zrow, zrow, zrow, zrow, zrow, zrow, v_b_in, place_conv(v_conv_w), v_rg_wa[0],
                          v_rg_wx[0])
    packed, vector_out, loss_tile = _reduce_small(small_all, pre_all, small_w, small_m, small_v, vectors)
    outs_small = [_unpack_small(p) for p in packed]
    loss = loss_tile[0, 0]

    def leaf(kind, name):
        if name == "w_in":
            return (g_in, d_in, nm_in, nv_in)[kind][None]
        if name in ("w_branch_a", "w_branch_r", "w_out"):
            return sq_out[("w_branch_a", "w_branch_r", "w_out").index(name)][kind]
        if name == "conv_w":
            return lax.dynamic_slice(outs_small[kind]["conv_w_full"], (0, me * LANES), (CONV_W, LANES))[None]
        if name in vector_names:
            return vector_out[vector_names.index(name)][kind]
        return outs_small[kind][name]

    names = ["pre_norm_w", "w_in", "b_in", "conv_w", "conv_b", "rg_wa", "rg_ba", "rg_wx", "rg_bx", "rg_lambda",
             "w_branch_a", "w_branch_r", "w_out", "post_norm_w"]
    out = [loss, grad_x2.reshape(b, s, D_MODEL)]
    for kind in range(4):
        out += [leaf(kind, nm) for nm in names]
    return tuple(out)
```

```python
import jax
import jax.numpy as jnp
from jax import lax
from jax.experimental import pallas as pl
from jax.experimental.pallas import tpu as pltpu

F32 = jnp.float32
BF16 = jnp.bfloat16

N_DEV = 8
D_MODEL = 1024
HEADS = 16
HEAD_DIM = 64
HEAD_PAIRS = HEADS // 2
LANES = 128
N_CBLK = D_MODEL // LANES
CONV_W = 4
RG_C = 8.0
NORM_EPS = 1e-6
MASK_VALUE = -1e30
IN_USED = 8208
IN_TOTAL = 9232
W_SHARD = IN_TOTAL // N_DEV

ADAM_LR = 0.001
ADAM_B1 = 0.9
ADAM_B2 = 0.999
ADAM_EPS = 1e-08
ADAM_WD = 0.01
ADAM_STEP = 10

ATT_TILE_FWD = 256
ATT_TILE_BWD = 512
SCAN_TILE = 256
SMALL_ROWS = 152
LOSS_ROW = 6


def _cparams(sem=None, vmem_mb=None):
    kw = {}
    if sem is not None:
        kw["dimension_semantics"] = sem
    if vmem_mb is not None:
        kw["vmem_limit_bytes"] = vmem_mb * 1024 * 1024
    return pltpu.CompilerParams(**kw)


def _sigmoid(x):
    return 1.0 / (1.0 + jnp.exp(-x))


def _softplus(x):
    return jnp.maximum(x, 0.0) + jnp.log1p(jnp.exp(-jnp.abs(x)))


def _one_minus_exp(y, exp_y):
    series = -y * (1.0 + y * (1.0 / 2 + y * (1.0 / 6 + y * (1.0 / 24 + y * (1.0 / 120)))))
    return jnp.where(y > -0.0625, series, 1.0 - exp_y)


def _split3(x):
    hi = x.astype(BF16)
    r1 = x - hi.astype(F32)
    mid = r1.astype(BF16)
    lo = (r1 - mid.astype(F32)).astype(BF16)
    return hi, mid, lo


def _dot(a, b):
    return jnp.dot(a, b, preferred_element_type=F32)


def _dot_nt(a, b):
    return lax.dot_general(a, b, (((1,), (1,)), ((), ())), preferred_element_type=F32)


def _dot_tn(a, b):
    return lax.dot_general(a, b, (((0,), (0,)), ((), ())), preferred_element_type=F32)


def _iota(shape, dim):
    return lax.broadcasted_iota(jnp.int32, shape, dim)


_ANY = pl.BlockSpec(memory_space=pl.ANY)
_MESH = pl.DeviceIdType.MESH
N_CHIPS = 4


def _place():
    x, y, c = lax.axis_index("x"), lax.axis_index("y"), lax.axis_index("c")
    other_chips = [(1 - x, y), (x, 1 - y), (1 - x, 1 - y)]
    return x, y, c, other_chips


def _gather(x_shard, name):
    def body(x_ref, out_ref, send_sems, recv_sems, local_sem):
        x, y, c, chips = _place()
        me, sibling = (x, y, c), (x, y, 1 - c)

        def slot(p):
            return out_ref.at[4 * p[0] + 2 * p[1] + p[2]]

        def copy(k, block, to, src=None):
            return pltpu.make_async_remote_copy(
                src_ref=slot(block) if src is None else src, dst_ref=slot(block),
                send_sem=send_sems.at[k], recv_sem=recv_sems.at[k], device_id=to, device_id_type=_MESH)

        mine = pltpu.make_async_copy(x_ref, slot(me), local_sem)
        mine.start()
        first = [copy(0, me, sibling, src=x_ref)]
        first += [copy(1 + j, me, (*chip, c), src=x_ref) for j, chip in enumerate(chips)]
        for cp in first:
            cp.start()
        passed = [copy(4 + j, (*chip, c), sibling) for j, chip in enumerate(chips)]
        for j, chip in enumerate(chips):
            copy(1 + j, (*chip, c), me).wait_recv()
            passed[j].start()
        copy(0, sibling, me).wait_recv()
        for j, chip in enumerate(chips):
            copy(4 + j, (*chip, 1 - c), me).wait_recv()
        for cp in first + passed:
            cp.wait_send()
        mine.wait()

    return pl.pallas_call(
        body, name=name,
        out_shape=jax.ShapeDtypeStruct((N_DEV,) + tuple(x_shard.shape), x_shard.dtype),
        in_specs=[_ANY], out_specs=_ANY,
        scratch_shapes=[pltpu.SemaphoreType.DMA((7,)), pltpu.SemaphoreType.DMA((7,)), pltpu.SemaphoreType.DMA],
    )(x_shard)


def _blocks_2d(r, c):
    if r % 128 == 0:
        return (128, c), r // 128, lambda i: (i, 0)
    return (r, 256), c // 256, lambda i: (0, i)


def _pair_add(src, recv, place, name, after=None):
    _, _, r, c = src.shape
    blk, nblk, at = _blocks_2d(r, c)
    deps = [] if after is None else [after]

    def body(place_ref, a_ref, b_ref, *refs):
        q16_ref, own_ref = refs[len(deps):]
        q = a_ref[...] + b_ref[...]
        q16_ref[...] = q.astype(BF16)

        @pl.when(pl.program_id(1) == place_ref[1])
        def _():
            own_ref[...] = q

    grid_spec = pltpu.PrefetchScalarGridSpec(
        num_scalar_prefetch=1, grid=(nblk, N_CHIPS),
        in_specs=[pl.BlockSpec((None, None) + blk, lambda i, j, pr: (pr[0], j) + at(i)),
                  pl.BlockSpec((None,) + blk, lambda i, j, pr: (j,) + at(i))]
        + [pl.BlockSpec(d.shape, lambda i, j, pr: (0, 0)) for d in deps],
        out_specs=[pl.BlockSpec((None,) + blk, lambda i, j, pr: (j,) + at(i)),
                   pl.BlockSpec(blk, lambda i, j, pr: at(i))])
    return pl.pallas_call(
        body, name=name, grid_spec=grid_spec,
        out_shape=[jax.ShapeDtypeStruct((N_CHIPS, r, c), BF16), jax.ShapeDtypeStruct((r, c), F32)],
        compiler_params=_cparams(("parallel", "arbitrary")),
    )(place, src, recv, *deps)


_HBM = pl.BlockSpec(memory_space=pltpu.HBM)
_SEM = pl.BlockSpec(memory_space=pltpu.SEMAPHORE)
_DATAFLOW = pltpu.SideEffectType.DATAFLOW_SIDE_EFFECTING


def _chip_copy(src_ref, land_ref, send_sem, recv_sem, k, chips, c, land):
    chip = chips[k]
    return pltpu.make_async_remote_copy(
        src_ref=src_ref.at[2 * chip[0] + chip[1]], dst_ref=land_ref.at[land],
        send_sem=send_sem, recv_sem=recv_sem, device_id=(*chip, c), device_id_type=_MESH)


def _exchange_chips_start(srcs, name):
    n = len(srcs)
    ncp = 3 * n

    def body(*refs):
        src_refs, land_refs = refs[:n], refs[n:2 * n]
        sems = refs[4 * n:4 * n + 2 * ncp]
        token = refs[-1]
        x, y, c, chips = _place()
        for i in range(n):
            for k in range(3):
                j = 3 * i + k
                _chip_copy(src_refs[i], land_refs[i], sems[j], sems[ncp + j], k, chips, c, 2 * x + y).start()
        token[...] = jnp.zeros_like(token)

    hbm = [pltpu.HBM(a.shape, a.dtype) for a in srcs]
    lands = [pltpu.with_memory_space_constraint(lax.empty(a.shape, a.dtype), pltpu.HBM) for a in srcs]
    res = pl.pallas_call(
        body, name=name,
        out_shape=(*hbm, *hbm, *([pltpu.SemaphoreType.DMA(())] * (2 * ncp)), jax.ShapeDtypeStruct((8, LANES), F32)),
        in_specs=[_HBM] * (2 * n),
        out_specs=(*([_HBM] * (2 * n)), *([_SEM] * (2 * ncp)), pl.BlockSpec(memory_space=pltpu.VMEM)),
        input_output_aliases={i: i for i in range(2 * n)},
        compiler_params=pltpu.CompilerParams(has_side_effects=_DATAFLOW),
    )(*[pltpu.with_memory_space_constraint(a, pltpu.HBM) for a in srcs], *lands)
    return list(res[2 * n:2 * n + 2 * ncp]), list(res[:n]), list(res[n:2 * n]), res[-1]


def _exchange_chips_wait(sems, srcs, lands, after, name):
    n = len(srcs)
    ncp = 3 * n

    def body(*refs):
        src_refs, land_refs = refs[:n], refs[n:2 * n]
        sem_refs = refs[2 * n:2 * n + 2 * ncp]
        x, y, c, chips = _place()
        for i in range(n):
            for k in range(3):
                j = 3 * i + k
                cp = _chip_copy(src_refs[i], land_refs[i], sem_refs[j], sem_refs[ncp + j], k, chips, c,
                                2 * chips[k][0] + chips[k][1])
                cp.wait_send()
                cp.wait_recv()

    hbm = [pltpu.HBM(a.shape, a.dtype) for a in srcs]
    res = pl.pallas_call(
        body, name=name, out_shape=(*hbm, *hbm),
        in_specs=[_HBM] * (2 * n) + [_SEM] * (2 * ncp) + [_ANY], out_specs=tuple([_HBM] * (2 * n)),
        input_output_aliases={i: i for i in range(2 * n)},
        compiler_params=pltpu.CompilerParams(has_side_effects=_DATAFLOW),
    )(*srcs, *lands, *sems, after)
    return list(res[n:2 * n])


def _swap_start(src, after, name):
    def body(src_ref, land_ref, after_ref, src_thru, land_thru, send_sem, recv_sem, token):
        x, y, c, _ = _place()
        pltpu.make_async_remote_copy(src_ref=src_ref.at[1 - c], dst_ref=land_ref, send_sem=send_sem,
                                     recv_sem=recv_sem, device_id=(x, y, 1 - c), device_id_type=_MESH).start()
        token[...] = jnp.zeros_like(token)

    land = pltpu.with_memory_space_constraint(lax.empty(src.shape[1:], src.dtype), pltpu.HBM)
    res = pl.pallas_call(
        body, name=name,
        out_shape=(pltpu.HBM(src.shape, src.dtype), pltpu.HBM(land.shape, land.dtype),
                   pltpu.SemaphoreType.DMA(()), pltpu.SemaphoreType.DMA(()), jax.ShapeDtypeStruct((8, LANES), F32)),
        in_specs=[_HBM, _HBM, _ANY],
        out_specs=(_HBM, _HBM, _SEM, _SEM, pl.BlockSpec(memory_space=pltpu.VMEM)),
        input_output_aliases={0: 0, 1: 1},
        compiler_params=pltpu.CompilerParams(has_side_effects=_DATAFLOW),
    )(pltpu.with_memory_space_constraint(src, pltpu.HBM), land, after)
    return [res[2], res[3]], res[0], res[1], res[-1]


def _swap_wait(sems, src, land, after, name):
    def body(src_ref, land_ref, send_sem, recv_sem, after_ref, src_out, land_out):
        x, y, c, _ = _place()
        cp = pltpu.make_async_remote_copy(src_ref=src_ref.at[1 - c], dst_ref=land_ref, send_sem=send_sem,
                                          recv_sem=recv_sem, device_id=(x, y, 1 - c), device_id_type=_MESH)
        cp.wait_send()
        cp.wait_recv()

    res = pl.pallas_call(
        body, name=name, out_shape=(pltpu.HBM(src.shape, src.dtype), pltpu.HBM(land.shape, land.dtype)),
        in_specs=[_HBM, _HBM, _SEM, _SEM, _ANY], out_specs=(_HBM, _HBM),
        input_output_aliases={0: 0, 1: 1},
        compiler_params=pltpu.CompilerParams(has_side_effects=_DATAFLOW),
    )(src, land, *sems, after)
    return res[0], res[1]


def _peer_copy(src_ref, land_ref, send_sem, recv_sem, k, place, land):
    x, y, c = place
    peer = (1 - x if k & 4 else x, 1 - y if k & 2 else y, 1 - c if k & 1 else c)
    return pltpu.make_async_remote_copy(
        src_ref=src_ref, dst_ref=land_ref.at[land], send_sem=send_sem, recv_sem=recv_sem,
        device_id=peer, device_id_type=_MESH)


def _gather_start(x_shard, after, name):
    npeer = N_DEV - 1

    def body(x_ref, land_ref, after_ref, x_thru, land_thru, *rest):
        sems, token = rest[:2 * npeer], rest[-1]
        x, y, c, _ = _place()
        for k in range(1, N_DEV):
            _peer_copy(x_ref, land_ref, sems[k - 1], sems[npeer + k - 1], k, (x, y, c), 4 * x + 2 * y + c).start()
        token[...] = jnp.zeros_like(token)

    land = pltpu.with_memory_space_constraint(lax.empty((N_DEV,) + tuple(x_shard.shape), x_shard.dtype), pltpu.HBM)
    res = pl.pallas_call(
        body, name=name,
        out_shape=(pltpu.HBM(x_shard.shape, x_shard.dtype), pltpu.HBM(land.shape, land.dtype),
                   *([pltpu.SemaphoreType.DMA(())] * (2 * npeer)), jax.ShapeDtypeStruct((8, LANES), F32)),
        in_specs=[_HBM, _HBM, _ANY],
        out_specs=(_HBM, _HBM, *([_SEM] * (2 * npeer)), pl.BlockSpec(memory_space=pltpu.VMEM)),
        input_output_aliases={0: 0, 1: 1},
        compiler_params=pltpu.CompilerParams(has_side_effects=_DATAFLOW),
    )(pltpu.with_memory_space_constraint(x_shard, pltpu.HBM), land, after)
    return list(res[2:2 + 2 * npeer]), res[0], res[1], res[-1]


def _gather_wait(sems, src, land, after, name):
    npeer = N_DEV - 1

    def body(x_ref, land_ref, *rest):
        sem_refs = rest[:2 * npeer]
        x, y, c, _ = _place()
        for k in range(1, N_DEV):
            peer_index = (4 * x + 2 * y + c) ^ k
            cp = _peer_copy(x_ref, land_ref, sem_refs[k - 1], sem_refs[npeer + k - 1], k, (x, y, c), peer_index)
            cp.wait_send()
            cp.wait_recv()

    res = pl.pallas_call(
        body, name=name, out_shape=(pltpu.HBM(src.shape, src.dtype), pltpu.HBM(land.shape, land.dtype)),
        in_specs=[_HBM, _HBM] + [_SEM] * (2 * npeer) + [_ANY], out_specs=(_HBM, _HBM),
        input_output_aliases={0: 0, 1: 1},
        compiler_params=pltpu.CompilerParams(has_side_effects=_DATAFLOW),
    )(src, land, *sems, after)
    return res[1]


def _prenorm_inproj(x2, w, wt_qkv, b_qkv, wt_f, b_f):
    t = x2.shape[0]
    tm = min(512, t)
    n = wt_qkv.shape[0]
    tn = D_MODEL

    def body(x_ref, w_ref, wq_ref, bq_ref, wf_ref, bf_ref, h_ref, qkv_ref, zf_ref):
        x = x_ref[...]
        r = lax.rsqrt(jnp.mean(x * x, axis=-1, keepdims=True) + NORM_EPS)
        h = (x * r * w_ref[...]).astype(BF16)
        h_ref[...] = h
        for j in range(n // tn):
            cols = slice(j * tn, (j + 1) * tn)
            qkv_ref[:, cols] = (_dot_nt(h, wq_ref[cols, :]) + bq_ref[:, cols]).astype(BF16)
        zf_ref[...] = _dot_nt(h, wf_ref[...]) + bf_ref[...]

    row = lambda c: pl.BlockSpec((tm, c), lambda i: (i, 0))
    whole = lambda a: pl.BlockSpec(a.shape, lambda i: (0, 0))
    return pl.pallas_call(
        body, name="prenorm_inproj_qkv", grid=(t // tm,),
        in_specs=[row(D_MODEL), whole(w), whole(wt_qkv), whole(b_qkv), whole(wt_f), whole(b_f)],
        out_specs=[row(D_MODEL), row(n), row(LANES)],
        out_shape=[jax.ShapeDtypeStruct((t, D_MODEL), BF16), jax.ShapeDtypeStruct((t, n), BF16),
                   jax.ShapeDtypeStruct((t, LANES), F32)],
        compiler_params=_cparams(("parallel",), vmem_mb=48),
    )(x2, w, wt_qkv, b_qkv, wt_f, b_f)


def _mm_bias(a, bt, bias, out_dtype, name):
    m, k = a.shape
    n = bt.shape[0]
    tm = min(1024, m)
    tn = min(1024, n)

    def body(a_ref, bt_ref, bias_ref, o_ref):
        aa = a_ref[...]
        for j in range(n // tn):
            cols = slice(j * tn, (j + 1) * tn)
            o_ref[:, cols] = (_dot_nt(aa, bt_ref[cols, :]) + bias_ref[:, cols]).astype(o_ref.dtype)

    return pl.pallas_call(
        body, name=name, grid=(m // tm,),
        in_specs=[pl.BlockSpec((tm, k), lambda i: (i, 0)),
                  pl.BlockSpec((n, k), lambda i: (0, 0), pipeline_mode=pl.Buffered(1)),
                  pl.BlockSpec((1, n), lambda i: (0, 0))],
        out_specs=pl.BlockSpec((tm, n), lambda i: (i, 0)),
        out_shape=jax.ShapeDtypeStruct((m, n), out_dtype),
        compiler_params=_cparams(("parallel",), vmem_mb=48),
    )(a, bt, bias)


def _mm_tn(a, b, name, after=None):
    t, m = a.shape
    n = b.shape[1]
    tm = min(1024, m)
    tk = min(2048, t)
    deps = [] if after is None else [after]

    def body(a_ref, b_ref, *refs):
        o_ref, s_ref = refs[len(deps):]
        kk = pl.program_id(1)

        @pl.when(kk == 0)
        def _():
            o_ref[...] = jnp.zeros_like(o_ref)
            s_ref[...] = jnp.zeros_like(s_ref)

        aa = a_ref[...]
        o_ref[...] += _dot_tn(aa, b_ref[...])
        s_ref[0:1, :] += jnp.sum(aa.astype(F32), axis=0, keepdims=True)

    return pl.pallas_call(
        body, name=name, grid=(m // tm, t // tk),
        in_specs=[pl.BlockSpec((tk, tm), lambda i, kk: (kk, i)), pl.BlockSpec((tk, n), lambda i, kk: (kk, 0))]
        + [pl.BlockSpec(d.shape, lambda i, kk: (0, 0)) for d in deps],
        out_specs=[pl.BlockSpec((tm, n), lambda i, kk: (i, 0)), pl.BlockSpec((8, tm), lambda i, kk: (0, i))],
        out_shape=[jax.ShapeDtypeStruct((m, n), F32), jax.ShapeDtypeStruct((8, m), F32)],
        compiler_params=_cparams(("parallel", "arbitrary"), vmem_mb=48),
    )(a, b, *deps)


def _fgate_fwd(zf3):
    b, s, _ = zf3.shape
    tb = SCAN_TILE
    nb = s // tb

    def body(z_ref, cexp_ref, crow_ref):
        tri = (_iota((tb, tb), 1) <= _iota((tb, tb), 0)).astype(BF16)
        expand = ((_iota((LANES, D_MODEL), 1) >> 6) == _iota((LANES, D_MODEL), 0)).astype(BF16)
        carry = jnp.zeros((1, LANES), F32)
        for i in range(nb):
            rows = slice(i * tb, (i + 1) * tb)
            z = z_ref[rows, :]
            lf = jnp.minimum(z, 0.0) - jnp.log1p(jnp.exp(-jnp.abs(z)))
            cb = sum(_dot(tri, part) for part in _split3(lf)) + carry
            carry = cb[tb - 1:tb, :]
            cexp_ref[rows, :] = sum(_dot(part, expand) for part in _split3(cb))
            crow_ref[:, rows] = cb.T[0:HEADS, :]

    return pl.pallas_call(
        body, name="fgate_fwd", grid=(b,),
        in_specs=[pl.BlockSpec((None, s, LANES), lambda i: (i, 0, 0))],
        out_specs=[pl.BlockSpec((None, s, D_MODEL), lambda i: (i, 0, 0)),
                   pl.BlockSpec((None, HEADS, s), lambda i: (i, 0, 0))],
        out_shape=[jax.ShapeDtypeStruct((b, s, D_MODEL), F32), jax.ShapeDtypeStruct((b, HEADS, s), F32)],
        compiler_params=_cparams(("parallel",)),
    )(zf3)


def _fgate_bwd(dc3, zf3):
    b, s, _ = zf3.shape
    tb = SCAN_TILE
    nb = s // tb

    def body(dc_ref, z_ref, o_ref):
        tri = (_iota((tb, tb), 1) >= _iota((tb, tb), 0)).astype(BF16)
        carry = jnp.zeros((1, LANES), F32)
        for i in reversed(range(nb)):
            rows = slice(i * tb, (i + 1) * tb)
            dlf = sum(_dot(tri, part) for part in _split3(dc_ref[rows, :])) + carry
            carry = dlf[0:1, :]
            o_ref[rows, :] = (dlf * _sigmoid(-z_ref[rows, :])).astype(BF16)

    return pl.pallas_call(
        body, name="fgate_bwd", grid=(b,),
        in_specs=[pl.BlockSpec((None, s, LANES), lambda i: (i, 0, 0)),
                  pl.BlockSpec((None, s, LANES), lambda i: (i, 0, 0))],
        out_specs=pl.BlockSpec((s, LANES), lambda i: (i, 0)),
        out_shape=jax.ShapeDtypeStruct((b * s, LANES), BF16),
        compiler_params=_cparams(("parallel",)),
    )(dc3, zf3)


def _spare(hh):
    return HEAD_DIM if hh == 0 else 0


def _put_cols(tile, mine, cols, first):
    lane = _iota((1, LANES), 1)
    out = jnp.where(mine, tile, jnp.zeros((), tile.dtype))
    for j, c in enumerate(cols):
        out = jnp.where(lane == first + j, c, out)
    return out


def _put_rows(tile, mine, rows, first):
    sub = _iota((LANES, 1), 0)
    out = jnp.where(mine, tile, jnp.zeros((), tile.dtype))
    for j, r in enumerate(rows):
        out = jnp.where(sub == first + j, r, out)
    return out


def _transpose_bf16(a):
    return a.astype(F32).T.astype(BF16)


def _attn_fwd(qkv3, cexp3, crow, zrest3):
    b, s, _ = qkv3.shape
    ta = ATT_TILE_FWD
    nq = s // ta
    hd = HEAD_DIM
    crow5 = crow.reshape(b, HEAD_PAIRS, 2, nq, ta)

    def body(qkv_ref, cq_ref, ck_ref, g_ref, y_ref, lse_ref, ga_ref, kt_scr, v_scr):
        lane = _iota((1, LANES), 1)
        sub = _iota((LANES, 1), 0)
        lane_mine = (lane < hd, lane >= hd)
        sub_mine = (sub < hd, sub >= hd)
        causal = _iota((ta, ta), 0) >= _iota((ta, ta), 1)
        one = jnp.ones((), BF16)

        for kj in range(nq):
            rows = slice(kj * ta, (kj + 1) * ta)
            kt = _transpose_bf16(qkv_ref[rows, LANES:2 * LANES])
            v = qkv_ref[rows, 2 * LANES:3 * LANES]
            for hh in range(2):
                ck = list(_split3(-ck_ref[hh, kj:kj + 1, :]))
                kt_scr[hh, kj] = _put_rows(kt, sub_mine[hh], [one, one, one] + ck, _spare(hh))
                v_scr[hh, kj] = _put_cols(v, lane_mine[hh], [one], _spare(hh))

        for qi in range(nq):
            rows = slice(qi * ta, (qi + 1) * ta)
            q = qkv_ref[rows, 0:LANES] * 0.125
            cq = cq_ref[rows, :]
            qh = [_put_cols(q, lane_mine[hh], list(_split3(cq[:, hh * hd:hh * hd + 1])) + [one, one, one], _spare(hh))
                  for hh in range(2)]
            st = [(jnp.full((ta, 1), MASK_VALUE, F32), jnp.zeros((ta, LANES), F32))] * 2
            for kj in range(qi + 1):
                for hh in range(2):
                    m, acc = st[hh]
                    sc = _dot(qh[hh], kt_scr[hh, kj])
                    if kj == qi:
                        sc = jnp.where(causal, sc, MASK_VALUE)
                    mn = jnp.maximum(m, jnp.max(sc, axis=-1, keepdims=True))
                    p = jnp.exp(sc - mn).astype(BF16)
                    st[hh] = (mn, jnp.exp(m - mn) * acc + _dot(p, v_scr[hh, kj]))
            (ma, acca), (mb, accb) = st
            la = acca[:, hd:hd + 1]
            lb = accb[:, 0:1]
            y = jnp.where(lane_mine[0], acca * (1.0 / la), accb * (1.0 / lb))
            lse = jnp.where(lane_mine[0], ma + jnp.log(la), mb + jnp.log(lb)).T
            lse_ref[0, qi:qi + 1, :] = lse[0:1, :]
            lse_ref[1, qi:qi + 1, :] = lse[hd:hd + 1, :]
            y_ref[rows, :] = y.astype(BF16)
            g = g_ref[rows, :].astype(F32)
            ga_ref[rows, :] = (y * (g * _sigmoid(g))).astype(BF16)

    blk = lambda w: pl.BlockSpec((None, s, w), lambda i, p: (i, 0, p))
    rows5 = pl.BlockSpec((None, None, 2, nq, ta), lambda i, p: (i, p, 0, 0, 0))
    yatt3, lse5, ga = pl.pallas_call(
        body, name="attn_fwd", grid=(b, HEAD_PAIRS),
        in_specs=[blk(3 * LANES), blk(LANES), rows5, blk(LANES)],
        out_specs=[blk(LANES), rows5, pl.BlockSpec((s, LANES), lambda i, p: (i, p))],
        out_shape=[jax.ShapeDtypeStruct((b, s, D_MODEL), BF16),
                   jax.ShapeDtypeStruct((b, HEAD_PAIRS, 2, nq, ta), F32),
                   jax.ShapeDtypeStruct((b * s, D_MODEL), BF16)],
        scratch_shapes=[pltpu.VMEM((2, nq, LANES, ta), BF16), pltpu.VMEM((2, nq, ta, LANES), BF16)],
        compiler_params=_cparams(("parallel", "parallel")),
    )(qkv3, cexp3, crow5, zrest3)
    return yatt3, lse5.reshape(b, HEADS, s), ga


def _attn_bwd(qkv3, do3, y3, lse, crow, cexp3):
    b, s, _ = qkv3.shape
    ta = ATT_TILE_BWD
    nq = s // ta
    hd = HEAD_DIM
    lse5 = lse.reshape(b, HEAD_PAIRS, 2, nq, ta)
    crow5 = crow.reshape(b, HEAD_PAIRS, 2, nq, ta)

    def body(qkv_ref, do_ref, y_ref, lse_ref, crow_ref, cexp_ref, dqkv_ref, dc_ref,
             qa_scr, doa_scr, qst_scr, dot_scr, kt_scr, vt_scr, dq_scr, rs_scr):
        pair = pl.program_id(1)
        lane = _iota((1, LANES), 1)
        sub = _iota((LANES, 1), 0)
        lane_mine = (lane < hd, lane >= hd)
        sub_mine = (sub < hd, sub >= hd)
        causal = _iota((ta, ta), 0) >= _iota((ta, ta), 1)
        one = jnp.ones((), BF16)
        zero = jnp.zeros((), BF16)

        @pl.when(pair == 0)
        def _():
            dc_ref[...] = jnp.zeros_like(dc_ref)

        for i in range(nq):
            rows = slice(i * ta, (i + 1) * ta)
            qs = qkv_ref[rows, 0:LANES] * 0.125
            qst = _transpose_bf16(qs)
            kt = _transpose_bf16(qkv_ref[rows, LANES:2 * LANES])
            vt = _transpose_bf16(qkv_ref[rows, 2 * LANES:3 * LANES])
            do = do_ref[rows, :]
            dof = do.astype(F32)
            dot = dof.T.astype(BF16)
            pr = y_ref[rows, :].astype(F32) * dof
            cq = cexp_ref[rows, :]
            lse_c = jnp.where(sub == 0, lse_ref[0, i:i + 1, :],
                              jnp.where(sub == 1, lse_ref[1, i:i + 1, :], 0.0)).T
            for hh in range(2):
                sp = _spare(hh)
                dsum = jnp.sum(jnp.where(lane_mine[hh], pr, 0.0), axis=-1, keepdims=True)
                bias = cq[:, hh * hd:hh * hd + 1] - lse_c[:, hh:hh + 1]
                qa_scr[hh, i] = _put_cols(qs, lane_mine[hh], list(_split3(bias)) + [one, one, one], sp)
                doa_scr[hh, i] = _put_cols(do, lane_mine[hh], list(_split3(-dsum)), sp)
                qst_scr[hh, i] = jnp.where(sub_mine[hh], qst, zero)
                dot_scr[hh, i] = jnp.where(sub_mine[hh], dot, zero)
                ck = list(_split3(-crow_ref[hh, i:i + 1, :]))
                kt_scr[hh, i] = _put_rows(kt, sub_mine[hh], [one, one, one] + ck, sp)
                vt_scr[hh, i] = _put_rows(vt, sub_mine[hh], [one, one, one], sp)
            dq_scr[i] = jnp.zeros((ta, LANES), F32)
            rs_scr[i] = jnp.zeros((ta, LANES), F32)

        for kj in range(nq):
            krows = slice(kj * ta, (kj + 1) * ta)
            k = qkv_ref[krows, LANES:2 * LANES]
            km = (jnp.where(lane_mine[0], k, zero), jnp.where(lane_mine[1], k, zero))
            dkt = jnp.zeros((LANES, ta), F32)
            dvt = jnp.zeros((LANES, ta), F32)
            dcp = [jnp.zeros((8, ta), F32), jnp.zeros((8, ta), F32)]
            for qi in range(kj, nq):
                dq = jnp.zeros((ta, LANES), F32)
                rs = []
                for hh in range(2):
                    sc = _dot(qa_scr[hh, qi], kt_scr[hh, kj])
                    if qi == kj:
                        sc = jnp.where(causal, sc, MASK_VALUE)
                    p = jnp.exp(sc)
                    dsf = p * _dot(doa_scr[hh, qi], vt_scr[hh, kj])
                    dcp[hh] = dcp[hh] + jnp.sum(dsf.reshape(ta // 8, 8, ta), axis=0)
                    rs.append(jnp.sum(dsf, axis=-1, keepdims=True))
                    ds = dsf.astype(BF16)
                    dq = dq + _dot(ds, km[hh])
                    dkt = dkt + _dot(qst_scr[hh, qi], ds)
                    dvt = dvt + _dot(dot_scr[hh, qi], p.astype(BF16))
                dq_scr[qi] += dq
                rs_scr[qi] += jnp.where(lane == 0, rs[0], jnp.where(lane == 1, rs[1], 0.0))
            dqkv_ref[krows, LANES:2 * LANES] = dkt.T.astype(BF16)
            dqkv_ref[krows, 2 * LANES:3 * LANES] = dvt.T.astype(BF16)
            dca = jnp.sum(dcp[0], axis=0, keepdims=True)
            dcb = jnp.sum(dcp[1], axis=0, keepdims=True)
            dcs = jnp.where(sub == 0, dca, jnp.where(sub == 1, dcb, 0.0)).T
            dc_ref[krows, :] += (jnp.where(lane == 2 * pair, -dcs[:, 0:1], 0.0)
                                 + jnp.where(lane == 2 * pair + 1, -dcs[:, 1:2], 0.0))
        for qi in range(nq):
            rows = slice(qi * ta, (qi + 1) * ta)
            dqkv_ref[rows, 0:LANES] = (dq_scr[qi] * 0.125).astype(BF16)
            rq = rs_scr[qi]
            dc_ref[rows, :] += (jnp.where(lane == 2 * pair, rq[:, 0:1], 0.0)
                                + jnp.where(lane == 2 * pair + 1, rq[:, 1:2], 0.0))

    blk = lambda w: pl.BlockSpec((None, s, w), lambda i, p: (i, 0, p))
    rows5 = pl.BlockSpec((None, None, 2, nq, ta), lambda i, p: (i, p, 0, 0, 0))
    by_rows = lambda: pltpu.VMEM((2, nq, ta, LANES), BF16)
    by_cols = lambda: pltpu.VMEM((2, nq, LANES, ta), BF16)
    return pl.pallas_call(
        body, name="attn_bwd", grid=(b, HEAD_PAIRS),
        in_specs=[blk(3 * LANES), blk(LANES), blk(LANES), rows5, rows5, blk(LANES)],
        out_specs=[pl.BlockSpec((s, 3 * LANES), lambda i, p: (i, p)),
                   pl.BlockSpec((None, s, LANES), lambda i, p: (i, 0, 0))],
        out_shape=[jax.ShapeDtypeStruct((b * s, 3 * D_MODEL), BF16), jax.ShapeDtypeStruct((b, s, LANES), F32)],
        scratch_shapes=[by_rows(), by_rows(), by_cols(), by_cols(), by_cols(), by_cols(),
                        pltpu.VMEM((nq, ta, LANES), F32), pltpu.VMEM((nq, ta, LANES), F32)],
        compiler_params=_cparams(("parallel", "arbitrary")),
    )(qkv3, do3, y3, lse5, crow5, cexp3)


def _shifted(v, ks, rows, s):
    low = rows[0:8, :]
    out = []
    for k in ks:
        r = pltpu.roll(v, k % s, 0)
        if k > 0:
            out.append(jnp.concatenate([jnp.where(low >= k, r[0:8, :], 0.0), r[8:, :]], axis=0))
        else:
            out.append(jnp.concatenate([r[:s - 8, :], jnp.where(low < 8 + k, r[s - 8:, :], 0.0)], axis=0))
    return out


def _rnn_common(xr, cw_ref, cb_ref, bda_ref, bdx_ref, ba_ref, bx_ref, lam_ref, s):
    rows = _iota((s, LANES), 0)
    x1, x2, x3 = _shifted(xr, (1, 2, 3), rows, s)
    xc = cb_ref[...] + cw_ref[0:1, :] * x3
    xc = xc + cw_ref[1:2, :] * x2
    xc = xc + cw_ref[2:3, :] * x1
    xc = xc + cw_ref[3:4, :] * xr
    xcb = xc.astype(BF16)
    r = _sigmoid(_dot(xcb, bda_ref[...]) + ba_ref[...])
    i = _sigmoid(_dot(xcb, bdx_ref[...]) + bx_ref[...])
    sp = _softplus(-lam_ref[...])
    log_a = (-RG_C * r) * sp
    a = jnp.exp(log_a)
    a2 = a * a
    sq = jnp.sqrt(jnp.maximum(_one_minus_exp(log_a + log_a, a2), 0.0))
    return rows, (x1, x2, x3), xc, xcb, r, i, sp, a, a2, sq


def _scan_down(a, u, rows, s, s1, s2):
    low = rows & 7
    for sh in (1, 2, 4):
        keep = low >= sh
        u = u + a * jnp.where(keep, pltpu.roll(u, sh, 0), 0.0)
        a = a * jnp.where(keep, pltpu.roll(a, sh, 0), 1.0)
    ng = s // 8
    s1[...] = a
    s2[...] = u
    at = s1[pl.ds(7, ng, stride=8), :]
    ut = s2[pl.ds(7, ng, stride=8), :]
    grow = _iota((ng, LANES), 0)
    sh = 1
    while sh < ng:
        keep = grow >= sh
        ut = ut + at * jnp.where(keep, pltpu.roll(ut, sh, 0), 0.0)
        if sh * 2 < ng:
            at = at * jnp.where(keep, pltpu.roll(at, sh, 0), 1.0)
        sh *= 2
    h_in = jnp.where(grow >= 1, pltpu.roll(ut, 1, 0), 0.0)
    for k in range(8):
        s1[pl.ds(k, ng, stride=8), :] = h_in
    return u + a * s1[...]


def _scan_up(a, g, rows, s, s1, s2):
    low = rows & 7
    for sh in (1, 2, 4):
        keep = low < 8 - sh
        g = g + a * jnp.where(keep, pltpu.roll(g, s - sh, 0), 0.0)
        a = a * jnp.where(keep, pltpu.roll(a, s - sh, 0), 1.0)
    ng = s // 8
    s1[...] = a
    s2[...] = g
    at = s1[pl.ds(0, ng, stride=8), :]
    gt = s2[pl.ds(0, ng, stride=8), :]
    grow = _iota((ng, LANES), 0)
    sh = 1
    while sh < ng:
        keep = grow < ng - sh
        gt = gt + at * jnp.where(keep, pltpu.roll(gt, ng - sh, 0), 0.0)
        if sh * 2 < ng:
            at = at * jnp.where(keep, pltpu.roll(at, ng - sh, 0), 1.0)
        sh *= 2
    g_in = jnp.where(grow < ng - 1, pltpu.roll(gt, ng - 1, 0), 0.0)
    for k in range(8):
        s1[pl.ds(k, ng, stride=8), :] = g_in
    return g + a * s1[...]


def _rnn_specs(s):
    blk = lambda off: pl.BlockSpec((None, s, LANES), lambda cb, i: (i, 0, off + cb))
    vec = lambda r: pl.BlockSpec((r, LANES), lambda cb, i: (0, cb))
    mat = pl.BlockSpec((None, LANES, LANES), lambda cb, i: (cb, 0, 0))
    return blk, vec, mat


def _rnn_fwd(zrest3, conv_w, conv_b, bda, bdx, ba, bx, lam):
    b, s, _ = zrest3.shape

    def body(xr_ref, g_ref, cw_ref, cb_ref, bda_ref, bdx_ref, ba_ref, bx_ref, lam_ref, h_ref, gr_ref, s1, s2):
        xr = xr_ref[...].astype(F32)
        rows, _, xc, _, _, i, _, a, _, sq = _rnn_common(
            xr, cw_ref, cb_ref, bda_ref, bdx_ref, ba_ref, bx_ref, lam_ref, s)
        h = _scan_down(a, sq * (i * xc), rows, s, s1, s2)
        h_ref[...] = h
        g = g_ref[...].astype(F32)
        gr_ref[...] = (h * (g * _sigmoid(g))).astype(BF16)

    blk, vec, mat = _rnn_specs(s)
    return pl.pallas_call(
        body, name="rnn_fwd", grid=(N_CBLK, b),
        in_specs=[blk(N_CBLK), blk(2 * N_CBLK), vec(CONV_W), vec(1), mat, mat, vec(1), vec(1), vec(1)],
        out_specs=[blk(0), pl.BlockSpec((s, LANES), lambda cb, i: (i, cb))],
        out_shape=[jax.ShapeDtypeStruct((b, s, D_MODEL), F32), jax.ShapeDtypeStruct((b * s, D_MODEL), BF16)],
        scratch_shapes=[pltpu.VMEM((s, LANES), F32), pltpu.VMEM((s, LANES), F32)],
        compiler_params=_cparams(("parallel", "parallel")),
    )(zrest3, zrest3, conv_w, conv_b, bda, bdx, ba, bx, lam)


def _rnn_bwd(zrest3, h3, dh3, conv_w, conv_b, bda, bdx, ba, bx, lam):
    b, s, _ = zrest3.shape

    def body(xr_ref, h_ref, dh_ref, cw_ref, cb_ref, bda_ref, bdx_ref, ba_ref, bx_ref, lam_ref,
             dxr_ref, pv_ref, dbd_ref, s1, s2):
        @pl.when(pl.program_id(1) == 0)
        def _():
            pv_ref[...] = jnp.zeros_like(pv_ref)
            dbd_ref[...] = jnp.zeros_like(dbd_ref)

        xr = xr_ref[...].astype(F32)
        rows, (x1, x2, x3), xc, xcb, r, i, sp, a, a2, sq = _rnn_common(
            xr, cw_ref, cb_ref, bda_ref, bdx_ref, ba_ref, bx_ref, lam_ref, s)
        (a_next,) = _shifted(a, (-1,), rows, s)
        g = _scan_up(a_next, dh_ref[...], rows, s, s1, s2)
        (hp,) = _shifted(h_ref[...], (1,), rows, s)
        da = g * hp
        dsq = g * (i * xc)
        di = g * (sq * xc)
        dxc = g * (sq * i)
        dlog = da * a - dsq * (a2 / sq)
        dr = dlog * (-RG_C * sp)
        dpr = dr * (r * (1.0 - r))
        dpi = di * (i * (1.0 - i))
        dprb = dpr.astype(BF16)
        dpib = dpi.astype(BF16)
        dxc = dxc + _dot_nt(dprb, bda_ref[...]) + _dot_nt(dpib, bdx_ref[...])

        up1, up2, up3 = _shifted(dxc, (-1, -2, -3), rows, s)
        dxr = cw_ref[3:4, :] * dxc + cw_ref[2:3, :] * up1 + cw_ref[1:2, :] * up2 + cw_ref[0:1, :] * up3
        dxr_ref[...] = dxr.astype(BF16)

        def colsum(v):
            return jnp.sum(v, axis=0, keepdims=True)

        pv_ref[0:1, :] += colsum(dxc * x3)
        pv_ref[1:2, :] += colsum(dxc * x2)
        pv_ref[2:3, :] += colsum(dxc * x1)
        pv_ref[3:4, :] += colsum(dxc * xr)
        pv_ref[4:5, :] += colsum(dxc)
        pv_ref[5:6, :] += colsum(dpr)
        pv_ref[6:7, :] += colsum(dpi)
        pv_ref[7:8, :] += colsum(dlog * r) * (RG_C * _sigmoid(-lam_ref[...]))
        dbd_ref[0] += _dot_tn(xcb, dprb)
        dbd_ref[1] += _dot_tn(xcb, dpib)

    blk, vec, mat = _rnn_specs(s)
    hblk = pl.BlockSpec((None, s, LANES), lambda cb, i: (i, 0, cb))
    return pl.pallas_call(
        body, name="rnn_bwd", grid=(N_CBLK, b),
        in_specs=[blk(N_CBLK), hblk, hblk, vec(CONV_W), vec(1), mat, mat, vec(1), vec(1), vec(1)],
        out_specs=[pl.BlockSpec((s, LANES), lambda cb, i: (i, cb)), pl.BlockSpec((8, LANES), lambda cb, i: (0, cb)),
                   pl.BlockSpec((None, 2, LANES, LANES), lambda cb, i: (cb, 0, 0, 0))],
        out_shape=[jax.ShapeDtypeStruct((b * s, D_MODEL), BF16), jax.ShapeDtypeStruct((8, D_MODEL), F32),
                   jax.ShapeDtypeStruct((N_CBLK, 2, LANES, LANES), F32)],
        scratch_shapes=[pltpu.VMEM((s, LANES), F32), pltpu.VMEM((s, LANES), F32)],
        compiler_params=_cparams(("parallel", "arbitrary")),
    )(zrest3, h3, dh3, conv_w, conv_b, bda, bdx, ba, bx, lam)


def _branch_merge(ga, gr, wa, wr, zrest):
    t = ga.shape[0]
    tm = min(512, t)
    tn = D_MODEL

    def body(ga_ref, gr_ref, wa_ref, wr_ref, mga_ref, mgr_ref, ya_ref, yr_ref, m_ref):
        ya = _dot(ga_ref[...], wa_ref[...])
        yr = _dot(gr_ref[...], wr_ref[...])
        ya_ref[...] = ya.astype(BF16)
        yr_ref[...] = yr.astype(BF16)
        m_ref[...] = (_sigmoid(mga_ref[...].astype(F32)) * ya + _sigmoid(mgr_ref[...].astype(F32)) * yr).astype(BF16)

    nj = D_MODEL // tn
    act = pl.BlockSpec((tm, D_MODEL), lambda i, j: (i, 0))
    wgt = pl.BlockSpec((D_MODEL, tn), lambda i, j: (0, j))
    out = pl.BlockSpec((tm, tn), lambda i, j: (i, j))
    return pl.pallas_call(
        body, name="branch_merge", grid=(t // tm, nj),
        in_specs=[act, act, wgt, wgt, pl.BlockSpec((tm, tn), lambda i, j: (i, 3 * nj + j)),
                  pl.BlockSpec((tm, tn), lambda i, j: (i, 4 * nj + j))],
        out_specs=[out, out, out],
        out_shape=[jax.ShapeDtypeStruct((t, D_MODEL), BF16), jax.ShapeDtypeStruct((t, D_MODEL), BF16),
                   jax.ShapeDtypeStruct((t, D_MODEL), BF16)],
        compiler_params=_cparams(("parallel", "parallel")),
    )(ga, gr, wa, wr, zrest, zrest)


def _out_loss(m, wout, x2, tgt2, wpost):
    t = m.shape[0]
    tm = min(512, t)

    def body(m_ref, w_ref, x_ref, t_ref, wp_ref, dy_ref, do_ref, acc_ref):
        @pl.when(pl.program_id(0) == 0)
        def _():
            acc_ref[...] = jnp.zeros_like(acc_ref)

        o = _dot(m_ref[...], w_ref[...])
        r2 = lax.rsqrt(jnp.mean(o * o, axis=-1, keepdims=True) + NORM_EPS)
        n = o * r2
        wp = wp_ref[...]
        err = (x_ref[...] + n * wp) - t_ref[...]
        dy = err * (1.0 / D_MODEL)
        dn = dy * wp
        do = r2 * (dn - n * jnp.mean(dn * n, axis=-1, keepdims=True))
        dy_ref[...] = dy
        do_ref[...] = do.astype(BF16)
        acc_ref[0:1, :] += jnp.sum(dy * n, axis=0, keepdims=True)
        acc_ref[1:2, :] += jnp.sum(err * err, axis=0, keepdims=True)

    row = pl.BlockSpec((tm, D_MODEL), lambda i: (i, 0))
    return pl.pallas_call(
        body, name="out_loss", grid=(t // tm,),
        in_specs=[row, pl.BlockSpec((D_MODEL, D_MODEL), lambda i: (0, 0)), row, row,
                  pl.BlockSpec((1, D_MODEL), lambda i: (0, 0))],
        out_specs=[row, row, pl.BlockSpec((8, D_MODEL), lambda i: (0, 0))],
        out_shape=[jax.ShapeDtypeStruct((t, D_MODEL), F32), jax.ShapeDtypeStruct((t, D_MODEL), BF16),
                   jax.ShapeDtypeStruct((8, D_MODEL), F32)],
        compiler_params=_cparams(("arbitrary",)),
    )(m, wout, x2, tgt2, wpost)


def _merge_bwd(do, wout, zrest, ya, yr):
    t = do.shape[0]
    tm = min(512, t)
    tn = D_MODEL
    nj = D_MODEL // tn

    def body(do_ref, w_ref, mga_ref, mgr_ref, ya_ref, yr_ref, dya_ref, dyr_ref, dmga_ref, dmgr_ref):
        dm = _dot_nt(do_ref[...], w_ref[...])
        sa = _sigmoid(mga_ref[...].astype(F32))
        sr = _sigmoid(mgr_ref[...].astype(F32))
        dya_ref[...] = (dm * sa).astype(BF16)
        dyr_ref[...] = (dm * sr).astype(BF16)
        dmga_ref[...] = (dm * ya_ref[...].astype(F32) * (sa * (1.0 - sa))).astype(BF16)
        dmgr_ref[...] = (dm * yr_ref[...].astype(F32) * (sr * (1.0 - sr))).astype(BF16)

    out = pl.BlockSpec((tm, tn), lambda i, j: (i, j))
    bf = jax.ShapeDtypeStruct((t, D_MODEL), BF16)
    return pl.pallas_call(
        body, name="merge_bwd", grid=(t // tm, nj),
        in_specs=[pl.BlockSpec((tm, D_MODEL), lambda i, j: (i, 0)), pl.BlockSpec((tn, D_MODEL), lambda i, j: (j, 0)),
                  pl.BlockSpec((tm, tn), lambda i, j: (i, 3 * nj + j)),
                  pl.BlockSpec((tm, tn), lambda i, j: (i, 4 * nj + j)), out, out],
        out_specs=[out, out, out, out],
        out_shape=[bf, bf, bf, bf],
        compiler_params=_cparams(("parallel", "parallel")),
    )(do, wout, zrest, zrest, ya, yr)


def _branch_bwd(dya, dyr, wa, wr, zrest, yatt, ylru):
    t = dya.shape[0]
    tm = min(512, t)
    tn = D_MODEL
    nj = D_MODEL // tn

    def body(dya_ref, dyr_ref, wa_ref, wr_ref, ga_ref, gr_ref, ya_ref, yl_ref,
             dyatt_ref, dga_ref, dyl_ref, dgr_ref):
        dga = _dot_nt(dya_ref[...], wa_ref[...])
        dgr = _dot_nt(dyr_ref[...], wr_ref[...])
        g = ga_ref[...].astype(F32)
        sg = _sigmoid(g)
        dyatt_ref[...] = (dga * (g * sg)).astype(BF16)
        dga_ref[...] = (dga * ya_ref[...].astype(F32) * (sg * (1.0 + g * (1.0 - sg)))).astype(BF16)
        g = gr_ref[...].astype(F32)
        sg = _sigmoid(g)
        dyl_ref[...] = dgr * (g * sg)
        dgr_ref[...] = (dgr * yl_ref[...] * (sg * (1.0 + g * (1.0 - sg)))).astype(BF16)

    act = pl.BlockSpec((tm, D_MODEL), lambda i, j: (i, 0))
    wgt = pl.BlockSpec((tn, D_MODEL), lambda i, j: (j, 0))
    out = pl.BlockSpec((tm, tn), lambda i, j: (i, j))
    bf = jax.ShapeDtypeStruct((t, D_MODEL), BF16)
    return pl.pallas_call(
        body, name="branch_bwd", grid=(t // tm, nj),
        in_specs=[act, act, wgt, wgt, pl.BlockSpec((tm, tn), lambda i, j: (i, j)),
                  pl.BlockSpec((tm, tn), lambda i, j: (i, 2 * nj + j)), out, out],
        out_specs=[out, out, out, out],
        out_shape=[bf, bf, jax.ShapeDtypeStruct((t, D_MODEL), F32), bf],
        compiler_params=_cparams(("parallel", "parallel")),
    )(dya, dyr, wa, wr, zrest, zrest, yatt, ylru)


def _dh_final(parts, after, x2, dy, wpre):
    t = x2.shape[0]
    tm = min(256, t)
    np_ = len(parts)

    def body(*refs):
        x_ref, dy_ref, w_ref = refs[2 * np_ + 1:2 * np_ + 4]
        gx_ref, pw_ref = refs[2 * np_ + 4:]

        @pl.when(pl.program_id(0) == 0)
        def _():
            pw_ref[...] = jnp.zeros_like(pw_ref)

        dh = _dot(refs[0][...], refs[np_][...])
        for p in range(1, np_):
            dh = dh + _dot(refs[p][...], refs[np_ + p][...])
        x = x_ref[...]
        r = lax.rsqrt(jnp.mean(x * x, axis=-1, keepdims=True) + NORM_EPS)
        xn = x * r
        dxn = dh * w_ref[...]
        gx_ref[...] = r * (dxn - xn * jnp.mean(dxn * xn, axis=-1, keepdims=True)) + dy_ref[...]
        pw_ref[0:1, :] += jnp.sum(dh * xn, axis=0, keepdims=True)

    row = pl.BlockSpec((tm, D_MODEL), lambda i: (i, 0))
    in_specs = [pl.BlockSpec((tm, dz.shape[1]), lambda i: (i, 0)) for dz, _ in parts]
    in_specs += [pl.BlockSpec(w.shape, lambda i: (0, 0), pipeline_mode=pl.Buffered(1)) for _, w in parts]
    in_specs += [pl.BlockSpec(after.shape, lambda i: (0, 0)), row, row, pl.BlockSpec((1, D_MODEL), lambda i: (0, 0))]
    return pl.pallas_call(
        body, name="dh_final", grid=(t // tm,),
        in_specs=in_specs,
        out_specs=[row, pl.BlockSpec((8, D_MODEL), lambda i: (0, 0))],
        out_shape=[jax.ShapeDtypeStruct((t, D_MODEL), F32), jax.ShapeDtypeStruct((8, D_MODEL), F32)],
        compiler_params=_cparams(("arbitrary",), vmem_mb=48),
    )(*[dz for dz, _ in parts], *[w for _, w in parts], after, x2, dy, wpre)


def _adamw(w, g, m, v):
    m = ADAM_B1 * m + (1.0 - ADAM_B1) * g
    v = ADAM_B2 * v + (1.0 - ADAM_B2) * (g * g)
    m_hat = m / (1.0 - ADAM_B1 ** ADAM_STEP)
    v_hat = v / (1.0 - ADAM_B2 ** ADAM_STEP)
    delta = -ADAM_LR * (m_hat / (jnp.sqrt(v_hat) + ADAM_EPS) + ADAM_WD * w)
    return delta, m, v


def _reduce_adamw(own, parts, place, w, m, v, name):
    r, c = w.shape
    blk, nblk, at = _blocks_2d(r, c)

    def body(place_ref, own_ref, p_ref, w_ref, m_ref, v_ref, g_ref, d_ref, nm_ref, nv_ref):
        mine = place_ref[1]
        own_blk = own_ref[...]
        g = jnp.where(mine == 0, own_blk, p_ref[0].astype(F32))
        for j in range(1, N_CHIPS):
            g = g + jnp.where(mine == j, own_blk, p_ref[j].astype(F32))
        d, nm, nv = _adamw(w_ref[...], g, m_ref[...], v_ref[...])
        g_ref[...] = g
        d_ref[...] = d
        nm_ref[...] = nm
        nv_ref[...] = nv

    row = pl.BlockSpec(blk, lambda i, pr: at(i))
    sh = jax.ShapeDtypeStruct((r, c), F32)
    grid_spec = pltpu.PrefetchScalarGridSpec(
        num_scalar_prefetch=1, grid=(nblk,),
        in_specs=[row, pl.BlockSpec((N_CHIPS,) + blk, lambda i, pr: (0,) + at(i)), row, row, row],
        out_specs=[row, row, row, row])
    return pl.pallas_call(
        body, name=name, grid_spec=grid_spec, out_shape=[sh, sh, sh, sh],
        compiler_params=_cparams(("parallel",)),
    )(place, own, parts, w, m, v)


def _reduce_adamw_stacked(own, parts, place, triples, name):
    n = len(triples)
    _, r, c = triples[0][0].shape

    def body(place_ref, own_ref, p_ref, *refs):
        ins, outs = refs[:3 * n], refs[3 * n:]
        mine = place_ref[1]
        for i in range(n):
            rows = slice(i * r, (i + 1) * r)
            own_blk = own_ref[rows, :]
            g = jnp.where(mine == 0, own_blk, p_ref[0, rows, :].astype(F32))
            for j in range(1, N_CHIPS):
                g = g + jnp.where(mine == j, own_blk, p_ref[j, rows, :].astype(F32))
            d, nm, nv = _adamw(ins[3 * i][0], g, ins[3 * i + 1][0], ins[3 * i + 2][0])
            for k, val in enumerate((g, d, nm, nv)):
                outs[4 * i + k][0] = val

    whole = lambda shape: pl.BlockSpec(shape, lambda i, pr: (0,) * len(shape))
    grid_spec = pltpu.PrefetchScalarGridSpec(
        num_scalar_prefetch=1, grid=(1,),
        in_specs=[whole(own.shape), whole(parts.shape)] + [whole((1, r, c))] * (3 * n),
        out_specs=[whole((1, r, c))] * (4 * n))
    res = pl.pallas_call(
        body, name=name, grid_spec=grid_spec,
        out_shape=[jax.ShapeDtypeStruct((1, r, c), F32)] * (4 * n),
        compiler_params=_cparams(("arbitrary",)),
    )(place, own, parts, *[a for t3 in triples for a in t3])
    return [res[4 * i:4 * i + 4] for i in range(n)]


def _interleave_qkv(a):
    lead = a.shape[:-1]
    return a.reshape(lead + (3, HEAD_PAIRS, LANES)).swapaxes(-3, -2).reshape(lead + (3 * D_MODEL,))


def _deinterleave_qkv(a):
    lead = a.shape[:-1]
    return a.reshape(lead + (HEAD_PAIRS, 3, LANES)).swapaxes(-3, -2).reshape(lead + (3 * D_MODEL,))


def _interleave_rows(a):
    return a.reshape(3, HEAD_PAIRS, LANES, a.shape[1]).swapaxes(0, 1).reshape(a.shape)


def _deinterleave_rows(a):
    return a.reshape(HEAD_PAIRS, 3, LANES, a.shape[1]).swapaxes(0, 1).reshape(a.shape)


def _pack_small(pre, conv_b, rg_ba, rg_bx, lam, post, loss_row, b_in, conv_w_full, rg_wa, rg_wx):
    z = jnp.zeros((1, D_MODEL), F32)
    b_used = jnp.concatenate([b_in[:, 0:3 * D_MODEL], b_in[:, 3 * D_MODEL + HEADS:IN_TOTAL]], axis=1)
    b_f = jnp.pad(b_in[:, 3 * D_MODEL:3 * D_MODEL + HEADS], ((0, 0), (0, D_MODEL - HEADS)))
    return jnp.concatenate([
        pre, conv_b, rg_ba, rg_bx, lam, post, loss_row, z,
        b_used.reshape(9, D_MODEL), b_f, conv_w_full, z, z,
        rg_wa.reshape(64, D_MODEL), rg_wx.reshape(64, D_MODEL)], axis=0)


def _unpack_small(p):
    b_used = p[8:17].reshape(1, 9 * D_MODEL)
    b_in = jnp.concatenate([b_used[:, 0:3 * D_MODEL], p[17:18, 0:HEADS], b_used[:, 3 * D_MODEL:]], axis=1)
    return dict(pre_norm_w=p[0:1], conv_b=p[1:2], rg_ba=p[2:3], rg_bx=p[3:4], rg_lambda=p[4:5],
                post_norm_w=p[5:6], loss_row=p[6:7], b_in=b_in, conv_w_full=p[18:22],
                rg_wa=p[24:88].reshape(1, 16, 64, 64), rg_wx=p[88:152].reshape(1, 16, 64, 64))


def _reduce_small(parts, first, w, m, v, vectors):
    nvec = len(vectors)

    def body(p_ref, f_ref, w_ref, m_ref, v_ref, *refs):
        ins, outs = refs[:3 * nvec], refs[3 * nvec:]
        g = p_ref[0]
        g0 = f_ref[0, 0:1, :]
        for j in range(1, N_DEV):
            g = g + p_ref[j]
            g0 = g0 + f_ref[j, 0:1, :]
        d, nm, nv = _adamw(w_ref[...], g, m_ref[...], v_ref[...])
        for k, val in enumerate((g, d, nm, nv)):
            outs[k][...] = val
        for i in range(nvec):
            gi = g0 if i == 0 else g[i:i + 1, :]
            di, nmi, nvi = _adamw(ins[3 * i][...], gi, ins[3 * i + 1][...], ins[3 * i + 2][...])
            for k, val in enumerate((gi, di, nmi, nvi)):
                outs[4 + 4 * i + k][...] = val
        outs[-1][...] = jnp.zeros((8, LANES), F32) + (0.5 / D_MODEL) * jnp.sum(g[LOSS_ROW:LOSS_ROW + 1, :])

    sh = jax.ShapeDtypeStruct((SMALL_ROWS, D_MODEL), F32)
    vec = jax.ShapeDtypeStruct((1, D_MODEL), F32)
    res = pl.pallas_call(
        body, name="reduce_small",
        out_shape=[sh, sh, sh, sh] + [vec] * (4 * nvec) + [jax.ShapeDtypeStruct((8, LANES), F32)],
    )(parts, first, w, m, v, *[a for t3 in vectors for a in t3])
    return res[:4], [res[4 + 4 * i:8 + 4 * i] for i in range(nvec)], res[-1]


def kernel(x, pre_norm_w, w_in, b_in, conv_w, conv_b, rg_wa, rg_ba, rg_wx, rg_bx, rg_lambda, w_branch_a, w_branch_r, w_out, post_norm_w, loss_target, m_pre_norm_w, m_w_in, m_b_in, m_conv_w, m_conv_b, m_rg_wa, m_rg_ba, m_rg_wx, m_rg_bx, m_rg_lambda, m_w_branch_a, m_w_branch_r, m_w_out, m_post_norm_w, v_pre_norm_w, v_w_in, v_b_in, v_conv_w, v_conv_b, v_rg_wa, v_rg_ba, v_rg_wx, v_rg_bx, v_rg_lambda, v_w_branch_a, v_w_branch_r, v_w_out, v_post_norm_w):
    b, s, _ = x.shape
    t = b * s
    me = 4 * lax.axis_index("x") + 2 * lax.axis_index("y") + lax.axis_index("c")
    shard_rows = D_MODEL // N_DEV

    place = jnp.stack([lax.axis_index("c"), 2 * lax.axis_index("x") + lax.axis_index("y")]).astype(jnp.int32)
    w_in_all = _gather(w_in[0].T.astype(BF16), "gather_w_in")
    wt_full = w_in_all.reshape(IN_TOTAL, D_MODEL)
    conv_terms = jnp.concatenate(_split3(conv_w[0]), axis=0)
    conv_pad = jnp.pad(conv_terms, ((0, 16 - 3 * CONV_W), (0, D_MODEL - LANES)))
    sq_stack = jnp.concatenate([w_branch_a[0].astype(BF16), w_branch_r[0].astype(BF16), w_out[0].astype(BF16),
                                conv_pad], axis=0)
    sq_sems, sq_src, sq_land, sq_token = _gather_start(sq_stack, w_in_all, "gather_w_sq_start")

    w_qkv = _interleave_rows(wt_full[0:3 * D_MODEL])
    w_f = jnp.pad(wt_full[3 * D_MODEL:3 * D_MODEL + HEADS], ((0, LANES - HEADS), (0, 0)))
    w_rest = wt_full[3 * D_MODEL + HEADS:IN_USED]
    b_qkv = _interleave_qkv(b_in[:, 0:3 * D_MODEL]) + sq_token[0, 0]
    b_f = jnp.pad(b_in[:, 3 * D_MODEL:3 * D_MODEL + HEADS], ((0, 0), (0, LANES - HEADS)))
    b_rest = b_in[:, 3 * D_MODEL + HEADS:IN_USED]

    def blockdiag(w):
        w2 = w.reshape(N_CBLK, 2, HEAD_DIM, HEAD_DIM)
        zz = jnp.zeros((N_CBLK, HEAD_DIM, HEAD_DIM), w.dtype)
        top = jnp.concatenate([w2[:, 0], zz], axis=2)
        bot = jnp.concatenate([zz, w2[:, 1]], axis=2)
        return jnp.concatenate([top, bot], axis=1).astype(BF16)

    bda, bdx = blockdiag(rg_wa[0]), blockdiag(rg_wx[0])

    x2 = x.reshape(t, D_MODEL)
    tgt2 = loss_target.reshape(t, D_MODEL)
    h, qkv, zf = _prenorm_inproj(x2, pre_norm_w, w_qkv, b_qkv, w_f, b_f)
    zrest = _mm_bias(h, w_rest, b_rest, BF16, "inproj_rest")
    qkv3 = qkv.reshape(b, s, 3 * D_MODEL)
    zrest3 = zrest.reshape(b, s, 5 * D_MODEL)
    zf3 = zf.reshape(b, s, LANES)
    cexp3, crow = _fgate_fwd(zf3)
    yatt3, lse, ga = _attn_fwd(qkv3, cexp3, crow, zrest3)

    sq_all = _gather_wait(sq_sems, sq_src, sq_land, ga, "gather_w_sq_wait")
    sq_all = lax.dynamic_update_slice(sq_all, sq_stack[None], (me, 0, 0))
    wa = sq_all[:, 0:shard_rows].reshape(D_MODEL, D_MODEL)
    wr = sq_all[:, shard_rows:2 * shard_rows].reshape(D_MODEL, D_MODEL)
    wo = sq_all[:, 2 * shard_rows:3 * shard_rows].reshape(D_MODEL, D_MODEL)
    conv_all = sq_all[:, 3 * shard_rows:3 * shard_rows + 3 * CONV_W, 0:LANES].astype(F32)
    conv_all = (conv_all[:, 0:CONV_W] + conv_all[:, CONV_W:2 * CONV_W]) + conv_all[:, 2 * CONV_W:3 * CONV_W]
    conv_full = conv_all.transpose(1, 0, 2).reshape(CONV_W, D_MODEL)

    ylru3, gr = _rnn_fwd(zrest3, conv_full, conv_b, bda, bdx, rg_ba, rg_bx, rg_lambda)
    ya, yr, mm = _branch_merge(ga, gr, wa, wr, zrest)
    dy, do, acc_out = _out_loss(mm, wo, x2, tgt2, post_norm_w)

    dya, dyr, dz_mga, dz_mgr = _merge_bwd(do, wo, zrest, ya, yr)
    dyatt, dz_ga, dylru, dz_gr = _branch_bwd(dya, dyr, wa, wr, zrest, yatt3.reshape(t, D_MODEL),
                                             ylru3.reshape(t, D_MODEL))
    dz_xr, pvec, dbd = _rnn_bwd(zrest3, ylru3, dylru.reshape(b, s, D_MODEL), conv_full, conv_b, bda, bdx,
                                rg_ba, rg_bx, rg_lambda)
    dz_qkv, dc3 = _attn_bwd(qkv3, dyatt.reshape(b, s, D_MODEL), yatt3, lse, crow, cexp3)
    dz_f = _fgate_bwd(dc3, zf3)

    dw_qkv, db_qkv = _mm_tn(dz_qkv, h, "dw_qkv")
    dw_f, db_f = _mm_tn(dz_f, h, "dw_f")
    dw_parts, db_parts = [], []
    for nm, dzp in (("ga", dz_ga), ("xr", dz_xr), ("gr", dz_gr), ("mga", dz_mga), ("mgr", dz_mgr)):
        dwp, dbp = _mm_tn(dzp, h, "dw_" + nm)
        dw_parts.append(dwp)
        db_parts.append(dbp[0:1])

    zeros_tail = jnp.zeros((IN_TOTAL - IN_USED, D_MODEL), F32)
    dwt_full = jnp.concatenate([_deinterleave_rows(dw_qkv), dw_f[0:HEADS]] + dw_parts + [zeros_tail], axis=0)
    dw_in_send = dwt_full.reshape(N_CHIPS, 2, W_SHARD, D_MODEL).transpose(1, 0, 2, 3)
    swp_sems, dw_in_src, swp_land, swp_token = _swap_start(dw_in_send, db_f, "swap_dw_in_start")
    dw_a, _ = _mm_tn(ga, dya, "dw_a", after=swp_token)
    dw_r, _ = _mm_tn(gr, dyr, "dw_r", after=swp_token)
    dw_o, _ = _mm_tn(mm, do, "dw_o", after=swp_token)
    dw_in_send, sib_in = _swap_wait(swp_sems, dw_in_src, swp_land, dw_o, "swap_dw_in_wait")
    by_dest = lambda a: a.reshape(N_CHIPS, 2, shard_rows, D_MODEL).transpose(1, 0, 2, 3)
    dw_sq_send = jnp.concatenate([by_dest(dw_a), by_dest(dw_r), by_dest(dw_o)], axis=2)

    db_in_full = jnp.concatenate([_deinterleave_qkv(db_qkv[0:1]), db_f[0:1, 0:HEADS]] + db_parts
                                 + [jnp.zeros((1, IN_TOTAL - IN_USED), F32)], axis=1)
    d_rg_wa = jnp.stack([dbd[:, 0, 0:HEAD_DIM, 0:HEAD_DIM], dbd[:, 0, HEAD_DIM:, HEAD_DIM:]], axis=1)
    d_rg_wx = jnp.stack([dbd[:, 1, 0:HEAD_DIM, 0:HEAD_DIM], dbd[:, 1, HEAD_DIM:, HEAD_DIM:]], axis=1)
    small_g = _pack_small(jnp.zeros((1, D_MODEL), F32), pvec[4:5], pvec[5:6], pvec[6:7], pvec[7:8], acc_out[0:1],
                          acc_out[1:2], db_in_full, pvec[0:4], d_rg_wa, d_rg_wx)
    sm_sems, sm_src, sm_land, sm_token = _gather_start(small_g, dw_o, "gather_small_start")

    sqs_sems, dw_sq_src, sqs_land, sqs_token = _swap_start(dw_sq_send, sm_token, "swap_dw_sq_start")
    chip_in, own_in = _pair_add(dw_in_send, sib_in, place, "pair_add_in", after=sqs_token)
    dw_sq_send, sib_sq = _swap_wait(sqs_sems, dw_sq_src, sqs_land, chip_in, "swap_dw_sq_wait")
    chip_sq, own_sq = _pair_add(dw_sq_send, sib_sq, place, "pair_add_sq")
    sems, sent, lands, token = _exchange_chips_start([chip_in, chip_sq], "exchange_dw_start")

    wt = lambda lo: w_rest[lo * D_MODEL:(lo + 1) * D_MODEL]
    grad_x2, acc_pre = _dh_final(
        [(dz_qkv, w_qkv), (dz_f, w_f), (dz_ga, wt(0)), (dz_xr, wt(1)), (dz_gr, wt(2)), (dz_mga, wt(3)),
         (dz_mgr, wt(4))], token, x2, dy, pre_norm_w)
    pre_sems, pre_src, pre_land, pre_token = _gather_start(acc_pre, grad_x2, "gather_pre_start")
    recv_in, recv_sq = _exchange_chips_wait(sems, sent, lands, pre_token, "exchange_dw_wait")

    g_in, d_in, nm_in, nv_in = [a.T for a in _reduce_adamw(
        own_in, recv_in, place, w_in[0].T, m_w_in[0].T, v_w_in[0].T, "adamw_w_in")]
    sq_out = _reduce_adamw_stacked(
        own_sq, recv_sq, place,
        [(w_branch_a, m_w_branch_a, v_w_branch_a), (w_branch_r, m_w_branch_r, v_w_branch_r),
         (w_out, m_w_out, v_w_out)], "adamw_w_sq")
    pre_all = _gather_wait(pre_sems, pre_src, pre_land, sq_out[2][1], "gather_pre_wait")
    pre_all = lax.dynamic_update_slice(pre_all, acc_pre[None], (me, 0, 0))
    small_all = _gather_wait(sm_sems, sm_src, sm_land, pre_all, "gather_small_wait")
    small_all = lax.dynamic_update_slice(small_all, small_g[None], (me, 0, 0))

    def place_conv(a):
        return lax.dynamic_update_slice(jnp.zeros((CONV_W, D_MODEL), F32), a[0], (0, me * LANES))

    zrow = jnp.zeros((1, D_MODEL), F32)
    vector_names = ["pre_norm_w", "conv_b", "rg_ba", "rg_bx", "rg_lambda", "post_norm_w"]
    vectors = [(pre_norm_w, m_pre_norm_w, v_pre_norm_w), (conv_b, m_conv_b, v_conv_b), (rg_ba, m_rg_ba, v_rg_ba),
               (rg_bx, m_rg_bx, v_rg_bx), (rg_lambda, m_rg_lambda, v_rg_lambda),
               (post_norm_w, m_post_norm_w, v_post_norm_w)]
    small_w = _pack_small(zrow, zrow, zrow, zrow, zrow, zrow, zrow, b_in, place_conv(conv_w), rg_wa[0], rg_wx[0])
    small_m = _pack_small(zrow, zrow, zrow, zrow, zrow, zrow, zrow, m_b_in, place_conv(m_conv_w), m_rg_wa[0],
                          m_rg_wx[0])
    small_v = _pack_small(zrow, zrow, zrow, zrow, zrow, zrow, zrow, v_b_in, place_conv(v_conv_w), v_rg_wa[0],
                          v_rg_wx[0])
    packed, vector_out, loss_tile = _reduce_small(small_all, pre_all, small_w, small_m, small_v, vectors)
    outs_small = [_unpack_small(p) for p in packed]
    loss = loss_tile[0, 0]

    def leaf(kind, name):
        if name == "w_in":
            return (g_in, d_in, nm_in, nv_in)[kind][None]
        if name in ("w_branch_a", "w_branch_r", "w_out"):
            return sq_out[("w_branch_a", "w_branch_r", "w_out").index(name)][kind]
        if name == "conv_w":
            return lax.dynamic_slice(outs_small[kind]["conv_w_full"], (0, me * LANES), (CONV_W, LANES))[None]
        if name in vector_names:
            return vector_out[vector_names.index(name)][kind]
        return outs_small[kind][name]

    names = ["pre_norm_w", "w_in", "b_in", "conv_w", "conv_b", "rg_wa", "rg_ba", "rg_wx", "rg_bx", "rg_lambda",
             "w_branch_a", "w_branch_r", "w_out", "post_norm_w"]
    out = [loss, grad_x2.reshape(b, s, D_MODEL)]
    for kind in range(4):
        out += [leaf(kind, nm) for nm in names]
    return tuple(out)
```
